```python
import math
import jax, jax.numpy as jnp
from jax import lax
import numpy as np

D_MODEL = 1024
BATCH = 8
SEQ = 4096
DEPTH = 2

D_RG = D_MODEL
RG_HEADS = 8
RG_HEAD_DIM = D_RG // RG_HEADS
D_ML = D_MODEL
ML_HEADS = 4
ML_HEAD_DIM = D_ML // ML_HEADS
D_MIX = D_RG + D_ML
D_IN = 2 * D_RG + 3 * D_ML
CONV_WIDTH = 4
RG_C = 8.0
ML_CHUNK = 128
EPS = 1e-6

kernel_name = "hymba_style_rglru_mlstm_hybrid"


def rms_norm(x, g):
    x32 = x.astype(jnp.float32)
    y = x32 * lax.rsqrt(jnp.mean(x32 * x32, axis=-1, keepdims=True) + EPS)
    return (y * g.astype(jnp.float32)).astype(x.dtype)


def causal_depthwise_conv(x, w, b):
    ch = x.shape[-1]
    y = lax.conv_general_dilated(
        x, w[:, None, :].astype(x.dtype), window_strides=(1,),
        padding=[(CONV_WIDTH - 1, 0)], dimension_numbers=("NWC", "WIO", "NWC"),
        feature_group_count=ch)
    return y + b.astype(x.dtype)


def block_diag(x, w):
    h, dh, dout = w.shape
    xb = x.reshape(x.shape[:-1] + (h, dh))
    return jnp.einsum("bshd,hde->bshe", xb, w).reshape(x.shape[:-1] + (h * dout,))


def rglru(x, w_a, b_a, w_x, b_x, lam):
    r = jax.nn.sigmoid(block_diag(x, w_a) + b_a).astype(jnp.float32)
    i = jax.nn.sigmoid(block_diag(x, w_x) + b_x).astype(jnp.float32)
    log_a = -RG_C * r * jax.nn.softplus(-lam.astype(jnp.float32))
    a = jnp.exp(log_a)
    u = jnp.sqrt(-jnp.expm1(2.0 * log_a)) * (i * x.astype(jnp.float32))

    def combine(lhs, rhs):
        a1, b1 = lhs
        a2, b2 = rhs
        return a1 * a2, a2 * b1 + b2

    _, h = lax.associative_scan(combine, (a, u), axis=1)
    return h.astype(x.dtype)


def mlstm_chunkwise(q, k, v, log_i, log_f):
    bsz, s_len, nh, dh = q.shape
    nc = s_len // ML_CHUNK

    def to_chunks(t):
        return t.reshape(bsz, nc, ML_CHUNK, nh, dh).transpose(1, 0, 3, 2, 4)

    def gate_chunks(t):
        return t.reshape(bsz, nc, ML_CHUNK, nh).transpose(1, 0, 3, 2)

    causal = jnp.tril(jnp.ones((ML_CHUNK, ML_CHUNK), dtype=bool))

    def step(carry, xs):
        c_st, n_st, m_st = carry
        qc, kc, vc, li, lf = xs
        b = jnp.cumsum(lf, axis=-1)
        b_last = b[..., -1]
        d = jnp.where(causal, b[..., :, None] - b[..., None, :] + li[..., None, :], -jnp.inf)
        m_inter = b + m_st[..., None]
        m_t = jnp.maximum(m_inter, jnp.max(d, axis=-1))
        w_intra = jnp.exp(d - m_t[..., None])
        w_inter = jnp.exp(m_inter - m_t)
        s = jnp.einsum("bhtk,bhsk->bhts", qc, kc) * w_intra
        num = (jnp.einsum("bhts,bhsv->bhtv", s, vc)
               + w_inter[..., None] * jnp.einsum("bhtk,bhkv->bhtv", qc, c_st))
        den = jnp.sum(s, axis=-1) + w_inter * jnp.einsum("bhtk,bhk->bht", qc, n_st)
        h = num / jnp.maximum(jnp.abs(den), jnp.exp(-m_t))[..., None]
        g = b_last[..., None] - b + li
        m_new = jnp.maximum(b_last + m_st, jnp.max(g, axis=-1))
        w_state = jnp.exp(g - m_new[..., None])
        decay = jnp.exp(b_last + m_st - m_new)
        kw = kc * w_state[..., None]
        c_new = decay[..., None, None] * c_st + jnp.einsum("bhsk,bhsv->bhkv", kw, vc)
        n_new = decay[..., None] * n_st + jnp.sum(kw, axis=2)
        return (c_new, n_new, m_new), h

    init = (jnp.zeros((bsz, nh, dh, dh), jnp.float32),
            jnp.zeros((bsz, nh, dh), jnp.float32),
            jnp.zeros((bsz, nh), jnp.float32))
    _, h = lax.scan(step, init, (to_chunks(q), to_chunks(k), to_chunks(v),
                                 gate_chunks(log_i), gate_chunks(log_f)))
    return h.transpose(1, 0, 3, 2, 4).reshape(bsz, s_len, nh, dh)


def mlstm_branch(xm, o_pre, conv_w, conv_b, w_q, w_k, w_v, w_if, b_if, head_g):
    bsz, s_len, _ = xm.shape
    xc = jax.nn.silu(causal_depthwise_conv(xm, conv_w, conv_b))
    q = block_diag(xc, w_q)
    k = block_diag(xc, w_k)
    v = block_diag(xm, w_v)
    gates = (jnp.concatenate([q, k, v], axis=-1) @ w_if + b_if).astype(jnp.float32)
    log_i = gates[..., :ML_HEADS]
    log_f = jax.nn.log_sigmoid(gates[..., ML_HEADS:])

    def heads(t):
        return t.reshape(bsz, s_len, ML_HEADS, ML_HEAD_DIM).astype(jnp.float32)

    cell = mlstm_chunkwise(heads(q), heads(k) * (ML_HEAD_DIM ** -0.5), heads(v), log_i, log_f)
    h = jax.nn.sigmoid(heads(o_pre)) * cell
    h = h * lax.rsqrt(jnp.mean(h * h, axis=-1, keepdims=True) + EPS)
    return (h.reshape(bsz, s_len, D_ML) * head_g.astype(jnp.float32)).astype(xm.dtype)


def _fwd_setup_inputs(seed: int = 0) -> dict:
    key = jax.random.key(seed)
    ks = jax.random.split(key, 24)
    f32 = jnp.float32

    def nrm(k, shape, scale):
        return jax.random.normal(k, shape, f32) * scale

    x = jax.random.normal(ks[0], (BATCH, SEQ, D_MODEL), f32)
    c = jax.random.normal(ks[1], (BATCH, D_MODEL), f32)
    norm_g = 1.0 + nrm(ks[2], (DEPTH, D_MODEL), 0.02)
    w_ada = nrm(ks[3], (DEPTH, D_MODEL, 3 * D_MODEL), 0.3 * D_MODEL ** -0.5)
    b_ada = nrm(ks[4], (DEPTH, 3 * D_MODEL), 0.02)
    w_in = nrm(ks[5], (DEPTH, D_MODEL, D_IN), D_MODEL ** -0.5)
    rg_conv_w = nrm(ks[6], (DEPTH, CONV_WIDTH, D_RG), CONV_WIDTH ** -0.5)
    rg_conv_b = nrm(ks[7], (DEPTH, D_RG), 0.02)
    rg_w_a = nrm(ks[8], (DEPTH, RG_HEADS, RG_HEAD_DIM, RG_HEAD_DIM), RG_HEAD_DIM ** -0.5)
    rg_b_a = nrm(ks[9], (DEPTH, D_RG), 0.02)
    rg_w_x = nrm(ks[10], (DEPTH, RG_HEADS, RG_HEAD_DIM, RG_HEAD_DIM), RG_HEAD_DIM ** -0.5)
    rg_b_x = nrm(ks[11], (DEPTH, D_RG), 0.02)
    a_c = jax.random.uniform(ks[12], (DEPTH, D_RG), f32, 0.9, 0.999)
    a0 = a_c ** (1.0 / RG_C)
    rg_lambda = jnp.log(a0) - jnp.log1p(-a0)
    ml_conv_w = nrm(ks[13], (DEPTH, CONV_WIDTH, D_ML), CONV_WIDTH ** -0.5)
    ml_conv_b = nrm(ks[14], (DEPTH, D_ML), 0.02)
    ml_w_q = nrm(ks[15], (DEPTH, ML_HEADS, ML_HEAD_DIM, ML_HEAD_DIM), ML_HEAD_DIM ** -0.5)
    ml_w_k = nrm(ks[16], (DEPTH, ML_HEADS, ML_HEAD_DIM, ML_HEAD_DIM), ML_HEAD_DIM ** -0.5)
    ml_w_v = nrm(ks[17], (DEPTH, ML_HEADS, ML_HEAD_DIM, ML_HEAD_DIM), ML_HEAD_DIM ** -0.5)
    ml_w_if = nrm(ks[18], (DEPTH, 3 * D_ML, 2 * ML_HEADS), 0.1 * (3 * D_ML) ** -0.5)
    b_i = nrm(ks[19], (DEPTH, ML_HEADS), 0.1) - 1.0
    b_f = jnp.linspace(3.0, 6.0, ML_HEADS, dtype=f32)[None, :] + nrm(ks[20], (DEPTH, ML_HEADS), 0.1)
    ml_b_if = jnp.concatenate([b_i, b_f], axis=-1)
    ml_norm_g = 1.0 + nrm(ks[21], (DEPTH, D_ML), 0.02)
    w_out = nrm(ks[22], (DEPTH, D_MIX, D_MODEL), D_MIX ** -0.5)
    final_g = 1.0 + nrm(ks[23], (D_MODEL,), 0.02)
    return {"x": x, "c": c, "norm_g": norm_g, "w_ada": w_ada, "b_ada": b_ada,
            "w_in": w_in, "rg_conv_w": rg_conv_w, "rg_conv_b": rg_conv_b,
            "rg_w_a": rg_w_a, "rg_b_a": rg_b_a, "rg_w_x": rg_w_x, "rg_b_x": rg_b_x,
            "rg_lambda": rg_lambda, "ml_conv_w": ml_conv_w, "ml_conv_b": ml_conv_b,
            "ml_w_q": ml_w_q, "ml_w_k": ml_w_k, "ml_w_v": ml_w_v, "ml_w_if": ml_w_if,
            "ml_b_if": ml_b_if, "ml_norm_g": ml_norm_g, "w_out": w_out, "final_g": final_g}


def _fwd_reference(x, c, norm_g, w_ada, b_ada, w_in, rg_conv_w, rg_conv_b, rg_w_a, rg_b_a,
              rg_w_x, rg_b_x, rg_lambda, ml_conv_w, ml_conv_b, ml_w_q, ml_w_k, ml_w_v,
              ml_w_if, ml_b_if, ml_norm_g, w_out, final_g):
    split_pts = [D_RG, 2 * D_RG, 2 * D_RG + D_ML, 2 * D_RG + 2 * D_ML]
    c_act = jax.nn.silu(c)
    for l in range(DEPTH):
        mod = c_act @ w_ada[l] + b_ada[l]
        shift, scale, gate = jnp.split(mod, 3, axis=-1)
        h = rms_norm(x, norm_g[l]) * (1.0 + scale[:, None, :]) + shift[:, None, :]
        u = h @ w_in[l]
        rg_x, rg_z, ml_x, ml_o, ml_z = jnp.split(u, split_pts, axis=-1)
        y_rg = rglru(causal_depthwise_conv(rg_x, rg_conv_w[l], rg_conv_b[l]),
                     rg_w_a[l], rg_b_a[l], rg_w_x[l], rg_b_x[l], rg_lambda[l]) * jax.nn.silu(rg_z)
        y_ml = mlstm_branch(ml_x, ml_o, ml_conv_w[l], ml_conv_b[l], ml_w_q[l], ml_w_k[l],
                            ml_w_v[l], ml_w_if[l], ml_b_if[l], ml_norm_g[l]) * jax.nn.silu(ml_z)
        y = jnp.concatenate([y_rg, y_ml], axis=-1) @ w_out[l]
        x = x + gate[:, None, :] * y
    return rms_norm(x, final_g)


import jax as _jax
import jax.numpy as _jnp

TWIN_FORMAT = 'train_step'
FWD_PARAMS = ['x', 'c', 'norm_g', 'w_ada', 'b_ada', 'w_in', 'rg_conv_w', 'rg_conv_b', 'rg_w_a', 'rg_b_a', 'rg_w_x', 'rg_b_x', 'rg_lambda', 'ml_conv_w', 'ml_conv_b', 'ml_w_q', 'ml_w_k', 'ml_w_v', 'ml_w_if', 'ml_b_if', 'ml_norm_g', 'w_out', 'final_g']
TWIN_WEIGHTS = ['norm_g', 'w_ada', 'b_ada', 'w_in', 'rg_conv_w', 'rg_conv_b', 'rg_w_a', 'rg_b_a', 'rg_w_x', 'rg_b_x', 'rg_lambda', 'ml_conv_w', 'ml_conv_b', 'ml_w_q', 'ml_w_k', 'ml_w_v', 'ml_w_if', 'ml_b_if', 'ml_norm_g', 'w_out', 'final_g']
TWIN_DIFF_INPUT = 'x'
TWIN_INPUTS = ['x', 'c', 'norm_g', 'w_ada', 'b_ada', 'w_in', 'rg_conv_w', 'rg_conv_b', 'rg_w_a', 'rg_b_a', 'rg_w_x', 'rg_b_x', 'rg_lambda', 'ml_conv_w', 'ml_conv_b', 'ml_w_q', 'ml_w_k', 'ml_w_v', 'ml_w_if', 'ml_b_if', 'ml_norm_g', 'w_out', 'final_g', 'loss_target', 'm_norm_g', 'm_w_ada', 'm_b_ada', 'm_w_in', 'm_rg_conv_w', 'm_rg_conv_b', 'm_rg_w_a', 'm_rg_b_a', 'm_rg_w_x', 'm_rg_b_x', 'm_rg_lambda', 'm_ml_conv_w', 'm_ml_conv_b', 'm_ml_w_q', 'm_ml_w_k', 'm_ml_w_v', 'm_ml_w_if', 'm_ml_b_if', 'm_ml_norm_g', 'm_w_out', 'm_final_g', 'v_norm_g', 'v_w_ada', 'v_b_ada', 'v_w_in', 'v_rg_conv_w', 'v_rg_conv_b', 'v_rg_w_a', 'v_rg_b_a', 'v_rg_w_x', 'v_rg_b_x', 'v_rg_lambda', 'v_ml_conv_w', 'v_ml_conv_b', 'v_ml_w_q', 'v_ml_w_k', 'v_ml_w_v', 'v_ml_w_if', 'v_ml_b_if', 'v_ml_norm_g', 'v_w_out', 'v_final_g']
TWIN_OUTPUTS = ['loss', 'grad_x', 'grad_norm_g', 'grad_w_ada', 'grad_b_ada', 'grad_w_in', 'grad_rg_conv_w', 'grad_rg_conv_b', 'grad_rg_w_a', 'grad_rg_b_a', 'grad_rg_w_x', 'grad_rg_b_x', 'grad_rg_lambda', 'grad_ml_conv_w', 'grad_ml_conv_b', 'grad_ml_w_q', 'grad_ml_w_k', 'grad_ml_w_v', 'grad_ml_w_if', 'grad_ml_b_if', 'grad_ml_norm_g', 'grad_w_out', 'grad_final_g', 'delta_norm_g', 'delta_w_ada', 'delta_b_ada', 'delta_w_in', 'delta_rg_conv_w', 'delta_rg_conv_b', 'delta_rg_w_a', 'delta_rg_b_a', 'delta_rg_w_x', 'delta_rg_b_x', 'delta_rg_lambda', 'delta_ml_conv_w', 'delta_ml_conv_b', 'delta_ml_w_q', 'delta_ml_w_k', 'delta_ml_w_v', 'delta_ml_w_if', 'delta_ml_b_if', 'delta_ml_norm_g', 'delta_w_out', 'delta_final_g', 'new_m_norm_g', 'new_m_w_ada', 'new_m_b_ada', 'new_m_w_in', 'new_m_rg_conv_w', 'new_m_rg_conv_b', 'new_m_rg_w_a', 'new_m_rg_b_a', 'new_m_rg_w_x', 'new_m_rg_b_x', 'new_m_rg_lambda', 'new_m_ml_conv_w', 'new_m_ml_conv_b', 'new_m_ml_w_q', 'new_m_ml_w_k', 'new_m_ml_w_v', 'new_m_ml_w_if', 'new_m_ml_b_if', 'new_m_ml_norm_g', 'new_m_w_out', 'new_m_final_g', 'new_v_norm_g', 'new_v_w_ada', 'new_v_b_ada', 'new_v_w_in', 'new_v_rg_conv_w', 'new_v_rg_conv_b', 'new_v_rg_w_a', 'new_v_rg_b_a', 'new_v_rg_w_x', 'new_v_rg_b_x', 'new_v_rg_lambda', 'new_v_ml_conv_w', 'new_v_ml_conv_b', 'new_v_ml_w_q', 'new_v_ml_w_k', 'new_v_ml_w_v', 'new_v_ml_w_if', 'new_v_ml_b_if', 'new_v_ml_norm_g', 'new_v_w_out', 'new_v_final_g']
TWIN_LEAF_KINDS = {'loss': 'loss', 'grad_x': 'grad_x', 'grad_norm_g': 'grad_w', 'grad_w_ada': 'grad_w', 'grad_b_ada': 'grad_w', 'grad_w_in': 'grad_w', 'grad_rg_conv_w': 'grad_w', 'grad_rg_conv_b': 'grad_w', 'grad_rg_w_a': 'grad_w', 'grad_rg_b_a': 'grad_w', 'grad_rg_w_x': 'grad_w', 'grad_rg_b_x': 'grad_w', 'grad_rg_lambda': 'grad_w', 'grad_ml_conv_w': 'grad_w', 'grad_ml_conv_b': 'grad_w', 'grad_ml_w_q': 'grad_w', 'grad_ml_w_k': 'grad_w', 'grad_ml_w_v': 'grad_w', 'grad_ml_w_if': 'grad_w', 'grad_ml_b_if': 'grad_w', 'grad_ml_norm_g': 'grad_w', 'grad_w_out': 'grad_w', 'grad_final_g': 'grad_w', 'delta_norm_g': 'delta_w', 'delta_w_ada': 'delta_w', 'delta_b_ada': 'delta_w', 'delta_w_in': 'delta_w', 'delta_rg_conv_w': 'delta_w', 'delta_rg_conv_b': 'delta_w', 'delta_rg_w_a': 'delta_w', 'delta_rg_b_a': 'delta_w', 'delta_rg_w_x': 'delta_w', 'delta_rg_b_x': 'delta_w', 'delta_rg_lambda': 'delta_w', 'delta_ml_conv_w': 'delta_w', 'delta_ml_conv_b': 'delta_w', 'delta_ml_w_q': 'delta_w', 'delta_ml_w_k': 'delta_w', 'delta_ml_w_v': 'delta_w', 'delta_ml_w_if': 'delta_w', 'delta_ml_b_if': 'delta_w', 'delta_ml_norm_g': 'delta_w', 'delta_w_out': 'delta_w', 'delta_final_g': 'delta_w', 'new_m_norm_g': 'new_m', 'new_m_w_ada': 'new_m', 'new_m_b_ada': 'new_m', 'new_m_w_in': 'new_m', 'new_m_rg_conv_w': 'new_m', 'new_m_rg_conv_b': 'new_m', 'new_m_rg_w_a': 'new_m', 'new_m_rg_b_a': 'new_m', 'new_m_rg_w_x': 'new_m', 'new_m_rg_b_x': 'new_m', 'new_m_rg_lambda': 'new_m', 'new_m_ml_conv_w': 'new_m', 'new_m_ml_conv_b': 'new_m', 'new_m_ml_w_q': 'new_m', 'new_m_ml_w_k': 'new_m', 'new_m_ml_w_v': 'new_m', 'new_m_ml_w_if': 'new_m', 'new_m_ml_b_if': 'new_m', 'new_m_ml_norm_g': 'new_m', 'new_m_w_out': 'new_m', 'new_m_final_g': 'new_m', 'new_v_norm_g': 'new_v', 'new_v_w_ada': 'new_v', 'new_v_b_ada': 'new_v', 'new_v_w_in': 'new_v', 'new_v_rg_conv_w': 'new_v', 'new_v_rg_conv_b': 'new_v', 'new_v_rg_w_a': 'new_v', 'new_v_rg_b_a': 'new_v', 'new_v_rg_w_x': 'new_v', 'new_v_rg_b_x': 'new_v', 'new_v_rg_lambda': 'new_v', 'new_v_ml_conv_w': 'new_v', 'new_v_ml_conv_b': 'new_v', 'new_v_ml_w_q': 'new_v', 'new_v_ml_w_k': 'new_v', 'new_v_ml_w_v': 'new_v', 'new_v_ml_w_if': 'new_v', 'new_v_ml_b_if': 'new_v', 'new_v_ml_norm_g': 'new_v', 'new_v_w_out': 'new_v', 'new_v_final_g': 'new_v'}


def _forward(args):
    return _fwd_reference(*[args[k] for k in FWD_PARAMS])


def _output_shape():
    def fwd():
        inp = _fwd_setup_inputs(0)
        return _fwd_reference(*[inp[k] for k in FWD_PARAMS])
    out = _jax.eval_shape(fwd)
    return out.shape, out.dtype

N_MICROBATCH = 1
ADAM_LR = 0.001
ADAM_B1 = 0.9
ADAM_B2 = 0.999
ADAM_EPS = 1e-08
ADAM_WD = 0.01
ADAM_STEP = 10
PER_EXAMPLE_BATCH_AXIS = {'x': 0, 'c': 0, 'loss_target': 0}
SHARED_INPUTS = []
_WEIGHT_DTYPES = {'norm_g': _jnp.float32, 'w_ada': _jnp.float32, 'b_ada': _jnp.float32, 'w_in': _jnp.float32, 'rg_conv_w': _jnp.float32, 'rg_conv_b': _jnp.float32, 'rg_w_a': _jnp.float32, 'rg_b_a': _jnp.float32, 'rg_w_x': _jnp.float32, 'rg_b_x': _jnp.float32, 'rg_lambda': _jnp.float32, 'ml_conv_w': _jnp.float32, 'ml_conv_b': _jnp.float32, 'ml_w_q': _jnp.float32, 'ml_w_k': _jnp.float32, 'ml_w_v': _jnp.float32, 'ml_w_if': _jnp.float32, 'ml_b_if': _jnp.float32, 'ml_norm_g': _jnp.float32, 'w_out': _jnp.float32, 'final_g': _jnp.float32}
MOMENT_SCALE = {'norm_g': 3.386551e-02, 'w_ada': 6.290276e-02, 'b_ada': 1.045255e-01, 'w_in': 1.757341e-02, 'rg_conv_w': 1.915378e-02, 'rg_conv_b': 9.875070e-02, 'rg_w_a': 3.155280e-03, 'rg_b_a': 4.132899e-03, 'rg_w_x': 5.665913e-03, 'rg_b_x': 6.924717e-03, 'rg_lambda': 9.545591e-03, 'ml_conv_w': 2.192198e-02, 'ml_conv_b': 2.550502e-02, 'ml_w_q': 1.550980e-02, 'ml_w_k': 1.539157e-02, 'ml_w_v': 1.413874e-02, 'ml_w_if': 9.753345e-02, 'ml_b_if': 5.575099e-02, 'ml_norm_g': 1.407594e-02, 'w_out': 2.423112e-02, 'final_g': 3.198168e+01}


def _to_microbatches(a, axis):
    t = _jnp.moveaxis(a, axis, 0)
    t = t.reshape((N_MICROBATCH, t.shape[0] // N_MICROBATCH) + t.shape[1:])
    return _jnp.moveaxis(t, 1, axis + 1)


def setup_inputs(seed: int = 0) -> dict:
    inp = _fwd_setup_inputs(seed)
    key = _jax.random.fold_in(_jax.random.key(seed), 7919)
    shape, _ = _output_shape()
    out = dict(inp)
    out["loss_target"] = _jax.random.normal(_jax.random.fold_in(key, 0), shape, _jnp.float32)
    for i, name in enumerate(TWIN_WEIGHTS):
        w = inp[name].astype(_jnp.float32)
        if MOMENT_SCALE is None:
            s = _jnp.sqrt(_jnp.mean(_jnp.square(w)) + 1e-30)
        else:
            s = MOMENT_SCALE[name]
        km, kv = _jax.random.split(_jax.random.fold_in(key, i + 1))
        out[name] = w
        out["m_" + name] = s * _jax.random.normal(km, w.shape, _jnp.float32)
        out["v_" + name] = (s * s) * _jax.random.uniform(kv, w.shape, _jnp.float32, 0.5, 1.5)
    if N_MICROBATCH > 1:
        for name, axis in PER_EXAMPLE_BATCH_AXIS.items():
            out[name] = _to_microbatches(out[name], axis)
    return {'x': out['x'], 'c': out['c'], 'norm_g': out['norm_g'], 'w_ada': out['w_ada'], 'b_ada': out['b_ada'], 'w_in': out['w_in'], 'rg_conv_w': out['rg_conv_w'], 'rg_conv_b': out['rg_conv_b'], 'rg_w_a': out['rg_w_a'], 'rg_b_a': out['rg_b_a'], 'rg_w_x': out['rg_w_x'], 'rg_b_x': out['rg_b_x'], 'rg_lambda': out['rg_lambda'], 'ml_conv_w': out['ml_conv_w'], 'ml_conv_b': out['ml_conv_b'], 'ml_w_q': out['ml_w_q'], 'ml_w_k': out['ml_w_k'], 'ml_w_v': out['ml_w_v'], 'ml_w_if': out['ml_w_if'], 'ml_b_if': out['ml_b_if'], 'ml_norm_g': out['ml_norm_g'], 'w_out': out['w_out'], 'final_g': out['final_g'], 'loss_target': out['loss_target'], 'm_norm_g': out['m_norm_g'], 'm_w_ada': out['m_w_ada'], 'm_b_ada': out['m_b_ada'], 'm_w_in': out['m_w_in'], 'm_rg_conv_w': out['m_rg_conv_w'], 'm_rg_conv_b': out['m_rg_conv_b'], 'm_rg_w_a': out['m_rg_w_a'], 'm_rg_b_a': out['m_rg_b_a'], 'm_rg_w_x': out['m_rg_w_x'], 'm_rg_b_x': out['m_rg_b_x'], 'm_rg_lambda': out['m_rg_lambda'], 'm_ml_conv_w': out['m_ml_conv_w'], 'm_ml_conv_b': out['m_ml_conv_b'], 'm_ml_w_q': out['m_ml_w_q'], 'm_ml_w_k': out['m_ml_w_k'], 'm_ml_w_v': out['m_ml_w_v'], 'm_ml_w_if': out['m_ml_w_if'], 'm_ml_b_if': out['m_ml_b_if'], 'm_ml_norm_g': out['m_ml_norm_g'], 'm_w_out': out['m_w_out'], 'm_final_g': out['m_final_g'], 'v_norm_g': out['v_norm_g'], 'v_w_ada': out['v_w_ada'], 'v_b_ada': out['v_b_ada'], 'v_w_in': out['v_w_in'], 'v_rg_conv_w': out['v_rg_conv_w'], 'v_rg_conv_b': out['v_rg_conv_b'], 'v_rg_w_a': out['v_rg_w_a'], 'v_rg_b_a': out['v_rg_b_a'], 'v_rg_w_x': out['v_rg_w_x'], 'v_rg_b_x': out['v_rg_b_x'], 'v_rg_lambda': out['v_rg_lambda'], 'v_ml_conv_w': out['v_ml_conv_w'], 'v_ml_conv_b': out['v_ml_conv_b'], 'v_ml_w_q': out['v_ml_w_q'], 'v_ml_w_k': out['v_ml_w_k'], 'v_ml_w_v': out['v_ml_w_v'], 'v_ml_w_if': out['v_ml_w_if'], 'v_ml_b_if': out['v_ml_b_if'], 'v_ml_norm_g': out['v_ml_norm_g'], 'v_w_out': out['v_w_out'], 'v_final_g': out['v_final_g']}


def _loss(weights, diff, rest, loss_target):
    with _jax.named_scope("forward"):
        args = {**rest, TWIN_DIFF_INPUT: diff, **{k: w.astype(_WEIGHT_DTYPES[k]) for k, w in weights.items()}}
        y = _forward(args)
    with _jax.named_scope("loss_head"):
        err = _jnp.square(y.astype(_jnp.float32) - loss_target)
        return 0.5 * _jnp.sum(_jnp.mean(err, axis=-1)) if err.ndim else 0.5 * err


def _adamw(w, g, m, v):
    m = ADAM_B1 * m + (1.0 - ADAM_B1) * g
    v = ADAM_B2 * v + (1.0 - ADAM_B2) * _jnp.square(g)
    m_hat = m / (1.0 - ADAM_B1 ** ADAM_STEP)
    v_hat = v / (1.0 - ADAM_B2 ** ADAM_STEP)
    delta = -ADAM_LR * (m_hat / (_jnp.sqrt(v_hat) + ADAM_EPS) + ADAM_WD * w)
    return delta, m, v


def reference(x, c, norm_g, w_ada, b_ada, w_in, rg_conv_w, rg_conv_b, rg_w_a, rg_b_a, rg_w_x, rg_b_x, rg_lambda, ml_conv_w, ml_conv_b, ml_w_q, ml_w_k, ml_w_v, ml_w_if, ml_b_if, ml_norm_g, w_out, final_g, loss_target, m_norm_g, m_w_ada, m_b_ada, m_w_in, m_rg_conv_w, m_rg_conv_b, m_rg_w_a, m_rg_b_a, m_rg_w_x, m_rg_b_x, m_rg_lambda, m_ml_conv_w, m_ml_conv_b, m_ml_w_q, m_ml_w_k, m_ml_w_v, m_ml_w_if, m_ml_b_if, m_ml_norm_g, m_w_out, m_final_g, v_norm_g, v_w_ada, v_b_ada, v_w_in, v_rg_conv_w, v_rg_conv_b, v_rg_w_a, v_rg_b_a, v_rg_w_x, v_rg_b_x, v_rg_lambda, v_ml_conv_w, v_ml_conv_b, v_ml_w_q, v_ml_w_k, v_ml_w_v, v_ml_w_if, v_ml_b_if, v_ml_norm_g, v_w_out, v_final_g):
    given = dict(x=x, c=c, norm_g=norm_g, w_ada=w_ada, b_ada=b_ada, w_in=w_in, rg_conv_w=rg_conv_w, rg_conv_b=rg_conv_b, rg_w_a=rg_w_a, rg_b_a=rg_b_a, rg_w_x=rg_w_x, rg_b_x=rg_b_x, rg_lambda=rg_lambda, ml_conv_w=ml_conv_w, ml_conv_b=ml_conv_b, ml_w_q=ml_w_q, ml_w_k=ml_w_k, ml_w_v=ml_w_v, ml_w_if=ml_w_if, ml_b_if=ml_b_if, ml_norm_g=ml_norm_g, w_out=w_out, final_g=final_g, loss_target=loss_target, m_norm_g=m_norm_g, m_w_ada=m_w_ada, m_b_ada=m_b_ada, m_w_in=m_w_in, m_rg_conv_w=m_rg_conv_w, m_rg_conv_b=m_rg_conv_b, m_rg_w_a=m_rg_w_a, m_rg_b_a=m_rg_b_a, m_rg_w_x=m_rg_w_x, m_rg_b_x=m_rg_b_x, m_rg_lambda=m_rg_lambda, m_ml_conv_w=m_ml_conv_w, m_ml_conv_b=m_ml_conv_b, m_ml_w_q=m_ml_w_q, m_ml_w_k=m_ml_w_k, m_ml_w_v=m_ml_w_v, m_ml_w_if=m_ml_w_if, m_ml_b_if=m_ml_b_if, m_ml_norm_g=m_ml_norm_g, m_w_out=m_w_out, m_final_g=m_final_g, v_norm_g=v_norm_g, v_w_ada=v_w_ada, v_b_ada=v_b_ada, v_w_in=v_w_in, v_rg_conv_w=v_rg_conv_w, v_rg_conv_b=v_rg_conv_b, v_rg_w_a=v_rg_w_a, v_rg_b_a=v_rg_b_a, v_rg_w_x=v_rg_w_x, v_rg_b_x=v_rg_b_x, v_rg_lambda=v_rg_lambda, v_ml_conv_w=v_ml_conv_w, v_ml_conv_b=v_ml_conv_b, v_ml_w_q=v_ml_w_q, v_ml_w_k=v_ml_w_k, v_ml_w_v=v_ml_w_v, v_ml_w_if=v_ml_w_if, v_ml_b_if=v_ml_b_if, v_ml_norm_g=v_ml_norm_g, v_w_out=v_w_out, v_final_g=v_final_g)
    weights = {n: given[n] for n in TWIN_WEIGHTS}
    shared = {n: given[n] for n in SHARED_INPUTS}
    per_example = {n: given[n] for n in ['x', 'c']}
    grad_fn = _jax.value_and_grad(_loss, argnums=(0, 1))

    def one_microbatch(ex, loss_target):
        ex = dict(ex)
        diff = ex.pop(TWIN_DIFF_INPUT)
        return grad_fn(weights, diff, {**shared, **ex}, loss_target)

    if N_MICROBATCH == 1:
        loss, (grad_w, grad_x) = one_microbatch(per_example, given["loss_target"])
    else:
        def body(carry, xs):
            loss_sum, grad_sum = carry
            l_k, (gw_k, gx_k) = one_microbatch(xs[0], xs[1])
            with _jax.named_scope("update"):
                return (loss_sum + l_k, _jax.tree.map(_jnp.add, grad_sum, gw_k)), gx_k

        init = (_jnp.zeros((), _jnp.float32), _jax.tree.map(_jnp.zeros_like, weights))
        (loss, grad_w), grad_x = _jax.lax.scan(body, init, (per_example, given["loss_target"]))
    with _jax.named_scope("update"):
        delta_w, new_m, new_v = {}, {}, {}
        for n in TWIN_WEIGHTS:
            delta_w[n], new_m[n], new_v[n] = _adamw(weights[n], grad_w[n], given["m_" + n], given["v_" + n])
    return (loss, grad_x, *[grad_w[n] for n in TWIN_WEIGHTS], *[delta_w[n] for n in TWIN_WEIGHTS],
            *[new_m[n] for n in TWIN_WEIGHTS], *[new_v[n] for n in TWIN_WEIGHTS])
```

```python
import functools

import jax
import jax.numpy as jnp
from jax import lax
from jax.experimental import pallas as pl
from jax.experimental.pallas import tpu as pltpu

F32 = jnp.float32
BF16 = jnp.bfloat16

EPS = 1e-6
RG_C = 8.0
CONV_WIDTH = 4
ML_CHUNK = 128
HALO = 8
ADAM_LR = 0.001
ADAM_B1 = 0.9
ADAM_B2 = 0.999
ADAM_EPS = 1e-08
ADAM_WD = 0.01
ADAM_STEP = 10
MESH = pl.DeviceIdType.MESH


def _pcall(body, **kw):
    return pl.pallas_call(body, **kw)


def _seq(n=1):
    return pltpu.CompilerParams(dimension_semantics=("arbitrary",) * n)


def _dot(a, b):
    return jnp.dot(a, b, preferred_element_type=F32)


def _dot_nt(a, b):
    return lax.dot_general(a, b, (((1,), (1,)), ((), ())), preferred_element_type=F32)


def _dot_tn(a, b):
    return lax.dot_general(a, b, (((0,), (0,)), ((), ())), preferred_element_type=F32)


def _bf(x):
    return x.astype(BF16)


def _sigmoid(x):
    return 1.0 / (1.0 + jnp.exp(-x))


def _log1p(z):
    u = 1.0 + z
    return jnp.where(u == 1.0, z, jnp.log(u) * (z / jnp.where(u == 1.0, 1.0, u - 1.0)))


def _softplus(x):
    return jnp.maximum(x, 0.0) + _log1p(jnp.exp(-jnp.abs(x)))


def _log_sigmoid(x):
    return -_softplus(-x)


def _expm1(x):
    small = x * (1.0 + x * (0.5 + x * (1.0 / 6.0 + x * (1.0 / 24.0 + x * (1.0 / 120.0)))))
    return jnp.where(jnp.abs(x) < 0.03, small, jnp.exp(x) - 1.0)


def _dsilu(x, s):
    return s * (1.0 + x * (1.0 - s))


def _rowsum(x):
    return jnp.sum(x, axis=1, keepdims=True)


def _colsum(x):
    return jnp.sum(x, axis=0, keepdims=True)


def _shift_down(win, s):
    return win if s == 0 else pltpu.roll(win, s, 0)


def _shift_up(win, s):
    return win if s == 0 else pltpu.roll(win, win.shape[0] - s, 0)


def _conv_fwd(win, w_ref, b_ref):
    acc = b_ref[...] + w_ref[CONV_WIDTH - 1:CONV_WIDTH, :] * win[HALO:]
    for k in range(CONV_WIDTH - 1):
        acc = acc + w_ref[k:k + 1, :] * _shift_down(win, CONV_WIDTH - 1 - k)[HALO:]
    return acc


def _split3(x):
    hi = _bf(x)
    r1 = x - hi.astype(F32)
    mid = _bf(r1)
    lo = _bf(r1 - mid.astype(F32))
    return hi, mid, lo


def _tri_dot_left(tri, x):
    hi, mid, lo = _split3(x)
    return _dot(tri, hi) + _dot(tri, mid) + _dot(tri, lo)


def _tri_dot_right(x, tri):
    hi, mid, lo = _split3(x)
    return _dot(hi, tri) + _dot(mid, tri) + _dot(lo, tri)


def _tile(n, want):
    t = min(n, want)
    assert n % t == 0
    return t


def _ln_inproj(x, g, scale, shift, w_in_b):
    s_len, d = x.shape
    n_in = w_in_b.shape[1]
    tm = _tile(s_len, 512)
    nj = n_in // d

    def body(x_ref, g_ref, sc_ref, sh_ref, w_ref, h_ref, u_ref, hs):
        @pl.when(pl.program_id(1) == 0)
        def _():
            xv = x_ref[...]
            r = lax.rsqrt(jnp.mean(xv * xv, axis=-1, keepdims=True) + EPS)
            hv = (xv * r * g_ref[...]) * (1.0 + sc_ref[...]) + sh_ref[...]
            hs[...] = _bf(hv)
            h_ref[...] = hs[...]

        u_ref[...] = _dot(hs[...], w_ref[...])

    vec = pl.BlockSpec((1, d), lambda i, j: (0, 0))
    return _pcall(
        body, name="ln_inproj", grid=(s_len // tm, nj),
        in_specs=[pl.BlockSpec((tm, d), lambda i, j: (i, 0)), vec, vec, vec,
                  pl.BlockSpec((d, d), lambda i, j: (0, j))],
        out_specs=[pl.BlockSpec((tm, d), lambda i, j: (i, 0)), pl.BlockSpec((tm, d), lambda i, j: (i, j))],
        out_shape=[jax.ShapeDtypeStruct((s_len, d), BF16), jax.ShapeDtypeStruct((s_len, n_in), F32)],
        scratch_shapes=[pltpu.VMEM((tm, d), BF16)],
        compiler_params=_seq(2),
    )(x, g, scale, shift, w_in_b)


def _rg_gates(xc, wa_ref, ba_ref, wx_ref, bx_ref, lam_ref):
    heads, hd, _ = wa_ref.shape
    xb = _bf(xc)
    ga = jnp.concatenate([_dot(xb[:, h * hd:(h + 1) * hd], wa_ref[h]) for h in range(heads)], axis=1) + ba_ref[...]
    gx = jnp.concatenate([_dot(xb[:, h * hd:(h + 1) * hd], wx_ref[h]) for h in range(heads)], axis=1) + bx_ref[...]
    r = _sigmoid(ga)
    ig = _sigmoid(gx)
    sp = _softplus(-lam_ref[...])
    log_a = (-RG_C) * r * sp
    a = jnp.exp(log_a)
    mult = jnp.sqrt(-_expm1(2.0 * log_a))
    return r, ig, sp, log_a, a, mult


def _scan_groups(a, u, reverse):
    n = a.shape[0]
    row = lax.broadcasted_iota(jnp.int32, a.shape, 0) & 7
    for k in (1, 2, 4):
        if reverse:
            a_sh, u_sh = _shift_up(a, k), _shift_up(u, k)
            ok = row < 8 - k
        else:
            a_sh, u_sh = _shift_down(a, k), _shift_down(u, k)
            ok = row >= k
        u = jnp.where(ok, a * u_sh + u, u)
        a = jnp.where(ok, a * a_sh, a)
    del n
    return a, u


def _rg_fwd(u, conv_w, conv_b, wa_b, ba, wx_b, bx, lam):
    s_len = u.shape[0]
    d = conv_w.shape[1]
    tm = _tile(s_len, 256)
    per = tm // HALO

    def body(x_ref, xp_ref, z_ref, cw_ref, cb_ref, wa_ref, ba_ref, wx_ref, bx_ref, lam_ref,
             hh_ref, y_ref, carry):
        i = pl.program_id(0)

        @pl.when(i == 0)
        def _():
            carry[...] = jnp.zeros_like(carry)

        prev = jnp.where(i == 0, 0.0, xp_ref[...])
        xc = _conv_fwd(jnp.concatenate([prev, x_ref[...]], axis=0), cw_ref, cb_ref)
        _, ig, _, _, a, mult = _rg_gates(xc, wa_ref, ba_ref, wx_ref, bx_ref, lam_ref)
        ca, cu = _scan_groups(a, mult * (ig * xc), reverse=False)
        c = carry[0:1, :]
        for j in range(per):
            blk = ca[j * 8:(j + 1) * 8] * c + cu[j * 8:(j + 1) * 8]
            hh_ref[j * 8:(j + 1) * 8, :] = blk
            c = blk[7:8]
        carry[0:1, :] = c
        z = z_ref[...]
        y_ref[...] = _bf(hh_ref[...] * (z * _sigmoid(z)))

    vec = pl.BlockSpec((1, d), lambda i: (0, 0))
    whole3 = lambda a: pl.BlockSpec(a.shape, lambda i: (0, 0, 0))
    return _pcall(
        body, name="rg_fwd", grid=(s_len // tm,),
        in_specs=[pl.BlockSpec((tm, d), lambda i: (i, 0)),
                  pl.BlockSpec((HALO, d), lambda i: (jnp.maximum(i * per - 1, 0), 0)),
                  pl.BlockSpec((tm, d), lambda i: (i, 1)),
                  pl.BlockSpec((CONV_WIDTH, d), lambda i: (0, 0)), vec,
                  whole3(wa_b), vec, whole3(wx_b), vec, vec],
        out_specs=[pl.BlockSpec((tm, d), lambda i: (i, 0)), pl.BlockSpec((tm, d), lambda i: (i, 0))],
        out_shape=[jax.ShapeDtypeStruct((s_len, d), F32), jax.ShapeDtypeStruct((s_len, d), BF16)],
        scratch_shapes=[pltpu.VMEM((8, d), F32)],
        compiler_params=_seq(),
    )(u, u, u, conv_w, conv_b, wa_b, ba, wx_b, bx, lam)


def _ml_pre(u, conv_w, conv_b, wq_b, wk_b, wv_b, wif_b, wift_b, b_if, b_ift):
    s_len = u.shape[0]
    d = conv_w.shape[1]
    heads, hd, _ = wq_b.shape
    ng = 2 * heads
    tm = _tile(s_len, 256)
    per = tm // HALO

    def body(x_ref, xp_ref, cw_ref, cb_ref, wq_ref, wk_ref, wv_ref, wif_ref, wift_ref, bif_ref, bift_ref,
             q_ref, k_ref, v_ref, gt_ref, gtt_ref):
        i = pl.program_id(0)
        prev = jnp.where(i == 0, 0.0, xp_ref[...])
        xm = x_ref[...]
        pre = _conv_fwd(jnp.concatenate([prev, xm], axis=0), cw_ref, cb_ref)
        xcb = _bf(pre * _sigmoid(pre))
        xmb = _bf(xm)
        for h in range(heads):
            hs = slice(h * hd, (h + 1) * hd)
            q_ref[:, hs] = _bf(_dot(xcb[:, hs], wq_ref[h]))
            k_ref[:, hs] = _bf(_dot(xcb[:, hs], wk_ref[h]))
            v_ref[:, hs] = _bf(_dot(xmb[:, hs], wv_ref[h]))
        qb, kb, vb = q_ref[...], k_ref[...], v_ref[...]
        gt_ref[...] = (_dot(qb, wif_ref[0:d, :]) + _dot(kb, wif_ref[d:2 * d, :]) + _dot(vb, wif_ref[2 * d:3 * d, :])
                       + bif_ref[...])
        gtt_ref[...] = (_dot_nt(wift_ref[:, 0:d], qb) + _dot_nt(wift_ref[:, d:2 * d], kb)
                        + _dot_nt(wift_ref[:, 2 * d:3 * d], vb) + bift_ref[...])

    vec = pl.BlockSpec((1, d), lambda i: (0, 0))
    whole3 = lambda a: pl.BlockSpec(a.shape, lambda i: (0, 0, 0))
    whole2 = lambda a: pl.BlockSpec(a.shape, lambda i: (0, 0))
    row = pl.BlockSpec((tm, d), lambda i: (i, 0))
    return _pcall(
        body, name="ml_pre", grid=(s_len // tm,),
        in_specs=[pl.BlockSpec((tm, d), lambda i: (i, 2)),
                  pl.BlockSpec((HALO, d), lambda i: (jnp.maximum(i * per - 1, 0), 2)),
                  pl.BlockSpec((CONV_WIDTH, d), lambda i: (0, 0)), vec,
                  whole3(wq_b), whole3(wk_b), whole3(wv_b), whole2(wif_b), whole2(wift_b), whole2(b_if),
                  whole2(b_ift)],
        out_specs=[row, row, row, pl.BlockSpec((tm, ng), lambda i: (i, 0)), pl.BlockSpec((ng, tm), lambda i: (0, i))],
        out_shape=[jax.ShapeDtypeStruct((s_len, d), BF16)] * 3
        + [jax.ShapeDtypeStruct((s_len, ng), F32), jax.ShapeDtypeStruct((ng, s_len), F32)],
        compiler_params=_seq(),
    )(u, u, conv_w, conv_b, wq_b, wk_b, wv_b, wif_b, wift_b, b_if, b_ift)


def _chunk_gates(gt, gtt, h, heads, tril, triu):
    li_c = gt[:, h:h + 1]
    li_r = gtt[h:h + 1, :]
    gf_c = gt[:, heads + h:heads + h + 1]
    lf_c = _log_sigmoid(gf_c)
    lf_r = _log_sigmoid(gtt[heads + h:heads + h + 1, :])
    b_c = _tri_dot_left(tril, lf_c)
    b_r = _tri_dot_right(lf_r, triu)
    return li_c, li_r, gf_c, b_c, b_r


def _chunk_weights(li_c, li_r, b_c, b_r, m_prev, causal):
    lc = b_c.shape[0]
    b_last = b_c[lc - 1:lc, :]
    dmat = jnp.where(causal, b_c - b_r + li_r, -jnp.inf)
    m_inter = b_c + m_prev
    m_t = jnp.maximum(m_inter, jnp.max(dmat, axis=1, keepdims=True))
    w_intra = jnp.exp(dmat - m_t)
    w_inter = jnp.exp(m_inter - m_t)
    g_c = b_last - b_c + li_c
    m_new = jnp.maximum(b_last + m_prev, jnp.max(g_c, axis=0, keepdims=True))
    w_state = jnp.exp(g_c - m_new)
    decay = jnp.exp(b_last + m_prev - m_new)
    return m_t, w_intra, w_inter, m_new, w_state, decay


def _tri_masks(lc):
    r = lax.broadcasted_iota(jnp.int32, (lc, lc), 0)
    c = lax.broadcasted_iota(jnp.int32, (lc, lc), 1)
    causal = r >= c
    return causal, causal.astype(BF16), (r <= c).astype(BF16)


def _mlstm_fwd(q, k, v, gt, gtt, u, ml_g):
    s_len, d = q.shape
    ng = gt.shape[1]
    heads = ng // 2
    hd = d // heads
    lc = ML_CHUNK
    nc = s_len // lc
    kscale = hd ** -0.5

    def body(q_ref, k_ref, v_ref, gt_ref, gtt_ref, o_ref, z_ref, g_ref,
             cell_ref, y_ref, cst_ref, nst_ref, mst_ref, cs, ns, ms):
        @pl.when(pl.program_id(0) == 0)
        def _():
            cs[...] = jnp.zeros_like(cs)
            ns[...] = jnp.zeros_like(ns)
            ms[...] = jnp.zeros_like(ms)

        causal, tril, triu = _tri_masks(lc)
        gtv, gttv = gt_ref[...], gtt_ref[...]
        for h in range(heads):
            hs = slice(h * hd, (h + 1) * hd)
            li_c, li_r, _, b_c, b_r = _chunk_gates(gtv, gttv, h, heads, tril, triu)
            m_prev = ms[h][:, 0:1]
            m_t, w_intra, w_inter, m_new, w_state, decay = _chunk_weights(li_c, li_r, b_c, b_r, m_prev, causal)
            qb = q_ref[:, hs]
            ks = k_ref[:, hs].astype(F32) * kscale
            kb = _bf(ks)
            vb = v_ref[:, hs]
            c_old = cs[h]
            n_old = ns[h]
            cst_ref[0, h] = _bf(c_old)
            nst_ref[0, h] = n_old
            mst_ref[0, h] = ms[h]
            s = _dot_nt(qb, kb) * w_intra
            num = _dot(_bf(s), vb) + w_inter * _dot(qb, _bf(c_old))
            den = _rowsum(s) + w_inter * _rowsum(qb.astype(F32) * n_old)
            cell = num / jnp.maximum(jnp.abs(den), jnp.exp(-m_t))
            kw = ks * w_state
            cs[h] = decay * c_old + _dot_tn(_bf(kw), vb)
            ns[h] = decay * n_old + _colsum(kw)
            ms[h] = jnp.broadcast_to(m_new, ms[h].shape)
            cell_ref[:, hs] = cell
            hm = _sigmoid(o_ref[:, hs]) * cell
            hn = hm * lax.rsqrt(jnp.mean(hm * hm, axis=-1, keepdims=True) + EPS)
            z = z_ref[:, hs]
            y_ref[:, hs] = _bf((hn * g_ref[:, hs]) * (z * _sigmoid(z)))

    row = pl.BlockSpec((lc, d), lambda c: (c, 0))
    return _pcall(
        body, name="mlstm_fwd", grid=(nc,),
        in_specs=[row, row, row, pl.BlockSpec((lc, ng), lambda c: (c, 0)), pl.BlockSpec((ng, lc), lambda c: (0, c)),
                  pl.BlockSpec((lc, d), lambda c: (c, 3)), pl.BlockSpec((lc, d), lambda c: (c, 4)),
                  pl.BlockSpec((1, d), lambda c: (0, 0))],
        out_specs=[row, row,
                   pl.BlockSpec((1, heads, hd, hd), lambda c: (c, 0, 0, 0)),
                   pl.BlockSpec((1, heads, 1, hd), lambda c: (c, 0, 0, 0)),
                   pl.BlockSpec((1, heads, 1, 128), lambda c: (c, 0, 0, 0))],
        out_shape=[jax.ShapeDtypeStruct((s_len, d), F32), jax.ShapeDtypeStruct((s_len, d), BF16),
                   jax.ShapeDtypeStruct((nc, heads, hd, hd), BF16),
                   jax.ShapeDtypeStruct((nc, heads, 1, hd), F32),
                   jax.ShapeDtypeStruct((nc, heads, 1, 128), F32)],
        scratch_shapes=[pltpu.VMEM((heads, hd, hd), F32), pltpu.VMEM((heads, 1, hd), F32),
                        pltpu.VMEM((heads, 1, 128), F32)],
        compiler_params=_seq(),
    )(q, k, v, gt, gtt, u, u, ml_g)


def _out_proj(y_rg, y_ml, w_out_b, x, gate):
    s_len, d = x.shape
    tm = _tile(s_len, 512)

    def body(a_ref, b_ref, w_ref, x_ref, g_ref, y_ref, xn_ref):
        y = _dot(a_ref[...], w_ref[0:d, :]) + _dot(b_ref[...], w_ref[d:2 * d, :])
        y_ref[...] = y
        xn_ref[...] = x_ref[...] + g_ref[...] * y

    row = pl.BlockSpec((tm, d), lambda i: (i, 0))
    return _pcall(
        body, name="out_proj", grid=(s_len // tm,),
        in_specs=[row, row, pl.BlockSpec((2 * d, d), lambda i: (0, 0)), row, pl.BlockSpec((1, d), lambda i: (0, 0))],
        out_specs=[row, row],
        out_shape=[jax.ShapeDtypeStruct((s_len, d), F32)] * 2,
        compiler_params=_seq(),
    )(y_rg, y_ml, w_out_b, x, gate)


def _final_loss(x, g, target):
    s_len, d = x.shape
    tm = _tile(s_len, 256)

    def body(x_ref, g_ref, t_ref, dx_ref, dg_ref, loss_ref):
        @pl.when(pl.program_id(0) == 0)
        def _():
            dg_ref[...] = jnp.zeros_like(dg_ref)
            loss_ref[...] = jnp.zeros_like(loss_ref)

        xv = x_ref[...]
        r = lax.rsqrt(jnp.mean(xv * xv, axis=-1, keepdims=True) + EPS)
        xn = xv * r
        err = xn * g_ref[...] - t_ref[...]
        loss_ref[...] += 0.5 * jnp.sum(jnp.mean(err * err, axis=-1, keepdims=True))
        dout = err * (1.0 / d)
        dg_ref[...] += _colsum(dout * xn)
        dxn = dout * g_ref[...]
        dx_ref[...] = r * (dxn - xn * jnp.mean(dxn * xn, axis=-1, keepdims=True))

    row = pl.BlockSpec((tm, d), lambda i: (i, 0))
    vec = pl.BlockSpec((1, d), lambda i: (0, 0))
    return _pcall(
        body, name="final_loss", grid=(s_len // tm,),
        in_specs=[row, vec, row],
        out_specs=[row, vec, pl.BlockSpec((1, 128), lambda i: (0, 0))],
        out_shape=[jax.ShapeDtypeStruct((s_len, d), F32), jax.ShapeDtypeStruct((1, d), F32),
                   jax.ShapeDtypeStruct((1, 128), F32)],
        compiler_params=_seq(),
    )(x, g, target)


def _out_bwd(dxn, y, gate, w_out_b):
    s_len, d = dxn.shape
    tm = _tile(s_len, 512)

    def body(dx_ref, y_ref, g_ref, w_ref, dg_ref, dy_ref, da_ref, db_ref):
        @pl.when(pl.program_id(0) == 0)
        def _():
            dg_ref[...] = jnp.zeros_like(dg_ref)

        dx = dx_ref[...]
        dg_ref[...] += _colsum(dx * y_ref[...])
        dy = _bf(g_ref[...] * dx)
        dy_ref[...] = dy
        da_ref[...] = _dot_nt(dy, w_ref[0:d, :])
        db_ref[...] = _dot_nt(dy, w_ref[d:2 * d, :])

    row = pl.BlockSpec((tm, d), lambda i: (i, 0))
    vec = pl.BlockSpec((1, d), lambda i: (0, 0))
    return _pcall(
        body, name="out_bwd", grid=(s_len // tm,),
        in_specs=[row, row, vec, pl.BlockSpec((2 * d, d), lambda i: (0, 0))],
        out_specs=[vec, row, row, row],
        out_shape=[jax.ShapeDtypeStruct((1, d), F32), jax.ShapeDtypeStruct((s_len, d), BF16),
                   jax.ShapeDtypeStruct((s_len, d), F32), jax.ShapeDtypeStruct((s_len, d), F32)],
        compiler_params=_seq(),
    )(dxn, y, gate, w_out_b)


def _grad_matmul(a, b):
    s_len, m = a.shape
    n = b.shape[1]
    tk = _tile(s_len, 512)

    def body(a_ref, b_ref, o_ref):
        @pl.when(pl.program_id(0) == 0)
        def _():
            o_ref[...] = jnp.zeros_like(o_ref)

        o_ref[...] += _dot_tn(a_ref[...], b_ref[...])

    return _pcall(
        body, name="grad_matmul", grid=(s_len // tk,),
        in_specs=[pl.BlockSpec((tk, m), lambda t: (t, 0)), pl.BlockSpec((tk, n), lambda t: (t, 0))],
        out_specs=pl.BlockSpec((m, n), lambda t: (0, 0)),
        out_shape=jax.ShapeDtypeStruct((m, n), F32),
        compiler_params=_seq(),
    )(a, b)


def _mlstm_bwd(q, k, v, gt, gtt, cst, nst, mst, cell, u, ml_g, d_y, wif_b):
    s_len, d = q.shape
    ng = gt.shape[1]
    heads = ng // 2
    hd = d // heads
    lc = ML_CHUNK
    nc = s_len // lc
    kscale = hd ** -0.5

    def body(q_ref, k_ref, v_ref, gt_ref, gtt_ref, cst_ref, nst_ref, mst_ref, cell_ref, o_ref, z_ref, g_ref, dy_ref,
             wif_ref, dq_ref, dk_ref, dv_ref, dgt_ref, do_ref, dz_ref, dg_ref, dcs, dns, dqs, dks, dvs):
        @pl.when(pl.program_id(0) == 0)
        def _():
            dcs[...] = jnp.zeros_like(dcs)
            dns[...] = jnp.zeros_like(dns)
            dg_ref[...] = jnp.zeros_like(dg_ref)

        causal, tril, triu = _tri_masks(lc)
        tril_strict = (tril.astype(F32) - (tril * triu).astype(F32)).astype(BF16)
        gtv, gttv = gt_ref[...], gtt_ref[...]
        lane = lax.broadcasted_iota(jnp.int32, (lc, ng), 1)
        dgt = jnp.zeros((lc, ng), F32)
        for h in range(heads):
            hs = slice(h * hd, (h + 1) * hd)
            li_c, li_r, gf_c, b_c, b_r = _chunk_gates(gtv, gttv, h, heads, tril, triu)
            m_prev = mst_ref[0, h][:, 0:1]
            m_t, w_intra, w_inter, _, w_state, decay = _chunk_weights(li_c, li_r, b_c, b_r, m_prev, causal)
            qb = q_ref[:, hs]
            qf = qb.astype(F32)
            ks = k_ref[:, hs].astype(F32) * kscale
            kb = _bf(ks)
            vb = v_ref[:, hs]
            c_b = cst_ref[0, h]
            n_old = nst_ref[0, h]
            s = _dot_nt(qb, kb) * w_intra
            den = _rowsum(s) + w_inter * _rowsum(qf * n_old)
            floor = jnp.exp(-m_t)
            dstab = jnp.maximum(jnp.abs(den), floor)
            cell = cell_ref[:, hs]
            o = o_ref[:, hs]
            so = _sigmoid(o)
            hm = so * cell
            rinv = lax.rsqrt(jnp.mean(hm * hm, axis=-1, keepdims=True) + EPS)
            hn = hm * rinv
            z = z_ref[:, hs]
            sgz = _sigmoid(z)
            sz = z * sgz
            gh = g_ref[:, hs]
            dy = dy_ref[:, hs]
            dz_ref[:, hs] = _bf(dy * (hn * gh) * _dsilu(z, sgz))
            dg_ref[:, hs] += _colsum(dy * hn * sz)
            dhn = dy * gh * sz
            dhm = rinv * (dhn - hn * jnp.mean(dhn * hn, axis=-1, keepdims=True))
            do_ref[:, hs] = _bf(dhm * cell * so * (1.0 - so))
            dcell = dhm * so
            dnum = dcell / dstab
            dnb = _bf(dnum)
            dden = -_rowsum(dcell * cell) / dstab * jnp.where(jnp.abs(den) > floor, jnp.where(den > 0.0, 1.0, -1.0), 0.0)
            dst = _dot_nt(dnb, vb) + dden
            dsdb = _bf(dst * w_intra)
            dc_out = dcs[h]
            dn_out = dns[h]
            dcb = _bf(dc_out)
            dq_inter = w_inter * (_dot_nt(dnb, c_b) + dden * n_old)
            dk_inter = w_state * (_dot_nt(vb, dcb) + dn_out)
            dq = _dot(dsdb, kb) + dq_inter
            dk = _dot_tn(dsdb, qb) + dk_inter
            dv = _dot_tn(_bf(s), dnb) + _dot(_bf(ks * w_state), dcb)
            wq = w_inter * qf
            dcs[h] = decay * dc_out + _dot_tn(_bf(wq), dnb)
            dns[h] = decay * dn_out + _colsum(wq * dden)
            pmat = dst * s
            p_rows = _rowsum(pmat)
            p_cols = _rowsum(pmat.T)
            q_in = _rowsum(qf * dq_inter)
            k_in = _rowsum(ks * dk_inter)
            across = decay * (jnp.sum(dc_out * c_b.astype(F32), keepdims=True) + jnp.sum(dn_out * n_old, keepdims=True))
            dli = p_cols + k_in
            dlf = _tri_dot_left(triu, p_rows - p_cols + q_in) + _tri_dot_left(tril_strict, k_in) + across
            dgf = dlf * _sigmoid(-gf_c)
            dgt = dgt + jnp.where(lane == h, dli, 0.0) + jnp.where(lane == heads + h, dgf, 0.0)
            dqs[:, hs] = dq
            dks[:, hs] = dk * kscale
            dvs[:, hs] = dv
        dgt_ref[...] = dgt
        dgb = _bf(dgt)
        dq_ref[...] = _bf(dqs[...] + _dot_nt(dgb, wif_ref[0:d, :]))
        dk_ref[...] = _bf(dks[...] + _dot_nt(dgb, wif_ref[d:2 * d, :]))
        dv_ref[...] = _bf(dvs[...] + _dot_nt(dgb, wif_ref[2 * d:3 * d, :]))

    rev = lambda c: nc - 1 - c
    row = pl.BlockSpec((lc, d), lambda c: (rev(c), 0))
    return _pcall(
        body, name="mlstm_bwd", grid=(nc,),
        in_specs=[row, row, row, pl.BlockSpec((lc, ng), lambda c: (rev(c), 0)),
                  pl.BlockSpec((ng, lc), lambda c: (0, rev(c))),
                  pl.BlockSpec((1, heads, hd, hd), lambda c: (rev(c), 0, 0, 0)),
                  pl.BlockSpec((1, heads, 1, hd), lambda c: (rev(c), 0, 0, 0)),
                  pl.BlockSpec((1, heads, 1, 128), lambda c: (rev(c), 0, 0, 0)),
                  row, pl.BlockSpec((lc, d), lambda c: (rev(c), 3)), pl.BlockSpec((lc, d), lambda c: (rev(c), 4)),
                  pl.BlockSpec((1, d), lambda c: (0, 0)), row,
                  pl.BlockSpec((3 * d, ng), lambda c: (0, 0))],
        out_specs=[row, row, row, pl.BlockSpec((lc, ng), lambda c: (rev(c), 0)), row, row,
                   pl.BlockSpec((1, d), lambda c: (0, 0))],
        out_shape=[jax.ShapeDtypeStruct((s_len, d), BF16)] * 3 + [jax.ShapeDtypeStruct((s_len, ng), F32)]
        + [jax.ShapeDtypeStruct((s_len, d), BF16)] * 2 + [jax.ShapeDtypeStruct((1, d), F32)],
        scratch_shapes=[pltpu.VMEM((heads, hd, hd), F32), pltpu.VMEM((heads, 1, hd), F32)]
        + [pltpu.VMEM((lc, d), F32)] * 3,
        compiler_params=_seq(),
    )(q, k, v, gt, gtt, cst, nst, mst, cell, u, u, ml_g, d_y, wif_b)


def _ml_pre_bwd(dq, dk, dv, u, conv_w, conv_b, wq_b, wk_b, wv_b):
    s_len = u.shape[0]
    d = conv_w.shape[1]
    heads, hd, _ = wq_b.shape
    tm = _tile(s_len, 256)
    per = tm // HALO

    def body(dq_ref, dk_ref, dv_ref, x_ref, xp_ref, cw_ref, cb_ref, wq_ref, wk_ref, wv_ref,
             dpre_ref, dx_ref, gq_ref, gk_ref, gv_ref):
        i = pl.program_id(0)

        @pl.when(i == 0)
        def _():
            gq_ref[...] = jnp.zeros_like(gq_ref)
            gk_ref[...] = jnp.zeros_like(gk_ref)
            gv_ref[...] = jnp.zeros_like(gv_ref)

        prev = jnp.where(i == 0, 0.0, xp_ref[...])
        xm = x_ref[...]
        pre = _conv_fwd(jnp.concatenate([prev, xm], axis=0), cw_ref, cb_ref)
        sg = _sigmoid(pre)
        xcb = _bf(pre * sg)
        xmb = _bf(xm)
        for h in range(heads):
            hs = slice(h * hd, (h + 1) * hd)
            dqh, dkh, dvh = dq_ref[:, hs], dk_ref[:, hs], dv_ref[:, hs]
            dxc = _dot_nt(dqh, wq_ref[h]) + _dot_nt(dkh, wk_ref[h])
            dpre_ref[:, hs] = dxc * _dsilu(pre[:, hs], sg[:, hs])
            dx_ref[:, hs] = _dot_nt(dvh, wv_ref[h])
            gq_ref[h] += _dot_tn(xcb[:, hs], dqh)
            gk_ref[h] += _dot_tn(xcb[:, hs], dkh)
            gv_ref[h] += _dot_tn(xmb[:, hs], dvh)

    row = pl.BlockSpec((tm, d), lambda i: (i, 0))
    vec = pl.BlockSpec((1, d), lambda i: (0, 0))
    whole3 = lambda a: pl.BlockSpec(a.shape, lambda i: (0, 0, 0))
    return _pcall(
        body, name="ml_pre_bwd", grid=(s_len // tm,),
        in_specs=[row, row, row, pl.BlockSpec((tm, d), lambda i: (i, 2)),
                  pl.BlockSpec((HALO, d), lambda i: (jnp.maximum(i * per - 1, 0), 2)),
                  pl.BlockSpec((CONV_WIDTH, d), lambda i: (0, 0)), vec, whole3(wq_b), whole3(wk_b), whole3(wv_b)],
        out_specs=[row, row, whole3(wq_b), whole3(wq_b), whole3(wq_b)],
        out_shape=[jax.ShapeDtypeStruct((s_len, d), F32)] * 2 + [jax.ShapeDtypeStruct(wq_b.shape, F32)] * 3,
        compiler_params=_seq(),
    )(dq, dk, dv, u, u, conv_w, conv_b, wq_b, wk_b, wv_b)


def _conv_bwd(dpre, u, col, conv_w, direct=None):
    s_len, d = dpre.shape
    tm = _tile(s_len, 256)
    per = tm // HALO
    nt = s_len // tm

    def body(*refs):
        if direct is None:
            dp_ref, dn_ref, x_ref, xp_ref, cw_ref, dx_ref, gw_ref, gb_ref = refs
        else:
            dp_ref, dn_ref, x_ref, xp_ref, cw_ref, dir_ref, dx_ref, gw_ref, gb_ref = refs
        i = pl.program_id(0)

        @pl.when(i == 0)
        def _():
            gw_ref[...] = jnp.zeros_like(gw_ref)
            gb_ref[...] = jnp.zeros_like(gb_ref)

        dp = dp_ref[...]
        nxt = jnp.where(i == nt - 1, 0.0, dn_ref[...])
        dwin = jnp.concatenate([dp, nxt], axis=0)
        prev = jnp.where(i == 0, 0.0, xp_ref[...])
        xwin = jnp.concatenate([prev, x_ref[...]], axis=0)
        acc = cw_ref[CONV_WIDTH - 1:CONV_WIDTH, :] * dp
        if direct is not None:
            acc = acc + dir_ref[...]
        gw_ref[CONV_WIDTH - 1:CONV_WIDTH, :] += _colsum(dp * xwin[HALO:])
        for k in range(CONV_WIDTH - 1):
            sft = CONV_WIDTH - 1 - k
            acc = acc + cw_ref[k:k + 1, :] * _shift_up(dwin, sft)[0:tm]
            gw_ref[k:k + 1, :] += _colsum(dp * _shift_down(xwin, sft)[HALO:])
        gb_ref[...] += _colsum(dp)
        dx_ref[...] = _bf(acc)

    row = pl.BlockSpec((tm, d), lambda i: (i, 0))
    in_specs = [row, pl.BlockSpec((HALO, d), lambda i: (jnp.minimum((i + 1) * per, s_len // HALO - 1), 0)),
                pl.BlockSpec((tm, d), lambda i: (i, col)),
                pl.BlockSpec((HALO, d), lambda i: (jnp.maximum(i * per - 1, 0), col)),
                pl.BlockSpec((CONV_WIDTH, d), lambda i: (0, 0))]
    args = [dpre, dpre, u, u, conv_w]
    if direct is not None:
        in_specs.append(row)
        args.append(direct)
    return _pcall(
        body, name="conv_bwd", grid=(nt,),
        in_specs=in_specs,
        out_specs=[row, pl.BlockSpec((CONV_WIDTH, d), lambda i: (0, 0)), pl.BlockSpec((1, d), lambda i: (0, 0))],
        out_shape=[jax.ShapeDtypeStruct((s_len, d), BF16), jax.ShapeDtypeStruct((CONV_WIDTH, d), F32),
                   jax.ShapeDtypeStruct((1, d), F32)],
        compiler_params=_seq(),
    )(*args)


def _rg_bwd(d_y, u, hh, conv_w, conv_b, wa_b, ba, wx_b, bx, lam):
    s_len = u.shape[0]
    d = conv_w.shape[1]
    heads, hd, _ = wa_b.shape
    tm = _tile(s_len, 256)
    per = tm // HALO
    nt = s_len // tm

    def body(dy_ref, x_ref, xp_ref, z_ref, hh_ref, hp_ref, cw_ref, cb_ref, wa_ref, ba_ref, wx_ref, bx_ref, lam_ref,
             dxc_ref, dz_ref, gwa_ref, gwx_ref, gba_ref, gbx_ref, glam_ref, carry, gbuf):
        i = pl.program_id(0)
        first = i == nt - 1

        @pl.when(i == 0)
        def _():
            carry[...] = jnp.zeros_like(carry)
            gwa_ref[...] = jnp.zeros_like(gwa_ref)
            gwx_ref[...] = jnp.zeros_like(gwx_ref)
            gba_ref[...] = jnp.zeros_like(gba_ref)
            gbx_ref[...] = jnp.zeros_like(gbx_ref)
            glam_ref[...] = jnp.zeros_like(glam_ref)

        prev = jnp.where(first, 0.0, xp_ref[...])
        xc = _conv_fwd(jnp.concatenate([prev, x_ref[...]], axis=0), cw_ref, cb_ref)
        r, ig, sp, log_a, a, mult = _rg_gates(xc, wa_ref, ba_ref, wx_ref, bx_ref, lam_ref)
        z = z_ref[...]
        sgz = _sigmoid(z)
        dy = dy_ref[...]
        hh_v = hh_ref[...]
        dz_ref[...] = _bf(dy * hh_v * _dsilu(z, sgz))
        dhh = dy * (z * sgz)
        rows = lax.broadcasted_iota(jnp.int32, a.shape, 0)
        coef = jnp.where(rows == tm - 1, carry[1:2, :], _shift_up(a, 1))
        ca, cu = _scan_groups(coef, dhh, reverse=True)
        c = carry[0:1, :]
        for j in range(per - 1, -1, -1):
            blk = ca[j * 8:(j + 1) * 8] * c + cu[j * 8:(j + 1) * 8]
            gbuf[j * 8:(j + 1) * 8, :] = blk
            c = blk[0:1]
        carry[0:1, :] = c
        carry[1:2, :] = a[0:1]
        g = gbuf[...]
        hprev_tile = jnp.where(first, 0.0, hp_ref[...])
        hprev = _shift_down(jnp.concatenate([hprev_tile, hh_v], axis=0), 1)[HALO:]
        da = g * hprev
        gx_ = g * xc
        d_mult = gx_ * ig
        d_ig = gx_ * mult
        dxc = g * mult * ig
        a2 = jnp.exp(2.0 * log_a)
        dlog_a = da * a - d_mult * (a2 / mult)
        d_r = dlog_a * ((-RG_C) * sp)
        glam_ref[...] += _colsum(dlog_a * ((-RG_C) * r)) * (-_sigmoid(-lam_ref[...]))
        d_ga = d_r * r * (1.0 - r)
        d_gx = d_ig * ig * (1.0 - ig)
        gba_ref[...] += _colsum(d_ga)
        gbx_ref[...] += _colsum(d_gx)
        xb = _bf(xc)
        dgab = _bf(d_ga)
        dgxb = _bf(d_gx)
        for h in range(heads):
            hs = slice(h * hd, (h + 1) * hd)
            dxc_ref[:, hs] = dxc[:, hs] + _dot_nt(dgab[:, hs], wa_ref[h]) + _dot_nt(dgxb[:, hs], wx_ref[h])
            gwa_ref[h] += _dot_tn(xb[:, hs], dgab[:, hs])
            gwx_ref[h] += _dot_tn(xb[:, hs], dgxb[:, hs])

    rev = lambda i: nt - 1 - i
    row = pl.BlockSpec((tm, d), lambda i: (rev(i), 0))
    halo_prev = lambda col: pl.BlockSpec((HALO, d), lambda i: (jnp.maximum(rev(i) * per - 1, 0), col))
    vec = pl.BlockSpec((1, d), lambda i: (0, 0))
    whole3 = lambda a: pl.BlockSpec(a.shape, lambda i: (0, 0, 0))
    return _pcall(
        body, name="rg_bwd", grid=(nt,),
        in_specs=[row, row, halo_prev(0), pl.BlockSpec((tm, d), lambda i: (rev(i), 1)), row, halo_prev(0),
                  pl.BlockSpec((CONV_WIDTH, d), lambda i: (0, 0)), vec, whole3(wa_b), vec, whole3(wx_b), vec, vec],
        out_specs=[row, row, whole3(wa_b), whole3(wa_b), vec, vec, vec],
        out_shape=[jax.ShapeDtypeStruct((s_len, d), F32), jax.ShapeDtypeStruct((s_len, d), BF16),
                   jax.ShapeDtypeStruct(wa_b.shape, F32), jax.ShapeDtypeStruct(wa_b.shape, F32)]
        + [jax.ShapeDtypeStruct((1, d), F32)] * 3,
        scratch_shapes=[pltpu.VMEM((8, d), F32), pltpu.VMEM((tm, d), F32)],
        compiler_params=_seq(),
    )(d_y, u, u, u, hh, hh, conv_w, conv_b, wa_b, ba, wx_b, bx, lam)


def _in_bwd(du, w_in_b, x, dxn, g, scale):
    s_len, d = x.shape
    tm = _tile(s_len, 256)
    npc = len(du)

    def body(*refs):
        du_refs = refs[:npc]
        w_ref, x_ref, dxn_ref, g_ref, sc_ref, dx_ref, dsh_ref, dsc_ref, dg_ref = refs[npc:]

        @pl.when(pl.program_id(0) == 0)
        def _():
            dsh_ref[...] = jnp.zeros_like(dsh_ref)
            dsc_ref[...] = jnp.zeros_like(dsc_ref)
            dg_ref[...] = jnp.zeros_like(dg_ref)

        dh = _dot_nt(du_refs[0][...], w_ref[:, 0:d])
        for p in range(1, npc):
            dh = dh + _dot_nt(du_refs[p][...], w_ref[:, p * d:(p + 1) * d])
        xv = x_ref[...]
        r = lax.rsqrt(jnp.mean(xv * xv, axis=-1, keepdims=True) + EPS)
        xn = xv * r
        gv = g_ref[...]
        onesc = 1.0 + sc_ref[...]
        dsh_ref[...] += _colsum(dh)
        dsc_ref[...] += _colsum(dh * (xn * gv))
        dg_ref[...] += _colsum(dh * xn * onesc)
        dxh = dh * (gv * onesc)
        dx_ref[...] = dxn_ref[...] + r * (dxh - xn * jnp.mean(dxh * xn, axis=-1, keepdims=True))

    row = pl.BlockSpec((tm, d), lambda i: (i, 0))
    vec = pl.BlockSpec((1, d), lambda i: (0, 0))
    return _pcall(
        body, name="in_bwd", grid=(s_len // tm,),
        in_specs=[row] * npc + [pl.BlockSpec(w_in_b.shape, lambda i: (0, 0)), row, row, vec, vec],
        out_specs=[row, vec, vec, vec],
        out_shape=[jax.ShapeDtypeStruct((s_len, d), F32)] + [jax.ShapeDtypeStruct((1, d), F32)] * 3,
        compiler_params=_seq(),
    )(*du, w_in_b, x, dxn, g, scale)


def _layer_fwd(x, p):
    h_b, u = _ln_inproj(x, p["norm_g"], p["scale"], p["shift"], p["w_in_b"])
    hh, y_rg = _rg_fwd(u, p["rg_conv_w"], p["rg_conv_b"], p["rg_wa_b"], p["rg_ba"], p["rg_wx_b"], p["rg_bx"],
                       p["rg_lam"])
    q, k, v, gt, gtt = _ml_pre(u, p["ml_conv_w"], p["ml_conv_b"], p["wq_b"], p["wk_b"], p["wv_b"], p["wif_b"],
                               p["wift_b"], p["b_if"], p["b_ift"])
    cell, y_ml, cst, nst, mst = _mlstm_fwd(q, k, v, gt, gtt, u, p["ml_g"])
    y, x_new = _out_proj(y_rg, y_ml, p["w_out_b"], x, p["gate"])
    saved = dict(x=x, h_b=h_b, u=u, hh=hh, y_rg=y_rg, q=q, k=k, v=v, gt=gt, gtt=gtt, cell=cell, y_ml=y_ml,
                 cst=cst, nst=nst, mst=mst, y=y)
    return x_new, saved


def _layer_bwd(dxn, p, s):
    u = s["u"]
    d_gate, dy_b, d_yrg, d_yml = _out_bwd(dxn, s["y"], p["gate"], p["w_out_b"])
    g_w_out = jnp.concatenate([_grad_matmul(s["y_rg"], dy_b), _grad_matmul(s["y_ml"], dy_b)], axis=0)
    dq, dk, dv, dgt, d_mlo, d_mlz, g_ml_g = _mlstm_bwd(s["q"], s["k"], s["v"], s["gt"], s["gtt"], s["cst"], s["nst"],
                                                       s["mst"], s["cell"], u, p["ml_g"], d_yml, p["wif_b"])
    qkv = jnp.concatenate([s["q"], s["k"], s["v"]], axis=1)
    g_w_if = _grad_matmul(qkv, _bf(dgt))
    g_b_if = jnp.sum(dgt, axis=0, keepdims=True)
    dpre_ml, dmlx_direct, g_wq, g_wk, g_wv = _ml_pre_bwd(dq, dk, dv, u, p["ml_conv_w"], p["ml_conv_b"], p["wq_b"],
                                                         p["wk_b"], p["wv_b"])
    d_mlx, g_ml_cw, g_ml_cb = _conv_bwd(dpre_ml, u, 2, p["ml_conv_w"], direct=dmlx_direct)
    dxc_rg, d_rgz, g_wa, g_wx, g_ba, g_bx, g_lam = _rg_bwd(d_yrg, u, s["hh"], p["rg_conv_w"], p["rg_conv_b"],
                                                           p["rg_wa_b"], p["rg_ba"], p["rg_wx_b"], p["rg_bx"],
                                                           p["rg_lam"])
    d_rgx, g_rg_cw, g_rg_cb = _conv_bwd(dxc_rg, u, 0, p["rg_conv_w"])
    du = [d_rgx, d_rgz, d_mlx, d_mlo, d_mlz]
    g_w_in = jnp.concatenate([_grad_matmul(s["h_b"], t) for t in du], axis=1)
    dx, d_shift, d_scale, g_norm_g = _in_bwd(du, p["w_in_b"], s["x"], dxn, p["norm_g"], p["scale"])
    grads = dict(norm_g=g_norm_g, w_in=g_w_in, rg_conv_w=g_rg_cw, rg_conv_b=g_rg_cb, rg_w_a=g_wa, rg_b_a=g_ba,
                 rg_w_x=g_wx, rg_b_x=g_bx, rg_lambda=g_lam, ml_conv_w=g_ml_cw, ml_conv_b=g_ml_cb, ml_w_q=g_wq,
                 ml_w_k=g_wk, ml_w_v=g_wv, ml_w_if=g_w_if, ml_b_if=g_b_if, ml_norm_g=g_ml_g, w_out=g_w_out)
    return dx, grads, jnp.concatenate([d_shift, d_scale, d_gate], axis=1)


def _trunk_fwd_bwd(x, target, final_g, layers):
    saved = []
    for p in layers:
        x, s = _layer_fwd(x, p)
        saved.append(s)
    dx, g_final, loss = _final_loss(x, final_g, target)
    grads, dmods = [], []
    for p, s in zip(reversed(layers), reversed(saved)):
        dx, g, dm = _layer_bwd(dx, p, s)
        grads.append(g)
        dmods.append(dm)
    return loss, dx, g_final, grads[::-1], dmods[::-1]


def _me():
    return lax.axis_index("x"), lax.axis_index("y"), lax.axis_index("c")


def _remote(src, dst, send_sem, recv_sem, to):
    return pltpu.make_async_remote_copy(src_ref=src, dst_ref=dst, send_sem=send_sem, recv_sem=recv_sem,
                                        device_id=to, device_id_type=MESH)


def _all_gather8(block, space):
    m_per, n = block.shape

    def body(x_ref, out_ref, send_sems, recv_sems, local_sem):
        x, y, c = _me()
        me, sibling = (x, y, c), (x, y, 1 - c)
        chips = [(1 - x, y), (x, 1 - y), (1 - x, 1 - y)]

        def rows(px, py, pc):
            return out_ref.at[pl.ds((4 * px + 2 * py + pc) * m_per, m_per), :]

        def copy(k, blk, to, src=None):
            return _remote(rows(*blk) if src is None else src, rows(*blk), send_sems.at[k], recv_sems.at[k], to)

        mine = pltpu.make_async_copy(x_ref, rows(*me), local_sem)
        mine.start()
        first = [copy(0, me, sibling, src=x_ref)]
        first += [copy(1 + j, me, (*chip, c), src=x_ref) for j, chip in enumerate(chips)]
        for cp in first:
            cp.start()
        passed = [copy(4 + j, (*chip, c), sibling) for j, chip in enumerate(chips)]
        for j, chip in enumerate(chips):
            copy(1 + j, (*chip, c), me).wait_recv()
            passed[j].start()
        copy(0, sibling, me).wait_recv()
        for j, chip in enumerate(chips):
            copy(4 + j, (*chip, 1 - c), me).wait_recv()
        for cp in first + passed:
            cp.wait_send()
        mine.wait()

    return _pcall(
        body, name="all_gather8",
        out_shape=jax.ShapeDtypeStruct((8 * m_per, n), block.dtype),
        in_specs=[pl.BlockSpec(memory_space=space)], out_specs=pl.BlockSpec(memory_space=space),
        scratch_shapes=[pltpu.SemaphoreType.DMA((7,)), pltpu.SemaphoreType.DMA((7,)), pltpu.SemaphoreType.DMA],
    )(block)


def _sib_swap(arrs):
    n = len(arrs)

    def body(*refs):
        src, dst = refs[:n], refs[n:2 * n]
        send_sems, recv_sems = refs[2 * n:]
        x, y, c = _me()
        copies = [_remote(src[i], dst[i], send_sems.at[i], recv_sems.at[i], (x, y, 1 - c)) for i in range(n)]
        for cp in copies:
            cp.start()
        for cp in copies:
            cp.wait_recv()
        for cp in copies:
            cp.wait_send()

    hbm = pl.BlockSpec(memory_space=pltpu.HBM)
    return _pcall(
        body, name="sib_swap",
        out_shape=[jax.ShapeDtypeStruct(a.shape, a.dtype) for a in arrs],
        in_specs=[hbm] * n, out_specs=[hbm] * n,
        scratch_shapes=[pltpu.SemaphoreType.DMA((n,)), pltpu.SemaphoreType.DMA((n,))],
    )(*arrs)


def _chip_exchange(arrs):
    n = len(arrs)

    def body(*refs):
        src, dst = refs[:n], refs[n:2 * n]
        send_sems, recv_sems, local_sems = refs[2 * n:]
        x, y, c = _me()
        me_s = 2 * x + y
        chips = [(1 - x, y), (x, 1 - y), (1 - x, 1 - y)]
        local = [pltpu.make_async_copy(src[i].at[me_s], dst[i].at[me_s], local_sems.at[i]) for i in range(n)]
        copies = [_remote(src[i].at[2 * px + py], dst[i].at[me_s], send_sems.at[3 * i + k], recv_sems.at[3 * i + k],
                          (px, py, c))
                  for i in range(n) for k, (px, py) in enumerate(chips)]
        for cp in local + copies:
            cp.start()
        for cp in copies:
            cp.wait_recv()
        for cp in copies:
            cp.wait_send()
        for cp in local:
            cp.wait()

    hbm = pl.BlockSpec(memory_space=pltpu.HBM)
    return _pcall(
        body, name="chip_exchange",
        out_shape=[jax.ShapeDtypeStruct(a.shape, a.dtype) for a in arrs],
        in_specs=[hbm] * n, out_specs=[hbm] * n,
        scratch_shapes=[pltpu.SemaphoreType.DMA((3 * n,)), pltpu.SemaphoreType.DMA((3 * n,)),
                        pltpu.SemaphoreType.DMA((n,))],
    )(*arrs)


def _row_tile(rows, cap=4096, mult=16):
    best = None
    for t in range(mult, min(rows, cap) + 1, mult):
        if rows % t == 0:
            best = t
    return rows if best is None else best


def _pair_sum(own, got, out_dtype):
    rows, n = own.shape
    tr = _row_tile(rows)

    def body(a_ref, b_ref, o_ref):
        o_ref[...] = (a_ref[...] + b_ref[...].astype(F32)).astype(out_dtype)

    blk = pl.BlockSpec((tr, n), lambda i: (i, 0))
    return _pcall(body, name="pair_sum", grid=(rows // tr,), in_specs=[blk, blk], out_specs=blk,
                  out_shape=jax.ShapeDtypeStruct((rows, n), out_dtype), compiler_params=_seq())(own, got)


def _chip_sum(parts):
    _, rows, n = parts.shape
    tr = _row_tile(rows, cap=2048)

    def body(p_ref, o_ref):
        acc = p_ref[0].astype(F32) + p_ref[1].astype(F32)
        acc = acc + p_ref[2].astype(F32)
        o_ref[...] = acc + p_ref[3].astype(F32)

    return _pcall(body, name="chip_sum", grid=(rows // tr,),
                  in_specs=[pl.BlockSpec((4, tr, n), lambda i: (0, i, 0))], out_specs=pl.BlockSpec((tr, n), lambda i: (i, 0)),
                  out_shape=jax.ShapeDtypeStruct((rows, n), F32), compiler_params=_seq())(parts)


def _ada_mod(c_all, w_ada, b_ada_cols):
    depth, d, n = w_ada.shape
    nb = c_all.shape[0]

    def body(c_ref, w_ref, b_ref, o_ref):
        cv = c_ref[...]
        ca = _bf(cv * _sigmoid(cv))
        o_ref[0] = _dot(ca, _bf(w_ref[0])) + b_ref[0]

    return _pcall(body, name="ada_mod", grid=(depth,),
                  in_specs=[pl.BlockSpec((nb, d), lambda l: (0, 0)), pl.BlockSpec((1, d, n), lambda l: (l, 0, 0)),
                            pl.BlockSpec((1, 1, n), lambda l: (l, 0, 0))],
                  out_specs=pl.BlockSpec((1, nb, n), lambda l: (l, 0, 0)),
                  out_shape=jax.ShapeDtypeStruct((depth, nb, n), F32), compiler_params=_seq())(c_all, w_ada, b_ada_cols)


def _ada_grad(c_all, dmod_cols, dmod_all):
    nb, d = c_all.shape
    depth, _, n = dmod_cols.shape
    n_all = dmod_all.shape[2]

    def body(c_ref, dm_ref, da_ref, gw_ref, gb_ref):
        cv = c_ref[...]
        ca = _bf(cv * _sigmoid(cv))
        gw_ref[0] = _dot_tn(ca, _bf(dm_ref[0]))
        gb_ref[0] = _colsum(da_ref[0])

    return _pcall(body, name="ada_grad", grid=(depth,),
                  in_specs=[pl.BlockSpec((nb, d), lambda l: (0, 0)), pl.BlockSpec((1, nb, n), lambda l: (l, 0, 0)),
                            pl.BlockSpec((1, nb, n_all), lambda l: (l, 0, 0))],
                  out_specs=[pl.BlockSpec((1, d, n), lambda l: (l, 0, 0)), pl.BlockSpec((1, 1, n_all), lambda l: (l, 0, 0))],
                  out_shape=[jax.ShapeDtypeStruct((depth, d, n), F32), jax.ShapeDtypeStruct((depth, 1, n_all), F32)],
                  compiler_params=_seq())(c_all, dmod_cols, dmod_all)


def _adamw(w, g, m, v):
    shape = w.shape
    cols = shape[-1]
    rows = w.size // cols
    w2, g2, m2, v2 = (t.reshape(rows, cols) for t in (w, g, m, v))
    tr = _row_tile(rows, cap=max(8, (1 << 18) // cols), mult=8)

    def body(w_ref, g_ref, m_ref, v_ref, d_ref, mo_ref, vo_ref):
        gv = g_ref[...]
        mn = ADAM_B1 * m_ref[...] + (1.0 - ADAM_B1) * gv
        vn = ADAM_B2 * v_ref[...] + (1.0 - ADAM_B2) * (gv * gv)
        m_hat = mn / (1.0 - ADAM_B1 ** ADAM_STEP)
        v_hat = vn / (1.0 - ADAM_B2 ** ADAM_STEP)
        d_ref[...] = -ADAM_LR * (m_hat / (jnp.sqrt(v_hat) + ADAM_EPS) + ADAM_WD * w_ref[...])
        mo_ref[...] = mn
        vo_ref[...] = vn

    blk = pl.BlockSpec((tr, cols), lambda i: (i, 0))
    outs = _pcall(body, name="adamw", grid=(rows // tr,), in_specs=[blk] * 4, out_specs=[blk] * 3,
                  out_shape=[jax.ShapeDtypeStruct((rows, cols), F32)] * 3, compiler_params=_seq())(w2, g2, m2, v2)
    return tuple(o.reshape(shape) for o in outs)


WEIGHTS = ["norm_g", "w_ada", "b_ada", "w_in", "rg_conv_w", "rg_conv_b", "rg_w_a", "rg_b_a", "rg_w_x", "rg_b_x",
           "rg_lambda", "ml_conv_w", "ml_conv_b", "ml_w_q", "ml_w_k", "ml_w_v", "ml_w_if", "ml_b_if", "ml_norm_g",
           "w_out", "final_g"]
SHARDED = {"w_in": 1, "w_out": 0, "ml_w_q": 1, "ml_w_k": 1, "ml_w_v": 1, "rg_conv_w": 1, "ml_conv_w": 1, "ml_w_if": 0}
SHARDED_ORDER = ["w_in", "w_out", "ml_w_q", "ml_w_k", "ml_w_v", "rg_conv_w", "ml_conv_w", "ml_w_if"]
GATHERED = ["w_in", "w_out", "ml_w_q", "ml_w_k", "ml_w_v", "ml_w_if"]
REPLICATED = ["rg_w_a", "rg_w_x", "norm_g", "rg_conv_b", "rg_b_a", "rg_b_x", "rg_lambda", "ml_conv_b", "ml_norm_g",
              "ml_b_if"]
LANES = 128


def _to_pieces(g, axis):
    shp = g.shape
    g = g.reshape(shp[:axis] + (4, 2, shp[axis] // 8) + shp[axis + 1:])
    g = jnp.moveaxis(g, (axis, axis + 1), (0, 1))
    return g.reshape(4, 2, -1)


def _from_pieces(p, shard_shape, axis):
    k = p.shape[0]
    rest = shard_shape[:axis] + (shard_shape[axis] // k,) + shard_shape[axis + 1:]
    t = jnp.moveaxis(p.reshape((k,) + rest), 0, axis)
    return t.reshape(shard_shape)


def _pad_rows(flat, mult):
    n = flat.shape[-1]
    pad = (-n) % mult
    if pad:
        flat = jnp.concatenate([flat, jnp.zeros(flat.shape[:-1] + (pad,), flat.dtype)], axis=-1)
    return flat


def kernel(x, c, norm_g, w_ada, b_ada, w_in, rg_conv_w, rg_conv_b, rg_w_a, rg_b_a, rg_w_x, rg_b_x, rg_lambda, ml_conv_w, ml_conv_b, ml_w_q, ml_w_k, ml_w_v, ml_w_if, ml_b_if, ml_norm_g, w_out, final_g, loss_target, m_norm_g, m_w_ada, m_b_ada, m_w_in, m_rg_conv_w, m_rg_conv_b, m_rg_w_a, m_rg_b_a, m_rg_w_x, m_rg_b_x, m_rg_lambda, m_ml_conv_w, m_ml_conv_b, m_ml_w_q, m_ml_w_k, m_ml_w_v, m_ml_w_if, m_ml_b_if, m_ml_norm_g, m_w_out, m_final_g, v_norm_g, v_w_ada, v_b_ada, v_w_in, v_rg_conv_w, v_rg_conv_b, v_rg_w_a, v_rg_b_a, v_rg_w_x, v_rg_b_x, v_rg_lambda, v_ml_conv_w, v_ml_conv_b, v_ml_w_q, v_ml_w_k, v_ml_w_v, v_ml_w_if, v_ml_b_if, v_ml_norm_g, v_w_out, v_final_g):
    given = dict(locals())
    ax, ay, ac = lax.axis_index("x"), lax.axis_index("y"), lax.axis_index("c")
    chip = 2 * ax + ay
    me = 2 * chip + ac
    depth, d = norm_g.shape
    n_ada = w_ada.shape[2]
    pick = lambda a, i, axis=0: lax.dynamic_index_in_dim(a, i, axis, keepdims=False)

    convs = jnp.stack([rg_conv_w, ml_conv_w])
    n_conv = 2 * depth * CONV_WIDTH // 4
    blk = jnp.concatenate([c, convs.reshape(n_conv, d), jnp.zeros((8 - 1 - n_conv, d), F32)], axis=0)
    g0 = _all_gather8(blk, pltpu.VMEM).reshape(8, 8, d)
    c_all = g0[:, 0, :]
    conv_full = g0[0::2, 1:1 + n_conv].reshape(4, 2, depth, CONV_WIDTH, d // 4)
    conv_full = conv_full.transpose(1, 2, 3, 0, 4).reshape(2, depth, CONV_WIDTH, d)

    b_cols = lax.dynamic_slice_in_dim(b_ada, chip * n_ada, n_ada, axis=1)[:, None, :]
    mod_part = _ada_mod(c_all, w_ada, b_cols)
    g1 = _all_gather8(mod_part.transpose(1, 0, 2).reshape(8, depth * n_ada), pltpu.VMEM)
    g1 = g1.reshape(8, 8, depth, n_ada)[0::2]
    mod_me = pick(g1.transpose(1, 2, 0, 3).reshape(8, depth, 4 * n_ada), me)

    parts = []
    for l in range(depth):
        for name in GATHERED:
            w = given[name][l]
            axis = SHARDED[name]
            half = w.shape[axis] // 2
            parts.append(lax.dynamic_slice_in_dim(w, ac * half, half, axis).astype(BF16).reshape(-1))
    flat = _pad_rows(jnp.concatenate(parts), 16 * LANES)
    gw = _all_gather8(flat.reshape(-1, LANES), pltpu.HBM).reshape(8, -1)
    full = [dict() for _ in range(depth)]
    off = 0
    for l in range(depth):
        for name in GATHERED:
            shp = given[name].shape[1:]
            axis = SHARDED[name]
            n = (shp[axis] // 2) * (given[name][l].size // shp[axis])
            full_shape = shp[:axis] + (4 * shp[axis],) + shp[axis + 1:]
            full[l][name] = _from_pieces(gw[:, off:off + n], full_shape, axis)
            off += n

    layers = []
    for l in range(depth):
        wif = full[l]["ml_w_if"]
        layers.append(dict(
            norm_g=norm_g[l][None], shift=mod_me[l, 0:d][None], scale=mod_me[l, d:2 * d][None],
            gate=mod_me[l, 2 * d:3 * d][None], w_in_b=full[l]["w_in"],
            rg_conv_w=conv_full[0, l], rg_conv_b=rg_conv_b[l][None], rg_wa_b=_bf(rg_w_a[l]), rg_ba=rg_b_a[l][None],
            rg_wx_b=_bf(rg_w_x[l]), rg_bx=rg_b_x[l][None], rg_lam=rg_lambda[l][None],
            ml_conv_w=conv_full[1, l], ml_conv_b=ml_conv_b[l][None],
            wq_b=full[l]["ml_w_q"], wk_b=full[l]["ml_w_k"], wv_b=full[l]["ml_w_v"], wif_b=wif, wift_b=wif.T,
            b_if=ml_b_if[l][None], b_ift=ml_b_if[l][:, None], ml_g=ml_norm_g[l][None], w_out_b=full[l]["w_out"]))

    loss, dx, g_final, grads, dmods = _trunk_fwd_bwd(x[0], loss_target[0], final_g[None], layers)

    dm_blk = jnp.concatenate(dmods + [jnp.zeros((8 - depth, 3 * d), F32)], axis=0)
    dm_all = _all_gather8(dm_blk, pltpu.VMEM).reshape(8, 8, 3 * d)[:, :depth].transpose(1, 0, 2)
    dm_cols = lax.dynamic_slice_in_dim(dm_all, chip * n_ada, n_ada, axis=2)
    g_w_ada, g_b_ada = _ada_grad(c_all, dm_cols, dm_all)

    big = jnp.concatenate([_to_pieces(grads[l][name], SHARDED[name]) for l in range(depth) for name in SHARDED_ORDER],
                          axis=-1)
    big = _pad_rows(big, 16 * LANES)
    n_big = big.shape[-1] // LANES
    big = big.transpose(1, 0, 2).reshape(2, 4 * n_big, LANES)
    rep = [grads[l][name].reshape(-1) for l in range(depth) for name in REPLICATED[:-1]]
    rep += [_pad_rows(grads[l]["ml_b_if"].reshape(-1), LANES) for l in range(depth)]
    rep += [g_final.reshape(-1), loss.reshape(-1)]
    rep = _pad_rows(jnp.concatenate(rep), 8 * 8 * LANES)
    n_rep = rep.shape[0] // (8 * LANES)
    rep = rep.reshape(4, 2, n_rep, LANES).transpose(1, 0, 2, 3).reshape(2, 4 * n_rep, LANES)
    got_big, got_rep = _sib_swap([_bf(pick(big, 1 - ac)), pick(rep, 1 - ac)])
    part_big = _pair_sum(pick(big, ac), got_big, BF16).reshape(4, n_big, LANES)
    part_rep = _pair_sum(pick(rep, ac), got_rep, F32).reshape(4, n_rep, LANES)
    met_big, met_rep = _chip_exchange([part_big, part_rep])
    red_big = _chip_sum(met_big)
    red_rep = _chip_sum(met_rep)
    (other_big,) = _sib_swap([red_big])
    shard = jnp.where(ac == 0, jnp.stack([red_big, other_big]), jnp.stack([other_big, red_big])).reshape(2, -1)
    rep_all = _all_gather8(red_rep, pltpu.VMEM).reshape(-1)

    g = dict(w_ada=g_w_ada, b_ada=g_b_ada.reshape(b_ada.shape))
    off = 0
    per_layer = {name: [] for name in SHARDED_ORDER}
    for l in range(depth):
        for name in SHARDED_ORDER:
            shp = given[name].shape[1:]
            n = given[name][l].size // 2
            per_layer[name].append(_from_pieces(shard[:, off:off + n], shp, SHARDED[name]))
            off += n
    for name in SHARDED_ORDER:
        g[name] = jnp.stack(per_layer[name])
    off = 0
    per_layer = {name: [] for name in REPLICATED}
    for l in range(depth):
        for name in REPLICATED[:-1]:
            n = given[name][l].size
            per_layer[name].append(rep_all[off:off + n].reshape(given[name].shape[1:]))
            off += n
    for l in range(depth):
        n = given["ml_b_if"][l].size
        per_layer["ml_b_if"].append(rep_all[off:off + n])
        off += LANES
    for name in REPLICATED:
        g[name] = jnp.stack(per_layer[name])
    g["final_g"] = rep_all[off:off + d]
    loss_all = rep_all[off + d]

    deltas, new_m, new_v = [], [], []
    for name in WEIGHTS:
        dl, mn, vn = _adamw(given[name], g[name], given["m_" + name], given["v_" + name])
        deltas.append(dl)
        new_m.append(mn)
        new_v.append(vn)
    return (loss_all, dx[None], *[g[name] for name in WEIGHTS], *deltas, *new_m, *new_v)
```

```python
import functools

import jax
import jax.numpy as jnp
from jax import lax
from jax.experimental import pallas as pl
from jax.experimental.pallas import tpu as pltpu

F32 = jnp.float32
BF16 = jnp.bfloat16

EPS = 1e-6
RG_C = 8.0
CONV_WIDTH = 4
ML_CHUNK = 128
HALO = 8
ADAM_LR = 0.001
ADAM_B1 = 0.9
ADAM_B2 = 0.999
ADAM_EPS = 1e-08
ADAM_WD = 0.01
ADAM_STEP = 10
MESH = pl.DeviceIdType.MESH


def _pcall(body, **kw):
    return pl.pallas_call(body, **kw)


def _seq(n=1):
    return pltpu.CompilerParams(dimension_semantics=("arbitrary",) * n)


def _dot(a, b):
    return jnp.dot(a, b, preferred_element_type=F32)


def _dot_nt(a, b):
    return lax.dot_general(a, b, (((1,), (1,)), ((), ())), preferred_element_type=F32)


def _dot_tn(a, b):
    return lax.dot_general(a, b, (((0,), (0,)), ((), ())), preferred_element_type=F32)


def _bf(x):
    return x.astype(BF16)


def _sigmoid(x):
    return 1.0 / (1.0 + jnp.exp(-x))


def _log1p(z):
    u = 1.0 + z
    return jnp.where(u == 1.0, z, jnp.log(u) * (z / jnp.where(u == 1.0, 1.0, u - 1.0)))


def _softplus(x):
    return jnp.maximum(x, 0.0) + _log1p(jnp.exp(-jnp.abs(x)))


def _log_sigmoid(x):
    return -_softplus(-x)


def _expm1(x):
    small = x * (1.0 + x * (0.5 + x * (1.0 / 6.0 + x * (1.0 / 24.0 + x * (1.0 / 120.0)))))
    return jnp.where(jnp.abs(x) < 0.03, small, jnp.exp(x) - 1.0)


def _dsilu(x, s):
    return s * (1.0 + x * (1.0 - s))


def _rowsum(x):
    return jnp.sum(x, axis=1, keepdims=True)


def _colsum(x):
    return jnp.sum(x, axis=0, keepdims=True)


def _shift_down(win, s):
    return win if s == 0 else pltpu.roll(win, s, 0)


def _shift_up(win, s):
    return win if s == 0 else pltpu.roll(win, win.shape[0] - s, 0)


def _conv_fwd(win, w_ref, b_ref):
    acc = b_ref[...] + w_ref[CONV_WIDTH - 1:CONV_WIDTH, :] * win[HALO:]
    for k in range(CONV_WIDTH - 1):
        acc = acc + w_ref[k:k + 1, :] * _shift_down(win, CONV_WIDTH - 1 - k)[HALO:]
    return acc


def _split3(x):
    hi = _bf(x)
    r1 = x - hi.astype(F32)
    mid = _bf(r1)
    lo = _bf(r1 - mid.astype(F32))
    return hi, mid, lo


def _tri_dot_left(tri, x):
    hi, mid, lo = _split3(x)
    return _dot(tri, hi) + _dot(tri, mid) + _dot(tri, lo)


def _tri_dot_right(x, tri):
    hi, mid, lo = _split3(x)
    return _dot(hi, tri) + _dot(mid, tri) + _dot(lo, tri)


def _tile(n, want):
    t = min(n, want)
    assert n % t == 0
    return t


def _ln_inproj(x, g, scale, shift, w4):
    s_len, d = x.shape
    nj, _, nsh = w4.shape
    tm = _tile(s_len, 512)

    def body(x_ref, g_ref, sc_ref, sh_ref, w_ref, h_ref, u_ref, hs):
        @pl.when(pl.program_id(1) == 0)
        def _():
            xv = x_ref[...]
            r = lax.rsqrt(jnp.mean(xv * xv, axis=-1, keepdims=True) + EPS)
            hv = (xv * r * g_ref[...]) * (1.0 + sc_ref[...]) + sh_ref[...]
            hs[...] = _bf(hv)
            h_ref[...] = hs[...]

        u_ref[...] = _dot(hs[...], w_ref[0])

    vec = pl.BlockSpec((1, d), lambda i, j: (0, 0))
    return _pcall(
        body, name="ln_inproj", grid=(s_len // tm, nj),
        in_specs=[pl.BlockSpec((tm, d), lambda i, j: (i, 0)), vec, vec, vec,
                  pl.BlockSpec((1, d, nsh), lambda i, j: (j, 0, 0))],
        out_specs=[pl.BlockSpec((tm, d), lambda i, j: (i, 0)), pl.BlockSpec((tm, nsh), lambda i, j: (i, j))],
        out_shape=[jax.ShapeDtypeStruct((s_len, d), BF16), jax.ShapeDtypeStruct((s_len, nj * nsh), F32)],
        scratch_shapes=[pltpu.VMEM((tm, d), BF16)],
        compiler_params=_seq(2),
    )(x, g, scale, shift, w4)


def _rg_gates(xc, wa_ref, ba_ref, wx_ref, bx_ref, lam_ref):
    heads, hd, _ = wa_ref.shape
    xb = _bf(xc)
    ga = jnp.concatenate([_dot(xb[:, h * hd:(h + 1) * hd], wa_ref[h]) for h in range(heads)], axis=1) + ba_ref[...]
    gx = jnp.concatenate([_dot(xb[:, h * hd:(h + 1) * hd], wx_ref[h]) for h in range(heads)], axis=1) + bx_ref[...]
    r = _sigmoid(ga)
    ig = _sigmoid(gx)
    sp = _softplus(-lam_ref[...])
    log_a = (-RG_C) * r * sp
    a = jnp.exp(log_a)
    mult = jnp.sqrt(-_expm1(2.0 * log_a))
    return r, ig, sp, log_a, a, mult


def _scan_groups(a, u, reverse):
    n = a.shape[0]
    row = lax.broadcasted_iota(jnp.int32, a.shape, 0) & 7
    for k in (1, 2, 4):
        if reverse:
            a_sh, u_sh = _shift_up(a, k), _shift_up(u, k)
            ok = row < 8 - k
        else:
            a_sh, u_sh = _shift_down(a, k), _shift_down(u, k)
            ok = row >= k
        u = jnp.where(ok, a * u_sh + u, u)
        a = jnp.where(ok, a * a_sh, a)
    del n
    return a, u


def _rg_fwd(u, conv_w, conv_b, wa_b, ba, wx_b, bx, lam):
    s_len = u.shape[0]
    d = conv_w.shape[1]
    tm = _tile(s_len, 256)
    per = tm // HALO

    def body(x_ref, xp_ref, z_ref, cw_ref, cb_ref, wa_ref, ba_ref, wx_ref, bx_ref, lam_ref,
             hh_ref, y_ref, carry):
        i = pl.program_id(0)

        @pl.when(i == 0)
        def _():
            carry[...] = jnp.zeros_like(carry)

        prev = jnp.where(i == 0, 0.0, xp_ref[...])
        xc = _conv_fwd(jnp.concatenate([prev, x_ref[...]], axis=0), cw_ref, cb_ref)
        _, ig, _, _, a, mult = _rg_gates(xc, wa_ref, ba_ref, wx_ref, bx_ref, lam_ref)
        ca, cu = _scan_groups(a, mult * (ig * xc), reverse=False)
        c = carry[0:1, :]
        for j in range(per):
            blk = ca[j * 8:(j + 1) * 8] * c + cu[j * 8:(j + 1) * 8]
            hh_ref[j * 8:(j + 1) * 8, :] = blk
            c = blk[7:8]
        carry[0:1, :] = c
        z = z_ref[...]
        y_ref[0] = _bf(hh_ref[...] * (z * _sigmoid(z)))

    vec = pl.BlockSpec((1, d), lambda i: (0, 0))
    whole3 = lambda a: pl.BlockSpec(a.shape, lambda i: (0, 0, 0))
    return _pcall(
        body, name="rg_fwd", grid=(s_len // tm,),
        in_specs=[pl.BlockSpec((tm, d), lambda i: (i, 0)),
                  pl.BlockSpec((HALO, d), lambda i: (jnp.maximum(i * per - 1, 0), 0)),
                  pl.BlockSpec((tm, d), lambda i: (i, 1)),
                  pl.BlockSpec((CONV_WIDTH, d), lambda i: (0, 0)), vec,
                  whole3(wa_b), vec, whole3(wx_b), vec, vec],
        out_specs=[pl.BlockSpec((tm, d), lambda i: (i, 0)), pl.BlockSpec((1, tm, d), lambda i: (0, i, 0))],
        out_shape=[jax.ShapeDtypeStruct((s_len, d), F32), jax.ShapeDtypeStruct((2, s_len, d), BF16)],
        scratch_shapes=[pltpu.VMEM((8, d), F32)],
        compiler_params=_seq(),
    )(u, u, u, conv_w, conv_b, wa_b, ba, wx_b, bx, lam)


def _ml_pre(u, conv_w, conv_b, wqkv_b, wif_b, wift_b, b_if, b_ift):
    s_len = u.shape[0]
    d = conv_w.shape[1]
    _, heads, hd, _ = wqkv_b.shape
    ng = 2 * heads
    tm = _tile(s_len, 256)
    per = tm // HALO

    def body(x_ref, xp_ref, cw_ref, cb_ref, w_ref, wif_ref, wift_ref, bif_ref, bift_ref,
             qkv_ref, gt_ref, gtt_ref):
        i = pl.program_id(0)
        prev = jnp.where(i == 0, 0.0, xp_ref[...])
        xm = x_ref[...]
        pre = _conv_fwd(jnp.concatenate([prev, xm], axis=0), cw_ref, cb_ref)
        xcb = _bf(pre * _sigmoid(pre))
        xmb = _bf(xm)
        for h in range(heads):
            hs = slice(h * hd, (h + 1) * hd)
            qkv_ref[0, :, hs] = _bf(_dot(xcb[:, hs], w_ref[0, h]))
            qkv_ref[1, :, hs] = _bf(_dot(xcb[:, hs], w_ref[1, h]))
            qkv_ref[2, :, hs] = _bf(_dot(xmb[:, hs], w_ref[2, h]))
        qb, kb, vb = qkv_ref[0], qkv_ref[1], qkv_ref[2]
        gt_ref[...] = (_dot(qb, wif_ref[0:d, :]) + _dot(kb, wif_ref[d:2 * d, :]) + _dot(vb, wif_ref[2 * d:3 * d, :])
                       + bif_ref[...])
        gtt_ref[...] = (_dot_nt(wift_ref[:, 0:d], qb) + _dot_nt(wift_ref[:, d:2 * d], kb)
                        + _dot_nt(wift_ref[:, 2 * d:3 * d], vb) + bift_ref[...])

    vec = pl.BlockSpec((1, d), lambda i: (0, 0))
    whole2 = lambda a: pl.BlockSpec(a.shape, lambda i: (0, 0))
    return _pcall(
        body, name="ml_pre", grid=(s_len // tm,),
        in_specs=[pl.BlockSpec((tm, d), lambda i: (i, 2)),
                  pl.BlockSpec((HALO, d), lambda i: (jnp.maximum(i * per - 1, 0), 2)),
                  pl.BlockSpec((CONV_WIDTH, d), lambda i: (0, 0)), vec,
                  pl.BlockSpec(wqkv_b.shape, lambda i: (0, 0, 0, 0)), whole2(wif_b), whole2(wift_b), whole2(b_if),
                  whole2(b_ift)],
        out_specs=[pl.BlockSpec((3, tm, d), lambda i: (0, i, 0)), pl.BlockSpec((tm, ng), lambda i: (i, 0)),
                   pl.BlockSpec((ng, tm), lambda i: (0, i))],
        out_shape=[jax.ShapeDtypeStruct((3, s_len, d), BF16), jax.ShapeDtypeStruct((s_len, ng), F32),
                   jax.ShapeDtypeStruct((ng, s_len), F32)],
        compiler_params=_seq(),
    )(u, u, conv_w, conv_b, wqkv_b, wif_b, wift_b, b_if, b_ift)


def _chunk_gates(gt, gtt, h, heads, tril, triu):
    li_c = gt[:, h:h + 1]
    li_r = gtt[h:h + 1, :]
    gf_c = gt[:, heads + h:heads + h + 1]
    lf_c = _log_sigmoid(gf_c)
    lf_r = _log_sigmoid(gtt[heads + h:heads + h + 1, :])
    b_c = _tri_dot_left(tril, lf_c)
    b_r = _tri_dot_right(lf_r, triu)
    return li_c, li_r, gf_c, b_c, b_r


def _chunk_weights(li_c, li_r, b_c, b_r, m_prev, causal):
    lc = b_c.shape[0]
    b_last = b_c[lc - 1:lc, :]
    dmat = jnp.where(causal, b_c - b_r + li_r, -jnp.inf)
    m_inter = b_c + m_prev
    m_t = jnp.maximum(m_inter, jnp.max(dmat, axis=1, keepdims=True))
    w_intra = jnp.exp(dmat - m_t)
    w_inter = jnp.exp(m_inter - m_t)
    g_c = b_last - b_c + li_c
    m_new = jnp.maximum(b_last + m_prev, jnp.max(g_c, axis=0, keepdims=True))
    w_state = jnp.exp(g_c - m_new)
    decay = jnp.exp(b_last + m_prev - m_new)
    return m_t, w_intra, w_inter, m_new, w_state, decay


def _tri_masks(lc):
    r = lax.broadcasted_iota(jnp.int32, (lc, lc), 0)
    c = lax.broadcasted_iota(jnp.int32, (lc, lc), 1)
    causal = r >= c
    return causal, causal.astype(BF16), (r <= c).astype(BF16)


def _mlstm_fwd(qkv, gt, gtt, u, ml_g, ycat):
    _, s_len, d = qkv.shape
    ng = gt.shape[1]
    heads = ng // 2
    hd = d // heads
    lc = ML_CHUNK
    nc = s_len // lc
    kscale = hd ** -0.5

    def body(qkv_ref, gt_ref, gtt_ref, o_ref, z_ref, g_ref, _, cell_ref, y_ref, cst_ref, nst_ref, mst_ref, cs, ns, ms):
        @pl.when(pl.program_id(0) == 0)
        def _():
            cs[...] = jnp.zeros_like(cs)
            ns[...] = jnp.zeros_like(ns)
            ms[...] = jnp.zeros_like(ms)

        causal, tril, triu = _tri_masks(lc)
        gtv, gttv = gt_ref[...], gtt_ref[...]
        for h in range(heads):
            hs = slice(h * hd, (h + 1) * hd)
            li_c, li_r, _, b_c, b_r = _chunk_gates(gtv, gttv, h, heads, tril, triu)
            m_prev = ms[h][:, 0:1]
            m_t, w_intra, w_inter, m_new, w_state, decay = _chunk_weights(li_c, li_r, b_c, b_r, m_prev, causal)
            qb = qkv_ref[0, :, hs]
            ks = qkv_ref[1, :, hs].astype(F32) * kscale
            kb = _bf(ks)
            vb = qkv_ref[2, :, hs]
            c_old = cs[h]
            n_old = ns[h]
            cst_ref[0, h] = _bf(c_old)
            nst_ref[0, h] = n_old
            mst_ref[0, h] = ms[h]
            s = _dot_nt(qb, kb) * w_intra
            num = _dot(_bf(s), vb) + w_inter * _dot(qb, _bf(c_old))
            den = _rowsum(s) + w_inter * _rowsum(qb.astype(F32) * n_old)
            cell = num / jnp.maximum(jnp.abs(den), jnp.exp(-m_t))
            kw = ks * w_state
            cs[h] = decay * c_old + _dot_tn(_bf(kw), vb)
            ns[h] = decay * n_old + _colsum(kw)
            ms[h] = jnp.broadcast_to(m_new, ms[h].shape)
            cell_ref[:, hs] = cell
            hm = _sigmoid(o_ref[:, hs]) * cell
            hn = hm * lax.rsqrt(jnp.mean(hm * hm, axis=-1, keepdims=True) + EPS)
            z = z_ref[:, hs]
            y_ref[0, :, hs] = _bf((hn * g_ref[:, hs]) * (z * _sigmoid(z)))

    row = pl.BlockSpec((lc, d), lambda c: (c, 0))
    return _pcall(
        body, name="mlstm_fwd", grid=(nc,),
        in_specs=[pl.BlockSpec((3, lc, d), lambda c: (0, c, 0)), pl.BlockSpec((lc, ng), lambda c: (c, 0)),
                  pl.BlockSpec((ng, lc), lambda c: (0, c)),
                  pl.BlockSpec((lc, d), lambda c: (c, 3)), pl.BlockSpec((lc, d), lambda c: (c, 4)),
                  pl.BlockSpec((1, d), lambda c: (0, 0)), pl.BlockSpec(memory_space=pl.ANY)],
        out_specs=[row, pl.BlockSpec((1, lc, d), lambda c: (1, c, 0)),
                   pl.BlockSpec((1, heads, hd, hd), lambda c: (c, 0, 0, 0)),
                   pl.BlockSpec((1, heads, 1, hd), lambda c: (c, 0, 0, 0)),
                   pl.BlockSpec((1, heads, 1, 128), lambda c: (c, 0, 0, 0))],
        out_shape=[jax.ShapeDtypeStruct((s_len, d), F32), jax.ShapeDtypeStruct(ycat.shape, BF16),
                   jax.ShapeDtypeStruct((nc, heads, hd, hd), BF16),
                   jax.ShapeDtypeStruct((nc, heads, 1, hd), F32),
                   jax.ShapeDtypeStruct((nc, heads, 1, 128), F32)],
        scratch_shapes=[pltpu.VMEM((heads, hd, hd), F32), pltpu.VMEM((heads, 1, hd), F32),
                        pltpu.VMEM((heads, 1, 128), F32)],
        input_output_aliases={6: 1},
        compiler_params=_seq(),
    )(qkv, gt, gtt, u, u, ml_g, ycat)


def _out_proj(ycat, w_out_b, x, gate):
    s_len, d = x.shape
    tm = _tile(s_len, 512)

    def body(a_ref, w_ref, x_ref, g_ref, y_ref, xn_ref):
        y = _dot(a_ref[0], w_ref[0:d, :]) + _dot(a_ref[1], w_ref[d:2 * d, :])
        y_ref[...] = y
        xn_ref[...] = x_ref[...] + g_ref[...] * y

    row = pl.BlockSpec((tm, d), lambda i: (i, 0))
    return _pcall(
        body, name="out_proj", grid=(s_len // tm,),
        in_specs=[pl.BlockSpec((2, tm, d), lambda i: (0, i, 0)), pl.BlockSpec((2 * d, d), lambda i: (0, 0)), row,
                  pl.BlockSpec((1, d), lambda i: (0, 0))],
        out_specs=[row, row],
        out_shape=[jax.ShapeDtypeStruct((s_len, d), F32)] * 2,
        compiler_params=_seq(),
    )(ycat, w_out_b, x, gate)


def _final_loss(x, g, target):
    s_len, d = x.shape
    tm = _tile(s_len, 256)

    def body(x_ref, g_ref, t_ref, dx_ref, dg_ref, loss_ref):
        @pl.when(pl.program_id(0) == 0)
        def _():
            dg_ref[...] = jnp.zeros_like(dg_ref)
            loss_ref[...] = jnp.zeros_like(loss_ref)

        xv = x_ref[...]
        r = lax.rsqrt(jnp.mean(xv * xv, axis=-1, keepdims=True) + EPS)
        xn = xv * r
        err = xn * g_ref[...] - t_ref[...]
        loss_ref[...] += 0.5 * jnp.sum(jnp.mean(err * err, axis=-1, keepdims=True))
        dout = err * (1.0 / d)
        dg_ref[...] += _colsum(dout * xn)
        dxn = dout * g_ref[...]
        dx_ref[...] = r * (dxn - xn * jnp.mean(dxn * xn, axis=-1, keepdims=True))

    row = pl.BlockSpec((tm, d), lambda i: (i, 0))
    vec = pl.BlockSpec((1, d), lambda i: (0, 0))
    return _pcall(
        body, name="final_loss", grid=(s_len // tm,),
        in_specs=[row, vec, row],
        out_specs=[row, vec, pl.BlockSpec((1, 128), lambda i: (0, 0))],
        out_shape=[jax.ShapeDtypeStruct((s_len, d), F32), jax.ShapeDtypeStruct((1, d), F32),
                   jax.ShapeDtypeStruct((1, 128), F32)],
        compiler_params=_seq(),
    )(x, g, target)


def _out_bwd(dxn, y, gate, w_out_b):
    s_len, d = dxn.shape
    tm = _tile(s_len, 512)

    def body(dx_ref, y_ref, g_ref, w_ref, dg_ref, dy_ref, dc_ref):
        @pl.when(pl.program_id(0) == 0)
        def _():
            dg_ref[...] = jnp.zeros_like(dg_ref)

        dx = dx_ref[...]
        dg_ref[...] += _colsum(dx * y_ref[...])
        dy = _bf(g_ref[...] * dx)
        dy_ref[...] = dy
        dc_ref[0] = _dot_nt(dy, w_ref[0:d, :])
        dc_ref[1] = _dot_nt(dy, w_ref[d:2 * d, :])

    row = pl.BlockSpec((tm, d), lambda i: (i, 0))
    vec = pl.BlockSpec((1, d), lambda i: (0, 0))
    return _pcall(
        body, name="out_bwd", grid=(s_len // tm,),
        in_specs=[row, row, vec, pl.BlockSpec((2 * d, d), lambda i: (0, 0))],
        out_specs=[vec, row, pl.BlockSpec((2, tm, d), lambda i: (0, i, 0))],
        out_shape=[jax.ShapeDtypeStruct((1, d), F32), jax.ShapeDtypeStruct((s_len, d), BF16),
                   jax.ShapeDtypeStruct((2, s_len, d), F32)],
        compiler_params=_seq(),
    )(dxn, y, gate, w_out_b)


def _grad_matmul(a3, b3, nblk, a_idx, b_idx, out_shape, out_block, out_idx, layer, stack):
    _, s_len, m = a3.shape
    n = b3.shape[2]
    tk = _tile(s_len, 512)
    first = isinstance(stack, int)

    def body(a_ref, b_ref, *rest):
        o_ref = rest[-1]

        @pl.when(pl.program_id(1) == 0)
        def _():
            o_ref[...] = jnp.zeros_like(o_ref)

        o_ref[...] += _dot_tn(a_ref[0], b_ref[0])

    in_specs = [pl.BlockSpec((1, tk, m), lambda p, t: (a_idx(p), t, 0)),
                pl.BlockSpec((1, tk, n), lambda p, t: (b_idx(p), t, 0))]
    return _pcall(
        body, name="grad_matmul", grid=(nblk, s_len // tk),
        in_specs=in_specs if first else in_specs + [pl.BlockSpec(memory_space=pl.ANY)],
        out_specs=pl.BlockSpec((None,) + out_block, lambda p, t: (layer,) + out_idx(p)),
        out_shape=jax.ShapeDtypeStruct(((stack,) if first else stack.shape[:1]) + out_shape, F32),
        input_output_aliases={} if first else {2: 0},
        compiler_params=_seq(2),
    )(*((a3, b3) if first else (a3, b3, stack)))


DU_PLANE = (2, 3, 4, 0, 1)


def _mlstm_bwd(qkv, gt, gtt, cst, nst, mst, cell, u, ml_g, d_ycat, wif_b):
    _, s_len, d = qkv.shape
    ng = gt.shape[1]
    heads = ng // 2
    hd = d // heads
    lc = ML_CHUNK
    nc = s_len // lc
    kscale = hd ** -0.5

    def body(qkv_ref, gt_ref, gtt_ref, cst_ref, nst_ref, mst_ref, cell_ref, o_ref, z_ref, g_ref, dy_ref,
             wif_ref, dqkv_ref, dgt_ref, dbif_ref, du_ref, dg_ref, dcs, dns, dqs, dks, dvs):
        @pl.when(pl.program_id(0) == 0)
        def _():
            dbif_ref[...] = jnp.zeros_like(dbif_ref)
            dcs[...] = jnp.zeros_like(dcs)
            dns[...] = jnp.zeros_like(dns)
            dg_ref[...] = jnp.zeros_like(dg_ref)

        causal, tril, triu = _tri_masks(lc)
        tril_strict = (tril.astype(F32) - (tril * triu).astype(F32)).astype(BF16)
        gtv, gttv = gt_ref[...], gtt_ref[...]
        lane = lax.broadcasted_iota(jnp.int32, (lc, ng), 1)
        dgt = jnp.zeros((lc, ng), F32)
        for h in range(heads):
            hs = slice(h * hd, (h + 1) * hd)
            li_c, li_r, gf_c, b_c, b_r = _chunk_gates(gtv, gttv, h, heads, tril, triu)
            m_prev = mst_ref[0, h][:, 0:1]
            m_t, w_intra, w_inter, _, w_state, decay = _chunk_weights(li_c, li_r, b_c, b_r, m_prev, causal)
            qb = qkv_ref[0, :, hs]
            qf = qb.astype(F32)
            ks = qkv_ref[1, :, hs].astype(F32) * kscale
            kb = _bf(ks)
            vb = qkv_ref[2, :, hs]
            c_b = cst_ref[0, h]
            n_old = nst_ref[0, h]
            s = _dot_nt(qb, kb) * w_intra
            den = _rowsum(s) + w_inter * _rowsum(qf * n_old)
            floor = jnp.exp(-m_t)
            dstab = jnp.maximum(jnp.abs(den), floor)
            cell = cell_ref[:, hs]
            o = o_ref[:, hs]
            so = _sigmoid(o)
            hm = so * cell
            rinv = lax.rsqrt(jnp.mean(hm * hm, axis=-1, keepdims=True) + EPS)
            hn = hm * rinv
            z = z_ref[:, hs]
            sgz = _sigmoid(z)
            sz = z * sgz
            gh = g_ref[:, hs]
            dy = dy_ref[0, :, hs]
            du_ref[1, :, hs] = _bf(dy * (hn * gh) * _dsilu(z, sgz))
            dg_ref[:, hs] += _colsum(dy * hn * sz)
            dhn = dy * gh * sz
            dhm = rinv * (dhn - hn * jnp.mean(dhn * hn, axis=-1, keepdims=True))
            du_ref[0, :, hs] = _bf(dhm * cell * so * (1.0 - so))
            dcell = dhm * so
            dnum = dcell / dstab
            dnb = _bf(dnum)
            dden = -_rowsum(dcell * cell) / dstab * jnp.where(jnp.abs(den) > floor, jnp.where(den > 0.0, 1.0, -1.0), 0.0)
            dst = _dot_nt(dnb, vb) + dden
            dsdb = _bf(dst * w_intra)
            dc_out = dcs[h]
            dn_out = dns[h]
            dcb = _bf(dc_out)
            dq_inter = w_inter * (_dot_nt(dnb, c_b) + dden * n_old)
            dk_inter = w_state * (_dot_nt(vb, dcb) + dn_out)
            dq = _dot(dsdb, kb) + dq_inter
            dk = _dot_tn(dsdb, qb) + dk_inter
            dv = _dot_tn(_bf(s), dnb) + _dot(_bf(ks * w_state), dcb)
            wq = w_inter * qf
            dcs[h] = decay * dc_out + _dot_tn(_bf(wq), dnb)
            dns[h] = decay * dn_out + _colsum(wq * dden)
            pmat = dst * s
            p_rows = _rowsum(pmat)
            p_cols = _rowsum(pmat.T)
            q_in = _rowsum(qf * dq_inter)
            k_in = _rowsum(ks * dk_inter)
            across = decay * (jnp.sum(dc_out * c_b.astype(F32), keepdims=True) + jnp.sum(dn_out * n_old, keepdims=True))
            dli = p_cols + k_in
            dlf = _tri_dot_left(triu, p_rows - p_cols + q_in) + _tri_dot_left(tril_strict, k_in) + across
            dgf = dlf * _sigmoid(-gf_c)
            dgt = dgt + jnp.where(lane == h, dli, 0.0) + jnp.where(lane == heads + h, dgf, 0.0)
            dqs[:, hs] = dq
            dks[:, hs] = dk * kscale
            dvs[:, hs] = dv
        dgt_ref[...] = dgt
        dbif_ref[...] += _colsum(dgt)
        dgb = _bf(dgt)
        dqkv_ref[0] = _bf(dqs[...] + _dot_nt(dgb, wif_ref[0:d, :]))
        dqkv_ref[1] = _bf(dks[...] + _dot_nt(dgb, wif_ref[d:2 * d, :]))
        dqkv_ref[2] = _bf(dvs[...] + _dot_nt(dgb, wif_ref[2 * d:3 * d, :]))

    rev = lambda c: nc - 1 - c
    row = pl.BlockSpec((lc, d), lambda c: (rev(c), 0))
    return _pcall(
        body, name="mlstm_bwd", grid=(nc,),
        in_specs=[pl.BlockSpec((3, lc, d), lambda c: (0, rev(c), 0)), pl.BlockSpec((lc, ng), lambda c: (rev(c), 0)),
                  pl.BlockSpec((ng, lc), lambda c: (0, rev(c))),
                  pl.BlockSpec((1, heads, hd, hd), lambda c: (rev(c), 0, 0, 0)),
                  pl.BlockSpec((1, heads, 1, hd), lambda c: (rev(c), 0, 0, 0)),
                  pl.BlockSpec((1, heads, 1, 128), lambda c: (rev(c), 0, 0, 0)),
                  row, pl.BlockSpec((lc, d), lambda c: (rev(c), 3)), pl.BlockSpec((lc, d), lambda c: (rev(c), 4)),
                  pl.BlockSpec((1, d), lambda c: (0, 0)), pl.BlockSpec((1, lc, d), lambda c: (1, rev(c), 0)),
                  pl.BlockSpec((3 * d, ng), lambda c: (0, 0))],
        out_specs=[pl.BlockSpec((3, lc, d), lambda c: (0, rev(c), 0)), pl.BlockSpec((lc, ng), lambda c: (rev(c), 0)),
                   pl.BlockSpec((1, ng), lambda c: (0, 0)), pl.BlockSpec((2, lc, d), lambda c: (0, rev(c), 0)),
                   pl.BlockSpec((1, d), lambda c: (0, 0))],
        out_shape=[jax.ShapeDtypeStruct((3, s_len, d), BF16), jax.ShapeDtypeStruct((s_len, ng), F32),
                   jax.ShapeDtypeStruct((1, ng), F32), jax.ShapeDtypeStruct((5, s_len, d), BF16),
                   jax.ShapeDtypeStruct((1, d), F32)],
        scratch_shapes=[pltpu.VMEM((heads, hd, hd), F32), pltpu.VMEM((heads, 1, hd), F32)]
        + [pltpu.VMEM((lc, d), F32)] * 3,
        compiler_params=_seq(),
    )(qkv, gt, gtt, cst, nst, mst, cell, u, u, ml_g, d_ycat, wif_b)


def _ml_pre_bwd(dqkv, u, conv_w, conv_b, wqkv_b):
    s_len = u.shape[0]
    d = conv_w.shape[1]
    _, heads, hd, _ = wqkv_b.shape
    tm = _tile(s_len, 256)
    per = tm // HALO

    def body(dqkv_ref, x_ref, xp_ref, cw_ref, cb_ref, w_ref, dpre_ref, dx_ref, gw_ref):
        i = pl.program_id(0)

        @pl.when(i == 0)
        def _():
            gw_ref[...] = jnp.zeros_like(gw_ref)

        prev = jnp.where(i == 0, 0.0, xp_ref[...])
        xm = x_ref[...]
        pre = _conv_fwd(jnp.concatenate([prev, xm], axis=0), cw_ref, cb_ref)
        sg = _sigmoid(pre)
        xcb = _bf(pre * sg)
        xmb = _bf(xm)
        for h in range(heads):
            hs = slice(h * hd, (h + 1) * hd)
            dqh, dkh, dvh = dqkv_ref[0, :, hs], dqkv_ref[1, :, hs], dqkv_ref[2, :, hs]
            dxc = _dot_nt(dqh, w_ref[0, h]) + _dot_nt(dkh, w_ref[1, h])
            dpre_ref[:, hs] = dxc * _dsilu(pre[:, hs], sg[:, hs])
            dx_ref[:, hs] = _dot_nt(dvh, w_ref[2, h])
            gw_ref[0, h] += _dot_tn(xcb[:, hs], dqh)
            gw_ref[1, h] += _dot_tn(xcb[:, hs], dkh)
            gw_ref[2, h] += _dot_tn(xmb[:, hs], dvh)

    row = pl.BlockSpec((tm, d), lambda i: (i, 0))
    vec = pl.BlockSpec((1, d), lambda i: (0, 0))
    whole4 = pl.BlockSpec(wqkv_b.shape, lambda i: (0, 0, 0, 0))
    return _pcall(
        body, name="ml_pre_bwd", grid=(s_len // tm,),
        in_specs=[pl.BlockSpec((3, tm, d), lambda i: (0, i, 0)), pl.BlockSpec((tm, d), lambda i: (i, 2)),
                  pl.BlockSpec((HALO, d), lambda i: (jnp.maximum(i * per - 1, 0), 2)),
                  pl.BlockSpec((CONV_WIDTH, d), lambda i: (0, 0)), vec, whole4],
        out_specs=[row, row, whole4],
        out_shape=[jax.ShapeDtypeStruct((s_len, d), F32)] * 2 + [jax.ShapeDtypeStruct(wqkv_b.shape, F32)],
        compiler_params=_seq(),
    )(dqkv, u, u, conv_w, conv_b, wqkv_b)


def _conv_bwd(dpre, u, col, conv_w, du, plane, direct=None):
    s_len, d = dpre.shape
    tm = _tile(s_len, 256)
    per = tm // HALO
    nt = s_len // tm

    def body(*refs):
        if direct is None:
            dp_ref, dn_ref, x_ref, xp_ref, cw_ref, _, dx_ref, gw_ref, gb_ref = refs
        else:
            dp_ref, dn_ref, x_ref, xp_ref, cw_ref, _, dir_ref, dx_ref, gw_ref, gb_ref = refs
        i = pl.program_id(0)

        @pl.when(i == 0)
        def _():
            gw_ref[...] = jnp.zeros_like(gw_ref)
            gb_ref[...] = jnp.zeros_like(gb_ref)

        dp = dp_ref[...]
        nxt = jnp.where(i == nt - 1, 0.0, dn_ref[...])
        dwin = jnp.concatenate([dp, nxt], axis=0)
        prev = jnp.where(i == 0, 0.0, xp_ref[...])
        xwin = jnp.concatenate([prev, x_ref[...]], axis=0)
        acc = cw_ref[CONV_WIDTH - 1:CONV_WIDTH, :] * dp
        if direct is not None:
            acc = acc + dir_ref[...]
        gw_ref[CONV_WIDTH - 1:CONV_WIDTH, :] += _colsum(dp * xwin[HALO:])
        for k in range(CONV_WIDTH - 1):
            sft = CONV_WIDTH - 1 - k
            acc = acc + cw_ref[k:k + 1, :] * _shift_up(dwin, sft)[0:tm]
            gw_ref[k:k + 1, :] += _colsum(dp * _shift_down(xwin, sft)[HALO:])
        gb_ref[...] += _colsum(dp)
        dx_ref[0] = _bf(acc)

    row = pl.BlockSpec((tm, d), lambda i: (i, 0))
    in_specs = [row, pl.BlockSpec((HALO, d), lambda i: (jnp.minimum((i + 1) * per, s_len // HALO - 1), 0)),
                pl.BlockSpec((tm, d), lambda i: (i, col)),
                pl.BlockSpec((HALO, d), lambda i: (jnp.maximum(i * per - 1, 0), col)),
                pl.BlockSpec((CONV_WIDTH, d), lambda i: (0, 0)), pl.BlockSpec(memory_space=pl.ANY)]
    args = [dpre, dpre, u, u, conv_w, du]
    if direct is not None:
        in_specs.append(row)
        args.append(direct)
    return _pcall(
        body, name="conv_bwd", grid=(nt,),
        in_specs=in_specs,
        out_specs=[pl.BlockSpec((1, tm, d), lambda i: (plane, i, 0)), pl.BlockSpec((CONV_WIDTH, d), lambda i: (0, 0)),
                   pl.BlockSpec((1, d), lambda i: (0, 0))],
        out_shape=[jax.ShapeDtypeStruct(du.shape, BF16), jax.ShapeDtypeStruct((CONV_WIDTH, d), F32),
                   jax.ShapeDtypeStruct((1, d), F32)],
        input_output_aliases={5: 0},
        compiler_params=_seq(),
    )(*args)


def _rg_bwd(d_ycat, u, hh, conv_w, conv_b, wa_b, ba, wx_b, bx, lam, du):
    s_len = u.shape[0]
    d = conv_w.shape[1]
    heads, hd, _ = wa_b.shape
    tm = _tile(s_len, 256)
    per = tm // HALO
    nt = s_len // tm

    def body(dy_ref, x_ref, xp_ref, z_ref, hh_ref, hp_ref, cw_ref, cb_ref, wa_ref, ba_ref, wx_ref, bx_ref, lam_ref, _,
             dxc_ref, dz_ref, gwa_ref, gwx_ref, gba_ref, gbx_ref, glam_ref, carry, gbuf):
        i = pl.program_id(0)
        first = i == nt - 1

        @pl.when(i == 0)
        def _():
            carry[...] = jnp.zeros_like(carry)
            gwa_ref[...] = jnp.zeros_like(gwa_ref)
            gwx_ref[...] = jnp.zeros_like(gwx_ref)
            gba_ref[...] = jnp.zeros_like(gba_ref)
            gbx_ref[...] = jnp.zeros_like(gbx_ref)
            glam_ref[...] = jnp.zeros_like(glam_ref)

        prev = jnp.where(first, 0.0, xp_ref[...])
        xc = _conv_fwd(jnp.concatenate([prev, x_ref[...]], axis=0), cw_ref, cb_ref)
        r, ig, sp, log_a, a, mult = _rg_gates(xc, wa_ref, ba_ref, wx_ref, bx_ref, lam_ref)
        z = z_ref[...]
        sgz = _sigmoid(z)
        dy = dy_ref[0]
        hh_v = hh_ref[...]
        dz_ref[0] = _bf(dy * hh_v * _dsilu(z, sgz))
        dhh = dy * (z * sgz)
        rows = lax.broadcasted_iota(jnp.int32, a.shape, 0)
        coef = jnp.where(rows == tm - 1, carry[1:2, :], _shift_up(a, 1))
        ca, cu = _scan_groups(coef, dhh, reverse=True)
        c = carry[0:1, :]
        for j in range(per - 1, -1, -1):
            blk = ca[j * 8:(j + 1) * 8] * c + cu[j * 8:(j + 1) * 8]
            gbuf[j * 8:(j + 1) * 8, :] = blk
            c = blk[0:1]
        carry[0:1, :] = c
        carry[1:2, :] = a[0:1]
        g = gbuf[...]
        hprev_tile = jnp.where(first, 0.0, hp_ref[...])
        hprev = _shift_down(jnp.concatenate([hprev_tile, hh_v], axis=0), 1)[HALO:]
        da = g * hprev
        gx_ = g * xc
        d_mult = gx_ * ig
        d_ig = gx_ * mult
        dxc = g * mult * ig
        a2 = jnp.exp(2.0 * log_a)
        dlog_a = da * a - d_mult * (a2 / mult)
        d_r = dlog_a * ((-RG_C) * sp)
        glam_ref[...] += _colsum(dlog_a * ((-RG_C) * r)) * (-_sigmoid(-lam_ref[...]))
        d_ga = d_r * r * (1.0 - r)
        d_gx = d_ig * ig * (1.0 - ig)
        gba_ref[...] += _colsum(d_ga)
        gbx_ref[...] += _colsum(d_gx)
        xb = _bf(xc)
        dgab = _bf(d_ga)
        dgxb = _bf(d_gx)
        for h in range(heads):
            hs = slice(h * hd, (h + 1) * hd)
            dxc_ref[:, hs] = dxc[:, hs] + _dot_nt(dgab[:, hs], wa_ref[h]) + _dot_nt(dgxb[:, hs], wx_ref[h])
            gwa_ref[h] += _dot_tn(xb[:, hs], dgab[:, hs])
            gwx_ref[h] += _dot_tn(xb[:, hs], dgxb[:, hs])

    rev = lambda i: nt - 1 - i
    row = pl.BlockSpec((tm, d), lambda i: (rev(i), 0))
    halo_prev = lambda col: pl.BlockSpec((HALO, d), lambda i: (jnp.maximum(rev(i) * per - 1, 0), col))
    vec = pl.BlockSpec((1, d), lambda i: (0, 0))
    whole3 = lambda a: pl.BlockSpec(a.shape, lambda i: (0, 0, 0))
    return _pcall(
        body, name="rg_bwd", grid=(nt,),
        in_specs=[pl.BlockSpec((1, tm, d), lambda i: (0, rev(i), 0)), row, halo_prev(0),
                  pl.BlockSpec((tm, d), lambda i: (rev(i), 1)), row, halo_prev(0),
                  pl.BlockSpec((CONV_WIDTH, d), lambda i: (0, 0)), vec, whole3(wa_b), vec, whole3(wx_b), vec, vec,
                  pl.BlockSpec(memory_space=pl.ANY)],
        out_specs=[row, pl.BlockSpec((1, tm, d), lambda i: (DU_PLANE[1], rev(i), 0)), whole3(wa_b), whole3(wa_b),
                   vec, vec, vec],
        out_shape=[jax.ShapeDtypeStruct((s_len, d), F32), jax.ShapeDtypeStruct(du.shape, BF16),
                   jax.ShapeDtypeStruct(wa_b.shape, F32), jax.ShapeDtypeStruct(wa_b.shape, F32)]
        + [jax.ShapeDtypeStruct((1, d), F32)] * 3,
        scratch_shapes=[pltpu.VMEM((8, d), F32), pltpu.VMEM((tm, d), F32)],
        input_output_aliases={13: 1},
        compiler_params=_seq(),
    )(d_ycat, u, u, u, hh, hh, conv_w, conv_b, wa_b, ba, wx_b, bx, lam, du)


def _in_bwd(du, w4, x, dxn, g, scale):
    s_len, d = x.shape
    tm = _tile(s_len, 256)
    nsh_chips, _, nsh = w4.shape
    npc = du.shape[0]
    ck = d // 4
    assert nsh % ck == 0 and npc * d == nsh_chips * nsh

    def body(du_ref, w_ref, x_ref, dxn_ref, g_ref, sc_ref, dx_ref, dsh_ref, dsc_ref, dg_ref):
        @pl.when(pl.program_id(0) == 0)
        def _():
            dsh_ref[...] = jnp.zeros_like(dsh_ref)
            dsc_ref[...] = jnp.zeros_like(dsc_ref)
            dg_ref[...] = jnp.zeros_like(dg_ref)

        dh = None
        for q in range(npc * d // ck):
            col = q * ck
            p, pc = col // d, col % d
            s, sc = col // nsh, col % nsh
            t = _dot_nt(du_ref[DU_PLANE[p], :, pc:pc + ck], w_ref[s, :, sc:sc + ck])
            dh = t if dh is None else dh + t
        xv = x_ref[...]
        r = lax.rsqrt(jnp.mean(xv * xv, axis=-1, keepdims=True) + EPS)
        xn = xv * r
        gv = g_ref[...]
        onesc = 1.0 + sc_ref[...]
        dsh_ref[...] += _colsum(dh)
        dsc_ref[...] += _colsum(dh * (xn * gv))
        dg_ref[...] += _colsum(dh * xn * onesc)
        dxh = dh * (gv * onesc)
        dx_ref[...] = dxn_ref[...] + r * (dxh - xn * jnp.mean(dxh * xn, axis=-1, keepdims=True))

    row = pl.BlockSpec((tm, d), lambda i: (i, 0))
    vec = pl.BlockSpec((1, d), lambda i: (0, 0))
    return _pcall(
        body, name="in_bwd", grid=(s_len // tm,),
        in_specs=[pl.BlockSpec((npc, tm, d), lambda i: (0, i, 0)), pl.BlockSpec(w4.shape, lambda i: (0, 0, 0)), row, row,
                  vec, vec],
        out_specs=[row, vec, vec, vec],
        out_shape=[jax.ShapeDtypeStruct((s_len, d), F32)] + [jax.ShapeDtypeStruct((1, d), F32)] * 3,
        compiler_params=_seq(),
    )(du, w4, x, dxn, g, scale)


def _layer_fwd(x, p):
    h_b, u = _ln_inproj(x, p["norm_g"], p["scale"], p["shift"], p["w4"])
    hh, ycat = _rg_fwd(u, p["rg_conv_w"], p["rg_conv_b"], p["rg_wa_b"], p["rg_ba"], p["rg_wx_b"], p["rg_bx"],
                       p["rg_lam"])
    qkv, gt, gtt = _ml_pre(u, p["ml_conv_w"], p["ml_conv_b"], p["wqkv_b"], p["wif_b"], p["wift_b"], p["b_if"],
                           p["b_ift"])
    cell, ycat, cst, nst, mst = _mlstm_fwd(qkv, gt, gtt, u, p["ml_g"], ycat)
    y, x_new = _out_proj(ycat, p["w_out_b"], x, p["gate"])
    saved = dict(x=x, h_b=h_b, u=u, hh=hh, qkv=qkv, gt=gt, gtt=gtt, cell=cell, ycat=ycat, cst=cst, nst=nst, mst=mst,
                 y=y)
    return x_new, saved


def _layer_bwd(dxn, p, s, layer, stacks):
    u = s["u"]
    d = dxn.shape[1]
    d_gate, dy_b, d_ycat = _out_bwd(dxn, s["y"], p["gate"], p["w_out_b"])
    stacks["w_out"] = _grad_matmul(s["ycat"], dy_b[None], 2, lambda b: b, lambda b: 0, (2 * d, d), (d, d),
                                   lambda b: (b, 0), layer, stacks["w_out"])
    dqkv, dgt, g_b_if, du, g_ml_g = _mlstm_bwd(s["qkv"], s["gt"], s["gtt"], s["cst"], s["nst"], s["mst"], s["cell"], u,
                                               p["ml_g"], d_ycat, p["wif_b"])
    ng = dgt.shape[1]
    g_w_if = _grad_matmul(s["qkv"], _bf(dgt)[None], 3, lambda b: b, lambda b: 0, (3 * d, ng), (d, ng),
                          lambda b: (b, 0), 0, 1)[0]
    dpre_ml, dmlx_direct, g_wqkv = _ml_pre_bwd(dqkv, u, p["ml_conv_w"], p["ml_conv_b"], p["wqkv_b"])
    du, g_ml_cw, g_ml_cb = _conv_bwd(dpre_ml, u, 2, p["ml_conv_w"], du, DU_PLANE[2], direct=dmlx_direct)
    dxc_rg, du, g_wa, g_wx, g_ba, g_bx, g_lam = _rg_bwd(d_ycat, u, s["hh"], p["rg_conv_w"], p["rg_conv_b"],
                                                        p["rg_wa_b"], p["rg_ba"], p["rg_wx_b"], p["rg_bx"],
                                                        p["rg_lam"], du)
    du, g_rg_cw, g_rg_cb = _conv_bwd(dxc_rg, u, 0, p["rg_conv_w"], du, DU_PLANE[0])
    npc = du.shape[0]
    stacks["w_in"] = _grad_matmul(s["h_b"][None], du, npc, lambda b: 0, lambda b: (b + DU_PLANE[0]) % npc,
                                  (d, npc * d), (d, d), lambda b: (0, b), layer, stacks["w_in"])
    dx, d_shift, d_scale, g_norm_g = _in_bwd(du, p["w4"], s["x"], dxn, p["norm_g"], p["scale"])
    grads = dict(norm_g=g_norm_g, rg_conv_w=g_rg_cw, rg_conv_b=g_rg_cb, rg_w_a=g_wa, rg_b_a=g_ba,
                 rg_w_x=g_wx, rg_b_x=g_bx, rg_lambda=g_lam, ml_conv_w=g_ml_cw, ml_conv_b=g_ml_cb, ml_w_qkv=g_wqkv,
                 ml_w_if=g_w_if, ml_b_if=g_b_if, ml_norm_g=g_ml_g)
    return dx, grads, jnp.concatenate([d_shift, d_scale, d_gate], axis=1)


def _trunk_fwd_bwd(x, target, final_g, layers):
    saved = []
    for p in layers:
        x, s = _layer_fwd(x, p)
        saved.append(s)
    dx, g_final, loss = _final_loss(x, final_g, target)
    grads, dmods = [], []
    stacks = dict(w_in=len(layers), w_out=len(layers))
    for layer in reversed(range(len(layers))):
        dx, g, dm = _layer_bwd(dx, layers[layer], saved[layer], layer, stacks)
        grads.append(g)
        dmods.append(dm)
    return loss, dx, g_final, grads[::-1], stacks["w_in"], stacks["w_out"], dmods[::-1]


def _me():
    return lax.axis_index("x"), lax.axis_index("y"), lax.axis_index("c")


def _remote(src, dst, send_sem, recv_sem, to):
    return pltpu.make_async_remote_copy(src_ref=src, dst_ref=dst, send_sem=send_sem, recv_sem=recv_sem,
                                        device_id=to, device_id_type=MESH)


def _all_gather8(blocks, space):
    n = len(blocks)

    def body(*refs):
        x_refs, out_refs = refs[:n], refs[n:2 * n]
        send_sems, recv_sems, local_sems = refs[2 * n:]
        x, y, c = _me()
        me, sibling = (x, y, c), (x, y, 1 - c)
        chips = [(1 - x, y), (x, 1 - y), (1 - x, 1 - y)]

        def rows(i, px, py, pc):
            m_per = blocks[i].shape[0]
            return out_refs[i].at[pl.ds((4 * px + 2 * py + pc) * m_per, m_per), :]

        def copy(i, k, blk, to, src=None):
            return _remote(rows(i, *blk) if src is None else src, rows(i, *blk), send_sems.at[7 * i + k],
                           recv_sems.at[7 * i + k], to)

        mine = [pltpu.make_async_copy(x_refs[i], rows(i, *me), local_sems.at[i]) for i in range(n)]
        first = []
        for i in range(n):
            first.append(copy(i, 0, me, sibling, src=x_refs[i]))
            first += [copy(i, 1 + j, me, (*chip, c), src=x_refs[i]) for j, chip in enumerate(chips)]
        for cp in mine + first:
            cp.start()
        passed = []
        for j, chip in enumerate(chips):
            for i in range(n):
                copy(i, 1 + j, (*chip, c), me).wait_recv()
                passed.append(copy(i, 4 + j, (*chip, c), sibling))
                passed[-1].start()
        for i in range(n):
            copy(i, 0, sibling, me).wait_recv()
            for j, chip in enumerate(chips):
                copy(i, 4 + j, (*chip, 1 - c), me).wait_recv()
        for cp in first + passed:
            cp.wait_send()
        for cp in mine:
            cp.wait()

    spec = pl.BlockSpec(memory_space=space)
    return _pcall(
        body, name="all_gather8",
        out_shape=[jax.ShapeDtypeStruct((8 * b.shape[0], b.shape[1]), b.dtype) for b in blocks],
        in_specs=[spec] * n, out_specs=[spec] * n,
        scratch_shapes=[pltpu.SemaphoreType.DMA((7 * n,)), pltpu.SemaphoreType.DMA((7 * n,)),
                        pltpu.SemaphoreType.DMA((n,))],
    )(*blocks)


def _sib_halves(g_in, g_out, slabs):
    depth, d, n4 = g_in.shape
    n = n4 // 4
    ns = len(slabs)

    def body(*refs):
        gi, go = refs[0], refs[1]
        sl = refs[2:2 + ns]
        ri, ro = refs[2 + ns], refs[3 + ns]
        rs = refs[4 + ns:4 + 2 * ns]
        send_sems, recv_sems = refs[4 + 2 * ns:]
        x, y, c = _me()
        o = 1 - c
        pairs = [(gi.at[pl.ds(0, depth), pl.ds(o * (d // 2), d // 2), pl.ds(s * n, n)], ri.at[pl.ds(0, depth), s])
                 for s in range(4)]
        pairs.append((go.at[pl.ds(0, depth), pl.ds(0, 4), o], ro))
        pairs += [(sl[i].at[o], rs[i]) for i in range(ns)]
        copies = [_remote(src, dst, send_sems.at[k], recv_sems.at[k], (x, y, o)) for k, (src, dst) in enumerate(pairs)]
        for cp in copies:
            cp.start()
        for cp in copies:
            cp.wait_recv()
        for cp in copies:
            cp.wait_send()

    hbm = pl.BlockSpec(memory_space=pltpu.HBM)
    ncp = 5 + ns
    return _pcall(
        body, name="sib_halves",
        out_shape=[jax.ShapeDtypeStruct((depth, 4, d // 2, n), g_in.dtype),
                   jax.ShapeDtypeStruct(g_out.shape[:2] + g_out.shape[3:], g_out.dtype)]
        + [jax.ShapeDtypeStruct(s.shape[1:], s.dtype) for s in slabs],
        in_specs=[hbm] * (2 + ns), out_specs=[hbm] * (2 + ns),
        scratch_shapes=[pltpu.SemaphoreType.DMA((ncp,)), pltpu.SemaphoreType.DMA((ncp,))],
    )(g_in, g_out, *slabs)


def _sib_fill(halves):
    n = len(halves)

    def body(*refs):
        src, dst = refs[:n], refs[n:2 * n]
        send_sems, recv_sems, local_sems = refs[2 * n:]
        x, y, c = _me()
        view = lambda i: dst[i].at[pl.ds(0, halves[i].shape[0]), c]
        local = [pltpu.make_async_copy(src[i], view(i), local_sems.at[i]) for i in range(n)]
        copies = [_remote(src[i], view(i), send_sems.at[i], recv_sems.at[i], (x, y, 1 - c)) for i in range(n)]
        for cp in local + copies:
            cp.start()
        for cp in copies:
            cp.wait_recv()
        for cp in copies:
            cp.wait_send()
        for cp in local:
            cp.wait()

    hbm = pl.BlockSpec(memory_space=pltpu.HBM)
    return _pcall(
        body, name="sib_fill",
        out_shape=[jax.ShapeDtypeStruct((h.shape[0], 2) + h.shape[1:], h.dtype) for h in halves],
        in_specs=[hbm] * n, out_specs=[hbm] * n,
        scratch_shapes=[pltpu.SemaphoreType.DMA((n,)), pltpu.SemaphoreType.DMA((n,)), pltpu.SemaphoreType.DMA((n,))],
    )(*halves)


def _chip_exchange(arrs):
    n = len(arrs)

    def body(*refs):
        src, dst = refs[:n], refs[n:2 * n]
        send_sems, recv_sems, local_sems = refs[2 * n:]
        x, y, c = _me()
        me_s = 2 * x + y
        chips = [(1 - x, y), (x, 1 - y), (1 - x, 1 - y)]
        local = [pltpu.make_async_copy(src[i].at[me_s], dst[i].at[me_s], local_sems.at[i]) for i in range(n)]
        copies = [_remote(src[i].at[2 * px + py], dst[i].at[me_s], send_sems.at[3 * i + k], recv_sems.at[3 * i + k],
                          (px, py, c))
                  for i in range(n) for k, (px, py) in enumerate(chips)]
        for cp in local + copies:
            cp.start()
        for cp in copies:
            cp.wait_recv()
        for cp in copies:
            cp.wait_send()
        for cp in local:
            cp.wait()

    hbm = pl.BlockSpec(memory_space=pltpu.HBM)
    return _pcall(
        body, name="chip_exchange",
        out_shape=[jax.ShapeDtypeStruct(a.shape, a.dtype) for a in arrs],
        in_specs=[hbm] * n, out_specs=[hbm] * n,
        scratch_shapes=[pltpu.SemaphoreType.DMA((3 * n,)), pltpu.SemaphoreType.DMA((3 * n,)),
                        pltpu.SemaphoreType.DMA((n,))],
    )(*arrs)


def _row_tile(rows, cap=4096, mult=16):
    best = None
    for t in range(mult, min(rows, cap) + 1, mult):
        if rows % t == 0:
            best = t
    return rows if best is None else best


def _pair_sum(half, own, own_spec, got, got_spec, out_shape, out_spec, grid):
    def body(_, a_ref, b_ref, o_ref):
        o_ref[...] = (a_ref[...] + b_ref[...].astype(F32)).astype(o_ref.dtype)

    return _pcall(
        body, name="pair_sum",
        grid_spec=pltpu.PrefetchScalarGridSpec(num_scalar_prefetch=1, grid=grid, in_specs=[own_spec, got_spec],
                                               out_specs=out_spec),
        out_shape=out_shape, compiler_params=_seq(len(grid)))(half, own, got)


def _chip_sum(parts):
    _, rows, n = parts.shape
    tr = _row_tile(rows, cap=max(16, (1 << 18) // n))

    def body(p_ref, o_ref):
        acc = p_ref[0].astype(F32) + p_ref[1].astype(F32)
        acc = acc + p_ref[2].astype(F32)
        o_ref[...] = acc + p_ref[3].astype(F32)

    return _pcall(body, name="chip_sum", grid=(rows // tr,),
                  in_specs=[pl.BlockSpec((4, tr, n), lambda i: (0, i, 0))], out_specs=pl.BlockSpec((tr, n), lambda i: (i, 0)),
                  out_shape=jax.ShapeDtypeStruct((rows, n), F32), compiler_params=_seq())(parts)


def _ada_mod(c_all, w_ada, b_ada_cols):
    depth, d, n = w_ada.shape
    nb = c_all.shape[0]

    def body(c_ref, w_ref, b_ref, o_ref):
        cv = c_ref[...]
        ca = _bf(cv * _sigmoid(cv))
        o_ref[0] = _dot(ca, _bf(w_ref[0])) + b_ref[0]

    return _pcall(body, name="ada_mod", grid=(depth,),
                  in_specs=[pl.BlockSpec((nb, d), lambda l: (0, 0)), pl.BlockSpec((1, d, n), lambda l: (l, 0, 0)),
                            pl.BlockSpec((1, 1, n), lambda l: (l, 0, 0))],
                  out_specs=pl.BlockSpec((1, nb, n), lambda l: (l, 0, 0)),
                  out_shape=jax.ShapeDtypeStruct((depth, nb, n), F32), compiler_params=_seq())(c_all, w_ada, b_ada_cols)


def _ada_grad(c_all, dmod_cols, dmod_all):
    nb, d = c_all.shape
    depth, _, n = dmod_cols.shape
    n_all = dmod_all.shape[2]

    def body(c_ref, dm_ref, da_ref, gw_ref, gb_ref):
        cv = c_ref[...]
        ca = _bf(cv * _sigmoid(cv))
        gw_ref[0] = _dot_tn(ca, _bf(dm_ref[0]))
        gb_ref[0] = _colsum(da_ref[0])

    return _pcall(body, name="ada_grad", grid=(depth,),
                  in_specs=[pl.BlockSpec((nb, d), lambda l: (0, 0)), pl.BlockSpec((1, nb, n), lambda l: (l, 0, 0)),
                            pl.BlockSpec((1, nb, n_all), lambda l: (l, 0, 0))],
                  out_specs=[pl.BlockSpec((1, d, n), lambda l: (l, 0, 0)), pl.BlockSpec((1, 1, n_all), lambda l: (l, 0, 0))],
                  out_shape=[jax.ShapeDtypeStruct((depth, d, n), F32), jax.ShapeDtypeStruct((depth, 1, n_all), F32)],
                  compiler_params=_seq())(c_all, dmod_cols, dmod_all)


def _adamw(w, g, m, v):
    shape = w.shape
    cols = shape[-1]
    rows = w.size // cols
    w2, g2, m2, v2 = (t.reshape(rows, cols) for t in (w, g, m, v))
    tr = _row_tile(rows, cap=max(8, (1 << 18) // cols), mult=8)

    def body(w_ref, g_ref, m_ref, v_ref, d_ref, mo_ref, vo_ref):
        gv = g_ref[...]
        mn = ADAM_B1 * m_ref[...] + (1.0 - ADAM_B1) * gv
        vn = ADAM_B2 * v_ref[...] + (1.0 - ADAM_B2) * (gv * gv)
        m_hat = mn / (1.0 - ADAM_B1 ** ADAM_STEP)
        v_hat = vn / (1.0 - ADAM_B2 ** ADAM_STEP)
        d_ref[...] = -ADAM_LR * (m_hat / (jnp.sqrt(v_hat) + ADAM_EPS) + ADAM_WD * w_ref[...])
        mo_ref[...] = mn
        vo_ref[...] = vn

    blk = pl.BlockSpec((tr, cols), lambda i: (i, 0))
    outs = _pcall(body, name="adamw", grid=(rows // tr,), in_specs=[blk] * 4, out_specs=[blk] * 3,
                  out_shape=[jax.ShapeDtypeStruct((rows, cols), F32)] * 3, compiler_params=_seq())(w2, g2, m2, v2)
    return tuple(o.reshape(shape) for o in outs)


WEIGHTS = ["norm_g", "w_ada", "b_ada", "w_in", "rg_conv_w", "rg_conv_b", "rg_w_a", "rg_b_a", "rg_w_x", "rg_b_x",
           "rg_lambda", "ml_conv_w", "ml_conv_b", "ml_w_q", "ml_w_k", "ml_w_v", "ml_w_if", "ml_b_if", "ml_norm_g",
           "w_out", "final_g"]
SMALL_SHARDED = {"ml_w_qkv": 2, "rg_conv_w": 1, "ml_conv_w": 1, "ml_w_if": 0}
REPLICATED = ["rg_w_a", "rg_w_x", "norm_g", "rg_conv_b", "rg_b_a", "rg_b_x", "rg_lambda", "ml_conv_b", "ml_norm_g",
              "ml_b_if"]
LANES = 128


def _to_pieces(g, axis):
    shp = g.shape
    g = g.reshape(shp[:axis] + (4, 2, shp[axis] // 8) + shp[axis + 1:])
    g = jnp.moveaxis(g, (axis, axis + 1), (0, 1))
    return g.reshape(4, 2, -1)


def _from_pieces(p, shard_shape, axis):
    k = p.shape[0]
    rest = shard_shape[:axis] + (shard_shape[axis] // k,) + shard_shape[axis + 1:]
    t = jnp.moveaxis(p.reshape((k,) + rest), 0, axis)
    return t.reshape(shard_shape)


def _pad_rows(flat, mult):
    n = flat.shape[-1]
    pad = (-n) % mult
    if pad:
        flat = jnp.concatenate([flat, jnp.zeros(flat.shape[:-1] + (pad,), flat.dtype)], axis=-1)
    return flat


def kernel(x, c, norm_g, w_ada, b_ada, w_in, rg_conv_w, rg_conv_b, rg_w_a, rg_b_a, rg_w_x, rg_b_x, rg_lambda, ml_conv_w, ml_conv_b, ml_w_q, ml_w_k, ml_w_v, ml_w_if, ml_b_if, ml_norm_g, w_out, final_g, loss_target, m_norm_g, m_w_ada, m_b_ada, m_w_in, m_rg_conv_w, m_rg_conv_b, m_rg_w_a, m_rg_b_a, m_rg_w_x, m_rg_b_x, m_rg_lambda, m_ml_conv_w, m_ml_conv_b, m_ml_w_q, m_ml_w_k, m_ml_w_v, m_ml_w_if, m_ml_b_if, m_ml_norm_g, m_w_out, m_final_g, v_norm_g, v_w_ada, v_b_ada, v_w_in, v_rg_conv_w, v_rg_conv_b, v_rg_w_a, v_rg_b_a, v_rg_w_x, v_rg_b_x, v_rg_lambda, v_ml_conv_w, v_ml_conv_b, v_ml_w_q, v_ml_w_k, v_ml_w_v, v_ml_w_if, v_ml_b_if, v_ml_norm_g, v_w_out, v_final_g):
    given = dict(locals())
    ax, ay, ac = lax.axis_index("x"), lax.axis_index("y"), lax.axis_index("c")
    chip = 2 * ax + ay
    me = 2 * chip + ac
    depth, d = norm_g.shape
    n_ada = w_ada.shape[2]
    pick = lambda a, i, axis=0: lax.dynamic_index_in_dim(a, i, axis, keepdims=False)

    convs = jnp.stack([rg_conv_w, ml_conv_w])
    n_conv = 2 * depth * CONV_WIDTH // 4
    blk = jnp.concatenate([c, convs.reshape(n_conv, d), jnp.zeros((8 - 1 - n_conv, d), F32)], axis=0)
    g0 = _all_gather8([blk], pltpu.VMEM)[0].reshape(8, 8, d)
    c_all = g0[:, 0, :]
    conv_full = g0[0::2, 1:1 + n_conv].reshape(4, 2, depth, CONV_WIDTH, d // 4)
    conv_full = conv_full.transpose(1, 2, 3, 0, 4).reshape(2, depth, CONV_WIDTH, d)

    b_cols = lax.dynamic_slice_in_dim(b_ada, chip * n_ada, n_ada, axis=1)[:, None, :]
    mod_part = _ada_mod(c_all, w_ada, b_cols)
    g1 = _all_gather8([mod_part.transpose(1, 0, 2).reshape(8, depth * n_ada)], pltpu.VMEM)[0]
    g1 = g1.reshape(8, 8, depth, n_ada)[0::2]
    mod_me = pick(g1.transpose(1, 2, 0, 3).reshape(8, depth, 4 * n_ada), me)

    def half_of(w, axis):
        n = w.shape[axis] // 2
        return lax.dynamic_slice_in_dim(w, ac * n, n, axis).astype(BF16)

    n_sh = w_in.shape[2]
    heads, hd_cut, hd = ml_w_q.shape[1:]
    blocks = []
    for l in range(depth):
        wqkv = jnp.stack([ml_w_q[l], ml_w_k[l], ml_w_v[l]])
        blocks += [half_of(w_in[l], 0), half_of(w_out[l], 0), half_of(wqkv, 2).reshape(-1, hd), half_of(ml_w_if[l], 0)]
    gathered = _all_gather8(blocks, pltpu.HBM)
    layers = []
    for l in range(depth):
        w4, w_out_b, wqkv_g, wif = gathered[4 * l:4 * l + 4]
        wqkv_b = _from_pieces(wqkv_g.reshape(8, -1), (3, heads, hd, hd), 2)
        layers.append(dict(
            norm_g=norm_g[l][None], shift=mod_me[l, 0:d][None], scale=mod_me[l, d:2 * d][None],
            gate=mod_me[l, 2 * d:3 * d][None], w4=w4.reshape(4, d, n_sh),
            rg_conv_w=conv_full[0, l], rg_conv_b=rg_conv_b[l][None], rg_wa_b=_bf(rg_w_a[l]), rg_ba=rg_b_a[l][None],
            rg_wx_b=_bf(rg_w_x[l]), rg_bx=rg_b_x[l][None], rg_lam=rg_lambda[l][None],
            ml_conv_w=conv_full[1, l], ml_conv_b=ml_conv_b[l][None], wqkv_b=wqkv_b, wif_b=wif, wift_b=wif.T,
            b_if=ml_b_if[l][None], b_ift=ml_b_if[l][:, None], ml_g=ml_norm_g[l][None], w_out_b=w_out_b))

    loss, dx, g_final, grads, g_in, g_out, dmods = _trunk_fwd_bwd(x[0], loss_target[0], final_g[None], layers)

    dm_blk = jnp.concatenate(dmods + [jnp.zeros((8 - depth, 3 * d), F32)], axis=0)
    dm_all = _all_gather8([dm_blk], pltpu.VMEM)[0].reshape(8, 8, 3 * d)[:, :depth].transpose(1, 0, 2)
    dm_cols = lax.dynamic_slice_in_dim(dm_all, chip * n_ada, n_ada, axis=2)
    g_w_ada, g_b_ada = _ada_grad(c_all, dm_cols, dm_all)

    r_out = g_out.shape[1] // 8
    g_out5 = g_out.reshape(depth, 4, 2, r_out, d)
    sm = jnp.concatenate([_to_pieces(grads[l][name], axis) for l in range(depth) for name, axis in SMALL_SHARDED.items()],
                         axis=-1)
    sm = _pad_rows(sm, 16 * LANES)
    n_sm = sm.shape[-1] // LANES
    sm = sm.transpose(1, 0, 2).reshape(2, 4 * n_sm, LANES)
    rep = [grads[l][name].reshape(-1) for l in range(depth) for name in REPLICATED[:-1]]
    rep += [_pad_rows(grads[l]["ml_b_if"].reshape(-1), LANES) for l in range(depth)]
    rep += [g_final.reshape(-1), loss.reshape(-1)]
    rep = _pad_rows(jnp.concatenate(rep), 8 * 8 * LANES)
    n_rep = rep.shape[0] // (8 * LANES)
    rep = rep.reshape(4, 2, n_rep, LANES).transpose(1, 0, 2, 3).reshape(2, 4 * n_rep, LANES)
    got_in, got_out, got_sm, got_rep = _sib_halves(g_in, g_out5, [sm, rep])
    half = ac.reshape(1)
    part_in = _pair_sum(
        half, g_in, pl.BlockSpec((None, d // 2, n_sh), lambda l, s, h: (l, h[0], s)),
        got_in, pl.BlockSpec((None, None, d // 2, n_sh), lambda l, s, h: (l, s, 0, 0)),
        jax.ShapeDtypeStruct((4, depth, d // 2, n_sh), BF16),
        pl.BlockSpec((None, None, d // 2, n_sh), lambda l, s, h: (s, l, 0, 0)), (depth, 4))
    part_out = _pair_sum(
        half, g_out5, pl.BlockSpec((None, None, None, r_out, d), lambda l, s, h: (l, s, h[0], 0, 0)),
        got_out, pl.BlockSpec((None, None, r_out, d), lambda l, s, h: (l, s, 0, 0)),
        jax.ShapeDtypeStruct((4, depth, r_out, d), BF16),
        pl.BlockSpec((None, None, r_out, d), lambda l, s, h: (s, l, 0, 0)), (depth, 4))

    def slab_sum(slab, got, rows, dtype):
        blk = pl.BlockSpec((rows, LANES), lambda s, h: (s, 0))
        return _pair_sum(half, slab, pl.BlockSpec((None, rows, LANES), lambda s, h: (h[0], s, 0)), got, blk,
                         jax.ShapeDtypeStruct((4 * rows, LANES), dtype), blk, (4,)).reshape(4, rows, LANES)

    part_sm = slab_sum(sm, got_sm, n_sm, BF16)
    part_rep = slab_sum(rep, got_rep, n_rep, F32)
    met_in, met_out, met_sm, met_rep = _chip_exchange([part_in, part_out, part_sm, part_rep])
    red_in = _chip_sum(met_in.reshape(4, depth * (d // 2), n_sh)).reshape(depth, d // 2, n_sh)
    red_out = _chip_sum(met_out.reshape(4, depth * r_out, d)).reshape(depth, r_out, d)
    red_sm = _chip_sum(met_sm)
    red_rep = _chip_sum(met_rep)
    both_in, both_out, both_sm = _sib_fill([red_in, red_out, red_sm[None]])
    rep_all = _all_gather8([red_rep], pltpu.VMEM)[0].reshape(-1)

    g = dict(w_ada=g_w_ada, b_ada=g_b_ada.reshape(b_ada.shape), w_in=both_in.reshape(w_in.shape),
             w_out=both_out.reshape(w_out.shape))
    shard = both_sm.reshape(2, -1)
    off = 0
    per_layer = {name: [] for name in SMALL_SHARDED}
    for l in range(depth):
        for name, axis in SMALL_SHARDED.items():
            shp = (3,) + ml_w_q.shape[1:] if name == "ml_w_qkv" else given[name].shape[1:]
            n = grads[l][name].size // 8
            per_layer[name].append(_from_pieces(shard[:, off:off + n], shp, axis))
            off += n
    for name in SMALL_SHARDED:
        g[name] = jnp.stack(per_layer[name])
    for i, name in enumerate(["ml_w_q", "ml_w_k", "ml_w_v"]):
        g[name] = g["ml_w_qkv"][:, i]
    off = 0
    per_layer = {name: [] for name in REPLICATED}
    for l in range(depth):
        for name in REPLICATED[:-1]:
            n = given[name][l].size
            per_layer[name].append(rep_all[off:off + n].reshape(given[name].shape[1:]))
            off += n
    for l in range(depth):
        n = given["ml_b_if"][l].size
        per_layer["ml_b_if"].append(rep_all[off:off + n])
        off += LANES
    for name in REPLICATED:
        g[name] = jnp.stack(per_layer[name])
    g["final_g"] = rep_all[off:off + d]
    loss_all = rep_all[off + d]

    deltas, new_m, new_v = [], [], []
    for name in WEIGHTS:
        dl, mn, vn = _adamw(given[name], g[name], given["m_" + name], given["v_" + name])
        deltas.append(dl)
        new_m.append(mn)
        new_v.append(vn)
    return (loss_all, dx[None], *[g[name] for name in WEIGHTS], *deltas, *new_m, *new_v)
```

```python
import functools

import jax
import jax.numpy as jnp
from jax import lax
from jax.experimental import pallas as pl
from jax.experimental.pallas import tpu as pltpu

F32 = jnp.float32
BF16 = jnp.bfloat16

EPS = 1e-6
RG_C = 8.0
CONV_WIDTH = 4
ML_CHUNK = 128
HALO = 8
ADAM_LR = 0.001
ADAM_B1 = 0.9
ADAM_B2 = 0.999
ADAM_EPS = 1e-08
ADAM_WD = 0.01
ADAM_STEP = 10
MESH = pl.DeviceIdType.MESH


def _pcall(body, **kw):
    return pl.pallas_call(body, **kw)


def _seq(n=1):
    return pltpu.CompilerParams(dimension_semantics=("arbitrary",) * n)


def _dot(a, b):
    return jnp.dot(a, b, preferred_element_type=F32)


def _dot_nt(a, b):
    return lax.dot_general(a, b, (((1,), (1,)), ((), ())), preferred_element_type=F32)


def _dot_tn(a, b):
    return lax.dot_general(a, b, (((0,), (0,)), ((), ())), preferred_element_type=F32)


def _bf(x):
    return x.astype(BF16)


def _sigmoid(x):
    return 1.0 / (1.0 + jnp.exp(-x))


def _log1p(z):
    u = 1.0 + z
    return jnp.where(u == 1.0, z, jnp.log(u) * (z / jnp.where(u == 1.0, 1.0, u - 1.0)))


def _softplus(x):
    return jnp.maximum(x, 0.0) + _log1p(jnp.exp(-jnp.abs(x)))


def _log_sigmoid(x):
    return -_softplus(-x)


def _expm1(x):
    small = x * (1.0 + x * (0.5 + x * (1.0 / 6.0 + x * (1.0 / 24.0 + x * (1.0 / 120.0)))))
    return jnp.where(jnp.abs(x) < 0.03, small, jnp.exp(x) - 1.0)


def _dsilu(x, s):
    return s * (1.0 + x * (1.0 - s))


def _rowsum(x):
    return jnp.sum(x, axis=1, keepdims=True)


def _colsum(x):
    return jnp.sum(x, axis=0, keepdims=True)


def _shift_down(win, s):
    return win if s == 0 else pltpu.roll(win, s, 0)


def _shift_up(win, s):
    return win if s == 0 else pltpu.roll(win, win.shape[0] - s, 0)


def _conv_fwd(win, w_ref, b_ref):
    acc = b_ref[...] + w_ref[CONV_WIDTH - 1:CONV_WIDTH, :] * win[HALO:]
    for k in range(CONV_WIDTH - 1):
        acc = acc + w_ref[k:k + 1, :] * _shift_down(win, CONV_WIDTH - 1 - k)[HALO:]
    return acc


def _split3(x):
    hi = _bf(x)
    r1 = x - hi.astype(F32)
    mid = _bf(r1)
    lo = _bf(r1 - mid.astype(F32))
    return hi, mid, lo


def _tri_dot_left(tri, x):
    hi, mid, lo = _split3(x)
    return _dot(tri, hi) + _dot(tri, mid) + _dot(tri, lo)


def _tri_dot_right(x, tri):
    hi, mid, lo = _split3(x)
    return _dot(hi, tri) + _dot(mid, tri) + _dot(lo, tri)


def _tile(n, want):
    t = min(n, want)
    assert n % t == 0
    return t


def _ln_inproj(x, g, scale, shift, w4):
    s_len, d = x.shape
    nj, _, nsh = w4.shape
    tm = _tile(s_len, 512)

    def body(x_ref, g_ref, sc_ref, sh_ref, w_ref, h_ref, u_ref, hs):
        @pl.when(pl.program_id(1) == 0)
        def _():
            xv = x_ref[...]
            r = lax.rsqrt(jnp.mean(xv * xv, axis=-1, keepdims=True) + EPS)
            hv = (xv * r * g_ref[...]) * (1.0 + sc_ref[...]) + sh_ref[...]
            hs[...] = _bf(hv)
            h_ref[...] = hs[...]

        u_ref[...] = _dot(hs[...], w_ref[0])

    vec = pl.BlockSpec((1, d), lambda i, j: (0, 0))
    return _pcall(
        body, name="ln_inproj", grid=(s_len // tm, nj),
        in_specs=[pl.BlockSpec((tm, d), lambda i, j: (i, 0)), vec, vec, vec,
                  pl.BlockSpec((1, d, nsh), lambda i, j: (j, 0, 0))],
        out_specs=[pl.BlockSpec((tm, d), lambda i, j: (i, 0)), pl.BlockSpec((tm, nsh), lambda i, j: (i, j))],
        out_shape=[jax.ShapeDtypeStruct((s_len, d), BF16), jax.ShapeDtypeStruct((s_len, nj * nsh), F32)],
        scratch_shapes=[pltpu.VMEM((tm, d), BF16)],
        compiler_params=_seq(2),
    )(x, g, scale, shift, w4)


def _rg_gates(xc, wa_ref, ba_ref, wx_ref, bx_ref, lam_ref):
    heads, hd, _ = wa_ref.shape
    xb = _bf(xc)
    ga = jnp.concatenate([_dot(xb[:, h * hd:(h + 1) * hd], wa_ref[h]) for h in range(heads)], axis=1) + ba_ref[...]
    gx = jnp.concatenate([_dot(xb[:, h * hd:(h + 1) * hd], wx_ref[h]) for h in range(heads)], axis=1) + bx_ref[...]
    r = _sigmoid(ga)
    ig = _sigmoid(gx)
    sp = _softplus(-lam_ref[...])
    log_a = (-RG_C) * r * sp
    a = jnp.exp(log_a)
    mult = jnp.sqrt(-_expm1(2.0 * log_a))
    return r, ig, sp, log_a, a, mult


def _scan_groups(a, u, reverse):
    n = a.shape[0]
    row = lax.broadcasted_iota(jnp.int32, a.shape, 0) & 7
    for k in (1, 2, 4):
        if reverse:
            a_sh, u_sh = _shift_up(a, k), _shift_up(u, k)
            ok = row < 8 - k
        else:
            a_sh, u_sh = _shift_down(a, k), _shift_down(u, k)
            ok = row >= k
        u = jnp.where(ok, a * u_sh + u, u)
        a = jnp.where(ok, a * a_sh, a)
    del n
    return a, u


def _rg_fwd(u, conv_w, conv_b, wa_b, ba, wx_b, bx, lam):
    s_len = u.shape[0]
    d = conv_w.shape[1]
    tm = _tile(s_len, 256)
    per = tm // HALO

    def body(x_ref, xp_ref, z_ref, cw_ref, cb_ref, wa_ref, ba_ref, wx_ref, bx_ref, lam_ref,
             hh_ref, y_ref, carry):
        i = pl.program_id(0)

        @pl.when(i == 0)
        def _():
            carry[...] = jnp.zeros_like(carry)

        prev = jnp.where(i == 0, 0.0, xp_ref[...])
        xc = _conv_fwd(jnp.concatenate([prev, x_ref[...]], axis=0), cw_ref, cb_ref)
        _, ig, _, _, a, mult = _rg_gates(xc, wa_ref, ba_ref, wx_ref, bx_ref, lam_ref)
        ca, cu = _scan_groups(a, mult * (ig * xc), reverse=False)
        c = carry[0:1, :]
        for j in range(per):
            blk = ca[j * 8:(j + 1) * 8] * c + cu[j * 8:(j + 1) * 8]
            hh_ref[j * 8:(j + 1) * 8, :] = blk
            c = blk[7:8]
        carry[0:1, :] = c
        z = z_ref[...]
        y_ref[0] = _bf(hh_ref[...] * (z * _sigmoid(z)))

    vec = pl.BlockSpec((1, d), lambda i: (0, 0))
    whole3 = lambda a: pl.BlockSpec(a.shape, lambda i: (0, 0, 0))
    return _pcall(
        body, name="rg_fwd", grid=(s_len // tm,),
        in_specs=[pl.BlockSpec((tm, d), lambda i: (i, 0)),
                  pl.BlockSpec((HALO, d), lambda i: (jnp.maximum(i * per - 1, 0), 0)),
                  pl.BlockSpec((tm, d), lambda i: (i, 1)),
                  pl.BlockSpec((CONV_WIDTH, d), lambda i: (0, 0)), vec,
                  whole3(wa_b), vec, whole3(wx_b), vec, vec],
        out_specs=[pl.BlockSpec((tm, d), lambda i: (i, 0)), pl.BlockSpec((1, tm, d), lambda i: (0, i, 0))],
        out_shape=[jax.ShapeDtypeStruct((s_len, d), F32), jax.ShapeDtypeStruct((2, s_len, d), BF16)],
        scratch_shapes=[pltpu.VMEM((8, d), F32)],
        compiler_params=_seq(),
    )(u, u, u, conv_w, conv_b, wa_b, ba, wx_b, bx, lam)


def _ml_pre(u, conv_w, conv_b, wqkv_b, wif_b, wift_b, b_if, b_ift):
    s_len = u.shape[0]
    d = conv_w.shape[1]
    _, heads, hd, _ = wqkv_b.shape
    ng = 2 * heads
    tm = _tile(s_len, 256)
    per = tm // HALO

    def body(x_ref, xp_ref, cw_ref, cb_ref, w_ref, wif_ref, wift_ref, bif_ref, bift_ref,
             qkv_ref, gt_ref, gtt_ref):
        i = pl.program_id(0)
        prev = jnp.where(i == 0, 0.0, xp_ref[...])
        xm = x_ref[...]
        pre = _conv_fwd(jnp.concatenate([prev, xm], axis=0), cw_ref, cb_ref)
        xcb = _bf(pre * _sigmoid(pre))
        xmb = _bf(xm)
        for h in range(heads):
            hs = slice(h * hd, (h + 1) * hd)
            qkv_ref[0, :, hs] = _bf(_dot(xcb[:, hs], w_ref[0, h]))
            qkv_ref[1, :, hs] = _bf(_dot(xcb[:, hs], w_ref[1, h]))
            qkv_ref[2, :, hs] = _bf(_dot(xmb[:, hs], w_ref[2, h]))
        qb, kb, vb = qkv_ref[0], qkv_ref[1], qkv_ref[2]
        gt_ref[...] = (_dot(qb, wif_ref[0:d, :]) + _dot(kb, wif_ref[d:2 * d, :]) + _dot(vb, wif_ref[2 * d:3 * d, :])
                       + bif_ref[...])
        gtt_ref[...] = (_dot_nt(wift_ref[:, 0:d], qb) + _dot_nt(wift_ref[:, d:2 * d], kb)
                        + _dot_nt(wift_ref[:, 2 * d:3 * d], vb) + bift_ref[...])

    vec = pl.BlockSpec((1, d), lambda i: (0, 0))
    whole2 = lambda a: pl.BlockSpec(a.shape, lambda i: (0, 0))
    return _pcall(
        body, name="ml_pre", grid=(s_len // tm,),
        in_specs=[pl.BlockSpec((tm, d), lambda i: (i, 2)),
                  pl.BlockSpec((HALO, d), lambda i: (jnp.maximum(i * per - 1, 0), 2)),
                  pl.BlockSpec((CONV_WIDTH, d), lambda i: (0, 0)), vec,
                  pl.BlockSpec(wqkv_b.shape, lambda i: (0, 0, 0, 0)), whole2(wif_b), whole2(wift_b), whole2(b_if),
                  whole2(b_ift)],
        out_specs=[pl.BlockSpec((3, tm, d), lambda i: (0, i, 0)), pl.BlockSpec((tm, ng), lambda i: (i, 0)),
                   pl.BlockSpec((ng, tm), lambda i: (0, i))],
        out_shape=[jax.ShapeDtypeStruct((3, s_len, d), BF16), jax.ShapeDtypeStruct((s_len, ng), F32),
                   jax.ShapeDtypeStruct((ng, s_len), F32)],
        compiler_params=_seq(),
    )(u, u, conv_w, conv_b, wqkv_b, wif_b, wift_b, b_if, b_ift)


def _chunk_gates(gt, gtt, h, heads, tril, triu):
    li_c = gt[:, h:h + 1]
    li_r = gtt[h:h + 1, :]
    gf_c = gt[:, heads + h:heads + h + 1]
    lf_c = _log_sigmoid(gf_c)
    lf_r = _log_sigmoid(gtt[heads + h:heads + h + 1, :])
    b_c = _tri_dot_left(tril, lf_c)
    b_r = _tri_dot_right(lf_r, triu)
    return li_c, li_r, gf_c, b_c, b_r


def _chunk_weights(li_c, li_r, b_c, b_r, m_prev, causal):
    lc = b_c.shape[0]
    b_last = b_c[lc - 1:lc, :]
    dmat = jnp.where(causal, b_c - b_r + li_r, -jnp.inf)
    m_inter = b_c + m_prev
    m_t = jnp.maximum(m_inter, jnp.max(dmat, axis=1, keepdims=True))
    w_intra = jnp.exp(dmat - m_t)
    w_inter = jnp.exp(m_inter - m_t)
    g_c = b_last - b_c + li_c
    m_new = jnp.maximum(b_last + m_prev, jnp.max(g_c, axis=0, keepdims=True))
    w_state = jnp.exp(g_c - m_new)
    decay = jnp.exp(b_last + m_prev - m_new)
    return m_t, w_intra, w_inter, m_new, w_state, decay


def _tri_masks(lc):
    r = lax.broadcasted_iota(jnp.int32, (lc, lc), 0)
    c = lax.broadcasted_iota(jnp.int32, (lc, lc), 1)
    causal = r >= c
    return causal, causal.astype(BF16), (r <= c).astype(BF16)


def _mlstm_fwd(qkv, gt, gtt, u, ml_g, ycat):
    _, s_len, d = qkv.shape
    ng = gt.shape[1]
    heads = ng // 2
    hd = d // heads
    lc = ML_CHUNK
    nc = s_len // lc
    kscale = hd ** -0.5

    def body(qkv_ref, gt_ref, gtt_ref, o_ref, z_ref, g_ref, _, cell_ref, y_ref, cst_ref, nst_ref, mst_ref, cs, ns, ms):
        @pl.when(pl.program_id(0) == 0)
        def _():
            cs[...] = jnp.zeros_like(cs)
            ns[...] = jnp.zeros_like(ns)
            ms[...] = jnp.zeros_like(ms)

        causal, tril, triu = _tri_masks(lc)
        gtv, gttv = gt_ref[...], gtt_ref[...]
        for h in range(heads):
            hs = slice(h * hd, (h + 1) * hd)
            li_c, li_r, _, b_c, b_r = _chunk_gates(gtv, gttv, h, heads, tril, triu)
            m_prev = ms[h][:, 0:1]
            m_t, w_intra, w_inter, m_new, w_state, decay = _chunk_weights(li_c, li_r, b_c, b_r, m_prev, causal)
            qb = qkv_ref[0, :, hs]
            ks = qkv_ref[1, :, hs].astype(F32) * kscale
            kb = _bf(ks)
            vb = qkv_ref[2, :, hs]
            c_old = cs[h]
            n_old = ns[h]
            cst_ref[0, h] = _bf(c_old)
            nst_ref[0, h] = n_old
            mst_ref[0, h] = ms[h]
            s = _dot_nt(qb, kb) * w_intra
            num = _dot(_bf(s), vb) + w_inter * _dot(qb, _bf(c_old))
            den = _rowsum(s) + w_inter * _rowsum(qb.astype(F32) * n_old)
            cell = num / jnp.maximum(jnp.abs(den), jnp.exp(-m_t))
            kw = ks * w_state
            cs[h] = decay * c_old + _dot_tn(_bf(kw), vb)
            ns[h] = decay * n_old + _colsum(kw)
            ms[h] = jnp.broadcast_to(m_new, ms[h].shape)
            cell_ref[:, hs] = cell
            hm = _sigmoid(o_ref[:, hs]) * cell
            hn = hm * lax.rsqrt(jnp.mean(hm * hm, axis=-1, keepdims=True) + EPS)
            z = z_ref[:, hs]
            y_ref[0, :, hs] = _bf((hn * g_ref[:, hs]) * (z * _sigmoid(z)))

    row = pl.BlockSpec((lc, d), lambda c: (c, 0))
    return _pcall(
        body, name="mlstm_fwd", grid=(nc,),
        in_specs=[pl.BlockSpec((3, lc, d), lambda c: (0, c, 0)), pl.BlockSpec((lc, ng), lambda c: (c, 0)),
                  pl.BlockSpec((ng, lc), lambda c: (0, c)),
                  pl.BlockSpec((lc, d), lambda c: (c, 3)), pl.BlockSpec((lc, d), lambda c: (c, 4)),
                  pl.BlockSpec((1, d), lambda c: (0, 0)), pl.BlockSpec(memory_space=pl.ANY)],
        out_specs=[row, pl.BlockSpec((1, lc, d), lambda c: (1, c, 0)),
                   pl.BlockSpec((1, heads, hd, hd), lambda c: (c, 0, 0, 0)),
                   pl.BlockSpec((1, heads, 1, hd), lambda c: (c, 0, 0, 0)),
                   pl.BlockSpec((1, heads, 1, 128), lambda c: (c, 0, 0, 0))],
        out_shape=[jax.ShapeDtypeStruct((s_len, d), F32), jax.ShapeDtypeStruct(ycat.shape, BF16),
                   jax.ShapeDtypeStruct((nc, heads, hd, hd), BF16),
                   jax.ShapeDtypeStruct((nc, heads, 1, hd), F32),
                   jax.ShapeDtypeStruct((nc, heads, 1, 128), F32)],
        scratch_shapes=[pltpu.VMEM((heads, hd, hd), F32), pltpu.VMEM((heads, 1, hd), F32),
                        pltpu.VMEM((heads, 1, 128), F32)],
        input_output_aliases={6: 1},
        compiler_params=_seq(),
    )(qkv, gt, gtt, u, u, ml_g, ycat)


def _out_proj(ycat, w_out_b, x, gate):
    s_len, d = x.shape
    tm = _tile(s_len, 512)

    def body(a_ref, w_ref, x_ref, g_ref, y_ref, xn_ref):
        y = _dot(a_ref[0], w_ref[0:d, :]) + _dot(a_ref[1], w_ref[d:2 * d, :])
        y_ref[...] = y
        xn_ref[...] = x_ref[...] + g_ref[...] * y

    row = pl.BlockSpec((tm, d), lambda i: (i, 0))
    return _pcall(
        body, name="out_proj", grid=(s_len // tm,),
        in_specs=[pl.BlockSpec((2, tm, d), lambda i: (0, i, 0)), pl.BlockSpec((2 * d, d), lambda i: (0, 0)), row,
                  pl.BlockSpec((1, d), lambda i: (0, 0))],
        out_specs=[row, row],
        out_shape=[jax.ShapeDtypeStruct((s_len, d), F32)] * 2,
        compiler_params=_seq(),
    )(ycat, w_out_b, x, gate)


def _final_loss(x, g, target):
    s_len, d = x.shape
    tm = _tile(s_len, 256)

    def body(x_ref, g_ref, t_ref, dx_ref, dg_ref, loss_ref):
        @pl.when(pl.program_id(0) == 0)
        def _():
            dg_ref[...] = jnp.zeros_like(dg_ref)
            loss_ref[...] = jnp.zeros_like(loss_ref)

        xv = x_ref[...]
        r = lax.rsqrt(jnp.mean(xv * xv, axis=-1, keepdims=True) + EPS)
        xn = xv * r
        err = xn * g_ref[...] - t_ref[...]
        loss_ref[...] += 0.5 * jnp.sum(jnp.mean(err * err, axis=-1, keepdims=True))
        dout = err * (1.0 / d)
        dg_ref[...] += _colsum(dout * xn)
        dxn = dout * g_ref[...]
        dx_ref[...] = r * (dxn - xn * jnp.mean(dxn * xn, axis=-1, keepdims=True))

    row = pl.BlockSpec((tm, d), lambda i: (i, 0))
    vec = pl.BlockSpec((1, d), lambda i: (0, 0))
    return _pcall(
        body, name="final_loss", grid=(s_len // tm,),
        in_specs=[row, vec, row],
        out_specs=[row, vec, pl.BlockSpec((1, 128), lambda i: (0, 0))],
        out_shape=[jax.ShapeDtypeStruct((s_len, d), F32), jax.ShapeDtypeStruct((1, d), F32),
                   jax.ShapeDtypeStruct((1, 128), F32)],
        compiler_params=_seq(),
    )(x, g, target)


def _out_bwd(dxn, y, gate, w_out_b):
    s_len, d = dxn.shape
    tm = _tile(s_len, 512)

    def body(dx_ref, y_ref, g_ref, w_ref, dg_ref, dy_ref, dc_ref):
        @pl.when(pl.program_id(0) == 0)
        def _():
            dg_ref[...] = jnp.zeros_like(dg_ref)

        dx = dx_ref[...]
        dg_ref[...] += _colsum(dx * y_ref[...])
        dy = _bf(g_ref[...] * dx)
        dy_ref[...] = dy
        dc_ref[0] = _dot_nt(dy, w_ref[0:d, :])
        dc_ref[1] = _dot_nt(dy, w_ref[d:2 * d, :])

    row = pl.BlockSpec((tm, d), lambda i: (i, 0))
    vec = pl.BlockSpec((1, d), lambda i: (0, 0))
    return _pcall(
        body, name="out_bwd", grid=(s_len // tm,),
        in_specs=[row, row, vec, pl.BlockSpec((2 * d, d), lambda i: (0, 0))],
        out_specs=[vec, row, pl.BlockSpec((2, tm, d), lambda i: (0, i, 0))],
        out_shape=[jax.ShapeDtypeStruct((1, d), F32), jax.ShapeDtypeStruct((s_len, d), BF16),
                   jax.ShapeDtypeStruct((2, s_len, d), F32)],
        compiler_params=_seq(),
    )(dxn, y, gate, w_out_b)


def _grad_matmul(a3, b3, nblk, a_idx, b_idx, out_shape, out_block, out_idx, layer, stack):
    _, s_len, m = a3.shape
    n = b3.shape[2]
    tk = _tile(s_len, 512)
    first = isinstance(stack, int)

    def body(a_ref, b_ref, *rest):
        o_ref = rest[-1]

        @pl.when(pl.program_id(1) == 0)
        def _():
            o_ref[...] = jnp.zeros_like(o_ref)

        o_ref[...] += _dot_tn(a_ref[0], b_ref[0])

    in_specs = [pl.BlockSpec((1, tk, m), lambda p, t: (a_idx(p), t, 0)),
                pl.BlockSpec((1, tk, n), lambda p, t: (b_idx(p), t, 0))]
    return _pcall(
        body, name="grad_matmul", grid=(nblk, s_len // tk),
        in_specs=in_specs if first else in_specs + [pl.BlockSpec(memory_space=pl.ANY)],
        out_specs=pl.BlockSpec((None,) + out_block, lambda p, t: (layer,) + out_idx(p)),
        out_shape=jax.ShapeDtypeStruct(((stack,) if first else stack.shape[:1]) + out_shape, F32),
        input_output_aliases={} if first else {2: 0},
        compiler_params=_seq(2),
    )(*((a3, b3) if first else (a3, b3, stack)))


DU_PLANE = (2, 3, 4, 0, 1)


def _mlstm_bwd(qkv, gt, gtt, cst, nst, mst, cell, u, ml_g, d_ycat, wif_b):
    _, s_len, d = qkv.shape
    ng = gt.shape[1]
    heads = ng // 2
    hd = d // heads
    lc = ML_CHUNK
    nc = s_len // lc
    kscale = hd ** -0.5

    def body(qkv_ref, gt_ref, gtt_ref, cst_ref, nst_ref, mst_ref, cell_ref, o_ref, z_ref, g_ref, dy_ref,
             wif_ref, dqkv_ref, dgt_ref, dbif_ref, du_ref, dg_ref, dcs, dns, dqs, dks, dvs):
        @pl.when(pl.program_id(0) == 0)
        def _():
            dbif_ref[...] = jnp.zeros_like(dbif_ref)
            dcs[...] = jnp.zeros_like(dcs)
            dns[...] = jnp.zeros_like(dns)
            dg_ref[...] = jnp.zeros_like(dg_ref)

        causal, tril, triu = _tri_masks(lc)
        tril_strict = (tril.astype(F32) - (tril * triu).astype(F32)).astype(BF16)
        gtv, gttv = gt_ref[...], gtt_ref[...]
        lane = lax.broadcasted_iota(jnp.int32, (lc, ng), 1)
        dgt = jnp.zeros((lc, ng), F32)
        for h in range(heads):
            hs = slice(h * hd, (h + 1) * hd)
            li_c, li_r, gf_c, b_c, b_r = _chunk_gates(gtv, gttv, h, heads, tril, triu)
            m_prev = mst_ref[0, h][:, 0:1]
            m_t, w_intra, w_inter, _, w_state, decay = _chunk_weights(li_c, li_r, b_c, b_r, m_prev, causal)
            qb = qkv_ref[0, :, hs]
            qf = qb.astype(F32)
            ks = qkv_ref[1, :, hs].astype(F32) * kscale
            kb = _bf(ks)
            vb = qkv_ref[2, :, hs]
            c_b = cst_ref[0, h]
            n_old = nst_ref[0, h]
            s = _dot_nt(qb, kb) * w_intra
            den = _rowsum(s) + w_inter * _rowsum(qf * n_old)
            floor = jnp.exp(-m_t)
            dstab = jnp.maximum(jnp.abs(den), floor)
            cell = cell_ref[:, hs]
            o = o_ref[:, hs]
            so = _sigmoid(o)
            hm = so * cell
            rinv = lax.rsqrt(jnp.mean(hm * hm, axis=-1, keepdims=True) + EPS)
            hn = hm * rinv
            z = z_ref[:, hs]
            sgz = _sigmoid(z)
            sz = z * sgz
            gh = g_ref[:, hs]
            dy = dy_ref[0, :, hs]
            du_ref[1, :, hs] = _bf(dy * (hn * gh) * _dsilu(z, sgz))
            dg_ref[:, hs] += _colsum(dy * hn * sz)
            dhn = dy * gh * sz
            dhm = rinv * (dhn - hn * jnp.mean(dhn * hn, axis=-1, keepdims=True))
            du_ref[0, :, hs] = _bf(dhm * cell * so * (1.0 - so))
            dcell = dhm * so
            dnum = dcell / dstab
            dnb = _bf(dnum)
            dden = -_rowsum(dcell * cell) / dstab * jnp.where(jnp.abs(den) > floor, jnp.where(den > 0.0, 1.0, -1.0), 0.0)
            dst = _dot_nt(dnb, vb) + dden
            dsdb = _bf(dst * w_intra)
            dc_out = dcs[h]
            dn_out = dns[h]
            dcb = _bf(dc_out)
            dq_inter = w_inter * (_dot_nt(dnb, c_b) + dden * n_old)
            dk_inter = w_state * (_dot_nt(vb, dcb) + dn_out)
            dq = _dot(dsdb, kb) + dq_inter
            dk = _dot_tn(dsdb, qb) + dk_inter
            dv = _dot_tn(_bf(s), dnb) + _dot(_bf(ks * w_state), dcb)
            wq = w_inter * qf
            dcs[h] = decay * dc_out + _dot_tn(_bf(wq), dnb)
            dns[h] = decay * dn_out + _colsum(wq * dden)
            pmat = dst * s
            p_rows = _rowsum(pmat)
            p_cols = _rowsum(pmat.T)
            q_in = _rowsum(qf * dq_inter)
            k_in = _rowsum(ks * dk_inter)
            across = decay * (jnp.sum(dc_out * c_b.astype(F32), keepdims=True) + jnp.sum(dn_out * n_old, keepdims=True))
            dli = p_cols + k_in
            dlf = _tri_dot_left(triu, p_rows - p_cols + q_in) + _tri_dot_left(tril_strict, k_in) + across
            dgf = dlf * _sigmoid(-gf_c)
            dgt = dgt + jnp.where(lane == h, dli, 0.0) + jnp.where(lane == heads + h, dgf, 0.0)
            dqs[:, hs] = dq
            dks[:, hs] = dk * kscale
            dvs[:, hs] = dv
        dgt_ref[...] = dgt
        dbif_ref[...] += _colsum(dgt)
        dgb = _bf(dgt)
        dqkv_ref[0] = _bf(dqs[...] + _dot_nt(dgb, wif_ref[0:d, :]))
        dqkv_ref[1] = _bf(dks[...] + _dot_nt(dgb, wif_ref[d:2 * d, :]))
        dqkv_ref[2] = _bf(dvs[...] + _dot_nt(dgb, wif_ref[2 * d:3 * d, :]))

    rev = lambda c: nc - 1 - c
    row = pl.BlockSpec((lc, d), lambda c: (rev(c), 0))
    return _pcall(
        body, name="mlstm_bwd", grid=(nc,),
        in_specs=[pl.BlockSpec((3, lc, d), lambda c: (0, rev(c), 0)), pl.BlockSpec((lc, ng), lambda c: (rev(c), 0)),
                  pl.BlockSpec((ng, lc), lambda c: (0, rev(c))),
                  pl.BlockSpec((1, heads, hd, hd), lambda c: (rev(c), 0, 0, 0)),
                  pl.BlockSpec((1, heads, 1, hd), lambda c: (rev(c), 0, 0, 0)),
                  pl.BlockSpec((1, heads, 1, 128), lambda c: (rev(c), 0, 0, 0)),
                  row, pl.BlockSpec((lc, d), lambda c: (rev(c), 3)), pl.BlockSpec((lc, d), lambda c: (rev(c), 4)),
                  pl.BlockSpec((1, d), lambda c: (0, 0)), pl.BlockSpec((1, lc, d), lambda c: (1, rev(c), 0)),
                  pl.BlockSpec((3 * d, ng), lambda c: (0, 0))],
        out_specs=[pl.BlockSpec((3, lc, d), lambda c: (0, rev(c), 0)), pl.BlockSpec((lc, ng), lambda c: (rev(c), 0)),
                   pl.BlockSpec((1, ng), lambda c: (0, 0)), pl.BlockSpec((2, lc, d), lambda c: (0, rev(c), 0)),
                   pl.BlockSpec((1, d), lambda c: (0, 0))],
        out_shape=[jax.ShapeDtypeStruct((3, s_len, d), BF16), jax.ShapeDtypeStruct((s_len, ng), F32),
                   jax.ShapeDtypeStruct((1, ng), F32), jax.ShapeDtypeStruct((5, s_len, d), BF16),
                   jax.ShapeDtypeStruct((1, d), F32)],
        scratch_shapes=[pltpu.VMEM((heads, hd, hd), F32), pltpu.VMEM((heads, 1, hd), F32)]
        + [pltpu.VMEM((lc, d), F32)] * 3,
        compiler_params=_seq(),
    )(qkv, gt, gtt, cst, nst, mst, cell, u, u, ml_g, d_ycat, wif_b)


def _ml_pre_bwd(dqkv, u, conv_w, conv_b, wqkv_b):
    s_len = u.shape[0]
    d = conv_w.shape[1]
    _, heads, hd, _ = wqkv_b.shape
    tm = _tile(s_len, 256)
    per = tm // HALO

    def body(dqkv_ref, x_ref, xp_ref, cw_ref, cb_ref, w_ref, dpre_ref, dx_ref, gw_ref):
        i = pl.program_id(0)

        @pl.when(i == 0)
        def _():
            gw_ref[...] = jnp.zeros_like(gw_ref)

        prev = jnp.where(i == 0, 0.0, xp_ref[...])
        xm = x_ref[...]
        pre = _conv_fwd(jnp.concatenate([prev, xm], axis=0), cw_ref, cb_ref)
        sg = _sigmoid(pre)
        xcb = _bf(pre * sg)
        xmb = _bf(xm)
        for h in range(heads):
            hs = slice(h * hd, (h + 1) * hd)
            dqh, dkh, dvh = dqkv_ref[0, :, hs], dqkv_ref[1, :, hs], dqkv_ref[2, :, hs]
            dxc = _dot_nt(dqh, w_ref[0, h]) + _dot_nt(dkh, w_ref[1, h])
            dpre_ref[:, hs] = dxc * _dsilu(pre[:, hs], sg[:, hs])
            dx_ref[:, hs] = _dot_nt(dvh, w_ref[2, h])
            gw_ref[0, h] += _dot_tn(xcb[:, hs], dqh)
            gw_ref[1, h] += _dot_tn(xcb[:, hs], dkh)
            gw_ref[2, h] += _dot_tn(xmb[:, hs], dvh)

    row = pl.BlockSpec((tm, d), lambda i: (i, 0))
    vec = pl.BlockSpec((1, d), lambda i: (0, 0))
    whole4 = pl.BlockSpec(wqkv_b.shape, lambda i: (0, 0, 0, 0))
    return _pcall(
        body, name="ml_pre_bwd", grid=(s_len // tm,),
        in_specs=[pl.BlockSpec((3, tm, d), lambda i: (0, i, 0)), pl.BlockSpec((tm, d), lambda i: (i, 2)),
                  pl.BlockSpec((HALO, d), lambda i: (jnp.maximum(i * per - 1, 0), 2)),
                  pl.BlockSpec((CONV_WIDTH, d), lambda i: (0, 0)), vec, whole4],
        out_specs=[row, row, whole4],
        out_shape=[jax.ShapeDtypeStruct((s_len, d), F32)] * 2 + [jax.ShapeDtypeStruct(wqkv_b.shape, F32)],
        compiler_params=_seq(),
    )(dqkv, u, u, conv_w, conv_b, wqkv_b)


def _conv_bwd(dpre, u, col, conv_w, du, plane, direct=None):
    s_len, d = dpre.shape
    tm = _tile(s_len, 256)
    per = tm // HALO
    nt = s_len // tm

    def body(*refs):
        if direct is None:
            dp_ref, dn_ref, x_ref, xp_ref, cw_ref, _, dx_ref, gw_ref, gb_ref = refs
        else:
            dp_ref, dn_ref, x_ref, xp_ref, cw_ref, _, dir_ref, dx_ref, gw_ref, gb_ref = refs
        i = pl.program_id(0)

        @pl.when(i == 0)
        def _():
            gw_ref[...] = jnp.zeros_like(gw_ref)
            gb_ref[...] = jnp.zeros_like(gb_ref)

        dp = dp_ref[...]
        nxt = jnp.where(i == nt - 1, 0.0, dn_ref[...])
        dwin = jnp.concatenate([dp, nxt], axis=0)
        prev = jnp.where(i == 0, 0.0, xp_ref[...])
        xwin = jnp.concatenate([prev, x_ref[...]], axis=0)
        acc = cw_ref[CONV_WIDTH - 1:CONV_WIDTH, :] * dp
        if direct is not None:
            acc = acc + dir_ref[...]
        gw_ref[CONV_WIDTH - 1:CONV_WIDTH, :] += _colsum(dp * xwin[HALO:])
        for k in range(CONV_WIDTH - 1):
            sft = CONV_WIDTH - 1 - k
            acc = acc + cw_ref[k:k + 1, :] * _shift_up(dwin, sft)[0:tm]
            gw_ref[k:k + 1, :] += _colsum(dp * _shift_down(xwin, sft)[HALO:])
        gb_ref[...] += _colsum(dp)
        dx_ref[0] = _bf(acc)

    row = pl.BlockSpec((tm, d), lambda i: (i, 0))
    in_specs = [row, pl.BlockSpec((HALO, d), lambda i: (jnp.minimum((i + 1) * per, s_len // HALO - 1), 0)),
                pl.BlockSpec((tm, d), lambda i: (i, col)),
                pl.BlockSpec((HALO, d), lambda i: (jnp.maximum(i * per - 1, 0), col)),
                pl.BlockSpec((CONV_WIDTH, d), lambda i: (0, 0)), pl.BlockSpec(memory_space=pl.ANY)]
    args = [dpre, dpre, u, u, conv_w, du]
    if direct is not None:
        in_specs.append(row)
        args.append(direct)
    return _pcall(
        body, name="conv_bwd", grid=(nt,),
        in_specs=in_specs,
        out_specs=[pl.BlockSpec((1, tm, d), lambda i: (plane, i, 0)), pl.BlockSpec((CONV_WIDTH, d), lambda i: (0, 0)),
                   pl.BlockSpec((1, d), lambda i: (0, 0))],
        out_shape=[jax.ShapeDtypeStruct(du.shape, BF16), jax.ShapeDtypeStruct((CONV_WIDTH, d), F32),
                   jax.ShapeDtypeStruct((1, d), F32)],
        input_output_aliases={5: 0},
        compiler_params=_seq(),
    )(*args)


def _rg_bwd(d_ycat, u, hh, conv_w, conv_b, wa_b, ba, wx_b, bx, lam, du):
    s_len = u.shape[0]
    d = conv_w.shape[1]
    heads, hd, _ = wa_b.shape
    tm = _tile(s_len, 256)
    per = tm // HALO
    nt = s_len // tm

    def body(dy_ref, x_ref, xp_ref, z_ref, hh_ref, hp_ref, cw_ref, cb_ref, wa_ref, ba_ref, wx_ref, bx_ref, lam_ref, _,
             dxc_ref, dz_ref, gwa_ref, gwx_ref, gba_ref, gbx_ref, glam_ref, carry, gbuf):
        i = pl.program_id(0)
        first = i == nt - 1

        @pl.when(i == 0)
        def _():
            carry[...] = jnp.zeros_like(carry)
            gwa_ref[...] = jnp.zeros_like(gwa_ref)
            gwx_ref[...] = jnp.zeros_like(gwx_ref)
            gba_ref[...] = jnp.zeros_like(gba_ref)
            gbx_ref[...] = jnp.zeros_like(gbx_ref)
            glam_ref[...] = jnp.zeros_like(glam_ref)

        prev = jnp.where(first, 0.0, xp_ref[...])
        xc = _conv_fwd(jnp.concatenate([prev, x_ref[...]], axis=0), cw_ref, cb_ref)
        r, ig, sp, log_a, a, mult = _rg_gates(xc, wa_ref, ba_ref, wx_ref, bx_ref, lam_ref)
        z = z_ref[...]
        sgz = _sigmoid(z)
        dy = dy_ref[0]
        hh_v = hh_ref[...]
        dz_ref[0] = _bf(dy * hh_v * _dsilu(z, sgz))
        dhh = dy * (z * sgz)
        rows = lax.broadcasted_iota(jnp.int32, a.shape, 0)
        coef = jnp.where(rows == tm - 1, carry[1:2, :], _shift_up(a, 1))
        ca, cu = _scan_groups(coef, dhh, reverse=True)
        c = carry[0:1, :]
        for j in range(per - 1, -1, -1):
            blk = ca[j * 8:(j + 1) * 8] * c + cu[j * 8:(j + 1) * 8]
            gbuf[j * 8:(j + 1) * 8, :] = blk
            c = blk[0:1]
        carry[0:1, :] = c
        carry[1:2, :] = a[0:1]
        g = gbuf[...]
        hprev_tile = jnp.where(first, 0.0, hp_ref[...])
        hprev = _shift_down(jnp.concatenate([hprev_tile, hh_v], axis=0), 1)[HALO:]
        da = g * hprev
        gx_ = g * xc
        d_mult = gx_ * ig
        d_ig = gx_ * mult
        dxc = g * mult * ig
        a2 = jnp.exp(2.0 * log_a)
        dlog_a = da * a - d_mult * (a2 / mult)
        d_r = dlog_a * ((-RG_C) * sp)
        glam_ref[...] += _colsum(dlog_a * ((-RG_C) * r)) * (-_sigmoid(-lam_ref[...]))
        d_ga = d_r * r * (1.0 - r)
        d_gx = d_ig * ig * (1.0 - ig)
        gba_ref[...] += _colsum(d_ga)
        gbx_ref[...] += _colsum(d_gx)
        xb = _bf(xc)
        dgab = _bf(d_ga)
        dgxb = _bf(d_gx)
        for h in range(heads):
            hs = slice(h * hd, (h + 1) * hd)
            dxc_ref[:, hs] = dxc[:, hs] + _dot_nt(dgab[:, hs], wa_ref[h]) + _dot_nt(dgxb[:, hs], wx_ref[h])
            gwa_ref[h] += _dot_tn(xb[:, hs], dgab[:, hs])
            gwx_ref[h] += _dot_tn(xb[:, hs], dgxb[:, hs])

    rev = lambda i: nt - 1 - i
    row = pl.BlockSpec((tm, d), lambda i: (rev(i), 0))
    halo_prev = lambda col: pl.BlockSpec((HALO, d), lambda i: (jnp.maximum(rev(i) * per - 1, 0), col))
    vec = pl.BlockSpec((1, d), lambda i: (0, 0))
    whole3 = lambda a: pl.BlockSpec(a.shape, lambda i: (0, 0, 0))
    return _pcall(
        body, name="rg_bwd", grid=(nt,),
        in_specs=[pl.BlockSpec((1, tm, d), lambda i: (0, rev(i), 0)), row, halo_prev(0),
                  pl.BlockSpec((tm, d), lambda i: (rev(i), 1)), row, halo_prev(0),
                  pl.BlockSpec((CONV_WIDTH, d), lambda i: (0, 0)), vec, whole3(wa_b), vec, whole3(wx_b), vec, vec,
                  pl.BlockSpec(memory_space=pl.ANY)],
        out_specs=[row, pl.BlockSpec((1, tm, d), lambda i: (DU_PLANE[1], rev(i), 0)), whole3(wa_b), whole3(wa_b),
                   vec, vec, vec],
        out_shape=[jax.ShapeDtypeStruct((s_len, d), F32), jax.ShapeDtypeStruct(du.shape, BF16),
                   jax.ShapeDtypeStruct(wa_b.shape, F32), jax.ShapeDtypeStruct(wa_b.shape, F32)]
        + [jax.ShapeDtypeStruct((1, d), F32)] * 3,
        scratch_shapes=[pltpu.VMEM((8, d), F32), pltpu.VMEM((tm, d), F32)],
        input_output_aliases={13: 1},
        compiler_params=_seq(),
    )(d_ycat, u, u, u, hh, hh, conv_w, conv_b, wa_b, ba, wx_b, bx, lam, du)


def _in_bwd(du, w4, x, dxn, g, scale):
    s_len, d = x.shape
    tm = _tile(s_len, 256)
    nsh_chips, _, nsh = w4.shape
    npc = du.shape[0]
    ck = d // 4
    assert nsh % ck == 0 and npc * d == nsh_chips * nsh

    def body(du_ref, w_ref, x_ref, dxn_ref, g_ref, sc_ref, dx_ref, dsh_ref, dsc_ref, dg_ref):
        @pl.when(pl.program_id(0) == 0)
        def _():
            dsh_ref[...] = jnp.zeros_like(dsh_ref)
            dsc_ref[...] = jnp.zeros_like(dsc_ref)
            dg_ref[...] = jnp.zeros_like(dg_ref)

        dh = None
        for q in range(npc * d // ck):
            col = q * ck
            p, pc = col // d, col % d
            s, sc = col // nsh, col % nsh
            t = _dot_nt(du_ref[DU_PLANE[p], :, pc:pc + ck], w_ref[s, :, sc:sc + ck])
            dh = t if dh is None else dh + t
        xv = x_ref[...]
        r = lax.rsqrt(jnp.mean(xv * xv, axis=-1, keepdims=True) + EPS)
        xn = xv * r
        gv = g_ref[...]
        onesc = 1.0 + sc_ref[...]
        dsh_ref[...] += _colsum(dh)
        dsc_ref[...] += _colsum(dh * (xn * gv))
        dg_ref[...] += _colsum(dh * xn * onesc)
        dxh = dh * (gv * onesc)
        dx_ref[...] = dxn_ref[...] + r * (dxh - xn * jnp.mean(dxh * xn, axis=-1, keepdims=True))

    row = pl.BlockSpec((tm, d), lambda i: (i, 0))
    vec = pl.BlockSpec((1, d), lambda i: (0, 0))
    return _pcall(
        body, name="in_bwd", grid=(s_len // tm,),
        in_specs=[pl.BlockSpec((npc, tm, d), lambda i: (0, i, 0)), pl.BlockSpec(w4.shape, lambda i: (0, 0, 0)), row, row,
                  vec, vec],
        out_specs=[row, vec, vec, vec],
        out_shape=[jax.ShapeDtypeStruct((s_len, d), F32)] + [jax.ShapeDtypeStruct((1, d), F32)] * 3,
        compiler_params=_seq(),
    )(du, w4, x, dxn, g, scale)


def _layer_fwd(x, p):
    h_b, u = _ln_inproj(x, p["norm_g"], p["scale"], p["shift"], p["w4"])
    hh, ycat = _rg_fwd(u, p["rg_conv_w"], p["rg_conv_b"], p["rg_wa_b"], p["rg_ba"], p["rg_wx_b"], p["rg_bx"],
                       p["rg_lam"])
    qkv, gt, gtt = _ml_pre(u, p["ml_conv_w"], p["ml_conv_b"], p["wqkv_b"], p["wif_b"], p["wift_b"], p["b_if"],
                           p["b_ift"])
    cell, ycat, cst, nst, mst = _mlstm_fwd(qkv, gt, gtt, u, p["ml_g"], ycat)
    y, x_new = _out_proj(ycat, p["w_out_b"], x, p["gate"])
    saved = dict(x=x, h_b=h_b, u=u, hh=hh, qkv=qkv, gt=gt, gtt=gtt, cell=cell, ycat=ycat, cst=cst, nst=nst, mst=mst,
                 y=y)
    return x_new, saved


def _layer_bwd(dxn, p, s, layer, stacks):
    u = s["u"]
    d = dxn.shape[1]
    d_gate, dy_b, d_ycat = _out_bwd(dxn, s["y"], p["gate"], p["w_out_b"])
    stacks["w_out"] = _grad_matmul(s["ycat"], dy_b[None], 2, lambda b: b, lambda b: 0, (2 * d, d), (d, d),
                                   lambda b: (b, 0), layer, stacks["w_out"])
    dqkv, dgt, g_b_if, du, g_ml_g = _mlstm_bwd(s["qkv"], s["gt"], s["gtt"], s["cst"], s["nst"], s["mst"], s["cell"], u,
                                               p["ml_g"], d_ycat, p["wif_b"])
    ng = dgt.shape[1]
    g_w_if = _grad_matmul(s["qkv"], _bf(dgt)[None], 3, lambda b: b, lambda b: 0, (3 * d, ng), (d, ng),
                          lambda b: (b, 0), 0, 1)[0]
    dpre_ml, dmlx_direct, g_wqkv = _ml_pre_bwd(dqkv, u, p["ml_conv_w"], p["ml_conv_b"], p["wqkv_b"])
    du, g_ml_cw, g_ml_cb = _conv_bwd(dpre_ml, u, 2, p["ml_conv_w"], du, DU_PLANE[2], direct=dmlx_direct)
    dxc_rg, du, g_wa, g_wx, g_ba, g_bx, g_lam = _rg_bwd(d_ycat, u, s["hh"], p["rg_conv_w"], p["rg_conv_b"],
                                                        p["rg_wa_b"], p["rg_ba"], p["rg_wx_b"], p["rg_bx"],
                                                        p["rg_lam"], du)
    du, g_rg_cw, g_rg_cb = _conv_bwd(dxc_rg, u, 0, p["rg_conv_w"], du, DU_PLANE[0])
    npc = du.shape[0]
    stacks["w_in"] = _grad_matmul(s["h_b"][None], du, npc, lambda b: 0, lambda b: (b + DU_PLANE[0]) % npc,
                                  (d, npc * d), (d, d), lambda b: (0, b), layer, stacks["w_in"])
    dx, d_shift, d_scale, g_norm_g = _in_bwd(du, p["w4"], s["x"], dxn, p["norm_g"], p["scale"])
    grads = dict(norm_g=g_norm_g, rg_conv_w=g_rg_cw, rg_conv_b=g_rg_cb, rg_w_a=g_wa, rg_b_a=g_ba,
                 rg_w_x=g_wx, rg_b_x=g_bx, rg_lambda=g_lam, ml_conv_w=g_ml_cw, ml_conv_b=g_ml_cb, ml_w_qkv=g_wqkv,
                 ml_w_if=g_w_if, ml_b_if=g_b_if, ml_norm_g=g_ml_g)
    return dx, grads, jnp.concatenate([d_shift, d_scale, d_gate], axis=1)


def _trunk_fwd_bwd(x, target, final_g, layers):
    saved = []
    for p in layers:
        x, s = _layer_fwd(x, p)
        saved.append(s)
    dx, g_final, loss = _final_loss(x, final_g, target)
    grads, dmods = [], []
    stacks = dict(w_in=len(layers), w_out=len(layers))
    for layer in reversed(range(len(layers))):
        dx, g, dm = _layer_bwd(dx, layers[layer], saved[layer], layer, stacks)
        grads.append(g)
        dmods.append(dm)
    return loss, dx, g_final, grads[::-1], stacks["w_in"], stacks["w_out"], dmods[::-1]


def _me():
    return lax.axis_index("x"), lax.axis_index("y"), lax.axis_index("c")


def _remote(src, dst, send_sem, recv_sem, to):
    return pltpu.make_async_remote_copy(src_ref=src, dst_ref=dst, send_sem=send_sem, recv_sem=recv_sem,
                                        device_id=to, device_id_type=MESH)


def _all_gather8(blocks, space):
    n = len(blocks)

    def body(*refs):
        x_refs, out_refs = refs[:n], refs[n:2 * n]
        send_sems, recv_sems, local_sems = refs[2 * n:]
        x, y, c = _me()
        me, sibling = (x, y, c), (x, y, 1 - c)
        chips = [(1 - x, y), (x, 1 - y), (1 - x, 1 - y)]

        def rows(i, px, py, pc):
            m_per = blocks[i].shape[0]
            return out_refs[i].at[pl.ds((4 * px + 2 * py + pc) * m_per, m_per), :]

        def copy(i, k, blk, to, src=None):
            return _remote(rows(i, *blk) if src is None else src, rows(i, *blk), send_sems.at[7 * i + k],
                           recv_sems.at[7 * i + k], to)

        mine = [pltpu.make_async_copy(x_refs[i], rows(i, *me), local_sems.at[i]) for i in range(n)]
        first = []
        for i in range(n):
            first.append(copy(i, 0, me, sibling, src=x_refs[i]))
            first += [copy(i, 1 + j, me, (*chip, c), src=x_refs[i]) for j, chip in enumerate(chips)]
        for cp in mine + first:
            cp.start()
        passed = []
        for j, chip in enumerate(chips):
            for i in range(n):
                copy(i, 1 + j, (*chip, c), me).wait_recv()
                passed.append(copy(i, 4 + j, (*chip, c), sibling))
                passed[-1].start()
        for i in range(n):
            copy(i, 0, sibling, me).wait_recv()
            for j, chip in enumerate(chips):
                copy(i, 4 + j, (*chip, 1 - c), me).wait_recv()
        for cp in first + passed:
            cp.wait_send()
        for cp in mine:
            cp.wait()

    spec = pl.BlockSpec(memory_space=space)
    return _pcall(
        body, name="all_gather8",
        out_shape=[jax.ShapeDtypeStruct((8 * b.shape[0], b.shape[1]), b.dtype) for b in blocks],
        in_specs=[spec] * n, out_specs=[spec] * n,
        scratch_shapes=[pltpu.SemaphoreType.DMA((7 * n,)), pltpu.SemaphoreType.DMA((7 * n,)),
                        pltpu.SemaphoreType.DMA((n,))],
    )(*blocks)


def _sib_halves(g_in, g_out, slabs):
    depth, d, n4 = g_in.shape
    n = n4 // 4
    ns = len(slabs)

    def body(*refs):
        gi, go = refs[0], refs[1]
        sl = refs[2:2 + ns]
        ri, ro = refs[2 + ns], refs[3 + ns]
        rs = refs[4 + ns:4 + 2 * ns]
        send_sems, recv_sems = refs[4 + 2 * ns:]
        x, y, c = _me()
        o = 1 - c
        pairs = [(gi.at[pl.ds(0, depth), pl.ds(o * (d // 2), d // 2), pl.ds(s * n, n)], ri.at[pl.ds(0, depth), s])
                 for s in range(4)]
        pairs.append((go.at[pl.ds(0, depth), pl.ds(0, 4), o], ro))
        pairs += [(sl[i].at[o], rs[i]) for i in range(ns)]
        copies = [_remote(src, dst, send_sems.at[k], recv_sems.at[k], (x, y, o)) for k, (src, dst) in enumerate(pairs)]
        for cp in copies:
            cp.start()
        for cp in copies:
            cp.wait_recv()
        for cp in copies:
            cp.wait_send()

    hbm = pl.BlockSpec(memory_space=pltpu.HBM)
    ncp = 5 + ns
    return _pcall(
        body, name="sib_halves",
        out_shape=[jax.ShapeDtypeStruct((depth, 4, d // 2, n), g_in.dtype),
                   jax.ShapeDtypeStruct(g_out.shape[:2] + g_out.shape[3:], g_out.dtype)]
        + [jax.ShapeDtypeStruct(s.shape[1:], s.dtype) for s in slabs],
        in_specs=[hbm] * (2 + ns), out_specs=[hbm] * (2 + ns),
        scratch_shapes=[pltpu.SemaphoreType.DMA((ncp,)), pltpu.SemaphoreType.DMA((ncp,))],
    )(g_in, g_out, *slabs)


def _sib_fill(boths):
    n = len(boths)

    def body(*refs):
        dst = refs[n:2 * n]
        send_sems, recv_sems = refs[2 * n:]
        x, y, c = _me()
        view = lambda i: dst[i].at[pl.ds(0, boths[i].shape[0]), c]
        copies = [_remote(view(i), view(i), send_sems.at[i], recv_sems.at[i], (x, y, 1 - c)) for i in range(n)]
        for cp in copies:
            cp.start()
        for cp in copies:
            cp.wait_recv()
        for cp in copies:
            cp.wait_send()

    hbm = pl.BlockSpec(memory_space=pltpu.HBM)
    return _pcall(
        body, name="sib_fill",
        out_shape=[jax.ShapeDtypeStruct(b.shape, b.dtype) for b in boths],
        in_specs=[hbm] * n, out_specs=[hbm] * n, input_output_aliases={i: i for i in range(n)},
        scratch_shapes=[pltpu.SemaphoreType.DMA((n,)), pltpu.SemaphoreType.DMA((n,))],
    )(*boths)


def _chip_exchange(arrs):
    n = len(arrs)

    def body(*refs):
        src, dst = refs[:n], refs[n:2 * n]
        send_sems, recv_sems = refs[2 * n:]
        x, y, c = _me()
        me_s = 2 * x + y
        chips = [(1 - x, y), (x, 1 - y), (1 - x, 1 - y)]
        copies = [_remote(src[i].at[2 * px + py], dst[i].at[me_s], send_sems.at[3 * i + k], recv_sems.at[3 * i + k],
                          (px, py, c))
                  for i in range(n) for k, (px, py) in enumerate(chips)]
        for cp in copies:
            cp.start()
        for cp in copies:
            cp.wait_recv()
        for cp in copies:
            cp.wait_send()

    hbm = pl.BlockSpec(memory_space=pltpu.HBM)
    return _pcall(
        body, name="chip_exchange",
        out_shape=[jax.ShapeDtypeStruct(a.shape, a.dtype) for a in arrs],
        in_specs=[hbm] * n, out_specs=[hbm] * n,
        scratch_shapes=[pltpu.SemaphoreType.DMA((3 * n,)), pltpu.SemaphoreType.DMA((3 * n,))],
    )(*arrs)


def _row_tile(rows, cap=4096, mult=16):
    best = None
    for t in range(mult, min(rows, cap) + 1, mult):
        if rows % t == 0:
            best = t
    return rows if best is None else best


def _pair_sum(half, own, own_spec, got, got_spec, out_shape, out_spec, grid):
    def body(_, a_ref, b_ref, o_ref):
        o_ref[...] = (a_ref[...] + b_ref[...].astype(F32)).astype(o_ref.dtype)

    return _pcall(
        body, name="pair_sum",
        grid_spec=pltpu.PrefetchScalarGridSpec(num_scalar_prefetch=1, grid=grid, in_specs=[own_spec, got_spec],
                                               out_specs=out_spec),
        out_shape=out_shape, compiler_params=_seq(len(grid)))(half, own, got)


def _chip_sum(ids, part, met, fill):
    _, depth, rows, n = part.shape
    tr = _row_tile(rows, cap=max(16, (1 << 18) // n))

    def body(_, own_ref, a_ref, b_ref, c_ref, o_ref):
        acc = own_ref[...].astype(F32) + a_ref[...].astype(F32)
        acc = acc + b_ref[...].astype(F32)
        o_ref[...] = acc + c_ref[...].astype(F32)

    blk = (None, None, tr, n)
    other = lambda k: pl.BlockSpec(blk, lambda l, j, ids: ((ids[0] + k) % 4, l, j, 0))
    return _pcall(
        body, name="chip_sum",
        grid_spec=pltpu.PrefetchScalarGridSpec(
            num_scalar_prefetch=1, grid=(depth, rows // tr),
            in_specs=[pl.BlockSpec(blk, lambda l, j, ids: (ids[0], l, j, 0)), other(1), other(2), other(3)],
            out_specs=pl.BlockSpec(blk, lambda l, j, ids: (l, ids[1] if fill else 0, j, 0))),
        out_shape=jax.ShapeDtypeStruct((depth, 2 if fill else 1, rows, n), F32),
        compiler_params=_seq(2))(ids, part, met, met, met)


def _ada_mod(c_all, w_ada, b_ada_cols):
    depth, d, n = w_ada.shape
    nb = c_all.shape[0]

    def body(c_ref, w_ref, b_ref, o_ref):
        cv = c_ref[...]
        ca = _bf(cv * _sigmoid(cv))
        o_ref[0] = _dot(ca, _bf(w_ref[0])) + b_ref[0]

    return _pcall(body, name="ada_mod", grid=(depth,),
                  in_specs=[pl.BlockSpec((nb, d), lambda l: (0, 0)), pl.BlockSpec((1, d, n), lambda l: (l, 0, 0)),
                            pl.BlockSpec((1, 1, n), lambda l: (l, 0, 0))],
                  out_specs=pl.BlockSpec((1, nb, n), lambda l: (l, 0, 0)),
                  out_shape=jax.ShapeDtypeStruct((depth, nb, n), F32), compiler_params=_seq())(c_all, w_ada, b_ada_cols)


def _ada_grad(c_all, dmod_cols, dmod_all):
    nb, d = c_all.shape
    depth, _, n = dmod_cols.shape
    n_all = dmod_all.shape[2]

    def body(c_ref, dm_ref, da_ref, gw_ref, gb_ref):
        cv = c_ref[...]
        ca = _bf(cv * _sigmoid(cv))
        gw_ref[0] = _dot_tn(ca, _bf(dm_ref[0]))
        gb_ref[0] = _colsum(da_ref[0])

    return _pcall(body, name="ada_grad", grid=(depth,),
                  in_specs=[pl.BlockSpec((nb, d), lambda l: (0, 0)), pl.BlockSpec((1, nb, n), lambda l: (l, 0, 0)),
                            pl.BlockSpec((1, nb, n_all), lambda l: (l, 0, 0))],
                  out_specs=[pl.BlockSpec((1, d, n), lambda l: (l, 0, 0)), pl.BlockSpec((1, 1, n_all), lambda l: (l, 0, 0))],
                  out_shape=[jax.ShapeDtypeStruct((depth, d, n), F32), jax.ShapeDtypeStruct((depth, 1, n_all), F32)],
                  compiler_params=_seq())(c_all, dmod_cols, dmod_all)


def _adamw(w, g, m, v):
    shape = w.shape
    cols = shape[-1]
    rows = w.size // cols
    w2, g2, m2, v2 = (t.reshape(rows, cols) for t in (w, g, m, v))
    tr = _row_tile(rows, cap=max(8, (1 << 18) // cols), mult=8)

    def body(w_ref, g_ref, m_ref, v_ref, d_ref, mo_ref, vo_ref):
        gv = g_ref[...]
        mn = ADAM_B1 * m_ref[...] + (1.0 - ADAM_B1) * gv
        vn = ADAM_B2 * v_ref[...] + (1.0 - ADAM_B2) * (gv * gv)
        m_hat = mn / (1.0 - ADAM_B1 ** ADAM_STEP)
        v_hat = vn / (1.0 - ADAM_B2 ** ADAM_STEP)
        d_ref[...] = -ADAM_LR * (m_hat / (jnp.sqrt(v_hat) + ADAM_EPS) + ADAM_WD * w_ref[...])
        mo_ref[...] = mn
        vo_ref[...] = vn

    blk = pl.BlockSpec((tr, cols), lambda i: (i, 0))
    outs = _pcall(body, name="adamw", grid=(rows // tr,), in_specs=[blk] * 4, out_specs=[blk] * 3,
                  out_shape=[jax.ShapeDtypeStruct((rows, cols), F32)] * 3, compiler_params=_seq())(w2, g2, m2, v2)
    return tuple(o.reshape(shape) for o in outs)


WEIGHTS = ["norm_g", "w_ada", "b_ada", "w_in", "rg_conv_w", "rg_conv_b", "rg_w_a", "rg_b_a", "rg_w_x", "rg_b_x",
           "rg_lambda", "ml_conv_w", "ml_conv_b", "ml_w_q", "ml_w_k", "ml_w_v", "ml_w_if", "ml_b_if", "ml_norm_g",
           "w_out", "final_g"]
SMALL_SHARDED = {"ml_w_qkv": 2, "rg_conv_w": 1, "ml_conv_w": 1, "ml_w_if": 0}
REPLICATED = ["rg_w_a", "rg_w_x", "norm_g", "rg_conv_b", "rg_b_a", "rg_b_x", "rg_lambda", "ml_conv_b", "ml_norm_g",
              "ml_b_if"]
LANES = 128


def _to_pieces(g, axis):
    shp = g.shape
    g = g.reshape(shp[:axis] + (4, 2, shp[axis] // 8) + shp[axis + 1:])
    g = jnp.moveaxis(g, (axis, axis + 1), (0, 1))
    return g.reshape(4, 2, -1)


def _from_pieces(p, shard_shape, axis):
    k = p.shape[0]
    rest = shard_shape[:axis] + (shard_shape[axis] // k,) + shard_shape[axis + 1:]
    t = jnp.moveaxis(p.reshape((k,) + rest), 0, axis)
    return t.reshape(shard_shape)


def _pad_rows(flat, mult):
    n = flat.shape[-1]
    pad = (-n) % mult
    if pad:
        flat = jnp.concatenate([flat, jnp.zeros(flat.shape[:-1] + (pad,), flat.dtype)], axis=-1)
    return flat


def kernel(x, c, norm_g, w_ada, b_ada, w_in, rg_conv_w, rg_conv_b, rg_w_a, rg_b_a, rg_w_x, rg_b_x, rg_lambda, ml_conv_w, ml_conv_b, ml_w_q, ml_w_k, ml_w_v, ml_w_if, ml_b_if, ml_norm_g, w_out, final_g, loss_target, m_norm_g, m_w_ada, m_b_ada, m_w_in, m_rg_conv_w, m_rg_conv_b, m_rg_w_a, m_rg_b_a, m_rg_w_x, m_rg_b_x, m_rg_lambda, m_ml_conv_w, m_ml_conv_b, m_ml_w_q, m_ml_w_k, m_ml_w_v, m_ml_w_if, m_ml_b_if, m_ml_norm_g, m_w_out, m_final_g, v_norm_g, v_w_ada, v_b_ada, v_w_in, v_rg_conv_w, v_rg_conv_b, v_rg_w_a, v_rg_b_a, v_rg_w_x, v_rg_b_x, v_rg_lambda, v_ml_conv_w, v_ml_conv_b, v_ml_w_q, v_ml_w_k, v_ml_w_v, v_ml_w_if, v_ml_b_if, v_ml_norm_g, v_w_out, v_final_g):
    given = dict(locals())
    ax, ay, ac = lax.axis_index("x"), lax.axis_index("y"), lax.axis_index("c")
    chip = 2 * ax + ay
    me = 2 * chip + ac
    depth, d = norm_g.shape
    n_ada = w_ada.shape[2]
    pick = lambda a, i, axis=0: lax.dynamic_index_in_dim(a, i, axis, keepdims=False)

    convs = jnp.stack([rg_conv_w, ml_conv_w])
    n_conv = 2 * depth * CONV_WIDTH // 4
    blk = jnp.concatenate([c, convs.reshape(n_conv, d), jnp.zeros((8 - 1 - n_conv, d), F32)], axis=0)
    g0 = _all_gather8([blk], pltpu.VMEM)[0].reshape(8, 8, d)
    c_all = g0[:, 0, :]
    conv_full = g0[0::2, 1:1 + n_conv].reshape(4, 2, depth, CONV_WIDTH, d // 4)
    conv_full = conv_full.transpose(1, 2, 3, 0, 4).reshape(2, depth, CONV_WIDTH, d)

    b_cols = lax.dynamic_slice_in_dim(b_ada, chip * n_ada, n_ada, axis=1)[:, None, :]
    mod_part = _ada_mod(c_all, w_ada, b_cols)
    g1 = _all_gather8([mod_part.transpose(1, 0, 2).reshape(8, depth * n_ada)], pltpu.VMEM)[0]
    g1 = g1.reshape(8, 8, depth, n_ada)[0::2]
    mod_me = pick(g1.transpose(1, 2, 0, 3).reshape(8, depth, 4 * n_ada), me)

    def half_of(w, axis):
        n = w.shape[axis] // 2
        return lax.dynamic_slice_in_dim(w, ac * n, n, axis).astype(BF16)

    n_sh = w_in.shape[2]
    heads, hd_cut, hd = ml_w_q.shape[1:]
    blocks = []
    for l in range(depth):
        wqkv = jnp.stack([ml_w_q[l], ml_w_k[l], ml_w_v[l]])
        blocks += [half_of(w_in[l], 0), half_of(w_out[l], 0), half_of(wqkv, 2).reshape(-1, hd), half_of(ml_w_if[l], 0)]
    gathered = _all_gather8(blocks, pltpu.HBM)
    layers = []
    for l in range(depth):
        w4, w_out_b, wqkv_g, wif = gathered[4 * l:4 * l + 4]
        wqkv_b = _from_pieces(wqkv_g.reshape(8, -1), (3, heads, hd, hd), 2)
        layers.append(dict(
            norm_g=norm_g[l][None], shift=mod_me[l, 0:d][None], scale=mod_me[l, d:2 * d][None],
            gate=mod_me[l, 2 * d:3 * d][None], w4=w4.reshape(4, d, n_sh),
            rg_conv_w=conv_full[0, l], rg_conv_b=rg_conv_b[l][None], rg_wa_b=_bf(rg_w_a[l]), rg_ba=rg_b_a[l][None],
            rg_wx_b=_bf(rg_w_x[l]), rg_bx=rg_b_x[l][None], rg_lam=rg_lambda[l][None],
            ml_conv_w=conv_full[1, l], ml_conv_b=ml_conv_b[l][None], wqkv_b=wqkv_b, wif_b=wif, wift_b=wif.T,
            b_if=ml_b_if[l][None], b_ift=ml_b_if[l][:, None], ml_g=ml_norm_g[l][None], w_out_b=w_out_b))

    loss, dx, g_final, grads, g_in, g_out, dmods = _trunk_fwd_bwd(x[0], loss_target[0], final_g[None], layers)

    dm_blk = jnp.concatenate(dmods + [jnp.zeros((8 - depth, 3 * d), F32)], axis=0)
    dm_all = _all_gather8([dm_blk], pltpu.VMEM)[0].reshape(8, 8, 3 * d)[:, :depth].transpose(1, 0, 2)
    dm_cols = lax.dynamic_slice_in_dim(dm_all, chip * n_ada, n_ada, axis=2)
    g_w_ada, g_b_ada = _ada_grad(c_all, dm_cols, dm_all)

    r_out = g_out.shape[1] // 8
    g_out5 = g_out.reshape(depth, 4, 2, r_out, d)
    sm = jnp.concatenate([_to_pieces(grads[l][name], axis) for l in range(depth) for name, axis in SMALL_SHARDED.items()],
                         axis=-1)
    sm = _pad_rows(sm, 16 * LANES)
    n_sm = sm.shape[-1] // LANES
    sm = sm.transpose(1, 0, 2).reshape(2, 4 * n_sm, LANES)
    rep = [grads[l][name].reshape(-1) for l in range(depth) for name in REPLICATED[:-1]]
    rep += [_pad_rows(grads[l]["ml_b_if"].reshape(-1), LANES) for l in range(depth)]
    rep += [g_final.reshape(-1), loss.reshape(-1)]
    rep = _pad_rows(jnp.concatenate(rep), 8 * 8 * LANES)
    n_rep = rep.shape[0] // (8 * LANES)
    rep = rep.reshape(4, 2, n_rep, LANES).transpose(1, 0, 2, 3).reshape(2, 4 * n_rep, LANES)
    got_in, got_out, got_sm, got_rep = _sib_halves(g_in, g_out5, [sm, rep])
    half = ac.reshape(1)
    part_in = _pair_sum(
        half, g_in, pl.BlockSpec((None, d // 2, n_sh), lambda l, s, h: (l, h[0], s)),
        got_in, pl.BlockSpec((None, None, d // 2, n_sh), lambda l, s, h: (l, s, 0, 0)),
        jax.ShapeDtypeStruct((4, depth, d // 2, n_sh), BF16),
        pl.BlockSpec((None, None, d // 2, n_sh), lambda l, s, h: (s, l, 0, 0)), (depth, 4))
    part_out = _pair_sum(
        half, g_out5, pl.BlockSpec((None, None, None, r_out, d), lambda l, s, h: (l, s, h[0], 0, 0)),
        got_out, pl.BlockSpec((None, None, r_out, d), lambda l, s, h: (l, s, 0, 0)),
        jax.ShapeDtypeStruct((4, depth, r_out, d), BF16),
        pl.BlockSpec((None, None, r_out, d), lambda l, s, h: (s, l, 0, 0)), (depth, 4))

    def slab_sum(slab, got, rows, dtype):
        blk = pl.BlockSpec((rows, LANES), lambda s, h: (s, 0))
        return _pair_sum(half, slab, pl.BlockSpec((None, rows, LANES), lambda s, h: (h[0], s, 0)), got, blk,
                         jax.ShapeDtypeStruct((4 * rows, LANES), dtype), blk, (4,)).reshape(4, rows, LANES)

    part_sm = slab_sum(sm, got_sm, n_sm, BF16)
    part_rep = slab_sum(rep, got_rep, n_rep, F32)
    part_sm, part_rep = part_sm[:, None], part_rep[:, None]
    met_in, met_out, met_sm, met_rep = _chip_exchange([part_in, part_out, part_sm, part_rep])
    ids = jnp.stack([chip, ac])
    both_in, both_out, both_sm = _sib_fill([_chip_sum(ids, part_in, met_in, True), _chip_sum(ids, part_out, met_out, True),
                                            _chip_sum(ids, part_sm, met_sm, True)])
    red_rep = _chip_sum(ids, part_rep, met_rep, False).reshape(n_rep, LANES)
    rep_all = _all_gather8([red_rep], pltpu.VMEM)[0].reshape(-1)

    g = dict(w_ada=g_w_ada, b_ada=g_b_ada.reshape(b_ada.shape), w_in=both_in.reshape(w_in.shape),
             w_out=both_out.reshape(w_out.shape))
    shard = both_sm.reshape(2, -1)
    off = 0
    per_layer = {name: [] for name in SMALL_SHARDED}
    for l in range(depth):
        for name, axis in SMALL_SHARDED.items():
            shp = (3,) + ml_w_q.shape[1:] if name == "ml_w_qkv" else given[name].shape[1:]
            n = grads[l][name].size // 8
            per_layer[name].append(_from_pieces(shard[:, off:off + n], shp, axis))
            off += n
    for name in SMALL_SHARDED:
        g[name] = jnp.stack(per_layer[name])
    for i, name in enumerate(["ml_w_q", "ml_w_k", "ml_w_v"]):
        g[name] = g["ml_w_qkv"][:, i]
    off = 0
    per_layer = {name: [] for name in REPLICATED}
    for l in range(depth):
        for name in REPLICATED[:-1]:
            n = given[name][l].size
            per_layer[name].append(rep_all[off:off + n].reshape(given[name].shape[1:]))
            off += n
    for l in range(depth):
        n = given["ml_b_if"][l].size
        per_layer["ml_b_if"].append(rep_all[off:off + n])
        off += LANES
    for name in REPLICATED:
        g[name] = jnp.stack(per_layer[name])
    g["final_g"] = rep_all[off:off + d]
    loss_all = rep_all[off + d]

    deltas, new_m, new_v = [], [], []
    for name in WEIGHTS:
        dl, mn, vn = _adamw(given[name], g[name], given["m_" + name], given["v_" + name])
        deltas.append(dl)
        new_m.append(mn)
        new_v.append(vn)
    return (loss_all, dx[None], *[g[name] for name in WEIGHTS], *deltas, *new_m, *new_v)
```

```python
import functools

import jax
import jax.numpy as jnp
from jax import lax
from jax.experimental import pallas as pl
from jax.experimental.pallas import tpu as pltpu

F32 = jnp.float32
BF16 = jnp.bfloat16

EPS = 1e-6
RG_C = 8.0
CONV_WIDTH = 4
ML_CHUNK = 128
HALO = 8
ADAM_LR = 0.001
ADAM_B1 = 0.9
ADAM_B2 = 0.999
ADAM_EPS = 1e-08
ADAM_WD = 0.01
ADAM_STEP = 10
MESH = pl.DeviceIdType.MESH


def _pcall(body, **kw):
    return pl.pallas_call(body, **kw)


def _seq(n=1):
    return pltpu.CompilerParams(dimension_semantics=("arbitrary",) * n)


def _dot(a, b):
    return jnp.dot(a, b, preferred_element_type=F32)


def _dot_nt(a, b):
    return lax.dot_general(a, b, (((1,), (1,)), ((), ())), preferred_element_type=F32)


def _dot_tn(a, b):
    return lax.dot_general(a, b, (((0,), (0,)), ((), ())), preferred_element_type=F32)


def _bf(x):
    return x.astype(BF16)


def _sigmoid(x):
    return 1.0 / (1.0 + jnp.exp(-x))


def _log1p(z):
    u = 1.0 + z
    return jnp.where(u == 1.0, z, jnp.log(u) * (z / jnp.where(u == 1.0, 1.0, u - 1.0)))


def _softplus(x):
    return jnp.maximum(x, 0.0) + _log1p(jnp.exp(-jnp.abs(x)))


def _log_sigmoid(x):
    return -_softplus(-x)


def _expm1(x):
    small = x * (1.0 + x * (0.5 + x * (1.0 / 6.0 + x * (1.0 / 24.0 + x * (1.0 / 120.0)))))
    return jnp.where(jnp.abs(x) < 0.03, small, jnp.exp(x) - 1.0)


def _dsilu(x, s):
    return s * (1.0 + x * (1.0 - s))


def _rowsum(x):
    return jnp.sum(x, axis=1, keepdims=True)


def _colsum(x):
    return jnp.sum(x, axis=0, keepdims=True)


def _shift_down(win, s):
    return win if s == 0 else pltpu.roll(win, s, 0)


def _shift_up(win, s):
    return win if s == 0 else pltpu.roll(win, win.shape[0] - s, 0)


def _conv_fwd(win, w_ref, b_ref):
    acc = b_ref[...] + w_ref[CONV_WIDTH - 1:CONV_WIDTH, :] * win[HALO:]
    for k in range(CONV_WIDTH - 1):
        acc = acc + w_ref[k:k + 1, :] * _shift_down(win, CONV_WIDTH - 1 - k)[HALO:]
    return acc


def _split3(x):
    hi = _bf(x)
    r1 = x - hi.astype(F32)
    mid = _bf(r1)
    lo = _bf(r1 - mid.astype(F32))
    return hi, mid, lo


def _tri_dot_left(tri, x):
    hi, mid, lo = _split3(x)
    return _dot(tri, hi) + _dot(tri, mid) + _dot(tri, lo)


def _tri_dot_right(x, tri):
    hi, mid, lo = _split3(x)
    return _dot(hi, tri) + _dot(mid, tri) + _dot(lo, tri)


def _tile(n, want):
    t = min(n, want)
    assert n % t == 0
    return t


def _ln_inproj(x, g, scale, shift, w4):
    s_len, d = x.shape
    nj, _, nsh = w4.shape
    tm = _tile(s_len, 512)

    def body(x_ref, g_ref, sc_ref, sh_ref, w_ref, h_ref, u_ref, hs):
        @pl.when(pl.program_id(1) == 0)
        def _():
            xv = x_ref[...]
            r = lax.rsqrt(jnp.mean(xv * xv, axis=-1, keepdims=True) + EPS)
            hv = (xv * r * g_ref[...]) * (1.0 + sc_ref[...]) + sh_ref[...]
            hs[...] = _bf(hv)
            h_ref[...] = hs[...]

        u_ref[...] = _dot(hs[...], w_ref[0])

    vec = pl.BlockSpec((1, d), lambda i, j: (0, 0))
    return _pcall(
        body, name="ln_inproj", grid=(s_len // tm, nj),
        in_specs=[pl.BlockSpec((tm, d), lambda i, j: (i, 0)), vec, vec, vec,
                  pl.BlockSpec((1, d, nsh), lambda i, j: (j, 0, 0))],
        out_specs=[pl.BlockSpec((tm, d), lambda i, j: (i, 0)), pl.BlockSpec((tm, nsh), lambda i, j: (i, j))],
        out_shape=[jax.ShapeDtypeStruct((s_len, d), BF16), jax.ShapeDtypeStruct((s_len, nj * nsh), F32)],
        scratch_shapes=[pltpu.VMEM((tm, d), BF16)],
        compiler_params=_seq(2),
    )(x, g, scale, shift, w4)


def _rg_gates(xc, wa_ref, ba_ref, wx_ref, bx_ref, lam_ref):
    heads, hd, _ = wa_ref.shape
    xb = _bf(xc)
    ga = jnp.concatenate([_dot(xb[:, h * hd:(h + 1) * hd], wa_ref[h]) for h in range(heads)], axis=1) + ba_ref[...]
    gx = jnp.concatenate([_dot(xb[:, h * hd:(h + 1) * hd], wx_ref[h]) for h in range(heads)], axis=1) + bx_ref[...]
    r = _sigmoid(ga)
    ig = _sigmoid(gx)
    sp = _softplus(-lam_ref[...])
    log_a = (-RG_C) * r * sp
    a = jnp.exp(log_a)
    mult = jnp.sqrt(-_expm1(2.0 * log_a))
    return r, ig, sp, log_a, a, mult


def _scan_groups(a, u, reverse):
    n = a.shape[0]
    row = lax.broadcasted_iota(jnp.int32, a.shape, 0) & 7
    for k in (1, 2, 4):
        if reverse:
            a_sh, u_sh = _shift_up(a, k), _shift_up(u, k)
            ok = row < 8 - k
        else:
            a_sh, u_sh = _shift_down(a, k), _shift_down(u, k)
            ok = row >= k
        u = jnp.where(ok, a * u_sh + u, u)
        a = jnp.where(ok, a * a_sh, a)
    del n
    return a, u


def _rg_fwd(u, conv_w, conv_b, wa_b, ba, wx_b, bx, lam):
    s_len = u.shape[0]
    d = conv_w.shape[1]
    tm = _tile(s_len, 256)
    per = tm // HALO

    def body(x_ref, xp_ref, z_ref, cw_ref, cb_ref, wa_ref, ba_ref, wx_ref, bx_ref, lam_ref,
             hh_ref, y_ref, carry):
        i = pl.program_id(0)

        @pl.when(i == 0)
        def _():
            carry[...] = jnp.zeros_like(carry)

        prev = jnp.where(i == 0, 0.0, xp_ref[...])
        xc = _conv_fwd(jnp.concatenate([prev, x_ref[...]], axis=0), cw_ref, cb_ref)
        _, ig, _, _, a, mult = _rg_gates(xc, wa_ref, ba_ref, wx_ref, bx_ref, lam_ref)
        ca, cu = _scan_groups(a, mult * (ig * xc), reverse=False)
        c = carry[0:1, :]
        for j in range(per):
            blk = ca[j * 8:(j + 1) * 8] * c + cu[j * 8:(j + 1) * 8]
            hh_ref[j * 8:(j + 1) * 8, :] = blk
            c = blk[7:8]
        carry[0:1, :] = c
        z = z_ref[...]
        y_ref[0] = _bf(hh_ref[...] * (z * _sigmoid(z)))

    vec = pl.BlockSpec((1, d), lambda i: (0, 0))
    whole3 = lambda a: pl.BlockSpec(a.shape, lambda i: (0, 0, 0))
    return _pcall(
        body, name="rg_fwd", grid=(s_len // tm,),
        in_specs=[pl.BlockSpec((tm, d), lambda i: (i, 0)),
                  pl.BlockSpec((HALO, d), lambda i: (jnp.maximum(i * per - 1, 0), 0)),
                  pl.BlockSpec((tm, d), lambda i: (i, 1)),
                  pl.BlockSpec((CONV_WIDTH, d), lambda i: (0, 0)), vec,
                  whole3(wa_b), vec, whole3(wx_b), vec, vec],
        out_specs=[pl.BlockSpec((tm, d), lambda i: (i, 0)), pl.BlockSpec((1, tm, d), lambda i: (0, i, 0))],
        out_shape=[jax.ShapeDtypeStruct((s_len, d), F32), jax.ShapeDtypeStruct((2, s_len, d), BF16)],
        scratch_shapes=[pltpu.VMEM((8, d), F32)],
        compiler_params=_seq(),
    )(u, u, u, conv_w, conv_b, wa_b, ba, wx_b, bx, lam)


def _ml_pre(u, conv_w, conv_b, wqkv_b, wif_b, wift_b, b_if, b_ift):
    s_len = u.shape[0]
    d = conv_w.shape[1]
    _, heads, hd, _ = wqkv_b.shape
    ng = 2 * heads
    tm = _tile(s_len, 256)
    per = tm // HALO

    def body(x_ref, xp_ref, cw_ref, cb_ref, w_ref, wif_ref, wift_ref, bif_ref, bift_ref,
             qkv_ref, gt_ref, gtt_ref):
        i = pl.program_id(0)
        prev = jnp.where(i == 0, 0.0, xp_ref[...])
        xm = x_ref[...]
        pre = _conv_fwd(jnp.concatenate([prev, xm], axis=0), cw_ref, cb_ref)
        xcb = _bf(pre * _sigmoid(pre))
        xmb = _bf(xm)
        for h in range(heads):
            hs = slice(h * hd, (h + 1) * hd)
            qkv_ref[0, :, hs] = _bf(_dot(xcb[:, hs], w_ref[0, h]))
            qkv_ref[1, :, hs] = _bf(_dot(xcb[:, hs], w_ref[1, h]))
            qkv_ref[2, :, hs] = _bf(_dot(xmb[:, hs], w_ref[2, h]))
        qb, kb, vb = qkv_ref[0], qkv_ref[1], qkv_ref[2]
        gt_ref[...] = (_dot(qb, wif_ref[0:d, :]) + _dot(kb, wif_ref[d:2 * d, :]) + _dot(vb, wif_ref[2 * d:3 * d, :])
                       + bif_ref[...])
        gtt_ref[...] = (_dot_nt(wift_ref[:, 0:d], qb) + _dot_nt(wift_ref[:, d:2 * d], kb)
                        + _dot_nt(wift_ref[:, 2 * d:3 * d], vb) + bift_ref[...])

    vec = pl.BlockSpec((1, d), lambda i: (0, 0))
    whole2 = lambda a: pl.BlockSpec(a.shape, lambda i: (0, 0))
    return _pcall(
        body, name="ml_pre", grid=(s_len // tm,),
        in_specs=[pl.BlockSpec((tm, d), lambda i: (i, 2)),
                  pl.BlockSpec((HALO, d), lambda i: (jnp.maximum(i * per - 1, 0), 2)),
                  pl.BlockSpec((CONV_WIDTH, d), lambda i: (0, 0)), vec,
                  pl.BlockSpec(wqkv_b.shape, lambda i: (0, 0, 0, 0)), whole2(wif_b), whole2(wift_b), whole2(b_if),
                  whole2(b_ift)],
        out_specs=[pl.BlockSpec((3, tm, d), lambda i: (0, i, 0)), pl.BlockSpec((tm, ng), lambda i: (i, 0)),
                   pl.BlockSpec((ng, tm), lambda i: (0, i))],
        out_shape=[jax.ShapeDtypeStruct((3, s_len, d), BF16), jax.ShapeDtypeStruct((s_len, ng), F32),
                   jax.ShapeDtypeStruct((ng, s_len), F32)],
        compiler_params=_seq(),
    )(u, u, conv_w, conv_b, wqkv_b, wif_b, wift_b, b_if, b_ift)


def _chunk_gates(gt, gtt, h, heads, tril, triu):
    li_c = gt[:, h:h + 1]
    li_r = gtt[h:h + 1, :]
    gf_c = gt[:, heads + h:heads + h + 1]
    lf_c = _log_sigmoid(gf_c)
    lf_r = _log_sigmoid(gtt[heads + h:heads + h + 1, :])
    b_c = _tri_dot_left(tril, lf_c)
    b_r = _tri_dot_right(lf_r, triu)
    return li_c, li_r, gf_c, b_c, b_r


def _chunk_weights(li_c, li_r, b_c, b_r, m_prev, causal):
    lc = b_c.shape[0]
    b_last = b_c[lc - 1:lc, :]
    dmat = jnp.where(causal, b_c - b_r + li_r, -jnp.inf)
    m_inter = b_c + m_prev
    m_t = jnp.maximum(m_inter, jnp.max(dmat, axis=1, keepdims=True))
    w_intra = jnp.exp(dmat - m_t)
    w_inter = jnp.exp(m_inter - m_t)
    g_c = b_last - b_c + li_c
    m_new = jnp.maximum(b_last + m_prev, jnp.max(g_c, axis=0, keepdims=True))
    w_state = jnp.exp(g_c - m_new)
    decay = jnp.exp(b_last + m_prev - m_new)
    return m_t, w_intra, w_inter, m_new, w_state, decay


def _tri_masks(lc):
    r = lax.broadcasted_iota(jnp.int32, (lc, lc), 0)
    c = lax.broadcasted_iota(jnp.int32, (lc, lc), 1)
    causal = r >= c
    return causal, causal.astype(BF16), (r <= c).astype(BF16)


def _mlstm_fwd(qkv, gt, gtt, u, ml_g, ycat):
    _, s_len, d = qkv.shape
    ng = gt.shape[1]
    heads = ng // 2
    hd = d // heads
    lc = ML_CHUNK
    nc = s_len // lc
    kscale = hd ** -0.5

    def body(qkv_ref, gt_ref, gtt_ref, o_ref, z_ref, g_ref, _, cell_ref, y_ref, cst_ref, nst_ref, mst_ref, cs, ns, ms):
        @pl.when(pl.program_id(0) == 0)
        def _():
            cs[...] = jnp.zeros_like(cs)
            ns[...] = jnp.zeros_like(ns)
            ms[...] = jnp.zeros_like(ms)

        causal, tril, triu = _tri_masks(lc)
        gtv, gttv = gt_ref[...], gtt_ref[...]
        for h in range(heads):
            hs = slice(h * hd, (h + 1) * hd)
            li_c, li_r, _, b_c, b_r = _chunk_gates(gtv, gttv, h, heads, tril, triu)
            m_prev = ms[h][:, 0:1]
            m_t, w_intra, w_inter, m_new, w_state, decay = _chunk_weights(li_c, li_r, b_c, b_r, m_prev, causal)
            qb = qkv_ref[0, :, hs]
            ks = qkv_ref[1, :, hs].astype(F32) * kscale
            kb = _bf(ks)
            vb = qkv_ref[2, :, hs]
            c_old = cs[h]
            n_old = ns[h]
            cst_ref[0, h] = _bf(c_old)
            nst_ref[0, h] = n_old
            mst_ref[0, h] = ms[h]
            s = _dot_nt(qb, kb) * w_intra
            num = _dot(_bf(s), vb) + w_inter * _dot(qb, _bf(c_old))
            den = _rowsum(s) + w_inter * _rowsum(qb.astype(F32) * n_old)
            cell = num / jnp.maximum(jnp.abs(den), jnp.exp(-m_t))
            kw = ks * w_state
            cs[h] = decay * c_old + _dot_tn(_bf(kw), vb)
            ns[h] = decay * n_old + _colsum(kw)
            ms[h] = jnp.broadcast_to(m_new, ms[h].shape)
            cell_ref[:, hs] = cell
            hm = _sigmoid(o_ref[:, hs]) * cell
            hn = hm * lax.rsqrt(jnp.mean(hm * hm, axis=-1, keepdims=True) + EPS)
            z = z_ref[:, hs]
            y_ref[0, :, hs] = _bf((hn * g_ref[:, hs]) * (z * _sigmoid(z)))

    row = pl.BlockSpec((lc, d), lambda c: (c, 0))
    return _pcall(
        body, name="mlstm_fwd", grid=(nc,),
        in_specs=[pl.BlockSpec((3, lc, d), lambda c: (0, c, 0)), pl.BlockSpec((lc, ng), lambda c: (c, 0)),
                  pl.BlockSpec((ng, lc), lambda c: (0, c)),
                  pl.BlockSpec((lc, d), lambda c: (c, 3)), pl.BlockSpec((lc, d), lambda c: (c, 4)),
                  pl.BlockSpec((1, d), lambda c: (0, 0)), pl.BlockSpec(memory_space=pl.ANY)],
        out_specs=[row, pl.BlockSpec((1, lc, d), lambda c: (1, c, 0)),
                   pl.BlockSpec((1, heads, hd, hd), lambda c: (c, 0, 0, 0)),
                   pl.BlockSpec((1, heads, 1, hd), lambda c: (c, 0, 0, 0)),
                   pl.BlockSpec((1, heads, 1, 128), lambda c: (c, 0, 0, 0))],
        out_shape=[jax.ShapeDtypeStruct((s_len, d), F32), jax.ShapeDtypeStruct(ycat.shape, BF16),
                   jax.ShapeDtypeStruct((nc, heads, hd, hd), BF16),
                   jax.ShapeDtypeStruct((nc, heads, 1, hd), F32),
                   jax.ShapeDtypeStruct((nc, heads, 1, 128), F32)],
        scratch_shapes=[pltpu.VMEM((heads, hd, hd), F32), pltpu.VMEM((heads, 1, hd), F32),
                        pltpu.VMEM((heads, 1, 128), F32)],
        input_output_aliases={6: 1},
        compiler_params=_seq(),
    )(qkv, gt, gtt, u, u, ml_g, ycat)


def _out_proj(ycat, w_out_b, x, gate):
    s_len, d = x.shape
    tm = _tile(s_len, 512)

    def body(a_ref, w_ref, x_ref, g_ref, y_ref, xn_ref):
        y = _dot(a_ref[0], w_ref[0:d, :]) + _dot(a_ref[1], w_ref[d:2 * d, :])
        y_ref[...] = y
        xn_ref[...] = x_ref[...] + g_ref[...] * y

    row = pl.BlockSpec((tm, d), lambda i: (i, 0))
    return _pcall(
        body, name="out_proj", grid=(s_len // tm,),
        in_specs=[pl.BlockSpec((2, tm, d), lambda i: (0, i, 0)), pl.BlockSpec((2 * d, d), lambda i: (0, 0)), row,
                  pl.BlockSpec((1, d), lambda i: (0, 0))],
        out_specs=[row, row],
        out_shape=[jax.ShapeDtypeStruct((s_len, d), F32)] * 2,
        compiler_params=_seq(),
    )(ycat, w_out_b, x, gate)


def _final_loss(x, g, target):
    s_len, d = x.shape
    tm = _tile(s_len, 256)

    def body(x_ref, g_ref, t_ref, dx_ref, dg_ref, loss_ref):
        @pl.when(pl.program_id(0) == 0)
        def _():
            dg_ref[...] = jnp.zeros_like(dg_ref)
            loss_ref[...] = jnp.zeros_like(loss_ref)

        xv = x_ref[...]
        r = lax.rsqrt(jnp.mean(xv * xv, axis=-1, keepdims=True) + EPS)
        xn = xv * r
        err = xn * g_ref[...] - t_ref[...]
        loss_ref[...] += 0.5 * jnp.sum(jnp.mean(err * err, axis=-1, keepdims=True))
        dout = err * (1.0 / d)
        dg_ref[...] += _colsum(dout * xn)
        dxn = dout * g_ref[...]
        dx_ref[...] = r * (dxn - xn * jnp.mean(dxn * xn, axis=-1, keepdims=True))

    row = pl.BlockSpec((tm, d), lambda i: (i, 0))
    vec = pl.BlockSpec((1, d), lambda i: (0, 0))
    return _pcall(
        body, name="final_loss", grid=(s_len // tm,),
        in_specs=[row, vec, row],
        out_specs=[row, vec, pl.BlockSpec((1, 128), lambda i: (0, 0))],
        out_shape=[jax.ShapeDtypeStruct((s_len, d), F32), jax.ShapeDtypeStruct((1, d), F32),
                   jax.ShapeDtypeStruct((1, 128), F32)],
        compiler_params=_seq(),
    )(x, g, target)


def _out_bwd(dxn, y, gate, w_out_b):
    s_len, d = dxn.shape
    tm = _tile(s_len, 512)

    def body(dx_ref, y_ref, g_ref, w_ref, dg_ref, dy_ref, dc_ref):
        @pl.when(pl.program_id(0) == 0)
        def _():
            dg_ref[...] = jnp.zeros_like(dg_ref)

        dx = dx_ref[...]
        dg_ref[...] += _colsum(dx * y_ref[...])
        dy = _bf(g_ref[...] * dx)
        dy_ref[...] = dy
        dc_ref[0] = _dot_nt(dy, w_ref[0:d, :])
        dc_ref[1] = _dot_nt(dy, w_ref[d:2 * d, :])

    row = pl.BlockSpec((tm, d), lambda i: (i, 0))
    vec = pl.BlockSpec((1, d), lambda i: (0, 0))
    return _pcall(
        body, name="out_bwd", grid=(s_len // tm,),
        in_specs=[row, row, vec, pl.BlockSpec((2 * d, d), lambda i: (0, 0))],
        out_specs=[vec, row, pl.BlockSpec((2, tm, d), lambda i: (0, i, 0))],
        out_shape=[jax.ShapeDtypeStruct((1, d), F32), jax.ShapeDtypeStruct((s_len, d), BF16),
                   jax.ShapeDtypeStruct((2, s_len, d), F32)],
        compiler_params=_seq(),
    )(dxn, y, gate, w_out_b)


def _grad_matmul(a3, b3, nblk, a_idx, b_idx, out_shape, out_block, out_idx, layer, stack):
    _, s_len, m = a3.shape
    n = b3.shape[2]
    tk = _tile(s_len, 512)
    first = isinstance(stack, int)

    def body(a_ref, b_ref, *rest):
        o_ref = rest[-1]

        @pl.when(pl.program_id(1) == 0)
        def _():
            o_ref[...] = jnp.zeros_like(o_ref)

        o_ref[...] += _dot_tn(a_ref[0], b_ref[0])

    in_specs = [pl.BlockSpec((1, tk, m), lambda p, t: (a_idx(p), t, 0)),
                pl.BlockSpec((1, tk, n), lambda p, t: (b_idx(p), t, 0))]
    return _pcall(
        body, name="grad_matmul", grid=(nblk, s_len // tk),
        in_specs=in_specs if first else in_specs + [pl.BlockSpec(memory_space=pl.ANY)],
        out_specs=pl.BlockSpec((None,) + out_block, lambda p, t: (layer,) + out_idx(p)),
        out_shape=jax.ShapeDtypeStruct(((stack,) if first else stack.shape[:1]) + out_shape, F32),
        input_output_aliases={} if first else {2: 0},
        compiler_params=_seq(2),
    )(*((a3, b3) if first else (a3, b3, stack)))


DU_PLANE = (2, 3, 4, 0, 1)


def _mlstm_bwd(qkv, gt, gtt, cst, nst, mst, cell, u, ml_g, d_ycat, wif_b):
    _, s_len, d = qkv.shape
    ng = gt.shape[1]
    heads = ng // 2
    hd = d // heads
    lc = ML_CHUNK
    nc = s_len // lc
    kscale = hd ** -0.5

    def body(qkv_ref, gt_ref, gtt_ref, cst_ref, nst_ref, mst_ref, cell_ref, o_ref, z_ref, g_ref, dy_ref,
             wif_ref, dqkv_ref, dgt_ref, dbif_ref, du_ref, dg_ref, dcs, dns, dqs, dks, dvs):
        @pl.when(pl.program_id(0) == 0)
        def _():
            dbif_ref[...] = jnp.zeros_like(dbif_ref)
            dcs[...] = jnp.zeros_like(dcs)
            dns[...] = jnp.zeros_like(dns)
            dg_ref[...] = jnp.zeros_like(dg_ref)

        causal, tril, triu = _tri_masks(lc)
        tril_strict = (tril.astype(F32) - (tril * triu).astype(F32)).astype(BF16)
        gtv, gttv = gt_ref[...], gtt_ref[...]
        lane = lax.broadcasted_iota(jnp.int32, (lc, ng), 1)
        dgt = jnp.zeros((lc, ng), F32)
        for h in range(heads):
            hs = slice(h * hd, (h + 1) * hd)
            li_c, li_r, gf_c, b_c, b_r = _chunk_gates(gtv, gttv, h, heads, tril, triu)
            m_prev = mst_ref[0, h][:, 0:1]
            m_t, w_intra, w_inter, _, w_state, decay = _chunk_weights(li_c, li_r, b_c, b_r, m_prev, causal)
            qb = qkv_ref[0, :, hs]
            qf = qb.astype(F32)
            ks = qkv_ref[1, :, hs].astype(F32) * kscale
            kb = _bf(ks)
            vb = qkv_ref[2, :, hs]
            c_b = cst_ref[0, h]
            n_old = nst_ref[0, h]
            s = _dot_nt(qb, kb) * w_intra
            den = _rowsum(s) + w_inter * _rowsum(qf * n_old)
            floor = jnp.exp(-m_t)
            dstab = jnp.maximum(jnp.abs(den), floor)
            cell = cell_ref[:, hs]
            o = o_ref[:, hs]
            so = _sigmoid(o)
            hm = so * cell
            rinv = lax.rsqrt(jnp.mean(hm * hm, axis=-1, keepdims=True) + EPS)
            hn = hm * rinv
            z = z_ref[:, hs]
            sgz = _sigmoid(z)
            sz = z * sgz
            gh = g_ref[:, hs]
            dy = dy_ref[0, :, hs]
            du_ref[1, :, hs] = _bf(dy * (hn * gh) * _dsilu(z, sgz))
            dg_ref[:, hs] += _colsum(dy * hn * sz)
            dhn = dy * gh * sz
            dhm = rinv * (dhn - hn * jnp.mean(dhn * hn, axis=-1, keepdims=True))
            du_ref[0, :, hs] = _bf(dhm * cell * so * (1.0 - so))
            dcell = dhm * so
            dnum = dcell / dstab
            dnb = _bf(dnum)
            dden = -_rowsum(dcell * cell) / dstab * jnp.where(jnp.abs(den) > floor, jnp.where(den > 0.0, 1.0, -1.0), 0.0)
            dst = _dot_nt(dnb, vb) + dden
            dsdb = _bf(dst * w_intra)
            dc_out = dcs[h]
            dn_out = dns[h]
            dcb = _bf(dc_out)
            dq_inter = w_inter * (_dot_nt(dnb, c_b) + dden * n_old)
            dk_inter = w_state * (_dot_nt(vb, dcb) + dn_out)
            dq = _dot(dsdb, kb) + dq_inter
            dk = _dot_tn(dsdb, qb) + dk_inter
            dv = _dot_tn(_bf(s), dnb) + _dot(_bf(ks * w_state), dcb)
            wq = w_inter * qf
            dcs[h] = decay * dc_out + _dot_tn(_bf(wq), dnb)
            dns[h] = decay * dn_out + _colsum(wq * dden)
            pmat = dst * s
            p_rows = _rowsum(pmat)
            p_cols = _rowsum(pmat.T)
            q_in = _rowsum(qf * dq_inter)
            k_in = _rowsum(ks * dk_inter)
            across = decay * (jnp.sum(dc_out * c_b.astype(F32), keepdims=True) + jnp.sum(dn_out * n_old, keepdims=True))
            dli = p_cols + k_in
            dlf = _tri_dot_left(triu, p_rows - p_cols + q_in) + _tri_dot_left(tril_strict, k_in) + across
            dgf = dlf * _sigmoid(-gf_c)
            dgt = dgt + jnp.where(lane == h, dli, 0.0) + jnp.where(lane == heads + h, dgf, 0.0)
            dqs[:, hs] = dq
            dks[:, hs] = dk * kscale
            dvs[:, hs] = dv
        dgt_ref[...] = dgt
        dbif_ref[...] += _colsum(dgt)
        dgb = _bf(dgt)
        dqkv_ref[0] = _bf(dqs[...] + _dot_nt(dgb, wif_ref[0:d, :]))
        dqkv_ref[1] = _bf(dks[...] + _dot_nt(dgb, wif_ref[d:2 * d, :]))
        dqkv_ref[2] = _bf(dvs[...] + _dot_nt(dgb, wif_ref[2 * d:3 * d, :]))

    rev = lambda c: nc - 1 - c
    row = pl.BlockSpec((lc, d), lambda c: (rev(c), 0))
    return _pcall(
        body, name="mlstm_bwd", grid=(nc,),
        in_specs=[pl.BlockSpec((3, lc, d), lambda c: (0, rev(c), 0)), pl.BlockSpec((lc, ng), lambda c: (rev(c), 0)),
                  pl.BlockSpec((ng, lc), lambda c: (0, rev(c))),
                  pl.BlockSpec((1, heads, hd, hd), lambda c: (rev(c), 0, 0, 0)),
                  pl.BlockSpec((1, heads, 1, hd), lambda c: (rev(c), 0, 0, 0)),
                  pl.BlockSpec((1, heads, 1, 128), lambda c: (rev(c), 0, 0, 0)),
                  row, pl.BlockSpec((lc, d), lambda c: (rev(c), 3)), pl.BlockSpec((lc, d), lambda c: (rev(c), 4)),
                  pl.BlockSpec((1, d), lambda c: (0, 0)), pl.BlockSpec((1, lc, d), lambda c: (1, rev(c), 0)),
                  pl.BlockSpec((3 * d, ng), lambda c: (0, 0))],
        out_specs=[pl.BlockSpec((3, lc, d), lambda c: (0, rev(c), 0)), pl.BlockSpec((lc, ng), lambda c: (rev(c), 0)),
                   pl.BlockSpec((1, ng), lambda c: (0, 0)), pl.BlockSpec((2, lc, d), lambda c: (0, rev(c), 0)),
                   pl.BlockSpec((1, d), lambda c: (0, 0))],
        out_shape=[jax.ShapeDtypeStruct((3, s_len, d), BF16), jax.ShapeDtypeStruct((s_len, ng), F32),
                   jax.ShapeDtypeStruct((1, ng), F32), jax.ShapeDtypeStruct((5, s_len, d), BF16),
                   jax.ShapeDtypeStruct((1, d), F32)],
        scratch_shapes=[pltpu.VMEM((heads, hd, hd), F32), pltpu.VMEM((heads, 1, hd), F32)]
        + [pltpu.VMEM((lc, d), F32)] * 3,
        compiler_params=_seq(),
    )(qkv, gt, gtt, cst, nst, mst, cell, u, u, ml_g, d_ycat, wif_b)


def _conv_bwd_tile(dp, later, xwin, cw_ref, gw_ref, gb_ref):
    tm = dp.shape[0]
    dwin = jnp.concatenate([dp, later[...]], axis=0)
    later[...] = dp[0:HALO]
    acc = cw_ref[CONV_WIDTH - 1:CONV_WIDTH, :] * dp
    gw_ref[CONV_WIDTH - 1:CONV_WIDTH, :] += _colsum(dp * xwin[HALO:])
    for k in range(CONV_WIDTH - 1):
        sft = CONV_WIDTH - 1 - k
        acc = acc + cw_ref[k:k + 1, :] * _shift_up(dwin, sft)[0:tm]
        gw_ref[k:k + 1, :] += _colsum(dp * _shift_down(xwin, sft)[HALO:])
    gb_ref[...] += _colsum(dp)
    return acc


def _ml_pre_bwd(dqkv, u, conv_w, conv_b, wqkv_b, du):
    s_len = u.shape[0]
    d = conv_w.shape[1]
    _, heads, hd, _ = wqkv_b.shape
    tm = _tile(s_len, 256)
    per = tm // HALO
    nt = s_len // tm

    def body(dqkv_ref, x_ref, xp_ref, cw_ref, cb_ref, w_ref, _, dx_ref, gw_ref, gcw_ref, gcb_ref, later, dps, dxs):
        i = pl.program_id(0)

        @pl.when(i == 0)
        def _():
            gw_ref[...] = jnp.zeros_like(gw_ref)
            gcw_ref[...] = jnp.zeros_like(gcw_ref)
            gcb_ref[...] = jnp.zeros_like(gcb_ref)
            later[...] = jnp.zeros_like(later)

        prev = jnp.where(i == nt - 1, 0.0, xp_ref[...])
        xm = x_ref[...]
        xwin = jnp.concatenate([prev, xm], axis=0)
        pre = _conv_fwd(xwin, cw_ref, cb_ref)
        sg = _sigmoid(pre)
        xcb = _bf(pre * sg)
        xmb = _bf(xm)
        for h in range(heads):
            hs = slice(h * hd, (h + 1) * hd)
            dqh, dkh, dvh = dqkv_ref[0, :, hs], dqkv_ref[1, :, hs], dqkv_ref[2, :, hs]
            dxc = _dot_nt(dqh, w_ref[0, h]) + _dot_nt(dkh, w_ref[1, h])
            dps[:, hs] = dxc * _dsilu(pre[:, hs], sg[:, hs])
            dxs[:, hs] = _dot_nt(dvh, w_ref[2, h])
            gw_ref[0, h] += _dot_tn(xcb[:, hs], dqh)
            gw_ref[1, h] += _dot_tn(xcb[:, hs], dkh)
            gw_ref[2, h] += _dot_tn(xmb[:, hs], dvh)
        dx_ref[0] = _bf(_conv_bwd_tile(dps[...], later, xwin, cw_ref, gcw_ref, gcb_ref) + dxs[...])

    rev = lambda i: nt - 1 - i
    vec = pl.BlockSpec((1, d), lambda i: (0, 0))
    cwb = pl.BlockSpec((CONV_WIDTH, d), lambda i: (0, 0))
    whole4 = pl.BlockSpec(wqkv_b.shape, lambda i: (0, 0, 0, 0))
    return _pcall(
        body, name="ml_pre_bwd", grid=(nt,),
        in_specs=[pl.BlockSpec((3, tm, d), lambda i: (0, rev(i), 0)), pl.BlockSpec((tm, d), lambda i: (rev(i), 2)),
                  pl.BlockSpec((HALO, d), lambda i: (jnp.maximum(rev(i) * per - 1, 0), 2)),
                  cwb, vec, whole4, pl.BlockSpec(memory_space=pl.ANY)],
        out_specs=[pl.BlockSpec((1, tm, d), lambda i: (DU_PLANE[2], rev(i), 0)), whole4, cwb, vec],
        out_shape=[jax.ShapeDtypeStruct(du.shape, BF16), jax.ShapeDtypeStruct(wqkv_b.shape, F32),
                   jax.ShapeDtypeStruct((CONV_WIDTH, d), F32), jax.ShapeDtypeStruct((1, d), F32)],
        scratch_shapes=[pltpu.VMEM((HALO, d), F32), pltpu.VMEM((tm, d), F32), pltpu.VMEM((tm, d), F32)],
        input_output_aliases={6: 0},
        compiler_params=_seq(),
    )(dqkv, u, u, conv_w, conv_b, wqkv_b, du)


def _rg_bwd(d_ycat, u, hh, conv_w, conv_b, wa_b, ba, wx_b, bx, lam, du):
    s_len = u.shape[0]
    d = conv_w.shape[1]
    heads, hd, _ = wa_b.shape
    tm = _tile(s_len, 256)
    per = tm // HALO
    nt = s_len // tm

    def body(dy_ref, x_ref, xp_ref, z_ref, hh_ref, hp_ref, cw_ref, cb_ref, wa_ref, ba_ref, wx_ref, bx_ref, lam_ref, _,
             du_ref, gwa_ref, gwx_ref, gba_ref, gbx_ref, glam_ref, gcw_ref, gcb_ref, carry, gbuf, later, dxcs):
        i = pl.program_id(0)
        first = i == nt - 1

        @pl.when(i == 0)
        def _():
            carry[...] = jnp.zeros_like(carry)
            later[...] = jnp.zeros_like(later)
            gwa_ref[...] = jnp.zeros_like(gwa_ref)
            gwx_ref[...] = jnp.zeros_like(gwx_ref)
            gba_ref[...] = jnp.zeros_like(gba_ref)
            gbx_ref[...] = jnp.zeros_like(gbx_ref)
            glam_ref[...] = jnp.zeros_like(glam_ref)
            gcw_ref[...] = jnp.zeros_like(gcw_ref)
            gcb_ref[...] = jnp.zeros_like(gcb_ref)

        prev = jnp.where(first, 0.0, xp_ref[...])
        xwin = jnp.concatenate([prev, x_ref[...]], axis=0)
        xc = _conv_fwd(xwin, cw_ref, cb_ref)
        r, ig, sp, log_a, a, mult = _rg_gates(xc, wa_ref, ba_ref, wx_ref, bx_ref, lam_ref)
        z = z_ref[...]
        sgz = _sigmoid(z)
        dy = dy_ref[0]
        hh_v = hh_ref[...]
        du_ref[1] = _bf(dy * hh_v * _dsilu(z, sgz))
        dhh = dy * (z * sgz)
        rows = lax.broadcasted_iota(jnp.int32, a.shape, 0)
        coef = jnp.where(rows == tm - 1, carry[1:2, :], _shift_up(a, 1))
        ca, cu = _scan_groups(coef, dhh, reverse=True)
        c = carry[0:1, :]
        for j in range(per - 1, -1, -1):
            blk = ca[j * 8:(j + 1) * 8] * c + cu[j * 8:(j + 1) * 8]
            gbuf[j * 8:(j + 1) * 8, :] = blk
            c = blk[0:1]
        carry[0:1, :] = c
        carry[1:2, :] = a[0:1]
        g = gbuf[...]
        hprev_tile = jnp.where(first, 0.0, hp_ref[...])
        hprev = _shift_down(jnp.concatenate([hprev_tile, hh_v], axis=0), 1)[HALO:]
        da = g * hprev
        gx_ = g * xc
        d_mult = gx_ * ig
        d_ig = gx_ * mult
        dxc = g * mult * ig
        a2 = jnp.exp(2.0 * log_a)
        dlog_a = da * a - d_mult * (a2 / mult)
        d_r = dlog_a * ((-RG_C) * sp)
        glam_ref[...] += _colsum(dlog_a * ((-RG_C) * r)) * (-_sigmoid(-lam_ref[...]))
        d_ga = d_r * r * (1.0 - r)
        d_gx = d_ig * ig * (1.0 - ig)
        gba_ref[...] += _colsum(d_ga)
        gbx_ref[...] += _colsum(d_gx)
        xb = _bf(xc)
        dgab = _bf(d_ga)
        dgxb = _bf(d_gx)
        for h in range(heads):
            hs = slice(h * hd, (h + 1) * hd)
            dxcs[:, hs] = dxc[:, hs] + _dot_nt(dgab[:, hs], wa_ref[h]) + _dot_nt(dgxb[:, hs], wx_ref[h])
            gwa_ref[h] += _dot_tn(xb[:, hs], dgab[:, hs])
            gwx_ref[h] += _dot_tn(xb[:, hs], dgxb[:, hs])
        du_ref[0] = _bf(_conv_bwd_tile(dxcs[...], later, xwin, cw_ref, gcw_ref, gcb_ref))

    assert DU_PLANE[0] % 2 == 0 and DU_PLANE[1] == DU_PLANE[0] + 1
    rev = lambda i: nt - 1 - i
    row = pl.BlockSpec((tm, d), lambda i: (rev(i), 0))
    halo_prev = lambda col: pl.BlockSpec((HALO, d), lambda i: (jnp.maximum(rev(i) * per - 1, 0), col))
    vec = pl.BlockSpec((1, d), lambda i: (0, 0))
    cwb = pl.BlockSpec((CONV_WIDTH, d), lambda i: (0, 0))
    whole3 = lambda a: pl.BlockSpec(a.shape, lambda i: (0, 0, 0))
    return _pcall(
        body, name="rg_bwd", grid=(nt,),
        in_specs=[pl.BlockSpec((1, tm, d), lambda i: (0, rev(i), 0)), row, halo_prev(0),
                  pl.BlockSpec((tm, d), lambda i: (rev(i), 1)), row, halo_prev(0),
                  cwb, vec, whole3(wa_b), vec, whole3(wx_b), vec, vec, pl.BlockSpec(memory_space=pl.ANY)],
        out_specs=[pl.BlockSpec((2, tm, d), lambda i: (DU_PLANE[0] // 2, rev(i), 0)), whole3(wa_b), whole3(wa_b),
                   vec, vec, vec, cwb, vec],
        out_shape=[jax.ShapeDtypeStruct(du.shape, BF16), jax.ShapeDtypeStruct(wa_b.shape, F32),
                   jax.ShapeDtypeStruct(wa_b.shape, F32)] + [jax.ShapeDtypeStruct((1, d), F32)] * 3
        + [jax.ShapeDtypeStruct((CONV_WIDTH, d), F32), jax.ShapeDtypeStruct((1, d), F32)],
        scratch_shapes=[pltpu.VMEM((8, d), F32), pltpu.VMEM((tm, d), F32), pltpu.VMEM((HALO, d), F32),
                        pltpu.VMEM((tm, d), F32)],
        input_output_aliases={13: 0},
        compiler_params=_seq(),
    )(d_ycat, u, u, u, hh, hh, conv_w, conv_b, wa_b, ba, wx_b, bx, lam, du)


def _in_bwd(du, w4, x, dxn, g, scale):
    s_len, d = x.shape
    tm = _tile(s_len, 256)
    nsh_chips, _, nsh = w4.shape
    npc = du.shape[0]
    ck = d // 4
    assert nsh % ck == 0 and npc * d == nsh_chips * nsh

    def body(du_ref, w_ref, x_ref, dxn_ref, g_ref, sc_ref, dx_ref, dsh_ref, dsc_ref, dg_ref):
        @pl.when(pl.program_id(0) == 0)
        def _():
            dsh_ref[...] = jnp.zeros_like(dsh_ref)
            dsc_ref[...] = jnp.zeros_like(dsc_ref)
            dg_ref[...] = jnp.zeros_like(dg_ref)

        dh = None
        for q in range(npc * d // ck):
            col = q * ck
            p, pc = col // d, col % d
            s, sc = col // nsh, col % nsh
            t = _dot_nt(du_ref[DU_PLANE[p], :, pc:pc + ck], w_ref[s, :, sc:sc + ck])
            dh = t if dh is None else dh + t
        xv = x_ref[...]
        r = lax.rsqrt(jnp.mean(xv * xv, axis=-1, keepdims=True) + EPS)
        xn = xv * r
        gv = g_ref[...]
        onesc = 1.0 + sc_ref[...]
        dsh_ref[...] += _colsum(dh)
        dsc_ref[...] += _colsum(dh * (xn * gv))
        dg_ref[...] += _colsum(dh * xn * onesc)
        dxh = dh * (gv * onesc)
        dx_ref[...] = dxn_ref[...] + r * (dxh - xn * jnp.mean(dxh * xn, axis=-1, keepdims=True))

    row = pl.BlockSpec((tm, d), lambda i: (i, 0))
    vec = pl.BlockSpec((1, d), lambda i: (0, 0))
    return _pcall(
        body, name="in_bwd", grid=(s_len // tm,),
        in_specs=[pl.BlockSpec((npc, tm, d), lambda i: (0, i, 0)), pl.BlockSpec(w4.shape, lambda i: (0, 0, 0)), row, row,
                  vec, vec],
        out_specs=[row, vec, vec, vec],
        out_shape=[jax.ShapeDtypeStruct((s_len, d), F32)] + [jax.ShapeDtypeStruct((1, d), F32)] * 3,
        compiler_params=_seq(),
    )(du, w4, x, dxn, g, scale)


def _layer_fwd(x, p):
    h_b, u = _ln_inproj(x, p["norm_g"], p["scale"], p["shift"], p["w4"])
    hh, ycat = _rg_fwd(u, p["rg_conv_w"], p["rg_conv_b"], p["rg_wa_b"], p["rg_ba"], p["rg_wx_b"], p["rg_bx"],
                       p["rg_lam"])
    qkv, gt, gtt = _ml_pre(u, p["ml_conv_w"], p["ml_conv_b"], p["wqkv_b"], p["wif_b"], p["wift_b"], p["b_if"],
                           p["b_ift"])
    cell, ycat, cst, nst, mst = _mlstm_fwd(qkv, gt, gtt, u, p["ml_g"], ycat)
    y, x_new = _out_proj(ycat, p["w_out_b"], x, p["gate"])
    saved = dict(x=x, h_b=h_b, u=u, hh=hh, qkv=qkv, gt=gt, gtt=gtt, cell=cell, ycat=ycat, cst=cst, nst=nst, mst=mst,
                 y=y)
    return x_new, saved


def _layer_bwd(dxn, p, s, layer, stacks):
    u = s["u"]
    d = dxn.shape[1]
    d_gate, dy_b, d_ycat = _out_bwd(dxn, s["y"], p["gate"], p["w_out_b"])
    stacks["w_out"] = _grad_matmul(s["ycat"], dy_b[None], 2, lambda b: b, lambda b: 0, (2 * d, d), (d, d),
                                   lambda b: (b, 0), layer, stacks["w_out"])
    dqkv, dgt, g_b_if, du, g_ml_g = _mlstm_bwd(s["qkv"], s["gt"], s["gtt"], s["cst"], s["nst"], s["mst"], s["cell"], u,
                                               p["ml_g"], d_ycat, p["wif_b"])
    ng = dgt.shape[1]
    g_w_if = _grad_matmul(s["qkv"], _bf(dgt)[None], 3, lambda b: b, lambda b: 0, (3 * d, ng), (d, ng),
                          lambda b: (b, 0), 0, 1)[0]
    du, g_wqkv, g_ml_cw, g_ml_cb = _ml_pre_bwd(dqkv, u, p["ml_conv_w"], p["ml_conv_b"], p["wqkv_b"], du)
    du, g_wa, g_wx, g_ba, g_bx, g_lam, g_rg_cw, g_rg_cb = _rg_bwd(d_ycat, u, s["hh"], p["rg_conv_w"], p["rg_conv_b"],
                                                                  p["rg_wa_b"], p["rg_ba"], p["rg_wx_b"], p["rg_bx"],
                                                                  p["rg_lam"], du)
    npc = du.shape[0]
    stacks["w_in"] = _grad_matmul(s["h_b"][None], du, npc, lambda b: 0, lambda b: (b + DU_PLANE[0]) % npc,
                                  (d, npc * d), (d, d), lambda b: (0, b), layer, stacks["w_in"])
    dx, d_shift, d_scale, g_norm_g = _in_bwd(du, p["w4"], s["x"], dxn, p["norm_g"], p["scale"])
    grads = dict(norm_g=g_norm_g, rg_conv_w=g_rg_cw, rg_conv_b=g_rg_cb, rg_w_a=g_wa, rg_b_a=g_ba,
                 rg_w_x=g_wx, rg_b_x=g_bx, rg_lambda=g_lam, ml_conv_w=g_ml_cw, ml_conv_b=g_ml_cb, ml_w_qkv=g_wqkv,
                 ml_w_if=g_w_if, ml_b_if=g_b_if, ml_norm_g=g_ml_g)
    return dx, grads, jnp.concatenate([d_shift, d_scale, d_gate], axis=1)


def _trunk_fwd_bwd(x, target, final_g, layers):
    saved = []
    for p in layers:
        x, s = _layer_fwd(x, p)
        saved.append(s)
    dx, g_final, loss = _final_loss(x, final_g, target)
    grads, dmods = [], []
    stacks = dict(w_in=len(layers), w_out=len(layers))
    for layer in reversed(range(len(layers))):
        dx, g, dm = _layer_bwd(dx, layers[layer], saved[layer], layer, stacks)
        grads.append(g)
        dmods.append(dm)
    return loss, dx, g_final, grads[::-1], stacks["w_in"], stacks["w_out"], dmods[::-1]


def _me():
    return lax.axis_index("x"), lax.axis_index("y"), lax.axis_index("c")


def _remote(src, dst, send_sem, recv_sem, to):
    return pltpu.make_async_remote_copy(src_ref=src, dst_ref=dst, send_sem=send_sem, recv_sem=recv_sem,
                                        device_id=to, device_id_type=MESH)


def _all_gather8(blocks, space):
    n = len(blocks)

    def body(*refs):
        x_refs, out_refs = refs[:n], refs[n:2 * n]
        send_sems, recv_sems, local_sems = refs[2 * n:]
        x, y, c = _me()
        me, sibling = (x, y, c), (x, y, 1 - c)
        chips = [(1 - x, y), (x, 1 - y), (1 - x, 1 - y)]

        def rows(i, px, py, pc):
            m_per = blocks[i].shape[0]
            return out_refs[i].at[pl.ds((4 * px + 2 * py + pc) * m_per, m_per), :]

        def copy(i, k, blk, to, src=None):
            return _remote(rows(i, *blk) if src is None else src, rows(i, *blk), send_sems.at[7 * i + k],
                           recv_sems.at[7 * i + k], to)

        mine = [pltpu.make_async_copy(x_refs[i], rows(i, *me), local_sems.at[i]) for i in range(n)]
        first = []
        for i in range(n):
            first.append(copy(i, 0, me, sibling, src=x_refs[i]))
            first += [copy(i, 1 + j, me, (*chip, c), src=x_refs[i]) for j, chip in enumerate(chips)]
        for cp in mine + first:
            cp.start()
        passed = []
        for j, chip in enumerate(chips):
            for i in range(n):
                copy(i, 1 + j, (*chip, c), me).wait_recv()
                passed.append(copy(i, 4 + j, (*chip, c), sibling))
                passed[-1].start()
        for i in range(n):
            copy(i, 0, sibling, me).wait_recv()
            for j, chip in enumerate(chips):
                copy(i, 4 + j, (*chip, 1 - c), me).wait_recv()
        for cp in first + passed:
            cp.wait_send()
        for cp in mine:
            cp.wait()

    spec = pl.BlockSpec(memory_space=space)
    return _pcall(
        body, name="all_gather8",
        out_shape=[jax.ShapeDtypeStruct((8 * b.shape[0], b.shape[1]), b.dtype) for b in blocks],
        in_specs=[spec] * n, out_specs=[spec] * n,
        scratch_shapes=[pltpu.SemaphoreType.DMA((7 * n,)), pltpu.SemaphoreType.DMA((7 * n,)),
                        pltpu.SemaphoreType.DMA((n,))],
    )(*blocks)


def _sib_halves(g_in, g_out, slabs):
    depth, d, n4 = g_in.shape
    n = n4 // 4
    ns = len(slabs)

    def body(*refs):
        gi, go = refs[0], refs[1]
        sl = refs[2:2 + ns]
        ri, ro = refs[2 + ns], refs[3 + ns]
        rs = refs[4 + ns:4 + 2 * ns]
        send_sems, recv_sems = refs[4 + 2 * ns:]
        x, y, c = _me()
        o = 1 - c
        pairs = [(gi.at[pl.ds(0, depth), pl.ds(o * (d // 2), d // 2), pl.ds(s * n, n)], ri.at[pl.ds(0, depth), s])
                 for s in range(4)]
        pairs.append((go.at[pl.ds(0, depth), pl.ds(0, 4), o], ro))
        pairs += [(sl[i].at[o], rs[i]) for i in range(ns)]
        copies = [_remote(src, dst, send_sems.at[k], recv_sems.at[k], (x, y, o)) for k, (src, dst) in enumerate(pairs)]
        for cp in copies:
            cp.start()
        for cp in copies:
            cp.wait_recv()
        for cp in copies:
            cp.wait_send()

    hbm = pl.BlockSpec(memory_space=pltpu.HBM)
    ncp = 5 + ns
    return _pcall(
        body, name="sib_halves",
        out_shape=[jax.ShapeDtypeStruct((depth, 4, d // 2, n), g_in.dtype),
                   jax.ShapeDtypeStruct(g_out.shape[:2] + g_out.shape[3:], g_out.dtype)]
        + [jax.ShapeDtypeStruct(s.shape[1:], s.dtype) for s in slabs],
        in_specs=[hbm] * (2 + ns), out_specs=[hbm] * (2 + ns),
        scratch_shapes=[pltpu.SemaphoreType.DMA((ncp,)), pltpu.SemaphoreType.DMA((ncp,))],
    )(g_in, g_out, *slabs)


def _sib_fill(boths):
    n = len(boths)

    def body(*refs):
        dst = refs[n:2 * n]
        send_sems, recv_sems = refs[2 * n:]
        x, y, c = _me()
        view = lambda i: dst[i].at[pl.ds(0, boths[i].shape[0]), c]
        copies = [_remote(view(i), view(i), send_sems.at[i], recv_sems.at[i], (x, y, 1 - c)) for i in range(n)]
        for cp in copies:
            cp.start()
        for cp in copies:
            cp.wait_recv()
        for cp in copies:
            cp.wait_send()

    hbm = pl.BlockSpec(memory_space=pltpu.HBM)
    return _pcall(
        body, name="sib_fill",
        out_shape=[jax.ShapeDtypeStruct(b.shape, b.dtype) for b in boths],
        in_specs=[hbm] * n, out_specs=[hbm] * n, input_output_aliases={i: i for i in range(n)},
        scratch_shapes=[pltpu.SemaphoreType.DMA((n,)), pltpu.SemaphoreType.DMA((n,))],
    )(*boths)


def _chip_exchange(arrs):
    n = len(arrs)

    def body(*refs):
        src, dst = refs[:n], refs[n:2 * n]
        send_sems, recv_sems = refs[2 * n:]
        x, y, c = _me()
        me_s = 2 * x + y
        chips = [(1 - x, y), (x, 1 - y), (1 - x, 1 - y)]
        copies = [_remote(src[i].at[2 * px + py], dst[i].at[me_s], send_sems.at[3 * i + k], recv_sems.at[3 * i + k],
                          (px, py, c))
                  for i in range(n) for k, (px, py) in enumerate(chips)]
        for cp in copies:
            cp.start()
        for cp in copies:
            cp.wait_recv()
        for cp in copies:
            cp.wait_send()

    hbm = pl.BlockSpec(memory_space=pltpu.HBM)
    return _pcall(
        body, name="chip_exchange",
        out_shape=[jax.ShapeDtypeStruct(a.shape, a.dtype) for a in arrs],
        in_specs=[hbm] * n, out_specs=[hbm] * n,
        scratch_shapes=[pltpu.SemaphoreType.DMA((3 * n,)), pltpu.SemaphoreType.DMA((3 * n,))],
    )(*arrs)


def _row_tile(rows, cap=4096, mult=16):
    best = None
    for t in range(mult, min(rows, cap) + 1, mult):
        if rows % t == 0:
            best = t
    return rows if best is None else best


def _pair_sum(half, own, own_spec, got, got_spec, out_shape, out_spec, grid):
    def body(_, a_ref, b_ref, o_ref):
        o_ref[...] = (a_ref[...] + b_ref[...].astype(F32)).astype(o_ref.dtype)

    return _pcall(
        body, name="pair_sum",
        grid_spec=pltpu.PrefetchScalarGridSpec(num_scalar_prefetch=1, grid=grid, in_specs=[own_spec, got_spec],
                                               out_specs=out_spec),
        out_shape=out_shape, compiler_params=_seq(len(grid)))(half, own, got)


def _chip_sum(ids, part, met, fill):
    _, depth, rows, n = part.shape
    tr = _row_tile(rows, cap=max(16, (1 << 18) // n))

    def body(_, own_ref, a_ref, b_ref, c_ref, o_ref):
        acc = own_ref[...].astype(F32) + a_ref[...].astype(F32)
        acc = acc + b_ref[...].astype(F32)
        o_ref[...] = acc + c_ref[...].astype(F32)

    blk = (None, None, tr, n)
    other = lambda k: pl.BlockSpec(blk, lambda l, j, ids: ((ids[0] + k) % 4, l, j, 0))
    return _pcall(
        body, name="chip_sum",
        grid_spec=pltpu.PrefetchScalarGridSpec(
            num_scalar_prefetch=1, grid=(depth, rows // tr),
            in_specs=[pl.BlockSpec(blk, lambda l, j, ids: (ids[0], l, j, 0)), other(1), other(2), other(3)],
            out_specs=pl.BlockSpec(blk, lambda l, j, ids: (l, ids[1] if fill else 0, j, 0))),
        out_shape=jax.ShapeDtypeStruct((depth, 2 if fill else 1, rows, n), F32),
        compiler_params=_seq(2))(ids, part, met, met, met)


def _ada_mod(c_all, w_ada, b_ada_cols):
    depth, d, n = w_ada.shape
    nb = c_all.shape[0]

    def body(c_ref, w_ref, b_ref, o_ref):
        cv = c_ref[...]
        ca = _bf(cv * _sigmoid(cv))
        o_ref[0] = _dot(ca, _bf(w_ref[0])) + b_ref[0]

    return _pcall(body, name="ada_mod", grid=(depth,),
                  in_specs=[pl.BlockSpec((nb, d), lambda l: (0, 0)), pl.BlockSpec((1, d, n), lambda l: (l, 0, 0)),
                            pl.BlockSpec((1, 1, n), lambda l: (l, 0, 0))],
                  out_specs=pl.BlockSpec((1, nb, n), lambda l: (l, 0, 0)),
                  out_shape=jax.ShapeDtypeStruct((depth, nb, n), F32), compiler_params=_seq())(c_all, w_ada, b_ada_cols)


def _ada_grad(c_all, dmod_cols, dmod_all):
    nb, d = c_all.shape
    depth, _, n = dmod_cols.shape
    n_all = dmod_all.shape[2]

    def body(c_ref, dm_ref, da_ref, gw_ref, gb_ref):
        cv = c_ref[...]
        ca = _bf(cv * _sigmoid(cv))
        gw_ref[0] = _dot_tn(ca, _bf(dm_ref[0]))
        gb_ref[0] = _colsum(da_ref[0])

    return _pcall(body, name="ada_grad", grid=(depth,),
                  in_specs=[pl.BlockSpec((nb, d), lambda l: (0, 0)), pl.BlockSpec((1, nb, n), lambda l: (l, 0, 0)),
                            pl.BlockSpec((1, nb, n_all), lambda l: (l, 0, 0))],
                  out_specs=[pl.BlockSpec((1, d, n), lambda l: (l, 0, 0)), pl.BlockSpec((1, 1, n_all), lambda l: (l, 0, 0))],
                  out_shape=[jax.ShapeDtypeStruct((depth, d, n), F32), jax.ShapeDtypeStruct((depth, 1, n_all), F32)],
                  compiler_params=_seq())(c_all, dmod_cols, dmod_all)


def _adamw(w, g, m, v):
    shape = w.shape
    cols = shape[-1]
    rows = w.size // cols
    w2, g2, m2, v2 = (t.reshape(rows, cols) for t in (w, g, m, v))
    tr = _row_tile(rows, cap=max(8, (1 << 18) // cols), mult=8)

    def body(w_ref, g_ref, m_ref, v_ref, d_ref, mo_ref, vo_ref):
        gv = g_ref[...]
        mn = ADAM_B1 * m_ref[...] + (1.0 - ADAM_B1) * gv
        vn = ADAM_B2 * v_ref[...] + (1.0 - ADAM_B2) * (gv * gv)
        m_hat = mn / (1.0 - ADAM_B1 ** ADAM_STEP)
        v_hat = vn / (1.0 - ADAM_B2 ** ADAM_STEP)
        d_ref[...] = -ADAM_LR * (m_hat / (jnp.sqrt(v_hat) + ADAM_EPS) + ADAM_WD * w_ref[...])
        mo_ref[...] = mn
        vo_ref[...] = vn

    blk = pl.BlockSpec((tr, cols), lambda i: (i, 0))
    outs = _pcall(body, name="adamw", grid=(rows // tr,), in_specs=[blk] * 4, out_specs=[blk] * 3,
                  out_shape=[jax.ShapeDtypeStruct((rows, cols), F32)] * 3, compiler_params=_seq())(w2, g2, m2, v2)
    return tuple(o.reshape(shape) for o in outs)


WEIGHTS = ["norm_g", "w_ada", "b_ada", "w_in", "rg_conv_w", "rg_conv_b", "rg_w_a", "rg_b_a", "rg_w_x", "rg_b_x",
           "rg_lambda", "ml_conv_w", "ml_conv_b", "ml_w_q", "ml_w_k", "ml_w_v", "ml_w_if", "ml_b_if", "ml_norm_g",
           "w_out", "final_g"]
SMALL_SHARDED = {"ml_w_qkv": 2, "rg_conv_w": 1, "ml_conv_w": 1, "ml_w_if": 0}
REPLICATED = ["rg_w_a", "rg_w_x", "norm_g", "rg_conv_b", "rg_b_a", "rg_b_x", "rg_lambda", "ml_conv_b", "ml_norm_g",
              "ml_b_if"]
LANES = 128


def _to_pieces(g, axis):
    shp = g.shape
    g = g.reshape(shp[:axis] + (4, 2, shp[axis] // 8) + shp[axis + 1:])
    g = jnp.moveaxis(g, (axis, axis + 1), (0, 1))
    return g.reshape(4, 2, -1)


def _from_pieces(p, shard_shape, axis):
    k = p.shape[0]
    rest = shard_shape[:axis] + (shard_shape[axis] // k,) + shard_shape[axis + 1:]
    t = jnp.moveaxis(p.reshape((k,) + rest), 0, axis)
    return t.reshape(shard_shape)


def _pad_rows(flat, mult):
    n = flat.shape[-1]
    pad = (-n) % mult
    if pad:
        flat = jnp.concatenate([flat, jnp.zeros(flat.shape[:-1] + (pad,), flat.dtype)], axis=-1)
    return flat


def kernel(x, c, norm_g, w_ada, b_ada, w_in, rg_conv_w, rg_conv_b, rg_w_a, rg_b_a, rg_w_x, rg_b_x, rg_lambda, ml_conv_w, ml_conv_b, ml_w_q, ml_w_k, ml_w_v, ml_w_if, ml_b_if, ml_norm_g, w_out, final_g, loss_target, m_norm_g, m_w_ada, m_b_ada, m_w_in, m_rg_conv_w, m_rg_conv_b, m_rg_w_a, m_rg_b_a, m_rg_w_x, m_rg_b_x, m_rg_lambda, m_ml_conv_w, m_ml_conv_b, m_ml_w_q, m_ml_w_k, m_ml_w_v, m_ml_w_if, m_ml_b_if, m_ml_norm_g, m_w_out, m_final_g, v_norm_g, v_w_ada, v_b_ada, v_w_in, v_rg_conv_w, v_rg_conv_b, v_rg_w_a, v_rg_b_a, v_rg_w_x, v_rg_b_x, v_rg_lambda, v_ml_conv_w, v_ml_conv_b, v_ml_w_q, v_ml_w_k, v_ml_w_v, v_ml_w_if, v_ml_b_if, v_ml_norm_g, v_w_out, v_final_g):
    given = dict(locals())
    ax, ay, ac = lax.axis_index("x"), lax.axis_index("y"), lax.axis_index("c")
    chip = 2 * ax + ay
    me = 2 * chip + ac
    depth, d = norm_g.shape
    n_ada = w_ada.shape[2]
    pick = lambda a, i, axis=0: lax.dynamic_index_in_dim(a, i, axis, keepdims=False)

    convs = jnp.stack([rg_conv_w, ml_conv_w])
    n_conv = 2 * depth * CONV_WIDTH // 4
    blk = jnp.concatenate([c, convs.reshape(n_conv, d), jnp.zeros((8 - 1 - n_conv, d), F32)], axis=0)
    g0 = _all_gather8([blk], pltpu.VMEM)[0].reshape(8, 8, d)
    c_all = g0[:, 0, :]
    conv_full = g0[0::2, 1:1 + n_conv].reshape(4, 2, depth, CONV_WIDTH, d // 4)
    conv_full = conv_full.transpose(1, 2, 3, 0, 4).reshape(2, depth, CONV_WIDTH, d)

    b_cols = lax.dynamic_slice_in_dim(b_ada, chip * n_ada, n_ada, axis=1)[:, None, :]
    mod_part = _ada_mod(c_all, w_ada, b_cols)
    g1 = _all_gather8([mod_part.transpose(1, 0, 2).reshape(8, depth * n_ada)], pltpu.VMEM)[0]
    g1 = g1.reshape(8, 8, depth, n_ada)[0::2]
    mod_me = pick(g1.transpose(1, 2, 0, 3).reshape(8, depth, 4 * n_ada), me)

    def half_of(w, axis):
        n = w.shape[axis] // 2
        return lax.dynamic_slice_in_dim(w, ac * n, n, axis).astype(BF16)

    n_sh = w_in.shape[2]
    heads, hd_cut, hd = ml_w_q.shape[1:]
    blocks = []
    for l in range(depth):
        wqkv = jnp.stack([ml_w_q[l], ml_w_k[l], ml_w_v[l]])
        blocks += [half_of(w_in[l], 0), half_of(w_out[l], 0), half_of(wqkv, 2).reshape(-1, hd), half_of(ml_w_if[l], 0)]
    gathered = _all_gather8(blocks, pltpu.HBM)
    layers = []
    for l in range(depth):
        w4, w_out_b, wqkv_g, wif = gathered[4 * l:4 * l + 4]
        wqkv_b = _from_pieces(wqkv_g.reshape(8, -1), (3, heads, hd, hd), 2)
        layers.append(dict(
            norm_g=norm_g[l][None], shift=mod_me[l, 0:d][None], scale=mod_me[l, d:2 * d][None],
            gate=mod_me[l, 2 * d:3 * d][None], w4=w4.reshape(4, d, n_sh),
            rg_conv_w=conv_full[0, l], rg_conv_b=rg_conv_b[l][None], rg_wa_b=_bf(rg_w_a[l]), rg_ba=rg_b_a[l][None],
            rg_wx_b=_bf(rg_w_x[l]), rg_bx=rg_b_x[l][None], rg_lam=rg_lambda[l][None],
            ml_conv_w=conv_full[1, l], ml_conv_b=ml_conv_b[l][None], wqkv_b=wqkv_b, wif_b=wif, wift_b=wif.T,
            b_if=ml_b_if[l][None], b_ift=ml_b_if[l][:, None], ml_g=ml_norm_g[l][None], w_out_b=w_out_b))

    loss, dx, g_final, grads, g_in, g_out, dmods = _trunk_fwd_bwd(x[0], loss_target[0], final_g[None], layers)

    dm_blk = jnp.concatenate(dmods + [jnp.zeros((8 - depth, 3 * d), F32)], axis=0)
    dm_all = _all_gather8([dm_blk], pltpu.VMEM)[0].reshape(8, 8, 3 * d)[:, :depth].transpose(1, 0, 2)
    dm_cols = lax.dynamic_slice_in_dim(dm_all, chip * n_ada, n_ada, axis=2)
    g_w_ada, g_b_ada = _ada_grad(c_all, dm_cols, dm_all)

    r_out = g_out.shape[1] // 8
    g_out5 = g_out.reshape(depth, 4, 2, r_out, d)
    sm = jnp.concatenate([_to_pieces(grads[l][name], axis) for l in range(depth) for name, axis in SMALL_SHARDED.items()],
                         axis=-1)
    sm = _pad_rows(sm, 16 * LANES)
    n_sm = sm.shape[-1] // LANES
    sm = sm.transpose(1, 0, 2).reshape(2, 4 * n_sm, LANES)
    rep = [grads[l][name].reshape(-1) for l in range(depth) for name in REPLICATED[:-1]]
    rep += [_pad_rows(grads[l]["ml_b_if"].reshape(-1), LANES) for l in range(depth)]
    rep += [g_final.reshape(-1), loss.reshape(-1)]
    rep = _pad_rows(jnp.concatenate(rep), 8 * 8 * LANES)
    n_rep = rep.shape[0] // (8 * LANES)
    rep = rep.reshape(4, 2, n_rep, LANES).transpose(1, 0, 2, 3).reshape(2, 4 * n_rep, LANES)
    got_in, got_out, got_sm, got_rep = _sib_halves(g_in, g_out5, [sm, rep])
    half = ac.reshape(1)
    part_in = _pair_sum(
        half, g_in, pl.BlockSpec((None, d // 2, n_sh), lambda l, s, h: (l, h[0], s)),
        got_in, pl.BlockSpec((None, None, d // 2, n_sh), lambda l, s, h: (l, s, 0, 0)),
        jax.ShapeDtypeStruct((4, depth, d // 2, n_sh), BF16),
        pl.BlockSpec((None, None, d // 2, n_sh), lambda l, s, h: (s, l, 0, 0)), (depth, 4))
    part_out = _pair_sum(
        half, g_out5, pl.BlockSpec((None, None, None, r_out, d), lambda l, s, h: (l, s, h[0], 0, 0)),
        got_out, pl.BlockSpec((None, None, r_out, d), lambda l, s, h: (l, s, 0, 0)),
        jax.ShapeDtypeStruct((4, depth, r_out, d), BF16),
        pl.BlockSpec((None, None, r_out, d), lambda l, s, h: (s, l, 0, 0)), (depth, 4))

    def slab_sum(slab, got, rows, dtype):
        blk = pl.BlockSpec((rows, LANES), lambda s, h: (s, 0))
        return _pair_sum(half, slab, pl.BlockSpec((None, rows, LANES), lambda s, h: (h[0], s, 0)), got, blk,
                         jax.ShapeDtypeStruct((4 * rows, LANES), dtype), blk, (4,)).reshape(4, rows, LANES)

    part_sm = slab_sum(sm, got_sm, n_sm, BF16)
    part_rep = slab_sum(rep, got_rep, n_rep, F32)
    part_sm, part_rep = part_sm[:, None], part_rep[:, None]
    met_in, met_out, met_sm, met_rep = _chip_exchange([part_in, part_out, part_sm, part_rep])
    ids = jnp.stack([chip, ac])
    both_in, both_out, both_sm = _sib_fill([_chip_sum(ids, part_in, met_in, True), _chip_sum(ids, part_out, met_out, True),
                                            _chip_sum(ids, part_sm, met_sm, True)])
    red_rep = _chip_sum(ids, part_rep, met_rep, False).reshape(n_rep, LANES)
    rep_all = _all_gather8([red_rep], pltpu.VMEM)[0].reshape(-1)

    g = dict(w_ada=g_w_ada, b_ada=g_b_ada.reshape(b_ada.shape), w_in=both_in.reshape(w_in.shape),
             w_out=both_out.reshape(w_out.shape))
    shard = both_sm.reshape(2, -1)
    off = 0
    per_layer = {name: [] for name in SMALL_SHARDED}
    for l in range(depth):
        for name, axis in SMALL_SHARDED.items():
            shp = (3,) + ml_w_q.shape[1:] if name == "ml_w_qkv" else given[name].shape[1:]
            n = grads[l][name].size // 8
            per_layer[name].append(_from_pieces(shard[:, off:off + n], shp, axis))
            off += n
    for name in SMALL_SHARDED:
        g[name] = jnp.stack(per_layer[name])
    for i, name in enumerate(["ml_w_q", "ml_w_k", "ml_w_v"]):
        g[name] = g["ml_w_qkv"][:, i]
    off = 0
    per_layer = {name: [] for name in REPLICATED}
    for l in range(depth):
        for name in REPLICATED[:-1]:
            n = given[name][l].size
            per_layer[name].append(rep_all[off:off + n].reshape(given[name].shape[1:]))
            off += n
    for l in range(depth):
        n = given["ml_b_if"][l].size
        per_layer["ml_b_if"].append(rep_all[off:off + n])
        off += LANES
    for name in REPLICATED:
        g[name] = jnp.stack(per_layer[name])
    g["final_g"] = rep_all[off:off + d]
    loss_all = rep_all[off + d]

    deltas, new_m, new_v = [], [], []
    for name in WEIGHTS:
        dl, mn, vn = _adamw(given[name], g[name], given["m_" + name], given["v_" + name])
        deltas.append(dl)
        new_m.append(mn)
        new_v.append(vn)
    return (loss_all, dx[None], *[g[name] for name in WEIGHTS], *deltas, *new_m, *new_v)
```

```python
import functools
from typing import NamedTuple

import jax
import jax.numpy as jnp
from jax import lax
from jax.experimental import pallas as pl
from jax.experimental.pallas import tpu as pltpu

F32 = jnp.float32
BF16 = jnp.bfloat16

EPS = 1e-6
RG_C = 8.0
CONV_WIDTH = 4
ML_CHUNK = 128
HALO = 8
ADAM_LR = 0.001
ADAM_B1 = 0.9
ADAM_B2 = 0.999
ADAM_EPS = 1e-08
ADAM_WD = 0.01
ADAM_STEP = 10
MESH = pl.DeviceIdType.MESH


def _pcall(body, **kw):
    return pl.pallas_call(body, **kw)


class Ride(NamedTuple):
    srcs: list
    dst_shapes: list
    sliced: bool


def _pcall_ride(body, ride, *, grid, in_specs, out_specs, out_shape, args, scratch_shapes=(), **kw):
    n_in, n_out, n_scr = len(in_specs), len(out_specs), len(scratch_shapes)
    if ride is None:
        res = _pcall(body, grid=grid, in_specs=in_specs, out_specs=out_specs, out_shape=out_shape,
                     scratch_shapes=list(scratch_shapes), **kw)(*args)
        return res, []
    nr = len(ride.srcs)

    def riding(*refs):
        ins, rsrc = refs[:n_in], refs[n_in:n_in + nr]
        outs, rdst = refs[n_in + nr:n_in + nr + n_out], refs[n_in + nr + n_out:n_in + 2 * nr + n_out]
        scr = refs[n_in + 2 * nr + n_out:n_in + 2 * nr + n_out + n_scr]
        send_sems, recv_sems, local_sems = refs[n_in + 2 * nr + n_out + n_scr:]
        x, y, c = _me()
        me_s = 2 * x + y
        chips = [(1 - x, y), (x, 1 - y), (1 - x, 1 - y)]
        copies, local = [], []
        for i in range(nr):
            for k, (px, py) in enumerate(chips):
                src = rsrc[i].at[2 * px + py] if ride.sliced else rsrc[i]
                dst = rdst[i].at[me_s] if ride.sliced else rdst[i].at[me_s, c]
                copies.append(_remote(src, dst, send_sems.at[3 * i + k], recv_sems.at[3 * i + k], (px, py, c)))
            if not ride.sliced:
                local.append(pltpu.make_async_copy(rsrc[i], rdst[i].at[me_s, c], local_sems.at[i]))
        first = functools.reduce(jnp.logical_and, [pl.program_id(a) == 0 for a in range(len(grid))])
        last = functools.reduce(jnp.logical_and, [pl.program_id(a) == grid[a] - 1 for a in range(len(grid))])

        @pl.when(first)
        def _():
            for cp in copies + local:
                cp.start()

        body(*ins, *outs, *scr)

        @pl.when(last)
        def _():
            for cp in copies:
                cp.wait_recv()
            for cp in copies:
                cp.wait_send()
            for cp in local:
                cp.wait()

    hbm = pl.BlockSpec(memory_space=pltpu.HBM)
    res = _pcall(
        riding, grid=grid, in_specs=list(in_specs) + [hbm] * nr, out_specs=list(out_specs) + [hbm] * nr,
        out_shape=list(out_shape) + list(ride.dst_shapes),
        scratch_shapes=list(scratch_shapes) + [pltpu.SemaphoreType.DMA((3 * nr,)), pltpu.SemaphoreType.DMA((3 * nr,)),
                                               pltpu.SemaphoreType.DMA((nr,))], **kw)(*args, *ride.srcs)
    return res[:n_out], res[n_out:]


def _seq(n=1):
    return pltpu.CompilerParams(dimension_semantics=("arbitrary",) * n)


def _dot(a, b):
    return jnp.dot(a, b, preferred_element_type=F32)


def _dot_nt(a, b):
    return lax.dot_general(a, b, (((1,), (1,)), ((), ())), preferred_element_type=F32)


def _dot_tn(a, b):
    return lax.dot_general(a, b, (((0,), (0,)), ((), ())), preferred_element_type=F32)


def _bf(x):
    return x.astype(BF16)


def _sigmoid(x):
    return 1.0 / (1.0 + jnp.exp(-x))


def _log1p(z):
    u = 1.0 + z
    return jnp.where(u == 1.0, z, jnp.log(u) * (z / jnp.where(u == 1.0, 1.0, u - 1.0)))


def _softplus(x):
    return jnp.maximum(x, 0.0) + _log1p(jnp.exp(-jnp.abs(x)))


def _log_sigmoid(x):
    return -_softplus(-x)


def _expm1(x):
    small = x * (1.0 + x * (0.5 + x * (1.0 / 6.0 + x * (1.0 / 24.0 + x * (1.0 / 120.0)))))
    return jnp.where(jnp.abs(x) < 0.03, small, jnp.exp(x) - 1.0)


def _dsilu(x, s):
    return s * (1.0 + x * (1.0 - s))


def _rowsum(x):
    return jnp.sum(x, axis=1, keepdims=True)


def _colsum(x):
    return jnp.sum(x, axis=0, keepdims=True)


def _shift_down(win, s):
    return win if s == 0 else pltpu.roll(win, s, 0)


def _shift_up(win, s):
    return win if s == 0 else pltpu.roll(win, win.shape[0] - s, 0)


def _conv_fwd(win, w_ref, b_ref):
    acc = b_ref[...] + w_ref[CONV_WIDTH - 1:CONV_WIDTH, :] * win[HALO:]
    for k in range(CONV_WIDTH - 1):
        acc = acc + w_ref[k:k + 1, :] * _shift_down(win, CONV_WIDTH - 1 - k)[HALO:]
    return acc


def _split3(x):
    hi = _bf(x)
    r1 = x - hi.astype(F32)
    mid = _bf(r1)
    lo = _bf(r1 - mid.astype(F32))
    return hi, mid, lo


def _tri_dot_left(tri, x):
    hi, mid, lo = _split3(x)
    return _dot(tri, hi) + _dot(tri, mid) + _dot(tri, lo)


def _tri_dot_right(x, tri):
    hi, mid, lo = _split3(x)
    return _dot(hi, tri) + _dot(mid, tri) + _dot(lo, tri)


def _tile(n, want):
    t = min(n, want)
    assert n % t == 0
    return t


def _ln_inproj(x, g, scale, shift, w4):
    s_len, d = x.shape
    nj, _, nsh = w4.shape
    tm = _tile(s_len, 512)

    def body(x_ref, g_ref, sc_ref, sh_ref, w_ref, h_ref, u_ref, hs):
        @pl.when(pl.program_id(1) == 0)
        def _():
            xv = x_ref[...]
            r = lax.rsqrt(jnp.mean(xv * xv, axis=-1, keepdims=True) + EPS)
            hv = (xv * r * g_ref[...]) * (1.0 + sc_ref[...]) + sh_ref[...]
            hs[...] = _bf(hv)
            h_ref[...] = hs[...]

        u_ref[...] = _dot(hs[...], w_ref[0])

    vec = pl.BlockSpec((1, d), lambda i, j: (0, 0))
    return _pcall(
        body, name="ln_inproj", grid=(s_len // tm, nj),
        in_specs=[pl.BlockSpec((tm, d), lambda i, j: (i, 0)), vec, vec, vec,
                  pl.BlockSpec((1, d, nsh), lambda i, j: (j, 0, 0))],
        out_specs=[pl.BlockSpec((tm, d), lambda i, j: (i, 0)), pl.BlockSpec((tm, nsh), lambda i, j: (i, j))],
        out_shape=[jax.ShapeDtypeStruct((s_len, d), BF16), jax.ShapeDtypeStruct((s_len, nj * nsh), F32)],
        scratch_shapes=[pltpu.VMEM((tm, d), BF16)],
        compiler_params=_seq(2),
    )(x, g, scale, shift, w4)


def _rg_gates(xc, wa_ref, ba_ref, wx_ref, bx_ref, lam_ref):
    heads, hd, _ = wa_ref.shape
    xb = _bf(xc)
    ga = jnp.concatenate([_dot(xb[:, h * hd:(h + 1) * hd], wa_ref[h]) for h in range(heads)], axis=1) + ba_ref[...]
    gx = jnp.concatenate([_dot(xb[:, h * hd:(h + 1) * hd], wx_ref[h]) for h in range(heads)], axis=1) + bx_ref[...]
    r = _sigmoid(ga)
    ig = _sigmoid(gx)
    sp = _softplus(-lam_ref[...])
    log_a = (-RG_C) * r * sp
    a = jnp.exp(log_a)
    mult = jnp.sqrt(-_expm1(2.0 * log_a))
    return r, ig, sp, log_a, a, mult


def _scan_groups(a, u, reverse):
    n = a.shape[0]
    row = lax.broadcasted_iota(jnp.int32, a.shape, 0) & 7
    for k in (1, 2, 4):
        if reverse:
            a_sh, u_sh = _shift_up(a, k), _shift_up(u, k)
            ok = row < 8 - k
        else:
            a_sh, u_sh = _shift_down(a, k), _shift_down(u, k)
            ok = row >= k
        u = jnp.where(ok, a * u_sh + u, u)
        a = jnp.where(ok, a * a_sh, a)
    del n
    return a, u


def _rg_fwd(u, conv_w, conv_b, wa_b, ba, wx_b, bx, lam, ride=None):
    s_len = u.shape[0]
    d = conv_w.shape[1]
    tm = _tile(s_len, 256)
    per = tm // HALO

    def body(x_ref, xp_ref, z_ref, cw_ref, cb_ref, wa_ref, ba_ref, wx_ref, bx_ref, lam_ref,
             hh_ref, y_ref, carry):
        i = pl.program_id(0)

        @pl.when(i == 0)
        def _():
            carry[...] = jnp.zeros_like(carry)

        prev = jnp.where(i == 0, 0.0, xp_ref[...])
        xc = _conv_fwd(jnp.concatenate([prev, x_ref[...]], axis=0), cw_ref, cb_ref)
        _, ig, _, _, a, mult = _rg_gates(xc, wa_ref, ba_ref, wx_ref, bx_ref, lam_ref)
        ca, cu = _scan_groups(a, mult * (ig * xc), reverse=False)
        c = carry[0:1, :]
        for j in range(per):
            blk = ca[j * 8:(j + 1) * 8] * c + cu[j * 8:(j + 1) * 8]
            hh_ref[j * 8:(j + 1) * 8, :] = blk
            c = blk[7:8]
        carry[0:1, :] = c
        z = z_ref[...]
        y_ref[0] = _bf(hh_ref[...] * (z * _sigmoid(z)))

    vec = pl.BlockSpec((1, d), lambda i: (0, 0))
    whole3 = lambda a: pl.BlockSpec(a.shape, lambda i: (0, 0, 0))
    return _pcall_ride(
        body, ride, name="rg_fwd", grid=(s_len // tm,),
        in_specs=[pl.BlockSpec((tm, d), lambda i: (i, 0)),
                  pl.BlockSpec((HALO, d), lambda i: (jnp.maximum(i * per - 1, 0), 0)),
                  pl.BlockSpec((tm, d), lambda i: (i, 1)),
                  pl.BlockSpec((CONV_WIDTH, d), lambda i: (0, 0)), vec,
                  whole3(wa_b), vec, whole3(wx_b), vec, vec],
        out_specs=[pl.BlockSpec((tm, d), lambda i: (i, 0)), pl.BlockSpec((1, tm, d), lambda i: (0, i, 0))],
        out_shape=[jax.ShapeDtypeStruct((s_len, d), F32), jax.ShapeDtypeStruct((2, s_len, d), BF16)],
        scratch_shapes=[pltpu.VMEM((8, d), F32)],
        compiler_params=_seq(),
        args=(u, u, u, conv_w, conv_b, wa_b, ba, wx_b, bx, lam))


def _ml_pre(u, conv_w, conv_b, wqkv_b, wif_b, wift_b, b_if, b_ift):
    s_len = u.shape[0]
    d = conv_w.shape[1]
    _, heads, hd, _ = wqkv_b.shape
    ng = 2 * heads
    tm = _tile(s_len, 256)
    per = tm // HALO

    def body(x_ref, xp_ref, cw_ref, cb_ref, w_ref, wif_ref, wift_ref, bif_ref, bift_ref,
             qkv_ref, gt_ref, gtt_ref):
        i = pl.program_id(0)
        prev = jnp.where(i == 0, 0.0, xp_ref[...])
        xm = x_ref[...]
        pre = _conv_fwd(jnp.concatenate([prev, xm], axis=0), cw_ref, cb_ref)
        xcb = _bf(pre * _sigmoid(pre))
        xmb = _bf(xm)
        for h in range(heads):
            hs = slice(h * hd, (h + 1) * hd)
            qkv_ref[0, :, hs] = _bf(_dot(xcb[:, hs], w_ref[0, h]))
            qkv_ref[1, :, hs] = _bf(_dot(xcb[:, hs], w_ref[1, h]))
            qkv_ref[2, :, hs] = _bf(_dot(xmb[:, hs], w_ref[2, h]))
        qb, kb, vb = qkv_ref[0], qkv_ref[1], qkv_ref[2]
        gt_ref[...] = (_dot(qb, wif_ref[0:d, :]) + _dot(kb, wif_ref[d:2 * d, :]) + _dot(vb, wif_ref[2 * d:3 * d, :])
                       + bif_ref[...])
        gtt_ref[...] = (_dot_nt(wift_ref[:, 0:d], qb) + _dot_nt(wift_ref[:, d:2 * d], kb)
                        + _dot_nt(wift_ref[:, 2 * d:3 * d], vb) + bift_ref[...])

    vec = pl.BlockSpec((1, d), lambda i: (0, 0))
    whole2 = lambda a: pl.BlockSpec(a.shape, lambda i: (0, 0))
    return _pcall(
        body, name="ml_pre", grid=(s_len // tm,),
        in_specs=[pl.BlockSpec((tm, d), lambda i: (i, 2)),
                  pl.BlockSpec((HALO, d), lambda i: (jnp.maximum(i * per - 1, 0), 2)),
                  pl.BlockSpec((CONV_WIDTH, d), lambda i: (0, 0)), vec,
                  pl.BlockSpec(wqkv_b.shape, lambda i: (0, 0, 0, 0)), whole2(wif_b), whole2(wift_b), whole2(b_if),
                  whole2(b_ift)],
        out_specs=[pl.BlockSpec((3, tm, d), lambda i: (0, i, 0)), pl.BlockSpec((tm, ng), lambda i: (i, 0)),
                   pl.BlockSpec((ng, tm), lambda i: (0, i))],
        out_shape=[jax.ShapeDtypeStruct((3, s_len, d), BF16), jax.ShapeDtypeStruct((s_len, ng), F32),
                   jax.ShapeDtypeStruct((ng, s_len), F32)],
        compiler_params=_seq(),
    )(u, u, conv_w, conv_b, wqkv_b, wif_b, wift_b, b_if, b_ift)


def _chunk_gates(gt, gtt, h, heads, tril, triu):
    li_c = gt[:, h:h + 1]
    li_r = gtt[h:h + 1, :]
    gf_c = gt[:, heads + h:heads + h + 1]
    lf_c = _log_sigmoid(gf_c)
    lf_r = _log_sigmoid(gtt[heads + h:heads + h + 1, :])
    b_c = _tri_dot_left(tril, lf_c)
    b_r = _tri_dot_right(lf_r, triu)
    return li_c, li_r, gf_c, b_c, b_r


def _chunk_weights(li_c, li_r, b_c, b_r, m_prev, causal):
    lc = b_c.shape[0]
    b_last = b_c[lc - 1:lc, :]
    dmat = jnp.where(causal, b_c - b_r + li_r, -jnp.inf)
    m_inter = b_c + m_prev
    m_t = jnp.maximum(m_inter, jnp.max(dmat, axis=1, keepdims=True))
    w_intra = jnp.exp(dmat - m_t)
    w_inter = jnp.exp(m_inter - m_t)
    g_c = b_last - b_c + li_c
    m_new = jnp.maximum(b_last + m_prev, jnp.max(g_c, axis=0, keepdims=True))
    w_state = jnp.exp(g_c - m_new)
    decay = jnp.exp(b_last + m_prev - m_new)
    return m_t, w_intra, w_inter, m_new, w_state, decay


def _tri_masks(lc):
    r = lax.broadcasted_iota(jnp.int32, (lc, lc), 0)
    c = lax.broadcasted_iota(jnp.int32, (lc, lc), 1)
    causal = r >= c
    return causal, causal.astype(BF16), (r <= c).astype(BF16)


def _mlstm_fwd(qkv, gt, gtt, u, ml_g, ycat, ride=None):
    _, s_len, d = qkv.shape
    ng = gt.shape[1]
    heads = ng // 2
    hd = d // heads
    lc = ML_CHUNK
    nc = s_len // lc
    kscale = hd ** -0.5

    def body(qkv_ref, gt_ref, gtt_ref, o_ref, z_ref, g_ref, _, cell_ref, y_ref, cst_ref, nst_ref, mst_ref, cs, ns, ms):
        @pl.when(pl.program_id(0) == 0)
        def _():
            cs[...] = jnp.zeros_like(cs)
            ns[...] = jnp.zeros_like(ns)
            ms[...] = jnp.zeros_like(ms)

        causal, tril, triu = _tri_masks(lc)
        gtv, gttv = gt_ref[...], gtt_ref[...]
        for h in range(heads):
            hs = slice(h * hd, (h + 1) * hd)
            li_c, li_r, _, b_c, b_r = _chunk_gates(gtv, gttv, h, heads, tril, triu)
            m_prev = ms[h][:, 0:1]
            m_t, w_intra, w_inter, m_new, w_state, decay = _chunk_weights(li_c, li_r, b_c, b_r, m_prev, causal)
            qb = qkv_ref[0, :, hs]
            ks = qkv_ref[1, :, hs].astype(F32) * kscale
            kb = _bf(ks)
            vb = qkv_ref[2, :, hs]
            c_old = cs[h]
            n_old = ns[h]
            cst_ref[0, h] = _bf(c_old)
            nst_ref[0, h] = n_old
            mst_ref[0, h] = ms[h]
            s = _dot_nt(qb, kb) * w_intra
            num = _dot(_bf(s), vb) + w_inter * _dot(qb, _bf(c_old))
            den = _rowsum(s) + w_inter * _rowsum(qb.astype(F32) * n_old)
            cell = num / jnp.maximum(jnp.abs(den), jnp.exp(-m_t))
            kw = ks * w_state
            cs[h] = decay * c_old + _dot_tn(_bf(kw), vb)
            ns[h] = decay * n_old + _colsum(kw)
            ms[h] = jnp.broadcast_to(m_new, ms[h].shape)
            cell_ref[:, hs] = cell
            hm = _sigmoid(o_ref[:, hs]) * cell
            hn = hm * lax.rsqrt(jnp.mean(hm * hm, axis=-1, keepdims=True) + EPS)
            z = z_ref[:, hs]
            y_ref[0, :, hs] = _bf((hn * g_ref[:, hs]) * (z * _sigmoid(z)))

    row = pl.BlockSpec((lc, d), lambda c: (c, 0))
    return _pcall_ride(
        body, ride, name="mlstm_fwd", grid=(nc,),
        in_specs=[pl.BlockSpec((3, lc, d), lambda c: (0, c, 0)), pl.BlockSpec((lc, ng), lambda c: (c, 0)),
                  pl.BlockSpec((ng, lc), lambda c: (0, c)),
                  pl.BlockSpec((lc, d), lambda c: (c, 3)), pl.BlockSpec((lc, d), lambda c: (c, 4)),
                  pl.BlockSpec((1, d), lambda c: (0, 0)), pl.BlockSpec(memory_space=pl.ANY)],
        out_specs=[row, pl.BlockSpec((1, lc, d), lambda c: (1, c, 0)),
                   pl.BlockSpec((1, heads, hd, hd), lambda c: (c, 0, 0, 0)),
                   pl.BlockSpec((1, heads, 1, hd), lambda c: (c, 0, 0, 0)),
                   pl.BlockSpec((1, heads, 1, 128), lambda c: (c, 0, 0, 0))],
        out_shape=[jax.ShapeDtypeStruct((s_len, d), F32), jax.ShapeDtypeStruct(ycat.shape, BF16),
                   jax.ShapeDtypeStruct((nc, heads, hd, hd), BF16),
                   jax.ShapeDtypeStruct((nc, heads, 1, hd), F32),
                   jax.ShapeDtypeStruct((nc, heads, 1, 128), F32)],
        scratch_shapes=[pltpu.VMEM((heads, hd, hd), F32), pltpu.VMEM((heads, 1, hd), F32),
                        pltpu.VMEM((heads, 1, 128), F32)],
        input_output_aliases={6: 1},
        compiler_params=_seq(),
        args=(qkv, gt, gtt, u, u, ml_g, ycat))


def _out_proj(ycat, w_out_b, x, gate):
    s_len, d = x.shape
    tm = _tile(s_len, 512)

    def body(a_ref, w_ref, x_ref, g_ref, y_ref, xn_ref):
        y = _dot(a_ref[0], w_ref[0:d, :]) + _dot(a_ref[1], w_ref[d:2 * d, :])
        y_ref[...] = y
        xn_ref[...] = x_ref[...] + g_ref[...] * y

    row = pl.BlockSpec((tm, d), lambda i: (i, 0))
    return _pcall(
        body, name="out_proj", grid=(s_len // tm,),
        in_specs=[pl.BlockSpec((2, tm, d), lambda i: (0, i, 0)), pl.BlockSpec((2 * d, d), lambda i: (0, 0)), row,
                  pl.BlockSpec((1, d), lambda i: (0, 0))],
        out_specs=[row, row],
        out_shape=[jax.ShapeDtypeStruct((s_len, d), F32)] * 2,
        compiler_params=_seq(),
    )(ycat, w_out_b, x, gate)


def _final_loss(x, g, target):
    s_len, d = x.shape
    tm = _tile(s_len, 256)

    def body(x_ref, g_ref, t_ref, dx_ref, dg_ref, loss_ref):
        @pl.when(pl.program_id(0) == 0)
        def _():
            dg_ref[...] = jnp.zeros_like(dg_ref)
            loss_ref[...] = jnp.zeros_like(loss_ref)

        xv = x_ref[...]
        r = lax.rsqrt(jnp.mean(xv * xv, axis=-1, keepdims=True) + EPS)
        xn = xv * r
        err = xn * g_ref[...] - t_ref[...]
        loss_ref[...] += 0.5 * jnp.sum(jnp.mean(err * err, axis=-1, keepdims=True))
        dout = err * (1.0 / d)
        dg_ref[...] += _colsum(dout * xn)
        dxn = dout * g_ref[...]
        dx_ref[...] = r * (dxn - xn * jnp.mean(dxn * xn, axis=-1, keepdims=True))

    row = pl.BlockSpec((tm, d), lambda i: (i, 0))
    vec = pl.BlockSpec((1, d), lambda i: (0, 0))
    return _pcall(
        body, name="final_loss", grid=(s_len // tm,),
        in_specs=[row, vec, row],
        out_specs=[row, vec, pl.BlockSpec((1, 128), lambda i: (0, 0))],
        out_shape=[jax.ShapeDtypeStruct((s_len, d), F32), jax.ShapeDtypeStruct((1, d), F32),
                   jax.ShapeDtypeStruct((1, 128), F32)],
        compiler_params=_seq(),
    )(x, g, target)


def _out_bwd(dxn, y, gate, w_out_b):
    s_len, d = dxn.shape
    tm = _tile(s_len, 512)

    def body(dx_ref, y_ref, g_ref, w_ref, dg_ref, dy_ref, dc_ref):
        @pl.when(pl.program_id(0) == 0)
        def _():
            dg_ref[...] = jnp.zeros_like(dg_ref)

        dx = dx_ref[...]
        dg_ref[...] += _colsum(dx * y_ref[...])
        dy = _bf(g_ref[...] * dx)
        dy_ref[...] = dy
        dc_ref[0] = _dot_nt(dy, w_ref[0:d, :])
        dc_ref[1] = _dot_nt(dy, w_ref[d:2 * d, :])

    row = pl.BlockSpec((tm, d), lambda i: (i, 0))
    vec = pl.BlockSpec((1, d), lambda i: (0, 0))
    return _pcall(
        body, name="out_bwd", grid=(s_len // tm,),
        in_specs=[row, row, vec, pl.BlockSpec((2 * d, d), lambda i: (0, 0))],
        out_specs=[vec, row, pl.BlockSpec((2, tm, d), lambda i: (0, i, 0))],
        out_shape=[jax.ShapeDtypeStruct((1, d), F32), jax.ShapeDtypeStruct((s_len, d), BF16),
                   jax.ShapeDtypeStruct((2, s_len, d), F32)],
        compiler_params=_seq(),
    )(dxn, y, gate, w_out_b)


def _grad_matmul(a3, b3, nblk, a_idx, b_idx, out_shape, out_block, out_idx, layer, stack):
    _, s_len, m = a3.shape
    n = b3.shape[2]
    tk = _tile(s_len, 512)
    first = isinstance(stack, int)

    def body(a_ref, b_ref, *rest):
        o_ref = rest[-1]

        @pl.when(pl.program_id(1) == 0)
        def _():
            o_ref[...] = jnp.zeros_like(o_ref)

        o_ref[...] += _dot_tn(a_ref[0], b_ref[0])

    in_specs = [pl.BlockSpec((1, tk, m), lambda p, t: (a_idx(p), t, 0)),
                pl.BlockSpec((1, tk, n), lambda p, t: (b_idx(p), t, 0))]
    return _pcall(
        body, name="grad_matmul", grid=(nblk, s_len // tk),
        in_specs=in_specs if first else in_specs + [pl.BlockSpec(memory_space=pl.ANY)],
        out_specs=pl.BlockSpec((None,) + out_block, lambda p, t: (layer,) + out_idx(p)),
        out_shape=jax.ShapeDtypeStruct(((stack,) if first else stack.shape[:1]) + out_shape, F32),
        input_output_aliases={} if first else {2: 0},
        compiler_params=_seq(2),
    )(*((a3, b3) if first else (a3, b3, stack)))


DU_PLANE = (2, 3, 4, 0, 1)


def _mlstm_bwd(qkv, gt, gtt, cst, nst, mst, cell, u, ml_g, d_ycat, wif_b, ride=None):
    _, s_len, d = qkv.shape
    ng = gt.shape[1]
    heads = ng // 2
    hd = d // heads
    lc = ML_CHUNK
    nc = s_len // lc
    kscale = hd ** -0.5

    def body(qkv_ref, gt_ref, gtt_ref, cst_ref, nst_ref, mst_ref, cell_ref, o_ref, z_ref, g_ref, dy_ref,
             wif_ref, dqkv_ref, dgt_ref, dbif_ref, du_ref, dg_ref, dcs, dns, dqs, dks, dvs):
        @pl.when(pl.program_id(0) == 0)
        def _():
            dbif_ref[...] = jnp.zeros_like(dbif_ref)
            dcs[...] = jnp.zeros_like(dcs)
            dns[...] = jnp.zeros_like(dns)
            dg_ref[...] = jnp.zeros_like(dg_ref)

        causal, tril, triu = _tri_masks(lc)
        tril_strict = (tril.astype(F32) - (tril * triu).astype(F32)).astype(BF16)
        gtv, gttv = gt_ref[...], gtt_ref[...]
        lane = lax.broadcasted_iota(jnp.int32, (lc, ng), 1)
        dgt = jnp.zeros((lc, ng), F32)
        for h in range(heads):
            hs = slice(h * hd, (h + 1) * hd)
            li_c, li_r, gf_c, b_c, b_r = _chunk_gates(gtv, gttv, h, heads, tril, triu)
            m_prev = mst_ref[0, h][:, 0:1]
            m_t, w_intra, w_inter, _, w_state, decay = _chunk_weights(li_c, li_r, b_c, b_r, m_prev, causal)
            qb = qkv_ref[0, :, hs]
            qf = qb.astype(F32)
            ks = qkv_ref[1, :, hs].astype(F32) * kscale
            kb = _bf(ks)
            vb = qkv_ref[2, :, hs]
            c_b = cst_ref[0, h]
            n_old = nst_ref[0, h]
            s = _dot_nt(qb, kb) * w_intra
            den = _rowsum(s) + w_inter * _rowsum(qf * n_old)
            floor = jnp.exp(-m_t)
            dstab = jnp.maximum(jnp.abs(den), floor)
            cell = cell_ref[:, hs]
            o = o_ref[:, hs]
            so = _sigmoid(o)
            hm = so * cell
            rinv = lax.rsqrt(jnp.mean(hm * hm, axis=-1, keepdims=True) + EPS)
            hn = hm * rinv
            z = z_ref[:, hs]
            sgz = _sigmoid(z)
            sz = z * sgz
            gh = g_ref[:, hs]
            dy = dy_ref[0, :, hs]
            du_ref[1, :, hs] = _bf(dy * (hn * gh) * _dsilu(z, sgz))
            dg_ref[:, hs] += _colsum(dy * hn * sz)
            dhn = dy * gh * sz
            dhm = rinv * (dhn - hn * jnp.mean(dhn * hn, axis=-1, keepdims=True))
            du_ref[0, :, hs] = _bf(dhm * cell * so * (1.0 - so))
            dcell = dhm * so
            dnum = dcell / dstab
            dnb = _bf(dnum)
            dden = -_rowsum(dcell * cell) / dstab * jnp.where(jnp.abs(den) > floor, jnp.where(den > 0.0, 1.0, -1.0), 0.0)
            dst = _dot_nt(dnb, vb) + dden
            dsdb = _bf(dst * w_intra)
            dc_out = dcs[h]
            dn_out = dns[h]
            dcb = _bf(dc_out)
            dq_inter = w_inter * (_dot_nt(dnb, c_b) + dden * n_old)
            dk_inter = w_state * (_dot_nt(vb, dcb) + dn_out)
            dq = _dot(dsdb, kb) + dq_inter
            dk = _dot_tn(dsdb, qb) + dk_inter
            dv = _dot_tn(_bf(s), dnb) + _dot(_bf(ks * w_state), dcb)
            wq = w_inter * qf
            dcs[h] = decay * dc_out + _dot_tn(_bf(wq), dnb)
            dns[h] = decay * dn_out + _colsum(wq * dden)
            pmat = dst * s
            p_rows = _rowsum(pmat)
            p_cols = _rowsum(pmat.T)
            q_in = _rowsum(qf * dq_inter)
            k_in = _rowsum(ks * dk_inter)
            across = decay * (jnp.sum(dc_out * c_b.astype(F32), keepdims=True) + jnp.sum(dn_out * n_old, keepdims=True))
            dli = p_cols + k_in
            dlf = _tri_dot_left(triu, p_rows - p_cols + q_in) + _tri_dot_left(tril_strict, k_in) + across
            dgf = dlf * _sigmoid(-gf_c)
            dgt = dgt + jnp.where(lane == h, dli, 0.0) + jnp.where(lane == heads + h, dgf, 0.0)
            dqs[:, hs] = dq
            dks[:, hs] = dk * kscale
            dvs[:, hs] = dv
        dgt_ref[...] = dgt
        dbif_ref[...] += _colsum(dgt)
        dgb = _bf(dgt)
        dqkv_ref[0] = _bf(dqs[...] + _dot_nt(dgb, wif_ref[0:d, :]))
        dqkv_ref[1] = _bf(dks[...] + _dot_nt(dgb, wif_ref[d:2 * d, :]))
        dqkv_ref[2] = _bf(dvs[...] + _dot_nt(dgb, wif_ref[2 * d:3 * d, :]))

    rev = lambda c: nc - 1 - c
    row = pl.BlockSpec((lc, d), lambda c: (rev(c), 0))
    return _pcall_ride(
        body, ride, name="mlstm_bwd", grid=(nc,),
        in_specs=[pl.BlockSpec((3, lc, d), lambda c: (0, rev(c), 0)), pl.BlockSpec((lc, ng), lambda c: (rev(c), 0)),
                  pl.BlockSpec((ng, lc), lambda c: (0, rev(c))),
                  pl.BlockSpec((1, heads, hd, hd), lambda c: (rev(c), 0, 0, 0)),
                  pl.BlockSpec((1, heads, 1, hd), lambda c: (rev(c), 0, 0, 0)),
                  pl.BlockSpec((1, heads, 1, 128), lambda c: (rev(c), 0, 0, 0)),
                  row, pl.BlockSpec((lc, d), lambda c: (rev(c), 3)), pl.BlockSpec((lc, d), lambda c: (rev(c), 4)),
                  pl.BlockSpec((1, d), lambda c: (0, 0)), pl.BlockSpec((1, lc, d), lambda c: (1, rev(c), 0)),
                  pl.BlockSpec((3 * d, ng), lambda c: (0, 0))],
        out_specs=[pl.BlockSpec((3, lc, d), lambda c: (0, rev(c), 0)), pl.BlockSpec((lc, ng), lambda c: (rev(c), 0)),
                   pl.BlockSpec((1, ng), lambda c: (0, 0)), pl.BlockSpec((2, lc, d), lambda c: (0, rev(c), 0)),
                   pl.BlockSpec((1, d), lambda c: (0, 0))],
        out_shape=[jax.ShapeDtypeStruct((3, s_len, d), BF16), jax.ShapeDtypeStruct((s_len, ng), F32),
                   jax.ShapeDtypeStruct((1, ng), F32), jax.ShapeDtypeStruct((5, s_len, d), BF16),
                   jax.ShapeDtypeStruct((1, d), F32)],
        scratch_shapes=[pltpu.VMEM((heads, hd, hd), F32), pltpu.VMEM((heads, 1, hd), F32)]
        + [pltpu.VMEM((lc, d), F32)] * 3,
        compiler_params=_seq(),
        args=(qkv, gt, gtt, cst, nst, mst, cell, u, u, ml_g, d_ycat, wif_b))


def _conv_bwd_tile(dp, later, xwin, cw_ref, gw_ref, gb_ref):
    tm = dp.shape[0]
    dwin = jnp.concatenate([dp, later[...]], axis=0)
    later[...] = dp[0:HALO]
    acc = cw_ref[CONV_WIDTH - 1:CONV_WIDTH, :] * dp
    gw_ref[CONV_WIDTH - 1:CONV_WIDTH, :] += _colsum(dp * xwin[HALO:])
    for k in range(CONV_WIDTH - 1):
        sft = CONV_WIDTH - 1 - k
        acc = acc + cw_ref[k:k + 1, :] * _shift_up(dwin, sft)[0:tm]
        gw_ref[k:k + 1, :] += _colsum(dp * _shift_down(xwin, sft)[HALO:])
    gb_ref[...] += _colsum(dp)
    return acc


def _ml_pre_bwd(dqkv, u, conv_w, conv_b, wqkv_b, du):
    s_len = u.shape[0]
    d = conv_w.shape[1]
    _, heads, hd, _ = wqkv_b.shape
    tm = _tile(s_len, 256)
    per = tm // HALO
    nt = s_len // tm

    def body(dqkv_ref, x_ref, xp_ref, cw_ref, cb_ref, w_ref, _, dx_ref, gw_ref, gcw_ref, gcb_ref, later, dps, dxs):
        i = pl.program_id(0)

        @pl.when(i == 0)
        def _():
            gw_ref[...] = jnp.zeros_like(gw_ref)
            gcw_ref[...] = jnp.zeros_like(gcw_ref)
            gcb_ref[...] = jnp.zeros_like(gcb_ref)
            later[...] = jnp.zeros_like(later)

        prev = jnp.where(i == nt - 1, 0.0, xp_ref[...])
        xm = x_ref[...]
        xwin = jnp.concatenate([prev, xm], axis=0)
        pre = _conv_fwd(xwin, cw_ref, cb_ref)
        sg = _sigmoid(pre)
        xcb = _bf(pre * sg)
        xmb = _bf(xm)
        for h in range(heads):
            hs = slice(h * hd, (h + 1) * hd)
            dqh, dkh, dvh = dqkv_ref[0, :, hs], dqkv_ref[1, :, hs], dqkv_ref[2, :, hs]
            dxc = _dot_nt(dqh, w_ref[0, h]) + _dot_nt(dkh, w_ref[1, h])
            dps[:, hs] = dxc * _dsilu(pre[:, hs], sg[:, hs])
            dxs[:, hs] = _dot_nt(dvh, w_ref[2, h])
            gw_ref[0, h] += _dot_tn(xcb[:, hs], dqh)
            gw_ref[1, h] += _dot_tn(xcb[:, hs], dkh)
            gw_ref[2, h] += _dot_tn(xmb[:, hs], dvh)
        dx_ref[0] = _bf(_conv_bwd_tile(dps[...], later, xwin, cw_ref, gcw_ref, gcb_ref) + dxs[...])

    rev = lambda i: nt - 1 - i
    vec = pl.BlockSpec((1, d), lambda i: (0, 0))
    cwb = pl.BlockSpec((CONV_WIDTH, d), lambda i: (0, 0))
    whole4 = pl.BlockSpec(wqkv_b.shape, lambda i: (0, 0, 0, 0))
    return _pcall(
        body, name="ml_pre_bwd", grid=(nt,),
        in_specs=[pl.BlockSpec((3, tm, d), lambda i: (0, rev(i), 0)), pl.BlockSpec((tm, d), lambda i: (rev(i), 2)),
                  pl.BlockSpec((HALO, d), lambda i: (jnp.maximum(rev(i) * per - 1, 0), 2)),
                  cwb, vec, whole4, pl.BlockSpec(memory_space=pl.ANY)],
        out_specs=[pl.BlockSpec((1, tm, d), lambda i: (DU_PLANE[2], rev(i), 0)), whole4, cwb, vec],
        out_shape=[jax.ShapeDtypeStruct(du.shape, BF16), jax.ShapeDtypeStruct(wqkv_b.shape, F32),
                   jax.ShapeDtypeStruct((CONV_WIDTH, d), F32), jax.ShapeDtypeStruct((1, d), F32)],
        scratch_shapes=[pltpu.VMEM((HALO, d), F32), pltpu.VMEM((tm, d), F32), pltpu.VMEM((tm, d), F32)],
        input_output_aliases={6: 0},
        compiler_params=_seq(),
    )(dqkv, u, u, conv_w, conv_b, wqkv_b, du)


def _rg_bwd(d_ycat, u, hh, conv_w, conv_b, wa_b, ba, wx_b, bx, lam, du):
    s_len = u.shape[0]
    d = conv_w.shape[1]
    heads, hd, _ = wa_b.shape
    tm = _tile(s_len, 256)
    per = tm // HALO
    nt = s_len // tm

    def body(dy_ref, x_ref, xp_ref, z_ref, hh_ref, hp_ref, cw_ref, cb_ref, wa_ref, ba_ref, wx_ref, bx_ref, lam_ref, _,
             du_ref, gwa_ref, gwx_ref, gba_ref, gbx_ref, glam_ref, gcw_ref, gcb_ref, carry, gbuf, later, dxcs):
        i = pl.program_id(0)
        first = i == nt - 1

        @pl.when(i == 0)
        def _():
            carry[...] = jnp.zeros_like(carry)
            later[...] = jnp.zeros_like(later)
            gwa_ref[...] = jnp.zeros_like(gwa_ref)
            gwx_ref[...] = jnp.zeros_like(gwx_ref)
            gba_ref[...] = jnp.zeros_like(gba_ref)
            gbx_ref[...] = jnp.zeros_like(gbx_ref)
            glam_ref[...] = jnp.zeros_like(glam_ref)
            gcw_ref[...] = jnp.zeros_like(gcw_ref)
            gcb_ref[...] = jnp.zeros_like(gcb_ref)

        prev = jnp.where(first, 0.0, xp_ref[...])
        xwin = jnp.concatenate([prev, x_ref[...]], axis=0)
        xc = _conv_fwd(xwin, cw_ref, cb_ref)
        r, ig, sp, log_a, a, mult = _rg_gates(xc, wa_ref, ba_ref, wx_ref, bx_ref, lam_ref)
        z = z_ref[...]
        sgz = _sigmoid(z)
        dy = dy_ref[0]
        hh_v = hh_ref[...]
        du_ref[1] = _bf(dy * hh_v * _dsilu(z, sgz))
        dhh = dy * (z * sgz)
        rows = lax.broadcasted_iota(jnp.int32, a.shape, 0)
        coef = jnp.where(rows == tm - 1, carry[1:2, :], _shift_up(a, 1))
        ca, cu = _scan_groups(coef, dhh, reverse=True)
        c = carry[0:1, :]
        for j in range(per - 1, -1, -1):
            blk = ca[j * 8:(j + 1) * 8] * c + cu[j * 8:(j + 1) * 8]
            gbuf[j * 8:(j + 1) * 8, :] = blk
            c = blk[0:1]
        carry[0:1, :] = c
        carry[1:2, :] = a[0:1]
        g = gbuf[...]
        hprev_tile = jnp.where(first, 0.0, hp_ref[...])
        hprev = _shift_down(jnp.concatenate([hprev_tile, hh_v], axis=0), 1)[HALO:]
        da = g * hprev
        gx_ = g * xc
        d_mult = gx_ * ig
        d_ig = gx_ * mult
        dxc = g * mult * ig
        a2 = jnp.exp(2.0 * log_a)
        dlog_a = da * a - d_mult * (a2 / mult)
        d_r = dlog_a * ((-RG_C) * sp)
        glam_ref[...] += _colsum(dlog_a * ((-RG_C) * r)) * (-_sigmoid(-lam_ref[...]))
        d_ga = d_r * r * (1.0 - r)
        d_gx = d_ig * ig * (1.0 - ig)
        gba_ref[...] += _colsum(d_ga)
        gbx_ref[...] += _colsum(d_gx)
        xb = _bf(xc)
        dgab = _bf(d_ga)
        dgxb = _bf(d_gx)
        for h in range(heads):
            hs = slice(h * hd, (h + 1) * hd)
            dxcs[:, hs] = dxc[:, hs] + _dot_nt(dgab[:, hs], wa_ref[h]) + _dot_nt(dgxb[:, hs], wx_ref[h])
            gwa_ref[h] += _dot_tn(xb[:, hs], dgab[:, hs])
            gwx_ref[h] += _dot_tn(xb[:, hs], dgxb[:, hs])
        du_ref[0] = _bf(_conv_bwd_tile(dxcs[...], later, xwin, cw_ref, gcw_ref, gcb_ref))

    assert DU_PLANE[0] % 2 == 0 and DU_PLANE[1] == DU_PLANE[0] + 1
    rev = lambda i: nt - 1 - i
    row = pl.BlockSpec((tm, d), lambda i: (rev(i), 0))
    halo_prev = lambda col: pl.BlockSpec((HALO, d), lambda i: (jnp.maximum(rev(i) * per - 1, 0), col))
    vec = pl.BlockSpec((1, d), lambda i: (0, 0))
    cwb = pl.BlockSpec((CONV_WIDTH, d), lambda i: (0, 0))
    whole3 = lambda a: pl.BlockSpec(a.shape, lambda i: (0, 0, 0))
    return _pcall(
        body, name="rg_bwd", grid=(nt,),
        in_specs=[pl.BlockSpec((1, tm, d), lambda i: (0, rev(i), 0)), row, halo_prev(0),
                  pl.BlockSpec((tm, d), lambda i: (rev(i), 1)), row, halo_prev(0),
                  cwb, vec, whole3(wa_b), vec, whole3(wx_b), vec, vec, pl.BlockSpec(memory_space=pl.ANY)],
        out_specs=[pl.BlockSpec((2, tm, d), lambda i: (DU_PLANE[0] // 2, rev(i), 0)), whole3(wa_b), whole3(wa_b),
                   vec, vec, vec, cwb, vec],
        out_shape=[jax.ShapeDtypeStruct(du.shape, BF16), jax.ShapeDtypeStruct(wa_b.shape, F32),
                   jax.ShapeDtypeStruct(wa_b.shape, F32)] + [jax.ShapeDtypeStruct((1, d), F32)] * 3
        + [jax.ShapeDtypeStruct((CONV_WIDTH, d), F32), jax.ShapeDtypeStruct((1, d), F32)],
        scratch_shapes=[pltpu.VMEM((8, d), F32), pltpu.VMEM((tm, d), F32), pltpu.VMEM((HALO, d), F32),
                        pltpu.VMEM((tm, d), F32)],
        input_output_aliases={13: 0},
        compiler_params=_seq(),
    )(d_ycat, u, u, u, hh, hh, conv_w, conv_b, wa_b, ba, wx_b, bx, lam, du)


def _in_bwd(du, w4, x, dxn, g, scale):
    s_len, d = x.shape
    tm = _tile(s_len, 256)
    nsh_chips, _, nsh = w4.shape
    npc = du.shape[0]
    ck = d // 4
    assert nsh % ck == 0 and npc * d == nsh_chips * nsh

    def body(du_ref, w_ref, x_ref, dxn_ref, g_ref, sc_ref, dx_ref, dsh_ref, dsc_ref, dg_ref):
        @pl.when(pl.program_id(0) == 0)
        def _():
            dsh_ref[...] = jnp.zeros_like(dsh_ref)
            dsc_ref[...] = jnp.zeros_like(dsc_ref)
            dg_ref[...] = jnp.zeros_like(dg_ref)

        dh = None
        for q in range(npc * d // ck):
            col = q * ck
            p, pc = col // d, col % d
            s, sc = col // nsh, col % nsh
            t = _dot_nt(du_ref[DU_PLANE[p], :, pc:pc + ck], w_ref[s, :, sc:sc + ck])
            dh = t if dh is None else dh + t
        xv = x_ref[...]
        r = lax.rsqrt(jnp.mean(xv * xv, axis=-1, keepdims=True) + EPS)
        xn = xv * r
        gv = g_ref[...]
        onesc = 1.0 + sc_ref[...]
        dsh_ref[...] += _colsum(dh)
        dsc_ref[...] += _colsum(dh * (xn * gv))
        dg_ref[...] += _colsum(dh * xn * onesc)
        dxh = dh * (gv * onesc)
        dx_ref[...] = dxn_ref[...] + r * (dxh - xn * jnp.mean(dxh * xn, axis=-1, keepdims=True))

    row = pl.BlockSpec((tm, d), lambda i: (i, 0))
    vec = pl.BlockSpec((1, d), lambda i: (0, 0))
    return _pcall(
        body, name="in_bwd", grid=(s_len // tm,),
        in_specs=[pl.BlockSpec((npc, tm, d), lambda i: (0, i, 0)), pl.BlockSpec(w4.shape, lambda i: (0, 0, 0)), row, row,
                  vec, vec],
        out_specs=[row, vec, vec, vec],
        out_shape=[jax.ShapeDtypeStruct((s_len, d), F32)] + [jax.ShapeDtypeStruct((1, d), F32)] * 3,
        compiler_params=_seq(),
    )(du, w4, x, dxn, g, scale)


def _layer_fwd(x, p, rides=(None, None)):
    h_b, u = _ln_inproj(x, p["norm_g"], p["scale"], p["shift"], p["w4"])
    (hh, ycat), got_a = _rg_fwd(u, p["rg_conv_w"], p["rg_conv_b"], p["rg_wa_b"], p["rg_ba"], p["rg_wx_b"], p["rg_bx"],
                                p["rg_lam"], rides[0])
    qkv, gt, gtt = _ml_pre(u, p["ml_conv_w"], p["ml_conv_b"], p["wqkv_b"], p["wif_b"], p["wift_b"], p["b_if"],
                           p["b_ift"])
    (cell, ycat, cst, nst, mst), got_b = _mlstm_fwd(qkv, gt, gtt, u, p["ml_g"], ycat, rides[1])
    y, x_new = _out_proj(ycat, p["w_out_b"], x, p["gate"])
    saved = dict(x=x, h_b=h_b, u=u, hh=hh, qkv=qkv, gt=gt, gtt=gtt, cell=cell, ycat=ycat, cst=cst, nst=nst, mst=mst,
                 y=y)
    return x_new, saved, list(got_a) + list(got_b)


def _layer_bwd(dxn, p, s, ride=None):
    u = s["u"]
    d = dxn.shape[1]
    d_gate, dy_b, d_ycat = _out_bwd(dxn, s["y"], p["gate"], p["w_out_b"])
    g_w_out = _grad_matmul(s["ycat"], dy_b[None], 2, lambda b: b, lambda b: 0, (2 * d, d), (d, d), lambda b: (b, 0),
                           0, 1)
    (dqkv, dgt, g_b_if, du, g_ml_g), got = _mlstm_bwd(s["qkv"], s["gt"], s["gtt"], s["cst"], s["nst"], s["mst"],
                                                      s["cell"], u, p["ml_g"], d_ycat, p["wif_b"], ride)
    ng = dgt.shape[1]
    g_w_if = _grad_matmul(s["qkv"], _bf(dgt)[None], 3, lambda b: b, lambda b: 0, (3 * d, ng), (d, ng),
                          lambda b: (b, 0), 0, 1)[0]
    du, g_wqkv, g_ml_cw, g_ml_cb = _ml_pre_bwd(dqkv, u, p["ml_conv_w"], p["ml_conv_b"], p["wqkv_b"], du)
    du, g_wa, g_wx, g_ba, g_bx, g_lam, g_rg_cw, g_rg_cb = _rg_bwd(d_ycat, u, s["hh"], p["rg_conv_w"], p["rg_conv_b"],
                                                                  p["rg_wa_b"], p["rg_ba"], p["rg_wx_b"], p["rg_bx"],
                                                                  p["rg_lam"], du)
    npc = du.shape[0]
    g_w_in = _grad_matmul(s["h_b"][None], du, npc, lambda b: 0, lambda b: (b + DU_PLANE[0]) % npc, (d, npc * d),
                          (d, d), lambda b: (0, b), 0, 1)
    dx, d_shift, d_scale, g_norm_g = _in_bwd(du, p["w4"], s["x"], dxn, p["norm_g"], p["scale"])
    grads = dict(norm_g=g_norm_g, w_in=g_w_in, rg_conv_w=g_rg_cw, rg_conv_b=g_rg_cb, rg_w_a=g_wa, rg_b_a=g_ba,
                 rg_w_x=g_wx, rg_b_x=g_bx, rg_lambda=g_lam, ml_conv_w=g_ml_cw, ml_conv_b=g_ml_cb, ml_w_qkv=g_wqkv,
                 ml_w_if=g_w_if, ml_b_if=g_b_if, ml_norm_g=g_ml_g, w_out=g_w_out)
    return dx, grads, jnp.concatenate([d_shift, d_scale, d_gate], axis=1), got


def _trunk_fwd_bwd(x, target, final_g, layers):
    saved = []
    for p in layers:
        x, s, _ = _layer_fwd(x, p)
        saved.append(s)
    dx, g_final, loss = _final_loss(x, final_g, target)
    grads, dmods = [], []
    for layer in reversed(range(len(layers))):
        dx, g, dm, _ = _layer_bwd(dx, layers[layer], saved[layer])
        grads.append(g)
        dmods.append(dm)
    return loss, dx, g_final, grads[::-1], dmods[::-1]


def _me():
    return lax.axis_index("x"), lax.axis_index("y"), lax.axis_index("c")


def _remote(src, dst, send_sem, recv_sem, to):
    return pltpu.make_async_remote_copy(src_ref=src, dst_ref=dst, send_sem=send_sem, recv_sem=recv_sem,
                                        device_id=to, device_id_type=MESH)


def _all_gather8(blocks, space):
    n = len(blocks)

    def body(*refs):
        x_refs, out_refs = refs[:n], refs[n:2 * n]
        send_sems, recv_sems, local_sems = refs[2 * n:]
        x, y, c = _me()
        me, sibling = (x, y, c), (x, y, 1 - c)
        chips = [(1 - x, y), (x, 1 - y), (1 - x, 1 - y)]

        def rows(i, px, py, pc):
            m_per = blocks[i].shape[0]
            return out_refs[i].at[pl.ds((4 * px + 2 * py + pc) * m_per, m_per), :]

        def copy(i, k, blk, to, src=None):
            return _remote(rows(i, *blk) if src is None else src, rows(i, *blk), send_sems.at[7 * i + k],
                           recv_sems.at[7 * i + k], to)

        mine = [pltpu.make_async_copy(x_refs[i], rows(i, *me), local_sems.at[i]) for i in range(n)]
        first = []
        for i in range(n):
            first.append(copy(i, 0, me, sibling, src=x_refs[i]))
            first += [copy(i, 1 + j, me, (*chip, c), src=x_refs[i]) for j, chip in enumerate(chips)]
        for cp in mine + first:
            cp.start()
        passed = []
        for j, chip in enumerate(chips):
            for i in range(n):
                copy(i, 1 + j, (*chip, c), me).wait_recv()
                passed.append(copy(i, 4 + j, (*chip, c), sibling))
                passed[-1].start()
        for i in range(n):
            copy(i, 0, sibling, me).wait_recv()
            for j, chip in enumerate(chips):
                copy(i, 4 + j, (*chip, 1 - c), me).wait_recv()
        for cp in first + passed:
            cp.wait_send()
        for cp in mine:
            cp.wait()

    spec = pl.BlockSpec(memory_space=space)
    return _pcall(
        body, name="all_gather8",
        out_shape=[jax.ShapeDtypeStruct((8 * b.shape[0], b.shape[1]), b.dtype) for b in blocks],
        in_specs=[spec] * n, out_specs=[spec] * n,
        scratch_shapes=[pltpu.SemaphoreType.DMA((7 * n,)), pltpu.SemaphoreType.DMA((7 * n,)),
                        pltpu.SemaphoreType.DMA((n,))],
    )(*blocks)


def _sib_halves(g_in, g_out, slabs):
    depth, d, n4 = g_in.shape
    n = n4 // 4
    ns = len(slabs)

    def body(*refs):
        gi, go = refs[0], refs[1]
        sl = refs[2:2 + ns]
        ri, ro = refs[2 + ns], refs[3 + ns]
        rs = refs[4 + ns:4 + 2 * ns]
        send_sems, recv_sems = refs[4 + 2 * ns:]
        x, y, c = _me()
        o = 1 - c
        pairs = [(gi.at[pl.ds(0, depth), pl.ds(o * (d // 2), d // 2), pl.ds(s * n, n)], ri.at[pl.ds(0, depth), s])
                 for s in range(4)]
        pairs.append((go.at[pl.ds(0, depth), pl.ds(0, 4), o], ro))
        pairs += [(sl[i].at[o], rs[i]) for i in range(ns)]
        copies = [_remote(src, dst, send_sems.at[k], recv_sems.at[k], (x, y, o)) for k, (src, dst) in enumerate(pairs)]
        for cp in copies:
            cp.start()
        for cp in copies:
            cp.wait_recv()
        for cp in copies:
            cp.wait_send()

    hbm = pl.BlockSpec(memory_space=pltpu.HBM)
    ncp = 5 + ns
    return _pcall(
        body, name="sib_halves",
        out_shape=[jax.ShapeDtypeStruct((depth, 4, d // 2, n), g_in.dtype),
                   jax.ShapeDtypeStruct(g_out.shape[:2] + g_out.shape[3:], g_out.dtype)]
        + [jax.ShapeDtypeStruct(s.shape[1:], s.dtype) for s in slabs],
        in_specs=[hbm] * (2 + ns), out_specs=[hbm] * (2 + ns),
        scratch_shapes=[pltpu.SemaphoreType.DMA((ncp,)), pltpu.SemaphoreType.DMA((ncp,))],
    )(g_in, g_out, *slabs)


def _sib_fill(boths):
    n = len(boths)

    def body(*refs):
        dst = refs[n:2 * n]
        send_sems, recv_sems = refs[2 * n:]
        x, y, c = _me()
        view = lambda i: dst[i].at[pl.ds(0, boths[i].shape[0]), c]
        copies = [_remote(view(i), view(i), send_sems.at[i], recv_sems.at[i], (x, y, 1 - c)) for i in range(n)]
        for cp in copies:
            cp.start()
        for cp in copies:
            cp.wait_recv()
        for cp in copies:
            cp.wait_send()

    hbm = pl.BlockSpec(memory_space=pltpu.HBM)
    return _pcall(
        body, name="sib_fill",
        out_shape=[jax.ShapeDtypeStruct(b.shape, b.dtype) for b in boths],
        in_specs=[hbm] * n, out_specs=[hbm] * n, input_output_aliases={i: i for i in range(n)},
        scratch_shapes=[pltpu.SemaphoreType.DMA((n,)), pltpu.SemaphoreType.DMA((n,))],
    )(*boths)


def _chip_exchange(arrs):
    n = len(arrs)

    def body(*refs):
        src, dst = refs[:n], refs[n:2 * n]
        send_sems, recv_sems = refs[2 * n:]
        x, y, c = _me()
        me_s = 2 * x + y
        chips = [(1 - x, y), (x, 1 - y), (1 - x, 1 - y)]
        copies = [_remote(src[i].at[2 * px + py], dst[i].at[me_s], send_sems.at[3 * i + k], recv_sems.at[3 * i + k],
                          (px, py, c))
                  for i in range(n) for k, (px, py) in enumerate(chips)]
        for cp in copies:
            cp.start()
        for cp in copies:
            cp.wait_recv()
        for cp in copies:
            cp.wait_send()

    hbm = pl.BlockSpec(memory_space=pltpu.HBM)
    return _pcall(
        body, name="chip_exchange",
        out_shape=[jax.ShapeDtypeStruct(a.shape, a.dtype) for a in arrs],
        in_specs=[hbm] * n, out_specs=[hbm] * n,
        scratch_shapes=[pltpu.SemaphoreType.DMA((3 * n,)), pltpu.SemaphoreType.DMA((3 * n,))],
    )(*arrs)


def _row_tile(rows, cap=4096, mult=16):
    best = None
    for t in range(mult, min(rows, cap) + 1, mult):
        if rows % t == 0:
            best = t
    return rows if best is None else best


def _pair_sum(half, own, own_spec, got, got_spec, out_shape, out_spec, grid):
    def body(_, a_ref, b_ref, o_ref):
        o_ref[...] = (a_ref[...] + b_ref[...].astype(F32)).astype(o_ref.dtype)

    return _pcall(
        body, name="pair_sum",
        grid_spec=pltpu.PrefetchScalarGridSpec(num_scalar_prefetch=1, grid=grid, in_specs=[own_spec, got_spec],
                                               out_specs=out_spec),
        out_shape=out_shape, compiler_params=_seq(len(grid)))(half, own, got)


def _chip_sum(ids, part, met, fill, layer=0, stack=1):
    _, _, rows, n = part.shape
    tr = _row_tile(rows, cap=max(16, (1 << 18) // n))
    first = isinstance(stack, int)

    def body(_, own_ref, a_ref, b_ref, c_ref, *rest):
        acc = own_ref[...].astype(F32) + a_ref[...].astype(F32)
        acc = acc + b_ref[...].astype(F32)
        rest[-1][...] = acc + c_ref[...].astype(F32)

    blk = (None, None, tr, n)
    other = lambda k: pl.BlockSpec(blk, lambda j, ids: ((ids[0] + k) % 4, 0, j, 0))
    in_specs = [pl.BlockSpec(blk, lambda j, ids: (ids[0], 0, j, 0)), other(1), other(2), other(3)]
    return _pcall(
        body, name="chip_sum",
        grid_spec=pltpu.PrefetchScalarGridSpec(
            num_scalar_prefetch=1, grid=(rows // tr,),
            in_specs=in_specs if first else in_specs + [pl.BlockSpec(memory_space=pl.ANY)],
            out_specs=pl.BlockSpec(blk, lambda j, ids: (layer, ids[1] if fill else 0, j, 0))),
        out_shape=jax.ShapeDtypeStruct(((stack,) if first else stack.shape[:1]) + (2 if fill else 1, rows, n), F32),
        input_output_aliases={} if first else {5: 0},
        compiler_params=_seq())(*((ids, part, met, met, met) if first else (ids, part, met, met, met, stack)))


def _ada_mod(c_all, w_ada, b_ada_cols):
    depth, d, n = w_ada.shape
    nb = c_all.shape[0]

    def body(c_ref, w_ref, b_ref, o_ref):
        cv = c_ref[...]
        ca = _bf(cv * _sigmoid(cv))
        o_ref[0] = _dot(ca, _bf(w_ref[0])) + b_ref[0]

    return _pcall(body, name="ada_mod", grid=(depth,),
                  in_specs=[pl.BlockSpec((nb, d), lambda l: (0, 0)), pl.BlockSpec((1, d, n), lambda l: (l, 0, 0)),
                            pl.BlockSpec((1, 1, n), lambda l: (l, 0, 0))],
                  out_specs=pl.BlockSpec((1, nb, n), lambda l: (l, 0, 0)),
                  out_shape=jax.ShapeDtypeStruct((depth, nb, n), F32), compiler_params=_seq())(c_all, w_ada, b_ada_cols)


def _ada_grad(c_all, dmod_cols, dmod_all):
    nb, d = c_all.shape
    depth, _, n = dmod_cols.shape
    n_all = dmod_all.shape[2]

    def body(c_ref, dm_ref, da_ref, gw_ref, gb_ref):
        cv = c_ref[...]
        ca = _bf(cv * _sigmoid(cv))
        gw_ref[0] = _dot_tn(ca, _bf(dm_ref[0]))
        gb_ref[0] = _colsum(da_ref[0])

    return _pcall(body, name="ada_grad", grid=(depth,),
                  in_specs=[pl.BlockSpec((nb, d), lambda l: (0, 0)), pl.BlockSpec((1, nb, n), lambda l: (l, 0, 0)),
                            pl.BlockSpec((1, nb, n_all), lambda l: (l, 0, 0))],
                  out_specs=[pl.BlockSpec((1, d, n), lambda l: (l, 0, 0)), pl.BlockSpec((1, 1, n_all), lambda l: (l, 0, 0))],
                  out_shape=[jax.ShapeDtypeStruct((depth, d, n), F32), jax.ShapeDtypeStruct((depth, 1, n_all), F32)],
                  compiler_params=_seq())(c_all, dmod_cols, dmod_all)


def _adamw(w, g, m, v):
    shape = w.shape
    cols = shape[-1]
    rows = w.size // cols
    w2, g2, m2, v2 = (t.reshape(rows, cols) for t in (w, g, m, v))
    tr = _row_tile(rows, cap=max(8, (1 << 18) // cols), mult=8)

    def body(w_ref, g_ref, m_ref, v_ref, d_ref, mo_ref, vo_ref):
        gv = g_ref[...]
        mn = ADAM_B1 * m_ref[...] + (1.0 - ADAM_B1) * gv
        vn = ADAM_B2 * v_ref[...] + (1.0 - ADAM_B2) * (gv * gv)
        m_hat = mn / (1.0 - ADAM_B1 ** ADAM_STEP)
        v_hat = vn / (1.0 - ADAM_B2 ** ADAM_STEP)
        d_ref[...] = -ADAM_LR * (m_hat / (jnp.sqrt(v_hat) + ADAM_EPS) + ADAM_WD * w_ref[...])
        mo_ref[...] = mn
        vo_ref[...] = vn

    blk = pl.BlockSpec((tr, cols), lambda i: (i, 0))
    outs = _pcall(body, name="adamw", grid=(rows // tr,), in_specs=[blk] * 4, out_specs=[blk] * 3,
                  out_shape=[jax.ShapeDtypeStruct((rows, cols), F32)] * 3, compiler_params=_seq())(w2, g2, m2, v2)
    return tuple(o.reshape(shape) for o in outs)


WEIGHTS = ["norm_g", "w_ada", "b_ada", "w_in", "rg_conv_w", "rg_conv_b", "rg_w_a", "rg_b_a", "rg_w_x", "rg_b_x",
           "rg_lambda", "ml_conv_w", "ml_conv_b", "ml_w_q", "ml_w_k", "ml_w_v", "ml_w_if", "ml_b_if", "ml_norm_g",
           "w_out", "final_g"]
SMALL_SHARDED = {"ml_w_qkv": 2, "rg_conv_w": 1, "ml_conv_w": 1, "ml_w_if": 0}
REPLICATED = ["rg_w_a", "rg_w_x", "norm_g", "rg_conv_b", "rg_b_a", "rg_b_x", "rg_lambda", "ml_conv_b", "ml_norm_g",
              "ml_b_if"]
LANES = 128


def _to_pieces(g, axis):
    shp = g.shape
    g = g.reshape(shp[:axis] + (4, 2, shp[axis] // 8) + shp[axis + 1:])
    g = jnp.moveaxis(g, (axis, axis + 1), (0, 1))
    return g.reshape(4, 2, -1)


def _from_pieces(p, shard_shape, axis):
    k = p.shape[0]
    rest = shard_shape[:axis] + (shard_shape[axis] // k,) + shard_shape[axis + 1:]
    t = jnp.moveaxis(p.reshape((k,) + rest), 0, axis)
    return t.reshape(shard_shape)


def _pad_rows(flat, mult):
    n = flat.shape[-1]
    pad = (-n) % mult
    if pad:
        flat = jnp.concatenate([flat, jnp.zeros(flat.shape[:-1] + (pad,), flat.dtype)], axis=-1)
    return flat


def kernel(x, c, norm_g, w_ada, b_ada, w_in, rg_conv_w, rg_conv_b, rg_w_a, rg_b_a, rg_w_x, rg_b_x, rg_lambda, ml_conv_w, ml_conv_b, ml_w_q, ml_w_k, ml_w_v, ml_w_if, ml_b_if, ml_norm_g, w_out, final_g, loss_target, m_norm_g, m_w_ada, m_b_ada, m_w_in, m_rg_conv_w, m_rg_conv_b, m_rg_w_a, m_rg_b_a, m_rg_w_x, m_rg_b_x, m_rg_lambda, m_ml_conv_w, m_ml_conv_b, m_ml_w_q, m_ml_w_k, m_ml_w_v, m_ml_w_if, m_ml_b_if, m_ml_norm_g, m_w_out, m_final_g, v_norm_g, v_w_ada, v_b_ada, v_w_in, v_rg_conv_w, v_rg_conv_b, v_rg_w_a, v_rg_b_a, v_rg_w_x, v_rg_b_x, v_rg_lambda, v_ml_conv_w, v_ml_conv_b, v_ml_w_q, v_ml_w_k, v_ml_w_v, v_ml_w_if, v_ml_b_if, v_ml_norm_g, v_w_out, v_final_g):
    given = dict(locals())
    ax, ay, ac = lax.axis_index("x"), lax.axis_index("y"), lax.axis_index("c")
    chip = 2 * ax + ay
    me = 2 * chip + ac
    depth, d = norm_g.shape
    n_ada = w_ada.shape[2]
    pick = lambda a, i, axis=0: lax.dynamic_index_in_dim(a, i, axis, keepdims=False)

    convs = jnp.stack([rg_conv_w, ml_conv_w])
    n_conv = 2 * depth * CONV_WIDTH // 4
    blk = jnp.concatenate([c, convs.reshape(n_conv, d), jnp.zeros((8 - 1 - n_conv, d), F32)], axis=0)
    g0 = _all_gather8([blk], pltpu.VMEM)[0].reshape(8, 8, d)
    c_all = g0[:, 0, :]
    conv_full = g0[0::2, 1:1 + n_conv].reshape(4, 2, depth, CONV_WIDTH, d // 4)
    conv_full = conv_full.transpose(1, 2, 3, 0, 4).reshape(2, depth, CONV_WIDTH, d)

    b_cols = lax.dynamic_slice_in_dim(b_ada, chip * n_ada, n_ada, axis=1)[:, None, :]
    mod_part = _ada_mod(c_all, w_ada, b_cols)
    g1 = _all_gather8([mod_part.transpose(1, 0, 2).reshape(8, depth * n_ada)], pltpu.VMEM)[0]
    g1 = g1.reshape(8, 8, depth, n_ada)[0::2]
    mod_me = pick(g1.transpose(1, 2, 0, 3).reshape(8, depth, 4 * n_ada), me)

    def half_of(w, axis):
        n = w.shape[axis] // 2
        return lax.dynamic_slice_in_dim(w, ac * n, n, axis).astype(BF16)

    n_sh = w_in.shape[2]
    heads, hd_cut, hd = ml_w_q.shape[1:]

    def blocks_of(l):
        wqkv = jnp.stack([ml_w_q[l], ml_w_k[l], ml_w_v[l]])
        return [half_of(w_in[l], 0), half_of(w_out[l], 0), half_of(wqkv, 2).reshape(-1, hd), half_of(ml_w_if[l], 0)]

    def layer_of(l, gathered):
        w4, w_out_b, wqkv_g, wif = gathered
        wqkv_b = _from_pieces(wqkv_g.reshape(8, -1), (3, heads, hd, hd), 2)
        return dict(
            norm_g=norm_g[l][None], shift=mod_me[l, 0:d][None], scale=mod_me[l, d:2 * d][None],
            gate=mod_me[l, 2 * d:3 * d][None], w4=w4.reshape(4, d, n_sh),
            rg_conv_w=conv_full[0, l], rg_conv_b=rg_conv_b[l][None], rg_wa_b=_bf(rg_w_a[l]), rg_ba=rg_b_a[l][None],
            rg_wx_b=_bf(rg_w_x[l]), rg_bx=rg_b_x[l][None], rg_lam=rg_lambda[l][None],
            ml_conv_w=conv_full[1, l], ml_conv_b=ml_conv_b[l][None], wqkv_b=wqkv_b, wif_b=wif, wift_b=wif.T,
            b_if=ml_b_if[l][None], b_ift=ml_b_if[l][:, None], ml_g=ml_norm_g[l][None], w_out_b=w_out_b)

    landing = lambda b: jax.ShapeDtypeStruct((4, 2) + b.shape, b.dtype)
    layers = [layer_of(0, _all_gather8(blocks_of(0), pltpu.HBM))]
    saved = []
    xl = x[0]
    for l in range(depth):
        rides = (None, None)
        if l + 1 < depth:
            nxt = blocks_of(l + 1)
            rides = (Ride(nxt[:1], [landing(nxt[0])], False), Ride(nxt[1:], [landing(b) for b in nxt[1:]], False))
        xl, s, got = _layer_fwd(xl, layers[l], rides)
        saved.append(s)
        if l + 1 < depth:
            layers.append(layer_of(l + 1, [t.reshape(-1, t.shape[-1]) for t in _sib_fill(got)]))
    dx, g_final, loss = _final_loss(xl, final_g[None], loss_target[0])

    half = ac.reshape(1)
    ids = jnp.stack([chip, ac])
    r_out = w_out.shape[1] // 2

    def halves_summed(g, slabs):
        g_out5 = g["w_out"].reshape(1, 4, 2, r_out, d)
        got_in, got_out, *got_slabs = _sib_halves(g["w_in"], g_out5, slabs)
        part_in = _pair_sum(
            half, g["w_in"], pl.BlockSpec((None, d // 2, n_sh), lambda s, h: (0, h[0], s)),
            got_in, pl.BlockSpec((None, None, d // 2, n_sh), lambda s, h: (0, s, 0, 0)),
            jax.ShapeDtypeStruct((4, 1, d // 2, n_sh), BF16),
            pl.BlockSpec((None, None, d // 2, n_sh), lambda s, h: (s, 0, 0, 0)), (4,))
        part_out = _pair_sum(
            half, g_out5, pl.BlockSpec((None, None, None, r_out, d), lambda s, h: (0, s, h[0], 0, 0)),
            got_out, pl.BlockSpec((None, None, r_out, d), lambda s, h: (0, s, 0, 0)),
            jax.ShapeDtypeStruct((4, 1, r_out, d), BF16),
            pl.BlockSpec((None, None, r_out, d), lambda s, h: (s, 0, 0, 0)), (4,))
        return [part_in, part_out], got_slabs

    grads, dmods, parts, mets = [None] * depth, [None] * depth, [None] * depth, [None] * depth
    ride = None
    for l in reversed(range(depth)):
        dx, grads[l], dmods[l], got = _layer_bwd(dx, layers[l], saved[l], ride)
        if ride is not None:
            mets[l + 1] = got
        if l > 0:
            parts[l], _ = halves_summed(grads[l], [])
            ride = Ride(parts[l], [jax.ShapeDtypeStruct(t.shape, t.dtype) for t in parts[l]], True)

    dm_blk = jnp.concatenate(dmods + [jnp.zeros((8 - depth, 3 * d), F32)], axis=0)
    dm_all = _all_gather8([dm_blk], pltpu.VMEM)[0].reshape(8, 8, 3 * d)[:, :depth].transpose(1, 0, 2)
    dm_cols = lax.dynamic_slice_in_dim(dm_all, chip * n_ada, n_ada, axis=2)
    g_w_ada, g_b_ada = _ada_grad(c_all, dm_cols, dm_all)

    sm = jnp.concatenate([_to_pieces(grads[l][name], axis) for l in range(depth) for name, axis in SMALL_SHARDED.items()],
                         axis=-1)
    sm = _pad_rows(sm, 16 * LANES)
    n_sm = sm.shape[-1] // LANES
    sm = sm.transpose(1, 0, 2).reshape(2, 4 * n_sm, LANES)
    rep = [grads[l][name].reshape(-1) for l in range(depth) for name in REPLICATED[:-1]]
    rep += [_pad_rows(grads[l]["ml_b_if"].reshape(-1), LANES) for l in range(depth)]
    rep += [g_final.reshape(-1), loss.reshape(-1)]
    rep = _pad_rows(jnp.concatenate(rep), 8 * 8 * LANES)
    n_rep = rep.shape[0] // (8 * LANES)
    rep = rep.reshape(4, 2, n_rep, LANES).transpose(1, 0, 2, 3).reshape(2, 4 * n_rep, LANES)
    parts[0], (got_sm, got_rep) = halves_summed(grads[0], [sm, rep])

    def slab_sum(slab, got, rows, dtype):
        blk = pl.BlockSpec((rows, LANES), lambda s, h: (s, 0))
        return _pair_sum(half, slab, pl.BlockSpec((None, rows, LANES), lambda s, h: (h[0], s, 0)), got, blk,
                         jax.ShapeDtypeStruct((4 * rows, LANES), dtype), blk, (4,)).reshape(4, 1, rows, LANES)

    part_sm = slab_sum(sm, got_sm, n_sm, BF16)
    part_rep = slab_sum(rep, got_rep, n_rep, F32)
    *mets[0], met_sm, met_rep = _chip_exchange(parts[0] + [part_sm, part_rep])
    both_in, both_out = depth, depth
    for l in range(depth):
        both_in = _chip_sum(ids, parts[l][0], mets[l][0], True, l, both_in)
        both_out = _chip_sum(ids, parts[l][1], mets[l][1], True, l, both_out)
    both_in, both_out, both_sm = _sib_fill([both_in, both_out, _chip_sum(ids, part_sm, met_sm, True)])
    red_rep = _chip_sum(ids, part_rep, met_rep, False).reshape(n_rep, LANES)
    rep_all = _all_gather8([red_rep], pltpu.VMEM)[0].reshape(-1)

    g = dict(w_ada=g_w_ada, b_ada=g_b_ada.reshape(b_ada.shape), w_in=both_in.reshape(w_in.shape),
             w_out=both_out.reshape(w_out.shape))
    shard = both_sm.reshape(2, -1)
    off = 0
    per_layer = {name: [] for name in SMALL_SHARDED}
    for l in range(depth):
        for name, axis in SMALL_SHARDED.items():
            shp = (3,) + ml_w_q.shape[1:] if name == "ml_w_qkv" else given[name].shape[1:]
            n = grads[l][name].size // 8
            per_layer[name].append(_from_pieces(shard[:, off:off + n], shp, axis))
            off += n
    for name in SMALL_SHARDED:
        g[name] = jnp.stack(per_layer[name])
    for i, name in enumerate(["ml_w_q", "ml_w_k", "ml_w_v"]):
        g[name] = g["ml_w_qkv"][:, i]
    off = 0
    per_layer = {name: [] for name in REPLICATED}
    for l in range(depth):
        for name in REPLICATED[:-1]:
            n = given[name][l].size
            per_layer[name].append(rep_all[off:off + n].reshape(given[name].shape[1:]))
            off += n
    for l in range(depth):
        n = given["ml_b_if"][l].size
        per_layer["ml_b_if"].append(rep_all[off:off + n])
        off += LANES
    for name in REPLICATED:
        g[name] = jnp.stack(per_layer[name])
    g["final_g"] = rep_all[off:off + d]
    loss_all = rep_all[off + d]

    deltas, new_m, new_v = [], [], []
    for name in WEIGHTS:
        dl, mn, vn = _adamw(given[name], g[name], given["m_" + name], given["v_" + name])
        deltas.append(dl)
        new_m.append(mn)
        new_v.append(vn)
    return (loss_all, dx[None], *[g[name] for name in WEIGHTS], *deltas, *new_m, *new_v)
```

```python
import functools
from typing import NamedTuple

import jax
import jax.numpy as jnp
from jax import lax
from jax.experimental import pallas as pl
from jax.experimental.pallas import tpu as pltpu

F32 = jnp.float32
BF16 = jnp.bfloat16

EPS = 1e-6
RG_C = 8.0
CONV_WIDTH = 4
ML_CHUNK = 128
HALO = 8
ADAM_LR = 0.001
ADAM_B1 = 0.9
ADAM_B2 = 0.999
ADAM_EPS = 1e-08
ADAM_WD = 0.01
ADAM_STEP = 10
MESH = pl.DeviceIdType.MESH


def _pcall(body, **kw):
    return pl.pallas_call(body, **kw)


class Ride(NamedTuple):
    srcs: list
    dst_shapes: list
    sliced: bool


def _pcall_ride(body, ride, *, grid, in_specs, out_specs, out_shape, args, scratch_shapes=(), **kw):
    n_in, n_out, n_scr = len(in_specs), len(out_specs), len(scratch_shapes)
    if ride is None:
        res = _pcall(body, grid=grid, in_specs=in_specs, out_specs=out_specs, out_shape=out_shape,
                     scratch_shapes=list(scratch_shapes), **kw)(*args)
        return res, []
    nr = len(ride.srcs)

    def riding(*refs):
        ins, rsrc = refs[:n_in], refs[n_in:n_in + nr]
        outs, rdst = refs[n_in + nr:n_in + nr + n_out], refs[n_in + nr + n_out:n_in + 2 * nr + n_out]
        scr = refs[n_in + 2 * nr + n_out:n_in + 2 * nr + n_out + n_scr]
        send_sems, recv_sems, local_sems = refs[n_in + 2 * nr + n_out + n_scr:]
        x, y, c = _me()
        me_s = 2 * x + y
        chips = [(1 - x, y), (x, 1 - y), (1 - x, 1 - y)]
        copies, local = [], []
        for i in range(nr):
            for k, (px, py) in enumerate(chips):
                src = rsrc[i].at[2 * px + py] if ride.sliced else rsrc[i]
                dst = rdst[i].at[me_s] if ride.sliced else rdst[i].at[me_s, c]
                copies.append(_remote(src, dst, send_sems.at[3 * i + k], recv_sems.at[3 * i + k], (px, py, c)))
            if not ride.sliced:
                local.append(pltpu.make_async_copy(rsrc[i], rdst[i].at[me_s, c], local_sems.at[i]))
        first = functools.reduce(jnp.logical_and, [pl.program_id(a) == 0 for a in range(len(grid))])
        last = functools.reduce(jnp.logical_and, [pl.program_id(a) == grid[a] - 1 for a in range(len(grid))])

        @pl.when(first)
        def _():
            for cp in copies + local:
                cp.start()

        body(*ins, *outs, *scr)

        @pl.when(last)
        def _():
            for cp in copies:
                cp.wait_recv()
            for cp in copies:
                cp.wait_send()
            for cp in local:
                cp.wait()

    hbm = pl.BlockSpec(memory_space=pltpu.HBM)
    res = _pcall(
        riding, grid=grid, in_specs=list(in_specs) + [hbm] * nr, out_specs=list(out_specs) + [hbm] * nr,
        out_shape=list(out_shape) + list(ride.dst_shapes),
        scratch_shapes=list(scratch_shapes) + [pltpu.SemaphoreType.DMA((3 * nr,)), pltpu.SemaphoreType.DMA((3 * nr,)),
                                               pltpu.SemaphoreType.DMA((nr,))], **kw)(*args, *ride.srcs)
    return res[:n_out], res[n_out:]


def _seq(n=1):
    return pltpu.CompilerParams(dimension_semantics=("arbitrary",) * n)


def _dot(a, b):
    return jnp.dot(a, b, preferred_element_type=F32)


def _dot_nt(a, b):
    return lax.dot_general(a, b, (((1,), (1,)), ((), ())), preferred_element_type=F32)


def _dot_tn(a, b):
    return lax.dot_general(a, b, (((0,), (0,)), ((), ())), preferred_element_type=F32)


def _bf(x):
    return x.astype(BF16)


def _sigmoid(x):
    return 0.5 * jnp.tanh(0.5 * x) + 0.5


def _log1p(z):
    u = 1.0 + z
    return jnp.where(u == 1.0, z, jnp.log(u) * (z / jnp.where(u == 1.0, 1.0, u - 1.0)))


def _softplus(x):
    return jnp.maximum(x, 0.0) + _log1p(jnp.exp(-jnp.abs(x)))


def _log_sigmoid(x):
    return -_softplus(-x)


def _one_minus_sq(a, log_a):
    x = 2.0 * log_a
    small = -x * (1.0 + x * (0.5 + x * (1.0 / 6.0)))
    return jnp.where(x > -0.004, small, 1.0 - a * a)


def _dsilu(x, s):
    return s * (1.0 + x * (1.0 - s))


def _rowsum(x):
    return jnp.sum(x, axis=1, keepdims=True)


def _colsum(x):
    return jnp.sum(x, axis=0, keepdims=True)


def _shift_down(win, s):
    return win if s == 0 else pltpu.roll(win, s, 0)


def _shift_up(win, s):
    return win if s == 0 else pltpu.roll(win, win.shape[0] - s, 0)


def _conv_taps(win):
    return [_shift_down(win, CONV_WIDTH - 1 - k)[HALO:] for k in range(CONV_WIDTH)]


def _conv_fwd(taps, w_ref, b_ref):
    acc = b_ref[...] + w_ref[CONV_WIDTH - 1:CONV_WIDTH, :] * taps[CONV_WIDTH - 1]
    for k in range(CONV_WIDTH - 1):
        acc = acc + w_ref[k:k + 1, :] * taps[k]
    return acc


def _split3(x):
    hi = _bf(x)
    r1 = x - hi.astype(F32)
    mid = _bf(r1)
    lo = _bf(r1 - mid.astype(F32))
    return hi, mid, lo


def _tri_dot_left(tri, x):
    hi, mid, lo = _split3(x)
    return _dot(tri, hi) + _dot(tri, mid) + _dot(tri, lo)


def _tri_dot_right(x, tri):
    hi, mid, lo = _split3(x)
    return _dot(hi, tri) + _dot(mid, tri) + _dot(lo, tri)


def _tile(n, want):
    t = min(n, want)
    assert n % t == 0
    return t


def _ln_inproj(x, g, scale, shift, w4):
    s_len, d = x.shape
    nj, _, nsh = w4.shape
    tm = _tile(s_len, 1024)

    def body(x_ref, g_ref, sc_ref, sh_ref, w_ref, h_ref, u_ref, hs):
        @pl.when(pl.program_id(1) == 0)
        def _():
            xv = x_ref[...]
            r = lax.rsqrt(jnp.mean(xv * xv, axis=-1, keepdims=True) + EPS)
            hv = (xv * r * g_ref[...]) * (1.0 + sc_ref[...]) + sh_ref[...]
            hs[...] = _bf(hv)
            h_ref[...] = hs[...]

        u_ref[...] = _dot(hs[...], w_ref[0])

    vec = pl.BlockSpec((1, d), lambda i, j: (0, 0))
    return _pcall(
        body, name="ln_inproj", grid=(s_len // tm, nj),
        in_specs=[pl.BlockSpec((tm, d), lambda i, j: (i, 0)), vec, vec, vec,
                  pl.BlockSpec((1, d, nsh), lambda i, j: (j, 0, 0))],
        out_specs=[pl.BlockSpec((tm, d), lambda i, j: (i, 0)), pl.BlockSpec((tm, nsh), lambda i, j: (i, j))],
        out_shape=[jax.ShapeDtypeStruct((s_len, d), BF16), jax.ShapeDtypeStruct((s_len, nj * nsh), F32)],
        scratch_shapes=[pltpu.VMEM((tm, d), BF16)],
        compiler_params=_seq(2),
    )(x, g, scale, shift, w4)


def _rg_gates(xc, wa_ref, ba_ref, wx_ref, bx_ref, lam_ref):
    heads, hd, _ = wa_ref.shape
    xb = _bf(xc)
    ga = jnp.concatenate([_dot(xb[:, h * hd:(h + 1) * hd], wa_ref[h]) for h in range(heads)], axis=1) + ba_ref[...]
    gx = jnp.concatenate([_dot(xb[:, h * hd:(h + 1) * hd], wx_ref[h]) for h in range(heads)], axis=1) + bx_ref[...]
    r = _sigmoid(ga)
    ig = _sigmoid(gx)
    sp = _softplus(-lam_ref[...])
    log_a = (-RG_C) * r * sp
    a = jnp.exp(log_a)
    mult = jnp.sqrt(_one_minus_sq(a, log_a))
    return r, ig, sp, log_a, a, mult


def _scan_groups(a, u, reverse):
    n, c = a.shape
    a = a.reshape(n // 8, 8, c)
    u = u.reshape(n // 8, 8, c)
    row = lax.broadcasted_iota(jnp.int32, a.shape, 1)
    for k in (1, 2, 4):
        sft = 8 - k if reverse else k
        a_sh, u_sh = pltpu.roll(a, sft, 1), pltpu.roll(u, sft, 1)
        ok = row < 8 - k if reverse else row >= k
        u = jnp.where(ok, a * u_sh + u, u)
        a = jnp.where(ok, a * a_sh, a)
    return a.reshape(n, c), u.reshape(n, c)


def _rg_fwd(u, conv_w, conv_b, wa_b, ba, wx_b, bx, lam, ride=None):
    s_len = u.shape[0]
    d = conv_w.shape[1]
    tm = _tile(s_len, 256)
    per = tm // HALO

    def body(x_ref, xp_ref, z_ref, cw_ref, cb_ref, wa_ref, ba_ref, wx_ref, bx_ref, lam_ref,
             hh_ref, y_ref, carry):
        i = pl.program_id(0)

        @pl.when(i == 0)
        def _():
            carry[...] = jnp.zeros_like(carry)

        prev = jnp.where(i == 0, 0.0, xp_ref[...])
        xc = _conv_fwd(_conv_taps(jnp.concatenate([prev, x_ref[...]], axis=0)), cw_ref, cb_ref)
        _, ig, _, _, a, mult = _rg_gates(xc, wa_ref, ba_ref, wx_ref, bx_ref, lam_ref)
        ca, cu = _scan_groups(a, mult * (ig * xc), reverse=False)
        c = carry[0:1, :]
        for j in range(per):
            blk = ca[j * 8:(j + 1) * 8] * c + cu[j * 8:(j + 1) * 8]
            hh_ref[j * 8:(j + 1) * 8, :] = blk
            c = blk[7:8]
        carry[0:1, :] = c
        z = z_ref[...]
        y_ref[0] = _bf(hh_ref[...] * (z * _sigmoid(z)))

    vec = pl.BlockSpec((1, d), lambda i: (0, 0))
    whole3 = lambda a: pl.BlockSpec(a.shape, lambda i: (0, 0, 0))
    return _pcall_ride(
        body, ride, name="rg_fwd", grid=(s_len // tm,),
        in_specs=[pl.BlockSpec((tm, d), lambda i: (i, 0)),
                  pl.BlockSpec((HALO, d), lambda i: (jnp.maximum(i * per - 1, 0), 0)),
                  pl.BlockSpec((tm, d), lambda i: (i, 1)),
                  pl.BlockSpec((CONV_WIDTH, d), lambda i: (0, 0)), vec,
                  whole3(wa_b), vec, whole3(wx_b), vec, vec],
        out_specs=[pl.BlockSpec((tm, d), lambda i: (i, 0)), pl.BlockSpec((1, tm, d), lambda i: (0, i, 0))],
        out_shape=[jax.ShapeDtypeStruct((s_len, d), F32), jax.ShapeDtypeStruct((2, s_len, d), BF16)],
        scratch_shapes=[pltpu.VMEM((8, d), F32)],
        compiler_params=_seq(),
        args=(u, u, u, conv_w, conv_b, wa_b, ba, wx_b, bx, lam))


def _ml_pre(u, conv_w, conv_b, wqkv_b, wif_b, wift_b, b_if, b_ift):
    s_len = u.shape[0]
    d = conv_w.shape[1]
    _, heads, hd, _ = wqkv_b.shape
    ng = 2 * heads
    tm = _tile(s_len, 256)
    per = tm // HALO

    def body(x_ref, xp_ref, cw_ref, cb_ref, w_ref, wif_ref, wift_ref, bif_ref, bift_ref,
             qkv_ref, gt_ref, gtt_ref):
        i = pl.program_id(0)
        prev = jnp.where(i == 0, 0.0, xp_ref[...])
        xm = x_ref[...]
        pre = _conv_fwd(_conv_taps(jnp.concatenate([prev, xm], axis=0)), cw_ref, cb_ref)
        xcb = _bf(pre * _sigmoid(pre))
        xmb = _bf(xm)
        for h in range(heads):
            hs = slice(h * hd, (h + 1) * hd)
            qkv_ref[0, :, hs] = _bf(_dot(xcb[:, hs], w_ref[0, h]))
            qkv_ref[1, :, hs] = _bf(_dot(xcb[:, hs], w_ref[1, h]))
            qkv_ref[2, :, hs] = _bf(_dot(xmb[:, hs], w_ref[2, h]))
        qb, kb, vb = qkv_ref[0], qkv_ref[1], qkv_ref[2]
        gt_ref[...] = (_dot(qb, wif_ref[0:d, :]) + _dot(kb, wif_ref[d:2 * d, :]) + _dot(vb, wif_ref[2 * d:3 * d, :])
                       + bif_ref[...])
        gtt_ref[...] = (_dot_nt(wift_ref[:, 0:d], qb) + _dot_nt(wift_ref[:, d:2 * d], kb)
                        + _dot_nt(wift_ref[:, 2 * d:3 * d], vb) + bift_ref[...])

    vec = pl.BlockSpec((1, d), lambda i: (0, 0))
    whole2 = lambda a: pl.BlockSpec(a.shape, lambda i: (0, 0))
    return _pcall(
        body, name="ml_pre", grid=(s_len // tm,),
        in_specs=[pl.BlockSpec((tm, d), lambda i: (i, 2)),
                  pl.BlockSpec((HALO, d), lambda i: (jnp.maximum(i * per - 1, 0), 2)),
                  pl.BlockSpec((CONV_WIDTH, d), lambda i: (0, 0)), vec,
                  pl.BlockSpec(wqkv_b.shape, lambda i: (0, 0, 0, 0)), whole2(wif_b), whole2(wift_b), whole2(b_if),
                  whole2(b_ift)],
        out_specs=[pl.BlockSpec((3, tm, d), lambda i: (0, i, 0)), pl.BlockSpec((tm, ng), lambda i: (i, 0)),
                   pl.BlockSpec((ng, tm), lambda i: (0, i))],
        out_shape=[jax.ShapeDtypeStruct((3, s_len, d), BF16), jax.ShapeDtypeStruct((s_len, ng), F32),
                   jax.ShapeDtypeStruct((ng, s_len), F32)],
        compiler_params=_seq(),
    )(u, u, conv_w, conv_b, wqkv_b, wif_b, wift_b, b_if, b_ift)


def _chunk_gates(gt, gtt, h, heads, tril, triu):
    li_c = gt[:, h:h + 1]
    li_r = gtt[h:h + 1, :]
    gf_c = gt[:, heads + h:heads + h + 1]
    lf_c = _log_sigmoid(gf_c)
    lf_r = _log_sigmoid(gtt[heads + h:heads + h + 1, :])
    b_c = _tri_dot_left(tril, lf_c)
    b_r = _tri_dot_right(lf_r, triu)
    return li_c, li_r, gf_c, b_c, b_r


def _chunk_weights(li_c, li_r, b_c, b_r, m_prev, causal):
    lc = b_c.shape[0]
    b_last = b_c[lc - 1:lc, :]
    dmat = jnp.where(causal, b_c - b_r + li_r, -jnp.inf)
    m_inter = b_c + m_prev
    m_t = jnp.maximum(m_inter, jnp.max(dmat, axis=1, keepdims=True))
    w_intra = jnp.exp(dmat - m_t)
    w_inter = jnp.exp(m_inter - m_t)
    g_c = b_last - b_c + li_c
    m_new = jnp.maximum(b_last + m_prev, jnp.max(g_c, axis=0, keepdims=True))
    w_state = jnp.exp(g_c - m_new)
    decay = jnp.exp(b_last + m_prev - m_new)
    return m_t, w_intra, w_inter, m_new, w_state, decay


def _tri_masks(lc):
    r = lax.broadcasted_iota(jnp.int32, (lc, lc), 0)
    c = lax.broadcasted_iota(jnp.int32, (lc, lc), 1)
    causal = r >= c
    return causal, causal.astype(BF16), (r <= c).astype(BF16)


def _mlstm_fwd(qkv, gt, gtt, u, ml_g, ycat, ride=None):
    _, s_len, d = qkv.shape
    ng = gt.shape[1]
    heads = ng // 2
    hd = d // heads
    lc = ML_CHUNK
    nc = s_len // lc
    kscale = hd ** -0.5

    def body(qkv_ref, gt_ref, gtt_ref, o_ref, z_ref, g_ref, _, cell_ref, y_ref, cst_ref, nst_ref, mst_ref, cs, ns, ms):
        @pl.when(pl.program_id(0) == 0)
        def _():
            cs[...] = jnp.zeros_like(cs)
            ns[...] = jnp.zeros_like(ns)
            ms[...] = jnp.zeros_like(ms)

        causal, tril, triu = _tri_masks(lc)
        gtv, gttv = gt_ref[...], gtt_ref[...]
        for h in range(heads):
            hs = slice(h * hd, (h + 1) * hd)
            li_c, li_r, _, b_c, b_r = _chunk_gates(gtv, gttv, h, heads, tril, triu)
            m_prev = ms[h][:, 0:1]
            m_t, w_intra, w_inter, m_new, w_state, decay = _chunk_weights(li_c, li_r, b_c, b_r, m_prev, causal)
            qb = qkv_ref[0, :, hs]
            ks = qkv_ref[1, :, hs].astype(F32) * kscale
            kb = _bf(ks)
            vb = qkv_ref[2, :, hs]
            c_old = cs[h]
            n_old = ns[h]
            cst_ref[0, h] = _bf(c_old)
            nst_ref[0, h] = n_old
            mst_ref[0, h] = ms[h]
            s = _dot_nt(qb, kb) * w_intra
            num = _dot(_bf(s), vb) + w_inter * _dot(qb, _bf(c_old))
            den = _rowsum(s) + w_inter * _rowsum(qb.astype(F32) * n_old)
            cell = num / jnp.maximum(jnp.abs(den), jnp.exp(-m_t))
            kw = ks * w_state
            cs[h] = decay * c_old + _dot_tn(_bf(kw), vb)
            ns[h] = decay * n_old + _colsum(kw)
            ms[h] = jnp.broadcast_to(m_new, ms[h].shape)
            cell_ref[:, hs] = cell
            hm = _sigmoid(o_ref[:, hs]) * cell
            hn = hm * lax.rsqrt(jnp.mean(hm * hm, axis=-1, keepdims=True) + EPS)
            z = z_ref[:, hs]
            y_ref[0, :, hs] = _bf((hn * g_ref[:, hs]) * (z * _sigmoid(z)))

    row = pl.BlockSpec((lc, d), lambda c: (c, 0))
    return _pcall_ride(
        body, ride, name="mlstm_fwd", grid=(nc,),
        in_specs=[pl.BlockSpec((3, lc, d), lambda c: (0, c, 0)), pl.BlockSpec((lc, ng), lambda c: (c, 0)),
                  pl.BlockSpec((ng, lc), lambda c: (0, c)),
                  pl.BlockSpec((lc, d), lambda c: (c, 3)), pl.BlockSpec((lc, d), lambda c: (c, 4)),
                  pl.BlockSpec((1, d), lambda c: (0, 0)), pl.BlockSpec(memory_space=pl.ANY)],
        out_specs=[row, pl.BlockSpec((1, lc, d), lambda c: (1, c, 0)),
                   pl.BlockSpec((1, heads, hd, hd), lambda c: (c, 0, 0, 0)),
                   pl.BlockSpec((1, heads, 1, hd), lambda c: (c, 0, 0, 0)),
                   pl.BlockSpec((1, heads, 1, 128), lambda c: (c, 0, 0, 0))],
        out_shape=[jax.ShapeDtypeStruct((s_len, d), F32), jax.ShapeDtypeStruct(ycat.shape, BF16),
                   jax.ShapeDtypeStruct((nc, heads, hd, hd), BF16),
                   jax.ShapeDtypeStruct((nc, heads, 1, hd), F32),
                   jax.ShapeDtypeStruct((nc, heads, 1, 128), F32)],
        scratch_shapes=[pltpu.VMEM((heads, hd, hd), F32), pltpu.VMEM((heads, 1, hd), F32),
                        pltpu.VMEM((heads, 1, 128), F32)],
        input_output_aliases={6: 1},
        compiler_params=_seq(),
        args=(qkv, gt, gtt, u, u, ml_g, ycat))


def _out_proj(ycat, w_out_b, x, gate):
    s_len, d = x.shape
    tm = _tile(s_len, 1024)

    def body(a_ref, w_ref, x_ref, g_ref, y_ref, xn_ref):
        y = _dot(a_ref[0], w_ref[0:d, :]) + _dot(a_ref[1], w_ref[d:2 * d, :])
        y_ref[...] = y
        xn_ref[...] = x_ref[...] + g_ref[...] * y

    row = pl.BlockSpec((tm, d), lambda i: (i, 0))
    return _pcall(
        body, name="out_proj", grid=(s_len // tm,),
        in_specs=[pl.BlockSpec((2, tm, d), lambda i: (0, i, 0)), pl.BlockSpec((2 * d, d), lambda i: (0, 0)), row,
                  pl.BlockSpec((1, d), lambda i: (0, 0))],
        out_specs=[row, row],
        out_shape=[jax.ShapeDtypeStruct((s_len, d), F32)] * 2,
        compiler_params=_seq(),
    )(ycat, w_out_b, x, gate)


def _final_loss(x, g, target):
    s_len, d = x.shape
    tm = _tile(s_len, 256)

    def body(x_ref, g_ref, t_ref, dx_ref, dg_ref, loss_ref):
        @pl.when(pl.program_id(0) == 0)
        def _():
            dg_ref[...] = jnp.zeros_like(dg_ref)
            loss_ref[...] = jnp.zeros_like(loss_ref)

        xv = x_ref[...]
        r = lax.rsqrt(jnp.mean(xv * xv, axis=-1, keepdims=True) + EPS)
        xn = xv * r
        err = xn * g_ref[...] - t_ref[...]
        loss_ref[...] += 0.5 * jnp.sum(jnp.mean(err * err, axis=-1, keepdims=True))
        dout = err * (1.0 / d)
        dg_ref[...] += _colsum(dout * xn)
        dxn = dout * g_ref[...]
        dx_ref[...] = r * (dxn - xn * jnp.mean(dxn * xn, axis=-1, keepdims=True))

    row = pl.BlockSpec((tm, d), lambda i: (i, 0))
    vec = pl.BlockSpec((1, d), lambda i: (0, 0))
    return _pcall(
        body, name="final_loss", grid=(s_len // tm,),
        in_specs=[row, vec, row],
        out_specs=[row, vec, pl.BlockSpec((1, 128), lambda i: (0, 0))],
        out_shape=[jax.ShapeDtypeStruct((s_len, d), F32), jax.ShapeDtypeStruct((1, d), F32),
                   jax.ShapeDtypeStruct((1, 128), F32)],
        compiler_params=_seq(),
    )(x, g, target)


def _out_bwd(dxn, y, gate, w_out_b):
    s_len, d = dxn.shape
    tm = _tile(s_len, 1024)

    def body(dx_ref, y_ref, g_ref, w_ref, dg_ref, dy_ref, dc_ref):
        @pl.when(pl.program_id(0) == 0)
        def _():
            dg_ref[...] = jnp.zeros_like(dg_ref)

        dx = dx_ref[...]
        dg_ref[...] += _colsum(dx * y_ref[...])
        dy = _bf(g_ref[...] * dx)
        dy_ref[...] = dy
        dc_ref[0] = _dot_nt(dy, w_ref[0:d, :])
        dc_ref[1] = _dot_nt(dy, w_ref[d:2 * d, :])

    row = pl.BlockSpec((tm, d), lambda i: (i, 0))
    vec = pl.BlockSpec((1, d), lambda i: (0, 0))
    return _pcall(
        body, name="out_bwd", grid=(s_len // tm,),
        in_specs=[row, row, vec, pl.BlockSpec((2 * d, d), lambda i: (0, 0))],
        out_specs=[vec, row, pl.BlockSpec((2, tm, d), lambda i: (0, i, 0))],
        out_shape=[jax.ShapeDtypeStruct((1, d), F32), jax.ShapeDtypeStruct((s_len, d), BF16),
                   jax.ShapeDtypeStruct((2, s_len, d), F32)],
        compiler_params=_seq(),
    )(dxn, y, gate, w_out_b)


def _grad_matmul(a3, b3, nblk, a_idx, b_idx, out_shape, out_block, out_idx, layer, stack):
    _, s_len, m = a3.shape
    n = b3.shape[2]
    tk = _tile(s_len, 2048)
    first = isinstance(stack, int)

    def body(a_ref, b_ref, *rest):
        o_ref = rest[-1]

        @pl.when(pl.program_id(1) == 0)
        def _():
            o_ref[...] = jnp.zeros_like(o_ref)

        o_ref[...] += _dot_tn(a_ref[0], b_ref[0])

    in_specs = [pl.BlockSpec((1, tk, m), lambda p, t: (a_idx(p), t, 0)),
                pl.BlockSpec((1, tk, n), lambda p, t: (b_idx(p), t, 0))]
    return _pcall(
        body, name="grad_matmul", grid=(nblk, s_len // tk),
        in_specs=in_specs if first else in_specs + [pl.BlockSpec(memory_space=pl.ANY)],
        out_specs=pl.BlockSpec((None,) + out_block, lambda p, t: (layer,) + out_idx(p)),
        out_shape=jax.ShapeDtypeStruct(((stack,) if first else stack.shape[:1]) + out_shape, F32),
        input_output_aliases={} if first else {2: 0},
        compiler_params=_seq(2),
    )(*((a3, b3) if first else (a3, b3, stack)))


DU_PLANE = (2, 3, 4, 0, 1)


def _mlstm_bwd(qkv, gt, gtt, cst, nst, mst, cell, u, ml_g, d_ycat, wif_b, ride=None):
    _, s_len, d = qkv.shape
    ng = gt.shape[1]
    heads = ng // 2
    hd = d // heads
    lc = ML_CHUNK
    nc = s_len // lc
    kscale = hd ** -0.5

    def body(qkv_ref, gt_ref, gtt_ref, cst_ref, nst_ref, mst_ref, cell_ref, o_ref, z_ref, g_ref, dy_ref,
             wif_ref, dqkv_ref, dgt_ref, dbif_ref, du_ref, dg_ref, dcs, dns, dqs, dks, dvs):
        @pl.when(pl.program_id(0) == 0)
        def _():
            dbif_ref[...] = jnp.zeros_like(dbif_ref)
            dcs[...] = jnp.zeros_like(dcs)
            dns[...] = jnp.zeros_like(dns)
            dg_ref[...] = jnp.zeros_like(dg_ref)

        causal, tril, triu = _tri_masks(lc)
        tril_strict = (tril.astype(F32) - (tril * triu).astype(F32)).astype(BF16)
        gtv, gttv = gt_ref[...], gtt_ref[...]
        lane = lax.broadcasted_iota(jnp.int32, (lc, ng), 1)
        dgt = jnp.zeros((lc, ng), F32)
        for h in range(heads):
            hs = slice(h * hd, (h + 1) * hd)
            li_c, li_r, gf_c, b_c, b_r = _chunk_gates(gtv, gttv, h, heads, tril, triu)
            m_prev = mst_ref[0, h][:, 0:1]
            m_t, w_intra, w_inter, _, w_state, decay = _chunk_weights(li_c, li_r, b_c, b_r, m_prev, causal)
            qb = qkv_ref[0, :, hs]
            qf = qb.astype(F32)
            ks = qkv_ref[1, :, hs].astype(F32) * kscale
            kb = _bf(ks)
            vb = qkv_ref[2, :, hs]
            c_b = cst_ref[0, h]
            n_old = nst_ref[0, h]
            s = _dot_nt(qb, kb) * w_intra
            den = _rowsum(s) + w_inter * _rowsum(qf * n_old)
            floor = jnp.exp(-m_t)
            dstab = jnp.maximum(jnp.abs(den), floor)
            cell = cell_ref[:, hs]
            o = o_ref[:, hs]
            so = _sigmoid(o)
            hm = so * cell
            rinv = lax.rsqrt(jnp.mean(hm * hm, axis=-1, keepdims=True) + EPS)
            hn = hm * rinv
            z = z_ref[:, hs]
            sgz = _sigmoid(z)
            sz = z * sgz
            gh = g_ref[:, hs]
            dy = dy_ref[0, :, hs]
            du_ref[1, :, hs] = _bf(dy * (hn * gh) * _dsilu(z, sgz))
            dg_ref[:, hs] += _colsum(dy * hn * sz)
            dhn = dy * gh * sz
            dhm = rinv * (dhn - hn * jnp.mean(dhn * hn, axis=-1, keepdims=True))
            du_ref[0, :, hs] = _bf(dhm * cell * so * (1.0 - so))
            dcell = dhm * so
            dnum = dcell / dstab
            dnb = _bf(dnum)
            dden = -_rowsum(dcell * cell) / dstab * jnp.where(jnp.abs(den) > floor, jnp.where(den > 0.0, 1.0, -1.0), 0.0)
            dst = _dot_nt(dnb, vb) + dden
            dsdb = _bf(dst * w_intra)
            dc_out = dcs[h]
            dn_out = dns[h]
            dcb = _bf(dc_out)
            dq_inter = w_inter * (_dot_nt(dnb, c_b) + dden * n_old)
            dk_inter = w_state * (_dot_nt(vb, dcb) + dn_out)
            dq = _dot(dsdb, kb) + dq_inter
            dk = _dot_tn(dsdb, qb) + dk_inter
            dv = _dot_tn(_bf(s), dnb) + _dot(_bf(ks * w_state), dcb)
            wq = w_inter * qf
            dcs[h] = decay * dc_out + _dot_tn(_bf(wq), dnb)
            dns[h] = decay * dn_out + _colsum(wq * dden)
            pmat = dst * s
            p_rows = _rowsum(pmat)
            p_cols = _rowsum(pmat.T)
            q_in = _rowsum(qf * dq_inter)
            k_in = _rowsum(ks * dk_inter)
            across = decay * (jnp.sum(dc_out * c_b.astype(F32), keepdims=True) + jnp.sum(dn_out * n_old, keepdims=True))
            dli = p_cols + k_in
            dlf = _tri_dot_left(triu, p_rows - p_cols + q_in) + _tri_dot_left(tril_strict, k_in) + across
            dgf = dlf * _sigmoid(-gf_c)
            dgt = dgt + jnp.where(lane == h, dli, 0.0) + jnp.where(lane == heads + h, dgf, 0.0)
            dqs[:, hs] = dq
            dks[:, hs] = dk * kscale
            dvs[:, hs] = dv
        dgt_ref[...] = dgt
        dbif_ref[...] += _colsum(dgt)
        dgb = _bf(dgt)
        dqkv_ref[0] = _bf(dqs[...] + _dot_nt(dgb, wif_ref[0:d, :]))
        dqkv_ref[1] = _bf(dks[...] + _dot_nt(dgb, wif_ref[d:2 * d, :]))
        dqkv_ref[2] = _bf(dvs[...] + _dot_nt(dgb, wif_ref[2 * d:3 * d, :]))

    rev = lambda c: nc - 1 - c
    row = pl.BlockSpec((lc, d), lambda c: (rev(c), 0))
    return _pcall_ride(
        body, ride, name="mlstm_bwd", grid=(nc,),
        in_specs=[pl.BlockSpec((3, lc, d), lambda c: (0, rev(c), 0)), pl.BlockSpec((lc, ng), lambda c: (rev(c), 0)),
                  pl.BlockSpec((ng, lc), lambda c: (0, rev(c))),
                  pl.BlockSpec((1, heads, hd, hd), lambda c: (rev(c), 0, 0, 0)),
                  pl.BlockSpec((1, heads, 1, hd), lambda c: (rev(c), 0, 0, 0)),
                  pl.BlockSpec((1, heads, 1, 128), lambda c: (rev(c), 0, 0, 0)),
                  row, pl.BlockSpec((lc, d), lambda c: (rev(c), 3)), pl.BlockSpec((lc, d), lambda c: (rev(c), 4)),
                  pl.BlockSpec((1, d), lambda c: (0, 0)), pl.BlockSpec((1, lc, d), lambda c: (1, rev(c), 0)),
                  pl.BlockSpec((3 * d, ng), lambda c: (0, 0))],
        out_specs=[pl.BlockSpec((3, lc, d), lambda c: (0, rev(c), 0)), pl.BlockSpec((lc, ng), lambda c: (rev(c), 0)),
                   pl.BlockSpec((1, ng), lambda c: (0, 0)), pl.BlockSpec((2, lc, d), lambda c: (0, rev(c), 0)),
                   pl.BlockSpec((1, d), lambda c: (0, 0))],
        out_shape=[jax.ShapeDtypeStruct((3, s_len, d), BF16), jax.ShapeDtypeStruct((s_len, ng), F32),
                   jax.ShapeDtypeStruct((1, ng), F32), jax.ShapeDtypeStruct((5, s_len, d), BF16),
                   jax.ShapeDtypeStruct((1, d), F32)],
        scratch_shapes=[pltpu.VMEM((heads, hd, hd), F32), pltpu.VMEM((heads, 1, hd), F32)]
        + [pltpu.VMEM((lc, d), F32)] * 3,
        compiler_params=_seq(),
        args=(qkv, gt, gtt, cst, nst, mst, cell, u, u, ml_g, d_ycat, wif_b))


def _conv_bwd_tile(dp, later, taps, cw_ref, gw_ref, gb_ref):
    tm = dp.shape[0]
    dwin = jnp.concatenate([dp, later[...]], axis=0)
    later[...] = dp[0:HALO]
    acc = cw_ref[CONV_WIDTH - 1:CONV_WIDTH, :] * dp
    for k in range(CONV_WIDTH):
        if k < CONV_WIDTH - 1:
            acc = acc + cw_ref[k:k + 1, :] * _shift_up(dwin, CONV_WIDTH - 1 - k)[0:tm]
        gw_ref[k:k + 1, :] += _colsum(dp * taps[k])
    gb_ref[...] += _colsum(dp)
    return acc


def _ml_pre_bwd(dqkv, u, conv_w, conv_b, wqkv_b, du):
    s_len = u.shape[0]
    d = conv_w.shape[1]
    _, heads, hd, _ = wqkv_b.shape
    tm = _tile(s_len, 256)
    per = tm // HALO
    nt = s_len // tm

    def body(dqkv_ref, x_ref, xp_ref, cw_ref, cb_ref, w_ref, _, dx_ref, gw_ref, gcw_ref, gcb_ref, later, dps, dxs):
        i = pl.program_id(0)

        @pl.when(i == 0)
        def _():
            gw_ref[...] = jnp.zeros_like(gw_ref)
            gcw_ref[...] = jnp.zeros_like(gcw_ref)
            gcb_ref[...] = jnp.zeros_like(gcb_ref)
            later[...] = jnp.zeros_like(later)

        prev = jnp.where(i == nt - 1, 0.0, xp_ref[...])
        xm = x_ref[...]
        taps = _conv_taps(jnp.concatenate([prev, xm], axis=0))
        pre = _conv_fwd(taps, cw_ref, cb_ref)
        sg = _sigmoid(pre)
        xcb = _bf(pre * sg)
        xmb = _bf(xm)
        for h in range(heads):
            hs = slice(h * hd, (h + 1) * hd)
            dqh, dkh, dvh = dqkv_ref[0, :, hs], dqkv_ref[1, :, hs], dqkv_ref[2, :, hs]
            dxc = _dot_nt(dqh, w_ref[0, h]) + _dot_nt(dkh, w_ref[1, h])
            dps[:, hs] = dxc * _dsilu(pre[:, hs], sg[:, hs])
            dxs[:, hs] = _dot_nt(dvh, w_ref[2, h])
            gw_ref[0, h] += _dot_tn(xcb[:, hs], dqh)
            gw_ref[1, h] += _dot_tn(xcb[:, hs], dkh)
            gw_ref[2, h] += _dot_tn(xmb[:, hs], dvh)
        dx_ref[0] = _bf(_conv_bwd_tile(dps[...], later, taps, cw_ref, gcw_ref, gcb_ref) + dxs[...])

    rev = lambda i: nt - 1 - i
    vec = pl.BlockSpec((1, d), lambda i: (0, 0))
    cwb = pl.BlockSpec((CONV_WIDTH, d), lambda i: (0, 0))
    whole4 = pl.BlockSpec(wqkv_b.shape, lambda i: (0, 0, 0, 0))
    return _pcall(
        body, name="ml_pre_bwd", grid=(nt,),
        in_specs=[pl.BlockSpec((3, tm, d), lambda i: (0, rev(i), 0)), pl.BlockSpec((tm, d), lambda i: (rev(i), 2)),
                  pl.BlockSpec((HALO, d), lambda i: (jnp.maximum(rev(i) * per - 1, 0), 2)),
                  cwb, vec, whole4, pl.BlockSpec(memory_space=pl.ANY)],
        out_specs=[pl.BlockSpec((1, tm, d), lambda i: (DU_PLANE[2], rev(i), 0)), whole4, cwb, vec],
        out_shape=[jax.ShapeDtypeStruct(du.shape, BF16), jax.ShapeDtypeStruct(wqkv_b.shape, F32),
                   jax.ShapeDtypeStruct((CONV_WIDTH, d), F32), jax.ShapeDtypeStruct((1, d), F32)],
        scratch_shapes=[pltpu.VMEM((HALO, d), F32), pltpu.VMEM((tm, d), F32), pltpu.VMEM((tm, d), F32)],
        input_output_aliases={6: 0},
        compiler_params=_seq(),
    )(dqkv, u, u, conv_w, conv_b, wqkv_b, du)


def _rg_bwd(d_ycat, u, hh, conv_w, conv_b, wa_b, ba, wx_b, bx, lam, du):
    s_len = u.shape[0]
    d = conv_w.shape[1]
    heads, hd, _ = wa_b.shape
    tm = _tile(s_len, 256)
    per = tm // HALO
    nt = s_len // tm

    def body(dy_ref, x_ref, xp_ref, z_ref, hh_ref, hp_ref, cw_ref, cb_ref, wa_ref, ba_ref, wx_ref, bx_ref, lam_ref, _,
             du_ref, gwa_ref, gwx_ref, gba_ref, gbx_ref, glam_ref, gcw_ref, gcb_ref, carry, gbuf, later, dxcs):
        i = pl.program_id(0)
        first = i == nt - 1

        @pl.when(i == 0)
        def _():
            carry[...] = jnp.zeros_like(carry)
            later[...] = jnp.zeros_like(later)
            gwa_ref[...] = jnp.zeros_like(gwa_ref)
            gwx_ref[...] = jnp.zeros_like(gwx_ref)
            gba_ref[...] = jnp.zeros_like(gba_ref)
            gbx_ref[...] = jnp.zeros_like(gbx_ref)
            glam_ref[...] = jnp.zeros_like(glam_ref)
            gcw_ref[...] = jnp.zeros_like(gcw_ref)
            gcb_ref[...] = jnp.zeros_like(gcb_ref)

        prev = jnp.where(first, 0.0, xp_ref[...])
        taps = _conv_taps(jnp.concatenate([prev, x_ref[...]], axis=0))
        xc = _conv_fwd(taps, cw_ref, cb_ref)
        r, ig, sp, log_a, a, mult = _rg_gates(xc, wa_ref, ba_ref, wx_ref, bx_ref, lam_ref)
        z = z_ref[...]
        sgz = _sigmoid(z)
        dy = dy_ref[0]
        hh_v = hh_ref[...]
        du_ref[1] = _bf(dy * hh_v * _dsilu(z, sgz))
        dhh = dy * (z * sgz)
        rows = lax.broadcasted_iota(jnp.int32, a.shape, 0)
        coef = jnp.where(rows == tm - 1, carry[1:2, :], _shift_up(a, 1))
        ca, cu = _scan_groups(coef, dhh, reverse=True)
        c = carry[0:1, :]
        for j in range(per - 1, -1, -1):
            blk = ca[j * 8:(j + 1) * 8] * c + cu[j * 8:(j + 1) * 8]
            gbuf[j * 8:(j + 1) * 8, :] = blk
            c = blk[0:1]
        carry[0:1, :] = c
        carry[1:2, :] = a[0:1]
        g = gbuf[...]
        hprev_tile = jnp.where(first, 0.0, hp_ref[...])
        hprev = _shift_down(jnp.concatenate([hprev_tile, hh_v], axis=0), 1)[HALO:]
        da = g * hprev
        gx_ = g * xc
        d_mult = gx_ * ig
        d_ig = gx_ * mult
        dxc = g * mult * ig
        dlog_a = da * a - d_mult * (a * a / mult)
        d_r = dlog_a * ((-RG_C) * sp)
        glam_ref[...] += _colsum(dlog_a * ((-RG_C) * r)) * (-_sigmoid(-lam_ref[...]))
        d_ga = d_r * r * (1.0 - r)
        d_gx = d_ig * ig * (1.0 - ig)
        gba_ref[...] += _colsum(d_ga)
        gbx_ref[...] += _colsum(d_gx)
        xb = _bf(xc)
        dgab = _bf(d_ga)
        dgxb = _bf(d_gx)
        for h in range(heads):
            hs = slice(h * hd, (h + 1) * hd)
            dxcs[:, hs] = dxc[:, hs] + _dot_nt(dgab[:, hs], wa_ref[h]) + _dot_nt(dgxb[:, hs], wx_ref[h])
            gwa_ref[h] += _dot_tn(xb[:, hs], dgab[:, hs])
            gwx_ref[h] += _dot_tn(xb[:, hs], dgxb[:, hs])
        du_ref[0] = _bf(_conv_bwd_tile(dxcs[...], later, taps, cw_ref, gcw_ref, gcb_ref))

    assert DU_PLANE[0] % 2 == 0 and DU_PLANE[1] == DU_PLANE[0] + 1
    rev = lambda i: nt - 1 - i
    row = pl.BlockSpec((tm, d), lambda i: (rev(i), 0))
    halo_prev = lambda col: pl.BlockSpec((HALO, d), lambda i: (jnp.maximum(rev(i) * per - 1, 0), col))
    vec = pl.BlockSpec((1, d), lambda i: (0, 0))
    cwb = pl.BlockSpec((CONV_WIDTH, d), lambda i: (0, 0))
    whole3 = lambda a: pl.BlockSpec(a.shape, lambda i: (0, 0, 0))
    return _pcall(
        body, name="rg_bwd", grid=(nt,),
        in_specs=[pl.BlockSpec((1, tm, d), lambda i: (0, rev(i), 0)), row, halo_prev(0),
                  pl.BlockSpec((tm, d), lambda i: (rev(i), 1)), row, halo_prev(0),
                  cwb, vec, whole3(wa_b), vec, whole3(wx_b), vec, vec, pl.BlockSpec(memory_space=pl.ANY)],
        out_specs=[pl.BlockSpec((2, tm, d), lambda i: (DU_PLANE[0] // 2, rev(i), 0)), whole3(wa_b), whole3(wa_b),
                   vec, vec, vec, cwb, vec],
        out_shape=[jax.ShapeDtypeStruct(du.shape, BF16), jax.ShapeDtypeStruct(wa_b.shape, F32),
                   jax.ShapeDtypeStruct(wa_b.shape, F32)] + [jax.ShapeDtypeStruct((1, d), F32)] * 3
        + [jax.ShapeDtypeStruct((CONV_WIDTH, d), F32), jax.ShapeDtypeStruct((1, d), F32)],
        scratch_shapes=[pltpu.VMEM((8, d), F32), pltpu.VMEM((tm, d), F32), pltpu.VMEM((HALO, d), F32),
                        pltpu.VMEM((tm, d), F32)],
        input_output_aliases={13: 0},
        compiler_params=_seq(),
    )(d_ycat, u, u, u, hh, hh, conv_w, conv_b, wa_b, ba, wx_b, bx, lam, du)


def _in_bwd(du, w4, x, dxn, g, scale):
    s_len, d = x.shape
    tm = _tile(s_len, 512)
    nsh_chips, _, nsh = w4.shape
    npc = du.shape[0]
    ck = d // 4
    assert nsh % ck == 0 and npc * d == nsh_chips * nsh

    def body(du_ref, w_ref, x_ref, dxn_ref, g_ref, sc_ref, dx_ref, dsh_ref, dsc_ref, dg_ref):
        @pl.when(pl.program_id(0) == 0)
        def _():
            dsh_ref[...] = jnp.zeros_like(dsh_ref)
            dsc_ref[...] = jnp.zeros_like(dsc_ref)
            dg_ref[...] = jnp.zeros_like(dg_ref)

        dh = None
        for q in range(npc * d // ck):
            col = q * ck
            p, pc = col // d, col % d
            s, sc = col // nsh, col % nsh
            t = _dot_nt(du_ref[DU_PLANE[p], :, pc:pc + ck], w_ref[s, :, sc:sc + ck])
            dh = t if dh is None else dh + t
        xv = x_ref[...]
        r = lax.rsqrt(jnp.mean(xv * xv, axis=-1, keepdims=True) + EPS)
        xn = xv * r
        gv = g_ref[...]
        onesc = 1.0 + sc_ref[...]
        dsh_ref[...] += _colsum(dh)
        dsc_ref[...] += _colsum(dh * (xn * gv))
        dg_ref[...] += _colsum(dh * xn * onesc)
        dxh = dh * (gv * onesc)
        dx_ref[...] = dxn_ref[...] + r * (dxh - xn * jnp.mean(dxh * xn, axis=-1, keepdims=True))

    row = pl.BlockSpec((tm, d), lambda i: (i, 0))
    vec = pl.BlockSpec((1, d), lambda i: (0, 0))
    return _pcall(
        body, name="in_bwd", grid=(s_len // tm,),
        in_specs=[pl.BlockSpec((npc, tm, d), lambda i: (0, i, 0)), pl.BlockSpec(w4.shape, lambda i: (0, 0, 0)), row, row,
                  vec, vec],
        out_specs=[row, vec, vec, vec],
        out_shape=[jax.ShapeDtypeStruct((s_len, d), F32)] + [jax.ShapeDtypeStruct((1, d), F32)] * 3,
        compiler_params=_seq(),
    )(du, w4, x, dxn, g, scale)


def _layer_fwd(x, p, rides=(None, None)):
    h_b, u = _ln_inproj(x, p["norm_g"], p["scale"], p["shift"], p["w4"])
    (hh, ycat), got_a = _rg_fwd(u, p["rg_conv_w"], p["rg_conv_b"], p["rg_wa_b"], p["rg_ba"], p["rg_wx_b"], p["rg_bx"],
                                p["rg_lam"], rides[0])
    qkv, gt, gtt = _ml_pre(u, p["ml_conv_w"], p["ml_conv_b"], p["wqkv_b"], p["wif_b"], p["wift_b"], p["b_if"],
                           p["b_ift"])
    (cell, ycat, cst, nst, mst), got_b = _mlstm_fwd(qkv, gt, gtt, u, p["ml_g"], ycat, rides[1])
    y, x_new = _out_proj(ycat, p["w_out_b"], x, p["gate"])
    saved = dict(x=x, h_b=h_b, u=u, hh=hh, qkv=qkv, gt=gt, gtt=gtt, cell=cell, ycat=ycat, cst=cst, nst=nst, mst=mst,
                 y=y)
    return x_new, saved, list(got_a) + list(got_b)


def _layer_bwd(dxn, p, s, ride=None):
    u = s["u"]
    d = dxn.shape[1]
    d_gate, dy_b, d_ycat = _out_bwd(dxn, s["y"], p["gate"], p["w_out_b"])
    g_w_out = _grad_matmul(s["ycat"], dy_b[None], 2, lambda b: b, lambda b: 0, (2 * d, d), (d, d), lambda b: (b, 0),
                           0, 1)
    (dqkv, dgt, g_b_if, du, g_ml_g), got = _mlstm_bwd(s["qkv"], s["gt"], s["gtt"], s["cst"], s["nst"], s["mst"],
                                                      s["cell"], u, p["ml_g"], d_ycat, p["wif_b"], ride)
    ng = dgt.shape[1]
    g_w_if = _grad_matmul(s["qkv"], _bf(dgt)[None], 3, lambda b: b, lambda b: 0, (3 * d, ng), (d, ng),
                          lambda b: (b, 0), 0, 1)[0]
    du, g_wqkv, g_ml_cw, g_ml_cb = _ml_pre_bwd(dqkv, u, p["ml_conv_w"], p["ml_conv_b"], p["wqkv_b"], du)
    du, g_wa, g_wx, g_ba, g_bx, g_lam, g_rg_cw, g_rg_cb = _rg_bwd(d_ycat, u, s["hh"], p["rg_conv_w"], p["rg_conv_b"],
                                                                  p["rg_wa_b"], p["rg_ba"], p["rg_wx_b"], p["rg_bx"],
                                                                  p["rg_lam"], du)
    npc = du.shape[0]
    g_w_in = _grad_matmul(s["h_b"][None], du, npc, lambda b: 0, lambda b: (b + DU_PLANE[0]) % npc, (d, npc * d),
                          (d, d), lambda b: (0, b), 0, 1)
    dx, d_shift, d_scale, g_norm_g = _in_bwd(du, p["w4"], s["x"], dxn, p["norm_g"], p["scale"])
    grads = dict(norm_g=g_norm_g, w_in=g_w_in, rg_conv_w=g_rg_cw, rg_conv_b=g_rg_cb, rg_w_a=g_wa, rg_b_a=g_ba,
                 rg_w_x=g_wx, rg_b_x=g_bx, rg_lambda=g_lam, ml_conv_w=g_ml_cw, ml_conv_b=g_ml_cb, ml_w_qkv=g_wqkv,
                 ml_w_if=g_w_if, ml_b_if=g_b_if, ml_norm_g=g_ml_g, w_out=g_w_out)
    return dx, grads, jnp.concatenate([d_shift, d_scale, d_gate], axis=1), got


def _trunk_fwd_bwd(x, target, final_g, layers):
    saved = []
    for p in layers:
        x, s, _ = _layer_fwd(x, p)
        saved.append(s)
    dx, g_final, loss = _final_loss(x, final_g, target)
    grads, dmods = [], []
    for layer in reversed(range(len(layers))):
        dx, g, dm, _ = _layer_bwd(dx, layers[layer], saved[layer])
        grads.append(g)
        dmods.append(dm)
    return loss, dx, g_final, grads[::-1], dmods[::-1]


def _me():
    return lax.axis_index("x"), lax.axis_index("y"), lax.axis_index("c")


def _remote(src, dst, send_sem, recv_sem, to):
    return pltpu.make_async_remote_copy(src_ref=src, dst_ref=dst, send_sem=send_sem, recv_sem=recv_sem,
                                        device_id=to, device_id_type=MESH)


def _all_gather8(blocks, space):
    n = len(blocks)

    def body(*refs):
        x_refs, out_refs = refs[:n], refs[n:2 * n]
        send_sems, recv_sems, local_sems = refs[2 * n:]
        x, y, c = _me()
        me, sibling = (x, y, c), (x, y, 1 - c)
        chips = [(1 - x, y), (x, 1 - y), (1 - x, 1 - y)]

        def rows(i, px, py, pc):
            m_per = blocks[i].shape[0]
            return out_refs[i].at[pl.ds((4 * px + 2 * py + pc) * m_per, m_per), :]

        def copy(i, k, blk, to, src=None):
            return _remote(rows(i, *blk) if src is None else src, rows(i, *blk), send_sems.at[7 * i + k],
                           recv_sems.at[7 * i + k], to)

        mine = [pltpu.make_async_copy(x_refs[i], rows(i, *me), local_sems.at[i]) for i in range(n)]
        first = []
        for i in range(n):
            first.append(copy(i, 0, me, sibling, src=x_refs[i]))
            first += [copy(i, 1 + j, me, (*chip, c), src=x_refs[i]) for j, chip in enumerate(chips)]
        for cp in mine + first:
            cp.start()
        passed = []
        for j, chip in enumerate(chips):
            for i in range(n):
                copy(i, 1 + j, (*chip, c), me).wait_recv()
                passed.append(copy(i, 4 + j, (*chip, c), sibling))
                passed[-1].start()
        for i in range(n):
            copy(i, 0, sibling, me).wait_recv()
            for j, chip in enumerate(chips):
                copy(i, 4 + j, (*chip, 1 - c), me).wait_recv()
        for cp in first + passed:
            cp.wait_send()
        for cp in mine:
            cp.wait()

    spec = pl.BlockSpec(memory_space=space)
    return _pcall(
        body, name="all_gather8",
        out_shape=[jax.ShapeDtypeStruct((8 * b.shape[0], b.shape[1]), b.dtype) for b in blocks],
        in_specs=[spec] * n, out_specs=[spec] * n,
        scratch_shapes=[pltpu.SemaphoreType.DMA((7 * n,)), pltpu.SemaphoreType.DMA((7 * n,)),
                        pltpu.SemaphoreType.DMA((n,))],
    )(*blocks)


def _sib_halves(g_in, g_out, slabs):
    depth, d, n4 = g_in.shape
    n = n4 // 4
    ns = len(slabs)

    def body(*refs):
        gi, go = refs[0], refs[1]
        sl = refs[2:2 + ns]
        ri, ro = refs[2 + ns], refs[3 + ns]
        rs = refs[4 + ns:4 + 2 * ns]
        send_sems, recv_sems = refs[4 + 2 * ns:]
        x, y, c = _me()
        o = 1 - c
        pairs = [(gi.at[pl.ds(0, depth), pl.ds(o * (d // 2), d // 2), pl.ds(s * n, n)], ri.at[pl.ds(0, depth), s])
                 for s in range(4)]
        pairs.append((go.at[pl.ds(0, depth), pl.ds(0, 4), o], ro))
        pairs += [(sl[i].at[o], rs[i]) for i in range(ns)]
        copies = [_remote(src, dst, send_sems.at[k], recv_sems.at[k], (x, y, o)) for k, (src, dst) in enumerate(pairs)]
        for cp in copies:
            cp.start()
        for cp in copies:
            cp.wait_recv()
        for cp in copies:
            cp.wait_send()

    hbm = pl.BlockSpec(memory_space=pltpu.HBM)
    ncp = 5 + ns
    return _pcall(
        body, name="sib_halves",
        out_shape=[jax.ShapeDtypeStruct((depth, 4, d // 2, n), g_in.dtype),
                   jax.ShapeDtypeStruct(g_out.shape[:2] + g_out.shape[3:], g_out.dtype)]
        + [jax.ShapeDtypeStruct(s.shape[1:], s.dtype) for s in slabs],
        in_specs=[hbm] * (2 + ns), out_specs=[hbm] * (2 + ns),
        scratch_shapes=[pltpu.SemaphoreType.DMA((ncp,)), pltpu.SemaphoreType.DMA((ncp,))],
    )(g_in, g_out, *slabs)


def _sib_fill(boths):
    n = len(boths)

    def body(*refs):
        dst = refs[n:2 * n]
        send_sems, recv_sems = refs[2 * n:]
        x, y, c = _me()
        view = lambda i: dst[i].at[pl.ds(0, boths[i].shape[0]), c]
        copies = [_remote(view(i), view(i), send_sems.at[i], recv_sems.at[i], (x, y, 1 - c)) for i in range(n)]
        for cp in copies:
            cp.start()
        for cp in copies:
            cp.wait_recv()
        for cp in copies:
            cp.wait_send()

    hbm = pl.BlockSpec(memory_space=pltpu.HBM)
    return _pcall(
        body, name="sib_fill",
        out_shape=[jax.ShapeDtypeStruct(b.shape, b.dtype) for b in boths],
        in_specs=[hbm] * n, out_specs=[hbm] * n, input_output_aliases={i: i for i in range(n)},
        scratch_shapes=[pltpu.SemaphoreType.DMA((n,)), pltpu.SemaphoreType.DMA((n,))],
    )(*boths)


def _chip_exchange(arrs):
    n = len(arrs)

    def body(*refs):
        src, dst = refs[:n], refs[n:2 * n]
        send_sems, recv_sems = refs[2 * n:]
        x, y, c = _me()
        me_s = 2 * x + y
        chips = [(1 - x, y), (x, 1 - y), (1 - x, 1 - y)]
        copies = [_remote(src[i].at[2 * px + py], dst[i].at[me_s], send_sems.at[3 * i + k], recv_sems.at[3 * i + k],
                          (px, py, c))
                  for i in range(n) for k, (px, py) in enumerate(chips)]
        for cp in copies:
            cp.start()
        for cp in copies:
            cp.wait_recv()
        for cp in copies:
            cp.wait_send()

    hbm = pl.BlockSpec(memory_space=pltpu.HBM)
    return _pcall(
        body, name="chip_exchange",
        out_shape=[jax.ShapeDtypeStruct(a.shape, a.dtype) for a in arrs],
        in_specs=[hbm] * n, out_specs=[hbm] * n,
        scratch_shapes=[pltpu.SemaphoreType.DMA((3 * n,)), pltpu.SemaphoreType.DMA((3 * n,))],
    )(*arrs)


def _row_tile(rows, cap=4096, mult=16):
    best = None
    for t in range(mult, min(rows, cap) + 1, mult):
        if rows % t == 0:
            best = t
    return rows if best is None else best


def _pair_sum(half, own, own_spec, got, got_spec, out_shape, out_spec, grid):
    def body(_, a_ref, b_ref, o_ref):
        o_ref[...] = (a_ref[...] + b_ref[...].astype(F32)).astype(o_ref.dtype)

    return _pcall(
        body, name="pair_sum",
        grid_spec=pltpu.PrefetchScalarGridSpec(num_scalar_prefetch=1, grid=grid, in_specs=[own_spec, got_spec],
                                               out_specs=out_spec),
        out_shape=out_shape, compiler_params=_seq(len(grid)))(half, own, got)


def _chip_sum(ids, part, met, fill, layer=0, stack=1):
    _, _, rows, n = part.shape
    tr = _row_tile(rows, cap=max(16, (1 << 18) // n))
    first = isinstance(stack, int)

    def body(_, own_ref, a_ref, b_ref, c_ref, *rest):
        acc = own_ref[...].astype(F32) + a_ref[...].astype(F32)
        acc = acc + b_ref[...].astype(F32)
        rest[-1][...] = acc + c_ref[...].astype(F32)

    blk = (None, None, tr, n)
    other = lambda k: pl.BlockSpec(blk, lambda j, ids: ((ids[0] + k) % 4, 0, j, 0))
    in_specs = [pl.BlockSpec(blk, lambda j, ids: (ids[0], 0, j, 0)), other(1), other(2), other(3)]
    return _pcall(
        body, name="chip_sum",
        grid_spec=pltpu.PrefetchScalarGridSpec(
            num_scalar_prefetch=1, grid=(rows // tr,),
            in_specs=in_specs if first else in_specs + [pl.BlockSpec(memory_space=pl.ANY)],
            out_specs=pl.BlockSpec(blk, lambda j, ids: (layer, ids[1] if fill else 0, j, 0))),
        out_shape=jax.ShapeDtypeStruct(((stack,) if first else stack.shape[:1]) + (2 if fill else 1, rows, n), F32),
        input_output_aliases={} if first else {5: 0},
        compiler_params=_seq())(*((ids, part, met, met, met) if first else (ids, part, met, met, met, stack)))


def _ada_mod(c_all, w_ada, b_ada_cols):
    depth, d, n = w_ada.shape
    nb = c_all.shape[0]

    def body(c_ref, w_ref, b_ref, o_ref):
        cv = c_ref[...]
        ca = _bf(cv * _sigmoid(cv))
        o_ref[0] = _dot(ca, _bf(w_ref[0])) + b_ref[0]

    return _pcall(body, name="ada_mod", grid=(depth,),
                  in_specs=[pl.BlockSpec((nb, d), lambda l: (0, 0)), pl.BlockSpec((1, d, n), lambda l: (l, 0, 0)),
                            pl.BlockSpec((1, 1, n), lambda l: (l, 0, 0))],
                  out_specs=pl.BlockSpec((1, nb, n), lambda l: (l, 0, 0)),
                  out_shape=jax.ShapeDtypeStruct((depth, nb, n), F32), compiler_params=_seq())(c_all, w_ada, b_ada_cols)


def _ada_grad(c_all, dmod_cols, dmod_all):
    nb, d = c_all.shape
    depth, _, n = dmod_cols.shape
    n_all = dmod_all.shape[2]

    def body(c_ref, dm_ref, da_ref, gw_ref, gb_ref):
        cv = c_ref[...]
        ca = _bf(cv * _sigmoid(cv))
        gw_ref[0] = _dot_tn(ca, _bf(dm_ref[0]))
        gb_ref[0] = _colsum(da_ref[0])

    return _pcall(body, name="ada_grad", grid=(depth,),
                  in_specs=[pl.BlockSpec((nb, d), lambda l: (0, 0)), pl.BlockSpec((1, nb, n), lambda l: (l, 0, 0)),
                            pl.BlockSpec((1, nb, n_all), lambda l: (l, 0, 0))],
                  out_specs=[pl.BlockSpec((1, d, n), lambda l: (l, 0, 0)), pl.BlockSpec((1, 1, n_all), lambda l: (l, 0, 0))],
                  out_shape=[jax.ShapeDtypeStruct((depth, d, n), F32), jax.ShapeDtypeStruct((depth, 1, n_all), F32)],
                  compiler_params=_seq())(c_all, dmod_cols, dmod_all)


def _adamw(w, g, m, v):
    shape = w.shape
    cols = shape[-1]
    rows = w.size // cols
    w2, g2, m2, v2 = (t.reshape(rows, cols) for t in (w, g, m, v))
    tr = _row_tile(rows, cap=max(8, (1 << 18) // cols), mult=8)

    def body(w_ref, g_ref, m_ref, v_ref, d_ref, mo_ref, vo_ref):
        gv = g_ref[...]
        mn = ADAM_B1 * m_ref[...] + (1.0 - ADAM_B1) * gv
        vn = ADAM_B2 * v_ref[...] + (1.0 - ADAM_B2) * (gv * gv)
        m_hat = mn / (1.0 - ADAM_B1 ** ADAM_STEP)
        v_hat = vn / (1.0 - ADAM_B2 ** ADAM_STEP)
        d_ref[...] = -ADAM_LR * (m_hat / (jnp.sqrt(v_hat) + ADAM_EPS) + ADAM_WD * w_ref[...])
        mo_ref[...] = mn
        vo_ref[...] = vn

    blk = pl.BlockSpec((tr, cols), lambda i: (i, 0))
    outs = _pcall(body, name="adamw", grid=(rows // tr,), in_specs=[blk] * 4, out_specs=[blk] * 3,
                  out_shape=[jax.ShapeDtypeStruct((rows, cols), F32)] * 3, compiler_params=_seq())(w2, g2, m2, v2)
    return tuple(o.reshape(shape) for o in outs)


WEIGHTS = ["norm_g", "w_ada", "b_ada", "w_in", "rg_conv_w", "rg_conv_b", "rg_w_a", "rg_b_a", "rg_w_x", "rg_b_x",
           "rg_lambda", "ml_conv_w", "ml_conv_b", "ml_w_q", "ml_w_k", "ml_w_v", "ml_w_if", "ml_b_if", "ml_norm_g",
           "w_out", "final_g"]
SMALL_SHARDED = {"ml_w_qkv": 2, "rg_conv_w": 1, "ml_conv_w": 1, "ml_w_if": 0}
REPLICATED = ["rg_w_a", "rg_w_x", "norm_g", "rg_conv_b", "rg_b_a", "rg_b_x", "rg_lambda", "ml_conv_b", "ml_norm_g",
              "ml_b_if"]
LANES = 128


def _to_pieces(g, axis):
    shp = g.shape
    g = g.reshape(shp[:axis] + (4, 2, shp[axis] // 8) + shp[axis + 1:])
    g = jnp.moveaxis(g, (axis, axis + 1), (0, 1))
    return g.reshape(4, 2, -1)


def _from_pieces(p, shard_shape, axis):
    k = p.shape[0]
    rest = shard_shape[:axis] + (shard_shape[axis] // k,) + shard_shape[axis + 1:]
    t = jnp.moveaxis(p.reshape((k,) + rest), 0, axis)
    return t.reshape(shard_shape)


def _pad_rows(flat, mult):
    n = flat.shape[-1]
    pad = (-n) % mult
    if pad:
        flat = jnp.concatenate([flat, jnp.zeros(flat.shape[:-1] + (pad,), flat.dtype)], axis=-1)
    return flat


def kernel(x, c, norm_g, w_ada, b_ada, w_in, rg_conv_w, rg_conv_b, rg_w_a, rg_b_a, rg_w_x, rg_b_x, rg_lambda, ml_conv_w, ml_conv_b, ml_w_q, ml_w_k, ml_w_v, ml_w_if, ml_b_if, ml_norm_g, w_out, final_g, loss_target, m_norm_g, m_w_ada, m_b_ada, m_w_in, m_rg_conv_w, m_rg_conv_b, m_rg_w_a, m_rg_b_a, m_rg_w_x, m_rg_b_x, m_rg_lambda, m_ml_conv_w, m_ml_conv_b, m_ml_w_q, m_ml_w_k, m_ml_w_v, m_ml_w_if, m_ml_b_if, m_ml_norm_g, m_w_out, m_final_g, v_norm_g, v_w_ada, v_b_ada, v_w_in, v_rg_conv_w, v_rg_conv_b, v_rg_w_a, v_rg_b_a, v_rg_w_x, v_rg_b_x, v_rg_lambda, v_ml_conv_w, v_ml_conv_b, v_ml_w_q, v_ml_w_k, v_ml_w_v, v_ml_w_if, v_ml_b_if, v_ml_norm_g, v_w_out, v_final_g):
    given = dict(locals())
    ax, ay, ac = lax.axis_index("x"), lax.axis_index("y"), lax.axis_index("c")
    chip = 2 * ax + ay
    me = 2 * chip + ac
    depth, d = norm_g.shape
    n_ada = w_ada.shape[2]
    pick = lambda a, i, axis=0: lax.dynamic_index_in_dim(a, i, axis, keepdims=False)

    convs = jnp.stack([rg_conv_w, ml_conv_w])
    n_conv = 2 * depth * CONV_WIDTH // 4
    blk = jnp.concatenate([c, convs.reshape(n_conv, d), jnp.zeros((8 - 1 - n_conv, d), F32)], axis=0)
    g0 = _all_gather8([blk], pltpu.VMEM)[0].reshape(8, 8, d)
    c_all = g0[:, 0, :]
    conv_full = g0[0::2, 1:1 + n_conv].reshape(4, 2, depth, CONV_WIDTH, d // 4)
    conv_full = conv_full.transpose(1, 2, 3, 0, 4).reshape(2, depth, CONV_WIDTH, d)

    b_cols = lax.dynamic_slice_in_dim(b_ada, chip * n_ada, n_ada, axis=1)[:, None, :]
    mod_part = _ada_mod(c_all, w_ada, b_cols)
    g1 = _all_gather8([mod_part.transpose(1, 0, 2).reshape(8, depth * n_ada)], pltpu.VMEM)[0]
    g1 = g1.reshape(8, 8, depth, n_ada)[0::2]
    mod_me = pick(g1.transpose(1, 2, 0, 3).reshape(8, depth, 4 * n_ada), me)

    def half_of(w, axis):
        n = w.shape[axis] // 2
        return lax.dynamic_slice_in_dim(w, ac * n, n, axis).astype(BF16)

    n_sh = w_in.shape[2]
    heads, hd_cut, hd = ml_w_q.shape[1:]

    def blocks_of(l):
        wqkv = jnp.stack([ml_w_q[l], ml_w_k[l], ml_w_v[l]])
        return [half_of(w_in[l], 0), half_of(w_out[l], 0), half_of(wqkv, 2).reshape(-1, hd), half_of(ml_w_if[l], 0)]

    def layer_of(l, gathered):
        w4, w_out_b, wqkv_g, wif = gathered
        wqkv_b = _from_pieces(wqkv_g.reshape(8, -1), (3, heads, hd, hd), 2)
        return dict(
            norm_g=norm_g[l][None], shift=mod_me[l, 0:d][None], scale=mod_me[l, d:2 * d][None],
            gate=mod_me[l, 2 * d:3 * d][None], w4=w4.reshape(4, d, n_sh),
            rg_conv_w=conv_full[0, l], rg_conv_b=rg_conv_b[l][None], rg_wa_b=_bf(rg_w_a[l]), rg_ba=rg_b_a[l][None],
            rg_wx_b=_bf(rg_w_x[l]), rg_bx=rg_b_x[l][None], rg_lam=rg_lambda[l][None],
            ml_conv_w=conv_full[1, l], ml_conv_b=ml_conv_b[l][None], wqkv_b=wqkv_b, wif_b=wif, wift_b=wif.T,
            b_if=ml_b_if[l][None], b_ift=ml_b_if[l][:, None], ml_g=ml_norm_g[l][None], w_out_b=w_out_b)

    landing = lambda b: jax.ShapeDtypeStruct((4, 2) + b.shape, b.dtype)
    layers = [layer_of(0, _all_gather8(blocks_of(0), pltpu.HBM))]
    saved = []
    xl = x[0]
    for l in range(depth):
        rides = (None, None)
        if l + 1 < depth:
            nxt = blocks_of(l + 1)
            rides = (Ride(nxt[:1], [landing(nxt[0])], False), Ride(nxt[1:], [landing(b) for b in nxt[1:]], False))
        xl, s, got = _layer_fwd(xl, layers[l], rides)
        saved.append(s)
        if l + 1 < depth:
            layers.append(layer_of(l + 1, [t.reshape(-1, t.shape[-1]) for t in _sib_fill(got)]))
    dx, g_final, loss = _final_loss(xl, final_g[None], loss_target[0])

    half = ac.reshape(1)
    ids = jnp.stack([chip, ac])
    r_out = w_out.shape[1] // 2

    def halves_summed(g, slabs):
        g_out5 = g["w_out"].reshape(1, 4, 2, r_out, d)
        got_in, got_out, *got_slabs = _sib_halves(g["w_in"], g_out5, slabs)
        part_in = _pair_sum(
            half, g["w_in"], pl.BlockSpec((None, d // 2, n_sh), lambda s, h: (0, h[0], s)),
            got_in, pl.BlockSpec((None, None, d // 2, n_sh), lambda s, h: (0, s, 0, 0)),
            jax.ShapeDtypeStruct((4, 1, d // 2, n_sh), BF16),
            pl.BlockSpec((None, None, d // 2, n_sh), lambda s, h: (s, 0, 0, 0)), (4,))
        part_out = _pair_sum(
            half, g_out5, pl.BlockSpec((None, None, None, r_out, d), lambda s, h: (0, s, h[0], 0, 0)),
            got_out, pl.BlockSpec((None, None, r_out, d), lambda s, h: (0, s, 0, 0)),
            jax.ShapeDtypeStruct((4, 1, r_out, d), BF16),
            pl.BlockSpec((None, None, r_out, d), lambda s, h: (s, 0, 0, 0)), (4,))
        return [part_in, part_out], got_slabs

    grads, dmods, parts, mets = [None] * depth, [None] * depth, [None] * depth, [None] * depth
    ride = None
    for l in reversed(range(depth)):
        dx, grads[l], dmods[l], got = _layer_bwd(dx, layers[l], saved[l], ride)
        if ride is not None:
            mets[l + 1] = got
        if l > 0:
            parts[l], _ = halves_summed(grads[l], [])
            ride = Ride(parts[l], [jax.ShapeDtypeStruct(t.shape, t.dtype) for t in parts[l]], True)

    dm_blk = jnp.concatenate(dmods + [jnp.zeros((8 - depth, 3 * d), F32)], axis=0)
    dm_all = _all_gather8([dm_blk], pltpu.VMEM)[0].reshape(8, 8, 3 * d)[:, :depth].transpose(1, 0, 2)
    dm_cols = lax.dynamic_slice_in_dim(dm_all, chip * n_ada, n_ada, axis=2)
    g_w_ada, g_b_ada = _ada_grad(c_all, dm_cols, dm_all)

    sm = jnp.concatenate([_to_pieces(grads[l][name], axis) for l in range(depth) for name, axis in SMALL_SHARDED.items()],
                         axis=-1)
    sm = _pad_rows(sm, 16 * LANES)
    n_sm = sm.shape[-1] // LANES
    sm = sm.transpose(1, 0, 2).reshape(2, 4 * n_sm, LANES)
    rep = [grads[l][name].reshape(-1) for l in range(depth) for name in REPLICATED[:-1]]
    rep += [_pad_rows(grads[l]["ml_b_if"].reshape(-1), LANES) for l in range(depth)]
    rep += [g_final.reshape(-1), loss.reshape(-1)]
    rep = _pad_rows(jnp.concatenate(rep), 8 * 8 * LANES)
    n_rep = rep.shape[0] // (8 * LANES)
    rep = rep.reshape(4, 2, n_rep, LANES).transpose(1, 0, 2, 3).reshape(2, 4 * n_rep, LANES)
    parts[0], (got_sm, got_rep) = halves_summed(grads[0], [sm, rep])

    def slab_sum(slab, got, rows, dtype):
        blk = pl.BlockSpec((rows, LANES), lambda s, h: (s, 0))
        return _pair_sum(half, slab, pl.BlockSpec((None, rows, LANES), lambda s, h: (h[0], s, 0)), got, blk,
                         jax.ShapeDtypeStruct((4 * rows, LANES), dtype), blk, (4,)).reshape(4, 1, rows, LANES)

    part_sm = slab_sum(sm, got_sm, n_sm, BF16)
    part_rep = slab_sum(rep, got_rep, n_rep, F32)
    *mets[0], met_sm, met_rep = _chip_exchange(parts[0] + [part_sm, part_rep])
    both_in, both_out = depth, depth
    for l in range(depth):
        both_in = _chip_sum(ids, parts[l][0], mets[l][0], True, l, both_in)
        both_out = _chip_sum(ids, parts[l][1], mets[l][1], True, l, both_out)
    both_in, both_out, both_sm = _sib_fill([both_in, both_out, _chip_sum(ids, part_sm, met_sm, True)])
    red_rep = _chip_sum(ids, part_rep, met_rep, False).reshape(n_rep, LANES)
    rep_all = _all_gather8([red_rep], pltpu.VMEM)[0].reshape(-1)

    g = dict(w_ada=g_w_ada, b_ada=g_b_ada.reshape(b_ada.shape), w_in=both_in.reshape(w_in.shape),
             w_out=both_out.reshape(w_out.shape))
    shard = both_sm.reshape(2, -1)
    off = 0
    per_layer = {name: [] for name in SMALL_SHARDED}
    for l in range(depth):
        for name, axis in SMALL_SHARDED.items():
            shp = (3,) + ml_w_q.shape[1:] if name == "ml_w_qkv" else given[name].shape[1:]
            n = grads[l][name].size // 8
            per_layer[name].append(_from_pieces(shard[:, off:off + n], shp, axis))
            off += n
    for name in SMALL_SHARDED:
        g[name] = jnp.stack(per_layer[name])
    for i, name in enumerate(["ml_w_q", "ml_w_k", "ml_w_v"]):
        g[name] = g["ml_w_qkv"][:, i]
    off = 0
    per_layer = {name: [] for name in REPLICATED}
    for l in range(depth):
        for name in REPLICATED[:-1]:
            n = given[name][l].size
            per_layer[name].append(rep_all[off:off + n].reshape(given[name].shape[1:]))
            off += n
    for l in range(depth):
        n = given["ml_b_if"][l].size
        per_layer["ml_b_if"].append(rep_all[off:off + n])
        off += LANES
    for name in REPLICATED:
        g[name] = jnp.stack(per_layer[name])
    g["final_g"] = rep_all[off:off + d]
    loss_all = rep_all[off + d]

    deltas, new_m, new_v = [], [], []
    for name in WEIGHTS:
        dl, mn, vn = _adamw(given[name], g[name], given["m_" + name], given["v_" + name])
        deltas.append(dl)
        new_m.append(mn)
        new_v.append(vn)
    return (loss_all, dx[None], *[g[name] for name in WEIGHTS], *deltas, *new_m, *new_v)
```

```python
import functools
from typing import NamedTuple

import jax
import jax.numpy as jnp
from jax import lax
from jax.experimental import pallas as pl
from jax.experimental.pallas import tpu as pltpu

F32 = jnp.float32
BF16 = jnp.bfloat16

EPS = 1e-6
RG_C = 8.0
CONV_WIDTH = 4
ML_CHUNK = 128
HALO = 8
ADAM_LR = 0.001
ADAM_B1 = 0.9
ADAM_B2 = 0.999
ADAM_EPS = 1e-08
ADAM_WD = 0.01
ADAM_STEP = 10
MESH = pl.DeviceIdType.MESH


def _pcall(body, **kw):
    return pl.pallas_call(body, **kw)


class Ride(NamedTuple):
    srcs: list
    dst_shapes: list
    sliced: bool


def _pcall_ride(body, ride, *, grid, in_specs, out_specs, out_shape, args, scratch_shapes=(), **kw):
    n_in, n_out, n_scr = len(in_specs), len(out_specs), len(scratch_shapes)
    if ride is None:
        res = _pcall(body, grid=grid, in_specs=in_specs, out_specs=out_specs, out_shape=out_shape,
                     scratch_shapes=list(scratch_shapes), **kw)(*args)
        return res, []
    nr = len(ride.srcs)

    def riding(*refs):
        ins, rsrc = refs[:n_in], refs[n_in:n_in + nr]
        outs, rdst = refs[n_in + nr:n_in + nr + n_out], refs[n_in + nr + n_out:n_in + 2 * nr + n_out]
        scr = refs[n_in + 2 * nr + n_out:n_in + 2 * nr + n_out + n_scr]
        send_sems, recv_sems, local_sems = refs[n_in + 2 * nr + n_out + n_scr:]
        x, y, c = _me()
        me_s = 2 * x + y
        chips = [(1 - x, y), (x, 1 - y), (1 - x, 1 - y)]
        copies, local = [], []
        for i in range(nr):
            for k, (px, py) in enumerate(chips):
                src = rsrc[i].at[2 * px + py] if ride.sliced else rsrc[i]
                dst = rdst[i].at[me_s] if ride.sliced else rdst[i].at[me_s, c]
                copies.append(_remote(src, dst, send_sems.at[3 * i + k], recv_sems.at[3 * i + k], (px, py, c)))
            if not ride.sliced:
                local.append(pltpu.make_async_copy(rsrc[i], rdst[i].at[me_s, c], local_sems.at[i]))
        first = functools.reduce(jnp.logical_and, [pl.program_id(a) == 0 for a in range(len(grid))])
        last = functools.reduce(jnp.logical_and, [pl.program_id(a) == grid[a] - 1 for a in range(len(grid))])

        @pl.when(first)
        def _():
            for cp in copies + local:
                cp.start()

        body(*ins, *outs, *scr)

        @pl.when(last)
        def _():
            for cp in copies:
                cp.wait_recv()
            for cp in copies:
                cp.wait_send()
            for cp in local:
                cp.wait()

    hbm = pl.BlockSpec(memory_space=pltpu.HBM)
    res = _pcall(
        riding, grid=grid, in_specs=list(in_specs) + [hbm] * nr, out_specs=list(out_specs) + [hbm] * nr,
        out_shape=list(out_shape) + list(ride.dst_shapes),
        scratch_shapes=list(scratch_shapes) + [pltpu.SemaphoreType.DMA((3 * nr,)), pltpu.SemaphoreType.DMA((3 * nr,)),
                                               pltpu.SemaphoreType.DMA((nr,))], **kw)(*args, *ride.srcs)
    return res[:n_out], res[n_out:]


def _seq(n=1):
    return pltpu.CompilerParams(dimension_semantics=("arbitrary",) * n)


def _dot(a, b):
    return jnp.dot(a, b, preferred_element_type=F32)


def _dot_nt(a, b):
    return lax.dot_general(a, b, (((1,), (1,)), ((), ())), preferred_element_type=F32)


def _dot_tn(a, b):
    return lax.dot_general(a, b, (((0,), (0,)), ((), ())), preferred_element_type=F32)


def _bf(x):
    return x.astype(BF16)


def _sigmoid(x):
    return 0.5 * jnp.tanh(0.5 * x) + 0.5


def _log1p(z):
    u = 1.0 + z
    return jnp.where(u == 1.0, z, jnp.log(u) * (z / jnp.where(u == 1.0, 1.0, u - 1.0)))


def _softplus(x):
    return jnp.maximum(x, 0.0) + _log1p(jnp.exp(-jnp.abs(x)))


def _log_sigmoid(x):
    return -_softplus(-x)


def _one_minus_sq(a, log_a):
    x = 2.0 * log_a
    small = -x * (1.0 + x * (0.5 + x * (1.0 / 6.0)))
    return jnp.where(x > -0.004, small, 1.0 - a * a)


def _dsilu(x, s):
    return s * (1.0 + x * (1.0 - s))


def _rowsum(x):
    return jnp.sum(x, axis=1, keepdims=True)


def _colsum(x):
    return jnp.sum(x, axis=0, keepdims=True)


def _shift_down(win, s):
    return win if s == 0 else pltpu.roll(win, s, 0)


def _shift_up(win, s):
    return win if s == 0 else pltpu.roll(win, win.shape[0] - s, 0)


def _conv_taps(win):
    return [_shift_down(win, CONV_WIDTH - 1 - k)[HALO:] for k in range(CONV_WIDTH)]


def _conv_fwd(taps, w_ref, b_ref):
    acc = b_ref[...] + w_ref[CONV_WIDTH - 1:CONV_WIDTH, :] * taps[CONV_WIDTH - 1]
    for k in range(CONV_WIDTH - 1):
        acc = acc + w_ref[k:k + 1, :] * taps[k]
    return acc


def _split3(x):
    hi = _bf(x)
    r1 = x - hi.astype(F32)
    mid = _bf(r1)
    lo = _bf(r1 - mid.astype(F32))
    return hi, mid, lo


def _tri_dot_left(tri, x):
    hi, mid, lo = _split3(x)
    return _dot(tri, hi) + _dot(tri, mid) + _dot(tri, lo)


def _tri_dot_right(x, tri):
    hi, mid, lo = _split3(x)
    return _dot(hi, tri) + _dot(mid, tri) + _dot(lo, tri)


def _tile(n, want):
    t = min(n, want)
    assert n % t == 0
    return t


def _ln_inproj(x, g, scale, shift, w4):
    s_len, d = x.shape
    nj, _, nsh = w4.shape
    tm = _tile(s_len, 1024)

    def body(x_ref, g_ref, sc_ref, sh_ref, w_ref, h_ref, u_ref, hs):
        @pl.when(pl.program_id(1) == 0)
        def _():
            xv = x_ref[...]
            r = lax.rsqrt(jnp.mean(xv * xv, axis=-1, keepdims=True) + EPS)
            hv = (xv * r * g_ref[...]) * (1.0 + sc_ref[...]) + sh_ref[...]
            hs[...] = _bf(hv)
            h_ref[...] = hs[...]

        u_ref[...] = _dot(hs[...], w_ref[0])

    vec = pl.BlockSpec((1, d), lambda i, j: (0, 0))
    return _pcall(
        body, name="ln_inproj", grid=(s_len // tm, nj),
        in_specs=[pl.BlockSpec((tm, d), lambda i, j: (i, 0)), vec, vec, vec,
                  pl.BlockSpec((1, d, nsh), lambda i, j: (j, 0, 0))],
        out_specs=[pl.BlockSpec((tm, d), lambda i, j: (i, 0)), pl.BlockSpec((tm, nsh), lambda i, j: (i, j))],
        out_shape=[jax.ShapeDtypeStruct((s_len, d), BF16), jax.ShapeDtypeStruct((s_len, nj * nsh), F32)],
        scratch_shapes=[pltpu.VMEM((tm, d), BF16)],
        compiler_params=_seq(2),
    )(x, g, scale, shift, w4)


def _rg_gates(xc, wa_ref, ba_ref, wx_ref, bx_ref, lam_ref):
    heads, hd, _ = wa_ref.shape
    xb = _bf(xc)
    ga = jnp.concatenate([_dot(xb[:, h * hd:(h + 1) * hd], wa_ref[h]) for h in range(heads)], axis=1) + ba_ref[...]
    gx = jnp.concatenate([_dot(xb[:, h * hd:(h + 1) * hd], wx_ref[h]) for h in range(heads)], axis=1) + bx_ref[...]
    r = _sigmoid(ga)
    ig = _sigmoid(gx)
    sp = _softplus(-lam_ref[...])
    log_a = (-RG_C) * r * sp
    a = jnp.exp(log_a)
    mult = jnp.sqrt(_one_minus_sq(a, log_a))
    return r, ig, sp, log_a, a, mult


def _scan_groups(a, u, reverse):
    n, c = a.shape
    a = a.reshape(n // 8, 8, c)
    u = u.reshape(n // 8, 8, c)
    row = lax.broadcasted_iota(jnp.int32, a.shape, 1)
    for k in (1, 2, 4):
        sft = 8 - k if reverse else k
        a_sh, u_sh = pltpu.roll(a, sft, 1), pltpu.roll(u, sft, 1)
        ok = row < 8 - k if reverse else row >= k
        u = jnp.where(ok, a * u_sh + u, u)
        a = jnp.where(ok, a * a_sh, a)
    return a.reshape(n, c), u.reshape(n, c)


def _rg_fwd(u, conv_w, conv_b, wa_b, ba, wx_b, bx, lam, ride=None):
    s_len = u.shape[0]
    d = conv_w.shape[1]
    tm = _tile(s_len, 256)
    per = tm // HALO

    def body(x_ref, xp_ref, z_ref, cw_ref, cb_ref, wa_ref, ba_ref, wx_ref, bx_ref, lam_ref,
             hh_ref, y_ref, carry):
        i = pl.program_id(0)

        @pl.when(i == 0)
        def _():
            carry[...] = jnp.zeros_like(carry)

        prev = jnp.where(i == 0, 0.0, xp_ref[...])
        xc = _conv_fwd(_conv_taps(jnp.concatenate([prev, x_ref[...]], axis=0)), cw_ref, cb_ref)
        _, ig, _, _, a, mult = _rg_gates(xc, wa_ref, ba_ref, wx_ref, bx_ref, lam_ref)
        ca, cu = _scan_groups(a, mult * (ig * xc), reverse=False)
        c = carry[0:1, :]
        for j in range(per):
            blk = ca[j * 8:(j + 1) * 8] * c + cu[j * 8:(j + 1) * 8]
            hh_ref[j * 8:(j + 1) * 8, :] = blk
            c = blk[7:8]
        carry[0:1, :] = c
        z = z_ref[...]
        y_ref[0] = _bf(hh_ref[...] * (z * _sigmoid(z)))

    vec = pl.BlockSpec((1, d), lambda i: (0, 0))
    whole3 = lambda a: pl.BlockSpec(a.shape, lambda i: (0, 0, 0))
    return _pcall_ride(
        body, ride, name="rg_fwd", grid=(s_len // tm,),
        in_specs=[pl.BlockSpec((tm, d), lambda i: (i, 0)),
                  pl.BlockSpec((HALO, d), lambda i: (jnp.maximum(i * per - 1, 0), 0)),
                  pl.BlockSpec((tm, d), lambda i: (i, 1)),
                  pl.BlockSpec((CONV_WIDTH, d), lambda i: (0, 0)), vec,
                  whole3(wa_b), vec, whole3(wx_b), vec, vec],
        out_specs=[pl.BlockSpec((tm, d), lambda i: (i, 0)), pl.BlockSpec((1, tm, d), lambda i: (0, i, 0))],
        out_shape=[jax.ShapeDtypeStruct((s_len, d), F32), jax.ShapeDtypeStruct((2, s_len, d), BF16)],
        scratch_shapes=[pltpu.VMEM((8, d), F32)],
        compiler_params=_seq(),
        args=(u, u, u, conv_w, conv_b, wa_b, ba, wx_b, bx, lam))


def _ml_pre(u, conv_w, conv_b, wqkv_b, wif_b, wift_b, b_if, b_ift):
    s_len = u.shape[0]
    d = conv_w.shape[1]
    _, heads, hd, _ = wqkv_b.shape
    ng = 2 * heads
    tm = _tile(s_len, 256)
    per = tm // HALO

    def body(x_ref, xp_ref, cw_ref, cb_ref, w_ref, wif_ref, wift_ref, bif_ref, bift_ref,
             qkv_ref, gt_ref, gtt_ref, bc_ref, bct_ref):
        i = pl.program_id(0)
        prev = jnp.where(i == 0, 0.0, xp_ref[...])
        xm = x_ref[...]
        pre = _conv_fwd(_conv_taps(jnp.concatenate([prev, xm], axis=0)), cw_ref, cb_ref)
        xcb = _bf(pre * _sigmoid(pre))
        xmb = _bf(xm)
        for h in range(heads):
            hs = slice(h * hd, (h + 1) * hd)
            qkv_ref[0, :, hs] = _bf(_dot(xcb[:, hs], w_ref[0, h]))
            qkv_ref[1, :, hs] = _bf(_dot(xcb[:, hs], w_ref[1, h]))
            qkv_ref[2, :, hs] = _bf(_dot(xmb[:, hs], w_ref[2, h]))
        qb, kb, vb = qkv_ref[0], qkv_ref[1], qkv_ref[2]
        gt = (_dot(qb, wif_ref[0:d, :]) + _dot(kb, wif_ref[d:2 * d, :]) + _dot(vb, wif_ref[2 * d:3 * d, :])
              + bif_ref[...])
        gtt = (_dot_nt(wift_ref[:, 0:d], qb) + _dot_nt(wift_ref[:, d:2 * d], kb)
               + _dot_nt(wift_ref[:, 2 * d:3 * d], vb) + bift_ref[...])
        gt_ref[...] = gt
        gtt_ref[...] = gtt
        r = lax.broadcasted_iota(jnp.int32, (tm, tm), 0)
        c = lax.broadcasted_iota(jnp.int32, (tm, tm), 1)
        same = (r // ML_CHUNK) == (c // ML_CHUNK)
        bc_ref[...] = _tri_dot_left(((r >= c) & same).astype(BF16), _log_sigmoid(gt))
        bct_ref[...] = _tri_dot_right(_log_sigmoid(gtt), ((r <= c) & same).astype(BF16))

    vec = pl.BlockSpec((1, d), lambda i: (0, 0))
    whole2 = lambda a: pl.BlockSpec(a.shape, lambda i: (0, 0))
    col = pl.BlockSpec((tm, ng), lambda i: (i, 0))
    row = pl.BlockSpec((ng, tm), lambda i: (0, i))
    return _pcall(
        body, name="ml_pre", grid=(s_len // tm,),
        in_specs=[pl.BlockSpec((tm, d), lambda i: (i, 2)),
                  pl.BlockSpec((HALO, d), lambda i: (jnp.maximum(i * per - 1, 0), 2)),
                  pl.BlockSpec((CONV_WIDTH, d), lambda i: (0, 0)), vec,
                  pl.BlockSpec(wqkv_b.shape, lambda i: (0, 0, 0, 0)), whole2(wif_b), whole2(wift_b), whole2(b_if),
                  whole2(b_ift)],
        out_specs=[pl.BlockSpec((3, tm, d), lambda i: (0, i, 0)), col, row, col, row],
        out_shape=[jax.ShapeDtypeStruct((3, s_len, d), BF16), jax.ShapeDtypeStruct((s_len, ng), F32),
                   jax.ShapeDtypeStruct((ng, s_len), F32), jax.ShapeDtypeStruct((s_len, ng), F32),
                   jax.ShapeDtypeStruct((ng, s_len), F32)],
        compiler_params=_seq(),
    )(u, u, conv_w, conv_b, wqkv_b, wif_b, wift_b, b_if, b_ift)


def _chunk_gates(gt, gtt, bc, bct, h, heads):
    li_c = gt[:, h:h + 1]
    li_r = gtt[h:h + 1, :]
    gf_c = gt[:, heads + h:heads + h + 1]
    b_c = bc[:, heads + h:heads + h + 1]
    b_r = bct[heads + h:heads + h + 1, :]
    return li_c, li_r, gf_c, b_c, b_r


def _chunk_weights(li_c, li_r, b_c, b_r, m_prev, causal):
    lc = b_c.shape[0]
    b_last = b_c[lc - 1:lc, :]
    dmat = jnp.where(causal, b_c - b_r + li_r, -jnp.inf)
    m_inter = b_c + m_prev
    m_t = jnp.maximum(m_inter, jnp.max(dmat, axis=1, keepdims=True))
    w_intra = jnp.exp(dmat - m_t)
    w_inter = jnp.exp(m_inter - m_t)
    g_c = b_last - b_c + li_c
    m_new = jnp.maximum(b_last + m_prev, jnp.max(g_c, axis=0, keepdims=True))
    w_state = jnp.exp(g_c - m_new)
    decay = jnp.exp(b_last + m_prev - m_new)
    return m_t, w_intra, w_inter, m_new, w_state, decay


def _tri_masks(lc):
    r = lax.broadcasted_iota(jnp.int32, (lc, lc), 0)
    c = lax.broadcasted_iota(jnp.int32, (lc, lc), 1)
    causal = r >= c
    return causal, causal.astype(BF16), (r <= c).astype(BF16)


def _mlstm_fwd(qkv, gates, u, ml_g, ycat, ride=None):
    _, s_len, d = qkv.shape
    ng = gates[0].shape[1]
    heads = ng // 2
    hd = d // heads
    lc = ML_CHUNK
    nc = s_len // lc
    kscale = hd ** -0.5

    def body(qkv_ref, gt_ref, gtt_ref, bc_ref, bct_ref, o_ref, z_ref, g_ref, _, cell_ref, y_ref, cst_ref, nst_ref,
             mst_ref, cs, ns, ms):
        @pl.when(pl.program_id(0) == 0)
        def _():
            cs[...] = jnp.zeros_like(cs)
            ns[...] = jnp.zeros_like(ns)
            ms[...] = jnp.zeros_like(ms)

        causal = _tri_masks(lc)[0]
        gtv, gttv, bcv, bctv = gt_ref[...], gtt_ref[...], bc_ref[...], bct_ref[...]
        for h in range(heads):
            hs = slice(h * hd, (h + 1) * hd)
            li_c, li_r, _, b_c, b_r = _chunk_gates(gtv, gttv, bcv, bctv, h, heads)
            m_prev = ms[h][:, 0:1]
            m_t, w_intra, w_inter, m_new, w_state, decay = _chunk_weights(li_c, li_r, b_c, b_r, m_prev, causal)
            qb = qkv_ref[0, :, hs]
            ks = qkv_ref[1, :, hs].astype(F32) * kscale
            kb = _bf(ks)
            vb = qkv_ref[2, :, hs]
            c_old = cs[h]
            n_old = ns[h]
            cst_ref[0, h] = _bf(c_old)
            nst_ref[0, h] = n_old
            mst_ref[0, h] = ms[h]
            s = _dot_nt(qb, kb) * w_intra
            num = _dot(_bf(s), vb) + w_inter * _dot(qb, _bf(c_old))
            den = _rowsum(s) + w_inter * _rowsum(qb.astype(F32) * n_old)
            cell = num / jnp.maximum(jnp.abs(den), jnp.exp(-m_t))
            kw = ks * w_state
            cs[h] = decay * c_old + _dot_tn(_bf(kw), vb)
            ns[h] = decay * n_old + _colsum(kw)
            ms[h] = jnp.broadcast_to(m_new, ms[h].shape)
            cell_ref[:, hs] = cell
            hm = _sigmoid(o_ref[:, hs]) * cell
            hn = hm * lax.rsqrt(jnp.mean(hm * hm, axis=-1, keepdims=True) + EPS)
            z = z_ref[:, hs]
            y_ref[0, :, hs] = _bf((hn * g_ref[:, hs]) * (z * _sigmoid(z)))

    row = pl.BlockSpec((lc, d), lambda c: (c, 0))
    gcol = pl.BlockSpec((lc, ng), lambda c: (c, 0))
    grow = pl.BlockSpec((ng, lc), lambda c: (0, c))
    return _pcall_ride(
        body, ride, name="mlstm_fwd", grid=(nc,),
        in_specs=[pl.BlockSpec((3, lc, d), lambda c: (0, c, 0)), gcol, grow, gcol, grow,
                  pl.BlockSpec((lc, d), lambda c: (c, 3)), pl.BlockSpec((lc, d), lambda c: (c, 4)),
                  pl.BlockSpec((1, d), lambda c: (0, 0)), pl.BlockSpec(memory_space=pl.ANY)],
        out_specs=[row, pl.BlockSpec((1, lc, d), lambda c: (1, c, 0)),
                   pl.BlockSpec((1, heads, hd, hd), lambda c: (c, 0, 0, 0)),
                   pl.BlockSpec((1, heads, 1, hd), lambda c: (c, 0, 0, 0)),
                   pl.BlockSpec((1, heads, 1, 128), lambda c: (c, 0, 0, 0))],
        out_shape=[jax.ShapeDtypeStruct((s_len, d), F32), jax.ShapeDtypeStruct(ycat.shape, BF16),
                   jax.ShapeDtypeStruct((nc, heads, hd, hd), BF16),
                   jax.ShapeDtypeStruct((nc, heads, 1, hd), F32),
                   jax.ShapeDtypeStruct((nc, heads, 1, 128), F32)],
        scratch_shapes=[pltpu.VMEM((heads, hd, hd), F32), pltpu.VMEM((heads, 1, hd), F32),
                        pltpu.VMEM((heads, 1, 128), F32)],
        input_output_aliases={8: 1},
        compiler_params=_seq(),
        args=(qkv, *gates, u, u, ml_g, ycat))


def _out_proj(ycat, w_out_b, x, gate):
    s_len, d = x.shape
    tm = _tile(s_len, 1024)

    def body(a_ref, w_ref, x_ref, g_ref, y_ref, xn_ref):
        y = _dot(a_ref[0], w_ref[0:d, :]) + _dot(a_ref[1], w_ref[d:2 * d, :])
        y_ref[...] = y
        xn_ref[...] = x_ref[...] + g_ref[...] * y

    row = pl.BlockSpec((tm, d), lambda i: (i, 0))
    return _pcall(
        body, name="out_proj", grid=(s_len // tm,),
        in_specs=[pl.BlockSpec((2, tm, d), lambda i: (0, i, 0)), pl.BlockSpec((2 * d, d), lambda i: (0, 0)), row,
                  pl.BlockSpec((1, d), lambda i: (0, 0))],
        out_specs=[row, row],
        out_shape=[jax.ShapeDtypeStruct((s_len, d), F32)] * 2,
        compiler_params=_seq(),
    )(ycat, w_out_b, x, gate)


def _final_loss(x, g, target):
    s_len, d = x.shape
    tm = _tile(s_len, 256)

    def body(x_ref, g_ref, t_ref, dx_ref, dg_ref, loss_ref):
        @pl.when(pl.program_id(0) == 0)
        def _():
            dg_ref[...] = jnp.zeros_like(dg_ref)
            loss_ref[...] = jnp.zeros_like(loss_ref)

        xv = x_ref[...]
        r = lax.rsqrt(jnp.mean(xv * xv, axis=-1, keepdims=True) + EPS)
        xn = xv * r
        err = xn * g_ref[...] - t_ref[...]
        loss_ref[...] += 0.5 * jnp.sum(jnp.mean(err * err, axis=-1, keepdims=True))
        dout = err * (1.0 / d)
        dg_ref[...] += _colsum(dout * xn)
        dxn = dout * g_ref[...]
        dx_ref[...] = r * (dxn - xn * jnp.mean(dxn * xn, axis=-1, keepdims=True))

    row = pl.BlockSpec((tm, d), lambda i: (i, 0))
    vec = pl.BlockSpec((1, d), lambda i: (0, 0))
    return _pcall(
        body, name="final_loss", grid=(s_len // tm,),
        in_specs=[row, vec, row],
        out_specs=[row, vec, pl.BlockSpec((1, 128), lambda i: (0, 0))],
        out_shape=[jax.ShapeDtypeStruct((s_len, d), F32), jax.ShapeDtypeStruct((1, d), F32),
                   jax.ShapeDtypeStruct((1, 128), F32)],
        compiler_params=_seq(),
    )(x, g, target)


def _out_bwd(dxn, y, gate, w_out_b):
    s_len, d = dxn.shape
    tm = _tile(s_len, 1024)

    def body(dx_ref, y_ref, g_ref, w_ref, dg_ref, dy_ref, dc_ref):
        @pl.when(pl.program_id(0) == 0)
        def _():
            dg_ref[...] = jnp.zeros_like(dg_ref)

        dx = dx_ref[...]
        dg_ref[...] += _colsum(dx * y_ref[...])
        dy = _bf(g_ref[...] * dx)
        dy_ref[...] = dy
        dc_ref[0] = _dot_nt(dy, w_ref[0:d, :])
        dc_ref[1] = _dot_nt(dy, w_ref[d:2 * d, :])

    row = pl.BlockSpec((tm, d), lambda i: (i, 0))
    vec = pl.BlockSpec((1, d), lambda i: (0, 0))
    return _pcall(
        body, name="out_bwd", grid=(s_len // tm,),
        in_specs=[row, row, vec, pl.BlockSpec((2 * d, d), lambda i: (0, 0))],
        out_specs=[vec, row, pl.BlockSpec((2, tm, d), lambda i: (0, i, 0))],
        out_shape=[jax.ShapeDtypeStruct((1, d), F32), jax.ShapeDtypeStruct((s_len, d), BF16),
                   jax.ShapeDtypeStruct((2, s_len, d), F32)],
        compiler_params=_seq(),
    )(dxn, y, gate, w_out_b)


def _grad_matmul(a3, b3, nblk, a_idx, b_idx, out_shape, out_block, out_idx, layer, stack):
    _, s_len, m = a3.shape
    n = b3.shape[2]
    tk = _tile(s_len, 2048)
    first = isinstance(stack, int)

    def body(a_ref, b_ref, *rest):
        o_ref = rest[-1]

        @pl.when(pl.program_id(1) == 0)
        def _():
            o_ref[...] = jnp.zeros_like(o_ref)

        o_ref[...] += _dot_tn(a_ref[0], b_ref[0])

    in_specs = [pl.BlockSpec((1, tk, m), lambda p, t: (a_idx(p), t, 0)),
                pl.BlockSpec((1, tk, n), lambda p, t: (b_idx(p), t, 0))]
    return _pcall(
        body, name="grad_matmul", grid=(nblk, s_len // tk),
        in_specs=in_specs if first else in_specs + [pl.BlockSpec(memory_space=pl.ANY)],
        out_specs=pl.BlockSpec((None,) + out_block, lambda p, t: (layer,) + out_idx(p)),
        out_shape=jax.ShapeDtypeStruct(((stack,) if first else stack.shape[:1]) + out_shape, F32),
        input_output_aliases={} if first else {2: 0},
        compiler_params=_seq(2),
    )(*((a3, b3) if first else (a3, b3, stack)))


DU_PLANE = (2, 3, 4, 0, 1)


def _mlstm_bwd(qkv, gates, cst, nst, mst, cell, u, ml_g, d_ycat, wif_b, ride=None):
    _, s_len, d = qkv.shape
    ng = gates[0].shape[1]
    heads = ng // 2
    hd = d // heads
    lc = ML_CHUNK
    nc = s_len // lc
    kscale = hd ** -0.5

    def body(qkv_ref, gt_ref, gtt_ref, bc_ref, bct_ref, cst_ref, nst_ref, mst_ref, cell_ref, o_ref, z_ref, g_ref, dy_ref,
             wif_ref, dqkv_ref, dgt_ref, dbif_ref, du_ref, dg_ref, dcs, dns, dqs, dks, dvs):
        @pl.when(pl.program_id(0) == 0)
        def _():
            dbif_ref[...] = jnp.zeros_like(dbif_ref)
            dcs[...] = jnp.zeros_like(dcs)
            dns[...] = jnp.zeros_like(dns)
            dg_ref[...] = jnp.zeros_like(dg_ref)

        causal, tril, triu = _tri_masks(lc)
        tril_strict = (tril.astype(F32) - (tril * triu).astype(F32)).astype(BF16)
        gtv, gttv, bcv, bctv = gt_ref[...], gtt_ref[...], bc_ref[...], bct_ref[...]
        lane = lax.broadcasted_iota(jnp.int32, (lc, ng), 1)
        dli_all = jnp.zeros((lc, ng), F32)
        from_later = jnp.zeros((lc, ng), F32)
        from_earlier = jnp.zeros((lc, ng), F32)
        across_all = jnp.zeros((1, ng), F32)
        for h in range(heads):
            hs = slice(h * hd, (h + 1) * hd)
            li_c, li_r, gf_c, b_c, b_r = _chunk_gates(gtv, gttv, bcv, bctv, h, heads)
            m_prev = mst_ref[0, h][:, 0:1]
            m_t, w_intra, w_inter, _, w_state, decay = _chunk_weights(li_c, li_r, b_c, b_r, m_prev, causal)
            qb = qkv_ref[0, :, hs]
            qf = qb.astype(F32)
            ks = qkv_ref[1, :, hs].astype(F32) * kscale
            kb = _bf(ks)
            vb = qkv_ref[2, :, hs]
            c_b = cst_ref[0, h]
            n_old = nst_ref[0, h]
            s = _dot_nt(qb, kb) * w_intra
            den = _rowsum(s) + w_inter * _rowsum(qf * n_old)
            floor = jnp.exp(-m_t)
            dstab = jnp.maximum(jnp.abs(den), floor)
            cell = cell_ref[:, hs]
            o = o_ref[:, hs]
            so = _sigmoid(o)
            hm = so * cell
            rinv = lax.rsqrt(jnp.mean(hm * hm, axis=-1, keepdims=True) + EPS)
            hn = hm * rinv
            z = z_ref[:, hs]
            sgz = _sigmoid(z)
            sz = z * sgz
            gh = g_ref[:, hs]
            dy = dy_ref[0, :, hs]
            du_ref[1, :, hs] = _bf(dy * (hn * gh) * _dsilu(z, sgz))
            dg_ref[:, hs] += _colsum(dy * hn * sz)
            dhn = dy * gh * sz
            dhm = rinv * (dhn - hn * jnp.mean(dhn * hn, axis=-1, keepdims=True))
            du_ref[0, :, hs] = _bf(dhm * cell * so * (1.0 - so))
            dcell = dhm * so
            dnum = dcell / dstab
            dnb = _bf(dnum)
            dden = -_rowsum(dcell * cell) / dstab * jnp.where(jnp.abs(den) > floor, jnp.where(den > 0.0, 1.0, -1.0), 0.0)
            dst = _dot_nt(dnb, vb) + dden
            dsdb = _bf(dst * w_intra)
            dc_out = dcs[h]
            dn_out = dns[h]
            dcb = _bf(dc_out)
            dq_inter = w_inter * (_dot_nt(dnb, c_b) + dden * n_old)
            dk_inter = w_state * (_dot_nt(vb, dcb) + dn_out)
            dq = _dot(dsdb, kb) + dq_inter
            dk = _dot_tn(dsdb, qb) + dk_inter
            dv = _dot_tn(_bf(s), dnb) + _dot(_bf(ks * w_state), dcb)
            wq = w_inter * qf
            dcs[h] = decay * dc_out + _dot_tn(_bf(wq), dnb)
            dns[h] = decay * dn_out + _colsum(wq * dden)
            pmat = dst * s
            p_rows = _rowsum(pmat)
            p_cols = _rowsum(pmat.T)
            q_in = _rowsum(qf * dq_inter)
            k_in = _rowsum(ks * dk_inter)
            across = decay * (jnp.sum(dc_out * c_b.astype(F32), keepdims=True) + jnp.sum(dn_out * n_old, keepdims=True))
            dli_all = dli_all + jnp.where(lane == h, p_cols + k_in, 0.0)
            from_later = from_later + jnp.where(lane == heads + h, p_rows - p_cols + q_in, 0.0)
            from_earlier = from_earlier + jnp.where(lane == heads + h, k_in, 0.0)
            across_all = across_all + jnp.where(lane[0:1] == heads + h, across, 0.0)
            dqs[:, hs] = dq
            dks[:, hs] = dk * kscale
            dvs[:, hs] = dv
        dlf = _tri_dot_left(triu, from_later) + _tri_dot_left(tril_strict, from_earlier) + across_all
        dgt = dli_all + dlf * _sigmoid(-gtv)
        dgt_ref[...] = dgt
        dbif_ref[...] += _colsum(dgt)
        dgb = _bf(dgt)
        dqkv_ref[0] = _bf(dqs[...] + _dot_nt(dgb, wif_ref[0:d, :]))
        dqkv_ref[1] = _bf(dks[...] + _dot_nt(dgb, wif_ref[d:2 * d, :]))
        dqkv_ref[2] = _bf(dvs[...] + _dot_nt(dgb, wif_ref[2 * d:3 * d, :]))

    rev = lambda c: nc - 1 - c
    row = pl.BlockSpec((lc, d), lambda c: (rev(c), 0))
    gcol = pl.BlockSpec((lc, ng), lambda c: (rev(c), 0))
    grow = pl.BlockSpec((ng, lc), lambda c: (0, rev(c)))
    return _pcall_ride(
        body, ride, name="mlstm_bwd", grid=(nc,),
        in_specs=[pl.BlockSpec((3, lc, d), lambda c: (0, rev(c), 0)), gcol, grow, gcol, grow,
                  pl.BlockSpec((1, heads, hd, hd), lambda c: (rev(c), 0, 0, 0)),
                  pl.BlockSpec((1, heads, 1, hd), lambda c: (rev(c), 0, 0, 0)),
                  pl.BlockSpec((1, heads, 1, 128), lambda c: (rev(c), 0, 0, 0)),
                  row, pl.BlockSpec((lc, d), lambda c: (rev(c), 3)), pl.BlockSpec((lc, d), lambda c: (rev(c), 4)),
                  pl.BlockSpec((1, d), lambda c: (0, 0)), pl.BlockSpec((1, lc, d), lambda c: (1, rev(c), 0)),
                  pl.BlockSpec((3 * d, ng), lambda c: (0, 0))],
        out_specs=[pl.BlockSpec((3, lc, d), lambda c: (0, rev(c), 0)), pl.BlockSpec((lc, ng), lambda c: (rev(c), 0)),
                   pl.BlockSpec((1, ng), lambda c: (0, 0)), pl.BlockSpec((2, lc, d), lambda c: (0, rev(c), 0)),
                   pl.BlockSpec((1, d), lambda c: (0, 0))],
        out_shape=[jax.ShapeDtypeStruct((3, s_len, d), BF16), jax.ShapeDtypeStruct((s_len, ng), F32),
                   jax.ShapeDtypeStruct((1, ng), F32), jax.ShapeDtypeStruct((5, s_len, d), BF16),
                   jax.ShapeDtypeStruct((1, d), F32)],
        scratch_shapes=[pltpu.VMEM((heads, hd, hd), F32), pltpu.VMEM((heads, 1, hd), F32)]
        + [pltpu.VMEM((lc, d), F32)] * 3,
        compiler_params=_seq(),
        args=(qkv, *gates, cst, nst, mst, cell, u, u, ml_g, d_ycat, wif_b))


def _conv_bwd_tile(dp, later, taps, cw_ref, gw_ref, gb_ref):
    tm = dp.shape[0]
    dwin = jnp.concatenate([dp, later[...]], axis=0)
    later[...] = dp[0:HALO]
    acc = cw_ref[CONV_WIDTH - 1:CONV_WIDTH, :] * dp
    for k in range(CONV_WIDTH):
        if k < CONV_WIDTH - 1:
            acc = acc + cw_ref[k:k + 1, :] * _shift_up(dwin, CONV_WIDTH - 1 - k)[0:tm]
        gw_ref[k:k + 1, :] += _colsum(dp * taps[k])
    gb_ref[...] += _colsum(dp)
    return acc


def _ml_pre_bwd(dqkv, u, conv_w, conv_b, wqkv_b, du):
    s_len = u.shape[0]
    d = conv_w.shape[1]
    _, heads, hd, _ = wqkv_b.shape
    tm = _tile(s_len, 256)
    per = tm // HALO
    nt = s_len // tm

    def body(dqkv_ref, x_ref, xp_ref, cw_ref, cb_ref, w_ref, _, dx_ref, gw_ref, gcw_ref, gcb_ref, later, dps, dxs):
        i = pl.program_id(0)

        @pl.when(i == 0)
        def _():
            gw_ref[...] = jnp.zeros_like(gw_ref)
            gcw_ref[...] = jnp.zeros_like(gcw_ref)
            gcb_ref[...] = jnp.zeros_like(gcb_ref)
            later[...] = jnp.zeros_like(later)

        prev = jnp.where(i == nt - 1, 0.0, xp_ref[...])
        xm = x_ref[...]
        taps = _conv_taps(jnp.concatenate([prev, xm], axis=0))
        pre = _conv_fwd(taps, cw_ref, cb_ref)
        sg = _sigmoid(pre)
        xcb = _bf(pre * sg)
        xmb = _bf(xm)
        for h in range(heads):
            hs = slice(h * hd, (h + 1) * hd)
            dqh, dkh, dvh = dqkv_ref[0, :, hs], dqkv_ref[1, :, hs], dqkv_ref[2, :, hs]
            dxc = _dot_nt(dqh, w_ref[0, h]) + _dot_nt(dkh, w_ref[1, h])
            dps[:, hs] = dxc * _dsilu(pre[:, hs], sg[:, hs])
            dxs[:, hs] = _dot_nt(dvh, w_ref[2, h])
            gw_ref[0, h] += _dot_tn(xcb[:, hs], dqh)
            gw_ref[1, h] += _dot_tn(xcb[:, hs], dkh)
            gw_ref[2, h] += _dot_tn(xmb[:, hs], dvh)
        dx_ref[0] = _bf(_conv_bwd_tile(dps[...], later, taps, cw_ref, gcw_ref, gcb_ref) + dxs[...])

    rev = lambda i: nt - 1 - i
    vec = pl.BlockSpec((1, d), lambda i: (0, 0))
    cwb = pl.BlockSpec((CONV_WIDTH, d), lambda i: (0, 0))
    whole4 = pl.BlockSpec(wqkv_b.shape, lambda i: (0, 0, 0, 0))
    return _pcall(
        body, name="ml_pre_bwd", grid=(nt,),
        in_specs=[pl.BlockSpec((3, tm, d), lambda i: (0, rev(i), 0)), pl.BlockSpec((tm, d), lambda i: (rev(i), 2)),
                  pl.BlockSpec((HALO, d), lambda i: (jnp.maximum(rev(i) * per - 1, 0), 2)),
                  cwb, vec, whole4, pl.BlockSpec(memory_space=pl.ANY)],
        out_specs=[pl.BlockSpec((1, tm, d), lambda i: (DU_PLANE[2], rev(i), 0)), whole4, cwb, vec],
        out_shape=[jax.ShapeDtypeStruct(du.shape, BF16), jax.ShapeDtypeStruct(wqkv_b.shape, F32),
                   jax.ShapeDtypeStruct((CONV_WIDTH, d), F32), jax.ShapeDtypeStruct((1, d), F32)],
        scratch_shapes=[pltpu.VMEM((HALO, d), F32), pltpu.VMEM((tm, d), F32), pltpu.VMEM((tm, d), F32)],
        input_output_aliases={6: 0},
        compiler_params=_seq(),
    )(dqkv, u, u, conv_w, conv_b, wqkv_b, du)


def _rg_bwd(d_ycat, u, hh, conv_w, conv_b, wa_b, ba, wx_b, bx, lam, du):
    s_len = u.shape[0]
    d = conv_w.shape[1]
    heads, hd, _ = wa_b.shape
    tm = _tile(s_len, 256)
    per = tm // HALO
    nt = s_len // tm

    def body(dy_ref, x_ref, xp_ref, z_ref, hh_ref, hp_ref, cw_ref, cb_ref, wa_ref, ba_ref, wx_ref, bx_ref, lam_ref, _,
             du_ref, gwa_ref, gwx_ref, gba_ref, gbx_ref, glam_ref, gcw_ref, gcb_ref, carry, gbuf, later, dxcs):
        i = pl.program_id(0)
        first = i == nt - 1

        @pl.when(i == 0)
        def _():
            carry[...] = jnp.zeros_like(carry)
            later[...] = jnp.zeros_like(later)
            gwa_ref[...] = jnp.zeros_like(gwa_ref)
            gwx_ref[...] = jnp.zeros_like(gwx_ref)
            gba_ref[...] = jnp.zeros_like(gba_ref)
            gbx_ref[...] = jnp.zeros_like(gbx_ref)
            glam_ref[...] = jnp.zeros_like(glam_ref)
            gcw_ref[...] = jnp.zeros_like(gcw_ref)
            gcb_ref[...] = jnp.zeros_like(gcb_ref)

        prev = jnp.where(first, 0.0, xp_ref[...])
        taps = _conv_taps(jnp.concatenate([prev, x_ref[...]], axis=0))
        xc = _conv_fwd(taps, cw_ref, cb_ref)
        r, ig, sp, log_a, a, mult = _rg_gates(xc, wa_ref, ba_ref, wx_ref, bx_ref, lam_ref)
        z = z_ref[...]
        sgz = _sigmoid(z)
        dy = dy_ref[0]
        hh_v = hh_ref[...]
        du_ref[1] = _bf(dy * hh_v * _dsilu(z, sgz))
        dhh = dy * (z * sgz)
        rows = lax.broadcasted_iota(jnp.int32, a.shape, 0)
        coef = jnp.where(rows == tm - 1, carry[1:2, :], _shift_up(a, 1))
        ca, cu = _scan_groups(coef, dhh, reverse=True)
        c = carry[0:1, :]
        for j in range(per - 1, -1, -1):
            blk = ca[j * 8:(j + 1) * 8] * c + cu[j * 8:(j + 1) * 8]
            gbuf[j * 8:(j + 1) * 8, :] = blk
            c = blk[0:1]
        carry[0:1, :] = c
        carry[1:2, :] = a[0:1]
        g = gbuf[...]
        hprev_tile = jnp.where(first, 0.0, hp_ref[...])
        hprev = _shift_down(jnp.concatenate([hprev_tile, hh_v], axis=0), 1)[HALO:]
        da = g * hprev
        gx_ = g * xc
        d_mult = gx_ * ig
        d_ig = gx_ * mult
        dxc = g * mult * ig
        dlog_a = da * a - d_mult * (a * a / mult)
        d_r = dlog_a * ((-RG_C) * sp)
        glam_ref[...] += _colsum(dlog_a * ((-RG_C) * r)) * (-_sigmoid(-lam_ref[...]))
        d_ga = d_r * r * (1.0 - r)
        d_gx = d_ig * ig * (1.0 - ig)
        gba_ref[...] += _colsum(d_ga)
        gbx_ref[...] += _colsum(d_gx)
        xb = _bf(xc)
        dgab = _bf(d_ga)
        dgxb = _bf(d_gx)
        for h in range(heads):
            hs = slice(h * hd, (h + 1) * hd)
            dxcs[:, hs] = dxc[:, hs] + _dot_nt(dgab[:, hs], wa_ref[h]) + _dot_nt(dgxb[:, hs], wx_ref[h])
            gwa_ref[h] += _dot_tn(xb[:, hs], dgab[:, hs])
            gwx_ref[h] += _dot_tn(xb[:, hs], dgxb[:, hs])
        du_ref[0] = _bf(_conv_bwd_tile(dxcs[...], later, taps, cw_ref, gcw_ref, gcb_ref))

    assert DU_PLANE[0] % 2 == 0 and DU_PLANE[1] == DU_PLANE[0] + 1
    rev = lambda i: nt - 1 - i
    row = pl.BlockSpec((tm, d), lambda i: (rev(i), 0))
    halo_prev = lambda col: pl.BlockSpec((HALO, d), lambda i: (jnp.maximum(rev(i) * per - 1, 0), col))
    vec = pl.BlockSpec((1, d), lambda i: (0, 0))
    cwb = pl.BlockSpec((CONV_WIDTH, d), lambda i: (0, 0))
    whole3 = lambda a: pl.BlockSpec(a.shape, lambda i: (0, 0, 0))
    return _pcall(
        body, name="rg_bwd", grid=(nt,),
        in_specs=[pl.BlockSpec((1, tm, d), lambda i: (0, rev(i), 0)), row, halo_prev(0),
                  pl.BlockSpec((tm, d), lambda i: (rev(i), 1)), row, halo_prev(0),
                  cwb, vec, whole3(wa_b), vec, whole3(wx_b), vec, vec, pl.BlockSpec(memory_space=pl.ANY)],
        out_specs=[pl.BlockSpec((2, tm, d), lambda i: (DU_PLANE[0] // 2, rev(i), 0)), whole3(wa_b), whole3(wa_b),
                   vec, vec, vec, cwb, vec],
        out_shape=[jax.ShapeDtypeStruct(du.shape, BF16), jax.ShapeDtypeStruct(wa_b.shape, F32),
                   jax.ShapeDtypeStruct(wa_b.shape, F32)] + [jax.ShapeDtypeStruct((1, d), F32)] * 3
        + [jax.ShapeDtypeStruct((CONV_WIDTH, d), F32), jax.ShapeDtypeStruct((1, d), F32)],
        scratch_shapes=[pltpu.VMEM((8, d), F32), pltpu.VMEM((tm, d), F32), pltpu.VMEM((HALO, d), F32),
                        pltpu.VMEM((tm, d), F32)],
        input_output_aliases={13: 0},
        compiler_params=_seq(),
    )(d_ycat, u, u, u, hh, hh, conv_w, conv_b, wa_b, ba, wx_b, bx, lam, du)


def _in_bwd(du, w4, x, dxn, g, scale):
    s_len, d = x.shape
    tm = _tile(s_len, 512)
    nsh_chips, _, nsh = w4.shape
    npc = du.shape[0]
    ck = d // 4
    assert nsh % ck == 0 and npc * d == nsh_chips * nsh

    def body(du_ref, w_ref, x_ref, dxn_ref, g_ref, sc_ref, dx_ref, dsh_ref, dsc_ref, dg_ref):
        @pl.when(pl.program_id(0) == 0)
        def _():
            dsh_ref[...] = jnp.zeros_like(dsh_ref)
            dsc_ref[...] = jnp.zeros_like(dsc_ref)
            dg_ref[...] = jnp.zeros_like(dg_ref)

        dh = None
        for q in range(npc * d // ck):
            col = q * ck
            p, pc = col // d, col % d
            s, sc = col // nsh, col % nsh
            t = _dot_nt(du_ref[DU_PLANE[p], :, pc:pc + ck], w_ref[s, :, sc:sc + ck])
            dh = t if dh is None else dh + t
        xv = x_ref[...]
        r = lax.rsqrt(jnp.mean(xv * xv, axis=-1, keepdims=True) + EPS)
        xn = xv * r
        gv = g_ref[...]
        onesc = 1.0 + sc_ref[...]
        dsh_ref[...] += _colsum(dh)
        dsc_ref[...] += _colsum(dh * (xn * gv))
        dg_ref[...] += _colsum(dh * xn * onesc)
        dxh = dh * (gv * onesc)
        dx_ref[...] = dxn_ref[...] + r * (dxh - xn * jnp.mean(dxh * xn, axis=-1, keepdims=True))

    row = pl.BlockSpec((tm, d), lambda i: (i, 0))
    vec = pl.BlockSpec((1, d), lambda i: (0, 0))
    return _pcall(
        body, name="in_bwd", grid=(s_len // tm,),
        in_specs=[pl.BlockSpec((npc, tm, d), lambda i: (0, i, 0)), pl.BlockSpec(w4.shape, lambda i: (0, 0, 0)), row, row,
                  vec, vec],
        out_specs=[row, vec, vec, vec],
        out_shape=[jax.ShapeDtypeStruct((s_len, d), F32)] + [jax.ShapeDtypeStruct((1, d), F32)] * 3,
        compiler_params=_seq(),
    )(du, w4, x, dxn, g, scale)


def _layer_fwd(x, p, rides=(None, None)):
    h_b, u = _ln_inproj(x, p["norm_g"], p["scale"], p["shift"], p["w4"])
    (hh, ycat), got_a = _rg_fwd(u, p["rg_conv_w"], p["rg_conv_b"], p["rg_wa_b"], p["rg_ba"], p["rg_wx_b"], p["rg_bx"],
                                p["rg_lam"], rides[0])
    qkv, *gates = _ml_pre(u, p["ml_conv_w"], p["ml_conv_b"], p["wqkv_b"], p["wif_b"], p["wift_b"], p["b_if"],
                          p["b_ift"])
    (cell, ycat, cst, nst, mst), got_b = _mlstm_fwd(qkv, gates, u, p["ml_g"], ycat, rides[1])
    y, x_new = _out_proj(ycat, p["w_out_b"], x, p["gate"])
    saved = dict(x=x, h_b=h_b, u=u, hh=hh, qkv=qkv, gates=gates, cell=cell, ycat=ycat, cst=cst, nst=nst, mst=mst, y=y)
    return x_new, saved, list(got_a) + list(got_b)


def _layer_bwd(dxn, p, s, ride=None):
    u = s["u"]
    d = dxn.shape[1]
    d_gate, dy_b, d_ycat = _out_bwd(dxn, s["y"], p["gate"], p["w_out_b"])
    g_w_out = _grad_matmul(s["ycat"], dy_b[None], 2, lambda b: b, lambda b: 0, (2 * d, d), (d, d), lambda b: (b, 0),
                           0, 1)
    (dqkv, dgt, g_b_if, du, g_ml_g), got = _mlstm_bwd(s["qkv"], s["gates"], s["cst"], s["nst"], s["mst"], s["cell"], u,
                                                      p["ml_g"], d_ycat, p["wif_b"], ride)
    ng = dgt.shape[1]
    g_w_if = _grad_matmul(s["qkv"], _bf(dgt)[None], 3, lambda b: b, lambda b: 0, (3 * d, ng), (d, ng),
                          lambda b: (b, 0), 0, 1)[0]
    du, g_wqkv, g_ml_cw, g_ml_cb = _ml_pre_bwd(dqkv, u, p["ml_conv_w"], p["ml_conv_b"], p["wqkv_b"], du)
    du, g_wa, g_wx, g_ba, g_bx, g_lam, g_rg_cw, g_rg_cb = _rg_bwd(d_ycat, u, s["hh"], p["rg_conv_w"], p["rg_conv_b"],
                                                                  p["rg_wa_b"], p["rg_ba"], p["rg_wx_b"], p["rg_bx"],
                                                                  p["rg_lam"], du)
    npc = du.shape[0]
    g_w_in = _grad_matmul(s["h_b"][None], du, npc, lambda b: 0, lambda b: (b + DU_PLANE[0]) % npc, (d, npc * d),
                          (d, d), lambda b: (0, b), 0, 1)
    dx, d_shift, d_scale, g_norm_g = _in_bwd(du, p["w4"], s["x"], dxn, p["norm_g"], p["scale"])
    grads = dict(norm_g=g_norm_g, w_in=g_w_in, rg_conv_w=g_rg_cw, rg_conv_b=g_rg_cb, rg_w_a=g_wa, rg_b_a=g_ba,
                 rg_w_x=g_wx, rg_b_x=g_bx, rg_lambda=g_lam, ml_conv_w=g_ml_cw, ml_conv_b=g_ml_cb, ml_w_qkv=g_wqkv,
                 ml_w_if=g_w_if, ml_b_if=g_b_if, ml_norm_g=g_ml_g, w_out=g_w_out)
    return dx, grads, jnp.concatenate([d_shift, d_scale, d_gate], axis=1), got


def _trunk_fwd_bwd(x, target, final_g, layers):
    saved = []
    for p in layers:
        x, s, _ = _layer_fwd(x, p)
        saved.append(s)
    dx, g_final, loss = _final_loss(x, final_g, target)
    grads, dmods = [], []
    for layer in reversed(range(len(layers))):
        dx, g, dm, _ = _layer_bwd(dx, layers[layer], saved[layer])
        grads.append(g)
        dmods.append(dm)
    return loss, dx, g_final, grads[::-1], dmods[::-1]


def _me():
    return lax.axis_index("x"), lax.axis_index("y"), lax.axis_index("c")


def _remote(src, dst, send_sem, recv_sem, to):
    return pltpu.make_async_remote_copy(src_ref=src, dst_ref=dst, send_sem=send_sem, recv_sem=recv_sem,
                                        device_id=to, device_id_type=MESH)


def _all_gather8(blocks, space):
    n = len(blocks)

    def body(*refs):
        x_refs, out_refs = refs[:n], refs[n:2 * n]
        send_sems, recv_sems, local_sems = refs[2 * n:]
        x, y, c = _me()
        me, sibling = (x, y, c), (x, y, 1 - c)
        chips = [(1 - x, y), (x, 1 - y), (1 - x, 1 - y)]

        def rows(i, px, py, pc):
            m_per = blocks[i].shape[0]
            return out_refs[i].at[pl.ds((4 * px + 2 * py + pc) * m_per, m_per), :]

        def copy(i, k, blk, to, src=None):
            return _remote(rows(i, *blk) if src is None else src, rows(i, *blk), send_sems.at[7 * i + k],
                           recv_sems.at[7 * i + k], to)

        mine = [pltpu.make_async_copy(x_refs[i], rows(i, *me), local_sems.at[i]) for i in range(n)]
        first = []
        for i in range(n):
            first.append(copy(i, 0, me, sibling, src=x_refs[i]))
            first += [copy(i, 1 + j, me, (*chip, c), src=x_refs[i]) for j, chip in enumerate(chips)]
        for cp in mine + first:
            cp.start()
        passed = []
        for j, chip in enumerate(chips):
            for i in range(n):
                copy(i, 1 + j, (*chip, c), me).wait_recv()
                passed.append(copy(i, 4 + j, (*chip, c), sibling))
                passed[-1].start()
        for i in range(n):
            copy(i, 0, sibling, me).wait_recv()
            for j, chip in enumerate(chips):
                copy(i, 4 + j, (*chip, 1 - c), me).wait_recv()
        for cp in first + passed:
            cp.wait_send()
        for cp in mine:
            cp.wait()

    spec = pl.BlockSpec(memory_space=space)
    return _pcall(
        body, name="all_gather8",
        out_shape=[jax.ShapeDtypeStruct((8 * b.shape[0], b.shape[1]), b.dtype) for b in blocks],
        in_specs=[spec] * n, out_specs=[spec] * n,
        scratch_shapes=[pltpu.SemaphoreType.DMA((7 * n,)), pltpu.SemaphoreType.DMA((7 * n,)),
                        pltpu.SemaphoreType.DMA((n,))],
    )(*blocks)


def _sib_halves(g_in, g_out, slabs):
    depth, d, n4 = g_in.shape
    n = n4 // 4
    ns = len(slabs)

    def body(*refs):
        gi, go = refs[0], refs[1]
        sl = refs[2:2 + ns]
        ri, ro = refs[2 + ns], refs[3 + ns]
        rs = refs[4 + ns:4 + 2 * ns]
        send_sems, recv_sems = refs[4 + 2 * ns:]
        x, y, c = _me()
        o = 1 - c
        pairs = [(gi.at[pl.ds(0, depth), pl.ds(o * (d // 2), d // 2), pl.ds(s * n, n)], ri.at[pl.ds(0, depth), s])
                 for s in range(4)]
        pairs.append((go.at[pl.ds(0, depth), pl.ds(0, 4), o], ro))
        pairs += [(sl[i].at[o], rs[i]) for i in range(ns)]
        copies = [_remote(src, dst, send_sems.at[k], recv_sems.at[k], (x, y, o)) for k, (src, dst) in enumerate(pairs)]
        for cp in copies:
            cp.start()
        for cp in copies:
            cp.wait_recv()
        for cp in copies:
            cp.wait_send()

    hbm = pl.BlockSpec(memory_space=pltpu.HBM)
    ncp = 5 + ns
    return _pcall(
        body, name="sib_halves",
        out_shape=[jax.ShapeDtypeStruct((depth, 4, d // 2, n), g_in.dtype),
                   jax.ShapeDtypeStruct(g_out.shape[:2] + g_out.shape[3:], g_out.dtype)]
        + [jax.ShapeDtypeStruct(s.shape[1:], s.dtype) for s in slabs],
        in_specs=[hbm] * (2 + ns), out_specs=[hbm] * (2 + ns),
        scratch_shapes=[pltpu.SemaphoreType.DMA((ncp,)), pltpu.SemaphoreType.DMA((ncp,))],
    )(g_in, g_out, *slabs)


def _sib_fill(boths):
    n = len(boths)

    def body(*refs):
        dst = refs[n:2 * n]
        send_sems, recv_sems = refs[2 * n:]
        x, y, c = _me()
        view = lambda i: dst[i].at[pl.ds(0, boths[i].shape[0]), c]
        copies = [_remote(view(i), view(i), send_sems.at[i], recv_sems.at[i], (x, y, 1 - c)) for i in range(n)]
        for cp in copies:
            cp.start()
        for cp in copies:
            cp.wait_recv()
        for cp in copies:
            cp.wait_send()

    hbm = pl.BlockSpec(memory_space=pltpu.HBM)
    return _pcall(
        body, name="sib_fill",
        out_shape=[jax.ShapeDtypeStruct(b.shape, b.dtype) for b in boths],
        in_specs=[hbm] * n, out_specs=[hbm] * n, input_output_aliases={i: i for i in range(n)},
        scratch_shapes=[pltpu.SemaphoreType.DMA((n,)), pltpu.SemaphoreType.DMA((n,))],
    )(*boths)


def _chip_exchange(arrs):
    n = len(arrs)

    def body(*refs):
        src, dst = refs[:n], refs[n:2 * n]
        send_sems, recv_sems = refs[2 * n:]
        x, y, c = _me()
        me_s = 2 * x + y
        chips = [(1 - x, y), (x, 1 - y), (1 - x, 1 - y)]
        copies = [_remote(src[i].at[2 * px + py], dst[i].at[me_s], send_sems.at[3 * i + k], recv_sems.at[3 * i + k],
                          (px, py, c))
                  for i in range(n) for k, (px, py) in enumerate(chips)]
        for cp in copies:
            cp.start()
        for cp in copies:
            cp.wait_recv()
        for cp in copies:
            cp.wait_send()

    hbm = pl.BlockSpec(memory_space=pltpu.HBM)
    return _pcall(
        body, name="chip_exchange",
        out_shape=[jax.ShapeDtypeStruct(a.shape, a.dtype) for a in arrs],
        in_specs=[hbm] * n, out_specs=[hbm] * n,
        scratch_shapes=[pltpu.SemaphoreType.DMA((3 * n,)), pltpu.SemaphoreType.DMA((3 * n,))],
    )(*arrs)


def _row_tile(rows, cap=4096, mult=16):
    best = None
    for t in range(mult, min(rows, cap) + 1, mult):
        if rows % t == 0:
            best = t
    return rows if best is None else best


def _pair_sum(half, own, own_spec, got, got_spec, out_shape, out_spec, grid):
    def body(_, a_ref, b_ref, o_ref):
        o_ref[...] = (a_ref[...] + b_ref[...].astype(F32)).astype(o_ref.dtype)

    return _pcall(
        body, name="pair_sum",
        grid_spec=pltpu.PrefetchScalarGridSpec(num_scalar_prefetch=1, grid=grid, in_specs=[own_spec, got_spec],
                                               out_specs=out_spec),
        out_shape=out_shape, compiler_params=_seq(len(grid)))(half, own, got)


def _chip_sum(ids, part, met, fill, layer=0, stack=1):
    _, _, rows, n = part.shape
    tr = _row_tile(rows, cap=max(16, (1 << 18) // n))
    first = isinstance(stack, int)

    def body(_, own_ref, a_ref, b_ref, c_ref, *rest):
        acc = own_ref[...].astype(F32) + a_ref[...].astype(F32)
        acc = acc + b_ref[...].astype(F32)
        rest[-1][...] = acc + c_ref[...].astype(F32)

    blk = (None, None, tr, n)
    other = lambda k: pl.BlockSpec(blk, lambda j, ids: ((ids[0] + k) % 4, 0, j, 0))
    in_specs = [pl.BlockSpec(blk, lambda j, ids: (ids[0], 0, j, 0)), other(1), other(2), other(3)]
    return _pcall(
        body, name="chip_sum",
        grid_spec=pltpu.PrefetchScalarGridSpec(
            num_scalar_prefetch=1, grid=(rows // tr,),
            in_specs=in_specs if first else in_specs + [pl.BlockSpec(memory_space=pl.ANY)],
            out_specs=pl.BlockSpec(blk, lambda j, ids: (layer, ids[1] if fill else 0, j, 0))),
        out_shape=jax.ShapeDtypeStruct(((stack,) if first else stack.shape[:1]) + (2 if fill else 1, rows, n), F32),
        input_output_aliases={} if first else {5: 0},
        compiler_params=_seq())(*((ids, part, met, met, met) if first else (ids, part, met, met, met, stack)))


def _ada_mod(c_all, w_ada, b_ada_cols):
    depth, d, n = w_ada.shape
    nb = c_all.shape[0]

    def body(c_ref, w_ref, b_ref, o_ref):
        cv = c_ref[...]
        ca = _bf(cv * _sigmoid(cv))
        o_ref[0] = _dot(ca, _bf(w_ref[0])) + b_ref[0]

    return _pcall(body, name="ada_mod", grid=(depth,),
                  in_specs=[pl.BlockSpec((nb, d), lambda l: (0, 0)), pl.BlockSpec((1, d, n), lambda l: (l, 0, 0)),
                            pl.BlockSpec((1, 1, n), lambda l: (l, 0, 0))],
                  out_specs=pl.BlockSpec((1, nb, n), lambda l: (l, 0, 0)),
                  out_shape=jax.ShapeDtypeStruct((depth, nb, n), F32), compiler_params=_seq())(c_all, w_ada, b_ada_cols)


def _ada_grad(c_all, dmod_cols, dmod_all):
    nb, d = c_all.shape
    depth, _, n = dmod_cols.shape
    n_all = dmod_all.shape[2]

    def body(c_ref, dm_ref, da_ref, gw_ref, gb_ref):
        cv = c_ref[...]
        ca = _bf(cv * _sigmoid(cv))
        gw_ref[0] = _dot_tn(ca, _bf(dm_ref[0]))
        gb_ref[0] = _colsum(da_ref[0])

    return _pcall(body, name="ada_grad", grid=(depth,),
                  in_specs=[pl.BlockSpec((nb, d), lambda l: (0, 0)), pl.BlockSpec((1, nb, n), lambda l: (l, 0, 0)),
                            pl.BlockSpec((1, nb, n_all), lambda l: (l, 0, 0))],
                  out_specs=[pl.BlockSpec((1, d, n), lambda l: (l, 0, 0)), pl.BlockSpec((1, 1, n_all), lambda l: (l, 0, 0))],
                  out_shape=[jax.ShapeDtypeStruct((depth, d, n), F32), jax.ShapeDtypeStruct((depth, 1, n_all), F32)],
                  compiler_params=_seq())(c_all, dmod_cols, dmod_all)


def _adamw(w, g, m, v):
    shape = w.shape
    cols = shape[-1]
    rows = w.size // cols
    w2, g2, m2, v2 = (t.reshape(rows, cols) for t in (w, g, m, v))
    tr = _row_tile(rows, cap=max(8, (1 << 18) // cols), mult=8)

    def body(w_ref, g_ref, m_ref, v_ref, d_ref, mo_ref, vo_ref):
        gv = g_ref[...]
        mn = ADAM_B1 * m_ref[...] + (1.0 - ADAM_B1) * gv
        vn = ADAM_B2 * v_ref[...] + (1.0 - ADAM_B2) * (gv * gv)
        m_hat = mn / (1.0 - ADAM_B1 ** ADAM_STEP)
        v_hat = vn / (1.0 - ADAM_B2 ** ADAM_STEP)
        d_ref[...] = -ADAM_LR * (m_hat / (jnp.sqrt(v_hat) + ADAM_EPS) + ADAM_WD * w_ref[...])
        mo_ref[...] = mn
        vo_ref[...] = vn

    blk = pl.BlockSpec((tr, cols), lambda i: (i, 0))
    outs = _pcall(body, name="adamw", grid=(rows // tr,), in_specs=[blk] * 4, out_specs=[blk] * 3,
                  out_shape=[jax.ShapeDtypeStruct((rows, cols), F32)] * 3, compiler_params=_seq())(w2, g2, m2, v2)
    return tuple(o.reshape(shape) for o in outs)


WEIGHTS = ["norm_g", "w_ada", "b_ada", "w_in", "rg_conv_w", "rg_conv_b", "rg_w_a", "rg_b_a", "rg_w_x", "rg_b_x",
           "rg_lambda", "ml_conv_w", "ml_conv_b", "ml_w_q", "ml_w_k", "ml_w_v", "ml_w_if", "ml_b_if", "ml_norm_g",
           "w_out", "final_g"]
SMALL_SHARDED = {"ml_w_qkv": 2, "rg_conv_w": 1, "ml_conv_w": 1, "ml_w_if": 0}
REPLICATED = ["rg_w_a", "rg_w_x", "norm_g", "rg_conv_b", "rg_b_a", "rg_b_x", "rg_lambda", "ml_conv_b", "ml_norm_g",
              "ml_b_if"]
LANES = 128


def _to_pieces(g, axis):
    shp = g.shape
    g = g.reshape(shp[:axis] + (4, 2, shp[axis] // 8) + shp[axis + 1:])
    g = jnp.moveaxis(g, (axis, axis + 1), (0, 1))
    return g.reshape(4, 2, -1)


def _from_pieces(p, shard_shape, axis):
    k = p.shape[0]
    rest = shard_shape[:axis] + (shard_shape[axis] // k,) + shard_shape[axis + 1:]
    t = jnp.moveaxis(p.reshape((k,) + rest), 0, axis)
    return t.reshape(shard_shape)


def _pad_rows(flat, mult):
    n = flat.shape[-1]
    pad = (-n) % mult
    if pad:
        flat = jnp.concatenate([flat, jnp.zeros(flat.shape[:-1] + (pad,), flat.dtype)], axis=-1)
    return flat


def kernel(x, c, norm_g, w_ada, b_ada, w_in, rg_conv_w, rg_conv_b, rg_w_a, rg_b_a, rg_w_x, rg_b_x, rg_lambda, ml_conv_w, ml_conv_b, ml_w_q, ml_w_k, ml_w_v, ml_w_if, ml_b_if, ml_norm_g, w_out, final_g, loss_target, m_norm_g, m_w_ada, m_b_ada, m_w_in, m_rg_conv_w, m_rg_conv_b, m_rg_w_a, m_rg_b_a, m_rg_w_x, m_rg_b_x, m_rg_lambda, m_ml_conv_w, m_ml_conv_b, m_ml_w_q, m_ml_w_k, m_ml_w_v, m_ml_w_if, m_ml_b_if, m_ml_norm_g, m_w_out, m_final_g, v_norm_g, v_w_ada, v_b_ada, v_w_in, v_rg_conv_w, v_rg_conv_b, v_rg_w_a, v_rg_b_a, v_rg_w_x, v_rg_b_x, v_rg_lambda, v_ml_conv_w, v_ml_conv_b, v_ml_w_q, v_ml_w_k, v_ml_w_v, v_ml_w_if, v_ml_b_if, v_ml_norm_g, v_w_out, v_final_g):
    given = dict(locals())
    ax, ay, ac = lax.axis_index("x"), lax.axis_index("y"), lax.axis_index("c")
    chip = 2 * ax + ay
    me = 2 * chip + ac
    depth, d = norm_g.shape
    n_ada = w_ada.shape[2]
    pick = lambda a, i, axis=0: lax.dynamic_index_in_dim(a, i, axis, keepdims=False)

    convs = jnp.stack([rg_conv_w, ml_conv_w])
    n_conv = 2 * depth * CONV_WIDTH // 4
    blk = jnp.concatenate([c, convs.reshape(n_conv, d), jnp.zeros((8 - 1 - n_conv, d), F32)], axis=0)
    g0 = _all_gather8([blk], pltpu.VMEM)[0].reshape(8, 8, d)
    c_all = g0[:, 0, :]
    conv_full = g0[0::2, 1:1 + n_conv].reshape(4, 2, depth, CONV_WIDTH, d // 4)
    conv_full = conv_full.transpose(1, 2, 3, 0, 4).reshape(2, depth, CONV_WIDTH, d)

    b_cols = lax.dynamic_slice_in_dim(b_ada, chip * n_ada, n_ada, axis=1)[:, None, :]
    mod_part = _ada_mod(c_all, w_ada, b_cols)
    g1 = _all_gather8([mod_part.transpose(1, 0, 2).reshape(8, depth * n_ada)], pltpu.VMEM)[0]
    g1 = g1.reshape(8, 8, depth, n_ada)[0::2]
    mod_me = pick(g1.transpose(1, 2, 0, 3).reshape(8, depth, 4 * n_ada), me)

    def half_of(w, axis):
        n = w.shape[axis] // 2
        return lax.dynamic_slice_in_dim(w, ac * n, n, axis).astype(BF16)

    n_sh = w_in.shape[2]
    heads, hd_cut, hd = ml_w_q.shape[1:]

    def blocks_of(l):
        wqkv = jnp.stack([ml_w_q[l], ml_w_k[l], ml_w_v[l]])
        return [half_of(w_in[l], 0), half_of(w_out[l], 0), half_of(wqkv, 2).reshape(-1, hd), half_of(ml_w_if[l], 0)]

    def layer_of(l, gathered):
        w4, w_out_b, wqkv_g, wif = gathered
        wqkv_b = _from_pieces(wqkv_g.reshape(8, -1), (3, heads, hd, hd), 2)
        return dict(
            norm_g=norm_g[l][None], shift=mod_me[l, 0:d][None], scale=mod_me[l, d:2 * d][None],
            gate=mod_me[l, 2 * d:3 * d][None], w4=w4.reshape(4, d, n_sh),
            rg_conv_w=conv_full[0, l], rg_conv_b=rg_conv_b[l][None], rg_wa_b=_bf(rg_w_a[l]), rg_ba=rg_b_a[l][None],
            rg_wx_b=_bf(rg_w_x[l]), rg_bx=rg_b_x[l][None], rg_lam=rg_lambda[l][None],
            ml_conv_w=conv_full[1, l], ml_conv_b=ml_conv_b[l][None], wqkv_b=wqkv_b, wif_b=wif, wift_b=wif.T,
            b_if=ml_b_if[l][None], b_ift=ml_b_if[l][:, None], ml_g=ml_norm_g[l][None], w_out_b=w_out_b)

    landing = lambda b: jax.ShapeDtypeStruct((4, 2) + b.shape, b.dtype)
    layers = [layer_of(0, _all_gather8(blocks_of(0), pltpu.HBM))]
    saved = []
    xl = x[0]
    for l in range(depth):
        rides = (None, None)
        if l + 1 < depth:
            nxt = blocks_of(l + 1)
            rides = (Ride(nxt[:1], [landing(nxt[0])], False), Ride(nxt[1:], [landing(b) for b in nxt[1:]], False))
        xl, s, got = _layer_fwd(xl, layers[l], rides)
        saved.append(s)
        if l + 1 < depth:
            layers.append(layer_of(l + 1, [t.reshape(-1, t.shape[-1]) for t in _sib_fill(got)]))
    dx, g_final, loss = _final_loss(xl, final_g[None], loss_target[0])

    half = ac.reshape(1)
    ids = jnp.stack([chip, ac])
    r_out = w_out.shape[1] // 2

    def halves_summed(g, slabs):
        g_out5 = g["w_out"].reshape(1, 4, 2, r_out, d)
        got_in, got_out, *got_slabs = _sib_halves(g["w_in"], g_out5, slabs)
        part_in = _pair_sum(
            half, g["w_in"], pl.BlockSpec((None, d // 2, n_sh), lambda s, h: (0, h[0], s)),
            got_in, pl.BlockSpec((None, None, d // 2, n_sh), lambda s, h: (0, s, 0, 0)),
            jax.ShapeDtypeStruct((4, 1, d // 2, n_sh), BF16),
            pl.BlockSpec((None, None, d // 2, n_sh), lambda s, h: (s, 0, 0, 0)), (4,))
        part_out = _pair_sum(
            half, g_out5, pl.BlockSpec((None, None, None, r_out, d), lambda s, h: (0, s, h[0], 0, 0)),
            got_out, pl.BlockSpec((None, None, r_out, d), lambda s, h: (0, s, 0, 0)),
            jax.ShapeDtypeStruct((4, 1, r_out, d), BF16),
            pl.BlockSpec((None, None, r_out, d), lambda s, h: (s, 0, 0, 0)), (4,))
        return [part_in, part_out], got_slabs

    grads, dmods, parts, mets = [None] * depth, [None] * depth, [None] * depth, [None] * depth
    ride = None
    for l in reversed(range(depth)):
        dx, grads[l], dmods[l], got = _layer_bwd(dx, layers[l], saved[l], ride)
        if ride is not None:
            mets[l + 1] = got
        if l > 0:
            parts[l], _ = halves_summed(grads[l], [])
            ride = Ride(parts[l], [jax.ShapeDtypeStruct(t.shape, t.dtype) for t in parts[l]], True)

    dm_blk = jnp.concatenate(dmods + [jnp.zeros((8 - depth, 3 * d), F32)], axis=0)
    dm_all = _all_gather8([dm_blk], pltpu.VMEM)[0].reshape(8, 8, 3 * d)[:, :depth].transpose(1, 0, 2)
    dm_cols = lax.dynamic_slice_in_dim(dm_all, chip * n_ada, n_ada, axis=2)
    g_w_ada, g_b_ada = _ada_grad(c_all, dm_cols, dm_all)

    sm = jnp.concatenate([_to_pieces(grads[l][name], axis) for l in range(depth) for name, axis in SMALL_SHARDED.items()],
                         axis=-1)
    sm = _pad_rows(sm, 16 * LANES)
    n_sm = sm.shape[-1] // LANES
    sm = sm.transpose(1, 0, 2).reshape(2, 4 * n_sm, LANES)
    rep = [grads[l][name].reshape(-1) for l in range(depth) for name in REPLICATED[:-1]]
    rep += [_pad_rows(grads[l]["ml_b_if"].reshape(-1), LANES) for l in range(depth)]
    rep += [g_final.reshape(-1), loss.reshape(-1)]
    rep = _pad_rows(jnp.concatenate(rep), 8 * 8 * LANES)
    n_rep = rep.shape[0] // (8 * LANES)
    rep = rep.reshape(4, 2, n_rep, LANES).transpose(1, 0, 2, 3).reshape(2, 4 * n_rep, LANES)
    parts[0], (got_sm, got_rep) = halves_summed(grads[0], [sm, rep])

    def slab_sum(slab, got, rows, dtype):
        blk = pl.BlockSpec((rows, LANES), lambda s, h: (s, 0))
        return _pair_sum(half, slab, pl.BlockSpec((None, rows, LANES), lambda s, h: (h[0], s, 0)), got, blk,
                         jax.ShapeDtypeStruct((4 * rows, LANES), dtype), blk, (4,)).reshape(4, 1, rows, LANES)

    part_sm = slab_sum(sm, got_sm, n_sm, BF16)
    part_rep = slab_sum(rep, got_rep, n_rep, F32)
    *mets[0], met_sm, met_rep = _chip_exchange(parts[0] + [part_sm, part_rep])
    both_in, both_out = depth, depth
    for l in range(depth):
        both_in = _chip_sum(ids, parts[l][0], mets[l][0], True, l, both_in)
        both_out = _chip_sum(ids, parts[l][1], mets[l][1], True, l, both_out)
    both_in, both_out, both_sm = _sib_fill([both_in, both_out, _chip_sum(ids, part_sm, met_sm, True)])
    red_rep = _chip_sum(ids, part_rep, met_rep, False).reshape(n_rep, LANES)
    rep_all = _all_gather8([red_rep], pltpu.VMEM)[0].reshape(-1)

    g = dict(w_ada=g_w_ada, b_ada=g_b_ada.reshape(b_ada.shape), w_in=both_in.reshape(w_in.shape),
             w_out=both_out.reshape(w_out.shape))
    shard = both_sm.reshape(2, -1)
    off = 0
    per_layer = {name: [] for name in SMALL_SHARDED}
    for l in range(depth):
        for name, axis in SMALL_SHARDED.items():
            shp = (3,) + ml_w_q.shape[1:] if name == "ml_w_qkv" else given[name].shape[1:]
            n = grads[l][name].size // 8
            per_layer[name].append(_from_pieces(shard[:, off:off + n], shp, axis))
            off += n
    for name in SMALL_SHARDED:
        g[name] = jnp.stack(per_layer[name])
    for i, name in enumerate(["ml_w_q", "ml_w_k", "ml_w_v"]):
        g[name] = g["ml_w_qkv"][:, i]
    off = 0
    per_layer = {name: [] for name in REPLICATED}
    for l in range(depth):
        for name in REPLICATED[:-1]:
            n = given[name][l].size
            per_layer[name].append(rep_all[off:off + n].reshape(given[name].shape[1:]))
            off += n
    for l in range(depth):
        n = given["ml_b_if"][l].size
        per_layer["ml_b_if"].append(rep_all[off:off + n])
        off += LANES
    for name in REPLICATED:
        g[name] = jnp.stack(per_layer[name])
    g["final_g"] = rep_all[off:off + d]
    loss_all = rep_all[off + d]

    deltas, new_m, new_v = [], [], []
    for name in WEIGHTS:
        dl, mn, vn = _adamw(given[name], g[name], given["m_" + name], given["v_" + name])
        deltas.append(dl)
        new_m.append(mn)
        new_v.append(vn)
    return (loss_all, dx[None], *[g[name] for name in WEIGHTS], *deltas, *new_m, *new_v)
```

```python
import functools
from typing import NamedTuple

import jax
import jax.numpy as jnp
from jax import lax
from jax.experimental import pallas as pl
from jax.experimental.pallas import tpu as pltpu

F32 = jnp.float32
BF16 = jnp.bfloat16

EPS = 1e-6
RG_C = 8.0
CONV_WIDTH = 4
ML_CHUNK = 128
HALO = 8
ADAM_LR = 0.001
ADAM_B1 = 0.9
ADAM_B2 = 0.999
ADAM_EPS = 1e-08
ADAM_WD = 0.01
ADAM_STEP = 10
MESH = pl.DeviceIdType.MESH


def _pcall(body, **kw):
    return pl.pallas_call(body, **kw)


class Ride(NamedTuple):
    srcs: list
    dst_shapes: list
    sliced: bool


def _pcall_ride(body, ride, *, grid, in_specs, out_specs, out_shape, args, scratch_shapes=(), **kw):
    n_in, n_out, n_scr = len(in_specs), len(out_specs), len(scratch_shapes)
    if ride is None:
        res = _pcall(body, grid=grid, in_specs=in_specs, out_specs=out_specs, out_shape=out_shape,
                     scratch_shapes=list(scratch_shapes), **kw)(*args)
        return res, []
    nr = len(ride.srcs)

    def riding(*refs):
        ins, rsrc = refs[:n_in], refs[n_in:n_in + nr]
        outs, rdst = refs[n_in + nr:n_in + nr + n_out], refs[n_in + nr + n_out:n_in + 2 * nr + n_out]
        scr = refs[n_in + 2 * nr + n_out:n_in + 2 * nr + n_out + n_scr]
        send_sems, recv_sems, local_sems = refs[n_in + 2 * nr + n_out + n_scr:]
        x, y, c = _me()
        me_s = 2 * x + y
        chips = [(1 - x, y), (x, 1 - y), (1 - x, 1 - y)]
        copies, local = [], []
        for i in range(nr):
            for k, (px, py) in enumerate(chips):
                src = rsrc[i].at[2 * px + py] if ride.sliced else rsrc[i]
                dst = rdst[i].at[me_s] if ride.sliced else rdst[i].at[me_s, c]
                copies.append(_remote(src, dst, send_sems.at[3 * i + k], recv_sems.at[3 * i + k], (px, py, c)))
            if not ride.sliced:
                local.append(pltpu.make_async_copy(rsrc[i], rdst[i].at[me_s, c], local_sems.at[i]))
        first = functools.reduce(jnp.logical_and, [pl.program_id(a) == 0 for a in range(len(grid))])
        last = functools.reduce(jnp.logical_and, [pl.program_id(a) == grid[a] - 1 for a in range(len(grid))])

        @pl.when(first)
        def _():
            for cp in copies + local:
                cp.start()

        body(*ins, *outs, *scr)

        @pl.when(last)
        def _():
            for cp in copies:
                cp.wait_recv()
            for cp in copies:
                cp.wait_send()
            for cp in local:
                cp.wait()

    hbm = pl.BlockSpec(memory_space=pltpu.HBM)
    res = _pcall(
        riding, grid=grid, in_specs=list(in_specs) + [hbm] * nr, out_specs=list(out_specs) + [hbm] * nr,
        out_shape=list(out_shape) + list(ride.dst_shapes),
        scratch_shapes=list(scratch_shapes) + [pltpu.SemaphoreType.DMA((3 * nr,)), pltpu.SemaphoreType.DMA((3 * nr,)),
                                               pltpu.SemaphoreType.DMA((nr,))], **kw)(*args, *ride.srcs)
    return res[:n_out], res[n_out:]


def _seq(n=1):
    return pltpu.CompilerParams(dimension_semantics=("arbitrary",) * n)


def _dot(a, b):
    return jnp.dot(a, b, preferred_element_type=F32)


def _dot_nt(a, b):
    return lax.dot_general(a, b, (((1,), (1,)), ((), ())), preferred_element_type=F32)


def _dot_tn(a, b):
    return lax.dot_general(a, b, (((0,), (0,)), ((), ())), preferred_element_type=F32)


def _bf(x):
    return x.astype(BF16)


def _sigmoid(x):
    return 0.5 * jnp.tanh(0.5 * x) + 0.5


def _log1p(z):
    u = 1.0 + z
    return jnp.where(u == 1.0, z, jnp.log(u) * (z / jnp.where(u == 1.0, 1.0, u - 1.0)))


def _softplus(x):
    return jnp.maximum(x, 0.0) + _log1p(jnp.exp(-jnp.abs(x)))


def _log_sigmoid(x):
    return -_softplus(-x)


def _one_minus_sq(a, log_a):
    x = 2.0 * log_a
    small = -x * (1.0 + x * (0.5 + x * (1.0 / 6.0)))
    return jnp.where(x > -0.004, small, 1.0 - a * a)


def _dsilu(x, s):
    return s * (1.0 + x * (1.0 - s))


def _rowsum(x):
    return jnp.sum(x, axis=1, keepdims=True)


def _colsum(x):
    return jnp.sum(x, axis=0, keepdims=True)


def _shift_down(win, s):
    return win if s == 0 else pltpu.roll(win, s, 0)


def _shift_up(win, s):
    return win if s == 0 else pltpu.roll(win, win.shape[0] - s, 0)


def _conv_taps(win):
    return [_shift_down(win, CONV_WIDTH - 1 - k)[HALO:] for k in range(CONV_WIDTH)]


def _conv_fwd(taps, w_ref, b_ref):
    acc = b_ref[...] + w_ref[CONV_WIDTH - 1:CONV_WIDTH, :] * taps[CONV_WIDTH - 1]
    for k in range(CONV_WIDTH - 1):
        acc = acc + w_ref[k:k + 1, :] * taps[k]
    return acc


def _split3(x):
    hi = _bf(x)
    r1 = x - hi.astype(F32)
    mid = _bf(r1)
    lo = _bf(r1 - mid.astype(F32))
    return hi, mid, lo


def _tri_dot_left(tri, x):
    hi, mid, lo = _split3(x)
    return _dot(tri, hi) + _dot(tri, mid) + _dot(tri, lo)


def _tri_dot_right(x, tri):
    hi, mid, lo = _split3(x)
    return _dot(hi, tri) + _dot(mid, tri) + _dot(lo, tri)


def _tile(n, want):
    t = min(n, want)
    assert n % t == 0
    return t


def _ln_inproj(x, g, scale, shift, w4, ride=None):
    s_len, d = x.shape
    nj, _, nsh = w4.shape
    tm = _tile(s_len, 1024)

    def body(x_ref, g_ref, sc_ref, sh_ref, w_ref, h_ref, u_ref, hs):
        @pl.when(pl.program_id(1) == 0)
        def _():
            xv = x_ref[...]
            r = lax.rsqrt(jnp.mean(xv * xv, axis=-1, keepdims=True) + EPS)
            hv = (xv * r * g_ref[...]) * (1.0 + sc_ref[...]) + sh_ref[...]
            hs[...] = _bf(hv)
            h_ref[...] = hs[...]

        u_ref[...] = _dot(hs[...], w_ref[0])

    vec = pl.BlockSpec((1, d), lambda i, j: (0, 0))
    return _pcall_ride(
        body, ride, name="ln_inproj", grid=(s_len // tm, nj),
        in_specs=[pl.BlockSpec((tm, d), lambda i, j: (i, 0)), vec, vec, vec,
                  pl.BlockSpec((1, d, nsh), lambda i, j: (j, 0, 0))],
        out_specs=[pl.BlockSpec((tm, d), lambda i, j: (i, 0)), pl.BlockSpec((tm, nsh), lambda i, j: (i, j))],
        out_shape=[jax.ShapeDtypeStruct((s_len, d), BF16), jax.ShapeDtypeStruct((s_len, nj * nsh), F32)],
        scratch_shapes=[pltpu.VMEM((tm, d), BF16)],
        compiler_params=_seq(2),
        args=(x, g, scale, shift, w4))


def _rg_gates(xc, wa_ref, ba_ref, wx_ref, bx_ref, lam_ref):
    heads, hd, _ = wa_ref.shape
    xb = _bf(xc)
    ga = jnp.concatenate([_dot(xb[:, h * hd:(h + 1) * hd], wa_ref[h]) for h in range(heads)], axis=1) + ba_ref[...]
    gx = jnp.concatenate([_dot(xb[:, h * hd:(h + 1) * hd], wx_ref[h]) for h in range(heads)], axis=1) + bx_ref[...]
    r = _sigmoid(ga)
    ig = _sigmoid(gx)
    sp = _softplus(-lam_ref[...])
    log_a = (-RG_C) * r * sp
    a = jnp.exp(log_a)
    mult = jnp.sqrt(_one_minus_sq(a, log_a))
    return r, ig, sp, log_a, a, mult


def _scan_groups(a, u, reverse):
    n, c = a.shape
    a = a.reshape(n // 8, 8, c)
    u = u.reshape(n // 8, 8, c)
    row = lax.broadcasted_iota(jnp.int32, a.shape, 1)
    for k in (1, 2, 4):
        sft = 8 - k if reverse else k
        a_sh, u_sh = pltpu.roll(a, sft, 1), pltpu.roll(u, sft, 1)
        ok = row < 8 - k if reverse else row >= k
        u = jnp.where(ok, a * u_sh + u, u)
        a = jnp.where(ok, a * a_sh, a)
    return a.reshape(n, c), u.reshape(n, c)


def _rg_fwd(u, conv_w, conv_b, wa_b, ba, wx_b, bx, lam, ride=None):
    s_len = u.shape[0]
    d = conv_w.shape[1]
    tm = _tile(s_len, 256)
    per = tm // HALO

    def body(x_ref, xp_ref, z_ref, cw_ref, cb_ref, wa_ref, ba_ref, wx_ref, bx_ref, lam_ref,
             hh_ref, y_ref, carry):
        i = pl.program_id(0)

        @pl.when(i == 0)
        def _():
            carry[...] = jnp.zeros_like(carry)

        prev = jnp.where(i == 0, 0.0, xp_ref[...])
        xc = _conv_fwd(_conv_taps(jnp.concatenate([prev, x_ref[...]], axis=0)), cw_ref, cb_ref)
        _, ig, _, _, a, mult = _rg_gates(xc, wa_ref, ba_ref, wx_ref, bx_ref, lam_ref)
        ca, cu = _scan_groups(a, mult * (ig * xc), reverse=False)
        c = carry[0:1, :]
        for j in range(per):
            blk = ca[j * 8:(j + 1) * 8] * c + cu[j * 8:(j + 1) * 8]
            hh_ref[j * 8:(j + 1) * 8, :] = blk
            c = blk[7:8]
        carry[0:1, :] = c
        z = z_ref[...]
        y_ref[0] = _bf(hh_ref[...] * (z * _sigmoid(z)))

    vec = pl.BlockSpec((1, d), lambda i: (0, 0))
    whole3 = lambda a: pl.BlockSpec(a.shape, lambda i: (0, 0, 0))
    return _pcall_ride(
        body, ride, name="rg_fwd", grid=(s_len // tm,),
        in_specs=[pl.BlockSpec((tm, d), lambda i: (i, 0)),
                  pl.BlockSpec((HALO, d), lambda i: (jnp.maximum(i * per - 1, 0), 0)),
                  pl.BlockSpec((tm, d), lambda i: (i, 1)),
                  pl.BlockSpec((CONV_WIDTH, d), lambda i: (0, 0)), vec,
                  whole3(wa_b), vec, whole3(wx_b), vec, vec],
        out_specs=[pl.BlockSpec((tm, d), lambda i: (i, 0)), pl.BlockSpec((1, tm, d), lambda i: (0, i, 0))],
        out_shape=[jax.ShapeDtypeStruct((s_len, d), F32), jax.ShapeDtypeStruct((2, s_len, d), BF16)],
        scratch_shapes=[pltpu.VMEM((8, d), F32)],
        compiler_params=_seq(),
        args=(u, u, u, conv_w, conv_b, wa_b, ba, wx_b, bx, lam))


def _ml_pre(u, conv_w, conv_b, wqkv_b, wif_b, wift_b, b_if, b_ift):
    s_len = u.shape[0]
    d = conv_w.shape[1]
    _, heads, hd, _ = wqkv_b.shape
    ng = 2 * heads
    tm = _tile(s_len, 256)
    per = tm // HALO

    def body(x_ref, xp_ref, cw_ref, cb_ref, w_ref, wif_ref, wift_ref, bif_ref, bift_ref,
             qkv_ref, gt_ref, gtt_ref, bc_ref, bct_ref):
        i = pl.program_id(0)
        prev = jnp.where(i == 0, 0.0, xp_ref[...])
        xm = x_ref[...]
        pre = _conv_fwd(_conv_taps(jnp.concatenate([prev, xm], axis=0)), cw_ref, cb_ref)
        xcb = _bf(pre * _sigmoid(pre))
        xmb = _bf(xm)
        for h in range(heads):
            hs = slice(h * hd, (h + 1) * hd)
            qkv_ref[0, :, hs] = _bf(_dot(xcb[:, hs], w_ref[0, h]))
            qkv_ref[1, :, hs] = _bf(_dot(xcb[:, hs], w_ref[1, h]))
            qkv_ref[2, :, hs] = _bf(_dot(xmb[:, hs], w_ref[2, h]))
        qb, kb, vb = qkv_ref[0], qkv_ref[1], qkv_ref[2]
        gt = (_dot(qb, wif_ref[0:d, :]) + _dot(kb, wif_ref[d:2 * d, :]) + _dot(vb, wif_ref[2 * d:3 * d, :])
              + bif_ref[...])
        gtt = (_dot_nt(wift_ref[:, 0:d], qb) + _dot_nt(wift_ref[:, d:2 * d], kb)
               + _dot_nt(wift_ref[:, 2 * d:3 * d], vb) + bift_ref[...])
        gt_ref[...] = gt
        gtt_ref[...] = gtt
        r = lax.broadcasted_iota(jnp.int32, (tm, tm), 0)
        c = lax.broadcasted_iota(jnp.int32, (tm, tm), 1)
        same = (r // ML_CHUNK) == (c // ML_CHUNK)
        bc_ref[...] = _tri_dot_left(((r >= c) & same).astype(BF16), _log_sigmoid(gt))
        bct_ref[...] = _tri_dot_right(_log_sigmoid(gtt), ((r <= c) & same).astype(BF16))

    vec = pl.BlockSpec((1, d), lambda i: (0, 0))
    whole2 = lambda a: pl.BlockSpec(a.shape, lambda i: (0, 0))
    col = pl.BlockSpec((tm, ng), lambda i: (i, 0))
    row = pl.BlockSpec((ng, tm), lambda i: (0, i))
    return _pcall(
        body, name="ml_pre", grid=(s_len // tm,),
        in_specs=[pl.BlockSpec((tm, d), lambda i: (i, 2)),
                  pl.BlockSpec((HALO, d), lambda i: (jnp.maximum(i * per - 1, 0), 2)),
                  pl.BlockSpec((CONV_WIDTH, d), lambda i: (0, 0)), vec,
                  pl.BlockSpec(wqkv_b.shape, lambda i: (0, 0, 0, 0)), whole2(wif_b), whole2(wift_b), whole2(b_if),
                  whole2(b_ift)],
        out_specs=[pl.BlockSpec((3, tm, d), lambda i: (0, i, 0)), col, row, col, row],
        out_shape=[jax.ShapeDtypeStruct((3, s_len, d), BF16), jax.ShapeDtypeStruct((s_len, ng), F32),
                   jax.ShapeDtypeStruct((ng, s_len), F32), jax.ShapeDtypeStruct((s_len, ng), F32),
                   jax.ShapeDtypeStruct((ng, s_len), F32)],
        compiler_params=_seq(),
    )(u, u, conv_w, conv_b, wqkv_b, wif_b, wift_b, b_if, b_ift)


def _chunk_gates(gt, gtt, bc, bct, h, heads):
    li_c = gt[:, h:h + 1]
    li_r = gtt[h:h + 1, :]
    gf_c = gt[:, heads + h:heads + h + 1]
    b_c = bc[:, heads + h:heads + h + 1]
    b_r = bct[heads + h:heads + h + 1, :]
    return li_c, li_r, gf_c, b_c, b_r


def _chunk_weights(li_c, li_r, b_c, b_r, m_prev, causal):
    lc = b_c.shape[0]
    b_last = b_c[lc - 1:lc, :]
    dmat = jnp.where(causal, b_c - b_r + li_r, -jnp.inf)
    m_inter = b_c + m_prev
    m_t = jnp.maximum(m_inter, jnp.max(dmat, axis=1, keepdims=True))
    w_intra = jnp.exp(dmat - m_t)
    w_inter = jnp.exp(m_inter - m_t)
    g_c = b_last - b_c + li_c
    m_new = jnp.maximum(b_last + m_prev, jnp.max(g_c, axis=0, keepdims=True))
    w_state = jnp.exp(g_c - m_new)
    decay = jnp.exp(b_last + m_prev - m_new)
    return m_t, w_intra, w_inter, m_new, w_state, decay


def _tri_masks(lc):
    r = lax.broadcasted_iota(jnp.int32, (lc, lc), 0)
    c = lax.broadcasted_iota(jnp.int32, (lc, lc), 1)
    causal = r >= c
    return causal, causal.astype(BF16), (r <= c).astype(BF16)


def _mlstm_fwd(qkv, gates, u, ml_g, ycat, ride=None):
    _, s_len, d = qkv.shape
    ng = gates[0].shape[1]
    heads = ng // 2
    hd = d // heads
    lc = ML_CHUNK
    nc = s_len // lc
    kscale = hd ** -0.5

    def body(qkv_ref, gt_ref, gtt_ref, bc_ref, bct_ref, o_ref, z_ref, g_ref, _, cell_ref, y_ref, cst_ref, nst_ref,
             mst_ref, cs, ns, ms):
        @pl.when(pl.program_id(0) == 0)
        def _():
            cs[...] = jnp.zeros_like(cs)
            ns[...] = jnp.zeros_like(ns)
            ms[...] = jnp.zeros_like(ms)

        causal = _tri_masks(lc)[0]
        gtv, gttv, bcv, bctv = gt_ref[...], gtt_ref[...], bc_ref[...], bct_ref[...]
        for h in range(heads):
            hs = slice(h * hd, (h + 1) * hd)
            li_c, li_r, _, b_c, b_r = _chunk_gates(gtv, gttv, bcv, bctv, h, heads)
            m_prev = ms[h][:, 0:1]
            m_t, w_intra, w_inter, m_new, w_state, decay = _chunk_weights(li_c, li_r, b_c, b_r, m_prev, causal)
            qb = qkv_ref[0, :, hs]
            ks = qkv_ref[1, :, hs].astype(F32) * kscale
            kb = _bf(ks)
            vb = qkv_ref[2, :, hs]
            c_old = cs[h]
            n_old = ns[h]
            cst_ref[0, h] = _bf(c_old)
            nst_ref[0, h] = n_old
            mst_ref[0, h] = ms[h]
            s = _dot_nt(qb, kb) * w_intra
            num = _dot(_bf(s), vb) + w_inter * _dot(qb, _bf(c_old))
            den = _rowsum(s) + w_inter * _rowsum(qb.astype(F32) * n_old)
            cell = num / jnp.maximum(jnp.abs(den), jnp.exp(-m_t))
            kw = ks * w_state
            cs[h] = decay * c_old + _dot_tn(_bf(kw), vb)
            ns[h] = decay * n_old + _colsum(kw)
            ms[h] = jnp.broadcast_to(m_new, ms[h].shape)
            cell_ref[:, hs] = cell
            hm = _sigmoid(o_ref[:, hs]) * cell
            hn = hm * lax.rsqrt(jnp.mean(hm * hm, axis=-1, keepdims=True) + EPS)
            z = z_ref[:, hs]
            y_ref[0, :, hs] = _bf((hn * g_ref[:, hs]) * (z * _sigmoid(z)))

    row = pl.BlockSpec((lc, d), lambda c: (c, 0))
    gcol = pl.BlockSpec((lc, ng), lambda c: (c, 0))
    grow = pl.BlockSpec((ng, lc), lambda c: (0, c))
    return _pcall_ride(
        body, ride, name="mlstm_fwd", grid=(nc,),
        in_specs=[pl.BlockSpec((3, lc, d), lambda c: (0, c, 0)), gcol, grow, gcol, grow,
                  pl.BlockSpec((lc, d), lambda c: (c, 3)), pl.BlockSpec((lc, d), lambda c: (c, 4)),
                  pl.BlockSpec((1, d), lambda c: (0, 0)), pl.BlockSpec(memory_space=pl.ANY)],
        out_specs=[row, pl.BlockSpec((1, lc, d), lambda c: (1, c, 0)),
                   pl.BlockSpec((1, heads, hd, hd), lambda c: (c, 0, 0, 0)),
                   pl.BlockSpec((1, heads, 1, hd), lambda c: (c, 0, 0, 0)),
                   pl.BlockSpec((1, heads, 1, 128), lambda c: (c, 0, 0, 0))],
        out_shape=[jax.ShapeDtypeStruct((s_len, d), F32), jax.ShapeDtypeStruct(ycat.shape, BF16),
                   jax.ShapeDtypeStruct((nc, heads, hd, hd), BF16),
                   jax.ShapeDtypeStruct((nc, heads, 1, hd), F32),
                   jax.ShapeDtypeStruct((nc, heads, 1, 128), F32)],
        scratch_shapes=[pltpu.VMEM((heads, hd, hd), F32), pltpu.VMEM((heads, 1, hd), F32),
                        pltpu.VMEM((heads, 1, 128), F32)],
        input_output_aliases={8: 1},
        compiler_params=_seq(),
        args=(qkv, *gates, u, u, ml_g, ycat))


def _out_proj(ycat, w_out_b, x, gate):
    s_len, d = x.shape
    tm = _tile(s_len, 1024)

    def body(a_ref, w_ref, x_ref, g_ref, y_ref, xn_ref):
        y = _dot(a_ref[0], w_ref[0:d, :]) + _dot(a_ref[1], w_ref[d:2 * d, :])
        y_ref[...] = y
        xn_ref[...] = x_ref[...] + g_ref[...] * y

    row = pl.BlockSpec((tm, d), lambda i: (i, 0))
    return _pcall(
        body, name="out_proj", grid=(s_len // tm,),
        in_specs=[pl.BlockSpec((2, tm, d), lambda i: (0, i, 0)), pl.BlockSpec((2 * d, d), lambda i: (0, 0)), row,
                  pl.BlockSpec((1, d), lambda i: (0, 0))],
        out_specs=[row, row],
        out_shape=[jax.ShapeDtypeStruct((s_len, d), F32)] * 2,
        compiler_params=_seq(),
    )(ycat, w_out_b, x, gate)


def _final_loss(x, g, target):
    s_len, d = x.shape
    tm = _tile(s_len, 256)

    def body(x_ref, g_ref, t_ref, dx_ref, dg_ref, loss_ref):
        @pl.when(pl.program_id(0) == 0)
        def _():
            dg_ref[...] = jnp.zeros_like(dg_ref)
            loss_ref[...] = jnp.zeros_like(loss_ref)

        xv = x_ref[...]
        r = lax.rsqrt(jnp.mean(xv * xv, axis=-1, keepdims=True) + EPS)
        xn = xv * r
        err = xn * g_ref[...] - t_ref[...]
        loss_ref[...] += 0.5 * jnp.sum(jnp.mean(err * err, axis=-1, keepdims=True))
        dout = err * (1.0 / d)
        dg_ref[...] += _colsum(dout * xn)
        dxn = dout * g_ref[...]
        dx_ref[...] = r * (dxn - xn * jnp.mean(dxn * xn, axis=-1, keepdims=True))

    row = pl.BlockSpec((tm, d), lambda i: (i, 0))
    vec = pl.BlockSpec((1, d), lambda i: (0, 0))
    return _pcall(
        body, name="final_loss", grid=(s_len // tm,),
        in_specs=[row, vec, row],
        out_specs=[row, vec, pl.BlockSpec((1, 128), lambda i: (0, 0))],
        out_shape=[jax.ShapeDtypeStruct((s_len, d), F32), jax.ShapeDtypeStruct((1, d), F32),
                   jax.ShapeDtypeStruct((1, 128), F32)],
        compiler_params=_seq(),
    )(x, g, target)


def _out_bwd(dxn, y, gate, w_out_b):
    s_len, d = dxn.shape
    tm = _tile(s_len, 1024)

    def body(dx_ref, y_ref, g_ref, w_ref, dg_ref, dy_ref, dc_ref):
        @pl.when(pl.program_id(0) == 0)
        def _():
            dg_ref[...] = jnp.zeros_like(dg_ref)

        dx = dx_ref[...]
        dg_ref[...] += _colsum(dx * y_ref[...])
        dy = _bf(g_ref[...] * dx)
        dy_ref[...] = dy
        dc_ref[0] = _dot_nt(dy, w_ref[0:d, :])
        dc_ref[1] = _dot_nt(dy, w_ref[d:2 * d, :])

    row = pl.BlockSpec((tm, d), lambda i: (i, 0))
    vec = pl.BlockSpec((1, d), lambda i: (0, 0))
    return _pcall(
        body, name="out_bwd", grid=(s_len // tm,),
        in_specs=[row, row, vec, pl.BlockSpec((2 * d, d), lambda i: (0, 0))],
        out_specs=[vec, row, pl.BlockSpec((2, tm, d), lambda i: (0, i, 0))],
        out_shape=[jax.ShapeDtypeStruct((1, d), F32), jax.ShapeDtypeStruct((s_len, d), BF16),
                   jax.ShapeDtypeStruct((2, s_len, d), F32)],
        compiler_params=_seq(),
    )(dxn, y, gate, w_out_b)


def _grad_matmul(a3, b3, nblk, a_idx, b_idx, out_shape, out_block, out_idx, layer, stack):
    _, s_len, m = a3.shape
    n = b3.shape[2]
    tk = _tile(s_len, 2048)
    first = isinstance(stack, int)

    def body(a_ref, b_ref, *rest):
        o_ref = rest[-1]

        @pl.when(pl.program_id(1) == 0)
        def _():
            o_ref[...] = jnp.zeros_like(o_ref)

        o_ref[...] += _dot_tn(a_ref[0], b_ref[0])

    in_specs = [pl.BlockSpec((1, tk, m), lambda p, t: (a_idx(p), t, 0)),
                pl.BlockSpec((1, tk, n), lambda p, t: (b_idx(p), t, 0))]
    return _pcall(
        body, name="grad_matmul", grid=(nblk, s_len // tk),
        in_specs=in_specs if first else in_specs + [pl.BlockSpec(memory_space=pl.ANY)],
        out_specs=pl.BlockSpec((None,) + out_block, lambda p, t: (layer,) + out_idx(p)),
        out_shape=jax.ShapeDtypeStruct(((stack,) if first else stack.shape[:1]) + out_shape, F32),
        input_output_aliases={} if first else {2: 0},
        compiler_params=_seq(2),
    )(*((a3, b3) if first else (a3, b3, stack)))


DU_PLANE = (2, 3, 4, 0, 1)


def _mlstm_bwd(qkv, gates, cst, nst, mst, cell, u, ml_g, d_ycat, wif_b, ride=None):
    _, s_len, d = qkv.shape
    ng = gates[0].shape[1]
    heads = ng // 2
    hd = d // heads
    lc = ML_CHUNK
    nc = s_len // lc
    kscale = hd ** -0.5

    def body(qkv_ref, gt_ref, gtt_ref, bc_ref, bct_ref, cst_ref, nst_ref, mst_ref, cell_ref, o_ref, z_ref, g_ref, dy_ref,
             wif_ref, dqkv_ref, dgt_ref, dbif_ref, du_ref, dg_ref, dcs, dns, dqs, dks, dvs):
        @pl.when(pl.program_id(0) == 0)
        def _():
            dbif_ref[...] = jnp.zeros_like(dbif_ref)
            dcs[...] = jnp.zeros_like(dcs)
            dns[...] = jnp.zeros_like(dns)
            dg_ref[...] = jnp.zeros_like(dg_ref)

        causal, tril, triu = _tri_masks(lc)
        tril_strict = (tril.astype(F32) - (tril * triu).astype(F32)).astype(BF16)
        gtv, gttv, bcv, bctv = gt_ref[...], gtt_ref[...], bc_ref[...], bct_ref[...]
        lane = lax.broadcasted_iota(jnp.int32, (lc, ng), 1)
        dli_all = jnp.zeros((lc, ng), F32)
        from_later = jnp.zeros((lc, ng), F32)
        from_earlier = jnp.zeros((lc, ng), F32)
        across_all = jnp.zeros((1, ng), F32)
        for h in range(heads):
            hs = slice(h * hd, (h + 1) * hd)
            li_c, li_r, gf_c, b_c, b_r = _chunk_gates(gtv, gttv, bcv, bctv, h, heads)
            m_prev = mst_ref[0, h][:, 0:1]
            m_t, w_intra, w_inter, _, w_state, decay = _chunk_weights(li_c, li_r, b_c, b_r, m_prev, causal)
            qb = qkv_ref[0, :, hs]
            qf = qb.astype(F32)
            ks = qkv_ref[1, :, hs].astype(F32) * kscale
            kb = _bf(ks)
            vb = qkv_ref[2, :, hs]
            c_b = cst_ref[0, h]
            n_old = nst_ref[0, h]
            s = _dot_nt(qb, kb) * w_intra
            den = _rowsum(s) + w_inter * _rowsum(qf * n_old)
            floor = jnp.exp(-m_t)
            dstab = jnp.maximum(jnp.abs(den), floor)
            cell = cell_ref[:, hs]
            o = o_ref[:, hs]
            so = _sigmoid(o)
            hm = so * cell
            rinv = lax.rsqrt(jnp.mean(hm * hm, axis=-1, keepdims=True) + EPS)
            hn = hm * rinv
            z = z_ref[:, hs]
            sgz = _sigmoid(z)
            sz = z * sgz
            gh = g_ref[:, hs]
            dy = dy_ref[0, :, hs]
            du_ref[1, :, hs] = _bf(dy * (hn * gh) * _dsilu(z, sgz))
            dg_ref[:, hs] += _colsum(dy * hn * sz)
            dhn = dy * gh * sz
            dhm = rinv * (dhn - hn * jnp.mean(dhn * hn, axis=-1, keepdims=True))
            du_ref[0, :, hs] = _bf(dhm * cell * so * (1.0 - so))
            dcell = dhm * so
            dnum = dcell / dstab
            dnb = _bf(dnum)
            dden = -_rowsum(dcell * cell) / dstab * jnp.where(jnp.abs(den) > floor, jnp.where(den > 0.0, 1.0, -1.0), 0.0)
            dst = _dot_nt(dnb, vb) + dden
            dsdb = _bf(dst * w_intra)
            dc_out = dcs[h]
            dn_out = dns[h]
            dcb = _bf(dc_out)
            dq_inter = w_inter * (_dot_nt(dnb, c_b) + dden * n_old)
            dk_inter = w_state * (_dot_nt(vb, dcb) + dn_out)
            dq = _dot(dsdb, kb) + dq_inter
            dk = _dot_tn(dsdb, qb) + dk_inter
            dv = _dot_tn(_bf(s), dnb) + _dot(_bf(ks * w_state), dcb)
            wq = w_inter * qf
            dcs[h] = decay * dc_out + _dot_tn(_bf(wq), dnb)
            dns[h] = decay * dn_out + _colsum(wq * dden)
            pmat = dst * s
            p_rows = _rowsum(pmat)
            p_cols = _rowsum(pmat.T)
            q_in = _rowsum(qf * dq_inter)
            k_in = _rowsum(ks * dk_inter)
            across = decay * (jnp.sum(dc_out * c_b.astype(F32), keepdims=True) + jnp.sum(dn_out * n_old, keepdims=True))
            dli_all = dli_all + jnp.where(lane == h, p_cols + k_in, 0.0)
            from_later = from_later + jnp.where(lane == heads + h, p_rows - p_cols + q_in, 0.0)
            from_earlier = from_earlier + jnp.where(lane == heads + h, k_in, 0.0)
            across_all = across_all + jnp.where(lane[0:1] == heads + h, across, 0.0)
            dqs[:, hs] = dq
            dks[:, hs] = dk * kscale
            dvs[:, hs] = dv
        dlf = _tri_dot_left(triu, from_later) + _tri_dot_left(tril_strict, from_earlier) + across_all
        dgt = dli_all + dlf * _sigmoid(-gtv)
        dgt_ref[...] = dgt
        dbif_ref[...] += _colsum(dgt)
        dgb = _bf(dgt)
        dqkv_ref[0] = _bf(dqs[...] + _dot_nt(dgb, wif_ref[0:d, :]))
        dqkv_ref[1] = _bf(dks[...] + _dot_nt(dgb, wif_ref[d:2 * d, :]))
        dqkv_ref[2] = _bf(dvs[...] + _dot_nt(dgb, wif_ref[2 * d:3 * d, :]))

    rev = lambda c: nc - 1 - c
    row = pl.BlockSpec((lc, d), lambda c: (rev(c), 0))
    gcol = pl.BlockSpec((lc, ng), lambda c: (rev(c), 0))
    grow = pl.BlockSpec((ng, lc), lambda c: (0, rev(c)))
    return _pcall_ride(
        body, ride, name="mlstm_bwd", grid=(nc,),
        in_specs=[pl.BlockSpec((3, lc, d), lambda c: (0, rev(c), 0)), gcol, grow, gcol, grow,
                  pl.BlockSpec((1, heads, hd, hd), lambda c: (rev(c), 0, 0, 0)),
                  pl.BlockSpec((1, heads, 1, hd), lambda c: (rev(c), 0, 0, 0)),
                  pl.BlockSpec((1, heads, 1, 128), lambda c: (rev(c), 0, 0, 0)),
                  row, pl.BlockSpec((lc, d), lambda c: (rev(c), 3)), pl.BlockSpec((lc, d), lambda c: (rev(c), 4)),
                  pl.BlockSpec((1, d), lambda c: (0, 0)), pl.BlockSpec((1, lc, d), lambda c: (1, rev(c), 0)),
                  pl.BlockSpec((3 * d, ng), lambda c: (0, 0))],
        out_specs=[pl.BlockSpec((3, lc, d), lambda c: (0, rev(c), 0)), pl.BlockSpec((lc, ng), lambda c: (rev(c), 0)),
                   pl.BlockSpec((1, ng), lambda c: (0, 0)), pl.BlockSpec((2, lc, d), lambda c: (0, rev(c), 0)),
                   pl.BlockSpec((1, d), lambda c: (0, 0))],
        out_shape=[jax.ShapeDtypeStruct((3, s_len, d), BF16), jax.ShapeDtypeStruct((s_len, ng), F32),
                   jax.ShapeDtypeStruct((1, ng), F32), jax.ShapeDtypeStruct((5, s_len, d), BF16),
                   jax.ShapeDtypeStruct((1, d), F32)],
        scratch_shapes=[pltpu.VMEM((heads, hd, hd), F32), pltpu.VMEM((heads, 1, hd), F32)]
        + [pltpu.VMEM((lc, d), F32)] * 3,
        compiler_params=_seq(),
        args=(qkv, *gates, cst, nst, mst, cell, u, u, ml_g, d_ycat, wif_b))


def _conv_bwd_tile(dp, later, taps, cw_ref, gw_ref, gb_ref):
    tm = dp.shape[0]
    dwin = jnp.concatenate([dp, later[...]], axis=0)
    later[...] = dp[0:HALO]
    acc = cw_ref[CONV_WIDTH - 1:CONV_WIDTH, :] * dp
    for k in range(CONV_WIDTH):
        if k < CONV_WIDTH - 1:
            acc = acc + cw_ref[k:k + 1, :] * _shift_up(dwin, CONV_WIDTH - 1 - k)[0:tm]
        gw_ref[k:k + 1, :] += _colsum(dp * taps[k])
    gb_ref[...] += _colsum(dp)
    return acc


def _ml_pre_bwd(dqkv, u, conv_w, conv_b, wqkv_b, du):
    s_len = u.shape[0]
    d = conv_w.shape[1]
    _, heads, hd, _ = wqkv_b.shape
    tm = _tile(s_len, 256)
    per = tm // HALO
    nt = s_len // tm

    def body(dqkv_ref, x_ref, xp_ref, cw_ref, cb_ref, w_ref, _, dx_ref, gw_ref, gcw_ref, gcb_ref, later, dps, dxs):
        i = pl.program_id(0)

        @pl.when(i == 0)
        def _():
            gw_ref[...] = jnp.zeros_like(gw_ref)
            gcw_ref[...] = jnp.zeros_like(gcw_ref)
            gcb_ref[...] = jnp.zeros_like(gcb_ref)
            later[...] = jnp.zeros_like(later)

        prev = jnp.where(i == nt - 1, 0.0, xp_ref[...])
        xm = x_ref[...]
        taps = _conv_taps(jnp.concatenate([prev, xm], axis=0))
        pre = _conv_fwd(taps, cw_ref, cb_ref)
        sg = _sigmoid(pre)
        xcb = _bf(pre * sg)
        xmb = _bf(xm)
        for h in range(heads):
            hs = slice(h * hd, (h + 1) * hd)
            dqh, dkh, dvh = dqkv_ref[0, :, hs], dqkv_ref[1, :, hs], dqkv_ref[2, :, hs]
            dxc = _dot_nt(dqh, w_ref[0, h]) + _dot_nt(dkh, w_ref[1, h])
            dps[:, hs] = dxc * _dsilu(pre[:, hs], sg[:, hs])
            dxs[:, hs] = _dot_nt(dvh, w_ref[2, h])
            gw_ref[0, h] += _dot_tn(xcb[:, hs], dqh)
            gw_ref[1, h] += _dot_tn(xcb[:, hs], dkh)
            gw_ref[2, h] += _dot_tn(xmb[:, hs], dvh)
        dx_ref[0] = _bf(_conv_bwd_tile(dps[...], later, taps, cw_ref, gcw_ref, gcb_ref) + dxs[...])

    rev = lambda i: nt - 1 - i
    vec = pl.BlockSpec((1, d), lambda i: (0, 0))
    cwb = pl.BlockSpec((CONV_WIDTH, d), lambda i: (0, 0))
    whole4 = pl.BlockSpec(wqkv_b.shape, lambda i: (0, 0, 0, 0))
    return _pcall(
        body, name="ml_pre_bwd", grid=(nt,),
        in_specs=[pl.BlockSpec((3, tm, d), lambda i: (0, rev(i), 0)), pl.BlockSpec((tm, d), lambda i: (rev(i), 2)),
                  pl.BlockSpec((HALO, d), lambda i: (jnp.maximum(rev(i) * per - 1, 0), 2)),
                  cwb, vec, whole4, pl.BlockSpec(memory_space=pl.ANY)],
        out_specs=[pl.BlockSpec((1, tm, d), lambda i: (DU_PLANE[2], rev(i), 0)), whole4, cwb, vec],
        out_shape=[jax.ShapeDtypeStruct(du.shape, BF16), jax.ShapeDtypeStruct(wqkv_b.shape, F32),
                   jax.ShapeDtypeStruct((CONV_WIDTH, d), F32), jax.ShapeDtypeStruct((1, d), F32)],
        scratch_shapes=[pltpu.VMEM((HALO, d), F32), pltpu.VMEM((tm, d), F32), pltpu.VMEM((tm, d), F32)],
        input_output_aliases={6: 0},
        compiler_params=_seq(),
    )(dqkv, u, u, conv_w, conv_b, wqkv_b, du)


def _rg_bwd(d_ycat, u, hh, conv_w, conv_b, wa_b, ba, wx_b, bx, lam, du):
    s_len = u.shape[0]
    d = conv_w.shape[1]
    heads, hd, _ = wa_b.shape
    tm = _tile(s_len, 256)
    per = tm // HALO
    nt = s_len // tm

    def body(dy_ref, x_ref, xp_ref, z_ref, hh_ref, hp_ref, cw_ref, cb_ref, wa_ref, ba_ref, wx_ref, bx_ref, lam_ref, _,
             du_ref, gwa_ref, gwx_ref, gba_ref, gbx_ref, glam_ref, gcw_ref, gcb_ref, carry, gbuf, later, dxcs):
        i = pl.program_id(0)
        first = i == nt - 1

        @pl.when(i == 0)
        def _():
            carry[...] = jnp.zeros_like(carry)
            later[...] = jnp.zeros_like(later)
            gwa_ref[...] = jnp.zeros_like(gwa_ref)
            gwx_ref[...] = jnp.zeros_like(gwx_ref)
            gba_ref[...] = jnp.zeros_like(gba_ref)
            gbx_ref[...] = jnp.zeros_like(gbx_ref)
            glam_ref[...] = jnp.zeros_like(glam_ref)
            gcw_ref[...] = jnp.zeros_like(gcw_ref)
            gcb_ref[...] = jnp.zeros_like(gcb_ref)

        prev = jnp.where(first, 0.0, xp_ref[...])
        taps = _conv_taps(jnp.concatenate([prev, x_ref[...]], axis=0))
        xc = _conv_fwd(taps, cw_ref, cb_ref)
        r, ig, sp, log_a, a, mult = _rg_gates(xc, wa_ref, ba_ref, wx_ref, bx_ref, lam_ref)
        z = z_ref[...]
        sgz = _sigmoid(z)
        dy = dy_ref[0]
        hh_v = hh_ref[...]
        du_ref[1] = _bf(dy * hh_v * _dsilu(z, sgz))
        dhh = dy * (z * sgz)
        rows = lax.broadcasted_iota(jnp.int32, a.shape, 0)
        coef = jnp.where(rows == tm - 1, carry[1:2, :], _shift_up(a, 1))
        ca, cu = _scan_groups(coef, dhh, reverse=True)
        c = carry[0:1, :]
        for j in range(per - 1, -1, -1):
            blk = ca[j * 8:(j + 1) * 8] * c + cu[j * 8:(j + 1) * 8]
            gbuf[j * 8:(j + 1) * 8, :] = blk
            c = blk[0:1]
        carry[0:1, :] = c
        carry[1:2, :] = a[0:1]
        g = gbuf[...]
        hprev_tile = jnp.where(first, 0.0, hp_ref[...])
        hprev = _shift_down(jnp.concatenate([hprev_tile, hh_v], axis=0), 1)[HALO:]
        da = g * hprev
        gx_ = g * xc
        d_mult = gx_ * ig
        d_ig = gx_ * mult
        dxc = g * mult * ig
        dlog_a = da * a - d_mult * (a * a / mult)
        d_r = dlog_a * ((-RG_C) * sp)
        glam_ref[...] += _colsum(dlog_a * ((-RG_C) * r)) * (-_sigmoid(-lam_ref[...]))
        d_ga = d_r * r * (1.0 - r)
        d_gx = d_ig * ig * (1.0 - ig)
        gba_ref[...] += _colsum(d_ga)
        gbx_ref[...] += _colsum(d_gx)
        xb = _bf(xc)
        dgab = _bf(d_ga)
        dgxb = _bf(d_gx)
        for h in range(heads):
            hs = slice(h * hd, (h + 1) * hd)
            dxcs[:, hs] = dxc[:, hs] + _dot_nt(dgab[:, hs], wa_ref[h]) + _dot_nt(dgxb[:, hs], wx_ref[h])
            gwa_ref[h] += _dot_tn(xb[:, hs], dgab[:, hs])
            gwx_ref[h] += _dot_tn(xb[:, hs], dgxb[:, hs])
        du_ref[0] = _bf(_conv_bwd_tile(dxcs[...], later, taps, cw_ref, gcw_ref, gcb_ref))

    assert DU_PLANE[0] % 2 == 0 and DU_PLANE[1] == DU_PLANE[0] + 1
    rev = lambda i: nt - 1 - i
    row = pl.BlockSpec((tm, d), lambda i: (rev(i), 0))
    halo_prev = lambda col: pl.BlockSpec((HALO, d), lambda i: (jnp.maximum(rev(i) * per - 1, 0), col))
    vec = pl.BlockSpec((1, d), lambda i: (0, 0))
    cwb = pl.BlockSpec((CONV_WIDTH, d), lambda i: (0, 0))
    whole3 = lambda a: pl.BlockSpec(a.shape, lambda i: (0, 0, 0))
    return _pcall(
        body, name="rg_bwd", grid=(nt,),
        in_specs=[pl.BlockSpec((1, tm, d), lambda i: (0, rev(i), 0)), row, halo_prev(0),
                  pl.BlockSpec((tm, d), lambda i: (rev(i), 1)), row, halo_prev(0),
                  cwb, vec, whole3(wa_b), vec, whole3(wx_b), vec, vec, pl.BlockSpec(memory_space=pl.ANY)],
        out_specs=[pl.BlockSpec((2, tm, d), lambda i: (DU_PLANE[0] // 2, rev(i), 0)), whole3(wa_b), whole3(wa_b),
                   vec, vec, vec, cwb, vec],
        out_shape=[jax.ShapeDtypeStruct(du.shape, BF16), jax.ShapeDtypeStruct(wa_b.shape, F32),
                   jax.ShapeDtypeStruct(wa_b.shape, F32)] + [jax.ShapeDtypeStruct((1, d), F32)] * 3
        + [jax.ShapeDtypeStruct((CONV_WIDTH, d), F32), jax.ShapeDtypeStruct((1, d), F32)],
        scratch_shapes=[pltpu.VMEM((8, d), F32), pltpu.VMEM((tm, d), F32), pltpu.VMEM((HALO, d), F32),
                        pltpu.VMEM((tm, d), F32)],
        input_output_aliases={13: 0},
        compiler_params=_seq(),
    )(d_ycat, u, u, u, hh, hh, conv_w, conv_b, wa_b, ba, wx_b, bx, lam, du)


def _in_bwd(du, w4, x, dxn, g, scale, ride=None):
    s_len, d = x.shape
    tm = _tile(s_len, 512)
    nsh_chips, _, nsh = w4.shape
    npc = du.shape[0]
    ck = d // 4
    assert nsh % ck == 0 and npc * d == nsh_chips * nsh

    def body(du_ref, w_ref, x_ref, dxn_ref, g_ref, sc_ref, dx_ref, dsh_ref, dsc_ref, dg_ref):
        @pl.when(pl.program_id(0) == 0)
        def _():
            dsh_ref[...] = jnp.zeros_like(dsh_ref)
            dsc_ref[...] = jnp.zeros_like(dsc_ref)
            dg_ref[...] = jnp.zeros_like(dg_ref)

        dh = None
        for q in range(npc * d // ck):
            col = q * ck
            p, pc = col // d, col % d
            s, sc = col // nsh, col % nsh
            t = _dot_nt(du_ref[DU_PLANE[p], :, pc:pc + ck], w_ref[s, :, sc:sc + ck])
            dh = t if dh is None else dh + t
        xv = x_ref[...]
        r = lax.rsqrt(jnp.mean(xv * xv, axis=-1, keepdims=True) + EPS)
        xn = xv * r
        gv = g_ref[...]
        onesc = 1.0 + sc_ref[...]
        dsh_ref[...] += _colsum(dh)
        dsc_ref[...] += _colsum(dh * (xn * gv))
        dg_ref[...] += _colsum(dh * xn * onesc)
        dxh = dh * (gv * onesc)
        dx_ref[...] = dxn_ref[...] + r * (dxh - xn * jnp.mean(dxh * xn, axis=-1, keepdims=True))

    row = pl.BlockSpec((tm, d), lambda i: (i, 0))
    vec = pl.BlockSpec((1, d), lambda i: (0, 0))
    return _pcall_ride(
        body, ride, name="in_bwd", grid=(s_len // tm,),
        in_specs=[pl.BlockSpec((npc, tm, d), lambda i: (0, i, 0)), pl.BlockSpec(w4.shape, lambda i: (0, 0, 0)), row, row,
                  vec, vec],
        out_specs=[row, vec, vec, vec],
        out_shape=[jax.ShapeDtypeStruct((s_len, d), F32)] + [jax.ShapeDtypeStruct((1, d), F32)] * 3,
        compiler_params=_seq(),
        args=(du, w4, x, dxn, g, scale))


def _layer_fwd(x, p, rides=None, late=None):
    rides = rides or {}
    (h_b, u), got = _ln_inproj(x, p["norm_g"], p["scale"], p["shift"], p["w4"], rides.get("ln_inproj"))
    if late is not None:
        p = {**p, **late(got)}
    (hh, ycat), got_a = _rg_fwd(u, p["rg_conv_w"], p["rg_conv_b"], p["rg_wa_b"], p["rg_ba"], p["rg_wx_b"], p["rg_bx"],
                                p["rg_lam"], rides.get("rg_fwd"))
    qkv, *gates = _ml_pre(u, p["ml_conv_w"], p["ml_conv_b"], p["wqkv_b"], p["wif_b"], p["wift_b"], p["b_if"],
                          p["b_ift"])
    (cell, ycat, cst, nst, mst), got_b = _mlstm_fwd(qkv, gates, u, p["ml_g"], ycat, rides.get("mlstm_fwd"))
    y, x_new = _out_proj(ycat, p["w_out_b"], x, p["gate"])
    saved = dict(x=x, h_b=h_b, u=u, hh=hh, qkv=qkv, gates=gates, cell=cell, ycat=ycat, cst=cst, nst=nst, mst=mst, y=y)
    return x_new, saved, p, dict(rg_fwd=got_a, mlstm_fwd=got_b)


def _layer_bwd(dxn, p, s, rides=None):
    rides = rides or {}
    u = s["u"]
    d = dxn.shape[1]
    d_gate, dy_b, d_ycat = _out_bwd(dxn, s["y"], p["gate"], p["w_out_b"])
    g_w_out = _grad_matmul(s["ycat"], dy_b[None], 2, lambda b: b, lambda b: 0, (2 * d, d), (d, d), lambda b: (b, 0),
                           0, 1)
    (dqkv, dgt, g_b_if, du, g_ml_g), got = _mlstm_bwd(s["qkv"], s["gates"], s["cst"], s["nst"], s["mst"], s["cell"], u,
                                                      p["ml_g"], d_ycat, p["wif_b"], rides.get("mlstm_bwd"))
    ng = dgt.shape[1]
    g_w_if = _grad_matmul(s["qkv"], _bf(dgt)[None], 3, lambda b: b, lambda b: 0, (3 * d, ng), (d, ng),
                          lambda b: (b, 0), 0, 1)[0]
    du, g_wqkv, g_ml_cw, g_ml_cb = _ml_pre_bwd(dqkv, u, p["ml_conv_w"], p["ml_conv_b"], p["wqkv_b"], du)
    du, g_wa, g_wx, g_ba, g_bx, g_lam, g_rg_cw, g_rg_cb = _rg_bwd(d_ycat, u, s["hh"], p["rg_conv_w"], p["rg_conv_b"],
                                                                  p["rg_wa_b"], p["rg_ba"], p["rg_wx_b"], p["rg_bx"],
                                                                  p["rg_lam"], du)
    npc = du.shape[0]
    g_w_in = _grad_matmul(s["h_b"][None], du, npc, lambda b: 0, lambda b: (b + DU_PLANE[0]) % npc, (d, npc * d),
                          (d, d), lambda b: (0, b), 0, 1)
    in_ride = rides["in_bwd"](g_w_in, g_w_out) if "in_bwd" in rides else None
    (dx, d_shift, d_scale, g_norm_g), got_in = _in_bwd(du, p["w4"], s["x"], dxn, p["norm_g"], p["scale"], in_ride)
    grads = dict(norm_g=g_norm_g, w_in=g_w_in, rg_conv_w=g_rg_cw, rg_conv_b=g_rg_cb, rg_w_a=g_wa, rg_b_a=g_ba,
                 rg_w_x=g_wx, rg_b_x=g_bx, rg_lambda=g_lam, ml_conv_w=g_ml_cw, ml_conv_b=g_ml_cb, ml_w_qkv=g_wqkv,
                 ml_w_if=g_w_if, ml_b_if=g_b_if, ml_norm_g=g_ml_g, w_out=g_w_out)
    return dx, grads, jnp.concatenate([d_shift, d_scale, d_gate], axis=1), dict(mlstm_bwd=got, in_bwd=got_in)


def _trunk_fwd_bwd(x, target, final_g, layers):
    saved = []
    for p in layers:
        x, s, _, _ = _layer_fwd(x, p)
        saved.append(s)
    dx, g_final, loss = _final_loss(x, final_g, target)
    grads, dmods = [], []
    for layer in reversed(range(len(layers))):
        dx, g, dm, _ = _layer_bwd(dx, layers[layer], saved[layer])
        grads.append(g)
        dmods.append(dm)
    return loss, dx, g_final, grads[::-1], dmods[::-1]


def _me():
    return lax.axis_index("x"), lax.axis_index("y"), lax.axis_index("c")


def _remote(src, dst, send_sem, recv_sem, to):
    return pltpu.make_async_remote_copy(src_ref=src, dst_ref=dst, send_sem=send_sem, recv_sem=recv_sem,
                                        device_id=to, device_id_type=MESH)


def _all_gather8(blocks, space):
    n = len(blocks)

    def body(*refs):
        x_refs, out_refs = refs[:n], refs[n:2 * n]
        send_sems, recv_sems, local_sems = refs[2 * n:]
        x, y, c = _me()
        me, sibling = (x, y, c), (x, y, 1 - c)
        chips = [(1 - x, y), (x, 1 - y), (1 - x, 1 - y)]

        def rows(i, px, py, pc):
            m_per = blocks[i].shape[0]
            return out_refs[i].at[pl.ds((4 * px + 2 * py + pc) * m_per, m_per), :]

        def copy(i, k, blk, to, src=None):
            return _remote(rows(i, *blk) if src is None else src, rows(i, *blk), send_sems.at[7 * i + k],
                           recv_sems.at[7 * i + k], to)

        mine = [pltpu.make_async_copy(x_refs[i], rows(i, *me), local_sems.at[i]) for i in range(n)]
        first = []
        for i in range(n):
            first.append(copy(i, 0, me, sibling, src=x_refs[i]))
            first += [copy(i, 1 + j, me, (*chip, c), src=x_refs[i]) for j, chip in enumerate(chips)]
        for cp in mine + first:
            cp.start()
        passed = []
        for j, chip in enumerate(chips):
            for i in range(n):
                copy(i, 1 + j, (*chip, c), me).wait_recv()
                passed.append(copy(i, 4 + j, (*chip, c), sibling))
                passed[-1].start()
        for i in range(n):
            copy(i, 0, sibling, me).wait_recv()
            for j, chip in enumerate(chips):
                copy(i, 4 + j, (*chip, 1 - c), me).wait_recv()
        for cp in first + passed:
            cp.wait_send()
        for cp in mine:
            cp.wait()

    spec = pl.BlockSpec(memory_space=space)
    return _pcall(
        body, name="all_gather8",
        out_shape=[jax.ShapeDtypeStruct((8 * b.shape[0], b.shape[1]), b.dtype) for b in blocks],
        in_specs=[spec] * n, out_specs=[spec] * n,
        scratch_shapes=[pltpu.SemaphoreType.DMA((7 * n,)), pltpu.SemaphoreType.DMA((7 * n,)),
                        pltpu.SemaphoreType.DMA((n,))],
    )(*blocks)


def _sib_halves(g_in, g_out):
    depth, d, n4 = g_in.shape
    n = n4 // 4

    def body(gi, go, ri, ro, send_sems, recv_sems):
        x, y, c = _me()
        o = 1 - c
        pairs = [(gi.at[pl.ds(0, depth), pl.ds(o * (d // 2), d // 2), pl.ds(s * n, n)], ri.at[pl.ds(0, depth), s])
                 for s in range(4)]
        pairs.append((go.at[pl.ds(0, depth), pl.ds(0, 4), o], ro))
        copies = [_remote(src, dst, send_sems.at[k], recv_sems.at[k], (x, y, o)) for k, (src, dst) in enumerate(pairs)]
        for cp in copies:
            cp.start()
        for cp in copies:
            cp.wait_recv()
        for cp in copies:
            cp.wait_send()

    hbm = pl.BlockSpec(memory_space=pltpu.HBM)
    return _pcall(
        body, name="sib_halves",
        out_shape=[jax.ShapeDtypeStruct((depth, 4, d // 2, n), g_in.dtype),
                   jax.ShapeDtypeStruct(g_out.shape[:2] + g_out.shape[3:], g_out.dtype)],
        in_specs=[hbm] * 2, out_specs=[hbm] * 2,
        scratch_shapes=[pltpu.SemaphoreType.DMA((5,)), pltpu.SemaphoreType.DMA((5,))],
    )(g_in, g_out)


def _sib_slabs(slabs):
    ns = len(slabs)

    def body(*refs):
        sl, rs = refs[:ns], refs[ns:2 * ns]
        send_sems, recv_sems = refs[2 * ns:]
        x, y, c = _me()
        copies = [_remote(sl[i].at[1 - c], rs[i], send_sems.at[i], recv_sems.at[i], (x, y, 1 - c)) for i in range(ns)]
        for cp in copies:
            cp.start()
        for cp in copies:
            cp.wait_recv()
        for cp in copies:
            cp.wait_send()

    hbm = pl.BlockSpec(memory_space=pltpu.HBM)
    return _pcall(
        body, name="sib_slabs",
        out_shape=[jax.ShapeDtypeStruct(s.shape[1:], s.dtype) for s in slabs],
        in_specs=[hbm] * ns, out_specs=[hbm] * ns,
        scratch_shapes=[pltpu.SemaphoreType.DMA((ns,)), pltpu.SemaphoreType.DMA((ns,))],
    )(*slabs)


def _sib_fill(boths):
    n = len(boths)

    def body(*refs):
        dst = refs[n:2 * n]
        send_sems, recv_sems = refs[2 * n:]
        x, y, c = _me()
        view = lambda i: dst[i].at[pl.ds(0, boths[i].shape[0]), c]
        copies = [_remote(view(i), view(i), send_sems.at[i], recv_sems.at[i], (x, y, 1 - c)) for i in range(n)]
        for cp in copies:
            cp.start()
        for cp in copies:
            cp.wait_recv()
        for cp in copies:
            cp.wait_send()

    hbm = pl.BlockSpec(memory_space=pltpu.HBM)
    return _pcall(
        body, name="sib_fill",
        out_shape=[jax.ShapeDtypeStruct(b.shape, b.dtype) for b in boths],
        in_specs=[hbm] * n, out_specs=[hbm] * n, input_output_aliases={i: i for i in range(n)},
        scratch_shapes=[pltpu.SemaphoreType.DMA((n,)), pltpu.SemaphoreType.DMA((n,))],
    )(*boths)


def _chip_exchange(arrs):
    n = len(arrs)

    def body(*refs):
        src, dst = refs[:n], refs[n:2 * n]
        send_sems, recv_sems = refs[2 * n:]
        x, y, c = _me()
        me_s = 2 * x + y
        chips = [(1 - x, y), (x, 1 - y), (1 - x, 1 - y)]
        copies = [_remote(src[i].at[2 * px + py], dst[i].at[me_s], send_sems.at[3 * i + k], recv_sems.at[3 * i + k],
                          (px, py, c))
                  for i in range(n) for k, (px, py) in enumerate(chips)]
        for cp in copies:
            cp.start()
        for cp in copies:
            cp.wait_recv()
        for cp in copies:
            cp.wait_send()

    hbm = pl.BlockSpec(memory_space=pltpu.HBM)
    return _pcall(
        body, name="chip_exchange",
        out_shape=[jax.ShapeDtypeStruct(a.shape, a.dtype) for a in arrs],
        in_specs=[hbm] * n, out_specs=[hbm] * n,
        scratch_shapes=[pltpu.SemaphoreType.DMA((3 * n,)), pltpu.SemaphoreType.DMA((3 * n,))],
    )(*arrs)


def _row_tile(rows, cap=4096, mult=16):
    best = None
    for t in range(mult, min(rows, cap) + 1, mult):
        if rows % t == 0:
            best = t
    return rows if best is None else best


def _pair_sum(half, own, own_spec, got, got_spec, out_shape, out_spec, grid):
    def body(_, a_ref, b_ref, o_ref):
        o_ref[...] = (a_ref[...] + b_ref[...].astype(F32)).astype(o_ref.dtype)

    return _pcall(
        body, name="pair_sum",
        grid_spec=pltpu.PrefetchScalarGridSpec(num_scalar_prefetch=1, grid=grid, in_specs=[own_spec, got_spec],
                                               out_specs=out_spec),
        out_shape=out_shape, compiler_params=_seq(len(grid)))(half, own, got)


def _chip_sum(ids, part, met, fill, layer=0, stack=1):
    _, _, rows, n = part.shape
    tr = _row_tile(rows, cap=max(16, (1 << 18) // n))
    first = isinstance(stack, int)

    def body(_, own_ref, a_ref, b_ref, c_ref, *rest):
        acc = own_ref[...].astype(F32) + a_ref[...].astype(F32)
        acc = acc + b_ref[...].astype(F32)
        rest[-1][...] = acc + c_ref[...].astype(F32)

    blk = (None, None, tr, n)
    other = lambda k: pl.BlockSpec(blk, lambda j, ids: ((ids[0] + k) % 4, 0, j, 0))
    in_specs = [pl.BlockSpec(blk, lambda j, ids: (ids[0], 0, j, 0)), other(1), other(2), other(3)]
    return _pcall(
        body, name="chip_sum",
        grid_spec=pltpu.PrefetchScalarGridSpec(
            num_scalar_prefetch=1, grid=(rows // tr,),
            in_specs=in_specs if first else in_specs + [pl.BlockSpec(memory_space=pl.ANY)],
            out_specs=pl.BlockSpec(blk, lambda j, ids: (layer, ids[1] if fill else 0, j, 0))),
        out_shape=jax.ShapeDtypeStruct(((stack,) if first else stack.shape[:1]) + (2 if fill else 1, rows, n), F32),
        input_output_aliases={} if first else {5: 0},
        compiler_params=_seq())(*((ids, part, met, met, met) if first else (ids, part, met, met, met, stack)))


def _ada_mod(c_all, w_ada, b_ada_cols):
    depth, d, n = w_ada.shape
    nb = c_all.shape[0]

    def body(c_ref, w_ref, b_ref, o_ref):
        cv = c_ref[...]
        ca = _bf(cv * _sigmoid(cv))
        o_ref[0] = _dot(ca, _bf(w_ref[0])) + b_ref[0]

    return _pcall(body, name="ada_mod", grid=(depth,),
                  in_specs=[pl.BlockSpec((nb, d), lambda l: (0, 0)), pl.BlockSpec((1, d, n), lambda l: (l, 0, 0)),
                            pl.BlockSpec((1, 1, n), lambda l: (l, 0, 0))],
                  out_specs=pl.BlockSpec((1, nb, n), lambda l: (l, 0, 0)),
                  out_shape=jax.ShapeDtypeStruct((depth, nb, n), F32), compiler_params=_seq())(c_all, w_ada, b_ada_cols)


def _ada_grad(c_all, dmod_cols, dmod_all):
    nb, d = c_all.shape
    depth, _, n = dmod_cols.shape
    n_all = dmod_all.shape[2]

    def body(c_ref, dm_ref, da_ref, gw_ref, gb_ref):
        cv = c_ref[...]
        ca = _bf(cv * _sigmoid(cv))
        gw_ref[0] = _dot_tn(ca, _bf(dm_ref[0]))
        gb_ref[0] = _colsum(da_ref[0])

    return _pcall(body, name="ada_grad", grid=(depth,),
                  in_specs=[pl.BlockSpec((nb, d), lambda l: (0, 0)), pl.BlockSpec((1, nb, n), lambda l: (l, 0, 0)),
                            pl.BlockSpec((1, nb, n_all), lambda l: (l, 0, 0))],
                  out_specs=[pl.BlockSpec((1, d, n), lambda l: (l, 0, 0)), pl.BlockSpec((1, 1, n_all), lambda l: (l, 0, 0))],
                  out_shape=[jax.ShapeDtypeStruct((depth, d, n), F32), jax.ShapeDtypeStruct((depth, 1, n_all), F32)],
                  compiler_params=_seq())(c_all, dmod_cols, dmod_all)


def _adamw(w, g, m, v):
    shape = w.shape
    cols = shape[-1]
    rows = w.size // cols
    w2, g2, m2, v2 = (t.reshape(rows, cols) for t in (w, g, m, v))
    tr = _row_tile(rows, cap=max(8, (1 << 18) // cols), mult=8)

    def body(w_ref, g_ref, m_ref, v_ref, d_ref, mo_ref, vo_ref):
        gv = g_ref[...]
        mn = ADAM_B1 * m_ref[...] + (1.0 - ADAM_B1) * gv
        vn = ADAM_B2 * v_ref[...] + (1.0 - ADAM_B2) * (gv * gv)
        m_hat = mn / (1.0 - ADAM_B1 ** ADAM_STEP)
        v_hat = vn / (1.0 - ADAM_B2 ** ADAM_STEP)
        d_ref[...] = -ADAM_LR * (m_hat / (jnp.sqrt(v_hat) + ADAM_EPS) + ADAM_WD * w_ref[...])
        mo_ref[...] = mn
        vo_ref[...] = vn

    blk = pl.BlockSpec((tr, cols), lambda i: (i, 0))
    outs = _pcall(body, name="adamw", grid=(rows // tr,), in_specs=[blk] * 4, out_specs=[blk] * 3,
                  out_shape=[jax.ShapeDtypeStruct((rows, cols), F32)] * 3, compiler_params=_seq())(w2, g2, m2, v2)
    return tuple(o.reshape(shape) for o in outs)


WEIGHTS = ["norm_g", "w_ada", "b_ada", "w_in", "rg_conv_w", "rg_conv_b", "rg_w_a", "rg_b_a", "rg_w_x", "rg_b_x",
           "rg_lambda", "ml_conv_w", "ml_conv_b", "ml_w_q", "ml_w_k", "ml_w_v", "ml_w_if", "ml_b_if", "ml_norm_g",
           "w_out", "final_g"]
SMALL_SHARDED = {"ml_w_qkv": 2, "rg_conv_w": 1, "ml_conv_w": 1, "ml_w_if": 0}
REPLICATED = ["rg_w_a", "rg_w_x", "norm_g", "rg_conv_b", "rg_b_a", "rg_b_x", "rg_lambda", "ml_conv_b", "ml_norm_g",
              "ml_b_if"]
LANES = 128


def _to_pieces(g, axis):
    shp = g.shape
    g = g.reshape(shp[:axis] + (4, 2, shp[axis] // 8) + shp[axis + 1:])
    g = jnp.moveaxis(g, (axis, axis + 1), (0, 1))
    return g.reshape(4, 2, -1)


def _from_pieces(p, shard_shape, axis):
    k = p.shape[0]
    rest = shard_shape[:axis] + (shard_shape[axis] // k,) + shard_shape[axis + 1:]
    t = jnp.moveaxis(p.reshape((k,) + rest), 0, axis)
    return t.reshape(shard_shape)


def _pad_rows(flat, mult):
    n = flat.shape[-1]
    pad = (-n) % mult
    if pad:
        flat = jnp.concatenate([flat, jnp.zeros(flat.shape[:-1] + (pad,), flat.dtype)], axis=-1)
    return flat


def kernel(x, c, norm_g, w_ada, b_ada, w_in, rg_conv_w, rg_conv_b, rg_w_a, rg_b_a, rg_w_x, rg_b_x, rg_lambda, ml_conv_w, ml_conv_b, ml_w_q, ml_w_k, ml_w_v, ml_w_if, ml_b_if, ml_norm_g, w_out, final_g, loss_target, m_norm_g, m_w_ada, m_b_ada, m_w_in, m_rg_conv_w, m_rg_conv_b, m_rg_w_a, m_rg_b_a, m_rg_w_x, m_rg_b_x, m_rg_lambda, m_ml_conv_w, m_ml_conv_b, m_ml_w_q, m_ml_w_k, m_ml_w_v, m_ml_w_if, m_ml_b_if, m_ml_norm_g, m_w_out, m_final_g, v_norm_g, v_w_ada, v_b_ada, v_w_in, v_rg_conv_w, v_rg_conv_b, v_rg_w_a, v_rg_b_a, v_rg_w_x, v_rg_b_x, v_rg_lambda, v_ml_conv_w, v_ml_conv_b, v_ml_w_q, v_ml_w_k, v_ml_w_v, v_ml_w_if, v_ml_b_if, v_ml_norm_g, v_w_out, v_final_g):
    given = dict(locals())
    ax, ay, ac = lax.axis_index("x"), lax.axis_index("y"), lax.axis_index("c")
    chip = 2 * ax + ay
    me = 2 * chip + ac
    depth, d = norm_g.shape
    n_ada = w_ada.shape[2]
    pick = lambda a, i, axis=0: lax.dynamic_index_in_dim(a, i, axis, keepdims=False)

    convs = jnp.stack([rg_conv_w, ml_conv_w])
    n_conv = 2 * depth * CONV_WIDTH // 4
    blk = jnp.concatenate([c, convs.reshape(n_conv, d), jnp.zeros((8 - 1 - n_conv, d), F32)], axis=0)
    g0 = _all_gather8([blk], pltpu.VMEM)[0].reshape(8, 8, d)
    c_all = g0[:, 0, :]
    conv_full = g0[0::2, 1:1 + n_conv].reshape(4, 2, depth, CONV_WIDTH, d // 4)
    conv_full = conv_full.transpose(1, 2, 3, 0, 4).reshape(2, depth, CONV_WIDTH, d)

    b_cols = lax.dynamic_slice_in_dim(b_ada, chip * n_ada, n_ada, axis=1)[:, None, :]
    mod_part = _ada_mod(c_all, w_ada, b_cols)
    g1 = _all_gather8([mod_part.transpose(1, 0, 2).reshape(8, depth * n_ada)], pltpu.VMEM)[0]
    g1 = g1.reshape(8, 8, depth, n_ada)[0::2]
    mod_me = pick(g1.transpose(1, 2, 0, 3).reshape(8, depth, 4 * n_ada), me)

    def half_of(w, axis):
        n = w.shape[axis] // 2
        return lax.dynamic_slice_in_dim(w, ac * n, n, axis).astype(BF16)

    n_sh = w_in.shape[2]
    heads, hd_cut, hd = ml_w_q.shape[1:]

    def blocks_of(l):
        wqkv = jnp.stack([ml_w_q[l], ml_w_k[l], ml_w_v[l]])
        return [half_of(w_in[l], 0), half_of(w_out[l], 0), half_of(wqkv, 2).reshape(-1, hd), half_of(ml_w_if[l], 0)]

    def layer_of(l, w4, rest):
        return dict(
            norm_g=norm_g[l][None], shift=mod_me[l, 0:d][None], scale=mod_me[l, d:2 * d][None],
            gate=mod_me[l, 2 * d:3 * d][None], w4=w4.reshape(4, d, n_sh),
            rg_conv_w=conv_full[0, l], rg_conv_b=rg_conv_b[l][None], rg_wa_b=_bf(rg_w_a[l]), rg_ba=rg_b_a[l][None],
            rg_wx_b=_bf(rg_w_x[l]), rg_bx=rg_b_x[l][None], rg_lam=rg_lambda[l][None],
            ml_conv_w=conv_full[1, l], ml_conv_b=ml_conv_b[l][None], b_if=ml_b_if[l][None], b_ift=ml_b_if[l][:, None],
            ml_g=ml_norm_g[l][None], **rest)

    def rest_of(gathered):
        w_out_b, wqkv_g, wif = gathered
        return dict(w_out_b=w_out_b, wqkv_b=_from_pieces(wqkv_g.reshape(8, -1), (3, heads, hd, hd), 2), wif_b=wif,
                    wift_b=wif.T)

    landing = lambda b: jax.ShapeDtypeStruct((4, 2) + b.shape, b.dtype)
    whole = lambda landed: [t.reshape(-1, t.shape[-1]) for t in _sib_fill(landed)]
    first = blocks_of(0)
    p = layer_of(0, _all_gather8(first[:1], pltpu.HBM)[0], {})
    rides = dict(ln_inproj=Ride(first[1:], [landing(b) for b in first[1:]], False))
    late = lambda landed: rest_of(whole(landed))
    layers, saved = [], []
    xl = x[0]
    for l in range(depth):
        if l + 1 < depth:
            nxt = blocks_of(l + 1)
            rides["rg_fwd"] = Ride(nxt[:1], [landing(nxt[0])], False)
            rides["mlstm_fwd"] = Ride(nxt[1:], [landing(b) for b in nxt[1:]], False)
        xl, s, p, got = _layer_fwd(xl, p, rides, late)
        layers.append(p)
        saved.append(s)
        if l + 1 < depth:
            nxt_whole = whole(list(got["rg_fwd"]) + list(got["mlstm_fwd"]))
            p = layer_of(l + 1, nxt_whole[0], rest_of(nxt_whole[1:]))
            rides, late = {}, None
    dx, g_final, loss = _final_loss(xl, final_g[None], loss_target[0])

    half = ac.reshape(1)
    ids = jnp.stack([chip, ac])
    r_out = w_out.shape[1] // 2

    def partial_sums(g_w_in, g_w_out):
        g_out5 = g_w_out.reshape(1, 4, 2, r_out, d)
        got_in, got_out = _sib_halves(g_w_in, g_out5)
        part_in = _pair_sum(
            half, g_w_in, pl.BlockSpec((None, d // 2, n_sh), lambda s, h: (0, h[0], s)),
            got_in, pl.BlockSpec((None, None, d // 2, n_sh), lambda s, h: (0, s, 0, 0)),
            jax.ShapeDtypeStruct((4, 1, d // 2, n_sh), BF16),
            pl.BlockSpec((None, None, d // 2, n_sh), lambda s, h: (s, 0, 0, 0)), (4,))
        part_out = _pair_sum(
            half, g_out5, pl.BlockSpec((None, None, None, r_out, d), lambda s, h: (0, s, h[0], 0, 0)),
            got_out, pl.BlockSpec((None, None, r_out, d), lambda s, h: (0, s, 0, 0)),
            jax.ShapeDtypeStruct((4, 1, r_out, d), BF16),
            pl.BlockSpec((None, None, r_out, d), lambda s, h: (s, 0, 0, 0)), (4,))
        return [part_in, part_out]

    exchange = lambda parts_l: Ride(parts_l, [jax.ShapeDtypeStruct(t.shape, t.dtype) for t in parts_l], True)
    grads, dmods, parts, mets = [None] * depth, [None] * depth, [None] * depth, [None] * depth

    def last_exchange(g_w_in, g_w_out):
        parts[0] = partial_sums(g_w_in, g_w_out)
        return exchange(parts[0])

    rides = {}
    for l in reversed(range(depth)):
        if l == 0:
            rides["in_bwd"] = last_exchange
        dx, grads[l], dmods[l], got = _layer_bwd(dx, layers[l], saved[l], rides)
        if "mlstm_bwd" in rides:
            mets[l + 1] = got["mlstm_bwd"]
        if l == 0:
            mets[0] = got["in_bwd"]
        else:
            parts[l] = partial_sums(grads[l]["w_in"], grads[l]["w_out"])
            rides = dict(mlstm_bwd=exchange(parts[l]))

    dm_blk = jnp.concatenate(dmods + [jnp.zeros((8 - depth, 3 * d), F32)], axis=0)
    dm_all = _all_gather8([dm_blk], pltpu.VMEM)[0].reshape(8, 8, 3 * d)[:, :depth].transpose(1, 0, 2)
    dm_cols = lax.dynamic_slice_in_dim(dm_all, chip * n_ada, n_ada, axis=2)
    g_w_ada, g_b_ada = _ada_grad(c_all, dm_cols, dm_all)

    sm = jnp.concatenate([_to_pieces(grads[l][name], axis) for l in range(depth) for name, axis in SMALL_SHARDED.items()],
                         axis=-1)
    sm = _pad_rows(sm, 16 * LANES)
    n_sm = sm.shape[-1] // LANES
    sm = sm.transpose(1, 0, 2).reshape(2, 4 * n_sm, LANES)
    rep = [grads[l][name].reshape(-1) for l in range(depth) for name in REPLICATED[:-1]]
    rep += [_pad_rows(grads[l]["ml_b_if"].reshape(-1), LANES) for l in range(depth)]
    rep += [g_final.reshape(-1), loss.reshape(-1)]
    rep = _pad_rows(jnp.concatenate(rep), 8 * 8 * LANES)
    n_rep = rep.shape[0] // (8 * LANES)
    rep = rep.reshape(4, 2, n_rep, LANES).transpose(1, 0, 2, 3).reshape(2, 4 * n_rep, LANES)
    got_sm, got_rep = _sib_slabs([sm, rep])

    def slab_sum(slab, got, rows, dtype):
        blk = pl.BlockSpec((rows, LANES), lambda s, h: (s, 0))
        return _pair_sum(half, slab, pl.BlockSpec((None, rows, LANES), lambda s, h: (h[0], s, 0)), got, blk,
                         jax.ShapeDtypeStruct((4 * rows, LANES), dtype), blk, (4,)).reshape(4, 1, rows, LANES)

    part_sm = slab_sum(sm, got_sm, n_sm, BF16)
    part_rep = slab_sum(rep, got_rep, n_rep, F32)
    met_sm, met_rep = _chip_exchange([part_sm, part_rep])
    both_in, both_out = depth, depth
    for l in range(depth):
        both_in = _chip_sum(ids, parts[l][0], mets[l][0], True, l, both_in)
        both_out = _chip_sum(ids, parts[l][1], mets[l][1], True, l, both_out)
    both_in, both_out, both_sm = _sib_fill([both_in, both_out, _chip_sum(ids, part_sm, met_sm, True)])
    red_rep = _chip_sum(ids, part_rep, met_rep, False).reshape(n_rep, LANES)
    rep_all = _all_gather8([red_rep], pltpu.VMEM)[0].reshape(-1)

    g = dict(w_ada=g_w_ada, b_ada=g_b_ada.reshape(b_ada.shape), w_in=both_in.reshape(w_in.shape),
             w_out=both_out.reshape(w_out.shape))
    shard = both_sm.reshape(2, -1)
    off = 0
    per_layer = {name: [] for name in SMALL_SHARDED}
    for l in range(depth):
        for name, axis in SMALL_SHARDED.items():
            shp = (3,) + ml_w_q.shape[1:] if name == "ml_w_qkv" else given[name].shape[1:]
            n = grads[l][name].size // 8
            per_layer[name].append(_from_pieces(shard[:, off:off + n], shp, axis))
            off += n
    for name in SMALL_SHARDED:
        g[name] = jnp.stack(per_layer[name])
    for i, name in enumerate(["ml_w_q", "ml_w_k", "ml_w_v"]):
        g[name] = g["ml_w_qkv"][:, i]
    off = 0
    per_layer = {name: [] for name in REPLICATED}
    for l in range(depth):
        for name in REPLICATED[:-1]:
            n = given[name][l].size
            per_layer[name].append(rep_all[off:off + n].reshape(given[name].shape[1:]))
            off += n
    for l in range(depth):
        n = given["ml_b_if"][l].size
        per_layer["ml_b_if"].append(rep_all[off:off + n])
        off += LANES
    for name in REPLICATED:
        g[name] = jnp.stack(per_layer[name])
    g["final_g"] = rep_all[off:off + d]
    loss_all = rep_all[off + d]

    deltas, new_m, new_v = [], [], []
    for name in WEIGHTS:
        dl, mn, vn = _adamw(given[name], g[name], given["m_" + name], given["v_" + name])
        deltas.append(dl)
        new_m.append(mn)
        new_v.append(vn)
    return (loss_all, dx[None], *[g[name] for name in WEIGHTS], *deltas, *new_m, *new_v)
```

```python
import functools
from typing import NamedTuple

import jax
import jax.numpy as jnp
from jax import lax
from jax.experimental import pallas as pl
from jax.experimental.pallas import tpu as pltpu

F32 = jnp.float32
BF16 = jnp.bfloat16

EPS = 1e-6
RG_C = 8.0
CONV_WIDTH = 4
ML_CHUNK = 128
HALO = 8
ADAM_LR = 0.001
ADAM_B1 = 0.9
ADAM_B2 = 0.999
ADAM_EPS = 1e-08
ADAM_WD = 0.01
ADAM_STEP = 10
MESH = pl.DeviceIdType.MESH


def _pcall(body, **kw):
    return pl.pallas_call(body, **kw)


class Ride(NamedTuple):
    srcs: list
    dst_shapes: list
    sliced: bool


def _pcall_ride(body, ride, *, grid, in_specs, out_specs, out_shape, args, scratch_shapes=(), **kw):
    n_in, n_out, n_scr = len(in_specs), len(out_specs), len(scratch_shapes)
    if ride is None:
        res = _pcall(body, grid=grid, in_specs=in_specs, out_specs=out_specs, out_shape=out_shape,
                     scratch_shapes=list(scratch_shapes), **kw)(*args)
        return res, []
    nr = len(ride.srcs)

    def riding(*refs):
        ins, rsrc = refs[:n_in], refs[n_in:n_in + nr]
        outs, rdst = refs[n_in + nr:n_in + nr + n_out], refs[n_in + nr + n_out:n_in + 2 * nr + n_out]
        scr = refs[n_in + 2 * nr + n_out:n_in + 2 * nr + n_out + n_scr]
        send_sems, recv_sems, local_sems = refs[n_in + 2 * nr + n_out + n_scr:]
        x, y, c = _me()
        me_s = 2 * x + y
        chips = [(1 - x, y), (x, 1 - y), (1 - x, 1 - y)]
        copies, local = [], []
        for i in range(nr):
            for k, (px, py) in enumerate(chips):
                src = rsrc[i].at[2 * px + py] if ride.sliced else rsrc[i]
                dst = rdst[i].at[me_s] if ride.sliced else rdst[i].at[me_s, c]
                copies.append(_remote(src, dst, send_sems.at[3 * i + k], recv_sems.at[3 * i + k], (px, py, c)))
            if not ride.sliced:
                local.append(pltpu.make_async_copy(rsrc[i], rdst[i].at[me_s, c], local_sems.at[i]))
        first = functools.reduce(jnp.logical_and, [pl.program_id(a) == 0 for a in range(len(grid))])
        last = functools.reduce(jnp.logical_and, [pl.program_id(a) == grid[a] - 1 for a in range(len(grid))])

        @pl.when(first)
        def _():
            for cp in copies + local:
                cp.start()

        body(*ins, *outs, *scr)

        @pl.when(last)
        def _():
            for cp in copies:
                cp.wait_recv()
            for cp in copies:
                cp.wait_send()
            for cp in local:
                cp.wait()

    hbm = pl.BlockSpec(memory_space=pltpu.HBM)
    res = _pcall(
        riding, grid=grid, in_specs=list(in_specs) + [hbm] * nr, out_specs=list(out_specs) + [hbm] * nr,
        out_shape=list(out_shape) + list(ride.dst_shapes),
        scratch_shapes=list(scratch_shapes) + [pltpu.SemaphoreType.DMA((3 * nr,)), pltpu.SemaphoreType.DMA((3 * nr,)),
                                               pltpu.SemaphoreType.DMA((nr,))], **kw)(*args, *ride.srcs)
    return res[:n_out], res[n_out:]


def _seq(n=1):
    return pltpu.CompilerParams(dimension_semantics=("arbitrary",) * n)


def _dot(a, b):
    return jnp.dot(a, b, preferred_element_type=F32)


def _dot_nt(a, b):
    return lax.dot_general(a, b, (((1,), (1,)), ((), ())), preferred_element_type=F32)


def _dot_tn(a, b):
    return lax.dot_general(a, b, (((0,), (0,)), ((), ())), preferred_element_type=F32)


def _bf(x):
    return x.astype(BF16)


def _sigmoid(x):
    return 0.5 * jnp.tanh(0.5 * x) + 0.5


def _log1p(z):
    u = 1.0 + z
    return jnp.where(u == 1.0, z, jnp.log(u) * (z / jnp.where(u == 1.0, 1.0, u - 1.0)))


def _softplus(x):
    return jnp.maximum(x, 0.0) + _log1p(jnp.exp(-jnp.abs(x)))


def _log_sigmoid(x):
    return -_softplus(-x)


def _one_minus_sq(a, log_a):
    x = 2.0 * log_a
    small = -x * (1.0 + x * (0.5 + x * (1.0 / 6.0)))
    return jnp.where(x > -0.004, small, 1.0 - a * a)


def _dsilu(x, s):
    return s * (1.0 + x * (1.0 - s))


def _rowsum(x):
    return jnp.sum(x, axis=1, keepdims=True)


def _colsum(x):
    return jnp.sum(x, axis=0, keepdims=True)


def _shift_down(win, s):
    return win if s == 0 else pltpu.roll(win, s, 0)


def _shift_up(win, s):
    return win if s == 0 else pltpu.roll(win, win.shape[0] - s, 0)


def _conv_taps(win):
    return [_shift_down(win, CONV_WIDTH - 1 - k)[HALO:] for k in range(CONV_WIDTH)]


def _conv_fwd(taps, w_ref, b_ref):
    acc = b_ref[...] + w_ref[CONV_WIDTH - 1:CONV_WIDTH, :] * taps[CONV_WIDTH - 1]
    for k in range(CONV_WIDTH - 1):
        acc = acc + w_ref[k:k + 1, :] * taps[k]
    return acc


def _split3(x):
    hi = _bf(x)
    r1 = x - hi.astype(F32)
    mid = _bf(r1)
    lo = _bf(r1 - mid.astype(F32))
    return hi, mid, lo


def _tri_dot_left(tri, x):
    hi, mid, lo = _split3(x)
    return _dot(tri, hi) + _dot(tri, mid) + _dot(tri, lo)


def _tri_dot_right(x, tri):
    hi, mid, lo = _split3(x)
    return _dot(hi, tri) + _dot(mid, tri) + _dot(lo, tri)


def _tile(n, want):
    t = min(n, want)
    assert n % t == 0
    return t


def _ln_inproj(x, g, scale, shift, w4, ride=None):
    s_len, d = x.shape
    nj, _, nsh = w4.shape
    tm = _tile(s_len, 1024)

    def body(x_ref, g_ref, sc_ref, sh_ref, w_ref, h_ref, u_ref, hs):
        @pl.when(pl.program_id(1) == 0)
        def _():
            xv = x_ref[...]
            r = lax.rsqrt(jnp.mean(xv * xv, axis=-1, keepdims=True) + EPS)
            hv = (xv * r * g_ref[...]) * (1.0 + sc_ref[...]) + sh_ref[...]
            hs[...] = _bf(hv)
            h_ref[...] = hs[...]

        u_ref[...] = _dot(hs[...], w_ref[0])

    vec = pl.BlockSpec((1, d), lambda i, j: (0, 0))
    return _pcall_ride(
        body, ride, name="ln_inproj", grid=(s_len // tm, nj),
        in_specs=[pl.BlockSpec((tm, d), lambda i, j: (i, 0)), vec, vec, vec,
                  pl.BlockSpec((1, d, nsh), lambda i, j: (j, 0, 0))],
        out_specs=[pl.BlockSpec((tm, d), lambda i, j: (i, 0)), pl.BlockSpec((tm, nsh), lambda i, j: (i, j))],
        out_shape=[jax.ShapeDtypeStruct((s_len, d), BF16), jax.ShapeDtypeStruct((s_len, nj * nsh), F32)],
        scratch_shapes=[pltpu.VMEM((tm, d), BF16)],
        compiler_params=_seq(2),
        args=(x, g, scale, shift, w4))


def _rg_gates(xc, wa_ref, ba_ref, wx_ref, bx_ref, lam_ref):
    heads, hd, _ = wa_ref.shape
    xb = _bf(xc)
    ga = jnp.concatenate([_dot(xb[:, h * hd:(h + 1) * hd], wa_ref[h]) for h in range(heads)], axis=1) + ba_ref[...]
    gx = jnp.concatenate([_dot(xb[:, h * hd:(h + 1) * hd], wx_ref[h]) for h in range(heads)], axis=1) + bx_ref[...]
    r = _sigmoid(ga)
    ig = _sigmoid(gx)
    sp = _softplus(-lam_ref[...])
    log_a = (-RG_C) * r * sp
    a = jnp.exp(log_a)
    mult = jnp.sqrt(_one_minus_sq(a, log_a))
    return r, ig, sp, log_a, a, mult


def _scan_groups(a, u, reverse):
    n, c = a.shape
    a = a.reshape(n // 8, 8, c)
    u = u.reshape(n // 8, 8, c)
    row = lax.broadcasted_iota(jnp.int32, a.shape, 1)
    for k in (1, 2, 4):
        sft = 8 - k if reverse else k
        a_sh, u_sh = pltpu.roll(a, sft, 1), pltpu.roll(u, sft, 1)
        ok = row < 8 - k if reverse else row >= k
        u = jnp.where(ok, a * u_sh + u, u)
        a = jnp.where(ok, a * a_sh, a)
    return a.reshape(n, c), u.reshape(n, c)


def _rg_fwd(u, conv_w, conv_b, wa_b, ba, wx_b, bx, lam, ride=None):
    s_len = u.shape[0]
    d = conv_w.shape[1]
    tm = _tile(s_len, 256)
    per = tm // HALO

    def body(x_ref, xp_ref, z_ref, cw_ref, cb_ref, wa_ref, ba_ref, wx_ref, bx_ref, lam_ref,
             hh_ref, y_ref, carry):
        i = pl.program_id(0)

        @pl.when(i == 0)
        def _():
            carry[...] = jnp.zeros_like(carry)

        prev = jnp.where(i == 0, 0.0, xp_ref[...])
        xc = _conv_fwd(_conv_taps(jnp.concatenate([prev, x_ref[...]], axis=0)), cw_ref, cb_ref)
        _, ig, _, _, a, mult = _rg_gates(xc, wa_ref, ba_ref, wx_ref, bx_ref, lam_ref)
        ca, cu = _scan_groups(a, mult * (ig * xc), reverse=False)
        c = carry[0:1, :]
        for j in range(per):
            blk = ca[j * 8:(j + 1) * 8] * c + cu[j * 8:(j + 1) * 8]
            hh_ref[j * 8:(j + 1) * 8, :] = blk
            c = blk[7:8]
        carry[0:1, :] = c
        z = z_ref[...]
        y_ref[0] = _bf(hh_ref[...] * (z * _sigmoid(z)))

    vec = pl.BlockSpec((1, d), lambda i: (0, 0))
    whole3 = lambda a: pl.BlockSpec(a.shape, lambda i: (0, 0, 0))
    return _pcall_ride(
        body, ride, name="rg_fwd", grid=(s_len // tm,),
        in_specs=[pl.BlockSpec((tm, d), lambda i: (i, 0)),
                  pl.BlockSpec((HALO, d), lambda i: (jnp.maximum(i * per - 1, 0), 0)),
                  pl.BlockSpec((tm, d), lambda i: (i, 1)),
                  pl.BlockSpec((CONV_WIDTH, d), lambda i: (0, 0)), vec,
                  whole3(wa_b), vec, whole3(wx_b), vec, vec],
        out_specs=[pl.BlockSpec((tm, d), lambda i: (i, 0)), pl.BlockSpec((1, tm, d), lambda i: (0, i, 0))],
        out_shape=[jax.ShapeDtypeStruct((s_len, d), F32), jax.ShapeDtypeStruct((2, s_len, d), BF16)],
        scratch_shapes=[pltpu.VMEM((8, d), F32)],
        compiler_params=_seq(),
        args=(u, u, u, conv_w, conv_b, wa_b, ba, wx_b, bx, lam))


def _ml_pre(u, conv_w, conv_b, wqkv_b, wif_b, wift_b, b_if, b_ift):
    s_len = u.shape[0]
    d = conv_w.shape[1]
    _, heads, hd, _ = wqkv_b.shape
    ng = 2 * heads
    tm = _tile(s_len, 256)
    per = tm // HALO

    def body(x_ref, xp_ref, cw_ref, cb_ref, w_ref, wif_ref, wift_ref, bif_ref, bift_ref,
             qkv_ref, gt_ref, gtt_ref, bc_ref, bct_ref):
        i = pl.program_id(0)
        prev = jnp.where(i == 0, 0.0, xp_ref[...])
        xm = x_ref[...]
        pre = _conv_fwd(_conv_taps(jnp.concatenate([prev, xm], axis=0)), cw_ref, cb_ref)
        xcb = _bf(pre * _sigmoid(pre))
        xmb = _bf(xm)
        for h in range(heads):
            hs = slice(h * hd, (h + 1) * hd)
            qkv_ref[0, :, hs] = _bf(_dot(xcb[:, hs], w_ref[0, h]))
            qkv_ref[1, :, hs] = _bf(_dot(xcb[:, hs], w_ref[1, h]))
            qkv_ref[2, :, hs] = _bf(_dot(xmb[:, hs], w_ref[2, h]))
        qb, kb, vb = qkv_ref[0], qkv_ref[1], qkv_ref[2]
        gt = (_dot(qb, wif_ref[0:d, :]) + _dot(kb, wif_ref[d:2 * d, :]) + _dot(vb, wif_ref[2 * d:3 * d, :])
              + bif_ref[...])
        gtt = (_dot_nt(wift_ref[:, 0:d], qb) + _dot_nt(wift_ref[:, d:2 * d], kb)
               + _dot_nt(wift_ref[:, 2 * d:3 * d], vb) + bift_ref[...])
        gt_ref[...] = gt
        gtt_ref[...] = gtt
        r = lax.broadcasted_iota(jnp.int32, (tm, tm), 0)
        c = lax.broadcasted_iota(jnp.int32, (tm, tm), 1)
        same = (r // ML_CHUNK) == (c // ML_CHUNK)
        bc_ref[...] = _tri_dot_left(((r >= c) & same).astype(BF16), _log_sigmoid(gt))
        bct_ref[...] = _tri_dot_right(_log_sigmoid(gtt), ((r <= c) & same).astype(BF16))

    vec = pl.BlockSpec((1, d), lambda i: (0, 0))
    whole2 = lambda a: pl.BlockSpec(a.shape, lambda i: (0, 0))
    col = pl.BlockSpec((tm, ng), lambda i: (i, 0))
    row = pl.BlockSpec((ng, tm), lambda i: (0, i))
    return _pcall(
        body, name="ml_pre", grid=(s_len // tm,),
        in_specs=[pl.BlockSpec((tm, d), lambda i: (i, 2)),
                  pl.BlockSpec((HALO, d), lambda i: (jnp.maximum(i * per - 1, 0), 2)),
                  pl.BlockSpec((CONV_WIDTH, d), lambda i: (0, 0)), vec,
                  pl.BlockSpec(wqkv_b.shape, lambda i: (0, 0, 0, 0)), whole2(wif_b), whole2(wift_b), whole2(b_if),
                  whole2(b_ift)],
        out_specs=[pl.BlockSpec((3, tm, d), lambda i: (0, i, 0)), col, row, col, row],
        out_shape=[jax.ShapeDtypeStruct((3, s_len, d), BF16), jax.ShapeDtypeStruct((s_len, ng), F32),
                   jax.ShapeDtypeStruct((ng, s_len), F32), jax.ShapeDtypeStruct((s_len, ng), F32),
                   jax.ShapeDtypeStruct((ng, s_len), F32)],
        compiler_params=_seq(),
    )(u, u, conv_w, conv_b, wqkv_b, wif_b, wift_b, b_if, b_ift)


def _chunk_gates(gt, gtt, bc, bct, h, heads):
    li_c = gt[:, h:h + 1]
    li_r = gtt[h:h + 1, :]
    gf_c = gt[:, heads + h:heads + h + 1]
    b_c = bc[:, heads + h:heads + h + 1]
    b_r = bct[heads + h:heads + h + 1, :]
    return li_c, li_r, gf_c, b_c, b_r


def _chunk_weights(li_c, li_r, b_c, b_r, m_prev, causal):
    lc = b_c.shape[0]
    b_last = b_c[lc - 1:lc, :]
    dmat = jnp.where(causal, b_c - b_r + li_r, -jnp.inf)
    m_inter = b_c + m_prev
    m_t = jnp.maximum(m_inter, jnp.max(dmat, axis=1, keepdims=True))
    w_intra = jnp.exp(dmat - m_t)
    w_inter = jnp.exp(m_inter - m_t)
    g_c = b_last - b_c + li_c
    m_new = jnp.maximum(b_last + m_prev, jnp.max(g_c, axis=0, keepdims=True))
    w_state = jnp.exp(g_c - m_new)
    decay = jnp.exp(b_last + m_prev - m_new)
    return m_t, w_intra, w_inter, m_new, w_state, decay


def _tri_masks(lc):
    r = lax.broadcasted_iota(jnp.int32, (lc, lc), 0)
    c = lax.broadcasted_iota(jnp.int32, (lc, lc), 1)
    causal = r >= c
    return causal, causal.astype(BF16), (r <= c).astype(BF16)


def _mlstm_fwd(qkv, gates, u, ml_g, ycat, ride=None):
    _, s_len, d = qkv.shape
    ng = gates[0].shape[1]
    heads = ng // 2
    hd = d // heads
    lc = ML_CHUNK
    nc = s_len // lc
    kscale = hd ** -0.5

    def body(qkv_ref, gt_ref, gtt_ref, bc_ref, bct_ref, o_ref, z_ref, g_ref, _, cell_ref, y_ref, cst_ref, nst_ref,
             mst_ref, cs, ns, ms):
        @pl.when(pl.program_id(0) == 0)
        def _():
            cs[...] = jnp.zeros_like(cs)
            ns[...] = jnp.zeros_like(ns)
            ms[...] = jnp.zeros_like(ms)

        causal = _tri_masks(lc)[0]
        gtv, gttv, bcv, bctv = gt_ref[...], gtt_ref[...], bc_ref[...], bct_ref[...]
        old = [(cs[h], ns[h], ms[h]) for h in range(heads)]
        new, cells, ys = [], [], []
        for h in range(heads):
            hs = slice(h * hd, (h + 1) * hd)
            li_c, li_r, _, b_c, b_r = _chunk_gates(gtv, gttv, bcv, bctv, h, heads)
            c_old, n_old, m_old = old[h]
            m_prev = m_old[:, 0:1]
            m_t, w_intra, w_inter, m_new, w_state, decay = _chunk_weights(li_c, li_r, b_c, b_r, m_prev, causal)
            qb = qkv_ref[0, :, hs]
            ks = qkv_ref[1, :, hs].astype(F32) * kscale
            kb = _bf(ks)
            vb = qkv_ref[2, :, hs]
            s = _dot_nt(qb, kb) * w_intra
            num = _dot(_bf(s), vb) + w_inter * _dot(qb, _bf(c_old))
            den = _rowsum(s) + w_inter * _rowsum(qb.astype(F32) * n_old)
            cell = num / jnp.maximum(jnp.abs(den), jnp.exp(-m_t))
            kw = ks * w_state
            new.append((decay * c_old + _dot_tn(_bf(kw), vb), decay * n_old + _colsum(kw),
                        jnp.broadcast_to(m_new, m_old.shape)))
            cells.append(cell)
            hm = _sigmoid(o_ref[:, hs]) * cell
            hn = hm * lax.rsqrt(jnp.mean(hm * hm, axis=-1, keepdims=True) + EPS)
            z = z_ref[:, hs]
            ys.append(_bf((hn * g_ref[:, hs]) * (z * _sigmoid(z))))
        for h in range(heads):
            cst_ref[0, h] = _bf(old[h][0])
            nst_ref[0, h] = old[h][1]
            mst_ref[0, h] = old[h][2]
            cs[h], ns[h], ms[h] = new[h]
        cell_ref[...] = jnp.concatenate(cells, axis=1)
        y_ref[0] = jnp.concatenate(ys, axis=1)

    row = pl.BlockSpec((lc, d), lambda c: (c, 0))
    gcol = pl.BlockSpec((lc, ng), lambda c: (c, 0))
    grow = pl.BlockSpec((ng, lc), lambda c: (0, c))
    return _pcall_ride(
        body, ride, name="mlstm_fwd", grid=(nc,),
        in_specs=[pl.BlockSpec((3, lc, d), lambda c: (0, c, 0)), gcol, grow, gcol, grow,
                  pl.BlockSpec((lc, d), lambda c: (c, 3)), pl.BlockSpec((lc, d), lambda c: (c, 4)),
                  pl.BlockSpec((1, d), lambda c: (0, 0)), pl.BlockSpec(memory_space=pl.ANY)],
        out_specs=[row, pl.BlockSpec((1, lc, d), lambda c: (1, c, 0)),
                   pl.BlockSpec((1, heads, hd, hd), lambda c: (c, 0, 0, 0)),
                   pl.BlockSpec((1, heads, 1, hd), lambda c: (c, 0, 0, 0)),
                   pl.BlockSpec((1, heads, 1, 128), lambda c: (c, 0, 0, 0))],
        out_shape=[jax.ShapeDtypeStruct((s_len, d), F32), jax.ShapeDtypeStruct(ycat.shape, BF16),
                   jax.ShapeDtypeStruct((nc, heads, hd, hd), BF16),
                   jax.ShapeDtypeStruct((nc, heads, 1, hd), F32),
                   jax.ShapeDtypeStruct((nc, heads, 1, 128), F32)],
        scratch_shapes=[pltpu.VMEM((heads, hd, hd), F32), pltpu.VMEM((heads, 1, hd), F32),
                        pltpu.VMEM((heads, 1, 128), F32)],
        input_output_aliases={8: 1},
        compiler_params=_seq(),
        args=(qkv, *gates, u, u, ml_g, ycat))


def _out_proj(ycat, w_out_b, x, gate):
    s_len, d = x.shape
    tm = _tile(s_len, 1024)

    def body(a_ref, w_ref, x_ref, g_ref, y_ref, xn_ref):
        y = _dot(a_ref[0], w_ref[0:d, :]) + _dot(a_ref[1], w_ref[d:2 * d, :])
        y_ref[...] = y
        xn_ref[...] = x_ref[...] + g_ref[...] * y

    row = pl.BlockSpec((tm, d), lambda i: (i, 0))
    return _pcall(
        body, name="out_proj", grid=(s_len // tm,),
        in_specs=[pl.BlockSpec((2, tm, d), lambda i: (0, i, 0)), pl.BlockSpec((2 * d, d), lambda i: (0, 0)), row,
                  pl.BlockSpec((1, d), lambda i: (0, 0))],
        out_specs=[row, row],
        out_shape=[jax.ShapeDtypeStruct((s_len, d), F32)] * 2,
        compiler_params=_seq(),
    )(ycat, w_out_b, x, gate)


def _final_loss(x, g, target):
    s_len, d = x.shape
    tm = _tile(s_len, 256)

    def body(x_ref, g_ref, t_ref, dx_ref, dg_ref, loss_ref):
        @pl.when(pl.program_id(0) == 0)
        def _():
            dg_ref[...] = jnp.zeros_like(dg_ref)
            loss_ref[...] = jnp.zeros_like(loss_ref)

        xv = x_ref[...]
        r = lax.rsqrt(jnp.mean(xv * xv, axis=-1, keepdims=True) + EPS)
        xn = xv * r
        err = xn * g_ref[...] - t_ref[...]
        loss_ref[...] += 0.5 * jnp.sum(jnp.mean(err * err, axis=-1, keepdims=True))
        dout = err * (1.0 / d)
        dg_ref[...] += _colsum(dout * xn)
        dxn = dout * g_ref[...]
        dx_ref[...] = r * (dxn - xn * jnp.mean(dxn * xn, axis=-1, keepdims=True))

    row = pl.BlockSpec((tm, d), lambda i: (i, 0))
    vec = pl.BlockSpec((1, d), lambda i: (0, 0))
    return _pcall(
        body, name="final_loss", grid=(s_len // tm,),
        in_specs=[row, vec, row],
        out_specs=[row, vec, pl.BlockSpec((1, 128), lambda i: (0, 0))],
        out_shape=[jax.ShapeDtypeStruct((s_len, d), F32), jax.ShapeDtypeStruct((1, d), F32),
                   jax.ShapeDtypeStruct((1, 128), F32)],
        compiler_params=_seq(),
    )(x, g, target)


def _out_bwd(dxn, y, gate, w_out_b):
    s_len, d = dxn.shape
    tm = _tile(s_len, 1024)

    def body(dx_ref, y_ref, g_ref, w_ref, dg_ref, dy_ref, dc_ref):
        @pl.when(pl.program_id(0) == 0)
        def _():
            dg_ref[...] = jnp.zeros_like(dg_ref)

        dx = dx_ref[...]
        dg_ref[...] += _colsum(dx * y_ref[...])
        dy = _bf(g_ref[...] * dx)
        dy_ref[...] = dy
        dc_ref[0] = _dot_nt(dy, w_ref[0:d, :])
        dc_ref[1] = _dot_nt(dy, w_ref[d:2 * d, :])

    row = pl.BlockSpec((tm, d), lambda i: (i, 0))
    vec = pl.BlockSpec((1, d), lambda i: (0, 0))
    return _pcall(
        body, name="out_bwd", grid=(s_len // tm,),
        in_specs=[row, row, vec, pl.BlockSpec((2 * d, d), lambda i: (0, 0))],
        out_specs=[vec, row, pl.BlockSpec((2, tm, d), lambda i: (0, i, 0))],
        out_shape=[jax.ShapeDtypeStruct((1, d), F32), jax.ShapeDtypeStruct((s_len, d), BF16),
                   jax.ShapeDtypeStruct((2, s_len, d), F32)],
        compiler_params=_seq(),
    )(dxn, y, gate, w_out_b)


def _grad_matmul(a3, b3, nblk, a_idx, b_idx, out_shape, out_block, out_idx, layer, stack):
    _, s_len, m = a3.shape
    n = b3.shape[2]
    tk = _tile(s_len, 2048)
    first = isinstance(stack, int)

    def body(a_ref, b_ref, *rest):
        o_ref = rest[-1]

        @pl.when(pl.program_id(1) == 0)
        def _():
            o_ref[...] = jnp.zeros_like(o_ref)

        o_ref[...] += _dot_tn(a_ref[0], b_ref[0])

    in_specs = [pl.BlockSpec((1, tk, m), lambda p, t: (a_idx(p), t, 0)),
                pl.BlockSpec((1, tk, n), lambda p, t: (b_idx(p), t, 0))]
    return _pcall(
        body, name="grad_matmul", grid=(nblk, s_len // tk),
        in_specs=in_specs if first else in_specs + [pl.BlockSpec(memory_space=pl.ANY)],
        out_specs=pl.BlockSpec((None,) + out_block, lambda p, t: (layer,) + out_idx(p)),
        out_shape=jax.ShapeDtypeStruct(((stack,) if first else stack.shape[:1]) + out_shape, F32),
        input_output_aliases={} if first else {2: 0},
        compiler_params=_seq(2),
    )(*((a3, b3) if first else (a3, b3, stack)))


DU_PLANE = (2, 3, 4, 0, 1)


def _mlstm_bwd(qkv, gates, cst, nst, mst, cell, u, ml_g, d_ycat, wif_b, ride=None):
    _, s_len, d = qkv.shape
    ng = gates[0].shape[1]
    heads = ng // 2
    hd = d // heads
    lc = ML_CHUNK
    nc = s_len // lc
    kscale = hd ** -0.5

    def body(qkv_ref, gt_ref, gtt_ref, bc_ref, bct_ref, cst_ref, nst_ref, mst_ref, cell_ref, o_ref, z_ref, g_ref, dy_ref,
             wif_ref, dqkv_ref, dgt_ref, dbif_ref, du_ref, dg_ref, dcs, dns):
        @pl.when(pl.program_id(0) == 0)
        def _():
            dbif_ref[...] = jnp.zeros_like(dbif_ref)
            dcs[...] = jnp.zeros_like(dcs)
            dns[...] = jnp.zeros_like(dns)
            dg_ref[...] = jnp.zeros_like(dg_ref)

        causal, tril, triu = _tri_masks(lc)
        tril_strict = (tril.astype(F32) - (tril * triu).astype(F32)).astype(BF16)
        gtv, gttv, bcv, bctv = gt_ref[...], gtt_ref[...], bc_ref[...], bct_ref[...]
        lane = lax.broadcasted_iota(jnp.int32, (lc, ng), 1)
        dli_all = jnp.zeros((lc, ng), F32)
        from_later = jnp.zeros((lc, ng), F32)
        from_earlier = jnp.zeros((lc, ng), F32)
        across_all = jnp.zeros((1, ng), F32)
        old = [(dcs[h], dns[h]) for h in range(heads)]
        new, d_o, d_z, d_g, dqs, dks, dvs = [], [], [], [], [], [], []
        for h in range(heads):
            hs = slice(h * hd, (h + 1) * hd)
            li_c, li_r, gf_c, b_c, b_r = _chunk_gates(gtv, gttv, bcv, bctv, h, heads)
            m_prev = mst_ref[0, h][:, 0:1]
            m_t, w_intra, w_inter, _, w_state, decay = _chunk_weights(li_c, li_r, b_c, b_r, m_prev, causal)
            qb = qkv_ref[0, :, hs]
            qf = qb.astype(F32)
            ks = qkv_ref[1, :, hs].astype(F32) * kscale
            kb = _bf(ks)
            vb = qkv_ref[2, :, hs]
            c_b = cst_ref[0, h]
            n_old = nst_ref[0, h]
            s = _dot_nt(qb, kb) * w_intra
            den = _rowsum(s) + w_inter * _rowsum(qf * n_old)
            floor = jnp.exp(-m_t)
            dstab = jnp.maximum(jnp.abs(den), floor)
            cell = cell_ref[:, hs]
            o = o_ref[:, hs]
            so = _sigmoid(o)
            hm = so * cell
            rinv = lax.rsqrt(jnp.mean(hm * hm, axis=-1, keepdims=True) + EPS)
            hn = hm * rinv
            z = z_ref[:, hs]
            sgz = _sigmoid(z)
            sz = z * sgz
            gh = g_ref[:, hs]
            dy = dy_ref[0, :, hs]
            d_z.append(_bf(dy * (hn * gh) * _dsilu(z, sgz)))
            d_g.append(_colsum(dy * hn * sz))
            dhn = dy * gh * sz
            dhm = rinv * (dhn - hn * jnp.mean(dhn * hn, axis=-1, keepdims=True))
            d_o.append(_bf(dhm * cell * so * (1.0 - so)))
            dcell = dhm * so
            dnum = dcell / dstab
            dnb = _bf(dnum)
            dden = -_rowsum(dcell * cell) / dstab * jnp.where(jnp.abs(den) > floor, jnp.where(den > 0.0, 1.0, -1.0), 0.0)
            dst = _dot_nt(dnb, vb) + dden
            dsdb = _bf(dst * w_intra)
            dc_out, dn_out = old[h]
            dcb = _bf(dc_out)
            dq_inter = w_inter * (_dot_nt(dnb, c_b) + dden * n_old)
            dk_inter = w_state * (_dot_nt(vb, dcb) + dn_out)
            dq = _dot(dsdb, kb) + dq_inter
            dk = _dot_tn(dsdb, qb) + dk_inter
            dv = _dot_tn(_bf(s), dnb) + _dot(_bf(ks * w_state), dcb)
            wq = w_inter * qf
            new.append((decay * dc_out + _dot_tn(_bf(wq), dnb), decay * dn_out + _colsum(wq * dden)))
            pmat = dst * s
            p_rows = _rowsum(pmat)
            p_cols = _rowsum(pmat.T)
            q_in = _rowsum(qf * dq_inter)
            k_in = _rowsum(ks * dk_inter)
            across = decay * (jnp.sum(dc_out * c_b.astype(F32), keepdims=True) + jnp.sum(dn_out * n_old, keepdims=True))
            dli_all = dli_all + jnp.where(lane == h, p_cols + k_in, 0.0)
            from_later = from_later + jnp.where(lane == heads + h, p_rows - p_cols + q_in, 0.0)
            from_earlier = from_earlier + jnp.where(lane == heads + h, k_in, 0.0)
            across_all = across_all + jnp.where(lane[0:1] == heads + h, across, 0.0)
            dqs.append(dq)
            dks.append(dk * kscale)
            dvs.append(dv)
        for h in range(heads):
            dcs[h], dns[h] = new[h]
        du_ref[0] = jnp.concatenate(d_o, axis=1)
        du_ref[1] = jnp.concatenate(d_z, axis=1)
        dg_ref[...] += jnp.concatenate(d_g, axis=1)
        dlf = _tri_dot_left(triu, from_later) + _tri_dot_left(tril_strict, from_earlier) + across_all
        dgt = dli_all + dlf * _sigmoid(-gtv)
        dgt_ref[...] = dgt
        dbif_ref[...] += _colsum(dgt)
        dgb = _bf(dgt)
        dqkv_ref[0] = _bf(jnp.concatenate(dqs, axis=1) + _dot_nt(dgb, wif_ref[0:d, :]))
        dqkv_ref[1] = _bf(jnp.concatenate(dks, axis=1) + _dot_nt(dgb, wif_ref[d:2 * d, :]))
        dqkv_ref[2] = _bf(jnp.concatenate(dvs, axis=1) + _dot_nt(dgb, wif_ref[2 * d:3 * d, :]))

    rev = lambda c: nc - 1 - c
    row = pl.BlockSpec((lc, d), lambda c: (rev(c), 0))
    gcol = pl.BlockSpec((lc, ng), lambda c: (rev(c), 0))
    grow = pl.BlockSpec((ng, lc), lambda c: (0, rev(c)))
    return _pcall_ride(
        body, ride, name="mlstm_bwd", grid=(nc,),
        in_specs=[pl.BlockSpec((3, lc, d), lambda c: (0, rev(c), 0)), gcol, grow, gcol, grow,
                  pl.BlockSpec((1, heads, hd, hd), lambda c: (rev(c), 0, 0, 0)),
                  pl.BlockSpec((1, heads, 1, hd), lambda c: (rev(c), 0, 0, 0)),
                  pl.BlockSpec((1, heads, 1, 128), lambda c: (rev(c), 0, 0, 0)),
                  row, pl.BlockSpec((lc, d), lambda c: (rev(c), 3)), pl.BlockSpec((lc, d), lambda c: (rev(c), 4)),
                  pl.BlockSpec((1, d), lambda c: (0, 0)), pl.BlockSpec((1, lc, d), lambda c: (1, rev(c), 0)),
                  pl.BlockSpec((3 * d, ng), lambda c: (0, 0))],
        out_specs=[pl.BlockSpec((3, lc, d), lambda c: (0, rev(c), 0)), pl.BlockSpec((lc, ng), lambda c: (rev(c), 0)),
                   pl.BlockSpec((1, ng), lambda c: (0, 0)), pl.BlockSpec((2, lc, d), lambda c: (0, rev(c), 0)),
                   pl.BlockSpec((1, d), lambda c: (0, 0))],
        out_shape=[jax.ShapeDtypeStruct((3, s_len, d), BF16), jax.ShapeDtypeStruct((s_len, ng), F32),
                   jax.ShapeDtypeStruct((1, ng), F32), jax.ShapeDtypeStruct((5, s_len, d), BF16),
                   jax.ShapeDtypeStruct((1, d), F32)],
        scratch_shapes=[pltpu.VMEM((heads, hd, hd), F32), pltpu.VMEM((heads, 1, hd), F32)],
        compiler_params=_seq(),
        args=(qkv, *gates, cst, nst, mst, cell, u, u, ml_g, d_ycat, wif_b))


def _conv_bwd_tile(dp, later, taps, cw_ref, gw_ref, gb_ref):
    tm = dp.shape[0]
    dwin = jnp.concatenate([dp, later[...]], axis=0)
    later[...] = dp[0:HALO]
    acc = cw_ref[CONV_WIDTH - 1:CONV_WIDTH, :] * dp
    for k in range(CONV_WIDTH):
        if k < CONV_WIDTH - 1:
            acc = acc + cw_ref[k:k + 1, :] * _shift_up(dwin, CONV_WIDTH - 1 - k)[0:tm]
        gw_ref[k:k + 1, :] += _colsum(dp * taps[k])
    gb_ref[...] += _colsum(dp)
    return acc


def _ml_pre_bwd(dqkv, u, conv_w, conv_b, wqkv_b, du):
    s_len = u.shape[0]
    d = conv_w.shape[1]
    _, heads, hd, _ = wqkv_b.shape
    tm = _tile(s_len, 256)
    per = tm // HALO
    nt = s_len // tm

    def body(dqkv_ref, x_ref, xp_ref, cw_ref, cb_ref, w_ref, _, dx_ref, gw_ref, gcw_ref, gcb_ref, later, dps, dxs):
        i = pl.program_id(0)

        @pl.when(i == 0)
        def _():
            gw_ref[...] = jnp.zeros_like(gw_ref)
            gcw_ref[...] = jnp.zeros_like(gcw_ref)
            gcb_ref[...] = jnp.zeros_like(gcb_ref)
            later[...] = jnp.zeros_like(later)

        prev = jnp.where(i == nt - 1, 0.0, xp_ref[...])
        xm = x_ref[...]
        taps = _conv_taps(jnp.concatenate([prev, xm], axis=0))
        pre = _conv_fwd(taps, cw_ref, cb_ref)
        sg = _sigmoid(pre)
        xcb = _bf(pre * sg)
        xmb = _bf(xm)
        for h in range(heads):
            hs = slice(h * hd, (h + 1) * hd)
            dqh, dkh, dvh = dqkv_ref[0, :, hs], dqkv_ref[1, :, hs], dqkv_ref[2, :, hs]
            dxc = _dot_nt(dqh, w_ref[0, h]) + _dot_nt(dkh, w_ref[1, h])
            dps[:, hs] = dxc * _dsilu(pre[:, hs], sg[:, hs])
            dxs[:, hs] = _dot_nt(dvh, w_ref[2, h])
            gw_ref[0, h] += _dot_tn(xcb[:, hs], dqh)
            gw_ref[1, h] += _dot_tn(xcb[:, hs], dkh)
            gw_ref[2, h] += _dot_tn(xmb[:, hs], dvh)
        dx_ref[0] = _bf(_conv_bwd_tile(dps[...], later, taps, cw_ref, gcw_ref, gcb_ref) + dxs[...])

    rev = lambda i: nt - 1 - i
    vec = pl.BlockSpec((1, d), lambda i: (0, 0))
    cwb = pl.BlockSpec((CONV_WIDTH, d), lambda i: (0, 0))
    whole4 = pl.BlockSpec(wqkv_b.shape, lambda i: (0, 0, 0, 0))
    return _pcall(
        body, name="ml_pre_bwd", grid=(nt,),
        in_specs=[pl.BlockSpec((3, tm, d), lambda i: (0, rev(i), 0)), pl.BlockSpec((tm, d), lambda i: (rev(i), 2)),
                  pl.BlockSpec((HALO, d), lambda i: (jnp.maximum(rev(i) * per - 1, 0), 2)),
                  cwb, vec, whole4, pl.BlockSpec(memory_space=pl.ANY)],
        out_specs=[pl.BlockSpec((1, tm, d), lambda i: (DU_PLANE[2], rev(i), 0)), whole4, cwb, vec],
        out_shape=[jax.ShapeDtypeStruct(du.shape, BF16), jax.ShapeDtypeStruct(wqkv_b.shape, F32),
                   jax.ShapeDtypeStruct((CONV_WIDTH, d), F32), jax.ShapeDtypeStruct((1, d), F32)],
        scratch_shapes=[pltpu.VMEM((HALO, d), F32), pltpu.VMEM((tm, d), F32), pltpu.VMEM((tm, d), F32)],
        input_output_aliases={6: 0},
        compiler_params=_seq(),
    )(dqkv, u, u, conv_w, conv_b, wqkv_b, du)


def _rg_bwd(d_ycat, u, hh, conv_w, conv_b, wa_b, ba, wx_b, bx, lam, du):
    s_len = u.shape[0]
    d = conv_w.shape[1]
    heads, hd, _ = wa_b.shape
    tm = _tile(s_len, 256)
    per = tm // HALO
    nt = s_len // tm

    def body(dy_ref, x_ref, xp_ref, z_ref, hh_ref, hp_ref, cw_ref, cb_ref, wa_ref, ba_ref, wx_ref, bx_ref, lam_ref, _,
             du_ref, gwa_ref, gwx_ref, gba_ref, gbx_ref, glam_ref, gcw_ref, gcb_ref, carry, gbuf, later, dxcs):
        i = pl.program_id(0)
        first = i == nt - 1

        @pl.when(i == 0)
        def _():
            carry[...] = jnp.zeros_like(carry)
            later[...] = jnp.zeros_like(later)
            gwa_ref[...] = jnp.zeros_like(gwa_ref)
            gwx_ref[...] = jnp.zeros_like(gwx_ref)
            gba_ref[...] = jnp.zeros_like(gba_ref)
            gbx_ref[...] = jnp.zeros_like(gbx_ref)
            glam_ref[...] = jnp.zeros_like(glam_ref)
            gcw_ref[...] = jnp.zeros_like(gcw_ref)
            gcb_ref[...] = jnp.zeros_like(gcb_ref)

        prev = jnp.where(first, 0.0, xp_ref[...])
        taps = _conv_taps(jnp.concatenate([prev, x_ref[...]], axis=0))
        xc = _conv_fwd(taps, cw_ref, cb_ref)
        r, ig, sp, log_a, a, mult = _rg_gates(xc, wa_ref, ba_ref, wx_ref, bx_ref, lam_ref)
        z = z_ref[...]
        sgz = _sigmoid(z)
        dy = dy_ref[0]
        hh_v = hh_ref[...]
        du_ref[1] = _bf(dy * hh_v * _dsilu(z, sgz))
        dhh = dy * (z * sgz)
        rows = lax.broadcasted_iota(jnp.int32, a.shape, 0)
        coef = jnp.where(rows == tm - 1, carry[1:2, :], _shift_up(a, 1))
        ca, cu = _scan_groups(coef, dhh, reverse=True)
        c = carry[0:1, :]
        for j in range(per - 1, -1, -1):
            blk = ca[j * 8:(j + 1) * 8] * c + cu[j * 8:(j + 1) * 8]
            gbuf[j * 8:(j + 1) * 8, :] = blk
            c = blk[0:1]
        carry[0:1, :] = c
        carry[1:2, :] = a[0:1]
        g = gbuf[...]
        hprev_tile = jnp.where(first, 0.0, hp_ref[...])
        hprev = _shift_down(jnp.concatenate([hprev_tile, hh_v], axis=0), 1)[HALO:]
        da = g * hprev
        gx_ = g * xc
        d_mult = gx_ * ig
        d_ig = gx_ * mult
        dxc = g * mult * ig
        dlog_a = da * a - d_mult * (a * a / mult)
        d_r = dlog_a * ((-RG_C) * sp)
        glam_ref[...] += _colsum(dlog_a * ((-RG_C) * r)) * (-_sigmoid(-lam_ref[...]))
        d_ga = d_r * r * (1.0 - r)
        d_gx = d_ig * ig * (1.0 - ig)
        gba_ref[...] += _colsum(d_ga)
        gbx_ref[...] += _colsum(d_gx)
        xb = _bf(xc)
        dgab = _bf(d_ga)
        dgxb = _bf(d_gx)
        for h in range(heads):
            hs = slice(h * hd, (h + 1) * hd)
            dxcs[:, hs] = dxc[:, hs] + _dot_nt(dgab[:, hs], wa_ref[h]) + _dot_nt(dgxb[:, hs], wx_ref[h])
            gwa_ref[h] += _dot_tn(xb[:, hs], dgab[:, hs])
            gwx_ref[h] += _dot_tn(xb[:, hs], dgxb[:, hs])
        du_ref[0] = _bf(_conv_bwd_tile(dxcs[...], later, taps, cw_ref, gcw_ref, gcb_ref))

    assert DU_PLANE[0] % 2 == 0 and DU_PLANE[1] == DU_PLANE[0] + 1
    rev = lambda i: nt - 1 - i
    row = pl.BlockSpec((tm, d), lambda i: (rev(i), 0))
    halo_prev = lambda col: pl.BlockSpec((HALO, d), lambda i: (jnp.maximum(rev(i) * per - 1, 0), col))
    vec = pl.BlockSpec((1, d), lambda i: (0, 0))
    cwb = pl.BlockSpec((CONV_WIDTH, d), lambda i: (0, 0))
    whole3 = lambda a: pl.BlockSpec(a.shape, lambda i: (0, 0, 0))
    return _pcall(
        body, name="rg_bwd", grid=(nt,),
        in_specs=[pl.BlockSpec((1, tm, d), lambda i: (0, rev(i), 0)), row, halo_prev(0),
                  pl.BlockSpec((tm, d), lambda i: (rev(i), 1)), row, halo_prev(0),
                  cwb, vec, whole3(wa_b), vec, whole3(wx_b), vec, vec, pl.BlockSpec(memory_space=pl.ANY)],
        out_specs=[pl.BlockSpec((2, tm, d), lambda i: (DU_PLANE[0] // 2, rev(i), 0)), whole3(wa_b), whole3(wa_b),
                   vec, vec, vec, cwb, vec],
        out_shape=[jax.ShapeDtypeStruct(du.shape, BF16), jax.ShapeDtypeStruct(wa_b.shape, F32),
                   jax.ShapeDtypeStruct(wa_b.shape, F32)] + [jax.ShapeDtypeStruct((1, d), F32)] * 3
        + [jax.ShapeDtypeStruct((CONV_WIDTH, d), F32), jax.ShapeDtypeStruct((1, d), F32)],
        scratch_shapes=[pltpu.VMEM((8, d), F32), pltpu.VMEM((tm, d), F32), pltpu.VMEM((HALO, d), F32),
                        pltpu.VMEM((tm, d), F32)],
        input_output_aliases={13: 0},
        compiler_params=_seq(),
    )(d_ycat, u, u, u, hh, hh, conv_w, conv_b, wa_b, ba, wx_b, bx, lam, du)


def _in_bwd(du, w4, x, dxn, g, scale, ride=None):
    s_len, d = x.shape
    tm = _tile(s_len, 512)
    nsh_chips, _, nsh = w4.shape
    npc = du.shape[0]
    ck = d // 4
    assert nsh % ck == 0 and npc * d == nsh_chips * nsh

    def body(du_ref, w_ref, x_ref, dxn_ref, g_ref, sc_ref, dx_ref, dsh_ref, dsc_ref, dg_ref):
        @pl.when(pl.program_id(0) == 0)
        def _():
            dsh_ref[...] = jnp.zeros_like(dsh_ref)
            dsc_ref[...] = jnp.zeros_like(dsc_ref)
            dg_ref[...] = jnp.zeros_like(dg_ref)

        dh = None
        for q in range(npc * d // ck):
            col = q * ck
            p, pc = col // d, col % d
            s, sc = col // nsh, col % nsh
            t = _dot_nt(du_ref[DU_PLANE[p], :, pc:pc + ck], w_ref[s, :, sc:sc + ck])
            dh = t if dh is None else dh + t
        xv = x_ref[...]
        r = lax.rsqrt(jnp.mean(xv * xv, axis=-1, keepdims=True) + EPS)
        xn = xv * r
        gv = g_ref[...]
        onesc = 1.0 + sc_ref[...]
        dsh_ref[...] += _colsum(dh)
        dsc_ref[...] += _colsum(dh * (xn * gv))
        dg_ref[...] += _colsum(dh * xn * onesc)
        dxh = dh * (gv * onesc)
        dx_ref[...] = dxn_ref[...] + r * (dxh - xn * jnp.mean(dxh * xn, axis=-1, keepdims=True))

    row = pl.BlockSpec((tm, d), lambda i: (i, 0))
    vec = pl.BlockSpec((1, d), lambda i: (0, 0))
    return _pcall_ride(
        body, ride, name="in_bwd", grid=(s_len // tm,),
        in_specs=[pl.BlockSpec((npc, tm, d), lambda i: (0, i, 0)), pl.BlockSpec(w4.shape, lambda i: (0, 0, 0)), row, row,
                  vec, vec],
        out_specs=[row, vec, vec, vec],
        out_shape=[jax.ShapeDtypeStruct((s_len, d), F32)] + [jax.ShapeDtypeStruct((1, d), F32)] * 3,
        compiler_params=_seq(),
        args=(du, w4, x, dxn, g, scale))


def _layer_fwd(x, p, rides=None, late=None):
    rides = rides or {}
    (h_b, u), got = _ln_inproj(x, p["norm_g"], p["scale"], p["shift"], p["w4"], rides.get("ln_inproj"))
    if late is not None:
        p = {**p, **late(got)}
    (hh, ycat), got_a = _rg_fwd(u, p["rg_conv_w"], p["rg_conv_b"], p["rg_wa_b"], p["rg_ba"], p["rg_wx_b"], p["rg_bx"],
                                p["rg_lam"], rides.get("rg_fwd"))
    qkv, *gates = _ml_pre(u, p["ml_conv_w"], p["ml_conv_b"], p["wqkv_b"], p["wif_b"], p["wift_b"], p["b_if"],
                          p["b_ift"])
    (cell, ycat, cst, nst, mst), got_b = _mlstm_fwd(qkv, gates, u, p["ml_g"], ycat, rides.get("mlstm_fwd"))
    y, x_new = _out_proj(ycat, p["w_out_b"], x, p["gate"])
    saved = dict(x=x, h_b=h_b, u=u, hh=hh, qkv=qkv, gates=gates, cell=cell, ycat=ycat, cst=cst, nst=nst, mst=mst, y=y)
    return x_new, saved, p, dict(rg_fwd=got_a, mlstm_fwd=got_b)


def _layer_bwd(dxn, p, s, rides=None):
    rides = rides or {}
    u = s["u"]
    d = dxn.shape[1]
    d_gate, dy_b, d_ycat = _out_bwd(dxn, s["y"], p["gate"], p["w_out_b"])
    g_w_out = _grad_matmul(s["ycat"], dy_b[None], 2, lambda b: b, lambda b: 0, (2 * d, d), (d, d), lambda b: (b, 0),
                           0, 1)
    (dqkv, dgt, g_b_if, du, g_ml_g), got = _mlstm_bwd(s["qkv"], s["gates"], s["cst"], s["nst"], s["mst"], s["cell"], u,
                                                      p["ml_g"], d_ycat, p["wif_b"], rides.get("mlstm_bwd"))
    ng = dgt.shape[1]
    g_w_if = _grad_matmul(s["qkv"], _bf(dgt)[None], 3, lambda b: b, lambda b: 0, (3 * d, ng), (d, ng),
                          lambda b: (b, 0), 0, 1)[0]
    du, g_wqkv, g_ml_cw, g_ml_cb = _ml_pre_bwd(dqkv, u, p["ml_conv_w"], p["ml_conv_b"], p["wqkv_b"], du)
    du, g_wa, g_wx, g_ba, g_bx, g_lam, g_rg_cw, g_rg_cb = _rg_bwd(d_ycat, u, s["hh"], p["rg_conv_w"], p["rg_conv_b"],
                                                                  p["rg_wa_b"], p["rg_ba"], p["rg_wx_b"], p["rg_bx"],
                                                                  p["rg_lam"], du)
    npc = du.shape[0]
    g_w_in = _grad_matmul(s["h_b"][None], du, npc, lambda b: 0, lambda b: (b + DU_PLANE[0]) % npc, (d, npc * d),
                          (d, d), lambda b: (0, b), 0, 1)
    in_ride = rides["in_bwd"](g_w_in, g_w_out) if "in_bwd" in rides else None
    (dx, d_shift, d_scale, g_norm_g), got_in = _in_bwd(du, p["w4"], s["x"], dxn, p["norm_g"], p["scale"], in_ride)
    grads = dict(norm_g=g_norm_g, w_in=g_w_in, rg_conv_w=g_rg_cw, rg_conv_b=g_rg_cb, rg_w_a=g_wa, rg_b_a=g_ba,
                 rg_w_x=g_wx, rg_b_x=g_bx, rg_lambda=g_lam, ml_conv_w=g_ml_cw, ml_conv_b=g_ml_cb, ml_w_qkv=g_wqkv,
                 ml_w_if=g_w_if, ml_b_if=g_b_if, ml_norm_g=g_ml_g, w_out=g_w_out)
    return dx, grads, jnp.concatenate([d_shift, d_scale, d_gate], axis=1), dict(mlstm_bwd=got, in_bwd=got_in)


def _trunk_fwd_bwd(x, target, final_g, layers):
    saved = []
    for p in layers:
        x, s, _, _ = _layer_fwd(x, p)
        saved.append(s)
    dx, g_final, loss = _final_loss(x, final_g, target)
    grads, dmods = [], []
    for layer in reversed(range(len(layers))):
        dx, g, dm, _ = _layer_bwd(dx, layers[layer], saved[layer])
        grads.append(g)
        dmods.append(dm)
    return loss, dx, g_final, grads[::-1], dmods[::-1]


def _me():
    return lax.axis_index("x"), lax.axis_index("y"), lax.axis_index("c")


def _remote(src, dst, send_sem, recv_sem, to):
    return pltpu.make_async_remote_copy(src_ref=src, dst_ref=dst, send_sem=send_sem, recv_sem=recv_sem,
                                        device_id=to, device_id_type=MESH)


def _all_gather8(blocks, space):
    n = len(blocks)

    def body(*refs):
        x_refs, out_refs = refs[:n], refs[n:2 * n]
        send_sems, recv_sems, local_sems = refs[2 * n:]
        x, y, c = _me()
        me, sibling = (x, y, c), (x, y, 1 - c)
        chips = [(1 - x, y), (x, 1 - y), (1 - x, 1 - y)]

        def rows(i, px, py, pc):
            m_per = blocks[i].shape[0]
            return out_refs[i].at[pl.ds((4 * px + 2 * py + pc) * m_per, m_per), :]

        def copy(i, k, blk, to, src=None):
            return _remote(rows(i, *blk) if src is None else src, rows(i, *blk), send_sems.at[7 * i + k],
                           recv_sems.at[7 * i + k], to)

        mine = [pltpu.make_async_copy(x_refs[i], rows(i, *me), local_sems.at[i]) for i in range(n)]
        first = []
        for i in range(n):
            first.append(copy(i, 0, me, sibling, src=x_refs[i]))
            first += [copy(i, 1 + j, me, (*chip, c), src=x_refs[i]) for j, chip in enumerate(chips)]
        for cp in mine + first:
            cp.start()
        passed = []
        for j, chip in enumerate(chips):
            for i in range(n):
                copy(i, 1 + j, (*chip, c), me).wait_recv()
                passed.append(copy(i, 4 + j, (*chip, c), sibling))
                passed[-1].start()
        for i in range(n):
            copy(i, 0, sibling, me).wait_recv()
            for j, chip in enumerate(chips):
                copy(i, 4 + j, (*chip, 1 - c), me).wait_recv()
        for cp in first + passed:
            cp.wait_send()
        for cp in mine:
            cp.wait()

    spec = pl.BlockSpec(memory_space=space)
    return _pcall(
        body, name="all_gather8",
        out_shape=[jax.ShapeDtypeStruct((8 * b.shape[0], b.shape[1]), b.dtype) for b in blocks],
        in_specs=[spec] * n, out_specs=[spec] * n,
        scratch_shapes=[pltpu.SemaphoreType.DMA((7 * n,)), pltpu.SemaphoreType.DMA((7 * n,)),
                        pltpu.SemaphoreType.DMA((n,))],
    )(*blocks)


def _sib_halves(g_in, g_out):
    depth, d, n4 = g_in.shape
    n = n4 // 4

    def body(gi, go, ri, ro, send_sems, recv_sems):
        x, y, c = _me()
        o = 1 - c
        pairs = [(gi.at[pl.ds(0, depth), pl.ds(o * (d // 2), d // 2), pl.ds(s * n, n)], ri.at[pl.ds(0, depth), s])
                 for s in range(4)]
        pairs.append((go.at[pl.ds(0, depth), pl.ds(0, 4), o], ro))
        copies = [_remote(src, dst, send_sems.at[k], recv_sems.at[k], (x, y, o)) for k, (src, dst) in enumerate(pairs)]
        for cp in copies:
            cp.start()
        for cp in copies:
            cp.wait_recv()
        for cp in copies:
            cp.wait_send()

    hbm = pl.BlockSpec(memory_space=pltpu.HBM)
    return _pcall(
        body, name="sib_halves",
        out_shape=[jax.ShapeDtypeStruct((depth, 4, d // 2, n), g_in.dtype),
                   jax.ShapeDtypeStruct(g_out.shape[:2] + g_out.shape[3:], g_out.dtype)],
        in_specs=[hbm] * 2, out_specs=[hbm] * 2,
        scratch_shapes=[pltpu.SemaphoreType.DMA((5,)), pltpu.SemaphoreType.DMA((5,))],
    )(g_in, g_out)


def _sib_slabs(slabs):
    ns = len(slabs)

    def body(*refs):
        sl, rs = refs[:ns], refs[ns:2 * ns]
        send_sems, recv_sems = refs[2 * ns:]
        x, y, c = _me()
        copies = [_remote(sl[i].at[1 - c], rs[i], send_sems.at[i], recv_sems.at[i], (x, y, 1 - c)) for i in range(ns)]
        for cp in copies:
            cp.start()
        for cp in copies:
            cp.wait_recv()
        for cp in copies:
            cp.wait_send()

    hbm = pl.BlockSpec(memory_space=pltpu.HBM)
    return _pcall(
        body, name="sib_slabs",
        out_shape=[jax.ShapeDtypeStruct(s.shape[1:], s.dtype) for s in slabs],
        in_specs=[hbm] * ns, out_specs=[hbm] * ns,
        scratch_shapes=[pltpu.SemaphoreType.DMA((ns,)), pltpu.SemaphoreType.DMA((ns,))],
    )(*slabs)


def _sib_fill(boths):
    n = len(boths)

    def body(*refs):
        dst = refs[n:2 * n]
        send_sems, recv_sems = refs[2 * n:]
        x, y, c = _me()
        view = lambda i: dst[i].at[pl.ds(0, boths[i].shape[0]), c]
        copies = [_remote(view(i), view(i), send_sems.at[i], recv_sems.at[i], (x, y, 1 - c)) for i in range(n)]
        for cp in copies:
            cp.start()
        for cp in copies:
            cp.wait_recv()
        for cp in copies:
            cp.wait_send()

    hbm = pl.BlockSpec(memory_space=pltpu.HBM)
    return _pcall(
        body, name="sib_fill",
        out_shape=[jax.ShapeDtypeStruct(b.shape, b.dtype) for b in boths],
        in_specs=[hbm] * n, out_specs=[hbm] * n, input_output_aliases={i: i for i in range(n)},
        scratch_shapes=[pltpu.SemaphoreType.DMA((n,)), pltpu.SemaphoreType.DMA((n,))],
    )(*boths)


def _chip_exchange(arrs):
    n = len(arrs)

    def body(*refs):
        src, dst = refs[:n], refs[n:2 * n]
        send_sems, recv_sems = refs[2 * n:]
        x, y, c = _me()
        me_s = 2 * x + y
        chips = [(1 - x, y), (x, 1 - y), (1 - x, 1 - y)]
        copies = [_remote(src[i].at[2 * px + py], dst[i].at[me_s], send_sems.at[3 * i + k], recv_sems.at[3 * i + k],
                          (px, py, c))
                  for i in range(n) for k, (px, py) in enumerate(chips)]
        for cp in copies:
            cp.start()
        for cp in copies:
            cp.wait_recv()
        for cp in copies:
            cp.wait_send()

    hbm = pl.BlockSpec(memory_space=pltpu.HBM)
    return _pcall(
        body, name="chip_exchange",
        out_shape=[jax.ShapeDtypeStruct(a.shape, a.dtype) for a in arrs],
        in_specs=[hbm] * n, out_specs=[hbm] * n,
        scratch_shapes=[pltpu.SemaphoreType.DMA((3 * n,)), pltpu.SemaphoreType.DMA((3 * n,))],
    )(*arrs)


def _row_tile(rows, cap=4096, mult=16):
    best = None
    for t in range(mult, min(rows, cap) + 1, mult):
        if rows % t == 0:
            best = t
    return rows if best is None else best


def _pair_sum(half, own, own_spec, got, got_spec, out_shape, out_spec, grid):
    def body(_, a_ref, b_ref, o_ref):
        o_ref[...] = (a_ref[...] + b_ref[...].astype(F32)).astype(o_ref.dtype)

    return _pcall(
        body, name="pair_sum",
        grid_spec=pltpu.PrefetchScalarGridSpec(num_scalar_prefetch=1, grid=grid, in_specs=[own_spec, got_spec],
                                               out_specs=out_spec),
        out_shape=out_shape, compiler_params=_seq(len(grid)))(half, own, got)


def _chip_sum(ids, part, met, fill, layer=0, stack=1):
    _, _, rows, n = part.shape
    tr = _row_tile(rows, cap=max(16, (1 << 18) // n))
    first = isinstance(stack, int)

    def body(_, own_ref, a_ref, b_ref, c_ref, *rest):
        acc = own_ref[...].astype(F32) + a_ref[...].astype(F32)
        acc = acc + b_ref[...].astype(F32)
        rest[-1][...] = acc + c_ref[...].astype(F32)

    blk = (None, None, tr, n)
    other = lambda k: pl.BlockSpec(blk, lambda j, ids: ((ids[0] + k) % 4, 0, j, 0))
    in_specs = [pl.BlockSpec(blk, lambda j, ids: (ids[0], 0, j, 0)), other(1), other(2), other(3)]
    return _pcall(
        body, name="chip_sum",
        grid_spec=pltpu.PrefetchScalarGridSpec(
            num_scalar_prefetch=1, grid=(rows // tr,),
            in_specs=in_specs if first else in_specs + [pl.BlockSpec(memory_space=pl.ANY)],
            out_specs=pl.BlockSpec(blk, lambda j, ids: (layer, ids[1] if fill else 0, j, 0))),
        out_shape=jax.ShapeDtypeStruct(((stack,) if first else stack.shape[:1]) + (2 if fill else 1, rows, n), F32),
        input_output_aliases={} if first else {5: 0},
        compiler_params=_seq())(*((ids, part, met, met, met) if first else (ids, part, met, met, met, stack)))


def _ada_mod(c_all, w_ada, b_ada_cols):
    depth, d, n = w_ada.shape
    nb = c_all.shape[0]

    def body(c_ref, w_ref, b_ref, o_ref):
        cv = c_ref[...]
        ca = _bf(cv * _sigmoid(cv))
        o_ref[0] = _dot(ca, _bf(w_ref[0])) + b_ref[0]

    return _pcall(body, name="ada_mod", grid=(depth,),
                  in_specs=[pl.BlockSpec((nb, d), lambda l: (0, 0)), pl.BlockSpec((1, d, n), lambda l: (l, 0, 0)),
                            pl.BlockSpec((1, 1, n), lambda l: (l, 0, 0))],
                  out_specs=pl.BlockSpec((1, nb, n), lambda l: (l, 0, 0)),
                  out_shape=jax.ShapeDtypeStruct((depth, nb, n), F32), compiler_params=_seq())(c_all, w_ada, b_ada_cols)


def _ada_grad(c_all, dmod_cols, dmod_all):
    nb, d = c_all.shape
    depth, _, n = dmod_cols.shape
    n_all = dmod_all.shape[2]

    def body(c_ref, dm_ref, da_ref, gw_ref, gb_ref):
        cv = c_ref[...]
        ca = _bf(cv * _sigmoid(cv))
        gw_ref[0] = _dot_tn(ca, _bf(dm_ref[0]))
        gb_ref[0] = _colsum(da_ref[0])

    return _pcall(body, name="ada_grad", grid=(depth,),
                  in_specs=[pl.BlockSpec((nb, d), lambda l: (0, 0)), pl.BlockSpec((1, nb, n), lambda l: (l, 0, 0)),
                            pl.BlockSpec((1, nb, n_all), lambda l: (l, 0, 0))],
                  out_specs=[pl.BlockSpec((1, d, n), lambda l: (l, 0, 0)), pl.BlockSpec((1, 1, n_all), lambda l: (l, 0, 0))],
                  out_shape=[jax.ShapeDtypeStruct((depth, d, n), F32), jax.ShapeDtypeStruct((depth, 1, n_all), F32)],
                  compiler_params=_seq())(c_all, dmod_cols, dmod_all)


def _adamw(items, ride=None):
    two_d = [tuple(t.reshape(w.size // w.shape[-1], w.shape[-1]) for t in (w, g, m, v)) for w, g, m, v in items]
    n = len(items)
    if n == 1:
        rows, cols = two_d[0][0].shape
        tr = _row_tile(rows, cap=max(8, (1 << 18) // cols), mult=8)
        blocks = [pl.BlockSpec((tr, cols), lambda i: (i, 0))]
        grid = (rows // tr,)
    else:
        blocks = [pl.BlockSpec(t[0].shape, lambda i: (0, 0)) for t in two_d]
        grid = (1,)

    def body(*refs):
        for k in range(n):
            w_ref, g_ref, m_ref, v_ref = refs[4 * k:4 * k + 4]
            d_ref, mo_ref, vo_ref = refs[4 * n + 3 * k:4 * n + 3 * k + 3]
            gv = g_ref[...]
            mn = ADAM_B1 * m_ref[...] + (1.0 - ADAM_B1) * gv
            vn = ADAM_B2 * v_ref[...] + (1.0 - ADAM_B2) * (gv * gv)
            m_hat = mn / (1.0 - ADAM_B1 ** ADAM_STEP)
            v_hat = vn / (1.0 - ADAM_B2 ** ADAM_STEP)
            d_ref[...] = -ADAM_LR * (m_hat / (jnp.sqrt(v_hat) + ADAM_EPS) + ADAM_WD * w_ref[...])
            mo_ref[...] = mn
            vo_ref[...] = vn

    outs, got = _pcall_ride(
        body, ride, name="adamw", grid=grid,
        in_specs=[b for b in blocks for _ in range(4)], out_specs=[b for b in blocks for _ in range(3)],
        out_shape=[jax.ShapeDtypeStruct(t[0].shape, F32) for t in two_d for _ in range(3)],
        compiler_params=_seq(), args=tuple(a for t in two_d for a in t))
    return [tuple(o.reshape(items[k][0].shape) for o in outs[3 * k:3 * k + 3]) for k in range(n)], got


WEIGHTS = ["norm_g", "w_ada", "b_ada", "w_in", "rg_conv_w", "rg_conv_b", "rg_w_a", "rg_b_a", "rg_w_x", "rg_b_x",
           "rg_lambda", "ml_conv_w", "ml_conv_b", "ml_w_q", "ml_w_k", "ml_w_v", "ml_w_if", "ml_b_if", "ml_norm_g",
           "w_out", "final_g"]
SMALL_SHARDED = {"ml_w_qkv": 2, "rg_conv_w": 1, "ml_conv_w": 1, "ml_w_if": 0}
REPLICATED = ["rg_w_a", "rg_w_x", "norm_g", "rg_conv_b", "rg_b_a", "rg_b_x", "rg_lambda", "ml_conv_b", "ml_norm_g",
              "ml_b_if"]
LANES = 128


def _to_pieces(g, axis):
    shp = g.shape
    g = g.reshape(shp[:axis] + (4, 2, shp[axis] // 8) + shp[axis + 1:])
    g = jnp.moveaxis(g, (axis, axis + 1), (0, 1))
    return g.reshape(4, 2, -1)


def _from_pieces(p, shard_shape, axis):
    k = p.shape[0]
    rest = shard_shape[:axis] + (shard_shape[axis] // k,) + shard_shape[axis + 1:]
    t = jnp.moveaxis(p.reshape((k,) + rest), 0, axis)
    return t.reshape(shard_shape)


def _pad_rows(flat, mult):
    n = flat.shape[-1]
    pad = (-n) % mult
    if pad:
        flat = jnp.concatenate([flat, jnp.zeros(flat.shape[:-1] + (pad,), flat.dtype)], axis=-1)
    return flat


def kernel(x, c, norm_g, w_ada, b_ada, w_in, rg_conv_w, rg_conv_b, rg_w_a, rg_b_a, rg_w_x, rg_b_x, rg_lambda, ml_conv_w, ml_conv_b, ml_w_q, ml_w_k, ml_w_v, ml_w_if, ml_b_if, ml_norm_g, w_out, final_g, loss_target, m_norm_g, m_w_ada, m_b_ada, m_w_in, m_rg_conv_w, m_rg_conv_b, m_rg_w_a, m_rg_b_a, m_rg_w_x, m_rg_b_x, m_rg_lambda, m_ml_conv_w, m_ml_conv_b, m_ml_w_q, m_ml_w_k, m_ml_w_v, m_ml_w_if, m_ml_b_if, m_ml_norm_g, m_w_out, m_final_g, v_norm_g, v_w_ada, v_b_ada, v_w_in, v_rg_conv_w, v_rg_conv_b, v_rg_w_a, v_rg_b_a, v_rg_w_x, v_rg_b_x, v_rg_lambda, v_ml_conv_w, v_ml_conv_b, v_ml_w_q, v_ml_w_k, v_ml_w_v, v_ml_w_if, v_ml_b_if, v_ml_norm_g, v_w_out, v_final_g):
    given = dict(locals())
    ax, ay, ac = lax.axis_index("x"), lax.axis_index("y"), lax.axis_index("c")
    chip = 2 * ax + ay
    me = 2 * chip + ac
    depth, d = norm_g.shape
    n_ada = w_ada.shape[2]
    pick = lambda a, i, axis=0: lax.dynamic_index_in_dim(a, i, axis, keepdims=False)

    convs = jnp.stack([rg_conv_w, ml_conv_w])
    n_conv = 2 * depth * CONV_WIDTH // 4
    blk = jnp.concatenate([c, convs.reshape(n_conv, d), jnp.zeros((8 - 1 - n_conv, d), F32)], axis=0)
    g0 = _all_gather8([blk], pltpu.VMEM)[0].reshape(8, 8, d)
    c_all = g0[:, 0, :]
    conv_full = g0[0::2, 1:1 + n_conv].reshape(4, 2, depth, CONV_WIDTH, d // 4)
    conv_full = conv_full.transpose(1, 2, 3, 0, 4).reshape(2, depth, CONV_WIDTH, d)

    b_cols = lax.dynamic_slice_in_dim(b_ada, chip * n_ada, n_ada, axis=1)[:, None, :]
    mod_part = _ada_mod(c_all, w_ada, b_cols)
    g1 = _all_gather8([mod_part.transpose(1, 0, 2).reshape(8, depth * n_ada)], pltpu.VMEM)[0]
    g1 = g1.reshape(8, 8, depth, n_ada)[0::2]
    mod_me = pick(g1.transpose(1, 2, 0, 3).reshape(8, depth, 4 * n_ada), me)

    def half_of(w, axis):
        n = w.shape[axis] // 2
        return lax.dynamic_slice_in_dim(w, ac * n, n, axis).astype(BF16)

    n_sh = w_in.shape[2]
    heads, hd_cut, hd = ml_w_q.shape[1:]

    def blocks_of(l):
        wqkv = jnp.stack([ml_w_q[l], ml_w_k[l], ml_w_v[l]])
        return [half_of(w_in[l], 0), half_of(w_out[l], 0), half_of(wqkv, 2).reshape(-1, hd), half_of(ml_w_if[l], 0)]

    def layer_of(l, w4, rest):
        return dict(
            norm_g=norm_g[l][None], shift=mod_me[l, 0:d][None], scale=mod_me[l, d:2 * d][None],
            gate=mod_me[l, 2 * d:3 * d][None], w4=w4.reshape(4, d, n_sh),
            rg_conv_w=conv_full[0, l], rg_conv_b=rg_conv_b[l][None], rg_wa_b=_bf(rg_w_a[l]), rg_ba=rg_b_a[l][None],
            rg_wx_b=_bf(rg_w_x[l]), rg_bx=rg_b_x[l][None], rg_lam=rg_lambda[l][None],
            ml_conv_w=conv_full[1, l], ml_conv_b=ml_conv_b[l][None], b_if=ml_b_if[l][None], b_ift=ml_b_if[l][:, None],
            ml_g=ml_norm_g[l][None], **rest)

    def rest_of(gathered):
        w_out_b, wqkv_g, wif = gathered
        return dict(w_out_b=w_out_b, wqkv_b=_from_pieces(wqkv_g.reshape(8, -1), (3, heads, hd, hd), 2), wif_b=wif,
                    wift_b=wif.T)

    landing = lambda b: jax.ShapeDtypeStruct((4, 2) + b.shape, b.dtype)
    whole = lambda landed: [t.reshape(-1, t.shape[-1]) for t in _sib_fill(landed)]
    first = blocks_of(0)
    p = layer_of(0, _all_gather8(first[:1], pltpu.HBM)[0], {})
    rides = dict(ln_inproj=Ride(first[1:], [landing(b) for b in first[1:]], False))
    late = lambda landed: rest_of(whole(landed))
    layers, saved = [], []
    xl = x[0]
    for l in range(depth):
        if l + 1 < depth:
            nxt = blocks_of(l + 1)
            rides["rg_fwd"] = Ride(nxt[:1], [landing(nxt[0])], False)
            rides["mlstm_fwd"] = Ride(nxt[1:], [landing(b) for b in nxt[1:]], False)
        xl, s, p, got = _layer_fwd(xl, p, rides, late)
        layers.append(p)
        saved.append(s)
        if l + 1 < depth:
            nxt_whole = whole(list(got["rg_fwd"]) + list(got["mlstm_fwd"]))
            p = layer_of(l + 1, nxt_whole[0], rest_of(nxt_whole[1:]))
            rides, late = {}, None
    dx, g_final, loss = _final_loss(xl, final_g[None], loss_target[0])

    half = ac.reshape(1)
    ids = jnp.stack([chip, ac])
    r_out = w_out.shape[1] // 2

    def partial_sums(g_w_in, g_w_out):
        g_out5 = g_w_out.reshape(1, 4, 2, r_out, d)
        got_in, got_out = _sib_halves(g_w_in, g_out5)
        part_in = _pair_sum(
            half, g_w_in, pl.BlockSpec((None, d // 2, n_sh), lambda s, h: (0, h[0], s)),
            got_in, pl.BlockSpec((None, None, d // 2, n_sh), lambda s, h: (0, s, 0, 0)),
            jax.ShapeDtypeStruct((4, 1, d // 2, n_sh), BF16),
            pl.BlockSpec((None, None, d // 2, n_sh), lambda s, h: (s, 0, 0, 0)), (4,))
        part_out = _pair_sum(
            half, g_out5, pl.BlockSpec((None, None, None, r_out, d), lambda s, h: (0, s, h[0], 0, 0)),
            got_out, pl.BlockSpec((None, None, r_out, d), lambda s, h: (0, s, 0, 0)),
            jax.ShapeDtypeStruct((4, 1, r_out, d), BF16),
            pl.BlockSpec((None, None, r_out, d), lambda s, h: (s, 0, 0, 0)), (4,))
        return [part_in, part_out]

    exchange = lambda parts_l: Ride(parts_l, [jax.ShapeDtypeStruct(t.shape, t.dtype) for t in parts_l], True)
    grads, dmods, parts, mets = [None] * depth, [None] * depth, [None] * depth, [None] * depth

    def last_exchange(g_w_in, g_w_out):
        parts[0] = partial_sums(g_w_in, g_w_out)
        return exchange(parts[0])

    rides = {}
    for l in reversed(range(depth)):
        if l == 0:
            rides["in_bwd"] = last_exchange
        dx, grads[l], dmods[l], got = _layer_bwd(dx, layers[l], saved[l], rides)
        if "mlstm_bwd" in rides:
            mets[l + 1] = got["mlstm_bwd"]
        if l == 0:
            mets[0] = got["in_bwd"]
        else:
            parts[l] = partial_sums(grads[l]["w_in"], grads[l]["w_out"])
            rides = dict(mlstm_bwd=exchange(parts[l]))

    dm_blk = jnp.concatenate(dmods + [jnp.zeros((8 - depth, 3 * d), F32)], axis=0)
    dm_all = _all_gather8([dm_blk], pltpu.VMEM)[0].reshape(8, 8, 3 * d)[:, :depth].transpose(1, 0, 2)
    dm_cols = lax.dynamic_slice_in_dim(dm_all, chip * n_ada, n_ada, axis=2)
    g_w_ada, g_b_ada = _ada_grad(c_all, dm_cols, dm_all)

    sm = jnp.concatenate([_to_pieces(grads[l][name], axis) for l in range(depth) for name, axis in SMALL_SHARDED.items()],
                         axis=-1)
    sm = _pad_rows(sm, 16 * LANES)
    n_sm = sm.shape[-1] // LANES
    sm = sm.transpose(1, 0, 2).reshape(2, 4 * n_sm, LANES)
    rep = [grads[l][name].reshape(-1) for l in range(depth) for name in REPLICATED[:-1]]
    rep += [_pad_rows(grads[l]["ml_b_if"].reshape(-1), LANES) for l in range(depth)]
    rep += [g_final.reshape(-1), loss.reshape(-1)]
    rep = _pad_rows(jnp.concatenate(rep), 8 * 8 * LANES)
    n_rep = rep.shape[0] // (8 * LANES)
    rep = rep.reshape(4, 2, n_rep, LANES).transpose(1, 0, 2, 3).reshape(2, 4 * n_rep, LANES)
    got_sm, got_rep = _sib_slabs([sm, rep])

    def slab_sum(slab, got, rows, dtype):
        blk = pl.BlockSpec((rows, LANES), lambda s, h: (s, 0))
        return _pair_sum(half, slab, pl.BlockSpec((None, rows, LANES), lambda s, h: (h[0], s, 0)), got, blk,
                         jax.ShapeDtypeStruct((4 * rows, LANES), dtype), blk, (4,)).reshape(4, 1, rows, LANES)

    part_sm = slab_sum(sm, got_sm, n_sm, BF16)
    part_rep = slab_sum(rep, got_rep, n_rep, F32)
    g = dict(w_ada=g_w_ada, b_ada=g_b_ada.reshape(b_ada.shape))
    item = lambda name: (given[name], g[name], given["m_" + name], given["v_" + name])
    (stepped_w_ada,), (met_sm, met_rep) = _adamw([item("w_ada")], exchange([part_sm, part_rep]))
    both_in, both_out = depth, depth
    for l in range(depth):
        both_in = _chip_sum(ids, parts[l][0], mets[l][0], True, l, both_in)
        both_out = _chip_sum(ids, parts[l][1], mets[l][1], True, l, both_out)
    both_in, both_out, both_sm = _sib_fill([both_in, both_out, _chip_sum(ids, part_sm, met_sm, True)])
    red_rep = _chip_sum(ids, part_rep, met_rep, False).reshape(n_rep, LANES)
    rep_all = _all_gather8([red_rep], pltpu.VMEM)[0].reshape(-1)

    g.update(w_in=both_in.reshape(w_in.shape), w_out=both_out.reshape(w_out.shape))
    shard = both_sm.reshape(2, -1)
    off = 0
    per_layer = {name: [] for name in SMALL_SHARDED}
    for l in range(depth):
        for name, axis in SMALL_SHARDED.items():
            shp = (3,) + ml_w_q.shape[1:] if name == "ml_w_qkv" else given[name].shape[1:]
            n = grads[l][name].size // 8
            per_layer[name].append(_from_pieces(shard[:, off:off + n], shp, axis))
            off += n
    for name in SMALL_SHARDED:
        g[name] = jnp.stack(per_layer[name])
    for i, name in enumerate(["ml_w_q", "ml_w_k", "ml_w_v"]):
        g[name] = g["ml_w_qkv"][:, i]
    off = 0
    per_layer = {name: [] for name in REPLICATED}
    for l in range(depth):
        for name in REPLICATED[:-1]:
            n = given[name][l].size
            per_layer[name].append(rep_all[off:off + n].reshape(given[name].shape[1:]))
            off += n
    for l in range(depth):
        n = given["ml_b_if"][l].size
        per_layer["ml_b_if"].append(rep_all[off:off + n])
        off += LANES
    for name in REPLICATED:
        g[name] = jnp.stack(per_layer[name])
    g["final_g"] = rep_all[off:off + d]
    loss_all = rep_all[off + d]

    stepped = dict(w_ada=stepped_w_ada)
    rg_mats, ml_mats = ["rg_w_a", "rg_w_x"], ["ml_w_q", "ml_w_k", "ml_w_v"]
    small = [n for n in WEIGHTS if n not in ["w_ada", "w_in", "w_out"] + rg_mats + ml_mats]
    for names in (["w_in"], ["w_out"], rg_mats, ml_mats, small):
        stepped.update(zip(names, _adamw([item(name) for name in names])[0]))
    deltas, new_m, new_v = zip(*[stepped[name] for name in WEIGHTS])
    return (loss_all, dx[None], *[g[name] for name in WEIGHTS], *deltas, *new_m, *new_v)
```

```python
import functools
from typing import NamedTuple

import jax
import jax.numpy as jnp
from jax import lax
from jax.experimental import pallas as pl
from jax.experimental.pallas import tpu as pltpu

F32 = jnp.float32
BF16 = jnp.bfloat16

EPS = 1e-6
RG_C = 8.0
CONV_WIDTH = 4
ML_CHUNK = 128
HALO = 8
ADAM_LR = 0.001
ADAM_B1 = 0.9
ADAM_B2 = 0.999
ADAM_EPS = 1e-08
ADAM_WD = 0.01
ADAM_STEP = 10
MESH = pl.DeviceIdType.MESH


def _pcall(body, **kw):
    return pl.pallas_call(body, **kw)


class Ride(NamedTuple):
    srcs: list
    dst_shapes: list
    sliced: bool


def _pcall_ride(body, ride, *, grid, in_specs, out_specs, out_shape, args, scratch_shapes=(), **kw):
    n_in, n_out, n_scr = len(in_specs), len(out_specs), len(scratch_shapes)
    if ride is None:
        res = _pcall(body, grid=grid, in_specs=in_specs, out_specs=out_specs, out_shape=out_shape,
                     scratch_shapes=list(scratch_shapes), **kw)(*args)
        return res, []
    nr = len(ride.srcs)

    def riding(*refs):
        ins, rsrc = refs[:n_in], refs[n_in:n_in + nr]
        outs, rdst = refs[n_in + nr:n_in + nr + n_out], refs[n_in + nr + n_out:n_in + 2 * nr + n_out]
        scr = refs[n_in + 2 * nr + n_out:n_in + 2 * nr + n_out + n_scr]
        send_sems, recv_sems, local_sems = refs[n_in + 2 * nr + n_out + n_scr:]
        x, y, c = _me()
        me_s = 2 * x + y
        chips = [(1 - x, y), (x, 1 - y), (1 - x, 1 - y)]
        copies, local = [], []
        for i in range(nr):
            for k, (px, py) in enumerate(chips):
                src = rsrc[i].at[2 * px + py] if ride.sliced else rsrc[i]
                dst = rdst[i].at[me_s] if ride.sliced else rdst[i].at[me_s, c]
                copies.append(_remote(src, dst, send_sems.at[3 * i + k], recv_sems.at[3 * i + k], (px, py, c)))
            if not ride.sliced:
                local.append(pltpu.make_async_copy(rsrc[i], rdst[i].at[me_s, c], local_sems.at[i]))
        first = functools.reduce(jnp.logical_and, [pl.program_id(a) == 0 for a in range(len(grid))])
        last = functools.reduce(jnp.logical_and, [pl.program_id(a) == grid[a] - 1 for a in range(len(grid))])

        @pl.when(first)
        def _():
            for cp in copies + local:
                cp.start()

        body(*ins, *outs, *scr)

        @pl.when(last)
        def _():
            for cp in copies:
                cp.wait_recv()
            for cp in copies:
                cp.wait_send()
            for cp in local:
                cp.wait()

    hbm = pl.BlockSpec(memory_space=pltpu.HBM)
    res = _pcall(
        riding, grid=grid, in_specs=list(in_specs) + [hbm] * nr, out_specs=list(out_specs) + [hbm] * nr,
        out_shape=list(out_shape) + list(ride.dst_shapes),
        scratch_shapes=list(scratch_shapes) + [pltpu.SemaphoreType.DMA((3 * nr,)), pltpu.SemaphoreType.DMA((3 * nr,)),
                                               pltpu.SemaphoreType.DMA((nr,))], **kw)(*args, *ride.srcs)
    return res[:n_out], res[n_out:]


def _seq(n=1):
    return pltpu.CompilerParams(dimension_semantics=("arbitrary",) * n)


def _dot(a, b):
    return jnp.dot(a, b, preferred_element_type=F32)


def _dot_nt(a, b):
    return lax.dot_general(a, b, (((1,), (1,)), ((), ())), preferred_element_type=F32)


def _dot_tn(a, b):
    return lax.dot_general(a, b, (((0,), (0,)), ((), ())), preferred_element_type=F32)


def _bf(x):
    return x.astype(BF16)


def _sigmoid(x):
    return 0.5 * jnp.tanh(0.5 * x) + 0.5


def _log1p(z):
    u = 1.0 + z
    return jnp.where(u == 1.0, z, jnp.log(u) * (z / jnp.where(u == 1.0, 1.0, u - 1.0)))


def _softplus(x):
    return jnp.maximum(x, 0.0) + _log1p(jnp.exp(-jnp.abs(x)))


def _log_sigmoid(x):
    return -_softplus(-x)


def _one_minus_sq(a, log_a):
    x = 2.0 * log_a
    small = -x * (1.0 + x * (0.5 + x * (1.0 / 6.0)))
    return jnp.where(x > -0.004, small, 1.0 - a * a)


def _dsilu(x, s):
    return s * (1.0 + x * (1.0 - s))


def _rowsum(x):
    return jnp.sum(x, axis=1, keepdims=True)


def _colsum(x):
    return jnp.sum(x, axis=0, keepdims=True)


def _shift_down(win, s):
    return win if s == 0 else pltpu.roll(win, s, 0)


def _shift_up(win, s):
    return win if s == 0 else pltpu.roll(win, win.shape[0] - s, 0)


def _conv_taps(win):
    return [_shift_down(win, CONV_WIDTH - 1 - k)[HALO:] for k in range(CONV_WIDTH)]


def _conv_fwd(taps, w_ref, b_ref):
    acc = b_ref[...] + w_ref[CONV_WIDTH - 1:CONV_WIDTH, :] * taps[CONV_WIDTH - 1]
    for k in range(CONV_WIDTH - 1):
        acc = acc + w_ref[k:k + 1, :] * taps[k]
    return acc


def _split3(x):
    hi = _bf(x)
    r1 = x - hi.astype(F32)
    mid = _bf(r1)
    lo = _bf(r1 - mid.astype(F32))
    return hi, mid, lo


def _tri_dot_left(tri, x):
    hi, mid, lo = _split3(x)
    return _dot(tri, hi) + _dot(tri, mid) + _dot(tri, lo)


def _tri_dot_right(x, tri):
    hi, mid, lo = _split3(x)
    return _dot(hi, tri) + _dot(mid, tri) + _dot(lo, tri)


def _tile(n, want):
    t = min(n, want)
    assert n % t == 0
    return t


def _ln_inproj(x, g, scale, shift, w4, ride=None):
    s_len, d = x.shape
    nj, _, nsh = w4.shape
    tm = _tile(s_len, 1024)

    def body(x_ref, g_ref, sc_ref, sh_ref, w_ref, h_ref, u_ref, hs):
        @pl.when(pl.program_id(1) == 0)
        def _():
            xv = x_ref[...]
            r = lax.rsqrt(jnp.mean(xv * xv, axis=-1, keepdims=True) + EPS)
            hv = (xv * r * g_ref[...]) * (1.0 + sc_ref[...]) + sh_ref[...]
            hs[...] = _bf(hv)
            h_ref[...] = hs[...]

        u_ref[...] = _dot(hs[...], w_ref[0])

    vec = pl.BlockSpec((1, d), lambda i, j: (0, 0))
    return _pcall_ride(
        body, ride, name="ln_inproj", grid=(s_len // tm, nj),
        in_specs=[pl.BlockSpec((tm, d), lambda i, j: (i, 0)), vec, vec, vec,
                  pl.BlockSpec((1, d, nsh), lambda i, j: (j, 0, 0))],
        out_specs=[pl.BlockSpec((tm, d), lambda i, j: (i, 0)), pl.BlockSpec((tm, nsh), lambda i, j: (i, j))],
        out_shape=[jax.ShapeDtypeStruct((s_len, d), BF16), jax.ShapeDtypeStruct((s_len, nj * nsh), F32)],
        scratch_shapes=[pltpu.VMEM((tm, d), BF16)],
        compiler_params=_seq(2),
        args=(x, g, scale, shift, w4))


def _rg_gates(xc, wa_ref, ba_ref, wx_ref, bx_ref, lam_ref):
    heads, hd, _ = wa_ref.shape
    xb = _bf(xc)
    ga = jnp.concatenate([_dot(xb[:, h * hd:(h + 1) * hd], wa_ref[h]) for h in range(heads)], axis=1) + ba_ref[...]
    gx = jnp.concatenate([_dot(xb[:, h * hd:(h + 1) * hd], wx_ref[h]) for h in range(heads)], axis=1) + bx_ref[...]
    r = _sigmoid(ga)
    ig = _sigmoid(gx)
    sp = _softplus(-lam_ref[...])
    log_a = (-RG_C) * r * sp
    a = jnp.exp(log_a)
    mult = jnp.sqrt(_one_minus_sq(a, log_a))
    return r, ig, sp, log_a, a, mult


def _scan_groups(a, u, reverse):
    n, c = a.shape
    a = a.reshape(n // 8, 8, c)
    u = u.reshape(n // 8, 8, c)
    row = lax.broadcasted_iota(jnp.int32, a.shape, 1)
    for k in (1, 2, 4):
        sft = 8 - k if reverse else k
        a_sh, u_sh = pltpu.roll(a, sft, 1), pltpu.roll(u, sft, 1)
        ok = row < 8 - k if reverse else row >= k
        u = jnp.where(ok, a * u_sh + u, u)
        a = jnp.where(ok, a * a_sh, a)
    return a.reshape(n, c), u.reshape(n, c)


def _rg_fwd(u, conv_w, conv_b, wa_b, ba, wx_b, bx, lam, ride=None):
    s_len = u.shape[0]
    d = conv_w.shape[1]
    tm = _tile(s_len, 256)
    per = tm // HALO

    def body(x_ref, xp_ref, z_ref, cw_ref, cb_ref, wa_ref, ba_ref, wx_ref, bx_ref, lam_ref,
             hh_ref, y_ref, carry):
        i = pl.program_id(0)

        @pl.when(i == 0)
        def _():
            carry[...] = jnp.zeros_like(carry)

        prev = jnp.where(i == 0, 0.0, xp_ref[...])
        xc = _conv_fwd(_conv_taps(jnp.concatenate([prev, x_ref[...]], axis=0)), cw_ref, cb_ref)
        _, ig, _, _, a, mult = _rg_gates(xc, wa_ref, ba_ref, wx_ref, bx_ref, lam_ref)
        ca, cu = _scan_groups(a, mult * (ig * xc), reverse=False)
        c = carry[0:1, :]
        for j in range(per):
            blk = ca[j * 8:(j + 1) * 8] * c + cu[j * 8:(j + 1) * 8]
            hh_ref[j * 8:(j + 1) * 8, :] = blk
            c = blk[7:8]
        carry[0:1, :] = c
        z = z_ref[...]
        y_ref[0] = _bf(hh_ref[...] * (z * _sigmoid(z)))

    vec = pl.BlockSpec((1, d), lambda i: (0, 0))
    whole3 = lambda a: pl.BlockSpec(a.shape, lambda i: (0, 0, 0))
    return _pcall_ride(
        body, ride, name="rg_fwd", grid=(s_len // tm,),
        in_specs=[pl.BlockSpec((tm, d), lambda i: (i, 0)),
                  pl.BlockSpec((HALO, d), lambda i: (jnp.maximum(i * per - 1, 0), 0)),
                  pl.BlockSpec((tm, d), lambda i: (i, 1)),
                  pl.BlockSpec((CONV_WIDTH, d), lambda i: (0, 0)), vec,
                  whole3(wa_b), vec, whole3(wx_b), vec, vec],
        out_specs=[pl.BlockSpec((tm, d), lambda i: (i, 0)), pl.BlockSpec((1, tm, d), lambda i: (0, i, 0))],
        out_shape=[jax.ShapeDtypeStruct((s_len, d), F32), jax.ShapeDtypeStruct((2, s_len, d), BF16)],
        scratch_shapes=[pltpu.VMEM((8, d), F32)],
        compiler_params=_seq(),
        args=(u, u, u, conv_w, conv_b, wa_b, ba, wx_b, bx, lam))


def _ml_pre(u, conv_w, conv_b, wqkv_b, wif_b, wift_b, b_if, b_ift):
    s_len = u.shape[0]
    d = conv_w.shape[1]
    _, heads, hd, _ = wqkv_b.shape
    ng = 2 * heads
    tm = _tile(s_len, 256)
    per = tm // HALO

    def body(x_ref, xp_ref, cw_ref, cb_ref, w_ref, wif_ref, wift_ref, bif_ref, bift_ref,
             qkv_ref, gt_ref, gtt_ref, bc_ref, bct_ref):
        i = pl.program_id(0)
        prev = jnp.where(i == 0, 0.0, xp_ref[...])
        xm = x_ref[...]
        pre = _conv_fwd(_conv_taps(jnp.concatenate([prev, xm], axis=0)), cw_ref, cb_ref)
        xcb = _bf(pre * _sigmoid(pre))
        xmb = _bf(xm)
        for h in range(heads):
            hs = slice(h * hd, (h + 1) * hd)
            qkv_ref[0, :, hs] = _bf(_dot(xcb[:, hs], w_ref[0, h]))
            qkv_ref[1, :, hs] = _bf(_dot(xcb[:, hs], w_ref[1, h]))
            qkv_ref[2, :, hs] = _bf(_dot(xmb[:, hs], w_ref[2, h]))
        qb, kb, vb = qkv_ref[0], qkv_ref[1], qkv_ref[2]
        gt = (_dot(qb, wif_ref[0:d, :]) + _dot(kb, wif_ref[d:2 * d, :]) + _dot(vb, wif_ref[2 * d:3 * d, :])
              + bif_ref[...])
        gtt = (_dot_nt(wift_ref[:, 0:d], qb) + _dot_nt(wift_ref[:, d:2 * d], kb)
               + _dot_nt(wift_ref[:, 2 * d:3 * d], vb) + bift_ref[...])
        gt_ref[...] = gt
        gtt_ref[...] = gtt
        r = lax.broadcasted_iota(jnp.int32, (tm, tm), 0)
        c = lax.broadcasted_iota(jnp.int32, (tm, tm), 1)
        same = (r // ML_CHUNK) == (c // ML_CHUNK)
        bc_ref[...] = _tri_dot_left(((r >= c) & same).astype(BF16), _log_sigmoid(gt))
        bct_ref[...] = _tri_dot_right(_log_sigmoid(gtt), ((r <= c) & same).astype(BF16))

    vec = pl.BlockSpec((1, d), lambda i: (0, 0))
    whole2 = lambda a: pl.BlockSpec(a.shape, lambda i: (0, 0))
    col = pl.BlockSpec((tm, ng), lambda i: (i, 0))
    row = pl.BlockSpec((ng, tm), lambda i: (0, i))
    return _pcall(
        body, name="ml_pre", grid=(s_len // tm,),
        in_specs=[pl.BlockSpec((tm, d), lambda i: (i, 2)),
                  pl.BlockSpec((HALO, d), lambda i: (jnp.maximum(i * per - 1, 0), 2)),
                  pl.BlockSpec((CONV_WIDTH, d), lambda i: (0, 0)), vec,
                  pl.BlockSpec(wqkv_b.shape, lambda i: (0, 0, 0, 0)), whole2(wif_b), whole2(wift_b), whole2(b_if),
                  whole2(b_ift)],
        out_specs=[pl.BlockSpec((3, tm, d), lambda i: (0, i, 0)), col, row, col, row],
        out_shape=[jax.ShapeDtypeStruct((3, s_len, d), BF16), jax.ShapeDtypeStruct((s_len, ng), F32),
                   jax.ShapeDtypeStruct((ng, s_len), F32), jax.ShapeDtypeStruct((s_len, ng), F32),
                   jax.ShapeDtypeStruct((ng, s_len), F32)],
        compiler_params=_seq(),
    )(u, u, conv_w, conv_b, wqkv_b, wif_b, wift_b, b_if, b_ift)


def _chunk_gates(gt, gtt, bc, bct, h, heads):
    li_c = gt[:, h:h + 1]
    li_r = gtt[h:h + 1, :]
    gf_c = gt[:, heads + h:heads + h + 1]
    b_c = bc[:, heads + h:heads + h + 1]
    b_r = bct[heads + h:heads + h + 1, :]
    return li_c, li_r, gf_c, b_c, b_r


def _chunk_weights(li_c, li_r, b_c, b_r, m_prev, causal):
    lc = b_c.shape[0]
    b_last = b_c[lc - 1:lc, :]
    dmat = jnp.where(causal, b_c - b_r + li_r, -jnp.inf)
    m_inter = b_c + m_prev
    m_t = jnp.maximum(m_inter, jnp.max(dmat, axis=1, keepdims=True))
    w_intra = jnp.exp(dmat - m_t)
    w_inter = jnp.exp(m_inter - m_t)
    g_c = b_last - b_c + li_c
    m_new = jnp.maximum(b_last + m_prev, jnp.max(g_c, axis=0, keepdims=True))
    w_state = jnp.exp(g_c - m_new)
    decay = jnp.exp(b_last + m_prev - m_new)
    return m_t, w_intra, w_inter, m_new, w_state, decay


def _tri_masks(lc):
    r = lax.broadcasted_iota(jnp.int32, (lc, lc), 0)
    c = lax.broadcasted_iota(jnp.int32, (lc, lc), 1)
    causal = r >= c
    return causal, causal.astype(BF16), (r <= c).astype(BF16)


def _mlstm_fwd(qkv, gates, u, ml_g, ycat, ride=None):
    _, s_len, d = qkv.shape
    ng = gates[0].shape[1]
    heads = ng // 2
    hd = d // heads
    lc = ML_CHUNK
    nc = s_len // lc
    kscale = hd ** -0.5

    def body(qkv_ref, gt_ref, gtt_ref, bc_ref, bct_ref, o_ref, z_ref, g_ref, _, cell_ref, y_ref, cst_ref, nst_ref,
             mst_ref, cs, ns, ms):
        @pl.when(pl.program_id(0) == 0)
        def _():
            cs[...] = jnp.zeros_like(cs)
            ns[...] = jnp.zeros_like(ns)
            ms[...] = jnp.zeros_like(ms)

        causal = _tri_masks(lc)[0]
        gtv, gttv, bcv, bctv = gt_ref[...], gtt_ref[...], bc_ref[...], bct_ref[...]
        old = [(cs[h], ns[h], ms[h]) for h in range(heads)]
        new, cells, ys = [], [], []
        for h in range(heads):
            hs = slice(h * hd, (h + 1) * hd)
            li_c, li_r, _, b_c, b_r = _chunk_gates(gtv, gttv, bcv, bctv, h, heads)
            c_old, n_old, m_old = old[h]
            m_prev = m_old[:, 0:1]
            m_t, w_intra, w_inter, m_new, w_state, decay = _chunk_weights(li_c, li_r, b_c, b_r, m_prev, causal)
            qb = qkv_ref[0, :, hs]
            ks = qkv_ref[1, :, hs].astype(F32) * kscale
            kb = _bf(ks)
            vb = qkv_ref[2, :, hs]
            s = _dot_nt(qb, kb) * w_intra
            num = _dot(_bf(s), vb) + w_inter * _dot(qb, _bf(c_old))
            den = _rowsum(s) + w_inter * _rowsum(qb.astype(F32) * n_old)
            cell = num / jnp.maximum(jnp.abs(den), jnp.exp(-m_t))
            kw = ks * w_state
            new.append((decay * c_old + _dot_tn(_bf(kw), vb), decay * n_old + _colsum(kw),
                        jnp.broadcast_to(m_new, m_old.shape)))
            cells.append(cell)
            hm = _sigmoid(o_ref[:, hs]) * cell
            hn = hm * lax.rsqrt(jnp.mean(hm * hm, axis=-1, keepdims=True) + EPS)
            z = z_ref[:, hs]
            ys.append(_bf((hn * g_ref[:, hs]) * (z * _sigmoid(z))))
        for h in range(heads):
            cst_ref[0, h] = _bf(old[h][0])
            nst_ref[0, h] = old[h][1]
            mst_ref[0, h] = old[h][2]
            cs[h], ns[h], ms[h] = new[h]
        cell_ref[...] = jnp.concatenate(cells, axis=1)
        y_ref[0] = jnp.concatenate(ys, axis=1)

    row = pl.BlockSpec((lc, d), lambda c: (c, 0))
    gcol = pl.BlockSpec((lc, ng), lambda c: (c, 0))
    grow = pl.BlockSpec((ng, lc), lambda c: (0, c))
    return _pcall_ride(
        body, ride, name="mlstm_fwd", grid=(nc,),
        in_specs=[pl.BlockSpec((3, lc, d), lambda c: (0, c, 0)), gcol, grow, gcol, grow,
                  pl.BlockSpec((lc, d), lambda c: (c, 3)), pl.BlockSpec((lc, d), lambda c: (c, 4)),
                  pl.BlockSpec((1, d), lambda c: (0, 0)), pl.BlockSpec(memory_space=pl.ANY)],
        out_specs=[row, pl.BlockSpec((1, lc, d), lambda c: (1, c, 0)),
                   pl.BlockSpec((1, heads, hd, hd), lambda c: (c, 0, 0, 0)),
                   pl.BlockSpec((1, heads, 1, hd), lambda c: (c, 0, 0, 0)),
                   pl.BlockSpec((1, heads, 1, 128), lambda c: (c, 0, 0, 0))],
        out_shape=[jax.ShapeDtypeStruct((s_len, d), F32), jax.ShapeDtypeStruct(ycat.shape, BF16),
                   jax.ShapeDtypeStruct((nc, heads, hd, hd), BF16),
                   jax.ShapeDtypeStruct((nc, heads, 1, hd), F32),
                   jax.ShapeDtypeStruct((nc, heads, 1, 128), F32)],
        scratch_shapes=[pltpu.VMEM((heads, hd, hd), F32), pltpu.VMEM((heads, 1, hd), F32),
                        pltpu.VMEM((heads, 1, 128), F32)],
        input_output_aliases={8: 1},
        compiler_params=_seq(),
        args=(qkv, *gates, u, u, ml_g, ycat))


def _out_proj(ycat, w_out_b, x, gate):
    s_len, d = x.shape
    tm = _tile(s_len, 1024)

    def body(a_ref, w_ref, x_ref, g_ref, y_ref, xn_ref):
        y = _dot(a_ref[0], w_ref[0:d, :]) + _dot(a_ref[1], w_ref[d:2 * d, :])
        y_ref[...] = y
        xn_ref[...] = x_ref[...] + g_ref[...] * y

    row = pl.BlockSpec((tm, d), lambda i: (i, 0))
    return _pcall(
        body, name="out_proj", grid=(s_len // tm,),
        in_specs=[pl.BlockSpec((2, tm, d), lambda i: (0, i, 0)), pl.BlockSpec((2 * d, d), lambda i: (0, 0)), row,
                  pl.BlockSpec((1, d), lambda i: (0, 0))],
        out_specs=[row, row],
        out_shape=[jax.ShapeDtypeStruct((s_len, d), F32)] * 2,
        compiler_params=_seq(),
    )(ycat, w_out_b, x, gate)


def _final_loss(x, g, target):
    s_len, d = x.shape
    tm = _tile(s_len, 256)

    def body(x_ref, g_ref, t_ref, dx_ref, dg_ref, loss_ref):
        @pl.when(pl.program_id(0) == 0)
        def _():
            dg_ref[...] = jnp.zeros_like(dg_ref)
            loss_ref[...] = jnp.zeros_like(loss_ref)

        xv = x_ref[...]
        r = lax.rsqrt(jnp.mean(xv * xv, axis=-1, keepdims=True) + EPS)
        xn = xv * r
        err = xn * g_ref[...] - t_ref[...]
        loss_ref[...] += 0.5 * jnp.sum(jnp.mean(err * err, axis=-1, keepdims=True))
        dout = err * (1.0 / d)
        dg_ref[...] += _colsum(dout * xn)
        dxn = dout * g_ref[...]
        dx_ref[...] = r * (dxn - xn * jnp.mean(dxn * xn, axis=-1, keepdims=True))

    row = pl.BlockSpec((tm, d), lambda i: (i, 0))
    vec = pl.BlockSpec((1, d), lambda i: (0, 0))
    return _pcall(
        body, name="final_loss", grid=(s_len // tm,),
        in_specs=[row, vec, row],
        out_specs=[row, vec, pl.BlockSpec((1, 128), lambda i: (0, 0))],
        out_shape=[jax.ShapeDtypeStruct((s_len, d), F32), jax.ShapeDtypeStruct((1, d), F32),
                   jax.ShapeDtypeStruct((1, 128), F32)],
        compiler_params=_seq(),
    )(x, g, target)


def _out_bwd(dxn, y, gate, w_out_b):
    s_len, d = dxn.shape
    tm = _tile(s_len, 1024)

    def body(dx_ref, y_ref, g_ref, w_ref, dg_ref, dy_ref, dc_ref):
        @pl.when(pl.program_id(0) == 0)
        def _():
            dg_ref[...] = jnp.zeros_like(dg_ref)

        dx = dx_ref[...]
        dg_ref[...] += _colsum(dx * y_ref[...])
        dy = _bf(g_ref[...] * dx)
        dy_ref[...] = dy
        dc_ref[0] = _dot_nt(dy, w_ref[0:d, :])
        dc_ref[1] = _dot_nt(dy, w_ref[d:2 * d, :])

    row = pl.BlockSpec((tm, d), lambda i: (i, 0))
    vec = pl.BlockSpec((1, d), lambda i: (0, 0))
    return _pcall(
        body, name="out_bwd", grid=(s_len // tm,),
        in_specs=[row, row, vec, pl.BlockSpec((2 * d, d), lambda i: (0, 0))],
        out_specs=[vec, row, pl.BlockSpec((2, tm, d), lambda i: (0, i, 0))],
        out_shape=[jax.ShapeDtypeStruct((1, d), F32), jax.ShapeDtypeStruct((s_len, d), BF16),
                   jax.ShapeDtypeStruct((2, s_len, d), F32)],
        compiler_params=_seq(),
    )(dxn, y, gate, w_out_b)


def _grad_matmul(a3, b3, nblk, a_idx, b_idx, out_shape, out_block, out_idx, ride=None):
    _, s_len, m = a3.shape
    n = b3.shape[2]
    tk = _tile(s_len, 2048)

    def body(a_ref, b_ref, o_ref):
        @pl.when(pl.program_id(1) == 0)
        def _():
            o_ref[...] = jnp.zeros_like(o_ref)

        o_ref[...] += _dot_tn(a_ref[0], b_ref[0])

    (out,), got = _pcall_ride(
        body, ride, name="grad_matmul", grid=(nblk, s_len // tk),
        in_specs=[pl.BlockSpec((1, tk, m), lambda p, t: (a_idx(p), t, 0)),
                  pl.BlockSpec((1, tk, n), lambda p, t: (b_idx(p), t, 0))],
        out_specs=[pl.BlockSpec((None,) + out_block, lambda p, t: (0,) + out_idx(p))],
        out_shape=[jax.ShapeDtypeStruct((1,) + out_shape, F32)],
        compiler_params=_seq(2), args=(a3, b3))
    return out, got


DU_PLANE = (2, 3, 4, 0, 1)


def _mlstm_bwd(qkv, gates, cst, nst, mst, cell, u, ml_g, d_ycat, wif_b, ride=None):
    _, s_len, d = qkv.shape
    ng = gates[0].shape[1]
    heads = ng // 2
    hd = d // heads
    lc = ML_CHUNK
    nc = s_len // lc
    kscale = hd ** -0.5

    def body(qkv_ref, gt_ref, gtt_ref, bc_ref, bct_ref, cst_ref, nst_ref, mst_ref, cell_ref, o_ref, z_ref, g_ref, dy_ref,
             wif_ref, dqkv_ref, dgt_ref, dbif_ref, du_ref, dg_ref, dcs, dns):
        @pl.when(pl.program_id(0) == 0)
        def _():
            dbif_ref[...] = jnp.zeros_like(dbif_ref)
            dcs[...] = jnp.zeros_like(dcs)
            dns[...] = jnp.zeros_like(dns)
            dg_ref[...] = jnp.zeros_like(dg_ref)

        causal, tril, triu = _tri_masks(lc)
        tril_strict = (tril.astype(F32) - (tril * triu).astype(F32)).astype(BF16)
        gtv, gttv, bcv, bctv = gt_ref[...], gtt_ref[...], bc_ref[...], bct_ref[...]
        lane = lax.broadcasted_iota(jnp.int32, (lc, ng), 1)
        dli_all = jnp.zeros((lc, ng), F32)
        from_later = jnp.zeros((lc, ng), F32)
        from_earlier = jnp.zeros((lc, ng), F32)
        across_all = jnp.zeros((1, ng), F32)
        old = [(dcs[h], dns[h]) for h in range(heads)]
        new, d_o, d_z, d_g, dqs, dks, dvs = [], [], [], [], [], [], []
        for h in range(heads):
            hs = slice(h * hd, (h + 1) * hd)
            li_c, li_r, gf_c, b_c, b_r = _chunk_gates(gtv, gttv, bcv, bctv, h, heads)
            m_prev = mst_ref[0, h][:, 0:1]
            m_t, w_intra, w_inter, _, w_state, decay = _chunk_weights(li_c, li_r, b_c, b_r, m_prev, causal)
            qb = qkv_ref[0, :, hs]
            qf = qb.astype(F32)
            ks = qkv_ref[1, :, hs].astype(F32) * kscale
            kb = _bf(ks)
            vb = qkv_ref[2, :, hs]
            c_b = cst_ref[0, h]
            n_old = nst_ref[0, h]
            s = _dot_nt(qb, kb) * w_intra
            den = _rowsum(s) + w_inter * _rowsum(qf * n_old)
            floor = jnp.exp(-m_t)
            dstab = jnp.maximum(jnp.abs(den), floor)
            cell = cell_ref[:, hs]
            o = o_ref[:, hs]
            so = _sigmoid(o)
            hm = so * cell
            rinv = lax.rsqrt(jnp.mean(hm * hm, axis=-1, keepdims=True) + EPS)
            hn = hm * rinv
            z = z_ref[:, hs]
            sgz = _sigmoid(z)
            sz = z * sgz
            gh = g_ref[:, hs]
            dy = dy_ref[0, :, hs]
            d_z.append(_bf(dy * (hn * gh) * _dsilu(z, sgz)))
            d_g.append(_colsum(dy * hn * sz))
            dhn = dy * gh * sz
            dhm = rinv * (dhn - hn * jnp.mean(dhn * hn, axis=-1, keepdims=True))
            d_o.append(_bf(dhm * cell * so * (1.0 - so)))
            dcell = dhm * so
            dnum = dcell / dstab
            dnb = _bf(dnum)
            dden = -_rowsum(dcell * cell) / dstab * jnp.where(jnp.abs(den) > floor, jnp.where(den > 0.0, 1.0, -1.0), 0.0)
            dst = _dot_nt(dnb, vb) + dden
            dsdb = _bf(dst * w_intra)
            dc_out, dn_out = old[h]
            dcb = _bf(dc_out)
            dq_inter = w_inter * (_dot_nt(dnb, c_b) + dden * n_old)
            dk_inter = w_state * (_dot_nt(vb, dcb) + dn_out)
            dq = _dot(dsdb, kb) + dq_inter
            dk = _dot_tn(dsdb, qb) + dk_inter
            dv = _dot_tn(_bf(s), dnb) + _dot(_bf(ks * w_state), dcb)
            wq = w_inter * qf
            new.append((decay * dc_out + _dot_tn(_bf(wq), dnb), decay * dn_out + _colsum(wq * dden)))
            pmat = dst * s
            p_rows = _rowsum(pmat)
            p_cols = _rowsum(pmat.T)
            q_in = _rowsum(qf * dq_inter)
            k_in = _rowsum(ks * dk_inter)
            across = decay * (jnp.sum(dc_out * c_b.astype(F32), keepdims=True) + jnp.sum(dn_out * n_old, keepdims=True))
            dli_all = dli_all + jnp.where(lane == h, p_cols + k_in, 0.0)
            from_later = from_later + jnp.where(lane == heads + h, p_rows - p_cols + q_in, 0.0)
            from_earlier = from_earlier + jnp.where(lane == heads + h, k_in, 0.0)
            across_all = across_all + jnp.where(lane[0:1] == heads + h, across, 0.0)
            dqs.append(dq)
            dks.append(dk * kscale)
            dvs.append(dv)
        for h in range(heads):
            dcs[h], dns[h] = new[h]
        du_ref[0] = jnp.concatenate(d_o, axis=1)
        du_ref[1] = jnp.concatenate(d_z, axis=1)
        dg_ref[...] += jnp.concatenate(d_g, axis=1)
        dlf = _tri_dot_left(triu, from_later) + _tri_dot_left(tril_strict, from_earlier) + across_all
        dgt = dli_all + dlf * _sigmoid(-gtv)
        dgt_ref[...] = dgt
        dbif_ref[...] += _colsum(dgt)
        dgb = _bf(dgt)
        dqkv_ref[0] = _bf(jnp.concatenate(dqs, axis=1) + _dot_nt(dgb, wif_ref[0:d, :]))
        dqkv_ref[1] = _bf(jnp.concatenate(dks, axis=1) + _dot_nt(dgb, wif_ref[d:2 * d, :]))
        dqkv_ref[2] = _bf(jnp.concatenate(dvs, axis=1) + _dot_nt(dgb, wif_ref[2 * d:3 * d, :]))

    rev = lambda c: nc - 1 - c
    row = pl.BlockSpec((lc, d), lambda c: (rev(c), 0))
    gcol = pl.BlockSpec((lc, ng), lambda c: (rev(c), 0))
    grow = pl.BlockSpec((ng, lc), lambda c: (0, rev(c)))
    return _pcall_ride(
        body, ride, name="mlstm_bwd", grid=(nc,),
        in_specs=[pl.BlockSpec((3, lc, d), lambda c: (0, rev(c), 0)), gcol, grow, gcol, grow,
                  pl.BlockSpec((1, heads, hd, hd), lambda c: (rev(c), 0, 0, 0)),
                  pl.BlockSpec((1, heads, 1, hd), lambda c: (rev(c), 0, 0, 0)),
                  pl.BlockSpec((1, heads, 1, 128), lambda c: (rev(c), 0, 0, 0)),
                  row, pl.BlockSpec((lc, d), lambda c: (rev(c), 3)), pl.BlockSpec((lc, d), lambda c: (rev(c), 4)),
                  pl.BlockSpec((1, d), lambda c: (0, 0)), pl.BlockSpec((1, lc, d), lambda c: (1, rev(c), 0)),
                  pl.BlockSpec((3 * d, ng), lambda c: (0, 0))],
        out_specs=[pl.BlockSpec((3, lc, d), lambda c: (0, rev(c), 0)), pl.BlockSpec((lc, ng), lambda c: (rev(c), 0)),
                   pl.BlockSpec((1, ng), lambda c: (0, 0)), pl.BlockSpec((2, lc, d), lambda c: (0, rev(c), 0)),
                   pl.BlockSpec((1, d), lambda c: (0, 0))],
        out_shape=[jax.ShapeDtypeStruct((3, s_len, d), BF16), jax.ShapeDtypeStruct((s_len, ng), F32),
                   jax.ShapeDtypeStruct((1, ng), F32), jax.ShapeDtypeStruct((5, s_len, d), BF16),
                   jax.ShapeDtypeStruct((1, d), F32)],
        scratch_shapes=[pltpu.VMEM((heads, hd, hd), F32), pltpu.VMEM((heads, 1, hd), F32)],
        compiler_params=_seq(),
        args=(qkv, *gates, cst, nst, mst, cell, u, u, ml_g, d_ycat, wif_b))


def _conv_bwd_tile(dp, later, taps, cw_ref, gw_ref, gb_ref):
    tm = dp.shape[0]
    dwin = jnp.concatenate([dp, later[...]], axis=0)
    later[...] = dp[0:HALO]
    acc = cw_ref[CONV_WIDTH - 1:CONV_WIDTH, :] * dp
    for k in range(CONV_WIDTH):
        if k < CONV_WIDTH - 1:
            acc = acc + cw_ref[k:k + 1, :] * _shift_up(dwin, CONV_WIDTH - 1 - k)[0:tm]
        gw_ref[k:k + 1, :] += _colsum(dp * taps[k])
    gb_ref[...] += _colsum(dp)
    return acc


def _ml_pre_bwd(dqkv, u, conv_w, conv_b, wqkv_b, du):
    s_len = u.shape[0]
    d = conv_w.shape[1]
    _, heads, hd, _ = wqkv_b.shape
    tm = _tile(s_len, 256)
    per = tm // HALO
    nt = s_len // tm

    def body(dqkv_ref, x_ref, xp_ref, cw_ref, cb_ref, w_ref, _, dx_ref, gw_ref, gcw_ref, gcb_ref, later, dps, dxs):
        i = pl.program_id(0)

        @pl.when(i == 0)
        def _():
            gw_ref[...] = jnp.zeros_like(gw_ref)
            gcw_ref[...] = jnp.zeros_like(gcw_ref)
            gcb_ref[...] = jnp.zeros_like(gcb_ref)
            later[...] = jnp.zeros_like(later)

        prev = jnp.where(i == nt - 1, 0.0, xp_ref[...])
        xm = x_ref[...]
        taps = _conv_taps(jnp.concatenate([prev, xm], axis=0))
        pre = _conv_fwd(taps, cw_ref, cb_ref)
        sg = _sigmoid(pre)
        xcb = _bf(pre * sg)
        xmb = _bf(xm)
        for h in range(heads):
            hs = slice(h * hd, (h + 1) * hd)
            dqh, dkh, dvh = dqkv_ref[0, :, hs], dqkv_ref[1, :, hs], dqkv_ref[2, :, hs]
            dxc = _dot_nt(dqh, w_ref[0, h]) + _dot_nt(dkh, w_ref[1, h])
            dps[:, hs] = dxc * _dsilu(pre[:, hs], sg[:, hs])
            dxs[:, hs] = _dot_nt(dvh, w_ref[2, h])
            gw_ref[0, h] += _dot_tn(xcb[:, hs], dqh)
            gw_ref[1, h] += _dot_tn(xcb[:, hs], dkh)
            gw_ref[2, h] += _dot_tn(xmb[:, hs], dvh)
        dx_ref[0] = _bf(_conv_bwd_tile(dps[...], later, taps, cw_ref, gcw_ref, gcb_ref) + dxs[...])

    rev = lambda i: nt - 1 - i
    vec = pl.BlockSpec((1, d), lambda i: (0, 0))
    cwb = pl.BlockSpec((CONV_WIDTH, d), lambda i: (0, 0))
    whole4 = pl.BlockSpec(wqkv_b.shape, lambda i: (0, 0, 0, 0))
    return _pcall(
        body, name="ml_pre_bwd", grid=(nt,),
        in_specs=[pl.BlockSpec((3, tm, d), lambda i: (0, rev(i), 0)), pl.BlockSpec((tm, d), lambda i: (rev(i), 2)),
                  pl.BlockSpec((HALO, d), lambda i: (jnp.maximum(rev(i) * per - 1, 0), 2)),
                  cwb, vec, whole4, pl.BlockSpec(memory_space=pl.ANY)],
        out_specs=[pl.BlockSpec((1, tm, d), lambda i: (DU_PLANE[2], rev(i), 0)), whole4, cwb, vec],
        out_shape=[jax.ShapeDtypeStruct(du.shape, BF16), jax.ShapeDtypeStruct(wqkv_b.shape, F32),
                   jax.ShapeDtypeStruct((CONV_WIDTH, d), F32), jax.ShapeDtypeStruct((1, d), F32)],
        scratch_shapes=[pltpu.VMEM((HALO, d), F32), pltpu.VMEM((tm, d), F32), pltpu.VMEM((tm, d), F32)],
        input_output_aliases={6: 0},
        compiler_params=_seq(),
    )(dqkv, u, u, conv_w, conv_b, wqkv_b, du)


def _rg_bwd(d_ycat, u, hh, conv_w, conv_b, wa_b, ba, wx_b, bx, lam, du):
    s_len = u.shape[0]
    d = conv_w.shape[1]
    heads, hd, _ = wa_b.shape
    tm = _tile(s_len, 256)
    per = tm // HALO
    nt = s_len // tm

    def body(dy_ref, x_ref, xp_ref, z_ref, hh_ref, hp_ref, cw_ref, cb_ref, wa_ref, ba_ref, wx_ref, bx_ref, lam_ref, _,
             du_ref, gwa_ref, gwx_ref, gba_ref, gbx_ref, glam_ref, gcw_ref, gcb_ref, carry, gbuf, later, dxcs):
        i = pl.program_id(0)
        first = i == nt - 1

        @pl.when(i == 0)
        def _():
            carry[...] = jnp.zeros_like(carry)
            later[...] = jnp.zeros_like(later)
            gwa_ref[...] = jnp.zeros_like(gwa_ref)
            gwx_ref[...] = jnp.zeros_like(gwx_ref)
            gba_ref[...] = jnp.zeros_like(gba_ref)
            gbx_ref[...] = jnp.zeros_like(gbx_ref)
            glam_ref[...] = jnp.zeros_like(glam_ref)
            gcw_ref[...] = jnp.zeros_like(gcw_ref)
            gcb_ref[...] = jnp.zeros_like(gcb_ref)

        prev = jnp.where(first, 0.0, xp_ref[...])
        taps = _conv_taps(jnp.concatenate([prev, x_ref[...]], axis=0))
        xc = _conv_fwd(taps, cw_ref, cb_ref)
        r, ig, sp, log_a, a, mult = _rg_gates(xc, wa_ref, ba_ref, wx_ref, bx_ref, lam_ref)
        z = z_ref[...]
        sgz = _sigmoid(z)
        dy = dy_ref[0]
        hh_v = hh_ref[...]
        du_ref[1] = _bf(dy * hh_v * _dsilu(z, sgz))
        dhh = dy * (z * sgz)
        rows = lax.broadcasted_iota(jnp.int32, a.shape, 0)
        coef = jnp.where(rows == tm - 1, carry[1:2, :], _shift_up(a, 1))
        ca, cu = _scan_groups(coef, dhh, reverse=True)
        c = carry[0:1, :]
        for j in range(per - 1, -1, -1):
            blk = ca[j * 8:(j + 1) * 8] * c + cu[j * 8:(j + 1) * 8]
            gbuf[j * 8:(j + 1) * 8, :] = blk
            c = blk[0:1]
        carry[0:1, :] = c
        carry[1:2, :] = a[0:1]
        g = gbuf[...]
        hprev_tile = jnp.where(first, 0.0, hp_ref[...])
        hprev = _shift_down(jnp.concatenate([hprev_tile, hh_v], axis=0), 1)[HALO:]
        da = g * hprev
        gx_ = g * xc
        d_mult = gx_ * ig
        d_ig = gx_ * mult
        dxc = g * mult * ig
        dlog_a = da * a - d_mult * (a * a / mult)
        d_r = dlog_a * ((-RG_C) * sp)
        glam_ref[...] += _colsum(dlog_a * ((-RG_C) * r)) * (-_sigmoid(-lam_ref[...]))
        d_ga = d_r * r * (1.0 - r)
        d_gx = d_ig * ig * (1.0 - ig)
        gba_ref[...] += _colsum(d_ga)
        gbx_ref[...] += _colsum(d_gx)
        xb = _bf(xc)
        dgab = _bf(d_ga)
        dgxb = _bf(d_gx)
        for h in range(heads):
            hs = slice(h * hd, (h + 1) * hd)
            dxcs[:, hs] = dxc[:, hs] + _dot_nt(dgab[:, hs], wa_ref[h]) + _dot_nt(dgxb[:, hs], wx_ref[h])
            gwa_ref[h] += _dot_tn(xb[:, hs], dgab[:, hs])
            gwx_ref[h] += _dot_tn(xb[:, hs], dgxb[:, hs])
        du_ref[0] = _bf(_conv_bwd_tile(dxcs[...], later, taps, cw_ref, gcw_ref, gcb_ref))

    assert DU_PLANE[0] % 2 == 0 and DU_PLANE[1] == DU_PLANE[0] + 1
    rev = lambda i: nt - 1 - i
    row = pl.BlockSpec((tm, d), lambda i: (rev(i), 0))
    halo_prev = lambda col: pl.BlockSpec((HALO, d), lambda i: (jnp.maximum(rev(i) * per - 1, 0), col))
    vec = pl.BlockSpec((1, d), lambda i: (0, 0))
    cwb = pl.BlockSpec((CONV_WIDTH, d), lambda i: (0, 0))
    whole3 = lambda a: pl.BlockSpec(a.shape, lambda i: (0, 0, 0))
    return _pcall(
        body, name="rg_bwd", grid=(nt,),
        in_specs=[pl.BlockSpec((1, tm, d), lambda i: (0, rev(i), 0)), row, halo_prev(0),
                  pl.BlockSpec((tm, d), lambda i: (rev(i), 1)), row, halo_prev(0),
                  cwb, vec, whole3(wa_b), vec, whole3(wx_b), vec, vec, pl.BlockSpec(memory_space=pl.ANY)],
        out_specs=[pl.BlockSpec((2, tm, d), lambda i: (DU_PLANE[0] // 2, rev(i), 0)), whole3(wa_b), whole3(wa_b),
                   vec, vec, vec, cwb, vec],
        out_shape=[jax.ShapeDtypeStruct(du.shape, BF16), jax.ShapeDtypeStruct(wa_b.shape, F32),
                   jax.ShapeDtypeStruct(wa_b.shape, F32)] + [jax.ShapeDtypeStruct((1, d), F32)] * 3
        + [jax.ShapeDtypeStruct((CONV_WIDTH, d), F32), jax.ShapeDtypeStruct((1, d), F32)],
        scratch_shapes=[pltpu.VMEM((8, d), F32), pltpu.VMEM((tm, d), F32), pltpu.VMEM((HALO, d), F32),
                        pltpu.VMEM((tm, d), F32)],
        input_output_aliases={13: 0},
        compiler_params=_seq(),
    )(d_ycat, u, u, u, hh, hh, conv_w, conv_b, wa_b, ba, wx_b, bx, lam, du)


def _in_bwd(du, w4, x, dxn, g, scale, ride=None):
    s_len, d = x.shape
    tm = _tile(s_len, 512)
    nsh_chips, _, nsh = w4.shape
    npc = du.shape[0]
    ck = d // 4
    assert nsh % ck == 0 and npc * d == nsh_chips * nsh

    def body(du_ref, w_ref, x_ref, dxn_ref, g_ref, sc_ref, dx_ref, dsh_ref, dsc_ref, dg_ref):
        @pl.when(pl.program_id(0) == 0)
        def _():
            dsh_ref[...] = jnp.zeros_like(dsh_ref)
            dsc_ref[...] = jnp.zeros_like(dsc_ref)
            dg_ref[...] = jnp.zeros_like(dg_ref)

        dh = None
        for q in range(npc * d // ck):
            col = q * ck
            p, pc = col // d, col % d
            s, sc = col // nsh, col % nsh
            t = _dot_nt(du_ref[DU_PLANE[p], :, pc:pc + ck], w_ref[s, :, sc:sc + ck])
            dh = t if dh is None else dh + t
        xv = x_ref[...]
        r = lax.rsqrt(jnp.mean(xv * xv, axis=-1, keepdims=True) + EPS)
        xn = xv * r
        gv = g_ref[...]
        onesc = 1.0 + sc_ref[...]
        dsh_ref[...] += _colsum(dh)
        dsc_ref[...] += _colsum(dh * (xn * gv))
        dg_ref[...] += _colsum(dh * xn * onesc)
        dxh = dh * (gv * onesc)
        dx_ref[...] = dxn_ref[...] + r * (dxh - xn * jnp.mean(dxh * xn, axis=-1, keepdims=True))

    row = pl.BlockSpec((tm, d), lambda i: (i, 0))
    vec = pl.BlockSpec((1, d), lambda i: (0, 0))
    return _pcall_ride(
        body, ride, name="in_bwd", grid=(s_len // tm,),
        in_specs=[pl.BlockSpec((npc, tm, d), lambda i: (0, i, 0)), pl.BlockSpec(w4.shape, lambda i: (0, 0, 0)), row, row,
                  vec, vec],
        out_specs=[row, vec, vec, vec],
        out_shape=[jax.ShapeDtypeStruct((s_len, d), F32)] + [jax.ShapeDtypeStruct((1, d), F32)] * 3,
        compiler_params=_seq(),
        args=(du, w4, x, dxn, g, scale))


def _layer_fwd(x, p, rides=None, late=None):
    rides = rides or {}
    (h_b, u), got = _ln_inproj(x, p["norm_g"], p["scale"], p["shift"], p["w4"], rides.get("ln_inproj"))
    if late is not None:
        p = {**p, **late(got)}
    (hh, ycat), got_a = _rg_fwd(u, p["rg_conv_w"], p["rg_conv_b"], p["rg_wa_b"], p["rg_ba"], p["rg_wx_b"], p["rg_bx"],
                                p["rg_lam"], rides.get("rg_fwd"))
    qkv, *gates = _ml_pre(u, p["ml_conv_w"], p["ml_conv_b"], p["wqkv_b"], p["wif_b"], p["wift_b"], p["b_if"],
                          p["b_ift"])
    (cell, ycat, cst, nst, mst), got_b = _mlstm_fwd(qkv, gates, u, p["ml_g"], ycat, rides.get("mlstm_fwd"))
    y, x_new = _out_proj(ycat, p["w_out_b"], x, p["gate"])
    saved = dict(x=x, h_b=h_b, u=u, hh=hh, qkv=qkv, gates=gates, cell=cell, ycat=ycat, cst=cst, nst=nst, mst=mst, y=y)
    return x_new, saved, p, dict(rg_fwd=got_a, mlstm_fwd=got_b)


def _layer_bwd(dxn, p, s, rides=None):
    rides = rides or {}
    u = s["u"]
    d = dxn.shape[1]
    d_gate, dy_b, d_ycat = _out_bwd(dxn, s["y"], p["gate"], p["w_out_b"])
    g_w_out = _grad_matmul(s["ycat"], dy_b[None], 2, lambda b: b, lambda b: 0, (2 * d, d), (d, d), lambda b: (b, 0))[0]
    (dqkv, dgt, g_b_if, du, g_ml_g), got = _mlstm_bwd(s["qkv"], s["gates"], s["cst"], s["nst"], s["mst"], s["cell"], u,
                                                      p["ml_g"], d_ycat, p["wif_b"], rides.get("mlstm_bwd"))
    ng = dgt.shape[1]
    g_w_if = _grad_matmul(s["qkv"], _bf(dgt)[None], 3, lambda b: b, lambda b: 0, (3 * d, ng), (d, ng),
                          lambda b: (b, 0))[0][0]
    du, g_wqkv, g_ml_cw, g_ml_cb = _ml_pre_bwd(dqkv, u, p["ml_conv_w"], p["ml_conv_b"], p["wqkv_b"], du)
    du, g_wa, g_wx, g_ba, g_bx, g_lam, g_rg_cw, g_rg_cb = _rg_bwd(d_ycat, u, s["hh"], p["rg_conv_w"], p["rg_conv_b"],
                                                                  p["rg_wa_b"], p["rg_ba"], p["rg_wx_b"], p["rg_bx"],
                                                                  p["rg_lam"], du)
    grads = dict(rg_conv_w=g_rg_cw, rg_conv_b=g_rg_cb, rg_w_a=g_wa, rg_b_a=g_ba, rg_w_x=g_wx, rg_b_x=g_bx,
                 rg_lambda=g_lam, ml_conv_w=g_ml_cw, ml_conv_b=g_ml_cb, ml_w_qkv=g_wqkv, ml_w_if=g_w_if, ml_b_if=g_b_if,
                 ml_norm_g=g_ml_g, w_out=g_w_out)
    npc = du.shape[0]
    gm_ride = rides["grad_w_in"](grads) if "grad_w_in" in rides else None
    grads["w_in"], got_gm = _grad_matmul(s["h_b"][None], du, npc, lambda b: 0, lambda b: (b + DU_PLANE[0]) % npc,
                                         (d, npc * d), (d, d), lambda b: (0, b), gm_ride)
    in_ride = rides["in_bwd"](grads["w_in"]) if "in_bwd" in rides else None
    (dx, d_shift, d_scale, grads["norm_g"]), got_in = _in_bwd(du, p["w4"], s["x"], dxn, p["norm_g"], p["scale"], in_ride)
    return (dx, grads, jnp.concatenate([d_shift, d_scale, d_gate], axis=1),
            dict(mlstm_bwd=got, grad_w_in=got_gm, in_bwd=got_in))


def _trunk_fwd_bwd(x, target, final_g, layers):
    saved = []
    for p in layers:
        x, s, _, _ = _layer_fwd(x, p)
        saved.append(s)
    dx, g_final, loss = _final_loss(x, final_g, target)
    grads, dmods = [], []
    for layer in reversed(range(len(layers))):
        dx, g, dm, _ = _layer_bwd(dx, layers[layer], saved[layer])
        grads.append(g)
        dmods.append(dm)
    return loss, dx, g_final, grads[::-1], dmods[::-1]


def _me():
    return lax.axis_index("x"), lax.axis_index("y"), lax.axis_index("c")


def _remote(src, dst, send_sem, recv_sem, to):
    return pltpu.make_async_remote_copy(src_ref=src, dst_ref=dst, send_sem=send_sem, recv_sem=recv_sem,
                                        device_id=to, device_id_type=MESH)


def _all_gather8(blocks, space):
    n = len(blocks)

    def body(*refs):
        x_refs, out_refs = refs[:n], refs[n:2 * n]
        send_sems, recv_sems, local_sems = refs[2 * n:]
        x, y, c = _me()
        me, sibling = (x, y, c), (x, y, 1 - c)
        chips = [(1 - x, y), (x, 1 - y), (1 - x, 1 - y)]

        def rows(i, px, py, pc):
            m_per = blocks[i].shape[0]
            return out_refs[i].at[pl.ds((4 * px + 2 * py + pc) * m_per, m_per), :]

        def copy(i, k, blk, to, src=None):
            return _remote(rows(i, *blk) if src is None else src, rows(i, *blk), send_sems.at[7 * i + k],
                           recv_sems.at[7 * i + k], to)

        mine = [pltpu.make_async_copy(x_refs[i], rows(i, *me), local_sems.at[i]) for i in range(n)]
        first = []
        for i in range(n):
            first.append(copy(i, 0, me, sibling, src=x_refs[i]))
            first += [copy(i, 1 + j, me, (*chip, c), src=x_refs[i]) for j, chip in enumerate(chips)]
        for cp in mine + first:
            cp.start()
        passed = []
        for j, chip in enumerate(chips):
            for i in range(n):
                copy(i, 1 + j, (*chip, c), me).wait_recv()
                passed.append(copy(i, 4 + j, (*chip, c), sibling))
                passed[-1].start()
        for i in range(n):
            copy(i, 0, sibling, me).wait_recv()
            for j, chip in enumerate(chips):
                copy(i, 4 + j, (*chip, 1 - c), me).wait_recv()
        for cp in first + passed:
            cp.wait_send()
        for cp in mine:
            cp.wait()

    spec = pl.BlockSpec(memory_space=space)
    return _pcall(
        body, name="all_gather8",
        out_shape=[jax.ShapeDtypeStruct((8 * b.shape[0], b.shape[1]), b.dtype) for b in blocks],
        in_specs=[spec] * n, out_specs=[spec] * n,
        scratch_shapes=[pltpu.SemaphoreType.DMA((7 * n,)), pltpu.SemaphoreType.DMA((7 * n,)),
                        pltpu.SemaphoreType.DMA((n,))],
    )(*blocks)


def _sib_halves(arrs, kinds):
    na = len(arrs)
    ncopies = sum(4 if k == "w_in" else 1 for k in kinds)

    def out_shape(a, kind):
        if kind == "w_in":
            return (a.shape[0], 4, a.shape[1] // 2, a.shape[2] // 4)
        return a.shape[:2] + a.shape[3:] if kind == "w_out" else a.shape[1:]

    def body(*refs):
        src, dst = refs[:na], refs[na:2 * na]
        send_sems, recv_sems = refs[2 * na:]
        x, y, c = _me()
        o = 1 - c
        pairs = []
        for i, kind in enumerate(kinds):
            depth = arrs[i].shape[0]
            if kind == "w_in":
                half, n = arrs[i].shape[1] // 2, arrs[i].shape[2] // 4
                pairs += [(src[i].at[pl.ds(0, depth), pl.ds(o * half, half), pl.ds(s * n, n)],
                           dst[i].at[pl.ds(0, depth), s]) for s in range(4)]
            elif kind == "w_out":
                pairs.append((src[i].at[pl.ds(0, depth), pl.ds(0, 4), o], dst[i]))
            else:
                pairs.append((src[i].at[o], dst[i]))
        copies = [_remote(s_, d_, send_sems.at[k], recv_sems.at[k], (x, y, o)) for k, (s_, d_) in enumerate(pairs)]
        for cp in copies:
            cp.start()
        for cp in copies:
            cp.wait_recv()
        for cp in copies:
            cp.wait_send()

    hbm = pl.BlockSpec(memory_space=pltpu.HBM)
    return _pcall(
        body, name="sib_halves",
        out_shape=[jax.ShapeDtypeStruct(out_shape(a, k), a.dtype) for a, k in zip(arrs, kinds)],
        in_specs=[hbm] * na, out_specs=[hbm] * na,
        scratch_shapes=[pltpu.SemaphoreType.DMA((ncopies,)), pltpu.SemaphoreType.DMA((ncopies,))],
    )(*arrs)


def _sib_fill(boths):
    n = len(boths)

    def body(*refs):
        dst = refs[n:2 * n]
        send_sems, recv_sems = refs[2 * n:]
        x, y, c = _me()
        view = lambda i: dst[i].at[pl.ds(0, boths[i].shape[0]), c]
        copies = [_remote(view(i), view(i), send_sems.at[i], recv_sems.at[i], (x, y, 1 - c)) for i in range(n)]
        for cp in copies:
            cp.start()
        for cp in copies:
            cp.wait_recv()
        for cp in copies:
            cp.wait_send()

    hbm = pl.BlockSpec(memory_space=pltpu.HBM)
    return _pcall(
        body, name="sib_fill",
        out_shape=[jax.ShapeDtypeStruct(b.shape, b.dtype) for b in boths],
        in_specs=[hbm] * n, out_specs=[hbm] * n, input_output_aliases={i: i for i in range(n)},
        scratch_shapes=[pltpu.SemaphoreType.DMA((n,)), pltpu.SemaphoreType.DMA((n,))],
    )(*boths)


def _chip_exchange(arrs):
    n = len(arrs)

    def body(*refs):
        src, dst = refs[:n], refs[n:2 * n]
        send_sems, recv_sems = refs[2 * n:]
        x, y, c = _me()
        me_s = 2 * x + y
        chips = [(1 - x, y), (x, 1 - y), (1 - x, 1 - y)]
        copies = [_remote(src[i].at[2 * px + py], dst[i].at[me_s], send_sems.at[3 * i + k], recv_sems.at[3 * i + k],
                          (px, py, c))
                  for i in range(n) for k, (px, py) in enumerate(chips)]
        for cp in copies:
            cp.start()
        for cp in copies:
            cp.wait_recv()
        for cp in copies:
            cp.wait_send()

    hbm = pl.BlockSpec(memory_space=pltpu.HBM)
    return _pcall(
        body, name="chip_exchange",
        out_shape=[jax.ShapeDtypeStruct(a.shape, a.dtype) for a in arrs],
        in_specs=[hbm] * n, out_specs=[hbm] * n,
        scratch_shapes=[pltpu.SemaphoreType.DMA((3 * n,)), pltpu.SemaphoreType.DMA((3 * n,))],
    )(*arrs)


def _row_tile(rows, cap=4096, mult=16):
    best = None
    for t in range(mult, min(rows, cap) + 1, mult):
        if rows % t == 0:
            best = t
    return rows if best is None else best


def _pair_sum(half, own, own_spec, got, got_spec, out_shape, out_spec, grid):
    def body(_, a_ref, b_ref, o_ref):
        o_ref[...] = (a_ref[...] + b_ref[...].astype(F32)).astype(o_ref.dtype)

    return _pcall(
        body, name="pair_sum",
        grid_spec=pltpu.PrefetchScalarGridSpec(num_scalar_prefetch=1, grid=grid, in_specs=[own_spec, got_spec],
                                               out_specs=out_spec),
        out_shape=out_shape, compiler_params=_seq(len(grid)))(half, own, got)


def _chip_sum(ids, part, met, fill, layer=0, stack=1):
    _, _, rows, n = part.shape
    tr = _row_tile(rows, cap=max(16, (1 << 18) // n))
    first = isinstance(stack, int)

    def body(_, own_ref, a_ref, b_ref, c_ref, *rest):
        acc = own_ref[...].astype(F32) + a_ref[...].astype(F32)
        acc = acc + b_ref[...].astype(F32)
        rest[-1][...] = acc + c_ref[...].astype(F32)

    blk = (None, None, tr, n)
    other = lambda k: pl.BlockSpec(blk, lambda j, ids: ((ids[0] + k) % 4, 0, j, 0))
    in_specs = [pl.BlockSpec(blk, lambda j, ids: (ids[0], 0, j, 0)), other(1), other(2), other(3)]
    return _pcall(
        body, name="chip_sum",
        grid_spec=pltpu.PrefetchScalarGridSpec(
            num_scalar_prefetch=1, grid=(rows // tr,),
            in_specs=in_specs if first else in_specs + [pl.BlockSpec(memory_space=pl.ANY)],
            out_specs=pl.BlockSpec(blk, lambda j, ids: (layer, ids[1] if fill else 0, j, 0))),
        out_shape=jax.ShapeDtypeStruct(((stack,) if first else stack.shape[:1]) + (2 if fill else 1, rows, n), F32),
        input_output_aliases={} if first else {5: 0},
        compiler_params=_seq())(*((ids, part, met, met, met) if first else (ids, part, met, met, met, stack)))


def _ada_mod(c_all, w_ada, b_ada_cols):
    depth, d, n = w_ada.shape
    nb = c_all.shape[0]

    def body(c_ref, w_ref, b_ref, o_ref):
        cv = c_ref[...]
        ca = _bf(cv * _sigmoid(cv))
        o_ref[0] = _dot(ca, _bf(w_ref[0])) + b_ref[0]

    return _pcall(body, name="ada_mod", grid=(depth,),
                  in_specs=[pl.BlockSpec((nb, d), lambda l: (0, 0)), pl.BlockSpec((1, d, n), lambda l: (l, 0, 0)),
                            pl.BlockSpec((1, 1, n), lambda l: (l, 0, 0))],
                  out_specs=pl.BlockSpec((1, nb, n), lambda l: (l, 0, 0)),
                  out_shape=jax.ShapeDtypeStruct((depth, nb, n), F32), compiler_params=_seq())(c_all, w_ada, b_ada_cols)


def _ada_grad(c_all, dmod_cols, rows_all):
    nb, d = c_all.shape
    depth, _, n = dmod_cols.shape
    kinds, n_all = rows_all.shape[1], rows_all.shape[3]

    def body(c_ref, dm_ref, da_ref, gw_ref, gb_ref):
        cv = c_ref[...]
        ca = _bf(cv * _sigmoid(cv))
        gw_ref[0] = _dot_tn(ca, _bf(dm_ref[0]))
        for k in range(kinds):
            gb_ref[0, k] = _colsum(da_ref[0, k])

    return _pcall(body, name="ada_grad", grid=(depth,),
                  in_specs=[pl.BlockSpec((nb, d), lambda l: (0, 0)), pl.BlockSpec((1, nb, n), lambda l: (l, 0, 0)),
                            pl.BlockSpec((1, kinds, nb, n_all), lambda l: (l, 0, 0, 0))],
                  out_specs=[pl.BlockSpec((1, d, n), lambda l: (l, 0, 0)),
                             pl.BlockSpec((1, kinds, 1, n_all), lambda l: (l, 0, 0, 0))],
                  out_shape=[jax.ShapeDtypeStruct((depth, d, n), F32), jax.ShapeDtypeStruct((depth, kinds, 1, n_all), F32)],
                  compiler_params=_seq())(c_all, dmod_cols, rows_all)


def _adamw(items, ride=None):
    two_d = [tuple(t.reshape(w.size // w.shape[-1], w.shape[-1]) for t in (w, g, m, v)) for w, g, m, v in items]
    n = len(items)
    if n == 1:
        rows, cols = two_d[0][0].shape
        tr = _row_tile(rows, cap=max(8, (1 << 18) // cols), mult=8)
        blocks = [pl.BlockSpec((tr, cols), lambda i: (i, 0))]
        grid = (rows // tr,)
    else:
        blocks = [pl.BlockSpec(t[0].shape, lambda i: (0, 0)) for t in two_d]
        grid = (1,)

    def body(*refs):
        for k in range(n):
            w_ref, g_ref, m_ref, v_ref = refs[4 * k:4 * k + 4]
            d_ref, mo_ref, vo_ref = refs[4 * n + 3 * k:4 * n + 3 * k + 3]
            gv = g_ref[...]
            mn = ADAM_B1 * m_ref[...] + (1.0 - ADAM_B1) * gv
            vn = ADAM_B2 * v_ref[...] + (1.0 - ADAM_B2) * (gv * gv)
            m_hat = mn / (1.0 - ADAM_B1 ** ADAM_STEP)
            v_hat = vn / (1.0 - ADAM_B2 ** ADAM_STEP)
            d_ref[...] = -ADAM_LR * (m_hat / (jnp.sqrt(v_hat) + ADAM_EPS) + ADAM_WD * w_ref[...])
            mo_ref[...] = mn
            vo_ref[...] = vn

    outs, got = _pcall_ride(
        body, ride, name="adamw", grid=grid,
        in_specs=[b for b in blocks for _ in range(4)], out_specs=[b for b in blocks for _ in range(3)],
        out_shape=[jax.ShapeDtypeStruct(t[0].shape, F32) for t in two_d for _ in range(3)],
        compiler_params=_seq(), args=tuple(a for t in two_d for a in t))
    return [tuple(o.reshape(items[k][0].shape) for o in outs[3 * k:3 * k + 3]) for k in range(n)], got


WEIGHTS = ["norm_g", "w_ada", "b_ada", "w_in", "rg_conv_w", "rg_conv_b", "rg_w_a", "rg_b_a", "rg_w_x", "rg_b_x",
           "rg_lambda", "ml_conv_w", "ml_conv_b", "ml_w_q", "ml_w_k", "ml_w_v", "ml_w_if", "ml_b_if", "ml_norm_g",
           "w_out", "final_g"]
SMALL_SHARDED = {"ml_w_qkv": 2, "rg_conv_w": 1, "ml_conv_w": 1, "ml_w_if": 0}
REPLICATED = ["rg_w_a", "rg_w_x", "rg_conv_b", "rg_b_a", "rg_b_x", "rg_lambda", "ml_conv_b", "ml_norm_g", "ml_b_if"]
LANES = 128


def _to_pieces(g, axis):
    shp = g.shape
    g = g.reshape(shp[:axis] + (4, 2, shp[axis] // 8) + shp[axis + 1:])
    g = jnp.moveaxis(g, (axis, axis + 1), (0, 1))
    return g.reshape(4, 2, -1)


def _from_pieces(p, shard_shape, axis):
    k = p.shape[0]
    rest = shard_shape[:axis] + (shard_shape[axis] // k,) + shard_shape[axis + 1:]
    t = jnp.moveaxis(p.reshape((k,) + rest), 0, axis)
    return t.reshape(shard_shape)


def _pad_rows(flat, mult):
    n = flat.shape[-1]
    pad = (-n) % mult
    if pad:
        flat = jnp.concatenate([flat, jnp.zeros(flat.shape[:-1] + (pad,), flat.dtype)], axis=-1)
    return flat


def kernel(x, c, norm_g, w_ada, b_ada, w_in, rg_conv_w, rg_conv_b, rg_w_a, rg_b_a, rg_w_x, rg_b_x, rg_lambda, ml_conv_w, ml_conv_b, ml_w_q, ml_w_k, ml_w_v, ml_w_if, ml_b_if, ml_norm_g, w_out, final_g, loss_target, m_norm_g, m_w_ada, m_b_ada, m_w_in, m_rg_conv_w, m_rg_conv_b, m_rg_w_a, m_rg_b_a, m_rg_w_x, m_rg_b_x, m_rg_lambda, m_ml_conv_w, m_ml_conv_b, m_ml_w_q, m_ml_w_k, m_ml_w_v, m_ml_w_if, m_ml_b_if, m_ml_norm_g, m_w_out, m_final_g, v_norm_g, v_w_ada, v_b_ada, v_w_in, v_rg_conv_w, v_rg_conv_b, v_rg_w_a, v_rg_b_a, v_rg_w_x, v_rg_b_x, v_rg_lambda, v_ml_conv_w, v_ml_conv_b, v_ml_w_q, v_ml_w_k, v_ml_w_v, v_ml_w_if, v_ml_b_if, v_ml_norm_g, v_w_out, v_final_g):
    given = dict(locals())
    ax, ay, ac = lax.axis_index("x"), lax.axis_index("y"), lax.axis_index("c")
    chip = 2 * ax + ay
    me = 2 * chip + ac
    depth, d = norm_g.shape
    n_ada = w_ada.shape[2]
    pick = lambda a, i, axis=0: lax.dynamic_index_in_dim(a, i, axis, keepdims=False)

    convs = jnp.stack([rg_conv_w, ml_conv_w])
    n_conv = 2 * depth * CONV_WIDTH // 4
    blk = jnp.concatenate([c, convs.reshape(n_conv, d), jnp.zeros((8 - 1 - n_conv, d), F32)], axis=0)
    g0 = _all_gather8([blk], pltpu.VMEM)[0].reshape(8, 8, d)
    c_all = g0[:, 0, :]
    conv_full = g0[0::2, 1:1 + n_conv].reshape(4, 2, depth, CONV_WIDTH, d // 4)
    conv_full = conv_full.transpose(1, 2, 3, 0, 4).reshape(2, depth, CONV_WIDTH, d)

    b_cols = lax.dynamic_slice_in_dim(b_ada, chip * n_ada, n_ada, axis=1)[:, None, :]
    mod_part = _ada_mod(c_all, w_ada, b_cols)
    g1 = _all_gather8([mod_part.transpose(1, 0, 2).reshape(8, depth * n_ada)], pltpu.VMEM)[0]
    g1 = g1.reshape(8, 8, depth, n_ada)[0::2]
    mod_me = pick(g1.transpose(1, 2, 0, 3).reshape(8, depth, 4 * n_ada), me)

    def half_of(w, axis):
        n = w.shape[axis] // 2
        return lax.dynamic_slice_in_dim(w, ac * n, n, axis).astype(BF16)

    n_sh = w_in.shape[2]
    heads, hd_cut, hd = ml_w_q.shape[1:]

    def blocks_of(l):
        wqkv = jnp.stack([ml_w_q[l], ml_w_k[l], ml_w_v[l]])
        return [half_of(w_in[l], 0), half_of(w_out[l], 0), half_of(wqkv, 2).reshape(-1, hd), half_of(ml_w_if[l], 0)]

    def layer_of(l, w4, rest):
        return dict(
            norm_g=norm_g[l][None], shift=mod_me[l, 0:d][None], scale=mod_me[l, d:2 * d][None],
            gate=mod_me[l, 2 * d:3 * d][None], w4=w4.reshape(4, d, n_sh),
            rg_conv_w=conv_full[0, l], rg_conv_b=rg_conv_b[l][None], rg_wa_b=_bf(rg_w_a[l]), rg_ba=rg_b_a[l][None],
            rg_wx_b=_bf(rg_w_x[l]), rg_bx=rg_b_x[l][None], rg_lam=rg_lambda[l][None],
            ml_conv_w=conv_full[1, l], ml_conv_b=ml_conv_b[l][None], b_if=ml_b_if[l][None], b_ift=ml_b_if[l][:, None],
            ml_g=ml_norm_g[l][None], **rest)

    def rest_of(gathered):
        w_out_b, wqkv_g, wif = gathered
        return dict(w_out_b=w_out_b, wqkv_b=_from_pieces(wqkv_g.reshape(8, -1), (3, heads, hd, hd), 2), wif_b=wif,
                    wift_b=wif.T)

    landing = lambda b: jax.ShapeDtypeStruct((4, 2) + b.shape, b.dtype)
    whole = lambda landed: [t.reshape(-1, t.shape[-1]) for t in _sib_fill(landed)]
    first = blocks_of(0)
    p = layer_of(0, _all_gather8(first[:1], pltpu.HBM)[0], {})
    rides = dict(ln_inproj=Ride(first[1:], [landing(b) for b in first[1:]], False))
    late = lambda landed: rest_of(whole(landed))
    layers, saved = [], []
    xl = x[0]
    for l in range(depth):
        if l + 1 < depth:
            nxt = blocks_of(l + 1)
            rides["rg_fwd"] = Ride(nxt[:1], [landing(nxt[0])], False)
            rides["mlstm_fwd"] = Ride(nxt[1:], [landing(b) for b in nxt[1:]], False)
        xl, s, p, got = _layer_fwd(xl, p, rides, late)
        layers.append(p)
        saved.append(s)
        if l + 1 < depth:
            nxt_whole = whole(list(got["rg_fwd"]) + list(got["mlstm_fwd"]))
            p = layer_of(l + 1, nxt_whole[0], rest_of(nxt_whole[1:]))
            rides, late = {}, None
    dx, g_final, loss = _final_loss(xl, final_g[None], loss_target[0])

    half = ac.reshape(1)
    ids = jnp.stack([chip, ac])
    r_out = w_out.shape[1] // 2

    def partial_in(g_w_in):
        (got_in,) = _sib_halves([g_w_in], ["w_in"])
        return _pair_sum(
            half, g_w_in, pl.BlockSpec((None, d // 2, n_sh), lambda s, h: (0, h[0], s)),
            got_in, pl.BlockSpec((None, None, d // 2, n_sh), lambda s, h: (0, s, 0, 0)),
            jax.ShapeDtypeStruct((4, 1, d // 2, n_sh), BF16),
            pl.BlockSpec((None, None, d // 2, n_sh), lambda s, h: (s, 0, 0, 0)), (4,))

    def pair_out(g_out5, got_out):
        return _pair_sum(
            half, g_out5, pl.BlockSpec((None, None, None, r_out, d), lambda s, h: (0, s, h[0], 0, 0)),
            got_out, pl.BlockSpec((None, None, r_out, d), lambda s, h: (0, s, 0, 0)),
            jax.ShapeDtypeStruct((4, 1, r_out, d), BF16),
            pl.BlockSpec((None, None, r_out, d), lambda s, h: (s, 0, 0, 0)), (4,))

    def pair_slab(slab, got, dtype):
        rows = got.shape[0] // 4
        blk = pl.BlockSpec((rows, LANES), lambda s, h: (s, 0))
        return _pair_sum(half, slab, pl.BlockSpec((None, rows, LANES), lambda s, h: (h[0], s, 0)), got, blk,
                         jax.ShapeDtypeStruct((4 * rows, LANES), dtype), blk, (4,)).reshape(4, 1, rows, LANES)

    exchange = lambda parts_l: Ride(parts_l, [jax.ShapeDtypeStruct(t.shape, t.dtype) for t in parts_l], True)
    grads, dmods, parts, mets = [None] * depth, [None] * depth, [None] * depth, [None] * depth
    small = {}

    def early_exchange(first_grads):
        every = [first_grads] + grads[1:]
        sm = jnp.concatenate([_to_pieces(every[l][name], axis) for l in range(depth)
                              for name, axis in SMALL_SHARDED.items()], axis=-1)
        sm = _pad_rows(sm, 16 * LANES)
        sm = sm.transpose(1, 0, 2).reshape(2, -1, LANES)
        rep = [every[l][name].reshape(-1) for l in range(depth) for name in REPLICATED[:-1]]
        rep += [_pad_rows(every[l]["ml_b_if"].reshape(-1), LANES) for l in range(depth)]
        rep += [g_final.reshape(-1), loss.reshape(-1)]
        rep = _pad_rows(jnp.concatenate(rep), 8 * 8 * LANES)
        rep = rep.reshape(4, 2, -1, LANES).transpose(1, 0, 2, 3).reshape(2, -1, LANES)
        g_out5 = first_grads["w_out"].reshape(1, 4, 2, r_out, d)
        got_out, got_sm, got_rep = _sib_halves([g_out5, sm, rep], ["w_out", "slab", "slab"])
        small["parts"] = [pair_out(g_out5, got_out), pair_slab(sm, got_sm, BF16), pair_slab(rep, got_rep, F32)]
        return exchange(small["parts"])

    def last_exchange(g_w_in):
        small["part_in"] = partial_in(g_w_in)
        return exchange([small["part_in"]])

    rides = {}
    for l in reversed(range(depth)):
        if l == 0:
            rides.update(grad_w_in=early_exchange, in_bwd=last_exchange)
        dx, grads[l], dmods[l], got = _layer_bwd(dx, layers[l], saved[l], rides)
        if "mlstm_bwd" in rides:
            mets[l + 1] = got["mlstm_bwd"]
        if l > 0:
            g_out5 = grads[l]["w_out"].reshape(1, 4, 2, r_out, d)
            (got_out,) = _sib_halves([g_out5], ["w_out"])
            parts[l] = [partial_in(grads[l]["w_in"]), pair_out(g_out5, got_out)]
            rides = dict(mlstm_bwd=exchange(parts[l]))
    part_out, part_sm, part_rep = small["parts"]
    met_out, met_sm, met_rep = got["grad_w_in"]
    parts[0], mets[0] = [small["part_in"], part_out], [got["in_bwd"][0], met_out]
    n_rep = part_rep.shape[2]

    pad = lambda t: jnp.concatenate([t, jnp.zeros((1, 2 * d), F32)], axis=1)
    rows = [r for l in range(depth) for r in (dmods[l], pad(grads[l]["norm_g"]))]
    blk = jnp.concatenate(rows + [jnp.zeros((8 - 2 * depth, 3 * d), F32)], axis=0)
    rows_all = _all_gather8([blk], pltpu.VMEM)[0].reshape(8, 8, 3 * d)[:, :2 * depth]
    rows_all = rows_all.transpose(1, 0, 2).reshape(depth, 2, 8, 3 * d)
    dm_cols = lax.dynamic_slice_in_dim(rows_all[:, 0], chip * n_ada, n_ada, axis=2)
    g_w_ada, summed = _ada_grad(c_all, dm_cols, rows_all)

    g = dict(w_ada=g_w_ada, b_ada=summed[:, 0, 0], norm_g=summed[:, 1, 0, :d])
    item = lambda name: (given[name], g[name], given["m_" + name], given["v_" + name])
    both_in, both_out = depth, depth
    for l in range(depth):
        both_in = _chip_sum(ids, parts[l][0], mets[l][0], True, l, both_in)
        both_out = _chip_sum(ids, parts[l][1], mets[l][1], True, l, both_out)
    both_in, both_out, both_sm = _sib_fill([both_in, both_out, _chip_sum(ids, part_sm, met_sm, True)])
    red_rep = _chip_sum(ids, part_rep, met_rep, False).reshape(n_rep, LANES)
    rep_all = _all_gather8([red_rep], pltpu.VMEM)[0].reshape(-1)

    g.update(w_in=both_in.reshape(w_in.shape), w_out=both_out.reshape(w_out.shape))
    shard = both_sm.reshape(2, -1)
    off = 0
    per_layer = {name: [] for name in SMALL_SHARDED}
    for l in range(depth):
        for name, axis in SMALL_SHARDED.items():
            shp = (3,) + ml_w_q.shape[1:] if name == "ml_w_qkv" else given[name].shape[1:]
            n = grads[l][name].size // 8
            per_layer[name].append(_from_pieces(shard[:, off:off + n], shp, axis))
            off += n
    for name in SMALL_SHARDED:
        g[name] = jnp.stack(per_layer[name])
    for i, name in enumerate(["ml_w_q", "ml_w_k", "ml_w_v"]):
        g[name] = g["ml_w_qkv"][:, i]
    off = 0
    per_layer = {name: [] for name in REPLICATED}
    for l in range(depth):
        for name in REPLICATED[:-1]:
            n = given[name][l].size
            per_layer[name].append(rep_all[off:off + n].reshape(given[name].shape[1:]))
            off += n
    for l in range(depth):
        n = given["ml_b_if"][l].size
        per_layer["ml_b_if"].append(rep_all[off:off + n])
        off += LANES
    for name in REPLICATED:
        g[name] = jnp.stack(per_layer[name])
    g["final_g"] = rep_all[off:off + d]
    loss_all = rep_all[off + d]

    stepped = {}
    rg_mats, ml_mats = ["rg_w_a", "rg_w_x"], ["ml_w_q", "ml_w_k", "ml_w_v"]
    vectors = [n for n in WEIGHTS if n not in ["w_ada", "w_in", "w_out"] + rg_mats + ml_mats]
    for names in (["w_ada"], ["w_in"], ["w_out"], rg_mats, ml_mats, vectors):
        stepped.update(zip(names, _adamw([item(name) for name in names])[0]))
    deltas, new_m, new_v = zip(*[stepped[name] for name in WEIGHTS])
    return (loss_all, dx[None], *[g[name] for name in WEIGHTS], *deltas, *new_m, *new_v)
```

```python
import functools
from typing import NamedTuple

import jax
import jax.numpy as jnp
from jax import lax
from jax.experimental import pallas as pl
from jax.experimental.pallas import tpu as pltpu

F32 = jnp.float32
BF16 = jnp.bfloat16

EPS = 1e-6
RG_C = 8.0
CONV_WIDTH = 4
ML_CHUNK = 256
HALO = 8
ADAM_LR = 0.001
ADAM_B1 = 0.9
ADAM_B2 = 0.999
ADAM_EPS = 1e-08
ADAM_WD = 0.01
ADAM_STEP = 10
MESH = pl.DeviceIdType.MESH


def _pcall(body, **kw):
    return pl.pallas_call(body, **kw)


class Ride(NamedTuple):
    srcs: list
    dst_shapes: list
    sliced: bool


def _pcall_ride(body, ride, *, grid, in_specs, out_specs, out_shape, args, scratch_shapes=(), **kw):
    n_in, n_out, n_scr = len(in_specs), len(out_specs), len(scratch_shapes)
    if ride is None:
        res = _pcall(body, grid=grid, in_specs=in_specs, out_specs=out_specs, out_shape=out_shape,
                     scratch_shapes=list(scratch_shapes), **kw)(*args)
        return res, []
    nr = len(ride.srcs)

    def riding(*refs):
        ins, rsrc = refs[:n_in], refs[n_in:n_in + nr]
        outs, rdst = refs[n_in + nr:n_in + nr + n_out], refs[n_in + nr + n_out:n_in + 2 * nr + n_out]
        scr = refs[n_in + 2 * nr + n_out:n_in + 2 * nr + n_out + n_scr]
        send_sems, recv_sems, local_sems = refs[n_in + 2 * nr + n_out + n_scr:]
        x, y, c = _me()
        me_s = 2 * x + y
        chips = [(1 - x, y), (x, 1 - y), (1 - x, 1 - y)]
        copies, local = [], []
        for i in range(nr):
            for k, (px, py) in enumerate(chips):
                src = rsrc[i].at[2 * px + py] if ride.sliced else rsrc[i]
                dst = rdst[i].at[me_s] if ride.sliced else rdst[i].at[me_s, c]
                copies.append(_remote(src, dst, send_sems.at[3 * i + k], recv_sems.at[3 * i + k], (px, py, c)))
            if not ride.sliced:
                local.append(pltpu.make_async_copy(rsrc[i], rdst[i].at[me_s, c], local_sems.at[i]))
        first = functools.reduce(jnp.logical_and, [pl.program_id(a) == 0 for a in range(len(grid))])
        last = functools.reduce(jnp.logical_and, [pl.program_id(a) == grid[a] - 1 for a in range(len(grid))])

        @pl.when(first)
        def _():
            for cp in copies + local:
                cp.start()

        body(*ins, *outs, *scr)

        @pl.when(last)
        def _():
            for cp in copies:
                cp.wait_recv()
            for cp in copies:
                cp.wait_send()
            for cp in local:
                cp.wait()

    hbm = pl.BlockSpec(memory_space=pltpu.HBM)
    res = _pcall(
        riding, grid=grid, in_specs=list(in_specs) + [hbm] * nr, out_specs=list(out_specs) + [hbm] * nr,
        out_shape=list(out_shape) + list(ride.dst_shapes),
        scratch_shapes=list(scratch_shapes) + [pltpu.SemaphoreType.DMA((3 * nr,)), pltpu.SemaphoreType.DMA((3 * nr,)),
                                               pltpu.SemaphoreType.DMA((nr,))], **kw)(*args, *ride.srcs)
    return res[:n_out], res[n_out:]


def _seq(n=1):
    return pltpu.CompilerParams(dimension_semantics=("arbitrary",) * n)


def _dot(a, b):
    return jnp.dot(a, b, preferred_element_type=F32)


def _dot_nt(a, b):
    return lax.dot_general(a, b, (((1,), (1,)), ((), ())), preferred_element_type=F32)


def _dot_tn(a, b):
    return lax.dot_general(a, b, (((0,), (0,)), ((), ())), preferred_element_type=F32)


def _bf(x):
    return x.astype(BF16)


def _sigmoid(x):
    return 0.5 * jnp.tanh(0.5 * x) + 0.5


def _log1p(z):
    u = 1.0 + z
    return jnp.where(u == 1.0, z, jnp.log(u) * (z / jnp.where(u == 1.0, 1.0, u - 1.0)))


def _softplus(x):
    return jnp.maximum(x, 0.0) + _log1p(jnp.exp(-jnp.abs(x)))


def _log_sigmoid(x):
    return -_softplus(-x)


def _one_minus_sq(a, log_a):
    x = 2.0 * log_a
    small = -x * (1.0 + x * (0.5 + x * (1.0 / 6.0)))
    return jnp.where(x > -0.004, small, 1.0 - a * a)


def _dsilu(x, s):
    return s * (1.0 + x * (1.0 - s))


def _rowsum(x):
    return jnp.sum(x, axis=1, keepdims=True)


def _colsum(x):
    return jnp.sum(x, axis=0, keepdims=True)


def _shift_down(win, s):
    return win if s == 0 else pltpu.roll(win, s, 0)


def _shift_up(win, s):
    return win if s == 0 else pltpu.roll(win, win.shape[0] - s, 0)


def _conv_taps(win):
    return [_shift_down(win, CONV_WIDTH - 1 - k)[HALO:] for k in range(CONV_WIDTH)]


def _conv_fwd(taps, w_ref, b_ref):
    acc = b_ref[...] + w_ref[CONV_WIDTH - 1:CONV_WIDTH, :] * taps[CONV_WIDTH - 1]
    for k in range(CONV_WIDTH - 1):
        acc = acc + w_ref[k:k + 1, :] * taps[k]
    return acc


def _split3(x):
    hi = _bf(x)
    r1 = x - hi.astype(F32)
    mid = _bf(r1)
    lo = _bf(r1 - mid.astype(F32))
    return hi, mid, lo


def _tri_dot_left(tri, x):
    hi, mid, lo = _split3(x)
    return _dot(tri, hi) + _dot(tri, mid) + _dot(tri, lo)


def _tri_dot_right(x, tri):
    hi, mid, lo = _split3(x)
    return _dot(hi, tri) + _dot(mid, tri) + _dot(lo, tri)


def _tile(n, want):
    t = min(n, want)
    assert n % t == 0
    return t


def _ln_inproj(x, g, scale, shift, w4, ride=None):
    s_len, d = x.shape
    nj, _, nsh = w4.shape
    tm = _tile(s_len, 1024)

    def body(x_ref, g_ref, sc_ref, sh_ref, w_ref, h_ref, u_ref, hs):
        @pl.when(pl.program_id(1) == 0)
        def _():
            xv = x_ref[...]
            r = lax.rsqrt(jnp.mean(xv * xv, axis=-1, keepdims=True) + EPS)
            hv = (xv * r * g_ref[...]) * (1.0 + sc_ref[...]) + sh_ref[...]
            hs[...] = _bf(hv)
            h_ref[...] = hs[...]

        u_ref[...] = _dot(hs[...], w_ref[0])

    vec = pl.BlockSpec((1, d), lambda i, j: (0, 0))
    return _pcall_ride(
        body, ride, name="ln_inproj", grid=(s_len // tm, nj),
        in_specs=[pl.BlockSpec((tm, d), lambda i, j: (i, 0)), vec, vec, vec,
                  pl.BlockSpec((1, d, nsh), lambda i, j: (j, 0, 0))],
        out_specs=[pl.BlockSpec((tm, d), lambda i, j: (i, 0)), pl.BlockSpec((tm, nsh), lambda i, j: (i, j))],
        out_shape=[jax.ShapeDtypeStruct((s_len, d), BF16), jax.ShapeDtypeStruct((s_len, nj * nsh), F32)],
        scratch_shapes=[pltpu.VMEM((tm, d), BF16)],
        compiler_params=_seq(2),
        args=(x, g, scale, shift, w4))


def _rg_gates(xc, wa_ref, ba_ref, wx_ref, bx_ref, lam_ref):
    heads, hd, _ = wa_ref.shape
    xb = _bf(xc)
    ga = jnp.concatenate([_dot(xb[:, h * hd:(h + 1) * hd], wa_ref[h]) for h in range(heads)], axis=1) + ba_ref[...]
    gx = jnp.concatenate([_dot(xb[:, h * hd:(h + 1) * hd], wx_ref[h]) for h in range(heads)], axis=1) + bx_ref[...]
    r = _sigmoid(ga)
    ig = _sigmoid(gx)
    sp = _softplus(-lam_ref[...])
    log_a = (-RG_C) * r * sp
    a = jnp.exp(log_a)
    mult = jnp.sqrt(_one_minus_sq(a, log_a))
    return r, ig, sp, log_a, a, mult


def _scan_groups(a, u, reverse):
    n, c = a.shape
    a = a.reshape(n // 8, 8, c)
    u = u.reshape(n // 8, 8, c)
    row = lax.broadcasted_iota(jnp.int32, a.shape, 1)
    for k in (1, 2, 4):
        sft = 8 - k if reverse else k
        a_sh, u_sh = pltpu.roll(a, sft, 1), pltpu.roll(u, sft, 1)
        ok = row < 8 - k if reverse else row >= k
        u = jnp.where(ok, a * u_sh + u, u)
        a = jnp.where(ok, a * a_sh, a)
    return a.reshape(n, c), u.reshape(n, c)


def _rg_fwd(u, conv_w, conv_b, wa_b, ba, wx_b, bx, lam, ride=None):
    s_len = u.shape[0]
    d = conv_w.shape[1]
    tm = _tile(s_len, 256)
    per = tm // HALO

    def body(x_ref, xp_ref, z_ref, cw_ref, cb_ref, wa_ref, ba_ref, wx_ref, bx_ref, lam_ref,
             hh_ref, y_ref, carry):
        i = pl.program_id(0)

        @pl.when(i == 0)
        def _():
            carry[...] = jnp.zeros_like(carry)

        prev = jnp.where(i == 0, 0.0, xp_ref[...])
        xc = _conv_fwd(_conv_taps(jnp.concatenate([prev, x_ref[...]], axis=0)), cw_ref, cb_ref)
        _, ig, _, _, a, mult = _rg_gates(xc, wa_ref, ba_ref, wx_ref, bx_ref, lam_ref)
        ca, cu = _scan_groups(a, mult * (ig * xc), reverse=False)
        c = carry[0:1, :]
        for j in range(per):
            blk = ca[j * 8:(j + 1) * 8] * c + cu[j * 8:(j + 1) * 8]
            hh_ref[j * 8:(j + 1) * 8, :] = blk
            c = blk[7:8]
        carry[0:1, :] = c
        z = z_ref[...]
        y_ref[0] = _bf(hh_ref[...] * (z * _sigmoid(z)))

    vec = pl.BlockSpec((1, d), lambda i: (0, 0))
    whole3 = lambda a: pl.BlockSpec(a.shape, lambda i: (0, 0, 0))
    return _pcall_ride(
        body, ride, name="rg_fwd", grid=(s_len // tm,),
        in_specs=[pl.BlockSpec((tm, d), lambda i: (i, 0)),
                  pl.BlockSpec((HALO, d), lambda i: (jnp.maximum(i * per - 1, 0), 0)),
                  pl.BlockSpec((tm, d), lambda i: (i, 1)),
                  pl.BlockSpec((CONV_WIDTH, d), lambda i: (0, 0)), vec,
                  whole3(wa_b), vec, whole3(wx_b), vec, vec],
        out_specs=[pl.BlockSpec((tm, d), lambda i: (i, 0)), pl.BlockSpec((1, tm, d), lambda i: (0, i, 0))],
        out_shape=[jax.ShapeDtypeStruct((s_len, d), F32), jax.ShapeDtypeStruct((2, s_len, d), BF16)],
        scratch_shapes=[pltpu.VMEM((8, d), F32)],
        compiler_params=_seq(),
        args=(u, u, u, conv_w, conv_b, wa_b, ba, wx_b, bx, lam))


def _ml_pre(u, conv_w, conv_b, wqkv_b, wif_b, wift_b, b_if, b_ift):
    s_len = u.shape[0]
    d = conv_w.shape[1]
    _, heads, hd, _ = wqkv_b.shape
    ng = 2 * heads
    tm = _tile(s_len, 256)
    per = tm // HALO

    def body(x_ref, xp_ref, cw_ref, cb_ref, w_ref, wif_ref, wift_ref, bif_ref, bift_ref,
             qkv_ref, gt_ref, gtt_ref, bc_ref, bct_ref):
        i = pl.program_id(0)
        prev = jnp.where(i == 0, 0.0, xp_ref[...])
        xm = x_ref[...]
        pre = _conv_fwd(_conv_taps(jnp.concatenate([prev, xm], axis=0)), cw_ref, cb_ref)
        xcb = _bf(pre * _sigmoid(pre))
        xmb = _bf(xm)
        for h in range(heads):
            hs = slice(h * hd, (h + 1) * hd)
            qkv_ref[0, :, hs] = _bf(_dot(xcb[:, hs], w_ref[0, h]))
            qkv_ref[1, :, hs] = _bf(_dot(xcb[:, hs], w_ref[1, h]))
            qkv_ref[2, :, hs] = _bf(_dot(xmb[:, hs], w_ref[2, h]))
        qb, kb, vb = qkv_ref[0], qkv_ref[1], qkv_ref[2]
        gt = (_dot(qb, wif_ref[0:d, :]) + _dot(kb, wif_ref[d:2 * d, :]) + _dot(vb, wif_ref[2 * d:3 * d, :])
              + bif_ref[...])
        gtt = (_dot_nt(wift_ref[:, 0:d], qb) + _dot_nt(wift_ref[:, d:2 * d], kb)
               + _dot_nt(wift_ref[:, 2 * d:3 * d], vb) + bift_ref[...])
        gt_ref[...] = gt
        gtt_ref[...] = gtt
        r = lax.broadcasted_iota(jnp.int32, (tm, tm), 0)
        c = lax.broadcasted_iota(jnp.int32, (tm, tm), 1)
        same = (r // ML_CHUNK) == (c // ML_CHUNK)
        bc_ref[...] = _tri_dot_left(((r >= c) & same).astype(BF16), _log_sigmoid(gt))
        bct_ref[...] = _tri_dot_right(_log_sigmoid(gtt), ((r <= c) & same).astype(BF16))

    vec = pl.BlockSpec((1, d), lambda i: (0, 0))
    whole2 = lambda a: pl.BlockSpec(a.shape, lambda i: (0, 0))
    col = pl.BlockSpec((tm, ng), lambda i: (i, 0))
    row = pl.BlockSpec((ng, tm), lambda i: (0, i))
    return _pcall(
        body, name="ml_pre", grid=(s_len // tm,),
        in_specs=[pl.BlockSpec((tm, d), lambda i: (i, 2)),
                  pl.BlockSpec((HALO, d), lambda i: (jnp.maximum(i * per - 1, 0), 2)),
                  pl.BlockSpec((CONV_WIDTH, d), lambda i: (0, 0)), vec,
                  pl.BlockSpec(wqkv_b.shape, lambda i: (0, 0, 0, 0)), whole2(wif_b), whole2(wift_b), whole2(b_if),
                  whole2(b_ift)],
        out_specs=[pl.BlockSpec((3, tm, d), lambda i: (0, i, 0)), col, row, col, row],
        out_shape=[jax.ShapeDtypeStruct((3, s_len, d), BF16), jax.ShapeDtypeStruct((s_len, ng), F32),
                   jax.ShapeDtypeStruct((ng, s_len), F32), jax.ShapeDtypeStruct((s_len, ng), F32),
                   jax.ShapeDtypeStruct((ng, s_len), F32)],
        compiler_params=_seq(),
    )(u, u, conv_w, conv_b, wqkv_b, wif_b, wift_b, b_if, b_ift)


def _chunk_gates(gt, gtt, bc, bct, h, heads):
    li_c = gt[:, h:h + 1]
    li_r = gtt[h:h + 1, :]
    gf_c = gt[:, heads + h:heads + h + 1]
    b_c = bc[:, heads + h:heads + h + 1]
    b_r = bct[heads + h:heads + h + 1, :]
    return li_c, li_r, gf_c, b_c, b_r


def _chunk_weights(li_c, li_r, b_c, b_r, m_prev, causal):
    lc = b_c.shape[0]
    b_last = b_c[lc - 1:lc, :]
    dmat = jnp.where(causal, b_c - b_r + li_r, -jnp.inf)
    m_inter = b_c + m_prev
    m_t = jnp.maximum(m_inter, jnp.max(dmat, axis=1, keepdims=True))
    w_intra = jnp.exp(dmat - m_t)
    w_inter = jnp.exp(m_inter - m_t)
    g_c = b_last - b_c + li_c
    m_new = jnp.maximum(b_last + m_prev, jnp.max(g_c, axis=0, keepdims=True))
    w_state = jnp.exp(g_c - m_new)
    decay = jnp.exp(b_last + m_prev - m_new)
    return m_t, w_intra, w_inter, m_new, w_state, decay


def _tri_masks(lc):
    r = lax.broadcasted_iota(jnp.int32, (lc, lc), 0)
    c = lax.broadcasted_iota(jnp.int32, (lc, lc), 1)
    causal = r >= c
    return causal, causal.astype(BF16), (r <= c).astype(BF16)


def _mlstm_fwd(qkv, gates, u, ml_g, ycat, ride=None):
    _, s_len, d = qkv.shape
    ng = gates[0].shape[1]
    heads = ng // 2
    hd = d // heads
    lc = ML_CHUNK
    nc = s_len // lc
    kscale = hd ** -0.5

    def body(qkv_ref, gt_ref, gtt_ref, bc_ref, bct_ref, o_ref, z_ref, g_ref, _, cell_ref, y_ref, cst_ref, nst_ref,
             mst_ref, cs, ns, ms):
        @pl.when(pl.program_id(0) == 0)
        def _():
            cs[...] = jnp.zeros_like(cs)
            ns[...] = jnp.zeros_like(ns)
            ms[...] = jnp.zeros_like(ms)

        causal = _tri_masks(lc)[0]
        gtv, gttv, bcv, bctv = gt_ref[...], gtt_ref[...], bc_ref[...], bct_ref[...]
        old = [(cs[h], ns[h], ms[h]) for h in range(heads)]
        new, cells, ys = [], [], []
        for h in range(heads):
            hs = slice(h * hd, (h + 1) * hd)
            li_c, li_r, _, b_c, b_r = _chunk_gates(gtv, gttv, bcv, bctv, h, heads)
            c_old, n_old, m_old = old[h]
            m_prev = m_old[:, 0:1]
            m_t, w_intra, w_inter, m_new, w_state, decay = _chunk_weights(li_c, li_r, b_c, b_r, m_prev, causal)
            qb = qkv_ref[0, :, hs]
            ks = qkv_ref[1, :, hs].astype(F32) * kscale
            kb = _bf(ks)
            vb = qkv_ref[2, :, hs]
            s = _dot_nt(qb, kb) * w_intra
            num = _dot(_bf(s), vb) + w_inter * _dot(qb, _bf(c_old))
            den = _rowsum(s) + w_inter * _rowsum(qb.astype(F32) * n_old)
            cell = num / jnp.maximum(jnp.abs(den), jnp.exp(-m_t))
            kw = ks * w_state
            new.append((decay * c_old + _dot_tn(_bf(kw), vb), decay * n_old + _colsum(kw),
                        jnp.broadcast_to(m_new, m_old.shape)))
            cells.append(cell)
            hm = _sigmoid(o_ref[:, hs]) * cell
            hn = hm * lax.rsqrt(jnp.mean(hm * hm, axis=-1, keepdims=True) + EPS)
            z = z_ref[:, hs]
            ys.append(_bf((hn * g_ref[:, hs]) * (z * _sigmoid(z))))
        for h in range(heads):
            cst_ref[0, h] = _bf(old[h][0])
            nst_ref[0, h] = old[h][1]
            mst_ref[0, h] = old[h][2]
            cs[h], ns[h], ms[h] = new[h]
        cell_ref[...] = jnp.concatenate(cells, axis=1)
        y_ref[0] = jnp.concatenate(ys, axis=1)

    row = pl.BlockSpec((lc, d), lambda c: (c, 0))
    gcol = pl.BlockSpec((lc, ng), lambda c: (c, 0))
    grow = pl.BlockSpec((ng, lc), lambda c: (0, c))
    return _pcall_ride(
        body, ride, name="mlstm_fwd", grid=(nc,),
        in_specs=[pl.BlockSpec((3, lc, d), lambda c: (0, c, 0)), gcol, grow, gcol, grow,
                  pl.BlockSpec((lc, d), lambda c: (c, 3)), pl.BlockSpec((lc, d), lambda c: (c, 4)),
                  pl.BlockSpec((1, d), lambda c: (0, 0)), pl.BlockSpec(memory_space=pl.ANY)],
        out_specs=[row, pl.BlockSpec((1, lc, d), lambda c: (1, c, 0)),
                   pl.BlockSpec((1, heads, hd, hd), lambda c: (c, 0, 0, 0)),
                   pl.BlockSpec((1, heads, 1, hd), lambda c: (c, 0, 0, 0)),
                   pl.BlockSpec((1, heads, 1, 128), lambda c: (c, 0, 0, 0))],
        out_shape=[jax.ShapeDtypeStruct((s_len, d), F32), jax.ShapeDtypeStruct(ycat.shape, BF16),
                   jax.ShapeDtypeStruct((nc, heads, hd, hd), BF16),
                   jax.ShapeDtypeStruct((nc, heads, 1, hd), F32),
                   jax.ShapeDtypeStruct((nc, heads, 1, 128), F32)],
        scratch_shapes=[pltpu.VMEM((heads, hd, hd), F32), pltpu.VMEM((heads, 1, hd), F32),
                        pltpu.VMEM((heads, 1, 128), F32)],
        input_output_aliases={8: 1},
        compiler_params=_seq(),
        args=(qkv, *gates, u, u, ml_g, ycat))


def _out_proj(ycat, w_out_b, x, gate):
    s_len, d = x.shape
    tm = _tile(s_len, 1024)

    def body(a_ref, w_ref, x_ref, g_ref, y_ref, xn_ref):
        y = _dot(a_ref[0], w_ref[0:d, :]) + _dot(a_ref[1], w_ref[d:2 * d, :])
        y_ref[...] = y
        xn_ref[...] = x_ref[...] + g_ref[...] * y

    row = pl.BlockSpec((tm, d), lambda i: (i, 0))
    return _pcall(
        body, name="out_proj", grid=(s_len // tm,),
        in_specs=[pl.BlockSpec((2, tm, d), lambda i: (0, i, 0)), pl.BlockSpec((2 * d, d), lambda i: (0, 0)), row,
                  pl.BlockSpec((1, d), lambda i: (0, 0))],
        out_specs=[row, row],
        out_shape=[jax.ShapeDtypeStruct((s_len, d), F32)] * 2,
        compiler_params=_seq(),
    )(ycat, w_out_b, x, gate)


def _final_loss(x, g, target):
    s_len, d = x.shape
    tm = _tile(s_len, 256)

    def body(x_ref, g_ref, t_ref, dx_ref, dg_ref, loss_ref):
        @pl.when(pl.program_id(0) == 0)
        def _():
            dg_ref[...] = jnp.zeros_like(dg_ref)
            loss_ref[...] = jnp.zeros_like(loss_ref)

        xv = x_ref[...]
        r = lax.rsqrt(jnp.mean(xv * xv, axis=-1, keepdims=True) + EPS)
        xn = xv * r
        err = xn * g_ref[...] - t_ref[...]
        loss_ref[...] += 0.5 * jnp.sum(jnp.mean(err * err, axis=-1, keepdims=True))
        dout = err * (1.0 / d)
        dg_ref[...] += _colsum(dout * xn)
        dxn = dout * g_ref[...]
        dx_ref[...] = r * (dxn - xn * jnp.mean(dxn * xn, axis=-1, keepdims=True))

    row = pl.BlockSpec((tm, d), lambda i: (i, 0))
    vec = pl.BlockSpec((1, d), lambda i: (0, 0))
    return _pcall(
        body, name="final_loss", grid=(s_len // tm,),
        in_specs=[row, vec, row],
        out_specs=[row, vec, pl.BlockSpec((1, 128), lambda i: (0, 0))],
        out_shape=[jax.ShapeDtypeStruct((s_len, d), F32), jax.ShapeDtypeStruct((1, d), F32),
                   jax.ShapeDtypeStruct((1, 128), F32)],
        compiler_params=_seq(),
    )(x, g, target)


def _out_bwd(dxn, y, gate, w_out_b):
    s_len, d = dxn.shape
    tm = _tile(s_len, 1024)

    def body(dx_ref, y_ref, g_ref, w_ref, dg_ref, dy_ref, dc_ref):
        @pl.when(pl.program_id(0) == 0)
        def _():
            dg_ref[...] = jnp.zeros_like(dg_ref)

        dx = dx_ref[...]
        dg_ref[...] += _colsum(dx * y_ref[...])
        dy = _bf(g_ref[...] * dx)
        dy_ref[...] = dy
        dc_ref[0] = _dot_nt(dy, w_ref[0:d, :])
        dc_ref[1] = _dot_nt(dy, w_ref[d:2 * d, :])

    row = pl.BlockSpec((tm, d), lambda i: (i, 0))
    vec = pl.BlockSpec((1, d), lambda i: (0, 0))
    return _pcall(
        body, name="out_bwd", grid=(s_len // tm,),
        in_specs=[row, row, vec, pl.BlockSpec((2 * d, d), lambda i: (0, 0))],
        out_specs=[vec, row, pl.BlockSpec((2, tm, d), lambda i: (0, i, 0))],
        out_shape=[jax.ShapeDtypeStruct((1, d), F32), jax.ShapeDtypeStruct((s_len, d), BF16),
                   jax.ShapeDtypeStruct((2, s_len, d), F32)],
        compiler_params=_seq(),
    )(dxn, y, gate, w_out_b)


def _grad_matmul(a3, b3, nblk, a_idx, b_idx, out_shape, out_block, out_idx, ride=None):
    _, s_len, m = a3.shape
    n = b3.shape[2]
    tk = _tile(s_len, 2048)

    def body(a_ref, b_ref, o_ref):
        @pl.when(pl.program_id(1) == 0)
        def _():
            o_ref[...] = jnp.zeros_like(o_ref)

        o_ref[...] += _dot_tn(a_ref[0], b_ref[0])

    (out,), got = _pcall_ride(
        body, ride, name="grad_matmul", grid=(nblk, s_len // tk),
        in_specs=[pl.BlockSpec((1, tk, m), lambda p, t: (a_idx(p), t, 0)),
                  pl.BlockSpec((1, tk, n), lambda p, t: (b_idx(p), t, 0))],
        out_specs=[pl.BlockSpec((None,) + out_block, lambda p, t: (0,) + out_idx(p))],
        out_shape=[jax.ShapeDtypeStruct((1,) + out_shape, F32)],
        compiler_params=_seq(2), args=(a3, b3))
    return out, got


DU_PLANE = (2, 3, 4, 0, 1)


def _mlstm_bwd(qkv, gates, cst, nst, mst, cell, u, ml_g, d_ycat, wif_b, ride=None):
    _, s_len, d = qkv.shape
    ng = gates[0].shape[1]
    heads = ng // 2
    hd = d // heads
    lc = ML_CHUNK
    nc = s_len // lc
    kscale = hd ** -0.5

    def body(qkv_ref, gt_ref, gtt_ref, bc_ref, bct_ref, cst_ref, nst_ref, mst_ref, cell_ref, o_ref, z_ref, g_ref, dy_ref,
             wif_ref, dqkv_ref, dgt_ref, dbif_ref, du_ref, dg_ref, dcs, dns):
        @pl.when(pl.program_id(0) == 0)
        def _():
            dbif_ref[...] = jnp.zeros_like(dbif_ref)
            dcs[...] = jnp.zeros_like(dcs)
            dns[...] = jnp.zeros_like(dns)
            dg_ref[...] = jnp.zeros_like(dg_ref)

        causal, tril, triu = _tri_masks(lc)
        tril_strict = (tril.astype(F32) - (tril * triu).astype(F32)).astype(BF16)
        gtv, gttv, bcv, bctv = gt_ref[...], gtt_ref[...], bc_ref[...], bct_ref[...]
        lane = lax.broadcasted_iota(jnp.int32, (lc, ng), 1)
        dli_all = jnp.zeros((lc, ng), F32)
        from_later = jnp.zeros((lc, ng), F32)
        from_earlier = jnp.zeros((lc, ng), F32)
        across_all = jnp.zeros((1, ng), F32)
        old = [(dcs[h], dns[h]) for h in range(heads)]
        new, d_o, d_z, d_g, dqs, dks, dvs = [], [], [], [], [], [], []
        for h in range(heads):
            hs = slice(h * hd, (h + 1) * hd)
            li_c, li_r, gf_c, b_c, b_r = _chunk_gates(gtv, gttv, bcv, bctv, h, heads)
            m_prev = mst_ref[0, h][:, 0:1]
            m_t, w_intra, w_inter, _, w_state, decay = _chunk_weights(li_c, li_r, b_c, b_r, m_prev, causal)
            qb = qkv_ref[0, :, hs]
            qf = qb.astype(F32)
            ks = qkv_ref[1, :, hs].astype(F32) * kscale
            kb = _bf(ks)
            vb = qkv_ref[2, :, hs]
            c_b = cst_ref[0, h]
            n_old = nst_ref[0, h]
            s = _dot_nt(qb, kb) * w_intra
            den = _rowsum(s) + w_inter * _rowsum(qf * n_old)
            floor = jnp.exp(-m_t)
            dstab = jnp.maximum(jnp.abs(den), floor)
            cell = cell_ref[:, hs]
            o = o_ref[:, hs]
            so = _sigmoid(o)
            hm = so * cell
            rinv = lax.rsqrt(jnp.mean(hm * hm, axis=-1, keepdims=True) + EPS)
            hn = hm * rinv
            z = z_ref[:, hs]
            sgz = _sigmoid(z)
            sz = z * sgz
            gh = g_ref[:, hs]
            dy = dy_ref[0, :, hs]
            d_z.append(_bf(dy * (hn * gh) * _dsilu(z, sgz)))
            d_g.append(_colsum(dy * hn * sz))
            dhn = dy * gh * sz
            dhm = rinv * (dhn - hn * jnp.mean(dhn * hn, axis=-1, keepdims=True))
            d_o.append(_bf(dhm * cell * so * (1.0 - so)))
            dcell = dhm * so
            dnum = dcell / dstab
            dnb = _bf(dnum)
            dden = -_rowsum(dcell * cell) / dstab * jnp.where(jnp.abs(den) > floor, jnp.where(den > 0.0, 1.0, -1.0), 0.0)
            dst = _dot_nt(dnb, vb) + dden
            dsdb = _bf(dst * w_intra)
            dc_out, dn_out = old[h]
            dcb = _bf(dc_out)
            dq_inter = w_inter * (_dot_nt(dnb, c_b) + dden * n_old)
            dk_inter = w_state * (_dot_nt(vb, dcb) + dn_out)
            dq = _dot(dsdb, kb) + dq_inter
            dk = _dot_tn(dsdb, qb) + dk_inter
            dv = _dot_tn(_bf(s), dnb) + _dot(_bf(ks * w_state), dcb)
            wq = w_inter * qf
            new.append((decay * dc_out + _dot_tn(_bf(wq), dnb), decay * dn_out + _colsum(wq * dden)))
            pmat = dst * s
            p_rows = _rowsum(pmat)
            p_cols = _rowsum(pmat.T)
            q_in = _rowsum(qf * dq_inter)
            k_in = _rowsum(ks * dk_inter)
            across = decay * (jnp.sum(dc_out * c_b.astype(F32), keepdims=True) + jnp.sum(dn_out * n_old, keepdims=True))
            dli_all = dli_all + jnp.where(lane == h, p_cols + k_in, 0.0)
            from_later = from_later + jnp.where(lane == heads + h, p_rows - p_cols + q_in, 0.0)
            from_earlier = from_earlier + jnp.where(lane == heads + h, k_in, 0.0)
            across_all = across_all + jnp.where(lane[0:1] == heads + h, across, 0.0)
            dqs.append(dq)
            dks.append(dk * kscale)
            dvs.append(dv)
        for h in range(heads):
            dcs[h], dns[h] = new[h]
        du_ref[0] = jnp.concatenate(d_o, axis=1)
        du_ref[1] = jnp.concatenate(d_z, axis=1)
        dg_ref[...] += jnp.concatenate(d_g, axis=1)
        dlf = _tri_dot_left(triu, from_later) + _tri_dot_left(tril_strict, from_earlier) + across_all
        dgt = dli_all + dlf * _sigmoid(-gtv)
        dgt_ref[...] = dgt
        dbif_ref[...] += _colsum(dgt)
        dgb = _bf(dgt)
        dqkv_ref[0] = _bf(jnp.concatenate(dqs, axis=1) + _dot_nt(dgb, wif_ref[0:d, :]))
        dqkv_ref[1] = _bf(jnp.concatenate(dks, axis=1) + _dot_nt(dgb, wif_ref[d:2 * d, :]))
        dqkv_ref[2] = _bf(jnp.concatenate(dvs, axis=1) + _dot_nt(dgb, wif_ref[2 * d:3 * d, :]))

    rev = lambda c: nc - 1 - c
    row = pl.BlockSpec((lc, d), lambda c: (rev(c), 0))
    gcol = pl.BlockSpec((lc, ng), lambda c: (rev(c), 0))
    grow = pl.BlockSpec((ng, lc), lambda c: (0, rev(c)))
    return _pcall_ride(
        body, ride, name="mlstm_bwd", grid=(nc,),
        in_specs=[pl.BlockSpec((3, lc, d), lambda c: (0, rev(c), 0)), gcol, grow, gcol, grow,
                  pl.BlockSpec((1, heads, hd, hd), lambda c: (rev(c), 0, 0, 0)),
                  pl.BlockSpec((1, heads, 1, hd), lambda c: (rev(c), 0, 0, 0)),
                  pl.BlockSpec((1, heads, 1, 128), lambda c: (rev(c), 0, 0, 0)),
                  row, pl.BlockSpec((lc, d), lambda c: (rev(c), 3)), pl.BlockSpec((lc, d), lambda c: (rev(c), 4)),
                  pl.BlockSpec((1, d), lambda c: (0, 0)), pl.BlockSpec((1, lc, d), lambda c: (1, rev(c), 0)),
                  pl.BlockSpec((3 * d, ng), lambda c: (0, 0))],
        out_specs=[pl.BlockSpec((3, lc, d), lambda c: (0, rev(c), 0)), pl.BlockSpec((lc, ng), lambda c: (rev(c), 0)),
                   pl.BlockSpec((1, ng), lambda c: (0, 0)), pl.BlockSpec((2, lc, d), lambda c: (0, rev(c), 0)),
                   pl.BlockSpec((1, d), lambda c: (0, 0))],
        out_shape=[jax.ShapeDtypeStruct((3, s_len, d), BF16), jax.ShapeDtypeStruct((s_len, ng), F32),
                   jax.ShapeDtypeStruct((1, ng), F32), jax.ShapeDtypeStruct((5, s_len, d), BF16),
                   jax.ShapeDtypeStruct((1, d), F32)],
        scratch_shapes=[pltpu.VMEM((heads, hd, hd), F32), pltpu.VMEM((heads, 1, hd), F32)],
        compiler_params=_seq(),
        args=(qkv, *gates, cst, nst, mst, cell, u, u, ml_g, d_ycat, wif_b))


def _conv_bwd_tile(dp, later, taps, cw_ref, gw_ref, gb_ref):
    tm = dp.shape[0]
    dwin = jnp.concatenate([dp, later[...]], axis=0)
    later[...] = dp[0:HALO]
    acc = cw_ref[CONV_WIDTH - 1:CONV_WIDTH, :] * dp
    for k in range(CONV_WIDTH):
        if k < CONV_WIDTH - 1:
            acc = acc + cw_ref[k:k + 1, :] * _shift_up(dwin, CONV_WIDTH - 1 - k)[0:tm]
        gw_ref[k:k + 1, :] += _colsum(dp * taps[k])
    gb_ref[...] += _colsum(dp)
    return acc


def _ml_pre_bwd(dqkv, u, conv_w, conv_b, wqkv_b, du):
    s_len = u.shape[0]
    d = conv_w.shape[1]
    _, heads, hd, _ = wqkv_b.shape
    tm = _tile(s_len, 256)
    per = tm // HALO
    nt = s_len // tm

    def body(dqkv_ref, x_ref, xp_ref, cw_ref, cb_ref, w_ref, _, dx_ref, gw_ref, gcw_ref, gcb_ref, later, dps, dxs):
        i = pl.program_id(0)

        @pl.when(i == 0)
        def _():
            gw_ref[...] = jnp.zeros_like(gw_ref)
            gcw_ref[...] = jnp.zeros_like(gcw_ref)
            gcb_ref[...] = jnp.zeros_like(gcb_ref)
            later[...] = jnp.zeros_like(later)

        prev = jnp.where(i == nt - 1, 0.0, xp_ref[...])
        xm = x_ref[...]
        taps = _conv_taps(jnp.concatenate([prev, xm], axis=0))
        pre = _conv_fwd(taps, cw_ref, cb_ref)
        sg = _sigmoid(pre)
        xcb = _bf(pre * sg)
        xmb = _bf(xm)
        for h in range(heads):
            hs = slice(h * hd, (h + 1) * hd)
            dqh, dkh, dvh = dqkv_ref[0, :, hs], dqkv_ref[1, :, hs], dqkv_ref[2, :, hs]
            dxc = _dot_nt(dqh, w_ref[0, h]) + _dot_nt(dkh, w_ref[1, h])
            dps[:, hs] = dxc * _dsilu(pre[:, hs], sg[:, hs])
            dxs[:, hs] = _dot_nt(dvh, w_ref[2, h])
            gw_ref[0, h] += _dot_tn(xcb[:, hs], dqh)
            gw_ref[1, h] += _dot_tn(xcb[:, hs], dkh)
            gw_ref[2, h] += _dot_tn(xmb[:, hs], dvh)
        dx_ref[0] = _bf(_conv_bwd_tile(dps[...], later, taps, cw_ref, gcw_ref, gcb_ref) + dxs[...])

    rev = lambda i: nt - 1 - i
    vec = pl.BlockSpec((1, d), lambda i: (0, 0))
    cwb = pl.BlockSpec((CONV_WIDTH, d), lambda i: (0, 0))
    whole4 = pl.BlockSpec(wqkv_b.shape, lambda i: (0, 0, 0, 0))
    return _pcall(
        body, name="ml_pre_bwd", grid=(nt,),
        in_specs=[pl.BlockSpec((3, tm, d), lambda i: (0, rev(i), 0)), pl.BlockSpec((tm, d), lambda i: (rev(i), 2)),
                  pl.BlockSpec((HALO, d), lambda i: (jnp.maximum(rev(i) * per - 1, 0), 2)),
                  cwb, vec, whole4, pl.BlockSpec(memory_space=pl.ANY)],
        out_specs=[pl.BlockSpec((1, tm, d), lambda i: (DU_PLANE[2], rev(i), 0)), whole4, cwb, vec],
        out_shape=[jax.ShapeDtypeStruct(du.shape, BF16), jax.ShapeDtypeStruct(wqkv_b.shape, F32),
                   jax.ShapeDtypeStruct((CONV_WIDTH, d), F32), jax.ShapeDtypeStruct((1, d), F32)],
        scratch_shapes=[pltpu.VMEM((HALO, d), F32), pltpu.VMEM((tm, d), F32), pltpu.VMEM((tm, d), F32)],
        input_output_aliases={6: 0},
        compiler_params=_seq(),
    )(dqkv, u, u, conv_w, conv_b, wqkv_b, du)


def _rg_bwd(d_ycat, u, hh, conv_w, conv_b, wa_b, ba, wx_b, bx, lam, du):
    s_len = u.shape[0]
    d = conv_w.shape[1]
    heads, hd, _ = wa_b.shape
    tm = _tile(s_len, 256)
    per = tm // HALO
    nt = s_len // tm

    def body(dy_ref, x_ref, xp_ref, z_ref, hh_ref, hp_ref, cw_ref, cb_ref, wa_ref, ba_ref, wx_ref, bx_ref, lam_ref, _,
             du_ref, gwa_ref, gwx_ref, gba_ref, gbx_ref, glam_ref, gcw_ref, gcb_ref, carry, gbuf, later, dxcs):
        i = pl.program_id(0)
        first = i == nt - 1

        @pl.when(i == 0)
        def _():
            carry[...] = jnp.zeros_like(carry)
            later[...] = jnp.zeros_like(later)
            gwa_ref[...] = jnp.zeros_like(gwa_ref)
            gwx_ref[...] = jnp.zeros_like(gwx_ref)
            gba_ref[...] = jnp.zeros_like(gba_ref)
            gbx_ref[...] = jnp.zeros_like(gbx_ref)
            glam_ref[...] = jnp.zeros_like(glam_ref)
            gcw_ref[...] = jnp.zeros_like(gcw_ref)
            gcb_ref[...] = jnp.zeros_like(gcb_ref)

        prev = jnp.where(first, 0.0, xp_ref[...])
        taps = _conv_taps(jnp.concatenate([prev, x_ref[...]], axis=0))
        xc = _conv_fwd(taps, cw_ref, cb_ref)
        r, ig, sp, log_a, a, mult = _rg_gates(xc, wa_ref, ba_ref, wx_ref, bx_ref, lam_ref)
        z = z_ref[...]
        sgz = _sigmoid(z)
        dy = dy_ref[0]
        hh_v = hh_ref[...]
        du_ref[1] = _bf(dy * hh_v * _dsilu(z, sgz))
        dhh = dy * (z * sgz)
        rows = lax.broadcasted_iota(jnp.int32, a.shape, 0)
        coef = jnp.where(rows == tm - 1, carry[1:2, :], _shift_up(a, 1))
        ca, cu = _scan_groups(coef, dhh, reverse=True)
        c = carry[0:1, :]
        for j in range(per - 1, -1, -1):
            blk = ca[j * 8:(j + 1) * 8] * c + cu[j * 8:(j + 1) * 8]
            gbuf[j * 8:(j + 1) * 8, :] = blk
            c = blk[0:1]
        carry[0:1, :] = c
        carry[1:2, :] = a[0:1]
        g = gbuf[...]
        hprev_tile = jnp.where(first, 0.0, hp_ref[...])
        hprev = _shift_down(jnp.concatenate([hprev_tile, hh_v], axis=0), 1)[HALO:]
        da = g * hprev
        gx_ = g * xc
        d_mult = gx_ * ig
        d_ig = gx_ * mult
        dxc = g * mult * ig
        dlog_a = da * a - d_mult * (a * a / mult)
        d_r = dlog_a * ((-RG_C) * sp)
        glam_ref[...] += _colsum(dlog_a * ((-RG_C) * r)) * (-_sigmoid(-lam_ref[...]))
        d_ga = d_r * r * (1.0 - r)
        d_gx = d_ig * ig * (1.0 - ig)
        gba_ref[...] += _colsum(d_ga)
        gbx_ref[...] += _colsum(d_gx)
        xb = _bf(xc)
        dgab = _bf(d_ga)
        dgxb = _bf(d_gx)
        for h in range(heads):
            hs = slice(h * hd, (h + 1) * hd)
            dxcs[:, hs] = dxc[:, hs] + _dot_nt(dgab[:, hs], wa_ref[h]) + _dot_nt(dgxb[:, hs], wx_ref[h])
            gwa_ref[h] += _dot_tn(xb[:, hs], dgab[:, hs])
            gwx_ref[h] += _dot_tn(xb[:, hs], dgxb[:, hs])
        du_ref[0] = _bf(_conv_bwd_tile(dxcs[...], later, taps, cw_ref, gcw_ref, gcb_ref))

    assert DU_PLANE[0] % 2 == 0 and DU_PLANE[1] == DU_PLANE[0] + 1
    rev = lambda i: nt - 1 - i
    row = pl.BlockSpec((tm, d), lambda i: (rev(i), 0))
    halo_prev = lambda col: pl.BlockSpec((HALO, d), lambda i: (jnp.maximum(rev(i) * per - 1, 0), col))
    vec = pl.BlockSpec((1, d), lambda i: (0, 0))
    cwb = pl.BlockSpec((CONV_WIDTH, d), lambda i: (0, 0))
    whole3 = lambda a: pl.BlockSpec(a.shape, lambda i: (0, 0, 0))
    return _pcall(
        body, name="rg_bwd", grid=(nt,),
        in_specs=[pl.BlockSpec((1, tm, d), lambda i: (0, rev(i), 0)), row, halo_prev(0),
                  pl.BlockSpec((tm, d), lambda i: (rev(i), 1)), row, halo_prev(0),
                  cwb, vec, whole3(wa_b), vec, whole3(wx_b), vec, vec, pl.BlockSpec(memory_space=pl.ANY)],
        out_specs=[pl.BlockSpec((2, tm, d), lambda i: (DU_PLANE[0] // 2, rev(i), 0)), whole3(wa_b), whole3(wa_b),
                   vec, vec, vec, cwb, vec],
        out_shape=[jax.ShapeDtypeStruct(du.shape, BF16), jax.ShapeDtypeStruct(wa_b.shape, F32),
                   jax.ShapeDtypeStruct(wa_b.shape, F32)] + [jax.ShapeDtypeStruct((1, d), F32)] * 3
        + [jax.ShapeDtypeStruct((CONV_WIDTH, d), F32), jax.ShapeDtypeStruct((1, d), F32)],
        scratch_shapes=[pltpu.VMEM((8, d), F32), pltpu.VMEM((tm, d), F32), pltpu.VMEM((HALO, d), F32),
                        pltpu.VMEM((tm, d), F32)],
        input_output_aliases={13: 0},
        compiler_params=_seq(),
    )(d_ycat, u, u, u, hh, hh, conv_w, conv_b, wa_b, ba, wx_b, bx, lam, du)


def _in_bwd(du, w4, x, dxn, g, scale, ride=None):
    s_len, d = x.shape
    tm = _tile(s_len, 512)
    nsh_chips, _, nsh = w4.shape
    npc = du.shape[0]
    ck = d // 4
    assert nsh % ck == 0 and npc * d == nsh_chips * nsh

    def body(du_ref, w_ref, x_ref, dxn_ref, g_ref, sc_ref, dx_ref, dsh_ref, dsc_ref, dg_ref):
        @pl.when(pl.program_id(0) == 0)
        def _():
            dsh_ref[...] = jnp.zeros_like(dsh_ref)
            dsc_ref[...] = jnp.zeros_like(dsc_ref)
            dg_ref[...] = jnp.zeros_like(dg_ref)

        dh = None
        for q in range(npc * d // ck):
            col = q * ck
            p, pc = col // d, col % d
            s, sc = col // nsh, col % nsh
            t = _dot_nt(du_ref[DU_PLANE[p], :, pc:pc + ck], w_ref[s, :, sc:sc + ck])
            dh = t if dh is None else dh + t
        xv = x_ref[...]
        r = lax.rsqrt(jnp.mean(xv * xv, axis=-1, keepdims=True) + EPS)
        xn = xv * r
        gv = g_ref[...]
        onesc = 1.0 + sc_ref[...]
        dsh_ref[...] += _colsum(dh)
        dsc_ref[...] += _colsum(dh * (xn * gv))
        dg_ref[...] += _colsum(dh * xn * onesc)
        dxh = dh * (gv * onesc)
        dx_ref[...] = dxn_ref[...] + r * (dxh - xn * jnp.mean(dxh * xn, axis=-1, keepdims=True))

    row = pl.BlockSpec((tm, d), lambda i: (i, 0))
    vec = pl.BlockSpec((1, d), lambda i: (0, 0))
    return _pcall_ride(
        body, ride, name="in_bwd", grid=(s_len // tm,),
        in_specs=[pl.BlockSpec((npc, tm, d), lambda i: (0, i, 0)), pl.BlockSpec(w4.shape, lambda i: (0, 0, 0)), row, row,
                  vec, vec],
        out_specs=[row, vec, vec, vec],
        out_shape=[jax.ShapeDtypeStruct((s_len, d), F32)] + [jax.ShapeDtypeStruct((1, d), F32)] * 3,
        compiler_params=_seq(),
        args=(du, w4, x, dxn, g, scale))


def _layer_fwd(x, p, rides=None, late=None):
    rides = rides or {}
    (h_b, u), got = _ln_inproj(x, p["norm_g"], p["scale"], p["shift"], p["w4"], rides.get("ln_inproj"))
    if late is not None:
        p = {**p, **late(got)}
    (hh, ycat), got_a = _rg_fwd(u, p["rg_conv_w"], p["rg_conv_b"], p["rg_wa_b"], p["rg_ba"], p["rg_wx_b"], p["rg_bx"],
                                p["rg_lam"], rides.get("rg_fwd"))
    qkv, *gates = _ml_pre(u, p["ml_conv_w"], p["ml_conv_b"], p["wqkv_b"], p["wif_b"], p["wift_b"], p["b_if"],
                          p["b_ift"])
    (cell, ycat, cst, nst, mst), got_b = _mlstm_fwd(qkv, gates, u, p["ml_g"], ycat, rides.get("mlstm_fwd"))
    y, x_new = _out_proj(ycat, p["w_out_b"], x, p["gate"])
    saved = dict(x=x, h_b=h_b, u=u, hh=hh, qkv=qkv, gates=gates, cell=cell, ycat=ycat, cst=cst, nst=nst, mst=mst, y=y)
    return x_new, saved, p, dict(rg_fwd=got_a, mlstm_fwd=got_b)


def _layer_bwd(dxn, p, s, rides=None):
    rides = rides or {}
    u = s["u"]
    d = dxn.shape[1]
    d_gate, dy_b, d_ycat = _out_bwd(dxn, s["y"], p["gate"], p["w_out_b"])
    g_w_out = _grad_matmul(s["ycat"], dy_b[None], 2, lambda b: b, lambda b: 0, (2 * d, d), (d, d), lambda b: (b, 0))[0]
    (dqkv, dgt, g_b_if, du, g_ml_g), got = _mlstm_bwd(s["qkv"], s["gates"], s["cst"], s["nst"], s["mst"], s["cell"], u,
                                                      p["ml_g"], d_ycat, p["wif_b"], rides.get("mlstm_bwd"))
    ng = dgt.shape[1]
    g_w_if = _grad_matmul(s["qkv"], _bf(dgt)[None], 3, lambda b: b, lambda b: 0, (3 * d, ng), (d, ng),
                          lambda b: (b, 0))[0][0]
    du, g_wqkv, g_ml_cw, g_ml_cb = _ml_pre_bwd(dqkv, u, p["ml_conv_w"], p["ml_conv_b"], p["wqkv_b"], du)
    du, g_wa, g_wx, g_ba, g_bx, g_lam, g_rg_cw, g_rg_cb = _rg_bwd(d_ycat, u, s["hh"], p["rg_conv_w"], p["rg_conv_b"],
                                                                  p["rg_wa_b"], p["rg_ba"], p["rg_wx_b"], p["rg_bx"],
                                                                  p["rg_lam"], du)
    grads = dict(rg_conv_w=g_rg_cw, rg_conv_b=g_rg_cb, rg_w_a=g_wa, rg_b_a=g_ba, rg_w_x=g_wx, rg_b_x=g_bx,
                 rg_lambda=g_lam, ml_conv_w=g_ml_cw, ml_conv_b=g_ml_cb, ml_w_qkv=g_wqkv, ml_w_if=g_w_if, ml_b_if=g_b_if,
                 ml_norm_g=g_ml_g, w_out=g_w_out)
    npc = du.shape[0]
    gm_ride = rides["grad_w_in"](grads) if "grad_w_in" in rides else None
    grads["w_in"], got_gm = _grad_matmul(s["h_b"][None], du, npc, lambda b: 0, lambda b: (b + DU_PLANE[0]) % npc,
                                         (d, npc * d), (d, d), lambda b: (0, b), gm_ride)
    in_ride = rides["in_bwd"](grads["w_in"]) if "in_bwd" in rides else None
    (dx, d_shift, d_scale, grads["norm_g"]), got_in = _in_bwd(du, p["w4"], s["x"], dxn, p["norm_g"], p["scale"], in_ride)
    return (dx, grads, jnp.concatenate([d_shift, d_scale, d_gate], axis=1),
            dict(mlstm_bwd=got, grad_w_in=got_gm, in_bwd=got_in))


def _trunk_fwd_bwd(x, target, final_g, layers):
    saved = []
    for p in layers:
        x, s, _, _ = _layer_fwd(x, p)
        saved.append(s)
    dx, g_final, loss = _final_loss(x, final_g, target)
    grads, dmods = [], []
    for layer in reversed(range(len(layers))):
        dx, g, dm, _ = _layer_bwd(dx, layers[layer], saved[layer])
        grads.append(g)
        dmods.append(dm)
    return loss, dx, g_final, grads[::-1], dmods[::-1]


def _me():
    return lax.axis_index("x"), lax.axis_index("y"), lax.axis_index("c")


def _remote(src, dst, send_sem, recv_sem, to):
    return pltpu.make_async_remote_copy(src_ref=src, dst_ref=dst, send_sem=send_sem, recv_sem=recv_sem,
                                        device_id=to, device_id_type=MESH)


def _all_gather8(blocks, space):
    n = len(blocks)

    def body(*refs):
        x_refs, out_refs = refs[:n], refs[n:2 * n]
        send_sems, recv_sems, local_sems = refs[2 * n:]
        x, y, c = _me()
        me, sibling = (x, y, c), (x, y, 1 - c)
        chips = [(1 - x, y), (x, 1 - y), (1 - x, 1 - y)]

        def rows(i, px, py, pc):
            m_per = blocks[i].shape[0]
            return out_refs[i].at[pl.ds((4 * px + 2 * py + pc) * m_per, m_per), :]

        def copy(i, k, blk, to, src=None):
            return _remote(rows(i, *blk) if src is None else src, rows(i, *blk), send_sems.at[7 * i + k],
                           recv_sems.at[7 * i + k], to)

        mine = [pltpu.make_async_copy(x_refs[i], rows(i, *me), local_sems.at[i]) for i in range(n)]
        first = []
        for i in range(n):
            first.append(copy(i, 0, me, sibling, src=x_refs[i]))
            first += [copy(i, 1 + j, me, (*chip, c), src=x_refs[i]) for j, chip in enumerate(chips)]
        for cp in mine + first:
            cp.start()
        passed = []
        for j, chip in enumerate(chips):
            for i in range(n):
                copy(i, 1 + j, (*chip, c), me).wait_recv()
                passed.append(copy(i, 4 + j, (*chip, c), sibling))
                passed[-1].start()
        for i in range(n):
            copy(i, 0, sibling, me).wait_recv()
            for j, chip in enumerate(chips):
                copy(i, 4 + j, (*chip, 1 - c), me).wait_recv()
        for cp in first + passed:
            cp.wait_send()
        for cp in mine:
            cp.wait()

    spec = pl.BlockSpec(memory_space=space)
    return _pcall(
        body, name="all_gather8",
        out_shape=[jax.ShapeDtypeStruct((8 * b.shape[0], b.shape[1]), b.dtype) for b in blocks],
        in_specs=[spec] * n, out_specs=[spec] * n,
        scratch_shapes=[pltpu.SemaphoreType.DMA((7 * n,)), pltpu.SemaphoreType.DMA((7 * n,)),
                        pltpu.SemaphoreType.DMA((n,))],
    )(*blocks)


def _sib_halves(arrs, kinds):
    na = len(arrs)
    ncopies = sum(4 if k == "w_in" else 1 for k in kinds)

    def out_shape(a, kind):
        if kind == "w_in":
            return (a.shape[0], 4, a.shape[1] // 2, a.shape[2] // 4)
        return a.shape[:2] + a.shape[3:] if kind == "w_out" else a.shape[1:]

    def body(*refs):
        src, dst = refs[:na], refs[na:2 * na]
        send_sems, recv_sems = refs[2 * na:]
        x, y, c = _me()
        o = 1 - c
        pairs = []
        for i, kind in enumerate(kinds):
            depth = arrs[i].shape[0]
            if kind == "w_in":
                half, n = arrs[i].shape[1] // 2, arrs[i].shape[2] // 4
                pairs += [(src[i].at[pl.ds(0, depth), pl.ds(o * half, half), pl.ds(s * n, n)],
                           dst[i].at[pl.ds(0, depth), s]) for s in range(4)]
            elif kind == "w_out":
                pairs.append((src[i].at[pl.ds(0, depth), pl.ds(0, 4), o], dst[i]))
            else:
                pairs.append((src[i].at[o], dst[i]))
        copies = [_remote(s_, d_, send_sems.at[k], recv_sems.at[k], (x, y, o)) for k, (s_, d_) in enumerate(pairs)]
        for cp in copies:
            cp.start()
        for cp in copies:
            cp.wait_recv()
        for cp in copies:
            cp.wait_send()

    hbm = pl.BlockSpec(memory_space=pltpu.HBM)
    return _pcall(
        body, name="sib_halves",
        out_shape=[jax.ShapeDtypeStruct(out_shape(a, k), a.dtype) for a, k in zip(arrs, kinds)],
        in_specs=[hbm] * na, out_specs=[hbm] * na,
        scratch_shapes=[pltpu.SemaphoreType.DMA((ncopies,)), pltpu.SemaphoreType.DMA((ncopies,))],
    )(*arrs)


def _sib_fill(boths):
    n = len(boths)

    def body(*refs):
        dst = refs[n:2 * n]
        send_sems, recv_sems = refs[2 * n:]
        x, y, c = _me()
        view = lambda i: dst[i].at[pl.ds(0, boths[i].shape[0]), c]
        copies = [_remote(view(i), view(i), send_sems.at[i], recv_sems.at[i], (x, y, 1 - c)) for i in range(n)]
        for cp in copies:
            cp.start()
        for cp in copies:
            cp.wait_recv()
        for cp in copies:
            cp.wait_send()

    hbm = pl.BlockSpec(memory_space=pltpu.HBM)
    return _pcall(
        body, name="sib_fill",
        out_shape=[jax.ShapeDtypeStruct(b.shape, b.dtype) for b in boths],
        in_specs=[hbm] * n, out_specs=[hbm] * n, input_output_aliases={i: i for i in range(n)},
        scratch_shapes=[pltpu.SemaphoreType.DMA((n,)), pltpu.SemaphoreType.DMA((n,))],
    )(*boths)


def _chip_exchange(arrs):
    n = len(arrs)

    def body(*refs):
        src, dst = refs[:n], refs[n:2 * n]
        send_sems, recv_sems = refs[2 * n:]
        x, y, c = _me()
        me_s = 2 * x + y
        chips = [(1 - x, y), (x, 1 - y), (1 - x, 1 - y)]
        copies = [_remote(src[i].at[2 * px + py], dst[i].at[me_s], send_sems.at[3 * i + k], recv_sems.at[3 * i + k],
                          (px, py, c))
                  for i in range(n) for k, (px, py) in enumerate(chips)]
        for cp in copies:
            cp.start()
        for cp in copies:
            cp.wait_recv()
        for cp in copies:
            cp.wait_send()

    hbm = pl.BlockSpec(memory_space=pltpu.HBM)
    return _pcall(
        body, name="chip_exchange",
        out_shape=[jax.ShapeDtypeStruct(a.shape, a.dtype) for a in arrs],
        in_specs=[hbm] * n, out_specs=[hbm] * n,
        scratch_shapes=[pltpu.SemaphoreType.DMA((3 * n,)), pltpu.SemaphoreType.DMA((3 * n,))],
    )(*arrs)


def _row_tile(rows, cap=4096, mult=16):
    best = None
    for t in range(mult, min(rows, cap) + 1, mult):
        if rows % t == 0:
            best = t
    return rows if best is None else best


def _pair_sum(half, own, own_spec, got, got_spec, out_shape, out_spec, grid):
    def body(_, a_ref, b_ref, o_ref):
        o_ref[...] = (a_ref[...] + b_ref[...].astype(F32)).astype(o_ref.dtype)

    return _pcall(
        body, name="pair_sum",
        grid_spec=pltpu.PrefetchScalarGridSpec(num_scalar_prefetch=1, grid=grid, in_specs=[own_spec, got_spec],
                                               out_specs=out_spec),
        out_shape=out_shape, compiler_params=_seq(len(grid)))(half, own, got)


def _chip_sum(ids, part, met, fill, layer=0, stack=1):
    _, _, rows, n = part.shape
    tr = _row_tile(rows, cap=max(16, (1 << 18) // n))
    first = isinstance(stack, int)

    def body(_, own_ref, a_ref, b_ref, c_ref, *rest):
        acc = own_ref[...].astype(F32) + a_ref[...].astype(F32)
        acc = acc + b_ref[...].astype(F32)
        rest[-1][...] = acc + c_ref[...].astype(F32)

    blk = (None, None, tr, n)
    other = lambda k: pl.BlockSpec(blk, lambda j, ids: ((ids[0] + k) % 4, 0, j, 0))
    in_specs = [pl.BlockSpec(blk, lambda j, ids: (ids[0], 0, j, 0)), other(1), other(2), other(3)]
    return _pcall(
        body, name="chip_sum",
        grid_spec=pltpu.PrefetchScalarGridSpec(
            num_scalar_prefetch=1, grid=(rows // tr,),
            in_specs=in_specs if first else in_specs + [pl.BlockSpec(memory_space=pl.ANY)],
            out_specs=pl.BlockSpec(blk, lambda j, ids: (layer, ids[1] if fill else 0, j, 0))),
        out_shape=jax.ShapeDtypeStruct(((stack,) if first else stack.shape[:1]) + (2 if fill else 1, rows, n), F32),
        input_output_aliases={} if first else {5: 0},
        compiler_params=_seq())(*((ids, part, met, met, met) if first else (ids, part, met, met, met, stack)))


def _ada_mod(c_all, w_ada, b_ada_cols):
    depth, d, n = w_ada.shape
    nb = c_all.shape[0]

    def body(c_ref, w_ref, b_ref, o_ref):
        cv = c_ref[...]
        ca = _bf(cv * _sigmoid(cv))
        o_ref[0] = _dot(ca, _bf(w_ref[0])) + b_ref[0]

    return _pcall(body, name="ada_mod", grid=(depth,),
                  in_specs=[pl.BlockSpec((nb, d), lambda l: (0, 0)), pl.BlockSpec((1, d, n), lambda l: (l, 0, 0)),
                            pl.BlockSpec((1, 1, n), lambda l: (l, 0, 0))],
                  out_specs=pl.BlockSpec((1, nb, n), lambda l: (l, 0, 0)),
                  out_shape=jax.ShapeDtypeStruct((depth, nb, n), F32), compiler_params=_seq())(c_all, w_ada, b_ada_cols)


def _ada_grad(c_all, dmod_cols, rows_all):
    nb, d = c_all.shape
    depth, _, n = dmod_cols.shape
    kinds, n_all = rows_all.shape[1], rows_all.shape[3]

    def body(c_ref, dm_ref, da_ref, gw_ref, gb_ref):
        cv = c_ref[...]
        ca = _bf(cv * _sigmoid(cv))
        gw_ref[0] = _dot_tn(ca, _bf(dm_ref[0]))
        for k in range(kinds):
            gb_ref[0, k] = _colsum(da_ref[0, k])

    return _pcall(body, name="ada_grad", grid=(depth,),
                  in_specs=[pl.BlockSpec((nb, d), lambda l: (0, 0)), pl.BlockSpec((1, nb, n), lambda l: (l, 0, 0)),
                            pl.BlockSpec((1, kinds, nb, n_all), lambda l: (l, 0, 0, 0))],
                  out_specs=[pl.BlockSpec((1, d, n), lambda l: (l, 0, 0)),
                             pl.BlockSpec((1, kinds, 1, n_all), lambda l: (l, 0, 0, 0))],
                  out_shape=[jax.ShapeDtypeStruct((depth, d, n), F32), jax.ShapeDtypeStruct((depth, kinds, 1, n_all), F32)],
                  compiler_params=_seq())(c_all, dmod_cols, rows_all)


def _adamw(items, ride=None):
    two_d = [tuple(t.reshape(w.size // w.shape[-1], w.shape[-1]) for t in (w, g, m, v)) for w, g, m, v in items]
    n = len(items)
    if n == 1:
        rows, cols = two_d[0][0].shape
        tr = _row_tile(rows, cap=max(8, (1 << 18) // cols), mult=8)
        blocks = [pl.BlockSpec((tr, cols), lambda i: (i, 0))]
        grid = (rows // tr,)
    else:
        blocks = [pl.BlockSpec(t[0].shape, lambda i: (0, 0)) for t in two_d]
        grid = (1,)

    def body(*refs):
        for k in range(n):
            w_ref, g_ref, m_ref, v_ref = refs[4 * k:4 * k + 4]
            d_ref, mo_ref, vo_ref = refs[4 * n + 3 * k:4 * n + 3 * k + 3]
            gv = g_ref[...]
            mn = ADAM_B1 * m_ref[...] + (1.0 - ADAM_B1) * gv
            vn = ADAM_B2 * v_ref[...] + (1.0 - ADAM_B2) * (gv * gv)
            m_hat = mn / (1.0 - ADAM_B1 ** ADAM_STEP)
            v_hat = vn / (1.0 - ADAM_B2 ** ADAM_STEP)
            d_ref[...] = -ADAM_LR * (m_hat / (jnp.sqrt(v_hat) + ADAM_EPS) + ADAM_WD * w_ref[...])
            mo_ref[...] = mn
            vo_ref[...] = vn

    outs, got = _pcall_ride(
        body, ride, name="adamw", grid=grid,
        in_specs=[b for b in blocks for _ in range(4)], out_specs=[b for b in blocks for _ in range(3)],
        out_shape=[jax.ShapeDtypeStruct(t[0].shape, F32) for t in two_d for _ in range(3)],
        compiler_params=_seq(), args=tuple(a for t in two_d for a in t))
    return [tuple(o.reshape(items[k][0].shape) for o in outs[3 * k:3 * k + 3]) for k in range(n)], got


WEIGHTS = ["norm_g", "w_ada", "b_ada", "w_in", "rg_conv_w", "rg_conv_b", "rg_w_a", "rg_b_a", "rg_w_x", "rg_b_x",
           "rg_lambda", "ml_conv_w", "ml_conv_b", "ml_w_q", "ml_w_k", "ml_w_v", "ml_w_if", "ml_b_if", "ml_norm_g",
           "w_out", "final_g"]
SMALL_SHARDED = {"ml_w_qkv": 2, "rg_conv_w": 1, "ml_conv_w": 1, "ml_w_if": 0}
REPLICATED = ["rg_w_a", "rg_w_x", "rg_conv_b", "rg_b_a", "rg_b_x", "rg_lambda", "ml_conv_b", "ml_norm_g", "ml_b_if"]
LANES = 128


def _to_pieces(g, axis):
    shp = g.shape
    g = g.reshape(shp[:axis] + (4, 2, shp[axis] // 8) + shp[axis + 1:])
    g = jnp.moveaxis(g, (axis, axis + 1), (0, 1))
    return g.reshape(4, 2, -1)


def _from_pieces(p, shard_shape, axis):
    k = p.shape[0]
    rest = shard_shape[:axis] + (shard_shape[axis] // k,) + shard_shape[axis + 1:]
    t = jnp.moveaxis(p.reshape((k,) + rest), 0, axis)
    return t.reshape(shard_shape)


def _pad_rows(flat, mult):
    n = flat.shape[-1]
    pad = (-n) % mult
    if pad:
        flat = jnp.concatenate([flat, jnp.zeros(flat.shape[:-1] + (pad,), flat.dtype)], axis=-1)
    return flat


def kernel(x, c, norm_g, w_ada, b_ada, w_in, rg_conv_w, rg_conv_b, rg_w_a, rg_b_a, rg_w_x, rg_b_x, rg_lambda, ml_conv_w, ml_conv_b, ml_w_q, ml_w_k, ml_w_v, ml_w_if, ml_b_if, ml_norm_g, w_out, final_g, loss_target, m_norm_g, m_w_ada, m_b_ada, m_w_in, m_rg_conv_w, m_rg_conv_b, m_rg_w_a, m_rg_b_a, m_rg_w_x, m_rg_b_x, m_rg_lambda, m_ml_conv_w, m_ml_conv_b, m_ml_w_q, m_ml_w_k, m_ml_w_v, m_ml_w_if, m_ml_b_if, m_ml_norm_g, m_w_out, m_final_g, v_norm_g, v_w_ada, v_b_ada, v_w_in, v_rg_conv_w, v_rg_conv_b, v_rg_w_a, v_rg_b_a, v_rg_w_x, v_rg_b_x, v_rg_lambda, v_ml_conv_w, v_ml_conv_b, v_ml_w_q, v_ml_w_k, v_ml_w_v, v_ml_w_if, v_ml_b_if, v_ml_norm_g, v_w_out, v_final_g):
    given = dict(locals())
    ax, ay, ac = lax.axis_index("x"), lax.axis_index("y"), lax.axis_index("c")
    chip = 2 * ax + ay
    me = 2 * chip + ac
    depth, d = norm_g.shape
    n_ada = w_ada.shape[2]
    pick = lambda a, i, axis=0: lax.dynamic_index_in_dim(a, i, axis, keepdims=False)

    convs = jnp.stack([rg_conv_w, ml_conv_w])
    n_conv = 2 * depth * CONV_WIDTH // 4
    blk = jnp.concatenate([c, convs.reshape(n_conv, d), jnp.zeros((8 - 1 - n_conv, d), F32)], axis=0)
    g0 = _all_gather8([blk], pltpu.VMEM)[0].reshape(8, 8, d)
    c_all = g0[:, 0, :]
    conv_full = g0[0::2, 1:1 + n_conv].reshape(4, 2, depth, CONV_WIDTH, d // 4)
    conv_full = conv_full.transpose(1, 2, 3, 0, 4).reshape(2, depth, CONV_WIDTH, d)

    b_cols = lax.dynamic_slice_in_dim(b_ada, chip * n_ada, n_ada, axis=1)[:, None, :]
    mod_part = _ada_mod(c_all, w_ada, b_cols)
    g1 = _all_gather8([mod_part.transpose(1, 0, 2).reshape(8, depth * n_ada)], pltpu.VMEM)[0]
    g1 = g1.reshape(8, 8, depth, n_ada)[0::2]
    mod_me = pick(g1.transpose(1, 2, 0, 3).reshape(8, depth, 4 * n_ada), me)

    def half_of(w, axis):
        n = w.shape[axis] // 2
        return lax.dynamic_slice_in_dim(w, ac * n, n, axis).astype(BF16)

    n_sh = w_in.shape[2]
    heads, hd_cut, hd = ml_w_q.shape[1:]

    def blocks_of(l):
        wqkv = jnp.stack([ml_w_q[l], ml_w_k[l], ml_w_v[l]])
        return [half_of(w_in[l], 0), half_of(w_out[l], 0), half_of(wqkv, 2).reshape(-1, hd), half_of(ml_w_if[l], 0)]

    def layer_of(l, w4, rest):
        return dict(
            norm_g=norm_g[l][None], shift=mod_me[l, 0:d][None], scale=mod_me[l, d:2 * d][None],
            gate=mod_me[l, 2 * d:3 * d][None], w4=w4.reshape(4, d, n_sh),
            rg_conv_w=conv_full[0, l], rg_conv_b=rg_conv_b[l][None], rg_wa_b=_bf(rg_w_a[l]), rg_ba=rg_b_a[l][None],
            rg_wx_b=_bf(rg_w_x[l]), rg_bx=rg_b_x[l][None], rg_lam=rg_lambda[l][None],
            ml_conv_w=conv_full[1, l], ml_conv_b=ml_conv_b[l][None], b_if=ml_b_if[l][None], b_ift=ml_b_if[l][:, None],
            ml_g=ml_norm_g[l][None], **rest)

    def rest_of(gathered):
        w_out_b, wqkv_g, wif = gathered
        return dict(w_out_b=w_out_b, wqkv_b=_from_pieces(wqkv_g.reshape(8, -1), (3, heads, hd, hd), 2), wif_b=wif,
                    wift_b=wif.T)

    landing = lambda b: jax.ShapeDtypeStruct((4, 2) + b.shape, b.dtype)
    whole = lambda landed: [t.reshape(-1, t.shape[-1]) for t in _sib_fill(landed)]
    first = blocks_of(0)
    p = layer_of(0, _all_gather8(first[:1], pltpu.HBM)[0], {})
    rides = dict(ln_inproj=Ride(first[1:], [landing(b) for b in first[1:]], False))
    late = lambda landed: rest_of(whole(landed))
    layers, saved = [], []
    xl = x[0]
    for l in range(depth):
        if l + 1 < depth:
            nxt = blocks_of(l + 1)
            rides["rg_fwd"] = Ride(nxt[:1], [landing(nxt[0])], False)
            rides["mlstm_fwd"] = Ride(nxt[1:], [landing(b) for b in nxt[1:]], False)
        xl, s, p, got = _layer_fwd(xl, p, rides, late)
        layers.append(p)
        saved.append(s)
        if l + 1 < depth:
            nxt_whole = whole(list(got["rg_fwd"]) + list(got["mlstm_fwd"]))
            p = layer_of(l + 1, nxt_whole[0], rest_of(nxt_whole[1:]))
            rides, late = {}, None
    dx, g_final, loss = _final_loss(xl, final_g[None], loss_target[0])

    half = ac.reshape(1)
    ids = jnp.stack([chip, ac])
    r_out = w_out.shape[1] // 2

    def partial_in(g_w_in):
        (got_in,) = _sib_halves([g_w_in], ["w_in"])
        return _pair_sum(
            half, g_w_in, pl.BlockSpec((None, d // 2, n_sh), lambda s, h: (0, h[0], s)),
            got_in, pl.BlockSpec((None, None, d // 2, n_sh), lambda s, h: (0, s, 0, 0)),
            jax.ShapeDtypeStruct((4, 1, d // 2, n_sh), BF16),
            pl.BlockSpec((None, None, d // 2, n_sh), lambda s, h: (s, 0, 0, 0)), (4,))

    def pair_out(g_out5, got_out):
        return _pair_sum(
            half, g_out5, pl.BlockSpec((None, None, None, r_out, d), lambda s, h: (0, s, h[0], 0, 0)),
            got_out, pl.BlockSpec((None, None, r_out, d), lambda s, h: (0, s, 0, 0)),
            jax.ShapeDtypeStruct((4, 1, r_out, d), BF16),
            pl.BlockSpec((None, None, r_out, d), lambda s, h: (s, 0, 0, 0)), (4,))

    def pair_slab(slab, got, dtype):
        rows = got.shape[0] // 4
        blk = pl.BlockSpec((rows, LANES), lambda s, h: (s, 0))
        return _pair_sum(half, slab, pl.BlockSpec((None, rows, LANES), lambda s, h: (h[0], s, 0)), got, blk,
                         jax.ShapeDtypeStruct((4 * rows, LANES), dtype), blk, (4,)).reshape(4, 1, rows, LANES)

    row_pad = lambda n: -(-n // (8 * LANES)) * (8 * LANES)

    def as_rows(t):
        if t.shape[-1] == LANES and t.size % (8 * LANES) == 0:
            return t.reshape(-1, LANES)
        return _pad_rows(t.reshape(-1), 8 * LANES).reshape(-1, LANES)

    exchange = lambda parts_l: Ride(parts_l, [jax.ShapeDtypeStruct(t.shape, t.dtype) for t in parts_l], True)
    grads, dmods, parts, mets = [None] * depth, [None] * depth, [None] * depth, [None] * depth
    small = {}

    def early_exchange(first_grads):
        every = [first_grads] + grads[1:]
        sm = jnp.concatenate([_to_pieces(every[l][name], axis) for l in range(depth)
                              for name, axis in SMALL_SHARDED.items()], axis=-1)
        sm = _pad_rows(sm, 16 * LANES)
        sm = sm.transpose(1, 0, 2).reshape(2, -1, LANES)
        rep = [as_rows(every[l][name]) for l in range(depth) for name in REPLICATED]
        rep = jnp.concatenate(rep + [as_rows(g_final), as_rows(loss)], axis=0)
        rep = jnp.concatenate([rep, jnp.zeros(((-rep.shape[0]) % 64, LANES), F32)], axis=0)
        rep = rep.reshape(4, 2, -1, LANES).transpose(1, 0, 2, 3).reshape(2, -1, LANES)
        g_out5 = first_grads["w_out"].reshape(1, 4, 2, r_out, d)
        got_out, got_sm, got_rep = _sib_halves([g_out5, sm, rep], ["w_out", "slab", "slab"])
        small["parts"] = [pair_out(g_out5, got_out), pair_slab(sm, got_sm, BF16), pair_slab(rep, got_rep, F32)]
        return exchange(small["parts"])

    def last_exchange(g_w_in):
        small["part_in"] = partial_in(g_w_in)
        return exchange([small["part_in"]])

    rides = {}
    for l in reversed(range(depth)):
        if l == 0:
            rides.update(grad_w_in=early_exchange, in_bwd=last_exchange)
        dx, grads[l], dmods[l], got = _layer_bwd(dx, layers[l], saved[l], rides)
        if "mlstm_bwd" in rides:
            mets[l + 1] = got["mlstm_bwd"]
        if l > 0:
            g_out5 = grads[l]["w_out"].reshape(1, 4, 2, r_out, d)
            (got_out,) = _sib_halves([g_out5], ["w_out"])
            parts[l] = [partial_in(grads[l]["w_in"]), pair_out(g_out5, got_out)]
            rides = dict(mlstm_bwd=exchange(parts[l]))
    part_out, part_sm, part_rep = small["parts"]
    met_out, met_sm, met_rep = got["grad_w_in"]
    parts[0], mets[0] = [small["part_in"], part_out], [got["in_bwd"][0], met_out]
    n_rep = part_rep.shape[2]

    pad = lambda t: jnp.concatenate([t, jnp.zeros((1, 2 * d), F32)], axis=1)
    rows = [r for l in range(depth) for r in (dmods[l], pad(grads[l]["norm_g"]))]
    blk = jnp.concatenate(rows + [jnp.zeros((8 - 2 * depth, 3 * d), F32)], axis=0)
    rows_all = _all_gather8([blk], pltpu.VMEM)[0].reshape(8, 8, 3 * d)[:, :2 * depth]
    rows_all = rows_all.transpose(1, 0, 2).reshape(depth, 2, 8, 3 * d)
    dm_cols = lax.dynamic_slice_in_dim(rows_all[:, 0], chip * n_ada, n_ada, axis=2)
    g_w_ada, summed = _ada_grad(c_all, dm_cols, rows_all)

    g = dict(w_ada=g_w_ada, b_ada=summed[:, 0, 0], norm_g=summed[:, 1, 0, :d])
    item = lambda name: (given[name], g[name], given["m_" + name], given["v_" + name])
    both_in, both_out = depth, depth
    for l in range(depth):
        both_in = _chip_sum(ids, parts[l][0], mets[l][0], True, l, both_in)
        both_out = _chip_sum(ids, parts[l][1], mets[l][1], True, l, both_out)
    both_in, both_out, both_sm = _sib_fill([both_in, both_out, _chip_sum(ids, part_sm, met_sm, True)])
    red_rep = _chip_sum(ids, part_rep, met_rep, False).reshape(n_rep, LANES)
    rep_all = _all_gather8([red_rep], pltpu.VMEM)[0].reshape(-1)

    g.update(w_in=both_in.reshape(w_in.shape), w_out=both_out.reshape(w_out.shape))
    shard = both_sm.reshape(2, -1)
    off = 0
    per_layer = {name: [] for name in SMALL_SHARDED}
    for l in range(depth):
        for name, axis in SMALL_SHARDED.items():
            shp = (3,) + ml_w_q.shape[1:] if name == "ml_w_qkv" else given[name].shape[1:]
            n = grads[l][name].size // 8
            per_layer[name].append(_from_pieces(shard[:, off:off + n], shp, axis))
            off += n
    for name in SMALL_SHARDED:
        g[name] = jnp.stack(per_layer[name])
    for i, name in enumerate(["ml_w_q", "ml_w_k", "ml_w_v"]):
        g[name] = g["ml_w_qkv"][:, i]
    off = 0
    per_layer = {name: [] for name in REPLICATED}
    for l in range(depth):
        for name in REPLICATED:
            n = given[name][l].size
            per_layer[name].append(rep_all[off:off + n].reshape(given[name].shape[1:]))
            off += row_pad(n)
    for name in REPLICATED:
        g[name] = jnp.stack(per_layer[name])
    g["final_g"] = rep_all[off:off + d]
    loss_all = rep_all[off + row_pad(d)]

    stepped = {}
    rg_mats, ml_mats = ["rg_w_a", "rg_w_x"], ["ml_w_q", "ml_w_k", "ml_w_v"]
    vectors = [n for n in WEIGHTS if n not in ["w_ada", "w_in", "w_out"] + rg_mats + ml_mats]
    for names in (["w_ada"], ["w_in"], ["w_out"], rg_mats, ml_mats, vectors):
        stepped.update(zip(names, _adamw([item(name) for name in names])[0]))
    deltas, new_m, new_v = zip(*[stepped[name] for name in WEIGHTS])
    return (loss_all, dx[None], *[g[name] for name in WEIGHTS], *deltas, *new_m, *new_v)
```

```python
import functools
from typing import NamedTuple

import jax
import jax.numpy as jnp
from jax import lax
from jax.experimental import pallas as pl
from jax.experimental.pallas import tpu as pltpu

F32 = jnp.float32
BF16 = jnp.bfloat16

EPS = 1e-6
RG_C = 8.0
CONV_WIDTH = 4
ML_CHUNK = 512
HALO = 8
ADAM_LR = 0.001
ADAM_B1 = 0.9
ADAM_B2 = 0.999
ADAM_EPS = 1e-08
ADAM_WD = 0.01
ADAM_STEP = 10
MESH = pl.DeviceIdType.MESH


def _pcall(body, **kw):
    return pl.pallas_call(body, **kw)


class Ride(NamedTuple):
    srcs: list
    dst_shapes: list
    sliced: bool


def _pcall_ride(body, ride, *, grid, in_specs, out_specs, out_shape, args, scratch_shapes=(), **kw):
    n_in, n_out, n_scr = len(in_specs), len(out_specs), len(scratch_shapes)
    if ride is None:
        res = _pcall(body, grid=grid, in_specs=in_specs, out_specs=out_specs, out_shape=out_shape,
                     scratch_shapes=list(scratch_shapes), **kw)(*args)
        return res, []
    nr = len(ride.srcs)

    def riding(*refs):
        ins, rsrc = refs[:n_in], refs[n_in:n_in + nr]
        outs, rdst = refs[n_in + nr:n_in + nr + n_out], refs[n_in + nr + n_out:n_in + 2 * nr + n_out]
        scr = refs[n_in + 2 * nr + n_out:n_in + 2 * nr + n_out + n_scr]
        send_sems, recv_sems, local_sems = refs[n_in + 2 * nr + n_out + n_scr:]
        x, y, c = _me()
        me_s = 2 * x + y
        chips = [(1 - x, y), (x, 1 - y), (1 - x, 1 - y)]
        copies, local = [], []
        for i in range(nr):
            for k, (px, py) in enumerate(chips):
                src = rsrc[i].at[2 * px + py] if ride.sliced else rsrc[i]
                dst = rdst[i].at[me_s] if ride.sliced else rdst[i].at[me_s, c]
                copies.append(_remote(src, dst, send_sems.at[3 * i + k], recv_sems.at[3 * i + k], (px, py, c)))
            if not ride.sliced:
                local.append(pltpu.make_async_copy(rsrc[i], rdst[i].at[me_s, c], local_sems.at[i]))
        first = functools.reduce(jnp.logical_and, [pl.program_id(a) == 0 for a in range(len(grid))])
        last = functools.reduce(jnp.logical_and, [pl.program_id(a) == grid[a] - 1 for a in range(len(grid))])

        @pl.when(first)
        def _():
            for cp in copies + local:
                cp.start()

        body(*ins, *outs, *scr)

        @pl.when(last)
        def _():
            for cp in copies:
                cp.wait_recv()
            for cp in copies:
                cp.wait_send()
            for cp in local:
                cp.wait()

    hbm = pl.BlockSpec(memory_space=pltpu.HBM)
    res = _pcall(
        riding, grid=grid, in_specs=list(in_specs) + [hbm] * nr, out_specs=list(out_specs) + [hbm] * nr,
        out_shape=list(out_shape) + list(ride.dst_shapes),
        scratch_shapes=list(scratch_shapes) + [pltpu.SemaphoreType.DMA((3 * nr,)), pltpu.SemaphoreType.DMA((3 * nr,)),
                                               pltpu.SemaphoreType.DMA((nr,))], **kw)(*args, *ride.srcs)
    return res[:n_out], res[n_out:]


def _seq(n=1):
    return pltpu.CompilerParams(dimension_semantics=("arbitrary",) * n)


def _dot(a, b):
    return jnp.dot(a, b, preferred_element_type=F32)


def _dot_nt(a, b):
    return lax.dot_general(a, b, (((1,), (1,)), ((), ())), preferred_element_type=F32)


def _dot_tn(a, b):
    return lax.dot_general(a, b, (((0,), (0,)), ((), ())), preferred_element_type=F32)


def _bf(x):
    return x.astype(BF16)


def _sigmoid(x):
    return 0.5 * jnp.tanh(0.5 * x) + 0.5


def _log1p(z):
    u = 1.0 + z
    return jnp.where(u == 1.0, z, jnp.log(u) * (z / jnp.where(u == 1.0, 1.0, u - 1.0)))


def _softplus(x):
    return jnp.maximum(x, 0.0) + _log1p(jnp.exp(-jnp.abs(x)))


def _log_sigmoid(x):
    return -_softplus(-x)


def _one_minus_sq(a, log_a):
    x = 2.0 * log_a
    small = -x * (1.0 + x * (0.5 + x * (1.0 / 6.0)))
    return jnp.where(x > -0.004, small, 1.0 - a * a)


def _dsilu(x, s):
    return s * (1.0 + x * (1.0 - s))


def _rowsum(x):
    return jnp.sum(x, axis=1, keepdims=True)


def _colsum(x):
    return jnp.sum(x, axis=0, keepdims=True)


def _shift_down(win, s):
    return win if s == 0 else pltpu.roll(win, s, 0)


def _shift_up(win, s):
    return win if s == 0 else pltpu.roll(win, win.shape[0] - s, 0)


def _conv_taps(win):
    return [_shift_down(win, CONV_WIDTH - 1 - k)[HALO:] for k in range(CONV_WIDTH)]


def _conv_fwd(taps, w_ref, b_ref):
    acc = b_ref[...] + w_ref[CONV_WIDTH - 1:CONV_WIDTH, :] * taps[CONV_WIDTH - 1]
    for k in range(CONV_WIDTH - 1):
        acc = acc + w_ref[k:k + 1, :] * taps[k]
    return acc


def _split3(x):
    hi = _bf(x)
    r1 = x - hi.astype(F32)
    mid = _bf(r1)
    lo = _bf(r1 - mid.astype(F32))
    return hi, mid, lo


def _tri_dot_left(tri, x):
    hi, mid, lo = _split3(x)
    return _dot(tri, hi) + _dot(tri, mid) + _dot(tri, lo)


def _tri_dot_right(x, tri):
    hi, mid, lo = _split3(x)
    return _dot(hi, tri) + _dot(mid, tri) + _dot(lo, tri)


def _tile(n, want):
    t = min(n, want)
    assert n % t == 0
    return t


def _ln_inproj(x, g, scale, shift, w4, ride=None):
    s_len, d = x.shape
    nj, _, nsh = w4.shape
    tm = _tile(s_len, 1024)

    def body(x_ref, g_ref, sc_ref, sh_ref, w_ref, h_ref, u_ref, hs):
        @pl.when(pl.program_id(1) == 0)
        def _():
            xv = x_ref[...]
            r = lax.rsqrt(jnp.mean(xv * xv, axis=-1, keepdims=True) + EPS)
            hv = (xv * r * g_ref[...]) * (1.0 + sc_ref[...]) + sh_ref[...]
            hs[...] = _bf(hv)
            h_ref[...] = hs[...]

        u_ref[...] = _dot(hs[...], w_ref[0])

    vec = pl.BlockSpec((1, d), lambda i, j: (0, 0))
    return _pcall_ride(
        body, ride, name="ln_inproj", grid=(s_len // tm, nj),
        in_specs=[pl.BlockSpec((tm, d), lambda i, j: (i, 0)), vec, vec, vec,
                  pl.BlockSpec((1, d, nsh), lambda i, j: (j, 0, 0))],
        out_specs=[pl.BlockSpec((tm, d), lambda i, j: (i, 0)), pl.BlockSpec((tm, nsh), lambda i, j: (i, j))],
        out_shape=[jax.ShapeDtypeStruct((s_len, d), BF16), jax.ShapeDtypeStruct((s_len, nj * nsh), F32)],
        scratch_shapes=[pltpu.VMEM((tm, d), BF16)],
        compiler_params=_seq(2),
        args=(x, g, scale, shift, w4))


def _rg_gates(xc, wa_ref, ba_ref, wx_ref, bx_ref, lam_ref):
    heads, hd, _ = wa_ref.shape
    xb = _bf(xc)
    ga = jnp.concatenate([_dot(xb[:, h * hd:(h + 1) * hd], wa_ref[h]) for h in range(heads)], axis=1) + ba_ref[...]
    gx = jnp.concatenate([_dot(xb[:, h * hd:(h + 1) * hd], wx_ref[h]) for h in range(heads)], axis=1) + bx_ref[...]
    r = _sigmoid(ga)
    ig = _sigmoid(gx)
    sp = _softplus(-lam_ref[...])
    log_a = (-RG_C) * r * sp
    a = jnp.exp(log_a)
    mult = jnp.sqrt(_one_minus_sq(a, log_a))
    return r, ig, sp, log_a, a, mult


def _scan_groups(a, u, reverse):
    n, c = a.shape
    a = a.reshape(n // 8, 8, c)
    u = u.reshape(n // 8, 8, c)
    row = lax.broadcasted_iota(jnp.int32, a.shape, 1)
    for k in (1, 2, 4):
        sft = 8 - k if reverse else k
        a_sh, u_sh = pltpu.roll(a, sft, 1), pltpu.roll(u, sft, 1)
        ok = row < 8 - k if reverse else row >= k
        u = jnp.where(ok, a * u_sh + u, u)
        a = jnp.where(ok, a * a_sh, a)
    return a.reshape(n, c), u.reshape(n, c)


def _rg_fwd(u, conv_w, conv_b, wa_b, ba, wx_b, bx, lam, ride=None):
    s_len = u.shape[0]
    d = conv_w.shape[1]
    tm = _tile(s_len, 256)
    per = tm // HALO

    def body(x_ref, xp_ref, z_ref, cw_ref, cb_ref, wa_ref, ba_ref, wx_ref, bx_ref, lam_ref,
             hh_ref, y_ref, carry):
        i = pl.program_id(0)

        @pl.when(i == 0)
        def _():
            carry[...] = jnp.zeros_like(carry)

        prev = jnp.where(i == 0, 0.0, xp_ref[...])
        xc = _conv_fwd(_conv_taps(jnp.concatenate([prev, x_ref[...]], axis=0)), cw_ref, cb_ref)
        _, ig, _, _, a, mult = _rg_gates(xc, wa_ref, ba_ref, wx_ref, bx_ref, lam_ref)
        ca, cu = _scan_groups(a, mult * (ig * xc), reverse=False)
        c = carry[0:1, :]
        for j in range(per):
            blk = ca[j * 8:(j + 1) * 8] * c + cu[j * 8:(j + 1) * 8]
            hh_ref[j * 8:(j + 1) * 8, :] = blk
            c = blk[7:8]
        carry[0:1, :] = c
        z = z_ref[...]
        y_ref[0] = _bf(hh_ref[...] * (z * _sigmoid(z)))

    vec = pl.BlockSpec((1, d), lambda i: (0, 0))
    whole3 = lambda a: pl.BlockSpec(a.shape, lambda i: (0, 0, 0))
    return _pcall_ride(
        body, ride, name="rg_fwd", grid=(s_len // tm,),
        in_specs=[pl.BlockSpec((tm, d), lambda i: (i, 0)),
                  pl.BlockSpec((HALO, d), lambda i: (jnp.maximum(i * per - 1, 0), 0)),
                  pl.BlockSpec((tm, d), lambda i: (i, 1)),
                  pl.BlockSpec((CONV_WIDTH, d), lambda i: (0, 0)), vec,
                  whole3(wa_b), vec, whole3(wx_b), vec, vec],
        out_specs=[pl.BlockSpec((tm, d), lambda i: (i, 0)), pl.BlockSpec((1, tm, d), lambda i: (0, i, 0))],
        out_shape=[jax.ShapeDtypeStruct((s_len, d), F32), jax.ShapeDtypeStruct((2, s_len, d), BF16)],
        scratch_shapes=[pltpu.VMEM((8, d), F32)],
        compiler_params=_seq(),
        args=(u, u, u, conv_w, conv_b, wa_b, ba, wx_b, bx, lam))


def _ml_pre(u, conv_w, conv_b, wqkv_b, wif_b, wift_b, b_if, b_ift):
    s_len = u.shape[0]
    d = conv_w.shape[1]
    _, heads, hd, _ = wqkv_b.shape
    ng = 2 * heads
    tm = _tile(s_len, max(256, ML_CHUNK))
    per = tm // HALO

    def body(x_ref, xp_ref, cw_ref, cb_ref, w_ref, wif_ref, wift_ref, bif_ref, bift_ref,
             qkv_ref, gt_ref, gtt_ref, bc_ref, bct_ref):
        i = pl.program_id(0)
        prev = jnp.where(i == 0, 0.0, xp_ref[...])
        xm = x_ref[...]
        pre = _conv_fwd(_conv_taps(jnp.concatenate([prev, xm], axis=0)), cw_ref, cb_ref)
        xcb = _bf(pre * _sigmoid(pre))
        xmb = _bf(xm)
        for h in range(heads):
            hs = slice(h * hd, (h + 1) * hd)
            qkv_ref[0, :, hs] = _bf(_dot(xcb[:, hs], w_ref[0, h]))
            qkv_ref[1, :, hs] = _bf(_dot(xcb[:, hs], w_ref[1, h]))
            qkv_ref[2, :, hs] = _bf(_dot(xmb[:, hs], w_ref[2, h]))
        qb, kb, vb = qkv_ref[0], qkv_ref[1], qkv_ref[2]
        gt = (_dot(qb, wif_ref[0:d, :]) + _dot(kb, wif_ref[d:2 * d, :]) + _dot(vb, wif_ref[2 * d:3 * d, :])
              + bif_ref[...])
        gtt = (_dot_nt(wift_ref[:, 0:d], qb) + _dot_nt(wift_ref[:, d:2 * d], kb)
               + _dot_nt(wift_ref[:, 2 * d:3 * d], vb) + bift_ref[...])
        gt_ref[...] = gt
        gtt_ref[...] = gtt
        r = lax.broadcasted_iota(jnp.int32, (tm, tm), 0)
        c = lax.broadcasted_iota(jnp.int32, (tm, tm), 1)
        same = (r // ML_CHUNK) == (c // ML_CHUNK)
        bc_ref[...] = _tri_dot_left(((r >= c) & same).astype(BF16), _log_sigmoid(gt))
        bct_ref[...] = _tri_dot_right(_log_sigmoid(gtt), ((r <= c) & same).astype(BF16))

    vec = pl.BlockSpec((1, d), lambda i: (0, 0))
    whole2 = lambda a: pl.BlockSpec(a.shape, lambda i: (0, 0))
    col = pl.BlockSpec((tm, ng), lambda i: (i, 0))
    row = pl.BlockSpec((ng, tm), lambda i: (0, i))
    return _pcall(
        body, name="ml_pre", grid=(s_len // tm,),
        in_specs=[pl.BlockSpec((tm, d), lambda i: (i, 2)),
                  pl.BlockSpec((HALO, d), lambda i: (jnp.maximum(i * per - 1, 0), 2)),
                  pl.BlockSpec((CONV_WIDTH, d), lambda i: (0, 0)), vec,
                  pl.BlockSpec(wqkv_b.shape, lambda i: (0, 0, 0, 0)), whole2(wif_b), whole2(wift_b), whole2(b_if),
                  whole2(b_ift)],
        out_specs=[pl.BlockSpec((3, tm, d), lambda i: (0, i, 0)), col, row, col, row],
        out_shape=[jax.ShapeDtypeStruct((3, s_len, d), BF16), jax.ShapeDtypeStruct((s_len, ng), F32),
                   jax.ShapeDtypeStruct((ng, s_len), F32), jax.ShapeDtypeStruct((s_len, ng), F32),
                   jax.ShapeDtypeStruct((ng, s_len), F32)],
        compiler_params=_seq(),
    )(u, u, conv_w, conv_b, wqkv_b, wif_b, wift_b, b_if, b_ift)


def _chunk_gates(gt, gtt, bc, bct, h, heads):
    li_c = gt[:, h:h + 1]
    li_r = gtt[h:h + 1, :]
    gf_c = gt[:, heads + h:heads + h + 1]
    b_c = bc[:, heads + h:heads + h + 1]
    b_r = bct[heads + h:heads + h + 1, :]
    return li_c, li_r, gf_c, b_c, b_r


def _chunk_weights(li_c, li_r, b_c, b_r, m_prev, causal):
    lc = b_c.shape[0]
    b_last = b_c[lc - 1:lc, :]
    dmat = jnp.where(causal, b_c - b_r + li_r, -jnp.inf)
    m_inter = b_c + m_prev
    m_t = jnp.maximum(m_inter, jnp.max(dmat, axis=1, keepdims=True))
    w_intra = jnp.exp(dmat - m_t)
    w_inter = jnp.exp(m_inter - m_t)
    g_c = b_last - b_c + li_c
    m_new = jnp.maximum(b_last + m_prev, jnp.max(g_c, axis=0, keepdims=True))
    w_state = jnp.exp(g_c - m_new)
    decay = jnp.exp(b_last + m_prev - m_new)
    return m_t, w_intra, w_inter, m_new, w_state, decay


def _tri_masks(lc):
    r = lax.broadcasted_iota(jnp.int32, (lc, lc), 0)
    c = lax.broadcasted_iota(jnp.int32, (lc, lc), 1)
    causal = r >= c
    return causal, causal.astype(BF16), (r <= c).astype(BF16)


def _mlstm_fwd(qkv, gates, u, ml_g, ycat, ride=None):
    _, s_len, d = qkv.shape
    ng = gates[0].shape[1]
    heads = ng // 2
    hd = d // heads
    lc = ML_CHUNK
    nc = s_len // lc
    kscale = hd ** -0.5

    def body(qkv_ref, gt_ref, gtt_ref, bc_ref, bct_ref, o_ref, z_ref, g_ref, _, cell_ref, y_ref, cst_ref, nst_ref,
             mst_ref, cs, ns, ms):
        @pl.when(pl.program_id(0) == 0)
        def _():
            cs[...] = jnp.zeros_like(cs)
            ns[...] = jnp.zeros_like(ns)
            ms[...] = jnp.zeros_like(ms)

        causal = _tri_masks(lc)[0]
        gtv, gttv, bcv, bctv = gt_ref[...], gtt_ref[...], bc_ref[...], bct_ref[...]
        old = [(cs[h], ns[h], ms[h]) for h in range(heads)]
        new, cells, ys = [], [], []
        for h in range(heads):
            hs = slice(h * hd, (h + 1) * hd)
            li_c, li_r, _, b_c, b_r = _chunk_gates(gtv, gttv, bcv, bctv, h, heads)
            c_old, n_old, m_old = old[h]
            m_prev = m_old[:, 0:1]
            m_t, w_intra, w_inter, m_new, w_state, decay = _chunk_weights(li_c, li_r, b_c, b_r, m_prev, causal)
            qb = qkv_ref[0, :, hs]
            ks = qkv_ref[1, :, hs].astype(F32) * kscale
            kb = _bf(ks)
            vb = qkv_ref[2, :, hs]
            s = _dot_nt(qb, kb) * w_intra
            num = _dot(_bf(s), vb) + w_inter * _dot(qb, _bf(c_old))
            den = _rowsum(s) + w_inter * _rowsum(qb.astype(F32) * n_old)
            cell = num / jnp.maximum(jnp.abs(den), jnp.exp(-m_t))
            kw = ks * w_state
            new.append((decay * c_old + _dot_tn(_bf(kw), vb), decay * n_old + _colsum(kw),
                        jnp.broadcast_to(m_new, m_old.shape)))
            cells.append(cell)
            hm = _sigmoid(o_ref[:, hs]) * cell
            hn = hm * lax.rsqrt(jnp.mean(hm * hm, axis=-1, keepdims=True) + EPS)
            z = z_ref[:, hs]
            ys.append(_bf((hn * g_ref[:, hs]) * (z * _sigmoid(z))))
        for h in range(heads):
            cst_ref[0, h] = _bf(old[h][0])
            nst_ref[0, h] = old[h][1]
            mst_ref[0, h] = old[h][2]
            cs[h], ns[h], ms[h] = new[h]
        cell_ref[...] = jnp.concatenate(cells, axis=1)
        y_ref[0] = jnp.concatenate(ys, axis=1)

    row = pl.BlockSpec((lc, d), lambda c: (c, 0))
    gcol = pl.BlockSpec((lc, ng), lambda c: (c, 0))
    grow = pl.BlockSpec((ng, lc), lambda c: (0, c))
    return _pcall_ride(
        body, ride, name="mlstm_fwd", grid=(nc,),
        in_specs=[pl.BlockSpec((3, lc, d), lambda c: (0, c, 0)), gcol, grow, gcol, grow,
                  pl.BlockSpec((lc, d), lambda c: (c, 3)), pl.BlockSpec((lc, d), lambda c: (c, 4)),
                  pl.BlockSpec((1, d), lambda c: (0, 0)), pl.BlockSpec(memory_space=pl.ANY)],
        out_specs=[row, pl.BlockSpec((1, lc, d), lambda c: (1, c, 0)),
                   pl.BlockSpec((1, heads, hd, hd), lambda c: (c, 0, 0, 0)),
                   pl.BlockSpec((1, heads, 1, hd), lambda c: (c, 0, 0, 0)),
                   pl.BlockSpec((1, heads, 1, 128), lambda c: (c, 0, 0, 0))],
        out_shape=[jax.ShapeDtypeStruct((s_len, d), F32), jax.ShapeDtypeStruct(ycat.shape, BF16),
                   jax.ShapeDtypeStruct((nc, heads, hd, hd), BF16),
                   jax.ShapeDtypeStruct((nc, heads, 1, hd), F32),
                   jax.ShapeDtypeStruct((nc, heads, 1, 128), F32)],
        scratch_shapes=[pltpu.VMEM((heads, hd, hd), F32), pltpu.VMEM((heads, 1, hd), F32),
                        pltpu.VMEM((heads, 1, 128), F32)],
        input_output_aliases={8: 1},
        compiler_params=_seq(),
        args=(qkv, *gates, u, u, ml_g, ycat))


def _out_proj(ycat, w_out_b, x, gate):
    s_len, d = x.shape
    tm = _tile(s_len, 1024)

    def body(a_ref, w_ref, x_ref, g_ref, y_ref, xn_ref):
        y = _dot(a_ref[0], w_ref[0:d, :]) + _dot(a_ref[1], w_ref[d:2 * d, :])
        y_ref[...] = y
        xn_ref[...] = x_ref[...] + g_ref[...] * y

    row = pl.BlockSpec((tm, d), lambda i: (i, 0))
    return _pcall(
        body, name="out_proj", grid=(s_len // tm,),
        in_specs=[pl.BlockSpec((2, tm, d), lambda i: (0, i, 0)), pl.BlockSpec((2 * d, d), lambda i: (0, 0)), row,
                  pl.BlockSpec((1, d), lambda i: (0, 0))],
        out_specs=[row, row],
        out_shape=[jax.ShapeDtypeStruct((s_len, d), F32)] * 2,
        compiler_params=_seq(),
    )(ycat, w_out_b, x, gate)


def _final_loss(x, g, target):
    s_len, d = x.shape
    tm = _tile(s_len, 256)

    def body(x_ref, g_ref, t_ref, dx_ref, dg_ref, loss_ref):
        @pl.when(pl.program_id(0) == 0)
        def _():
            dg_ref[...] = jnp.zeros_like(dg_ref)
            loss_ref[...] = jnp.zeros_like(loss_ref)

        xv = x_ref[...]
        r = lax.rsqrt(jnp.mean(xv * xv, axis=-1, keepdims=True) + EPS)
        xn = xv * r
        err = xn * g_ref[...] - t_ref[...]
        loss_ref[...] += 0.5 * jnp.sum(jnp.mean(err * err, axis=-1, keepdims=True))
        dout = err * (1.0 / d)
        dg_ref[...] += _colsum(dout * xn)
        dxn = dout * g_ref[...]
        dx_ref[...] = r * (dxn - xn * jnp.mean(dxn * xn, axis=-1, keepdims=True))

    row = pl.BlockSpec((tm, d), lambda i: (i, 0))
    vec = pl.BlockSpec((1, d), lambda i: (0, 0))
    return _pcall(
        body, name="final_loss", grid=(s_len // tm,),
        in_specs=[row, vec, row],
        out_specs=[row, vec, pl.BlockSpec((1, 128), lambda i: (0, 0))],
        out_shape=[jax.ShapeDtypeStruct((s_len, d), F32), jax.ShapeDtypeStruct((1, d), F32),
                   jax.ShapeDtypeStruct((1, 128), F32)],
        compiler_params=_seq(),
    )(x, g, target)


def _out_bwd(dxn, y, gate, w_out_b):
    s_len, d = dxn.shape
    tm = _tile(s_len, 1024)

    def body(dx_ref, y_ref, g_ref, w_ref, dg_ref, dy_ref, dc_ref):
        @pl.when(pl.program_id(0) == 0)
        def _():
            dg_ref[...] = jnp.zeros_like(dg_ref)

        dx = dx_ref[...]
        dg_ref[...] += _colsum(dx * y_ref[...])
        dy = _bf(g_ref[...] * dx)
        dy_ref[...] = dy
        dc_ref[0] = _dot_nt(dy, w_ref[0:d, :])
        dc_ref[1] = _dot_nt(dy, w_ref[d:2 * d, :])

    row = pl.BlockSpec((tm, d), lambda i: (i, 0))
    vec = pl.BlockSpec((1, d), lambda i: (0, 0))
    return _pcall(
        body, name="out_bwd", grid=(s_len // tm,),
        in_specs=[row, row, vec, pl.BlockSpec((2 * d, d), lambda i: (0, 0))],
        out_specs=[vec, row, pl.BlockSpec((2, tm, d), lambda i: (0, i, 0))],
        out_shape=[jax.ShapeDtypeStruct((1, d), F32), jax.ShapeDtypeStruct((s_len, d), BF16),
                   jax.ShapeDtypeStruct((2, s_len, d), F32)],
        compiler_params=_seq(),
    )(dxn, y, gate, w_out_b)


def _grad_matmul(a3, b3, nblk, a_idx, b_idx, out_shape, out_block, out_idx, ride=None):
    _, s_len, m = a3.shape
    n = b3.shape[2]
    tk = _tile(s_len, 2048)

    def body(a_ref, b_ref, o_ref):
        @pl.when(pl.program_id(1) == 0)
        def _():
            o_ref[...] = jnp.zeros_like(o_ref)

        o_ref[...] += _dot_tn(a_ref[0], b_ref[0])

    (out,), got = _pcall_ride(
        body, ride, name="grad_matmul", grid=(nblk, s_len // tk),
        in_specs=[pl.BlockSpec((1, tk, m), lambda p, t: (a_idx(p), t, 0)),
                  pl.BlockSpec((1, tk, n), lambda p, t: (b_idx(p), t, 0))],
        out_specs=[pl.BlockSpec((None,) + out_block, lambda p, t: (0,) + out_idx(p))],
        out_shape=[jax.ShapeDtypeStruct((1,) + out_shape, F32)],
        compiler_params=_seq(2), args=(a3, b3))
    return out, got


DU_PLANE = (2, 3, 4, 0, 1)


def _mlstm_bwd(qkv, gates, cst, nst, mst, cell, u, ml_g, d_ycat, wif_b, ride=None):
    _, s_len, d = qkv.shape
    ng = gates[0].shape[1]
    heads = ng // 2
    hd = d // heads
    lc = ML_CHUNK
    nc = s_len // lc
    kscale = hd ** -0.5

    def body(qkv_ref, gt_ref, gtt_ref, bc_ref, bct_ref, cst_ref, nst_ref, mst_ref, cell_ref, o_ref, z_ref, g_ref, dy_ref,
             wif_ref, dqkv_ref, dgt_ref, dbif_ref, du_ref, dg_ref, dcs, dns):
        @pl.when(pl.program_id(0) == 0)
        def _():
            dbif_ref[...] = jnp.zeros_like(dbif_ref)
            dcs[...] = jnp.zeros_like(dcs)
            dns[...] = jnp.zeros_like(dns)
            dg_ref[...] = jnp.zeros_like(dg_ref)

        causal, tril, triu = _tri_masks(lc)
        tril_strict = (tril.astype(F32) - (tril * triu).astype(F32)).astype(BF16)
        gtv, gttv, bcv, bctv = gt_ref[...], gtt_ref[...], bc_ref[...], bct_ref[...]
        lane = lax.broadcasted_iota(jnp.int32, (lc, ng), 1)
        dli_all = jnp.zeros((lc, ng), F32)
        from_later = jnp.zeros((lc, ng), F32)
        from_earlier = jnp.zeros((lc, ng), F32)
        across_all = jnp.zeros((1, ng), F32)
        old = [(dcs[h], dns[h]) for h in range(heads)]
        new, d_o, d_z, d_g, dqs, dks, dvs = [], [], [], [], [], [], []
        for h in range(heads):
            hs = slice(h * hd, (h + 1) * hd)
            li_c, li_r, gf_c, b_c, b_r = _chunk_gates(gtv, gttv, bcv, bctv, h, heads)
            m_prev = mst_ref[0, h][:, 0:1]
            m_t, w_intra, w_inter, _, w_state, decay = _chunk_weights(li_c, li_r, b_c, b_r, m_prev, causal)
            qb = qkv_ref[0, :, hs]
            qf = qb.astype(F32)
            ks = qkv_ref[1, :, hs].astype(F32) * kscale
            kb = _bf(ks)
            vb = qkv_ref[2, :, hs]
            c_b = cst_ref[0, h]
            n_old = nst_ref[0, h]
            s = _dot_nt(qb, kb) * w_intra
            den = _rowsum(s) + w_inter * _rowsum(qf * n_old)
            floor = jnp.exp(-m_t)
            dstab = jnp.maximum(jnp.abs(den), floor)
            cell = cell_ref[:, hs]
            o = o_ref[:, hs]
            so = _sigmoid(o)
            hm = so * cell
            rinv = lax.rsqrt(jnp.mean(hm * hm, axis=-1, keepdims=True) + EPS)
            hn = hm * rinv
            z = z_ref[:, hs]
            sgz = _sigmoid(z)
            sz = z * sgz
            gh = g_ref[:, hs]
            dy = dy_ref[0, :, hs]
            d_z.append(_bf(dy * (hn * gh) * _dsilu(z, sgz)))
            d_g.append(_colsum(dy * hn * sz))
            dhn = dy * gh * sz
            dhm = rinv * (dhn - hn * jnp.mean(dhn * hn, axis=-1, keepdims=True))
            d_o.append(_bf(dhm * cell * so * (1.0 - so)))
            dcell = dhm * so
            dnum = dcell / dstab
            dnb = _bf(dnum)
            dden = -_rowsum(dcell * cell) / dstab * jnp.where(jnp.abs(den) > floor, jnp.where(den > 0.0, 1.0, -1.0), 0.0)
            dst = _dot_nt(dnb, vb) + dden
            dsdb = _bf(dst * w_intra)
            dc_out, dn_out = old[h]
            dcb = _bf(dc_out)
            dq_inter = w_inter * (_dot_nt(dnb, c_b) + dden * n_old)
            dk_inter = w_state * (_dot_nt(vb, dcb) + dn_out)
            dq = _dot(dsdb, kb) + dq_inter
            dk = _dot_tn(dsdb, qb) + dk_inter
            dv = _dot_tn(_bf(s), dnb) + _dot(_bf(ks * w_state), dcb)
            wq = w_inter * qf
            new.append((decay * dc_out + _dot_tn(_bf(wq), dnb), decay * dn_out + _colsum(wq * dden)))
            pmat = dst * s
            p_rows = _rowsum(pmat)
            p_cols = _rowsum(pmat.T)
            q_in = _rowsum(qf * dq_inter)
            k_in = _rowsum(ks * dk_inter)
            across = decay * (jnp.sum(dc_out * c_b.astype(F32), keepdims=True) + jnp.sum(dn_out * n_old, keepdims=True))
            dli_all = dli_all + jnp.where(lane == h, p_cols + k_in, 0.0)
            from_later = from_later + jnp.where(lane == heads + h, p_rows - p_cols + q_in, 0.0)
            from_earlier = from_earlier + jnp.where(lane == heads + h, k_in, 0.0)
            across_all = across_all + jnp.where(lane[0:1] == heads + h, across, 0.0)
            dqs.append(dq)
            dks.append(dk * kscale)
            dvs.append(dv)
        for h in range(heads):
            dcs[h], dns[h] = new[h]
        du_ref[0] = jnp.concatenate(d_o, axis=1)
        du_ref[1] = jnp.concatenate(d_z, axis=1)
        dg_ref[...] += jnp.concatenate(d_g, axis=1)
        dlf = _tri_dot_left(triu, from_later) + _tri_dot_left(tril_strict, from_earlier) + across_all
        dgt = dli_all + dlf * _sigmoid(-gtv)
        dgt_ref[...] = dgt
        dbif_ref[...] += _colsum(dgt)
        dgb = _bf(dgt)
        dqkv_ref[0] = _bf(jnp.concatenate(dqs, axis=1) + _dot_nt(dgb, wif_ref[0:d, :]))
        dqkv_ref[1] = _bf(jnp.concatenate(dks, axis=1) + _dot_nt(dgb, wif_ref[d:2 * d, :]))
        dqkv_ref[2] = _bf(jnp.concatenate(dvs, axis=1) + _dot_nt(dgb, wif_ref[2 * d:3 * d, :]))

    rev = lambda c: nc - 1 - c
    row = pl.BlockSpec((lc, d), lambda c: (rev(c), 0))
    gcol = pl.BlockSpec((lc, ng), lambda c: (rev(c), 0))
    grow = pl.BlockSpec((ng, lc), lambda c: (0, rev(c)))
    return _pcall_ride(
        body, ride, name="mlstm_bwd", grid=(nc,),
        in_specs=[pl.BlockSpec((3, lc, d), lambda c: (0, rev(c), 0)), gcol, grow, gcol, grow,
                  pl.BlockSpec((1, heads, hd, hd), lambda c: (rev(c), 0, 0, 0)),
                  pl.BlockSpec((1, heads, 1, hd), lambda c: (rev(c), 0, 0, 0)),
                  pl.BlockSpec((1, heads, 1, 128), lambda c: (rev(c), 0, 0, 0)),
                  row, pl.BlockSpec((lc, d), lambda c: (rev(c), 3)), pl.BlockSpec((lc, d), lambda c: (rev(c), 4)),
                  pl.BlockSpec((1, d), lambda c: (0, 0)), pl.BlockSpec((1, lc, d), lambda c: (1, rev(c), 0)),
                  pl.BlockSpec((3 * d, ng), lambda c: (0, 0))],
        out_specs=[pl.BlockSpec((3, lc, d), lambda c: (0, rev(c), 0)), pl.BlockSpec((lc, ng), lambda c: (rev(c), 0)),
                   pl.BlockSpec((1, ng), lambda c: (0, 0)), pl.BlockSpec((2, lc, d), lambda c: (0, rev(c), 0)),
                   pl.BlockSpec((1, d), lambda c: (0, 0))],
        out_shape=[jax.ShapeDtypeStruct((3, s_len, d), BF16), jax.ShapeDtypeStruct((s_len, ng), F32),
                   jax.ShapeDtypeStruct((1, ng), F32), jax.ShapeDtypeStruct((5, s_len, d), BF16),
                   jax.ShapeDtypeStruct((1, d), F32)],
        scratch_shapes=[pltpu.VMEM((heads, hd, hd), F32), pltpu.VMEM((heads, 1, hd), F32)],
        compiler_params=_seq(),
        args=(qkv, *gates, cst, nst, mst, cell, u, u, ml_g, d_ycat, wif_b))


def _conv_bwd_tile(dp, later, taps, cw_ref, gw_ref, gb_ref):
    tm = dp.shape[0]
    dwin = jnp.concatenate([dp, later[...]], axis=0)
    later[...] = dp[0:HALO]
    acc = cw_ref[CONV_WIDTH - 1:CONV_WIDTH, :] * dp
    for k in range(CONV_WIDTH):
        if k < CONV_WIDTH - 1:
            acc = acc + cw_ref[k:k + 1, :] * _shift_up(dwin, CONV_WIDTH - 1 - k)[0:tm]
        gw_ref[k:k + 1, :] += _colsum(dp * taps[k])
    gb_ref[...] += _colsum(dp)
    return acc


def _ml_pre_bwd(dqkv, u, conv_w, conv_b, wqkv_b, du):
    s_len = u.shape[0]
    d = conv_w.shape[1]
    _, heads, hd, _ = wqkv_b.shape
    tm = _tile(s_len, 256)
    per = tm // HALO
    nt = s_len // tm

    def body(dqkv_ref, x_ref, xp_ref, cw_ref, cb_ref, w_ref, _, dx_ref, gw_ref, gcw_ref, gcb_ref, later, dps, dxs):
        i = pl.program_id(0)

        @pl.when(i == 0)
        def _():
            gw_ref[...] = jnp.zeros_like(gw_ref)
            gcw_ref[...] = jnp.zeros_like(gcw_ref)
            gcb_ref[...] = jnp.zeros_like(gcb_ref)
            later[...] = jnp.zeros_like(later)

        prev = jnp.where(i == nt - 1, 0.0, xp_ref[...])
        xm = x_ref[...]
        taps = _conv_taps(jnp.concatenate([prev, xm], axis=0))
        pre = _conv_fwd(taps, cw_ref, cb_ref)
        sg = _sigmoid(pre)
        xcb = _bf(pre * sg)
        xmb = _bf(xm)
        for h in range(heads):
            hs = slice(h * hd, (h + 1) * hd)
            dqh, dkh, dvh = dqkv_ref[0, :, hs], dqkv_ref[1, :, hs], dqkv_ref[2, :, hs]
            dxc = _dot_nt(dqh, w_ref[0, h]) + _dot_nt(dkh, w_ref[1, h])
            dps[:, hs] = dxc * _dsilu(pre[:, hs], sg[:, hs])
            dxs[:, hs] = _dot_nt(dvh, w_ref[2, h])
            gw_ref[0, h] += _dot_tn(xcb[:, hs], dqh)
            gw_ref[1, h] += _dot_tn(xcb[:, hs], dkh)
            gw_ref[2, h] += _dot_tn(xmb[:, hs], dvh)
        dx_ref[0] = _bf(_conv_bwd_tile(dps[...], later, taps, cw_ref, gcw_ref, gcb_ref) + dxs[...])

    rev = lambda i: nt - 1 - i
    vec = pl.BlockSpec((1, d), lambda i: (0, 0))
    cwb = pl.BlockSpec((CONV_WIDTH, d), lambda i: (0, 0))
    whole4 = pl.BlockSpec(wqkv_b.shape, lambda i: (0, 0, 0, 0))
    return _pcall(
        body, name="ml_pre_bwd", grid=(nt,),
        in_specs=[pl.BlockSpec((3, tm, d), lambda i: (0, rev(i), 0)), pl.BlockSpec((tm, d), lambda i: (rev(i), 2)),
                  pl.BlockSpec((HALO, d), lambda i: (jnp.maximum(rev(i) * per - 1, 0), 2)),
                  cwb, vec, whole4, pl.BlockSpec(memory_space=pl.ANY)],
        out_specs=[pl.BlockSpec((1, tm, d), lambda i: (DU_PLANE[2], rev(i), 0)), whole4, cwb, vec],
        out_shape=[jax.ShapeDtypeStruct(du.shape, BF16), jax.ShapeDtypeStruct(wqkv_b.shape, F32),
                   jax.ShapeDtypeStruct((CONV_WIDTH, d), F32), jax.ShapeDtypeStruct((1, d), F32)],
        scratch_shapes=[pltpu.VMEM((HALO, d), F32), pltpu.VMEM((tm, d), F32), pltpu.VMEM((tm, d), F32)],
        input_output_aliases={6: 0},
        compiler_params=_seq(),
    )(dqkv, u, u, conv_w, conv_b, wqkv_b, du)


def _rg_bwd(d_ycat, u, hh, conv_w, conv_b, wa_b, ba, wx_b, bx, lam, du):
    s_len = u.shape[0]
    d = conv_w.shape[1]
    heads, hd, _ = wa_b.shape
    tm = _tile(s_len, 256)
    per = tm // HALO
    nt = s_len // tm

    def body(dy_ref, x_ref, xp_ref, z_ref, hh_ref, hp_ref, cw_ref, cb_ref, wa_ref, ba_ref, wx_ref, bx_ref, lam_ref, _,
             du_ref, gwa_ref, gwx_ref, gba_ref, gbx_ref, glam_ref, gcw_ref, gcb_ref, carry, gbuf, later, dxcs):
        i = pl.program_id(0)
        first = i == nt - 1

        @pl.when(i == 0)
        def _():
            carry[...] = jnp.zeros_like(carry)
            later[...] = jnp.zeros_like(later)
            gwa_ref[...] = jnp.zeros_like(gwa_ref)
            gwx_ref[...] = jnp.zeros_like(gwx_ref)
            gba_ref[...] = jnp.zeros_like(gba_ref)
            gbx_ref[...] = jnp.zeros_like(gbx_ref)
            glam_ref[...] = jnp.zeros_like(glam_ref)
            gcw_ref[...] = jnp.zeros_like(gcw_ref)
            gcb_ref[...] = jnp.zeros_like(gcb_ref)

        prev = jnp.where(first, 0.0, xp_ref[...])
        taps = _conv_taps(jnp.concatenate([prev, x_ref[...]], axis=0))
        xc = _conv_fwd(taps, cw_ref, cb_ref)
        r, ig, sp, log_a, a, mult = _rg_gates(xc, wa_ref, ba_ref, wx_ref, bx_ref, lam_ref)
        z = z_ref[...]
        sgz = _sigmoid(z)
        dy = dy_ref[0]
        hh_v = hh_ref[...]
        du_ref[1] = _bf(dy * hh_v * _dsilu(z, sgz))
        dhh = dy * (z * sgz)
        rows = lax.broadcasted_iota(jnp.int32, a.shape, 0)
        coef = jnp.where(rows == tm - 1, carry[1:2, :], _shift_up(a, 1))
        ca, cu = _scan_groups(coef, dhh, reverse=True)
        c = carry[0:1, :]
        for j in range(per - 1, -1, -1):
            blk = ca[j * 8:(j + 1) * 8] * c + cu[j * 8:(j + 1) * 8]
            gbuf[j * 8:(j + 1) * 8, :] = blk
            c = blk[0:1]
        carry[0:1, :] = c
        carry[1:2, :] = a[0:1]
        g = gbuf[...]
        hprev_tile = jnp.where(first, 0.0, hp_ref[...])
        hprev = _shift_down(jnp.concatenate([hprev_tile, hh_v], axis=0), 1)[HALO:]
        da = g * hprev
        gx_ = g * xc
        d_mult = gx_ * ig
        d_ig = gx_ * mult
        dxc = g * mult * ig
        dlog_a = da * a - d_mult * (a * a / mult)
        d_r = dlog_a * ((-RG_C) * sp)
        glam_ref[...] += _colsum(dlog_a * ((-RG_C) * r)) * (-_sigmoid(-lam_ref[...]))
        d_ga = d_r * r * (1.0 - r)
        d_gx = d_ig * ig * (1.0 - ig)
        gba_ref[...] += _colsum(d_ga)
        gbx_ref[...] += _colsum(d_gx)
        xb = _bf(xc)
        dgab = _bf(d_ga)
        dgxb = _bf(d_gx)
        for h in range(heads):
            hs = slice(h * hd, (h + 1) * hd)
            dxcs[:, hs] = dxc[:, hs] + _dot_nt(dgab[:, hs], wa_ref[h]) + _dot_nt(dgxb[:, hs], wx_ref[h])
            gwa_ref[h] += _dot_tn(xb[:, hs], dgab[:, hs])
            gwx_ref[h] += _dot_tn(xb[:, hs], dgxb[:, hs])
        du_ref[0] = _bf(_conv_bwd_tile(dxcs[...], later, taps, cw_ref, gcw_ref, gcb_ref))

    assert DU_PLANE[0] % 2 == 0 and DU_PLANE[1] == DU_PLANE[0] + 1
    rev = lambda i: nt - 1 - i
    row = pl.BlockSpec((tm, d), lambda i: (rev(i), 0))
    halo_prev = lambda col: pl.BlockSpec((HALO, d), lambda i: (jnp.maximum(rev(i) * per - 1, 0), col))
    vec = pl.BlockSpec((1, d), lambda i: (0, 0))
    cwb = pl.BlockSpec((CONV_WIDTH, d), lambda i: (0, 0))
    whole3 = lambda a: pl.BlockSpec(a.shape, lambda i: (0, 0, 0))
    return _pcall(
        body, name="rg_bwd", grid=(nt,),
        in_specs=[pl.BlockSpec((1, tm, d), lambda i: (0, rev(i), 0)), row, halo_prev(0),
                  pl.BlockSpec((tm, d), lambda i: (rev(i), 1)), row, halo_prev(0),
                  cwb, vec, whole3(wa_b), vec, whole3(wx_b), vec, vec, pl.BlockSpec(memory_space=pl.ANY)],
        out_specs=[pl.BlockSpec((2, tm, d), lambda i: (DU_PLANE[0] // 2, rev(i), 0)), whole3(wa_b), whole3(wa_b),
                   vec, vec, vec, cwb, vec],
        out_shape=[jax.ShapeDtypeStruct(du.shape, BF16), jax.ShapeDtypeStruct(wa_b.shape, F32),
                   jax.ShapeDtypeStruct(wa_b.shape, F32)] + [jax.ShapeDtypeStruct((1, d), F32)] * 3
        + [jax.ShapeDtypeStruct((CONV_WIDTH, d), F32), jax.ShapeDtypeStruct((1, d), F32)],
        scratch_shapes=[pltpu.VMEM((8, d), F32), pltpu.VMEM((tm, d), F32), pltpu.VMEM((HALO, d), F32),
                        pltpu.VMEM((tm, d), F32)],
        input_output_aliases={13: 0},
        compiler_params=_seq(),
    )(d_ycat, u, u, u, hh, hh, conv_w, conv_b, wa_b, ba, wx_b, bx, lam, du)


def _in_bwd(du, w4, x, dxn, g, scale, ride=None):
    s_len, d = x.shape
    tm = _tile(s_len, 512)
    nsh_chips, _, nsh = w4.shape
    npc = du.shape[0]
    ck = d // 4
    assert nsh % ck == 0 and npc * d == nsh_chips * nsh

    def body(du_ref, w_ref, x_ref, dxn_ref, g_ref, sc_ref, dx_ref, dsh_ref, dsc_ref, dg_ref):
        @pl.when(pl.program_id(0) == 0)
        def _():
            dsh_ref[...] = jnp.zeros_like(dsh_ref)
            dsc_ref[...] = jnp.zeros_like(dsc_ref)
            dg_ref[...] = jnp.zeros_like(dg_ref)

        dh = None
        for q in range(npc * d // ck):
            col = q * ck
            p, pc = col // d, col % d
            s, sc = col // nsh, col % nsh
            t = _dot_nt(du_ref[DU_PLANE[p], :, pc:pc + ck], w_ref[s, :, sc:sc + ck])
            dh = t if dh is None else dh + t
        xv = x_ref[...]
        r = lax.rsqrt(jnp.mean(xv * xv, axis=-1, keepdims=True) + EPS)
        xn = xv * r
        gv = g_ref[...]
        onesc = 1.0 + sc_ref[...]
        dsh_ref[...] += _colsum(dh)
        dsc_ref[...] += _colsum(dh * (xn * gv))
        dg_ref[...] += _colsum(dh * xn * onesc)
        dxh = dh * (gv * onesc)
        dx_ref[...] = dxn_ref[...] + r * (dxh - xn * jnp.mean(dxh * xn, axis=-1, keepdims=True))

    row = pl.BlockSpec((tm, d), lambda i: (i, 0))
    vec = pl.BlockSpec((1, d), lambda i: (0, 0))
    return _pcall_ride(
        body, ride, name="in_bwd", grid=(s_len // tm,),
        in_specs=[pl.BlockSpec((npc, tm, d), lambda i: (0, i, 0)), pl.BlockSpec(w4.shape, lambda i: (0, 0, 0)), row, row,
                  vec, vec],
        out_specs=[row, vec, vec, vec],
        out_shape=[jax.ShapeDtypeStruct((s_len, d), F32)] + [jax.ShapeDtypeStruct((1, d), F32)] * 3,
        compiler_params=_seq(),
        args=(du, w4, x, dxn, g, scale))


def _layer_fwd(x, p, rides=None, late=None):
    rides = rides or {}
    (h_b, u), got = _ln_inproj(x, p["norm_g"], p["scale"], p["shift"], p["w4"], rides.get("ln_inproj"))
    if late is not None:
        p = {**p, **late(got)}
    (hh, ycat), got_a = _rg_fwd(u, p["rg_conv_w"], p["rg_conv_b"], p["rg_wa_b"], p["rg_ba"], p["rg_wx_b"], p["rg_bx"],
                                p["rg_lam"], rides.get("rg_fwd"))
    qkv, *gates = _ml_pre(u, p["ml_conv_w"], p["ml_conv_b"], p["wqkv_b"], p["wif_b"], p["wift_b"], p["b_if"],
                          p["b_ift"])
    (cell, ycat, cst, nst, mst), got_b = _mlstm_fwd(qkv, gates, u, p["ml_g"], ycat, rides.get("mlstm_fwd"))
    y, x_new = _out_proj(ycat, p["w_out_b"], x, p["gate"])
    saved = dict(x=x, h_b=h_b, u=u, hh=hh, qkv=qkv, gates=gates, cell=cell, ycat=ycat, cst=cst, nst=nst, mst=mst, y=y)
    return x_new, saved, p, dict(rg_fwd=got_a, mlstm_fwd=got_b)


def _layer_bwd(dxn, p, s, rides=None):
    rides = rides or {}
    u = s["u"]
    d = dxn.shape[1]
    d_gate, dy_b, d_ycat = _out_bwd(dxn, s["y"], p["gate"], p["w_out_b"])
    g_w_out = _grad_matmul(s["ycat"], dy_b[None], 2, lambda b: b, lambda b: 0, (2 * d, d), (d, d), lambda b: (b, 0))[0]
    (dqkv, dgt, g_b_if, du, g_ml_g), got = _mlstm_bwd(s["qkv"], s["gates"], s["cst"], s["nst"], s["mst"], s["cell"], u,
                                                      p["ml_g"], d_ycat, p["wif_b"], rides.get("mlstm_bwd"))
    ng = dgt.shape[1]
    g_w_if = _grad_matmul(s["qkv"], _bf(dgt)[None], 3, lambda b: b, lambda b: 0, (3 * d, ng), (d, ng),
                          lambda b: (b, 0))[0][0]
    du, g_wqkv, g_ml_cw, g_ml_cb = _ml_pre_bwd(dqkv, u, p["ml_conv_w"], p["ml_conv_b"], p["wqkv_b"], du)
    du, g_wa, g_wx, g_ba, g_bx, g_lam, g_rg_cw, g_rg_cb = _rg_bwd(d_ycat, u, s["hh"], p["rg_conv_w"], p["rg_conv_b"],
                                                                  p["rg_wa_b"], p["rg_ba"], p["rg_wx_b"], p["rg_bx"],
                                                                  p["rg_lam"], du)
    grads = dict(rg_conv_w=g_rg_cw, rg_conv_b=g_rg_cb, rg_w_a=g_wa, rg_b_a=g_ba, rg_w_x=g_wx, rg_b_x=g_bx,
                 rg_lambda=g_lam, ml_conv_w=g_ml_cw, ml_conv_b=g_ml_cb, ml_w_qkv=g_wqkv, ml_w_if=g_w_if, ml_b_if=g_b_if,
                 ml_norm_g=g_ml_g, w_out=g_w_out)
    npc = du.shape[0]
    gm_ride = rides["grad_w_in"](grads) if "grad_w_in" in rides else None
    grads["w_in"], got_gm = _grad_matmul(s["h_b"][None], du, npc, lambda b: 0, lambda b: (b + DU_PLANE[0]) % npc,
                                         (d, npc * d), (d, d), lambda b: (0, b), gm_ride)
    in_ride = rides["in_bwd"](grads["w_in"]) if "in_bwd" in rides else None
    (dx, d_shift, d_scale, grads["norm_g"]), got_in = _in_bwd(du, p["w4"], s["x"], dxn, p["norm_g"], p["scale"], in_ride)
    return (dx, grads, jnp.concatenate([d_shift, d_scale, d_gate], axis=1),
            dict(mlstm_bwd=got, grad_w_in=got_gm, in_bwd=got_in))


def _trunk_fwd_bwd(x, target, final_g, layers):
    saved = []
    for p in layers:
        x, s, _, _ = _layer_fwd(x, p)
        saved.append(s)
    dx, g_final, loss = _final_loss(x, final_g, target)
    grads, dmods = [], []
    for layer in reversed(range(len(layers))):
        dx, g, dm, _ = _layer_bwd(dx, layers[layer], saved[layer])
        grads.append(g)
        dmods.append(dm)
    return loss, dx, g_final, grads[::-1], dmods[::-1]


def _me():
    return lax.axis_index("x"), lax.axis_index("y"), lax.axis_index("c")


def _remote(src, dst, send_sem, recv_sem, to):
    return pltpu.make_async_remote_copy(src_ref=src, dst_ref=dst, send_sem=send_sem, recv_sem=recv_sem,
                                        device_id=to, device_id_type=MESH)


def _all_gather8(blocks, space):
    n = len(blocks)

    def body(*refs):
        x_refs, out_refs = refs[:n], refs[n:2 * n]
        send_sems, recv_sems, local_sems = refs[2 * n:]
        x, y, c = _me()
        me, sibling = (x, y, c), (x, y, 1 - c)
        chips = [(1 - x, y), (x, 1 - y), (1 - x, 1 - y)]

        def rows(i, px, py, pc):
            m_per = blocks[i].shape[0]
            return out_refs[i].at[pl.ds((4 * px + 2 * py + pc) * m_per, m_per), :]

        def copy(i, k, blk, to, src=None):
            return _remote(rows(i, *blk) if src is None else src, rows(i, *blk), send_sems.at[7 * i + k],
                           recv_sems.at[7 * i + k], to)

        mine = [pltpu.make_async_copy(x_refs[i], rows(i, *me), local_sems.at[i]) for i in range(n)]
        first = []
        for i in range(n):
            first.append(copy(i, 0, me, sibling, src=x_refs[i]))
            first += [copy(i, 1 + j, me, (*chip, c), src=x_refs[i]) for j, chip in enumerate(chips)]
        for cp in mine + first:
            cp.start()
        passed = []
        for j, chip in enumerate(chips):
            for i in range(n):
                copy(i, 1 + j, (*chip, c), me).wait_recv()
                passed.append(copy(i, 4 + j, (*chip, c), sibling))
                passed[-1].start()
        for i in range(n):
            copy(i, 0, sibling, me).wait_recv()
            for j, chip in enumerate(chips):
                copy(i, 4 + j, (*chip, 1 - c), me).wait_recv()
        for cp in first + passed:
            cp.wait_send()
        for cp in mine:
            cp.wait()

    spec = pl.BlockSpec(memory_space=space)
    return _pcall(
        body, name="all_gather8",
        out_shape=[jax.ShapeDtypeStruct((8 * b.shape[0], b.shape[1]), b.dtype) for b in blocks],
        in_specs=[spec] * n, out_specs=[spec] * n,
        scratch_shapes=[pltpu.SemaphoreType.DMA((7 * n,)), pltpu.SemaphoreType.DMA((7 * n,)),
                        pltpu.SemaphoreType.DMA((n,))],
    )(*blocks)


def _sib_halves(arrs, kinds):
    na = len(arrs)
    ncopies = sum(4 if k == "w_in" else 1 for k in kinds)

    def out_shape(a, kind):
        if kind == "w_in":
            return (a.shape[0], 4, a.shape[1] // 2, a.shape[2] // 4)
        return a.shape[:2] + a.shape[3:] if kind == "w_out" else a.shape[1:]

    def body(*refs):
        src, dst = refs[:na], refs[na:2 * na]
        send_sems, recv_sems = refs[2 * na:]
        x, y, c = _me()
        o = 1 - c
        pairs = []
        for i, kind in enumerate(kinds):
            depth = arrs[i].shape[0]
            if kind == "w_in":
                half, n = arrs[i].shape[1] // 2, arrs[i].shape[2] // 4
                pairs += [(src[i].at[pl.ds(0, depth), pl.ds(o * half, half), pl.ds(s * n, n)],
                           dst[i].at[pl.ds(0, depth), s]) for s in range(4)]
            elif kind == "w_out":
                pairs.append((src[i].at[pl.ds(0, depth), pl.ds(0, 4), o], dst[i]))
            else:
                pairs.append((src[i].at[o], dst[i]))
        copies = [_remote(s_, d_, send_sems.at[k], recv_sems.at[k], (x, y, o)) for k, (s_, d_) in enumerate(pairs)]
        for cp in copies:
            cp.start()
        for cp in copies:
            cp.wait_recv()
        for cp in copies:
            cp.wait_send()

    hbm = pl.BlockSpec(memory_space=pltpu.HBM)
    return _pcall(
        body, name="sib_halves",
        out_shape=[jax.ShapeDtypeStruct(out_shape(a, k), a.dtype) for a, k in zip(arrs, kinds)],
        in_specs=[hbm] * na, out_specs=[hbm] * na,
        scratch_shapes=[pltpu.SemaphoreType.DMA((ncopies,)), pltpu.SemaphoreType.DMA((ncopies,))],
    )(*arrs)


def _sib_fill(boths):
    n = len(boths)

    def body(*refs):
        dst = refs[n:2 * n]
        send_sems, recv_sems = refs[2 * n:]
        x, y, c = _me()
        view = lambda i: dst[i].at[pl.ds(0, boths[i].shape[0]), c]
        copies = [_remote(view(i), view(i), send_sems.at[i], recv_sems.at[i], (x, y, 1 - c)) for i in range(n)]
        for cp in copies:
            cp.start()
        for cp in copies:
            cp.wait_recv()
        for cp in copies:
            cp.wait_send()

    hbm = pl.BlockSpec(memory_space=pltpu.HBM)
    return _pcall(
        body, name="sib_fill",
        out_shape=[jax.ShapeDtypeStruct(b.shape, b.dtype) for b in boths],
        in_specs=[hbm] * n, out_specs=[hbm] * n, input_output_aliases={i: i for i in range(n)},
        scratch_shapes=[pltpu.SemaphoreType.DMA((n,)), pltpu.SemaphoreType.DMA((n,))],
    )(*boths)


def _chip_exchange(arrs):
    n = len(arrs)

    def body(*refs):
        src, dst = refs[:n], refs[n:2 * n]
        send_sems, recv_sems = refs[2 * n:]
        x, y, c = _me()
        me_s = 2 * x + y
        chips = [(1 - x, y), (x, 1 - y), (1 - x, 1 - y)]
        copies = [_remote(src[i].at[2 * px + py], dst[i].at[me_s], send_sems.at[3 * i + k], recv_sems.at[3 * i + k],
                          (px, py, c))
                  for i in range(n) for k, (px, py) in enumerate(chips)]
        for cp in copies:
            cp.start()
        for cp in copies:
            cp.wait_recv()
        for cp in copies:
            cp.wait_send()

    hbm = pl.BlockSpec(memory_space=pltpu.HBM)
    return _pcall(
        body, name="chip_exchange",
        out_shape=[jax.ShapeDtypeStruct(a.shape, a.dtype) for a in arrs],
        in_specs=[hbm] * n, out_specs=[hbm] * n,
        scratch_shapes=[pltpu.SemaphoreType.DMA((3 * n,)), pltpu.SemaphoreType.DMA((3 * n,))],
    )(*arrs)


def _row_tile(rows, cap=4096, mult=16):
    best = None
    for t in range(mult, min(rows, cap) + 1, mult):
        if rows % t == 0:
            best = t
    return rows if best is None else best


def _pair_sum(half, own, own_spec, got, got_spec, out_shape, out_spec, grid):
    def body(_, a_ref, b_ref, o_ref):
        o_ref[...] = (a_ref[...] + b_ref[...].astype(F32)).astype(o_ref.dtype)

    return _pcall(
        body, name="pair_sum",
        grid_spec=pltpu.PrefetchScalarGridSpec(num_scalar_prefetch=1, grid=grid, in_specs=[own_spec, got_spec],
                                               out_specs=out_spec),
        out_shape=out_shape, compiler_params=_seq(len(grid)))(half, own, got)


def _chip_sum(ids, part, met, fill, layer=0, stack=1):
    _, _, rows, n = part.shape
    tr = _row_tile(rows, cap=max(16, (1 << 18) // n))
    first = isinstance(stack, int)

    def body(_, own_ref, a_ref, b_ref, c_ref, *rest):
        acc = own_ref[...].astype(F32) + a_ref[...].astype(F32)
        acc = acc + b_ref[...].astype(F32)
        rest[-1][...] = acc + c_ref[...].astype(F32)

    blk = (None, None, tr, n)
    other = lambda k: pl.BlockSpec(blk, lambda j, ids: ((ids[0] + k) % 4, 0, j, 0))
    in_specs = [pl.BlockSpec(blk, lambda j, ids: (ids[0], 0, j, 0)), other(1), other(2), other(3)]
    return _pcall(
        body, name="chip_sum",
        grid_spec=pltpu.PrefetchScalarGridSpec(
            num_scalar_prefetch=1, grid=(rows // tr,),
            in_specs=in_specs if first else in_specs + [pl.BlockSpec(memory_space=pl.ANY)],
            out_specs=pl.BlockSpec(blk, lambda j, ids: (layer, ids[1] if fill else 0, j, 0))),
        out_shape=jax.ShapeDtypeStruct(((stack,) if first else stack.shape[:1]) + (2 if fill else 1, rows, n), F32),
        input_output_aliases={} if first else {5: 0},
        compiler_params=_seq())(*((ids, part, met, met, met) if first else (ids, part, met, met, met, stack)))


def _ada_mod(c_all, w_ada, b_ada_cols):
    depth, d, n = w_ada.shape
    nb = c_all.shape[0]

    def body(c_ref, w_ref, b_ref, o_ref):
        cv = c_ref[...]
        ca = _bf(cv * _sigmoid(cv))
        o_ref[0] = _dot(ca, _bf(w_ref[0])) + b_ref[0]

    return _pcall(body, name="ada_mod", grid=(depth,),
                  in_specs=[pl.BlockSpec((nb, d), lambda l: (0, 0)), pl.BlockSpec((1, d, n), lambda l: (l, 0, 0)),
                            pl.BlockSpec((1, 1, n), lambda l: (l, 0, 0))],
                  out_specs=pl.BlockSpec((1, nb, n), lambda l: (l, 0, 0)),
                  out_shape=jax.ShapeDtypeStruct((depth, nb, n), F32), compiler_params=_seq())(c_all, w_ada, b_ada_cols)


def _ada_grad(c_all, dmod_cols, rows_all):
    nb, d = c_all.shape
    depth, _, n = dmod_cols.shape
    kinds, n_all = rows_all.shape[1], rows_all.shape[3]

    def body(c_ref, dm_ref, da_ref, gw_ref, gb_ref):
        cv = c_ref[...]
        ca = _bf(cv * _sigmoid(cv))
        gw_ref[0] = _dot_tn(ca, _bf(dm_ref[0]))
        for k in range(kinds):
            gb_ref[0, k] = _colsum(da_ref[0, k])

    return _pcall(body, name="ada_grad", grid=(depth,),
                  in_specs=[pl.BlockSpec((nb, d), lambda l: (0, 0)), pl.BlockSpec((1, nb, n), lambda l: (l, 0, 0)),
                            pl.BlockSpec((1, kinds, nb, n_all), lambda l: (l, 0, 0, 0))],
                  out_specs=[pl.BlockSpec((1, d, n), lambda l: (l, 0, 0)),
                             pl.BlockSpec((1, kinds, 1, n_all), lambda l: (l, 0, 0, 0))],
                  out_shape=[jax.ShapeDtypeStruct((depth, d, n), F32), jax.ShapeDtypeStruct((depth, kinds, 1, n_all), F32)],
                  compiler_params=_seq())(c_all, dmod_cols, rows_all)


def _adamw(items, ride=None):
    two_d = [tuple(t.reshape(w.size // w.shape[-1], w.shape[-1]) for t in (w, g, m, v)) for w, g, m, v in items]
    n = len(items)
    if n == 1:
        rows, cols = two_d[0][0].shape
        tr = _row_tile(rows, cap=max(8, (1 << 18) // cols), mult=8)
        blocks = [pl.BlockSpec((tr, cols), lambda i: (i, 0))]
        grid = (rows // tr,)
    else:
        blocks = [pl.BlockSpec(t[0].shape, lambda i: (0, 0)) for t in two_d]
        grid = (1,)

    def body(*refs):
        for k in range(n):
            w_ref, g_ref, m_ref, v_ref = refs[4 * k:4 * k + 4]
            d_ref, mo_ref, vo_ref = refs[4 * n + 3 * k:4 * n + 3 * k + 3]
            gv = g_ref[...]
            mn = ADAM_B1 * m_ref[...] + (1.0 - ADAM_B1) * gv
            vn = ADAM_B2 * v_ref[...] + (1.0 - ADAM_B2) * (gv * gv)
            m_hat = mn / (1.0 - ADAM_B1 ** ADAM_STEP)
            v_hat = vn / (1.0 - ADAM_B2 ** ADAM_STEP)
            d_ref[...] = -ADAM_LR * (m_hat / (jnp.sqrt(v_hat) + ADAM_EPS) + ADAM_WD * w_ref[...])
            mo_ref[...] = mn
            vo_ref[...] = vn

    outs, got = _pcall_ride(
        body, ride, name="adamw", grid=grid,
        in_specs=[b for b in blocks for _ in range(4)], out_specs=[b for b in blocks for _ in range(3)],
        out_shape=[jax.ShapeDtypeStruct(t[0].shape, F32) for t in two_d for _ in range(3)],
        compiler_params=_seq(), args=tuple(a for t in two_d for a in t))
    return [tuple(o.reshape(items[k][0].shape) for o in outs[3 * k:3 * k + 3]) for k in range(n)], got


WEIGHTS = ["norm_g", "w_ada", "b_ada", "w_in", "rg_conv_w", "rg_conv_b", "rg_w_a", "rg_b_a", "rg_w_x", "rg_b_x",
           "rg_lambda", "ml_conv_w", "ml_conv_b", "ml_w_q", "ml_w_k", "ml_w_v", "ml_w_if", "ml_b_if", "ml_norm_g",
           "w_out", "final_g"]
SMALL_SHARDED = {"ml_w_qkv": 2, "rg_conv_w": 1, "ml_conv_w": 1, "ml_w_if": 0}
REPLICATED = ["rg_w_a", "rg_w_x", "rg_conv_b", "rg_b_a", "rg_b_x", "rg_lambda", "ml_conv_b", "ml_norm_g", "ml_b_if"]
LANES = 128


def _to_pieces(g, axis):
    shp = g.shape
    g = g.reshape(shp[:axis] + (4, 2, shp[axis] // 8) + shp[axis + 1:])
    g = jnp.moveaxis(g, (axis, axis + 1), (0, 1))
    return g.reshape(4, 2, -1)


def _from_pieces(p, shard_shape, axis):
    k = p.shape[0]
    rest = shard_shape[:axis] + (shard_shape[axis] // k,) + shard_shape[axis + 1:]
    t = jnp.moveaxis(p.reshape((k,) + rest), 0, axis)
    return t.reshape(shard_shape)


def _pad_rows(flat, mult):
    n = flat.shape[-1]
    pad = (-n) % mult
    if pad:
        flat = jnp.concatenate([flat, jnp.zeros(flat.shape[:-1] + (pad,), flat.dtype)], axis=-1)
    return flat


def kernel(x, c, norm_g, w_ada, b_ada, w_in, rg_conv_w, rg_conv_b, rg_w_a, rg_b_a, rg_w_x, rg_b_x, rg_lambda, ml_conv_w, ml_conv_b, ml_w_q, ml_w_k, ml_w_v, ml_w_if, ml_b_if, ml_norm_g, w_out, final_g, loss_target, m_norm_g, m_w_ada, m_b_ada, m_w_in, m_rg_conv_w, m_rg_conv_b, m_rg_w_a, m_rg_b_a, m_rg_w_x, m_rg_b_x, m_rg_lambda, m_ml_conv_w, m_ml_conv_b, m_ml_w_q, m_ml_w_k, m_ml_w_v, m_ml_w_if, m_ml_b_if, m_ml_norm_g, m_w_out, m_final_g, v_norm_g, v_w_ada, v_b_ada, v_w_in, v_rg_conv_w, v_rg_conv_b, v_rg_w_a, v_rg_b_a, v_rg_w_x, v_rg_b_x, v_rg_lambda, v_ml_conv_w, v_ml_conv_b, v_ml_w_q, v_ml_w_k, v_ml_w_v, v_ml_w_if, v_ml_b_if, v_ml_norm_g, v_w_out, v_final_g):
    given = dict(locals())
    ax, ay, ac = lax.axis_index("x"), lax.axis_index("y"), lax.axis_index("c")
    chip = 2 * ax + ay
    me = 2 * chip + ac
    depth, d = norm_g.shape
    n_ada = w_ada.shape[2]
    pick = lambda a, i, axis=0: lax.dynamic_index_in_dim(a, i, axis, keepdims=False)

    convs = jnp.stack([rg_conv_w, ml_conv_w])
    n_conv = 2 * depth * CONV_WIDTH // 4
    blk = jnp.concatenate([c, convs.reshape(n_conv, d), jnp.zeros((8 - 1 - n_conv, d), F32)], axis=0)
    g0 = _all_gather8([blk], pltpu.VMEM)[0].reshape(8, 8, d)
    c_all = g0[:, 0, :]
    conv_full = g0[0::2, 1:1 + n_conv].reshape(4, 2, depth, CONV_WIDTH, d // 4)
    conv_full = conv_full.transpose(1, 2, 3, 0, 4).reshape(2, depth, CONV_WIDTH, d)

    b_cols = lax.dynamic_slice_in_dim(b_ada, chip * n_ada, n_ada, axis=1)[:, None, :]
    mod_part = _ada_mod(c_all, w_ada, b_cols)
    g1 = _all_gather8([mod_part.transpose(1, 0, 2).reshape(8, depth * n_ada)], pltpu.VMEM)[0]
    g1 = g1.reshape(8, 8, depth, n_ada)[0::2]
    mod_me = pick(g1.transpose(1, 2, 0, 3).reshape(8, depth, 4 * n_ada), me)

    def half_of(w, axis):
        n = w.shape[axis] // 2
        return lax.dynamic_slice_in_dim(w, ac * n, n, axis).astype(BF16)

    n_sh = w_in.shape[2]
    heads, hd_cut, hd = ml_w_q.shape[1:]

    def blocks_of(l):
        wqkv = jnp.stack([ml_w_q[l], ml_w_k[l], ml_w_v[l]])
        return [half_of(w_in[l], 0), half_of(w_out[l], 0), half_of(wqkv, 2).reshape(-1, hd), half_of(ml_w_if[l], 0)]

    def layer_of(l, w4, rest):
        return dict(
            norm_g=norm_g[l][None], shift=mod_me[l, 0:d][None], scale=mod_me[l, d:2 * d][None],
            gate=mod_me[l, 2 * d:3 * d][None], w4=w4.reshape(4, d, n_sh),
            rg_conv_w=conv_full[0, l], rg_conv_b=rg_conv_b[l][None], rg_wa_b=_bf(rg_w_a[l]), rg_ba=rg_b_a[l][None],
            rg_wx_b=_bf(rg_w_x[l]), rg_bx=rg_b_x[l][None], rg_lam=rg_lambda[l][None],
            ml_conv_w=conv_full[1, l], ml_conv_b=ml_conv_b[l][None], b_if=ml_b_if[l][None], b_ift=ml_b_if[l][:, None],
            ml_g=ml_norm_g[l][None], **rest)

    def rest_of(gathered):
        w_out_b, wqkv_g, wif = gathered
        return dict(w_out_b=w_out_b, wqkv_b=_from_pieces(wqkv_g.reshape(8, -1), (3, heads, hd, hd), 2), wif_b=wif,
                    wift_b=wif.T)

    landing = lambda b: jax.ShapeDtypeStruct((4, 2) + b.shape, b.dtype)
    whole = lambda landed: [t.reshape(-1, t.shape[-1]) for t in _sib_fill(landed)]
    first = blocks_of(0)
    p = layer_of(0, _all_gather8(first[:1], pltpu.HBM)[0], {})
    rides = dict(ln_inproj=Ride(first[1:], [landing(b) for b in first[1:]], False))
    late = lambda landed: rest_of(whole(landed))
    layers, saved = [], []
    xl = x[0]
    for l in range(depth):
        if l + 1 < depth:
            nxt = blocks_of(l + 1)
            rides["rg_fwd"] = Ride(nxt[:1], [landing(nxt[0])], False)
            rides["mlstm_fwd"] = Ride(nxt[1:], [landing(b) for b in nxt[1:]], False)
        xl, s, p, got = _layer_fwd(xl, p, rides, late)
        layers.append(p)
        saved.append(s)
        if l + 1 < depth:
            nxt_whole = whole(list(got["rg_fwd"]) + list(got["mlstm_fwd"]))
            p = layer_of(l + 1, nxt_whole[0], rest_of(nxt_whole[1:]))
            rides, late = {}, None
    dx, g_final, loss = _final_loss(xl, final_g[None], loss_target[0])

    half = ac.reshape(1)
    ids = jnp.stack([chip, ac])
    r_out = w_out.shape[1] // 2

    def partial_in(g_w_in):
        (got_in,) = _sib_halves([g_w_in], ["w_in"])
        return _pair_sum(
            half, g_w_in, pl.BlockSpec((None, d // 2, n_sh), lambda s, h: (0, h[0], s)),
            got_in, pl.BlockSpec((None, None, d // 2, n_sh), lambda s, h: (0, s, 0, 0)),
            jax.ShapeDtypeStruct((4, 1, d // 2, n_sh), BF16),
            pl.BlockSpec((None, None, d // 2, n_sh), lambda s, h: (s, 0, 0, 0)), (4,))

    def pair_out(g_out5, got_out):
        return _pair_sum(
            half, g_out5, pl.BlockSpec((None, None, None, r_out, d), lambda s, h: (0, s, h[0], 0, 0)),
            got_out, pl.BlockSpec((None, None, r_out, d), lambda s, h: (0, s, 0, 0)),
            jax.ShapeDtypeStruct((4, 1, r_out, d), BF16),
            pl.BlockSpec((None, None, r_out, d), lambda s, h: (s, 0, 0, 0)), (4,))

    def pair_slab(slab, got, dtype):
        rows = got.shape[0] // 4
        blk = pl.BlockSpec((rows, LANES), lambda s, h: (s, 0))
        return _pair_sum(half, slab, pl.BlockSpec((None, rows, LANES), lambda s, h: (h[0], s, 0)), got, blk,
                         jax.ShapeDtypeStruct((4 * rows, LANES), dtype), blk, (4,)).reshape(4, 1, rows, LANES)

    row_pad = lambda n: -(-n // (8 * LANES)) * (8 * LANES)

    def as_rows(t):
        if t.shape[-1] == LANES and t.size % (8 * LANES) == 0:
            return t.reshape(-1, LANES)
        return _pad_rows(t.reshape(-1), 8 * LANES).reshape(-1, LANES)

    exchange = lambda parts_l: Ride(parts_l, [jax.ShapeDtypeStruct(t.shape, t.dtype) for t in parts_l], True)
    grads, dmods, parts, mets = [None] * depth, [None] * depth, [None] * depth, [None] * depth
    small = {}

    def early_exchange(first_grads):
        every = [first_grads] + grads[1:]
        sm = jnp.concatenate([_to_pieces(every[l][name], axis) for l in range(depth)
                              for name, axis in SMALL_SHARDED.items()], axis=-1)
        sm = _pad_rows(sm, 16 * LANES)
        sm = sm.transpose(1, 0, 2).reshape(2, -1, LANES)
        rep = [as_rows(every[l][name]) for l in range(depth) for name in REPLICATED]
        rep = jnp.concatenate(rep + [as_rows(g_final), as_rows(loss)], axis=0)
        rep = jnp.concatenate([rep, jnp.zeros(((-rep.shape[0]) % 64, LANES), F32)], axis=0)
        rep = rep.reshape(4, 2, -1, LANES).transpose(1, 0, 2, 3).reshape(2, -1, LANES)
        g_out5 = first_grads["w_out"].reshape(1, 4, 2, r_out, d)
        got_out, got_sm, got_rep = _sib_halves([g_out5, sm, rep], ["w_out", "slab", "slab"])
        small["parts"] = [pair_out(g_out5, got_out), pair_slab(sm, got_sm, BF16), pair_slab(rep, got_rep, F32)]
        return exchange(small["parts"])

    def last_exchange(g_w_in):
        small["part_in"] = partial_in(g_w_in)
        return exchange([small["part_in"]])

    rides = {}
    for l in reversed(range(depth)):
        if l == 0:
            rides.update(grad_w_in=early_exchange, in_bwd=last_exchange)
        dx, grads[l], dmods[l], got = _layer_bwd(dx, layers[l], saved[l], rides)
        if "mlstm_bwd" in rides:
            mets[l + 1] = got["mlstm_bwd"]
        if l > 0:
            g_out5 = grads[l]["w_out"].reshape(1, 4, 2, r_out, d)
            (got_out,) = _sib_halves([g_out5], ["w_out"])
            parts[l] = [partial_in(grads[l]["w_in"]), pair_out(g_out5, got_out)]
            rides = dict(mlstm_bwd=exchange(parts[l]))
    part_out, part_sm, part_rep = small["parts"]
    met_out, met_sm, met_rep = got["grad_w_in"]
    parts[0], mets[0] = [small["part_in"], part_out], [got["in_bwd"][0], met_out]
    n_rep = part_rep.shape[2]

    pad = lambda t: jnp.concatenate([t, jnp.zeros((1, 2 * d), F32)], axis=1)
    rows = [r for l in range(depth) for r in (dmods[l], pad(grads[l]["norm_g"]))]
    blk = jnp.concatenate(rows + [jnp.zeros((8 - 2 * depth, 3 * d), F32)], axis=0)
    rows_all = _all_gather8([blk], pltpu.VMEM)[0].reshape(8, 8, 3 * d)[:, :2 * depth]
    rows_all = rows_all.transpose(1, 0, 2).reshape(depth, 2, 8, 3 * d)
    dm_cols = lax.dynamic_slice_in_dim(rows_all[:, 0], chip * n_ada, n_ada, axis=2)
    g_w_ada, summed = _ada_grad(c_all, dm_cols, rows_all)

    g = dict(w_ada=g_w_ada, b_ada=summed[:, 0, 0], norm_g=summed[:, 1, 0, :d])
    item = lambda name: (given[name], g[name], given["m_" + name], given["v_" + name])
    both_in, both_out = depth, depth
    for l in range(depth):
        both_in = _chip_sum(ids, parts[l][0], mets[l][0], True, l, both_in)
        both_out = _chip_sum(ids, parts[l][1], mets[l][1], True, l, both_out)
    both_in, both_out, both_sm = _sib_fill([both_in, both_out, _chip_sum(ids, part_sm, met_sm, True)])
    red_rep = _chip_sum(ids, part_rep, met_rep, False).reshape(n_rep, LANES)
    rep_all = _all_gather8([red_rep], pltpu.VMEM)[0].reshape(-1)

    g.update(w_in=both_in.reshape(w_in.shape), w_out=both_out.reshape(w_out.shape))
    shard = both_sm.reshape(2, -1)
    off = 0
    per_layer = {name: [] for name in SMALL_SHARDED}
    for l in range(depth):
        for name, axis in SMALL_SHARDED.items():
            shp = (3,) + ml_w_q.shape[1:] if name == "ml_w_qkv" else given[name].shape[1:]
            n = grads[l][name].size // 8
            per_layer[name].append(_from_pieces(shard[:, off:off + n], shp, axis))
            off += n
    for name in SMALL_SHARDED:
        g[name] = jnp.stack(per_layer[name])
    for i, name in enumerate(["ml_w_q", "ml_w_k", "ml_w_v"]):
        g[name] = g["ml_w_qkv"][:, i]
    off = 0
    per_layer = {name: [] for name in REPLICATED}
    for l in range(depth):
        for name in REPLICATED:
            n = given[name][l].size
            per_layer[name].append(rep_all[off:off + n].reshape(given[name].shape[1:]))
            off += row_pad(n)
    for name in REPLICATED:
        g[name] = jnp.stack(per_layer[name])
    g["final_g"] = rep_all[off:off + d]
    loss_all = rep_all[off + row_pad(d)]

    stepped = {}
    rg_mats, ml_mats = ["rg_w_a", "rg_w_x"], ["ml_w_q", "ml_w_k", "ml_w_v"]
    vectors = [n for n in WEIGHTS if n not in ["w_ada", "w_in", "w_out"] + rg_mats + ml_mats]
    for names in (["w_ada"], ["w_in"], ["w_out"], rg_mats, ml_mats, vectors):
        stepped.update(zip(names, _adamw([item(name) for name in names])[0]))
    deltas, new_m, new_v = zip(*[stepped[name] for name in WEIGHTS])
    return (loss_all, dx[None], *[g[name] for name in WEIGHTS], *deltas, *new_m, *new_v)
```

```python
import functools
from typing import NamedTuple

import jax
import jax.numpy as jnp
from jax import lax
from jax.experimental import pallas as pl
from jax.experimental.pallas import tpu as pltpu

F32 = jnp.float32
BF16 = jnp.bfloat16

EPS = 1e-6
RG_C = 8.0
CONV_WIDTH = 4
ML_CHUNK = 512
HALO = 8
ADAM_LR = 0.001
ADAM_B1 = 0.9
ADAM_B2 = 0.999
ADAM_EPS = 1e-08
ADAM_WD = 0.01
ADAM_STEP = 10
MESH = pl.DeviceIdType.MESH


def _pcall(body, **kw):
    return pl.pallas_call(body, **kw)


class Leg(NamedTuple):
    src: jax.Array
    kind: str

    def landing(self):
        a = self.src
        shape = {"chips": lambda: a.shape, "spread": lambda: (4, 2) + a.shape,
                 "sib_w_in": lambda: (a.shape[0], 4, a.shape[1] // 2, a.shape[2] // 4),
                 "sib_w_out": lambda: a.shape[:2] + a.shape[3:], "sib_slab": lambda: a.shape[1:]}[self.kind]()
        return jax.ShapeDtypeStruct(shape, a.dtype)

    def copies(self, src, dst, x, y, c):
        a, me_s, o = self.src, 2 * x + y, 1 - c
        chips = [(1 - x, y), (x, 1 - y), (1 - x, 1 - y)]
        if self.kind == "chips":
            return [(src.at[2 * px + py], dst.at[me_s], (px, py, c)) for px, py in chips], []
        if self.kind == "spread":
            return [(src, dst.at[me_s, c], (px, py, c)) for px, py in chips], [(src, dst.at[me_s, c])]
        depth = pl.ds(0, a.shape[0])
        if self.kind == "sib_w_in":
            half, n = a.shape[1] // 2, a.shape[2] // 4
            return [(src.at[depth, pl.ds(o * half, half), pl.ds(s * n, n)], dst.at[depth, s], (x, y, o))
                    for s in range(4)], []
        if self.kind == "sib_w_out":
            return [(src.at[depth, pl.ds(0, 4), o], dst, (x, y, o))], []
        return [(src.at[o], dst, (x, y, o))], []

    def n_copies(self):
        return {"chips": 3, "spread": 3, "sib_w_in": 4}.get(self.kind, 1)


def _exchange_body(legs, srcs, dsts, send_sems, recv_sems, local_sems):
    x, y, c = _me()
    remote, local, k = [], [], 0
    for i, leg in enumerate(legs):
        far, near = leg.copies(srcs[i], dsts[i], x, y, c)
        for src, dst, to in far:
            remote.append(_remote(src, dst, send_sems.at[k], recv_sems.at[k], to))
            k += 1
        local += [pltpu.make_async_copy(src, dst, local_sems.at[i]) for src, dst in near]
    return remote, local


def _exchange_sems(legs):
    n = sum(leg.n_copies() for leg in legs)
    return [pltpu.SemaphoreType.DMA((n,)), pltpu.SemaphoreType.DMA((n,)), pltpu.SemaphoreType.DMA((len(legs),))]


def _pcall_ride(body, ride, *, grid, in_specs, out_specs, out_shape, args, scratch_shapes=(), **kw):
    n_in, n_out, n_scr = len(in_specs), len(out_specs), len(scratch_shapes)
    if not ride:
        res = _pcall(body, grid=grid, in_specs=in_specs, out_specs=out_specs, out_shape=out_shape,
                     scratch_shapes=list(scratch_shapes), **kw)(*args)
        return res, []
    nr = len(ride)

    def riding(*refs):
        ins, rsrc = refs[:n_in], refs[n_in:n_in + nr]
        outs, rdst = refs[n_in + nr:n_in + nr + n_out], refs[n_in + nr + n_out:n_in + 2 * nr + n_out]
        scr = refs[n_in + 2 * nr + n_out:n_in + 2 * nr + n_out + n_scr]
        copies, local = _exchange_body(ride, rsrc, rdst, *refs[n_in + 2 * nr + n_out + n_scr:])
        first = functools.reduce(jnp.logical_and, [pl.program_id(a) == 0 for a in range(len(grid))])
        last = functools.reduce(jnp.logical_and, [pl.program_id(a) == grid[a] - 1 for a in range(len(grid))])

        @pl.when(first)
        def _():
            for cp in copies + local:
                cp.start()

        body(*ins, *outs, *scr)

        @pl.when(last)
        def _():
            for cp in copies:
                cp.wait_recv()
            for cp in copies:
                cp.wait_send()
            for cp in local:
                cp.wait()

    hbm = pl.BlockSpec(memory_space=pltpu.HBM)
    res = _pcall(
        riding, grid=grid, in_specs=list(in_specs) + [hbm] * nr, out_specs=list(out_specs) + [hbm] * nr,
        out_shape=list(out_shape) + [leg.landing() for leg in ride],
        scratch_shapes=list(scratch_shapes) + _exchange_sems(ride), **kw)(*args, *[leg.src for leg in ride])
    return res[:n_out], res[n_out:]


def _seq(n=1):
    return pltpu.CompilerParams(dimension_semantics=("arbitrary",) * n)


def _dot(a, b):
    return jnp.dot(a, b, preferred_element_type=F32)


def _dot_nt(a, b):
    return lax.dot_general(a, b, (((1,), (1,)), ((), ())), preferred_element_type=F32)


def _dot_tn(a, b):
    return lax.dot_general(a, b, (((0,), (0,)), ((), ())), preferred_element_type=F32)


def _bf(x):
    return x.astype(BF16)


def _sigmoid(x):
    return 0.5 * jnp.tanh(0.5 * x) + 0.5


def _log1p(z):
    u = 1.0 + z
    return jnp.where(u == 1.0, z, jnp.log(u) * (z / jnp.where(u == 1.0, 1.0, u - 1.0)))


def _softplus(x):
    return jnp.maximum(x, 0.0) + _log1p(jnp.exp(-jnp.abs(x)))


def _log_sigmoid(x):
    return -_softplus(-x)


def _one_minus_sq(a, log_a):
    x = 2.0 * log_a
    small = -x * (1.0 + x * (0.5 + x * (1.0 / 6.0)))
    return jnp.where(x > -0.004, small, 1.0 - a * a)


def _dsilu(x, s):
    return s * (1.0 + x * (1.0 - s))


def _rowsum(x):
    return jnp.sum(x, axis=1, keepdims=True)


def _colsum(x):
    return jnp.sum(x, axis=0, keepdims=True)


def _shift_down(win, s):
    return win if s == 0 else pltpu.roll(win, s, 0)


def _shift_up(win, s):
    return win if s == 0 else pltpu.roll(win, win.shape[0] - s, 0)


def _conv_taps(win):
    return [_shift_down(win, CONV_WIDTH - 1 - k)[HALO:] for k in range(CONV_WIDTH)]


def _conv_fwd(taps, w_ref, b_ref):
    acc = b_ref[...] + w_ref[CONV_WIDTH - 1:CONV_WIDTH, :] * taps[CONV_WIDTH - 1]
    for k in range(CONV_WIDTH - 1):
        acc = acc + w_ref[k:k + 1, :] * taps[k]
    return acc


def _split3(x):
    hi = _bf(x)
    r1 = x - hi.astype(F32)
    mid = _bf(r1)
    lo = _bf(r1 - mid.astype(F32))
    return hi, mid, lo


def _tri_dot_left(tri, x):
    hi, mid, lo = _split3(x)
    return _dot(tri, hi) + _dot(tri, mid) + _dot(tri, lo)


def _tri_dot_right(x, tri):
    hi, mid, lo = _split3(x)
    return _dot(hi, tri) + _dot(mid, tri) + _dot(lo, tri)


def _tile(n, want):
    t = min(n, want)
    assert n % t == 0
    return t


def _ln_inproj(x, g, scale, shift, w4, ride=None):
    s_len, d = x.shape
    nj, _, nsh = w4.shape
    tm = _tile(s_len, 1024)

    def body(x_ref, g_ref, sc_ref, sh_ref, w_ref, h_ref, u_ref, hs):
        @pl.when(pl.program_id(1) == 0)
        def _():
            xv = x_ref[...]
            r = lax.rsqrt(jnp.mean(xv * xv, axis=-1, keepdims=True) + EPS)
            hv = (xv * r * g_ref[...]) * (1.0 + sc_ref[...]) + sh_ref[...]
            hs[...] = _bf(hv)
            h_ref[...] = hs[...]

        u_ref[...] = _dot(hs[...], w_ref[0])

    vec = pl.BlockSpec((1, d), lambda i, j: (0, 0))
    return _pcall_ride(
        body, ride, name="ln_inproj", grid=(s_len // tm, nj),
        in_specs=[pl.BlockSpec((tm, d), lambda i, j: (i, 0)), vec, vec, vec,
                  pl.BlockSpec((1, d, nsh), lambda i, j: (j, 0, 0))],
        out_specs=[pl.BlockSpec((tm, d), lambda i, j: (i, 0)), pl.BlockSpec((tm, nsh), lambda i, j: (i, j))],
        out_shape=[jax.ShapeDtypeStruct((s_len, d), BF16), jax.ShapeDtypeStruct((s_len, nj * nsh), F32)],
        scratch_shapes=[pltpu.VMEM((tm, d), BF16)],
        compiler_params=_seq(2),
        args=(x, g, scale, shift, w4))


def _rg_gates(xc, wa_ref, ba_ref, wx_ref, bx_ref, lam_ref):
    heads, hd, _ = wa_ref.shape
    xb = _bf(xc)
    ga = jnp.concatenate([_dot(xb[:, h * hd:(h + 1) * hd], wa_ref[h]) for h in range(heads)], axis=1) + ba_ref[...]
    gx = jnp.concatenate([_dot(xb[:, h * hd:(h + 1) * hd], wx_ref[h]) for h in range(heads)], axis=1) + bx_ref[...]
    r = _sigmoid(ga)
    ig = _sigmoid(gx)
    sp = _softplus(-lam_ref[...])
    log_a = (-RG_C) * r * sp
    a = jnp.exp(log_a)
    mult = jnp.sqrt(_one_minus_sq(a, log_a))
    return r, ig, sp, log_a, a, mult


def _scan_groups(a, u, reverse):
    n, c = a.shape
    a = a.reshape(n // 8, 8, c)
    u = u.reshape(n // 8, 8, c)
    row = lax.broadcasted_iota(jnp.int32, a.shape, 1)
    for k in (1, 2, 4):
        sft = 8 - k if reverse else k
        a_sh, u_sh = pltpu.roll(a, sft, 1), pltpu.roll(u, sft, 1)
        ok = row < 8 - k if reverse else row >= k
        u = jnp.where(ok, a * u_sh + u, u)
        a = jnp.where(ok, a * a_sh, a)
    return a.reshape(n, c), u.reshape(n, c)


def _rg_fwd(u, conv_w, conv_b, wa_b, ba, wx_b, bx, lam, ride=None):
    s_len = u.shape[0]
    d = conv_w.shape[1]
    tm = _tile(s_len, 256)
    per = tm // HALO

    def body(x_ref, xp_ref, z_ref, cw_ref, cb_ref, wa_ref, ba_ref, wx_ref, bx_ref, lam_ref,
             hh_ref, y_ref, carry):
        i = pl.program_id(0)

        @pl.when(i == 0)
        def _():
            carry[...] = jnp.zeros_like(carry)

        prev = jnp.where(i == 0, 0.0, xp_ref[...])
        xc = _conv_fwd(_conv_taps(jnp.concatenate([prev, x_ref[...]], axis=0)), cw_ref, cb_ref)
        _, ig, _, _, a, mult = _rg_gates(xc, wa_ref, ba_ref, wx_ref, bx_ref, lam_ref)
        ca, cu = _scan_groups(a, mult * (ig * xc), reverse=False)
        c = carry[0:1, :]
        for j in range(per):
            blk = ca[j * 8:(j + 1) * 8] * c + cu[j * 8:(j + 1) * 8]
            hh_ref[j * 8:(j + 1) * 8, :] = blk
            c = blk[7:8]
        carry[0:1, :] = c
        z = z_ref[...]
        y_ref[0] = _bf(hh_ref[...] * (z * _sigmoid(z)))

    vec = pl.BlockSpec((1, d), lambda i: (0, 0))
    whole3 = lambda a: pl.BlockSpec(a.shape, lambda i: (0, 0, 0))
    return _pcall_ride(
        body, ride, name="rg_fwd", grid=(s_len // tm,),
        in_specs=[pl.BlockSpec((tm, d), lambda i: (i, 0)),
                  pl.BlockSpec((HALO, d), lambda i: (jnp.maximum(i * per - 1, 0), 0)),
                  pl.BlockSpec((tm, d), lambda i: (i, 1)),
                  pl.BlockSpec((CONV_WIDTH, d), lambda i: (0, 0)), vec,
                  whole3(wa_b), vec, whole3(wx_b), vec, vec],
        out_specs=[pl.BlockSpec((tm, d), lambda i: (i, 0)), pl.BlockSpec((1, tm, d), lambda i: (0, i, 0))],
        out_shape=[jax.ShapeDtypeStruct((s_len, d), F32), jax.ShapeDtypeStruct((2, s_len, d), BF16)],
        scratch_shapes=[pltpu.VMEM((8, d), F32)],
        compiler_params=_seq(),
        args=(u, u, u, conv_w, conv_b, wa_b, ba, wx_b, bx, lam))


def _ml_pre(u, conv_w, conv_b, wqkv_b, wif_b, wift_b, b_if, b_ift):
    s_len = u.shape[0]
    d = conv_w.shape[1]
    _, heads, hd, _ = wqkv_b.shape
    ng = 2 * heads
    tm = _tile(s_len, max(256, ML_CHUNK))
    per = tm // HALO

    def body(x_ref, xp_ref, cw_ref, cb_ref, w_ref, wif_ref, wift_ref, bif_ref, bift_ref,
             qkv_ref, gt_ref, gtt_ref, bc_ref, bct_ref):
        i = pl.program_id(0)
        prev = jnp.where(i == 0, 0.0, xp_ref[...])
        xm = x_ref[...]
        pre = _conv_fwd(_conv_taps(jnp.concatenate([prev, xm], axis=0)), cw_ref, cb_ref)
        xcb = _bf(pre * _sigmoid(pre))
        xmb = _bf(xm)
        for h in range(heads):
            hs = slice(h * hd, (h + 1) * hd)
            qkv_ref[0, :, hs] = _bf(_dot(xcb[:, hs], w_ref[0, h]))
            qkv_ref[1, :, hs] = _bf(_dot(xcb[:, hs], w_ref[1, h]))
            qkv_ref[2, :, hs] = _bf(_dot(xmb[:, hs], w_ref[2, h]))
        qb, kb, vb = qkv_ref[0], qkv_ref[1], qkv_ref[2]
        gt = (_dot(qb, wif_ref[0:d, :]) + _dot(kb, wif_ref[d:2 * d, :]) + _dot(vb, wif_ref[2 * d:3 * d, :])
              + bif_ref[...])
        gtt = (_dot_nt(wift_ref[:, 0:d], qb) + _dot_nt(wift_ref[:, d:2 * d], kb)
               + _dot_nt(wift_ref[:, 2 * d:3 * d], vb) + bift_ref[...])
        gt_ref[...] = gt
        gtt_ref[...] = gtt
        r = lax.broadcasted_iota(jnp.int32, (tm, tm), 0)
        c = lax.broadcasted_iota(jnp.int32, (tm, tm), 1)
        same = (r // ML_CHUNK) == (c // ML_CHUNK)
        bc_ref[...] = _tri_dot_left(((r >= c) & same).astype(BF16), _log_sigmoid(gt))
        bct_ref[...] = _tri_dot_right(_log_sigmoid(gtt), ((r <= c) & same).astype(BF16))

    vec = pl.BlockSpec((1, d), lambda i: (0, 0))
    whole2 = lambda a: pl.BlockSpec(a.shape, lambda i: (0, 0))
    col = pl.BlockSpec((tm, ng), lambda i: (i, 0))
    row = pl.BlockSpec((ng, tm), lambda i: (0, i))
    return _pcall(
        body, name="ml_pre", grid=(s_len // tm,),
        in_specs=[pl.BlockSpec((tm, d), lambda i: (i, 2)),
                  pl.BlockSpec((HALO, d), lambda i: (jnp.maximum(i * per - 1, 0), 2)),
                  pl.BlockSpec((CONV_WIDTH, d), lambda i: (0, 0)), vec,
                  pl.BlockSpec(wqkv_b.shape, lambda i: (0, 0, 0, 0)), whole2(wif_b), whole2(wift_b), whole2(b_if),
                  whole2(b_ift)],
        out_specs=[pl.BlockSpec((3, tm, d), lambda i: (0, i, 0)), col, row, col, row],
        out_shape=[jax.ShapeDtypeStruct((3, s_len, d), BF16), jax.ShapeDtypeStruct((s_len, ng), F32),
                   jax.ShapeDtypeStruct((ng, s_len), F32), jax.ShapeDtypeStruct((s_len, ng), F32),
                   jax.ShapeDtypeStruct((ng, s_len), F32)],
        compiler_params=_seq(),
    )(u, u, conv_w, conv_b, wqkv_b, wif_b, wift_b, b_if, b_ift)


def _chunk_gates(gt, gtt, bc, bct, h, heads):
    li_c = gt[:, h:h + 1]
    li_r = gtt[h:h + 1, :]
    gf_c = gt[:, heads + h:heads + h + 1]
    b_c = bc[:, heads + h:heads + h + 1]
    b_r = bct[heads + h:heads + h + 1, :]
    return li_c, li_r, gf_c, b_c, b_r


def _chunk_weights(li_c, li_r, b_c, b_r, m_prev, causal):
    lc = b_c.shape[0]
    b_last = b_c[lc - 1:lc, :]
    dmat = jnp.where(causal, b_c - b_r + li_r, -jnp.inf)
    m_inter = b_c + m_prev
    m_t = jnp.maximum(m_inter, jnp.max(dmat, axis=1, keepdims=True))
    w_intra = jnp.exp(dmat - m_t)
    w_inter = jnp.exp(m_inter - m_t)
    g_c = b_last - b_c + li_c
    m_new = jnp.maximum(b_last + m_prev, jnp.max(g_c, axis=0, keepdims=True))
    w_state = jnp.exp(g_c - m_new)
    decay = jnp.exp(b_last + m_prev - m_new)
    return m_t, w_intra, w_inter, m_new, w_state, decay


def _tri_masks(lc):
    r = lax.broadcasted_iota(jnp.int32, (lc, lc), 0)
    c = lax.broadcasted_iota(jnp.int32, (lc, lc), 1)
    causal = r >= c
    return causal, causal.astype(BF16), (r <= c).astype(BF16)


def _mlstm_fwd(qkv, gates, u, ml_g, ycat, ride=None):
    _, s_len, d = qkv.shape
    ng = gates[0].shape[1]
    heads = ng // 2
    hd = d // heads
    lc = ML_CHUNK
    nc = s_len // lc
    kscale = hd ** -0.5

    def body(qkv_ref, gt_ref, gtt_ref, bc_ref, bct_ref, o_ref, z_ref, g_ref, _, cell_ref, y_ref, cst_ref, nst_ref,
             mst_ref, cs, ns, ms):
        @pl.when(pl.program_id(0) == 0)
        def _():
            cs[...] = jnp.zeros_like(cs)
            ns[...] = jnp.zeros_like(ns)
            ms[...] = jnp.zeros_like(ms)

        causal = _tri_masks(lc)[0]
        gtv, gttv, bcv, bctv = gt_ref[...], gtt_ref[...], bc_ref[...], bct_ref[...]
        old = [(cs[h], ns[h], ms[h]) for h in range(heads)]
        new, cells, ys = [], [], []
        for h in range(heads):
            hs = slice(h * hd, (h + 1) * hd)
            li_c, li_r, _, b_c, b_r = _chunk_gates(gtv, gttv, bcv, bctv, h, heads)
            c_old, n_old, m_old = old[h]
            m_prev = m_old[:, 0:1]
            m_t, w_intra, w_inter, m_new, w_state, decay = _chunk_weights(li_c, li_r, b_c, b_r, m_prev, causal)
            qb = qkv_ref[0, :, hs]
            ks = qkv_ref[1, :, hs].astype(F32) * kscale
            kb = _bf(ks)
            vb = qkv_ref[2, :, hs]
            s = _dot_nt(qb, kb) * w_intra
            num = _dot(_bf(s), vb) + w_inter * _dot(qb, _bf(c_old))
            den = _rowsum(s) + w_inter * _rowsum(qb.astype(F32) * n_old)
            cell = num / jnp.maximum(jnp.abs(den), jnp.exp(-m_t))
            kw = ks * w_state
            new.append((decay * c_old + _dot_tn(_bf(kw), vb), decay * n_old + _colsum(kw),
                        jnp.broadcast_to(m_new, m_old.shape)))
            cells.append(cell)
            hm = _sigmoid(o_ref[:, hs]) * cell
            hn = hm * lax.rsqrt(jnp.mean(hm * hm, axis=-1, keepdims=True) + EPS)
            z = z_ref[:, hs]
            ys.append(_bf((hn * g_ref[:, hs]) * (z * _sigmoid(z))))
        for h in range(heads):
            cst_ref[0, h] = _bf(old[h][0])
            nst_ref[0, h] = old[h][1]
            mst_ref[0, h] = old[h][2]
            cs[h], ns[h], ms[h] = new[h]
        cell_ref[...] = jnp.concatenate(cells, axis=1)
        y_ref[0] = jnp.concatenate(ys, axis=1)

    row = pl.BlockSpec((lc, d), lambda c: (c, 0))
    gcol = pl.BlockSpec((lc, ng), lambda c: (c, 0))
    grow = pl.BlockSpec((ng, lc), lambda c: (0, c))
    return _pcall_ride(
        body, ride, name="mlstm_fwd", grid=(nc,),
        in_specs=[pl.BlockSpec((3, lc, d), lambda c: (0, c, 0)), gcol, grow, gcol, grow,
                  pl.BlockSpec((lc, d), lambda c: (c, 3)), pl.BlockSpec((lc, d), lambda c: (c, 4)),
                  pl.BlockSpec((1, d), lambda c: (0, 0)), pl.BlockSpec(memory_space=pl.ANY)],
        out_specs=[row, pl.BlockSpec((1, lc, d), lambda c: (1, c, 0)),
                   pl.BlockSpec((1, heads, hd, hd), lambda c: (c, 0, 0, 0)),
                   pl.BlockSpec((1, heads, 1, hd), lambda c: (c, 0, 0, 0)),
                   pl.BlockSpec((1, heads, 1, 128), lambda c: (c, 0, 0, 0))],
        out_shape=[jax.ShapeDtypeStruct((s_len, d), F32), jax.ShapeDtypeStruct(ycat.shape, BF16),
                   jax.ShapeDtypeStruct((nc, heads, hd, hd), BF16),
                   jax.ShapeDtypeStruct((nc, heads, 1, hd), F32),
                   jax.ShapeDtypeStruct((nc, heads, 1, 128), F32)],
        scratch_shapes=[pltpu.VMEM((heads, hd, hd), F32), pltpu.VMEM((heads, 1, hd), F32),
                        pltpu.VMEM((heads, 1, 128), F32)],
        input_output_aliases={8: 1},
        compiler_params=_seq(),
        args=(qkv, *gates, u, u, ml_g, ycat))


def _out_proj(ycat, w_out_b, x, gate):
    s_len, d = x.shape
    tm = _tile(s_len, 1024)

    def body(a_ref, w_ref, x_ref, g_ref, y_ref, xn_ref):
        y = _dot(a_ref[0], w_ref[0:d, :]) + _dot(a_ref[1], w_ref[d:2 * d, :])
        y_ref[...] = y
        xn_ref[...] = x_ref[...] + g_ref[...] * y

    row = pl.BlockSpec((tm, d), lambda i: (i, 0))
    return _pcall(
        body, name="out_proj", grid=(s_len // tm,),
        in_specs=[pl.BlockSpec((2, tm, d), lambda i: (0, i, 0)), pl.BlockSpec((2 * d, d), lambda i: (0, 0)), row,
                  pl.BlockSpec((1, d), lambda i: (0, 0))],
        out_specs=[row, row],
        out_shape=[jax.ShapeDtypeStruct((s_len, d), F32)] * 2,
        compiler_params=_seq(),
    )(ycat, w_out_b, x, gate)


def _final_loss(x, g, target):
    s_len, d = x.shape
    tm = _tile(s_len, 256)

    def body(x_ref, g_ref, t_ref, dx_ref, dg_ref, loss_ref):
        @pl.when(pl.program_id(0) == 0)
        def _():
            dg_ref[...] = jnp.zeros_like(dg_ref)
            loss_ref[...] = jnp.zeros_like(loss_ref)

        xv = x_ref[...]
        r = lax.rsqrt(jnp.mean(xv * xv, axis=-1, keepdims=True) + EPS)
        xn = xv * r
        err = xn * g_ref[...] - t_ref[...]
        loss_ref[...] += 0.5 * jnp.sum(jnp.mean(err * err, axis=-1, keepdims=True))
        dout = err * (1.0 / d)
        dg_ref[...] += _colsum(dout * xn)
        dxn = dout * g_ref[...]
        dx_ref[...] = r * (dxn - xn * jnp.mean(dxn * xn, axis=-1, keepdims=True))

    row = pl.BlockSpec((tm, d), lambda i: (i, 0))
    vec = pl.BlockSpec((1, d), lambda i: (0, 0))
    return _pcall(
        body, name="final_loss", grid=(s_len // tm,),
        in_specs=[row, vec, row],
        out_specs=[row, vec, pl.BlockSpec((1, 128), lambda i: (0, 0))],
        out_shape=[jax.ShapeDtypeStruct((s_len, d), F32), jax.ShapeDtypeStruct((1, d), F32),
                   jax.ShapeDtypeStruct((1, 128), F32)],
        compiler_params=_seq(),
    )(x, g, target)


def _out_bwd(dxn, y, gate, w_out_b):
    s_len, d = dxn.shape
    tm = _tile(s_len, 1024)

    def body(dx_ref, y_ref, g_ref, w_ref, dg_ref, dy_ref, dc_ref):
        @pl.when(pl.program_id(0) == 0)
        def _():
            dg_ref[...] = jnp.zeros_like(dg_ref)

        dx = dx_ref[...]
        dg_ref[...] += _colsum(dx * y_ref[...])
        dy = _bf(g_ref[...] * dx)
        dy_ref[...] = dy
        dc_ref[0] = _dot_nt(dy, w_ref[0:d, :])
        dc_ref[1] = _dot_nt(dy, w_ref[d:2 * d, :])

    row = pl.BlockSpec((tm, d), lambda i: (i, 0))
    vec = pl.BlockSpec((1, d), lambda i: (0, 0))
    return _pcall(
        body, name="out_bwd", grid=(s_len // tm,),
        in_specs=[row, row, vec, pl.BlockSpec((2 * d, d), lambda i: (0, 0))],
        out_specs=[vec, row, pl.BlockSpec((2, tm, d), lambda i: (0, i, 0))],
        out_shape=[jax.ShapeDtypeStruct((1, d), F32), jax.ShapeDtypeStruct((s_len, d), BF16),
                   jax.ShapeDtypeStruct((2, s_len, d), F32)],
        compiler_params=_seq(),
    )(dxn, y, gate, w_out_b)


def _grad_matmul(a3, b3, nblk, a_idx, b_idx, out_shape, out_block, out_idx, ride=None):
    _, s_len, m = a3.shape
    n = b3.shape[2]
    tk = _tile(s_len, 2048)

    def body(a_ref, b_ref, o_ref):
        @pl.when(pl.program_id(1) == 0)
        def _():
            o_ref[...] = jnp.zeros_like(o_ref)

        o_ref[...] += _dot_tn(a_ref[0], b_ref[0])

    (out,), got = _pcall_ride(
        body, ride, name="grad_matmul", grid=(nblk, s_len // tk),
        in_specs=[pl.BlockSpec((1, tk, m), lambda p, t: (a_idx(p), t, 0)),
                  pl.BlockSpec((1, tk, n), lambda p, t: (b_idx(p), t, 0))],
        out_specs=[pl.BlockSpec((None,) + out_block, lambda p, t: (0,) + out_idx(p))],
        out_shape=[jax.ShapeDtypeStruct((1,) + out_shape, F32)],
        compiler_params=_seq(2), args=(a3, b3))
    return out, got


DU_PLANE = (2, 3, 4, 0, 1)


def _mlstm_bwd(qkv, gates, cst, nst, mst, cell, u, ml_g, d_ycat, wif_b, ride=None):
    _, s_len, d = qkv.shape
    ng = gates[0].shape[1]
    heads = ng // 2
    hd = d // heads
    lc = ML_CHUNK
    nc = s_len // lc
    kscale = hd ** -0.5

    def body(qkv_ref, gt_ref, gtt_ref, bc_ref, bct_ref, cst_ref, nst_ref, mst_ref, cell_ref, o_ref, z_ref, g_ref, dy_ref,
             wif_ref, dqkv_ref, dgt_ref, dbif_ref, du_ref, dg_ref, dcs, dns):
        @pl.when(pl.program_id(0) == 0)
        def _():
            dbif_ref[...] = jnp.zeros_like(dbif_ref)
            dcs[...] = jnp.zeros_like(dcs)
            dns[...] = jnp.zeros_like(dns)
            dg_ref[...] = jnp.zeros_like(dg_ref)

        causal, tril, triu = _tri_masks(lc)
        tril_strict = (tril.astype(F32) - (tril * triu).astype(F32)).astype(BF16)
        gtv, gttv, bcv, bctv = gt_ref[...], gtt_ref[...], bc_ref[...], bct_ref[...]
        lane = lax.broadcasted_iota(jnp.int32, (lc, ng), 1)
        dli_all = jnp.zeros((lc, ng), F32)
        from_later = jnp.zeros((lc, ng), F32)
        from_earlier = jnp.zeros((lc, ng), F32)
        across_all = jnp.zeros((1, ng), F32)
        old = [(dcs[h], dns[h]) for h in range(heads)]
        new, d_o, d_z, d_g, dqs, dks, dvs = [], [], [], [], [], [], []
        for h in range(heads):
            hs = slice(h * hd, (h + 1) * hd)
            li_c, li_r, gf_c, b_c, b_r = _chunk_gates(gtv, gttv, bcv, bctv, h, heads)
            m_prev = mst_ref[0, h][:, 0:1]
            m_t, w_intra, w_inter, _, w_state, decay = _chunk_weights(li_c, li_r, b_c, b_r, m_prev, causal)
            qb = qkv_ref[0, :, hs]
            qf = qb.astype(F32)
            ks = qkv_ref[1, :, hs].astype(F32) * kscale
            kb = _bf(ks)
            vb = qkv_ref[2, :, hs]
            c_b = cst_ref[0, h]
            n_old = nst_ref[0, h]
            s = _dot_nt(qb, kb) * w_intra
            den = _rowsum(s) + w_inter * _rowsum(qf * n_old)
            floor = jnp.exp(-m_t)
            dstab = jnp.maximum(jnp.abs(den), floor)
            cell = cell_ref[:, hs]
            o = o_ref[:, hs]
            so = _sigmoid(o)
            hm = so * cell
            rinv = lax.rsqrt(jnp.mean(hm * hm, axis=-1, keepdims=True) + EPS)
            hn = hm * rinv
            z = z_ref[:, hs]
            sgz = _sigmoid(z)
            sz = z * sgz
            gh = g_ref[:, hs]
            dy = dy_ref[0, :, hs]
            d_z.append(_bf(dy * (hn * gh) * _dsilu(z, sgz)))
            d_g.append(_colsum(dy * hn * sz))
            dhn = dy * gh * sz
            dhm = rinv * (dhn - hn * jnp.mean(dhn * hn, axis=-1, keepdims=True))
            d_o.append(_bf(dhm * cell * so * (1.0 - so)))
            dcell = dhm * so
            dnum = dcell / dstab
            dnb = _bf(dnum)
            dden = -_rowsum(dcell * cell) / dstab * jnp.where(jnp.abs(den) > floor, jnp.where(den > 0.0, 1.0, -1.0), 0.0)
            dst = _dot_nt(dnb, vb) + dden
            dsdb = _bf(dst * w_intra)
            dc_out, dn_out = old[h]
            dcb = _bf(dc_out)
            dq_inter = w_inter * (_dot_nt(dnb, c_b) + dden * n_old)
            dk_inter = w_state * (_dot_nt(vb, dcb) + dn_out)
            dq = _dot(dsdb, kb) + dq_inter
            dk = _dot_tn(dsdb, qb) + dk_inter
            dv = _dot_tn(_bf(s), dnb) + _dot(_bf(ks * w_state), dcb)
            wq = w_inter * qf
            new.append((decay * dc_out + _dot_tn(_bf(wq), dnb), decay * dn_out + _colsum(wq * dden)))
            pmat = dst * s
            p_rows = _rowsum(pmat)
            p_cols = _rowsum(pmat.T)
            q_in = _rowsum(qf * dq_inter)
            k_in = _rowsum(ks * dk_inter)
            across = decay * (jnp.sum(dc_out * c_b.astype(F32), keepdims=True) + jnp.sum(dn_out * n_old, keepdims=True))
            dli_all = dli_all + jnp.where(lane == h, p_cols + k_in, 0.0)
            from_later = from_later + jnp.where(lane == heads + h, p_rows - p_cols + q_in, 0.0)
            from_earlier = from_earlier + jnp.where(lane == heads + h, k_in, 0.0)
            across_all = across_all + jnp.where(lane[0:1] == heads + h, across, 0.0)
            dqs.append(dq)
            dks.append(dk * kscale)
            dvs.append(dv)
        for h in range(heads):
            dcs[h], dns[h] = new[h]
        du_ref[0] = jnp.concatenate(d_o, axis=1)
        du_ref[1] = jnp.concatenate(d_z, axis=1)
        dg_ref[...] += jnp.concatenate(d_g, axis=1)
        dlf = _tri_dot_left(triu, from_later) + _tri_dot_left(tril_strict, from_earlier) + across_all
        dgt = dli_all + dlf * _sigmoid(-gtv)
        dgt_ref[...] = dgt
        dbif_ref[...] += _colsum(dgt)
        dgb = _bf(dgt)
        dqkv_ref[0] = _bf(jnp.concatenate(dqs, axis=1) + _dot_nt(dgb, wif_ref[0:d, :]))
        dqkv_ref[1] = _bf(jnp.concatenate(dks, axis=1) + _dot_nt(dgb, wif_ref[d:2 * d, :]))
        dqkv_ref[2] = _bf(jnp.concatenate(dvs, axis=1) + _dot_nt(dgb, wif_ref[2 * d:3 * d, :]))

    rev = lambda c: nc - 1 - c
    row = pl.BlockSpec((lc, d), lambda c: (rev(c), 0))
    gcol = pl.BlockSpec((lc, ng), lambda c: (rev(c), 0))
    grow = pl.BlockSpec((ng, lc), lambda c: (0, rev(c)))
    return _pcall_ride(
        body, ride, name="mlstm_bwd", grid=(nc,),
        in_specs=[pl.BlockSpec((3, lc, d), lambda c: (0, rev(c), 0)), gcol, grow, gcol, grow,
                  pl.BlockSpec((1, heads, hd, hd), lambda c: (rev(c), 0, 0, 0)),
                  pl.BlockSpec((1, heads, 1, hd), lambda c: (rev(c), 0, 0, 0)),
                  pl.BlockSpec((1, heads, 1, 128), lambda c: (rev(c), 0, 0, 0)),
                  row, pl.BlockSpec((lc, d), lambda c: (rev(c), 3)), pl.BlockSpec((lc, d), lambda c: (rev(c), 4)),
                  pl.BlockSpec((1, d), lambda c: (0, 0)), pl.BlockSpec((1, lc, d), lambda c: (1, rev(c), 0)),
                  pl.BlockSpec((3 * d, ng), lambda c: (0, 0))],
        out_specs=[pl.BlockSpec((3, lc, d), lambda c: (0, rev(c), 0)), pl.BlockSpec((lc, ng), lambda c: (rev(c), 0)),
                   pl.BlockSpec((1, ng), lambda c: (0, 0)), pl.BlockSpec((2, lc, d), lambda c: (0, rev(c), 0)),
                   pl.BlockSpec((1, d), lambda c: (0, 0))],
        out_shape=[jax.ShapeDtypeStruct((3, s_len, d), BF16), jax.ShapeDtypeStruct((s_len, ng), F32),
                   jax.ShapeDtypeStruct((1, ng), F32), jax.ShapeDtypeStruct((5, s_len, d), BF16),
                   jax.ShapeDtypeStruct((1, d), F32)],
        scratch_shapes=[pltpu.VMEM((heads, hd, hd), F32), pltpu.VMEM((heads, 1, hd), F32)],
        compiler_params=_seq(),
        args=(qkv, *gates, cst, nst, mst, cell, u, u, ml_g, d_ycat, wif_b))


def _conv_bwd_tile(dp, later, taps, cw_ref, gw_ref, gb_ref):
    tm = dp.shape[0]
    dwin = jnp.concatenate([dp, later[...]], axis=0)
    later[...] = dp[0:HALO]
    acc = cw_ref[CONV_WIDTH - 1:CONV_WIDTH, :] * dp
    for k in range(CONV_WIDTH):
        if k < CONV_WIDTH - 1:
            acc = acc + cw_ref[k:k + 1, :] * _shift_up(dwin, CONV_WIDTH - 1 - k)[0:tm]
        gw_ref[k:k + 1, :] += _colsum(dp * taps[k])
    gb_ref[...] += _colsum(dp)
    return acc


def _ml_pre_bwd(dqkv, u, conv_w, conv_b, wqkv_b, du):
    s_len = u.shape[0]
    d = conv_w.shape[1]
    _, heads, hd, _ = wqkv_b.shape
    tm = _tile(s_len, 256)
    per = tm // HALO
    nt = s_len // tm

    def body(dqkv_ref, x_ref, xp_ref, cw_ref, cb_ref, w_ref, _, dx_ref, gw_ref, gcw_ref, gcb_ref, later, dps, dxs):
        i = pl.program_id(0)

        @pl.when(i == 0)
        def _():
            gw_ref[...] = jnp.zeros_like(gw_ref)
            gcw_ref[...] = jnp.zeros_like(gcw_ref)
            gcb_ref[...] = jnp.zeros_like(gcb_ref)
            later[...] = jnp.zeros_like(later)

        prev = jnp.where(i == nt - 1, 0.0, xp_ref[...])
        xm = x_ref[...]
        taps = _conv_taps(jnp.concatenate([prev, xm], axis=0))
        pre = _conv_fwd(taps, cw_ref, cb_ref)
        sg = _sigmoid(pre)
        xcb = _bf(pre * sg)
        xmb = _bf(xm)
        for h in range(heads):
            hs = slice(h * hd, (h + 1) * hd)
            dqh, dkh, dvh = dqkv_ref[0, :, hs], dqkv_ref[1, :, hs], dqkv_ref[2, :, hs]
            dxc = _dot_nt(dqh, w_ref[0, h]) + _dot_nt(dkh, w_ref[1, h])
            dps[:, hs] = dxc * _dsilu(pre[:, hs], sg[:, hs])
            dxs[:, hs] = _dot_nt(dvh, w_ref[2, h])
            gw_ref[0, h] += _dot_tn(xcb[:, hs], dqh)
            gw_ref[1, h] += _dot_tn(xcb[:, hs], dkh)
            gw_ref[2, h] += _dot_tn(xmb[:, hs], dvh)
        dx_ref[0] = _bf(_conv_bwd_tile(dps[...], later, taps, cw_ref, gcw_ref, gcb_ref) + dxs[...])

    rev = lambda i: nt - 1 - i
    vec = pl.BlockSpec((1, d), lambda i: (0, 0))
    cwb = pl.BlockSpec((CONV_WIDTH, d), lambda i: (0, 0))
    whole4 = pl.BlockSpec(wqkv_b.shape, lambda i: (0, 0, 0, 0))
    return _pcall(
        body, name="ml_pre_bwd", grid=(nt,),
        in_specs=[pl.BlockSpec((3, tm, d), lambda i: (0, rev(i), 0)), pl.BlockSpec((tm, d), lambda i: (rev(i), 2)),
                  pl.BlockSpec((HALO, d), lambda i: (jnp.maximum(rev(i) * per - 1, 0), 2)),
                  cwb, vec, whole4, pl.BlockSpec(memory_space=pl.ANY)],
        out_specs=[pl.BlockSpec((1, tm, d), lambda i: (DU_PLANE[2], rev(i), 0)), whole4, cwb, vec],
        out_shape=[jax.ShapeDtypeStruct(du.shape, BF16), jax.ShapeDtypeStruct(wqkv_b.shape, F32),
                   jax.ShapeDtypeStruct((CONV_WIDTH, d), F32), jax.ShapeDtypeStruct((1, d), F32)],
        scratch_shapes=[pltpu.VMEM((HALO, d), F32), pltpu.VMEM((tm, d), F32), pltpu.VMEM((tm, d), F32)],
        input_output_aliases={6: 0},
        compiler_params=_seq(),
    )(dqkv, u, u, conv_w, conv_b, wqkv_b, du)


def _rg_bwd(d_ycat, u, hh, conv_w, conv_b, wa_b, ba, wx_b, bx, lam, du):
    s_len = u.shape[0]
    d = conv_w.shape[1]
    heads, hd, _ = wa_b.shape
    tm = _tile(s_len, 256)
    per = tm // HALO
    nt = s_len // tm

    def body(dy_ref, x_ref, xp_ref, z_ref, hh_ref, hp_ref, cw_ref, cb_ref, wa_ref, ba_ref, wx_ref, bx_ref, lam_ref, _,
             du_ref, gwa_ref, gwx_ref, gba_ref, gbx_ref, glam_ref, gcw_ref, gcb_ref, carry, gbuf, later, dxcs):
        i = pl.program_id(0)
        first = i == nt - 1

        @pl.when(i == 0)
        def _():
            carry[...] = jnp.zeros_like(carry)
            later[...] = jnp.zeros_like(later)
            gwa_ref[...] = jnp.zeros_like(gwa_ref)
            gwx_ref[...] = jnp.zeros_like(gwx_ref)
            gba_ref[...] = jnp.zeros_like(gba_ref)
            gbx_ref[...] = jnp.zeros_like(gbx_ref)
            glam_ref[...] = jnp.zeros_like(glam_ref)
            gcw_ref[...] = jnp.zeros_like(gcw_ref)
            gcb_ref[...] = jnp.zeros_like(gcb_ref)

        prev = jnp.where(first, 0.0, xp_ref[...])
        taps = _conv_taps(jnp.concatenate([prev, x_ref[...]], axis=0))
        xc = _conv_fwd(taps, cw_ref, cb_ref)
        r, ig, sp, log_a, a, mult = _rg_gates(xc, wa_ref, ba_ref, wx_ref, bx_ref, lam_ref)
        z = z_ref[...]
        sgz = _sigmoid(z)
        dy = dy_ref[0]
        hh_v = hh_ref[...]
        du_ref[1] = _bf(dy * hh_v * _dsilu(z, sgz))
        dhh = dy * (z * sgz)
        rows = lax.broadcasted_iota(jnp.int32, a.shape, 0)
        coef = jnp.where(rows == tm - 1, carry[1:2, :], _shift_up(a, 1))
        ca, cu = _scan_groups(coef, dhh, reverse=True)
        c = carry[0:1, :]
        for j in range(per - 1, -1, -1):
            blk = ca[j * 8:(j + 1) * 8] * c + cu[j * 8:(j + 1) * 8]
            gbuf[j * 8:(j + 1) * 8, :] = blk
            c = blk[0:1]
        carry[0:1, :] = c
        carry[1:2, :] = a[0:1]
        g = gbuf[...]
        hprev_tile = jnp.where(first, 0.0, hp_ref[...])
        hprev = _shift_down(jnp.concatenate([hprev_tile, hh_v], axis=0), 1)[HALO:]
        da = g * hprev
        gx_ = g * xc
        d_mult = gx_ * ig
        d_ig = gx_ * mult
        dxc = g * mult * ig
        dlog_a = da * a - d_mult * (a * a / mult)
        d_r = dlog_a * ((-RG_C) * sp)
        glam_ref[...] += _colsum(dlog_a * ((-RG_C) * r)) * (-_sigmoid(-lam_ref[...]))
        d_ga = d_r * r * (1.0 - r)
        d_gx = d_ig * ig * (1.0 - ig)
        gba_ref[...] += _colsum(d_ga)
        gbx_ref[...] += _colsum(d_gx)
        xb = _bf(xc)
        dgab = _bf(d_ga)
        dgxb = _bf(d_gx)
        for h in range(heads):
            hs = slice(h * hd, (h + 1) * hd)
            dxcs[:, hs] = dxc[:, hs] + _dot_nt(dgab[:, hs], wa_ref[h]) + _dot_nt(dgxb[:, hs], wx_ref[h])
            gwa_ref[h] += _dot_tn(xb[:, hs], dgab[:, hs])
            gwx_ref[h] += _dot_tn(xb[:, hs], dgxb[:, hs])
        du_ref[0] = _bf(_conv_bwd_tile(dxcs[...], later, taps, cw_ref, gcw_ref, gcb_ref))

    assert DU_PLANE[0] % 2 == 0 and DU_PLANE[1] == DU_PLANE[0] + 1
    rev = lambda i: nt - 1 - i
    row = pl.BlockSpec((tm, d), lambda i: (rev(i), 0))
    halo_prev = lambda col: pl.BlockSpec((HALO, d), lambda i: (jnp.maximum(rev(i) * per - 1, 0), col))
    vec = pl.BlockSpec((1, d), lambda i: (0, 0))
    cwb = pl.BlockSpec((CONV_WIDTH, d), lambda i: (0, 0))
    whole3 = lambda a: pl.BlockSpec(a.shape, lambda i: (0, 0, 0))
    return _pcall(
        body, name="rg_bwd", grid=(nt,),
        in_specs=[pl.BlockSpec((1, tm, d), lambda i: (0, rev(i), 0)), row, halo_prev(0),
                  pl.BlockSpec((tm, d), lambda i: (rev(i), 1)), row, halo_prev(0),
                  cwb, vec, whole3(wa_b), vec, whole3(wx_b), vec, vec, pl.BlockSpec(memory_space=pl.ANY)],
        out_specs=[pl.BlockSpec((2, tm, d), lambda i: (DU_PLANE[0] // 2, rev(i), 0)), whole3(wa_b), whole3(wa_b),
                   vec, vec, vec, cwb, vec],
        out_shape=[jax.ShapeDtypeStruct(du.shape, BF16), jax.ShapeDtypeStruct(wa_b.shape, F32),
                   jax.ShapeDtypeStruct(wa_b.shape, F32)] + [jax.ShapeDtypeStruct((1, d), F32)] * 3
        + [jax.ShapeDtypeStruct((CONV_WIDTH, d), F32), jax.ShapeDtypeStruct((1, d), F32)],
        scratch_shapes=[pltpu.VMEM((8, d), F32), pltpu.VMEM((tm, d), F32), pltpu.VMEM((HALO, d), F32),
                        pltpu.VMEM((tm, d), F32)],
        input_output_aliases={13: 0},
        compiler_params=_seq(),
    )(d_ycat, u, u, u, hh, hh, conv_w, conv_b, wa_b, ba, wx_b, bx, lam, du)


def _in_bwd(du, w4, x, dxn, g, scale, ride=None):
    s_len, d = x.shape
    tm = _tile(s_len, 512)
    nsh_chips, _, nsh = w4.shape
    npc = du.shape[0]
    ck = d // 4
    assert nsh % ck == 0 and npc * d == nsh_chips * nsh

    def body(du_ref, w_ref, x_ref, dxn_ref, g_ref, sc_ref, dx_ref, dsh_ref, dsc_ref, dg_ref):
        @pl.when(pl.program_id(0) == 0)
        def _():
            dsh_ref[...] = jnp.zeros_like(dsh_ref)
            dsc_ref[...] = jnp.zeros_like(dsc_ref)
            dg_ref[...] = jnp.zeros_like(dg_ref)

        dh = None
        for q in range(npc * d // ck):
            col = q * ck
            p, pc = col // d, col % d
            s, sc = col // nsh, col % nsh
            t = _dot_nt(du_ref[DU_PLANE[p], :, pc:pc + ck], w_ref[s, :, sc:sc + ck])
            dh = t if dh is None else dh + t
        xv = x_ref[...]
        r = lax.rsqrt(jnp.mean(xv * xv, axis=-1, keepdims=True) + EPS)
        xn = xv * r
        gv = g_ref[...]
        onesc = 1.0 + sc_ref[...]
        dsh_ref[...] += _colsum(dh)
        dsc_ref[...] += _colsum(dh * (xn * gv))
        dg_ref[...] += _colsum(dh * xn * onesc)
        dxh = dh * (gv * onesc)
        dx_ref[...] = dxn_ref[...] + r * (dxh - xn * jnp.mean(dxh * xn, axis=-1, keepdims=True))

    row = pl.BlockSpec((tm, d), lambda i: (i, 0))
    vec = pl.BlockSpec((1, d), lambda i: (0, 0))
    return _pcall_ride(
        body, ride, name="in_bwd", grid=(s_len // tm,),
        in_specs=[pl.BlockSpec((npc, tm, d), lambda i: (0, i, 0)), pl.BlockSpec(w4.shape, lambda i: (0, 0, 0)), row, row,
                  vec, vec],
        out_specs=[row, vec, vec, vec],
        out_shape=[jax.ShapeDtypeStruct((s_len, d), F32)] + [jax.ShapeDtypeStruct((1, d), F32)] * 3,
        compiler_params=_seq(),
        args=(du, w4, x, dxn, g, scale))


def _layer_fwd(x, p, rides=None, late=None):
    rides = rides or {}
    (h_b, u), got = _ln_inproj(x, p["norm_g"], p["scale"], p["shift"], p["w4"], rides.get("ln_inproj"))
    if late is not None:
        p = {**p, **late(got)}
    (hh, ycat), got_a = _rg_fwd(u, p["rg_conv_w"], p["rg_conv_b"], p["rg_wa_b"], p["rg_ba"], p["rg_wx_b"], p["rg_bx"],
                                p["rg_lam"], rides.get("rg_fwd"))
    qkv, *gates = _ml_pre(u, p["ml_conv_w"], p["ml_conv_b"], p["wqkv_b"], p["wif_b"], p["wift_b"], p["b_if"],
                          p["b_ift"])
    (cell, ycat, cst, nst, mst), got_b = _mlstm_fwd(qkv, gates, u, p["ml_g"], ycat, rides.get("mlstm_fwd"))
    y, x_new = _out_proj(ycat, p["w_out_b"], x, p["gate"])
    saved = dict(x=x, h_b=h_b, u=u, hh=hh, qkv=qkv, gates=gates, cell=cell, ycat=ycat, cst=cst, nst=nst, mst=mst, y=y)
    return x_new, saved, p, dict(rg_fwd=got_a, mlstm_fwd=got_b)


def _layer_bwd(dxn, p, s, rides=None):
    rides = rides or {}
    landed = {}
    ride = lambda kernel: rides[kernel](grads, landed) if kernel in rides else None
    u = s["u"]
    d = dxn.shape[1]
    d_gate, dy_b, d_ycat = _out_bwd(dxn, s["y"], p["gate"], p["w_out_b"])
    grads = dict(w_out=_grad_matmul(s["ycat"], dy_b[None], 2, lambda b: b, lambda b: 0, (2 * d, d), (d, d),
                                    lambda b: (b, 0))[0])
    (dqkv, dgt, g_b_if, du, g_ml_g), landed["mlstm_bwd"] = _mlstm_bwd(
        s["qkv"], s["gates"], s["cst"], s["nst"], s["mst"], s["cell"], u, p["ml_g"], d_ycat, p["wif_b"],
        ride("mlstm_bwd"))
    ng = dgt.shape[1]
    g_w_if = _grad_matmul(s["qkv"], _bf(dgt)[None], 3, lambda b: b, lambda b: 0, (3 * d, ng), (d, ng),
                          lambda b: (b, 0))[0][0]
    du, g_wqkv, g_ml_cw, g_ml_cb = _ml_pre_bwd(dqkv, u, p["ml_conv_w"], p["ml_conv_b"], p["wqkv_b"], du)
    du, g_wa, g_wx, g_ba, g_bx, g_lam, g_rg_cw, g_rg_cb = _rg_bwd(d_ycat, u, s["hh"], p["rg_conv_w"], p["rg_conv_b"],
                                                                  p["rg_wa_b"], p["rg_ba"], p["rg_wx_b"], p["rg_bx"],
                                                                  p["rg_lam"], du)
    grads.update(rg_conv_w=g_rg_cw, rg_conv_b=g_rg_cb, rg_w_a=g_wa, rg_b_a=g_ba, rg_w_x=g_wx, rg_b_x=g_bx,
                 rg_lambda=g_lam, ml_conv_w=g_ml_cw, ml_conv_b=g_ml_cb, ml_w_qkv=g_wqkv, ml_w_if=g_w_if, ml_b_if=g_b_if,
                 ml_norm_g=g_ml_g)
    npc = du.shape[0]
    grads["w_in"], landed["grad_w_in"] = _grad_matmul(
        s["h_b"][None], du, npc, lambda b: 0, lambda b: (b + DU_PLANE[0]) % npc, (d, npc * d), (d, d),
        lambda b: (0, b), ride("grad_w_in"))
    (dx, d_shift, d_scale, grads["norm_g"]), landed["in_bwd"] = _in_bwd(du, p["w4"], s["x"], dxn, p["norm_g"],
                                                                        p["scale"], ride("in_bwd"))
    return dx, grads, jnp.concatenate([d_shift, d_scale, d_gate], axis=1), landed


def _trunk_fwd_bwd(x, target, final_g, layers):
    saved = []
    for p in layers:
        x, s, _, _ = _layer_fwd(x, p)
        saved.append(s)
    dx, g_final, loss = _final_loss(x, final_g, target)
    grads, dmods = [], []
    for layer in reversed(range(len(layers))):
        dx, g, dm, _ = _layer_bwd(dx, layers[layer], saved[layer])
        grads.append(g)
        dmods.append(dm)
    return loss, dx, g_final, grads[::-1], dmods[::-1]


def _me():
    return lax.axis_index("x"), lax.axis_index("y"), lax.axis_index("c")


def _remote(src, dst, send_sem, recv_sem, to):
    return pltpu.make_async_remote_copy(src_ref=src, dst_ref=dst, send_sem=send_sem, recv_sem=recv_sem,
                                        device_id=to, device_id_type=MESH)


def _all_gather8(blocks, space):
    n = len(blocks)

    def body(*refs):
        x_refs, out_refs = refs[:n], refs[n:2 * n]
        send_sems, recv_sems, local_sems = refs[2 * n:]
        x, y, c = _me()
        me, sibling = (x, y, c), (x, y, 1 - c)
        chips = [(1 - x, y), (x, 1 - y), (1 - x, 1 - y)]

        def rows(i, px, py, pc):
            m_per = blocks[i].shape[0]
            return out_refs[i].at[pl.ds((4 * px + 2 * py + pc) * m_per, m_per), :]

        def copy(i, k, blk, to, src=None):
            return _remote(rows(i, *blk) if src is None else src, rows(i, *blk), send_sems.at[7 * i + k],
                           recv_sems.at[7 * i + k], to)

        mine = [pltpu.make_async_copy(x_refs[i], rows(i, *me), local_sems.at[i]) for i in range(n)]
        first = []
        for i in range(n):
            first.append(copy(i, 0, me, sibling, src=x_refs[i]))
            first += [copy(i, 1 + j, me, (*chip, c), src=x_refs[i]) for j, chip in enumerate(chips)]
        for cp in mine + first:
            cp.start()
        passed = []
        for j, chip in enumerate(chips):
            for i in range(n):
                copy(i, 1 + j, (*chip, c), me).wait_recv()
                passed.append(copy(i, 4 + j, (*chip, c), sibling))
                passed[-1].start()
        for i in range(n):
            copy(i, 0, sibling, me).wait_recv()
            for j, chip in enumerate(chips):
                copy(i, 4 + j, (*chip, 1 - c), me).wait_recv()
        for cp in first + passed:
            cp.wait_send()
        for cp in mine:
            cp.wait()

    spec = pl.BlockSpec(memory_space=space)
    return _pcall(
        body, name="all_gather8",
        out_shape=[jax.ShapeDtypeStruct((8 * b.shape[0], b.shape[1]), b.dtype) for b in blocks],
        in_specs=[spec] * n, out_specs=[spec] * n,
        scratch_shapes=[pltpu.SemaphoreType.DMA((7 * n,)), pltpu.SemaphoreType.DMA((7 * n,)),
                        pltpu.SemaphoreType.DMA((n,))],
    )(*blocks)


def _exchange(legs):
    n = len(legs)

    def body(*refs):
        copies, local = _exchange_body(legs, refs[:n], refs[n:2 * n], *refs[2 * n:])
        for cp in copies + local:
            cp.start()
        for cp in copies:
            cp.wait_recv()
        for cp in copies:
            cp.wait_send()
        for cp in local:
            cp.wait()

    hbm = pl.BlockSpec(memory_space=pltpu.HBM)
    return _pcall(body, name="exchange", out_shape=[leg.landing() for leg in legs], in_specs=[hbm] * n,
                  out_specs=[hbm] * n, scratch_shapes=_exchange_sems(legs))(*[leg.src for leg in legs])


def _sib_fill(boths):
    n = len(boths)

    def body(*refs):
        dst = refs[n:2 * n]
        send_sems, recv_sems = refs[2 * n:]
        x, y, c = _me()
        view = lambda i: dst[i].at[pl.ds(0, boths[i].shape[0]), c]
        copies = [_remote(view(i), view(i), send_sems.at[i], recv_sems.at[i], (x, y, 1 - c)) for i in range(n)]
        for cp in copies:
            cp.start()
        for cp in copies:
            cp.wait_recv()
        for cp in copies:
            cp.wait_send()

    hbm = pl.BlockSpec(memory_space=pltpu.HBM)
    return _pcall(
        body, name="sib_fill",
        out_shape=[jax.ShapeDtypeStruct(b.shape, b.dtype) for b in boths],
        in_specs=[hbm] * n, out_specs=[hbm] * n, input_output_aliases={i: i for i in range(n)},
        scratch_shapes=[pltpu.SemaphoreType.DMA((n,)), pltpu.SemaphoreType.DMA((n,))],
    )(*boths)


def _row_tile(rows, cap=4096, mult=16):
    best = None
    for t in range(mult, min(rows, cap) + 1, mult):
        if rows % t == 0:
            best = t
    return rows if best is None else best


def _pair_sum(half, own, own_spec, got, got_spec, out_shape, out_spec, grid):
    def body(_, a_ref, b_ref, o_ref):
        o_ref[...] = (a_ref[...] + b_ref[...].astype(F32)).astype(o_ref.dtype)

    return _pcall(
        body, name="pair_sum",
        grid_spec=pltpu.PrefetchScalarGridSpec(num_scalar_prefetch=1, grid=grid, in_specs=[own_spec, got_spec],
                                               out_specs=out_spec),
        out_shape=out_shape, compiler_params=_seq(len(grid)))(half, own, got)


def _chip_sum(ids, part, met, fill, layer=0, stack=1):
    _, _, rows, n = part.shape
    tr = _row_tile(rows, cap=max(16, (1 << 18) // n))
    first = isinstance(stack, int)

    def body(_, own_ref, a_ref, b_ref, c_ref, *rest):
        acc = own_ref[...].astype(F32) + a_ref[...].astype(F32)
        acc = acc + b_ref[...].astype(F32)
        rest[-1][...] = acc + c_ref[...].astype(F32)

    blk = (None, None, tr, n)
    other = lambda k: pl.BlockSpec(blk, lambda j, ids: ((ids[0] + k) % 4, 0, j, 0))
    in_specs = [pl.BlockSpec(blk, lambda j, ids: (ids[0], 0, j, 0)), other(1), other(2), other(3)]
    return _pcall(
        body, name="chip_sum",
        grid_spec=pltpu.PrefetchScalarGridSpec(
            num_scalar_prefetch=1, grid=(rows // tr,),
            in_specs=in_specs if first else in_specs + [pl.BlockSpec(memory_space=pl.ANY)],
            out_specs=pl.BlockSpec(blk, lambda j, ids: (layer, ids[1] if fill else 0, j, 0))),
        out_shape=jax.ShapeDtypeStruct(((stack,) if first else stack.shape[:1]) + (2 if fill else 1, rows, n), F32),
        input_output_aliases={} if first else {5: 0},
        compiler_params=_seq())(*((ids, part, met, met, met) if first else (ids, part, met, met, met, stack)))


def _ada_mod(c_all, w_ada, b_ada_cols):
    depth, d, n = w_ada.shape
    nb = c_all.shape[0]

    def body(c_ref, w_ref, b_ref, o_ref):
        cv = c_ref[...]
        ca = _bf(cv * _sigmoid(cv))
        o_ref[0] = _dot(ca, _bf(w_ref[0])) + b_ref[0]

    return _pcall(body, name="ada_mod", grid=(depth,),
                  in_specs=[pl.BlockSpec((nb, d), lambda l: (0, 0)), pl.BlockSpec((1, d, n), lambda l: (l, 0, 0)),
                            pl.BlockSpec((1, 1, n), lambda l: (l, 0, 0))],
                  out_specs=pl.BlockSpec((1, nb, n), lambda l: (l, 0, 0)),
                  out_shape=jax.ShapeDtypeStruct((depth, nb, n), F32), compiler_params=_seq())(c_all, w_ada, b_ada_cols)


def _ada_grad(c_all, dmod_cols, rows_all):
    nb, d = c_all.shape
    depth, _, n = dmod_cols.shape
    kinds, n_all = rows_all.shape[1], rows_all.shape[3]

    def body(c_ref, dm_ref, da_ref, gw_ref, gb_ref):
        cv = c_ref[...]
        ca = _bf(cv * _sigmoid(cv))
        gw_ref[0] = _dot_tn(ca, _bf(dm_ref[0]))
        for k in range(kinds):
            gb_ref[0, k] = _colsum(da_ref[0, k])

    return _pcall(body, name="ada_grad", grid=(depth,),
                  in_specs=[pl.BlockSpec((nb, d), lambda l: (0, 0)), pl.BlockSpec((1, nb, n), lambda l: (l, 0, 0)),
                            pl.BlockSpec((1, kinds, nb, n_all), lambda l: (l, 0, 0, 0))],
                  out_specs=[pl.BlockSpec((1, d, n), lambda l: (l, 0, 0)),
                             pl.BlockSpec((1, kinds, 1, n_all), lambda l: (l, 0, 0, 0))],
                  out_shape=[jax.ShapeDtypeStruct((depth, d, n), F32), jax.ShapeDtypeStruct((depth, kinds, 1, n_all), F32)],
                  compiler_params=_seq())(c_all, dmod_cols, rows_all)


def _adamw(items, ride=None):
    two_d = [tuple(t.reshape(w.size // w.shape[-1], w.shape[-1]) for t in (w, g, m, v)) for w, g, m, v in items]
    n = len(items)
    if n == 1:
        rows, cols = two_d[0][0].shape
        tr = _row_tile(rows, cap=max(8, (1 << 18) // cols), mult=8)
        blocks = [pl.BlockSpec((tr, cols), lambda i: (i, 0))]
        grid = (rows // tr,)
    else:
        blocks = [pl.BlockSpec(t[0].shape, lambda i: (0, 0)) for t in two_d]
        grid = (1,)

    def body(*refs):
        for k in range(n):
            w_ref, g_ref, m_ref, v_ref = refs[4 * k:4 * k + 4]
            d_ref, mo_ref, vo_ref = refs[4 * n + 3 * k:4 * n + 3 * k + 3]
            gv = g_ref[...]
            mn = ADAM_B1 * m_ref[...] + (1.0 - ADAM_B1) * gv
            vn = ADAM_B2 * v_ref[...] + (1.0 - ADAM_B2) * (gv * gv)
            m_hat = mn / (1.0 - ADAM_B1 ** ADAM_STEP)
            v_hat = vn / (1.0 - ADAM_B2 ** ADAM_STEP)
            d_ref[...] = -ADAM_LR * (m_hat / (jnp.sqrt(v_hat) + ADAM_EPS) + ADAM_WD * w_ref[...])
            mo_ref[...] = mn
            vo_ref[...] = vn

    outs, got = _pcall_ride(
        body, ride, name="adamw", grid=grid,
        in_specs=[b for b in blocks for _ in range(4)], out_specs=[b for b in blocks for _ in range(3)],
        out_shape=[jax.ShapeDtypeStruct(t[0].shape, F32) for t in two_d for _ in range(3)],
        compiler_params=_seq(), args=tuple(a for t in two_d for a in t))
    return [tuple(o.reshape(items[k][0].shape) for o in outs[3 * k:3 * k + 3]) for k in range(n)], got


WEIGHTS = ["norm_g", "w_ada", "b_ada", "w_in", "rg_conv_w", "rg_conv_b", "rg_w_a", "rg_b_a", "rg_w_x", "rg_b_x",
           "rg_lambda", "ml_conv_w", "ml_conv_b", "ml_w_q", "ml_w_k", "ml_w_v", "ml_w_if", "ml_b_if", "ml_norm_g",
           "w_out", "final_g"]
SMALL_SHARDED = {"ml_w_qkv": 2, "rg_conv_w": 1, "ml_conv_w": 1, "ml_w_if": 0}
REPLICATED = ["rg_w_a", "rg_w_x", "rg_conv_b", "rg_b_a", "rg_b_x", "rg_lambda", "ml_conv_b", "ml_norm_g", "ml_b_if"]
LANES = 128


def _to_pieces(g, axis):
    shp = g.shape
    g = g.reshape(shp[:axis] + (4, 2, shp[axis] // 8) + shp[axis + 1:])
    g = jnp.moveaxis(g, (axis, axis + 1), (0, 1))
    return g.reshape(4, 2, -1)


def _from_pieces(p, shard_shape, axis):
    k = p.shape[0]
    rest = shard_shape[:axis] + (shard_shape[axis] // k,) + shard_shape[axis + 1:]
    t = jnp.moveaxis(p.reshape((k,) + rest), 0, axis)
    return t.reshape(shard_shape)


def _pad_rows(flat, mult):
    n = flat.shape[-1]
    pad = (-n) % mult
    if pad:
        flat = jnp.concatenate([flat, jnp.zeros(flat.shape[:-1] + (pad,), flat.dtype)], axis=-1)
    return flat


def kernel(x, c, norm_g, w_ada, b_ada, w_in, rg_conv_w, rg_conv_b, rg_w_a, rg_b_a, rg_w_x, rg_b_x, rg_lambda, ml_conv_w, ml_conv_b, ml_w_q, ml_w_k, ml_w_v, ml_w_if, ml_b_if, ml_norm_g, w_out, final_g, loss_target, m_norm_g, m_w_ada, m_b_ada, m_w_in, m_rg_conv_w, m_rg_conv_b, m_rg_w_a, m_rg_b_a, m_rg_w_x, m_rg_b_x, m_rg_lambda, m_ml_conv_w, m_ml_conv_b, m_ml_w_q, m_ml_w_k, m_ml_w_v, m_ml_w_if, m_ml_b_if, m_ml_norm_g, m_w_out, m_final_g, v_norm_g, v_w_ada, v_b_ada, v_w_in, v_rg_conv_w, v_rg_conv_b, v_rg_w_a, v_rg_b_a, v_rg_w_x, v_rg_b_x, v_rg_lambda, v_ml_conv_w, v_ml_conv_b, v_ml_w_q, v_ml_w_k, v_ml_w_v, v_ml_w_if, v_ml_b_if, v_ml_norm_g, v_w_out, v_final_g):
    given = dict(locals())
    ax, ay, ac = lax.axis_index("x"), lax.axis_index("y"), lax.axis_index("c")
    chip = 2 * ax + ay
    me = 2 * chip + ac
    depth, d = norm_g.shape
    n_ada = w_ada.shape[2]
    pick = lambda a, i, axis=0: lax.dynamic_index_in_dim(a, i, axis, keepdims=False)

    convs = jnp.stack([rg_conv_w, ml_conv_w])
    n_conv = 2 * depth * CONV_WIDTH // 4
    blk = jnp.concatenate([c, convs.reshape(n_conv, d), jnp.zeros((8 - 1 - n_conv, d), F32)], axis=0)
    g0 = _all_gather8([blk], pltpu.VMEM)[0].reshape(8, 8, d)
    c_all = g0[:, 0, :]
    conv_full = g0[0::2, 1:1 + n_conv].reshape(4, 2, depth, CONV_WIDTH, d // 4)
    conv_full = conv_full.transpose(1, 2, 3, 0, 4).reshape(2, depth, CONV_WIDTH, d)

    b_cols = lax.dynamic_slice_in_dim(b_ada, chip * n_ada, n_ada, axis=1)[:, None, :]
    mod_part = _ada_mod(c_all, w_ada, b_cols)
    g1 = _all_gather8([mod_part.transpose(1, 0, 2).reshape(8, depth * n_ada)], pltpu.VMEM)[0]
    g1 = g1.reshape(8, 8, depth, n_ada)[0::2]
    mod_me = pick(g1.transpose(1, 2, 0, 3).reshape(8, depth, 4 * n_ada), me)

    def half_of(w, axis):
        n = w.shape[axis] // 2
        return lax.dynamic_slice_in_dim(w, ac * n, n, axis).astype(BF16)

    n_sh = w_in.shape[2]
    heads, hd_cut, hd = ml_w_q.shape[1:]

    def blocks_of(l):
        wqkv = jnp.stack([ml_w_q[l], ml_w_k[l], ml_w_v[l]])
        return [half_of(w_in[l], 0), half_of(w_out[l], 0), half_of(wqkv, 2).reshape(-1, hd), half_of(ml_w_if[l], 0)]

    def layer_of(l, w4, rest):
        return dict(
            norm_g=norm_g[l][None], shift=mod_me[l, 0:d][None], scale=mod_me[l, d:2 * d][None],
            gate=mod_me[l, 2 * d:3 * d][None], w4=w4.reshape(4, d, n_sh),
            rg_conv_w=conv_full[0, l], rg_conv_b=rg_conv_b[l][None], rg_wa_b=_bf(rg_w_a[l]), rg_ba=rg_b_a[l][None],
            rg_wx_b=_bf(rg_w_x[l]), rg_bx=rg_b_x[l][None], rg_lam=rg_lambda[l][None],
            ml_conv_w=conv_full[1, l], ml_conv_b=ml_conv_b[l][None], b_if=ml_b_if[l][None], b_ift=ml_b_if[l][:, None],
            ml_g=ml_norm_g[l][None], **rest)

    def rest_of(gathered):
        w_out_b, wqkv_g, wif = gathered
        return dict(w_out_b=w_out_b, wqkv_b=_from_pieces(wqkv_g.reshape(8, -1), (3, heads, hd, hd), 2), wif_b=wif,
                    wift_b=wif.T)

    spread = lambda blocks: [Leg(b, "spread") for b in blocks]
    whole = lambda landed: [t.reshape(-1, t.shape[-1]) for t in _sib_fill(landed)]
    first = blocks_of(0)
    p = layer_of(0, _all_gather8(first[:1], pltpu.HBM)[0], {})
    rides = dict(ln_inproj=spread(first[1:]))
    late = lambda landed: rest_of(whole(landed))
    layers, saved = [], []
    xl = x[0]
    for l in range(depth):
        if l + 1 < depth:
            nxt = blocks_of(l + 1)
            rides.update(rg_fwd=spread(nxt[:1]), mlstm_fwd=spread(nxt[1:]))
        xl, s, p, got = _layer_fwd(xl, p, rides, late)
        layers.append(p)
        saved.append(s)
        if l + 1 < depth:
            nxt_whole = whole(list(got["rg_fwd"]) + list(got["mlstm_fwd"]))
            p = layer_of(l + 1, nxt_whole[0], rest_of(nxt_whole[1:]))
            rides, late = {}, None
    dx, g_final, loss = _final_loss(xl, final_g[None], loss_target[0])

    half = ac.reshape(1)
    ids = jnp.stack([chip, ac])
    r_out = w_out.shape[1] // 2

    def pair_in(g_w_in, got_in):
        return _pair_sum(
            half, g_w_in, pl.BlockSpec((None, d // 2, n_sh), lambda s, h: (0, h[0], s)),
            got_in, pl.BlockSpec((None, None, d // 2, n_sh), lambda s, h: (0, s, 0, 0)),
            jax.ShapeDtypeStruct((4, 1, d // 2, n_sh), BF16),
            pl.BlockSpec((None, None, d // 2, n_sh), lambda s, h: (s, 0, 0, 0)), (4,))

    def pair_out(g_out5, got_out):
        return _pair_sum(
            half, g_out5, pl.BlockSpec((None, None, None, r_out, d), lambda s, h: (0, s, h[0], 0, 0)),
            got_out, pl.BlockSpec((None, None, r_out, d), lambda s, h: (0, s, 0, 0)),
            jax.ShapeDtypeStruct((4, 1, r_out, d), BF16),
            pl.BlockSpec((None, None, r_out, d), lambda s, h: (s, 0, 0, 0)), (4,))

    def pair_slab(slab, got, dtype):
        rows = got.shape[0] // 4
        blk = pl.BlockSpec((rows, LANES), lambda s, h: (s, 0))
        return _pair_sum(half, slab, pl.BlockSpec((None, rows, LANES), lambda s, h: (h[0], s, 0)), got, blk,
                         jax.ShapeDtypeStruct((4 * rows, LANES), dtype), blk, (4,)).reshape(4, 1, rows, LANES)

    row_pad = lambda n: -(-n // (8 * LANES)) * (8 * LANES)

    def as_rows(t):
        if t.shape[-1] == LANES and t.size % (8 * LANES) == 0:
            return t.reshape(-1, LANES)
        return _pad_rows(t.reshape(-1), 8 * LANES).reshape(-1, LANES)

    chips = lambda arrs: [Leg(a, "chips") for a in arrs]
    out5 = lambda g: g["w_out"].reshape(1, 4, 2, r_out, d)
    grads, dmods, parts, mets = [None] * depth, [None] * depth, [None] * depth, [None] * depth
    small = {}

    def early_exchange(g, landed):
        every = [g] + grads[1:]
        sm = jnp.concatenate([_to_pieces(every[l][name], axis) for l in range(depth)
                              for name, axis in SMALL_SHARDED.items()], axis=-1)
        sm = _pad_rows(sm, 16 * LANES)
        sm = sm.transpose(1, 0, 2).reshape(2, -1, LANES)
        rep = [as_rows(every[l][name]) for l in range(depth) for name in REPLICATED]
        rep = jnp.concatenate(rep + [as_rows(g_final), as_rows(loss)], axis=0)
        rep = jnp.concatenate([rep, jnp.zeros(((-rep.shape[0]) % 64, LANES), F32)], axis=0)
        rep = rep.reshape(4, 2, -1, LANES).transpose(1, 0, 2, 3).reshape(2, -1, LANES)
        got_sm, got_rep = _exchange([Leg(sm, "sib_slab"), Leg(rep, "sib_slab")])
        small["parts"] = [pair_out(out5(g), landed["mlstm_bwd"][0]), pair_slab(sm, got_sm, BF16),
                          pair_slab(rep, got_rep, F32)]
        return chips(small["parts"])

    def last_exchange(g, landed):
        (got_in,) = _exchange([Leg(g["w_in"], "sib_w_in")])
        small["part_in"] = pair_in(g["w_in"], got_in)
        return chips([small["part_in"]])

    for l in reversed(range(depth)):
        above = parts[l + 1] if l + 1 < depth else []
        rides = dict(mlstm_bwd=lambda g, landed, above=above: [Leg(out5(g), "sib_w_out")] + chips(above),
                     in_bwd=lambda g, landed: [Leg(g["w_in"], "sib_w_in")])
        if l == 0:
            rides.update(grad_w_in=early_exchange, in_bwd=last_exchange)
        dx, grads[l], dmods[l], got = _layer_bwd(dx, layers[l], saved[l], rides)
        if above:
            mets[l + 1] = got["mlstm_bwd"][1:]
        if l > 0:
            parts[l] = [pair_in(grads[l]["w_in"], got["in_bwd"][0]), pair_out(out5(grads[l]), got["mlstm_bwd"][0])]
    part_out, part_sm, part_rep = small["parts"]
    met_out, met_sm, met_rep = got["grad_w_in"]
    parts[0], mets[0] = [small["part_in"], part_out], [got["in_bwd"][0], met_out]
    n_rep = part_rep.shape[2]

    pad = lambda t: jnp.concatenate([t, jnp.zeros((1, 2 * d), F32)], axis=1)
    rows = [r for l in range(depth) for r in (dmods[l], pad(grads[l]["norm_g"]))]
    blk = jnp.concatenate(rows + [jnp.zeros((8 - 2 * depth, 3 * d), F32)], axis=0)
    rows_all = _all_gather8([blk], pltpu.VMEM)[0].reshape(8, 8, 3 * d)[:, :2 * depth]
    rows_all = rows_all.transpose(1, 0, 2).reshape(depth, 2, 8, 3 * d)
    dm_cols = lax.dynamic_slice_in_dim(rows_all[:, 0], chip * n_ada, n_ada, axis=2)
    g_w_ada, summed = _ada_grad(c_all, dm_cols, rows_all)

    g = dict(w_ada=g_w_ada, b_ada=summed[:, 0, 0], norm_g=summed[:, 1, 0, :d])
    item = lambda name: (given[name], g[name], given["m_" + name], given["v_" + name])
    both_in, both_out = depth, depth
    for l in range(depth):
        both_in = _chip_sum(ids, parts[l][0], mets[l][0], True, l, both_in)
        both_out = _chip_sum(ids, parts[l][1], mets[l][1], True, l, both_out)
    both_in, both_out, both_sm = _sib_fill([both_in, both_out, _chip_sum(ids, part_sm, met_sm, True)])
    red_rep = _chip_sum(ids, part_rep, met_rep, False).reshape(n_rep, LANES)
    rep_all = _all_gather8([red_rep], pltpu.VMEM)[0].reshape(-1)

    g.update(w_in=both_in.reshape(w_in.shape), w_out=both_out.reshape(w_out.shape))
    shard = both_sm.reshape(2, -1)
    off = 0
    per_layer = {name: [] for name in SMALL_SHARDED}
    for l in range(depth):
        for name, axis in SMALL_SHARDED.items():
            shp = (3,) + ml_w_q.shape[1:] if name == "ml_w_qkv" else given[name].shape[1:]
            n = grads[l][name].size // 8
            per_layer[name].append(_from_pieces(shard[:, off:off + n], shp, axis))
            off += n
    for name in SMALL_SHARDED:
        g[name] = jnp.stack(per_layer[name])
    for i, name in enumerate(["ml_w_q", "ml_w_k", "ml_w_v"]):
        g[name] = g["ml_w_qkv"][:, i]
    off = 0
    per_layer = {name: [] for name in REPLICATED}
    for l in range(depth):
        for name in REPLICATED:
            n = given[name][l].size
            per_layer[name].append(rep_all[off:off + n].reshape(given[name].shape[1:]))
            off += row_pad(n)
    for name in REPLICATED:
        g[name] = jnp.stack(per_layer[name])
    g["final_g"] = rep_all[off:off + d]
    loss_all = rep_all[off + row_pad(d)]

    stepped = {}
    rg_mats, ml_mats = ["rg_w_a", "rg_w_x"], ["ml_w_q", "ml_w_k", "ml_w_v"]
    vectors = [n for n in WEIGHTS if n not in ["w_ada", "w_in", "w_out"] + rg_mats + ml_mats]
    for names in (["w_ada"], ["w_in"], ["w_out"], rg_mats, ml_mats, vectors):
        stepped.update(zip(names, _adamw([item(name) for name in names])[0]))
    deltas, new_m, new_v = zip(*[stepped[name] for name in WEIGHTS])
    return (loss_all, dx[None], *[g[name] for name in WEIGHTS], *deltas, *new_m, *new_v)
```

```python
import functools
from typing import NamedTuple

import jax
import jax.numpy as jnp
from jax import lax
from jax.experimental import pallas as pl
from jax.experimental.pallas import tpu as pltpu

F32 = jnp.float32
BF16 = jnp.bfloat16

EPS = 1e-6
RG_C = 8.0
CONV_WIDTH = 4
ML_CHUNK = 512
HALO = 8
ADAM_LR = 0.001
ADAM_B1 = 0.9
ADAM_B2 = 0.999
ADAM_EPS = 1e-08
ADAM_WD = 0.01
ADAM_STEP = 10
MESH = pl.DeviceIdType.MESH


def _pcall(body, **kw):
    return pl.pallas_call(body, **kw)


class Leg(NamedTuple):
    src: jax.Array
    kind: str

    def landing(self):
        a = self.src
        shape = {"chips": lambda: a.shape, "spread": lambda: (4, 2) + a.shape, "sib_fill": lambda: a.shape,
                 "sib_w_in": lambda: (a.shape[0], 4, a.shape[1] // 2, a.shape[2] // 4),
                 "sib_w_out": lambda: a.shape[:2] + a.shape[3:], "sib_slab": lambda: a.shape[1:]}[self.kind]()
        return jax.ShapeDtypeStruct(shape, a.dtype)

    def copies(self, src, dst, x, y, c):
        a, me_s, o = self.src, 2 * x + y, 1 - c
        chips = [(1 - x, y), (x, 1 - y), (1 - x, 1 - y)]
        if self.kind == "chips":
            return [(src.at[2 * px + py], dst.at[me_s], (px, py, c)) for px, py in chips], []
        if self.kind == "spread":
            return [(src, dst.at[me_s, c], (px, py, c)) for px, py in chips], [(src, dst.at[me_s, c])]
        depth = pl.ds(0, a.shape[0])
        if self.kind == "sib_fill":
            return [(dst.at[depth, c], dst.at[depth, c], (x, y, o))], []
        if self.kind == "sib_w_in":
            half, n = a.shape[1] // 2, a.shape[2] // 4
            return [(src.at[depth, pl.ds(o * half, half), pl.ds(s * n, n)], dst.at[depth, s], (x, y, o))
                    for s in range(4)], []
        if self.kind == "sib_w_out":
            return [(src.at[depth, pl.ds(0, 4), o], dst, (x, y, o))], []
        return [(src.at[o], dst, (x, y, o))], []

    def n_copies(self):
        return {"chips": 3, "spread": 3, "sib_w_in": 4}.get(self.kind, 1)


def _exchange_body(legs, srcs, dsts, send_sems, recv_sems, local_sems):
    x, y, c = _me()
    remote, local, k = [], [], 0
    for i, leg in enumerate(legs):
        far, near = leg.copies(srcs[i], dsts[i], x, y, c)
        for src, dst, to in far:
            remote.append(_remote(src, dst, send_sems.at[k], recv_sems.at[k], to))
            k += 1
        local += [pltpu.make_async_copy(src, dst, local_sems.at[i]) for src, dst in near]
    return remote, local


def _exchange_sems(legs):
    n = sum(leg.n_copies() for leg in legs)
    return [pltpu.SemaphoreType.DMA((n,)), pltpu.SemaphoreType.DMA((n,)), pltpu.SemaphoreType.DMA((len(legs),))]


def _exchange_aliases(legs, n_in, n_out):
    return {n_in + i: n_out + i for i, leg in enumerate(legs) if leg.kind == "sib_fill"}


def _pcall_ride(body, ride, *, grid, in_specs, out_specs, out_shape, args, scratch_shapes=(), **kw):
    n_in, n_out, n_scr = len(in_specs), len(out_specs), len(scratch_shapes)
    if not ride:
        res = _pcall(body, grid=grid, in_specs=in_specs, out_specs=out_specs, out_shape=out_shape,
                     scratch_shapes=list(scratch_shapes), **kw)(*args)
        return res, []
    nr = len(ride)

    def riding(*refs):
        ins, rsrc = refs[:n_in], refs[n_in:n_in + nr]
        outs, rdst = refs[n_in + nr:n_in + nr + n_out], refs[n_in + nr + n_out:n_in + 2 * nr + n_out]
        scr = refs[n_in + 2 * nr + n_out:n_in + 2 * nr + n_out + n_scr]
        copies, local = _exchange_body(ride, rsrc, rdst, *refs[n_in + 2 * nr + n_out + n_scr:])
        first = functools.reduce(jnp.logical_and, [pl.program_id(a) == 0 for a in range(len(grid))])
        last = functools.reduce(jnp.logical_and, [pl.program_id(a) == grid[a] - 1 for a in range(len(grid))])

        @pl.when(first)
        def _():
            for cp in copies + local:
                cp.start()

        body(*ins, *outs, *scr)

        @pl.when(last)
        def _():
            for cp in copies:
                cp.wait_recv()
            for cp in copies:
                cp.wait_send()
            for cp in local:
                cp.wait()

    hbm = pl.BlockSpec(memory_space=pltpu.HBM)
    aliases = {**kw.pop("input_output_aliases", {}), **_exchange_aliases(ride, n_in, n_out)}
    res = _pcall(
        riding, grid=grid, in_specs=list(in_specs) + [hbm] * nr, out_specs=list(out_specs) + [hbm] * nr,
        out_shape=list(out_shape) + [leg.landing() for leg in ride], input_output_aliases=aliases,
        scratch_shapes=list(scratch_shapes) + _exchange_sems(ride), **kw)(*args, *[leg.src for leg in ride])
    return res[:n_out], res[n_out:]


def _seq(n=1):
    return pltpu.CompilerParams(dimension_semantics=("arbitrary",) * n)


def _dot(a, b):
    return jnp.dot(a, b, preferred_element_type=F32)


def _dot_nt(a, b):
    return lax.dot_general(a, b, (((1,), (1,)), ((), ())), preferred_element_type=F32)


def _dot_tn(a, b):
    return lax.dot_general(a, b, (((0,), (0,)), ((), ())), preferred_element_type=F32)


def _bf(x):
    return x.astype(BF16)


def _sigmoid(x):
    return 0.5 * jnp.tanh(0.5 * x) + 0.5


def _log1p(z):
    u = 1.0 + z
    return jnp.where(u == 1.0, z, jnp.log(u) * (z / jnp.where(u == 1.0, 1.0, u - 1.0)))


def _softplus(x):
    return jnp.maximum(x, 0.0) + _log1p(jnp.exp(-jnp.abs(x)))


def _log_sigmoid(x):
    return -_softplus(-x)


def _one_minus_sq(a, log_a):
    x = 2.0 * log_a
    small = -x * (1.0 + x * (0.5 + x * (1.0 / 6.0)))
    return jnp.where(x > -0.004, small, 1.0 - a * a)


def _dsilu(x, s):
    return s * (1.0 + x * (1.0 - s))


def _rowsum(x):
    return jnp.sum(x, axis=1, keepdims=True)


def _colsum(x):
    return jnp.sum(x, axis=0, keepdims=True)


def _shift_down(win, s):
    return win if s == 0 else pltpu.roll(win, s, 0)


def _shift_up(win, s):
    return win if s == 0 else pltpu.roll(win, win.shape[0] - s, 0)


def _conv_taps(win):
    return [_shift_down(win, CONV_WIDTH - 1 - k)[HALO:] for k in range(CONV_WIDTH)]


def _conv_fwd(taps, w_ref, b_ref):
    acc = b_ref[...] + w_ref[CONV_WIDTH - 1:CONV_WIDTH, :] * taps[CONV_WIDTH - 1]
    for k in range(CONV_WIDTH - 1):
        acc = acc + w_ref[k:k + 1, :] * taps[k]
    return acc


def _split3(x):
    hi = _bf(x)
    r1 = x - hi.astype(F32)
    mid = _bf(r1)
    lo = _bf(r1 - mid.astype(F32))
    return hi, mid, lo


def _tri_dot_left(tri, x):
    hi, mid, lo = _split3(x)
    return _dot(tri, hi) + _dot(tri, mid) + _dot(tri, lo)


def _tri_dot_right(x, tri):
    hi, mid, lo = _split3(x)
    return _dot(hi, tri) + _dot(mid, tri) + _dot(lo, tri)


def _tile(n, want):
    t = min(n, want)
    assert n % t == 0
    return t


def _ln_inproj(x, g, scale, shift, w4, ride=None):
    s_len, d = x.shape
    nj, _, nsh = w4.shape
    tm = _tile(s_len, 1024)

    def body(x_ref, g_ref, sc_ref, sh_ref, w_ref, h_ref, u_ref, hs):
        @pl.when(pl.program_id(1) == 0)
        def _():
            xv = x_ref[...]
            r = lax.rsqrt(jnp.mean(xv * xv, axis=-1, keepdims=True) + EPS)
            hv = (xv * r * g_ref[...]) * (1.0 + sc_ref[...]) + sh_ref[...]
            hs[...] = _bf(hv)
            h_ref[...] = hs[...]

        u_ref[...] = _dot(hs[...], w_ref[0])

    vec = pl.BlockSpec((1, d), lambda i, j: (0, 0))
    return _pcall_ride(
        body, ride, name="ln_inproj", grid=(s_len // tm, nj),
        in_specs=[pl.BlockSpec((tm, d), lambda i, j: (i, 0)), vec, vec, vec,
                  pl.BlockSpec((1, d, nsh), lambda i, j: (j, 0, 0))],
        out_specs=[pl.BlockSpec((tm, d), lambda i, j: (i, 0)), pl.BlockSpec((tm, nsh), lambda i, j: (i, j))],
        out_shape=[jax.ShapeDtypeStruct((s_len, d), BF16), jax.ShapeDtypeStruct((s_len, nj * nsh), F32)],
        scratch_shapes=[pltpu.VMEM((tm, d), BF16)],
        compiler_params=_seq(2),
        args=(x, g, scale, shift, w4))


def _rg_gates(xc, wa_ref, ba_ref, wx_ref, bx_ref, lam_ref):
    heads, hd, _ = wa_ref.shape
    xb = _bf(xc)
    ga = jnp.concatenate([_dot(xb[:, h * hd:(h + 1) * hd], wa_ref[h]) for h in range(heads)], axis=1) + ba_ref[...]
    gx = jnp.concatenate([_dot(xb[:, h * hd:(h + 1) * hd], wx_ref[h]) for h in range(heads)], axis=1) + bx_ref[...]
    r = _sigmoid(ga)
    ig = _sigmoid(gx)
    sp = _softplus(-lam_ref[...])
    log_a = (-RG_C) * r * sp
    a = jnp.exp(log_a)
    mult = jnp.sqrt(_one_minus_sq(a, log_a))
    return r, ig, sp, log_a, a, mult


def _scan_groups(a, u, reverse):
    n, c = a.shape
    a = a.reshape(n // 8, 8, c)
    u = u.reshape(n // 8, 8, c)
    row = lax.broadcasted_iota(jnp.int32, a.shape, 1)
    for k in (1, 2, 4):
        sft = 8 - k if reverse else k
        a_sh, u_sh = pltpu.roll(a, sft, 1), pltpu.roll(u, sft, 1)
        ok = row < 8 - k if reverse else row >= k
        u = jnp.where(ok, a * u_sh + u, u)
        a = jnp.where(ok, a * a_sh, a)
    return a.reshape(n, c), u.reshape(n, c)


def _rg_fwd(u, conv_w, conv_b, wa_b, ba, wx_b, bx, lam, ride=None):
    s_len = u.shape[0]
    d = conv_w.shape[1]
    tm = _tile(s_len, 256)
    per = tm // HALO

    def body(x_ref, xp_ref, z_ref, cw_ref, cb_ref, wa_ref, ba_ref, wx_ref, bx_ref, lam_ref,
             hh_ref, y_ref, carry):
        i = pl.program_id(0)

        @pl.when(i == 0)
        def _():
            carry[...] = jnp.zeros_like(carry)

        prev = jnp.where(i == 0, 0.0, xp_ref[...])
        xc = _conv_fwd(_conv_taps(jnp.concatenate([prev, x_ref[...]], axis=0)), cw_ref, cb_ref)
        _, ig, _, _, a, mult = _rg_gates(xc, wa_ref, ba_ref, wx_ref, bx_ref, lam_ref)
        ca, cu = _scan_groups(a, mult * (ig * xc), reverse=False)
        c = carry[0:1, :]
        for j in range(per):
            blk = ca[j * 8:(j + 1) * 8] * c + cu[j * 8:(j + 1) * 8]
            hh_ref[j * 8:(j + 1) * 8, :] = blk
            c = blk[7:8]
        carry[0:1, :] = c
        z = z_ref[...]
        y_ref[0] = _bf(hh_ref[...] * (z * _sigmoid(z)))

    vec = pl.BlockSpec((1, d), lambda i: (0, 0))
    whole3 = lambda a: pl.BlockSpec(a.shape, lambda i: (0, 0, 0))
    return _pcall_ride(
        body, ride, name="rg_fwd", grid=(s_len // tm,),
        in_specs=[pl.BlockSpec((tm, d), lambda i: (i, 0)),
                  pl.BlockSpec((HALO, d), lambda i: (jnp.maximum(i * per - 1, 0), 0)),
                  pl.BlockSpec((tm, d), lambda i: (i, 1)),
                  pl.BlockSpec((CONV_WIDTH, d), lambda i: (0, 0)), vec,
                  whole3(wa_b), vec, whole3(wx_b), vec, vec],
        out_specs=[pl.BlockSpec((tm, d), lambda i: (i, 0)), pl.BlockSpec((1, tm, d), lambda i: (0, i, 0))],
        out_shape=[jax.ShapeDtypeStruct((s_len, d), F32), jax.ShapeDtypeStruct((2, s_len, d), BF16)],
        scratch_shapes=[pltpu.VMEM((8, d), F32)],
        compiler_params=_seq(),
        args=(u, u, u, conv_w, conv_b, wa_b, ba, wx_b, bx, lam))


def _ml_pre(u, conv_w, conv_b, wqkv_b, wif_b, wift_b, b_if, b_ift):
    s_len = u.shape[0]
    d = conv_w.shape[1]
    _, heads, hd, _ = wqkv_b.shape
    ng = 2 * heads
    tm = _tile(s_len, max(256, ML_CHUNK))
    per = tm // HALO

    def body(x_ref, xp_ref, cw_ref, cb_ref, w_ref, wif_ref, wift_ref, bif_ref, bift_ref,
             qkv_ref, gt_ref, gtt_ref, bc_ref, bct_ref):
        i = pl.program_id(0)
        prev = jnp.where(i == 0, 0.0, xp_ref[...])
        xm = x_ref[...]
        pre = _conv_fwd(_conv_taps(jnp.concatenate([prev, xm], axis=0)), cw_ref, cb_ref)
        xcb = _bf(pre * _sigmoid(pre))
        xmb = _bf(xm)
        for h in range(heads):
            hs = slice(h * hd, (h + 1) * hd)
            qkv_ref[0, :, hs] = _bf(_dot(xcb[:, hs], w_ref[0, h]))
            qkv_ref[1, :, hs] = _bf(_dot(xcb[:, hs], w_ref[1, h]))
            qkv_ref[2, :, hs] = _bf(_dot(xmb[:, hs], w_ref[2, h]))
        qb, kb, vb = qkv_ref[0], qkv_ref[1], qkv_ref[2]
        gt = (_dot(qb, wif_ref[0:d, :]) + _dot(kb, wif_ref[d:2 * d, :]) + _dot(vb, wif_ref[2 * d:3 * d, :])
              + bif_ref[...])
        gtt = (_dot_nt(wift_ref[:, 0:d], qb) + _dot_nt(wift_ref[:, d:2 * d], kb)
               + _dot_nt(wift_ref[:, 2 * d:3 * d], vb) + bift_ref[...])
        gt_ref[...] = gt
        gtt_ref[...] = gtt
        r = lax.broadcasted_iota(jnp.int32, (tm, tm), 0)
        c = lax.broadcasted_iota(jnp.int32, (tm, tm), 1)
        same = (r // ML_CHUNK) == (c // ML_CHUNK)
        bc_ref[...] = _tri_dot_left(((r >= c) & same).astype(BF16), _log_sigmoid(gt))
        bct_ref[...] = _tri_dot_right(_log_sigmoid(gtt), ((r <= c) & same).astype(BF16))

    vec = pl.BlockSpec((1, d), lambda i: (0, 0))
    whole2 = lambda a: pl.BlockSpec(a.shape, lambda i: (0, 0))
    col = pl.BlockSpec((tm, ng), lambda i: (i, 0))
    row = pl.BlockSpec((ng, tm), lambda i: (0, i))
    return _pcall(
        body, name="ml_pre", grid=(s_len // tm,),
        in_specs=[pl.BlockSpec((tm, d), lambda i: (i, 2)),
                  pl.BlockSpec((HALO, d), lambda i: (jnp.maximum(i * per - 1, 0), 2)),
                  pl.BlockSpec((CONV_WIDTH, d), lambda i: (0, 0)), vec,
                  pl.BlockSpec(wqkv_b.shape, lambda i: (0, 0, 0, 0)), whole2(wif_b), whole2(wift_b), whole2(b_if),
                  whole2(b_ift)],
        out_specs=[pl.BlockSpec((3, tm, d), lambda i: (0, i, 0)), col, row, col, row],
        out_shape=[jax.ShapeDtypeStruct((3, s_len, d), BF16), jax.ShapeDtypeStruct((s_len, ng), F32),
                   jax.ShapeDtypeStruct((ng, s_len), F32), jax.ShapeDtypeStruct((s_len, ng), F32),
                   jax.ShapeDtypeStruct((ng, s_len), F32)],
        compiler_params=_seq(),
    )(u, u, conv_w, conv_b, wqkv_b, wif_b, wift_b, b_if, b_ift)


def _chunk_gates(gt, gtt, bc, bct, h, heads):
    li_c = gt[:, h:h + 1]
    li_r = gtt[h:h + 1, :]
    gf_c = gt[:, heads + h:heads + h + 1]
    b_c = bc[:, heads + h:heads + h + 1]
    b_r = bct[heads + h:heads + h + 1, :]
    return li_c, li_r, gf_c, b_c, b_r


def _chunk_weights(li_c, li_r, b_c, b_r, m_prev, causal):
    lc = b_c.shape[0]
    b_last = b_c[lc - 1:lc, :]
    dmat = jnp.where(causal, b_c - b_r + li_r, -jnp.inf)
    m_inter = b_c + m_prev
    m_t = jnp.maximum(m_inter, jnp.max(dmat, axis=1, keepdims=True))
    w_intra = jnp.exp(dmat - m_t)
    w_inter = jnp.exp(m_inter - m_t)
    g_c = b_last - b_c + li_c
    m_new = jnp.maximum(b_last + m_prev, jnp.max(g_c, axis=0, keepdims=True))
    w_state = jnp.exp(g_c - m_new)
    decay = jnp.exp(b_last + m_prev - m_new)
    return m_t, w_intra, w_inter, m_new, w_state, decay


def _tri_masks(lc):
    r = lax.broadcasted_iota(jnp.int32, (lc, lc), 0)
    c = lax.broadcasted_iota(jnp.int32, (lc, lc), 1)
    causal = r >= c
    return causal, causal.astype(BF16), (r <= c).astype(BF16)


def _mlstm_fwd(qkv, gates, u, ml_g, ycat, ride=None):
    _, s_len, d = qkv.shape
    ng = gates[0].shape[1]
    heads = ng // 2
    hd = d // heads
    lc = ML_CHUNK
    nc = s_len // lc
    kscale = hd ** -0.5

    def body(qkv_ref, gt_ref, gtt_ref, bc_ref, bct_ref, o_ref, z_ref, g_ref, _, cell_ref, y_ref, cst_ref, nst_ref,
             mst_ref, cs, ns, ms):
        @pl.when(pl.program_id(0) == 0)
        def _():
            cs[...] = jnp.zeros_like(cs)
            ns[...] = jnp.zeros_like(ns)
            ms[...] = jnp.zeros_like(ms)

        causal = _tri_masks(lc)[0]
        gtv, gttv, bcv, bctv = gt_ref[...], gtt_ref[...], bc_ref[...], bct_ref[...]
        old = [(cs[h], ns[h], ms[h]) for h in range(heads)]
        new, cells, ys = [], [], []
        for h in range(heads):
            hs = slice(h * hd, (h + 1) * hd)
            li_c, li_r, _, b_c, b_r = _chunk_gates(gtv, gttv, bcv, bctv, h, heads)
            c_old, n_old, m_old = old[h]
            m_prev = m_old[:, 0:1]
            m_t, w_intra, w_inter, m_new, w_state, decay = _chunk_weights(li_c, li_r, b_c, b_r, m_prev, causal)
            qb = qkv_ref[0, :, hs]
            ks = qkv_ref[1, :, hs].astype(F32) * kscale
            kb = _bf(ks)
            vb = qkv_ref[2, :, hs]
            s = _dot_nt(qb, kb) * w_intra
            num = _dot(_bf(s), vb) + w_inter * _dot(qb, _bf(c_old))
            den = _rowsum(s) + w_inter * _rowsum(qb.astype(F32) * n_old)
            cell = num / jnp.maximum(jnp.abs(den), jnp.exp(-m_t))
            kw = ks * w_state
            new.append((decay * c_old + _dot_tn(_bf(kw), vb), decay * n_old + _colsum(kw),
                        jnp.broadcast_to(m_new, m_old.shape)))
            cells.append(cell)
            hm = _sigmoid(o_ref[:, hs]) * cell
            hn = hm * lax.rsqrt(jnp.mean(hm * hm, axis=-1, keepdims=True) + EPS)
            z = z_ref[:, hs]
            ys.append(_bf((hn * g_ref[:, hs]) * (z * _sigmoid(z))))
        for h in range(heads):
            cst_ref[0, h] = _bf(old[h][0])
            nst_ref[0, h] = old[h][1]
            mst_ref[0, h] = old[h][2]
            cs[h], ns[h], ms[h] = new[h]
        cell_ref[...] = jnp.concatenate(cells, axis=1)
        y_ref[0] = jnp.concatenate(ys, axis=1)

    row = pl.BlockSpec((lc, d), lambda c: (c, 0))
    gcol = pl.BlockSpec((lc, ng), lambda c: (c, 0))
    grow = pl.BlockSpec((ng, lc), lambda c: (0, c))
    return _pcall_ride(
        body, ride, name="mlstm_fwd", grid=(nc,),
        in_specs=[pl.BlockSpec((3, lc, d), lambda c: (0, c, 0)), gcol, grow, gcol, grow,
                  pl.BlockSpec((lc, d), lambda c: (c, 3)), pl.BlockSpec((lc, d), lambda c: (c, 4)),
                  pl.BlockSpec((1, d), lambda c: (0, 0)), pl.BlockSpec(memory_space=pl.ANY)],
        out_specs=[row, pl.BlockSpec((1, lc, d), lambda c: (1, c, 0)),
                   pl.BlockSpec((1, heads, hd, hd), lambda c: (c, 0, 0, 0)),
                   pl.BlockSpec((1, heads, 1, hd), lambda c: (c, 0, 0, 0)),
                   pl.BlockSpec((1, heads, 1, 128), lambda c: (c, 0, 0, 0))],
        out_shape=[jax.ShapeDtypeStruct((s_len, d), F32), jax.ShapeDtypeStruct(ycat.shape, BF16),
                   jax.ShapeDtypeStruct((nc, heads, hd, hd), BF16),
                   jax.ShapeDtypeStruct((nc, heads, 1, hd), F32),
                   jax.ShapeDtypeStruct((nc, heads, 1, 128), F32)],
        scratch_shapes=[pltpu.VMEM((heads, hd, hd), F32), pltpu.VMEM((heads, 1, hd), F32),
                        pltpu.VMEM((heads, 1, 128), F32)],
        input_output_aliases={8: 1},
        compiler_params=_seq(),
        args=(qkv, *gates, u, u, ml_g, ycat))


def _out_proj(ycat, w_out_b, x, gate, ride=None):
    s_len, d = x.shape
    tm = _tile(s_len, 1024)

    def body(a_ref, w_ref, x_ref, g_ref, y_ref, xn_ref):
        y = _dot(a_ref[0], w_ref[0:d, :]) + _dot(a_ref[1], w_ref[d:2 * d, :])
        y_ref[...] = y
        xn_ref[...] = x_ref[...] + g_ref[...] * y

    row = pl.BlockSpec((tm, d), lambda i: (i, 0))
    return _pcall_ride(
        body, ride, name="out_proj", grid=(s_len // tm,),
        in_specs=[pl.BlockSpec((2, tm, d), lambda i: (0, i, 0)), pl.BlockSpec((2 * d, d), lambda i: (0, 0)), row,
                  pl.BlockSpec((1, d), lambda i: (0, 0))],
        out_specs=[row, row],
        out_shape=[jax.ShapeDtypeStruct((s_len, d), F32)] * 2,
        compiler_params=_seq(),
        args=(ycat, w_out_b, x, gate))


def _final_loss(x, g, target):
    s_len, d = x.shape
    tm = _tile(s_len, 256)

    def body(x_ref, g_ref, t_ref, dx_ref, dg_ref, loss_ref):
        @pl.when(pl.program_id(0) == 0)
        def _():
            dg_ref[...] = jnp.zeros_like(dg_ref)
            loss_ref[...] = jnp.zeros_like(loss_ref)

        xv = x_ref[...]
        r = lax.rsqrt(jnp.mean(xv * xv, axis=-1, keepdims=True) + EPS)
        xn = xv * r
        err = xn * g_ref[...] - t_ref[...]
        loss_ref[...] += 0.5 * jnp.sum(jnp.mean(err * err, axis=-1, keepdims=True))
        dout = err * (1.0 / d)
        dg_ref[...] += _colsum(dout * xn)
        dxn = dout * g_ref[...]
        dx_ref[...] = r * (dxn - xn * jnp.mean(dxn * xn, axis=-1, keepdims=True))

    row = pl.BlockSpec((tm, d), lambda i: (i, 0))
    vec = pl.BlockSpec((1, d), lambda i: (0, 0))
    return _pcall(
        body, name="final_loss", grid=(s_len // tm,),
        in_specs=[row, vec, row],
        out_specs=[row, vec, pl.BlockSpec((1, 128), lambda i: (0, 0))],
        out_shape=[jax.ShapeDtypeStruct((s_len, d), F32), jax.ShapeDtypeStruct((1, d), F32),
                   jax.ShapeDtypeStruct((1, 128), F32)],
        compiler_params=_seq(),
    )(x, g, target)


def _out_bwd(dxn, y, gate, w_out_b):
    s_len, d = dxn.shape
    tm = _tile(s_len, 1024)

    def body(dx_ref, y_ref, g_ref, w_ref, dg_ref, dy_ref, dc_ref):
        @pl.when(pl.program_id(0) == 0)
        def _():
            dg_ref[...] = jnp.zeros_like(dg_ref)

        dx = dx_ref[...]
        dg_ref[...] += _colsum(dx * y_ref[...])
        dy = _bf(g_ref[...] * dx)
        dy_ref[...] = dy
        dc_ref[0] = _dot_nt(dy, w_ref[0:d, :])
        dc_ref[1] = _dot_nt(dy, w_ref[d:2 * d, :])

    row = pl.BlockSpec((tm, d), lambda i: (i, 0))
    vec = pl.BlockSpec((1, d), lambda i: (0, 0))
    return _pcall(
        body, name="out_bwd", grid=(s_len // tm,),
        in_specs=[row, row, vec, pl.BlockSpec((2 * d, d), lambda i: (0, 0))],
        out_specs=[vec, row, pl.BlockSpec((2, tm, d), lambda i: (0, i, 0))],
        out_shape=[jax.ShapeDtypeStruct((1, d), F32), jax.ShapeDtypeStruct((s_len, d), BF16),
                   jax.ShapeDtypeStruct((2, s_len, d), F32)],
        compiler_params=_seq(),
    )(dxn, y, gate, w_out_b)


def _grad_matmul(a3, b3, nblk, a_idx, b_idx, out_shape, out_block, out_idx, ride=None):
    _, s_len, m = a3.shape
    n = b3.shape[2]
    tk = _tile(s_len, 2048)

    def body(a_ref, b_ref, o_ref):
        @pl.when(pl.program_id(1) == 0)
        def _():
            o_ref[...] = jnp.zeros_like(o_ref)

        o_ref[...] += _dot_tn(a_ref[0], b_ref[0])

    (out,), got = _pcall_ride(
        body, ride, name="grad_matmul", grid=(nblk, s_len // tk),
        in_specs=[pl.BlockSpec((1, tk, m), lambda p, t: (a_idx(p), t, 0)),
                  pl.BlockSpec((1, tk, n), lambda p, t: (b_idx(p), t, 0))],
        out_specs=[pl.BlockSpec((None,) + out_block, lambda p, t: (0,) + out_idx(p))],
        out_shape=[jax.ShapeDtypeStruct((1,) + out_shape, F32)],
        compiler_params=_seq(2), args=(a3, b3))
    return out, got


DU_PLANE = (2, 3, 4, 0, 1)


def _mlstm_bwd(qkv, gates, cst, nst, mst, cell, u, ml_g, d_ycat, wif_b, ride=None):
    _, s_len, d = qkv.shape
    ng = gates[0].shape[1]
    heads = ng // 2
    hd = d // heads
    lc = ML_CHUNK
    nc = s_len // lc
    kscale = hd ** -0.5

    def body(qkv_ref, gt_ref, gtt_ref, bc_ref, bct_ref, cst_ref, nst_ref, mst_ref, cell_ref, o_ref, z_ref, g_ref, dy_ref,
             wif_ref, dqkv_ref, dgt_ref, dbif_ref, du_ref, dg_ref, dcs, dns):
        @pl.when(pl.program_id(0) == 0)
        def _():
            dbif_ref[...] = jnp.zeros_like(dbif_ref)
            dcs[...] = jnp.zeros_like(dcs)
            dns[...] = jnp.zeros_like(dns)
            dg_ref[...] = jnp.zeros_like(dg_ref)

        causal, tril, triu = _tri_masks(lc)
        tril_strict = (tril.astype(F32) - (tril * triu).astype(F32)).astype(BF16)
        gtv, gttv, bcv, bctv = gt_ref[...], gtt_ref[...], bc_ref[...], bct_ref[...]
        lane = lax.broadcasted_iota(jnp.int32, (lc, ng), 1)
        dli_all = jnp.zeros((lc, ng), F32)
        from_later = jnp.zeros((lc, ng), F32)
        from_earlier = jnp.zeros((lc, ng), F32)
        across_all = jnp.zeros((1, ng), F32)
        old = [(dcs[h], dns[h]) for h in range(heads)]
        new, d_o, d_z, d_g, dqs, dks, dvs = [], [], [], [], [], [], []
        for h in range(heads):
            hs = slice(h * hd, (h + 1) * hd)
            li_c, li_r, gf_c, b_c, b_r = _chunk_gates(gtv, gttv, bcv, bctv, h, heads)
            m_prev = mst_ref[0, h][:, 0:1]
            m_t, w_intra, w_inter, _, w_state, decay = _chunk_weights(li_c, li_r, b_c, b_r, m_prev, causal)
            qb = qkv_ref[0, :, hs]
            qf = qb.astype(F32)
            ks = qkv_ref[1, :, hs].astype(F32) * kscale
            kb = _bf(ks)
            vb = qkv_ref[2, :, hs]
            c_b = cst_ref[0, h]
            n_old = nst_ref[0, h]
            s = _dot_nt(qb, kb) * w_intra
            den = _rowsum(s) + w_inter * _rowsum(qf * n_old)
            floor = jnp.exp(-m_t)
            dstab = jnp.maximum(jnp.abs(den), floor)
            cell = cell_ref[:, hs]
            o = o_ref[:, hs]
            so = _sigmoid(o)
            hm = so * cell
            rinv = lax.rsqrt(jnp.mean(hm * hm, axis=-1, keepdims=True) + EPS)
            hn = hm * rinv
            z = z_ref[:, hs]
            sgz = _sigmoid(z)
            sz = z * sgz
            gh = g_ref[:, hs]
            dy = dy_ref[0, :, hs]
            d_z.append(_bf(dy * (hn * gh) * _dsilu(z, sgz)))
            d_g.append(_colsum(dy * hn * sz))
            dhn = dy * gh * sz
            dhm = rinv * (dhn - hn * jnp.mean(dhn * hn, axis=-1, keepdims=True))
            d_o.append(_bf(dhm * cell * so * (1.0 - so)))
            dcell = dhm * so
            dnum = dcell / dstab
            dnb = _bf(dnum)
            dden = -_rowsum(dcell * cell) / dstab * jnp.where(jnp.abs(den) > floor, jnp.where(den > 0.0, 1.0, -1.0), 0.0)
            dst = _dot_nt(dnb, vb) + dden
            dsdb = _bf(dst * w_intra)
            dc_out, dn_out = old[h]
            dcb = _bf(dc_out)
            dq_inter = w_inter * (_dot_nt(dnb, c_b) + dden * n_old)
            dk_inter = w_state * (_dot_nt(vb, dcb) + dn_out)
            dq = _dot(dsdb, kb) + dq_inter
            dk = _dot_tn(dsdb, qb) + dk_inter
            dv = _dot_tn(_bf(s), dnb) + _dot(_bf(ks * w_state), dcb)
            wq = w_inter * qf
            new.append((decay * dc_out + _dot_tn(_bf(wq), dnb), decay * dn_out + _colsum(wq * dden)))
            pmat = dst * s
            p_rows = _rowsum(pmat)
            p_cols = _rowsum(pmat.T)
            q_in = _rowsum(qf * dq_inter)
            k_in = _rowsum(ks * dk_inter)
            across = decay * (jnp.sum(dc_out * c_b.astype(F32), keepdims=True) + jnp.sum(dn_out * n_old, keepdims=True))
            dli_all = dli_all + jnp.where(lane == h, p_cols + k_in, 0.0)
            from_later = from_later + jnp.where(lane == heads + h, p_rows - p_cols + q_in, 0.0)
            from_earlier = from_earlier + jnp.where(lane == heads + h, k_in, 0.0)
            across_all = across_all + jnp.where(lane[0:1] == heads + h, across, 0.0)
            dqs.append(dq)
            dks.append(dk * kscale)
            dvs.append(dv)
        for h in range(heads):
            dcs[h], dns[h] = new[h]
        du_ref[0] = jnp.concatenate(d_o, axis=1)
        du_ref[1] = jnp.concatenate(d_z, axis=1)
        dg_ref[...] += jnp.concatenate(d_g, axis=1)
        dlf = _tri_dot_left(triu, from_later) + _tri_dot_left(tril_strict, from_earlier) + across_all
        dgt = dli_all + dlf * _sigmoid(-gtv)
        dgt_ref[...] = dgt
        dbif_ref[...] += _colsum(dgt)
        dgb = _bf(dgt)
        dqkv_ref[0] = _bf(jnp.concatenate(dqs, axis=1) + _dot_nt(dgb, wif_ref[0:d, :]))
        dqkv_ref[1] = _bf(jnp.concatenate(dks, axis=1) + _dot_nt(dgb, wif_ref[d:2 * d, :]))
        dqkv_ref[2] = _bf(jnp.concatenate(dvs, axis=1) + _dot_nt(dgb, wif_ref[2 * d:3 * d, :]))

    rev = lambda c: nc - 1 - c
    row = pl.BlockSpec((lc, d), lambda c: (rev(c), 0))
    gcol = pl.BlockSpec((lc, ng), lambda c: (rev(c), 0))
    grow = pl.BlockSpec((ng, lc), lambda c: (0, rev(c)))
    return _pcall_ride(
        body, ride, name="mlstm_bwd", grid=(nc,),
        in_specs=[pl.BlockSpec((3, lc, d), lambda c: (0, rev(c), 0)), gcol, grow, gcol, grow,
                  pl.BlockSpec((1, heads, hd, hd), lambda c: (rev(c), 0, 0, 0)),
                  pl.BlockSpec((1, heads, 1, hd), lambda c: (rev(c), 0, 0, 0)),
                  pl.BlockSpec((1, heads, 1, 128), lambda c: (rev(c), 0, 0, 0)),
                  row, pl.BlockSpec((lc, d), lambda c: (rev(c), 3)), pl.BlockSpec((lc, d), lambda c: (rev(c), 4)),
                  pl.BlockSpec((1, d), lambda c: (0, 0)), pl.BlockSpec((1, lc, d), lambda c: (1, rev(c), 0)),
                  pl.BlockSpec((3 * d, ng), lambda c: (0, 0))],
        out_specs=[pl.BlockSpec((3, lc, d), lambda c: (0, rev(c), 0)), pl.BlockSpec((lc, ng), lambda c: (rev(c), 0)),
                   pl.BlockSpec((1, ng), lambda c: (0, 0)), pl.BlockSpec((2, lc, d), lambda c: (0, rev(c), 0)),
                   pl.BlockSpec((1, d), lambda c: (0, 0))],
        out_shape=[jax.ShapeDtypeStruct((3, s_len, d), BF16), jax.ShapeDtypeStruct((s_len, ng), F32),
                   jax.ShapeDtypeStruct((1, ng), F32), jax.ShapeDtypeStruct((5, s_len, d), BF16),
                   jax.ShapeDtypeStruct((1, d), F32)],
        scratch_shapes=[pltpu.VMEM((heads, hd, hd), F32), pltpu.VMEM((heads, 1, hd), F32)],
        compiler_params=_seq(),
        args=(qkv, *gates, cst, nst, mst, cell, u, u, ml_g, d_ycat, wif_b))


def _conv_bwd_tile(dp, later, taps, cw_ref, gw_ref, gb_ref):
    tm = dp.shape[0]
    dwin = jnp.concatenate([dp, later[...]], axis=0)
    later[...] = dp[0:HALO]
    acc = cw_ref[CONV_WIDTH - 1:CONV_WIDTH, :] * dp
    for k in range(CONV_WIDTH):
        if k < CONV_WIDTH - 1:
            acc = acc + cw_ref[k:k + 1, :] * _shift_up(dwin, CONV_WIDTH - 1 - k)[0:tm]
        gw_ref[k:k + 1, :] += _colsum(dp * taps[k])
    gb_ref[...] += _colsum(dp)
    return acc


def _ml_pre_bwd(dqkv, u, conv_w, conv_b, wqkv_b, du):
    s_len = u.shape[0]
    d = conv_w.shape[1]
    _, heads, hd, _ = wqkv_b.shape
    tm = _tile(s_len, 256)
    per = tm // HALO
    nt = s_len // tm

    def body(dqkv_ref, x_ref, xp_ref, cw_ref, cb_ref, w_ref, _, dx_ref, gw_ref, gcw_ref, gcb_ref, later, dps, dxs):
        i = pl.program_id(0)

        @pl.when(i == 0)
        def _():
            gw_ref[...] = jnp.zeros_like(gw_ref)
            gcw_ref[...] = jnp.zeros_like(gcw_ref)
            gcb_ref[...] = jnp.zeros_like(gcb_ref)
            later[...] = jnp.zeros_like(later)

        prev = jnp.where(i == nt - 1, 0.0, xp_ref[...])
        xm = x_ref[...]
        taps = _conv_taps(jnp.concatenate([prev, xm], axis=0))
        pre = _conv_fwd(taps, cw_ref, cb_ref)
        sg = _sigmoid(pre)
        xcb = _bf(pre * sg)
        xmb = _bf(xm)
        for h in range(heads):
            hs = slice(h * hd, (h + 1) * hd)
            dqh, dkh, dvh = dqkv_ref[0, :, hs], dqkv_ref[1, :, hs], dqkv_ref[2, :, hs]
            dxc = _dot_nt(dqh, w_ref[0, h]) + _dot_nt(dkh, w_ref[1, h])
            dps[:, hs] = dxc * _dsilu(pre[:, hs], sg[:, hs])
            dxs[:, hs] = _dot_nt(dvh, w_ref[2, h])
            gw_ref[0, h] += _dot_tn(xcb[:, hs], dqh)
            gw_ref[1, h] += _dot_tn(xcb[:, hs], dkh)
            gw_ref[2, h] += _dot_tn(xmb[:, hs], dvh)
        dx_ref[0] = _bf(_conv_bwd_tile(dps[...], later, taps, cw_ref, gcw_ref, gcb_ref) + dxs[...])

    rev = lambda i: nt - 1 - i
    vec = pl.BlockSpec((1, d), lambda i: (0, 0))
    cwb = pl.BlockSpec((CONV_WIDTH, d), lambda i: (0, 0))
    whole4 = pl.BlockSpec(wqkv_b.shape, lambda i: (0, 0, 0, 0))
    return _pcall(
        body, name="ml_pre_bwd", grid=(nt,),
        in_specs=[pl.BlockSpec((3, tm, d), lambda i: (0, rev(i), 0)), pl.BlockSpec((tm, d), lambda i: (rev(i), 2)),
                  pl.BlockSpec((HALO, d), lambda i: (jnp.maximum(rev(i) * per - 1, 0), 2)),
                  cwb, vec, whole4, pl.BlockSpec(memory_space=pl.ANY)],
        out_specs=[pl.BlockSpec((1, tm, d), lambda i: (DU_PLANE[2], rev(i), 0)), whole4, cwb, vec],
        out_shape=[jax.ShapeDtypeStruct(du.shape, BF16), jax.ShapeDtypeStruct(wqkv_b.shape, F32),
                   jax.ShapeDtypeStruct((CONV_WIDTH, d), F32), jax.ShapeDtypeStruct((1, d), F32)],
        scratch_shapes=[pltpu.VMEM((HALO, d), F32), pltpu.VMEM((tm, d), F32), pltpu.VMEM((tm, d), F32)],
        input_output_aliases={6: 0},
        compiler_params=_seq(),
    )(dqkv, u, u, conv_w, conv_b, wqkv_b, du)


def _rg_bwd(d_ycat, u, hh, conv_w, conv_b, wa_b, ba, wx_b, bx, lam, du):
    s_len = u.shape[0]
    d = conv_w.shape[1]
    heads, hd, _ = wa_b.shape
    tm = _tile(s_len, 256)
    per = tm // HALO
    nt = s_len // tm

    def body(dy_ref, x_ref, xp_ref, z_ref, hh_ref, hp_ref, cw_ref, cb_ref, wa_ref, ba_ref, wx_ref, bx_ref, lam_ref, _,
             du_ref, gwa_ref, gwx_ref, gba_ref, gbx_ref, glam_ref, gcw_ref, gcb_ref, carry, gbuf, later, dxcs):
        i = pl.program_id(0)
        first = i == nt - 1

        @pl.when(i == 0)
        def _():
            carry[...] = jnp.zeros_like(carry)
            later[...] = jnp.zeros_like(later)
            gwa_ref[...] = jnp.zeros_like(gwa_ref)
            gwx_ref[...] = jnp.zeros_like(gwx_ref)
            gba_ref[...] = jnp.zeros_like(gba_ref)
            gbx_ref[...] = jnp.zeros_like(gbx_ref)
            glam_ref[...] = jnp.zeros_like(glam_ref)
            gcw_ref[...] = jnp.zeros_like(gcw_ref)
            gcb_ref[...] = jnp.zeros_like(gcb_ref)

        prev = jnp.where(first, 0.0, xp_ref[...])
        taps = _conv_taps(jnp.concatenate([prev, x_ref[...]], axis=0))
        xc = _conv_fwd(taps, cw_ref, cb_ref)
        r, ig, sp, log_a, a, mult = _rg_gates(xc, wa_ref, ba_ref, wx_ref, bx_ref, lam_ref)
        z = z_ref[...]
        sgz = _sigmoid(z)
        dy = dy_ref[0]
        hh_v = hh_ref[...]
        du_ref[1] = _bf(dy * hh_v * _dsilu(z, sgz))
        dhh = dy * (z * sgz)
        rows = lax.broadcasted_iota(jnp.int32, a.shape, 0)
        coef = jnp.where(rows == tm - 1, carry[1:2, :], _shift_up(a, 1))
        ca, cu = _scan_groups(coef, dhh, reverse=True)
        c = carry[0:1, :]
        for j in range(per - 1, -1, -1):
            blk = ca[j * 8:(j + 1) * 8] * c + cu[j * 8:(j + 1) * 8]
            gbuf[j * 8:(j + 1) * 8, :] = blk
            c = blk[0:1]
        carry[0:1, :] = c
        carry[1:2, :] = a[0:1]
        g = gbuf[...]
        hprev_tile = jnp.where(first, 0.0, hp_ref[...])
        hprev = _shift_down(jnp.concatenate([hprev_tile, hh_v], axis=0), 1)[HALO:]
        da = g * hprev
        gx_ = g * xc
        d_mult = gx_ * ig
        d_ig = gx_ * mult
        dxc = g * mult * ig
        dlog_a = da * a - d_mult * (a * a / mult)
        d_r = dlog_a * ((-RG_C) * sp)
        glam_ref[...] += _colsum(dlog_a * ((-RG_C) * r)) * (-_sigmoid(-lam_ref[...]))
        d_ga = d_r * r * (1.0 - r)
        d_gx = d_ig * ig * (1.0 - ig)
        gba_ref[...] += _colsum(d_ga)
        gbx_ref[...] += _colsum(d_gx)
        xb = _bf(xc)
        dgab = _bf(d_ga)
        dgxb = _bf(d_gx)
        for h in range(heads):
            hs = slice(h * hd, (h + 1) * hd)
            dxcs[:, hs] = dxc[:, hs] + _dot_nt(dgab[:, hs], wa_ref[h]) + _dot_nt(dgxb[:, hs], wx_ref[h])
            gwa_ref[h] += _dot_tn(xb[:, hs], dgab[:, hs])
            gwx_ref[h] += _dot_tn(xb[:, hs], dgxb[:, hs])
        du_ref[0] = _bf(_conv_bwd_tile(dxcs[...], later, taps, cw_ref, gcw_ref, gcb_ref))

    assert DU_PLANE[0] % 2 == 0 and DU_PLANE[1] == DU_PLANE[0] + 1
    rev = lambda i: nt - 1 - i
    row = pl.BlockSpec((tm, d), lambda i: (rev(i), 0))
    halo_prev = lambda col: pl.BlockSpec((HALO, d), lambda i: (jnp.maximum(rev(i) * per - 1, 0), col))
    vec = pl.BlockSpec((1, d), lambda i: (0, 0))
    cwb = pl.BlockSpec((CONV_WIDTH, d), lambda i: (0, 0))
    whole3 = lambda a: pl.BlockSpec(a.shape, lambda i: (0, 0, 0))
    return _pcall(
        body, name="rg_bwd", grid=(nt,),
        in_specs=[pl.BlockSpec((1, tm, d), lambda i: (0, rev(i), 0)), row, halo_prev(0),
                  pl.BlockSpec((tm, d), lambda i: (rev(i), 1)), row, halo_prev(0),
                  cwb, vec, whole3(wa_b), vec, whole3(wx_b), vec, vec, pl.BlockSpec(memory_space=pl.ANY)],
        out_specs=[pl.BlockSpec((2, tm, d), lambda i: (DU_PLANE[0] // 2, rev(i), 0)), whole3(wa_b), whole3(wa_b),
                   vec, vec, vec, cwb, vec],
        out_shape=[jax.ShapeDtypeStruct(du.shape, BF16), jax.ShapeDtypeStruct(wa_b.shape, F32),
                   jax.ShapeDtypeStruct(wa_b.shape, F32)] + [jax.ShapeDtypeStruct((1, d), F32)] * 3
        + [jax.ShapeDtypeStruct((CONV_WIDTH, d), F32), jax.ShapeDtypeStruct((1, d), F32)],
        scratch_shapes=[pltpu.VMEM((8, d), F32), pltpu.VMEM((tm, d), F32), pltpu.VMEM((HALO, d), F32),
                        pltpu.VMEM((tm, d), F32)],
        input_output_aliases={13: 0},
        compiler_params=_seq(),
    )(d_ycat, u, u, u, hh, hh, conv_w, conv_b, wa_b, ba, wx_b, bx, lam, du)


def _in_bwd(du, w4, x, dxn, g, scale, ride=None):
    s_len, d = x.shape
    tm = _tile(s_len, 512)
    nsh_chips, _, nsh = w4.shape
    npc = du.shape[0]
    ck = d // 4
    assert nsh % ck == 0 and npc * d == nsh_chips * nsh

    def body(du_ref, w_ref, x_ref, dxn_ref, g_ref, sc_ref, dx_ref, dsh_ref, dsc_ref, dg_ref):
        @pl.when(pl.program_id(0) == 0)
        def _():
            dsh_ref[...] = jnp.zeros_like(dsh_ref)
            dsc_ref[...] = jnp.zeros_like(dsc_ref)
            dg_ref[...] = jnp.zeros_like(dg_ref)

        dh = None
        for q in range(npc * d // ck):
            col = q * ck
            p, pc = col // d, col % d
            s, sc = col // nsh, col % nsh
            t = _dot_nt(du_ref[DU_PLANE[p], :, pc:pc + ck], w_ref[s, :, sc:sc + ck])
            dh = t if dh is None else dh + t
        xv = x_ref[...]
        r = lax.rsqrt(jnp.mean(xv * xv, axis=-1, keepdims=True) + EPS)
        xn = xv * r
        gv = g_ref[...]
        onesc = 1.0 + sc_ref[...]
        dsh_ref[...] += _colsum(dh)
        dsc_ref[...] += _colsum(dh * (xn * gv))
        dg_ref[...] += _colsum(dh * xn * onesc)
        dxh = dh * (gv * onesc)
        dx_ref[...] = dxn_ref[...] + r * (dxh - xn * jnp.mean(dxh * xn, axis=-1, keepdims=True))

    row = pl.BlockSpec((tm, d), lambda i: (i, 0))
    vec = pl.BlockSpec((1, d), lambda i: (0, 0))
    return _pcall_ride(
        body, ride, name="in_bwd", grid=(s_len // tm,),
        in_specs=[pl.BlockSpec((npc, tm, d), lambda i: (0, i, 0)), pl.BlockSpec(w4.shape, lambda i: (0, 0, 0)), row, row,
                  vec, vec],
        out_specs=[row, vec, vec, vec],
        out_shape=[jax.ShapeDtypeStruct((s_len, d), F32)] + [jax.ShapeDtypeStruct((1, d), F32)] * 3,
        compiler_params=_seq(),
        args=(du, w4, x, dxn, g, scale))


def _layer_fwd(x, p, rides=None):
    rides = rides or {}
    landed = {}
    ride = lambda kernel: rides[kernel](landed) if kernel in rides else None
    (h_b, u), landed["ln_inproj"] = _ln_inproj(x, p["norm_g"], p["scale"], p["shift"], p["w4"], ride("ln_inproj"))
    (hh, ycat), landed["rg_fwd"] = _rg_fwd(u, p["rg_conv_w"], p["rg_conv_b"], p["rg_wa_b"], p["rg_ba"], p["rg_wx_b"],
                                           p["rg_bx"], p["rg_lam"], ride("rg_fwd"))
    if "late" in rides:
        p = {**p, **rides["late"](landed)}
    qkv, *gates = _ml_pre(u, p["ml_conv_w"], p["ml_conv_b"], p["wqkv_b"], p["wif_b"], p["wift_b"], p["b_if"],
                          p["b_ift"])
    (cell, ycat, cst, nst, mst), landed["mlstm_fwd"] = _mlstm_fwd(qkv, gates, u, p["ml_g"], ycat, ride("mlstm_fwd"))
    (y, x_new), landed["out_proj"] = _out_proj(ycat, p["w_out_b"], x, p["gate"], ride("out_proj"))
    saved = dict(x=x, h_b=h_b, u=u, hh=hh, qkv=qkv, gates=gates, cell=cell, ycat=ycat, cst=cst, nst=nst, mst=mst, y=y)
    return x_new, saved, p, landed


def _layer_bwd(dxn, p, s, rides=None):
    rides = rides or {}
    landed = {}
    ride = lambda kernel: rides[kernel](grads, landed) if kernel in rides else None
    u = s["u"]
    d = dxn.shape[1]
    d_gate, dy_b, d_ycat = _out_bwd(dxn, s["y"], p["gate"], p["w_out_b"])
    grads = dict(w_out=_grad_matmul(s["ycat"], dy_b[None], 2, lambda b: b, lambda b: 0, (2 * d, d), (d, d),
                                    lambda b: (b, 0))[0])
    (dqkv, dgt, g_b_if, du, g_ml_g), landed["mlstm_bwd"] = _mlstm_bwd(
        s["qkv"], s["gates"], s["cst"], s["nst"], s["mst"], s["cell"], u, p["ml_g"], d_ycat, p["wif_b"],
        ride("mlstm_bwd"))
    ng = dgt.shape[1]
    g_w_if = _grad_matmul(s["qkv"], _bf(dgt)[None], 3, lambda b: b, lambda b: 0, (3 * d, ng), (d, ng),
                          lambda b: (b, 0))[0][0]
    du, g_wqkv, g_ml_cw, g_ml_cb = _ml_pre_bwd(dqkv, u, p["ml_conv_w"], p["ml_conv_b"], p["wqkv_b"], du)
    du, g_wa, g_wx, g_ba, g_bx, g_lam, g_rg_cw, g_rg_cb = _rg_bwd(d_ycat, u, s["hh"], p["rg_conv_w"], p["rg_conv_b"],
                                                                  p["rg_wa_b"], p["rg_ba"], p["rg_wx_b"], p["rg_bx"],
                                                                  p["rg_lam"], du)
    grads.update(rg_conv_w=g_rg_cw, rg_conv_b=g_rg_cb, rg_w_a=g_wa, rg_b_a=g_ba, rg_w_x=g_wx, rg_b_x=g_bx,
                 rg_lambda=g_lam, ml_conv_w=g_ml_cw, ml_conv_b=g_ml_cb, ml_w_qkv=g_wqkv, ml_w_if=g_w_if, ml_b_if=g_b_if,
                 ml_norm_g=g_ml_g)
    npc = du.shape[0]
    grads["w_in"], landed["grad_w_in"] = _grad_matmul(
        s["h_b"][None], du, npc, lambda b: 0, lambda b: (b + DU_PLANE[0]) % npc, (d, npc * d), (d, d),
        lambda b: (0, b), ride("grad_w_in"))
    (dx, d_shift, d_scale, grads["norm_g"]), landed["in_bwd"] = _in_bwd(du, p["w4"], s["x"], dxn, p["norm_g"],
                                                                        p["scale"], ride("in_bwd"))
    return dx, grads, jnp.concatenate([d_shift, d_scale, d_gate], axis=1), landed


def _trunk_fwd_bwd(x, target, final_g, layers):
    saved = []
    for p in layers:
        x, s, _, _ = _layer_fwd(x, p)
        saved.append(s)
    dx, g_final, loss = _final_loss(x, final_g, target)
    grads, dmods = [], []
    for layer in reversed(range(len(layers))):
        dx, g, dm, _ = _layer_bwd(dx, layers[layer], saved[layer])
        grads.append(g)
        dmods.append(dm)
    return loss, dx, g_final, grads[::-1], dmods[::-1]


def _me():
    return lax.axis_index("x"), lax.axis_index("y"), lax.axis_index("c")


def _remote(src, dst, send_sem, recv_sem, to):
    return pltpu.make_async_remote_copy(src_ref=src, dst_ref=dst, send_sem=send_sem, recv_sem=recv_sem,
                                        device_id=to, device_id_type=MESH)


def _all_gather8(blocks, space):
    n = len(blocks)

    def body(*refs):
        x_refs, out_refs = refs[:n], refs[n:2 * n]
        send_sems, recv_sems, local_sems = refs[2 * n:]
        x, y, c = _me()
        me, sibling = (x, y, c), (x, y, 1 - c)
        chips = [(1 - x, y), (x, 1 - y), (1 - x, 1 - y)]

        def rows(i, px, py, pc):
            m_per = blocks[i].shape[0]
            return out_refs[i].at[pl.ds((4 * px + 2 * py + pc) * m_per, m_per), :]

        def copy(i, k, blk, to, src=None):
            return _remote(rows(i, *blk) if src is None else src, rows(i, *blk), send_sems.at[7 * i + k],
                           recv_sems.at[7 * i + k], to)

        mine = [pltpu.make_async_copy(x_refs[i], rows(i, *me), local_sems.at[i]) for i in range(n)]
        first = []
        for i in range(n):
            first.append(copy(i, 0, me, sibling, src=x_refs[i]))
            first += [copy(i, 1 + j, me, (*chip, c), src=x_refs[i]) for j, chip in enumerate(chips)]
        for cp in mine + first:
            cp.start()
        passed = []
        for j, chip in enumerate(chips):
            for i in range(n):
                copy(i, 1 + j, (*chip, c), me).wait_recv()
                passed.append(copy(i, 4 + j, (*chip, c), sibling))
                passed[-1].start()
        for i in range(n):
            copy(i, 0, sibling, me).wait_recv()
            for j, chip in enumerate(chips):
                copy(i, 4 + j, (*chip, 1 - c), me).wait_recv()
        for cp in first + passed:
            cp.wait_send()
        for cp in mine:
            cp.wait()

    spec = pl.BlockSpec(memory_space=space)
    return _pcall(
        body, name="all_gather8",
        out_shape=[jax.ShapeDtypeStruct((8 * b.shape[0], b.shape[1]), b.dtype) for b in blocks],
        in_specs=[spec] * n, out_specs=[spec] * n,
        scratch_shapes=[pltpu.SemaphoreType.DMA((7 * n,)), pltpu.SemaphoreType.DMA((7 * n,)),
                        pltpu.SemaphoreType.DMA((n,))],
    )(*blocks)


def _exchange(legs):
    n = len(legs)

    def body(*refs):
        copies, local = _exchange_body(legs, refs[:n], refs[n:2 * n], *refs[2 * n:])
        for cp in copies + local:
            cp.start()
        for cp in copies:
            cp.wait_recv()
        for cp in copies:
            cp.wait_send()
        for cp in local:
            cp.wait()

    hbm = pl.BlockSpec(memory_space=pltpu.HBM)
    return _pcall(body, name="exchange", out_shape=[leg.landing() for leg in legs], in_specs=[hbm] * n,
                  out_specs=[hbm] * n, input_output_aliases=_exchange_aliases(legs, 0, 0),
                  scratch_shapes=_exchange_sems(legs))(*[leg.src for leg in legs])


def _row_tile(rows, cap=4096, mult=16):
    best = None
    for t in range(mult, min(rows, cap) + 1, mult):
        if rows % t == 0:
            best = t
    return rows if best is None else best


def _pair_sum(half, own, own_spec, got, got_spec, out_shape, out_spec, grid):
    def body(_, a_ref, b_ref, o_ref):
        o_ref[...] = (a_ref[...] + b_ref[...].astype(F32)).astype(o_ref.dtype)

    return _pcall(
        body, name="pair_sum",
        grid_spec=pltpu.PrefetchScalarGridSpec(num_scalar_prefetch=1, grid=grid, in_specs=[own_spec, got_spec],
                                               out_specs=out_spec),
        out_shape=out_shape, compiler_params=_seq(len(grid)))(half, own, got)


def _chip_sum(ids, part, met, fill, layer=0, stack=1):
    _, _, rows, n = part.shape
    tr = _row_tile(rows, cap=max(16, (1 << 18) // n))
    first = isinstance(stack, int)

    def body(_, own_ref, a_ref, b_ref, c_ref, *rest):
        acc = own_ref[...].astype(F32) + a_ref[...].astype(F32)
        acc = acc + b_ref[...].astype(F32)
        rest[-1][...] = acc + c_ref[...].astype(F32)

    blk = (None, None, tr, n)
    other = lambda k: pl.BlockSpec(blk, lambda j, ids: ((ids[0] + k) % 4, 0, j, 0))
    in_specs = [pl.BlockSpec(blk, lambda j, ids: (ids[0], 0, j, 0)), other(1), other(2), other(3)]
    return _pcall(
        body, name="chip_sum",
        grid_spec=pltpu.PrefetchScalarGridSpec(
            num_scalar_prefetch=1, grid=(rows // tr,),
            in_specs=in_specs if first else in_specs + [pl.BlockSpec(memory_space=pl.ANY)],
            out_specs=pl.BlockSpec(blk, lambda j, ids: (layer, ids[1] if fill else 0, j, 0))),
        out_shape=jax.ShapeDtypeStruct(((stack,) if first else stack.shape[:1]) + (2 if fill else 1, rows, n), F32),
        input_output_aliases={} if first else {5: 0},
        compiler_params=_seq())(*((ids, part, met, met, met) if first else (ids, part, met, met, met, stack)))


def _ada_mod(c_all, w_ada, b_ada_cols):
    depth, d, n = w_ada.shape
    nb = c_all.shape[0]

    def body(c_ref, w_ref, b_ref, o_ref):
        cv = c_ref[...]
        ca = _bf(cv * _sigmoid(cv))
        o_ref[0] = _dot(ca, _bf(w_ref[0])) + b_ref[0]

    return _pcall(body, name="ada_mod", grid=(depth,),
                  in_specs=[pl.BlockSpec((nb, d), lambda l: (0, 0)), pl.BlockSpec((1, d, n), lambda l: (l, 0, 0)),
                            pl.BlockSpec((1, 1, n), lambda l: (l, 0, 0))],
                  out_specs=pl.BlockSpec((1, nb, n), lambda l: (l, 0, 0)),
                  out_shape=jax.ShapeDtypeStruct((depth, nb, n), F32), compiler_params=_seq())(c_all, w_ada, b_ada_cols)


def _ada_grad(c_all, dmod_cols, rows_all):
    nb, d = c_all.shape
    depth, _, n = dmod_cols.shape
    kinds, n_all = rows_all.shape[1], rows_all.shape[3]

    def body(c_ref, dm_ref, da_ref, gw_ref, gb_ref):
        cv = c_ref[...]
        ca = _bf(cv * _sigmoid(cv))
        gw_ref[0] = _dot_tn(ca, _bf(dm_ref[0]))
        for k in range(kinds):
            gb_ref[0, k] = _colsum(da_ref[0, k])

    return _pcall(body, name="ada_grad", grid=(depth,),
                  in_specs=[pl.BlockSpec((nb, d), lambda l: (0, 0)), pl.BlockSpec((1, nb, n), lambda l: (l, 0, 0)),
                            pl.BlockSpec((1, kinds, nb, n_all), lambda l: (l, 0, 0, 0))],
                  out_specs=[pl.BlockSpec((1, d, n), lambda l: (l, 0, 0)),
                             pl.BlockSpec((1, kinds, 1, n_all), lambda l: (l, 0, 0, 0))],
                  out_shape=[jax.ShapeDtypeStruct((depth, d, n), F32), jax.ShapeDtypeStruct((depth, kinds, 1, n_all), F32)],
                  compiler_params=_seq())(c_all, dmod_cols, rows_all)


def _adamw(items, ride=None):
    two_d = [tuple(t.reshape(w.size // w.shape[-1], w.shape[-1]) for t in (w, g, m, v)) for w, g, m, v in items]
    n = len(items)
    if n == 1:
        rows, cols = two_d[0][0].shape
        tr = _row_tile(rows, cap=max(8, (1 << 18) // cols), mult=8)
        blocks = [pl.BlockSpec((tr, cols), lambda i: (i, 0))]
        grid = (rows // tr,)
    else:
        blocks = [pl.BlockSpec(t[0].shape, lambda i: (0, 0)) for t in two_d]
        grid = (1,)

    def body(*refs):
        for k in range(n):
            w_ref, g_ref, m_ref, v_ref = refs[4 * k:4 * k + 4]
            d_ref, mo_ref, vo_ref = refs[4 * n + 3 * k:4 * n + 3 * k + 3]
            gv = g_ref[...]
            mn = ADAM_B1 * m_ref[...] + (1.0 - ADAM_B1) * gv
            vn = ADAM_B2 * v_ref[...] + (1.0 - ADAM_B2) * (gv * gv)
            m_hat = mn / (1.0 - ADAM_B1 ** ADAM_STEP)
            v_hat = vn / (1.0 - ADAM_B2 ** ADAM_STEP)
            d_ref[...] = -ADAM_LR * (m_hat / (jnp.sqrt(v_hat) + ADAM_EPS) + ADAM_WD * w_ref[...])
            mo_ref[...] = mn
            vo_ref[...] = vn

    outs, got = _pcall_ride(
        body, ride, name="adamw", grid=grid,
        in_specs=[b for b in blocks for _ in range(4)], out_specs=[b for b in blocks for _ in range(3)],
        out_shape=[jax.ShapeDtypeStruct(t[0].shape, F32) for t in two_d for _ in range(3)],
        compiler_params=_seq(), args=tuple(a for t in two_d for a in t))
    return [tuple(o.reshape(items[k][0].shape) for o in outs[3 * k:3 * k + 3]) for k in range(n)], got


WEIGHTS = ["norm_g", "w_ada", "b_ada", "w_in", "rg_conv_w", "rg_conv_b", "rg_w_a", "rg_b_a", "rg_w_x", "rg_b_x",
           "rg_lambda", "ml_conv_w", "ml_conv_b", "ml_w_q", "ml_w_k", "ml_w_v", "ml_w_if", "ml_b_if", "ml_norm_g",
           "w_out", "final_g"]
SMALL_SHARDED = {"ml_w_qkv": 2, "rg_conv_w": 1, "ml_conv_w": 1, "ml_w_if": 0}
REPLICATED = ["rg_w_a", "rg_w_x", "rg_conv_b", "rg_b_a", "rg_b_x", "rg_lambda", "ml_conv_b", "ml_norm_g", "ml_b_if"]
LANES = 128


def _to_pieces(g, axis):
    shp = g.shape
    g = g.reshape(shp[:axis] + (4, 2, shp[axis] // 8) + shp[axis + 1:])
    g = jnp.moveaxis(g, (axis, axis + 1), (0, 1))
    return g.reshape(4, 2, -1)


def _from_pieces(p, shard_shape, axis):
    k = p.shape[0]
    rest = shard_shape[:axis] + (shard_shape[axis] // k,) + shard_shape[axis + 1:]
    t = jnp.moveaxis(p.reshape((k,) + rest), 0, axis)
    return t.reshape(shard_shape)


def _pad_rows(flat, mult):
    n = flat.shape[-1]
    pad = (-n) % mult
    if pad:
        flat = jnp.concatenate([flat, jnp.zeros(flat.shape[:-1] + (pad,), flat.dtype)], axis=-1)
    return flat


def kernel(x, c, norm_g, w_ada, b_ada, w_in, rg_conv_w, rg_conv_b, rg_w_a, rg_b_a, rg_w_x, rg_b_x, rg_lambda, ml_conv_w, ml_conv_b, ml_w_q, ml_w_k, ml_w_v, ml_w_if, ml_b_if, ml_norm_g, w_out, final_g, loss_target, m_norm_g, m_w_ada, m_b_ada, m_w_in, m_rg_conv_w, m_rg_conv_b, m_rg_w_a, m_rg_b_a, m_rg_w_x, m_rg_b_x, m_rg_lambda, m_ml_conv_w, m_ml_conv_b, m_ml_w_q, m_ml_w_k, m_ml_w_v, m_ml_w_if, m_ml_b_if, m_ml_norm_g, m_w_out, m_final_g, v_norm_g, v_w_ada, v_b_ada, v_w_in, v_rg_conv_w, v_rg_conv_b, v_rg_w_a, v_rg_b_a, v_rg_w_x, v_rg_b_x, v_rg_lambda, v_ml_conv_w, v_ml_conv_b, v_ml_w_q, v_ml_w_k, v_ml_w_v, v_ml_w_if, v_ml_b_if, v_ml_norm_g, v_w_out, v_final_g):
    given = dict(locals())
    ax, ay, ac = lax.axis_index("x"), lax.axis_index("y"), lax.axis_index("c")
    chip = 2 * ax + ay
    me = 2 * chip + ac
    depth, d = norm_g.shape
    n_ada = w_ada.shape[2]
    pick = lambda a, i, axis=0: lax.dynamic_index_in_dim(a, i, axis, keepdims=False)

    convs = jnp.stack([rg_conv_w, ml_conv_w])
    n_conv = 2 * depth * CONV_WIDTH // 4
    blk = jnp.concatenate([c, convs.reshape(n_conv, d), jnp.zeros((8 - 1 - n_conv, d), F32)], axis=0)
    w_in_first = lax.dynamic_slice_in_dim(w_in[0], ac * (d // 2), d // 2, 0).astype(BF16)
    g0, w_in_first = _all_gather8([blk, w_in_first], pltpu.HBM)
    g0 = g0.reshape(8, 8, d)
    c_all = g0[:, 0, :]
    conv_full = g0[0::2, 1:1 + n_conv].reshape(4, 2, depth, CONV_WIDTH, d // 4)
    conv_full = conv_full.transpose(1, 2, 3, 0, 4).reshape(2, depth, CONV_WIDTH, d)

    b_cols = lax.dynamic_slice_in_dim(b_ada, chip * n_ada, n_ada, axis=1)[:, None, :]
    mod_part = _ada_mod(c_all, w_ada, b_cols)
    g1 = _all_gather8([mod_part.transpose(1, 0, 2).reshape(8, depth * n_ada)], pltpu.VMEM)[0]
    g1 = g1.reshape(8, 8, depth, n_ada)[0::2]
    mod_me = pick(g1.transpose(1, 2, 0, 3).reshape(8, depth, 4 * n_ada), me)

    def half_of(w, axis):
        n = w.shape[axis] // 2
        return lax.dynamic_slice_in_dim(w, ac * n, n, axis).astype(BF16)

    n_sh = w_in.shape[2]
    heads, hd_cut, hd = ml_w_q.shape[1:]

    def blocks_of(l):
        wqkv = jnp.stack([ml_w_q[l], ml_w_k[l], ml_w_v[l]])
        return [half_of(w_in[l], 0), half_of(w_out[l], 0), half_of(wqkv, 2).reshape(-1, hd), half_of(ml_w_if[l], 0)]

    def layer_of(l, w4, rest):
        return dict(
            norm_g=norm_g[l][None], shift=mod_me[l, 0:d][None], scale=mod_me[l, d:2 * d][None],
            gate=mod_me[l, 2 * d:3 * d][None], w4=w4.reshape(4, d, n_sh),
            rg_conv_w=conv_full[0, l], rg_conv_b=rg_conv_b[l][None], rg_wa_b=_bf(rg_w_a[l]), rg_ba=rg_b_a[l][None],
            rg_wx_b=_bf(rg_w_x[l]), rg_bx=rg_b_x[l][None], rg_lam=rg_lambda[l][None],
            ml_conv_w=conv_full[1, l], ml_conv_b=ml_conv_b[l][None], b_if=ml_b_if[l][None], b_ift=ml_b_if[l][:, None],
            ml_g=ml_norm_g[l][None], **rest)

    def rest_of(gathered):
        w_out_b, wqkv_g, wif = gathered
        return dict(w_out_b=w_out_b, wqkv_b=_from_pieces(wqkv_g.reshape(8, -1), (3, heads, hd, hd), 2), wif_b=wif,
                    wift_b=wif.T)

    spread = lambda blocks: [Leg(b, "spread") for b in blocks]
    fill = lambda landed: [Leg(t, "sib_fill") for t in landed]
    flat = lambda filled: [t.reshape(-1, t.shape[-1]) for t in filled]
    first = blocks_of(0)
    n_rest = len(first) - 1
    p = layer_of(0, w_in_first, {})
    layers, saved = [], []
    xl = x[0]
    for l in range(depth):
        nxt = blocks_of(l + 1) if l + 1 < depth else []
        skip = n_rest if l == 0 else 0
        rides = dict(rg_fwd=lambda landed, nxt=nxt: spread(nxt[:1]))
        if l == 0:
            rides.update(ln_inproj=lambda landed: spread(first[1:]),
                         rg_fwd=lambda landed, nxt=nxt: fill(landed["ln_inproj"]) + spread(nxt[:1]),
                         late=lambda landed: rest_of(flat(landed["rg_fwd"][:n_rest])))
        if nxt:
            rides.update(mlstm_fwd=lambda landed, nxt=nxt: spread(nxt[1:]),
                         out_proj=lambda landed, skip=skip: fill(list(landed["rg_fwd"][skip:]) + list(landed["mlstm_fwd"])))
        xl, s, p, landed = _layer_fwd(xl, p, rides)
        layers.append(p)
        saved.append(s)
        if nxt:
            arrived = flat(landed["out_proj"])
            p = layer_of(l + 1, arrived[0], rest_of(arrived[1:]))
    dx, g_final, loss = _final_loss(xl, final_g[None], loss_target[0])

    half = ac.reshape(1)
    ids = jnp.stack([chip, ac])
    r_out = w_out.shape[1] // 2

    def pair_in(g_w_in, got_in):
        return _pair_sum(
            half, g_w_in, pl.BlockSpec((None, d // 2, n_sh), lambda s, h: (0, h[0], s)),
            got_in, pl.BlockSpec((None, None, d // 2, n_sh), lambda s, h: (0, s, 0, 0)),
            jax.ShapeDtypeStruct((4, 1, d // 2, n_sh), BF16),
            pl.BlockSpec((None, None, d // 2, n_sh), lambda s, h: (s, 0, 0, 0)), (4,))

    def pair_out(g_out5, got_out):
        return _pair_sum(
            half, g_out5, pl.BlockSpec((None, None, None, r_out, d), lambda s, h: (0, s, h[0], 0, 0)),
            got_out, pl.BlockSpec((None, None, r_out, d), lambda s, h: (0, s, 0, 0)),
            jax.ShapeDtypeStruct((4, 1, r_out, d), BF16),
            pl.BlockSpec((None, None, r_out, d), lambda s, h: (s, 0, 0, 0)), (4,))

    def pair_slab(slab, got, dtype):
        rows = got.shape[0] // 4
        blk = pl.BlockSpec((rows, LANES), lambda s, h: (s, 0))
        return _pair_sum(half, slab, pl.BlockSpec((None, rows, LANES), lambda s, h: (h[0], s, 0)), got, blk,
                         jax.ShapeDtypeStruct((4 * rows, LANES), dtype), blk, (4,)).reshape(4, 1, rows, LANES)

    row_pad = lambda n: -(-n // (8 * LANES)) * (8 * LANES)

    def as_rows(t):
        if t.shape[-1] == LANES and t.size % (8 * LANES) == 0:
            return t.reshape(-1, LANES)
        return _pad_rows(t.reshape(-1), 8 * LANES).reshape(-1, LANES)

    chips = lambda arrs: [Leg(a, "chips") for a in arrs]
    out5 = lambda g: g["w_out"].reshape(1, 4, 2, r_out, d)
    grads, dmods, parts, mets = [None] * depth, [None] * depth, [None] * depth, [None] * depth
    small = {}

    def early_exchange(g, landed):
        every = [g] + grads[1:]
        sm = jnp.concatenate([_to_pieces(every[l][name], axis) for l in range(depth)
                              for name, axis in SMALL_SHARDED.items()], axis=-1)
        sm = _pad_rows(sm, 16 * LANES)
        sm = sm.transpose(1, 0, 2).reshape(2, -1, LANES)
        rep = [as_rows(every[l][name]) for l in range(depth) for name in REPLICATED]
        rep = jnp.concatenate(rep + [as_rows(g_final), as_rows(loss)], axis=0)
        rep = jnp.concatenate([rep, jnp.zeros(((-rep.shape[0]) % 64, LANES), F32)], axis=0)
        rep = rep.reshape(4, 2, -1, LANES).transpose(1, 0, 2, 3).reshape(2, -1, LANES)
        got_sm, got_rep = _exchange([Leg(sm, "sib_slab"), Leg(rep, "sib_slab")])
        small["parts"] = [pair_out(out5(g), landed["mlstm_bwd"][0]), pair_slab(sm, got_sm, BF16),
                          pair_slab(rep, got_rep, F32)]
        return chips(small["parts"])

    def last_exchange(g, landed):
        (got_in,) = _exchange([Leg(g["w_in"], "sib_w_in")])
        small["part_in"] = pair_in(g["w_in"], got_in)
        return chips([small["part_in"]])

    for l in reversed(range(depth)):
        above = parts[l + 1] if l + 1 < depth else []
        rides = dict(mlstm_bwd=lambda g, landed, above=above: [Leg(out5(g), "sib_w_out")] + chips(above),
                     in_bwd=lambda g, landed: [Leg(g["w_in"], "sib_w_in")])
        if l == 0:
            rides.update(grad_w_in=early_exchange, in_bwd=last_exchange)
        dx, grads[l], dmods[l], got = _layer_bwd(dx, layers[l], saved[l], rides)
        if above:
            mets[l + 1] = got["mlstm_bwd"][1:]
        if l > 0:
            parts[l] = [pair_in(grads[l]["w_in"], got["in_bwd"][0]), pair_out(out5(grads[l]), got["mlstm_bwd"][0])]
    part_out, part_sm, part_rep = small["parts"]
    met_out, met_sm, met_rep = got["grad_w_in"]
    parts[0], mets[0] = [small["part_in"], part_out], [got["in_bwd"][0], met_out]
    n_rep = part_rep.shape[2]

    pad = lambda t: jnp.concatenate([t, jnp.zeros((1, 2 * d), F32)], axis=1)
    rows = [r for l in range(depth) for r in (dmods[l], pad(grads[l]["norm_g"]))]
    blk = jnp.concatenate(rows + [jnp.zeros((8 - 2 * depth, 3 * d), F32)], axis=0)
    rows_all = _all_gather8([blk], pltpu.VMEM)[0].reshape(8, 8, 3 * d)[:, :2 * depth]
    rows_all = rows_all.transpose(1, 0, 2).reshape(depth, 2, 8, 3 * d)
    dm_cols = lax.dynamic_slice_in_dim(rows_all[:, 0], chip * n_ada, n_ada, axis=2)
    g_w_ada, summed = _ada_grad(c_all, dm_cols, rows_all)

    g = dict(w_ada=g_w_ada, b_ada=summed[:, 0, 0], norm_g=summed[:, 1, 0, :d])
    item = lambda name: (given[name], g[name], given["m_" + name], given["v_" + name])
    both_in, both_out = depth, depth
    for l in range(depth):
        both_in = _chip_sum(ids, parts[l][0], mets[l][0], True, l, both_in)
        both_out = _chip_sum(ids, parts[l][1], mets[l][1], True, l, both_out)
    both_in, both_out, both_sm = _exchange(fill([both_in, both_out, _chip_sum(ids, part_sm, met_sm, True)]))
    red_rep = _chip_sum(ids, part_rep, met_rep, False).reshape(n_rep, LANES)
    rep_all = _all_gather8([red_rep], pltpu.VMEM)[0].reshape(-1)

    g.update(w_in=both_in.reshape(w_in.shape), w_out=both_out.reshape(w_out.shape))
    shard = both_sm.reshape(2, -1)
    off = 0
    per_layer = {name: [] for name in SMALL_SHARDED}
    for l in range(depth):
        for name, axis in SMALL_SHARDED.items():
            shp = (3,) + ml_w_q.shape[1:] if name == "ml_w_qkv" else given[name].shape[1:]
            n = grads[l][name].size // 8
            per_layer[name].append(_from_pieces(shard[:, off:off + n], shp, axis))
            off += n
    for name in SMALL_SHARDED:
        g[name] = jnp.stack(per_layer[name])
    for i, name in enumerate(["ml_w_q", "ml_w_k", "ml_w_v"]):
        g[name] = g["ml_w_qkv"][:, i]
    off = 0
    per_layer = {name: [] for name in REPLICATED}
    for l in range(depth):
        for name in REPLICATED:
            n = given[name][l].size
            per_layer[name].append(rep_all[off:off + n].reshape(given[name].shape[1:]))
            off += row_pad(n)
    for name in REPLICATED:
        g[name] = jnp.stack(per_layer[name])
    g["final_g"] = rep_all[off:off + d]
    loss_all = rep_all[off + row_pad(d)]

    stepped = {}
    rg_mats, ml_mats = ["rg_w_a", "rg_w_x"], ["ml_w_q", "ml_w_k", "ml_w_v"]
    vectors = [n for n in WEIGHTS if n not in ["w_ada", "w_in", "w_out"] + rg_mats + ml_mats]
    for names in (["w_ada"], ["w_in"], ["w_out"], rg_mats, ml_mats, vectors):
        stepped.update(zip(names, _adamw([item(name) for name in names])[0]))
    deltas, new_m, new_v = zip(*[stepped[name] for name in WEIGHTS])
    return (loss_all, dx[None], *[g[name] for name in WEIGHTS], *deltas, *new_m, *new_v)
```

```python
import functools
from typing import NamedTuple

import jax
import jax.numpy as jnp
from jax import lax
from jax.experimental import pallas as pl
from jax.experimental.pallas import tpu as pltpu

F32 = jnp.float32
BF16 = jnp.bfloat16

EPS = 1e-6
RG_C = 8.0
CONV_WIDTH = 4
ML_CHUNK = 512
HALO = 16
ROWS_VECTOR = 256
ROWS_MATMUL = 1024
ROWS_IN_BWD = 512
ROWS_GRAD_MATMUL = 2048
BLOCK_ELEMS = 1 << 18
ADAM_LR = 0.001
ADAM_B1 = 0.9
ADAM_B2 = 0.999
ADAM_EPS = 1e-08
ADAM_WD = 0.01
ADAM_STEP = 10
MESH = pl.DeviceIdType.MESH


def _pcall(body, **kw):
    return pl.pallas_call(body, **kw)


class Leg(NamedTuple):
    src: jax.Array
    kind: str

    def landing(self):
        a = self.src
        shape = {"chips": lambda: a.shape, "spread": lambda: (4, 2) + a.shape, "sib_fill": lambda: a.shape,
                 "sib_w_in": lambda: (a.shape[0], 4, a.shape[1] // 2, a.shape[2] // 4),
                 "sib_w_out": lambda: a.shape[:2] + a.shape[3:], "sib_slab": lambda: a.shape[1:]}[self.kind]()
        return jax.ShapeDtypeStruct(shape, a.dtype)

    def copies(self, src, dst, x, y, c):
        a, me_s, o = self.src, 2 * x + y, 1 - c
        chips = [(1 - x, y), (x, 1 - y), (1 - x, 1 - y)]
        if self.kind == "chips":
            return [(src.at[2 * px + py], dst.at[me_s], (px, py, c)) for px, py in chips], []
        if self.kind == "spread":
            return [(src, dst.at[me_s, c], (px, py, c)) for px, py in chips], [(src, dst.at[me_s, c])]
        depth = pl.ds(0, a.shape[0])
        if self.kind == "sib_fill":
            return [(dst.at[depth, c], dst.at[depth, c], (x, y, o))], []
        if self.kind == "sib_w_in":
            half, n = a.shape[1] // 2, a.shape[2] // 4
            return [(src.at[depth, pl.ds(o * half, half), pl.ds(s * n, n)], dst.at[depth, s], (x, y, o))
                    for s in range(4)], []
        if self.kind == "sib_w_out":
            return [(src.at[depth, pl.ds(0, 4), o], dst, (x, y, o))], []
        return [(src.at[o], dst, (x, y, o))], []

    def n_copies(self):
        return {"chips": 3, "spread": 3, "sib_w_in": 4}.get(self.kind, 1)


def _exchange_body(legs, srcs, dsts, send_sems, recv_sems, local_sems):
    x, y, c = _me()
    remote, local, k = [], [], 0
    for i, leg in enumerate(legs):
        far, near = leg.copies(srcs[i], dsts[i], x, y, c)
        for src, dst, to in far:
            remote.append(_remote(src, dst, send_sems.at[k], recv_sems.at[k], to))
            k += 1
        local += [pltpu.make_async_copy(src, dst, local_sems.at[i]) for src, dst in near]
    return remote, local


def _exchange_sems(legs):
    n = sum(leg.n_copies() for leg in legs)
    return [pltpu.SemaphoreType.DMA((n,)), pltpu.SemaphoreType.DMA((n,)), pltpu.SemaphoreType.DMA((len(legs),))]


def _exchange_aliases(legs, n_in, n_out):
    return {n_in + i: n_out + i for i, leg in enumerate(legs) if leg.kind == "sib_fill"}


def _pcall_ride(body, ride, *, grid, in_specs, out_specs, out_shape, args, scratch_shapes=(), **kw):
    n_in, n_out, n_scr = len(in_specs), len(out_specs), len(scratch_shapes)
    if not ride:
        res = _pcall(body, grid=grid, in_specs=in_specs, out_specs=out_specs, out_shape=out_shape,
                     scratch_shapes=list(scratch_shapes), **kw)(*args)
        return res, []
    nr = len(ride)

    def riding(*refs):
        ins, rsrc = refs[:n_in], refs[n_in:n_in + nr]
        outs, rdst = refs[n_in + nr:n_in + nr + n_out], refs[n_in + nr + n_out:n_in + 2 * nr + n_out]
        scr = refs[n_in + 2 * nr + n_out:n_in + 2 * nr + n_out + n_scr]
        copies, local = _exchange_body(ride, rsrc, rdst, *refs[n_in + 2 * nr + n_out + n_scr:])
        first = functools.reduce(jnp.logical_and, [pl.program_id(a) == 0 for a in range(len(grid))])
        last = functools.reduce(jnp.logical_and, [pl.program_id(a) == grid[a] - 1 for a in range(len(grid))])

        @pl.when(first)
        def _():
            for cp in copies + local:
                cp.start()

        body(*ins, *outs, *scr)

        @pl.when(last)
        def _():
            for cp in copies:
                cp.wait_recv()
            for cp in copies:
                cp.wait_send()
            for cp in local:
                cp.wait()

    hbm = pl.BlockSpec(memory_space=pltpu.HBM)
    aliases = {**kw.pop("input_output_aliases", {}), **_exchange_aliases(ride, n_in, n_out)}
    res = _pcall(
        riding, grid=grid, in_specs=list(in_specs) + [hbm] * nr, out_specs=list(out_specs) + [hbm] * nr,
        out_shape=list(out_shape) + [leg.landing() for leg in ride], input_output_aliases=aliases,
        scratch_shapes=list(scratch_shapes) + _exchange_sems(ride), **kw)(*args, *[leg.src for leg in ride])
    return res[:n_out], res[n_out:]


def _seq(n=1):
    return pltpu.CompilerParams(dimension_semantics=("arbitrary",) * n)


def _dot(a, b):
    return jnp.dot(a, b, preferred_element_type=F32)


def _dot_nt(a, b):
    return lax.dot_general(a, b, (((1,), (1,)), ((), ())), preferred_element_type=F32)


def _dot_tn(a, b):
    return lax.dot_general(a, b, (((0,), (0,)), ((), ())), preferred_element_type=F32)


def _bf(x):
    return x.astype(BF16)


def _sigmoid(x):
    return 0.5 * jnp.tanh(0.5 * x) + 0.5


def _log1p(z):
    u = 1.0 + z
    return jnp.where(u == 1.0, z, jnp.log(u) * (z / jnp.where(u == 1.0, 1.0, u - 1.0)))


def _softplus(x):
    return jnp.maximum(x, 0.0) + _log1p(jnp.exp(-jnp.abs(x)))


def _log_sigmoid(x):
    return -_softplus(-x)


def _one_minus_sq(a, log_a):
    x = 2.0 * log_a
    small = -x * (1.0 + x * (0.5 + x * (1.0 / 6.0)))
    return jnp.where(x > -0.004, small, 1.0 - a * a)


def _dsilu(x, s):
    return s * (1.0 + x * (1.0 - s))


def _rowsum(x):
    return jnp.sum(x, axis=1, keepdims=True)


def _colsum(x):
    return jnp.sum(x, axis=0, keepdims=True)


def _shift_down(win, s):
    return win if s == 0 else pltpu.roll(win, s, 0)


def _shift_up(win, s):
    return win if s == 0 else pltpu.roll(win, win.shape[0] - s, 0)


def _conv_taps(win):
    return [_shift_down(win, CONV_WIDTH - 1 - k)[HALO:] for k in range(CONV_WIDTH)]


def _conv_fwd(taps, w_ref, b_ref):
    acc = b_ref[...] + w_ref[CONV_WIDTH - 1:CONV_WIDTH, :] * taps[CONV_WIDTH - 1]
    for k in range(CONV_WIDTH - 1):
        acc = acc + w_ref[k:k + 1, :] * taps[k]
    return acc


def _split3(x):
    hi = _bf(x)
    r1 = x - hi.astype(F32)
    mid = _bf(r1)
    lo = _bf(r1 - mid.astype(F32))
    return hi, mid, lo


def _tri_dot_left(tri, x):
    hi, mid, lo = _split3(x)
    return _dot(tri, hi) + _dot(tri, mid) + _dot(tri, lo)


def _tri_dot_right(x, tri):
    hi, mid, lo = _split3(x)
    return _dot(hi, tri) + _dot(mid, tri) + _dot(lo, tri)


def _tile(n, want):
    t = min(n, want)
    assert n % t == 0
    return t


def _ln_inproj(x, g, scale, shift, w4, ride=None):
    s_len, d = x.shape
    nj, _, nsh = w4.shape
    tm = _tile(s_len, ROWS_MATMUL)

    def body(x_ref, g_ref, sc_ref, sh_ref, w_ref, h_ref, u_ref, hs):
        @pl.when(pl.program_id(1) == 0)
        def _():
            xv = x_ref[...]
            r = lax.rsqrt(jnp.mean(xv * xv, axis=-1, keepdims=True) + EPS)
            hv = (xv * r * g_ref[...]) * (1.0 + sc_ref[...]) + sh_ref[...]
            hs[...] = _bf(hv)
            h_ref[...] = hs[...]

        u_ref[...] = _bf(_dot(hs[...], w_ref[0]))

    vec = pl.BlockSpec((1, d), lambda i, j: (0, 0))
    return _pcall_ride(
        body, ride, name="ln_inproj", grid=(s_len // tm, nj),
        in_specs=[pl.BlockSpec((tm, d), lambda i, j: (i, 0)), vec, vec, vec,
                  pl.BlockSpec((1, d, nsh), lambda i, j: (j, 0, 0))],
        out_specs=[pl.BlockSpec((tm, d), lambda i, j: (i, 0)), pl.BlockSpec((tm, nsh), lambda i, j: (i, j))],
        out_shape=[jax.ShapeDtypeStruct((s_len, d), BF16), jax.ShapeDtypeStruct((s_len, nj * nsh), BF16)],
        scratch_shapes=[pltpu.VMEM((tm, d), BF16)],
        compiler_params=_seq(2),
        args=(x, g, scale, shift, w4))


def _rg_gates(xc, wa_ref, ba_ref, wx_ref, bx_ref, lam_ref):
    heads, hd, _ = wa_ref.shape
    xb = _bf(xc)
    ga = jnp.concatenate([_dot(xb[:, h * hd:(h + 1) * hd], wa_ref[h]) for h in range(heads)], axis=1) + ba_ref[...]
    gx = jnp.concatenate([_dot(xb[:, h * hd:(h + 1) * hd], wx_ref[h]) for h in range(heads)], axis=1) + bx_ref[...]
    r = _sigmoid(ga)
    ig = _sigmoid(gx)
    sp = _softplus(-lam_ref[...])
    log_a = (-RG_C) * r * sp
    a = jnp.exp(log_a)
    mult = jnp.sqrt(_one_minus_sq(a, log_a))
    return r, ig, sp, log_a, a, mult


def _scan_groups(a, u, reverse):
    n, c = a.shape
    a = a.reshape(n // 8, 8, c)
    u = u.reshape(n // 8, 8, c)
    row = lax.broadcasted_iota(jnp.int32, a.shape, 1)
    for k in (1, 2, 4):
        sft = 8 - k if reverse else k
        a_sh, u_sh = pltpu.roll(a, sft, 1), pltpu.roll(u, sft, 1)
        ok = row < 8 - k if reverse else row >= k
        u = jnp.where(ok, a * u_sh + u, u)
        a = jnp.where(ok, a * a_sh, a)
    return a.reshape(n, c), u.reshape(n, c)


def _rg_fwd(u, conv_w, conv_b, wa_b, ba, wx_b, bx, lam, ride=None):
    s_len = u.shape[0]
    d = conv_w.shape[1]
    tm = _tile(s_len, ROWS_VECTOR)
    per = tm // HALO

    def body(x_ref, xp_ref, z_ref, cw_ref, cb_ref, wa_ref, ba_ref, wx_ref, bx_ref, lam_ref,
             hh_ref, y_ref, carry):
        i = pl.program_id(0)

        @pl.when(i == 0)
        def _():
            carry[...] = jnp.zeros_like(carry)

        prev = jnp.where(i == 0, 0.0, xp_ref[...].astype(F32))
        xc = _conv_fwd(_conv_taps(jnp.concatenate([prev, x_ref[...].astype(F32)], axis=0)), cw_ref, cb_ref)
        _, ig, _, _, a, mult = _rg_gates(xc, wa_ref, ba_ref, wx_ref, bx_ref, lam_ref)
        ca, cu = _scan_groups(a, mult * (ig * xc), reverse=False)
        c = carry[0:1, :]
        for j in range(tm // 8):
            blk = ca[j * 8:(j + 1) * 8] * c + cu[j * 8:(j + 1) * 8]
            hh_ref[j * 8:(j + 1) * 8, :] = blk
            c = blk[7:8]
        carry[0:1, :] = c
        z = z_ref[...].astype(F32)
        y_ref[0] = _bf(hh_ref[...] * (z * _sigmoid(z)))

    vec = pl.BlockSpec((1, d), lambda i: (0, 0))
    whole3 = lambda a: pl.BlockSpec(a.shape, lambda i: (0, 0, 0))
    return _pcall_ride(
        body, ride, name="rg_fwd", grid=(s_len // tm,),
        in_specs=[pl.BlockSpec((tm, d), lambda i: (i, 0)),
                  pl.BlockSpec((HALO, d), lambda i: (jnp.maximum(i * per - 1, 0), 0)),
                  pl.BlockSpec((tm, d), lambda i: (i, 1)),
                  pl.BlockSpec((CONV_WIDTH, d), lambda i: (0, 0)), vec,
                  whole3(wa_b), vec, whole3(wx_b), vec, vec],
        out_specs=[pl.BlockSpec((tm, d), lambda i: (i, 0)), pl.BlockSpec((1, tm, d), lambda i: (0, i, 0))],
        out_shape=[jax.ShapeDtypeStruct((s_len, d), F32), jax.ShapeDtypeStruct((2, s_len, d), BF16)],
        scratch_shapes=[pltpu.VMEM((8, d), F32)],
        compiler_params=_seq(),
        args=(u, u, u, conv_w, conv_b, wa_b, ba, wx_b, bx, lam))


def _ml_pre(u, conv_w, conv_b, wqkv_b, wif_b, wift_b, b_if, b_ift):
    s_len = u.shape[0]
    d = conv_w.shape[1]
    _, heads, hd, _ = wqkv_b.shape
    ng = 2 * heads
    tm = _tile(s_len, max(ROWS_VECTOR, ML_CHUNK))
    per = tm // HALO

    def body(x_ref, xp_ref, cw_ref, cb_ref, w_ref, wif_ref, wift_ref, bif_ref, bift_ref,
             qkv_ref, gt_ref, gtt_ref, bc_ref, bct_ref):
        i = pl.program_id(0)
        prev = jnp.where(i == 0, 0.0, xp_ref[...].astype(F32))
        xm = x_ref[...].astype(F32)
        pre = _conv_fwd(_conv_taps(jnp.concatenate([prev, xm], axis=0)), cw_ref, cb_ref)
        xcb = _bf(pre * _sigmoid(pre))
        xmb = _bf(xm)
        for h in range(heads):
            hs = slice(h * hd, (h + 1) * hd)
            qkv_ref[0, :, hs] = _bf(_dot(xcb[:, hs], w_ref[0, h]))
            qkv_ref[1, :, hs] = _bf(_dot(xcb[:, hs], w_ref[1, h]))
            qkv_ref[2, :, hs] = _bf(_dot(xmb[:, hs], w_ref[2, h]))
        qb, kb, vb = qkv_ref[0], qkv_ref[1], qkv_ref[2]
        gt = (_dot(qb, wif_ref[0:d, :]) + _dot(kb, wif_ref[d:2 * d, :]) + _dot(vb, wif_ref[2 * d:3 * d, :])
              + bif_ref[...])
        gtt = (_dot_nt(wift_ref[:, 0:d], qb) + _dot_nt(wift_ref[:, d:2 * d], kb)
               + _dot_nt(wift_ref[:, 2 * d:3 * d], vb) + bift_ref[...])
        gt_ref[...] = gt
        gtt_ref[...] = gtt
        r = lax.broadcasted_iota(jnp.int32, (tm, tm), 0)
        c = lax.broadcasted_iota(jnp.int32, (tm, tm), 1)
        same = (r // ML_CHUNK) == (c // ML_CHUNK)
        bc_ref[...] = _tri_dot_left(((r >= c) & same).astype(BF16), _log_sigmoid(gt))
        bct_ref[...] = _tri_dot_right(_log_sigmoid(gtt), ((r <= c) & same).astype(BF16))

    vec = pl.BlockSpec((1, d), lambda i: (0, 0))
    whole2 = lambda a: pl.BlockSpec(a.shape, lambda i: (0, 0))
    col = pl.BlockSpec((tm, ng), lambda i: (i, 0))
    row = pl.BlockSpec((ng, tm), lambda i: (0, i))
    return _pcall(
        body, name="ml_pre", grid=(s_len // tm,),
        in_specs=[pl.BlockSpec((tm, d), lambda i: (i, 2)),
                  pl.BlockSpec((HALO, d), lambda i: (jnp.maximum(i * per - 1, 0), 2)),
                  pl.BlockSpec((CONV_WIDTH, d), lambda i: (0, 0)), vec,
                  pl.BlockSpec(wqkv_b.shape, lambda i: (0, 0, 0, 0)), whole2(wif_b), whole2(wift_b), whole2(b_if),
                  whole2(b_ift)],
        out_specs=[pl.BlockSpec((3, tm, d), lambda i: (0, i, 0)), col, row, col, row],
        out_shape=[jax.ShapeDtypeStruct((3, s_len, d), BF16), jax.ShapeDtypeStruct((s_len, ng), F32),
                   jax.ShapeDtypeStruct((ng, s_len), F32), jax.ShapeDtypeStruct((s_len, ng), F32),
                   jax.ShapeDtypeStruct((ng, s_len), F32)],
        compiler_params=_seq(),
    )(u, u, conv_w, conv_b, wqkv_b, wif_b, wift_b, b_if, b_ift)


def _chunk_gates(gt, gtt, bc, bct, h, heads):
    li_c = gt[:, h:h + 1]
    li_r = gtt[h:h + 1, :]
    gf_c = gt[:, heads + h:heads + h + 1]
    b_c = bc[:, heads + h:heads + h + 1]
    b_r = bct[heads + h:heads + h + 1, :]
    return li_c, li_r, gf_c, b_c, b_r


def _chunk_weights(li_c, li_r, b_c, b_r, m_prev, causal):
    lc = b_c.shape[0]
    b_last = b_c[lc - 1:lc, :]
    dmat = jnp.where(causal, b_c - b_r + li_r, -jnp.inf)
    m_inter = b_c + m_prev
    m_t = jnp.maximum(m_inter, jnp.max(dmat, axis=1, keepdims=True))
    w_intra = jnp.exp(dmat - m_t)
    w_inter = jnp.exp(m_inter - m_t)
    g_c = b_last - b_c + li_c
    m_new = jnp.maximum(b_last + m_prev, jnp.max(g_c, axis=0, keepdims=True))
    w_state = jnp.exp(g_c - m_new)
    decay = jnp.exp(b_last + m_prev - m_new)
    return m_t, w_intra, w_inter, m_new, w_state, decay


def _tri_masks(lc):
    r = lax.broadcasted_iota(jnp.int32, (lc, lc), 0)
    c = lax.broadcasted_iota(jnp.int32, (lc, lc), 1)
    causal = r >= c
    return causal, causal.astype(BF16), (r <= c).astype(BF16)


def _mlstm_fwd(qkv, gates, u, ml_g, ycat, ride=None):
    _, s_len, d = qkv.shape
    ng = gates[0].shape[1]
    heads = ng // 2
    hd = d // heads
    lc = ML_CHUNK
    nc = s_len // lc
    kscale = hd ** -0.5

    def body(qkv_ref, gt_ref, gtt_ref, bc_ref, bct_ref, o_ref, z_ref, g_ref, _, cell_ref, y_ref, cst_ref, nst_ref,
             mst_ref, cs, ns, ms):
        @pl.when(pl.program_id(0) == 0)
        def _():
            cs[...] = jnp.zeros_like(cs)
            ns[...] = jnp.zeros_like(ns)
            ms[...] = jnp.zeros_like(ms)

        causal = _tri_masks(lc)[0]
        gtv, gttv, bcv, bctv = gt_ref[...], gtt_ref[...], bc_ref[...], bct_ref[...]
        old = [(cs[h], ns[h], ms[h]) for h in range(heads)]
        new, cells, ys = [], [], []
        for h in range(heads):
            hs = slice(h * hd, (h + 1) * hd)
            li_c, li_r, _, b_c, b_r = _chunk_gates(gtv, gttv, bcv, bctv, h, heads)
            c_old, n_old, m_old = old[h]
            m_prev = m_old[:, 0:1]
            m_t, w_intra, w_inter, m_new, w_state, decay = _chunk_weights(li_c, li_r, b_c, b_r, m_prev, causal)
            qb = qkv_ref[0, :, hs]
            ks = qkv_ref[1, :, hs].astype(F32) * kscale
            kb = _bf(ks)
            vb = qkv_ref[2, :, hs]
            s = _dot_nt(qb, kb) * w_intra
            num = _dot(_bf(s), vb) + w_inter * _dot(qb, _bf(c_old))
            den = _rowsum(s) + w_inter * _rowsum(qb.astype(F32) * n_old)
            cell = num / jnp.maximum(jnp.abs(den), jnp.exp(-m_t))
            kw = ks * w_state
            new.append((decay * c_old + _dot_tn(_bf(kw), vb), decay * n_old + _colsum(kw),
                        jnp.broadcast_to(m_new, m_old.shape)))
            cells.append(cell)
            hm = _sigmoid(o_ref[:, hs].astype(F32)) * cell
            hn = hm * lax.rsqrt(jnp.mean(hm * hm, axis=-1, keepdims=True) + EPS)
            z = z_ref[:, hs].astype(F32)
            ys.append(_bf((hn * g_ref[:, hs]) * (z * _sigmoid(z))))
        for h in range(heads):
            cst_ref[0, h] = _bf(old[h][0])
            nst_ref[0, h] = old[h][1]
            mst_ref[0, h] = old[h][2]
            cs[h], ns[h], ms[h] = new[h]
        cell_ref[...] = jnp.concatenate(cells, axis=1)
        y_ref[0] = jnp.concatenate(ys, axis=1)

    row = pl.BlockSpec((lc, d), lambda c: (c, 0))
    gcol = pl.BlockSpec((lc, ng), lambda c: (c, 0))
    grow = pl.BlockSpec((ng, lc), lambda c: (0, c))
    return _pcall_ride(
        body, ride, name="mlstm_fwd", grid=(nc,),
        in_specs=[pl.BlockSpec((3, lc, d), lambda c: (0, c, 0)), gcol, grow, gcol, grow,
                  pl.BlockSpec((lc, d), lambda c: (c, 3)), pl.BlockSpec((lc, d), lambda c: (c, 4)),
                  pl.BlockSpec((1, d), lambda c: (0, 0)), pl.BlockSpec(memory_space=pl.ANY)],
        out_specs=[row, pl.BlockSpec((1, lc, d), lambda c: (1, c, 0)),
                   pl.BlockSpec((1, heads, hd, hd), lambda c: (c, 0, 0, 0)),
                   pl.BlockSpec((1, heads, 1, hd), lambda c: (c, 0, 0, 0)),
                   pl.BlockSpec((1, heads, 1, 128), lambda c: (c, 0, 0, 0))],
        out_shape=[jax.ShapeDtypeStruct((s_len, d), F32), jax.ShapeDtypeStruct(ycat.shape, BF16),
                   jax.ShapeDtypeStruct((nc, heads, hd, hd), BF16),
                   jax.ShapeDtypeStruct((nc, heads, 1, hd), F32),
                   jax.ShapeDtypeStruct((nc, heads, 1, 128), F32)],
        scratch_shapes=[pltpu.VMEM((heads, hd, hd), F32), pltpu.VMEM((heads, 1, hd), F32),
                        pltpu.VMEM((heads, 1, 128), F32)],
        input_output_aliases={8: 1},
        compiler_params=_seq(),
        args=(qkv, *gates, u, u, ml_g, ycat))


def _out_proj(ycat, w_out_b, x, gate, ride=None):
    s_len, d = x.shape
    tm = _tile(s_len, ROWS_MATMUL)

    def body(a_ref, w_ref, x_ref, g_ref, y_ref, xn_ref):
        y = _dot(a_ref[0], w_ref[0:d, :]) + _dot(a_ref[1], w_ref[d:2 * d, :])
        y_ref[...] = y
        xn_ref[...] = x_ref[...] + g_ref[...] * y

    row = pl.BlockSpec((tm, d), lambda i: (i, 0))
    return _pcall_ride(
        body, ride, name="out_proj", grid=(s_len // tm,),
        in_specs=[pl.BlockSpec((2, tm, d), lambda i: (0, i, 0)), pl.BlockSpec((2 * d, d), lambda i: (0, 0)), row,
                  pl.BlockSpec((1, d), lambda i: (0, 0))],
        out_specs=[row, row],
        out_shape=[jax.ShapeDtypeStruct((s_len, d), F32)] * 2,
        compiler_params=_seq(),
        args=(ycat, w_out_b, x, gate))


def _final_loss(x, g, target):
    s_len, d = x.shape
    tm = _tile(s_len, ROWS_VECTOR)

    def body(x_ref, g_ref, t_ref, dx_ref, dg_ref, loss_ref):
        @pl.when(pl.program_id(0) == 0)
        def _():
            dg_ref[...] = jnp.zeros_like(dg_ref)
            loss_ref[...] = jnp.zeros_like(loss_ref)

        xv = x_ref[...]
        r = lax.rsqrt(jnp.mean(xv * xv, axis=-1, keepdims=True) + EPS)
        xn = xv * r
        err = xn * g_ref[...] - t_ref[...]
        loss_ref[...] += 0.5 * jnp.sum(jnp.mean(err * err, axis=-1, keepdims=True))
        dout = err * (1.0 / d)
        dg_ref[...] += _colsum(dout * xn)
        dxn = dout * g_ref[...]
        dx_ref[...] = r * (dxn - xn * jnp.mean(dxn * xn, axis=-1, keepdims=True))

    row = pl.BlockSpec((tm, d), lambda i: (i, 0))
    vec = pl.BlockSpec((1, d), lambda i: (0, 0))
    return _pcall(
        body, name="final_loss", grid=(s_len // tm,),
        in_specs=[row, vec, row],
        out_specs=[row, vec, pl.BlockSpec((1, 128), lambda i: (0, 0))],
        out_shape=[jax.ShapeDtypeStruct((s_len, d), F32), jax.ShapeDtypeStruct((1, d), F32),
                   jax.ShapeDtypeStruct((1, 128), F32)],
        compiler_params=_seq(),
    )(x, g, target)


def _out_bwd(dxn, y, gate, w_out_b):
    s_len, d = dxn.shape
    tm = _tile(s_len, ROWS_MATMUL)

    def body(dx_ref, y_ref, g_ref, w_ref, dg_ref, dy_ref, dc_ref):
        @pl.when(pl.program_id(0) == 0)
        def _():
            dg_ref[...] = jnp.zeros_like(dg_ref)

        dx = dx_ref[...]
        dg_ref[...] += _colsum(dx * y_ref[...])
        dy = _bf(g_ref[...] * dx)
        dy_ref[...] = dy
        dc_ref[0] = _dot_nt(dy, w_ref[0:d, :])
        dc_ref[1] = _dot_nt(dy, w_ref[d:2 * d, :])

    row = pl.BlockSpec((tm, d), lambda i: (i, 0))
    vec = pl.BlockSpec((1, d), lambda i: (0, 0))
    return _pcall(
        body, name="out_bwd", grid=(s_len // tm,),
        in_specs=[row, row, vec, pl.BlockSpec((2 * d, d), lambda i: (0, 0))],
        out_specs=[vec, row, pl.BlockSpec((2, tm, d), lambda i: (0, i, 0))],
        out_shape=[jax.ShapeDtypeStruct((1, d), F32), jax.ShapeDtypeStruct((s_len, d), BF16),
                   jax.ShapeDtypeStruct((2, s_len, d), F32)],
        compiler_params=_seq(),
    )(dxn, y, gate, w_out_b)


def _grad_matmul(a3, b3, nblk, a_idx, b_idx, out_shape, out_block, out_idx, ride=None):
    _, s_len, m = a3.shape
    n = b3.shape[2]
    tk = _tile(s_len, ROWS_GRAD_MATMUL)

    def body(a_ref, b_ref, o_ref):
        @pl.when(pl.program_id(1) == 0)
        def _():
            o_ref[...] = jnp.zeros_like(o_ref)

        o_ref[...] += _dot_tn(a_ref[0], b_ref[0])

    (out,), got = _pcall_ride(
        body, ride, name="grad_matmul", grid=(nblk, s_len // tk),
        in_specs=[pl.BlockSpec((1, tk, m), lambda p, t: (a_idx(p), t, 0)),
                  pl.BlockSpec((1, tk, n), lambda p, t: (b_idx(p), t, 0))],
        out_specs=[pl.BlockSpec((None,) + out_block, lambda p, t: (0,) + out_idx(p))],
        out_shape=[jax.ShapeDtypeStruct((1,) + out_shape, F32)],
        compiler_params=_seq(2), args=(a3, b3))
    return out, got


DU_PLANE = (2, 3, 4, 0, 1)


def _mlstm_bwd(qkv, gates, cst, nst, mst, cell, u, ml_g, d_ycat, wif_b, ride=None):
    _, s_len, d = qkv.shape
    ng = gates[0].shape[1]
    heads = ng // 2
    hd = d // heads
    lc = ML_CHUNK
    nc = s_len // lc
    kscale = hd ** -0.5

    def body(qkv_ref, gt_ref, gtt_ref, bc_ref, bct_ref, cst_ref, nst_ref, mst_ref, cell_ref, o_ref, z_ref, g_ref, dy_ref,
             wif_ref, dqkv_ref, dgt_ref, dbif_ref, du_ref, dg_ref, dcs, dns):
        @pl.when(pl.program_id(0) == 0)
        def _():
            dbif_ref[...] = jnp.zeros_like(dbif_ref)
            dcs[...] = jnp.zeros_like(dcs)
            dns[...] = jnp.zeros_like(dns)
            dg_ref[...] = jnp.zeros_like(dg_ref)

        causal, tril, triu = _tri_masks(lc)
        tril_strict = (tril.astype(F32) - (tril * triu).astype(F32)).astype(BF16)
        gtv, gttv, bcv, bctv = gt_ref[...], gtt_ref[...], bc_ref[...], bct_ref[...]
        lane = lax.broadcasted_iota(jnp.int32, (lc, ng), 1)
        dli_all = jnp.zeros((lc, ng), F32)
        from_later = jnp.zeros((lc, ng), F32)
        from_earlier = jnp.zeros((lc, ng), F32)
        across_all = jnp.zeros((1, ng), F32)
        old = [(dcs[h], dns[h]) for h in range(heads)]
        new, d_o, d_z, d_g, dqs, dks, dvs = [], [], [], [], [], [], []
        for h in range(heads):
            hs = slice(h * hd, (h + 1) * hd)
            li_c, li_r, gf_c, b_c, b_r = _chunk_gates(gtv, gttv, bcv, bctv, h, heads)
            m_prev = mst_ref[0, h][:, 0:1]
            m_t, w_intra, w_inter, _, w_state, decay = _chunk_weights(li_c, li_r, b_c, b_r, m_prev, causal)
            qb = qkv_ref[0, :, hs]
            qf = qb.astype(F32)
            ks = qkv_ref[1, :, hs].astype(F32) * kscale
            kb = _bf(ks)
            vb = qkv_ref[2, :, hs]
            c_b = cst_ref[0, h]
            n_old = nst_ref[0, h]
            s = _dot_nt(qb, kb) * w_intra
            den = _rowsum(s) + w_inter * _rowsum(qf * n_old)
            floor = jnp.exp(-m_t)
            dstab = jnp.maximum(jnp.abs(den), floor)
            cell = cell_ref[:, hs]
            o = o_ref[:, hs].astype(F32)
            so = _sigmoid(o)
            hm = so * cell
            rinv = lax.rsqrt(jnp.mean(hm * hm, axis=-1, keepdims=True) + EPS)
            hn = hm * rinv
            z = z_ref[:, hs].astype(F32)
            sgz = _sigmoid(z)
            sz = z * sgz
            gh = g_ref[:, hs]
            dy = dy_ref[0, :, hs]
            d_z.append(_bf(dy * (hn * gh) * _dsilu(z, sgz)))
            d_g.append(_colsum(dy * hn * sz))
            dhn = dy * gh * sz
            dhm = rinv * (dhn - hn * jnp.mean(dhn * hn, axis=-1, keepdims=True))
            d_o.append(_bf(dhm * cell * so * (1.0 - so)))
            dcell = dhm * so
            dnum = dcell / dstab
            dnb = _bf(dnum)
            dden = -_rowsum(dcell * cell) / dstab * jnp.where(jnp.abs(den) > floor, jnp.where(den > 0.0, 1.0, -1.0), 0.0)
            dst = _dot_nt(dnb, vb) + dden
            dsdb = _bf(dst * w_intra)
            dc_out, dn_out = old[h]
            dcb = _bf(dc_out)
            dq_inter = w_inter * (_dot_nt(dnb, c_b) + dden * n_old)
            dk_inter = w_state * (_dot_nt(vb, dcb) + dn_out)
            dq = _dot(dsdb, kb) + dq_inter
            dk = _dot_tn(dsdb, qb) + dk_inter
            dv = _dot_tn(_bf(s), dnb) + _dot(_bf(ks * w_state), dcb)
            wq = w_inter * qf
            new.append((decay * dc_out + _dot_tn(_bf(wq), dnb), decay * dn_out + _colsum(wq * dden)))
            pmat = dst * s
            p_rows = _rowsum(pmat)
            p_cols = _rowsum(pmat.T)
            q_in = _rowsum(qf * dq_inter)
            k_in = _rowsum(ks * dk_inter)
            across = decay * (jnp.sum(dc_out * c_b.astype(F32), keepdims=True) + jnp.sum(dn_out * n_old, keepdims=True))
            dli_all = dli_all + jnp.where(lane == h, p_cols + k_in, 0.0)
            from_later = from_later + jnp.where(lane == heads + h, p_rows - p_cols + q_in, 0.0)
            from_earlier = from_earlier + jnp.where(lane == heads + h, k_in, 0.0)
            across_all = across_all + jnp.where(lane[0:1] == heads + h, across, 0.0)
            dqs.append(dq)
            dks.append(dk * kscale)
            dvs.append(dv)
        for h in range(heads):
            dcs[h], dns[h] = new[h]
        du_ref[0] = jnp.concatenate(d_o, axis=1)
        du_ref[1] = jnp.concatenate(d_z, axis=1)
        dg_ref[...] += jnp.concatenate(d_g, axis=1)
        dlf = _tri_dot_left(triu, from_later) + _tri_dot_left(tril_strict, from_earlier) + across_all
        dgt = dli_all + dlf * _sigmoid(-gtv)
        dgt_ref[...] = dgt
        dbif_ref[...] += _colsum(dgt)
        dgb = _bf(dgt)
        dqkv_ref[0] = _bf(jnp.concatenate(dqs, axis=1) + _dot_nt(dgb, wif_ref[0:d, :]))
        dqkv_ref[1] = _bf(jnp.concatenate(dks, axis=1) + _dot_nt(dgb, wif_ref[d:2 * d, :]))
        dqkv_ref[2] = _bf(jnp.concatenate(dvs, axis=1) + _dot_nt(dgb, wif_ref[2 * d:3 * d, :]))

    rev = lambda c: nc - 1 - c
    row = pl.BlockSpec((lc, d), lambda c: (rev(c), 0))
    gcol = pl.BlockSpec((lc, ng), lambda c: (rev(c), 0))
    grow = pl.BlockSpec((ng, lc), lambda c: (0, rev(c)))
    return _pcall_ride(
        body, ride, name="mlstm_bwd", grid=(nc,),
        in_specs=[pl.BlockSpec((3, lc, d), lambda c: (0, rev(c), 0)), gcol, grow, gcol, grow,
                  pl.BlockSpec((1, heads, hd, hd), lambda c: (rev(c), 0, 0, 0)),
                  pl.BlockSpec((1, heads, 1, hd), lambda c: (rev(c), 0, 0, 0)),
                  pl.BlockSpec((1, heads, 1, 128), lambda c: (rev(c), 0, 0, 0)),
                  row, pl.BlockSpec((lc, d), lambda c: (rev(c), 3)), pl.BlockSpec((lc, d), lambda c: (rev(c), 4)),
                  pl.BlockSpec((1, d), lambda c: (0, 0)), pl.BlockSpec((1, lc, d), lambda c: (1, rev(c), 0)),
                  pl.BlockSpec((3 * d, ng), lambda c: (0, 0))],
        out_specs=[pl.BlockSpec((3, lc, d), lambda c: (0, rev(c), 0)), pl.BlockSpec((lc, ng), lambda c: (rev(c), 0)),
                   pl.BlockSpec((1, ng), lambda c: (0, 0)), pl.BlockSpec((2, lc, d), lambda c: (0, rev(c), 0)),
                   pl.BlockSpec((1, d), lambda c: (0, 0))],
        out_shape=[jax.ShapeDtypeStruct((3, s_len, d), BF16), jax.ShapeDtypeStruct((s_len, ng), F32),
                   jax.ShapeDtypeStruct((1, ng), F32), jax.ShapeDtypeStruct((5, s_len, d), BF16),
                   jax.ShapeDtypeStruct((1, d), F32)],
        scratch_shapes=[pltpu.VMEM((heads, hd, hd), F32), pltpu.VMEM((heads, 1, hd), F32)],
        compiler_params=_seq(),
        args=(qkv, *gates, cst, nst, mst, cell, u, u, ml_g, d_ycat, wif_b))


def _conv_bwd_tile(dp, later, taps, cw_ref, gw_ref, gb_ref):
    tm = dp.shape[0]
    dwin = jnp.concatenate([dp, later[...]], axis=0)
    later[...] = dp[0:HALO]
    acc = cw_ref[CONV_WIDTH - 1:CONV_WIDTH, :] * dp
    for k in range(CONV_WIDTH):
        if k < CONV_WIDTH - 1:
            acc = acc + cw_ref[k:k + 1, :] * _shift_up(dwin, CONV_WIDTH - 1 - k)[0:tm]
        gw_ref[k:k + 1, :] += _colsum(dp * taps[k])
    gb_ref[...] += _colsum(dp)
    return acc


def _ml_pre_bwd(dqkv, u, conv_w, conv_b, wqkv_b, du):
    s_len = u.shape[0]
    d = conv_w.shape[1]
    _, heads, hd, _ = wqkv_b.shape
    tm = _tile(s_len, ROWS_VECTOR)
    per = tm // HALO
    nt = s_len // tm

    def body(dqkv_ref, x_ref, xp_ref, cw_ref, cb_ref, w_ref, _, dx_ref, gw_ref, gcw_ref, gcb_ref, later, dps, dxs):
        i = pl.program_id(0)

        @pl.when(i == 0)
        def _():
            gw_ref[...] = jnp.zeros_like(gw_ref)
            gcw_ref[...] = jnp.zeros_like(gcw_ref)
            gcb_ref[...] = jnp.zeros_like(gcb_ref)
            later[...] = jnp.zeros_like(later)

        prev = jnp.where(i == nt - 1, 0.0, xp_ref[...].astype(F32))
        xm = x_ref[...].astype(F32)
        taps = _conv_taps(jnp.concatenate([prev, xm], axis=0))
        pre = _conv_fwd(taps, cw_ref, cb_ref)
        sg = _sigmoid(pre)
        xcb = _bf(pre * sg)
        xmb = _bf(xm)
        for h in range(heads):
            hs = slice(h * hd, (h + 1) * hd)
            dqh, dkh, dvh = dqkv_ref[0, :, hs], dqkv_ref[1, :, hs], dqkv_ref[2, :, hs]
            dxc = _dot_nt(dqh, w_ref[0, h]) + _dot_nt(dkh, w_ref[1, h])
            dps[:, hs] = dxc * _dsilu(pre[:, hs], sg[:, hs])
            dxs[:, hs] = _dot_nt(dvh, w_ref[2, h])
            gw_ref[0, h] += _dot_tn(xcb[:, hs], dqh)
            gw_ref[1, h] += _dot_tn(xcb[:, hs], dkh)
            gw_ref[2, h] += _dot_tn(xmb[:, hs], dvh)
        dx_ref[0] = _bf(_conv_bwd_tile(dps[...], later, taps, cw_ref, gcw_ref, gcb_ref) + dxs[...])

    rev = lambda i: nt - 1 - i
    vec = pl.BlockSpec((1, d), lambda i: (0, 0))
    cwb = pl.BlockSpec((CONV_WIDTH, d), lambda i: (0, 0))
    whole4 = pl.BlockSpec(wqkv_b.shape, lambda i: (0, 0, 0, 0))
    return _pcall(
        body, name="ml_pre_bwd", grid=(nt,),
        in_specs=[pl.BlockSpec((3, tm, d), lambda i: (0, rev(i), 0)), pl.BlockSpec((tm, d), lambda i: (rev(i), 2)),
                  pl.BlockSpec((HALO, d), lambda i: (jnp.maximum(rev(i) * per - 1, 0), 2)),
                  cwb, vec, whole4, pl.BlockSpec(memory_space=pl.ANY)],
        out_specs=[pl.BlockSpec((1, tm, d), lambda i: (DU_PLANE[2], rev(i), 0)), whole4, cwb, vec],
        out_shape=[jax.ShapeDtypeStruct(du.shape, BF16), jax.ShapeDtypeStruct(wqkv_b.shape, F32),
                   jax.ShapeDtypeStruct((CONV_WIDTH, d), F32), jax.ShapeDtypeStruct((1, d), F32)],
        scratch_shapes=[pltpu.VMEM((HALO, d), F32), pltpu.VMEM((tm, d), F32), pltpu.VMEM((tm, d), F32)],
        input_output_aliases={6: 0},
        compiler_params=_seq(),
    )(dqkv, u, u, conv_w, conv_b, wqkv_b, du)


def _rg_bwd(d_ycat, u, hh, conv_w, conv_b, wa_b, ba, wx_b, bx, lam, du):
    s_len = u.shape[0]
    d = conv_w.shape[1]
    heads, hd, _ = wa_b.shape
    tm = _tile(s_len, ROWS_VECTOR)
    per = tm // HALO
    nt = s_len // tm

    def body(dy_ref, x_ref, xp_ref, z_ref, hh_ref, hp_ref, cw_ref, cb_ref, wa_ref, ba_ref, wx_ref, bx_ref, lam_ref, _,
             du_ref, gwa_ref, gwx_ref, gba_ref, gbx_ref, glam_ref, gcw_ref, gcb_ref, carry, gbuf, later, dxcs):
        i = pl.program_id(0)
        first = i == nt - 1

        @pl.when(i == 0)
        def _():
            carry[...] = jnp.zeros_like(carry)
            later[...] = jnp.zeros_like(later)
            gwa_ref[...] = jnp.zeros_like(gwa_ref)
            gwx_ref[...] = jnp.zeros_like(gwx_ref)
            gba_ref[...] = jnp.zeros_like(gba_ref)
            gbx_ref[...] = jnp.zeros_like(gbx_ref)
            glam_ref[...] = jnp.zeros_like(glam_ref)
            gcw_ref[...] = jnp.zeros_like(gcw_ref)
            gcb_ref[...] = jnp.zeros_like(gcb_ref)

        prev = jnp.where(first, 0.0, xp_ref[...].astype(F32))
        taps = _conv_taps(jnp.concatenate([prev, x_ref[...].astype(F32)], axis=0))
        xc = _conv_fwd(taps, cw_ref, cb_ref)
        r, ig, sp, log_a, a, mult = _rg_gates(xc, wa_ref, ba_ref, wx_ref, bx_ref, lam_ref)
        z = z_ref[...].astype(F32)
        sgz = _sigmoid(z)
        dy = dy_ref[0]
        hh_v = hh_ref[...]
        du_ref[1] = _bf(dy * hh_v * _dsilu(z, sgz))
        dhh = dy * (z * sgz)
        rows = lax.broadcasted_iota(jnp.int32, a.shape, 0)
        coef = jnp.where(rows == tm - 1, carry[1:2, :], _shift_up(a, 1))
        ca, cu = _scan_groups(coef, dhh, reverse=True)
        c = carry[0:1, :]
        for j in range(tm // 8 - 1, -1, -1):
            blk = ca[j * 8:(j + 1) * 8] * c + cu[j * 8:(j + 1) * 8]
            gbuf[j * 8:(j + 1) * 8, :] = blk
            c = blk[0:1]
        carry[0:1, :] = c
        carry[1:2, :] = a[0:1]
        g = gbuf[...]
        hprev_tile = jnp.where(first, 0.0, hp_ref[...])
        hprev = _shift_down(jnp.concatenate([hprev_tile, hh_v], axis=0), 1)[HALO:]
        da = g * hprev
        gx_ = g * xc
        d_mult = gx_ * ig
        d_ig = gx_ * mult
        dxc = g * mult * ig
        dlog_a = da * a - d_mult * (a * a / mult)
        d_r = dlog_a * ((-RG_C) * sp)
        glam_ref[...] += _colsum(dlog_a * ((-RG_C) * r)) * (-_sigmoid(-lam_ref[...]))
        d_ga = d_r * r * (1.0 - r)
        d_gx = d_ig * ig * (1.0 - ig)
        gba_ref[...] += _colsum(d_ga)
        gbx_ref[...] += _colsum(d_gx)
        xb = _bf(xc)
        dgab = _bf(d_ga)
        dgxb = _bf(d_gx)
        for h in range(heads):
            hs = slice(h * hd, (h + 1) * hd)
            dxcs[:, hs] = dxc[:, hs] + _dot_nt(dgab[:, hs], wa_ref[h]) + _dot_nt(dgxb[:, hs], wx_ref[h])
            gwa_ref[h] += _dot_tn(xb[:, hs], dgab[:, hs])
            gwx_ref[h] += _dot_tn(xb[:, hs], dgxb[:, hs])
        du_ref[0] = _bf(_conv_bwd_tile(dxcs[...], later, taps, cw_ref, gcw_ref, gcb_ref))

    assert DU_PLANE[0] % 2 == 0 and DU_PLANE[1] == DU_PLANE[0] + 1
    rev = lambda i: nt - 1 - i
    row = pl.BlockSpec((tm, d), lambda i: (rev(i), 0))
    halo_prev = lambda col: pl.BlockSpec((HALO, d), lambda i: (jnp.maximum(rev(i) * per - 1, 0), col))
    vec = pl.BlockSpec((1, d), lambda i: (0, 0))
    cwb = pl.BlockSpec((CONV_WIDTH, d), lambda i: (0, 0))
    whole3 = lambda a: pl.BlockSpec(a.shape, lambda i: (0, 0, 0))
    return _pcall(
        body, name="rg_bwd", grid=(nt,),
        in_specs=[pl.BlockSpec((1, tm, d), lambda i: (0, rev(i), 0)), row, halo_prev(0),
                  pl.BlockSpec((tm, d), lambda i: (rev(i), 1)), row, halo_prev(0),
                  cwb, vec, whole3(wa_b), vec, whole3(wx_b), vec, vec, pl.BlockSpec(memory_space=pl.ANY)],
        out_specs=[pl.BlockSpec((2, tm, d), lambda i: (DU_PLANE[0] // 2, rev(i), 0)), whole3(wa_b), whole3(wa_b),
                   vec, vec, vec, cwb, vec],
        out_shape=[jax.ShapeDtypeStruct(du.shape, BF16), jax.ShapeDtypeStruct(wa_b.shape, F32),
                   jax.ShapeDtypeStruct(wa_b.shape, F32)] + [jax.ShapeDtypeStruct((1, d), F32)] * 3
        + [jax.ShapeDtypeStruct((CONV_WIDTH, d), F32), jax.ShapeDtypeStruct((1, d), F32)],
        scratch_shapes=[pltpu.VMEM((8, d), F32), pltpu.VMEM((tm, d), F32), pltpu.VMEM((HALO, d), F32),
                        pltpu.VMEM((tm, d), F32)],
        input_output_aliases={13: 0},
        compiler_params=_seq(),
    )(d_ycat, u, u, u, hh, hh, conv_w, conv_b, wa_b, ba, wx_b, bx, lam, du)


def _in_bwd(du, w4, x, dxn, g, scale, ride=None):
    s_len, d = x.shape
    tm = _tile(s_len, ROWS_IN_BWD)
    nsh_chips, _, nsh = w4.shape
    npc = du.shape[0]
    ck = d // 4
    assert nsh % ck == 0 and npc * d == nsh_chips * nsh

    def body(du_ref, w_ref, x_ref, dxn_ref, g_ref, sc_ref, dx_ref, dsh_ref, dsc_ref, dg_ref):
        @pl.when(pl.program_id(0) == 0)
        def _():
            dsh_ref[...] = jnp.zeros_like(dsh_ref)
            dsc_ref[...] = jnp.zeros_like(dsc_ref)
            dg_ref[...] = jnp.zeros_like(dg_ref)

        dh = None
        for q in range(npc * d // ck):
            col = q * ck
            p, pc = col // d, col % d
            s, sc = col // nsh, col % nsh
            t = _dot_nt(du_ref[DU_PLANE[p], :, pc:pc + ck], w_ref[s, :, sc:sc + ck])
            dh = t if dh is None else dh + t
        xv = x_ref[...]
        r = lax.rsqrt(jnp.mean(xv * xv, axis=-1, keepdims=True) + EPS)
        xn = xv * r
        gv = g_ref[...]
        onesc = 1.0 + sc_ref[...]
        dsh_ref[...] += _colsum(dh)
        dsc_ref[...] += _colsum(dh * (xn * gv))
        dg_ref[...] += _colsum(dh * xn * onesc)
        dxh = dh * (gv * onesc)
        dx_ref[...] = dxn_ref[...] + r * (dxh - xn * jnp.mean(dxh * xn, axis=-1, keepdims=True))

    row = pl.BlockSpec((tm, d), lambda i: (i, 0))
    vec = pl.BlockSpec((1, d), lambda i: (0, 0))
    return _pcall_ride(
        body, ride, name="in_bwd", grid=(s_len // tm,),
        in_specs=[pl.BlockSpec((npc, tm, d), lambda i: (0, i, 0)), pl.BlockSpec(w4.shape, lambda i: (0, 0, 0)), row, row,
                  vec, vec],
        out_specs=[row, vec, vec, vec],
        out_shape=[jax.ShapeDtypeStruct((s_len, d), F32)] + [jax.ShapeDtypeStruct((1, d), F32)] * 3,
        compiler_params=_seq(),
        args=(du, w4, x, dxn, g, scale))


def _layer_fwd(x, p, rides=None):
    rides = rides or {}
    landed = {}
    ride = lambda kernel: rides[kernel](landed) if kernel in rides else None
    (h_b, u), landed["ln_inproj"] = _ln_inproj(x, p["norm_g"], p["scale"], p["shift"], p["w4"], ride("ln_inproj"))
    (hh, ycat), landed["rg_fwd"] = _rg_fwd(u, p["rg_conv_w"], p["rg_conv_b"], p["rg_wa_b"], p["rg_ba"], p["rg_wx_b"],
                                           p["rg_bx"], p["rg_lam"], ride("rg_fwd"))
    if "late" in rides:
        p = {**p, **rides["late"](landed)}
    qkv, *gates = _ml_pre(u, p["ml_conv_w"], p["ml_conv_b"], p["wqkv_b"], p["wif_b"], p["wift_b"], p["b_if"],
                          p["b_ift"])
    (cell, ycat, cst, nst, mst), landed["mlstm_fwd"] = _mlstm_fwd(qkv, gates, u, p["ml_g"], ycat, ride("mlstm_fwd"))
    (y, x_new), landed["out_proj"] = _out_proj(ycat, p["w_out_b"], x, p["gate"], ride("out_proj"))
    saved = dict(x=x, h_b=h_b, u=u, hh=hh, qkv=qkv, gates=gates, cell=cell, ycat=ycat, cst=cst, nst=nst, mst=mst, y=y)
    return x_new, saved, p, landed


def _layer_bwd(dxn, p, s, rides=None):
    rides = rides or {}
    landed = {}
    ride = lambda kernel: rides[kernel](grads, landed) if kernel in rides else None
    u = s["u"]
    d = dxn.shape[1]
    d_gate, dy_b, d_ycat = _out_bwd(dxn, s["y"], p["gate"], p["w_out_b"])
    grads = dict(w_out=_grad_matmul(s["ycat"], dy_b[None], 2, lambda b: b, lambda b: 0, (2 * d, d), (d, d),
                                    lambda b: (b, 0))[0])
    (dqkv, dgt, g_b_if, du, g_ml_g), landed["mlstm_bwd"] = _mlstm_bwd(
        s["qkv"], s["gates"], s["cst"], s["nst"], s["mst"], s["cell"], u, p["ml_g"], d_ycat, p["wif_b"],
        ride("mlstm_bwd"))
    ng = dgt.shape[1]
    g_w_if = _grad_matmul(s["qkv"], _bf(dgt)[None], 3, lambda b: b, lambda b: 0, (3 * d, ng), (d, ng),
                          lambda b: (b, 0))[0][0]
    du, g_wqkv, g_ml_cw, g_ml_cb = _ml_pre_bwd(dqkv, u, p["ml_conv_w"], p["ml_conv_b"], p["wqkv_b"], du)
    du, g_wa, g_wx, g_ba, g_bx, g_lam, g_rg_cw, g_rg_cb = _rg_bwd(d_ycat, u, s["hh"], p["rg_conv_w"], p["rg_conv_b"],
                                                                  p["rg_wa_b"], p["rg_ba"], p["rg_wx_b"], p["rg_bx"],
                                                                  p["rg_lam"], du)
    grads.update(rg_conv_w=g_rg_cw, rg_conv_b=g_rg_cb, rg_w_a=g_wa, rg_b_a=g_ba, rg_w_x=g_wx, rg_b_x=g_bx,
                 rg_lambda=g_lam, ml_conv_w=g_ml_cw, ml_conv_b=g_ml_cb, ml_w_qkv=g_wqkv, ml_w_if=g_w_if, ml_b_if=g_b_if,
                 ml_norm_g=g_ml_g)
    npc = du.shape[0]
    grads["w_in"], landed["grad_w_in"] = _grad_matmul(
        s["h_b"][None], du, npc, lambda b: 0, lambda b: (b + DU_PLANE[0]) % npc, (d, npc * d), (d, d),
        lambda b: (0, b), ride("grad_w_in"))
    (dx, d_shift, d_scale, grads["norm_g"]), landed["in_bwd"] = _in_bwd(du, p["w4"], s["x"], dxn, p["norm_g"],
                                                                        p["scale"], ride("in_bwd"))
    return dx, grads, jnp.concatenate([d_shift, d_scale, d_gate], axis=1), landed


def _me():
    return lax.axis_index("x"), lax.axis_index("y"), lax.axis_index("c")


def _remote(src, dst, send_sem, recv_sem, to):
    return pltpu.make_async_remote_copy(src_ref=src, dst_ref=dst, send_sem=send_sem, recv_sem=recv_sem,
                                        device_id=to, device_id_type=MESH)


def _all_gather8(blocks, space):
    n = len(blocks)

    def body(*refs):
        x_refs, out_refs = refs[:n], refs[n:2 * n]
        send_sems, recv_sems, local_sems = refs[2 * n:]
        x, y, c = _me()
        me, sibling = (x, y, c), (x, y, 1 - c)
        chips = [(1 - x, y), (x, 1 - y), (1 - x, 1 - y)]

        def rows(i, px, py, pc):
            m_per = blocks[i].shape[0]
            return out_refs[i].at[pl.ds((4 * px + 2 * py + pc) * m_per, m_per), :]

        def copy(i, k, blk, to, src=None):
            return _remote(rows(i, *blk) if src is None else src, rows(i, *blk), send_sems.at[7 * i + k],
                           recv_sems.at[7 * i + k], to)

        mine = [pltpu.make_async_copy(x_refs[i], rows(i, *me), local_sems.at[i]) for i in range(n)]
        first = []
        for i in range(n):
            first.append(copy(i, 0, me, sibling, src=x_refs[i]))
            first += [copy(i, 1 + j, me, (*chip, c), src=x_refs[i]) for j, chip in enumerate(chips)]
        for cp in mine + first:
            cp.start()
        passed = []
        for j, chip in enumerate(chips):
            for i in range(n):
                copy(i, 1 + j, (*chip, c), me).wait_recv()
                passed.append(copy(i, 4 + j, (*chip, c), sibling))
                passed[-1].start()
        for i in range(n):
            copy(i, 0, sibling, me).wait_recv()
            for j, chip in enumerate(chips):
                copy(i, 4 + j, (*chip, 1 - c), me).wait_recv()
        for cp in first + passed:
            cp.wait_send()
        for cp in mine:
            cp.wait()

    spec = pl.BlockSpec(memory_space=space)
    return _pcall(
        body, name="all_gather8",
        out_shape=[jax.ShapeDtypeStruct((8 * b.shape[0], b.shape[1]), b.dtype) for b in blocks],
        in_specs=[spec] * n, out_specs=[spec] * n,
        scratch_shapes=[pltpu.SemaphoreType.DMA((7 * n,)), pltpu.SemaphoreType.DMA((7 * n,)),
                        pltpu.SemaphoreType.DMA((n,))],
    )(*blocks)


def _exchange(legs):
    n = len(legs)

    def body(*refs):
        copies, local = _exchange_body(legs, refs[:n], refs[n:2 * n], *refs[2 * n:])
        for cp in copies + local:
            cp.start()
        for cp in copies:
            cp.wait_recv()
        for cp in copies:
            cp.wait_send()
        for cp in local:
            cp.wait()

    hbm = pl.BlockSpec(memory_space=pltpu.HBM)
    return _pcall(body, name="exchange", out_shape=[leg.landing() for leg in legs], in_specs=[hbm] * n,
                  out_specs=[hbm] * n, input_output_aliases=_exchange_aliases(legs, 0, 0),
                  scratch_shapes=_exchange_sems(legs))(*[leg.src for leg in legs])


def _row_tile(rows, cap=4096, mult=16):
    best = None
    for t in range(mult, min(rows, cap) + 1, mult):
        if rows % t == 0:
            best = t
    return rows if best is None else best


def _pair_sum(half, own, own_spec, got, got_spec, out_shape, out_spec, grid):
    def body(_, a_ref, b_ref, o_ref):
        o_ref[...] = (a_ref[...] + b_ref[...].astype(F32)).astype(o_ref.dtype)

    return _pcall(
        body, name="pair_sum",
        grid_spec=pltpu.PrefetchScalarGridSpec(num_scalar_prefetch=1, grid=grid, in_specs=[own_spec, got_spec],
                                               out_specs=out_spec),
        out_shape=out_shape, compiler_params=_seq(len(grid)))(half, own, got)


def _chip_sum(ids, part, met, fill, layer=0, stack=1):
    _, _, rows, n = part.shape
    tr = _row_tile(rows, cap=max(16, BLOCK_ELEMS // n))
    first = isinstance(stack, int)

    def body(_, own_ref, a_ref, b_ref, c_ref, *rest):
        acc = own_ref[...].astype(F32) + a_ref[...].astype(F32)
        acc = acc + b_ref[...].astype(F32)
        rest[-1][...] = acc + c_ref[...].astype(F32)

    blk = (None, None, tr, n)
    other = lambda k: pl.BlockSpec(blk, lambda j, ids: ((ids[0] + k) % 4, 0, j, 0))
    in_specs = [pl.BlockSpec(blk, lambda j, ids: (ids[0], 0, j, 0)), other(1), other(2), other(3)]
    return _pcall(
        body, name="chip_sum",
        grid_spec=pltpu.PrefetchScalarGridSpec(
            num_scalar_prefetch=1, grid=(rows // tr,),
            in_specs=in_specs if first else in_specs + [pl.BlockSpec(memory_space=pl.ANY)],
            out_specs=pl.BlockSpec(blk, lambda j, ids: (layer, ids[1] if fill else 0, j, 0))),
        out_shape=jax.ShapeDtypeStruct(((stack,) if first else stack.shape[:1]) + (2 if fill else 1, rows, n), F32),
        input_output_aliases={} if first else {5: 0},
        compiler_params=_seq())(*((ids, part, met, met, met) if first else (ids, part, met, met, met, stack)))


def _ada_mod(c_all, w_ada, b_ada_cols):
    depth, d, n = w_ada.shape
    nb = c_all.shape[0]

    def body(c_ref, w_ref, b_ref, o_ref):
        cv = c_ref[...]
        ca = _bf(cv * _sigmoid(cv))
        o_ref[0] = _dot(ca, _bf(w_ref[0])) + b_ref[0]

    return _pcall(body, name="ada_mod", grid=(depth,),
                  in_specs=[pl.BlockSpec((nb, d), lambda l: (0, 0)), pl.BlockSpec((1, d, n), lambda l: (l, 0, 0)),
                            pl.BlockSpec((1, 1, n), lambda l: (l, 0, 0))],
                  out_specs=pl.BlockSpec((1, nb, n), lambda l: (l, 0, 0)),
                  out_shape=jax.ShapeDtypeStruct((depth, nb, n), F32), compiler_params=_seq())(c_all, w_ada, b_ada_cols)


def _ada_grad(c_all, dmod_cols, rows_all):
    nb, d = c_all.shape
    depth, _, n = dmod_cols.shape
    kinds, n_all = rows_all.shape[1], rows_all.shape[3]

    def body(c_ref, dm_ref, da_ref, gw_ref, gb_ref):
        cv = c_ref[...]
        ca = _bf(cv * _sigmoid(cv))
        gw_ref[0] = _dot_tn(ca, _bf(dm_ref[0]))
        for k in range(kinds):
            gb_ref[0, k] = _colsum(da_ref[0, k])

    return _pcall(body, name="ada_grad", grid=(depth,),
                  in_specs=[pl.BlockSpec((nb, d), lambda l: (0, 0)), pl.BlockSpec((1, nb, n), lambda l: (l, 0, 0)),
                            pl.BlockSpec((1, kinds, nb, n_all), lambda l: (l, 0, 0, 0))],
                  out_specs=[pl.BlockSpec((1, d, n), lambda l: (l, 0, 0)),
                             pl.BlockSpec((1, kinds, 1, n_all), lambda l: (l, 0, 0, 0))],
                  out_shape=[jax.ShapeDtypeStruct((depth, d, n), F32), jax.ShapeDtypeStruct((depth, kinds, 1, n_all), F32)],
                  compiler_params=_seq())(c_all, dmod_cols, rows_all)


def _adamw(items, ride=None):
    two_d = [tuple(t.reshape(w.size // w.shape[-1], w.shape[-1]) for t in (w, g, m, v)) for w, g, m, v in items]
    n = len(items)
    if n == 1:
        rows, cols = two_d[0][0].shape
        tr = _row_tile(rows, cap=max(8, BLOCK_ELEMS // cols), mult=8)
        blocks = [pl.BlockSpec((tr, cols), lambda i: (i, 0))]
        grid = (rows // tr,)
    else:
        blocks = [pl.BlockSpec(t[0].shape, lambda i: (0, 0)) for t in two_d]
        grid = (1,)

    def body(*refs):
        for k in range(n):
            w_ref, g_ref, m_ref, v_ref = refs[4 * k:4 * k + 4]
            d_ref, mo_ref, vo_ref = refs[4 * n + 3 * k:4 * n + 3 * k + 3]
            gv = g_ref[...]
            mn = ADAM_B1 * m_ref[...] + (1.0 - ADAM_B1) * gv
            vn = ADAM_B2 * v_ref[...] + (1.0 - ADAM_B2) * (gv * gv)
            m_hat = mn / (1.0 - ADAM_B1 ** ADAM_STEP)
            v_hat = vn / (1.0 - ADAM_B2 ** ADAM_STEP)
            d_ref[...] = -ADAM_LR * (m_hat / (jnp.sqrt(v_hat) + ADAM_EPS) + ADAM_WD * w_ref[...])
            mo_ref[...] = mn
            vo_ref[...] = vn

    outs, got = _pcall_ride(
        body, ride, name="adamw", grid=grid,
        in_specs=[b for b in blocks for _ in range(4)], out_specs=[b for b in blocks for _ in range(3)],
        out_shape=[jax.ShapeDtypeStruct(t[0].shape, F32) for t in two_d for _ in range(3)],
        compiler_params=_seq(), args=tuple(a for t in two_d for a in t))
    return [tuple(o.reshape(items[k][0].shape) for o in outs[3 * k:3 * k + 3]) for k in range(n)], got


WEIGHTS = ["norm_g", "w_ada", "b_ada", "w_in", "rg_conv_w", "rg_conv_b", "rg_w_a", "rg_b_a", "rg_w_x", "rg_b_x",
           "rg_lambda", "ml_conv_w", "ml_conv_b", "ml_w_q", "ml_w_k", "ml_w_v", "ml_w_if", "ml_b_if", "ml_norm_g",
           "w_out", "final_g"]
SMALL_SHARDED = {"ml_w_qkv": 2, "rg_conv_w": 1, "ml_conv_w": 1, "ml_w_if": 0}
REPLICATED = ["rg_w_a", "rg_w_x", "rg_conv_b", "rg_b_a", "rg_b_x", "rg_lambda", "ml_conv_b", "ml_norm_g", "ml_b_if"]
LANES = 128


def _to_pieces(g, axis):
    shp = g.shape
    g = g.reshape(shp[:axis] + (4, 2, shp[axis] // 8) + shp[axis + 1:])
    g = jnp.moveaxis(g, (axis, axis + 1), (0, 1))
    return g.reshape(4, 2, -1)


def _from_pieces(p, shard_shape, axis):
    k = p.shape[0]
    rest = shard_shape[:axis] + (shard_shape[axis] // k,) + shard_shape[axis + 1:]
    t = jnp.moveaxis(p.reshape((k,) + rest), 0, axis)
    return t.reshape(shard_shape)


def _pad_rows(flat, mult):
    n = flat.shape[-1]
    pad = (-n) % mult
    if pad:
        flat = jnp.concatenate([flat, jnp.zeros(flat.shape[:-1] + (pad,), flat.dtype)], axis=-1)
    return flat


def kernel(x, c, norm_g, w_ada, b_ada, w_in, rg_conv_w, rg_conv_b, rg_w_a, rg_b_a, rg_w_x, rg_b_x, rg_lambda, ml_conv_w, ml_conv_b, ml_w_q, ml_w_k, ml_w_v, ml_w_if, ml_b_if, ml_norm_g, w_out, final_g, loss_target, m_norm_g, m_w_ada, m_b_ada, m_w_in, m_rg_conv_w, m_rg_conv_b, m_rg_w_a, m_rg_b_a, m_rg_w_x, m_rg_b_x, m_rg_lambda, m_ml_conv_w, m_ml_conv_b, m_ml_w_q, m_ml_w_k, m_ml_w_v, m_ml_w_if, m_ml_b_if, m_ml_norm_g, m_w_out, m_final_g, v_norm_g, v_w_ada, v_b_ada, v_w_in, v_rg_conv_w, v_rg_conv_b, v_rg_w_a, v_rg_b_a, v_rg_w_x, v_rg_b_x, v_rg_lambda, v_ml_conv_w, v_ml_conv_b, v_ml_w_q, v_ml_w_k, v_ml_w_v, v_ml_w_if, v_ml_b_if, v_ml_norm_g, v_w_out, v_final_g):
    given = dict(locals())
    ax, ay, ac = lax.axis_index("x"), lax.axis_index("y"), lax.axis_index("c")
    chip = 2 * ax + ay
    me = 2 * chip + ac
    depth, d = norm_g.shape
    n_ada = w_ada.shape[2]
    pick = lambda a, i, axis=0: lax.dynamic_index_in_dim(a, i, axis, keepdims=False)

    convs = jnp.stack([rg_conv_w, ml_conv_w])
    n_conv = 2 * depth * CONV_WIDTH // 4
    blk = jnp.concatenate([c, convs.reshape(n_conv, d), jnp.zeros((8 - 1 - n_conv, d), F32)], axis=0)
    w_in_first = lax.dynamic_slice_in_dim(w_in[0], ac * (d // 2), d // 2, 0).astype(BF16)
    g0, w_in_first = _all_gather8([blk, w_in_first], pltpu.HBM)
    g0 = g0.reshape(8, 8, d)
    c_all = g0[:, 0, :]
    conv_full = g0[0::2, 1:1 + n_conv].reshape(4, 2, depth, CONV_WIDTH, d // 4)
    conv_full = conv_full.transpose(1, 2, 3, 0, 4).reshape(2, depth, CONV_WIDTH, d)

    b_cols = lax.dynamic_slice_in_dim(b_ada, chip * n_ada, n_ada, axis=1)[:, None, :]
    mod_part = _ada_mod(c_all, w_ada, b_cols)
    g1 = _all_gather8([mod_part.transpose(1, 0, 2).reshape(8, depth * n_ada)], pltpu.VMEM)[0]
    g1 = g1.reshape(8, 8, depth, n_ada)[0::2]
    mod_me = pick(g1.transpose(1, 2, 0, 3).reshape(8, depth, 4 * n_ada), me)

    def half_of(w, axis):
        n = w.shape[axis] // 2
        return lax.dynamic_slice_in_dim(w, ac * n, n, axis).astype(BF16)

    n_sh = w_in.shape[2]
    heads, hd_cut, hd = ml_w_q.shape[1:]

    def blocks_of(l):
        wqkv = jnp.stack([ml_w_q[l], ml_w_k[l], ml_w_v[l]])
        return [half_of(w_in[l], 0), half_of(w_out[l], 0), half_of(wqkv, 2).reshape(-1, hd), half_of(ml_w_if[l], 0)]

    def layer_of(l, w4, rest):
        return dict(
            norm_g=norm_g[l][None], shift=mod_me[l, 0:d][None], scale=mod_me[l, d:2 * d][None],
            gate=mod_me[l, 2 * d:3 * d][None], w4=w4.reshape(4, d, n_sh),
            rg_conv_w=conv_full[0, l], rg_conv_b=rg_conv_b[l][None], rg_wa_b=_bf(rg_w_a[l]), rg_ba=rg_b_a[l][None],
            rg_wx_b=_bf(rg_w_x[l]), rg_bx=rg_b_x[l][None], rg_lam=rg_lambda[l][None],
            ml_conv_w=conv_full[1, l], ml_conv_b=ml_conv_b[l][None], b_if=ml_b_if[l][None], b_ift=ml_b_if[l][:, None],
            ml_g=ml_norm_g[l][None], **rest)

    def rest_of(gathered):
        w_out_b, wqkv_g, wif = gathered
        return dict(w_out_b=w_out_b, wqkv_b=_from_pieces(wqkv_g.reshape(8, -1), (3, heads, hd, hd), 2), wif_b=wif,
                    wift_b=wif.T)

    spread = lambda blocks: [Leg(b, "spread") for b in blocks]
    fill = lambda landed: [Leg(t, "sib_fill") for t in landed]
    flat = lambda filled: [t.reshape(-1, t.shape[-1]) for t in filled]
    first = blocks_of(0)
    n_rest = len(first) - 1
    p = layer_of(0, w_in_first, {})
    layers, saved = [], []
    xl = x[0]
    for l in range(depth):
        nxt = blocks_of(l + 1) if l + 1 < depth else []
        skip = n_rest if l == 0 else 0
        rides = dict(rg_fwd=lambda landed, nxt=nxt: spread(nxt[:1]))
        if l == 0:
            rides.update(ln_inproj=lambda landed: spread(first[1:]),
                         rg_fwd=lambda landed, nxt=nxt: fill(landed["ln_inproj"]) + spread(nxt[:1]),
                         late=lambda landed: rest_of(flat(landed["rg_fwd"][:n_rest])))
        if nxt:
            rides.update(mlstm_fwd=lambda landed, nxt=nxt: spread(nxt[1:]),
                         out_proj=lambda landed, skip=skip: fill(list(landed["rg_fwd"][skip:]) + list(landed["mlstm_fwd"])))
        xl, s, p, landed = _layer_fwd(xl, p, rides)
        layers.append(p)
        saved.append(s)
        if nxt:
            arrived = flat(landed["out_proj"])
            p = layer_of(l + 1, arrived[0], rest_of(arrived[1:]))
    dx, g_final, loss = _final_loss(xl, final_g[None], loss_target[0])

    half = ac.reshape(1)
    ids = jnp.stack([chip, ac])
    r_out = w_out.shape[1] // 2

    def pair_in(g_w_in, got_in):
        return _pair_sum(
            half, g_w_in, pl.BlockSpec((None, d // 2, n_sh), lambda s, h: (0, h[0], s)),
            got_in, pl.BlockSpec((None, None, d // 2, n_sh), lambda s, h: (0, s, 0, 0)),
            jax.ShapeDtypeStruct((4, 1, d // 2, n_sh), BF16),
            pl.BlockSpec((None, None, d // 2, n_sh), lambda s, h: (s, 0, 0, 0)), (4,))

    def pair_out(g_out5, got_out):
        return _pair_sum(
            half, g_out5, pl.BlockSpec((None, None, None, r_out, d), lambda s, h: (0, s, h[0], 0, 0)),
            got_out, pl.BlockSpec((None, None, r_out, d), lambda s, h: (0, s, 0, 0)),
            jax.ShapeDtypeStruct((4, 1, r_out, d), BF16),
            pl.BlockSpec((None, None, r_out, d), lambda s, h: (s, 0, 0, 0)), (4,))

    def pair_slab(slab, got, dtype):
        rows = got.shape[0] // 4
        blk = pl.BlockSpec((rows, LANES), lambda s, h: (s, 0))
        return _pair_sum(half, slab, pl.BlockSpec((None, rows, LANES), lambda s, h: (h[0], s, 0)), got, blk,
                         jax.ShapeDtypeStruct((4 * rows, LANES), dtype), blk, (4,)).reshape(4, 1, rows, LANES)

    row_pad = lambda n: -(-n // (8 * LANES)) * (8 * LANES)

    def as_rows(t):
        if t.shape[-1] == LANES and t.size % (8 * LANES) == 0:
            return t.reshape(-1, LANES)
        return _pad_rows(t.reshape(-1), 8 * LANES).reshape(-1, LANES)

    chips = lambda arrs: [Leg(a, "chips") for a in arrs]
    out5 = lambda g: g["w_out"].reshape(1, 4, 2, r_out, d)
    grads, dmods, parts, mets = [None] * depth, [None] * depth, [None] * depth, [None] * depth
    small = {}

    def early_exchange(g, landed):
        every = [g] + grads[1:]
        sm = jnp.concatenate([_to_pieces(every[l][name], axis) for l in range(depth)
                              for name, axis in SMALL_SHARDED.items()], axis=-1)
        sm = _pad_rows(sm, 16 * LANES)
        sm = sm.transpose(1, 0, 2).reshape(2, -1, LANES)
        rep = [as_rows(every[l][name]) for l in range(depth) for name in REPLICATED]
        rep = jnp.concatenate(rep + [as_rows(g_final), as_rows(loss)], axis=0)
        rep = jnp.concatenate([rep, jnp.zeros(((-rep.shape[0]) % 64, LANES), F32)], axis=0)
        rep = rep.reshape(4, 2, -1, LANES).transpose(1, 0, 2, 3).reshape(2, -1, LANES)
        got_sm, got_rep = _exchange([Leg(sm, "sib_slab"), Leg(rep, "sib_slab")])
        small["parts"] = [pair_out(out5(g), landed["mlstm_bwd"][0]), pair_slab(sm, got_sm, BF16),
                          pair_slab(rep, got_rep, F32)]
        return chips(small["parts"])

    def last_exchange(g, landed):
        (got_in,) = _exchange([Leg(g["w_in"], "sib_w_in")])
        small["part_in"] = pair_in(g["w_in"], got_in)
        return chips([small["part_in"]])

    for l in reversed(range(depth)):
        above = parts[l + 1] if l + 1 < depth else []
        rides = dict(mlstm_bwd=lambda g, landed, above=above: [Leg(out5(g), "sib_w_out")] + chips(above),
                     in_bwd=lambda g, landed: [Leg(g["w_in"], "sib_w_in")])
        if l == 0:
            rides.update(grad_w_in=early_exchange, in_bwd=last_exchange)
        dx, grads[l], dmods[l], got = _layer_bwd(dx, layers[l], saved[l], rides)
        if above:
            mets[l + 1] = got["mlstm_bwd"][1:]
        if l > 0:
            parts[l] = [pair_in(grads[l]["w_in"], got["in_bwd"][0]), pair_out(out5(grads[l]), got["mlstm_bwd"][0])]
    part_out, part_sm, part_rep = small["parts"]
    met_out, met_sm, met_rep = got["grad_w_in"]
    parts[0], mets[0] = [small["part_in"], part_out], [got["in_bwd"][0], met_out]
    n_rep = part_rep.shape[2]

    pad = lambda t: jnp.concatenate([t, jnp.zeros((1, 2 * d), F32)], axis=1)
    rows = [r for l in range(depth) for r in (dmods[l], pad(grads[l]["norm_g"]))]
    blk = jnp.concatenate(rows + [jnp.zeros((8 - 2 * depth, 3 * d), F32)], axis=0)
    rows_all = _all_gather8([blk], pltpu.VMEM)[0].reshape(8, 8, 3 * d)[:, :2 * depth]
    rows_all = rows_all.transpose(1, 0, 2).reshape(depth, 2, 8, 3 * d)
    dm_cols = lax.dynamic_slice_in_dim(rows_all[:, 0], chip * n_ada, n_ada, axis=2)
    g_w_ada, summed = _ada_grad(c_all, dm_cols, rows_all)

    g = dict(w_ada=g_w_ada, b_ada=summed[:, 0, 0], norm_g=summed[:, 1, 0, :d])
    item = lambda name: (given[name], g[name], given["m_" + name], given["v_" + name])
    both_in, both_out = depth, depth
    for l in range(depth):
        both_in = _chip_sum(ids, parts[l][0], mets[l][0], True, l, both_in)
        both_out = _chip_sum(ids, parts[l][1], mets[l][1], True, l, both_out)
    both_in, both_out, both_sm = _exchange(fill([both_in, both_out, _chip_sum(ids, part_sm, met_sm, True)]))
    red_rep = _chip_sum(ids, part_rep, met_rep, False).reshape(n_rep, LANES)
    rep_all = _all_gather8([red_rep], pltpu.VMEM)[0].reshape(-1)

    g.update(w_in=both_in.reshape(w_in.shape), w_out=both_out.reshape(w_out.shape))
    shard = both_sm.reshape(2, -1)
    off = 0
    per_layer = {name: [] for name in SMALL_SHARDED}
    for l in range(depth):
        for name, axis in SMALL_SHARDED.items():
            shp = (3,) + ml_w_q.shape[1:] if name == "ml_w_qkv" else given[name].shape[1:]
            n = grads[l][name].size // 8
            per_layer[name].append(_from_pieces(shard[:, off:off + n], shp, axis))
            off += n
    for name in SMALL_SHARDED:
        g[name] = jnp.stack(per_layer[name])
    for i, name in enumerate(["ml_w_q", "ml_w_k", "ml_w_v"]):
        g[name] = g["ml_w_qkv"][:, i]
    off = 0
    per_layer = {name: [] for name in REPLICATED}
    for l in range(depth):
        for name in REPLICATED:
            n = given[name][l].size
            per_layer[name].append(rep_all[off:off + n].reshape(given[name].shape[1:]))
            off += row_pad(n)
    for name in REPLICATED:
        g[name] = jnp.stack(per_layer[name])
    g["final_g"] = rep_all[off:off + d]
    loss_all = rep_all[off + row_pad(d)]

    stepped = {}
    rg_mats, ml_mats = ["rg_w_a", "rg_w_x"], ["ml_w_q", "ml_w_k", "ml_w_v"]
    vectors = [n for n in WEIGHTS if n not in ["w_ada", "w_in", "w_out"] + rg_mats + ml_mats]
    for names in (["w_ada"], ["w_in"], ["w_out"], rg_mats, ml_mats, vectors):
        stepped.update(zip(names, _adamw([item(name) for name in names])[0]))
    deltas, new_m, new_v = zip(*[stepped[name] for name in WEIGHTS])
    return (loss_all, dx[None], *[g[name] for name in WEIGHTS], *deltas, *new_m, *new_v)
```

```python
import functools
from typing import NamedTuple

import jax
import jax.numpy as jnp
from jax import lax
from jax.experimental import pallas as pl
from jax.experimental.pallas import tpu as pltpu

F32 = jnp.float32
BF16 = jnp.bfloat16

EPS = 1e-6
RG_C = 8.0
CONV_WIDTH = 4
ML_CHUNK = 512
HALO = 8
ROWS_VECTOR = 256
ROWS_MATMUL = 1024
ROWS_IN_BWD = 512
ROWS_GRAD_MATMUL = 2048
BLOCK_ELEMS = 1 << 18
ADAM_LR = 0.001
ADAM_B1 = 0.9
ADAM_B2 = 0.999
ADAM_EPS = 1e-08
ADAM_WD = 0.01
ADAM_STEP = 10
MESH = pl.DeviceIdType.MESH


def _pcall(body, **kw):
    return pl.pallas_call(body, **kw)


class Leg(NamedTuple):
    src: jax.Array
    kind: str

    def landing(self):
        a = self.src
        shape = {"chips": lambda: a.shape, "spread": lambda: (4, 2) + a.shape, "sib_fill": lambda: a.shape,
                 "sib_w_in": lambda: (a.shape[0], 4, a.shape[1] // 2, a.shape[2] // 4),
                 "sib_w_out": lambda: a.shape[:2] + a.shape[3:], "sib_slab": lambda: a.shape[1:]}[self.kind]()
        return jax.ShapeDtypeStruct(shape, a.dtype)

    def copies(self, src, dst, x, y, c):
        a, me_s, o = self.src, 2 * x + y, 1 - c
        chips = [(1 - x, y), (x, 1 - y), (1 - x, 1 - y)]
        if self.kind == "chips":
            return [(src.at[2 * px + py], dst.at[me_s], (px, py, c)) for px, py in chips], []
        if self.kind == "spread":
            return [(src, dst.at[me_s, c], (px, py, c)) for px, py in chips], [(src, dst.at[me_s, c])]
        depth = pl.ds(0, a.shape[0])
        if self.kind == "sib_fill":
            return [(dst.at[depth, c], dst.at[depth, c], (x, y, o))], []
        if self.kind == "sib_w_in":
            half, n = a.shape[1] // 2, a.shape[2] // 4
            return [(src.at[depth, pl.ds(o * half, half), pl.ds(s * n, n)], dst.at[depth, s], (x, y, o))
                    for s in range(4)], []
        if self.kind == "sib_w_out":
            return [(src.at[depth, pl.ds(0, 4), o], dst, (x, y, o))], []
        return [(src.at[o], dst, (x, y, o))], []

    def n_copies(self):
        return {"chips": 3, "spread": 3, "sib_w_in": 4}.get(self.kind, 1)


def _exchange_body(legs, srcs, dsts, send_sems, recv_sems, local_sems):
    x, y, c = _me()
    remote, local, k = [], [], 0
    for i, leg in enumerate(legs):
        far, near = leg.copies(srcs[i], dsts[i], x, y, c)
        for src, dst, to in far:
            remote.append(_remote(src, dst, send_sems.at[k], recv_sems.at[k], to))
            k += 1
        local += [pltpu.make_async_copy(src, dst, local_sems.at[i]) for src, dst in near]
    return remote, local


def _exchange_sems(legs):
    n = sum(leg.n_copies() for leg in legs)
    return [pltpu.SemaphoreType.DMA((n,)), pltpu.SemaphoreType.DMA((n,)), pltpu.SemaphoreType.DMA((len(legs),))]


def _exchange_aliases(legs, n_in, n_out):
    return {n_in + i: n_out + i for i, leg in enumerate(legs) if leg.kind == "sib_fill"}


def _pcall_ride(body, ride, *, grid, in_specs, out_specs, out_shape, args, scratch_shapes=(), **kw):
    n_in, n_out, n_scr = len(in_specs), len(out_specs), len(scratch_shapes)
    if not ride:
        res = _pcall(body, grid=grid, in_specs=in_specs, out_specs=out_specs, out_shape=out_shape,
                     scratch_shapes=list(scratch_shapes), **kw)(*args)
        return res, []
    nr = len(ride)

    def riding(*refs):
        ins, rsrc = refs[:n_in], refs[n_in:n_in + nr]
        outs, rdst = refs[n_in + nr:n_in + nr + n_out], refs[n_in + nr + n_out:n_in + 2 * nr + n_out]
        scr = refs[n_in + 2 * nr + n_out:n_in + 2 * nr + n_out + n_scr]
        copies, local = _exchange_body(ride, rsrc, rdst, *refs[n_in + 2 * nr + n_out + n_scr:])
        first = functools.reduce(jnp.logical_and, [pl.program_id(a) == 0 for a in range(len(grid))])
        last = functools.reduce(jnp.logical_and, [pl.program_id(a) == grid[a] - 1 for a in range(len(grid))])

        @pl.when(first)
        def _():
            for cp in copies + local:
                cp.start()

        body(*ins, *outs, *scr)

        @pl.when(last)
        def _():
            for cp in copies:
                cp.wait_recv()
            for cp in copies:
                cp.wait_send()
            for cp in local:
                cp.wait()

    hbm = pl.BlockSpec(memory_space=pltpu.HBM)
    aliases = {**kw.pop("input_output_aliases", {}), **_exchange_aliases(ride, n_in, n_out)}
    res = _pcall(
        riding, grid=grid, in_specs=list(in_specs) + [hbm] * nr, out_specs=list(out_specs) + [hbm] * nr,
        out_shape=list(out_shape) + [leg.landing() for leg in ride], input_output_aliases=aliases,
        scratch_shapes=list(scratch_shapes) + _exchange_sems(ride), **kw)(*args, *[leg.src for leg in ride])
    return res[:n_out], res[n_out:]


def _seq(n=1):
    return pltpu.CompilerParams(dimension_semantics=("arbitrary",) * n)


def _dot(a, b):
    return jnp.dot(a, b, preferred_element_type=F32)


def _dot_nt(a, b):
    return lax.dot_general(a, b, (((1,), (1,)), ((), ())), preferred_element_type=F32)


def _dot_tn(a, b):
    return lax.dot_general(a, b, (((0,), (0,)), ((), ())), preferred_element_type=F32)


def _bf(x):
    return x.astype(BF16)


def _sigmoid(x):
    return 0.5 * jnp.tanh(0.5 * x) + 0.5


def _log1p(z):
    u = 1.0 + z
    return jnp.where(u == 1.0, z, jnp.log(u) * (z / jnp.where(u == 1.0, 1.0, u - 1.0)))


def _softplus(x):
    return jnp.maximum(x, 0.0) + _log1p(jnp.exp(-jnp.abs(x)))


def _log_sigmoid(x):
    return -_softplus(-x)


def _one_minus_sq(a, log_a):
    x = 2.0 * log_a
    small = -x * (1.0 + x * (0.5 + x * (1.0 / 6.0)))
    return jnp.where(x > -0.004, small, 1.0 - a * a)


def _dsilu(x, s):
    return s * (1.0 + x * (1.0 - s))


def _rowsum(x):
    return jnp.sum(x, axis=1, keepdims=True)


def _colsum(x):
    return jnp.sum(x, axis=0, keepdims=True)


def _shift_down(win, s):
    return win if s == 0 else pltpu.roll(win, s, 0)


def _shift_up(win, s):
    return win if s == 0 else pltpu.roll(win, win.shape[0] - s, 0)


def _conv_taps(win):
    return [_shift_down(win, CONV_WIDTH - 1 - k)[HALO:] for k in range(CONV_WIDTH)]


def _conv_fwd(taps, w_ref, b_ref):
    acc = b_ref[...] + w_ref[CONV_WIDTH - 1:CONV_WIDTH, :] * taps[CONV_WIDTH - 1]
    for k in range(CONV_WIDTH - 1):
        acc = acc + w_ref[k:k + 1, :] * taps[k]
    return acc


def _split3(x):
    hi = _bf(x)
    r1 = x - hi.astype(F32)
    mid = _bf(r1)
    lo = _bf(r1 - mid.astype(F32))
    return hi, mid, lo


def _tri_dot_left(tri, x):
    hi, mid, lo = _split3(x)
    return _dot(tri, hi) + _dot(tri, mid) + _dot(tri, lo)


def _tri_dot_right(x, tri):
    hi, mid, lo = _split3(x)
    return _dot(hi, tri) + _dot(mid, tri) + _dot(lo, tri)


def _tile(n, want):
    t = min(n, want)
    assert n % t == 0
    return t


def _ln_inproj(x, g, scale, shift, w4, ride=None):
    s_len, d = x.shape
    nj, _, nsh = w4.shape
    tm = _tile(s_len, ROWS_MATMUL)

    def body(x_ref, g_ref, sc_ref, sh_ref, w_ref, h_ref, u_ref, hs):
        @pl.when(pl.program_id(1) == 0)
        def _():
            xv = x_ref[...]
            r = lax.rsqrt(jnp.mean(xv * xv, axis=-1, keepdims=True) + EPS)
            hv = (xv * r * g_ref[...]) * (1.0 + sc_ref[...]) + sh_ref[...]
            hs[...] = _bf(hv)
            h_ref[...] = hs[...]

        u_ref[...] = _dot(hs[...], w_ref[0])

    vec = pl.BlockSpec((1, d), lambda i, j: (0, 0))
    return _pcall_ride(
        body, ride, name="ln_inproj", grid=(s_len // tm, nj),
        in_specs=[pl.BlockSpec((tm, d), lambda i, j: (i, 0)), vec, vec, vec,
                  pl.BlockSpec((1, d, nsh), lambda i, j: (j, 0, 0))],
        out_specs=[pl.BlockSpec((tm, d), lambda i, j: (i, 0)), pl.BlockSpec((tm, nsh), lambda i, j: (i, j))],
        out_shape=[jax.ShapeDtypeStruct((s_len, d), BF16), jax.ShapeDtypeStruct((s_len, nj * nsh), F32)],
        scratch_shapes=[pltpu.VMEM((tm, d), BF16)],
        compiler_params=_seq(2),
        args=(x, g, scale, shift, w4))


def _rg_gates(xc, wa_ref, ba_ref, wx_ref, bx_ref, lam_ref):
    heads, hd, _ = wa_ref.shape
    xb = _bf(xc)
    ga = jnp.concatenate([_dot(xb[:, h * hd:(h + 1) * hd], wa_ref[h]) for h in range(heads)], axis=1) + ba_ref[...]
    gx = jnp.concatenate([_dot(xb[:, h * hd:(h + 1) * hd], wx_ref[h]) for h in range(heads)], axis=1) + bx_ref[...]
    r = _sigmoid(ga)
    ig = _sigmoid(gx)
    sp = _softplus(-lam_ref[...])
    log_a = (-RG_C) * r * sp
    a = jnp.exp(log_a)
    mult = jnp.sqrt(_one_minus_sq(a, log_a))
    return r, ig, sp, log_a, a, mult


def _scan_groups(a, u, reverse):
    n, c = a.shape
    a = a.reshape(n // 8, 8, c)
    u = u.reshape(n // 8, 8, c)
    row = lax.broadcasted_iota(jnp.int32, a.shape, 1)
    for k in (1, 2, 4):
        sft = 8 - k if reverse else k
        a_sh, u_sh = pltpu.roll(a, sft, 1), pltpu.roll(u, sft, 1)
        ok = row < 8 - k if reverse else row >= k
        u = jnp.where(ok, a * u_sh + u, u)
        a = jnp.where(ok, a * a_sh, a)
    return a.reshape(n, c), u.reshape(n, c)


def _rg_fwd(u, conv_w, conv_b, wa_b, ba, wx_b, bx, lam, ride=None):
    s_len = u.shape[0]
    d = conv_w.shape[1]
    tm = _tile(s_len, ROWS_VECTOR)
    per = tm // HALO

    def body(x_ref, xp_ref, z_ref, cw_ref, cb_ref, wa_ref, ba_ref, wx_ref, bx_ref, lam_ref,
             hh_ref, y_ref, carry):
        i = pl.program_id(0)

        @pl.when(i == 0)
        def _():
            carry[...] = jnp.zeros_like(carry)

        prev = jnp.where(i == 0, 0.0, xp_ref[...])
        xc = _conv_fwd(_conv_taps(jnp.concatenate([prev, x_ref[...]], axis=0)), cw_ref, cb_ref)
        _, ig, _, _, a, mult = _rg_gates(xc, wa_ref, ba_ref, wx_ref, bx_ref, lam_ref)
        ca, cu = _scan_groups(a, mult * (ig * xc), reverse=False)
        c = carry[0:1, :]
        for j in range(tm // 8):
            blk = ca[j * 8:(j + 1) * 8] * c + cu[j * 8:(j + 1) * 8]
            hh_ref[j * 8:(j + 1) * 8, :] = blk
            c = blk[7:8]
        carry[0:1, :] = c
        z = z_ref[...]
        y_ref[0] = _bf(hh_ref[...] * (z * _sigmoid(z)))

    vec = pl.BlockSpec((1, d), lambda i: (0, 0))
    whole3 = lambda a: pl.BlockSpec(a.shape, lambda i: (0, 0, 0))
    return _pcall_ride(
        body, ride, name="rg_fwd", grid=(s_len // tm,),
        in_specs=[pl.BlockSpec((tm, d), lambda i: (i, 0)),
                  pl.BlockSpec((HALO, d), lambda i: (jnp.maximum(i * per - 1, 0), 0)),
                  pl.BlockSpec((tm, d), lambda i: (i, 1)),
                  pl.BlockSpec((CONV_WIDTH, d), lambda i: (0, 0)), vec,
                  whole3(wa_b), vec, whole3(wx_b), vec, vec],
        out_specs=[pl.BlockSpec((tm, d), lambda i: (i, 0)), pl.BlockSpec((1, tm, d), lambda i: (0, i, 0))],
        out_shape=[jax.ShapeDtypeStruct((s_len, d), F32), jax.ShapeDtypeStruct((2, s_len, d), BF16)],
        scratch_shapes=[pltpu.VMEM((8, d), F32)],
        compiler_params=_seq(),
        args=(u, u, u, conv_w, conv_b, wa_b, ba, wx_b, bx, lam))


def _ml_pre(u, conv_w, conv_b, wqkv_b, wif_b, wift_b, b_if, b_ift):
    s_len = u.shape[0]
    d = conv_w.shape[1]
    _, heads, hd, _ = wqkv_b.shape
    ng = 2 * heads
    tm = _tile(s_len, max(ROWS_VECTOR, ML_CHUNK))
    per = tm // HALO

    def body(x_ref, xp_ref, cw_ref, cb_ref, w_ref, wif_ref, wift_ref, bif_ref, bift_ref,
             qkv_ref, gt_ref, gtt_ref, bc_ref, bct_ref):
        i = pl.program_id(0)
        prev = jnp.where(i == 0, 0.0, xp_ref[...])
        xm = x_ref[...]
        pre = _conv_fwd(_conv_taps(jnp.concatenate([prev, xm], axis=0)), cw_ref, cb_ref)
        xcb = _bf(pre * _sigmoid(pre))
        xmb = _bf(xm)
        for h in range(heads):
            hs = slice(h * hd, (h + 1) * hd)
            qkv_ref[0, :, hs] = _bf(_dot(xcb[:, hs], w_ref[0, h]))
            qkv_ref[1, :, hs] = _bf(_dot(xcb[:, hs], w_ref[1, h]))
            qkv_ref[2, :, hs] = _bf(_dot(xmb[:, hs], w_ref[2, h]))
        qb, kb, vb = qkv_ref[0], qkv_ref[1], qkv_ref[2]
        gt = (_dot(qb, wif_ref[0:d, :]) + _dot(kb, wif_ref[d:2 * d, :]) + _dot(vb, wif_ref[2 * d:3 * d, :])
              + bif_ref[...])
        gtt = (_dot_nt(wift_ref[:, 0:d], qb) + _dot_nt(wift_ref[:, d:2 * d], kb)
               + _dot_nt(wift_ref[:, 2 * d:3 * d], vb) + bift_ref[...])
        gt_ref[...] = gt
        gtt_ref[...] = gtt
        r = lax.broadcasted_iota(jnp.int32, (tm, tm), 0)
        c = lax.broadcasted_iota(jnp.int32, (tm, tm), 1)
        same = (r // ML_CHUNK) == (c // ML_CHUNK)
        bc_ref[...] = _tri_dot_left(((r >= c) & same).astype(BF16), _log_sigmoid(gt))
        bct_ref[...] = _tri_dot_right(_log_sigmoid(gtt), ((r <= c) & same).astype(BF16))

    vec = pl.BlockSpec((1, d), lambda i: (0, 0))
    whole2 = lambda a: pl.BlockSpec(a.shape, lambda i: (0, 0))
    col = pl.BlockSpec((tm, ng), lambda i: (i, 0))
    row = pl.BlockSpec((ng, tm), lambda i: (0, i))
    return _pcall(
        body, name="ml_pre", grid=(s_len // tm,),
        in_specs=[pl.BlockSpec((tm, d), lambda i: (i, 2)),
                  pl.BlockSpec((HALO, d), lambda i: (jnp.maximum(i * per - 1, 0), 2)),
                  pl.BlockSpec((CONV_WIDTH, d), lambda i: (0, 0)), vec,
                  pl.BlockSpec(wqkv_b.shape, lambda i: (0, 0, 0, 0)), whole2(wif_b), whole2(wift_b), whole2(b_if),
                  whole2(b_ift)],
        out_specs=[pl.BlockSpec((3, tm, d), lambda i: (0, i, 0)), col, row, col, row],
        out_shape=[jax.ShapeDtypeStruct((3, s_len, d), BF16), jax.ShapeDtypeStruct((s_len, ng), F32),
                   jax.ShapeDtypeStruct((ng, s_len), F32), jax.ShapeDtypeStruct((s_len, ng), F32),
                   jax.ShapeDtypeStruct((ng, s_len), F32)],
        compiler_params=_seq(),
    )(u, u, conv_w, conv_b, wqkv_b, wif_b, wift_b, b_if, b_ift)


def _chunk_gates(gt, gtt, bc, bct, h, heads):
    li_c = gt[:, h:h + 1]
    li_r = gtt[h:h + 1, :]
    gf_c = gt[:, heads + h:heads + h + 1]
    b_c = bc[:, heads + h:heads + h + 1]
    b_r = bct[heads + h:heads + h + 1, :]
    return li_c, li_r, gf_c, b_c, b_r


def _chunk_weights(li_c, li_r, b_c, b_r, m_prev, causal):
    lc = b_c.shape[0]
    b_last = b_c[lc - 1:lc, :]
    dmat = jnp.where(causal, b_c - b_r + li_r, -jnp.inf)
    m_inter = b_c + m_prev
    m_t = jnp.maximum(m_inter, jnp.max(dmat, axis=1, keepdims=True))
    w_intra = jnp.exp(dmat - m_t)
    w_inter = jnp.exp(m_inter - m_t)
    g_c = b_last - b_c + li_c
    m_new = jnp.maximum(b_last + m_prev, jnp.max(g_c, axis=0, keepdims=True))
    w_state = jnp.exp(g_c - m_new)
    decay = jnp.exp(b_last + m_prev - m_new)
    return m_t, w_intra, w_inter, m_new, w_state, decay


def _tri_masks(lc):
    r = lax.broadcasted_iota(jnp.int32, (lc, lc), 0)
    c = lax.broadcasted_iota(jnp.int32, (lc, lc), 1)
    causal = r >= c
    return causal, causal.astype(BF16), (r <= c).astype(BF16)


def _mlstm_fwd(qkv, gates, u, ml_g, ycat, ride=None):
    _, s_len, d = qkv.shape
    ng = gates[0].shape[1]
    heads = ng // 2
    hd = d // heads
    lc = ML_CHUNK
    nc = s_len // lc
    kscale = hd ** -0.5

    def body(qkv_ref, gt_ref, gtt_ref, bc_ref, bct_ref, o_ref, z_ref, g_ref, _, cell_ref, y_ref, cst_ref, nst_ref,
             mst_ref, cs, ns, ms):
        @pl.when(pl.program_id(0) == 0)
        def _():
            cs[...] = jnp.zeros_like(cs)
            ns[...] = jnp.zeros_like(ns)
            ms[...] = jnp.zeros_like(ms)

        causal = _tri_masks(lc)[0]
        gtv, gttv, bcv, bctv = gt_ref[...], gtt_ref[...], bc_ref[...], bct_ref[...]
        old = [(cs[h], ns[h], ms[h]) for h in range(heads)]
        new, cells, ys = [], [], []
        for h in range(heads):
            hs = slice(h * hd, (h + 1) * hd)
            li_c, li_r, _, b_c, b_r = _chunk_gates(gtv, gttv, bcv, bctv, h, heads)
            c_old, n_old, m_old = old[h]
            m_prev = m_old[:, 0:1]
            m_t, w_intra, w_inter, m_new, w_state, decay = _chunk_weights(li_c, li_r, b_c, b_r, m_prev, causal)
            qb = qkv_ref[0, :, hs]
            ks = qkv_ref[1, :, hs].astype(F32) * kscale
            kb = _bf(ks)
            vb = qkv_ref[2, :, hs]
            s = _dot_nt(qb, kb) * w_intra
            num = _dot(_bf(s), vb) + w_inter * _dot(qb, _bf(c_old))
            den = _rowsum(s) + w_inter * _rowsum(qb.astype(F32) * n_old)
            cell = num / jnp.maximum(jnp.abs(den), jnp.exp(-m_t))
            kw = ks * w_state
            new.append((decay * c_old + _dot_tn(_bf(kw), vb), decay * n_old + _colsum(kw),
                        jnp.broadcast_to(m_new, m_old.shape)))
            cells.append(cell)
            hm = _sigmoid(o_ref[:, hs]) * cell
            hn = hm * lax.rsqrt(jnp.mean(hm * hm, axis=-1, keepdims=True) + EPS)
            z = z_ref[:, hs]
            ys.append(_bf((hn * g_ref[:, hs]) * (z * _sigmoid(z))))
        for h in range(heads):
            cst_ref[0, h] = _bf(old[h][0])
            nst_ref[0, h] = old[h][1]
            mst_ref[0, h] = old[h][2]
            cs[h], ns[h], ms[h] = new[h]
        cell_ref[...] = jnp.concatenate(cells, axis=1)
        y_ref[0] = jnp.concatenate(ys, axis=1)

    row = pl.BlockSpec((lc, d), lambda c: (c, 0))
    gcol = pl.BlockSpec((lc, ng), lambda c: (c, 0))
    grow = pl.BlockSpec((ng, lc), lambda c: (0, c))
    return _pcall_ride(
        body, ride, name="mlstm_fwd", grid=(nc,),
        in_specs=[pl.BlockSpec((3, lc, d), lambda c: (0, c, 0)), gcol, grow, gcol, grow,
                  pl.BlockSpec((lc, d), lambda c: (c, 3)), pl.BlockSpec((lc, d), lambda c: (c, 4)),
                  pl.BlockSpec((1, d), lambda c: (0, 0)), pl.BlockSpec(memory_space=pl.ANY)],
        out_specs=[row, pl.BlockSpec((1, lc, d), lambda c: (1, c, 0)),
                   pl.BlockSpec((1, heads, hd, hd), lambda c: (c, 0, 0, 0)),
                   pl.BlockSpec((1, heads, 1, hd), lambda c: (c, 0, 0, 0)),
                   pl.BlockSpec((1, heads, 1, 128), lambda c: (c, 0, 0, 0))],
        out_shape=[jax.ShapeDtypeStruct((s_len, d), F32), jax.ShapeDtypeStruct(ycat.shape, BF16),
                   jax.ShapeDtypeStruct((nc, heads, hd, hd), BF16),
                   jax.ShapeDtypeStruct((nc, heads, 1, hd), F32),
                   jax.ShapeDtypeStruct((nc, heads, 1, 128), F32)],
        scratch_shapes=[pltpu.VMEM((heads, hd, hd), F32), pltpu.VMEM((heads, 1, hd), F32),
                        pltpu.VMEM((heads, 1, 128), F32)],
        input_output_aliases={8: 1},
        compiler_params=_seq(),
        args=(qkv, *gates, u, u, ml_g, ycat))


def _out_proj(ycat, w_out_b, x, gate, ride=None):
    s_len, d = x.shape
    tm = _tile(s_len, ROWS_MATMUL)

    def body(a_ref, w_ref, x_ref, g_ref, y_ref, xn_ref):
        y = _dot(a_ref[0], w_ref[0:d, :]) + _dot(a_ref[1], w_ref[d:2 * d, :])
        y_ref[...] = y
        xn_ref[...] = x_ref[...] + g_ref[...] * y

    row = pl.BlockSpec((tm, d), lambda i: (i, 0))
    return _pcall_ride(
        body, ride, name="out_proj", grid=(s_len // tm,),
        in_specs=[pl.BlockSpec((2, tm, d), lambda i: (0, i, 0)), pl.BlockSpec((2 * d, d), lambda i: (0, 0)), row,
                  pl.BlockSpec((1, d), lambda i: (0, 0))],
        out_specs=[row, row],
        out_shape=[jax.ShapeDtypeStruct((s_len, d), F32)] * 2,
        compiler_params=_seq(),
        args=(ycat, w_out_b, x, gate))


def _out_proj_loss(ycat, w_out_b, x, gate, g, target):
    s_len, d = x.shape
    tm = _tile(s_len, ROWS_IN_BWD)

    def body(a_ref, w_ref, x_ref, gate_ref, g_ref, t_ref, y_ref, dx_ref, dg_ref, loss_ref):
        @pl.when(pl.program_id(0) == 0)
        def _():
            dg_ref[...] = jnp.zeros_like(dg_ref)
            loss_ref[...] = jnp.zeros_like(loss_ref)

        y = _dot(a_ref[0], w_ref[0:d, :]) + _dot(a_ref[1], w_ref[d:2 * d, :])
        y_ref[...] = y
        xv = x_ref[...] + gate_ref[...] * y
        r = lax.rsqrt(jnp.mean(xv * xv, axis=-1, keepdims=True) + EPS)
        xn = xv * r
        err = xn * g_ref[...] - t_ref[...]
        loss_ref[...] += 0.5 * jnp.sum(jnp.mean(err * err, axis=-1, keepdims=True))
        dout = err * (1.0 / d)
        dg_ref[...] += _colsum(dout * xn)
        dxn = dout * g_ref[...]
        dx_ref[...] = r * (dxn - xn * jnp.mean(dxn * xn, axis=-1, keepdims=True))

    row = pl.BlockSpec((tm, d), lambda i: (i, 0))
    vec = pl.BlockSpec((1, d), lambda i: (0, 0))
    return _pcall(
        body, name="out_proj_loss", grid=(s_len // tm,),
        in_specs=[pl.BlockSpec((2, tm, d), lambda i: (0, i, 0)), pl.BlockSpec((2 * d, d), lambda i: (0, 0)), row, vec,
                  vec, row],
        out_specs=[row, row, vec, pl.BlockSpec((1, 128), lambda i: (0, 0))],
        out_shape=[jax.ShapeDtypeStruct((s_len, d), F32), jax.ShapeDtypeStruct((s_len, d), F32),
                   jax.ShapeDtypeStruct((1, d), F32), jax.ShapeDtypeStruct((1, 128), F32)],
        compiler_params=_seq(),
    )(ycat, w_out_b, x, gate, g, target)


def _out_bwd(dxn, y, gate, w_out_b):
    s_len, d = dxn.shape
    tm = _tile(s_len, ROWS_MATMUL)

    def body(dx_ref, y_ref, g_ref, w_ref, dg_ref, dy_ref, dc_ref):
        @pl.when(pl.program_id(0) == 0)
        def _():
            dg_ref[...] = jnp.zeros_like(dg_ref)

        dx = dx_ref[...]
        dg_ref[...] += _colsum(dx * y_ref[...])
        dy = _bf(g_ref[...] * dx)
        dy_ref[...] = dy
        dc_ref[0] = _dot_nt(dy, w_ref[0:d, :])
        dc_ref[1] = _dot_nt(dy, w_ref[d:2 * d, :])

    row = pl.BlockSpec((tm, d), lambda i: (i, 0))
    vec = pl.BlockSpec((1, d), lambda i: (0, 0))
    return _pcall(
        body, name="out_bwd", grid=(s_len // tm,),
        in_specs=[row, row, vec, pl.BlockSpec((2 * d, d), lambda i: (0, 0))],
        out_specs=[vec, row, pl.BlockSpec((2, tm, d), lambda i: (0, i, 0))],
        out_shape=[jax.ShapeDtypeStruct((1, d), F32), jax.ShapeDtypeStruct((s_len, d), BF16),
                   jax.ShapeDtypeStruct((2, s_len, d), F32)],
        compiler_params=_seq(),
    )(dxn, y, gate, w_out_b)


def _grad_matmul(a3, b3, nblk, a_idx, b_idx, out_shape, out_block, out_idx, ride=None):
    _, s_len, m = a3.shape
    n = b3.shape[2]
    tk = _tile(s_len, ROWS_GRAD_MATMUL)

    def body(a_ref, b_ref, o_ref):
        @pl.when(pl.program_id(1) == 0)
        def _():
            o_ref[...] = jnp.zeros_like(o_ref)

        o_ref[...] += _dot_tn(a_ref[0], b_ref[0])

    (out,), got = _pcall_ride(
        body, ride, name="grad_matmul", grid=(nblk, s_len // tk),
        in_specs=[pl.BlockSpec((1, tk, m), lambda p, t: (a_idx(p), t, 0)),
                  pl.BlockSpec((1, tk, n), lambda p, t: (b_idx(p), t, 0))],
        out_specs=[pl.BlockSpec((None,) + out_block, lambda p, t: (0,) + out_idx(p))],
        out_shape=[jax.ShapeDtypeStruct((1,) + out_shape, F32)],
        compiler_params=_seq(2), args=(a3, b3))
    return out, got


DU_PLANE = (2, 3, 4, 0, 1)


def _mlstm_bwd(qkv, gates, cst, nst, mst, cell, u, ml_g, d_ycat, wif_b, ride=None):
    _, s_len, d = qkv.shape
    ng = gates[0].shape[1]
    heads = ng // 2
    hd = d // heads
    lc = ML_CHUNK
    nc = s_len // lc
    kscale = hd ** -0.5

    def body(qkv_ref, gt_ref, gtt_ref, bc_ref, bct_ref, cst_ref, nst_ref, mst_ref, cell_ref, o_ref, z_ref, g_ref, dy_ref,
             wif_ref, dqkv_ref, dgt_ref, dbif_ref, du_ref, dg_ref, dcs, dns):
        @pl.when(pl.program_id(0) == 0)
        def _():
            dbif_ref[...] = jnp.zeros_like(dbif_ref)
            dcs[...] = jnp.zeros_like(dcs)
            dns[...] = jnp.zeros_like(dns)
            dg_ref[...] = jnp.zeros_like(dg_ref)

        causal, tril, triu = _tri_masks(lc)
        tril_strict = (tril.astype(F32) - (tril * triu).astype(F32)).astype(BF16)
        gtv, gttv, bcv, bctv = gt_ref[...], gtt_ref[...], bc_ref[...], bct_ref[...]
        lane = lax.broadcasted_iota(jnp.int32, (lc, ng), 1)
        dli_all = jnp.zeros((lc, ng), F32)
        from_later = jnp.zeros((lc, ng), F32)
        from_earlier = jnp.zeros((lc, ng), F32)
        across_all = jnp.zeros((1, ng), F32)
        old = [(dcs[h], dns[h]) for h in range(heads)]
        new, d_o, d_z, d_g, dqs, dks, dvs = [], [], [], [], [], [], []
        for h in range(heads):
            hs = slice(h * hd, (h + 1) * hd)
            li_c, li_r, gf_c, b_c, b_r = _chunk_gates(gtv, gttv, bcv, bctv, h, heads)
            m_prev = mst_ref[0, h][:, 0:1]
            m_t, w_intra, w_inter, _, w_state, decay = _chunk_weights(li_c, li_r, b_c, b_r, m_prev, causal)
            qb = qkv_ref[0, :, hs]
            qf = qb.astype(F32)
            ks = qkv_ref[1, :, hs].astype(F32) * kscale
            kb = _bf(ks)
            vb = qkv_ref[2, :, hs]
            c_b = cst_ref[0, h]
            n_old = nst_ref[0, h]
            s = _dot_nt(qb, kb) * w_intra
            den = _rowsum(s) + w_inter * _rowsum(qf * n_old)
            floor = jnp.exp(-m_t)
            dstab = jnp.maximum(jnp.abs(den), floor)
            cell = cell_ref[:, hs]
            o = o_ref[:, hs]
            so = _sigmoid(o)
            hm = so * cell
            rinv = lax.rsqrt(jnp.mean(hm * hm, axis=-1, keepdims=True) + EPS)
            hn = hm * rinv
            z = z_ref[:, hs]
            sgz = _sigmoid(z)
            sz = z * sgz
            gh = g_ref[:, hs]
            dy = dy_ref[0, :, hs]
            d_z.append(_bf(dy * (hn * gh) * _dsilu(z, sgz)))
            d_g.append(_colsum(dy * hn * sz))
            dhn = dy * gh * sz
            dhm = rinv * (dhn - hn * jnp.mean(dhn * hn, axis=-1, keepdims=True))
            d_o.append(_bf(dhm * cell * so * (1.0 - so)))
            dcell = dhm * so
            dnum = dcell / dstab
            dnb = _bf(dnum)
            dden = -_rowsum(dcell * cell) / dstab * jnp.where(jnp.abs(den) > floor, jnp.where(den > 0.0, 1.0, -1.0), 0.0)
            dst = _dot_nt(dnb, vb) + dden
            dsdb = _bf(dst * w_intra)
            dc_out, dn_out = old[h]
            dcb = _bf(dc_out)
            dq_inter = w_inter * (_dot_nt(dnb, c_b) + dden * n_old)
            dk_inter = w_state * (_dot_nt(vb, dcb) + dn_out)
            dq = _dot(dsdb, kb) + dq_inter
            dk = _dot_tn(dsdb, qb) + dk_inter
            dv = _dot_tn(_bf(s), dnb) + _dot(_bf(ks * w_state), dcb)
            wq = w_inter * qf
            new.append((decay * dc_out + _dot_tn(_bf(wq), dnb), decay * dn_out + _colsum(wq * dden)))
            pmat = dst * s
            p_rows = _rowsum(pmat)
            p_cols = _rowsum(pmat.T)
            q_in = _rowsum(qf * dq_inter)
            k_in = _rowsum(ks * dk_inter)
            across = decay * (jnp.sum(dc_out * c_b.astype(F32), keepdims=True) + jnp.sum(dn_out * n_old, keepdims=True))
            dli_all = dli_all + jnp.where(lane == h, p_cols + k_in, 0.0)
            from_later = from_later + jnp.where(lane == heads + h, p_rows - p_cols + q_in, 0.0)
            from_earlier = from_earlier + jnp.where(lane == heads + h, k_in, 0.0)
            across_all = across_all + jnp.where(lane[0:1] == heads + h, across, 0.0)
            dqs.append(dq)
            dks.append(dk * kscale)
            dvs.append(dv)
        for h in range(heads):
            dcs[h], dns[h] = new[h]
        du_ref[0] = jnp.concatenate(d_o, axis=1)
        du_ref[1] = jnp.concatenate(d_z, axis=1)
        dg_ref[...] += jnp.concatenate(d_g, axis=1)
        dlf = _tri_dot_left(triu, from_later) + _tri_dot_left(tril_strict, from_earlier) + across_all
        dgt = dli_all + dlf * _sigmoid(-gtv)
        dgt_ref[...] = dgt
        dbif_ref[...] += _colsum(dgt)
        dgb = _bf(dgt)
        dqkv_ref[0] = _bf(jnp.concatenate(dqs, axis=1) + _dot_nt(dgb, wif_ref[0:d, :]))
        dqkv_ref[1] = _bf(jnp.concatenate(dks, axis=1) + _dot_nt(dgb, wif_ref[d:2 * d, :]))
        dqkv_ref[2] = _bf(jnp.concatenate(dvs, axis=1) + _dot_nt(dgb, wif_ref[2 * d:3 * d, :]))

    rev = lambda c: nc - 1 - c
    row = pl.BlockSpec((lc, d), lambda c: (rev(c), 0))
    gcol = pl.BlockSpec((lc, ng), lambda c: (rev(c), 0))
    grow = pl.BlockSpec((ng, lc), lambda c: (0, rev(c)))
    return _pcall_ride(
        body, ride, name="mlstm_bwd", grid=(nc,),
        in_specs=[pl.BlockSpec((3, lc, d), lambda c: (0, rev(c), 0)), gcol, grow, gcol, grow,
                  pl.BlockSpec((1, heads, hd, hd), lambda c: (rev(c), 0, 0, 0)),
                  pl.BlockSpec((1, heads, 1, hd), lambda c: (rev(c), 0, 0, 0)),
                  pl.BlockSpec((1, heads, 1, 128), lambda c: (rev(c), 0, 0, 0)),
                  row, pl.BlockSpec((lc, d), lambda c: (rev(c), 3)), pl.BlockSpec((lc, d), lambda c: (rev(c), 4)),
                  pl.BlockSpec((1, d), lambda c: (0, 0)), pl.BlockSpec((1, lc, d), lambda c: (1, rev(c), 0)),
                  pl.BlockSpec((3 * d, ng), lambda c: (0, 0))],
        out_specs=[pl.BlockSpec((3, lc, d), lambda c: (0, rev(c), 0)), pl.BlockSpec((lc, ng), lambda c: (rev(c), 0)),
                   pl.BlockSpec((1, ng), lambda c: (0, 0)), pl.BlockSpec((2, lc, d), lambda c: (0, rev(c), 0)),
                   pl.BlockSpec((1, d), lambda c: (0, 0))],
        out_shape=[jax.ShapeDtypeStruct((3, s_len, d), BF16), jax.ShapeDtypeStruct((s_len, ng), F32),
                   jax.ShapeDtypeStruct((1, ng), F32), jax.ShapeDtypeStruct((5, s_len, d), BF16),
                   jax.ShapeDtypeStruct((1, d), F32)],
        scratch_shapes=[pltpu.VMEM((heads, hd, hd), F32), pltpu.VMEM((heads, 1, hd), F32)],
        compiler_params=_seq(),
        args=(qkv, *gates, cst, nst, mst, cell, u, u, ml_g, d_ycat, wif_b))


def _conv_bwd_tile(dp, later, taps, cw_ref, gw_ref, gb_ref):
    tm = dp.shape[0]
    dwin = jnp.concatenate([dp, later[...]], axis=0)
    later[...] = dp[0:HALO]
    acc = cw_ref[CONV_WIDTH - 1:CONV_WIDTH, :] * dp
    for k in range(CONV_WIDTH):
        if k < CONV_WIDTH - 1:
            acc = acc + cw_ref[k:k + 1, :] * _shift_up(dwin, CONV_WIDTH - 1 - k)[0:tm]
        gw_ref[k:k + 1, :] += _colsum(dp * taps[k])
    gb_ref[...] += _colsum(dp)
    return acc


def _ml_pre_bwd(dqkv, u, conv_w, conv_b, wqkv_b, du):
    s_len = u.shape[0]
    d = conv_w.shape[1]
    _, heads, hd, _ = wqkv_b.shape
    tm = _tile(s_len, ROWS_VECTOR)
    per = tm // HALO
    nt = s_len // tm

    def body(dqkv_ref, x_ref, xp_ref, cw_ref, cb_ref, w_ref, _, dx_ref, gw_ref, gcw_ref, gcb_ref, later, dps, dxs):
        i = pl.program_id(0)

        @pl.when(i == 0)
        def _():
            gw_ref[...] = jnp.zeros_like(gw_ref)
            gcw_ref[...] = jnp.zeros_like(gcw_ref)
            gcb_ref[...] = jnp.zeros_like(gcb_ref)
            later[...] = jnp.zeros_like(later)

        prev = jnp.where(i == nt - 1, 0.0, xp_ref[...])
        xm = x_ref[...]
        taps = _conv_taps(jnp.concatenate([prev, xm], axis=0))
        pre = _conv_fwd(taps, cw_ref, cb_ref)
        sg = _sigmoid(pre)
        xcb = _bf(pre * sg)
        xmb = _bf(xm)
        for h in range(heads):
            hs = slice(h * hd, (h + 1) * hd)
            dqh, dkh, dvh = dqkv_ref[0, :, hs], dqkv_ref[1, :, hs], dqkv_ref[2, :, hs]
            dxc = _dot_nt(dqh, w_ref[0, h]) + _dot_nt(dkh, w_ref[1, h])
            dps[:, hs] = dxc * _dsilu(pre[:, hs], sg[:, hs])
            dxs[:, hs] = _dot_nt(dvh, w_ref[2, h])
            gw_ref[0, h] += _dot_tn(xcb[:, hs], dqh)
            gw_ref[1, h] += _dot_tn(xcb[:, hs], dkh)
            gw_ref[2, h] += _dot_tn(xmb[:, hs], dvh)
        dx_ref[0] = _bf(_conv_bwd_tile(dps[...], later, taps, cw_ref, gcw_ref, gcb_ref) + dxs[...])

    rev = lambda i: nt - 1 - i
    vec = pl.BlockSpec((1, d), lambda i: (0, 0))
    cwb = pl.BlockSpec((CONV_WIDTH, d), lambda i: (0, 0))
    whole4 = pl.BlockSpec(wqkv_b.shape, lambda i: (0, 0, 0, 0))
    return _pcall(
        body, name="ml_pre_bwd", grid=(nt,),
        in_specs=[pl.BlockSpec((3, tm, d), lambda i: (0, rev(i), 0)), pl.BlockSpec((tm, d), lambda i: (rev(i), 2)),
                  pl.BlockSpec((HALO, d), lambda i: (jnp.maximum(rev(i) * per - 1, 0), 2)),
                  cwb, vec, whole4, pl.BlockSpec(memory_space=pl.ANY)],
        out_specs=[pl.BlockSpec((1, tm, d), lambda i: (DU_PLANE[2], rev(i), 0)), whole4, cwb, vec],
        out_shape=[jax.ShapeDtypeStruct(du.shape, BF16), jax.ShapeDtypeStruct(wqkv_b.shape, F32),
                   jax.ShapeDtypeStruct((CONV_WIDTH, d), F32), jax.ShapeDtypeStruct((1, d), F32)],
        scratch_shapes=[pltpu.VMEM((HALO, d), F32), pltpu.VMEM((tm, d), F32), pltpu.VMEM((tm, d), F32)],
        input_output_aliases={6: 0},
        compiler_params=_seq(),
    )(dqkv, u, u, conv_w, conv_b, wqkv_b, du)


def _rg_bwd(d_ycat, u, hh, conv_w, conv_b, wa_b, ba, wx_b, bx, lam, du):
    s_len = u.shape[0]
    d = conv_w.shape[1]
    heads, hd, _ = wa_b.shape
    tm = _tile(s_len, ROWS_VECTOR)
    per = tm // HALO
    nt = s_len // tm

    def body(dy_ref, x_ref, xp_ref, z_ref, hh_ref, hp_ref, cw_ref, cb_ref, wa_ref, ba_ref, wx_ref, bx_ref, lam_ref, _,
             du_ref, gwa_ref, gwx_ref, gba_ref, gbx_ref, glam_ref, gcw_ref, gcb_ref, carry, gbuf, later, dxcs):
        i = pl.program_id(0)
        first = i == nt - 1

        @pl.when(i == 0)
        def _():
            carry[...] = jnp.zeros_like(carry)
            later[...] = jnp.zeros_like(later)
            gwa_ref[...] = jnp.zeros_like(gwa_ref)
            gwx_ref[...] = jnp.zeros_like(gwx_ref)
            gba_ref[...] = jnp.zeros_like(gba_ref)
            gbx_ref[...] = jnp.zeros_like(gbx_ref)
            glam_ref[...] = jnp.zeros_like(glam_ref)
            gcw_ref[...] = jnp.zeros_like(gcw_ref)
            gcb_ref[...] = jnp.zeros_like(gcb_ref)

        prev = jnp.where(first, 0.0, xp_ref[...])
        taps = _conv_taps(jnp.concatenate([prev, x_ref[...]], axis=0))
        xc = _conv_fwd(taps, cw_ref, cb_ref)
        r, ig, sp, log_a, a, mult = _rg_gates(xc, wa_ref, ba_ref, wx_ref, bx_ref, lam_ref)
        z = z_ref[...]
        sgz = _sigmoid(z)
        dy = dy_ref[0]
        hh_v = hh_ref[...]
        du_ref[1] = _bf(dy * hh_v * _dsilu(z, sgz))
        dhh = dy * (z * sgz)
        rows = lax.broadcasted_iota(jnp.int32, a.shape, 0)
        coef = jnp.where(rows == tm - 1, carry[1:2, :], _shift_up(a, 1))
        ca, cu = _scan_groups(coef, dhh, reverse=True)
        c = carry[0:1, :]
        for j in range(tm // 8 - 1, -1, -1):
            blk = ca[j * 8:(j + 1) * 8] * c + cu[j * 8:(j + 1) * 8]
            gbuf[j * 8:(j + 1) * 8, :] = blk
            c = blk[0:1]
        carry[0:1, :] = c
        carry[1:2, :] = a[0:1]
        g = gbuf[...]
        hprev_tile = jnp.where(first, 0.0, hp_ref[...])
        hprev = _shift_down(jnp.concatenate([hprev_tile, hh_v], axis=0), 1)[HALO:]
        da = g * hprev
        gx_ = g * xc
        d_mult = gx_ * ig
        d_ig = gx_ * mult
        dxc = g * mult * ig
        dlog_a = da * a - d_mult * (a * a / mult)
        d_r = dlog_a * ((-RG_C) * sp)
        glam_ref[...] += _colsum(dlog_a * ((-RG_C) * r)) * (-_sigmoid(-lam_ref[...]))
        d_ga = d_r * r * (1.0 - r)
        d_gx = d_ig * ig * (1.0 - ig)
        gba_ref[...] += _colsum(d_ga)
        gbx_ref[...] += _colsum(d_gx)
        xb = _bf(xc)
        dgab = _bf(d_ga)
        dgxb = _bf(d_gx)
        for h in range(heads):
            hs = slice(h * hd, (h + 1) * hd)
            dxcs[:, hs] = dxc[:, hs] + _dot_nt(dgab[:, hs], wa_ref[h]) + _dot_nt(dgxb[:, hs], wx_ref[h])
            gwa_ref[h] += _dot_tn(xb[:, hs], dgab[:, hs])
            gwx_ref[h] += _dot_tn(xb[:, hs], dgxb[:, hs])
        du_ref[0] = _bf(_conv_bwd_tile(dxcs[...], later, taps, cw_ref, gcw_ref, gcb_ref))

    assert DU_PLANE[0] % 2 == 0 and DU_PLANE[1] == DU_PLANE[0] + 1
    rev = lambda i: nt - 1 - i
    row = pl.BlockSpec((tm, d), lambda i: (rev(i), 0))
    halo_prev = lambda col: pl.BlockSpec((HALO, d), lambda i: (jnp.maximum(rev(i) * per - 1, 0), col))
    vec = pl.BlockSpec((1, d), lambda i: (0, 0))
    cwb = pl.BlockSpec((CONV_WIDTH, d), lambda i: (0, 0))
    whole3 = lambda a: pl.BlockSpec(a.shape, lambda i: (0, 0, 0))
    return _pcall(
        body, name="rg_bwd", grid=(nt,),
        in_specs=[pl.BlockSpec((1, tm, d), lambda i: (0, rev(i), 0)), row, halo_prev(0),
                  pl.BlockSpec((tm, d), lambda i: (rev(i), 1)), row, halo_prev(0),
                  cwb, vec, whole3(wa_b), vec, whole3(wx_b), vec, vec, pl.BlockSpec(memory_space=pl.ANY)],
        out_specs=[pl.BlockSpec((2, tm, d), lambda i: (DU_PLANE[0] // 2, rev(i), 0)), whole3(wa_b), whole3(wa_b),
                   vec, vec, vec, cwb, vec],
        out_shape=[jax.ShapeDtypeStruct(du.shape, BF16), jax.ShapeDtypeStruct(wa_b.shape, F32),
                   jax.ShapeDtypeStruct(wa_b.shape, F32)] + [jax.ShapeDtypeStruct((1, d), F32)] * 3
        + [jax.ShapeDtypeStruct((CONV_WIDTH, d), F32), jax.ShapeDtypeStruct((1, d), F32)],
        scratch_shapes=[pltpu.VMEM((8, d), F32), pltpu.VMEM((tm, d), F32), pltpu.VMEM((HALO, d), F32),
                        pltpu.VMEM((tm, d), F32)],
        input_output_aliases={13: 0},
        compiler_params=_seq(),
    )(d_ycat, u, u, u, hh, hh, conv_w, conv_b, wa_b, ba, wx_b, bx, lam, du)


def _in_bwd(du, w4, x, dxn, g, scale, ride=None):
    s_len, d = x.shape
    tm = _tile(s_len, ROWS_IN_BWD)
    nsh_chips, _, nsh = w4.shape
    npc = du.shape[0]
    ck = d // 4
    assert nsh % ck == 0 and npc * d == nsh_chips * nsh

    def body(du_ref, w_ref, x_ref, dxn_ref, g_ref, sc_ref, dx_ref, dsh_ref, dsc_ref, dg_ref):
        @pl.when(pl.program_id(0) == 0)
        def _():
            dsh_ref[...] = jnp.zeros_like(dsh_ref)
            dsc_ref[...] = jnp.zeros_like(dsc_ref)
            dg_ref[...] = jnp.zeros_like(dg_ref)

        dh = None
        for q in range(npc * d // ck):
            col = q * ck
            p, pc = col // d, col % d
            s, sc = col // nsh, col % nsh
            t = _dot_nt(du_ref[DU_PLANE[p], :, pc:pc + ck], w_ref[s, :, sc:sc + ck])
            dh = t if dh is None else dh + t
        xv = x_ref[...]
        r = lax.rsqrt(jnp.mean(xv * xv, axis=-1, keepdims=True) + EPS)
        xn = xv * r
        gv = g_ref[...]
        onesc = 1.0 + sc_ref[...]
        dsh_ref[...] += _colsum(dh)
        dsc_ref[...] += _colsum(dh * (xn * gv))
        dg_ref[...] += _colsum(dh * xn * onesc)
        dxh = dh * (gv * onesc)
        dx_ref[...] = dxn_ref[...] + r * (dxh - xn * jnp.mean(dxh * xn, axis=-1, keepdims=True))

    row = pl.BlockSpec((tm, d), lambda i: (i, 0))
    vec = pl.BlockSpec((1, d), lambda i: (0, 0))
    return _pcall_ride(
        body, ride, name="in_bwd", grid=(s_len // tm,),
        in_specs=[pl.BlockSpec((npc, tm, d), lambda i: (0, i, 0)), pl.BlockSpec(w4.shape, lambda i: (0, 0, 0)), row, row,
                  vec, vec],
        out_specs=[row, vec, vec, vec],
        out_shape=[jax.ShapeDtypeStruct((s_len, d), F32)] + [jax.ShapeDtypeStruct((1, d), F32)] * 3,
        compiler_params=_seq(),
        args=(du, w4, x, dxn, g, scale))


def _layer_fwd(x, p, rides=None, loss_head=None):
    rides = rides or {}
    landed = {}
    ride = lambda kernel: rides[kernel](landed) if kernel in rides else None
    (h_b, u), landed["ln_inproj"] = _ln_inproj(x, p["norm_g"], p["scale"], p["shift"], p["w4"], ride("ln_inproj"))
    (hh, ycat), landed["rg_fwd"] = _rg_fwd(u, p["rg_conv_w"], p["rg_conv_b"], p["rg_wa_b"], p["rg_ba"], p["rg_wx_b"],
                                           p["rg_bx"], p["rg_lam"], ride("rg_fwd"))
    if "late" in rides:
        p = {**p, **rides["late"](landed)}
    qkv, *gates = _ml_pre(u, p["ml_conv_w"], p["ml_conv_b"], p["wqkv_b"], p["wif_b"], p["wift_b"], p["b_if"],
                          p["b_ift"])
    (cell, ycat, cst, nst, mst), landed["mlstm_fwd"] = _mlstm_fwd(qkv, gates, u, p["ml_g"], ycat, ride("mlstm_fwd"))
    if loss_head is None:
        (y, x_new), landed["out_proj"] = _out_proj(ycat, p["w_out_b"], x, p["gate"], ride("out_proj"))
    else:
        y, *x_new = _out_proj_loss(ycat, p["w_out_b"], x, p["gate"], *loss_head)
    saved = dict(x=x, h_b=h_b, u=u, hh=hh, qkv=qkv, gates=gates, cell=cell, ycat=ycat, cst=cst, nst=nst, mst=mst, y=y)
    return x_new, saved, p, landed


def _layer_bwd(dxn, p, s, rides=None):
    rides = rides or {}
    landed = {}
    ride = lambda kernel: rides[kernel](grads, landed) if kernel in rides else None
    u = s["u"]
    d = dxn.shape[1]
    d_gate, dy_b, d_ycat = _out_bwd(dxn, s["y"], p["gate"], p["w_out_b"])
    grads = dict(w_out=_grad_matmul(s["ycat"], dy_b[None], 2, lambda b: b, lambda b: 0, (2 * d, d), (d, d),
                                    lambda b: (b, 0))[0])
    (dqkv, dgt, g_b_if, du, g_ml_g), landed["mlstm_bwd"] = _mlstm_bwd(
        s["qkv"], s["gates"], s["cst"], s["nst"], s["mst"], s["cell"], u, p["ml_g"], d_ycat, p["wif_b"],
        ride("mlstm_bwd"))
    ng = dgt.shape[1]
    g_w_if = _grad_matmul(s["qkv"], _bf(dgt)[None], 3, lambda b: b, lambda b: 0, (3 * d, ng), (d, ng),
                          lambda b: (b, 0))[0][0]
    du, g_wqkv, g_ml_cw, g_ml_cb = _ml_pre_bwd(dqkv, u, p["ml_conv_w"], p["ml_conv_b"], p["wqkv_b"], du)
    du, g_wa, g_wx, g_ba, g_bx, g_lam, g_rg_cw, g_rg_cb = _rg_bwd(d_ycat, u, s["hh"], p["rg_conv_w"], p["rg_conv_b"],
                                                                  p["rg_wa_b"], p["rg_ba"], p["rg_wx_b"], p["rg_bx"],
                                                                  p["rg_lam"], du)
    grads.update(rg_conv_w=g_rg_cw, rg_conv_b=g_rg_cb, rg_w_a=g_wa, rg_b_a=g_ba, rg_w_x=g_wx, rg_b_x=g_bx,
                 rg_lambda=g_lam, ml_conv_w=g_ml_cw, ml_conv_b=g_ml_cb, ml_w_qkv=g_wqkv, ml_w_if=g_w_if, ml_b_if=g_b_if,
                 ml_norm_g=g_ml_g)
    npc = du.shape[0]
    grads["w_in"], landed["grad_w_in"] = _grad_matmul(
        s["h_b"][None], du, npc, lambda b: 0, lambda b: (b + DU_PLANE[0]) % npc, (d, npc * d), (d, d),
        lambda b: (0, b), ride("grad_w_in"))
    (dx, d_shift, d_scale, grads["norm_g"]), landed["in_bwd"] = _in_bwd(du, p["w4"], s["x"], dxn, p["norm_g"],
                                                                        p["scale"], ride("in_bwd"))
    return dx, grads, jnp.concatenate([d_shift, d_scale, d_gate], axis=1), landed


def _me():
    return lax.axis_index("x"), lax.axis_index("y"), lax.axis_index("c")


def _remote(src, dst, send_sem, recv_sem, to):
    return pltpu.make_async_remote_copy(src_ref=src, dst_ref=dst, send_sem=send_sem, recv_sem=recv_sem,
                                        device_id=to, device_id_type=MESH)


def _all_gather8(blocks, space):
    n = len(blocks)

    def body(*refs):
        x_refs, out_refs = refs[:n], refs[n:2 * n]
        send_sems, recv_sems, local_sems = refs[2 * n:]
        x, y, c = _me()
        me, sibling = (x, y, c), (x, y, 1 - c)
        chips = [(1 - x, y), (x, 1 - y), (1 - x, 1 - y)]

        def rows(i, px, py, pc):
            m_per = blocks[i].shape[0]
            return out_refs[i].at[pl.ds((4 * px + 2 * py + pc) * m_per, m_per), :]

        def copy(i, k, blk, to, src=None):
            return _remote(rows(i, *blk) if src is None else src, rows(i, *blk), send_sems.at[7 * i + k],
                           recv_sems.at[7 * i + k], to)

        mine = [pltpu.make_async_copy(x_refs[i], rows(i, *me), local_sems.at[i]) for i in range(n)]
        first = []
        for i in range(n):
            first.append(copy(i, 0, me, sibling, src=x_refs[i]))
            first += [copy(i, 1 + j, me, (*chip, c), src=x_refs[i]) for j, chip in enumerate(chips)]
        for cp in mine + first:
            cp.start()
        passed = []
        for j, chip in enumerate(chips):
            for i in range(n):
                copy(i, 1 + j, (*chip, c), me).wait_recv()
                passed.append(copy(i, 4 + j, (*chip, c), sibling))
                passed[-1].start()
        for i in range(n):
            copy(i, 0, sibling, me).wait_recv()
            for j, chip in enumerate(chips):
                copy(i, 4 + j, (*chip, 1 - c), me).wait_recv()
        for cp in first + passed:
            cp.wait_send()
        for cp in mine:
            cp.wait()

    spec = pl.BlockSpec(memory_space=space)
    return _pcall(
        body, name="all_gather8",
        out_shape=[jax.ShapeDtypeStruct((8 * b.shape[0], b.shape[1]), b.dtype) for b in blocks],
        in_specs=[spec] * n, out_specs=[spec] * n,
        scratch_shapes=[pltpu.SemaphoreType.DMA((7 * n,)), pltpu.SemaphoreType.DMA((7 * n,)),
                        pltpu.SemaphoreType.DMA((n,))],
    )(*blocks)


def _exchange(legs):
    n = len(legs)

    def body(*refs):
        copies, local = _exchange_body(legs, refs[:n], refs[n:2 * n], *refs[2 * n:])
        for cp in copies + local:
            cp.start()
        for cp in copies:
            cp.wait_recv()
        for cp in copies:
            cp.wait_send()
        for cp in local:
            cp.wait()

    hbm = pl.BlockSpec(memory_space=pltpu.HBM)
    return _pcall(body, name="exchange", out_shape=[leg.landing() for leg in legs], in_specs=[hbm] * n,
                  out_specs=[hbm] * n, input_output_aliases=_exchange_aliases(legs, 0, 0),
                  scratch_shapes=_exchange_sems(legs))(*[leg.src for leg in legs])


def _row_tile(rows, cap=4096, mult=16):
    best = None
    for t in range(mult, min(rows, cap) + 1, mult):
        if rows % t == 0:
            best = t
    return rows if best is None else best


def _pair_sum(half, own, own_spec, got, got_spec, out_shape, out_spec, grid):
    def body(_, a_ref, b_ref, o_ref):
        o_ref[...] = (a_ref[...] + b_ref[...].astype(F32)).astype(o_ref.dtype)

    return _pcall(
        body, name="pair_sum",
        grid_spec=pltpu.PrefetchScalarGridSpec(num_scalar_prefetch=1, grid=grid, in_specs=[own_spec, got_spec],
                                               out_specs=out_spec),
        out_shape=out_shape, compiler_params=_seq(len(grid)))(half, own, got)


def _chip_sum(ids, part, met, fill, layer=0, stack=1):
    _, _, rows, n = part.shape
    tr = _row_tile(rows, cap=max(16, BLOCK_ELEMS // n))
    first = isinstance(stack, int)

    def body(_, own_ref, a_ref, b_ref, c_ref, *rest):
        acc = own_ref[...].astype(F32) + a_ref[...].astype(F32)
        acc = acc + b_ref[...].astype(F32)
        rest[-1][...] = acc + c_ref[...].astype(F32)

    blk = (None, None, tr, n)
    other = lambda k: pl.BlockSpec(blk, lambda j, ids: ((ids[0] + k) % 4, 0, j, 0))
    in_specs = [pl.BlockSpec(blk, lambda j, ids: (ids[0], 0, j, 0)), other(1), other(2), other(3)]
    return _pcall(
        body, name="chip_sum",
        grid_spec=pltpu.PrefetchScalarGridSpec(
            num_scalar_prefetch=1, grid=(rows // tr,),
            in_specs=in_specs if first else in_specs + [pl.BlockSpec(memory_space=pl.ANY)],
            out_specs=pl.BlockSpec(blk, lambda j, ids: (layer, ids[1] if fill else 0, j, 0))),
        out_shape=jax.ShapeDtypeStruct(((stack,) if first else stack.shape[:1]) + (2 if fill else 1, rows, n), F32),
        input_output_aliases={} if first else {5: 0},
        compiler_params=_seq())(*((ids, part, met, met, met) if first else (ids, part, met, met, met, stack)))


def _ada_mod(c_all, w_ada, b_ada_cols):
    depth, d, n = w_ada.shape
    nb = c_all.shape[0]

    def body(c_ref, w_ref, b_ref, o_ref):
        cv = c_ref[...]
        ca = _bf(cv * _sigmoid(cv))
        o_ref[0] = _dot(ca, _bf(w_ref[0])) + b_ref[0]

    return _pcall(body, name="ada_mod", grid=(depth,),
                  in_specs=[pl.BlockSpec((nb, d), lambda l: (0, 0)), pl.BlockSpec((1, d, n), lambda l: (l, 0, 0)),
                            pl.BlockSpec((1, 1, n), lambda l: (l, 0, 0))],
                  out_specs=pl.BlockSpec((1, nb, n), lambda l: (l, 0, 0)),
                  out_shape=jax.ShapeDtypeStruct((depth, nb, n), F32), compiler_params=_seq())(c_all, w_ada, b_ada_cols)


def _ada_grad(c_all, dmod_cols, rows_all):
    nb, d = c_all.shape
    depth, _, n = dmod_cols.shape
    kinds, n_all = rows_all.shape[1], rows_all.shape[3]

    def body(c_ref, dm_ref, da_ref, gw_ref, gb_ref):
        cv = c_ref[...]
        ca = _bf(cv * _sigmoid(cv))
        gw_ref[0] = _dot_tn(ca, _bf(dm_ref[0]))
        for k in range(kinds):
            gb_ref[0, k] = _colsum(da_ref[0, k])

    return _pcall(body, name="ada_grad", grid=(depth,),
                  in_specs=[pl.BlockSpec((nb, d), lambda l: (0, 0)), pl.BlockSpec((1, nb, n), lambda l: (l, 0, 0)),
                            pl.BlockSpec((1, kinds, nb, n_all), lambda l: (l, 0, 0, 0))],
                  out_specs=[pl.BlockSpec((1, d, n), lambda l: (l, 0, 0)),
                             pl.BlockSpec((1, kinds, 1, n_all), lambda l: (l, 0, 0, 0))],
                  out_shape=[jax.ShapeDtypeStruct((depth, d, n), F32), jax.ShapeDtypeStruct((depth, kinds, 1, n_all), F32)],
                  compiler_params=_seq())(c_all, dmod_cols, rows_all)


def _adamw(items, ride=None):
    two_d = [tuple(t.reshape(w.size // w.shape[-1], w.shape[-1]) for t in (w, g, m, v)) for w, g, m, v in items]
    n = len(items)
    if n == 1:
        rows, cols = two_d[0][0].shape
        tr = _row_tile(rows, cap=max(8, BLOCK_ELEMS // cols), mult=8)
        blocks = [pl.BlockSpec((tr, cols), lambda i: (i, 0))]
        grid = (rows // tr,)
    else:
        blocks = [pl.BlockSpec(t[0].shape, lambda i: (0, 0)) for t in two_d]
        grid = (1,)

    def body(*refs):
        for k in range(n):
            w_ref, g_ref, m_ref, v_ref = refs[4 * k:4 * k + 4]
            d_ref, mo_ref, vo_ref = refs[4 * n + 3 * k:4 * n + 3 * k + 3]
            gv = g_ref[...]
            mn = ADAM_B1 * m_ref[...] + (1.0 - ADAM_B1) * gv
            vn = ADAM_B2 * v_ref[...] + (1.0 - ADAM_B2) * (gv * gv)
            m_hat = mn / (1.0 - ADAM_B1 ** ADAM_STEP)
            v_hat = vn / (1.0 - ADAM_B2 ** ADAM_STEP)
            d_ref[...] = -ADAM_LR * (m_hat / (jnp.sqrt(v_hat) + ADAM_EPS) + ADAM_WD * w_ref[...])
            mo_ref[...] = mn
            vo_ref[...] = vn

    outs, got = _pcall_ride(
        body, ride, name="adamw", grid=grid,
        in_specs=[b for b in blocks for _ in range(4)], out_specs=[b for b in blocks for _ in range(3)],
        out_shape=[jax.ShapeDtypeStruct(t[0].shape, F32) for t in two_d for _ in range(3)],
        compiler_params=_seq(), args=tuple(a for t in two_d for a in t))
    return [tuple(o.reshape(items[k][0].shape) for o in outs[3 * k:3 * k + 3]) for k in range(n)], got


WEIGHTS = ["norm_g", "w_ada", "b_ada", "w_in", "rg_conv_w", "rg_conv_b", "rg_w_a", "rg_b_a", "rg_w_x", "rg_b_x",
           "rg_lambda", "ml_conv_w", "ml_conv_b", "ml_w_q", "ml_w_k", "ml_w_v", "ml_w_if", "ml_b_if", "ml_norm_g",
           "w_out", "final_g"]
SMALL_SHARDED = {"ml_w_qkv": 2, "rg_conv_w": 1, "ml_conv_w": 1, "ml_w_if": 0}
REPLICATED = ["rg_w_a", "rg_w_x", "rg_conv_b", "rg_b_a", "rg_b_x", "rg_lambda", "ml_conv_b", "ml_norm_g", "ml_b_if"]
LANES = 128


def _to_pieces(g, axis):
    shp = g.shape
    g = g.reshape(shp[:axis] + (4, 2, shp[axis] // 8) + shp[axis + 1:])
    g = jnp.moveaxis(g, (axis, axis + 1), (0, 1))
    return g.reshape(4, 2, -1)


def _from_pieces(p, shard_shape, axis):
    k = p.shape[0]
    rest = shard_shape[:axis] + (shard_shape[axis] // k,) + shard_shape[axis + 1:]
    t = jnp.moveaxis(p.reshape((k,) + rest), 0, axis)
    return t.reshape(shard_shape)


def _pad_rows(flat, mult):
    n = flat.shape[-1]
    pad = (-n) % mult
    if pad:
        flat = jnp.concatenate([flat, jnp.zeros(flat.shape[:-1] + (pad,), flat.dtype)], axis=-1)
    return flat


def kernel(x, c, norm_g, w_ada, b_ada, w_in, rg_conv_w, rg_conv_b, rg_w_a, rg_b_a, rg_w_x, rg_b_x, rg_lambda, ml_conv_w, ml_conv_b, ml_w_q, ml_w_k, ml_w_v, ml_w_if, ml_b_if, ml_norm_g, w_out, final_g, loss_target, m_norm_g, m_w_ada, m_b_ada, m_w_in, m_rg_conv_w, m_rg_conv_b, m_rg_w_a, m_rg_b_a, m_rg_w_x, m_rg_b_x, m_rg_lambda, m_ml_conv_w, m_ml_conv_b, m_ml_w_q, m_ml_w_k, m_ml_w_v, m_ml_w_if, m_ml_b_if, m_ml_norm_g, m_w_out, m_final_g, v_norm_g, v_w_ada, v_b_ada, v_w_in, v_rg_conv_w, v_rg_conv_b, v_rg_w_a, v_rg_b_a, v_rg_w_x, v_rg_b_x, v_rg_lambda, v_ml_conv_w, v_ml_conv_b, v_ml_w_q, v_ml_w_k, v_ml_w_v, v_ml_w_if, v_ml_b_if, v_ml_norm_g, v_w_out, v_final_g):
    given = dict(locals())
    ax, ay, ac = lax.axis_index("x"), lax.axis_index("y"), lax.axis_index("c")
    chip = 2 * ax + ay
    me = 2 * chip + ac
    depth, d = norm_g.shape
    n_ada = w_ada.shape[2]
    pick = lambda a, i, axis=0: lax.dynamic_index_in_dim(a, i, axis, keepdims=False)

    convs = jnp.stack([rg_conv_w, ml_conv_w])
    n_conv = 2 * depth * CONV_WIDTH // 4
    blk = jnp.concatenate([c, convs.reshape(n_conv, d), jnp.zeros((8 - 1 - n_conv, d), F32)], axis=0)
    w_in_first = lax.dynamic_slice_in_dim(w_in[0], ac * (d // 2), d // 2, 0).astype(BF16)
    g0, w_in_first = _all_gather8([blk, w_in_first], pltpu.HBM)
    g0 = g0.reshape(8, 8, d)
    c_all = g0[:, 0, :]
    conv_full = g0[0::2, 1:1 + n_conv].reshape(4, 2, depth, CONV_WIDTH, d // 4)
    conv_full = conv_full.transpose(1, 2, 3, 0, 4).reshape(2, depth, CONV_WIDTH, d)

    b_cols = lax.dynamic_slice_in_dim(b_ada, chip * n_ada, n_ada, axis=1)[:, None, :]
    mod_part = _ada_mod(c_all, w_ada, b_cols)
    g1 = _all_gather8([mod_part.transpose(1, 0, 2).reshape(8, depth * n_ada)], pltpu.VMEM)[0]
    g1 = g1.reshape(8, 8, depth, n_ada)[0::2]
    mod_me = pick(g1.transpose(1, 2, 0, 3).reshape(8, depth, 4 * n_ada), me)

    def half_of(w, axis):
        n = w.shape[axis] // 2
        return lax.dynamic_slice_in_dim(w, ac * n, n, axis).astype(BF16)

    n_sh = w_in.shape[2]
    heads, hd_cut, hd = ml_w_q.shape[1:]

    def blocks_of(l):
        wqkv = jnp.stack([ml_w_q[l], ml_w_k[l], ml_w_v[l]])
        return [half_of(w_in[l], 0), half_of(w_out[l], 0), half_of(wqkv, 2).reshape(-1, hd), half_of(ml_w_if[l], 0)]

    def layer_of(l, w4, rest):
        return dict(
            norm_g=norm_g[l][None], shift=mod_me[l, 0:d][None], scale=mod_me[l, d:2 * d][None],
            gate=mod_me[l, 2 * d:3 * d][None], w4=w4.reshape(4, d, n_sh),
            rg_conv_w=conv_full[0, l], rg_conv_b=rg_conv_b[l][None], rg_wa_b=_bf(rg_w_a[l]), rg_ba=rg_b_a[l][None],
            rg_wx_b=_bf(rg_w_x[l]), rg_bx=rg_b_x[l][None], rg_lam=rg_lambda[l][None],
            ml_conv_w=conv_full[1, l], ml_conv_b=ml_conv_b[l][None], b_if=ml_b_if[l][None], b_ift=ml_b_if[l][:, None],
            ml_g=ml_norm_g[l][None], **rest)

    def rest_of(gathered):
        w_out_b, wqkv_g, wif = gathered
        return dict(w_out_b=w_out_b, wqkv_b=_from_pieces(wqkv_g.reshape(8, -1), (3, heads, hd, hd), 2), wif_b=wif,
                    wift_b=wif.T)

    spread = lambda blocks: [Leg(b, "spread") for b in blocks]
    fill = lambda landed: [Leg(t, "sib_fill") for t in landed]
    flat = lambda filled: [t.reshape(-1, t.shape[-1]) for t in filled]
    first = blocks_of(0)
    n_rest = len(first) - 1
    p = layer_of(0, w_in_first, {})
    layers, saved = [], []
    xl = x[0]
    for l in range(depth):
        nxt = blocks_of(l + 1) if l + 1 < depth else []
        skip = n_rest if l == 0 else 0
        rides = dict(rg_fwd=lambda landed, nxt=nxt: spread(nxt[:1]))
        if l == 0:
            rides.update(ln_inproj=lambda landed: spread(first[1:]),
                         rg_fwd=lambda landed, nxt=nxt: fill(landed["ln_inproj"]) + spread(nxt[:1]),
                         late=lambda landed: rest_of(flat(landed["rg_fwd"][:n_rest])))
        if nxt:
            rides.update(mlstm_fwd=lambda landed, nxt=nxt: spread(nxt[1:]),
                         out_proj=lambda landed, skip=skip: fill(list(landed["rg_fwd"][skip:]) + list(landed["mlstm_fwd"])))
        xl, s, p, landed = _layer_fwd(xl, p, rides, None if nxt else (final_g[None], loss_target[0]))
        layers.append(p)
        saved.append(s)
        if nxt:
            arrived = flat(landed["out_proj"])
            p = layer_of(l + 1, arrived[0], rest_of(arrived[1:]))
    dx, g_final, loss = xl

    half = ac.reshape(1)
    ids = jnp.stack([chip, ac])
    r_out = w_out.shape[1] // 2

    def pair_in(g_w_in, got_in):
        return _pair_sum(
            half, g_w_in, pl.BlockSpec((None, d // 2, n_sh), lambda s, h: (0, h[0], s)),
            got_in, pl.BlockSpec((None, None, d // 2, n_sh), lambda s, h: (0, s, 0, 0)),
            jax.ShapeDtypeStruct((4, 1, d // 2, n_sh), BF16),
            pl.BlockSpec((None, None, d // 2, n_sh), lambda s, h: (s, 0, 0, 0)), (4,))

    def pair_out(g_out5, got_out):
        return _pair_sum(
            half, g_out5, pl.BlockSpec((None, None, None, r_out, d), lambda s, h: (0, s, h[0], 0, 0)),
            got_out, pl.BlockSpec((None, None, r_out, d), lambda s, h: (0, s, 0, 0)),
            jax.ShapeDtypeStruct((4, 1, r_out, d), BF16),
            pl.BlockSpec((None, None, r_out, d), lambda s, h: (s, 0, 0, 0)), (4,))

    def pair_slab(slab, got, dtype):
        rows = got.shape[0] // 4
        blk = pl.BlockSpec((rows, LANES), lambda s, h: (s, 0))
        return _pair_sum(half, slab, pl.BlockSpec((None, rows, LANES), lambda s, h: (h[0], s, 0)), got, blk,
                         jax.ShapeDtypeStruct((4 * rows, LANES), dtype), blk, (4,)).reshape(4, 1, rows, LANES)

    row_pad = lambda n: -(-n // (8 * LANES)) * (8 * LANES)

    def as_rows(t):
        if t.shape[-1] == LANES and t.size % (8 * LANES) == 0:
            return t.reshape(-1, LANES)
        return _pad_rows(t.reshape(-1), 8 * LANES).reshape(-1, LANES)

    chips = lambda arrs: [Leg(a, "chips") for a in arrs]
    out5 = lambda g: g["w_out"].reshape(1, 4, 2, r_out, d)
    grads, dmods, parts, mets = [None] * depth, [None] * depth, [None] * depth, [None] * depth
    small = {}

    def early_exchange(g, landed):
        every = [g] + grads[1:]
        sm = jnp.concatenate([_to_pieces(every[l][name], axis) for l in range(depth)
                              for name, axis in SMALL_SHARDED.items()], axis=-1)
        sm = _pad_rows(sm, 16 * LANES)
        sm = sm.transpose(1, 0, 2).reshape(2, -1, LANES)
        rep = [as_rows(every[l][name]) for l in range(depth) for name in REPLICATED]
        rep = jnp.concatenate(rep + [as_rows(g_final), as_rows(loss)], axis=0)
        rep = jnp.concatenate([rep, jnp.zeros(((-rep.shape[0]) % 64, LANES), F32)], axis=0)
        rep = rep.reshape(4, 2, -1, LANES).transpose(1, 0, 2, 3).reshape(2, -1, LANES)
        got_sm, got_rep = _exchange([Leg(sm, "sib_slab"), Leg(rep, "sib_slab")])
        small["parts"] = [pair_out(out5(g), landed["mlstm_bwd"][0]), pair_slab(sm, got_sm, BF16),
                          pair_slab(rep, got_rep, F32)]
        return chips(small["parts"])

    def last_exchange(g, landed):
        (got_in,) = _exchange([Leg(g["w_in"], "sib_w_in")])
        small["part_in"] = pair_in(g["w_in"], got_in)
        return chips([small["part_in"]])

    for l in reversed(range(depth)):
        above = parts[l + 1] if l + 1 < depth else []
        rides = dict(mlstm_bwd=lambda g, landed, above=above: [Leg(out5(g), "sib_w_out")] + chips(above),
                     in_bwd=lambda g, landed: [Leg(g["w_in"], "sib_w_in")])
        if l == 0:
            rides.update(grad_w_in=early_exchange, in_bwd=last_exchange)
        dx, grads[l], dmods[l], got = _layer_bwd(dx, layers[l], saved[l], rides)
        if above:
            mets[l + 1] = got["mlstm_bwd"][1:]
        if l > 0:
            parts[l] = [pair_in(grads[l]["w_in"], got["in_bwd"][0]), pair_out(out5(grads[l]), got["mlstm_bwd"][0])]
    part_out, part_sm, part_rep = small["parts"]
    met_out, met_sm, met_rep = got["grad_w_in"]
    parts[0], mets[0] = [small["part_in"], part_out], [got["in_bwd"][0], met_out]
    n_rep = part_rep.shape[2]

    pad = lambda t: jnp.concatenate([t, jnp.zeros((1, 2 * d), F32)], axis=1)
    rows = [r for l in range(depth) for r in (dmods[l], pad(grads[l]["norm_g"]))]
    blk = jnp.concatenate(rows + [jnp.zeros((8 - 2 * depth, 3 * d), F32)], axis=0)
    rows_all = _all_gather8([blk], pltpu.VMEM)[0].reshape(8, 8, 3 * d)[:, :2 * depth]
    rows_all = rows_all.transpose(1, 0, 2).reshape(depth, 2, 8, 3 * d)
    dm_cols = lax.dynamic_slice_in_dim(rows_all[:, 0], chip * n_ada, n_ada, axis=2)
    g_w_ada, summed = _ada_grad(c_all, dm_cols, rows_all)

    g = dict(w_ada=g_w_ada, b_ada=summed[:, 0, 0], norm_g=summed[:, 1, 0, :d])
    item = lambda name: (given[name], g[name], given["m_" + name], given["v_" + name])
    both_in, both_out = depth, depth
    for l in range(depth):
        both_in = _chip_sum(ids, parts[l][0], mets[l][0], True, l, both_in)
        both_out = _chip_sum(ids, parts[l][1], mets[l][1], True, l, both_out)
    both_in, both_out, both_sm = _exchange(fill([both_in, both_out, _chip_sum(ids, part_sm, met_sm, True)]))
    red_rep = _chip_sum(ids, part_rep, met_rep, False).reshape(n_rep, LANES)
    rep_all = _all_gather8([red_rep], pltpu.VMEM)[0].reshape(-1)

    g.update(w_in=both_in.reshape(w_in.shape), w_out=both_out.reshape(w_out.shape))
    shard = both_sm.reshape(2, -1)
    off = 0
    per_layer = {name: [] for name in SMALL_SHARDED}
    for l in range(depth):
        for name, axis in SMALL_SHARDED.items():
            shp = (3,) + ml_w_q.shape[1:] if name == "ml_w_qkv" else given[name].shape[1:]
            n = grads[l][name].size // 8
            per_layer[name].append(_from_pieces(shard[:, off:off + n], shp, axis))
            off += n
    for name in SMALL_SHARDED:
        g[name] = jnp.stack(per_layer[name])
    for i, name in enumerate(["ml_w_q", "ml_w_k", "ml_w_v"]):
        g[name] = g["ml_w_qkv"][:, i]
    off = 0
    per_layer = {name: [] for name in REPLICATED}
    for l in range(depth):
        for name in REPLICATED:
            n = given[name][l].size
            per_layer[name].append(rep_all[off:off + n].reshape(given[name].shape[1:]))
            off += row_pad(n)
    for name in REPLICATED:
        g[name] = jnp.stack(per_layer[name])
    g["final_g"] = rep_all[off:off + d]
    loss_all = rep_all[off + row_pad(d)]

    stepped = {}
    rg_mats, ml_mats = ["rg_w_a", "rg_w_x"], ["ml_w_q", "ml_w_k", "ml_w_v"]
    vectors = [n for n in WEIGHTS if n not in ["w_ada", "w_in", "w_out"] + rg_mats + ml_mats]
    for names in (["w_ada"], ["w_in"], ["w_out"], rg_mats, ml_mats, vectors):
        stepped.update(zip(names, _adamw([item(name) for name in names])[0]))
    deltas, new_m, new_v = zip(*[stepped[name] for name in WEIGHTS])
    return (loss_all, dx[None], *[g[name] for name in WEIGHTS], *deltas, *new_m, *new_v)
```

```python
import functools
from typing import NamedTuple

import jax
import jax.numpy as jnp
from jax import lax
from jax.experimental import pallas as pl
from jax.experimental.pallas import tpu as pltpu

F32 = jnp.float32
BF16 = jnp.bfloat16

EPS = 1e-6
RG_C = 8.0
CONV_WIDTH = 4
ML_CHUNK = 512
HALO = 8
ROWS_VECTOR = 512
ROWS_MATMUL = 1024
ROWS_IN_BWD = 512
ROWS_GRAD_MATMUL = 2048
BLOCK_ELEMS = 1 << 18
ADAM_LR = 0.001
ADAM_B1 = 0.9
ADAM_B2 = 0.999
ADAM_EPS = 1e-08
ADAM_WD = 0.01
ADAM_STEP = 10
MESH = pl.DeviceIdType.MESH


def _pcall(body, **kw):
    return pl.pallas_call(body, **kw)


class Leg(NamedTuple):
    src: jax.Array
    kind: str

    def landing(self):
        a = self.src
        shape = {"chips": lambda: a.shape, "spread": lambda: (4, 2) + a.shape, "sib_fill": lambda: a.shape,
                 "sib_w_in": lambda: (a.shape[0], 4, a.shape[1] // 2, a.shape[2] // 4),
                 "sib_w_out": lambda: a.shape[:2] + a.shape[3:], "sib_slab": lambda: a.shape[1:]}[self.kind]()
        return jax.ShapeDtypeStruct(shape, a.dtype)

    def copies(self, src, dst, x, y, c):
        a, me_s, o = self.src, 2 * x + y, 1 - c
        chips = [(1 - x, y), (x, 1 - y), (1 - x, 1 - y)]
        if self.kind == "chips":
            return [(src.at[2 * px + py], dst.at[me_s], (px, py, c)) for px, py in chips], []
        if self.kind == "spread":
            return [(src, dst.at[me_s, c], (px, py, c)) for px, py in chips], [(src, dst.at[me_s, c])]
        depth = pl.ds(0, a.shape[0])
        if self.kind == "sib_fill":
            return [(dst.at[depth, c], dst.at[depth, c], (x, y, o))], []
        if self.kind == "sib_w_in":
            half, n = a.shape[1] // 2, a.shape[2] // 4
            return [(src.at[depth, pl.ds(o * half, half), pl.ds(s * n, n)], dst.at[depth, s], (x, y, o))
                    for s in range(4)], []
        if self.kind == "sib_w_out":
            return [(src.at[depth, pl.ds(0, 4), o], dst, (x, y, o))], []
        return [(src.at[o], dst, (x, y, o))], []

    def n_copies(self):
        return {"chips": 3, "spread": 3, "sib_w_in": 4}.get(self.kind, 1)


def _exchange_body(legs, srcs, dsts, send_sems, recv_sems, local_sems):
    x, y, c = _me()
    remote, local, k = [], [], 0
    for i, leg in enumerate(legs):
        far, near = leg.copies(srcs[i], dsts[i], x, y, c)
        for src, dst, to in far:
            remote.append(_remote(src, dst, send_sems.at[k], recv_sems.at[k], to))
            k += 1
        local += [pltpu.make_async_copy(src, dst, local_sems.at[i]) for src, dst in near]
    return remote, local


def _exchange_sems(legs):
    n = sum(leg.n_copies() for leg in legs)
    return [pltpu.SemaphoreType.DMA((n,)), pltpu.SemaphoreType.DMA((n,)), pltpu.SemaphoreType.DMA((len(legs),))]


def _exchange_aliases(legs, n_in, n_out):
    return {n_in + i: n_out + i for i, leg in enumerate(legs) if leg.kind == "sib_fill"}


def _pcall_ride(body, ride, *, grid, in_specs, out_specs, out_shape, args, scratch_shapes=(), **kw):
    n_in, n_out, n_scr = len(in_specs), len(out_specs), len(scratch_shapes)
    if not ride:
        res = _pcall(body, grid=grid, in_specs=in_specs, out_specs=out_specs, out_shape=out_shape,
                     scratch_shapes=list(scratch_shapes), **kw)(*args)
        return res, []
    nr = len(ride)

    def riding(*refs):
        ins, rsrc = refs[:n_in], refs[n_in:n_in + nr]
        outs, rdst = refs[n_in + nr:n_in + nr + n_out], refs[n_in + nr + n_out:n_in + 2 * nr + n_out]
        scr = refs[n_in + 2 * nr + n_out:n_in + 2 * nr + n_out + n_scr]
        copies, local = _exchange_body(ride, rsrc, rdst, *refs[n_in + 2 * nr + n_out + n_scr:])
        first = functools.reduce(jnp.logical_and, [pl.program_id(a) == 0 for a in range(len(grid))])
        last = functools.reduce(jnp.logical_and, [pl.program_id(a) == grid[a] - 1 for a in range(len(grid))])

        @pl.when(first)
        def _():
            for cp in copies + local:
                cp.start()

        body(*ins, *outs, *scr)

        @pl.when(last)
        def _():
            for cp in copies:
                cp.wait_recv()
            for cp in copies:
                cp.wait_send()
            for cp in local:
                cp.wait()

    hbm = pl.BlockSpec(memory_space=pltpu.HBM)
    aliases = {**kw.pop("input_output_aliases", {}), **_exchange_aliases(ride, n_in, n_out)}
    res = _pcall(
        riding, grid=grid, in_specs=list(in_specs) + [hbm] * nr, out_specs=list(out_specs) + [hbm] * nr,
        out_shape=list(out_shape) + [leg.landing() for leg in ride], input_output_aliases=aliases,
        scratch_shapes=list(scratch_shapes) + _exchange_sems(ride), **kw)(*args, *[leg.src for leg in ride])
    return res[:n_out], res[n_out:]


def _seq(n=1):
    return pltpu.CompilerParams(dimension_semantics=("arbitrary",) * n)


def _dot(a, b):
    return jnp.dot(a, b, preferred_element_type=F32)


def _dot_nt(a, b):
    return lax.dot_general(a, b, (((1,), (1,)), ((), ())), preferred_element_type=F32)


def _dot_tn(a, b):
    return lax.dot_general(a, b, (((0,), (0,)), ((), ())), preferred_element_type=F32)


def _bf(x):
    return x.astype(BF16)


def _sigmoid(x):
    return 0.5 * jnp.tanh(0.5 * x) + 0.5


def _log1p(z):
    u = 1.0 + z
    return jnp.where(u == 1.0, z, jnp.log(u) * (z / jnp.where(u == 1.0, 1.0, u - 1.0)))


def _softplus(x):
    return jnp.maximum(x, 0.0) + _log1p(jnp.exp(-jnp.abs(x)))


def _log_sigmoid(x):
    return -_softplus(-x)


def _one_minus_sq(a, log_a):
    x = 2.0 * log_a
    small = -x * (1.0 + x * (0.5 + x * (1.0 / 6.0)))
    return jnp.where(x > -0.004, small, 1.0 - a * a)


def _dsilu(x, s):
    return s * (1.0 + x * (1.0 - s))


def _rowsum(x):
    return jnp.sum(x, axis=1, keepdims=True)


def _colsum(x):
    return jnp.sum(x, axis=0, keepdims=True)


def _shift_down(win, s):
    return win if s == 0 else pltpu.roll(win, s, 0)


def _shift_up(win, s):
    return win if s == 0 else pltpu.roll(win, win.shape[0] - s, 0)


def _conv_taps(win):
    return [_shift_down(win, CONV_WIDTH - 1 - k)[HALO:] for k in range(CONV_WIDTH)]


def _conv_fwd(taps, w_ref, b_ref):
    acc = b_ref[...] + w_ref[CONV_WIDTH - 1:CONV_WIDTH, :] * taps[CONV_WIDTH - 1]
    for k in range(CONV_WIDTH - 1):
        acc = acc + w_ref[k:k + 1, :] * taps[k]
    return acc


def _split3(x):
    hi = _bf(x)
    r1 = x - hi.astype(F32)
    mid = _bf(r1)
    lo = _bf(r1 - mid.astype(F32))
    return hi, mid, lo


def _tri_dot_left(tri, x):
    hi, mid, lo = _split3(x)
    return _dot(tri, hi) + _dot(tri, mid) + _dot(tri, lo)


def _tri_dot_right(x, tri):
    hi, mid, lo = _split3(x)
    return _dot(hi, tri) + _dot(mid, tri) + _dot(lo, tri)


def _tile(n, want):
    t = min(n, want)
    assert n % t == 0
    return t


def _ln_inproj(x, g, scale, shift, w4, ride=None):
    s_len, d = x.shape
    nj, _, nsh = w4.shape
    tm = _tile(s_len, ROWS_MATMUL)

    def body(x_ref, g_ref, sc_ref, sh_ref, w_ref, h_ref, u_ref, hs):
        @pl.when(pl.program_id(1) == 0)
        def _():
            xv = x_ref[...]
            r = lax.rsqrt(jnp.mean(xv * xv, axis=-1, keepdims=True) + EPS)
            hv = (xv * r * g_ref[...]) * (1.0 + sc_ref[...]) + sh_ref[...]
            hs[...] = _bf(hv)
            h_ref[...] = hs[...]

        u_ref[...] = _dot(hs[...], w_ref[0])

    vec = pl.BlockSpec((1, d), lambda i, j: (0, 0))
    return _pcall_ride(
        body, ride, name="ln_inproj", grid=(s_len // tm, nj),
        in_specs=[pl.BlockSpec((tm, d), lambda i, j: (i, 0)), vec, vec, vec,
                  pl.BlockSpec((1, d, nsh), lambda i, j: (j, 0, 0))],
        out_specs=[pl.BlockSpec((tm, d), lambda i, j: (i, 0)), pl.BlockSpec((tm, nsh), lambda i, j: (i, j))],
        out_shape=[jax.ShapeDtypeStruct((s_len, d), BF16), jax.ShapeDtypeStruct((s_len, nj * nsh), F32)],
        scratch_shapes=[pltpu.VMEM((tm, d), BF16)],
        compiler_params=_seq(2),
        args=(x, g, scale, shift, w4))


def _rg_gates(xc, wa_ref, ba_ref, wx_ref, bx_ref, lam_ref):
    heads, hd, _ = wa_ref.shape
    xb = _bf(xc)
    ga = jnp.concatenate([_dot(xb[:, h * hd:(h + 1) * hd], wa_ref[h]) for h in range(heads)], axis=1) + ba_ref[...]
    gx = jnp.concatenate([_dot(xb[:, h * hd:(h + 1) * hd], wx_ref[h]) for h in range(heads)], axis=1) + bx_ref[...]
    r = _sigmoid(ga)
    ig = _sigmoid(gx)
    sp = _softplus(-lam_ref[...])
    log_a = (-RG_C) * r * sp
    a = jnp.exp(log_a)
    mult = jnp.sqrt(_one_minus_sq(a, log_a))
    return r, ig, sp, log_a, a, mult


def _scan_groups(a, u, reverse):
    n, c = a.shape
    a = a.reshape(n // 8, 8, c)
    u = u.reshape(n // 8, 8, c)
    row = lax.broadcasted_iota(jnp.int32, a.shape, 1)
    for k in (1, 2, 4):
        sft = 8 - k if reverse else k
        a_sh, u_sh = pltpu.roll(a, sft, 1), pltpu.roll(u, sft, 1)
        ok = row < 8 - k if reverse else row >= k
        u = jnp.where(ok, a * u_sh + u, u)
        a = jnp.where(ok, a * a_sh, a)
    return a.reshape(n, c), u.reshape(n, c)


def _rg_fwd(u, conv_w, conv_b, wa_b, ba, wx_b, bx, lam, ride=None):
    s_len = u.shape[0]
    d = conv_w.shape[1]
    tm = _tile(s_len, ROWS_VECTOR)
    per = tm // HALO

    def body(x_ref, xp_ref, z_ref, cw_ref, cb_ref, wa_ref, ba_ref, wx_ref, bx_ref, lam_ref,
             hh_ref, y_ref, carry):
        i = pl.program_id(0)

        @pl.when(i == 0)
        def _():
            carry[...] = jnp.zeros_like(carry)

        prev = jnp.where(i == 0, 0.0, xp_ref[...])
        xc = _conv_fwd(_conv_taps(jnp.concatenate([prev, x_ref[...]], axis=0)), cw_ref, cb_ref)
        _, ig, _, _, a, mult = _rg_gates(xc, wa_ref, ba_ref, wx_ref, bx_ref, lam_ref)
        ca, cu = _scan_groups(a, mult * (ig * xc), reverse=False)
        c = carry[0:1, :]
        for j in range(tm // 8):
            blk = ca[j * 8:(j + 1) * 8] * c + cu[j * 8:(j + 1) * 8]
            hh_ref[j * 8:(j + 1) * 8, :] = blk
            c = blk[7:8]
        carry[0:1, :] = c
        z = z_ref[...]
        y_ref[0] = _bf(hh_ref[...] * (z * _sigmoid(z)))

    vec = pl.BlockSpec((1, d), lambda i: (0, 0))
    whole3 = lambda a: pl.BlockSpec(a.shape, lambda i: (0, 0, 0))
    return _pcall_ride(
        body, ride, name="rg_fwd", grid=(s_len // tm,),
        in_specs=[pl.BlockSpec((tm, d), lambda i: (i, 0)),
                  pl.BlockSpec((HALO, d), lambda i: (jnp.maximum(i * per - 1, 0), 0)),
                  pl.BlockSpec((tm, d), lambda i: (i, 1)),
                  pl.BlockSpec((CONV_WIDTH, d), lambda i: (0, 0)), vec,
                  whole3(wa_b), vec, whole3(wx_b), vec, vec],
        out_specs=[pl.BlockSpec((tm, d), lambda i: (i, 0)), pl.BlockSpec((1, tm, d), lambda i: (0, i, 0))],
        out_shape=[jax.ShapeDtypeStruct((s_len, d), F32), jax.ShapeDtypeStruct((2, s_len, d), BF16)],
        scratch_shapes=[pltpu.VMEM((8, d), F32)],
        compiler_params=_seq(),
        args=(u, u, u, conv_w, conv_b, wa_b, ba, wx_b, bx, lam))


def _ml_pre(u, conv_w, conv_b, wqkv_b, wif_b, wift_b, b_if, b_ift):
    s_len = u.shape[0]
    d = conv_w.shape[1]
    _, heads, hd, _ = wqkv_b.shape
    ng = 2 * heads
    tm = _tile(s_len, max(ROWS_VECTOR, ML_CHUNK))
    per = tm // HALO

    def body(x_ref, xp_ref, cw_ref, cb_ref, w_ref, wif_ref, wift_ref, bif_ref, bift_ref,
             qkv_ref, gt_ref, gtt_ref, bc_ref, bct_ref):
        i = pl.program_id(0)
        prev = jnp.where(i == 0, 0.0, xp_ref[...])
        xm = x_ref[...]
        pre = _conv_fwd(_conv_taps(jnp.concatenate([prev, xm], axis=0)), cw_ref, cb_ref)
        xcb = _bf(pre * _sigmoid(pre))
        xmb = _bf(xm)
        for h in range(heads):
            hs = slice(h * hd, (h + 1) * hd)
            qkv_ref[0, :, hs] = _bf(_dot(xcb[:, hs], w_ref[0, h]))
            qkv_ref[1, :, hs] = _bf(_dot(xcb[:, hs], w_ref[1, h]))
            qkv_ref[2, :, hs] = _bf(_dot(xmb[:, hs], w_ref[2, h]))
        qb, kb, vb = qkv_ref[0], qkv_ref[1], qkv_ref[2]
        gt = (_dot(qb, wif_ref[0:d, :]) + _dot(kb, wif_ref[d:2 * d, :]) + _dot(vb, wif_ref[2 * d:3 * d, :])
              + bif_ref[...])
        gtt = (_dot_nt(wift_ref[:, 0:d], qb) + _dot_nt(wift_ref[:, d:2 * d], kb)
               + _dot_nt(wift_ref[:, 2 * d:3 * d], vb) + bift_ref[...])
        gt_ref[...] = gt
        gtt_ref[...] = gtt
        r = lax.broadcasted_iota(jnp.int32, (tm, tm), 0)
        c = lax.broadcasted_iota(jnp.int32, (tm, tm), 1)
        same = (r // ML_CHUNK) == (c // ML_CHUNK)
        bc_ref[...] = _tri_dot_left(((r >= c) & same).astype(BF16), _log_sigmoid(gt))
        bct_ref[...] = _tri_dot_right(_log_sigmoid(gtt), ((r <= c) & same).astype(BF16))

    vec = pl.BlockSpec((1, d), lambda i: (0, 0))
    whole2 = lambda a: pl.BlockSpec(a.shape, lambda i: (0, 0))
    col = pl.BlockSpec((tm, ng), lambda i: (i, 0))
    row = pl.BlockSpec((ng, tm), lambda i: (0, i))
    return _pcall(
        body, name="ml_pre", grid=(s_len // tm,),
        in_specs=[pl.BlockSpec((tm, d), lambda i: (i, 2)),
                  pl.BlockSpec((HALO, d), lambda i: (jnp.maximum(i * per - 1, 0), 2)),
                  pl.BlockSpec((CONV_WIDTH, d), lambda i: (0, 0)), vec,
                  pl.BlockSpec(wqkv_b.shape, lambda i: (0, 0, 0, 0)), whole2(wif_b), whole2(wift_b), whole2(b_if),
                  whole2(b_ift)],
        out_specs=[pl.BlockSpec((3, tm, d), lambda i: (0, i, 0)), col, row, col, row],
        out_shape=[jax.ShapeDtypeStruct((3, s_len, d), BF16), jax.ShapeDtypeStruct((s_len, ng), F32),
                   jax.ShapeDtypeStruct((ng, s_len), F32), jax.ShapeDtypeStruct((s_len, ng), F32),
                   jax.ShapeDtypeStruct((ng, s_len), F32)],
        compiler_params=_seq(),
    )(u, u, conv_w, conv_b, wqkv_b, wif_b, wift_b, b_if, b_ift)


def _chunk_gates(gt, gtt, bc, bct, h, heads):
    li_c = gt[:, h:h + 1]
    li_r = gtt[h:h + 1, :]
    gf_c = gt[:, heads + h:heads + h + 1]
    b_c = bc[:, heads + h:heads + h + 1]
    b_r = bct[heads + h:heads + h + 1, :]
    return li_c, li_r, gf_c, b_c, b_r


def _chunk_weights(li_c, li_r, b_c, b_r, m_prev, causal):
    lc = b_c.shape[0]
    b_last = b_c[lc - 1:lc, :]
    dmat = jnp.where(causal, b_c - b_r + li_r, -jnp.inf)
    m_inter = b_c + m_prev
    m_t = jnp.maximum(m_inter, jnp.max(dmat, axis=1, keepdims=True))
    w_intra = jnp.exp(dmat - m_t)
    w_inter = jnp.exp(m_inter - m_t)
    g_c = b_last - b_c + li_c
    m_new = jnp.maximum(b_last + m_prev, jnp.max(g_c, axis=0, keepdims=True))
    w_state = jnp.exp(g_c - m_new)
    decay = jnp.exp(b_last + m_prev - m_new)
    return m_t, w_intra, w_inter, m_new, w_state, decay


def _tri_masks(lc):
    r = lax.broadcasted_iota(jnp.int32, (lc, lc), 0)
    c = lax.broadcasted_iota(jnp.int32, (lc, lc), 1)
    causal = r >= c
    return causal, causal.astype(BF16), (r <= c).astype(BF16)


def _mlstm_fwd(qkv, gates, u, ml_g, ycat, ride=None):
    _, s_len, d = qkv.shape
    ng = gates[0].shape[1]
    heads = ng // 2
    hd = d // heads
    lc = ML_CHUNK
    nc = s_len // lc
    kscale = hd ** -0.5

    def body(qkv_ref, gt_ref, gtt_ref, bc_ref, bct_ref, o_ref, z_ref, g_ref, _, cell_ref, y_ref, cst_ref, nst_ref,
             mst_ref, cs, ns, ms):
        @pl.when(pl.program_id(0) == 0)
        def _():
            cs[...] = jnp.zeros_like(cs)
            ns[...] = jnp.zeros_like(ns)
            ms[...] = jnp.zeros_like(ms)

        causal = _tri_masks(lc)[0]
        gtv, gttv, bcv, bctv = gt_ref[...], gtt_ref[...], bc_ref[...], bct_ref[...]
        old = [(cs[h], ns[h], ms[h]) for h in range(heads)]
        new, cells, ys = [], [], []
        for h in range(heads):
            hs = slice(h * hd, (h + 1) * hd)
            li_c, li_r, _, b_c, b_r = _chunk_gates(gtv, gttv, bcv, bctv, h, heads)
            c_old, n_old, m_old = old[h]
            m_prev = m_old[:, 0:1]
            m_t, w_intra, w_inter, m_new, w_state, decay = _chunk_weights(li_c, li_r, b_c, b_r, m_prev, causal)
            qb = qkv_ref[0, :, hs]
            ks = qkv_ref[1, :, hs].astype(F32) * kscale
            kb = _bf(ks)
            vb = qkv_ref[2, :, hs]
            s = _dot_nt(qb, kb) * w_intra
            num = _dot(_bf(s), vb) + w_inter * _dot(qb, _bf(c_old))
            den = _rowsum(s) + w_inter * _rowsum(qb.astype(F32) * n_old)
            cell = num / jnp.maximum(jnp.abs(den), jnp.exp(-m_t))
            kw = ks * w_state
            new.append((decay * c_old + _dot_tn(_bf(kw), vb), decay * n_old + _colsum(kw),
                        jnp.broadcast_to(m_new, m_old.shape)))
            cells.append(cell)
            hm = _sigmoid(o_ref[:, hs]) * cell
            hn = hm * lax.rsqrt(jnp.mean(hm * hm, axis=-1, keepdims=True) + EPS)
            z = z_ref[:, hs]
            ys.append(_bf((hn * g_ref[:, hs]) * (z * _sigmoid(z))))
        for h in range(heads):
            cst_ref[0, h] = _bf(old[h][0])
            nst_ref[0, h] = old[h][1]
            mst_ref[0, h] = old[h][2]
            cs[h], ns[h], ms[h] = new[h]
        cell_ref[...] = jnp.concatenate(cells, axis=1)
        y_ref[0] = jnp.concatenate(ys, axis=1)

    row = pl.BlockSpec((lc, d), lambda c: (c, 0))
    gcol = pl.BlockSpec((lc, ng), lambda c: (c, 0))
    grow = pl.BlockSpec((ng, lc), lambda c: (0, c))
    return _pcall_ride(
        body, ride, name="mlstm_fwd", grid=(nc,),
        in_specs=[pl.BlockSpec((3, lc, d), lambda c: (0, c, 0)), gcol, grow, gcol, grow,
                  pl.BlockSpec((lc, d), lambda c: (c, 3)), pl.BlockSpec((lc, d), lambda c: (c, 4)),
                  pl.BlockSpec((1, d), lambda c: (0, 0)), pl.BlockSpec(memory_space=pl.ANY)],
        out_specs=[row, pl.BlockSpec((1, lc, d), lambda c: (1, c, 0)),
                   pl.BlockSpec((1, heads, hd, hd), lambda c: (c, 0, 0, 0)),
                   pl.BlockSpec((1, heads, 1, hd), lambda c: (c, 0, 0, 0)),
                   pl.BlockSpec((1, heads, 1, 128), lambda c: (c, 0, 0, 0))],
        out_shape=[jax.ShapeDtypeStruct((s_len, d), F32), jax.ShapeDtypeStruct(ycat.shape, BF16),
                   jax.ShapeDtypeStruct((nc, heads, hd, hd), BF16),
                   jax.ShapeDtypeStruct((nc, heads, 1, hd), F32),
                   jax.ShapeDtypeStruct((nc, heads, 1, 128), F32)],
        scratch_shapes=[pltpu.VMEM((heads, hd, hd), F32), pltpu.VMEM((heads, 1, hd), F32),
                        pltpu.VMEM((heads, 1, 128), F32)],
        input_output_aliases={8: 1},
        compiler_params=_seq(),
        args=(qkv, *gates, u, u, ml_g, ycat))


def _out_proj(ycat, w_out_b, x, gate, ride=None):
    s_len, d = x.shape
    tm = _tile(s_len, ROWS_MATMUL)

    def body(a_ref, w_ref, x_ref, g_ref, y_ref, xn_ref):
        y = _dot(a_ref[0], w_ref[0:d, :]) + _dot(a_ref[1], w_ref[d:2 * d, :])
        y_ref[...] = y
        xn_ref[...] = x_ref[...] + g_ref[...] * y

    row = pl.BlockSpec((tm, d), lambda i: (i, 0))
    return _pcall_ride(
        body, ride, name="out_proj", grid=(s_len // tm,),
        in_specs=[pl.BlockSpec((2, tm, d), lambda i: (0, i, 0)), pl.BlockSpec((2 * d, d), lambda i: (0, 0)), row,
                  pl.BlockSpec((1, d), lambda i: (0, 0))],
        out_specs=[row, row],
        out_shape=[jax.ShapeDtypeStruct((s_len, d), F32)] * 2,
        compiler_params=_seq(),
        args=(ycat, w_out_b, x, gate))


def _out_proj_loss(ycat, w_out_b, x, gate, g, target):
    s_len, d = x.shape
    tm = _tile(s_len, ROWS_IN_BWD)

    def body(a_ref, w_ref, x_ref, gate_ref, g_ref, t_ref, y_ref, dx_ref, dg_ref, loss_ref):
        @pl.when(pl.program_id(0) == 0)
        def _():
            dg_ref[...] = jnp.zeros_like(dg_ref)
            loss_ref[...] = jnp.zeros_like(loss_ref)

        y = _dot(a_ref[0], w_ref[0:d, :]) + _dot(a_ref[1], w_ref[d:2 * d, :])
        y_ref[...] = y
        xv = x_ref[...] + gate_ref[...] * y
        r = lax.rsqrt(jnp.mean(xv * xv, axis=-1, keepdims=True) + EPS)
        xn = xv * r
        err = xn * g_ref[...] - t_ref[...]
        loss_ref[...] += 0.5 * jnp.sum(jnp.mean(err * err, axis=-1, keepdims=True))
        dout = err * (1.0 / d)
        dg_ref[...] += _colsum(dout * xn)
        dxn = dout * g_ref[...]
        dx_ref[...] = r * (dxn - xn * jnp.mean(dxn * xn, axis=-1, keepdims=True))

    row = pl.BlockSpec((tm, d), lambda i: (i, 0))
    vec = pl.BlockSpec((1, d), lambda i: (0, 0))
    return _pcall(
        body, name="out_proj_loss", grid=(s_len // tm,),
        in_specs=[pl.BlockSpec((2, tm, d), lambda i: (0, i, 0)), pl.BlockSpec((2 * d, d), lambda i: (0, 0)), row, vec,
                  vec, row],
        out_specs=[row, row, vec, pl.BlockSpec((1, 128), lambda i: (0, 0))],
        out_shape=[jax.ShapeDtypeStruct((s_len, d), F32), jax.ShapeDtypeStruct((s_len, d), F32),
                   jax.ShapeDtypeStruct((1, d), F32), jax.ShapeDtypeStruct((1, 128), F32)],
        compiler_params=_seq(),
    )(ycat, w_out_b, x, gate, g, target)


def _out_bwd(dxn, y, gate, w_out_b):
    s_len, d = dxn.shape
    tm = _tile(s_len, ROWS_MATMUL)

    def body(dx_ref, y_ref, g_ref, w_ref, dg_ref, dy_ref, dc_ref):
        @pl.when(pl.program_id(0) == 0)
        def _():
            dg_ref[...] = jnp.zeros_like(dg_ref)

        dx = dx_ref[...]
        dg_ref[...] += _colsum(dx * y_ref[...])
        dy = _bf(g_ref[...] * dx)
        dy_ref[...] = dy
        dc_ref[0] = _dot_nt(dy, w_ref[0:d, :])
        dc_ref[1] = _dot_nt(dy, w_ref[d:2 * d, :])

    row = pl.BlockSpec((tm, d), lambda i: (i, 0))
    vec = pl.BlockSpec((1, d), lambda i: (0, 0))
    return _pcall(
        body, name="out_bwd", grid=(s_len // tm,),
        in_specs=[row, row, vec, pl.BlockSpec((2 * d, d), lambda i: (0, 0))],
        out_specs=[vec, row, pl.BlockSpec((2, tm, d), lambda i: (0, i, 0))],
        out_shape=[jax.ShapeDtypeStruct((1, d), F32), jax.ShapeDtypeStruct((s_len, d), BF16),
                   jax.ShapeDtypeStruct((2, s_len, d), F32)],
        compiler_params=_seq(),
    )(dxn, y, gate, w_out_b)


def _grad_matmul(a3, b3, nblk, a_idx, b_idx, out_shape, out_block, out_idx, ride=None):
    _, s_len, m = a3.shape
    n = b3.shape[2]
    tk = _tile(s_len, ROWS_GRAD_MATMUL)

    def body(a_ref, b_ref, o_ref):
        @pl.when(pl.program_id(1) == 0)
        def _():
            o_ref[...] = jnp.zeros_like(o_ref)

        o_ref[...] += _dot_tn(a_ref[0], b_ref[0])

    (out,), got = _pcall_ride(
        body, ride, name="grad_matmul", grid=(nblk, s_len // tk),
        in_specs=[pl.BlockSpec((1, tk, m), lambda p, t: (a_idx(p), t, 0)),
                  pl.BlockSpec((1, tk, n), lambda p, t: (b_idx(p), t, 0))],
        out_specs=[pl.BlockSpec((None,) + out_block, lambda p, t: (0,) + out_idx(p))],
        out_shape=[jax.ShapeDtypeStruct((1,) + out_shape, F32)],
        compiler_params=_seq(2), args=(a3, b3))
    return out, got


DU_PLANE = (2, 3, 4, 0, 1)


def _mlstm_bwd(qkv, gates, cst, nst, mst, cell, u, ml_g, d_ycat, wif_b, ride=None):
    _, s_len, d = qkv.shape
    ng = gates[0].shape[1]
    heads = ng // 2
    hd = d // heads
    lc = ML_CHUNK
    nc = s_len // lc
    kscale = hd ** -0.5

    def body(qkv_ref, gt_ref, gtt_ref, bc_ref, bct_ref, cst_ref, nst_ref, mst_ref, cell_ref, o_ref, z_ref, g_ref, dy_ref,
             wif_ref, dqkv_ref, dgt_ref, dbif_ref, du_ref, dg_ref, dcs, dns):
        @pl.when(pl.program_id(0) == 0)
        def _():
            dbif_ref[...] = jnp.zeros_like(dbif_ref)
            dcs[...] = jnp.zeros_like(dcs)
            dns[...] = jnp.zeros_like(dns)
            dg_ref[...] = jnp.zeros_like(dg_ref)

        causal, tril, triu = _tri_masks(lc)
        tril_strict = (tril.astype(F32) - (tril * triu).astype(F32)).astype(BF16)
        gtv, gttv, bcv, bctv = gt_ref[...], gtt_ref[...], bc_ref[...], bct_ref[...]
        lane = lax.broadcasted_iota(jnp.int32, (lc, ng), 1)
        dli_all = jnp.zeros((lc, ng), F32)
        from_later = jnp.zeros((lc, ng), F32)
        from_earlier = jnp.zeros((lc, ng), F32)
        across_all = jnp.zeros((1, ng), F32)
        old = [(dcs[h], dns[h]) for h in range(heads)]
        new, d_o, d_z, d_g, dqs, dks, dvs = [], [], [], [], [], [], []
        for h in range(heads):
            hs = slice(h * hd, (h + 1) * hd)
            li_c, li_r, gf_c, b_c, b_r = _chunk_gates(gtv, gttv, bcv, bctv, h, heads)
            m_prev = mst_ref[0, h][:, 0:1]
            m_t, w_intra, w_inter, _, w_state, decay = _chunk_weights(li_c, li_r, b_c, b_r, m_prev, causal)
            qb = qkv_ref[0, :, hs]
            qf = qb.astype(F32)
            ks = qkv_ref[1, :, hs].astype(F32) * kscale
            kb = _bf(ks)
            vb = qkv_ref[2, :, hs]
            c_b = cst_ref[0, h]
            n_old = nst_ref[0, h]
            s = _dot_nt(qb, kb) * w_intra
            den = _rowsum(s) + w_inter * _rowsum(qf * n_old)
            floor = jnp.exp(-m_t)
            dstab = jnp.maximum(jnp.abs(den), floor)
            cell = cell_ref[:, hs]
            o = o_ref[:, hs]
            so = _sigmoid(o)
            hm = so * cell
            rinv = lax.rsqrt(jnp.mean(hm * hm, axis=-1, keepdims=True) + EPS)
            hn = hm * rinv
            z = z_ref[:, hs]
            sgz = _sigmoid(z)
            sz = z * sgz
            gh = g_ref[:, hs]
            dy = dy_ref[0, :, hs]
            d_z.append(_bf(dy * (hn * gh) * _dsilu(z, sgz)))
            d_g.append(_colsum(dy * hn * sz))
            dhn = dy * gh * sz
            dhm = rinv * (dhn - hn * jnp.mean(dhn * hn, axis=-1, keepdims=True))
            d_o.append(_bf(dhm * cell * so * (1.0 - so)))
            dcell = dhm * so
            dnum = dcell / dstab
            dnb = _bf(dnum)
            dden = -_rowsum(dcell * cell) / dstab * jnp.where(jnp.abs(den) > floor, jnp.where(den > 0.0, 1.0, -1.0), 0.0)
            dst = _dot_nt(dnb, vb) + dden
            dsdb = _bf(dst * w_intra)
            dc_out, dn_out = old[h]
            dcb = _bf(dc_out)
            dq_inter = w_inter * (_dot_nt(dnb, c_b) + dden * n_old)
            dk_inter = w_state * (_dot_nt(vb, dcb) + dn_out)
            dq = _dot(dsdb, kb) + dq_inter
            dk = _dot_tn(dsdb, qb) + dk_inter
            dv = _dot_tn(_bf(s), dnb) + _dot(_bf(ks * w_state), dcb)
            wq = w_inter * qf
            new.append((decay * dc_out + _dot_tn(_bf(wq), dnb), decay * dn_out + _colsum(wq * dden)))
            pmat = dst * s
            p_rows = _rowsum(pmat)
            p_cols = _rowsum(pmat.T)
            q_in = _rowsum(qf * dq_inter)
            k_in = _rowsum(ks * dk_inter)
            across = decay * (jnp.sum(dc_out * c_b.astype(F32), keepdims=True) + jnp.sum(dn_out * n_old, keepdims=True))
            dli_all = dli_all + jnp.where(lane == h, p_cols + k_in, 0.0)
            from_later = from_later + jnp.where(lane == heads + h, p_rows - p_cols + q_in, 0.0)
            from_earlier = from_earlier + jnp.where(lane == heads + h, k_in, 0.0)
            across_all = across_all + jnp.where(lane[0:1] == heads + h, across, 0.0)
            dqs.append(dq)
            dks.append(dk * kscale)
            dvs.append(dv)
        for h in range(heads):
            dcs[h], dns[h] = new[h]
        du_ref[0] = jnp.concatenate(d_o, axis=1)
        du_ref[1] = jnp.concatenate(d_z, axis=1)
        dg_ref[...] += jnp.concatenate(d_g, axis=1)
        dlf = _tri_dot_left(triu, from_later) + _tri_dot_left(tril_strict, from_earlier) + across_all
        dgt = dli_all + dlf * _sigmoid(-gtv)
        dgt_ref[...] = dgt
        dbif_ref[...] += _colsum(dgt)
        dgb = _bf(dgt)
        dqkv_ref[0] = _bf(jnp.concatenate(dqs, axis=1) + _dot_nt(dgb, wif_ref[0:d, :]))
        dqkv_ref[1] = _bf(jnp.concatenate(dks, axis=1) + _dot_nt(dgb, wif_ref[d:2 * d, :]))
        dqkv_ref[2] = _bf(jnp.concatenate(dvs, axis=1) + _dot_nt(dgb, wif_ref[2 * d:3 * d, :]))

    rev = lambda c: nc - 1 - c
    row = pl.BlockSpec((lc, d), lambda c: (rev(c), 0))
    gcol = pl.BlockSpec((lc, ng), lambda c: (rev(c), 0))
    grow = pl.BlockSpec((ng, lc), lambda c: (0, rev(c)))
    return _pcall_ride(
        body, ride, name="mlstm_bwd", grid=(nc,),
        in_specs=[pl.BlockSpec((3, lc, d), lambda c: (0, rev(c), 0)), gcol, grow, gcol, grow,
                  pl.BlockSpec((1, heads, hd, hd), lambda c: (rev(c), 0, 0, 0)),
                  pl.BlockSpec((1, heads, 1, hd), lambda c: (rev(c), 0, 0, 0)),
                  pl.BlockSpec((1, heads, 1, 128), lambda c: (rev(c), 0, 0, 0)),
                  row, pl.BlockSpec((lc, d), lambda c: (rev(c), 3)), pl.BlockSpec((lc, d), lambda c: (rev(c), 4)),
                  pl.BlockSpec((1, d), lambda c: (0, 0)), pl.BlockSpec((1, lc, d), lambda c: (1, rev(c), 0)),
                  pl.BlockSpec((3 * d, ng), lambda c: (0, 0))],
        out_specs=[pl.BlockSpec((3, lc, d), lambda c: (0, rev(c), 0)), pl.BlockSpec((lc, ng), lambda c: (rev(c), 0)),
                   pl.BlockSpec((1, ng), lambda c: (0, 0)), pl.BlockSpec((2, lc, d), lambda c: (0, rev(c), 0)),
                   pl.BlockSpec((1, d), lambda c: (0, 0))],
        out_shape=[jax.ShapeDtypeStruct((3, s_len, d), BF16), jax.ShapeDtypeStruct((s_len, ng), F32),
                   jax.ShapeDtypeStruct((1, ng), F32), jax.ShapeDtypeStruct((5, s_len, d), BF16),
                   jax.ShapeDtypeStruct((1, d), F32)],
        scratch_shapes=[pltpu.VMEM((heads, hd, hd), F32), pltpu.VMEM((heads, 1, hd), F32)],
        compiler_params=_seq(),
        args=(qkv, *gates, cst, nst, mst, cell, u, u, ml_g, d_ycat, wif_b))


def _conv_bwd_tile(dp, later, taps, cw_ref, gw_ref, gb_ref):
    tm = dp.shape[0]
    dwin = jnp.concatenate([dp, later[...]], axis=0)
    later[...] = dp[0:HALO]
    acc = cw_ref[CONV_WIDTH - 1:CONV_WIDTH, :] * dp
    for k in range(CONV_WIDTH):
        if k < CONV_WIDTH - 1:
            acc = acc + cw_ref[k:k + 1, :] * _shift_up(dwin, CONV_WIDTH - 1 - k)[0:tm]
        gw_ref[k:k + 1, :] += _colsum(dp * taps[k])
    gb_ref[...] += _colsum(dp)
    return acc


def _ml_pre_bwd(dqkv, u, conv_w, conv_b, wqkv_b, du):
    s_len = u.shape[0]
    d = conv_w.shape[1]
    _, heads, hd, _ = wqkv_b.shape
    tm = _tile(s_len, ROWS_VECTOR)
    per = tm // HALO
    nt = s_len // tm

    def body(dqkv_ref, x_ref, xp_ref, cw_ref, cb_ref, w_ref, _, dx_ref, gw_ref, gcw_ref, gcb_ref, later, dps, dxs):
        i = pl.program_id(0)

        @pl.when(i == 0)
        def _():
            gw_ref[...] = jnp.zeros_like(gw_ref)
            gcw_ref[...] = jnp.zeros_like(gcw_ref)
            gcb_ref[...] = jnp.zeros_like(gcb_ref)
            later[...] = jnp.zeros_like(later)

        prev = jnp.where(i == nt - 1, 0.0, xp_ref[...])
        xm = x_ref[...]
        taps = _conv_taps(jnp.concatenate([prev, xm], axis=0))
        pre = _conv_fwd(taps, cw_ref, cb_ref)
        sg = _sigmoid(pre)
        xcb = _bf(pre * sg)
        xmb = _bf(xm)
        for h in range(heads):
            hs = slice(h * hd, (h + 1) * hd)
            dqh, dkh, dvh = dqkv_ref[0, :, hs], dqkv_ref[1, :, hs], dqkv_ref[2, :, hs]
            dxc = _dot_nt(dqh, w_ref[0, h]) + _dot_nt(dkh, w_ref[1, h])
            dps[:, hs] = dxc * _dsilu(pre[:, hs], sg[:, hs])
            dxs[:, hs] = _dot_nt(dvh, w_ref[2, h])
            gw_ref[0, h] += _dot_tn(xcb[:, hs], dqh)
            gw_ref[1, h] += _dot_tn(xcb[:, hs], dkh)
            gw_ref[2, h] += _dot_tn(xmb[:, hs], dvh)
        dx_ref[0] = _bf(_conv_bwd_tile(dps[...], later, taps, cw_ref, gcw_ref, gcb_ref) + dxs[...])

    rev = lambda i: nt - 1 - i
    vec = pl.BlockSpec((1, d), lambda i: (0, 0))
    cwb = pl.BlockSpec((CONV_WIDTH, d), lambda i: (0, 0))
    whole4 = pl.BlockSpec(wqkv_b.shape, lambda i: (0, 0, 0, 0))
    return _pcall(
        body, name="ml_pre_bwd", grid=(nt,),
        in_specs=[pl.BlockSpec((3, tm, d), lambda i: (0, rev(i), 0)), pl.BlockSpec((tm, d), lambda i: (rev(i), 2)),
                  pl.BlockSpec((HALO, d), lambda i: (jnp.maximum(rev(i) * per - 1, 0), 2)),
                  cwb, vec, whole4, pl.BlockSpec(memory_space=pl.ANY)],
        out_specs=[pl.BlockSpec((1, tm, d), lambda i: (DU_PLANE[2], rev(i), 0)), whole4, cwb, vec],
        out_shape=[jax.ShapeDtypeStruct(du.shape, BF16), jax.ShapeDtypeStruct(wqkv_b.shape, F32),
                   jax.ShapeDtypeStruct((CONV_WIDTH, d), F32), jax.ShapeDtypeStruct((1, d), F32)],
        scratch_shapes=[pltpu.VMEM((HALO, d), F32), pltpu.VMEM((tm, d), F32), pltpu.VMEM((tm, d), F32)],
        input_output_aliases={6: 0},
        compiler_params=_seq(),
    )(dqkv, u, u, conv_w, conv_b, wqkv_b, du)


def _rg_bwd(d_ycat, u, hh, conv_w, conv_b, wa_b, ba, wx_b, bx, lam, du):
    s_len = u.shape[0]
    d = conv_w.shape[1]
    heads, hd, _ = wa_b.shape
    tm = _tile(s_len, ROWS_VECTOR)
    per = tm // HALO
    nt = s_len // tm

    def body(dy_ref, x_ref, xp_ref, z_ref, hh_ref, hp_ref, cw_ref, cb_ref, wa_ref, ba_ref, wx_ref, bx_ref, lam_ref, _,
             du_ref, gwa_ref, gwx_ref, gba_ref, gbx_ref, glam_ref, gcw_ref, gcb_ref, carry, gbuf, later, dxcs):
        i = pl.program_id(0)
        first = i == nt - 1

        @pl.when(i == 0)
        def _():
            carry[...] = jnp.zeros_like(carry)
            later[...] = jnp.zeros_like(later)
            gwa_ref[...] = jnp.zeros_like(gwa_ref)
            gwx_ref[...] = jnp.zeros_like(gwx_ref)
            gba_ref[...] = jnp.zeros_like(gba_ref)
            gbx_ref[...] = jnp.zeros_like(gbx_ref)
            glam_ref[...] = jnp.zeros_like(glam_ref)
            gcw_ref[...] = jnp.zeros_like(gcw_ref)
            gcb_ref[...] = jnp.zeros_like(gcb_ref)

        prev = jnp.where(first, 0.0, xp_ref[...])
        taps = _conv_taps(jnp.concatenate([prev, x_ref[...]], axis=0))
        xc = _conv_fwd(taps, cw_ref, cb_ref)
        r, ig, sp, log_a, a, mult = _rg_gates(xc, wa_ref, ba_ref, wx_ref, bx_ref, lam_ref)
        z = z_ref[...]
        sgz = _sigmoid(z)
        dy = dy_ref[0]
        hh_v = hh_ref[...]
        du_ref[1] = _bf(dy * hh_v * _dsilu(z, sgz))
        dhh = dy * (z * sgz)
        rows = lax.broadcasted_iota(jnp.int32, a.shape, 0)
        coef = jnp.where(rows == tm - 1, carry[1:2, :], _shift_up(a, 1))
        ca, cu = _scan_groups(coef, dhh, reverse=True)
        c = carry[0:1, :]
        for j in range(tm // 8 - 1, -1, -1):
            blk = ca[j * 8:(j + 1) * 8] * c + cu[j * 8:(j + 1) * 8]
            gbuf[j * 8:(j + 1) * 8, :] = blk
            c = blk[0:1]
        carry[0:1, :] = c
        carry[1:2, :] = a[0:1]
        g = gbuf[...]
        hprev_tile = jnp.where(first, 0.0, hp_ref[...])
        hprev = _shift_down(jnp.concatenate([hprev_tile, hh_v], axis=0), 1)[HALO:]
        da = g * hprev
        gx_ = g * xc
        d_mult = gx_ * ig
        d_ig = gx_ * mult
        dxc = g * mult * ig
        dlog_a = da * a - d_mult * (a * a / mult)
        d_r = dlog_a * ((-RG_C) * sp)
        glam_ref[...] += _colsum(dlog_a * ((-RG_C) * r)) * (-_sigmoid(-lam_ref[...]))
        d_ga = d_r * r * (1.0 - r)
        d_gx = d_ig * ig * (1.0 - ig)
        gba_ref[...] += _colsum(d_ga)
        gbx_ref[...] += _colsum(d_gx)
        xb = _bf(xc)
        dgab = _bf(d_ga)
        dgxb = _bf(d_gx)
        for h in range(heads):
            hs = slice(h * hd, (h + 1) * hd)
            dxcs[:, hs] = dxc[:, hs] + _dot_nt(dgab[:, hs], wa_ref[h]) + _dot_nt(dgxb[:, hs], wx_ref[h])
            gwa_ref[h] += _dot_tn(xb[:, hs], dgab[:, hs])
            gwx_ref[h] += _dot_tn(xb[:, hs], dgxb[:, hs])
        du_ref[0] = _bf(_conv_bwd_tile(dxcs[...], later, taps, cw_ref, gcw_ref, gcb_ref))

    assert DU_PLANE[0] % 2 == 0 and DU_PLANE[1] == DU_PLANE[0] + 1
    rev = lambda i: nt - 1 - i
    row = pl.BlockSpec((tm, d), lambda i: (rev(i), 0))
    halo_prev = lambda col: pl.BlockSpec((HALO, d), lambda i: (jnp.maximum(rev(i) * per - 1, 0), col))
    vec = pl.BlockSpec((1, d), lambda i: (0, 0))
    cwb = pl.BlockSpec((CONV_WIDTH, d), lambda i: (0, 0))
    whole3 = lambda a: pl.BlockSpec(a.shape, lambda i: (0, 0, 0))
    return _pcall(
        body, name="rg_bwd", grid=(nt,),
        in_specs=[pl.BlockSpec((1, tm, d), lambda i: (0, rev(i), 0)), row, halo_prev(0),
                  pl.BlockSpec((tm, d), lambda i: (rev(i), 1)), row, halo_prev(0),
                  cwb, vec, whole3(wa_b), vec, whole3(wx_b), vec, vec, pl.BlockSpec(memory_space=pl.ANY)],
        out_specs=[pl.BlockSpec((2, tm, d), lambda i: (DU_PLANE[0] // 2, rev(i), 0)), whole3(wa_b), whole3(wa_b),
                   vec, vec, vec, cwb, vec],
        out_shape=[jax.ShapeDtypeStruct(du.shape, BF16), jax.ShapeDtypeStruct(wa_b.shape, F32),
                   jax.ShapeDtypeStruct(wa_b.shape, F32)] + [jax.ShapeDtypeStruct((1, d), F32)] * 3
        + [jax.ShapeDtypeStruct((CONV_WIDTH, d), F32), jax.ShapeDtypeStruct((1, d), F32)],
        scratch_shapes=[pltpu.VMEM((8, d), F32), pltpu.VMEM((tm, d), F32), pltpu.VMEM((HALO, d), F32),
                        pltpu.VMEM((tm, d), F32)],
        input_output_aliases={13: 0},
        compiler_params=_seq(),
    )(d_ycat, u, u, u, hh, hh, conv_w, conv_b, wa_b, ba, wx_b, bx, lam, du)


def _in_bwd(du, w4, x, dxn, g, scale, ride=None):
    s_len, d = x.shape
    tm = _tile(s_len, ROWS_IN_BWD)
    nsh_chips, _, nsh = w4.shape
    npc = du.shape[0]
    ck = d // 4
    assert nsh % ck == 0 and npc * d == nsh_chips * nsh

    def body(du_ref, w_ref, x_ref, dxn_ref, g_ref, sc_ref, dx_ref, dsh_ref, dsc_ref, dg_ref):
        @pl.when(pl.program_id(0) == 0)
        def _():
            dsh_ref[...] = jnp.zeros_like(dsh_ref)
            dsc_ref[...] = jnp.zeros_like(dsc_ref)
            dg_ref[...] = jnp.zeros_like(dg_ref)

        dh = None
        for q in range(npc * d // ck):
            col = q * ck
            p, pc = col // d, col % d
            s, sc = col // nsh, col % nsh
            t = _dot_nt(du_ref[DU_PLANE[p], :, pc:pc + ck], w_ref[s, :, sc:sc + ck])
            dh = t if dh is None else dh + t
        xv = x_ref[...]
        r = lax.rsqrt(jnp.mean(xv * xv, axis=-1, keepdims=True) + EPS)
        xn = xv * r
        gv = g_ref[...]
        onesc = 1.0 + sc_ref[...]
        dsh_ref[...] += _colsum(dh)
        dsc_ref[...] += _colsum(dh * (xn * gv))
        dg_ref[...] += _colsum(dh * xn * onesc)
        dxh = dh * (gv * onesc)
        dx_ref[...] = dxn_ref[...] + r * (dxh - xn * jnp.mean(dxh * xn, axis=-1, keepdims=True))

    row = pl.BlockSpec((tm, d), lambda i: (i, 0))
    vec = pl.BlockSpec((1, d), lambda i: (0, 0))
    return _pcall_ride(
        body, ride, name="in_bwd", grid=(s_len // tm,),
        in_specs=[pl.BlockSpec((npc, tm, d), lambda i: (0, i, 0)), pl.BlockSpec(w4.shape, lambda i: (0, 0, 0)), row, row,
                  vec, vec],
        out_specs=[row, vec, vec, vec],
        out_shape=[jax.ShapeDtypeStruct((s_len, d), F32)] + [jax.ShapeDtypeStruct((1, d), F32)] * 3,
        compiler_params=_seq(),
        args=(du, w4, x, dxn, g, scale))


def _layer_fwd(x, p, rides=None, loss_head=None):
    rides = rides or {}
    landed = {}
    ride = lambda kernel: rides[kernel](landed) if kernel in rides else None
    (h_b, u), landed["ln_inproj"] = _ln_inproj(x, p["norm_g"], p["scale"], p["shift"], p["w4"], ride("ln_inproj"))
    (hh, ycat), landed["rg_fwd"] = _rg_fwd(u, p["rg_conv_w"], p["rg_conv_b"], p["rg_wa_b"], p["rg_ba"], p["rg_wx_b"],
                                           p["rg_bx"], p["rg_lam"], ride("rg_fwd"))
    if "late" in rides:
        p = {**p, **rides["late"](landed)}
    qkv, *gates = _ml_pre(u, p["ml_conv_w"], p["ml_conv_b"], p["wqkv_b"], p["wif_b"], p["wift_b"], p["b_if"],
                          p["b_ift"])
    (cell, ycat, cst, nst, mst), landed["mlstm_fwd"] = _mlstm_fwd(qkv, gates, u, p["ml_g"], ycat, ride("mlstm_fwd"))
    if loss_head is None:
        (y, x_new), landed["out_proj"] = _out_proj(ycat, p["w_out_b"], x, p["gate"], ride("out_proj"))
    else:
        y, *x_new = _out_proj_loss(ycat, p["w_out_b"], x, p["gate"], *loss_head)
    saved = dict(x=x, h_b=h_b, u=u, hh=hh, qkv=qkv, gates=gates, cell=cell, ycat=ycat, cst=cst, nst=nst, mst=mst, y=y)
    return x_new, saved, p, landed


def _layer_bwd(dxn, p, s, rides=None):
    rides = rides or {}
    landed = {}
    ride = lambda kernel: rides[kernel](grads, landed) if kernel in rides else None
    u = s["u"]
    d = dxn.shape[1]
    d_gate, dy_b, d_ycat = _out_bwd(dxn, s["y"], p["gate"], p["w_out_b"])
    grads = dict(w_out=_grad_matmul(s["ycat"], dy_b[None], 2, lambda b: b, lambda b: 0, (2 * d, d), (d, d),
                                    lambda b: (b, 0))[0])
    (dqkv, dgt, g_b_if, du, g_ml_g), landed["mlstm_bwd"] = _mlstm_bwd(
        s["qkv"], s["gates"], s["cst"], s["nst"], s["mst"], s["cell"], u, p["ml_g"], d_ycat, p["wif_b"],
        ride("mlstm_bwd"))
    ng = dgt.shape[1]
    g_w_if = _grad_matmul(s["qkv"], _bf(dgt)[None], 3, lambda b: b, lambda b: 0, (3 * d, ng), (d, ng),
                          lambda b: (b, 0))[0][0]
    du, g_wqkv, g_ml_cw, g_ml_cb = _ml_pre_bwd(dqkv, u, p["ml_conv_w"], p["ml_conv_b"], p["wqkv_b"], du)
    du, g_wa, g_wx, g_ba, g_bx, g_lam, g_rg_cw, g_rg_cb = _rg_bwd(d_ycat, u, s["hh"], p["rg_conv_w"], p["rg_conv_b"],
                                                                  p["rg_wa_b"], p["rg_ba"], p["rg_wx_b"], p["rg_bx"],
                                                                  p["rg_lam"], du)
    grads.update(rg_conv_w=g_rg_cw, rg_conv_b=g_rg_cb, rg_w_a=g_wa, rg_b_a=g_ba, rg_w_x=g_wx, rg_b_x=g_bx,
                 rg_lambda=g_lam, ml_conv_w=g_ml_cw, ml_conv_b=g_ml_cb, ml_w_qkv=g_wqkv, ml_w_if=g_w_if, ml_b_if=g_b_if,
                 ml_norm_g=g_ml_g)
    npc = du.shape[0]
    grads["w_in"], landed["grad_w_in"] = _grad_matmul(
        s["h_b"][None], du, npc, lambda b: 0, lambda b: (b + DU_PLANE[0]) % npc, (d, npc * d), (d, d),
        lambda b: (0, b), ride("grad_w_in"))
    (dx, d_shift, d_scale, grads["norm_g"]), landed["in_bwd"] = _in_bwd(du, p["w4"], s["x"], dxn, p["norm_g"],
                                                                        p["scale"], ride("in_bwd"))
    return dx, grads, jnp.concatenate([d_shift, d_scale, d_gate], axis=1), landed


def _me():
    return lax.axis_index("x"), lax.axis_index("y"), lax.axis_index("c")


def _remote(src, dst, send_sem, recv_sem, to):
    return pltpu.make_async_remote_copy(src_ref=src, dst_ref=dst, send_sem=send_sem, recv_sem=recv_sem,
                                        device_id=to, device_id_type=MESH)


def _all_gather8(blocks, space):
    n = len(blocks)

    def body(*refs):
        x_refs, out_refs = refs[:n], refs[n:2 * n]
        send_sems, recv_sems, local_sems = refs[2 * n:]
        x, y, c = _me()
        me, sibling = (x, y, c), (x, y, 1 - c)
        chips = [(1 - x, y), (x, 1 - y), (1 - x, 1 - y)]

        def rows(i, px, py, pc):
            m_per = blocks[i].shape[0]
            return out_refs[i].at[pl.ds((4 * px + 2 * py + pc) * m_per, m_per), :]

        def copy(i, k, blk, to, src=None):
            return _remote(rows(i, *blk) if src is None else src, rows(i, *blk), send_sems.at[7 * i + k],
                           recv_sems.at[7 * i + k], to)

        mine = [pltpu.make_async_copy(x_refs[i], rows(i, *me), local_sems.at[i]) for i in range(n)]
        first = []
        for i in range(n):
            first.append(copy(i, 0, me, sibling, src=x_refs[i]))
            first += [copy(i, 1 + j, me, (*chip, c), src=x_refs[i]) for j, chip in enumerate(chips)]
        for cp in mine + first:
            cp.start()
        passed = []
        for j, chip in enumerate(chips):
            for i in range(n):
                copy(i, 1 + j, (*chip, c), me).wait_recv()
                passed.append(copy(i, 4 + j, (*chip, c), sibling))
                passed[-1].start()
        for i in range(n):
            copy(i, 0, sibling, me).wait_recv()
            for j, chip in enumerate(chips):
                copy(i, 4 + j, (*chip, 1 - c), me).wait_recv()
        for cp in first + passed:
            cp.wait_send()
        for cp in mine:
            cp.wait()

    spec = pl.BlockSpec(memory_space=space)
    return _pcall(
        body, name="all_gather8",
        out_shape=[jax.ShapeDtypeStruct((8 * b.shape[0], b.shape[1]), b.dtype) for b in blocks],
        in_specs=[spec] * n, out_specs=[spec] * n,
        scratch_shapes=[pltpu.SemaphoreType.DMA((7 * n,)), pltpu.SemaphoreType.DMA((7 * n,)),
                        pltpu.SemaphoreType.DMA((n,))],
    )(*blocks)


def _exchange(legs):
    n = len(legs)

    def body(*refs):
        copies, local = _exchange_body(legs, refs[:n], refs[n:2 * n], *refs[2 * n:])
        for cp in copies + local:
            cp.start()
        for cp in copies:
            cp.wait_recv()
        for cp in copies:
            cp.wait_send()
        for cp in local:
            cp.wait()

    hbm = pl.BlockSpec(memory_space=pltpu.HBM)
    return _pcall(body, name="exchange", out_shape=[leg.landing() for leg in legs], in_specs=[hbm] * n,
                  out_specs=[hbm] * n, input_output_aliases=_exchange_aliases(legs, 0, 0),
                  scratch_shapes=_exchange_sems(legs))(*[leg.src for leg in legs])


def _row_tile(rows, cap=4096, mult=16):
    best = None
    for t in range(mult, min(rows, cap) + 1, mult):
        if rows % t == 0:
            best = t
    return rows if best is None else best


def _pair_sum(half, own, own_spec, got, got_spec, out_shape, out_spec, grid):
    def body(_, a_ref, b_ref, o_ref):
        o_ref[...] = (a_ref[...] + b_ref[...].astype(F32)).astype(o_ref.dtype)

    return _pcall(
        body, name="pair_sum",
        grid_spec=pltpu.PrefetchScalarGridSpec(num_scalar_prefetch=1, grid=grid, in_specs=[own_spec, got_spec],
                                               out_specs=out_spec),
        out_shape=out_shape, compiler_params=_seq(len(grid)))(half, own, got)


def _chip_sum(ids, part, met, fill, layer=0, stack=1):
    _, _, rows, n = part.shape
    tr = _row_tile(rows, cap=max(16, BLOCK_ELEMS // n))
    first = isinstance(stack, int)

    def body(_, own_ref, a_ref, b_ref, c_ref, *rest):
        acc = own_ref[...].astype(F32) + a_ref[...].astype(F32)
        acc = acc + b_ref[...].astype(F32)
        rest[-1][...] = acc + c_ref[...].astype(F32)

    blk = (None, None, tr, n)
    other = lambda k: pl.BlockSpec(blk, lambda j, ids: ((ids[0] + k) % 4, 0, j, 0))
    in_specs = [pl.BlockSpec(blk, lambda j, ids: (ids[0], 0, j, 0)), other(1), other(2), other(3)]
    return _pcall(
        body, name="chip_sum",
        grid_spec=pltpu.PrefetchScalarGridSpec(
            num_scalar_prefetch=1, grid=(rows // tr,),
            in_specs=in_specs if first else in_specs + [pl.BlockSpec(memory_space=pl.ANY)],
            out_specs=pl.BlockSpec(blk, lambda j, ids: (layer, ids[1] if fill else 0, j, 0))),
        out_shape=jax.ShapeDtypeStruct(((stack,) if first else stack.shape[:1]) + (2 if fill else 1, rows, n), F32),
        input_output_aliases={} if first else {5: 0},
        compiler_params=_seq())(*((ids, part, met, met, met) if first else (ids, part, met, met, met, stack)))


def _ada_mod(c_all, w_ada, b_ada_cols):
    depth, d, n = w_ada.shape
    nb = c_all.shape[0]

    def body(c_ref, w_ref, b_ref, o_ref):
        cv = c_ref[...]
        ca = _bf(cv * _sigmoid(cv))
        o_ref[0] = _dot(ca, _bf(w_ref[0])) + b_ref[0]

    return _pcall(body, name="ada_mod", grid=(depth,),
                  in_specs=[pl.BlockSpec((nb, d), lambda l: (0, 0)), pl.BlockSpec((1, d, n), lambda l: (l, 0, 0)),
                            pl.BlockSpec((1, 1, n), lambda l: (l, 0, 0))],
                  out_specs=pl.BlockSpec((1, nb, n), lambda l: (l, 0, 0)),
                  out_shape=jax.ShapeDtypeStruct((depth, nb, n), F32), compiler_params=_seq())(c_all, w_ada, b_ada_cols)


def _ada_grad(c_all, dmod_cols, rows_all):
    nb, d = c_all.shape
    depth, _, n = dmod_cols.shape
    kinds, n_all = rows_all.shape[1], rows_all.shape[3]

    def body(c_ref, dm_ref, da_ref, gw_ref, gb_ref):
        cv = c_ref[...]
        ca = _bf(cv * _sigmoid(cv))
        gw_ref[0] = _dot_tn(ca, _bf(dm_ref[0]))
        for k in range(kinds):
            gb_ref[0, k] = _colsum(da_ref[0, k])

    return _pcall(body, name="ada_grad", grid=(depth,),
                  in_specs=[pl.BlockSpec((nb, d), lambda l: (0, 0)), pl.BlockSpec((1, nb, n), lambda l: (l, 0, 0)),
                            pl.BlockSpec((1, kinds, nb, n_all), lambda l: (l, 0, 0, 0))],
                  out_specs=[pl.BlockSpec((1, d, n), lambda l: (l, 0, 0)),
                             pl.BlockSpec((1, kinds, 1, n_all), lambda l: (l, 0, 0, 0))],
                  out_shape=[jax.ShapeDtypeStruct((depth, d, n), F32), jax.ShapeDtypeStruct((depth, kinds, 1, n_all), F32)],
                  compiler_params=_seq())(c_all, dmod_cols, rows_all)


def _adamw(items, ride=None):
    two_d = [tuple(t.reshape(w.size // w.shape[-1], w.shape[-1]) for t in (w, g, m, v)) for w, g, m, v in items]
    n = len(items)
    if n == 1:
        rows, cols = two_d[0][0].shape
        tr = _row_tile(rows, cap=max(8, BLOCK_ELEMS // cols), mult=8)
        blocks = [pl.BlockSpec((tr, cols), lambda i: (i, 0))]
        grid = (rows // tr,)
    else:
        blocks = [pl.BlockSpec(t[0].shape, lambda i: (0, 0)) for t in two_d]
        grid = (1,)

    def body(*refs):
        for k in range(n):
            w_ref, g_ref, m_ref, v_ref = refs[4 * k:4 * k + 4]
            d_ref, mo_ref, vo_ref = refs[4 * n + 3 * k:4 * n + 3 * k + 3]
            gv = g_ref[...]
            mn = ADAM_B1 * m_ref[...] + (1.0 - ADAM_B1) * gv
            vn = ADAM_B2 * v_ref[...] + (1.0 - ADAM_B2) * (gv * gv)
            m_hat = mn / (1.0 - ADAM_B1 ** ADAM_STEP)
            v_hat = vn / (1.0 - ADAM_B2 ** ADAM_STEP)
            d_ref[...] = -ADAM_LR * (m_hat / (jnp.sqrt(v_hat) + ADAM_EPS) + ADAM_WD * w_ref[...])
            mo_ref[...] = mn
            vo_ref[...] = vn

    outs, got = _pcall_ride(
        body, ride, name="adamw", grid=grid,
        in_specs=[b for b in blocks for _ in range(4)], out_specs=[b for b in blocks for _ in range(3)],
        out_shape=[jax.ShapeDtypeStruct(t[0].shape, F32) for t in two_d for _ in range(3)],
        compiler_params=_seq(), args=tuple(a for t in two_d for a in t))
    return [tuple(o.reshape(items[k][0].shape) for o in outs[3 * k:3 * k + 3]) for k in range(n)], got


WEIGHTS = ["norm_g", "w_ada", "b_ada", "w_in", "rg_conv_w", "rg_conv_b", "rg_w_a", "rg_b_a", "rg_w_x", "rg_b_x",
           "rg_lambda", "ml_conv_w", "ml_conv_b", "ml_w_q", "ml_w_k", "ml_w_v", "ml_w_if", "ml_b_if", "ml_norm_g",
           "w_out", "final_g"]
SMALL_SHARDED = {"ml_w_qkv": 2, "rg_conv_w": 1, "ml_conv_w": 1, "ml_w_if": 0}
REPLICATED = ["rg_w_a", "rg_w_x", "rg_conv_b", "rg_b_a", "rg_b_x", "rg_lambda", "ml_conv_b", "ml_norm_g", "ml_b_if"]
LANES = 128


def _to_pieces(g, axis):
    shp = g.shape
    g = g.reshape(shp[:axis] + (4, 2, shp[axis] // 8) + shp[axis + 1:])
    g = jnp.moveaxis(g, (axis, axis + 1), (0, 1))
    return g.reshape(4, 2, -1)


def _from_pieces(p, shard_shape, axis):
    k = p.shape[0]
    rest = shard_shape[:axis] + (shard_shape[axis] // k,) + shard_shape[axis + 1:]
    t = jnp.moveaxis(p.reshape((k,) + rest), 0, axis)
    return t.reshape(shard_shape)


def _pad_rows(flat, mult):
    n = flat.shape[-1]
    pad = (-n) % mult
    if pad:
        flat = jnp.concatenate([flat, jnp.zeros(flat.shape[:-1] + (pad,), flat.dtype)], axis=-1)
    return flat


def kernel(x, c, norm_g, w_ada, b_ada, w_in, rg_conv_w, rg_conv_b, rg_w_a, rg_b_a, rg_w_x, rg_b_x, rg_lambda, ml_conv_w, ml_conv_b, ml_w_q, ml_w_k, ml_w_v, ml_w_if, ml_b_if, ml_norm_g, w_out, final_g, loss_target, m_norm_g, m_w_ada, m_b_ada, m_w_in, m_rg_conv_w, m_rg_conv_b, m_rg_w_a, m_rg_b_a, m_rg_w_x, m_rg_b_x, m_rg_lambda, m_ml_conv_w, m_ml_conv_b, m_ml_w_q, m_ml_w_k, m_ml_w_v, m_ml_w_if, m_ml_b_if, m_ml_norm_g, m_w_out, m_final_g, v_norm_g, v_w_ada, v_b_ada, v_w_in, v_rg_conv_w, v_rg_conv_b, v_rg_w_a, v_rg_b_a, v_rg_w_x, v_rg_b_x, v_rg_lambda, v_ml_conv_w, v_ml_conv_b, v_ml_w_q, v_ml_w_k, v_ml_w_v, v_ml_w_if, v_ml_b_if, v_ml_norm_g, v_w_out, v_final_g):
    given = dict(locals())
    ax, ay, ac = lax.axis_index("x"), lax.axis_index("y"), lax.axis_index("c")
    chip = 2 * ax + ay
    me = 2 * chip + ac
    depth, d = norm_g.shape
    n_ada = w_ada.shape[2]
    pick = lambda a, i, axis=0: lax.dynamic_index_in_dim(a, i, axis, keepdims=False)

    convs = jnp.stack([rg_conv_w, ml_conv_w])
    n_conv = 2 * depth * CONV_WIDTH // 4
    blk = jnp.concatenate([c, convs.reshape(n_conv, d), jnp.zeros((8 - 1 - n_conv, d), F32)], axis=0)
    w_in_first = lax.dynamic_slice_in_dim(w_in[0], ac * (d // 2), d // 2, 0).astype(BF16)
    g0, w_in_first = _all_gather8([blk, w_in_first], pltpu.HBM)
    g0 = g0.reshape(8, 8, d)
    c_all = g0[:, 0, :]
    conv_full = g0[0::2, 1:1 + n_conv].reshape(4, 2, depth, CONV_WIDTH, d // 4)
    conv_full = conv_full.transpose(1, 2, 3, 0, 4).reshape(2, depth, CONV_WIDTH, d)

    b_cols = lax.dynamic_slice_in_dim(b_ada, chip * n_ada, n_ada, axis=1)[:, None, :]
    mod_part = _ada_mod(c_all, w_ada, b_cols)
    g1 = _all_gather8([mod_part.transpose(1, 0, 2).reshape(8, depth * n_ada)], pltpu.VMEM)[0]
    g1 = g1.reshape(8, 8, depth, n_ada)[0::2]
    mod_me = pick(g1.transpose(1, 2, 0, 3).reshape(8, depth, 4 * n_ada), me)

    def half_of(w, axis):
        n = w.shape[axis] // 2
        return lax.dynamic_slice_in_dim(w, ac * n, n, axis).astype(BF16)

    n_sh = w_in.shape[2]
    heads, hd_cut, hd = ml_w_q.shape[1:]

    def blocks_of(l):
        wqkv = jnp.stack([ml_w_q[l], ml_w_k[l], ml_w_v[l]])
        return [half_of(w_in[l], 0), half_of(w_out[l], 0), half_of(wqkv, 2).reshape(-1, hd), half_of(ml_w_if[l], 0)]

    def layer_of(l, w4, rest):
        return dict(
            norm_g=norm_g[l][None], shift=mod_me[l, 0:d][None], scale=mod_me[l, d:2 * d][None],
            gate=mod_me[l, 2 * d:3 * d][None], w4=w4.reshape(4, d, n_sh),
            rg_conv_w=conv_full[0, l], rg_conv_b=rg_conv_b[l][None], rg_wa_b=_bf(rg_w_a[l]), rg_ba=rg_b_a[l][None],
            rg_wx_b=_bf(rg_w_x[l]), rg_bx=rg_b_x[l][None], rg_lam=rg_lambda[l][None],
            ml_conv_w=conv_full[1, l], ml_conv_b=ml_conv_b[l][None], b_if=ml_b_if[l][None], b_ift=ml_b_if[l][:, None],
            ml_g=ml_norm_g[l][None], **rest)

    def rest_of(gathered):
        w_out_b, wqkv_g, wif = gathered
        return dict(w_out_b=w_out_b, wqkv_b=_from_pieces(wqkv_g.reshape(8, -1), (3, heads, hd, hd), 2), wif_b=wif,
                    wift_b=wif.T)

    spread = lambda blocks: [Leg(b, "spread") for b in blocks]
    fill = lambda landed: [Leg(t, "sib_fill") for t in landed]
    flat = lambda filled: [t.reshape(-1, t.shape[-1]) for t in filled]
    first = blocks_of(0)
    n_rest = len(first) - 1
    p = layer_of(0, w_in_first, {})
    layers, saved = [], []
    xl = x[0]
    for l in range(depth):
        nxt = blocks_of(l + 1) if l + 1 < depth else []
        skip = n_rest if l == 0 else 0
        rides = dict(rg_fwd=lambda landed, nxt=nxt: spread(nxt[:1]))
        if l == 0:
            rides.update(ln_inproj=lambda landed: spread(first[1:]),
                         rg_fwd=lambda landed, nxt=nxt: fill(landed["ln_inproj"]) + spread(nxt[:1]),
                         late=lambda landed: rest_of(flat(landed["rg_fwd"][:n_rest])))
        if nxt:
            rides.update(mlstm_fwd=lambda landed, nxt=nxt: spread(nxt[1:]),
                         out_proj=lambda landed, skip=skip: fill(list(landed["rg_fwd"][skip:]) + list(landed["mlstm_fwd"])))
        xl, s, p, landed = _layer_fwd(xl, p, rides, None if nxt else (final_g[None], loss_target[0]))
        layers.append(p)
        saved.append(s)
        if nxt:
            arrived = flat(landed["out_proj"])
            p = layer_of(l + 1, arrived[0], rest_of(arrived[1:]))
    dx, g_final, loss = xl

    half = ac.reshape(1)
    ids = jnp.stack([chip, ac])
    r_out = w_out.shape[1] // 2

    def pair_in(g_w_in, got_in):
        return _pair_sum(
            half, g_w_in, pl.BlockSpec((None, d // 2, n_sh), lambda s, h: (0, h[0], s)),
            got_in, pl.BlockSpec((None, None, d // 2, n_sh), lambda s, h: (0, s, 0, 0)),
            jax.ShapeDtypeStruct((4, 1, d // 2, n_sh), BF16),
            pl.BlockSpec((None, None, d // 2, n_sh), lambda s, h: (s, 0, 0, 0)), (4,))

    def pair_out(g_out5, got_out):
        return _pair_sum(
            half, g_out5, pl.BlockSpec((None, None, None, r_out, d), lambda s, h: (0, s, h[0], 0, 0)),
            got_out, pl.BlockSpec((None, None, r_out, d), lambda s, h: (0, s, 0, 0)),
            jax.ShapeDtypeStruct((4, 1, r_out, d), BF16),
            pl.BlockSpec((None, None, r_out, d), lambda s, h: (s, 0, 0, 0)), (4,))

    def pair_slab(slab, got, dtype):
        rows = got.shape[0] // 4
        blk = pl.BlockSpec((rows, LANES), lambda s, h: (s, 0))
        return _pair_sum(half, slab, pl.BlockSpec((None, rows, LANES), lambda s, h: (h[0], s, 0)), got, blk,
                         jax.ShapeDtypeStruct((4 * rows, LANES), dtype), blk, (4,)).reshape(4, 1, rows, LANES)

    row_pad = lambda n: -(-n // (8 * LANES)) * (8 * LANES)

    def as_rows(t):
        if t.shape[-1] == LANES and t.size % (8 * LANES) == 0:
            return t.reshape(-1, LANES)
        return _pad_rows(t.reshape(-1), 8 * LANES).reshape(-1, LANES)

    chips = lambda arrs: [Leg(a, "chips") for a in arrs]
    out5 = lambda g: g["w_out"].reshape(1, 4, 2, r_out, d)
    grads, dmods, parts, mets = [None] * depth, [None] * depth, [None] * depth, [None] * depth
    small = {}

    def early_exchange(g, landed):
        every = [g] + grads[1:]
        sm = jnp.concatenate([_to_pieces(every[l][name], axis) for l in range(depth)
                              for name, axis in SMALL_SHARDED.items()], axis=-1)
        sm = _pad_rows(sm, 16 * LANES)
        sm = sm.transpose(1, 0, 2).reshape(2, -1, LANES)
        rep = [as_rows(every[l][name]) for l in range(depth) for name in REPLICATED]
        rep = jnp.concatenate(rep + [as_rows(g_final), as_rows(loss)], axis=0)
        rep = jnp.concatenate([rep, jnp.zeros(((-rep.shape[0]) % 64, LANES), F32)], axis=0)
        rep = rep.reshape(4, 2, -1, LANES).transpose(1, 0, 2, 3).reshape(2, -1, LANES)
        got_sm, got_rep = _exchange([Leg(sm, "sib_slab"), Leg(rep, "sib_slab")])
        small["parts"] = [pair_out(out5(g), landed["mlstm_bwd"][0]), pair_slab(sm, got_sm, BF16),
                          pair_slab(rep, got_rep, F32)]
        return chips(small["parts"])

    def last_exchange(g, landed):
        (got_in,) = _exchange([Leg(g["w_in"], "sib_w_in")])
        small["part_in"] = pair_in(g["w_in"], got_in)
        return chips([small["part_in"]])

    for l in reversed(range(depth)):
        above = parts[l + 1] if l + 1 < depth else []
        rides = dict(mlstm_bwd=lambda g, landed, above=above: [Leg(out5(g), "sib_w_out")] + chips(above),
                     in_bwd=lambda g, landed: [Leg(g["w_in"], "sib_w_in")])
        if l == 0:
            rides.update(grad_w_in=early_exchange, in_bwd=last_exchange)
        dx, grads[l], dmods[l], got = _layer_bwd(dx, layers[l], saved[l], rides)
        if above:
            mets[l + 1] = got["mlstm_bwd"][1:]
        if l > 0:
            parts[l] = [pair_in(grads[l]["w_in"], got["in_bwd"][0]), pair_out(out5(grads[l]), got["mlstm_bwd"][0])]
    part_out, part_sm, part_rep = small["parts"]
    met_out, met_sm, met_rep = got["grad_w_in"]
    parts[0], mets[0] = [small["part_in"], part_out], [got["in_bwd"][0], met_out]
    n_rep = part_rep.shape[2]

    pad = lambda t: jnp.concatenate([t, jnp.zeros((1, 2 * d), F32)], axis=1)
    rows = [r for l in range(depth) for r in (dmods[l], pad(grads[l]["norm_g"]))]
    blk = jnp.concatenate(rows + [jnp.zeros((8 - 2 * depth, 3 * d), F32)], axis=0)
    rows_all = _all_gather8([blk], pltpu.VMEM)[0].reshape(8, 8, 3 * d)[:, :2 * depth]
    rows_all = rows_all.transpose(1, 0, 2).reshape(depth, 2, 8, 3 * d)
    dm_cols = lax.dynamic_slice_in_dim(rows_all[:, 0], chip * n_ada, n_ada, axis=2)
    g_w_ada, summed = _ada_grad(c_all, dm_cols, rows_all)

    g = dict(w_ada=g_w_ada, b_ada=summed[:, 0, 0], norm_g=summed[:, 1, 0, :d])
    item = lambda name: (given[name], g[name], given["m_" + name], given["v_" + name])
    both_in, both_out = depth, depth
    for l in range(depth):
        both_in = _chip_sum(ids, parts[l][0], mets[l][0], True, l, both_in)
        both_out = _chip_sum(ids, parts[l][1], mets[l][1], True, l, both_out)
    both_in, both_out, both_sm = _exchange(fill([both_in, both_out, _chip_sum(ids, part_sm, met_sm, True)]))
    red_rep = _chip_sum(ids, part_rep, met_rep, False).reshape(n_rep, LANES)
    rep_all = _all_gather8([red_rep], pltpu.VMEM)[0].reshape(-1)

    g.update(w_in=both_in.reshape(w_in.shape), w_out=both_out.reshape(w_out.shape))
    shard = both_sm.reshape(2, -1)
    off = 0
    per_layer = {name: [] for name in SMALL_SHARDED}
    for l in range(depth):
        for name, axis in SMALL_SHARDED.items():
            shp = (3,) + ml_w_q.shape[1:] if name == "ml_w_qkv" else given[name].shape[1:]
            n = grads[l][name].size // 8
            per_layer[name].append(_from_pieces(shard[:, off:off + n], shp, axis))
            off += n
    for name in SMALL_SHARDED:
        g[name] = jnp.stack(per_layer[name])
    for i, name in enumerate(["ml_w_q", "ml_w_k", "ml_w_v"]):
        g[name] = g["ml_w_qkv"][:, i]
    off = 0
    per_layer = {name: [] for name in REPLICATED}
    for l in range(depth):
        for name in REPLICATED:
            n = given[name][l].size
            per_layer[name].append(rep_all[off:off + n].reshape(given[name].shape[1:]))
            off += row_pad(n)
    for name in REPLICATED:
        g[name] = jnp.stack(per_layer[name])
    g["final_g"] = rep_all[off:off + d]
    loss_all = rep_all[off + row_pad(d)]

    stepped = {}
    rg_mats, ml_mats = ["rg_w_a", "rg_w_x"], ["ml_w_q", "ml_w_k", "ml_w_v"]
    vectors = [n for n in WEIGHTS if n not in ["w_ada", "w_in", "w_out"] + rg_mats + ml_mats]
    for names in (["w_ada"], ["w_in"], ["w_out"], rg_mats, ml_mats, vectors):
        stepped.update(zip(names, _adamw([item(name) for name in names])[0]))
    deltas, new_m, new_v = zip(*[stepped[name] for name in WEIGHTS])
    return (loss_all, dx[None], *[g[name] for name in WEIGHTS], *deltas, *new_m, *new_v)
```

```python
import functools
from typing import NamedTuple

import jax
import jax.numpy as jnp
from jax import lax
from jax.experimental import pallas as pl
from jax.experimental.pallas import tpu as pltpu

F32 = jnp.float32
BF16 = jnp.bfloat16

EPS = 1e-6
RG_C = 8.0
CONV_WIDTH = 4
ML_CHUNK = 512
HALO = 8
ROWS_VECTOR = 512
ROWS_MATMUL = 1024
ROWS_IN_BWD = 512
ROWS_GRAD_MATMUL = 2048
BLOCK_ELEMS = 1 << 18
ADAM_LR = 0.001
ADAM_B1 = 0.9
ADAM_B2 = 0.999
ADAM_EPS = 1e-08
ADAM_WD = 0.01
ADAM_STEP = 10
MESH = pl.DeviceIdType.MESH


def _pcall(body, **kw):
    return pl.pallas_call(body, **kw)


class Leg(NamedTuple):
    src: jax.Array
    kind: str

    def landing(self):
        a = self.src
        shape = {"chips": lambda: a.shape, "spread": lambda: (4, 2) + a.shape, "sib_fill": lambda: a.shape,
                 "sib_w_in": lambda: (a.shape[0], 4, a.shape[1] // 2, a.shape[2] // 4),
                 "sib_w_out": lambda: a.shape[:2] + a.shape[3:], "sib_slab": lambda: a.shape[1:]}[self.kind]()
        return jax.ShapeDtypeStruct(shape, a.dtype)

    def copies(self, src, dst, x, y, c):
        a, me_s, o = self.src, 2 * x + y, 1 - c
        chips = [(1 - x, y), (x, 1 - y), (1 - x, 1 - y)]
        if self.kind == "chips":
            return [(src.at[2 * px + py], dst.at[me_s], (px, py, c)) for px, py in chips], []
        if self.kind == "spread":
            return [(src, dst.at[me_s, c], (px, py, c)) for px, py in chips], [(src, dst.at[me_s, c])]
        depth = pl.ds(0, a.shape[0])
        if self.kind == "sib_fill":
            return [(dst.at[depth, c], dst.at[depth, c], (x, y, o))], []
        if self.kind == "sib_w_in":
            half, n = a.shape[1] // 2, a.shape[2] // 4
            return [(src.at[depth, pl.ds(o * half, half), pl.ds(s * n, n)], dst.at[depth, s], (x, y, o))
                    for s in range(4)], []
        if self.kind == "sib_w_out":
            return [(src.at[depth, pl.ds(0, 4), o], dst, (x, y, o))], []
        return [(src.at[o], dst, (x, y, o))], []

    def n_copies(self):
        return {"chips": 3, "spread": 3, "sib_w_in": 4}.get(self.kind, 1)


def _exchange_body(legs, srcs, dsts, send_sems, recv_sems, local_sems):
    x, y, c = _me()
    remote, local, k = [], [], 0
    for i, leg in enumerate(legs):
        far, near = leg.copies(srcs[i], dsts[i], x, y, c)
        for src, dst, to in far:
            remote.append(_remote(src, dst, send_sems.at[k], recv_sems.at[k], to))
            k += 1
        local += [pltpu.make_async_copy(src, dst, local_sems.at[i]) for src, dst in near]
    return remote, local


def _exchange_sems(legs):
    n = sum(leg.n_copies() for leg in legs)
    return [pltpu.SemaphoreType.DMA((n,)), pltpu.SemaphoreType.DMA((n,)), pltpu.SemaphoreType.DMA((len(legs),))]


def _exchange_aliases(legs, n_in, n_out):
    return {n_in + i: n_out + i for i, leg in enumerate(legs) if leg.kind == "sib_fill"}


def _pcall_ride(body, ride, *, grid, in_specs, out_specs, out_shape, args, scratch_shapes=(), **kw):
    n_in, n_out, n_scr = len(in_specs), len(out_specs), len(scratch_shapes)
    if not ride:
        res = _pcall(body, grid=grid, in_specs=in_specs, out_specs=out_specs, out_shape=out_shape,
                     scratch_shapes=list(scratch_shapes), **kw)(*args)
        return res, []
    nr = len(ride)

    def riding(*refs):
        ins, rsrc = refs[:n_in], refs[n_in:n_in + nr]
        outs, rdst = refs[n_in + nr:n_in + nr + n_out], refs[n_in + nr + n_out:n_in + 2 * nr + n_out]
        scr = refs[n_in + 2 * nr + n_out:n_in + 2 * nr + n_out + n_scr]
        copies, local = _exchange_body(ride, rsrc, rdst, *refs[n_in + 2 * nr + n_out + n_scr:])
        first = functools.reduce(jnp.logical_and, [pl.program_id(a) == 0 for a in range(len(grid))])
        last = functools.reduce(jnp.logical_and, [pl.program_id(a) == grid[a] - 1 for a in range(len(grid))])

        @pl.when(first)
        def _():
            for cp in copies + local:
                cp.start()

        body(*ins, *outs, *scr)

        @pl.when(last)
        def _():
            for cp in copies:
                cp.wait_recv()
            for cp in copies:
                cp.wait_send()
            for cp in local:
                cp.wait()

    hbm = pl.BlockSpec(memory_space=pltpu.HBM)
    aliases = {**kw.pop("input_output_aliases", {}), **_exchange_aliases(ride, n_in, n_out)}
    res = _pcall(
        riding, grid=grid, in_specs=list(in_specs) + [hbm] * nr, out_specs=list(out_specs) + [hbm] * nr,
        out_shape=list(out_shape) + [leg.landing() for leg in ride], input_output_aliases=aliases,
        scratch_shapes=list(scratch_shapes) + _exchange_sems(ride), **kw)(*args, *[leg.src for leg in ride])
    return res[:n_out], res[n_out:]


def _seq(n=1):
    return pltpu.CompilerParams(dimension_semantics=("arbitrary",) * n)


def _dot(a, b):
    return jnp.dot(a, b, preferred_element_type=F32)


def _dot_nt(a, b):
    return lax.dot_general(a, b, (((1,), (1,)), ((), ())), preferred_element_type=F32)


def _dot_tn(a, b):
    return lax.dot_general(a, b, (((0,), (0,)), ((), ())), preferred_element_type=F32)


def _bf(x):
    return x.astype(BF16)


def _sigmoid(x):
    return 0.5 * jnp.tanh(0.5 * x) + 0.5


def _log1p(z):
    u = 1.0 + z
    return jnp.where(u == 1.0, z, jnp.log(u) * (z / jnp.where(u == 1.0, 1.0, u - 1.0)))


def _softplus(x):
    return jnp.maximum(x, 0.0) + _log1p(jnp.exp(-jnp.abs(x)))


def _log_sigmoid(x):
    return -_softplus(-x)


def _one_minus_sq(a, log_a):
    x = 2.0 * log_a
    small = -x * (1.0 + x * (0.5 + x * (1.0 / 6.0)))
    return jnp.where(x > -0.004, small, 1.0 - a * a)


def _dsilu(x, s):
    return s * (1.0 + x * (1.0 - s))


def _rowsum(x):
    return jnp.sum(x, axis=1, keepdims=True)


def _colsum(x):
    return jnp.sum(x, axis=0, keepdims=True)


def _shift_down(win, s):
    return win if s == 0 else pltpu.roll(win, s, 0)


def _shift_up(win, s):
    return win if s == 0 else pltpu.roll(win, win.shape[0] - s, 0)


def _conv_taps(win):
    return [_shift_down(win, CONV_WIDTH - 1 - k)[HALO:] for k in range(CONV_WIDTH)]


def _conv_fwd(taps, w_ref, b_ref):
    acc = b_ref[...] + w_ref[CONV_WIDTH - 1:CONV_WIDTH, :] * taps[CONV_WIDTH - 1]
    for k in range(CONV_WIDTH - 1):
        acc = acc + w_ref[k:k + 1, :] * taps[k]
    return acc


def _split3(x):
    hi = _bf(x)
    r1 = x - hi.astype(F32)
    mid = _bf(r1)
    lo = _bf(r1 - mid.astype(F32))
    return hi, mid, lo


def _tri_dot_left(tri, x):
    hi, mid, lo = _split3(x)
    return _dot(tri, hi) + _dot(tri, mid) + _dot(tri, lo)


def _tri_dot_right(x, tri):
    hi, mid, lo = _split3(x)
    return _dot(hi, tri) + _dot(mid, tri) + _dot(lo, tri)


def _tile(n, want):
    t = min(n, want)
    assert n % t == 0
    return t


def _ln_inproj(x, g, scale, shift, w4, ride=None):
    s_len, d = x.shape
    nj, _, nsh = w4.shape
    tm = _tile(s_len, ROWS_MATMUL)

    def body(x_ref, g_ref, sc_ref, sh_ref, w_ref, h_ref, u_ref, hs):
        @pl.when(pl.program_id(1) == 0)
        def _():
            xv = x_ref[...]
            r = lax.rsqrt(jnp.mean(xv * xv, axis=-1, keepdims=True) + EPS)
            hv = (xv * r * g_ref[...]) * (1.0 + sc_ref[...]) + sh_ref[...]
            hs[...] = _bf(hv)
            h_ref[...] = hs[...]

        u_ref[...] = _dot(hs[...], w_ref[0])

    vec = pl.BlockSpec((1, d), lambda i, j: (0, 0))
    return _pcall_ride(
        body, ride, name="ln_inproj", grid=(s_len // tm, nj),
        in_specs=[pl.BlockSpec((tm, d), lambda i, j: (i, 0)), vec, vec, vec,
                  pl.BlockSpec((1, d, nsh), lambda i, j: (j, 0, 0))],
        out_specs=[pl.BlockSpec((tm, d), lambda i, j: (i, 0)), pl.BlockSpec((tm, nsh), lambda i, j: (i, j))],
        out_shape=[jax.ShapeDtypeStruct((s_len, d), BF16), jax.ShapeDtypeStruct((s_len, nj * nsh), F32)],
        scratch_shapes=[pltpu.VMEM((tm, d), BF16)],
        compiler_params=_seq(2),
        args=(x, g, scale, shift, w4))


def _rg_gates(xc, wa_ref, ba_ref, wx_ref, bx_ref, lam_ref):
    heads, hd, _ = wa_ref.shape
    xb = _bf(xc)
    ga = jnp.concatenate([_dot(xb[:, h * hd:(h + 1) * hd], wa_ref[h]) for h in range(heads)], axis=1) + ba_ref[...]
    gx = jnp.concatenate([_dot(xb[:, h * hd:(h + 1) * hd], wx_ref[h]) for h in range(heads)], axis=1) + bx_ref[...]
    r = _sigmoid(ga)
    ig = _sigmoid(gx)
    sp = _softplus(-lam_ref[...])
    log_a = (-RG_C) * r * sp
    a = jnp.exp(log_a)
    mult = jnp.sqrt(_one_minus_sq(a, log_a))
    return r, ig, sp, log_a, a, mult


def _scan_groups(a, u, reverse):
    n, c = a.shape
    a = a.reshape(n // 8, 8, c)
    u = u.reshape(n // 8, 8, c)
    row = lax.broadcasted_iota(jnp.int32, a.shape, 1)
    for k in (1, 2, 4):
        sft = 8 - k if reverse else k
        a_sh, u_sh = pltpu.roll(a, sft, 1), pltpu.roll(u, sft, 1)
        ok = row < 8 - k if reverse else row >= k
        u = jnp.where(ok, a * u_sh + u, u)
        a = jnp.where(ok, a * a_sh, a)
    return a.reshape(n, c), u.reshape(n, c)


def _rg_fwd(u, conv_w, conv_b, wa_b, ba, wx_b, bx, lam, ride=None):
    s_len = u.shape[0]
    d = conv_w.shape[1]
    tm = _tile(s_len, ROWS_VECTOR)
    per = tm // HALO

    def body(x_ref, xp_ref, z_ref, cw_ref, cb_ref, wa_ref, ba_ref, wx_ref, bx_ref, lam_ref,
             hh_ref, y_ref, carry):
        i = pl.program_id(0)

        @pl.when(i == 0)
        def _():
            carry[...] = jnp.zeros_like(carry)

        prev = jnp.where(i == 0, 0.0, xp_ref[...])
        xc = _conv_fwd(_conv_taps(jnp.concatenate([prev, x_ref[...]], axis=0)), cw_ref, cb_ref)
        _, ig, _, _, a, mult = _rg_gates(xc, wa_ref, ba_ref, wx_ref, bx_ref, lam_ref)
        ca, cu = _scan_groups(a, mult * (ig * xc), reverse=False)
        c = carry[0:1, :]
        for j in range(tm // 8):
            blk = ca[j * 8:(j + 1) * 8] * c + cu[j * 8:(j + 1) * 8]
            hh_ref[j * 8:(j + 1) * 8, :] = blk
            c = blk[7:8]
        carry[0:1, :] = c
        z = z_ref[...]
        y_ref[0] = _bf(hh_ref[...] * (z * _sigmoid(z)))

    vec = pl.BlockSpec((1, d), lambda i: (0, 0))
    whole3 = lambda a: pl.BlockSpec(a.shape, lambda i: (0, 0, 0))
    return _pcall_ride(
        body, ride, name="rg_fwd", grid=(s_len // tm,),
        in_specs=[pl.BlockSpec((tm, d), lambda i: (i, 0)),
                  pl.BlockSpec((HALO, d), lambda i: (jnp.maximum(i * per - 1, 0), 0)),
                  pl.BlockSpec((tm, d), lambda i: (i, 1)),
                  pl.BlockSpec((CONV_WIDTH, d), lambda i: (0, 0)), vec,
                  whole3(wa_b), vec, whole3(wx_b), vec, vec],
        out_specs=[pl.BlockSpec((tm, d), lambda i: (i, 0)), pl.BlockSpec((1, tm, d), lambda i: (0, i, 0))],
        out_shape=[jax.ShapeDtypeStruct((s_len, d), F32), jax.ShapeDtypeStruct((2, s_len, d), BF16)],
        scratch_shapes=[pltpu.VMEM((8, d), F32)],
        compiler_params=_seq(),
        args=(u, u, u, conv_w, conv_b, wa_b, ba, wx_b, bx, lam))


def _ml_pre(u, conv_w, conv_b, wqkv_b, wif_b, wift_b, b_if, b_ift):
    s_len = u.shape[0]
    d = conv_w.shape[1]
    _, heads, hd, _ = wqkv_b.shape
    ng = 2 * heads
    tm = _tile(s_len, max(ROWS_VECTOR, ML_CHUNK))
    per = tm // HALO

    def body(x_ref, xp_ref, cw_ref, cb_ref, w_ref, wif_ref, wift_ref, bif_ref, bift_ref,
             qkv_ref, gt_ref, gtt_ref, bc_ref, bct_ref):
        i = pl.program_id(0)
        prev = jnp.where(i == 0, 0.0, xp_ref[...])
        xm = x_ref[...]
        pre = _conv_fwd(_conv_taps(jnp.concatenate([prev, xm], axis=0)), cw_ref, cb_ref)
        xcb = _bf(pre * _sigmoid(pre))
        xmb = _bf(xm)
        for h in range(heads):
            hs = slice(h * hd, (h + 1) * hd)
            qkv_ref[0, :, hs] = _bf(_dot(xcb[:, hs], w_ref[0, h]))
            qkv_ref[1, :, hs] = _bf(_dot(xcb[:, hs], w_ref[1, h]))
            qkv_ref[2, :, hs] = _bf(_dot(xmb[:, hs], w_ref[2, h]))
        qb, kb, vb = qkv_ref[0], qkv_ref[1], qkv_ref[2]
        gt = (_dot(qb, wif_ref[0:d, :]) + _dot(kb, wif_ref[d:2 * d, :]) + _dot(vb, wif_ref[2 * d:3 * d, :])
              + bif_ref[...])
        gtt = (_dot_nt(wift_ref[:, 0:d], qb) + _dot_nt(wift_ref[:, d:2 * d], kb)
               + _dot_nt(wift_ref[:, 2 * d:3 * d], vb) + bift_ref[...])
        gt_ref[...] = gt
        gtt_ref[...] = gtt
        r = lax.broadcasted_iota(jnp.int32, (tm, tm), 0)
        c = lax.broadcasted_iota(jnp.int32, (tm, tm), 1)
        same = (r // ML_CHUNK) == (c // ML_CHUNK)
        bc_ref[...] = _tri_dot_left(((r >= c) & same).astype(BF16), _log_sigmoid(gt))
        bct_ref[...] = _tri_dot_right(_log_sigmoid(gtt), ((r <= c) & same).astype(BF16))

    vec = pl.BlockSpec((1, d), lambda i: (0, 0))
    whole2 = lambda a: pl.BlockSpec(a.shape, lambda i: (0, 0))
    col = pl.BlockSpec((tm, ng), lambda i: (i, 0))
    row = pl.BlockSpec((ng, tm), lambda i: (0, i))
    return _pcall(
        body, name="ml_pre", grid=(s_len // tm,),
        in_specs=[pl.BlockSpec((tm, d), lambda i: (i, 2)),
                  pl.BlockSpec((HALO, d), lambda i: (jnp.maximum(i * per - 1, 0), 2)),
                  pl.BlockSpec((CONV_WIDTH, d), lambda i: (0, 0)), vec,
                  pl.BlockSpec(wqkv_b.shape, lambda i: (0, 0, 0, 0)), whole2(wif_b), whole2(wift_b), whole2(b_if),
                  whole2(b_ift)],
        out_specs=[pl.BlockSpec((3, tm, d), lambda i: (0, i, 0)), col, row, col, row],
        out_shape=[jax.ShapeDtypeStruct((3, s_len, d), BF16), jax.ShapeDtypeStruct((s_len, ng), F32),
                   jax.ShapeDtypeStruct((ng, s_len), F32), jax.ShapeDtypeStruct((s_len, ng), F32),
                   jax.ShapeDtypeStruct((ng, s_len), F32)],
        compiler_params=_seq(),
    )(u, u, conv_w, conv_b, wqkv_b, wif_b, wift_b, b_if, b_ift)


def _chunk_gates(gt, gtt, bc, bct, h, heads):
    li_c = gt[:, h:h + 1]
    li_r = gtt[h:h + 1, :]
    gf_c = gt[:, heads + h:heads + h + 1]
    b_c = bc[:, heads + h:heads + h + 1]
    b_r = bct[heads + h:heads + h + 1, :]
    return li_c, li_r, gf_c, b_c, b_r


def _chunk_weights(li_c, li_r, b_c, b_r, m_prev, causal):
    lc = b_c.shape[0]
    b_last = b_c[lc - 1:lc, :]
    dmat = jnp.where(causal, b_c - b_r + li_r, -jnp.inf)
    m_inter = b_c + m_prev
    m_t = jnp.maximum(m_inter, jnp.max(dmat, axis=1, keepdims=True))
    w_intra = jnp.exp(dmat - m_t)
    w_inter = jnp.exp(m_inter - m_t)
    g_c = b_last - b_c + li_c
    m_new = jnp.maximum(b_last + m_prev, jnp.max(g_c, axis=0, keepdims=True))
    w_state = jnp.exp(g_c - m_new)
    decay = jnp.exp(b_last + m_prev - m_new)
    return m_t, w_intra, w_inter, m_new, w_state, decay


def _tri_masks(lc):
    r = lax.broadcasted_iota(jnp.int32, (lc, lc), 0)
    c = lax.broadcasted_iota(jnp.int32, (lc, lc), 1)
    causal = r >= c
    return causal, causal.astype(BF16), (r <= c).astype(BF16)


def _mlstm_fwd(qkv, gates, u, ml_g, ycat, ride=None):
    _, s_len, d = qkv.shape
    ng = gates[0].shape[1]
    heads = ng // 2
    hd = d // heads
    lc = ML_CHUNK
    nc = s_len // lc
    kscale = hd ** -0.5

    def body(qkv_ref, gt_ref, gtt_ref, bc_ref, bct_ref, o_ref, z_ref, g_ref, _, cell_ref, y_ref, cst_ref, nst_ref,
             mst_ref, cs, ns, ms):
        @pl.when(pl.program_id(0) == 0)
        def _():
            cs[...] = jnp.zeros_like(cs)
            ns[...] = jnp.zeros_like(ns)
            ms[...] = jnp.zeros_like(ms)

        causal = _tri_masks(lc)[0]
        gtv, gttv, bcv, bctv = gt_ref[...], gtt_ref[...], bc_ref[...], bct_ref[...]
        old = [(cs[h], ns[h], ms[h]) for h in range(heads)]
        new, cells, ys = [], [], []
        for h in range(heads):
            hs = slice(h * hd, (h + 1) * hd)
            li_c, li_r, _, b_c, b_r = _chunk_gates(gtv, gttv, bcv, bctv, h, heads)
            c_old, n_old, m_old = old[h]
            m_prev = m_old[:, 0:1]
            m_t, w_intra, w_inter, m_new, w_state, decay = _chunk_weights(li_c, li_r, b_c, b_r, m_prev, causal)
            qb = qkv_ref[0, :, hs]
            ks = qkv_ref[1, :, hs].astype(F32) * kscale
            kb = _bf(ks)
            vb = qkv_ref[2, :, hs]
            s = _dot_nt(qb, kb) * w_intra
            num = _dot(_bf(s), vb) + w_inter * _dot(qb, _bf(c_old))
            den = _rowsum(s) + w_inter * _rowsum(qb.astype(F32) * n_old)
            cell = num / jnp.maximum(jnp.abs(den), jnp.exp(-m_t))
            kw = ks * w_state
            new.append((decay * c_old + _dot_tn(_bf(kw), vb), decay * n_old + _colsum(kw),
                        jnp.broadcast_to(m_new, m_old.shape)))
            cells.append(cell)
            hm = _sigmoid(o_ref[:, hs]) * cell
            hn = hm * lax.rsqrt(jnp.mean(hm * hm, axis=-1, keepdims=True) + EPS)
            z = z_ref[:, hs]
            ys.append(_bf((hn * g_ref[:, hs]) * (z * _sigmoid(z))))
        for h in range(heads):
            cst_ref[0, h] = _bf(old[h][0])
            nst_ref[0, h] = old[h][1]
            mst_ref[0, h] = old[h][2]
            cs[h], ns[h], ms[h] = new[h]
        cell_ref[...] = jnp.concatenate(cells, axis=1)
        y_ref[0] = jnp.concatenate(ys, axis=1)

    row = pl.BlockSpec((lc, d), lambda c: (c, 0))
    gcol = pl.BlockSpec((lc, ng), lambda c: (c, 0))
    grow = pl.BlockSpec((ng, lc), lambda c: (0, c))
    return _pcall_ride(
        body, ride, name="mlstm_fwd", grid=(nc,),
        in_specs=[pl.BlockSpec((3, lc, d), lambda c: (0, c, 0)), gcol, grow, gcol, grow,
                  pl.BlockSpec((lc, d), lambda c: (c, 3)), pl.BlockSpec((lc, d), lambda c: (c, 4)),
                  pl.BlockSpec((1, d), lambda c: (0, 0)), pl.BlockSpec(memory_space=pl.ANY)],
        out_specs=[row, pl.BlockSpec((1, lc, d), lambda c: (1, c, 0)),
                   pl.BlockSpec((1, heads, hd, hd), lambda c: (c, 0, 0, 0)),
                   pl.BlockSpec((1, heads, 1, hd), lambda c: (c, 0, 0, 0)),
                   pl.BlockSpec((1, heads, 1, 128), lambda c: (c, 0, 0, 0))],
        out_shape=[jax.ShapeDtypeStruct((s_len, d), F32), jax.ShapeDtypeStruct(ycat.shape, BF16),
                   jax.ShapeDtypeStruct((nc, heads, hd, hd), BF16),
                   jax.ShapeDtypeStruct((nc, heads, 1, hd), F32),
                   jax.ShapeDtypeStruct((nc, heads, 1, 128), F32)],
        scratch_shapes=[pltpu.VMEM((heads, hd, hd), F32), pltpu.VMEM((heads, 1, hd), F32),
                        pltpu.VMEM((heads, 1, 128), F32)],
        input_output_aliases={8: 1},
        compiler_params=_seq(),
        args=(qkv, *gates, u, u, ml_g, ycat))


def _out_proj(ycat, w_out_b, x, gate, ride=None):
    s_len, d = x.shape
    tm = _tile(s_len, ROWS_MATMUL)

    def body(a_ref, w_ref, x_ref, g_ref, y_ref, xn_ref):
        y = _dot(a_ref[0], w_ref[0:d, :]) + _dot(a_ref[1], w_ref[d:2 * d, :])
        y_ref[...] = y
        xn_ref[...] = x_ref[...] + g_ref[...] * y

    row = pl.BlockSpec((tm, d), lambda i: (i, 0))
    return _pcall_ride(
        body, ride, name="out_proj", grid=(s_len // tm,),
        in_specs=[pl.BlockSpec((2, tm, d), lambda i: (0, i, 0)), pl.BlockSpec((2 * d, d), lambda i: (0, 0)), row,
                  pl.BlockSpec((1, d), lambda i: (0, 0))],
        out_specs=[row, row],
        out_shape=[jax.ShapeDtypeStruct((s_len, d), F32)] * 2,
        compiler_params=_seq(),
        args=(ycat, w_out_b, x, gate))


def _out_proj_loss(ycat, w_out_b, x, gate, g, target):
    s_len, d = x.shape
    tm = _tile(s_len, ROWS_IN_BWD)

    def body(a_ref, w_ref, x_ref, gate_ref, g_ref, t_ref, y_ref, dx_ref, dg_ref, loss_ref):
        @pl.when(pl.program_id(0) == 0)
        def _():
            dg_ref[...] = jnp.zeros_like(dg_ref)
            loss_ref[...] = jnp.zeros_like(loss_ref)

        y = _dot(a_ref[0], w_ref[0:d, :]) + _dot(a_ref[1], w_ref[d:2 * d, :])
        y_ref[...] = y
        xv = x_ref[...] + gate_ref[...] * y
        r = lax.rsqrt(jnp.mean(xv * xv, axis=-1, keepdims=True) + EPS)
        xn = xv * r
        err = xn * g_ref[...] - t_ref[...]
        loss_ref[...] += 0.5 * jnp.sum(jnp.mean(err * err, axis=-1, keepdims=True))
        dout = err * (1.0 / d)
        dg_ref[...] += _colsum(dout * xn)
        dxn = dout * g_ref[...]
        dx_ref[...] = r * (dxn - xn * jnp.mean(dxn * xn, axis=-1, keepdims=True))

    row = pl.BlockSpec((tm, d), lambda i: (i, 0))
    vec = pl.BlockSpec((1, d), lambda i: (0, 0))
    return _pcall(
        body, name="out_proj_loss", grid=(s_len // tm,),
        in_specs=[pl.BlockSpec((2, tm, d), lambda i: (0, i, 0)), pl.BlockSpec((2 * d, d), lambda i: (0, 0)), row, vec,
                  vec, row],
        out_specs=[row, row, vec, pl.BlockSpec((1, 128), lambda i: (0, 0))],
        out_shape=[jax.ShapeDtypeStruct((s_len, d), F32), jax.ShapeDtypeStruct((s_len, d), F32),
                   jax.ShapeDtypeStruct((1, d), F32), jax.ShapeDtypeStruct((1, 128), F32)],
        compiler_params=_seq(),
    )(ycat, w_out_b, x, gate, g, target)


def _out_bwd(dxn, y, gate, w_out_b):
    s_len, d = dxn.shape
    tm = _tile(s_len, ROWS_MATMUL)

    def body(dx_ref, y_ref, g_ref, w_ref, dg_ref, dy_ref, dc_ref):
        @pl.when(pl.program_id(0) == 0)
        def _():
            dg_ref[...] = jnp.zeros_like(dg_ref)

        dx = dx_ref[...]
        dg_ref[...] += _colsum(dx * y_ref[...])
        dy = _bf(g_ref[...] * dx)
        dy_ref[...] = dy
        dc_ref[0] = _dot_nt(dy, w_ref[0:d, :])
        dc_ref[1] = _dot_nt(dy, w_ref[d:2 * d, :])

    row = pl.BlockSpec((tm, d), lambda i: (i, 0))
    vec = pl.BlockSpec((1, d), lambda i: (0, 0))
    return _pcall(
        body, name="out_bwd", grid=(s_len // tm,),
        in_specs=[row, row, vec, pl.BlockSpec((2 * d, d), lambda i: (0, 0))],
        out_specs=[vec, row, pl.BlockSpec((2, tm, d), lambda i: (0, i, 0))],
        out_shape=[jax.ShapeDtypeStruct((1, d), F32), jax.ShapeDtypeStruct((s_len, d), BF16),
                   jax.ShapeDtypeStruct((2, s_len, d), F32)],
        compiler_params=_seq(),
    )(dxn, y, gate, w_out_b)


def _grad_matmul(a3, b3, nblk, a_idx, b_idx, out_shape, out_block, out_idx, ride=None):
    _, s_len, m = a3.shape
    n = b3.shape[2]
    tk = _tile(s_len, ROWS_GRAD_MATMUL)

    def body(a_ref, b_ref, o_ref):
        @pl.when(pl.program_id(1) == 0)
        def _():
            o_ref[...] = jnp.zeros_like(o_ref)

        o_ref[...] += _dot_tn(a_ref[0], b_ref[0])

    (out,), got = _pcall_ride(
        body, ride, name="grad_matmul", grid=(nblk, s_len // tk),
        in_specs=[pl.BlockSpec((1, tk, m), lambda p, t: (a_idx(p), t, 0)),
                  pl.BlockSpec((1, tk, n), lambda p, t: (b_idx(p), t, 0))],
        out_specs=[pl.BlockSpec((None,) + out_block, lambda p, t: (0,) + out_idx(p))],
        out_shape=[jax.ShapeDtypeStruct((1,) + out_shape, F32)],
        compiler_params=_seq(2), args=(a3, b3))
    return out, got


DU_PLANE = (2, 3, 4, 0, 1)


def _mlstm_bwd(qkv, gates, cst, nst, mst, cell, u, ml_g, d_ycat, wif_b, ride=None):
    _, s_len, d = qkv.shape
    ng = gates[0].shape[1]
    heads = ng // 2
    hd = d // heads
    lc = ML_CHUNK
    nc = s_len // lc
    kscale = hd ** -0.5

    def body(qkv_ref, gt_ref, gtt_ref, bc_ref, bct_ref, cst_ref, nst_ref, mst_ref, cell_ref, o_ref, z_ref, g_ref, dy_ref,
             wif_ref, dqkv_ref, dgt_ref, dbif_ref, du_ref, dg_ref, dcs, dns):
        @pl.when(pl.program_id(0) == 0)
        def _():
            dbif_ref[...] = jnp.zeros_like(dbif_ref)
            dcs[...] = jnp.zeros_like(dcs)
            dns[...] = jnp.zeros_like(dns)
            dg_ref[...] = jnp.zeros_like(dg_ref)

        causal, tril, triu = _tri_masks(lc)
        tril_strict = (tril.astype(F32) - (tril * triu).astype(F32)).astype(BF16)
        gtv, gttv, bcv, bctv = gt_ref[...], gtt_ref[...], bc_ref[...], bct_ref[...]
        lane = lax.broadcasted_iota(jnp.int32, (lc, ng), 1)
        dli_all = jnp.zeros((lc, ng), F32)
        from_later = jnp.zeros((lc, ng), F32)
        from_earlier = jnp.zeros((lc, ng), F32)
        across_all = jnp.zeros((1, ng), F32)
        old = [(dcs[h], dns[h]) for h in range(heads)]
        new, d_o, d_z, d_g, dqs, dks, dvs = [], [], [], [], [], [], []
        for h in range(heads):
            hs = slice(h * hd, (h + 1) * hd)
            li_c, li_r, gf_c, b_c, b_r = _chunk_gates(gtv, gttv, bcv, bctv, h, heads)
            m_prev = mst_ref[0, h][:, 0:1]
            m_t, w_intra, w_inter, _, w_state, decay = _chunk_weights(li_c, li_r, b_c, b_r, m_prev, causal)
            qb = qkv_ref[0, :, hs]
            qf = qb.astype(F32)
            ks = qkv_ref[1, :, hs].astype(F32) * kscale
            kb = _bf(ks)
            vb = qkv_ref[2, :, hs]
            c_b = cst_ref[0, h]
            n_old = nst_ref[0, h]
            s = _dot_nt(qb, kb) * w_intra
            den = _rowsum(s) + w_inter * _rowsum(qf * n_old)
            floor = jnp.exp(-m_t)
            dstab = jnp.maximum(jnp.abs(den), floor)
            cell = cell_ref[:, hs]
            o = o_ref[:, hs]
            so = _sigmoid(o)
            hm = so * cell
            rinv = lax.rsqrt(jnp.mean(hm * hm, axis=-1, keepdims=True) + EPS)
            hn = hm * rinv
            z = z_ref[:, hs]
            sgz = _sigmoid(z)
            sz = z * sgz
            gh = g_ref[:, hs]
            dy = dy_ref[0, :, hs]
            d_z.append(_bf(dy * (hn * gh) * _dsilu(z, sgz)))
            d_g.append(_colsum(dy * hn * sz))
            dhn = dy * gh * sz
            dhm = rinv * (dhn - hn * jnp.mean(dhn * hn, axis=-1, keepdims=True))
            d_o.append(_bf(dhm * cell * so * (1.0 - so)))
            dcell = dhm * so
            dnum = dcell / dstab
            dnb = _bf(dnum)
            dden = -_rowsum(dcell * cell) / dstab * jnp.where(jnp.abs(den) > floor, jnp.where(den > 0.0, 1.0, -1.0), 0.0)
            dst = _dot_nt(dnb, vb) + dden
            dsdb = _bf(dst * w_intra)
            dc_out, dn_out = old[h]
            dcb = _bf(dc_out)
            dq_inter = w_inter * (_dot_nt(dnb, c_b) + dden * n_old)
            dk_inter = w_state * (_dot_nt(vb, dcb) + dn_out)
            dq = _dot(dsdb, kb) + dq_inter
            dk = _dot_tn(dsdb, qb) + dk_inter
            dv = _dot_tn(_bf(s), dnb) + _dot(_bf(ks * w_state), dcb)
            wq = w_inter * qf
            new.append((decay * dc_out + _dot_tn(_bf(wq), dnb), decay * dn_out + _colsum(wq * dden)))
            pmat = dst * s
            p_rows = _rowsum(pmat)
            p_cols = _rowsum(pmat.T)
            q_in = _rowsum(qf * dq_inter)
            k_in = _rowsum(ks * dk_inter)
            across = decay * (jnp.sum(dc_out * c_b.astype(F32), keepdims=True) + jnp.sum(dn_out * n_old, keepdims=True))
            dli_all = dli_all + jnp.where(lane == h, p_cols + k_in, 0.0)
            from_later = from_later + jnp.where(lane == heads + h, p_rows - p_cols + q_in, 0.0)
            from_earlier = from_earlier + jnp.where(lane == heads + h, k_in, 0.0)
            across_all = across_all + jnp.where(lane[0:1] == heads + h, across, 0.0)
            dqs.append(dq)
            dks.append(dk * kscale)
            dvs.append(dv)
        for h in range(heads):
            dcs[h], dns[h] = new[h]
        du_ref[0] = jnp.concatenate(d_o, axis=1)
        du_ref[1] = jnp.concatenate(d_z, axis=1)
        dg_ref[...] += jnp.concatenate(d_g, axis=1)
        dlf = _tri_dot_left(triu, from_later) + _tri_dot_left(tril_strict, from_earlier) + across_all
        dgt = dli_all + dlf * _sigmoid(-gtv)
        dgt_ref[...] = dgt
        dbif_ref[...] += _colsum(dgt)
        dgb = _bf(dgt)
        dqkv_ref[0] = _bf(jnp.concatenate(dqs, axis=1) + _dot_nt(dgb, wif_ref[0:d, :]))
        dqkv_ref[1] = _bf(jnp.concatenate(dks, axis=1) + _dot_nt(dgb, wif_ref[d:2 * d, :]))
        dqkv_ref[2] = _bf(jnp.concatenate(dvs, axis=1) + _dot_nt(dgb, wif_ref[2 * d:3 * d, :]))

    rev = lambda c: nc - 1 - c
    row = pl.BlockSpec((lc, d), lambda c: (rev(c), 0))
    gcol = pl.BlockSpec((lc, ng), lambda c: (rev(c), 0))
    grow = pl.BlockSpec((ng, lc), lambda c: (0, rev(c)))
    return _pcall_ride(
        body, ride, name="mlstm_bwd", grid=(nc,),
        in_specs=[pl.BlockSpec((3, lc, d), lambda c: (0, rev(c), 0)), gcol, grow, gcol, grow,
                  pl.BlockSpec((1, heads, hd, hd), lambda c: (rev(c), 0, 0, 0)),
                  pl.BlockSpec((1, heads, 1, hd), lambda c: (rev(c), 0, 0, 0)),
                  pl.BlockSpec((1, heads, 1, 128), lambda c: (rev(c), 0, 0, 0)),
                  row, pl.BlockSpec((lc, d), lambda c: (rev(c), 3)), pl.BlockSpec((lc, d), lambda c: (rev(c), 4)),
                  pl.BlockSpec((1, d), lambda c: (0, 0)), pl.BlockSpec((1, lc, d), lambda c: (1, rev(c), 0)),
                  pl.BlockSpec((3 * d, ng), lambda c: (0, 0))],
        out_specs=[pl.BlockSpec((3, lc, d), lambda c: (0, rev(c), 0)), pl.BlockSpec((lc, ng), lambda c: (rev(c), 0)),
                   pl.BlockSpec((1, ng), lambda c: (0, 0)), pl.BlockSpec((2, lc, d), lambda c: (0, rev(c), 0)),
                   pl.BlockSpec((1, d), lambda c: (0, 0))],
        out_shape=[jax.ShapeDtypeStruct((3, s_len, d), BF16), jax.ShapeDtypeStruct((s_len, ng), F32),
                   jax.ShapeDtypeStruct((1, ng), F32), jax.ShapeDtypeStruct((5, s_len, d), BF16),
                   jax.ShapeDtypeStruct((1, d), F32)],
        scratch_shapes=[pltpu.VMEM((heads, hd, hd), F32), pltpu.VMEM((heads, 1, hd), F32)],
        compiler_params=_seq(),
        args=(qkv, *gates, cst, nst, mst, cell, u, u, ml_g, d_ycat, wif_b))


def _conv_bwd_tile(dp, later, taps, cw_ref, gw_ref, gb_ref):
    tm = dp.shape[0]
    dwin = jnp.concatenate([dp, later[...]], axis=0)
    later[...] = dp[0:HALO]
    acc = cw_ref[CONV_WIDTH - 1:CONV_WIDTH, :] * dp
    for k in range(CONV_WIDTH):
        if k < CONV_WIDTH - 1:
            acc = acc + cw_ref[k:k + 1, :] * _shift_up(dwin, CONV_WIDTH - 1 - k)[0:tm]
        gw_ref[k:k + 1, :] += _colsum(dp * taps[k])
    gb_ref[...] += _colsum(dp)
    return acc


def _ml_pre_bwd(dqkv, u, conv_w, conv_b, wqkv_b, du):
    s_len = u.shape[0]
    d = conv_w.shape[1]
    _, heads, hd, _ = wqkv_b.shape
    tm = _tile(s_len, ROWS_VECTOR)
    per = tm // HALO
    nt = s_len // tm

    def body(dqkv_ref, x_ref, xp_ref, cw_ref, cb_ref, w_ref, _, dx_ref, gw_ref, gcw_ref, gcb_ref, later, dps, dxs):
        i = pl.program_id(0)

        @pl.when(i == 0)
        def _():
            gw_ref[...] = jnp.zeros_like(gw_ref)
            gcw_ref[...] = jnp.zeros_like(gcw_ref)
            gcb_ref[...] = jnp.zeros_like(gcb_ref)
            later[...] = jnp.zeros_like(later)

        prev = jnp.where(i == nt - 1, 0.0, xp_ref[...])
        xm = x_ref[...]
        taps = _conv_taps(jnp.concatenate([prev, xm], axis=0))
        pre = _conv_fwd(taps, cw_ref, cb_ref)
        sg = _sigmoid(pre)
        xcb = _bf(pre * sg)
        xmb = _bf(xm)
        for h in range(heads):
            hs = slice(h * hd, (h + 1) * hd)
            dqh, dkh, dvh = dqkv_ref[0, :, hs], dqkv_ref[1, :, hs], dqkv_ref[2, :, hs]
            dxc = _dot_nt(dqh, w_ref[0, h]) + _dot_nt(dkh, w_ref[1, h])
            dps[:, hs] = dxc * _dsilu(pre[:, hs], sg[:, hs])
            dxs[:, hs] = _dot_nt(dvh, w_ref[2, h])
            gw_ref[0, h] += _dot_tn(xcb[:, hs], dqh)
            gw_ref[1, h] += _dot_tn(xcb[:, hs], dkh)
            gw_ref[2, h] += _dot_tn(xmb[:, hs], dvh)
        dx_ref[0] = _bf(_conv_bwd_tile(dps[...], later, taps, cw_ref, gcw_ref, gcb_ref) + dxs[...])

    rev = lambda i: nt - 1 - i
    vec = pl.BlockSpec((1, d), lambda i: (0, 0))
    cwb = pl.BlockSpec((CONV_WIDTH, d), lambda i: (0, 0))
    whole4 = pl.BlockSpec(wqkv_b.shape, lambda i: (0, 0, 0, 0))
    return _pcall(
        body, name="ml_pre_bwd", grid=(nt,),
        in_specs=[pl.BlockSpec((3, tm, d), lambda i: (0, rev(i), 0)), pl.BlockSpec((tm, d), lambda i: (rev(i), 2)),
                  pl.BlockSpec((HALO, d), lambda i: (jnp.maximum(rev(i) * per - 1, 0), 2)),
                  cwb, vec, whole4, pl.BlockSpec(memory_space=pl.ANY)],
        out_specs=[pl.BlockSpec((1, tm, d), lambda i: (DU_PLANE[2], rev(i), 0)), whole4, cwb, vec],
        out_shape=[jax.ShapeDtypeStruct(du.shape, BF16), jax.ShapeDtypeStruct(wqkv_b.shape, F32),
                   jax.ShapeDtypeStruct((CONV_WIDTH, d), F32), jax.ShapeDtypeStruct((1, d), F32)],
        scratch_shapes=[pltpu.VMEM((HALO, d), F32), pltpu.VMEM((tm, d), F32), pltpu.VMEM((tm, d), F32)],
        input_output_aliases={6: 0},
        compiler_params=_seq(),
    )(dqkv, u, u, conv_w, conv_b, wqkv_b, du)


def _rg_bwd(d_ycat, u, hh, conv_w, conv_b, wa_b, ba, wx_b, bx, lam, du):
    s_len = u.shape[0]
    d = conv_w.shape[1]
    heads, hd, _ = wa_b.shape
    tm = _tile(s_len, ROWS_VECTOR)
    per = tm // HALO
    nt = s_len // tm

    def body(dy_ref, x_ref, xp_ref, z_ref, hh_ref, hp_ref, cw_ref, cb_ref, wa_ref, ba_ref, wx_ref, bx_ref, lam_ref, _,
             du_ref, gwa_ref, gwx_ref, gba_ref, gbx_ref, glam_ref, gcw_ref, gcb_ref, carry, gbuf, later, dxcs):
        i = pl.program_id(0)
        first = i == nt - 1

        @pl.when(i == 0)
        def _():
            carry[...] = jnp.zeros_like(carry)
            later[...] = jnp.zeros_like(later)
            gwa_ref[...] = jnp.zeros_like(gwa_ref)
            gwx_ref[...] = jnp.zeros_like(gwx_ref)
            gba_ref[...] = jnp.zeros_like(gba_ref)
            gbx_ref[...] = jnp.zeros_like(gbx_ref)
            glam_ref[...] = jnp.zeros_like(glam_ref)
            gcw_ref[...] = jnp.zeros_like(gcw_ref)
            gcb_ref[...] = jnp.zeros_like(gcb_ref)

        prev = jnp.where(first, 0.0, xp_ref[...])
        taps = _conv_taps(jnp.concatenate([prev, x_ref[...]], axis=0))
        xc = _conv_fwd(taps, cw_ref, cb_ref)
        r, ig, sp, log_a, a, mult = _rg_gates(xc, wa_ref, ba_ref, wx_ref, bx_ref, lam_ref)
        z = z_ref[...]
        sgz = _sigmoid(z)
        dy = dy_ref[0]
        hh_v = hh_ref[...]
        du_ref[1] = _bf(dy * hh_v * _dsilu(z, sgz))
        dhh = dy * (z * sgz)
        rows = lax.broadcasted_iota(jnp.int32, a.shape, 0)
        coef = jnp.where(rows == tm - 1, carry[1:2, :], _shift_up(a, 1))
        ca, cu = _scan_groups(coef, dhh, reverse=True)
        c = carry[0:1, :]
        for j in range(tm // 8 - 1, -1, -1):
            blk = ca[j * 8:(j + 1) * 8] * c + cu[j * 8:(j + 1) * 8]
            gbuf[j * 8:(j + 1) * 8, :] = blk
            c = blk[0:1]
        carry[0:1, :] = c
        carry[1:2, :] = a[0:1]
        g = gbuf[...]
        hprev_tile = jnp.where(first, 0.0, hp_ref[...])
        hprev = _shift_down(jnp.concatenate([hprev_tile, hh_v], axis=0), 1)[HALO:]
        da = g * hprev
        gx_ = g * xc
        d_mult = gx_ * ig
        d_ig = gx_ * mult
        dxc = g * mult * ig
        dlog_a = da * a - d_mult * (a * a / mult)
        d_r = dlog_a * ((-RG_C) * sp)
        glam_ref[...] += _colsum(dlog_a * ((-RG_C) * r)) * (-_sigmoid(-lam_ref[...]))
        d_ga = d_r * r * (1.0 - r)
        d_gx = d_ig * ig * (1.0 - ig)
        gba_ref[...] += _colsum(d_ga)
        gbx_ref[...] += _colsum(d_gx)
        xb = _bf(xc)
        dgab = _bf(d_ga)
        dgxb = _bf(d_gx)
        for h in range(heads):
            hs = slice(h * hd, (h + 1) * hd)
            dxcs[:, hs] = dxc[:, hs] + _dot_nt(dgab[:, hs], wa_ref[h]) + _dot_nt(dgxb[:, hs], wx_ref[h])
            gwa_ref[h] += _dot_tn(xb[:, hs], dgab[:, hs])
            gwx_ref[h] += _dot_tn(xb[:, hs], dgxb[:, hs])
        du_ref[0] = _bf(_conv_bwd_tile(dxcs[...], later, taps, cw_ref, gcw_ref, gcb_ref))

    assert DU_PLANE[0] % 2 == 0 and DU_PLANE[1] == DU_PLANE[0] + 1
    rev = lambda i: nt - 1 - i
    row = pl.BlockSpec((tm, d), lambda i: (rev(i), 0))
    halo_prev = lambda col: pl.BlockSpec((HALO, d), lambda i: (jnp.maximum(rev(i) * per - 1, 0), col))
    vec = pl.BlockSpec((1, d), lambda i: (0, 0))
    cwb = pl.BlockSpec((CONV_WIDTH, d), lambda i: (0, 0))
    whole3 = lambda a: pl.BlockSpec(a.shape, lambda i: (0, 0, 0))
    return _pcall(
        body, name="rg_bwd", grid=(nt,),
        in_specs=[pl.BlockSpec((1, tm, d), lambda i: (0, rev(i), 0)), row, halo_prev(0),
                  pl.BlockSpec((tm, d), lambda i: (rev(i), 1)), row, halo_prev(0),
                  cwb, vec, whole3(wa_b), vec, whole3(wx_b), vec, vec, pl.BlockSpec(memory_space=pl.ANY)],
        out_specs=[pl.BlockSpec((2, tm, d), lambda i: (DU_PLANE[0] // 2, rev(i), 0)), whole3(wa_b), whole3(wa_b),
                   vec, vec, vec, cwb, vec],
        out_shape=[jax.ShapeDtypeStruct(du.shape, BF16), jax.ShapeDtypeStruct(wa_b.shape, F32),
                   jax.ShapeDtypeStruct(wa_b.shape, F32)] + [jax.ShapeDtypeStruct((1, d), F32)] * 3
        + [jax.ShapeDtypeStruct((CONV_WIDTH, d), F32), jax.ShapeDtypeStruct((1, d), F32)],
        scratch_shapes=[pltpu.VMEM((8, d), F32), pltpu.VMEM((tm, d), F32), pltpu.VMEM((HALO, d), F32),
                        pltpu.VMEM((tm, d), F32)],
        input_output_aliases={13: 0},
        compiler_params=_seq(),
    )(d_ycat, u, u, u, hh, hh, conv_w, conv_b, wa_b, ba, wx_b, bx, lam, du)


def _in_bwd(du, w4, x, dxn, g, scale, ride=None):
    s_len, d = x.shape
    tm = _tile(s_len, ROWS_IN_BWD)
    nsh_chips, _, nsh = w4.shape
    npc = du.shape[0]
    ck = d // 4
    assert nsh % ck == 0 and npc * d == nsh_chips * nsh

    def body(du_ref, w_ref, x_ref, dxn_ref, g_ref, sc_ref, dx_ref, dsh_ref, dsc_ref, dg_ref):
        @pl.when(pl.program_id(0) == 0)
        def _():
            dsh_ref[...] = jnp.zeros_like(dsh_ref)
            dsc_ref[...] = jnp.zeros_like(dsc_ref)
            dg_ref[...] = jnp.zeros_like(dg_ref)

        dh = None
        for q in range(npc * d // ck):
            col = q * ck
            p, pc = col // d, col % d
            s, sc = col // nsh, col % nsh
            t = _dot_nt(du_ref[DU_PLANE[p], :, pc:pc + ck], w_ref[s, :, sc:sc + ck])
            dh = t if dh is None else dh + t
        xv = x_ref[...]
        r = lax.rsqrt(jnp.mean(xv * xv, axis=-1, keepdims=True) + EPS)
        xn = xv * r
        gv = g_ref[...]
        onesc = 1.0 + sc_ref[...]
        dsh_ref[...] += _colsum(dh)
        dsc_ref[...] += _colsum(dh * (xn * gv))
        dg_ref[...] += _colsum(dh * xn * onesc)
        dxh = dh * (gv * onesc)
        dx_ref[...] = dxn_ref[...] + r * (dxh - xn * jnp.mean(dxh * xn, axis=-1, keepdims=True))

    row = pl.BlockSpec((tm, d), lambda i: (i, 0))
    vec = pl.BlockSpec((1, d), lambda i: (0, 0))
    return _pcall_ride(
        body, ride, name="in_bwd", grid=(s_len // tm,),
        in_specs=[pl.BlockSpec((npc, tm, d), lambda i: (0, i, 0)), pl.BlockSpec(w4.shape, lambda i: (0, 0, 0)), row, row,
                  vec, vec],
        out_specs=[row, vec, vec, vec],
        out_shape=[jax.ShapeDtypeStruct((s_len, d), F32)] + [jax.ShapeDtypeStruct((1, d), F32)] * 3,
        compiler_params=_seq(),
        args=(du, w4, x, dxn, g, scale))


def _layer_fwd(x, p, rides=None, loss_head=None):
    rides = rides or {}
    landed = {}
    ride = lambda kernel: rides[kernel](landed) if kernel in rides else None
    (h_b, u), landed["ln_inproj"] = _ln_inproj(x, p["norm_g"], p["scale"], p["shift"], p["w4"], ride("ln_inproj"))
    (hh, ycat), landed["rg_fwd"] = _rg_fwd(u, p["rg_conv_w"], p["rg_conv_b"], p["rg_wa_b"], p["rg_ba"], p["rg_wx_b"],
                                           p["rg_bx"], p["rg_lam"], ride("rg_fwd"))
    if "late" in rides:
        p = {**p, **rides["late"](landed)}
    qkv, *gates = _ml_pre(u, p["ml_conv_w"], p["ml_conv_b"], p["wqkv_b"], p["wif_b"], p["wift_b"], p["b_if"],
                          p["b_ift"])
    (cell, ycat, cst, nst, mst), landed["mlstm_fwd"] = _mlstm_fwd(qkv, gates, u, p["ml_g"], ycat, ride("mlstm_fwd"))
    if loss_head is None:
        (y, x_new), landed["out_proj"] = _out_proj(ycat, p["w_out_b"], x, p["gate"], ride("out_proj"))
    else:
        y, *x_new = _out_proj_loss(ycat, p["w_out_b"], x, p["gate"], *loss_head)
    saved = dict(x=x, h_b=h_b, u=u, hh=hh, qkv=qkv, gates=gates, cell=cell, ycat=ycat, cst=cst, nst=nst, mst=mst, y=y)
    return x_new, saved, p, landed


def _layer_bwd(dxn, p, s, rides=None):
    rides = rides or {}
    landed = {}
    ride = lambda kernel: rides[kernel](grads, landed) if kernel in rides else None
    u = s["u"]
    d = dxn.shape[1]
    d_gate, dy_b, d_ycat = _out_bwd(dxn, s["y"], p["gate"], p["w_out_b"])
    grads = dict(w_out=_grad_matmul(s["ycat"], dy_b[None], 2, lambda b: b, lambda b: 0, (2 * d, d), (d, d),
                                    lambda b: (b, 0))[0])
    (dqkv, dgt, g_b_if, du, g_ml_g), landed["mlstm_bwd"] = _mlstm_bwd(
        s["qkv"], s["gates"], s["cst"], s["nst"], s["mst"], s["cell"], u, p["ml_g"], d_ycat, p["wif_b"],
        ride("mlstm_bwd"))
    ng = dgt.shape[1]
    g_w_if = _grad_matmul(s["qkv"], _bf(dgt)[None], 3, lambda b: b, lambda b: 0, (3 * d, ng), (d, ng),
                          lambda b: (b, 0))[0][0]
    du, g_wqkv, g_ml_cw, g_ml_cb = _ml_pre_bwd(dqkv, u, p["ml_conv_w"], p["ml_conv_b"], p["wqkv_b"], du)
    du, g_wa, g_wx, g_ba, g_bx, g_lam, g_rg_cw, g_rg_cb = _rg_bwd(d_ycat, u, s["hh"], p["rg_conv_w"], p["rg_conv_b"],
                                                                  p["rg_wa_b"], p["rg_ba"], p["rg_wx_b"], p["rg_bx"],
                                                                  p["rg_lam"], du)
    grads.update(rg_conv_w=g_rg_cw, rg_conv_b=g_rg_cb, rg_w_a=g_wa, rg_b_a=g_ba, rg_w_x=g_wx, rg_b_x=g_bx,
                 rg_lambda=g_lam, ml_conv_w=g_ml_cw, ml_conv_b=g_ml_cb, ml_w_qkv=g_wqkv, ml_w_if=g_w_if, ml_b_if=g_b_if,
                 ml_norm_g=g_ml_g)
    npc = du.shape[0]
    grads["w_in"], landed["grad_w_in"] = _grad_matmul(
        s["h_b"][None], du, npc, lambda b: 0, lambda b: (b + DU_PLANE[0]) % npc, (d, npc * d), (d, d),
        lambda b: (0, b), ride("grad_w_in"))
    (dx, d_shift, d_scale, grads["norm_g"]), landed["in_bwd"] = _in_bwd(du, p["w4"], s["x"], dxn, p["norm_g"],
                                                                        p["scale"], ride("in_bwd"))
    return dx, grads, jnp.concatenate([d_shift, d_scale, d_gate], axis=1), landed


def _me():
    return lax.axis_index("x"), lax.axis_index("y"), lax.axis_index("c")


def _remote(src, dst, send_sem, recv_sem, to):
    return pltpu.make_async_remote_copy(src_ref=src, dst_ref=dst, send_sem=send_sem, recv_sem=recv_sem,
                                        device_id=to, device_id_type=MESH)


def _all_gather8(blocks, space):
    n = len(blocks)

    def body(*refs):
        x_refs, out_refs = refs[:n], refs[n:2 * n]
        send_sems, recv_sems, local_sems = refs[2 * n:]
        x, y, c = _me()
        me, sibling = (x, y, c), (x, y, 1 - c)
        chips = [(1 - x, y), (x, 1 - y), (1 - x, 1 - y)]

        def rows(i, px, py, pc):
            m_per = blocks[i].shape[0]
            return out_refs[i].at[pl.ds((4 * px + 2 * py + pc) * m_per, m_per), :]

        def copy(i, k, blk, to, src=None):
            return _remote(rows(i, *blk) if src is None else src, rows(i, *blk), send_sems.at[7 * i + k],
                           recv_sems.at[7 * i + k], to)

        mine = [pltpu.make_async_copy(x_refs[i], rows(i, *me), local_sems.at[i]) for i in range(n)]
        first = []
        for i in range(n):
            first.append(copy(i, 0, me, sibling, src=x_refs[i]))
            first += [copy(i, 1 + j, me, (*chip, c), src=x_refs[i]) for j, chip in enumerate(chips)]
        for cp in mine + first:
            cp.start()
        passed = []
        for j, chip in enumerate(chips):
            for i in range(n):
                copy(i, 1 + j, (*chip, c), me).wait_recv()
                passed.append(copy(i, 4 + j, (*chip, c), sibling))
                passed[-1].start()
        for i in range(n):
            copy(i, 0, sibling, me).wait_recv()
            for j, chip in enumerate(chips):
                copy(i, 4 + j, (*chip, 1 - c), me).wait_recv()
        for cp in first + passed:
            cp.wait_send()
        for cp in mine:
            cp.wait()

    spec = pl.BlockSpec(memory_space=space)
    return _pcall(
        body, name="all_gather8",
        out_shape=[jax.ShapeDtypeStruct((8 * b.shape[0], b.shape[1]), b.dtype) for b in blocks],
        in_specs=[spec] * n, out_specs=[spec] * n,
        scratch_shapes=[pltpu.SemaphoreType.DMA((7 * n,)), pltpu.SemaphoreType.DMA((7 * n,)),
                        pltpu.SemaphoreType.DMA((n,))],
    )(*blocks)


def _exchange(legs):
    n = len(legs)

    def body(*refs):
        copies, local = _exchange_body(legs, refs[:n], refs[n:2 * n], *refs[2 * n:])
        for cp in copies + local:
            cp.start()
        for cp in copies:
            cp.wait_recv()
        for cp in copies:
            cp.wait_send()
        for cp in local:
            cp.wait()

    hbm = pl.BlockSpec(memory_space=pltpu.HBM)
    return _pcall(body, name="exchange", out_shape=[leg.landing() for leg in legs], in_specs=[hbm] * n,
                  out_specs=[hbm] * n, input_output_aliases=_exchange_aliases(legs, 0, 0),
                  scratch_shapes=_exchange_sems(legs))(*[leg.src for leg in legs])


def _row_tile(rows, cap=4096, mult=16):
    best = None
    for t in range(mult, min(rows, cap) + 1, mult):
        if rows % t == 0:
            best = t
    return rows if best is None else best


def _pair_sum(half, own, own_spec, got, got_spec, out_shape, out_spec, grid):
    def body(_, a_ref, b_ref, o_ref):
        o_ref[...] = (a_ref[...] + b_ref[...].astype(F32)).astype(o_ref.dtype)

    return _pcall(
        body, name="pair_sum",
        grid_spec=pltpu.PrefetchScalarGridSpec(num_scalar_prefetch=1, grid=grid, in_specs=[own_spec, got_spec],
                                               out_specs=out_spec),
        out_shape=out_shape, compiler_params=_seq(len(grid)))(half, own, got)


def _chip_sum(ids, part, met, fill, layer=0, stack=1):
    _, _, rows, n = part.shape
    tr = _row_tile(rows, cap=max(16, BLOCK_ELEMS // n))
    first = isinstance(stack, int)

    def body(_, own_ref, a_ref, b_ref, c_ref, *rest):
        acc = own_ref[...].astype(F32) + a_ref[...].astype(F32)
        acc = acc + b_ref[...].astype(F32)
        rest[-1][...] = acc + c_ref[...].astype(F32)

    blk = (None, None, tr, n)
    other = lambda k: pl.BlockSpec(blk, lambda j, ids: ((ids[0] + k) % 4, 0, j, 0))
    in_specs = [pl.BlockSpec(blk, lambda j, ids: (ids[0], 0, j, 0)), other(1), other(2), other(3)]
    return _pcall(
        body, name="chip_sum",
        grid_spec=pltpu.PrefetchScalarGridSpec(
            num_scalar_prefetch=1, grid=(rows // tr,),
            in_specs=in_specs if first else in_specs + [pl.BlockSpec(memory_space=pl.ANY)],
            out_specs=pl.BlockSpec(blk, lambda j, ids: (layer, ids[1] if fill else 0, j, 0))),
        out_shape=jax.ShapeDtypeStruct(((stack,) if first else stack.shape[:1]) + (2 if fill else 1, rows, n), F32),
        input_output_aliases={} if first else {5: 0},
        compiler_params=_seq())(*((ids, part, met, met, met) if first else (ids, part, met, met, met, stack)))


def _ada_mod(c_all, w_ada, b_ada_cols):
    depth, d, n = w_ada.shape
    nb = c_all.shape[0]

    def body(c_ref, w_ref, b_ref, o_ref):
        cv = c_ref[...]
        ca = _bf(cv * _sigmoid(cv))
        o_ref[0] = _dot(ca, _bf(w_ref[0])) + b_ref[0]

    return _pcall(body, name="ada_mod", grid=(depth,),
                  in_specs=[pl.BlockSpec((nb, d), lambda l: (0, 0)), pl.BlockSpec((1, d, n), lambda l: (l, 0, 0)),
                            pl.BlockSpec((1, 1, n), lambda l: (l, 0, 0))],
                  out_specs=pl.BlockSpec((1, nb, n), lambda l: (l, 0, 0)),
                  out_shape=jax.ShapeDtypeStruct((depth, nb, n), F32), compiler_params=_seq())(c_all, w_ada, b_ada_cols)


def _ada_grad(c_all, dmod_cols, rows_all):
    nb, d = c_all.shape
    depth, _, n = dmod_cols.shape
    kinds, n_all = rows_all.shape[1], rows_all.shape[3]

    def body(c_ref, dm_ref, da_ref, gw_ref, gb_ref):
        cv = c_ref[...]
        ca = _bf(cv * _sigmoid(cv))
        gw_ref[0] = _dot_tn(ca, _bf(dm_ref[0]))
        for k in range(kinds):
            gb_ref[0, k] = _colsum(da_ref[0, k])

    return _pcall(body, name="ada_grad", grid=(depth,),
                  in_specs=[pl.BlockSpec((nb, d), lambda l: (0, 0)), pl.BlockSpec((1, nb, n), lambda l: (l, 0, 0)),
                            pl.BlockSpec((1, kinds, nb, n_all), lambda l: (l, 0, 0, 0))],
                  out_specs=[pl.BlockSpec((1, d, n), lambda l: (l, 0, 0)),
                             pl.BlockSpec((1, kinds, 1, n_all), lambda l: (l, 0, 0, 0))],
                  out_shape=[jax.ShapeDtypeStruct((depth, d, n), F32), jax.ShapeDtypeStruct((depth, kinds, 1, n_all), F32)],
                  compiler_params=_seq())(c_all, dmod_cols, rows_all)


def _adamw(items, ride=None):
    two_d = [tuple(t.reshape(w.size // w.shape[-1], w.shape[-1]) for t in (w, g, m, v)) for w, g, m, v in items]
    n = len(items)
    if n == 1:
        rows, cols = two_d[0][0].shape
        tr = _row_tile(rows, cap=max(8, BLOCK_ELEMS // cols), mult=8)
        blocks = [pl.BlockSpec((tr, cols), lambda i: (i, 0))]
        grid = (rows // tr,)
    else:
        blocks = [pl.BlockSpec(t[0].shape, lambda i: (0, 0)) for t in two_d]
        grid = (1,)

    def body(*refs):
        for k in range(n):
            w_ref, g_ref, m_ref, v_ref = refs[4 * k:4 * k + 4]
            d_ref, mo_ref, vo_ref = refs[4 * n + 3 * k:4 * n + 3 * k + 3]
            gv = g_ref[...]
            mn = ADAM_B1 * m_ref[...] + (1.0 - ADAM_B1) * gv
            vn = ADAM_B2 * v_ref[...] + (1.0 - ADAM_B2) * (gv * gv)
            m_hat = mn / (1.0 - ADAM_B1 ** ADAM_STEP)
            v_hat = vn / (1.0 - ADAM_B2 ** ADAM_STEP)
            d_ref[...] = -ADAM_LR * (m_hat / (jnp.sqrt(v_hat) + ADAM_EPS) + ADAM_WD * w_ref[...])
            mo_ref[...] = mn
            vo_ref[...] = vn

    outs, got = _pcall_ride(
        body, ride, name="adamw", grid=grid,
        in_specs=[b for b in blocks for _ in range(4)], out_specs=[b for b in blocks for _ in range(3)],
        out_shape=[jax.ShapeDtypeStruct(t[0].shape, F32) for t in two_d for _ in range(3)],
        compiler_params=_seq(), args=tuple(a for t in two_d for a in t))
    return [tuple(o.reshape(items[k][0].shape) for o in outs[3 * k:3 * k + 3]) for k in range(n)], got


WEIGHTS = ["norm_g", "w_ada", "b_ada", "w_in", "rg_conv_w", "rg_conv_b", "rg_w_a", "rg_b_a", "rg_w_x", "rg_b_x",
           "rg_lambda", "ml_conv_w", "ml_conv_b", "ml_w_q", "ml_w_k", "ml_w_v", "ml_w_if", "ml_b_if", "ml_norm_g",
           "w_out", "final_g"]
SMALL_SHARDED = {"rg_conv_w": 1, "ml_conv_w": 1, "ml_w_if": 0}
REPLICATED = ["rg_w_a", "rg_w_x", "rg_conv_b", "rg_b_a", "rg_b_x", "rg_lambda", "ml_conv_b", "ml_norm_g", "ml_b_if"]
LANES = 128


def _to_pieces(g, axis):
    shp = g.shape
    g = g.reshape(shp[:axis] + (4, 2, shp[axis] // 8) + shp[axis + 1:])
    g = jnp.moveaxis(g, (axis, axis + 1), (0, 1))
    return g.reshape(4, 2, -1)


def _from_pieces(p, shard_shape, axis):
    k = p.shape[0]
    rest = shard_shape[:axis] + (shard_shape[axis] // k,) + shard_shape[axis + 1:]
    t = jnp.moveaxis(p.reshape((k,) + rest), 0, axis)
    return t.reshape(shard_shape)


def _pad_rows(flat, mult):
    n = flat.shape[-1]
    pad = (-n) % mult
    if pad:
        flat = jnp.concatenate([flat, jnp.zeros(flat.shape[:-1] + (pad,), flat.dtype)], axis=-1)
    return flat


def kernel(x, c, norm_g, w_ada, b_ada, w_in, rg_conv_w, rg_conv_b, rg_w_a, rg_b_a, rg_w_x, rg_b_x, rg_lambda, ml_conv_w, ml_conv_b, ml_w_q, ml_w_k, ml_w_v, ml_w_if, ml_b_if, ml_norm_g, w_out, final_g, loss_target, m_norm_g, m_w_ada, m_b_ada, m_w_in, m_rg_conv_w, m_rg_conv_b, m_rg_w_a, m_rg_b_a, m_rg_w_x, m_rg_b_x, m_rg_lambda, m_ml_conv_w, m_ml_conv_b, m_ml_w_q, m_ml_w_k, m_ml_w_v, m_ml_w_if, m_ml_b_if, m_ml_norm_g, m_w_out, m_final_g, v_norm_g, v_w_ada, v_b_ada, v_w_in, v_rg_conv_w, v_rg_conv_b, v_rg_w_a, v_rg_b_a, v_rg_w_x, v_rg_b_x, v_rg_lambda, v_ml_conv_w, v_ml_conv_b, v_ml_w_q, v_ml_w_k, v_ml_w_v, v_ml_w_if, v_ml_b_if, v_ml_norm_g, v_w_out, v_final_g):
    given = dict(locals())
    ax, ay, ac = lax.axis_index("x"), lax.axis_index("y"), lax.axis_index("c")
    chip = 2 * ax + ay
    me = 2 * chip + ac
    depth, d = norm_g.shape
    n_ada = w_ada.shape[2]
    pick = lambda a, i, axis=0: lax.dynamic_index_in_dim(a, i, axis, keepdims=False)

    convs = jnp.stack([rg_conv_w, ml_conv_w])
    n_conv = 2 * depth * CONV_WIDTH // 4
    blk = jnp.concatenate([c, convs.reshape(n_conv, d), jnp.zeros((8 - 1 - n_conv, d), F32)], axis=0)
    w_in_first = lax.dynamic_slice_in_dim(w_in[0], ac * (d // 2), d // 2, 0).astype(BF16)
    g0, w_in_first = _all_gather8([blk, w_in_first], pltpu.HBM)
    g0 = g0.reshape(8, 8, d)
    c_all = g0[:, 0, :]
    conv_full = g0[0::2, 1:1 + n_conv].reshape(4, 2, depth, CONV_WIDTH, d // 4)
    conv_full = conv_full.transpose(1, 2, 3, 0, 4).reshape(2, depth, CONV_WIDTH, d)

    b_cols = lax.dynamic_slice_in_dim(b_ada, chip * n_ada, n_ada, axis=1)[:, None, :]
    mod_part = _ada_mod(c_all, w_ada, b_cols)
    g1 = _all_gather8([mod_part.transpose(1, 0, 2).reshape(8, depth * n_ada)], pltpu.VMEM)[0]
    g1 = g1.reshape(8, 8, depth, n_ada)[0::2]
    mod_me = pick(g1.transpose(1, 2, 0, 3).reshape(8, depth, 4 * n_ada), me)

    def half_of(w, axis):
        n = w.shape[axis] // 2
        return lax.dynamic_slice_in_dim(w, ac * n, n, axis).astype(BF16)

    n_sh = w_in.shape[2]
    heads, hd_cut, hd = ml_w_q.shape[1:]

    def blocks_of(l):
        wqkv = jnp.stack([ml_w_q[l], ml_w_k[l], ml_w_v[l]])
        return [half_of(w_in[l], 0), half_of(w_out[l], 0), half_of(wqkv, 2).reshape(-1, hd), half_of(ml_w_if[l], 0)]

    def layer_of(l, w4, rest):
        return dict(
            norm_g=norm_g[l][None], shift=mod_me[l, 0:d][None], scale=mod_me[l, d:2 * d][None],
            gate=mod_me[l, 2 * d:3 * d][None], w4=w4.reshape(4, d, n_sh),
            rg_conv_w=conv_full[0, l], rg_conv_b=rg_conv_b[l][None], rg_wa_b=_bf(rg_w_a[l]), rg_ba=rg_b_a[l][None],
            rg_wx_b=_bf(rg_w_x[l]), rg_bx=rg_b_x[l][None], rg_lam=rg_lambda[l][None],
            ml_conv_w=conv_full[1, l], ml_conv_b=ml_conv_b[l][None], b_if=ml_b_if[l][None], b_ift=ml_b_if[l][:, None],
            ml_g=ml_norm_g[l][None], **rest)

    def rest_of(gathered):
        w_out_b, wqkv_g, wif = gathered
        return dict(w_out_b=w_out_b, wqkv_b=_from_pieces(wqkv_g.reshape(8, -1), (3, heads, hd, hd), 2), wif_b=wif,
                    wift_b=wif.T)

    spread = lambda blocks: [Leg(b, "spread") for b in blocks]
    fill = lambda landed: [Leg(t, "sib_fill") for t in landed]
    flat = lambda filled: [t.reshape(-1, t.shape[-1]) for t in filled]
    first = blocks_of(0)
    n_rest = len(first) - 1
    p = layer_of(0, w_in_first, {})
    layers, saved = [], []
    xl = x[0]
    for l in range(depth):
        nxt = blocks_of(l + 1) if l + 1 < depth else []
        skip = n_rest if l == 0 else 0
        rides = dict(rg_fwd=lambda landed, nxt=nxt: spread(nxt[:1]))
        if l == 0:
            rides.update(ln_inproj=lambda landed: spread(first[1:]),
                         rg_fwd=lambda landed, nxt=nxt: fill(landed["ln_inproj"]) + spread(nxt[:1]),
                         late=lambda landed: rest_of(flat(landed["rg_fwd"][:n_rest])))
        if nxt:
            rides.update(mlstm_fwd=lambda landed, nxt=nxt: spread(nxt[1:]),
                         out_proj=lambda landed, skip=skip: fill(list(landed["rg_fwd"][skip:]) + list(landed["mlstm_fwd"])))
        xl, s, p, landed = _layer_fwd(xl, p, rides, None if nxt else (final_g[None], loss_target[0]))
        layers.append(p)
        saved.append(s)
        if nxt:
            arrived = flat(landed["out_proj"])
            p = layer_of(l + 1, arrived[0], rest_of(arrived[1:]))
    dx, g_final, loss = xl

    half = ac.reshape(1)
    ids = jnp.stack([chip, ac])
    r_out = w_out.shape[1] // 2

    def pair_in(g_w_in, got_in):
        return _pair_sum(
            half, g_w_in, pl.BlockSpec((None, d // 2, n_sh), lambda s, h: (0, h[0], s)),
            got_in, pl.BlockSpec((None, None, d // 2, n_sh), lambda s, h: (0, s, 0, 0)),
            jax.ShapeDtypeStruct((4, 1, d // 2, n_sh), BF16),
            pl.BlockSpec((None, None, d // 2, n_sh), lambda s, h: (s, 0, 0, 0)), (4,))

    def pair_out(g_out5, got_out):
        return _pair_sum(
            half, g_out5, pl.BlockSpec((None, None, None, r_out, d), lambda s, h: (0, s, h[0], 0, 0)),
            got_out, pl.BlockSpec((None, None, r_out, d), lambda s, h: (0, s, 0, 0)),
            jax.ShapeDtypeStruct((4, 1, r_out, d), BF16),
            pl.BlockSpec((None, None, r_out, d), lambda s, h: (s, 0, 0, 0)), (4,))

    def pair_slab(slab, got, dtype):
        rows = got.shape[0] // 4
        blk = pl.BlockSpec((rows, LANES), lambda s, h: (s, 0))
        return _pair_sum(half, slab, pl.BlockSpec((None, rows, LANES), lambda s, h: (h[0], s, 0)), got, blk,
                         jax.ShapeDtypeStruct((4 * rows, LANES), dtype), blk, (4,)).reshape(4, 1, rows, LANES)

    row_pad = lambda n: -(-n // (8 * LANES)) * (8 * LANES)

    def as_rows(t):
        if t.shape[-1] == LANES and t.size % (8 * LANES) == 0:
            return t.reshape(-1, LANES)
        return _pad_rows(t.reshape(-1), 8 * LANES).reshape(-1, LANES)

    chips = lambda arrs: [Leg(a, "chips") for a in arrs]
    out5 = lambda g: g["w_out"].reshape(1, 4, 2, r_out, d)
    r_q = hd // 8
    qkv5 = lambda g: g["ml_w_qkv"].reshape(3 * heads, 4, 2, r_q, hd)

    def pair_qkv(g5, got):
        return _pair_sum(
            half, g5, pl.BlockSpec((3 * heads, None, None, r_q, hd), lambda s, h: (0, s, h[0], 0, 0)),
            got, pl.BlockSpec((3 * heads, None, r_q, hd), lambda s, h: (0, s, 0, 0)),
            jax.ShapeDtypeStruct((4, 1, 3 * heads, r_q, hd), BF16),
            pl.BlockSpec((None, None, 3 * heads, r_q, hd), lambda s, h: (s, 0, 0, 0, 0)), (4,))

    grads, dmods, parts, mets = [None] * depth, [None] * depth, [None] * depth, [None] * depth
    small = {}

    def early_exchange(g, landed):
        every = [g] + grads[1:]
        sm = jnp.concatenate([_to_pieces(every[l][name], axis) for l in range(depth)
                              for name, axis in SMALL_SHARDED.items()], axis=-1)
        sm = _pad_rows(sm, 16 * LANES)
        sm = sm.transpose(1, 0, 2).reshape(2, -1, LANES)
        rep = [as_rows(every[l][name]) for l in range(depth) for name in REPLICATED]
        rep = jnp.concatenate(rep + [as_rows(g_final), as_rows(loss)], axis=0)
        rep = jnp.concatenate([rep, jnp.zeros(((-rep.shape[0]) % 64, LANES), F32)], axis=0)
        rep = rep.reshape(4, 2, -1, LANES).transpose(1, 0, 2, 3).reshape(2, -1, LANES)
        got_sm, got_rep, got_q = _exchange([Leg(sm, "sib_slab"), Leg(rep, "sib_slab"), Leg(qkv5(g), "sib_w_out")])
        small["parts"] = [pair_out(out5(g), landed["mlstm_bwd"][0]), pair_slab(sm, got_sm, BF16),
                          pair_slab(rep, got_rep, F32), pair_qkv(qkv5(g), got_q)]
        return chips(small["parts"])

    def last_exchange(g, landed):
        (got_in,) = _exchange([Leg(g["w_in"], "sib_w_in")])
        small["part_in"] = pair_in(g["w_in"], got_in)
        return chips([small["part_in"]])

    for l in reversed(range(depth)):
        above = parts[l + 1] if l + 1 < depth else []
        rides = dict(mlstm_bwd=lambda g, landed, above=above: [Leg(out5(g), "sib_w_out")] + chips(above),
                     in_bwd=lambda g, landed: [Leg(g["w_in"], "sib_w_in"), Leg(qkv5(g), "sib_w_out")])
        if l == 0:
            rides.update(grad_w_in=early_exchange, in_bwd=last_exchange)
        dx, grads[l], dmods[l], got = _layer_bwd(dx, layers[l], saved[l], rides)
        if above:
            mets[l + 1] = got["mlstm_bwd"][1:]
        if l > 0:
            parts[l] = [pair_in(grads[l]["w_in"], got["in_bwd"][0]), pair_out(out5(grads[l]), got["mlstm_bwd"][0]),
                        pair_qkv(qkv5(grads[l]), got["in_bwd"][1])]
    part_out, part_sm, part_rep, part_q = small["parts"]
    met_out, met_sm, met_rep, met_q = got["grad_w_in"]
    parts[0], mets[0] = [small["part_in"], part_out, part_q], [got["in_bwd"][0], met_out, met_q]
    n_rep = part_rep.shape[2]

    pad = lambda t: jnp.concatenate([t, jnp.zeros((1, 2 * d), F32)], axis=1)
    rows = [r for l in range(depth) for r in (dmods[l], pad(grads[l]["norm_g"]))]
    blk = jnp.concatenate(rows + [jnp.zeros((8 - 2 * depth, 3 * d), F32)], axis=0)
    rows_all = _all_gather8([blk], pltpu.VMEM)[0].reshape(8, 8, 3 * d)[:, :2 * depth]
    rows_all = rows_all.transpose(1, 0, 2).reshape(depth, 2, 8, 3 * d)
    dm_cols = lax.dynamic_slice_in_dim(rows_all[:, 0], chip * n_ada, n_ada, axis=2)
    g_w_ada, summed = _ada_grad(c_all, dm_cols, rows_all)

    g = dict(w_ada=g_w_ada, b_ada=summed[:, 0, 0], norm_g=summed[:, 1, 0, :d])
    item = lambda name: (given[name], g[name], given["m_" + name], given["v_" + name])
    both_in, both_out, both_q = depth, depth, depth
    flat_q = lambda t: t.reshape(4, 1, 3 * heads * r_q, hd)
    for l in range(depth):
        both_in = _chip_sum(ids, parts[l][0], mets[l][0], True, l, both_in)
        both_out = _chip_sum(ids, parts[l][1], mets[l][1], True, l, both_out)
        both_q = _chip_sum(ids, flat_q(parts[l][2]), flat_q(mets[l][2]), True, l, both_q)
    both_in, both_out, both_q, both_sm = _exchange(fill([both_in, both_out, both_q,
                                                         _chip_sum(ids, part_sm, met_sm, True)]))
    red_rep = _chip_sum(ids, part_rep, met_rep, False).reshape(n_rep, LANES)
    rep_all = _all_gather8([red_rep], pltpu.VMEM)[0].reshape(-1)

    g.update(w_in=both_in.reshape(w_in.shape), w_out=both_out.reshape(w_out.shape))
    g_qkv = both_q.reshape(depth, 2, 3, heads, r_q, hd).transpose(0, 2, 3, 1, 4, 5)
    g_qkv = g_qkv.reshape(depth, 3, heads, 2 * r_q, hd)
    for i, name in enumerate(["ml_w_q", "ml_w_k", "ml_w_v"]):
        g[name] = g_qkv[:, i]
    shard = both_sm.reshape(2, -1)
    off = 0
    per_layer = {name: [] for name in SMALL_SHARDED}
    for l in range(depth):
        for name, axis in SMALL_SHARDED.items():
            n = grads[l][name].size // 8
            per_layer[name].append(_from_pieces(shard[:, off:off + n], given[name].shape[1:], axis))
            off += n
    for name in SMALL_SHARDED:
        g[name] = jnp.stack(per_layer[name])
    off = 0
    per_layer = {name: [] for name in REPLICATED}
    for l in range(depth):
        for name in REPLICATED:
            n = given[name][l].size
            per_layer[name].append(rep_all[off:off + n].reshape(given[name].shape[1:]))
            off += row_pad(n)
    for name in REPLICATED:
        g[name] = jnp.stack(per_layer[name])
    g["final_g"] = rep_all[off:off + d]
    loss_all = rep_all[off + row_pad(d)]

    stepped = {}
    rg_mats, ml_mats = ["rg_w_a", "rg_w_x"], ["ml_w_q", "ml_w_k", "ml_w_v"]
    vectors = [n for n in WEIGHTS if n not in ["w_ada", "w_in", "w_out"] + rg_mats + ml_mats]
    for names in (["w_ada"], ["w_in"], ["w_out"], rg_mats, ml_mats, vectors):
        stepped.update(zip(names, _adamw([item(name) for name in names])[0]))
    deltas, new_m, new_v = zip(*[stepped[name] for name in WEIGHTS])
    return (loss_all, dx[None], *[g[name] for name in WEIGHTS], *deltas, *new_m, *new_v)
```

```python
import functools
from typing import NamedTuple

import jax
import jax.numpy as jnp
from jax import lax
from jax.experimental import pallas as pl
from jax.experimental.pallas import tpu as pltpu

F32 = jnp.float32
BF16 = jnp.bfloat16

EPS = 1e-6
RG_C = 8.0
CONV_WIDTH = 4
ML_CHUNK = 512
HALO = 8
ROWS_VECTOR = 512
ROWS_MATMUL = 1024
ROWS_IN_BWD = 512
ROWS_GRAD_MATMUL = 2048
BLOCK_ELEMS = 1 << 18
RELAY_BYTES = 1 << 18
ADAM_LR = 0.001
ADAM_B1 = 0.9
ADAM_B2 = 0.999
ADAM_EPS = 1e-08
ADAM_WD = 0.01
ADAM_STEP = 10
MESH = pl.DeviceIdType.MESH


def _pcall(body, **kw):
    return pl.pallas_call(body, **kw)


class Leg(NamedTuple):
    src: jax.Array
    kind: str

    def landing(self):
        a = self.src
        shape = {"chips": lambda: a.shape, "spread": lambda: (4, 2) + a.shape, "sib_fill": lambda: a.shape,
                 "sib_w_in": lambda: (a.shape[0], 4, a.shape[1] // 2, a.shape[2] // 4),
                 "sib_w_out": lambda: a.shape[:2] + a.shape[3:], "sib_slab": lambda: a.shape[1:]}[self.kind]()
        return jax.ShapeDtypeStruct(shape, a.dtype)

    def copies(self, src, dst, x, y, c):
        a, me_s, o = self.src, 2 * x + y, 1 - c
        chips = [(1 - x, y), (x, 1 - y), (1 - x, 1 - y)]
        if self.kind == "chips":
            return [(src.at[2 * px + py], dst.at[me_s], (px, py, c)) for px, py in chips], []
        if self.kind == "spread":
            return [(src, dst.at[me_s, c], (px, py, c)) for px, py in chips], [(src, dst.at[me_s, c])]
        depth = pl.ds(0, a.shape[0])
        if self.kind == "sib_fill":
            return [(dst.at[depth, c], dst.at[depth, c], (x, y, o))], []
        if self.kind == "sib_w_in":
            half, n = a.shape[1] // 2, a.shape[2] // 4
            return [(src.at[depth, pl.ds(o * half, half), pl.ds(s * n, n)], dst.at[depth, s], (x, y, o))
                    for s in range(4)], []
        if self.kind == "sib_w_out":
            return [(src.at[depth, pl.ds(0, 4), o], dst, (x, y, o))], []
        return [(src.at[o], dst, (x, y, o))], []

    def n_copies(self):
        return {"chips": 3, "spread": 3, "sib_w_in": 4}.get(self.kind, 1)


def _exchange_body(legs, srcs, dsts, send_sems, recv_sems, local_sems):
    x, y, c = _me()
    remote, local, k = [], [], 0
    for i, leg in enumerate(legs):
        far, near = leg.copies(srcs[i], dsts[i], x, y, c)
        for src, dst, to in far:
            remote.append(_remote(src, dst, send_sems.at[k], recv_sems.at[k], to))
            k += 1
        local += [pltpu.make_async_copy(src, dst, local_sems.at[i]) for src, dst in near]
    return remote, local


def _exchange_sems(legs):
    n = sum(leg.n_copies() for leg in legs)
    return [pltpu.SemaphoreType.DMA((n,)), pltpu.SemaphoreType.DMA((n,)), pltpu.SemaphoreType.DMA((len(legs),))]


def _exchange_aliases(legs, n_in, n_out):
    return {n_in + i: n_out + i for i, leg in enumerate(legs) if leg.kind == "sib_fill"}


def _pcall_ride(body, ride, *, grid, in_specs, out_specs, out_shape, args, scratch_shapes=(), **kw):
    n_in, n_out, n_scr = len(in_specs), len(out_specs), len(scratch_shapes)
    if not ride:
        res = _pcall(body, grid=grid, in_specs=in_specs, out_specs=out_specs, out_shape=out_shape,
                     scratch_shapes=list(scratch_shapes), **kw)(*args)
        return res, []
    nr = len(ride)

    def riding(*refs):
        ins, rsrc = refs[:n_in], refs[n_in:n_in + nr]
        outs, rdst = refs[n_in + nr:n_in + nr + n_out], refs[n_in + nr + n_out:n_in + 2 * nr + n_out]
        scr = refs[n_in + 2 * nr + n_out:n_in + 2 * nr + n_out + n_scr]
        copies, local = _exchange_body(ride, rsrc, rdst, *refs[n_in + 2 * nr + n_out + n_scr:])
        first = functools.reduce(jnp.logical_and, [pl.program_id(a) == 0 for a in range(len(grid))])
        last = functools.reduce(jnp.logical_and, [pl.program_id(a) == grid[a] - 1 for a in range(len(grid))])

        @pl.when(first)
        def _():
            for cp in copies + local:
                cp.start()

        body(*ins, *outs, *scr)

        @pl.when(last)
        def _():
            for cp in copies:
                cp.wait_recv()
            for cp in copies:
                cp.wait_send()
            for cp in local:
                cp.wait()

    hbm = pl.BlockSpec(memory_space=pltpu.HBM)
    aliases = {**kw.pop("input_output_aliases", {}), **_exchange_aliases(ride, n_in, n_out)}
    res = _pcall(
        riding, grid=grid, in_specs=list(in_specs) + [hbm] * nr, out_specs=list(out_specs) + [hbm] * nr,
        out_shape=list(out_shape) + [leg.landing() for leg in ride], input_output_aliases=aliases,
        scratch_shapes=list(scratch_shapes) + _exchange_sems(ride), **kw)(*args, *[leg.src for leg in ride])
    return res[:n_out], res[n_out:]


def _seq(n=1):
    return pltpu.CompilerParams(dimension_semantics=("arbitrary",) * n)


def _dot(a, b):
    return jnp.dot(a, b, preferred_element_type=F32)


def _dot_nt(a, b):
    return lax.dot_general(a, b, (((1,), (1,)), ((), ())), preferred_element_type=F32)


def _dot_tn(a, b):
    return lax.dot_general(a, b, (((0,), (0,)), ((), ())), preferred_element_type=F32)


def _bf(x):
    return x.astype(BF16)


def _sigmoid(x):
    return 0.5 * jnp.tanh(0.5 * x) + 0.5


def _log1p(z):
    u = 1.0 + z
    return jnp.where(u == 1.0, z, jnp.log(u) * (z / jnp.where(u == 1.0, 1.0, u - 1.0)))


def _softplus(x):
    return jnp.maximum(x, 0.0) + _log1p(jnp.exp(-jnp.abs(x)))


def _log_sigmoid(x):
    return -_softplus(-x)


def _one_minus_sq(a, log_a):
    x = 2.0 * log_a
    small = -x * (1.0 + x * (0.5 + x * (1.0 / 6.0)))
    return jnp.where(x > -0.004, small, 1.0 - a * a)


def _dsilu(x, s):
    return s * (1.0 + x * (1.0 - s))


def _rowsum(x):
    return jnp.sum(x, axis=1, keepdims=True)


def _colsum(x):
    return jnp.sum(x, axis=0, keepdims=True)


def _shift_down(win, s):
    return win if s == 0 else pltpu.roll(win, s, 0)


def _shift_up(win, s):
    return win if s == 0 else pltpu.roll(win, win.shape[0] - s, 0)


def _conv_taps(win):
    return [_shift_down(win, CONV_WIDTH - 1 - k)[HALO:] for k in range(CONV_WIDTH)]


def _conv_fwd(taps, w_ref, b_ref):
    acc = b_ref[...] + w_ref[CONV_WIDTH - 1:CONV_WIDTH, :] * taps[CONV_WIDTH - 1]
    for k in range(CONV_WIDTH - 1):
        acc = acc + w_ref[k:k + 1, :] * taps[k]
    return acc


def _split3(x):
    hi = _bf(x)
    r1 = x - hi.astype(F32)
    mid = _bf(r1)
    lo = _bf(r1 - mid.astype(F32))
    return hi, mid, lo


def _tri_dot_left(tri, x):
    hi, mid, lo = _split3(x)
    return _dot(tri, hi) + _dot(tri, mid) + _dot(tri, lo)


def _tri_dot_right(x, tri):
    hi, mid, lo = _split3(x)
    return _dot(hi, tri) + _dot(mid, tri) + _dot(lo, tri)


def _tile(n, want):
    t = min(n, want)
    assert n % t == 0
    return t


def _ln_inproj(x, g, scale, shift, w4, ride=None):
    s_len, d = x.shape
    nj, _, nsh = w4.shape
    tm = _tile(s_len, ROWS_MATMUL)

    def body(x_ref, g_ref, sc_ref, sh_ref, w_ref, h_ref, u_ref, hs):
        @pl.when(pl.program_id(1) == 0)
        def _():
            xv = x_ref[...]
            r = lax.rsqrt(jnp.mean(xv * xv, axis=-1, keepdims=True) + EPS)
            hv = (xv * r * g_ref[...]) * (1.0 + sc_ref[...]) + sh_ref[...]
            hs[...] = _bf(hv)
            h_ref[...] = hs[...]

        u_ref[...] = _dot(hs[...], w_ref[0])

    vec = pl.BlockSpec((1, d), lambda i, j: (0, 0))
    return _pcall_ride(
        body, ride, name="ln_inproj", grid=(s_len // tm, nj),
        in_specs=[pl.BlockSpec((tm, d), lambda i, j: (i, 0)), vec, vec, vec,
                  pl.BlockSpec((1, d, nsh), lambda i, j: (j, 0, 0))],
        out_specs=[pl.BlockSpec((tm, d), lambda i, j: (i, 0)), pl.BlockSpec((tm, nsh), lambda i, j: (i, j))],
        out_shape=[jax.ShapeDtypeStruct((s_len, d), BF16), jax.ShapeDtypeStruct((s_len, nj * nsh), F32)],
        scratch_shapes=[pltpu.VMEM((tm, d), BF16)],
        compiler_params=_seq(2),
        args=(x, g, scale, shift, w4))


def _rg_gates(xc, wa_ref, ba_ref, wx_ref, bx_ref, lam_ref):
    heads, hd, _ = wa_ref.shape
    xb = _bf(xc)
    ga = jnp.concatenate([_dot(xb[:, h * hd:(h + 1) * hd], wa_ref[h]) for h in range(heads)], axis=1) + ba_ref[...]
    gx = jnp.concatenate([_dot(xb[:, h * hd:(h + 1) * hd], wx_ref[h]) for h in range(heads)], axis=1) + bx_ref[...]
    r = _sigmoid(ga)
    ig = _sigmoid(gx)
    sp = _softplus(-lam_ref[...])
    log_a = (-RG_C) * r * sp
    a = jnp.exp(log_a)
    mult = jnp.sqrt(_one_minus_sq(a, log_a))
    return r, ig, sp, log_a, a, mult


def _scan_groups(a, u, reverse):
    n, c = a.shape
    a = a.reshape(n // 8, 8, c)
    u = u.reshape(n // 8, 8, c)
    row = lax.broadcasted_iota(jnp.int32, a.shape, 1)
    for k in (1, 2, 4):
        sft = 8 - k if reverse else k
        a_sh, u_sh = pltpu.roll(a, sft, 1), pltpu.roll(u, sft, 1)
        ok = row < 8 - k if reverse else row >= k
        u = jnp.where(ok, a * u_sh + u, u)
        a = jnp.where(ok, a * a_sh, a)
    return a.reshape(n, c), u.reshape(n, c)


def _rg_fwd(u, conv_w, conv_b, wa_b, ba, wx_b, bx, lam, ride=None):
    s_len = u.shape[0]
    d = conv_w.shape[1]
    tm = _tile(s_len, ROWS_VECTOR)
    per = tm // HALO

    def body(x_ref, xp_ref, z_ref, cw_ref, cb_ref, wa_ref, ba_ref, wx_ref, bx_ref, lam_ref,
             hh_ref, y_ref, carry):
        i = pl.program_id(0)

        @pl.when(i == 0)
        def _():
            carry[...] = jnp.zeros_like(carry)

        prev = jnp.where(i == 0, 0.0, xp_ref[...])
        xc = _conv_fwd(_conv_taps(jnp.concatenate([prev, x_ref[...]], axis=0)), cw_ref, cb_ref)
        _, ig, _, _, a, mult = _rg_gates(xc, wa_ref, ba_ref, wx_ref, bx_ref, lam_ref)
        ca, cu = _scan_groups(a, mult * (ig * xc), reverse=False)
        c = carry[0:1, :]
        for j in range(tm // 8):
            blk = ca[j * 8:(j + 1) * 8] * c + cu[j * 8:(j + 1) * 8]
            hh_ref[j * 8:(j + 1) * 8, :] = blk
            c = blk[7:8]
        carry[0:1, :] = c
        z = z_ref[...]
        y_ref[0] = _bf(hh_ref[...] * (z * _sigmoid(z)))

    vec = pl.BlockSpec((1, d), lambda i: (0, 0))
    whole3 = lambda a: pl.BlockSpec(a.shape, lambda i: (0, 0, 0))
    return _pcall_ride(
        body, ride, name="rg_fwd", grid=(s_len // tm,),
        in_specs=[pl.BlockSpec((tm, d), lambda i: (i, 0)),
                  pl.BlockSpec((HALO, d), lambda i: (jnp.maximum(i * per - 1, 0), 0)),
                  pl.BlockSpec((tm, d), lambda i: (i, 1)),
                  pl.BlockSpec((CONV_WIDTH, d), lambda i: (0, 0)), vec,
                  whole3(wa_b), vec, whole3(wx_b), vec, vec],
        out_specs=[pl.BlockSpec((tm, d), lambda i: (i, 0)), pl.BlockSpec((1, tm, d), lambda i: (0, i, 0))],
        out_shape=[jax.ShapeDtypeStruct((s_len, d), F32), jax.ShapeDtypeStruct((2, s_len, d), BF16)],
        scratch_shapes=[pltpu.VMEM((8, d), F32)],
        compiler_params=_seq(),
        args=(u, u, u, conv_w, conv_b, wa_b, ba, wx_b, bx, lam))


def _ml_pre(u, conv_w, conv_b, wqkv_b, wif_b, wift_b, b_if, b_ift):
    s_len = u.shape[0]
    d = conv_w.shape[1]
    _, heads, hd, _ = wqkv_b.shape
    ng = 2 * heads
    tm = _tile(s_len, max(ROWS_VECTOR, ML_CHUNK))
    per = tm // HALO

    def body(x_ref, xp_ref, cw_ref, cb_ref, w_ref, wif_ref, wift_ref, bif_ref, bift_ref,
             qkv_ref, gt_ref, gtt_ref, bc_ref, bct_ref):
        i = pl.program_id(0)
        prev = jnp.where(i == 0, 0.0, xp_ref[...])
        xm = x_ref[...]
        pre = _conv_fwd(_conv_taps(jnp.concatenate([prev, xm], axis=0)), cw_ref, cb_ref)
        xcb = _bf(pre * _sigmoid(pre))
        xmb = _bf(xm)
        for h in range(heads):
            hs = slice(h * hd, (h + 1) * hd)
            qkv_ref[0, :, hs] = _bf(_dot(xcb[:, hs], w_ref[0, h]))
            qkv_ref[1, :, hs] = _bf(_dot(xcb[:, hs], w_ref[1, h]))
            qkv_ref[2, :, hs] = _bf(_dot(xmb[:, hs], w_ref[2, h]))
        qb, kb, vb = qkv_ref[0], qkv_ref[1], qkv_ref[2]
        gt = (_dot(qb, wif_ref[0:d, :]) + _dot(kb, wif_ref[d:2 * d, :]) + _dot(vb, wif_ref[2 * d:3 * d, :])
              + bif_ref[...])
        gtt = (_dot_nt(wift_ref[:, 0:d], qb) + _dot_nt(wift_ref[:, d:2 * d], kb)
               + _dot_nt(wift_ref[:, 2 * d:3 * d], vb) + bift_ref[...])
        gt_ref[...] = gt
        gtt_ref[...] = gtt
        r = lax.broadcasted_iota(jnp.int32, (tm, tm), 0)
        c = lax.broadcasted_iota(jnp.int32, (tm, tm), 1)
        same = (r // ML_CHUNK) == (c // ML_CHUNK)
        bc_ref[...] = _tri_dot_left(((r >= c) & same).astype(BF16), _log_sigmoid(gt))
        bct_ref[...] = _tri_dot_right(_log_sigmoid(gtt), ((r <= c) & same).astype(BF16))

    vec = pl.BlockSpec((1, d), lambda i: (0, 0))
    whole2 = lambda a: pl.BlockSpec(a.shape, lambda i: (0, 0))
    col = pl.BlockSpec((tm, ng), lambda i: (i, 0))
    row = pl.BlockSpec((ng, tm), lambda i: (0, i))
    return _pcall(
        body, name="ml_pre", grid=(s_len // tm,),
        in_specs=[pl.BlockSpec((tm, d), lambda i: (i, 2)),
                  pl.BlockSpec((HALO, d), lambda i: (jnp.maximum(i * per - 1, 0), 2)),
                  pl.BlockSpec((CONV_WIDTH, d), lambda i: (0, 0)), vec,
                  pl.BlockSpec(wqkv_b.shape, lambda i: (0, 0, 0, 0)), whole2(wif_b), whole2(wift_b), whole2(b_if),
                  whole2(b_ift)],
        out_specs=[pl.BlockSpec((3, tm, d), lambda i: (0, i, 0)), col, row, col, row],
        out_shape=[jax.ShapeDtypeStruct((3, s_len, d), BF16), jax.ShapeDtypeStruct((s_len, ng), F32),
                   jax.ShapeDtypeStruct((ng, s_len), F32), jax.ShapeDtypeStruct((s_len, ng), F32),
                   jax.ShapeDtypeStruct((ng, s_len), F32)],
        compiler_params=_seq(),
    )(u, u, conv_w, conv_b, wqkv_b, wif_b, wift_b, b_if, b_ift)


def _chunk_gates(gt, gtt, bc, bct, h, heads):
    li_c = gt[:, h:h + 1]
    li_r = gtt[h:h + 1, :]
    gf_c = gt[:, heads + h:heads + h + 1]
    b_c = bc[:, heads + h:heads + h + 1]
    b_r = bct[heads + h:heads + h + 1, :]
    return li_c, li_r, gf_c, b_c, b_r


def _chunk_weights(li_c, li_r, b_c, b_r, m_prev, causal):
    lc = b_c.shape[0]
    b_last = b_c[lc - 1:lc, :]
    dmat = jnp.where(causal, b_c - b_r + li_r, -jnp.inf)
    m_inter = b_c + m_prev
    m_t = jnp.maximum(m_inter, jnp.max(dmat, axis=1, keepdims=True))
    w_intra = jnp.exp(dmat - m_t)
    w_inter = jnp.exp(m_inter - m_t)
    g_c = b_last - b_c + li_c
    m_new = jnp.maximum(b_last + m_prev, jnp.max(g_c, axis=0, keepdims=True))
    w_state = jnp.exp(g_c - m_new)
    decay = jnp.exp(b_last + m_prev - m_new)
    return m_t, w_intra, w_inter, m_new, w_state, decay


def _tri_masks(lc):
    r = lax.broadcasted_iota(jnp.int32, (lc, lc), 0)
    c = lax.broadcasted_iota(jnp.int32, (lc, lc), 1)
    causal = r >= c
    return causal, causal.astype(BF16), (r <= c).astype(BF16)


def _mlstm_fwd(qkv, gates, u, ml_g, ycat, ride=None):
    _, s_len, d = qkv.shape
    ng = gates[0].shape[1]
    heads = ng // 2
    hd = d // heads
    lc = ML_CHUNK
    nc = s_len // lc
    kscale = hd ** -0.5

    def body(qkv_ref, gt_ref, gtt_ref, bc_ref, bct_ref, o_ref, z_ref, g_ref, _, cell_ref, y_ref, cst_ref, nst_ref,
             mst_ref, cs, ns, ms):
        @pl.when(pl.program_id(0) == 0)
        def _():
            cs[...] = jnp.zeros_like(cs)
            ns[...] = jnp.zeros_like(ns)
            ms[...] = jnp.zeros_like(ms)

        causal = _tri_masks(lc)[0]
        gtv, gttv, bcv, bctv = gt_ref[...], gtt_ref[...], bc_ref[...], bct_ref[...]
        old = [(cs[h], ns[h], ms[h]) for h in range(heads)]
        new, cells, ys = [], [], []
        for h in range(heads):
            hs = slice(h * hd, (h + 1) * hd)
            li_c, li_r, _, b_c, b_r = _chunk_gates(gtv, gttv, bcv, bctv, h, heads)
            c_old, n_old, m_old = old[h]
            m_prev = m_old[:, 0:1]
            m_t, w_intra, w_inter, m_new, w_state, decay = _chunk_weights(li_c, li_r, b_c, b_r, m_prev, causal)
            qb = qkv_ref[0, :, hs]
            ks = qkv_ref[1, :, hs].astype(F32) * kscale
            kb = _bf(ks)
            vb = qkv_ref[2, :, hs]
            s = _dot_nt(qb, kb) * w_intra
            num = _dot(_bf(s), vb) + w_inter * _dot(qb, _bf(c_old))
            den = _rowsum(s) + w_inter * _rowsum(qb.astype(F32) * n_old)
            cell = num / jnp.maximum(jnp.abs(den), jnp.exp(-m_t))
            kw = ks * w_state
            new.append((decay * c_old + _dot_tn(_bf(kw), vb), decay * n_old + _colsum(kw),
                        jnp.broadcast_to(m_new, m_old.shape)))
            cells.append(cell)
            hm = _sigmoid(o_ref[:, hs]) * cell
            hn = hm * lax.rsqrt(jnp.mean(hm * hm, axis=-1, keepdims=True) + EPS)
            z = z_ref[:, hs]
            ys.append(_bf((hn * g_ref[:, hs]) * (z * _sigmoid(z))))
        for h in range(heads):
            cst_ref[0, h] = _bf(old[h][0])
            nst_ref[0, h] = old[h][1]
            mst_ref[0, h] = old[h][2]
            cs[h], ns[h], ms[h] = new[h]
        cell_ref[...] = jnp.concatenate(cells, axis=1)
        y_ref[0] = jnp.concatenate(ys, axis=1)

    row = pl.BlockSpec((lc, d), lambda c: (c, 0))
    gcol = pl.BlockSpec((lc, ng), lambda c: (c, 0))
    grow = pl.BlockSpec((ng, lc), lambda c: (0, c))
    return _pcall_ride(
        body, ride, name="mlstm_fwd", grid=(nc,),
        in_specs=[pl.BlockSpec((3, lc, d), lambda c: (0, c, 0)), gcol, grow, gcol, grow,
                  pl.BlockSpec((lc, d), lambda c: (c, 3)), pl.BlockSpec((lc, d), lambda c: (c, 4)),
                  pl.BlockSpec((1, d), lambda c: (0, 0)), pl.BlockSpec(memory_space=pl.ANY)],
        out_specs=[row, pl.BlockSpec((1, lc, d), lambda c: (1, c, 0)),
                   pl.BlockSpec((1, heads, hd, hd), lambda c: (c, 0, 0, 0)),
                   pl.BlockSpec((1, heads, 1, hd), lambda c: (c, 0, 0, 0)),
                   pl.BlockSpec((1, heads, 1, 128), lambda c: (c, 0, 0, 0))],
        out_shape=[jax.ShapeDtypeStruct((s_len, d), F32), jax.ShapeDtypeStruct(ycat.shape, BF16),
                   jax.ShapeDtypeStruct((nc, heads, hd, hd), BF16),
                   jax.ShapeDtypeStruct((nc, heads, 1, hd), F32),
                   jax.ShapeDtypeStruct((nc, heads, 1, 128), F32)],
        scratch_shapes=[pltpu.VMEM((heads, hd, hd), F32), pltpu.VMEM((heads, 1, hd), F32),
                        pltpu.VMEM((heads, 1, 128), F32)],
        input_output_aliases={8: 1},
        compiler_params=_seq(),
        args=(qkv, *gates, u, u, ml_g, ycat))


def _out_proj(ycat, w_out_b, x, gate, ride=None):
    s_len, d = x.shape
    tm = _tile(s_len, ROWS_MATMUL)

    def body(a_ref, w_ref, x_ref, g_ref, y_ref, xn_ref):
        y = _dot(a_ref[0], w_ref[0:d, :]) + _dot(a_ref[1], w_ref[d:2 * d, :])
        y_ref[...] = y
        xn_ref[...] = x_ref[...] + g_ref[...] * y

    row = pl.BlockSpec((tm, d), lambda i: (i, 0))
    return _pcall_ride(
        body, ride, name="out_proj", grid=(s_len // tm,),
        in_specs=[pl.BlockSpec((2, tm, d), lambda i: (0, i, 0)), pl.BlockSpec((2 * d, d), lambda i: (0, 0)), row,
                  pl.BlockSpec((1, d), lambda i: (0, 0))],
        out_specs=[row, row],
        out_shape=[jax.ShapeDtypeStruct((s_len, d), F32)] * 2,
        compiler_params=_seq(),
        args=(ycat, w_out_b, x, gate))


def _out_proj_loss(ycat, w_out_b, x, gate, g, target):
    s_len, d = x.shape
    tm = _tile(s_len, ROWS_IN_BWD)

    def body(a_ref, w_ref, x_ref, gate_ref, g_ref, t_ref, y_ref, dx_ref, dg_ref, loss_ref):
        @pl.when(pl.program_id(0) == 0)
        def _():
            dg_ref[...] = jnp.zeros_like(dg_ref)
            loss_ref[...] = jnp.zeros_like(loss_ref)

        y = _dot(a_ref[0], w_ref[0:d, :]) + _dot(a_ref[1], w_ref[d:2 * d, :])
        y_ref[...] = y
        xv = x_ref[...] + gate_ref[...] * y
        r = lax.rsqrt(jnp.mean(xv * xv, axis=-1, keepdims=True) + EPS)
        xn = xv * r
        err = xn * g_ref[...] - t_ref[...]
        loss_ref[...] += 0.5 * jnp.sum(jnp.mean(err * err, axis=-1, keepdims=True))
        dout = err * (1.0 / d)
        dg_ref[...] += _colsum(dout * xn)
        dxn = dout * g_ref[...]
        dx_ref[...] = r * (dxn - xn * jnp.mean(dxn * xn, axis=-1, keepdims=True))

    row = pl.BlockSpec((tm, d), lambda i: (i, 0))
    vec = pl.BlockSpec((1, d), lambda i: (0, 0))
    return _pcall(
        body, name="out_proj_loss", grid=(s_len // tm,),
        in_specs=[pl.BlockSpec((2, tm, d), lambda i: (0, i, 0)), pl.BlockSpec((2 * d, d), lambda i: (0, 0)), row, vec,
                  vec, row],
        out_specs=[row, row, vec, pl.BlockSpec((1, 128), lambda i: (0, 0))],
        out_shape=[jax.ShapeDtypeStruct((s_len, d), F32), jax.ShapeDtypeStruct((s_len, d), F32),
                   jax.ShapeDtypeStruct((1, d), F32), jax.ShapeDtypeStruct((1, 128), F32)],
        compiler_params=_seq(),
    )(ycat, w_out_b, x, gate, g, target)


def _out_bwd(dxn, y, gate, w_out_b):
    s_len, d = dxn.shape
    tm = _tile(s_len, ROWS_MATMUL)

    def body(dx_ref, y_ref, g_ref, w_ref, dg_ref, dy_ref, dc_ref):
        @pl.when(pl.program_id(0) == 0)
        def _():
            dg_ref[...] = jnp.zeros_like(dg_ref)

        dx = dx_ref[...]
        dg_ref[...] += _colsum(dx * y_ref[...])
        dy = _bf(g_ref[...] * dx)
        dy_ref[...] = dy
        dc_ref[0] = _dot_nt(dy, w_ref[0:d, :])
        dc_ref[1] = _dot_nt(dy, w_ref[d:2 * d, :])

    row = pl.BlockSpec((tm, d), lambda i: (i, 0))
    vec = pl.BlockSpec((1, d), lambda i: (0, 0))
    return _pcall(
        body, name="out_bwd", grid=(s_len // tm,),
        in_specs=[row, row, vec, pl.BlockSpec((2 * d, d), lambda i: (0, 0))],
        out_specs=[vec, row, pl.BlockSpec((2, tm, d), lambda i: (0, i, 0))],
        out_shape=[jax.ShapeDtypeStruct((1, d), F32), jax.ShapeDtypeStruct((s_len, d), BF16),
                   jax.ShapeDtypeStruct((2, s_len, d), F32)],
        compiler_params=_seq(),
    )(dxn, y, gate, w_out_b)


def _grad_matmul(a3, b3, nblk, a_idx, b_idx, out_shape, out_block, out_idx, ride=None):
    _, s_len, m = a3.shape
    n = b3.shape[2]
    tk = _tile(s_len, ROWS_GRAD_MATMUL)

    def body(a_ref, b_ref, o_ref):
        @pl.when(pl.program_id(1) == 0)
        def _():
            o_ref[...] = jnp.zeros_like(o_ref)

        o_ref[...] += _dot_tn(a_ref[0], b_ref[0])

    (out,), got = _pcall_ride(
        body, ride, name="grad_matmul", grid=(nblk, s_len // tk),
        in_specs=[pl.BlockSpec((1, tk, m), lambda p, t: (a_idx(p), t, 0)),
                  pl.BlockSpec((1, tk, n), lambda p, t: (b_idx(p), t, 0))],
        out_specs=[pl.BlockSpec((None,) + out_block, lambda p, t: (0,) + out_idx(p))],
        out_shape=[jax.ShapeDtypeStruct((1,) + out_shape, F32)],
        compiler_params=_seq(2), args=(a3, b3))
    return out, got


DU_PLANE = (2, 3, 4, 0, 1)


def _mlstm_bwd(qkv, gates, cst, nst, mst, cell, u, ml_g, d_ycat, wif_b, ride=None):
    _, s_len, d = qkv.shape
    ng = gates[0].shape[1]
    heads = ng // 2
    hd = d // heads
    lc = ML_CHUNK
    nc = s_len // lc
    kscale = hd ** -0.5

    def body(qkv_ref, gt_ref, gtt_ref, bc_ref, bct_ref, cst_ref, nst_ref, mst_ref, cell_ref, o_ref, z_ref, g_ref, dy_ref,
             wif_ref, dqkv_ref, dgt_ref, dbif_ref, du_ref, dg_ref, dcs, dns):
        @pl.when(pl.program_id(0) == 0)
        def _():
            dbif_ref[...] = jnp.zeros_like(dbif_ref)
            dcs[...] = jnp.zeros_like(dcs)
            dns[...] = jnp.zeros_like(dns)
            dg_ref[...] = jnp.zeros_like(dg_ref)

        causal, tril, triu = _tri_masks(lc)
        tril_strict = (tril.astype(F32) - (tril * triu).astype(F32)).astype(BF16)
        gtv, gttv, bcv, bctv = gt_ref[...], gtt_ref[...], bc_ref[...], bct_ref[...]
        lane = lax.broadcasted_iota(jnp.int32, (lc, ng), 1)
        dli_all = jnp.zeros((lc, ng), F32)
        from_later = jnp.zeros((lc, ng), F32)
        from_earlier = jnp.zeros((lc, ng), F32)
        across_all = jnp.zeros((1, ng), F32)
        old = [(dcs[h], dns[h]) for h in range(heads)]
        new, d_o, d_z, d_g, dqs, dks, dvs = [], [], [], [], [], [], []
        for h in range(heads):
            hs = slice(h * hd, (h + 1) * hd)
            li_c, li_r, gf_c, b_c, b_r = _chunk_gates(gtv, gttv, bcv, bctv, h, heads)
            m_prev = mst_ref[0, h][:, 0:1]
            m_t, w_intra, w_inter, _, w_state, decay = _chunk_weights(li_c, li_r, b_c, b_r, m_prev, causal)
            qb = qkv_ref[0, :, hs]
            qf = qb.astype(F32)
            ks = qkv_ref[1, :, hs].astype(F32) * kscale
            kb = _bf(ks)
            vb = qkv_ref[2, :, hs]
            c_b = cst_ref[0, h]
            n_old = nst_ref[0, h]
            s = _dot_nt(qb, kb) * w_intra
            den = _rowsum(s) + w_inter * _rowsum(qf * n_old)
            floor = jnp.exp(-m_t)
            dstab = jnp.maximum(jnp.abs(den), floor)
            cell = cell_ref[:, hs]
            o = o_ref[:, hs]
            so = _sigmoid(o)
            hm = so * cell
            rinv = lax.rsqrt(jnp.mean(hm * hm, axis=-1, keepdims=True) + EPS)
            hn = hm * rinv
            z = z_ref[:, hs]
            sgz = _sigmoid(z)
            sz = z * sgz
            gh = g_ref[:, hs]
            dy = dy_ref[0, :, hs]
            d_z.append(_bf(dy * (hn * gh) * _dsilu(z, sgz)))
            d_g.append(_colsum(dy * hn * sz))
            dhn = dy * gh * sz
            dhm = rinv * (dhn - hn * jnp.mean(dhn * hn, axis=-1, keepdims=True))
            d_o.append(_bf(dhm * cell * so * (1.0 - so)))
            dcell = dhm * so
            dnum = dcell / dstab
            dnb = _bf(dnum)
            dden = -_rowsum(dcell * cell) / dstab * jnp.where(jnp.abs(den) > floor, jnp.where(den > 0.0, 1.0, -1.0), 0.0)
            dst = _dot_nt(dnb, vb) + dden
            dsdb = _bf(dst * w_intra)
            dc_out, dn_out = old[h]
            dcb = _bf(dc_out)
            dq_inter = w_inter * (_dot_nt(dnb, c_b) + dden * n_old)
            dk_inter = w_state * (_dot_nt(vb, dcb) + dn_out)
            dq = _dot(dsdb, kb) + dq_inter
            dk = _dot_tn(dsdb, qb) + dk_inter
            dv = _dot_tn(_bf(s), dnb) + _dot(_bf(ks * w_state), dcb)
            wq = w_inter * qf
            new.append((decay * dc_out + _dot_tn(_bf(wq), dnb), decay * dn_out + _colsum(wq * dden)))
            pmat = dst * s
            p_rows = _rowsum(pmat)
            p_cols = _rowsum(pmat.T)
            q_in = _rowsum(qf * dq_inter)
            k_in = _rowsum(ks * dk_inter)
            across = decay * (jnp.sum(dc_out * c_b.astype(F32), keepdims=True) + jnp.sum(dn_out * n_old, keepdims=True))
            dli_all = dli_all + jnp.where(lane == h, p_cols + k_in, 0.0)
            from_later = from_later + jnp.where(lane == heads + h, p_rows - p_cols + q_in, 0.0)
            from_earlier = from_earlier + jnp.where(lane == heads + h, k_in, 0.0)
            across_all = across_all + jnp.where(lane[0:1] == heads + h, across, 0.0)
            dqs.append(dq)
            dks.append(dk * kscale)
            dvs.append(dv)
        for h in range(heads):
            dcs[h], dns[h] = new[h]
        du_ref[0] = jnp.concatenate(d_o, axis=1)
        du_ref[1] = jnp.concatenate(d_z, axis=1)
        dg_ref[...] += jnp.concatenate(d_g, axis=1)
        dlf = _tri_dot_left(triu, from_later) + _tri_dot_left(tril_strict, from_earlier) + across_all
        dgt = dli_all + dlf * _sigmoid(-gtv)
        dgt_ref[...] = dgt
        dbif_ref[...] += _colsum(dgt)
        dgb = _bf(dgt)
        dqkv_ref[0] = _bf(jnp.concatenate(dqs, axis=1) + _dot_nt(dgb, wif_ref[0:d, :]))
        dqkv_ref[1] = _bf(jnp.concatenate(dks, axis=1) + _dot_nt(dgb, wif_ref[d:2 * d, :]))
        dqkv_ref[2] = _bf(jnp.concatenate(dvs, axis=1) + _dot_nt(dgb, wif_ref[2 * d:3 * d, :]))

    rev = lambda c: nc - 1 - c
    row = pl.BlockSpec((lc, d), lambda c: (rev(c), 0))
    gcol = pl.BlockSpec((lc, ng), lambda c: (rev(c), 0))
    grow = pl.BlockSpec((ng, lc), lambda c: (0, rev(c)))
    return _pcall_ride(
        body, ride, name="mlstm_bwd", grid=(nc,),
        in_specs=[pl.BlockSpec((3, lc, d), lambda c: (0, rev(c), 0)), gcol, grow, gcol, grow,
                  pl.BlockSpec((1, heads, hd, hd), lambda c: (rev(c), 0, 0, 0)),
                  pl.BlockSpec((1, heads, 1, hd), lambda c: (rev(c), 0, 0, 0)),
                  pl.BlockSpec((1, heads, 1, 128), lambda c: (rev(c), 0, 0, 0)),
                  row, pl.BlockSpec((lc, d), lambda c: (rev(c), 3)), pl.BlockSpec((lc, d), lambda c: (rev(c), 4)),
                  pl.BlockSpec((1, d), lambda c: (0, 0)), pl.BlockSpec((1, lc, d), lambda c: (1, rev(c), 0)),
                  pl.BlockSpec((3 * d, ng), lambda c: (0, 0))],
        out_specs=[pl.BlockSpec((3, lc, d), lambda c: (0, rev(c), 0)), pl.BlockSpec((lc, ng), lambda c: (rev(c), 0)),
                   pl.BlockSpec((1, ng), lambda c: (0, 0)), pl.BlockSpec((2, lc, d), lambda c: (0, rev(c), 0)),
                   pl.BlockSpec((1, d), lambda c: (0, 0))],
        out_shape=[jax.ShapeDtypeStruct((3, s_len, d), BF16), jax.ShapeDtypeStruct((s_len, ng), F32),
                   jax.ShapeDtypeStruct((1, ng), F32), jax.ShapeDtypeStruct((5, s_len, d), BF16),
                   jax.ShapeDtypeStruct((1, d), F32)],
        scratch_shapes=[pltpu.VMEM((heads, hd, hd), F32), pltpu.VMEM((heads, 1, hd), F32)],
        compiler_params=_seq(),
        args=(qkv, *gates, cst, nst, mst, cell, u, u, ml_g, d_ycat, wif_b))


def _conv_bwd_tile(dp, later, taps, cw_ref, gw_ref, gb_ref):
    tm = dp.shape[0]
    dwin = jnp.concatenate([dp, later[...]], axis=0)
    later[...] = dp[0:HALO]
    acc = cw_ref[CONV_WIDTH - 1:CONV_WIDTH, :] * dp
    for k in range(CONV_WIDTH):
        if k < CONV_WIDTH - 1:
            acc = acc + cw_ref[k:k + 1, :] * _shift_up(dwin, CONV_WIDTH - 1 - k)[0:tm]
        gw_ref[k:k + 1, :] += _colsum(dp * taps[k])
    gb_ref[...] += _colsum(dp)
    return acc


def _ml_pre_bwd(dqkv, u, conv_w, conv_b, wqkv_b, du):
    s_len = u.shape[0]
    d = conv_w.shape[1]
    _, heads, hd, _ = wqkv_b.shape
    tm = _tile(s_len, ROWS_VECTOR)
    per = tm // HALO
    nt = s_len // tm

    def body(dqkv_ref, x_ref, xp_ref, cw_ref, cb_ref, w_ref, _, dx_ref, gw_ref, gcw_ref, gcb_ref, later, dps, dxs):
        i = pl.program_id(0)

        @pl.when(i == 0)
        def _():
            gw_ref[...] = jnp.zeros_like(gw_ref)
            gcw_ref[...] = jnp.zeros_like(gcw_ref)
            gcb_ref[...] = jnp.zeros_like(gcb_ref)
            later[...] = jnp.zeros_like(later)

        prev = jnp.where(i == nt - 1, 0.0, xp_ref[...])
        xm = x_ref[...]
        taps = _conv_taps(jnp.concatenate([prev, xm], axis=0))
        pre = _conv_fwd(taps, cw_ref, cb_ref)
        sg = _sigmoid(pre)
        xcb = _bf(pre * sg)
        xmb = _bf(xm)
        for h in range(heads):
            hs = slice(h * hd, (h + 1) * hd)
            dqh, dkh, dvh = dqkv_ref[0, :, hs], dqkv_ref[1, :, hs], dqkv_ref[2, :, hs]
            dxc = _dot_nt(dqh, w_ref[0, h]) + _dot_nt(dkh, w_ref[1, h])
            dps[:, hs] = dxc * _dsilu(pre[:, hs], sg[:, hs])
            dxs[:, hs] = _dot_nt(dvh, w_ref[2, h])
            gw_ref[0, h] += _dot_tn(xcb[:, hs], dqh)
            gw_ref[1, h] += _dot_tn(xcb[:, hs], dkh)
            gw_ref[2, h] += _dot_tn(xmb[:, hs], dvh)
        dx_ref[0] = _bf(_conv_bwd_tile(dps[...], later, taps, cw_ref, gcw_ref, gcb_ref) + dxs[...])

    rev = lambda i: nt - 1 - i
    vec = pl.BlockSpec((1, d), lambda i: (0, 0))
    cwb = pl.BlockSpec((CONV_WIDTH, d), lambda i: (0, 0))
    whole4 = pl.BlockSpec(wqkv_b.shape, lambda i: (0, 0, 0, 0))
    return _pcall(
        body, name="ml_pre_bwd", grid=(nt,),
        in_specs=[pl.BlockSpec((3, tm, d), lambda i: (0, rev(i), 0)), pl.BlockSpec((tm, d), lambda i: (rev(i), 2)),
                  pl.BlockSpec((HALO, d), lambda i: (jnp.maximum(rev(i) * per - 1, 0), 2)),
                  cwb, vec, whole4, pl.BlockSpec(memory_space=pl.ANY)],
        out_specs=[pl.BlockSpec((1, tm, d), lambda i: (DU_PLANE[2], rev(i), 0)), whole4, cwb, vec],
        out_shape=[jax.ShapeDtypeStruct(du.shape, BF16), jax.ShapeDtypeStruct(wqkv_b.shape, F32),
                   jax.ShapeDtypeStruct((CONV_WIDTH, d), F32), jax.ShapeDtypeStruct((1, d), F32)],
        scratch_shapes=[pltpu.VMEM((HALO, d), F32), pltpu.VMEM((tm, d), F32), pltpu.VMEM((tm, d), F32)],
        input_output_aliases={6: 0},
        compiler_params=_seq(),
    )(dqkv, u, u, conv_w, conv_b, wqkv_b, du)


def _rg_bwd(d_ycat, u, hh, conv_w, conv_b, wa_b, ba, wx_b, bx, lam, du):
    s_len = u.shape[0]
    d = conv_w.shape[1]
    heads, hd, _ = wa_b.shape
    tm = _tile(s_len, ROWS_VECTOR)
    per = tm // HALO
    nt = s_len // tm

    def body(dy_ref, x_ref, xp_ref, z_ref, hh_ref, hp_ref, cw_ref, cb_ref, wa_ref, ba_ref, wx_ref, bx_ref, lam_ref, _,
             du_ref, gwa_ref, gwx_ref, gba_ref, gbx_ref, glam_ref, gcw_ref, gcb_ref, carry, gbuf, later, dxcs):
        i = pl.program_id(0)
        first = i == nt - 1

        @pl.when(i == 0)
        def _():
            carry[...] = jnp.zeros_like(carry)
            later[...] = jnp.zeros_like(later)
            gwa_ref[...] = jnp.zeros_like(gwa_ref)
            gwx_ref[...] = jnp.zeros_like(gwx_ref)
            gba_ref[...] = jnp.zeros_like(gba_ref)
            gbx_ref[...] = jnp.zeros_like(gbx_ref)
            glam_ref[...] = jnp.zeros_like(glam_ref)
            gcw_ref[...] = jnp.zeros_like(gcw_ref)
            gcb_ref[...] = jnp.zeros_like(gcb_ref)

        prev = jnp.where(first, 0.0, xp_ref[...])
        taps = _conv_taps(jnp.concatenate([prev, x_ref[...]], axis=0))
        xc = _conv_fwd(taps, cw_ref, cb_ref)
        r, ig, sp, log_a, a, mult = _rg_gates(xc, wa_ref, ba_ref, wx_ref, bx_ref, lam_ref)
        z = z_ref[...]
        sgz = _sigmoid(z)
        dy = dy_ref[0]
        hh_v = hh_ref[...]
        du_ref[1] = _bf(dy * hh_v * _dsilu(z, sgz))
        dhh = dy * (z * sgz)
        rows = lax.broadcasted_iota(jnp.int32, a.shape, 0)
        coef = jnp.where(rows == tm - 1, carry[1:2, :], _shift_up(a, 1))
        ca, cu = _scan_groups(coef, dhh, reverse=True)
        c = carry[0:1, :]
        for j in range(tm // 8 - 1, -1, -1):
            blk = ca[j * 8:(j + 1) * 8] * c + cu[j * 8:(j + 1) * 8]
            gbuf[j * 8:(j + 1) * 8, :] = blk
            c = blk[0:1]
        carry[0:1, :] = c
        carry[1:2, :] = a[0:1]
        g = gbuf[...]
        hprev_tile = jnp.where(first, 0.0, hp_ref[...])
        hprev = _shift_down(jnp.concatenate([hprev_tile, hh_v], axis=0), 1)[HALO:]
        da = g * hprev
        gx_ = g * xc
        d_mult = gx_ * ig
        d_ig = gx_ * mult
        dxc = g * mult * ig
        dlog_a = da * a - d_mult * (a * a / mult)
        d_r = dlog_a * ((-RG_C) * sp)
        glam_ref[...] += _colsum(dlog_a * ((-RG_C) * r)) * (-_sigmoid(-lam_ref[...]))
        d_ga = d_r * r * (1.0 - r)
        d_gx = d_ig * ig * (1.0 - ig)
        gba_ref[...] += _colsum(d_ga)
        gbx_ref[...] += _colsum(d_gx)
        xb = _bf(xc)
        dgab = _bf(d_ga)
        dgxb = _bf(d_gx)
        for h in range(heads):
            hs = slice(h * hd, (h + 1) * hd)
            dxcs[:, hs] = dxc[:, hs] + _dot_nt(dgab[:, hs], wa_ref[h]) + _dot_nt(dgxb[:, hs], wx_ref[h])
            gwa_ref[h] += _dot_tn(xb[:, hs], dgab[:, hs])
            gwx_ref[h] += _dot_tn(xb[:, hs], dgxb[:, hs])
        du_ref[0] = _bf(_conv_bwd_tile(dxcs[...], later, taps, cw_ref, gcw_ref, gcb_ref))

    assert DU_PLANE[0] % 2 == 0 and DU_PLANE[1] == DU_PLANE[0] + 1
    rev = lambda i: nt - 1 - i
    row = pl.BlockSpec((tm, d), lambda i: (rev(i), 0))
    halo_prev = lambda col: pl.BlockSpec((HALO, d), lambda i: (jnp.maximum(rev(i) * per - 1, 0), col))
    vec = pl.BlockSpec((1, d), lambda i: (0, 0))
    cwb = pl.BlockSpec((CONV_WIDTH, d), lambda i: (0, 0))
    whole3 = lambda a: pl.BlockSpec(a.shape, lambda i: (0, 0, 0))
    return _pcall(
        body, name="rg_bwd", grid=(nt,),
        in_specs=[pl.BlockSpec((1, tm, d), lambda i: (0, rev(i), 0)), row, halo_prev(0),
                  pl.BlockSpec((tm, d), lambda i: (rev(i), 1)), row, halo_prev(0),
                  cwb, vec, whole3(wa_b), vec, whole3(wx_b), vec, vec, pl.BlockSpec(memory_space=pl.ANY)],
        out_specs=[pl.BlockSpec((2, tm, d), lambda i: (DU_PLANE[0] // 2, rev(i), 0)), whole3(wa_b), whole3(wa_b),
                   vec, vec, vec, cwb, vec],
        out_shape=[jax.ShapeDtypeStruct(du.shape, BF16), jax.ShapeDtypeStruct(wa_b.shape, F32),
                   jax.ShapeDtypeStruct(wa_b.shape, F32)] + [jax.ShapeDtypeStruct((1, d), F32)] * 3
        + [jax.ShapeDtypeStruct((CONV_WIDTH, d), F32), jax.ShapeDtypeStruct((1, d), F32)],
        scratch_shapes=[pltpu.VMEM((8, d), F32), pltpu.VMEM((tm, d), F32), pltpu.VMEM((HALO, d), F32),
                        pltpu.VMEM((tm, d), F32)],
        input_output_aliases={13: 0},
        compiler_params=_seq(),
    )(d_ycat, u, u, u, hh, hh, conv_w, conv_b, wa_b, ba, wx_b, bx, lam, du)


def _in_bwd(du, w4, x, dxn, g, scale, ride=None):
    s_len, d = x.shape
    tm = _tile(s_len, ROWS_IN_BWD)
    nsh_chips, _, nsh = w4.shape
    npc = du.shape[0]
    ck = d // 4
    assert nsh % ck == 0 and npc * d == nsh_chips * nsh

    def body(du_ref, w_ref, x_ref, dxn_ref, g_ref, sc_ref, dx_ref, dsh_ref, dsc_ref, dg_ref):
        @pl.when(pl.program_id(0) == 0)
        def _():
            dsh_ref[...] = jnp.zeros_like(dsh_ref)
            dsc_ref[...] = jnp.zeros_like(dsc_ref)
            dg_ref[...] = jnp.zeros_like(dg_ref)

        dh = None
        for q in range(npc * d // ck):
            col = q * ck
            p, pc = col // d, col % d
            s, sc = col // nsh, col % nsh
            t = _dot_nt(du_ref[DU_PLANE[p], :, pc:pc + ck], w_ref[s, :, sc:sc + ck])
            dh = t if dh is None else dh + t
        xv = x_ref[...]
        r = lax.rsqrt(jnp.mean(xv * xv, axis=-1, keepdims=True) + EPS)
        xn = xv * r
        gv = g_ref[...]
        onesc = 1.0 + sc_ref[...]
        dsh_ref[...] += _colsum(dh)
        dsc_ref[...] += _colsum(dh * (xn * gv))
        dg_ref[...] += _colsum(dh * xn * onesc)
        dxh = dh * (gv * onesc)
        dx_ref[...] = dxn_ref[...] + r * (dxh - xn * jnp.mean(dxh * xn, axis=-1, keepdims=True))

    row = pl.BlockSpec((tm, d), lambda i: (i, 0))
    vec = pl.BlockSpec((1, d), lambda i: (0, 0))
    return _pcall_ride(
        body, ride, name="in_bwd", grid=(s_len // tm,),
        in_specs=[pl.BlockSpec((npc, tm, d), lambda i: (0, i, 0)), pl.BlockSpec(w4.shape, lambda i: (0, 0, 0)), row, row,
                  vec, vec],
        out_specs=[row, vec, vec, vec],
        out_shape=[jax.ShapeDtypeStruct((s_len, d), F32)] + [jax.ShapeDtypeStruct((1, d), F32)] * 3,
        compiler_params=_seq(),
        args=(du, w4, x, dxn, g, scale))


def _layer_fwd(x, p, rides=None, loss_head=None):
    rides = rides or {}
    landed = {}
    ride = lambda kernel: rides[kernel](landed) if kernel in rides else None
    (h_b, u), landed["ln_inproj"] = _ln_inproj(x, p["norm_g"], p["scale"], p["shift"], p["w4"], ride("ln_inproj"))
    (hh, ycat), landed["rg_fwd"] = _rg_fwd(u, p["rg_conv_w"], p["rg_conv_b"], p["rg_wa_b"], p["rg_ba"], p["rg_wx_b"],
                                           p["rg_bx"], p["rg_lam"], ride("rg_fwd"))
    if "late" in rides:
        p = {**p, **rides["late"](landed)}
    qkv, *gates = _ml_pre(u, p["ml_conv_w"], p["ml_conv_b"], p["wqkv_b"], p["wif_b"], p["wift_b"], p["b_if"],
                          p["b_ift"])
    (cell, ycat, cst, nst, mst), landed["mlstm_fwd"] = _mlstm_fwd(qkv, gates, u, p["ml_g"], ycat, ride("mlstm_fwd"))
    if loss_head is None:
        (y, x_new), landed["out_proj"] = _out_proj(ycat, p["w_out_b"], x, p["gate"], ride("out_proj"))
    else:
        y, *x_new = _out_proj_loss(ycat, p["w_out_b"], x, p["gate"], *loss_head)
    saved = dict(x=x, h_b=h_b, u=u, hh=hh, qkv=qkv, gates=gates, cell=cell, ycat=ycat, cst=cst, nst=nst, mst=mst, y=y)
    return x_new, saved, p, landed


def _layer_bwd(dxn, p, s, rides=None):
    rides = rides or {}
    landed = {}
    ride = lambda kernel: rides[kernel](grads, landed) if kernel in rides else None
    u = s["u"]
    d = dxn.shape[1]
    d_gate, dy_b, d_ycat = _out_bwd(dxn, s["y"], p["gate"], p["w_out_b"])
    grads = dict(w_out=_grad_matmul(s["ycat"], dy_b[None], 2, lambda b: b, lambda b: 0, (2 * d, d), (d, d),
                                    lambda b: (b, 0))[0])
    (dqkv, dgt, g_b_if, du, g_ml_g), landed["mlstm_bwd"] = _mlstm_bwd(
        s["qkv"], s["gates"], s["cst"], s["nst"], s["mst"], s["cell"], u, p["ml_g"], d_ycat, p["wif_b"],
        ride("mlstm_bwd"))
    ng = dgt.shape[1]
    g_w_if = _grad_matmul(s["qkv"], _bf(dgt)[None], 3, lambda b: b, lambda b: 0, (3 * d, ng), (d, ng),
                          lambda b: (b, 0))[0][0]
    du, g_wqkv, g_ml_cw, g_ml_cb = _ml_pre_bwd(dqkv, u, p["ml_conv_w"], p["ml_conv_b"], p["wqkv_b"], du)
    du, g_wa, g_wx, g_ba, g_bx, g_lam, g_rg_cw, g_rg_cb = _rg_bwd(d_ycat, u, s["hh"], p["rg_conv_w"], p["rg_conv_b"],
                                                                  p["rg_wa_b"], p["rg_ba"], p["rg_wx_b"], p["rg_bx"],
                                                                  p["rg_lam"], du)
    grads.update(rg_conv_w=g_rg_cw, rg_conv_b=g_rg_cb, rg_w_a=g_wa, rg_b_a=g_ba, rg_w_x=g_wx, rg_b_x=g_bx,
                 rg_lambda=g_lam, ml_conv_w=g_ml_cw, ml_conv_b=g_ml_cb, ml_w_qkv=g_wqkv, ml_w_if=g_w_if, ml_b_if=g_b_if,
                 ml_norm_g=g_ml_g)
    npc = du.shape[0]
    grads["w_in"], landed["grad_w_in"] = _grad_matmul(
        s["h_b"][None], du, npc, lambda b: 0, lambda b: (b + DU_PLANE[0]) % npc, (d, npc * d), (d, d),
        lambda b: (0, b), ride("grad_w_in"))
    (dx, d_shift, d_scale, grads["norm_g"]), landed["in_bwd"] = _in_bwd(du, p["w4"], s["x"], dxn, p["norm_g"],
                                                                        p["scale"], ride("in_bwd"))
    return dx, grads, jnp.concatenate([d_shift, d_scale, d_gate], axis=1), landed


def _me():
    return lax.axis_index("x"), lax.axis_index("y"), lax.axis_index("c")


def _remote(src, dst, send_sem, recv_sem, to):
    return pltpu.make_async_remote_copy(src_ref=src, dst_ref=dst, send_sem=send_sem, recv_sem=recv_sem,
                                        device_id=to, device_id_type=MESH)


def _all_gather8(blocks, space):
    n = len(blocks)
    relay = [b.size * b.dtype.itemsize >= RELAY_BYTES and b.shape[0] % 32 == 0 for b in blocks]

    def body(*refs):
        x_refs, out_refs = refs[:n], refs[n:2 * n]
        send_sems, recv_sems, local_sems = refs[2 * n:]
        x, y, c = _me()
        me, sibling = (x, y, c), (x, y, 1 - c)
        by_x, by_y, across = (1 - x, y, c), (x, 1 - y, c), (1 - x, 1 - y, c)

        def rows(i, blk, part=None):
            m_per = blocks[i].shape[0]
            at = (4 * blk[0] + 2 * blk[1] + blk[2]) * m_per
            if part is not None:
                m_per //= 2
                at += part * m_per
            return out_refs[i].at[pl.ds(at, m_per), :]

        def copy(i, k, blk, to, src=None, part=None):
            return _remote(rows(i, blk, part) if src is None else src, rows(i, blk, part), send_sems.at[8 * i + k],
                           recv_sems.at[8 * i + k], to)

        mine = [pltpu.make_async_copy(x_refs[i], rows(i, me), local_sems.at[i]) for i in range(n)]
        first = []
        for i in range(n):
            first.append(copy(i, 0, me, sibling, src=x_refs[i]))
            first += [copy(i, 1, me, by_x, src=x_refs[i]), copy(i, 2, me, by_y, src=x_refs[i])]
            if not relay[i]:
                first.append(copy(i, 3, me, across, src=x_refs[i]))
        for cp in mine + first:
            cp.start()
        passed = []
        for i in range(n):
            copy(i, 1, by_x, me).wait_recv()
            passed.append(copy(i, 4, by_x, sibling))
            if relay[i]:
                passed.append(copy(i, 3, by_x, by_y, part=0))
        for cp in passed:
            cp.start()
        n_x = len(passed)
        for i in range(n):
            copy(i, 2, by_y, me).wait_recv()
            passed.append(copy(i, 5, by_y, sibling))
            if relay[i]:
                passed.append(copy(i, 7, by_y, by_x, part=1))
        for cp in passed[n_x:]:
            cp.start()
        for i in range(n):
            if relay[i]:
                copy(i, 3, across, me, part=0).wait_recv()
                copy(i, 7, across, me, part=1).wait_recv()
            else:
                copy(i, 3, across, me).wait_recv()
            passed.append(copy(i, 6, across, sibling))
            passed[-1].start()
        for i in range(n):
            copy(i, 0, sibling, me).wait_recv()
            for k, blk in ((4, by_x), (5, by_y), (6, across)):
                copy(i, k, (blk[0], blk[1], 1 - c), me).wait_recv()
        for cp in first + passed:
            cp.wait_send()
        for cp in mine:
            cp.wait()

    spec = pl.BlockSpec(memory_space=space)
    return _pcall(
        body, name="all_gather8",
        out_shape=[jax.ShapeDtypeStruct((8 * b.shape[0], b.shape[1]), b.dtype) for b in blocks],
        in_specs=[spec] * n, out_specs=[spec] * n,
        scratch_shapes=[pltpu.SemaphoreType.DMA((8 * n,)), pltpu.SemaphoreType.DMA((8 * n,)),
                        pltpu.SemaphoreType.DMA((n,))],
    )(*blocks)


def _exchange(legs):
    n = len(legs)

    def body(*refs):
        copies, local = _exchange_body(legs, refs[:n], refs[n:2 * n], *refs[2 * n:])
        for cp in copies + local:
            cp.start()
        for cp in copies:
            cp.wait_recv()
        for cp in copies:
            cp.wait_send()
        for cp in local:
            cp.wait()

    hbm = pl.BlockSpec(memory_space=pltpu.HBM)
    return _pcall(body, name="exchange", out_shape=[leg.landing() for leg in legs], in_specs=[hbm] * n,
                  out_specs=[hbm] * n, input_output_aliases=_exchange_aliases(legs, 0, 0),
                  scratch_shapes=_exchange_sems(legs))(*[leg.src for leg in legs])


def _row_tile(rows, cap=4096, mult=16):
    best = None
    for t in range(mult, min(rows, cap) + 1, mult):
        if rows % t == 0:
            best = t
    return rows if best is None else best


def _pair_sum(half, own, own_spec, got, got_spec, out_shape, out_spec, grid):
    def body(_, a_ref, b_ref, o_ref):
        o_ref[...] = (a_ref[...] + b_ref[...].astype(F32)).astype(o_ref.dtype)

    return _pcall(
        body, name="pair_sum",
        grid_spec=pltpu.PrefetchScalarGridSpec(num_scalar_prefetch=1, grid=grid, in_specs=[own_spec, got_spec],
                                               out_specs=out_spec),
        out_shape=out_shape, compiler_params=_seq(len(grid)))(half, own, got)


def _chip_sum(ids, part, met, fill, layer=0, stack=1):
    _, _, rows, n = part.shape
    tr = _row_tile(rows, cap=max(16, BLOCK_ELEMS // n))
    first = isinstance(stack, int)

    def body(_, own_ref, a_ref, b_ref, c_ref, *rest):
        acc = own_ref[...].astype(F32) + a_ref[...].astype(F32)
        acc = acc + b_ref[...].astype(F32)
        rest[-1][...] = acc + c_ref[...].astype(F32)

    blk = (None, None, tr, n)
    other = lambda k: pl.BlockSpec(blk, lambda j, ids: ((ids[0] + k) % 4, 0, j, 0))
    in_specs = [pl.BlockSpec(blk, lambda j, ids: (ids[0], 0, j, 0)), other(1), other(2), other(3)]
    return _pcall(
        body, name="chip_sum",
        grid_spec=pltpu.PrefetchScalarGridSpec(
            num_scalar_prefetch=1, grid=(rows // tr,),
            in_specs=in_specs if first else in_specs + [pl.BlockSpec(memory_space=pl.ANY)],
            out_specs=pl.BlockSpec(blk, lambda j, ids: (layer, ids[1] if fill else 0, j, 0))),
        out_shape=jax.ShapeDtypeStruct(((stack,) if first else stack.shape[:1]) + (2 if fill else 1, rows, n), F32),
        input_output_aliases={} if first else {5: 0},
        compiler_params=_seq())(*((ids, part, met, met, met) if first else (ids, part, met, met, met, stack)))


def _ada_mod(c_all, w_ada, b_ada_cols):
    depth, d, n = w_ada.shape
    nb = c_all.shape[0]

    def body(c_ref, w_ref, b_ref, o_ref):
        cv = c_ref[...]
        ca = _bf(cv * _sigmoid(cv))
        o_ref[0] = _dot(ca, _bf(w_ref[0])) + b_ref[0]

    return _pcall(body, name="ada_mod", grid=(depth,),
                  in_specs=[pl.BlockSpec((nb, d), lambda l: (0, 0)), pl.BlockSpec((1, d, n), lambda l: (l, 0, 0)),
                            pl.BlockSpec((1, 1, n), lambda l: (l, 0, 0))],
                  out_specs=pl.BlockSpec((1, nb, n), lambda l: (l, 0, 0)),
                  out_shape=jax.ShapeDtypeStruct((depth, nb, n), F32), compiler_params=_seq())(c_all, w_ada, b_ada_cols)


def _ada_grad(c_all, dmod_cols, rows_all):
    nb, d = c_all.shape
    depth, _, n = dmod_cols.shape
    kinds, n_all = rows_all.shape[1], rows_all.shape[3]

    def body(c_ref, dm_ref, da_ref, gw_ref, gb_ref):
        cv = c_ref[...]
        ca = _bf(cv * _sigmoid(cv))
        gw_ref[0] = _dot_tn(ca, _bf(dm_ref[0]))
        for k in range(kinds):
            gb_ref[0, k] = _colsum(da_ref[0, k])

    return _pcall(body, name="ada_grad", grid=(depth,),
                  in_specs=[pl.BlockSpec((nb, d), lambda l: (0, 0)), pl.BlockSpec((1, nb, n), lambda l: (l, 0, 0)),
                            pl.BlockSpec((1, kinds, nb, n_all), lambda l: (l, 0, 0, 0))],
                  out_specs=[pl.BlockSpec((1, d, n), lambda l: (l, 0, 0)),
                             pl.BlockSpec((1, kinds, 1, n_all), lambda l: (l, 0, 0, 0))],
                  out_shape=[jax.ShapeDtypeStruct((depth, d, n), F32), jax.ShapeDtypeStruct((depth, kinds, 1, n_all), F32)],
                  compiler_params=_seq())(c_all, dmod_cols, rows_all)


def _adamw(items, ride=None):
    two_d = [tuple(t.reshape(w.size // w.shape[-1], w.shape[-1]) for t in (w, g, m, v)) for w, g, m, v in items]
    n = len(items)
    if n == 1:
        rows, cols = two_d[0][0].shape
        tr = _row_tile(rows, cap=max(8, BLOCK_ELEMS // cols), mult=8)
        blocks = [pl.BlockSpec((tr, cols), lambda i: (i, 0))]
        grid = (rows // tr,)
    else:
        blocks = [pl.BlockSpec(t[0].shape, lambda i: (0, 0)) for t in two_d]
        grid = (1,)

    def body(*refs):
        for k in range(n):
            w_ref, g_ref, m_ref, v_ref = refs[4 * k:4 * k + 4]
            d_ref, mo_ref, vo_ref = refs[4 * n + 3 * k:4 * n + 3 * k + 3]
            gv = g_ref[...]
            mn = ADAM_B1 * m_ref[...] + (1.0 - ADAM_B1) * gv
            vn = ADAM_B2 * v_ref[...] + (1.0 - ADAM_B2) * (gv * gv)
            m_hat = mn / (1.0 - ADAM_B1 ** ADAM_STEP)
            v_hat = vn / (1.0 - ADAM_B2 ** ADAM_STEP)
            d_ref[...] = -ADAM_LR * (m_hat / (jnp.sqrt(v_hat) + ADAM_EPS) + ADAM_WD * w_ref[...])
            mo_ref[...] = mn
            vo_ref[...] = vn

    outs, got = _pcall_ride(
        body, ride, name="adamw", grid=grid,
        in_specs=[b for b in blocks for _ in range(4)], out_specs=[b for b in blocks for _ in range(3)],
        out_shape=[jax.ShapeDtypeStruct(t[0].shape, F32) for t in two_d for _ in range(3)],
        compiler_params=_seq(), args=tuple(a for t in two_d for a in t))
    return [tuple(o.reshape(items[k][0].shape) for o in outs[3 * k:3 * k + 3]) for k in range(n)], got


WEIGHTS = ["norm_g", "w_ada", "b_ada", "w_in", "rg_conv_w", "rg_conv_b", "rg_w_a", "rg_b_a", "rg_w_x", "rg_b_x",
           "rg_lambda", "ml_conv_w", "ml_conv_b", "ml_w_q", "ml_w_k", "ml_w_v", "ml_w_if", "ml_b_if", "ml_norm_g",
           "w_out", "final_g"]
SMALL_SHARDED = {"rg_conv_w": 1, "ml_conv_w": 1, "ml_w_if": 0}
REPLICATED = ["rg_w_a", "rg_w_x", "rg_conv_b", "rg_b_a", "rg_b_x", "rg_lambda", "ml_conv_b", "ml_norm_g", "ml_b_if"]
LANES = 128


def _to_pieces(g, axis):
    shp = g.shape
    g = g.reshape(shp[:axis] + (4, 2, shp[axis] // 8) + shp[axis + 1:])
    g = jnp.moveaxis(g, (axis, axis + 1), (0, 1))
    return g.reshape(4, 2, -1)


def _from_pieces(p, shard_shape, axis):
    k = p.shape[0]
    rest = shard_shape[:axis] + (shard_shape[axis] // k,) + shard_shape[axis + 1:]
    t = jnp.moveaxis(p.reshape((k,) + rest), 0, axis)
    return t.reshape(shard_shape)


def _pad_rows(flat, mult):
    n = flat.shape[-1]
    pad = (-n) % mult
    if pad:
        flat = jnp.concatenate([flat, jnp.zeros(flat.shape[:-1] + (pad,), flat.dtype)], axis=-1)
    return flat


def kernel(x, c, norm_g, w_ada, b_ada, w_in, rg_conv_w, rg_conv_b, rg_w_a, rg_b_a, rg_w_x, rg_b_x, rg_lambda, ml_conv_w, ml_conv_b, ml_w_q, ml_w_k, ml_w_v, ml_w_if, ml_b_if, ml_norm_g, w_out, final_g, loss_target, m_norm_g, m_w_ada, m_b_ada, m_w_in, m_rg_conv_w, m_rg_conv_b, m_rg_w_a, m_rg_b_a, m_rg_w_x, m_rg_b_x, m_rg_lambda, m_ml_conv_w, m_ml_conv_b, m_ml_w_q, m_ml_w_k, m_ml_w_v, m_ml_w_if, m_ml_b_if, m_ml_norm_g, m_w_out, m_final_g, v_norm_g, v_w_ada, v_b_ada, v_w_in, v_rg_conv_w, v_rg_conv_b, v_rg_w_a, v_rg_b_a, v_rg_w_x, v_rg_b_x, v_rg_lambda, v_ml_conv_w, v_ml_conv_b, v_ml_w_q, v_ml_w_k, v_ml_w_v, v_ml_w_if, v_ml_b_if, v_ml_norm_g, v_w_out, v_final_g):
    given = dict(locals())
    ax, ay, ac = lax.axis_index("x"), lax.axis_index("y"), lax.axis_index("c")
    chip = 2 * ax + ay
    me = 2 * chip + ac
    depth, d = norm_g.shape
    n_ada = w_ada.shape[2]
    pick = lambda a, i, axis=0: lax.dynamic_index_in_dim(a, i, axis, keepdims=False)

    convs = jnp.stack([rg_conv_w, ml_conv_w])
    n_conv = 2 * depth * CONV_WIDTH // 4
    blk = jnp.concatenate([c, convs.reshape(n_conv, d), jnp.zeros((8 - 1 - n_conv, d), F32)], axis=0)
    w_in_first = lax.dynamic_slice_in_dim(w_in[0], ac * (d // 2), d // 2, 0).astype(BF16)
    g0, w_in_first = _all_gather8([blk, w_in_first], pltpu.HBM)
    g0 = g0.reshape(8, 8, d)
    c_all = g0[:, 0, :]
    conv_full = g0[0::2, 1:1 + n_conv].reshape(4, 2, depth, CONV_WIDTH, d // 4)
    conv_full = conv_full.transpose(1, 2, 3, 0, 4).reshape(2, depth, CONV_WIDTH, d)

    b_cols = lax.dynamic_slice_in_dim(b_ada, chip * n_ada, n_ada, axis=1)[:, None, :]
    mod_part = _ada_mod(c_all, w_ada, b_cols)
    g1 = _all_gather8([mod_part.transpose(1, 0, 2).reshape(8, depth * n_ada)], pltpu.VMEM)[0]
    g1 = g1.reshape(8, 8, depth, n_ada)[0::2]
    mod_me = pick(g1.transpose(1, 2, 0, 3).reshape(8, depth, 4 * n_ada), me)

    def half_of(w, axis):
        n = w.shape[axis] // 2
        return lax.dynamic_slice_in_dim(w, ac * n, n, axis).astype(BF16)

    n_sh = w_in.shape[2]
    heads, hd_cut, hd = ml_w_q.shape[1:]

    def blocks_of(l):
        wqkv = jnp.stack([ml_w_q[l], ml_w_k[l], ml_w_v[l]])
        return [half_of(w_in[l], 0), half_of(w_out[l], 0), half_of(wqkv, 2).reshape(-1, hd), half_of(ml_w_if[l], 0)]

    def layer_of(l, w4, rest):
        return dict(
            norm_g=norm_g[l][None], shift=mod_me[l, 0:d][None], scale=mod_me[l, d:2 * d][None],
            gate=mod_me[l, 2 * d:3 * d][None], w4=w4.reshape(4, d, n_sh),
            rg_conv_w=conv_full[0, l], rg_conv_b=rg_conv_b[l][None], rg_wa_b=_bf(rg_w_a[l]), rg_ba=rg_b_a[l][None],
            rg_wx_b=_bf(rg_w_x[l]), rg_bx=rg_b_x[l][None], rg_lam=rg_lambda[l][None],
            ml_conv_w=conv_full[1, l], ml_conv_b=ml_conv_b[l][None], b_if=ml_b_if[l][None], b_ift=ml_b_if[l][:, None],
            ml_g=ml_norm_g[l][None], **rest)

    def rest_of(gathered):
        w_out_b, wqkv_g, wif = gathered
        return dict(w_out_b=w_out_b, wqkv_b=_from_pieces(wqkv_g.reshape(8, -1), (3, heads, hd, hd), 2), wif_b=wif,
                    wift_b=wif.T)

    spread = lambda blocks: [Leg(b, "spread") for b in blocks]
    fill = lambda landed: [Leg(t, "sib_fill") for t in landed]
    flat = lambda filled: [t.reshape(-1, t.shape[-1]) for t in filled]
    first = blocks_of(0)
    n_rest = len(first) - 1
    p = layer_of(0, w_in_first, {})
    layers, saved = [], []
    xl = x[0]
    for l in range(depth):
        nxt = blocks_of(l + 1) if l + 1 < depth else []
        skip = n_rest if l == 0 else 0
        rides = dict(rg_fwd=lambda landed, nxt=nxt: spread(nxt[:1]))
        if l == 0:
            rides.update(ln_inproj=lambda landed: spread(first[1:]),
                         rg_fwd=lambda landed, nxt=nxt: fill(landed["ln_inproj"]) + spread(nxt[:1]),
                         late=lambda landed: rest_of(flat(landed["rg_fwd"][:n_rest])))
        if nxt:
            rides.update(mlstm_fwd=lambda landed, nxt=nxt: spread(nxt[1:]),
                         out_proj=lambda landed, skip=skip: fill(list(landed["rg_fwd"][skip:]) + list(landed["mlstm_fwd"])))
        xl, s, p, landed = _layer_fwd(xl, p, rides, None if nxt else (final_g[None], loss_target[0]))
        layers.append(p)
        saved.append(s)
        if nxt:
            arrived = flat(landed["out_proj"])
            p = layer_of(l + 1, arrived[0], rest_of(arrived[1:]))
    dx, g_final, loss = xl

    half = ac.reshape(1)
    ids = jnp.stack([chip, ac])
    r_out = w_out.shape[1] // 2

    def pair_in(g_w_in, got_in):
        return _pair_sum(
            half, g_w_in, pl.BlockSpec((None, d // 2, n_sh), lambda s, h: (0, h[0], s)),
            got_in, pl.BlockSpec((None, None, d // 2, n_sh), lambda s, h: (0, s, 0, 0)),
            jax.ShapeDtypeStruct((4, 1, d // 2, n_sh), BF16),
            pl.BlockSpec((None, None, d // 2, n_sh), lambda s, h: (s, 0, 0, 0)), (4,))

    def pair_out(g_out5, got_out):
        return _pair_sum(
            half, g_out5, pl.BlockSpec((None, None, None, r_out, d), lambda s, h: (0, s, h[0], 0, 0)),
            got_out, pl.BlockSpec((None, None, r_out, d), lambda s, h: (0, s, 0, 0)),
            jax.ShapeDtypeStruct((4, 1, r_out, d), BF16),
            pl.BlockSpec((None, None, r_out, d), lambda s, h: (s, 0, 0, 0)), (4,))

    def pair_slab(slab, got, dtype):
        rows = got.shape[0] // 4
        blk = pl.BlockSpec((rows, LANES), lambda s, h: (s, 0))
        return _pair_sum(half, slab, pl.BlockSpec((None, rows, LANES), lambda s, h: (h[0], s, 0)), got, blk,
                         jax.ShapeDtypeStruct((4 * rows, LANES), dtype), blk, (4,)).reshape(4, 1, rows, LANES)

    row_pad = lambda n: -(-n // (8 * LANES)) * (8 * LANES)

    def as_rows(t):
        if t.shape[-1] == LANES and t.size % (8 * LANES) == 0:
            return t.reshape(-1, LANES)
        return _pad_rows(t.reshape(-1), 8 * LANES).reshape(-1, LANES)

    chips = lambda arrs: [Leg(a, "chips") for a in arrs]
    out5 = lambda g: g["w_out"].reshape(1, 4, 2, r_out, d)
    r_q = hd // 8
    qkv5 = lambda g: g["ml_w_qkv"].reshape(3 * heads, 4, 2, r_q, hd)

    def pair_qkv(g5, got):
        return _pair_sum(
            half, g5, pl.BlockSpec((3 * heads, None, None, r_q, hd), lambda s, h: (0, s, h[0], 0, 0)),
            got, pl.BlockSpec((3 * heads, None, r_q, hd), lambda s, h: (0, s, 0, 0)),
            jax.ShapeDtypeStruct((4, 1, 3 * heads, r_q, hd), BF16),
            pl.BlockSpec((None, None, 3 * heads, r_q, hd), lambda s, h: (s, 0, 0, 0, 0)), (4,))

    grads, dmods, parts, mets = [None] * depth, [None] * depth, [None] * depth, [None] * depth
    small = {}

    def early_exchange(g, landed):
        every = [g] + grads[1:]
        sm = jnp.concatenate([_to_pieces(every[l][name], axis) for l in range(depth)
                              for name, axis in SMALL_SHARDED.items()], axis=-1)
        sm = _pad_rows(sm, 16 * LANES)
        sm = sm.transpose(1, 0, 2).reshape(2, -1, LANES)
        rep = [as_rows(every[l][name]) for l in range(depth) for name in REPLICATED]
        rep = jnp.concatenate(rep + [as_rows(g_final), as_rows(loss)], axis=0)
        rep = jnp.concatenate([rep, jnp.zeros(((-rep.shape[0]) % 64, LANES), F32)], axis=0)
        rep = rep.reshape(4, 2, -1, LANES).transpose(1, 0, 2, 3).reshape(2, -1, LANES)
        got_sm, got_rep, got_q = _exchange([Leg(sm, "sib_slab"), Leg(rep, "sib_slab"), Leg(qkv5(g), "sib_w_out")])
        small["parts"] = [pair_out(out5(g), landed["mlstm_bwd"][0]), pair_slab(sm, got_sm, BF16),
                          pair_slab(rep, got_rep, F32), pair_qkv(qkv5(g), got_q)]
        return chips(small["parts"])

    def last_exchange(g, landed):
        (got_in,) = _exchange([Leg(g["w_in"], "sib_w_in")])
        small["part_in"] = pair_in(g["w_in"], got_in)
        return chips([small["part_in"]])

    for l in reversed(range(depth)):
        above = parts[l + 1] if l + 1 < depth else []
        rides = dict(mlstm_bwd=lambda g, landed, above=above: [Leg(out5(g), "sib_w_out")] + chips(above),
                     in_bwd=lambda g, landed: [Leg(g["w_in"], "sib_w_in"), Leg(qkv5(g), "sib_w_out")])
        if l == 0:
            rides.update(grad_w_in=early_exchange, in_bwd=last_exchange)
        dx, grads[l], dmods[l], got = _layer_bwd(dx, layers[l], saved[l], rides)
        if above:
            mets[l + 1] = got["mlstm_bwd"][1:]
        if l > 0:
            parts[l] = [pair_in(grads[l]["w_in"], got["in_bwd"][0]), pair_out(out5(grads[l]), got["mlstm_bwd"][0]),
                        pair_qkv(qkv5(grads[l]), got["in_bwd"][1])]
    part_out, part_sm, part_rep, part_q = small["parts"]
    met_out, met_sm, met_rep, met_q = got["grad_w_in"]
    parts[0], mets[0] = [small["part_in"], part_out, part_q], [got["in_bwd"][0], met_out, met_q]
    n_rep = part_rep.shape[2]

    pad = lambda t: jnp.concatenate([t, jnp.zeros((1, 2 * d), F32)], axis=1)
    rows = [r for l in range(depth) for r in (dmods[l], pad(grads[l]["norm_g"]))]
    blk = jnp.concatenate(rows + [jnp.zeros((8 - 2 * depth, 3 * d), F32)], axis=0)
    rows_all = _all_gather8([blk], pltpu.VMEM)[0].reshape(8, 8, 3 * d)[:, :2 * depth]
    rows_all = rows_all.transpose(1, 0, 2).reshape(depth, 2, 8, 3 * d)
    dm_cols = lax.dynamic_slice_in_dim(rows_all[:, 0], chip * n_ada, n_ada, axis=2)
    g_w_ada, summed = _ada_grad(c_all, dm_cols, rows_all)

    g = dict(w_ada=g_w_ada, b_ada=summed[:, 0, 0], norm_g=summed[:, 1, 0, :d])
    item = lambda name: (given[name], g[name], given["m_" + name], given["v_" + name])
    both_in, both_out, both_q = depth, depth, depth
    flat_q = lambda t: t.reshape(4, 1, 3 * heads * r_q, hd)
    for l in range(depth):
        both_in = _chip_sum(ids, parts[l][0], mets[l][0], True, l, both_in)
        both_out = _chip_sum(ids, parts[l][1], mets[l][1], True, l, both_out)
        both_q = _chip_sum(ids, flat_q(parts[l][2]), flat_q(mets[l][2]), True, l, both_q)
    both_in, both_out, both_q, both_sm = _exchange(fill([both_in, both_out, both_q,
                                                         _chip_sum(ids, part_sm, met_sm, True)]))
    red_rep = _chip_sum(ids, part_rep, met_rep, False).reshape(n_rep, LANES)
    rep_all = _all_gather8([red_rep], pltpu.VMEM)[0].reshape(-1)

    g.update(w_in=both_in.reshape(w_in.shape), w_out=both_out.reshape(w_out.shape))
    g_qkv = both_q.reshape(depth, 2, 3, heads, r_q, hd).transpose(0, 2, 3, 1, 4, 5)
    g_qkv = g_qkv.reshape(depth, 3, heads, 2 * r_q, hd)
    for i, name in enumerate(["ml_w_q", "ml_w_k", "ml_w_v"]):
        g[name] = g_qkv[:, i]
    shard = both_sm.reshape(2, -1)
    off = 0
    per_layer = {name: [] for name in SMALL_SHARDED}
    for l in range(depth):
        for name, axis in SMALL_SHARDED.items():
            n = grads[l][name].size // 8
            per_layer[name].append(_from_pieces(shard[:, off:off + n], given[name].shape[1:], axis))
            off += n
    for name in SMALL_SHARDED:
        g[name] = jnp.stack(per_layer[name])
    off = 0
    per_layer = {name: [] for name in REPLICATED}
    for l in range(depth):
        for name in REPLICATED:
            n = given[name][l].size
            per_layer[name].append(rep_all[off:off + n].reshape(given[name].shape[1:]))
            off += row_pad(n)
    for name in REPLICATED:
        g[name] = jnp.stack(per_layer[name])
    g["final_g"] = rep_all[off:off + d]
    loss_all = rep_all[off + row_pad(d)]

    stepped = {}
    rg_mats, ml_mats = ["rg_w_a", "rg_w_x"], ["ml_w_q", "ml_w_k", "ml_w_v"]
    vectors = [n for n in WEIGHTS if n not in ["w_ada", "w_in", "w_out"] + rg_mats + ml_mats]
    for names in (["w_ada"], ["w_in"], ["w_out"], rg_mats, ml_mats, vectors):
        stepped.update(zip(names, _adamw([item(name) for name in names])[0]))
    deltas, new_m, new_v = zip(*[stepped[name] for name in WEIGHTS])
    return (loss_all, dx[None], *[g[name] for name in WEIGHTS], *deltas, *new_m, *new_v)
```

```python
import functools
from typing import NamedTuple

import jax
import jax.numpy as jnp
from jax import lax
from jax.experimental import pallas as pl
from jax.experimental.pallas import tpu as pltpu

F32 = jnp.float32
BF16 = jnp.bfloat16

EPS = 1e-6
RG_C = 8.0
CONV_WIDTH = 4
ML_CHUNK = 512
HALO = 8
ROWS_VECTOR = 512
ROWS_MATMUL = 1024
ROWS_IN_BWD = 512
ROWS_GRAD_MATMUL = 2048
BLOCK_ELEMS = 1 << 18
RELAY_BYTES = 1 << 18
ADAM_LR = 0.001
ADAM_B1 = 0.9
ADAM_B2 = 0.999
ADAM_EPS = 1e-08
ADAM_WD = 0.01
ADAM_STEP = 10
MESH = pl.DeviceIdType.MESH


def _pcall(body, **kw):
    return pl.pallas_call(body, **kw)


class Leg(NamedTuple):
    src: jax.Array
    kind: str

    def landing(self):
        a = self.src
        shape = {"chips": lambda: a.shape, "spread": lambda: (4, 2) + a.shape, "sib_fill": lambda: a.shape,
                 "sib_w_in": lambda: (a.shape[0], 4, a.shape[1] // 2, a.shape[2] // 4),
                 "sib_w_out": lambda: a.shape[:2] + a.shape[3:], "sib_slab": lambda: a.shape[1:]}[self.kind]()
        return jax.ShapeDtypeStruct(shape, a.dtype)

    def relayed(self):
        a = self.src
        return self.kind == "spread" and a.size * a.dtype.itemsize >= RELAY_BYTES and a.shape[0] % 32 == 0

    def copies(self, src, dst, x, y, c):
        a, me_s, o = self.src, 2 * x + y, 1 - c
        chips = [(1 - x, y), (x, 1 - y), (1 - x, 1 - y)]
        if self.kind == "chips":
            return [(src.at[2 * px + py], dst.at[me_s], (px, py, c)) for px, py in chips], [], []
        if self.kind == "spread":
            own = dst.at[me_s, c]
            if not self.relayed():
                return [(src, own, (px, py, c)) for px, py in chips], [(src, own)], []
            by_x, by_y, half = chips[0], chips[1], a.shape[0] // 2
            part = lambda chip, k: dst.at[2 * chip[0] + chip[1], c, pl.ds(k * half, half)]
            return ([(src, own, (*by_x, c)), (src, own, (*by_y, c))], [(src, own)],
                    [(part(by_x, 0), part(by_x, 0), (*by_y, c), 0), (part(by_y, 1), part(by_y, 1), (*by_x, c), 1)])
        depth = pl.ds(0, a.shape[0])
        if self.kind == "sib_fill":
            return [(dst.at[depth, c], dst.at[depth, c], (x, y, o))], [], []
        if self.kind == "sib_w_in":
            half, n = a.shape[1] // 2, a.shape[2] // 4
            return [(src.at[depth, pl.ds(o * half, half), pl.ds(s * n, n)], dst.at[depth, s], (x, y, o))
                    for s in range(4)], [], []
        if self.kind == "sib_w_out":
            return [(src.at[depth, pl.ds(0, 4), o], dst, (x, y, o))], [], []
        return [(src.at[o], dst, (x, y, o))], [], []

    def n_copies(self):
        return 4 if self.relayed() else {"chips": 3, "spread": 3, "sib_w_in": 4}.get(self.kind, 1)


def _exchange_body(legs, srcs, dsts, send_sems, recv_sems, local_sems):
    x, y, c = _me()
    remote, local, relays, k = [], [], [], 0
    for i, leg in enumerate(legs):
        far, near, handed = leg.copies(srcs[i], dsts[i], x, y, c)
        at = len(remote)
        for src, dst, to in far:
            remote.append(_remote(src, dst, send_sems.at[k], recv_sems.at[k], to))
            k += 1
        for src, dst, to, after in handed:
            relays.append((_remote(src, dst, send_sems.at[k], recv_sems.at[k], to), remote[at + after]))
            k += 1
        local += [pltpu.make_async_copy(src, dst, local_sems.at[i]) for src, dst in near]
    return remote, local, relays


def _hand_on(relays):
    for cp, after in relays:
        after.wait_recv()
        cp.start()


def _wait_all(copies, local, relays):
    arrived, handed = [after for _, after in relays], [cp for cp, _ in relays]
    for cp in [cp for cp in copies if not any(cp is a for a in arrived)] + handed:
        cp.wait_recv()
    for cp in copies + handed:
        cp.wait_send()
    for cp in local:
        cp.wait()


def _exchange_sems(legs):
    n = sum(leg.n_copies() for leg in legs)
    return [pltpu.SemaphoreType.DMA((n,)), pltpu.SemaphoreType.DMA((n,)), pltpu.SemaphoreType.DMA((len(legs),))]


def _exchange_aliases(legs, n_in, n_out):
    return {n_in + i: n_out + i for i, leg in enumerate(legs) if leg.kind == "sib_fill"}


def _pcall_ride(body, ride, *, grid, in_specs, out_specs, out_shape, args, scratch_shapes=(), **kw):
    n_in, n_out, n_scr = len(in_specs), len(out_specs), len(scratch_shapes)
    if not ride:
        res = _pcall(body, grid=grid, in_specs=in_specs, out_specs=out_specs, out_shape=out_shape,
                     scratch_shapes=list(scratch_shapes), **kw)(*args)
        return res, []
    nr = len(ride)

    def riding(*refs):
        ins, rsrc = refs[:n_in], refs[n_in:n_in + nr]
        outs, rdst = refs[n_in + nr:n_in + nr + n_out], refs[n_in + nr + n_out:n_in + 2 * nr + n_out]
        scr = refs[n_in + 2 * nr + n_out:n_in + 2 * nr + n_out + n_scr]
        copies, local, relays = _exchange_body(ride, rsrc, rdst, *refs[n_in + 2 * nr + n_out + n_scr:])
        at_step = lambda steps: functools.reduce(jnp.logical_and, [pl.program_id(a) == s for a, s in enumerate(steps)])

        @pl.when(at_step([0] * len(grid)))
        def _():
            for cp in copies + local:
                cp.start()

        body(*ins, *outs, *scr)

        if relays:
            @pl.when(at_step([grid[0] // 2] + [0] * (len(grid) - 1)))
            def _():
                _hand_on(relays)

        @pl.when(at_step([g - 1 for g in grid]))
        def _():
            _wait_all(copies, local, relays)

    hbm = pl.BlockSpec(memory_space=pltpu.HBM)
    aliases = {**kw.pop("input_output_aliases", {}), **_exchange_aliases(ride, n_in, n_out)}
    res = _pcall(
        riding, grid=grid, in_specs=list(in_specs) + [hbm] * nr, out_specs=list(out_specs) + [hbm] * nr,
        out_shape=list(out_shape) + [leg.landing() for leg in ride], input_output_aliases=aliases,
        scratch_shapes=list(scratch_shapes) + _exchange_sems(ride), **kw)(*args, *[leg.src for leg in ride])
    return res[:n_out], res[n_out:]


def _seq(n=1):
    return pltpu.CompilerParams(dimension_semantics=("arbitrary",) * n)


def _dot(a, b):
    return jnp.dot(a, b, preferred_element_type=F32)


def _dot_nt(a, b):
    return lax.dot_general(a, b, (((1,), (1,)), ((), ())), preferred_element_type=F32)


def _dot_tn(a, b):
    return lax.dot_general(a, b, (((0,), (0,)), ((), ())), preferred_element_type=F32)


def _bf(x):
    return x.astype(BF16)


def _sigmoid(x):
    return 0.5 * jnp.tanh(0.5 * x) + 0.5


def _log1p(z):
    u = 1.0 + z
    return jnp.where(u == 1.0, z, jnp.log(u) * (z / jnp.where(u == 1.0, 1.0, u - 1.0)))


def _softplus(x):
    return jnp.maximum(x, 0.0) + _log1p(jnp.exp(-jnp.abs(x)))


def _log_sigmoid(x):
    return -_softplus(-x)


def _one_minus_sq(a, log_a):
    x = 2.0 * log_a
    small = -x * (1.0 + x * (0.5 + x * (1.0 / 6.0)))
    return jnp.where(x > -0.004, small, 1.0 - a * a)


def _dsilu(x, s):
    return s * (1.0 + x * (1.0 - s))


def _rowsum(x):
    return jnp.sum(x, axis=1, keepdims=True)


def _colsum(x):
    return jnp.sum(x, axis=0, keepdims=True)


def _shift_down(win, s):
    return win if s == 0 else pltpu.roll(win, s, 0)


def _shift_up(win, s):
    return win if s == 0 else pltpu.roll(win, win.shape[0] - s, 0)


def _conv_taps(win):
    return [_shift_down(win, CONV_WIDTH - 1 - k)[HALO:] for k in range(CONV_WIDTH)]


def _conv_fwd(taps, w_ref, b_ref):
    acc = b_ref[...] + w_ref[CONV_WIDTH - 1:CONV_WIDTH, :] * taps[CONV_WIDTH - 1]
    for k in range(CONV_WIDTH - 1):
        acc = acc + w_ref[k:k + 1, :] * taps[k]
    return acc


def _split3(x):
    hi = _bf(x)
    r1 = x - hi.astype(F32)
    mid = _bf(r1)
    lo = _bf(r1 - mid.astype(F32))
    return hi, mid, lo


def _tri_dot_left(tri, x):
    hi, mid, lo = _split3(x)
    return _dot(tri, hi) + _dot(tri, mid) + _dot(tri, lo)


def _tri_dot_right(x, tri):
    hi, mid, lo = _split3(x)
    return _dot(hi, tri) + _dot(mid, tri) + _dot(lo, tri)


def _tile(n, want):
    t = min(n, want)
    assert n % t == 0
    return t


def _ln_inproj(x, g, scale, shift, w4, ride=None):
    s_len, d = x.shape
    nj, _, nsh = w4.shape
    tm = _tile(s_len, ROWS_MATMUL)

    def body(x_ref, g_ref, sc_ref, sh_ref, w_ref, h_ref, u_ref, hs):
        @pl.when(pl.program_id(1) == 0)
        def _():
            xv = x_ref[...]
            r = lax.rsqrt(jnp.mean(xv * xv, axis=-1, keepdims=True) + EPS)
            hv = (xv * r * g_ref[...]) * (1.0 + sc_ref[...]) + sh_ref[...]
            hs[...] = _bf(hv)
            h_ref[...] = hs[...]

        u_ref[...] = _dot(hs[...], w_ref[0])

    vec = pl.BlockSpec((1, d), lambda i, j: (0, 0))
    return _pcall_ride(
        body, ride, name="ln_inproj", grid=(s_len // tm, nj),
        in_specs=[pl.BlockSpec((tm, d), lambda i, j: (i, 0)), vec, vec, vec,
                  pl.BlockSpec((1, d, nsh), lambda i, j: (j, 0, 0))],
        out_specs=[pl.BlockSpec((tm, d), lambda i, j: (i, 0)), pl.BlockSpec((tm, nsh), lambda i, j: (i, j))],
        out_shape=[jax.ShapeDtypeStruct((s_len, d), BF16), jax.ShapeDtypeStruct((s_len, nj * nsh), F32)],
        scratch_shapes=[pltpu.VMEM((tm, d), BF16)],
        compiler_params=_seq(2),
        args=(x, g, scale, shift, w4))


def _rg_gates(xc, wa_ref, ba_ref, wx_ref, bx_ref, lam_ref):
    heads, hd, _ = wa_ref.shape
    xb = _bf(xc)
    ga = jnp.concatenate([_dot(xb[:, h * hd:(h + 1) * hd], wa_ref[h]) for h in range(heads)], axis=1) + ba_ref[...]
    gx = jnp.concatenate([_dot(xb[:, h * hd:(h + 1) * hd], wx_ref[h]) for h in range(heads)], axis=1) + bx_ref[...]
    r = _sigmoid(ga)
    ig = _sigmoid(gx)
    sp = _softplus(-lam_ref[...])
    log_a = (-RG_C) * r * sp
    a = jnp.exp(log_a)
    mult = jnp.sqrt(_one_minus_sq(a, log_a))
    return r, ig, sp, log_a, a, mult


def _scan_groups(a, u, reverse):
    n, c = a.shape
    a = a.reshape(n // 8, 8, c)
    u = u.reshape(n // 8, 8, c)
    row = lax.broadcasted_iota(jnp.int32, a.shape, 1)
    for k in (1, 2, 4):
        sft = 8 - k if reverse else k
        a_sh, u_sh = pltpu.roll(a, sft, 1), pltpu.roll(u, sft, 1)
        ok = row < 8 - k if reverse else row >= k
        u = jnp.where(ok, a * u_sh + u, u)
        a = jnp.where(ok, a * a_sh, a)
    return a.reshape(n, c), u.reshape(n, c)


def _rg_fwd(u, conv_w, conv_b, wa_b, ba, wx_b, bx, lam, ride=None):
    s_len = u.shape[0]
    d = conv_w.shape[1]
    tm = _tile(s_len, ROWS_VECTOR)
    per = tm // HALO

    def body(x_ref, xp_ref, z_ref, cw_ref, cb_ref, wa_ref, ba_ref, wx_ref, bx_ref, lam_ref,
             hh_ref, y_ref, carry):
        i = pl.program_id(0)

        @pl.when(i == 0)
        def _():
            carry[...] = jnp.zeros_like(carry)

        prev = jnp.where(i == 0, 0.0, xp_ref[...])
        xc = _conv_fwd(_conv_taps(jnp.concatenate([prev, x_ref[...]], axis=0)), cw_ref, cb_ref)
        _, ig, _, _, a, mult = _rg_gates(xc, wa_ref, ba_ref, wx_ref, bx_ref, lam_ref)
        ca, cu = _scan_groups(a, mult * (ig * xc), reverse=False)
        c = carry[0:1, :]
        for j in range(tm // 8):
            blk = ca[j * 8:(j + 1) * 8] * c + cu[j * 8:(j + 1) * 8]
            hh_ref[j * 8:(j + 1) * 8, :] = blk
            c = blk[7:8]
        carry[0:1, :] = c
        z = z_ref[...]
        y_ref[0] = _bf(hh_ref[...] * (z * _sigmoid(z)))

    vec = pl.BlockSpec((1, d), lambda i: (0, 0))
    whole3 = lambda a: pl.BlockSpec(a.shape, lambda i: (0, 0, 0))
    return _pcall_ride(
        body, ride, name="rg_fwd", grid=(s_len // tm,),
        in_specs=[pl.BlockSpec((tm, d), lambda i: (i, 0)),
                  pl.BlockSpec((HALO, d), lambda i: (jnp.maximum(i * per - 1, 0), 0)),
                  pl.BlockSpec((tm, d), lambda i: (i, 1)),
                  pl.BlockSpec((CONV_WIDTH, d), lambda i: (0, 0)), vec,
                  whole3(wa_b), vec, whole3(wx_b), vec, vec],
        out_specs=[pl.BlockSpec((tm, d), lambda i: (i, 0)), pl.BlockSpec((1, tm, d), lambda i: (0, i, 0))],
        out_shape=[jax.ShapeDtypeStruct((s_len, d), F32), jax.ShapeDtypeStruct((2, s_len, d), BF16)],
        scratch_shapes=[pltpu.VMEM((8, d), F32)],
        compiler_params=_seq(),
        args=(u, u, u, conv_w, conv_b, wa_b, ba, wx_b, bx, lam))


def _ml_pre(u, conv_w, conv_b, wqkv_b, wif_b, wift_b, b_if, b_ift):
    s_len = u.shape[0]
    d = conv_w.shape[1]
    _, heads, hd, _ = wqkv_b.shape
    ng = 2 * heads
    tm = _tile(s_len, max(ROWS_VECTOR, ML_CHUNK))
    per = tm // HALO

    def body(x_ref, xp_ref, cw_ref, cb_ref, w_ref, wif_ref, wift_ref, bif_ref, bift_ref,
             qkv_ref, gt_ref, gtt_ref, bc_ref, bct_ref):
        i = pl.program_id(0)
        prev = jnp.where(i == 0, 0.0, xp_ref[...])
        xm = x_ref[...]
        pre = _conv_fwd(_conv_taps(jnp.concatenate([prev, xm], axis=0)), cw_ref, cb_ref)
        xcb = _bf(pre * _sigmoid(pre))
        xmb = _bf(xm)
        for h in range(heads):
            hs = slice(h * hd, (h + 1) * hd)
            qkv_ref[0, :, hs] = _bf(_dot(xcb[:, hs], w_ref[0, h]))
            qkv_ref[1, :, hs] = _bf(_dot(xcb[:, hs], w_ref[1, h]))
            qkv_ref[2, :, hs] = _bf(_dot(xmb[:, hs], w_ref[2, h]))
        qb, kb, vb = qkv_ref[0], qkv_ref[1], qkv_ref[2]
        gt = (_dot(qb, wif_ref[0:d, :]) + _dot(kb, wif_ref[d:2 * d, :]) + _dot(vb, wif_ref[2 * d:3 * d, :])
              + bif_ref[...])
        gtt = (_dot_nt(wift_ref[:, 0:d], qb) + _dot_nt(wift_ref[:, d:2 * d], kb)
               + _dot_nt(wift_ref[:, 2 * d:3 * d], vb) + bift_ref[...])
        gt_ref[...] = gt
        gtt_ref[...] = gtt
        r = lax.broadcasted_iota(jnp.int32, (tm, tm), 0)
        c = lax.broadcasted_iota(jnp.int32, (tm, tm), 1)
        same = (r // ML_CHUNK) == (c // ML_CHUNK)
        bc_ref[...] = _tri_dot_left(((r >= c) & same).astype(BF16), _log_sigmoid(gt))
        bct_ref[...] = _tri_dot_right(_log_sigmoid(gtt), ((r <= c) & same).astype(BF16))

    vec = pl.BlockSpec((1, d), lambda i: (0, 0))
    whole2 = lambda a: pl.BlockSpec(a.shape, lambda i: (0, 0))
    col = pl.BlockSpec((tm, ng), lambda i: (i, 0))
    row = pl.BlockSpec((ng, tm), lambda i: (0, i))
    return _pcall(
        body, name="ml_pre", grid=(s_len // tm,),
        in_specs=[pl.BlockSpec((tm, d), lambda i: (i, 2)),
                  pl.BlockSpec((HALO, d), lambda i: (jnp.maximum(i * per - 1, 0), 2)),
                  pl.BlockSpec((CONV_WIDTH, d), lambda i: (0, 0)), vec,
                  pl.BlockSpec(wqkv_b.shape, lambda i: (0, 0, 0, 0)), whole2(wif_b), whole2(wift_b), whole2(b_if),
                  whole2(b_ift)],
        out_specs=[pl.BlockSpec((3, tm, d), lambda i: (0, i, 0)), col, row, col, row],
        out_shape=[jax.ShapeDtypeStruct((3, s_len, d), BF16), jax.ShapeDtypeStruct((s_len, ng), F32),
                   jax.ShapeDtypeStruct((ng, s_len), F32), jax.ShapeDtypeStruct((s_len, ng), F32),
                   jax.ShapeDtypeStruct((ng, s_len), F32)],
        compiler_params=_seq(),
    )(u, u, conv_w, conv_b, wqkv_b, wif_b, wift_b, b_if, b_ift)


def _chunk_gates(gt, gtt, bc, bct, h, heads):
    li_c = gt[:, h:h + 1]
    li_r = gtt[h:h + 1, :]
    gf_c = gt[:, heads + h:heads + h + 1]
    b_c = bc[:, heads + h:heads + h + 1]
    b_r = bct[heads + h:heads + h + 1, :]
    return li_c, li_r, gf_c, b_c, b_r


def _chunk_weights(li_c, li_r, b_c, b_r, m_prev, causal):
    lc = b_c.shape[0]
    b_last = b_c[lc - 1:lc, :]
    dmat = jnp.where(causal, b_c - b_r + li_r, -jnp.inf)
    m_inter = b_c + m_prev
    m_t = jnp.maximum(m_inter, jnp.max(dmat, axis=1, keepdims=True))
    w_intra = jnp.exp(dmat - m_t)
    w_inter = jnp.exp(m_inter - m_t)
    g_c = b_last - b_c + li_c
    m_new = jnp.maximum(b_last + m_prev, jnp.max(g_c, axis=0, keepdims=True))
    w_state = jnp.exp(g_c - m_new)
    decay = jnp.exp(b_last + m_prev - m_new)
    return m_t, w_intra, w_inter, m_new, w_state, decay


def _tri_masks(lc):
    r = lax.broadcasted_iota(jnp.int32, (lc, lc), 0)
    c = lax.broadcasted_iota(jnp.int32, (lc, lc), 1)
    causal = r >= c
    return causal, causal.astype(BF16), (r <= c).astype(BF16)


def _mlstm_fwd(qkv, gates, u, ml_g, ycat, ride=None):
    _, s_len, d = qkv.shape
    ng = gates[0].shape[1]
    heads = ng // 2
    hd = d // heads
    lc = ML_CHUNK
    nc = s_len // lc
    kscale = hd ** -0.5

    def body(qkv_ref, gt_ref, gtt_ref, bc_ref, bct_ref, o_ref, z_ref, g_ref, _, cell_ref, y_ref, cst_ref, nst_ref,
             mst_ref, cs, ns, ms):
        @pl.when(pl.program_id(0) == 0)
        def _():
            cs[...] = jnp.zeros_like(cs)
            ns[...] = jnp.zeros_like(ns)
            ms[...] = jnp.zeros_like(ms)

        causal = _tri_masks(lc)[0]
        gtv, gttv, bcv, bctv = gt_ref[...], gtt_ref[...], bc_ref[...], bct_ref[...]
        old = [(cs[h], ns[h], ms[h]) for h in range(heads)]
        new, cells, ys = [], [], []
        for h in range(heads):
            hs = slice(h * hd, (h + 1) * hd)
            li_c, li_r, _, b_c, b_r = _chunk_gates(gtv, gttv, bcv, bctv, h, heads)
            c_old, n_old, m_old = old[h]
            m_prev = m_old[:, 0:1]
            m_t, w_intra, w_inter, m_new, w_state, decay = _chunk_weights(li_c, li_r, b_c, b_r, m_prev, causal)
            qb = qkv_ref[0, :, hs]
            ks = qkv_ref[1, :, hs].astype(F32) * kscale
            kb = _bf(ks)
            vb = qkv_ref[2, :, hs]
            s = _dot_nt(qb, kb) * w_intra
            num = _dot(_bf(s), vb) + w_inter * _dot(qb, _bf(c_old))
            den = _rowsum(s) + w_inter * _rowsum(qb.astype(F32) * n_old)
            cell = num / jnp.maximum(jnp.abs(den), jnp.exp(-m_t))
            kw = ks * w_state
            new.append((decay * c_old + _dot_tn(_bf(kw), vb), decay * n_old + _colsum(kw),
                        jnp.broadcast_to(m_new, m_old.shape)))
            cells.append(cell)
            hm = _sigmoid(o_ref[:, hs]) * cell
            hn = hm * lax.rsqrt(jnp.mean(hm * hm, axis=-1, keepdims=True) + EPS)
            z = z_ref[:, hs]
            ys.append(_bf((hn * g_ref[:, hs]) * (z * _sigmoid(z))))
        for h in range(heads):
            cst_ref[0, h] = _bf(old[h][0])
            nst_ref[0, h] = old[h][1]
            mst_ref[0, h] = old[h][2]
            cs[h], ns[h], ms[h] = new[h]
        cell_ref[...] = jnp.concatenate(cells, axis=1)
        y_ref[0] = jnp.concatenate(ys, axis=1)

    row = pl.BlockSpec((lc, d), lambda c: (c, 0))
    gcol = pl.BlockSpec((lc, ng), lambda c: (c, 0))
    grow = pl.BlockSpec((ng, lc), lambda c: (0, c))
    return _pcall_ride(
        body, ride, name="mlstm_fwd", grid=(nc,),
        in_specs=[pl.BlockSpec((3, lc, d), lambda c: (0, c, 0)), gcol, grow, gcol, grow,
                  pl.BlockSpec((lc, d), lambda c: (c, 3)), pl.BlockSpec((lc, d), lambda c: (c, 4)),
                  pl.BlockSpec((1, d), lambda c: (0, 0)), pl.BlockSpec(memory_space=pl.ANY)],
        out_specs=[row, pl.BlockSpec((1, lc, d), lambda c: (1, c, 0)),
                   pl.BlockSpec((1, heads, hd, hd), lambda c: (c, 0, 0, 0)),
                   pl.BlockSpec((1, heads, 1, hd), lambda c: (c, 0, 0, 0)),
                   pl.BlockSpec((1, heads, 1, 128), lambda c: (c, 0, 0, 0))],
        out_shape=[jax.ShapeDtypeStruct((s_len, d), F32), jax.ShapeDtypeStruct(ycat.shape, BF16),
                   jax.ShapeDtypeStruct((nc, heads, hd, hd), BF16),
                   jax.ShapeDtypeStruct((nc, heads, 1, hd), F32),
                   jax.ShapeDtypeStruct((nc, heads, 1, 128), F32)],
        scratch_shapes=[pltpu.VMEM((heads, hd, hd), F32), pltpu.VMEM((heads, 1, hd), F32),
                        pltpu.VMEM((heads, 1, 128), F32)],
        input_output_aliases={8: 1},
        compiler_params=_seq(),
        args=(qkv, *gates, u, u, ml_g, ycat))


def _out_proj(ycat, w_out_b, x, gate, ride=None):
    s_len, d = x.shape
    tm = _tile(s_len, ROWS_MATMUL)

    def body(a_ref, w_ref, x_ref, g_ref, y_ref, xn_ref):
        y = _dot(a_ref[0], w_ref[0:d, :]) + _dot(a_ref[1], w_ref[d:2 * d, :])
        y_ref[...] = y
        xn_ref[...] = x_ref[...] + g_ref[...] * y

    row = pl.BlockSpec((tm, d), lambda i: (i, 0))
    return _pcall_ride(
        body, ride, name="out_proj", grid=(s_len // tm,),
        in_specs=[pl.BlockSpec((2, tm, d), lambda i: (0, i, 0)), pl.BlockSpec((2 * d, d), lambda i: (0, 0)), row,
                  pl.BlockSpec((1, d), lambda i: (0, 0))],
        out_specs=[row, row],
        out_shape=[jax.ShapeDtypeStruct((s_len, d), F32)] * 2,
        compiler_params=_seq(),
        args=(ycat, w_out_b, x, gate))


def _out_proj_loss(ycat, w_out_b, x, gate, g, target):
    s_len, d = x.shape
    tm = _tile(s_len, ROWS_IN_BWD)

    def body(a_ref, w_ref, x_ref, gate_ref, g_ref, t_ref, y_ref, dx_ref, dg_ref, loss_ref):
        @pl.when(pl.program_id(0) == 0)
        def _():
            dg_ref[...] = jnp.zeros_like(dg_ref)
            loss_ref[...] = jnp.zeros_like(loss_ref)

        y = _dot(a_ref[0], w_ref[0:d, :]) + _dot(a_ref[1], w_ref[d:2 * d, :])
        y_ref[...] = y
        xv = x_ref[...] + gate_ref[...] * y
        r = lax.rsqrt(jnp.mean(xv * xv, axis=-1, keepdims=True) + EPS)
        xn = xv * r
        err = xn * g_ref[...] - t_ref[...]
        loss_ref[...] += 0.5 * jnp.sum(jnp.mean(err * err, axis=-1, keepdims=True))
        dout = err * (1.0 / d)
        dg_ref[...] += _colsum(dout * xn)
        dxn = dout * g_ref[...]
        dx_ref[...] = r * (dxn - xn * jnp.mean(dxn * xn, axis=-1, keepdims=True))

    row = pl.BlockSpec((tm, d), lambda i: (i, 0))
    vec = pl.BlockSpec((1, d), lambda i: (0, 0))
    return _pcall(
        body, name="out_proj_loss", grid=(s_len // tm,),
        in_specs=[pl.BlockSpec((2, tm, d), lambda i: (0, i, 0)), pl.BlockSpec((2 * d, d), lambda i: (0, 0)), row, vec,
                  vec, row],
        out_specs=[row, row, vec, pl.BlockSpec((1, 128), lambda i: (0, 0))],
        out_shape=[jax.ShapeDtypeStruct((s_len, d), F32), jax.ShapeDtypeStruct((s_len, d), F32),
                   jax.ShapeDtypeStruct((1, d), F32), jax.ShapeDtypeStruct((1, 128), F32)],
        compiler_params=_seq(),
    )(ycat, w_out_b, x, gate, g, target)


def _out_bwd(dxn, y, gate, w_out_b):
    s_len, d = dxn.shape
    tm = _tile(s_len, ROWS_MATMUL)

    def body(dx_ref, y_ref, g_ref, w_ref, dg_ref, dy_ref, dc_ref):
        @pl.when(pl.program_id(0) == 0)
        def _():
            dg_ref[...] = jnp.zeros_like(dg_ref)

        dx = dx_ref[...]
        dg_ref[...] += _colsum(dx * y_ref[...])
        dy = _bf(g_ref[...] * dx)
        dy_ref[...] = dy
        dc_ref[0] = _dot_nt(dy, w_ref[0:d, :])
        dc_ref[1] = _dot_nt(dy, w_ref[d:2 * d, :])

    row = pl.BlockSpec((tm, d), lambda i: (i, 0))
    vec = pl.BlockSpec((1, d), lambda i: (0, 0))
    return _pcall(
        body, name="out_bwd", grid=(s_len // tm,),
        in_specs=[row, row, vec, pl.BlockSpec((2 * d, d), lambda i: (0, 0))],
        out_specs=[vec, row, pl.BlockSpec((2, tm, d), lambda i: (0, i, 0))],
        out_shape=[jax.ShapeDtypeStruct((1, d), F32), jax.ShapeDtypeStruct((s_len, d), BF16),
                   jax.ShapeDtypeStruct((2, s_len, d), F32)],
        compiler_params=_seq(),
    )(dxn, y, gate, w_out_b)


def _grad_matmul(a3, b3, nblk, a_idx, b_idx, out_shape, out_block, out_idx, ride=None):
    _, s_len, m = a3.shape
    n = b3.shape[2]
    tk = _tile(s_len, ROWS_GRAD_MATMUL)

    def body(a_ref, b_ref, o_ref):
        @pl.when(pl.program_id(1) == 0)
        def _():
            o_ref[...] = jnp.zeros_like(o_ref)

        o_ref[...] += _dot_tn(a_ref[0], b_ref[0])

    (out,), got = _pcall_ride(
        body, ride, name="grad_matmul", grid=(nblk, s_len // tk),
        in_specs=[pl.BlockSpec((1, tk, m), lambda p, t: (a_idx(p), t, 0)),
                  pl.BlockSpec((1, tk, n), lambda p, t: (b_idx(p), t, 0))],
        out_specs=[pl.BlockSpec((None,) + out_block, lambda p, t: (0,) + out_idx(p))],
        out_shape=[jax.ShapeDtypeStruct((1,) + out_shape, F32)],
        compiler_params=_seq(2), args=(a3, b3))
    return out, got


DU_PLANE = (2, 3, 4, 0, 1)


def _mlstm_bwd(qkv, gates, cst, nst, mst, cell, u, ml_g, d_ycat, wif_b, ride=None):
    _, s_len, d = qkv.shape
    ng = gates[0].shape[1]
    heads = ng // 2
    hd = d // heads
    lc = ML_CHUNK
    nc = s_len // lc
    kscale = hd ** -0.5

    def body(qkv_ref, gt_ref, gtt_ref, bc_ref, bct_ref, cst_ref, nst_ref, mst_ref, cell_ref, o_ref, z_ref, g_ref, dy_ref,
             wif_ref, dqkv_ref, dgt_ref, dbif_ref, du_ref, dg_ref, dcs, dns):
        @pl.when(pl.program_id(0) == 0)
        def _():
            dbif_ref[...] = jnp.zeros_like(dbif_ref)
            dcs[...] = jnp.zeros_like(dcs)
            dns[...] = jnp.zeros_like(dns)
            dg_ref[...] = jnp.zeros_like(dg_ref)

        causal, tril, triu = _tri_masks(lc)
        tril_strict = (tril.astype(F32) - (tril * triu).astype(F32)).astype(BF16)
        gtv, gttv, bcv, bctv = gt_ref[...], gtt_ref[...], bc_ref[...], bct_ref[...]
        lane = lax.broadcasted_iota(jnp.int32, (lc, ng), 1)
        dli_all = jnp.zeros((lc, ng), F32)
        from_later = jnp.zeros((lc, ng), F32)
        from_earlier = jnp.zeros((lc, ng), F32)
        across_all = jnp.zeros((1, ng), F32)
        old = [(dcs[h], dns[h]) for h in range(heads)]
        new, d_o, d_z, d_g, dqs, dks, dvs = [], [], [], [], [], [], []
        for h in range(heads):
            hs = slice(h * hd, (h + 1) * hd)
            li_c, li_r, gf_c, b_c, b_r = _chunk_gates(gtv, gttv, bcv, bctv, h, heads)
            m_prev = mst_ref[0, h][:, 0:1]
            m_t, w_intra, w_inter, _, w_state, decay = _chunk_weights(li_c, li_r, b_c, b_r, m_prev, causal)
            qb = qkv_ref[0, :, hs]
            qf = qb.astype(F32)
            ks = qkv_ref[1, :, hs].astype(F32) * kscale
            kb = _bf(ks)
            vb = qkv_ref[2, :, hs]
            c_b = cst_ref[0, h]
            n_old = nst_ref[0, h]
            s = _dot_nt(qb, kb) * w_intra
            den = _rowsum(s) + w_inter * _rowsum(qf * n_old)
            floor = jnp.exp(-m_t)
            dstab = jnp.maximum(jnp.abs(den), floor)
            cell = cell_ref[:, hs]
            o = o_ref[:, hs]
            so = _sigmoid(o)
            hm = so * cell
            rinv = lax.rsqrt(jnp.mean(hm * hm, axis=-1, keepdims=True) + EPS)
            hn = hm * rinv
            z = z_ref[:, hs]
            sgz = _sigmoid(z)
            sz = z * sgz
            gh = g_ref[:, hs]
            dy = dy_ref[0, :, hs]
            d_z.append(_bf(dy * (hn * gh) * _dsilu(z, sgz)))
            d_g.append(_colsum(dy * hn * sz))
            dhn = dy * gh * sz
            dhm = rinv * (dhn - hn * jnp.mean(dhn * hn, axis=-1, keepdims=True))
            d_o.append(_bf(dhm * cell * so * (1.0 - so)))
            dcell = dhm * so
            dnum = dcell / dstab
            dnb = _bf(dnum)
            dden = -_rowsum(dcell * cell) / dstab * jnp.where(jnp.abs(den) > floor, jnp.where(den > 0.0, 1.0, -1.0), 0.0)
            dst = _dot_nt(dnb, vb) + dden
            dsdb = _bf(dst * w_intra)
            dc_out, dn_out = old[h]
            dcb = _bf(dc_out)
            dq_inter = w_inter * (_dot_nt(dnb, c_b) + dden * n_old)
            dk_inter = w_state * (_dot_nt(vb, dcb) + dn_out)
            dq = _dot(dsdb, kb) + dq_inter
            dk = _dot_tn(dsdb, qb) + dk_inter
            dv = _dot_tn(_bf(s), dnb) + _dot(_bf(ks * w_state), dcb)
            wq = w_inter * qf
            new.append((decay * dc_out + _dot_tn(_bf(wq), dnb), decay * dn_out + _colsum(wq * dden)))
            pmat = dst * s
            p_rows = _rowsum(pmat)
            p_cols = _rowsum(pmat.T)
            q_in = _rowsum(qf * dq_inter)
            k_in = _rowsum(ks * dk_inter)
            across = decay * (jnp.sum(dc_out * c_b.astype(F32), keepdims=True) + jnp.sum(dn_out * n_old, keepdims=True))
            dli_all = dli_all + jnp.where(lane == h, p_cols + k_in, 0.0)
            from_later = from_later + jnp.where(lane == heads + h, p_rows - p_cols + q_in, 0.0)
            from_earlier = from_earlier + jnp.where(lane == heads + h, k_in, 0.0)
            across_all = across_all + jnp.where(lane[0:1] == heads + h, across, 0.0)
            dqs.append(dq)
            dks.append(dk * kscale)
            dvs.append(dv)
        for h in range(heads):
            dcs[h], dns[h] = new[h]
        du_ref[0] = jnp.concatenate(d_o, axis=1)
        du_ref[1] = jnp.concatenate(d_z, axis=1)
        dg_ref[...] += jnp.concatenate(d_g, axis=1)
        dlf = _tri_dot_left(triu, from_later) + _tri_dot_left(tril_strict, from_earlier) + across_all
        dgt = dli_all + dlf * _sigmoid(-gtv)
        dgt_ref[...] = dgt
        dbif_ref[...] += _colsum(dgt)
        dgb = _bf(dgt)
        dqkv_ref[0] = _bf(jnp.concatenate(dqs, axis=1) + _dot_nt(dgb, wif_ref[0:d, :]))
        dqkv_ref[1] = _bf(jnp.concatenate(dks, axis=1) + _dot_nt(dgb, wif_ref[d:2 * d, :]))
        dqkv_ref[2] = _bf(jnp.concatenate(dvs, axis=1) + _dot_nt(dgb, wif_ref[2 * d:3 * d, :]))

    rev = lambda c: nc - 1 - c
    row = pl.BlockSpec((lc, d), lambda c: (rev(c), 0))
    gcol = pl.BlockSpec((lc, ng), lambda c: (rev(c), 0))
    grow = pl.BlockSpec((ng, lc), lambda c: (0, rev(c)))
    return _pcall_ride(
        body, ride, name="mlstm_bwd", grid=(nc,),
        in_specs=[pl.BlockSpec((3, lc, d), lambda c: (0, rev(c), 0)), gcol, grow, gcol, grow,
                  pl.BlockSpec((1, heads, hd, hd), lambda c: (rev(c), 0, 0, 0)),
                  pl.BlockSpec((1, heads, 1, hd), lambda c: (rev(c), 0, 0, 0)),
                  pl.BlockSpec((1, heads, 1, 128), lambda c: (rev(c), 0, 0, 0)),
                  row, pl.BlockSpec((lc, d), lambda c: (rev(c), 3)), pl.BlockSpec((lc, d), lambda c: (rev(c), 4)),
                  pl.BlockSpec((1, d), lambda c: (0, 0)), pl.BlockSpec((1, lc, d), lambda c: (1, rev(c), 0)),
                  pl.BlockSpec((3 * d, ng), lambda c: (0, 0))],
        out_specs=[pl.BlockSpec((3, lc, d), lambda c: (0, rev(c), 0)), pl.BlockSpec((lc, ng), lambda c: (rev(c), 0)),
                   pl.BlockSpec((1, ng), lambda c: (0, 0)), pl.BlockSpec((2, lc, d), lambda c: (0, rev(c), 0)),
                   pl.BlockSpec((1, d), lambda c: (0, 0))],
        out_shape=[jax.ShapeDtypeStruct((3, s_len, d), BF16), jax.ShapeDtypeStruct((s_len, ng), F32),
                   jax.ShapeDtypeStruct((1, ng), F32), jax.ShapeDtypeStruct((5, s_len, d), BF16),
                   jax.ShapeDtypeStruct((1, d), F32)],
        scratch_shapes=[pltpu.VMEM((heads, hd, hd), F32), pltpu.VMEM((heads, 1, hd), F32)],
        compiler_params=_seq(),
        args=(qkv, *gates, cst, nst, mst, cell, u, u, ml_g, d_ycat, wif_b))


def _conv_bwd_tile(dp, later, taps, cw_ref, gw_ref, gb_ref):
    tm = dp.shape[0]
    dwin = jnp.concatenate([dp, later[...]], axis=0)
    later[...] = dp[0:HALO]
    acc = cw_ref[CONV_WIDTH - 1:CONV_WIDTH, :] * dp
    for k in range(CONV_WIDTH):
        if k < CONV_WIDTH - 1:
            acc = acc + cw_ref[k:k + 1, :] * _shift_up(dwin, CONV_WIDTH - 1 - k)[0:tm]
        gw_ref[k:k + 1, :] += _colsum(dp * taps[k])
    gb_ref[...] += _colsum(dp)
    return acc


def _ml_pre_bwd(dqkv, u, conv_w, conv_b, wqkv_b, du):
    s_len = u.shape[0]
    d = conv_w.shape[1]
    _, heads, hd, _ = wqkv_b.shape
    tm = _tile(s_len, ROWS_VECTOR)
    per = tm // HALO
    nt = s_len // tm

    def body(dqkv_ref, x_ref, xp_ref, cw_ref, cb_ref, w_ref, _, dx_ref, gw_ref, gcw_ref, gcb_ref, later, dps, dxs):
        i = pl.program_id(0)

        @pl.when(i == 0)
        def _():
            gw_ref[...] = jnp.zeros_like(gw_ref)
            gcw_ref[...] = jnp.zeros_like(gcw_ref)
            gcb_ref[...] = jnp.zeros_like(gcb_ref)
            later[...] = jnp.zeros_like(later)

        prev = jnp.where(i == nt - 1, 0.0, xp_ref[...])
        xm = x_ref[...]
        taps = _conv_taps(jnp.concatenate([prev, xm], axis=0))
        pre = _conv_fwd(taps, cw_ref, cb_ref)
        sg = _sigmoid(pre)
        xcb = _bf(pre * sg)
        xmb = _bf(xm)
        for h in range(heads):
            hs = slice(h * hd, (h + 1) * hd)
            dqh, dkh, dvh = dqkv_ref[0, :, hs], dqkv_ref[1, :, hs], dqkv_ref[2, :, hs]
            dxc = _dot_nt(dqh, w_ref[0, h]) + _dot_nt(dkh, w_ref[1, h])
            dps[:, hs] = dxc * _dsilu(pre[:, hs], sg[:, hs])
            dxs[:, hs] = _dot_nt(dvh, w_ref[2, h])
            gw_ref[0, h] += _dot_tn(xcb[:, hs], dqh)
            gw_ref[1, h] += _dot_tn(xcb[:, hs], dkh)
            gw_ref[2, h] += _dot_tn(xmb[:, hs], dvh)
        dx_ref[0] = _bf(_conv_bwd_tile(dps[...], later, taps, cw_ref, gcw_ref, gcb_ref) + dxs[...])

    rev = lambda i: nt - 1 - i
    vec = pl.BlockSpec((1, d), lambda i: (0, 0))
    cwb = pl.BlockSpec((CONV_WIDTH, d), lambda i: (0, 0))
    whole4 = pl.BlockSpec(wqkv_b.shape, lambda i: (0, 0, 0, 0))
    return _pcall(
        body, name="ml_pre_bwd", grid=(nt,),
        in_specs=[pl.BlockSpec((3, tm, d), lambda i: (0, rev(i), 0)), pl.BlockSpec((tm, d), lambda i: (rev(i), 2)),
                  pl.BlockSpec((HALO, d), lambda i: (jnp.maximum(rev(i) * per - 1, 0), 2)),
                  cwb, vec, whole4, pl.BlockSpec(memory_space=pl.ANY)],
        out_specs=[pl.BlockSpec((1, tm, d), lambda i: (DU_PLANE[2], rev(i), 0)), whole4, cwb, vec],
        out_shape=[jax.ShapeDtypeStruct(du.shape, BF16), jax.ShapeDtypeStruct(wqkv_b.shape, F32),
                   jax.ShapeDtypeStruct((CONV_WIDTH, d), F32), jax.ShapeDtypeStruct((1, d), F32)],
        scratch_shapes=[pltpu.VMEM((HALO, d), F32), pltpu.VMEM((tm, d), F32), pltpu.VMEM((tm, d), F32)],
        input_output_aliases={6: 0},
        compiler_params=_seq(),
    )(dqkv, u, u, conv_w, conv_b, wqkv_b, du)


def _rg_bwd(d_ycat, u, hh, conv_w, conv_b, wa_b, ba, wx_b, bx, lam, du):
    s_len = u.shape[0]
    d = conv_w.shape[1]
    heads, hd, _ = wa_b.shape
    tm = _tile(s_len, ROWS_VECTOR)
    per = tm // HALO
    nt = s_len // tm

    def body(dy_ref, x_ref, xp_ref, z_ref, hh_ref, hp_ref, cw_ref, cb_ref, wa_ref, ba_ref, wx_ref, bx_ref, lam_ref, _,
             du_ref, gwa_ref, gwx_ref, gba_ref, gbx_ref, glam_ref, gcw_ref, gcb_ref, carry, gbuf, later, dxcs):
        i = pl.program_id(0)
        first = i == nt - 1

        @pl.when(i == 0)
        def _():
            carry[...] = jnp.zeros_like(carry)
            later[...] = jnp.zeros_like(later)
            gwa_ref[...] = jnp.zeros_like(gwa_ref)
            gwx_ref[...] = jnp.zeros_like(gwx_ref)
            gba_ref[...] = jnp.zeros_like(gba_ref)
            gbx_ref[...] = jnp.zeros_like(gbx_ref)
            glam_ref[...] = jnp.zeros_like(glam_ref)
            gcw_ref[...] = jnp.zeros_like(gcw_ref)
            gcb_ref[...] = jnp.zeros_like(gcb_ref)

        prev = jnp.where(first, 0.0, xp_ref[...])
        taps = _conv_taps(jnp.concatenate([prev, x_ref[...]], axis=0))
        xc = _conv_fwd(taps, cw_ref, cb_ref)
        r, ig, sp, log_a, a, mult = _rg_gates(xc, wa_ref, ba_ref, wx_ref, bx_ref, lam_ref)
        z = z_ref[...]
        sgz = _sigmoid(z)
        dy = dy_ref[0]
        hh_v = hh_ref[...]
        du_ref[1] = _bf(dy * hh_v * _dsilu(z, sgz))
        dhh = dy * (z * sgz)
        rows = lax.broadcasted_iota(jnp.int32, a.shape, 0)
        coef = jnp.where(rows == tm - 1, carry[1:2, :], _shift_up(a, 1))
        ca, cu = _scan_groups(coef, dhh, reverse=True)
        c = carry[0:1, :]
        for j in range(tm // 8 - 1, -1, -1):
            blk = ca[j * 8:(j + 1) * 8] * c + cu[j * 8:(j + 1) * 8]
            gbuf[j * 8:(j + 1) * 8, :] = blk
            c = blk[0:1]
        carry[0:1, :] = c
        carry[1:2, :] = a[0:1]
        g = gbuf[...]
        hprev_tile = jnp.where(first, 0.0, hp_ref[...])
        hprev = _shift_down(jnp.concatenate([hprev_tile, hh_v], axis=0), 1)[HALO:]
        da = g * hprev
        gx_ = g * xc
        d_mult = gx_ * ig
        d_ig = gx_ * mult
        dxc = g * mult * ig
        dlog_a = da * a - d_mult * (a * a / mult)
        d_r = dlog_a * ((-RG_C) * sp)
        glam_ref[...] += _colsum(dlog_a * ((-RG_C) * r)) * (-_sigmoid(-lam_ref[...]))
        d_ga = d_r * r * (1.0 - r)
        d_gx = d_ig * ig * (1.0 - ig)
        gba_ref[...] += _colsum(d_ga)
        gbx_ref[...] += _colsum(d_gx)
        xb = _bf(xc)
        dgab = _bf(d_ga)
        dgxb = _bf(d_gx)
        for h in range(heads):
            hs = slice(h * hd, (h + 1) * hd)
            dxcs[:, hs] = dxc[:, hs] + _dot_nt(dgab[:, hs], wa_ref[h]) + _dot_nt(dgxb[:, hs], wx_ref[h])
            gwa_ref[h] += _dot_tn(xb[:, hs], dgab[:, hs])
            gwx_ref[h] += _dot_tn(xb[:, hs], dgxb[:, hs])
        du_ref[0] = _bf(_conv_bwd_tile(dxcs[...], later, taps, cw_ref, gcw_ref, gcb_ref))

    assert DU_PLANE[0] % 2 == 0 and DU_PLANE[1] == DU_PLANE[0] + 1
    rev = lambda i: nt - 1 - i
    row = pl.BlockSpec((tm, d), lambda i: (rev(i), 0))
    halo_prev = lambda col: pl.BlockSpec((HALO, d), lambda i: (jnp.maximum(rev(i) * per - 1, 0), col))
    vec = pl.BlockSpec((1, d), lambda i: (0, 0))
    cwb = pl.BlockSpec((CONV_WIDTH, d), lambda i: (0, 0))
    whole3 = lambda a: pl.BlockSpec(a.shape, lambda i: (0, 0, 0))
    return _pcall(
        body, name="rg_bwd", grid=(nt,),
        in_specs=[pl.BlockSpec((1, tm, d), lambda i: (0, rev(i), 0)), row, halo_prev(0),
                  pl.BlockSpec((tm, d), lambda i: (rev(i), 1)), row, halo_prev(0),
                  cwb, vec, whole3(wa_b), vec, whole3(wx_b), vec, vec, pl.BlockSpec(memory_space=pl.ANY)],
        out_specs=[pl.BlockSpec((2, tm, d), lambda i: (DU_PLANE[0] // 2, rev(i), 0)), whole3(wa_b), whole3(wa_b),
                   vec, vec, vec, cwb, vec],
        out_shape=[jax.ShapeDtypeStruct(du.shape, BF16), jax.ShapeDtypeStruct(wa_b.shape, F32),
                   jax.ShapeDtypeStruct(wa_b.shape, F32)] + [jax.ShapeDtypeStruct((1, d), F32)] * 3
        + [jax.ShapeDtypeStruct((CONV_WIDTH, d), F32), jax.ShapeDtypeStruct((1, d), F32)],
        scratch_shapes=[pltpu.VMEM((8, d), F32), pltpu.VMEM((tm, d), F32), pltpu.VMEM((HALO, d), F32),
                        pltpu.VMEM((tm, d), F32)],
        input_output_aliases={13: 0},
        compiler_params=_seq(),
    )(d_ycat, u, u, u, hh, hh, conv_w, conv_b, wa_b, ba, wx_b, bx, lam, du)


def _in_bwd(du, w4, x, dxn, g, scale, ride=None):
    s_len, d = x.shape
    tm = _tile(s_len, ROWS_IN_BWD)
    nsh_chips, _, nsh = w4.shape
    npc = du.shape[0]
    ck = d // 4
    assert nsh % ck == 0 and npc * d == nsh_chips * nsh

    def body(du_ref, w_ref, x_ref, dxn_ref, g_ref, sc_ref, dx_ref, dsh_ref, dsc_ref, dg_ref):
        @pl.when(pl.program_id(0) == 0)
        def _():
            dsh_ref[...] = jnp.zeros_like(dsh_ref)
            dsc_ref[...] = jnp.zeros_like(dsc_ref)
            dg_ref[...] = jnp.zeros_like(dg_ref)

        dh = None
        for q in range(npc * d // ck):
            col = q * ck
            p, pc = col // d, col % d
            s, sc = col // nsh, col % nsh
            t = _dot_nt(du_ref[DU_PLANE[p], :, pc:pc + ck], w_ref[s, :, sc:sc + ck])
            dh = t if dh is None else dh + t
        xv = x_ref[...]
        r = lax.rsqrt(jnp.mean(xv * xv, axis=-1, keepdims=True) + EPS)
        xn = xv * r
        gv = g_ref[...]
        onesc = 1.0 + sc_ref[...]
        dsh_ref[...] += _colsum(dh)
        dsc_ref[...] += _colsum(dh * (xn * gv))
        dg_ref[...] += _colsum(dh * xn * onesc)
        dxh = dh * (gv * onesc)
        dx_ref[...] = dxn_ref[...] + r * (dxh - xn * jnp.mean(dxh * xn, axis=-1, keepdims=True))

    row = pl.BlockSpec((tm, d), lambda i: (i, 0))
    vec = pl.BlockSpec((1, d), lambda i: (0, 0))
    return _pcall_ride(
        body, ride, name="in_bwd", grid=(s_len // tm,),
        in_specs=[pl.BlockSpec((npc, tm, d), lambda i: (0, i, 0)), pl.BlockSpec(w4.shape, lambda i: (0, 0, 0)), row, row,
                  vec, vec],
        out_specs=[row, vec, vec, vec],
        out_shape=[jax.ShapeDtypeStruct((s_len, d), F32)] + [jax.ShapeDtypeStruct((1, d), F32)] * 3,
        compiler_params=_seq(),
        args=(du, w4, x, dxn, g, scale))


def _layer_fwd(x, p, rides=None, loss_head=None):
    rides = rides or {}
    landed = {}
    ride = lambda kernel: rides[kernel](landed) if kernel in rides else None
    (h_b, u), landed["ln_inproj"] = _ln_inproj(x, p["norm_g"], p["scale"], p["shift"], p["w4"], ride("ln_inproj"))
    (hh, ycat), landed["rg_fwd"] = _rg_fwd(u, p["rg_conv_w"], p["rg_conv_b"], p["rg_wa_b"], p["rg_ba"], p["rg_wx_b"],
                                           p["rg_bx"], p["rg_lam"], ride("rg_fwd"))
    if "late" in rides:
        p = {**p, **rides["late"](landed)}
    qkv, *gates = _ml_pre(u, p["ml_conv_w"], p["ml_conv_b"], p["wqkv_b"], p["wif_b"], p["wift_b"], p["b_if"],
                          p["b_ift"])
    (cell, ycat, cst, nst, mst), landed["mlstm_fwd"] = _mlstm_fwd(qkv, gates, u, p["ml_g"], ycat, ride("mlstm_fwd"))
    if loss_head is None:
        (y, x_new), landed["out_proj"] = _out_proj(ycat, p["w_out_b"], x, p["gate"], ride("out_proj"))
    else:
        y, *x_new = _out_proj_loss(ycat, p["w_out_b"], x, p["gate"], *loss_head)
    saved = dict(x=x, h_b=h_b, u=u, hh=hh, qkv=qkv, gates=gates, cell=cell, ycat=ycat, cst=cst, nst=nst, mst=mst, y=y)
    return x_new, saved, p, landed


def _layer_bwd(dxn, p, s, rides=None):
    rides = rides or {}
    landed = {}
    ride = lambda kernel: rides[kernel](grads, landed) if kernel in rides else None
    u = s["u"]
    d = dxn.shape[1]
    d_gate, dy_b, d_ycat = _out_bwd(dxn, s["y"], p["gate"], p["w_out_b"])
    grads = dict(w_out=_grad_matmul(s["ycat"], dy_b[None], 2, lambda b: b, lambda b: 0, (2 * d, d), (d, d),
                                    lambda b: (b, 0))[0])
    (dqkv, dgt, g_b_if, du, g_ml_g), landed["mlstm_bwd"] = _mlstm_bwd(
        s["qkv"], s["gates"], s["cst"], s["nst"], s["mst"], s["cell"], u, p["ml_g"], d_ycat, p["wif_b"],
        ride("mlstm_bwd"))
    ng = dgt.shape[1]
    g_w_if = _grad_matmul(s["qkv"], _bf(dgt)[None], 3, lambda b: b, lambda b: 0, (3 * d, ng), (d, ng),
                          lambda b: (b, 0))[0][0]
    du, g_wqkv, g_ml_cw, g_ml_cb = _ml_pre_bwd(dqkv, u, p["ml_conv_w"], p["ml_conv_b"], p["wqkv_b"], du)
    du, g_wa, g_wx, g_ba, g_bx, g_lam, g_rg_cw, g_rg_cb = _rg_bwd(d_ycat, u, s["hh"], p["rg_conv_w"], p["rg_conv_b"],
                                                                  p["rg_wa_b"], p["rg_ba"], p["rg_wx_b"], p["rg_bx"],
                                                                  p["rg_lam"], du)
    grads.update(rg_conv_w=g_rg_cw, rg_conv_b=g_rg_cb, rg_w_a=g_wa, rg_b_a=g_ba, rg_w_x=g_wx, rg_b_x=g_bx,
                 rg_lambda=g_lam, ml_conv_w=g_ml_cw, ml_conv_b=g_ml_cb, ml_w_qkv=g_wqkv, ml_w_if=g_w_if, ml_b_if=g_b_if,
                 ml_norm_g=g_ml_g)
    npc = du.shape[0]
    grads["w_in"], landed["grad_w_in"] = _grad_matmul(
        s["h_b"][None], du, npc, lambda b: 0, lambda b: (b + DU_PLANE[0]) % npc, (d, npc * d), (d, d),
        lambda b: (0, b), ride("grad_w_in"))
    (dx, d_shift, d_scale, grads["norm_g"]), landed["in_bwd"] = _in_bwd(du, p["w4"], s["x"], dxn, p["norm_g"],
                                                                        p["scale"], ride("in_bwd"))
    return dx, grads, jnp.concatenate([d_shift, d_scale, d_gate], axis=1), landed


def _me():
    return lax.axis_index("x"), lax.axis_index("y"), lax.axis_index("c")


def _remote(src, dst, send_sem, recv_sem, to):
    return pltpu.make_async_remote_copy(src_ref=src, dst_ref=dst, send_sem=send_sem, recv_sem=recv_sem,
                                        device_id=to, device_id_type=MESH)


def _all_gather8(blocks, space):
    n = len(blocks)
    relay = [b.size * b.dtype.itemsize >= RELAY_BYTES and b.shape[0] % 32 == 0 for b in blocks]

    def body(*refs):
        x_refs, out_refs = refs[:n], refs[n:2 * n]
        send_sems, recv_sems, local_sems = refs[2 * n:]
        x, y, c = _me()
        me, sibling = (x, y, c), (x, y, 1 - c)
        by_x, by_y, across = (1 - x, y, c), (x, 1 - y, c), (1 - x, 1 - y, c)

        def rows(i, blk, part=None):
            m_per = blocks[i].shape[0]
            at = (4 * blk[0] + 2 * blk[1] + blk[2]) * m_per
            if part is not None:
                m_per //= 2
                at += part * m_per
            return out_refs[i].at[pl.ds(at, m_per), :]

        def copy(i, k, blk, to, src=None, part=None):
            return _remote(rows(i, blk, part) if src is None else src, rows(i, blk, part), send_sems.at[8 * i + k],
                           recv_sems.at[8 * i + k], to)

        mine = [pltpu.make_async_copy(x_refs[i], rows(i, me), local_sems.at[i]) for i in range(n)]
        first = []
        for i in range(n):
            first.append(copy(i, 0, me, sibling, src=x_refs[i]))
            first += [copy(i, 1, me, by_x, src=x_refs[i]), copy(i, 2, me, by_y, src=x_refs[i])]
            if not relay[i]:
                first.append(copy(i, 3, me, across, src=x_refs[i]))
        for cp in mine + first:
            cp.start()
        passed = []
        for i in range(n):
            copy(i, 1, by_x, me).wait_recv()
            passed.append(copy(i, 4, by_x, sibling))
            if relay[i]:
                passed.append(copy(i, 3, by_x, by_y, part=0))
        for cp in passed:
            cp.start()
        n_x = len(passed)
        for i in range(n):
            copy(i, 2, by_y, me).wait_recv()
            passed.append(copy(i, 5, by_y, sibling))
            if relay[i]:
                passed.append(copy(i, 7, by_y, by_x, part=1))
        for cp in passed[n_x:]:
            cp.start()
        for i in range(n):
            if relay[i]:
                copy(i, 3, across, me, part=0).wait_recv()
                copy(i, 7, across, me, part=1).wait_recv()
            else:
                copy(i, 3, across, me).wait_recv()
            passed.append(copy(i, 6, across, sibling))
            passed[-1].start()
        for i in range(n):
            copy(i, 0, sibling, me).wait_recv()
            for k, blk in ((4, by_x), (5, by_y), (6, across)):
                copy(i, k, (blk[0], blk[1], 1 - c), me).wait_recv()
        for cp in first + passed:
            cp.wait_send()
        for cp in mine:
            cp.wait()

    spec = pl.BlockSpec(memory_space=space)
    return _pcall(
        body, name="all_gather8",
        out_shape=[jax.ShapeDtypeStruct((8 * b.shape[0], b.shape[1]), b.dtype) for b in blocks],
        in_specs=[spec] * n, out_specs=[spec] * n,
        scratch_shapes=[pltpu.SemaphoreType.DMA((8 * n,)), pltpu.SemaphoreType.DMA((8 * n,)),
                        pltpu.SemaphoreType.DMA((n,))],
    )(*blocks)


def _exchange(legs):
    n = len(legs)

    def body(*refs):
        copies, local, relays = _exchange_body(legs, refs[:n], refs[n:2 * n], *refs[2 * n:])
        for cp in copies + local:
            cp.start()
        _hand_on(relays)
        _wait_all(copies, local, relays)

    hbm = pl.BlockSpec(memory_space=pltpu.HBM)
    return _pcall(body, name="exchange", out_shape=[leg.landing() for leg in legs], in_specs=[hbm] * n,
                  out_specs=[hbm] * n, input_output_aliases=_exchange_aliases(legs, 0, 0),
                  scratch_shapes=_exchange_sems(legs))(*[leg.src for leg in legs])


def _row_tile(rows, cap=4096, mult=16):
    best = None
    for t in range(mult, min(rows, cap) + 1, mult):
        if rows % t == 0:
            best = t
    return rows if best is None else best


def _pair_sum(half, own, own_spec, got, got_spec, out_shape, out_spec, grid):
    def body(_, a_ref, b_ref, o_ref):
        o_ref[...] = (a_ref[...] + b_ref[...].astype(F32)).astype(o_ref.dtype)

    return _pcall(
        body, name="pair_sum",
        grid_spec=pltpu.PrefetchScalarGridSpec(num_scalar_prefetch=1, grid=grid, in_specs=[own_spec, got_spec],
                                               out_specs=out_spec),
        out_shape=out_shape, compiler_params=_seq(len(grid)))(half, own, got)


def _chip_sum(ids, part, met, fill, layer=0, stack=1):
    _, _, rows, n = part.shape
    tr = _row_tile(rows, cap=max(16, BLOCK_ELEMS // n))
    first = isinstance(stack, int)

    def body(_, own_ref, a_ref, b_ref, c_ref, *rest):
        acc = own_ref[...].astype(F32) + a_ref[...].astype(F32)
        acc = acc + b_ref[...].astype(F32)
        rest[-1][...] = acc + c_ref[...].astype(F32)

    blk = (None, None, tr, n)
    other = lambda k: pl.BlockSpec(blk, lambda j, ids: ((ids[0] + k) % 4, 0, j, 0))
    in_specs = [pl.BlockSpec(blk, lambda j, ids: (ids[0], 0, j, 0)), other(1), other(2), other(3)]
    return _pcall(
        body, name="chip_sum",
        grid_spec=pltpu.PrefetchScalarGridSpec(
            num_scalar_prefetch=1, grid=(rows // tr,),
            in_specs=in_specs if first else in_specs + [pl.BlockSpec(memory_space=pl.ANY)],
            out_specs=pl.BlockSpec(blk, lambda j, ids: (layer, ids[1] if fill else 0, j, 0))),
        out_shape=jax.ShapeDtypeStruct(((stack,) if first else stack.shape[:1]) + (2 if fill else 1, rows, n), F32),
        input_output_aliases={} if first else {5: 0},
        compiler_params=_seq())(*((ids, part, met, met, met) if first else (ids, part, met, met, met, stack)))


def _ada_mod(c_all, w_ada, b_ada_cols):
    depth, d, n = w_ada.shape
    nb = c_all.shape[0]

    def body(c_ref, w_ref, b_ref, o_ref):
        cv = c_ref[...]
        ca = _bf(cv * _sigmoid(cv))
        o_ref[0] = _dot(ca, _bf(w_ref[0])) + b_ref[0]

    return _pcall(body, name="ada_mod", grid=(depth,),
                  in_specs=[pl.BlockSpec((nb, d), lambda l: (0, 0)), pl.BlockSpec((1, d, n), lambda l: (l, 0, 0)),
                            pl.BlockSpec((1, 1, n), lambda l: (l, 0, 0))],
                  out_specs=pl.BlockSpec((1, nb, n), lambda l: (l, 0, 0)),
                  out_shape=jax.ShapeDtypeStruct((depth, nb, n), F32), compiler_params=_seq())(c_all, w_ada, b_ada_cols)


def _ada_grad(c_all, dmod_cols, rows_all):
    nb, d = c_all.shape
    depth, _, n = dmod_cols.shape
    kinds, n_all = rows_all.shape[1], rows_all.shape[3]

    def body(c_ref, dm_ref, da_ref, gw_ref, gb_ref):
        cv = c_ref[...]
        ca = _bf(cv * _sigmoid(cv))
        gw_ref[0] = _dot_tn(ca, _bf(dm_ref[0]))
        for k in range(kinds):
            gb_ref[0, k] = _colsum(da_ref[0, k])

    return _pcall(body, name="ada_grad", grid=(depth,),
                  in_specs=[pl.BlockSpec((nb, d), lambda l: (0, 0)), pl.BlockSpec((1, nb, n), lambda l: (l, 0, 0)),
                            pl.BlockSpec((1, kinds, nb, n_all), lambda l: (l, 0, 0, 0))],
                  out_specs=[pl.BlockSpec((1, d, n), lambda l: (l, 0, 0)),
                             pl.BlockSpec((1, kinds, 1, n_all), lambda l: (l, 0, 0, 0))],
                  out_shape=[jax.ShapeDtypeStruct((depth, d, n), F32), jax.ShapeDtypeStruct((depth, kinds, 1, n_all), F32)],
                  compiler_params=_seq())(c_all, dmod_cols, rows_all)


def _adamw(items, ride=None):
    two_d = [tuple(t.reshape(w.size // w.shape[-1], w.shape[-1]) for t in (w, g, m, v)) for w, g, m, v in items]
    n = len(items)
    if n == 1:
        rows, cols = two_d[0][0].shape
        tr = _row_tile(rows, cap=max(8, BLOCK_ELEMS // cols), mult=8)
        blocks = [pl.BlockSpec((tr, cols), lambda i: (i, 0))]
        grid = (rows // tr,)
    else:
        blocks = [pl.BlockSpec(t[0].shape, lambda i: (0, 0)) for t in two_d]
        grid = (1,)

    def body(*refs):
        for k in range(n):
            w_ref, g_ref, m_ref, v_ref = refs[4 * k:4 * k + 4]
            d_ref, mo_ref, vo_ref = refs[4 * n + 3 * k:4 * n + 3 * k + 3]
            gv = g_ref[...]
            mn = ADAM_B1 * m_ref[...] + (1.0 - ADAM_B1) * gv
            vn = ADAM_B2 * v_ref[...] + (1.0 - ADAM_B2) * (gv * gv)
            m_hat = mn / (1.0 - ADAM_B1 ** ADAM_STEP)
            v_hat = vn / (1.0 - ADAM_B2 ** ADAM_STEP)
            d_ref[...] = -ADAM_LR * (m_hat / (jnp.sqrt(v_hat) + ADAM_EPS) + ADAM_WD * w_ref[...])
            mo_ref[...] = mn
            vo_ref[...] = vn

    outs, got = _pcall_ride(
        body, ride, name="adamw", grid=grid,
        in_specs=[b for b in blocks for _ in range(4)], out_specs=[b for b in blocks for _ in range(3)],
        out_shape=[jax.ShapeDtypeStruct(t[0].shape, F32) for t in two_d for _ in range(3)],
        compiler_params=_seq(), args=tuple(a for t in two_d for a in t))
    return [tuple(o.reshape(items[k][0].shape) for o in outs[3 * k:3 * k + 3]) for k in range(n)], got


WEIGHTS = ["norm_g", "w_ada", "b_ada", "w_in", "rg_conv_w", "rg_conv_b", "rg_w_a", "rg_b_a", "rg_w_x", "rg_b_x",
           "rg_lambda", "ml_conv_w", "ml_conv_b", "ml_w_q", "ml_w_k", "ml_w_v", "ml_w_if", "ml_b_if", "ml_norm_g",
           "w_out", "final_g"]
SMALL_SHARDED = {"rg_conv_w": 1, "ml_conv_w": 1, "ml_w_if": 0}
REPLICATED = ["rg_w_a", "rg_w_x", "rg_conv_b", "rg_b_a", "rg_b_x", "rg_lambda", "ml_conv_b", "ml_norm_g", "ml_b_if"]
LANES = 128


def _to_pieces(g, axis):
    shp = g.shape
    g = g.reshape(shp[:axis] + (4, 2, shp[axis] // 8) + shp[axis + 1:])
    g = jnp.moveaxis(g, (axis, axis + 1), (0, 1))
    return g.reshape(4, 2, -1)


def _from_pieces(p, shard_shape, axis):
    k = p.shape[0]
    rest = shard_shape[:axis] + (shard_shape[axis] // k,) + shard_shape[axis + 1:]
    t = jnp.moveaxis(p.reshape((k,) + rest), 0, axis)
    return t.reshape(shard_shape)


def _pad_rows(flat, mult):
    n = flat.shape[-1]
    pad = (-n) % mult
    if pad:
        flat = jnp.concatenate([flat, jnp.zeros(flat.shape[:-1] + (pad,), flat.dtype)], axis=-1)
    return flat


def kernel(x, c, norm_g, w_ada, b_ada, w_in, rg_conv_w, rg_conv_b, rg_w_a, rg_b_a, rg_w_x, rg_b_x, rg_lambda, ml_conv_w, ml_conv_b, ml_w_q, ml_w_k, ml_w_v, ml_w_if, ml_b_if, ml_norm_g, w_out, final_g, loss_target, m_norm_g, m_w_ada, m_b_ada, m_w_in, m_rg_conv_w, m_rg_conv_b, m_rg_w_a, m_rg_b_a, m_rg_w_x, m_rg_b_x, m_rg_lambda, m_ml_conv_w, m_ml_conv_b, m_ml_w_q, m_ml_w_k, m_ml_w_v, m_ml_w_if, m_ml_b_if, m_ml_norm_g, m_w_out, m_final_g, v_norm_g, v_w_ada, v_b_ada, v_w_in, v_rg_conv_w, v_rg_conv_b, v_rg_w_a, v_rg_b_a, v_rg_w_x, v_rg_b_x, v_rg_lambda, v_ml_conv_w, v_ml_conv_b, v_ml_w_q, v_ml_w_k, v_ml_w_v, v_ml_w_if, v_ml_b_if, v_ml_norm_g, v_w_out, v_final_g):
    given = dict(locals())
    ax, ay, ac = lax.axis_index("x"), lax.axis_index("y"), lax.axis_index("c")
    chip = 2 * ax + ay
    me = 2 * chip + ac
    depth, d = norm_g.shape
    n_ada = w_ada.shape[2]
    pick = lambda a, i, axis=0: lax.dynamic_index_in_dim(a, i, axis, keepdims=False)

    convs = jnp.stack([rg_conv_w, ml_conv_w])
    n_conv = 2 * depth * CONV_WIDTH // 4
    blk = jnp.concatenate([c, convs.reshape(n_conv, d), jnp.zeros((8 - 1 - n_conv, d), F32)], axis=0)
    w_in_first = lax.dynamic_slice_in_dim(w_in[0], ac * (d // 2), d // 2, 0).astype(BF16)
    g0, w_in_first = _all_gather8([blk, w_in_first], pltpu.HBM)
    g0 = g0.reshape(8, 8, d)
    c_all = g0[:, 0, :]
    conv_full = g0[0::2, 1:1 + n_conv].reshape(4, 2, depth, CONV_WIDTH, d // 4)
    conv_full = conv_full.transpose(1, 2, 3, 0, 4).reshape(2, depth, CONV_WIDTH, d)

    b_cols = lax.dynamic_slice_in_dim(b_ada, chip * n_ada, n_ada, axis=1)[:, None, :]
    mod_part = _ada_mod(c_all, w_ada, b_cols)
    g1 = _all_gather8([mod_part.transpose(1, 0, 2).reshape(8, depth * n_ada)], pltpu.VMEM)[0]
    g1 = g1.reshape(8, 8, depth, n_ada)[0::2]
    mod_me = pick(g1.transpose(1, 2, 0, 3).reshape(8, depth, 4 * n_ada), me)

    def half_of(w, axis):
        n = w.shape[axis] // 2
        return lax.dynamic_slice_in_dim(w, ac * n, n, axis).astype(BF16)

    n_sh = w_in.shape[2]
    heads, hd_cut, hd = ml_w_q.shape[1:]

    def blocks_of(l):
        wqkv = jnp.stack([ml_w_q[l], ml_w_k[l], ml_w_v[l]])
        return [half_of(w_in[l], 0), half_of(w_out[l], 0), half_of(wqkv, 2).reshape(-1, hd), half_of(ml_w_if[l], 0)]

    def layer_of(l, w4, rest):
        return dict(
            norm_g=norm_g[l][None], shift=mod_me[l, 0:d][None], scale=mod_me[l, d:2 * d][None],
            gate=mod_me[l, 2 * d:3 * d][None], w4=w4.reshape(4, d, n_sh),
            rg_conv_w=conv_full[0, l], rg_conv_b=rg_conv_b[l][None], rg_wa_b=_bf(rg_w_a[l]), rg_ba=rg_b_a[l][None],
            rg_wx_b=_bf(rg_w_x[l]), rg_bx=rg_b_x[l][None], rg_lam=rg_lambda[l][None],
            ml_conv_w=conv_full[1, l], ml_conv_b=ml_conv_b[l][None], b_if=ml_b_if[l][None], b_ift=ml_b_if[l][:, None],
            ml_g=ml_norm_g[l][None], **rest)

    def rest_of(gathered):
        w_out_b, wqkv_g, wif = gathered
        return dict(w_out_b=w_out_b, wqkv_b=_from_pieces(wqkv_g.reshape(8, -1), (3, heads, hd, hd), 2), wif_b=wif,
                    wift_b=wif.T)

    spread = lambda blocks: [Leg(b, "spread") for b in blocks]
    fill = lambda landed: [Leg(t, "sib_fill") for t in landed]
    flat = lambda filled: [t.reshape(-1, t.shape[-1]) for t in filled]
    first = blocks_of(0)
    n_rest = len(first) - 1
    p = layer_of(0, w_in_first, {})
    layers, saved = [], []
    xl = x[0]
    for l in range(depth):
        nxt = blocks_of(l + 1) if l + 1 < depth else []
        skip = n_rest if l == 0 else 0
        rides = dict(rg_fwd=lambda landed, nxt=nxt: spread(nxt[:1]))
        if l == 0:
            rides.update(ln_inproj=lambda landed: spread(first[1:]),
                         rg_fwd=lambda landed, nxt=nxt: fill(landed["ln_inproj"]) + spread(nxt[:1]),
                         late=lambda landed: rest_of(flat(landed["rg_fwd"][:n_rest])))
        if nxt:
            rides.update(mlstm_fwd=lambda landed, nxt=nxt: spread(nxt[1:]),
                         out_proj=lambda landed, skip=skip: fill(list(landed["rg_fwd"][skip:]) + list(landed["mlstm_fwd"])))
        xl, s, p, landed = _layer_fwd(xl, p, rides, None if nxt else (final_g[None], loss_target[0]))
        layers.append(p)
        saved.append(s)
        if nxt:
            arrived = flat(landed["out_proj"])
            p = layer_of(l + 1, arrived[0], rest_of(arrived[1:]))
    dx, g_final, loss = xl

    half = ac.reshape(1)
    ids = jnp.stack([chip, ac])
    r_out = w_out.shape[1] // 2

    def pair_in(g_w_in, got_in):
        return _pair_sum(
            half, g_w_in, pl.BlockSpec((None, d // 2, n_sh), lambda s, h: (0, h[0], s)),
            got_in, pl.BlockSpec((None, None, d // 2, n_sh), lambda s, h: (0, s, 0, 0)),
            jax.ShapeDtypeStruct((4, 1, d // 2, n_sh), BF16),
            pl.BlockSpec((None, None, d // 2, n_sh), lambda s, h: (s, 0, 0, 0)), (4,))

    def pair_out(g_out5, got_out):
        return _pair_sum(
            half, g_out5, pl.BlockSpec((None, None, None, r_out, d), lambda s, h: (0, s, h[0], 0, 0)),
            got_out, pl.BlockSpec((None, None, r_out, d), lambda s, h: (0, s, 0, 0)),
            jax.ShapeDtypeStruct((4, 1, r_out, d), BF16),
            pl.BlockSpec((None, None, r_out, d), lambda s, h: (s, 0, 0, 0)), (4,))

    def pair_slab(slab, got, dtype):
        rows = got.shape[0] // 4
        blk = pl.BlockSpec((rows, LANES), lambda s, h: (s, 0))
        return _pair_sum(half, slab, pl.BlockSpec((None, rows, LANES), lambda s, h: (h[0], s, 0)), got, blk,
                         jax.ShapeDtypeStruct((4 * rows, LANES), dtype), blk, (4,)).reshape(4, 1, rows, LANES)

    row_pad = lambda n: -(-n // (8 * LANES)) * (8 * LANES)

    def as_rows(t):
        if t.shape[-1] == LANES and t.size % (8 * LANES) == 0:
            return t.reshape(-1, LANES)
        return _pad_rows(t.reshape(-1), 8 * LANES).reshape(-1, LANES)

    chips = lambda arrs: [Leg(a, "chips") for a in arrs]
    out5 = lambda g: g["w_out"].reshape(1, 4, 2, r_out, d)
    r_q = hd // 8
    qkv5 = lambda g: g["ml_w_qkv"].reshape(3 * heads, 4, 2, r_q, hd)

    def pair_qkv(g5, got):
        return _pair_sum(
            half, g5, pl.BlockSpec((3 * heads, None, None, r_q, hd), lambda s, h: (0, s, h[0], 0, 0)),
            got, pl.BlockSpec((3 * heads, None, r_q, hd), lambda s, h: (0, s, 0, 0)),
            jax.ShapeDtypeStruct((4, 1, 3 * heads, r_q, hd), BF16),
            pl.BlockSpec((None, None, 3 * heads, r_q, hd), lambda s, h: (s, 0, 0, 0, 0)), (4,))

    grads, dmods, parts, mets = [None] * depth, [None] * depth, [None] * depth, [None] * depth
    small = {}

    def early_exchange(g, landed):
        every = [g] + grads[1:]
        sm = jnp.concatenate([_to_pieces(every[l][name], axis) for l in range(depth)
                              for name, axis in SMALL_SHARDED.items()], axis=-1)
        sm = _pad_rows(sm, 16 * LANES)
        sm = sm.transpose(1, 0, 2).reshape(2, -1, LANES)
        rep = [as_rows(every[l][name]) for l in range(depth) for name in REPLICATED]
        rep = jnp.concatenate(rep + [as_rows(g_final), as_rows(loss)], axis=0)
        rep = jnp.concatenate([rep, jnp.zeros(((-rep.shape[0]) % 64, LANES), F32)], axis=0)
        rep = rep.reshape(4, 2, -1, LANES).transpose(1, 0, 2, 3).reshape(2, -1, LANES)
        got_sm, got_rep, got_q = _exchange([Leg(sm, "sib_slab"), Leg(rep, "sib_slab"), Leg(qkv5(g), "sib_w_out")])
        small["parts"] = [pair_out(out5(g), landed["mlstm_bwd"][0]), pair_slab(sm, got_sm, BF16),
                          pair_slab(rep, got_rep, F32), pair_qkv(qkv5(g), got_q)]
        return chips(small["parts"])

    def last_exchange(g, landed):
        (got_in,) = _exchange([Leg(g["w_in"], "sib_w_in")])
        small["part_in"] = pair_in(g["w_in"], got_in)
        return chips([small["part_in"]])

    for l in reversed(range(depth)):
        above = parts[l + 1] if l + 1 < depth else []
        rides = dict(mlstm_bwd=lambda g, landed, above=above: [Leg(out5(g), "sib_w_out")] + chips(above),
                     in_bwd=lambda g, landed: [Leg(g["w_in"], "sib_w_in"), Leg(qkv5(g), "sib_w_out")])
        if l == 0:
            rides.update(grad_w_in=early_exchange, in_bwd=last_exchange)
        dx, grads[l], dmods[l], got = _layer_bwd(dx, layers[l], saved[l], rides)
        if above:
            mets[l + 1] = got["mlstm_bwd"][1:]
        if l > 0:
            parts[l] = [pair_in(grads[l]["w_in"], got["in_bwd"][0]), pair_out(out5(grads[l]), got["mlstm_bwd"][0]),
                        pair_qkv(qkv5(grads[l]), got["in_bwd"][1])]
    part_out, part_sm, part_rep, part_q = small["parts"]
    met_out, met_sm, met_rep, met_q = got["grad_w_in"]
    parts[0], mets[0] = [small["part_in"], part_out, part_q], [got["in_bwd"][0], met_out, met_q]
    n_rep = part_rep.shape[2]

    pad = lambda t: jnp.concatenate([t, jnp.zeros((1, 2 * d), F32)], axis=1)
    rows = [r for l in range(depth) for r in (dmods[l], pad(grads[l]["norm_g"]))]
    blk = jnp.concatenate(rows + [jnp.zeros((8 - 2 * depth, 3 * d), F32)], axis=0)
    rows_all = _all_gather8([blk], pltpu.VMEM)[0].reshape(8, 8, 3 * d)[:, :2 * depth]
    rows_all = rows_all.transpose(1, 0, 2).reshape(depth, 2, 8, 3 * d)
    dm_cols = lax.dynamic_slice_in_dim(rows_all[:, 0], chip * n_ada, n_ada, axis=2)
    g_w_ada, summed = _ada_grad(c_all, dm_cols, rows_all)

    g = dict(w_ada=g_w_ada, b_ada=summed[:, 0, 0], norm_g=summed[:, 1, 0, :d])
    item = lambda name: (given[name], g[name], given["m_" + name], given["v_" + name])
    both_in, both_out, both_q = depth, depth, depth
    flat_q = lambda t: t.reshape(4, 1, 3 * heads * r_q, hd)
    for l in range(depth):
        both_in = _chip_sum(ids, parts[l][0], mets[l][0], True, l, both_in)
        both_out = _chip_sum(ids, parts[l][1], mets[l][1], True, l, both_out)
        both_q = _chip_sum(ids, flat_q(parts[l][2]), flat_q(mets[l][2]), True, l, both_q)
    both_in, both_out, both_q, both_sm = _exchange(fill([both_in, both_out, both_q,
                                                         _chip_sum(ids, part_sm, met_sm, True)]))
    red_rep = _chip_sum(ids, part_rep, met_rep, False).reshape(n_rep, LANES)
    rep_all = _all_gather8([red_rep], pltpu.VMEM)[0].reshape(-1)

    g.update(w_in=both_in.reshape(w_in.shape), w_out=both_out.reshape(w_out.shape))
    g_qkv = both_q.reshape(depth, 2, 3, heads, r_q, hd).transpose(0, 2, 3, 1, 4, 5)
    g_qkv = g_qkv.reshape(depth, 3, heads, 2 * r_q, hd)
    for i, name in enumerate(["ml_w_q", "ml_w_k", "ml_w_v"]):
        g[name] = g_qkv[:, i]
    shard = both_sm.reshape(2, -1)
    off = 0
    per_layer = {name: [] for name in SMALL_SHARDED}
    for l in range(depth):
        for name, axis in SMALL_SHARDED.items():
            n = grads[l][name].size // 8
            per_layer[name].append(_from_pieces(shard[:, off:off + n], given[name].shape[1:], axis))
            off += n
    for name in SMALL_SHARDED:
        g[name] = jnp.stack(per_layer[name])
    off = 0
    per_layer = {name: [] for name in REPLICATED}
    for l in range(depth):
        for name in REPLICATED:
            n = given[name][l].size
            per_layer[name].append(rep_all[off:off + n].reshape(given[name].shape[1:]))
            off += row_pad(n)
    for name in REPLICATED:
        g[name] = jnp.stack(per_layer[name])
    g["final_g"] = rep_all[off:off + d]
    loss_all = rep_all[off + row_pad(d)]

    stepped = {}
    rg_mats, ml_mats = ["rg_w_a", "rg_w_x"], ["ml_w_q", "ml_w_k", "ml_w_v"]
    vectors = [n for n in WEIGHTS if n not in ["w_ada", "w_in", "w_out"] + rg_mats + ml_mats]
    for names in (["w_ada"], ["w_in"], ["w_out"], rg_mats, ml_mats, vectors):
        stepped.update(zip(names, _adamw([item(name) for name in names])[0]))
    deltas, new_m, new_v = zip(*[stepped[name] for name in WEIGHTS])
    return (loss_all, dx[None], *[g[name] for name in WEIGHTS], *deltas, *new_m, *new_v)
```

```python
import functools
from typing import NamedTuple

import jax
import jax.numpy as jnp
from jax import lax
from jax.experimental import pallas as pl
from jax.experimental.pallas import tpu as pltpu

F32 = jnp.float32
BF16 = jnp.bfloat16

EPS = 1e-6
RG_C = 8.0
CONV_WIDTH = 4
ML_CHUNK = 512
HALO = 8
ROWS_VECTOR = 512
ROWS_MATMUL = 1024
ROWS_IN_BWD = 512
ROWS_GRAD_MATMUL = 2048
BLOCK_ELEMS = 1 << 18
RELAY_BYTES = 1 << 18
ADAM_LR = 0.001
ADAM_B1 = 0.9
ADAM_B2 = 0.999
ADAM_EPS = 1e-08
ADAM_WD = 0.01
ADAM_STEP = 10
MESH = pl.DeviceIdType.MESH


def _pcall(body, **kw):
    return pl.pallas_call(body, **kw)


class Leg(NamedTuple):
    src: jax.Array
    kind: str

    def landing(self):
        a = self.src
        shape = {"chips": lambda: a.shape, "spread": lambda: (4, 2) + a.shape, "sib_fill": lambda: a.shape,
                 "sib_w_in": lambda: (a.shape[0], 4, a.shape[1] // 2, a.shape[2] // 4),
                 "sib_w_out": lambda: a.shape[:2] + a.shape[3:], "sib_slab": lambda: a.shape[1:]}[self.kind]()
        return jax.ShapeDtypeStruct(shape, a.dtype)

    def relayed(self):
        a = self.src
        return self.kind == "spread" and a.size * a.dtype.itemsize >= RELAY_BYTES and a.shape[0] % 32 == 0

    def copies(self, src, dst, x, y, c):
        a, me_s, o = self.src, 2 * x + y, 1 - c
        chips = [(1 - x, y), (x, 1 - y), (1 - x, 1 - y)]
        if self.kind == "chips":
            return [(src.at[2 * px + py], dst.at[me_s], (px, py, c)) for px, py in chips], [], []
        if self.kind == "spread":
            own = dst.at[me_s, c]
            if not self.relayed():
                return [(src, own, (px, py, c)) for px, py in chips], [(src, own)], []
            by_x, by_y, half = chips[0], chips[1], a.shape[0] // 2
            part = lambda chip, k: dst.at[2 * chip[0] + chip[1], c, pl.ds(k * half, half)]
            return ([(src, own, (*by_x, c)), (src, own, (*by_y, c))], [(src, own)],
                    [(part(by_x, 0), part(by_x, 0), (*by_y, c), 0), (part(by_y, 1), part(by_y, 1), (*by_x, c), 1)])
        depth = pl.ds(0, a.shape[0])
        if self.kind == "sib_fill":
            return [(dst.at[depth, c], dst.at[depth, c], (x, y, o))], [], []
        if self.kind == "sib_w_in":
            half, n = a.shape[1] // 2, a.shape[2] // 4
            return [(src.at[depth, pl.ds(o * half, half), pl.ds(s * n, n)], dst.at[depth, s], (x, y, o))
                    for s in range(4)], [], []
        if self.kind == "sib_w_out":
            return [(src.at[depth, pl.ds(0, 4), o], dst, (x, y, o))], [], []
        return [(src.at[o], dst, (x, y, o))], [], []

    def n_copies(self):
        return 4 if self.relayed() else {"chips": 3, "spread": 3, "sib_w_in": 4}.get(self.kind, 1)


def _exchange_body(legs, srcs, dsts, send_sems, recv_sems, local_sems):
    x, y, c = _me()
    remote, local, relays, k = [], [], [], 0
    for i, leg in enumerate(legs):
        far, near, handed = leg.copies(srcs[i], dsts[i], x, y, c)
        at = len(remote)
        for src, dst, to in far:
            remote.append(_remote(src, dst, send_sems.at[k], recv_sems.at[k], to))
            k += 1
        for src, dst, to, after in handed:
            relays.append((_remote(src, dst, send_sems.at[k], recv_sems.at[k], to), remote[at + after]))
            k += 1
        local += [pltpu.make_async_copy(src, dst, local_sems.at[i]) for src, dst in near]
    return remote, local, relays


def _hand_on(relays):
    for cp, after in relays:
        after.wait_recv()
        cp.start()


def _wait_all(copies, local, relays):
    arrived, handed = [after for _, after in relays], [cp for cp, _ in relays]
    for cp in [cp for cp in copies if not any(cp is a for a in arrived)] + handed:
        cp.wait_recv()
    for cp in copies + handed:
        cp.wait_send()
    for cp in local:
        cp.wait()


def _exchange_sems(legs):
    n = sum(leg.n_copies() for leg in legs)
    return [pltpu.SemaphoreType.DMA((n,)), pltpu.SemaphoreType.DMA((n,)), pltpu.SemaphoreType.DMA((len(legs),))]


def _exchange_aliases(legs, n_in, n_out):
    return {n_in + i: n_out + i for i, leg in enumerate(legs) if leg.kind == "sib_fill"}


def _pcall_ride(body, ride, *, grid, in_specs, out_specs, out_shape, args, scratch_shapes=(), **kw):
    n_in, n_out, n_scr = len(in_specs), len(out_specs), len(scratch_shapes)
    if not ride:
        res = _pcall(body, grid=grid, in_specs=in_specs, out_specs=out_specs, out_shape=out_shape,
                     scratch_shapes=list(scratch_shapes), **kw)(*args)
        return res, []
    nr = len(ride)

    def riding(*refs):
        ins, rsrc = refs[:n_in], refs[n_in:n_in + nr]
        outs, rdst = refs[n_in + nr:n_in + nr + n_out], refs[n_in + nr + n_out:n_in + 2 * nr + n_out]
        scr = refs[n_in + 2 * nr + n_out:n_in + 2 * nr + n_out + n_scr]
        copies, local, relays = _exchange_body(ride, rsrc, rdst, *refs[n_in + 2 * nr + n_out + n_scr:])
        at_step = lambda steps: functools.reduce(jnp.logical_and, [pl.program_id(a) == s for a, s in enumerate(steps)])

        @pl.when(at_step([0] * len(grid)))
        def _():
            for cp in copies + local:
                cp.start()

        body(*ins, *outs, *scr)

        if relays:
            @pl.when(at_step([grid[0] // 2] + [0] * (len(grid) - 1)))
            def _():
                _hand_on(relays)

        @pl.when(at_step([g - 1 for g in grid]))
        def _():
            _wait_all(copies, local, relays)

    hbm = pl.BlockSpec(memory_space=pltpu.HBM)
    aliases = {**kw.pop("input_output_aliases", {}), **_exchange_aliases(ride, n_in, n_out)}
    res = _pcall(
        riding, grid=grid, in_specs=list(in_specs) + [hbm] * nr, out_specs=list(out_specs) + [hbm] * nr,
        out_shape=list(out_shape) + [leg.landing() for leg in ride], input_output_aliases=aliases,
        scratch_shapes=list(scratch_shapes) + _exchange_sems(ride), **kw)(*args, *[leg.src for leg in ride])
    return res[:n_out], res[n_out:]


def _seq(n=1):
    return pltpu.CompilerParams(dimension_semantics=("arbitrary",) * n)


def _dot(a, b):
    return jnp.dot(a, b, preferred_element_type=F32)


def _dot_nt(a, b):
    return lax.dot_general(a, b, (((1,), (1,)), ((), ())), preferred_element_type=F32)


def _dot_tn(a, b):
    return lax.dot_general(a, b, (((0,), (0,)), ((), ())), preferred_element_type=F32)


def _bf(x):
    return x.astype(BF16)


def _sigmoid(x):
    return 0.5 * jnp.tanh(0.5 * x) + 0.5


def _log1p(z):
    u = 1.0 + z
    return jnp.where(u == 1.0, z, jnp.log(u) * (z / jnp.where(u == 1.0, 1.0, u - 1.0)))


def _softplus(x):
    return jnp.maximum(x, 0.0) + _log1p(jnp.exp(-jnp.abs(x)))


def _log_sigmoid(x):
    return -_softplus(-x)


def _one_minus_sq(a, log_a):
    x = 2.0 * log_a
    small = -x * (1.0 + x * (0.5 + x * (1.0 / 6.0)))
    return jnp.where(x > -0.004, small, 1.0 - a * a)


def _dsilu(x, s):
    return s * (1.0 + x * (1.0 - s))


def _rowsum(x):
    return jnp.sum(x, axis=1, keepdims=True)


def _colsum(x):
    return jnp.sum(x, axis=0, keepdims=True)


def _shift_down(win, s):
    return win if s == 0 else pltpu.roll(win, s, 0)


def _shift_up(win, s):
    return win if s == 0 else pltpu.roll(win, win.shape[0] - s, 0)


def _conv_taps(win):
    return [_shift_down(win, CONV_WIDTH - 1 - k)[HALO:] for k in range(CONV_WIDTH)]


def _conv_fwd(taps, w_ref, b_ref):
    acc = b_ref[...] + w_ref[CONV_WIDTH - 1:CONV_WIDTH, :] * taps[CONV_WIDTH - 1]
    for k in range(CONV_WIDTH - 1):
        acc = acc + w_ref[k:k + 1, :] * taps[k]
    return acc


def _split3(x):
    hi = _bf(x)
    r1 = x - hi.astype(F32)
    mid = _bf(r1)
    lo = _bf(r1 - mid.astype(F32))
    return hi, mid, lo


def _tri_dot_left(tri, x):
    hi, mid, lo = _split3(x)
    return _dot(tri, hi) + _dot(tri, mid) + _dot(tri, lo)


def _tri_dot_right(x, tri):
    hi, mid, lo = _split3(x)
    return _dot(hi, tri) + _dot(mid, tri) + _dot(lo, tri)


def _tile(n, want):
    t = min(n, want)
    assert n % t == 0
    return t


def _ln_inproj(x, g, scale, shift, w4, ride=None):
    s_len, d = x.shape
    nj, _, nsh = w4.shape
    tm = _tile(s_len, ROWS_MATMUL)

    def body(x_ref, g_ref, sc_ref, sh_ref, w_ref, h_ref, u_ref, hs):
        @pl.when(pl.program_id(1) == 0)
        def _():
            xv = x_ref[...]
            r = lax.rsqrt(jnp.mean(xv * xv, axis=-1, keepdims=True) + EPS)
            hv = (xv * r * g_ref[...]) * (1.0 + sc_ref[...]) + sh_ref[...]
            hs[...] = _bf(hv)
            h_ref[...] = hs[...]

        u_ref[...] = _dot(hs[...], w_ref[0])

    vec = pl.BlockSpec((1, d), lambda i, j: (0, 0))
    return _pcall_ride(
        body, ride, name="ln_inproj", grid=(s_len // tm, nj),
        in_specs=[pl.BlockSpec((tm, d), lambda i, j: (i, 0)), vec, vec, vec,
                  pl.BlockSpec((1, d, nsh), lambda i, j: (j, 0, 0))],
        out_specs=[pl.BlockSpec((tm, d), lambda i, j: (i, 0)), pl.BlockSpec((tm, nsh), lambda i, j: (i, j))],
        out_shape=[jax.ShapeDtypeStruct((s_len, d), BF16), jax.ShapeDtypeStruct((s_len, nj * nsh), F32)],
        scratch_shapes=[pltpu.VMEM((tm, d), BF16)],
        compiler_params=_seq(2),
        args=(x, g, scale, shift, w4))


def _rg_gates(xc, wa_ref, ba_ref, wx_ref, bx_ref, lam_ref):
    heads, hd, _ = wa_ref.shape
    xb = _bf(xc)
    ga = jnp.concatenate([_dot(xb[:, h * hd:(h + 1) * hd], wa_ref[h]) for h in range(heads)], axis=1) + ba_ref[...]
    gx = jnp.concatenate([_dot(xb[:, h * hd:(h + 1) * hd], wx_ref[h]) for h in range(heads)], axis=1) + bx_ref[...]
    r = _sigmoid(ga)
    ig = _sigmoid(gx)
    sp = _softplus(-lam_ref[...])
    log_a = (-RG_C) * r * sp
    a = jnp.exp(log_a)
    mult = jnp.sqrt(_one_minus_sq(a, log_a))
    return r, ig, sp, log_a, a, mult


def _scan_groups(a, u, reverse):
    n, c = a.shape
    a = a.reshape(n // 8, 8, c)
    u = u.reshape(n // 8, 8, c)
    row = lax.broadcasted_iota(jnp.int32, a.shape, 1)
    for k in (1, 2, 4):
        sft = 8 - k if reverse else k
        a_sh, u_sh = pltpu.roll(a, sft, 1), pltpu.roll(u, sft, 1)
        ok = row < 8 - k if reverse else row >= k
        u = jnp.where(ok, a * u_sh + u, u)
        a = jnp.where(ok, a * a_sh, a)
    return a.reshape(n, c), u.reshape(n, c)


def _rg_fwd(u, conv_w, conv_b, wa_b, ba, wx_b, bx, lam, ride=None):
    s_len = u.shape[0]
    d = conv_w.shape[1]
    tm = _tile(s_len, ROWS_VECTOR)
    per = tm // HALO

    def body(x_ref, xp_ref, z_ref, cw_ref, cb_ref, wa_ref, ba_ref, wx_ref, bx_ref, lam_ref,
             hh_ref, y_ref, carry):
        i = pl.program_id(0)

        @pl.when(i == 0)
        def _():
            carry[...] = jnp.zeros_like(carry)

        prev = jnp.where(i == 0, 0.0, xp_ref[...])
        xc = _conv_fwd(_conv_taps(jnp.concatenate([prev, x_ref[...]], axis=0)), cw_ref, cb_ref)
        _, ig, _, _, a, mult = _rg_gates(xc, wa_ref, ba_ref, wx_ref, bx_ref, lam_ref)
        ca, cu = _scan_groups(a, mult * (ig * xc), reverse=False)
        c = carry[0:1, :]
        for j in range(tm // 8):
            blk = ca[j * 8:(j + 1) * 8] * c + cu[j * 8:(j + 1) * 8]
            hh_ref[j * 8:(j + 1) * 8, :] = blk
            c = blk[7:8]
        carry[0:1, :] = c
        z = z_ref[...]
        y_ref[0] = _bf(hh_ref[...] * (z * _sigmoid(z)))

    vec = pl.BlockSpec((1, d), lambda i: (0, 0))
    whole3 = lambda a: pl.BlockSpec(a.shape, lambda i: (0, 0, 0))
    return _pcall_ride(
        body, ride, name="rg_fwd", grid=(s_len // tm,),
        in_specs=[pl.BlockSpec((tm, d), lambda i: (i, 0)),
                  pl.BlockSpec((HALO, d), lambda i: (jnp.maximum(i * per - 1, 0), 0)),
                  pl.BlockSpec((tm, d), lambda i: (i, 1)),
                  pl.BlockSpec((CONV_WIDTH, d), lambda i: (0, 0)), vec,
                  whole3(wa_b), vec, whole3(wx_b), vec, vec],
        out_specs=[pl.BlockSpec((tm, d), lambda i: (i, 0)), pl.BlockSpec((1, tm, d), lambda i: (0, i, 0))],
        out_shape=[jax.ShapeDtypeStruct((s_len, d), F32), jax.ShapeDtypeStruct((2, s_len, d), BF16)],
        scratch_shapes=[pltpu.VMEM((8, d), F32)],
        compiler_params=_seq(),
        args=(u, u, u, conv_w, conv_b, wa_b, ba, wx_b, bx, lam))


def _ml_pre(u, conv_w, conv_b, wqkv_b, wif_b, wift_b, b_if, b_ift):
    s_len = u.shape[0]
    d = conv_w.shape[1]
    _, heads, hd, _ = wqkv_b.shape
    ng = 2 * heads
    tm = _tile(s_len, max(ROWS_VECTOR, ML_CHUNK))
    per = tm // HALO

    def body(x_ref, xp_ref, cw_ref, cb_ref, w_ref, wif_ref, wift_ref, bif_ref, bift_ref,
             qkv_ref, gt_ref, gtt_ref, bc_ref, bct_ref):
        i = pl.program_id(0)
        prev = jnp.where(i == 0, 0.0, xp_ref[...])
        xm = x_ref[...]
        pre = _conv_fwd(_conv_taps(jnp.concatenate([prev, xm], axis=0)), cw_ref, cb_ref)
        xcb = _bf(pre * _sigmoid(pre))
        xmb = _bf(xm)
        for h in range(heads):
            hs = slice(h * hd, (h + 1) * hd)
            qkv_ref[0, :, hs] = _bf(_dot(xcb[:, hs], w_ref[0, h]))
            qkv_ref[1, :, hs] = _bf(_dot(xcb[:, hs], w_ref[1, h]))
            qkv_ref[2, :, hs] = _bf(_dot(xmb[:, hs], w_ref[2, h]))
        qb, kb, vb = qkv_ref[0], qkv_ref[1], qkv_ref[2]
        gt = (_dot(qb, wif_ref[0:d, :]) + _dot(kb, wif_ref[d:2 * d, :]) + _dot(vb, wif_ref[2 * d:3 * d, :])
              + bif_ref[...])
        gtt = (_dot_nt(wift_ref[:, 0:d], qb) + _dot_nt(wift_ref[:, d:2 * d], kb)
               + _dot_nt(wift_ref[:, 2 * d:3 * d], vb) + bift_ref[...])
        gt_ref[...] = gt
        gtt_ref[...] = gtt
        r = lax.broadcasted_iota(jnp.int32, (tm, tm), 0)
        c = lax.broadcasted_iota(jnp.int32, (tm, tm), 1)
        same = (r // ML_CHUNK) == (c // ML_CHUNK)
        bc_ref[...] = _tri_dot_left(((r >= c) & same).astype(BF16), _log_sigmoid(gt))
        bct_ref[...] = _tri_dot_right(_log_sigmoid(gtt), ((r <= c) & same).astype(BF16))

    vec = pl.BlockSpec((1, d), lambda i: (0, 0))
    whole2 = lambda a: pl.BlockSpec(a.shape, lambda i: (0, 0))
    col = pl.BlockSpec((tm, ng), lambda i: (i, 0))
    row = pl.BlockSpec((ng, tm), lambda i: (0, i))
    return _pcall(
        body, name="ml_pre", grid=(s_len // tm,),
        in_specs=[pl.BlockSpec((tm, d), lambda i: (i, 2)),
                  pl.BlockSpec((HALO, d), lambda i: (jnp.maximum(i * per - 1, 0), 2)),
                  pl.BlockSpec((CONV_WIDTH, d), lambda i: (0, 0)), vec,
                  pl.BlockSpec(wqkv_b.shape, lambda i: (0, 0, 0, 0)), whole2(wif_b), whole2(wift_b), whole2(b_if),
                  whole2(b_ift)],
        out_specs=[pl.BlockSpec((3, tm, d), lambda i: (0, i, 0)), col, row, col, row],
        out_shape=[jax.ShapeDtypeStruct((3, s_len, d), BF16), jax.ShapeDtypeStruct((s_len, ng), F32),
                   jax.ShapeDtypeStruct((ng, s_len), F32), jax.ShapeDtypeStruct((s_len, ng), F32),
                   jax.ShapeDtypeStruct((ng, s_len), F32)],
        compiler_params=_seq(),
    )(u, u, conv_w, conv_b, wqkv_b, wif_b, wift_b, b_if, b_ift)


def _chunk_gates(gt, gtt, bc, bct, h, heads):
    li_c = gt[:, h:h + 1]
    li_r = gtt[h:h + 1, :]
    gf_c = gt[:, heads + h:heads + h + 1]
    b_c = bc[:, heads + h:heads + h + 1]
    b_r = bct[heads + h:heads + h + 1, :]
    return li_c, li_r, gf_c, b_c, b_r


def _chunk_weights(li_c, li_r, b_c, b_r, m_prev, causal):
    lc = b_c.shape[0]
    b_last = b_c[lc - 1:lc, :]
    dmat = jnp.where(causal, b_c - b_r + li_r, -jnp.inf)
    m_inter = b_c + m_prev
    m_t = jnp.maximum(m_inter, jnp.max(dmat, axis=1, keepdims=True))
    w_intra = jnp.exp(dmat - m_t)
    w_inter = jnp.exp(m_inter - m_t)
    g_c = b_last - b_c + li_c
    m_new = jnp.maximum(b_last + m_prev, jnp.max(g_c, axis=0, keepdims=True))
    w_state = jnp.exp(g_c - m_new)
    decay = jnp.exp(b_last + m_prev - m_new)
    return m_t, w_intra, w_inter, m_new, w_state, decay


def _tri_masks(lc):
    r = lax.broadcasted_iota(jnp.int32, (lc, lc), 0)
    c = lax.broadcasted_iota(jnp.int32, (lc, lc), 1)
    causal = r >= c
    return causal, causal.astype(BF16), (r <= c).astype(BF16)


def _mlstm_fwd(qkv, gates, u, ml_g, ycat, ride=None):
    _, s_len, d = qkv.shape
    ng = gates[0].shape[1]
    heads = ng // 2
    hd = d // heads
    lc = ML_CHUNK
    nc = s_len // lc
    kscale = hd ** -0.5

    def body(qkv_ref, gt_ref, gtt_ref, bc_ref, bct_ref, o_ref, z_ref, g_ref, _, cell_ref, y_ref, cst_ref, nst_ref,
             mst_ref, cs, ns, ms):
        @pl.when(pl.program_id(0) == 0)
        def _():
            cs[...] = jnp.zeros_like(cs)
            ns[...] = jnp.zeros_like(ns)
            ms[...] = jnp.zeros_like(ms)

        causal = _tri_masks(lc)[0]
        gtv, gttv, bcv, bctv = gt_ref[...], gtt_ref[...], bc_ref[...], bct_ref[...]
        old = [(cs[h], ns[h], ms[h]) for h in range(heads)]
        new, cells, ys = [], [], []
        for h in range(heads):
            hs = slice(h * hd, (h + 1) * hd)
            li_c, li_r, _, b_c, b_r = _chunk_gates(gtv, gttv, bcv, bctv, h, heads)
            c_old, n_old, m_old = old[h]
            m_prev = m_old[:, 0:1]
            m_t, w_intra, w_inter, m_new, w_state, decay = _chunk_weights(li_c, li_r, b_c, b_r, m_prev, causal)
            qb = qkv_ref[0, :, hs]
            ks = qkv_ref[1, :, hs].astype(F32) * kscale
            kb = _bf(ks)
            vb = qkv_ref[2, :, hs]
            s = _dot_nt(qb, kb) * w_intra
            num = _dot(_bf(s), vb) + w_inter * _dot(qb, _bf(c_old))
            den = _rowsum(s) + w_inter * _rowsum(qb.astype(F32) * n_old)
            cell = num / jnp.maximum(jnp.abs(den), jnp.exp(-m_t))
            kw = ks * w_state
            new.append((decay * c_old + _dot_tn(_bf(kw), vb), decay * n_old + _colsum(kw),
                        jnp.broadcast_to(m_new, m_old.shape)))
            cells.append(cell)
            hm = _sigmoid(o_ref[:, hs]) * cell
            hn = hm * lax.rsqrt(jnp.mean(hm * hm, axis=-1, keepdims=True) + EPS)
            z = z_ref[:, hs]
            ys.append(_bf((hn * g_ref[:, hs]) * (z * _sigmoid(z))))
        for h in range(heads):
            cst_ref[0, h] = _bf(old[h][0])
            nst_ref[0, h] = old[h][1]
            mst_ref[0, h] = old[h][2]
            cs[h], ns[h], ms[h] = new[h]
        cell_ref[...] = jnp.concatenate(cells, axis=1)
        y_ref[0] = jnp.concatenate(ys, axis=1)

    row = pl.BlockSpec((lc, d), lambda c: (c, 0))
    gcol = pl.BlockSpec((lc, ng), lambda c: (c, 0))
    grow = pl.BlockSpec((ng, lc), lambda c: (0, c))
    return _pcall_ride(
        body, ride, name="mlstm_fwd", grid=(nc,),
        in_specs=[pl.BlockSpec((3, lc, d), lambda c: (0, c, 0)), gcol, grow, gcol, grow,
                  pl.BlockSpec((lc, d), lambda c: (c, 3)), pl.BlockSpec((lc, d), lambda c: (c, 4)),
                  pl.BlockSpec((1, d), lambda c: (0, 0)), pl.BlockSpec(memory_space=pl.ANY)],
        out_specs=[row, pl.BlockSpec((1, lc, d), lambda c: (1, c, 0)),
                   pl.BlockSpec((1, heads, hd, hd), lambda c: (c, 0, 0, 0)),
                   pl.BlockSpec((1, heads, 1, hd), lambda c: (c, 0, 0, 0)),
                   pl.BlockSpec((1, heads, 1, 128), lambda c: (c, 0, 0, 0))],
        out_shape=[jax.ShapeDtypeStruct((s_len, d), F32), jax.ShapeDtypeStruct(ycat.shape, BF16),
                   jax.ShapeDtypeStruct((nc, heads, hd, hd), BF16),
                   jax.ShapeDtypeStruct((nc, heads, 1, hd), F32),
                   jax.ShapeDtypeStruct((nc, heads, 1, 128), F32)],
        scratch_shapes=[pltpu.VMEM((heads, hd, hd), F32), pltpu.VMEM((heads, 1, hd), F32),
                        pltpu.VMEM((heads, 1, 128), F32)],
        input_output_aliases={8: 1},
        compiler_params=_seq(),
        args=(qkv, *gates, u, u, ml_g, ycat))


def _out_proj(ycat, w_out_b, x, gate, ride=None):
    s_len, d = x.shape
    tm = _tile(s_len, ROWS_MATMUL)

    def body(a_ref, w_ref, x_ref, g_ref, y_ref, xn_ref):
        y = _dot(a_ref[0], w_ref[0:d, :]) + _dot(a_ref[1], w_ref[d:2 * d, :])
        y_ref[...] = y
        xn_ref[...] = x_ref[...] + g_ref[...] * y

    row = pl.BlockSpec((tm, d), lambda i: (i, 0))
    return _pcall_ride(
        body, ride, name="out_proj", grid=(s_len // tm,),
        in_specs=[pl.BlockSpec((2, tm, d), lambda i: (0, i, 0)), pl.BlockSpec((2 * d, d), lambda i: (0, 0)), row,
                  pl.BlockSpec((1, d), lambda i: (0, 0))],
        out_specs=[row, row],
        out_shape=[jax.ShapeDtypeStruct((s_len, d), F32)] * 2,
        compiler_params=_seq(),
        args=(ycat, w_out_b, x, gate))


def _out_proj_loss(ycat, w_out_b, x, gate, g, target):
    s_len, d = x.shape
    tm = _tile(s_len, ROWS_IN_BWD)

    def body(a_ref, w_ref, x_ref, gate_ref, g_ref, t_ref, y_ref, dx_ref, dg_ref, loss_ref):
        @pl.when(pl.program_id(0) == 0)
        def _():
            dg_ref[...] = jnp.zeros_like(dg_ref)
            loss_ref[...] = jnp.zeros_like(loss_ref)

        y = _dot(a_ref[0], w_ref[0:d, :]) + _dot(a_ref[1], w_ref[d:2 * d, :])
        y_ref[...] = y
        xv = x_ref[...] + gate_ref[...] * y
        r = lax.rsqrt(jnp.mean(xv * xv, axis=-1, keepdims=True) + EPS)
        xn = xv * r
        err = xn * g_ref[...] - t_ref[...]
        loss_ref[...] += 0.5 * jnp.sum(jnp.mean(err * err, axis=-1, keepdims=True))
        dout = err * (1.0 / d)
        dg_ref[...] += _colsum(dout * xn)
        dxn = dout * g_ref[...]
        dx_ref[...] = r * (dxn - xn * jnp.mean(dxn * xn, axis=-1, keepdims=True))

    row = pl.BlockSpec((tm, d), lambda i: (i, 0))
    vec = pl.BlockSpec((1, d), lambda i: (0, 0))
    return _pcall(
        body, name="out_proj_loss", grid=(s_len // tm,),
        in_specs=[pl.BlockSpec((2, tm, d), lambda i: (0, i, 0)), pl.BlockSpec((2 * d, d), lambda i: (0, 0)), row, vec,
                  vec, row],
        out_specs=[row, row, vec, pl.BlockSpec((1, 128), lambda i: (0, 0))],
        out_shape=[jax.ShapeDtypeStruct((s_len, d), F32), jax.ShapeDtypeStruct((s_len, d), F32),
                   jax.ShapeDtypeStruct((1, d), F32), jax.ShapeDtypeStruct((1, 128), F32)],
        compiler_params=_seq(),
    )(ycat, w_out_b, x, gate, g, target)


def _out_bwd(dxn, y, gate, w_out_b):
    s_len, d = dxn.shape
    tm = _tile(s_len, ROWS_MATMUL)

    def body(dx_ref, y_ref, g_ref, w_ref, dg_ref, dy_ref, dc_ref):
        @pl.when(pl.program_id(0) == 0)
        def _():
            dg_ref[...] = jnp.zeros_like(dg_ref)

        dx = dx_ref[...]
        dg_ref[...] += _colsum(dx * y_ref[...])
        dy = _bf(g_ref[...] * dx)
        dy_ref[...] = dy
        dc_ref[0] = _dot_nt(dy, w_ref[0:d, :])
        dc_ref[1] = _dot_nt(dy, w_ref[d:2 * d, :])

    row = pl.BlockSpec((tm, d), lambda i: (i, 0))
    vec = pl.BlockSpec((1, d), lambda i: (0, 0))
    return _pcall(
        body, name="out_bwd", grid=(s_len // tm,),
        in_specs=[row, row, vec, pl.BlockSpec((2 * d, d), lambda i: (0, 0))],
        out_specs=[vec, row, pl.BlockSpec((2, tm, d), lambda i: (0, i, 0))],
        out_shape=[jax.ShapeDtypeStruct((1, d), F32), jax.ShapeDtypeStruct((s_len, d), BF16),
                   jax.ShapeDtypeStruct((2, s_len, d), F32)],
        compiler_params=_seq(),
    )(dxn, y, gate, w_out_b)


def _grad_matmul(a3, b3, nblk, a_idx, b_idx, out_shape, out_block, out_idx, ride=None):
    _, s_len, m = a3.shape
    n = b3.shape[2]
    tk = _tile(s_len, ROWS_GRAD_MATMUL)

    def body(a_ref, b_ref, o_ref):
        @pl.when(pl.program_id(1) == 0)
        def _():
            o_ref[...] = jnp.zeros_like(o_ref)

        o_ref[...] += _dot_tn(a_ref[0], b_ref[0])

    (out,), got = _pcall_ride(
        body, ride, name="grad_matmul", grid=(nblk, s_len // tk),
        in_specs=[pl.BlockSpec((1, tk, m), lambda p, t: (a_idx(p), t, 0)),
                  pl.BlockSpec((1, tk, n), lambda p, t: (b_idx(p), t, 0))],
        out_specs=[pl.BlockSpec((None,) + out_block, lambda p, t: (0,) + out_idx(p))],
        out_shape=[jax.ShapeDtypeStruct((1,) + out_shape, F32)],
        compiler_params=_seq(2), args=(a3, b3))
    return out, got


DU_PLANE = (2, 3, 4, 0, 1)


def _mlstm_bwd(qkv, gates, cst, nst, mst, cell, u, ml_g, d_ycat, wif_b, ride=None):
    _, s_len, d = qkv.shape
    ng = gates[0].shape[1]
    heads = ng // 2
    hd = d // heads
    lc = ML_CHUNK
    nc = s_len // lc
    kscale = hd ** -0.5

    def body(qkv_ref, gt_ref, gtt_ref, bc_ref, bct_ref, cst_ref, nst_ref, mst_ref, cell_ref, o_ref, z_ref, g_ref, dy_ref,
             wif_ref, dqkv_ref, dgt_ref, dbif_ref, du_ref, dg_ref, dcs, dns):
        @pl.when(pl.program_id(0) == 0)
        def _():
            dbif_ref[...] = jnp.zeros_like(dbif_ref)
            dcs[...] = jnp.zeros_like(dcs)
            dns[...] = jnp.zeros_like(dns)
            dg_ref[...] = jnp.zeros_like(dg_ref)

        causal, tril, triu = _tri_masks(lc)
        tril_strict = (tril.astype(F32) - (tril * triu).astype(F32)).astype(BF16)
        gtv, gttv, bcv, bctv = gt_ref[...], gtt_ref[...], bc_ref[...], bct_ref[...]
        lane = lax.broadcasted_iota(jnp.int32, (lc, ng), 1)
        dli_all = jnp.zeros((lc, ng), F32)
        from_later = jnp.zeros((lc, ng), F32)
        from_earlier = jnp.zeros((lc, ng), F32)
        across_all = jnp.zeros((1, ng), F32)
        old = [(dcs[h], dns[h]) for h in range(heads)]
        new, d_o, d_z, d_g, dqs, dks, dvs = [], [], [], [], [], [], []
        for h in range(heads):
            hs = slice(h * hd, (h + 1) * hd)
            li_c, li_r, gf_c, b_c, b_r = _chunk_gates(gtv, gttv, bcv, bctv, h, heads)
            m_prev = mst_ref[0, h][:, 0:1]
            m_t, w_intra, w_inter, _, w_state, decay = _chunk_weights(li_c, li_r, b_c, b_r, m_prev, causal)
            qb = qkv_ref[0, :, hs]
            qf = qb.astype(F32)
            ks = qkv_ref[1, :, hs].astype(F32) * kscale
            kb = _bf(ks)
            vb = qkv_ref[2, :, hs]
            c_b = cst_ref[0, h]
            n_old = nst_ref[0, h]
            s = _dot_nt(qb, kb) * w_intra
            den = _rowsum(s) + w_inter * _rowsum(qf * n_old)
            floor = jnp.exp(-m_t)
            dstab = jnp.maximum(jnp.abs(den), floor)
            cell = cell_ref[:, hs]
            o = o_ref[:, hs]
            so = _sigmoid(o)
            hm = so * cell
            rinv = lax.rsqrt(jnp.mean(hm * hm, axis=-1, keepdims=True) + EPS)
            hn = hm * rinv
            z = z_ref[:, hs]
            sgz = _sigmoid(z)
            sz = z * sgz
            gh = g_ref[:, hs]
            dy = dy_ref[0, :, hs]
            d_z.append(_bf(dy * (hn * gh) * _dsilu(z, sgz)))
            d_g.append(_colsum(dy * hn * sz))
            dhn = dy * gh * sz
            dhm = rinv * (dhn - hn * jnp.mean(dhn * hn, axis=-1, keepdims=True))
            d_o.append(_bf(dhm * cell * so * (1.0 - so)))
            dcell = dhm * so
            dnum = dcell / dstab
            dnb = _bf(dnum)
            dden = -_rowsum(dcell * cell) / dstab * jnp.where(jnp.abs(den) > floor, jnp.where(den > 0.0, 1.0, -1.0), 0.0)
            dst = _dot_nt(dnb, vb) + dden
            dsdb = _bf(dst * w_intra)
            dc_out, dn_out = old[h]
            dcb = _bf(dc_out)
            dq_inter = w_inter * (_dot_nt(dnb, c_b) + dden * n_old)
            dk_inter = w_state * (_dot_nt(vb, dcb) + dn_out)
            dq = _dot(dsdb, kb) + dq_inter
            dk = _dot_tn(dsdb, qb) + dk_inter
            dv = _dot_tn(_bf(s), dnb) + _dot(_bf(ks * w_state), dcb)
            wq = w_inter * qf
            new.append((decay * dc_out + _dot_tn(_bf(wq), dnb), decay * dn_out + _colsum(wq * dden)))
            pmat = dst * s
            p_rows = _rowsum(pmat)
            p_cols = _rowsum(pmat.T)
            q_in = _rowsum(qf * dq_inter)
            k_in = _rowsum(ks * dk_inter)
            across = decay * (jnp.sum(dc_out * c_b.astype(F32), keepdims=True) + jnp.sum(dn_out * n_old, keepdims=True))
            dli_all = dli_all + jnp.where(lane == h, p_cols + k_in, 0.0)
            from_later = from_later + jnp.where(lane == heads + h, p_rows - p_cols + q_in, 0.0)
            from_earlier = from_earlier + jnp.where(lane == heads + h, k_in, 0.0)
            across_all = across_all + jnp.where(lane[0:1] == heads + h, across, 0.0)
            dqs.append(dq)
            dks.append(dk * kscale)
            dvs.append(dv)
        for h in range(heads):
            dcs[h], dns[h] = new[h]
        du_ref[0] = jnp.concatenate(d_o, axis=1)
        du_ref[1] = jnp.concatenate(d_z, axis=1)
        dg_ref[...] += jnp.concatenate(d_g, axis=1)
        dlf = _tri_dot_left(triu, from_later) + _tri_dot_left(tril_strict, from_earlier) + across_all
        dgt = dli_all + dlf * _sigmoid(-gtv)
        dgt_ref[...] = dgt
        dbif_ref[...] += _colsum(dgt)
        dgb = _bf(dgt)
        dqkv_ref[0] = _bf(jnp.concatenate(dqs, axis=1) + _dot_nt(dgb, wif_ref[0:d, :]))
        dqkv_ref[1] = _bf(jnp.concatenate(dks, axis=1) + _dot_nt(dgb, wif_ref[d:2 * d, :]))
        dqkv_ref[2] = _bf(jnp.concatenate(dvs, axis=1) + _dot_nt(dgb, wif_ref[2 * d:3 * d, :]))

    rev = lambda c: nc - 1 - c
    row = pl.BlockSpec((lc, d), lambda c: (rev(c), 0))
    gcol = pl.BlockSpec((lc, ng), lambda c: (rev(c), 0))
    grow = pl.BlockSpec((ng, lc), lambda c: (0, rev(c)))
    return _pcall_ride(
        body, ride, name="mlstm_bwd", grid=(nc,),
        in_specs=[pl.BlockSpec((3, lc, d), lambda c: (0, rev(c), 0)), gcol, grow, gcol, grow,
                  pl.BlockSpec((1, heads, hd, hd), lambda c: (rev(c), 0, 0, 0)),
                  pl.BlockSpec((1, heads, 1, hd), lambda c: (rev(c), 0, 0, 0)),
                  pl.BlockSpec((1, heads, 1, 128), lambda c: (rev(c), 0, 0, 0)),
                  row, pl.BlockSpec((lc, d), lambda c: (rev(c), 3)), pl.BlockSpec((lc, d), lambda c: (rev(c), 4)),
                  pl.BlockSpec((1, d), lambda c: (0, 0)), pl.BlockSpec((1, lc, d), lambda c: (1, rev(c), 0)),
                  pl.BlockSpec((3 * d, ng), lambda c: (0, 0))],
        out_specs=[pl.BlockSpec((3, lc, d), lambda c: (0, rev(c), 0)), pl.BlockSpec((lc, ng), lambda c: (rev(c), 0)),
                   pl.BlockSpec((1, ng), lambda c: (0, 0)), pl.BlockSpec((2, lc, d), lambda c: (0, rev(c), 0)),
                   pl.BlockSpec((1, d), lambda c: (0, 0))],
        out_shape=[jax.ShapeDtypeStruct((3, s_len, d), BF16), jax.ShapeDtypeStruct((s_len, ng), F32),
                   jax.ShapeDtypeStruct((1, ng), F32), jax.ShapeDtypeStruct((5, s_len, d), BF16),
                   jax.ShapeDtypeStruct((1, d), F32)],
        scratch_shapes=[pltpu.VMEM((heads, hd, hd), F32), pltpu.VMEM((heads, 1, hd), F32)],
        compiler_params=_seq(),
        args=(qkv, *gates, cst, nst, mst, cell, u, u, ml_g, d_ycat, wif_b))


def _conv_bwd_tile(dp, later, taps, cw_ref, gw_ref, gb_ref):
    tm = dp.shape[0]
    dwin = jnp.concatenate([dp, later[...]], axis=0)
    later[...] = dp[0:HALO]
    acc = cw_ref[CONV_WIDTH - 1:CONV_WIDTH, :] * dp
    for k in range(CONV_WIDTH):
        if k < CONV_WIDTH - 1:
            acc = acc + cw_ref[k:k + 1, :] * _shift_up(dwin, CONV_WIDTH - 1 - k)[0:tm]
        gw_ref[k:k + 1, :] += _colsum(dp * taps[k])
    gb_ref[...] += _colsum(dp)
    return acc


def _ml_pre_bwd(dqkv, u, conv_w, conv_b, wqkv_b, du):
    s_len = u.shape[0]
    d = conv_w.shape[1]
    _, heads, hd, _ = wqkv_b.shape
    tm = _tile(s_len, ROWS_VECTOR)
    per = tm // HALO
    nt = s_len // tm

    def body(dqkv_ref, x_ref, xp_ref, cw_ref, cb_ref, w_ref, _, dx_ref, gw_ref, gcw_ref, gcb_ref, later, dps, dxs):
        i = pl.program_id(0)

        @pl.when(i == 0)
        def _():
            gw_ref[...] = jnp.zeros_like(gw_ref)
            gcw_ref[...] = jnp.zeros_like(gcw_ref)
            gcb_ref[...] = jnp.zeros_like(gcb_ref)
            later[...] = jnp.zeros_like(later)

        prev = jnp.where(i == nt - 1, 0.0, xp_ref[...])
        xm = x_ref[...]
        taps = _conv_taps(jnp.concatenate([prev, xm], axis=0))
        pre = _conv_fwd(taps, cw_ref, cb_ref)
        sg = _sigmoid(pre)
        xcb = _bf(pre * sg)
        xmb = _bf(xm)
        for h in range(heads):
            hs = slice(h * hd, (h + 1) * hd)
            dqh, dkh, dvh = dqkv_ref[0, :, hs], dqkv_ref[1, :, hs], dqkv_ref[2, :, hs]
            dxc = _dot_nt(dqh, w_ref[0, h]) + _dot_nt(dkh, w_ref[1, h])
            dps[:, hs] = dxc * _dsilu(pre[:, hs], sg[:, hs])
            dxs[:, hs] = _dot_nt(dvh, w_ref[2, h])
            gw_ref[0, h] += _dot_tn(xcb[:, hs], dqh)
            gw_ref[1, h] += _dot_tn(xcb[:, hs], dkh)
            gw_ref[2, h] += _dot_tn(xmb[:, hs], dvh)
        dx_ref[0] = _bf(_conv_bwd_tile(dps[...], later, taps, cw_ref, gcw_ref, gcb_ref) + dxs[...])

    rev = lambda i: nt - 1 - i
    vec = pl.BlockSpec((1, d), lambda i: (0, 0))
    cwb = pl.BlockSpec((CONV_WIDTH, d), lambda i: (0, 0))
    whole4 = pl.BlockSpec(wqkv_b.shape, lambda i: (0, 0, 0, 0))
    return _pcall(
        body, name="ml_pre_bwd", grid=(nt,),
        in_specs=[pl.BlockSpec((3, tm, d), lambda i: (0, rev(i), 0)), pl.BlockSpec((tm, d), lambda i: (rev(i), 2)),
                  pl.BlockSpec((HALO, d), lambda i: (jnp.maximum(rev(i) * per - 1, 0), 2)),
                  cwb, vec, whole4, pl.BlockSpec(memory_space=pl.ANY)],
        out_specs=[pl.BlockSpec((1, tm, d), lambda i: (DU_PLANE[2], rev(i), 0)), whole4, cwb, vec],
        out_shape=[jax.ShapeDtypeStruct(du.shape, BF16), jax.ShapeDtypeStruct(wqkv_b.shape, F32),
                   jax.ShapeDtypeStruct((CONV_WIDTH, d), F32), jax.ShapeDtypeStruct((1, d), F32)],
        scratch_shapes=[pltpu.VMEM((HALO, d), F32), pltpu.VMEM((tm, d), F32), pltpu.VMEM((tm, d), F32)],
        input_output_aliases={6: 0},
        compiler_params=_seq(),
    )(dqkv, u, u, conv_w, conv_b, wqkv_b, du)


def _rg_bwd(d_ycat, u, hh, conv_w, conv_b, wa_b, ba, wx_b, bx, lam, du):
    s_len = u.shape[0]
    d = conv_w.shape[1]
    heads, hd, _ = wa_b.shape
    tm = _tile(s_len, ROWS_VECTOR)
    per = tm // HALO
    nt = s_len // tm

    def body(dy_ref, x_ref, xp_ref, z_ref, hh_ref, hp_ref, cw_ref, cb_ref, wa_ref, ba_ref, wx_ref, bx_ref, lam_ref, _,
             du_ref, gwa_ref, gwx_ref, gba_ref, gbx_ref, glam_ref, gcw_ref, gcb_ref, carry, gbuf, later, dxcs):
        i = pl.program_id(0)
        first = i == nt - 1

        @pl.when(i == 0)
        def _():
            carry[...] = jnp.zeros_like(carry)
            later[...] = jnp.zeros_like(later)
            gwa_ref[...] = jnp.zeros_like(gwa_ref)
            gwx_ref[...] = jnp.zeros_like(gwx_ref)
            gba_ref[...] = jnp.zeros_like(gba_ref)
            gbx_ref[...] = jnp.zeros_like(gbx_ref)
            glam_ref[...] = jnp.zeros_like(glam_ref)
            gcw_ref[...] = jnp.zeros_like(gcw_ref)
            gcb_ref[...] = jnp.zeros_like(gcb_ref)

        prev = jnp.where(first, 0.0, xp_ref[...])
        taps = _conv_taps(jnp.concatenate([prev, x_ref[...]], axis=0))
        xc = _conv_fwd(taps, cw_ref, cb_ref)
        r, ig, sp, log_a, a, mult = _rg_gates(xc, wa_ref, ba_ref, wx_ref, bx_ref, lam_ref)
        z = z_ref[...]
        sgz = _sigmoid(z)
        dy = dy_ref[0]
        hh_v = hh_ref[...]
        du_ref[1] = _bf(dy * hh_v * _dsilu(z, sgz))
        dhh = dy * (z * sgz)
        rows = lax.broadcasted_iota(jnp.int32, a.shape, 0)
        coef = jnp.where(rows == tm - 1, carry[1:2, :], _shift_up(a, 1))
        ca, cu = _scan_groups(coef, dhh, reverse=True)
        c = carry[0:1, :]
        for j in range(tm // 8 - 1, -1, -1):
            blk = ca[j * 8:(j + 1) * 8] * c + cu[j * 8:(j + 1) * 8]
            gbuf[j * 8:(j + 1) * 8, :] = blk
            c = blk[0:1]
        carry[0:1, :] = c
        carry[1:2, :] = a[0:1]
        g = gbuf[...]
        hprev_tile = jnp.where(first, 0.0, hp_ref[...])
        hprev = _shift_down(jnp.concatenate([hprev_tile, hh_v], axis=0), 1)[HALO:]
        da = g * hprev
        gx_ = g * xc
        d_mult = gx_ * ig
        d_ig = gx_ * mult
        dxc = g * mult * ig
        dlog_a = da * a - d_mult * (a * a / mult)
        d_r = dlog_a * ((-RG_C) * sp)
        glam_ref[...] += _colsum(dlog_a * ((-RG_C) * r)) * (-_sigmoid(-lam_ref[...]))
        d_ga = d_r * r * (1.0 - r)
        d_gx = d_ig * ig * (1.0 - ig)
        gba_ref[...] += _colsum(d_ga)
        gbx_ref[...] += _colsum(d_gx)
        xb = _bf(xc)
        dgab = _bf(d_ga)
        dgxb = _bf(d_gx)
        for h in range(heads):
            hs = slice(h * hd, (h + 1) * hd)
            dxcs[:, hs] = dxc[:, hs] + _dot_nt(dgab[:, hs], wa_ref[h]) + _dot_nt(dgxb[:, hs], wx_ref[h])
            gwa_ref[h] += _dot_tn(xb[:, hs], dgab[:, hs])
            gwx_ref[h] += _dot_tn(xb[:, hs], dgxb[:, hs])
        du_ref[0] = _bf(_conv_bwd_tile(dxcs[...], later, taps, cw_ref, gcw_ref, gcb_ref))

    assert DU_PLANE[0] % 2 == 0 and DU_PLANE[1] == DU_PLANE[0] + 1
    rev = lambda i: nt - 1 - i
    row = pl.BlockSpec((tm, d), lambda i: (rev(i), 0))
    halo_prev = lambda col: pl.BlockSpec((HALO, d), lambda i: (jnp.maximum(rev(i) * per - 1, 0), col))
    vec = pl.BlockSpec((1, d), lambda i: (0, 0))
    cwb = pl.BlockSpec((CONV_WIDTH, d), lambda i: (0, 0))
    whole3 = lambda a: pl.BlockSpec(a.shape, lambda i: (0, 0, 0))
    return _pcall(
        body, name="rg_bwd", grid=(nt,),
        in_specs=[pl.BlockSpec((1, tm, d), lambda i: (0, rev(i), 0)), row, halo_prev(0),
                  pl.BlockSpec((tm, d), lambda i: (rev(i), 1)), row, halo_prev(0),
                  cwb, vec, whole3(wa_b), vec, whole3(wx_b), vec, vec, pl.BlockSpec(memory_space=pl.ANY)],
        out_specs=[pl.BlockSpec((2, tm, d), lambda i: (DU_PLANE[0] // 2, rev(i), 0)), whole3(wa_b), whole3(wa_b),
                   vec, vec, vec, cwb, vec],
        out_shape=[jax.ShapeDtypeStruct(du.shape, BF16), jax.ShapeDtypeStruct(wa_b.shape, F32),
                   jax.ShapeDtypeStruct(wa_b.shape, F32)] + [jax.ShapeDtypeStruct((1, d), F32)] * 3
        + [jax.ShapeDtypeStruct((CONV_WIDTH, d), F32), jax.ShapeDtypeStruct((1, d), F32)],
        scratch_shapes=[pltpu.VMEM((8, d), F32), pltpu.VMEM((tm, d), F32), pltpu.VMEM((HALO, d), F32),
                        pltpu.VMEM((tm, d), F32)],
        input_output_aliases={13: 0},
        compiler_params=_seq(),
    )(d_ycat, u, u, u, hh, hh, conv_w, conv_b, wa_b, ba, wx_b, bx, lam, du)


def _in_bwd(du, w4, x, dxn, g, scale, ride=None):
    s_len, d = x.shape
    tm = _tile(s_len, ROWS_IN_BWD)
    nsh_chips, _, nsh = w4.shape
    npc = du.shape[0]
    ck = d // 4
    assert nsh % ck == 0 and npc * d == nsh_chips * nsh

    def body(du_ref, w_ref, x_ref, dxn_ref, g_ref, sc_ref, dx_ref, dsh_ref, dsc_ref, dg_ref):
        @pl.when(pl.program_id(0) == 0)
        def _():
            dsh_ref[...] = jnp.zeros_like(dsh_ref)
            dsc_ref[...] = jnp.zeros_like(dsc_ref)
            dg_ref[...] = jnp.zeros_like(dg_ref)

        dh = None
        for q in range(npc * d // ck):
            col = q * ck
            p, pc = col // d, col % d
            s, sc = col // nsh, col % nsh
            t = _dot_nt(du_ref[DU_PLANE[p], :, pc:pc + ck], w_ref[s, :, sc:sc + ck])
            dh = t if dh is None else dh + t
        xv = x_ref[...]
        r = lax.rsqrt(jnp.mean(xv * xv, axis=-1, keepdims=True) + EPS)
        xn = xv * r
        gv = g_ref[...]
        onesc = 1.0 + sc_ref[...]
        dsh_ref[...] += _colsum(dh)
        dsc_ref[...] += _colsum(dh * (xn * gv))
        dg_ref[...] += _colsum(dh * xn * onesc)
        dxh = dh * (gv * onesc)
        dx_ref[...] = dxn_ref[...] + r * (dxh - xn * jnp.mean(dxh * xn, axis=-1, keepdims=True))

    row = pl.BlockSpec((tm, d), lambda i: (i, 0))
    vec = pl.BlockSpec((1, d), lambda i: (0, 0))
    return _pcall_ride(
        body, ride, name="in_bwd", grid=(s_len // tm,),
        in_specs=[pl.BlockSpec((npc, tm, d), lambda i: (0, i, 0)), pl.BlockSpec(w4.shape, lambda i: (0, 0, 0)), row, row,
                  vec, vec],
        out_specs=[row, vec, vec, vec],
        out_shape=[jax.ShapeDtypeStruct((s_len, d), F32)] + [jax.ShapeDtypeStruct((1, d), F32)] * 3,
        compiler_params=_seq(),
        args=(du, w4, x, dxn, g, scale))


def _layer_fwd(x, p, rides=None, loss_head=None):
    rides = rides or {}
    landed = {}
    ride = lambda kernel: rides[kernel](landed) if kernel in rides else None
    (h_b, u), landed["ln_inproj"] = _ln_inproj(x, p["norm_g"], p["scale"], p["shift"], p["w4"], ride("ln_inproj"))
    (hh, ycat), landed["rg_fwd"] = _rg_fwd(u, p["rg_conv_w"], p["rg_conv_b"], p["rg_wa_b"], p["rg_ba"], p["rg_wx_b"],
                                           p["rg_bx"], p["rg_lam"], ride("rg_fwd"))
    if "late" in rides:
        p = {**p, **rides["late"](landed)}
    qkv, *gates = _ml_pre(u, p["ml_conv_w"], p["ml_conv_b"], p["wqkv_b"], p["wif_b"], p["wift_b"], p["b_if"],
                          p["b_ift"])
    (cell, ycat, cst, nst, mst), landed["mlstm_fwd"] = _mlstm_fwd(qkv, gates, u, p["ml_g"], ycat, ride("mlstm_fwd"))
    if loss_head is None:
        (y, x_new), landed["out_proj"] = _out_proj(ycat, p["w_out_b"], x, p["gate"], ride("out_proj"))
    else:
        y, *x_new = _out_proj_loss(ycat, p["w_out_b"], x, p["gate"], *loss_head)
    saved = dict(x=x, h_b=h_b, u=u, hh=hh, qkv=qkv, gates=gates, cell=cell, ycat=ycat, cst=cst, nst=nst, mst=mst, y=y)
    return x_new, saved, p, landed


def _layer_bwd(dxn, p, s, rides=None):
    rides = rides or {}
    landed = {}
    ride = lambda kernel: rides[kernel](grads, landed) if kernel in rides else None
    u = s["u"]
    d = dxn.shape[1]
    d_gate, dy_b, d_ycat = _out_bwd(dxn, s["y"], p["gate"], p["w_out_b"])
    grads = dict(w_out=_grad_matmul(s["ycat"], dy_b[None], 2, lambda b: b, lambda b: 0, (2 * d, d), (d, d),
                                    lambda b: (b, 0))[0])
    (dqkv, dgt, g_b_if, du, g_ml_g), landed["mlstm_bwd"] = _mlstm_bwd(
        s["qkv"], s["gates"], s["cst"], s["nst"], s["mst"], s["cell"], u, p["ml_g"], d_ycat, p["wif_b"],
        ride("mlstm_bwd"))
    ng = dgt.shape[1]
    g_w_if = _grad_matmul(s["qkv"], _bf(dgt)[None], 3, lambda b: b, lambda b: 0, (3 * d, ng), (d, ng),
                          lambda b: (b, 0))[0][0]
    du, g_wqkv, g_ml_cw, g_ml_cb = _ml_pre_bwd(dqkv, u, p["ml_conv_w"], p["ml_conv_b"], p["wqkv_b"], du)
    du, g_wa, g_wx, g_ba, g_bx, g_lam, g_rg_cw, g_rg_cb = _rg_bwd(d_ycat, u, s["hh"], p["rg_conv_w"], p["rg_conv_b"],
                                                                  p["rg_wa_b"], p["rg_ba"], p["rg_wx_b"], p["rg_bx"],
                                                                  p["rg_lam"], du)
    grads.update(rg_conv_w=g_rg_cw, rg_conv_b=g_rg_cb, rg_w_a=g_wa, rg_b_a=g_ba, rg_w_x=g_wx, rg_b_x=g_bx,
                 rg_lambda=g_lam, ml_conv_w=g_ml_cw, ml_conv_b=g_ml_cb, ml_w_qkv=g_wqkv, ml_w_if=g_w_if, ml_b_if=g_b_if,
                 ml_norm_g=g_ml_g)
    npc = du.shape[0]
    grads["w_in"], landed["grad_w_in"] = _grad_matmul(
        s["h_b"][None], du, npc, lambda b: 0, lambda b: (b + DU_PLANE[0]) % npc, (d, npc * d), (d, d),
        lambda b: (0, b), ride("grad_w_in"))
    (dx, d_shift, d_scale, grads["norm_g"]), landed["in_bwd"] = _in_bwd(du, p["w4"], s["x"], dxn, p["norm_g"],
                                                                        p["scale"], ride("in_bwd"))
    return dx, grads, jnp.concatenate([d_shift, d_scale, d_gate], axis=1), landed


def _me():
    return lax.axis_index("x"), lax.axis_index("y"), lax.axis_index("c")


def _remote(src, dst, send_sem, recv_sem, to):
    return pltpu.make_async_remote_copy(src_ref=src, dst_ref=dst, send_sem=send_sem, recv_sem=recv_sem,
                                        device_id=to, device_id_type=MESH)


def _all_gather8(blocks, space):
    n = len(blocks)
    relay = [b.size * b.dtype.itemsize >= RELAY_BYTES and b.shape[0] % 32 == 0 for b in blocks]

    def body(*refs):
        x_refs, out_refs = refs[:n], refs[n:2 * n]
        send_sems, recv_sems, local_sems = refs[2 * n:]
        x, y, c = _me()
        me, sibling = (x, y, c), (x, y, 1 - c)
        by_x, by_y, across = (1 - x, y, c), (x, 1 - y, c), (1 - x, 1 - y, c)

        def rows(i, blk, part=None):
            m_per = blocks[i].shape[0]
            at = (4 * blk[0] + 2 * blk[1] + blk[2]) * m_per
            if part is not None:
                m_per //= 2
                at += part * m_per
            return out_refs[i].at[pl.ds(at, m_per), :]

        def copy(i, k, blk, to, src=None, part=None):
            return _remote(rows(i, blk, part) if src is None else src, rows(i, blk, part), send_sems.at[8 * i + k],
                           recv_sems.at[8 * i + k], to)

        mine = [pltpu.make_async_copy(x_refs[i], rows(i, me), local_sems.at[i]) for i in range(n)]
        first = []
        for i in range(n):
            first.append(copy(i, 0, me, sibling, src=x_refs[i]))
            first += [copy(i, 1, me, by_x, src=x_refs[i]), copy(i, 2, me, by_y, src=x_refs[i])]
            if not relay[i]:
                first.append(copy(i, 3, me, across, src=x_refs[i]))
        for cp in mine + first:
            cp.start()
        passed = []
        for i in range(n):
            copy(i, 1, by_x, me).wait_recv()
            passed.append(copy(i, 4, by_x, sibling))
            if relay[i]:
                passed.append(copy(i, 3, by_x, by_y, part=0))
        for cp in passed:
            cp.start()
        n_x = len(passed)
        for i in range(n):
            copy(i, 2, by_y, me).wait_recv()
            passed.append(copy(i, 5, by_y, sibling))
            if relay[i]:
                passed.append(copy(i, 7, by_y, by_x, part=1))
        for cp in passed[n_x:]:
            cp.start()
        for i in range(n):
            if relay[i]:
                copy(i, 3, across, me, part=0).wait_recv()
                copy(i, 7, across, me, part=1).wait_recv()
            else:
                copy(i, 3, across, me).wait_recv()
            passed.append(copy(i, 6, across, sibling))
            passed[-1].start()
        for i in range(n):
            copy(i, 0, sibling, me).wait_recv()
            for k, blk in ((4, by_x), (5, by_y), (6, across)):
                copy(i, k, (blk[0], blk[1], 1 - c), me).wait_recv()
        for cp in first + passed:
            cp.wait_send()
        for cp in mine:
            cp.wait()

    spec = pl.BlockSpec(memory_space=space)
    return _pcall(
        body, name="all_gather8",
        out_shape=[jax.ShapeDtypeStruct((8 * b.shape[0], b.shape[1]), b.dtype) for b in blocks],
        in_specs=[spec] * n, out_specs=[spec] * n,
        scratch_shapes=[pltpu.SemaphoreType.DMA((8 * n,)), pltpu.SemaphoreType.DMA((8 * n,)),
                        pltpu.SemaphoreType.DMA((n,))],
    )(*blocks)


def _exchange(legs):
    n = len(legs)

    def body(*refs):
        copies, local, relays = _exchange_body(legs, refs[:n], refs[n:2 * n], *refs[2 * n:])
        for cp in copies + local:
            cp.start()
        _hand_on(relays)
        _wait_all(copies, local, relays)

    hbm = pl.BlockSpec(memory_space=pltpu.HBM)
    return _pcall(body, name="exchange", out_shape=[leg.landing() for leg in legs], in_specs=[hbm] * n,
                  out_specs=[hbm] * n, input_output_aliases=_exchange_aliases(legs, 0, 0),
                  scratch_shapes=_exchange_sems(legs))(*[leg.src for leg in legs])


def _row_tile(rows, cap=4096, mult=16):
    best = None
    for t in range(mult, min(rows, cap) + 1, mult):
        if rows % t == 0:
            best = t
    return rows if best is None else best


def _pair_sum(half, own, own_spec, got, got_spec, out_shape, out_spec, grid):
    def body(_, a_ref, b_ref, o_ref):
        o_ref[...] = (a_ref[...] + b_ref[...].astype(F32)).astype(o_ref.dtype)

    return _pcall(
        body, name="pair_sum",
        grid_spec=pltpu.PrefetchScalarGridSpec(num_scalar_prefetch=1, grid=grid, in_specs=[own_spec, got_spec],
                                               out_specs=out_spec),
        out_shape=out_shape, compiler_params=_seq(len(grid)))(half, own, got)


def _chip_sum(ids, part, met, fill, layer=0, stack=1):
    _, _, rows, n = part.shape
    tr = _row_tile(rows, cap=max(16, BLOCK_ELEMS // n))
    first = isinstance(stack, int)

    def body(_, own_ref, a_ref, b_ref, c_ref, *rest):
        acc = own_ref[...].astype(F32) + a_ref[...].astype(F32)
        acc = acc + b_ref[...].astype(F32)
        rest[-1][...] = acc + c_ref[...].astype(F32)

    blk = (None, None, tr, n)
    other = lambda k: pl.BlockSpec(blk, lambda j, ids: ((ids[0] + k) % 4, 0, j, 0))
    in_specs = [pl.BlockSpec(blk, lambda j, ids: (ids[0], 0, j, 0)), other(1), other(2), other(3)]
    return _pcall(
        body, name="chip_sum",
        grid_spec=pltpu.PrefetchScalarGridSpec(
            num_scalar_prefetch=1, grid=(rows // tr,),
            in_specs=in_specs if first else in_specs + [pl.BlockSpec(memory_space=pl.ANY)],
            out_specs=pl.BlockSpec(blk, lambda j, ids: (layer, ids[1] if fill else 0, j, 0))),
        out_shape=jax.ShapeDtypeStruct(((stack,) if first else stack.shape[:1]) + (2 if fill else 1, rows, n), F32),
        input_output_aliases={} if first else {5: 0},
        compiler_params=_seq())(*((ids, part, met, met, met) if first else (ids, part, met, met, met, stack)))


def _ada_mod(c_all, w_ada, b_ada_cols):
    depth, d, n = w_ada.shape
    nb = c_all.shape[0]

    def body(c_ref, w_ref, b_ref, o_ref):
        cv = c_ref[...]
        ca = _bf(cv * _sigmoid(cv))
        o_ref[0] = _dot(ca, _bf(w_ref[0])) + b_ref[0]

    return _pcall(body, name="ada_mod", grid=(depth,),
                  in_specs=[pl.BlockSpec((nb, d), lambda l: (0, 0)), pl.BlockSpec((1, d, n), lambda l: (l, 0, 0)),
                            pl.BlockSpec((1, 1, n), lambda l: (l, 0, 0))],
                  out_specs=pl.BlockSpec((1, nb, n), lambda l: (l, 0, 0)),
                  out_shape=jax.ShapeDtypeStruct((depth, nb, n), F32), compiler_params=_seq())(c_all, w_ada, b_ada_cols)


def _ada_grad(c_all, dmod_cols, rows_all):
    nb, d = c_all.shape
    depth, _, n = dmod_cols.shape
    kinds, n_all = rows_all.shape[1], rows_all.shape[3]

    def body(c_ref, dm_ref, da_ref, gw_ref, gb_ref):
        cv = c_ref[...]
        ca = _bf(cv * _sigmoid(cv))
        gw_ref[0] = _dot_tn(ca, _bf(dm_ref[0]))
        for k in range(kinds):
            gb_ref[0, k] = _colsum(da_ref[0, k])

    return _pcall(body, name="ada_grad", grid=(depth,),
                  in_specs=[pl.BlockSpec((nb, d), lambda l: (0, 0)), pl.BlockSpec((1, nb, n), lambda l: (l, 0, 0)),
                            pl.BlockSpec((1, kinds, nb, n_all), lambda l: (l, 0, 0, 0))],
                  out_specs=[pl.BlockSpec((1, d, n), lambda l: (l, 0, 0)),
                             pl.BlockSpec((1, kinds, 1, n_all), lambda l: (l, 0, 0, 0))],
                  out_shape=[jax.ShapeDtypeStruct((depth, d, n), F32), jax.ShapeDtypeStruct((depth, kinds, 1, n_all), F32)],
                  compiler_params=_seq())(c_all, dmod_cols, rows_all)


def _adamw(items, ride=None):
    two_d = [tuple(t.reshape(w.size // w.shape[-1], w.shape[-1]) for t in (w, g, m, v)) for w, g, m, v in items]
    n = len(items)
    if n == 1:
        rows, cols = two_d[0][0].shape
        tr = _row_tile(rows, cap=max(8, BLOCK_ELEMS // cols), mult=8)
        blocks = [pl.BlockSpec((tr, cols), lambda i: (i, 0))]
        grid = (rows // tr,)
    else:
        blocks = [pl.BlockSpec(t[0].shape, lambda i: (0, 0)) for t in two_d]
        grid = (1,)

    def body(*refs):
        for k in range(n):
            w_ref, g_ref, m_ref, v_ref = refs[4 * k:4 * k + 4]
            d_ref, mo_ref, vo_ref = refs[4 * n + 3 * k:4 * n + 3 * k + 3]
            gv = g_ref[...]
            mn = ADAM_B1 * m_ref[...] + (1.0 - ADAM_B1) * gv
            vn = ADAM_B2 * v_ref[...] + (1.0 - ADAM_B2) * (gv * gv)
            m_hat = mn / (1.0 - ADAM_B1 ** ADAM_STEP)
            v_hat = vn / (1.0 - ADAM_B2 ** ADAM_STEP)
            d_ref[...] = -ADAM_LR * (m_hat / (jnp.sqrt(v_hat) + ADAM_EPS) + ADAM_WD * w_ref[...])
            mo_ref[...] = mn
            vo_ref[...] = vn

    outs, got = _pcall_ride(
        body, ride, name="adamw", grid=grid,
        in_specs=[b for b in blocks for _ in range(4)], out_specs=[b for b in blocks for _ in range(3)],
        out_shape=[jax.ShapeDtypeStruct(t[0].shape, F32) for t in two_d for _ in range(3)],
        compiler_params=_seq(), args=tuple(a for t in two_d for a in t))
    return [tuple(o.reshape(items[k][0].shape) for o in outs[3 * k:3 * k + 3]) for k in range(n)], got


WEIGHTS = ["norm_g", "w_ada", "b_ada", "w_in", "rg_conv_w", "rg_conv_b", "rg_w_a", "rg_b_a", "rg_w_x", "rg_b_x",
           "rg_lambda", "ml_conv_w", "ml_conv_b", "ml_w_q", "ml_w_k", "ml_w_v", "ml_w_if", "ml_b_if", "ml_norm_g",
           "w_out", "final_g"]
SMALL_SHARDED = {"rg_conv_w": 1, "ml_conv_w": 1, "ml_w_if": 0}
REPLICATED = ["rg_w_a", "rg_w_x", "rg_conv_b", "rg_b_a", "rg_b_x", "rg_lambda", "ml_conv_b", "ml_norm_g", "ml_b_if"]
LANES = 128


def _to_pieces(g, axis):
    shp = g.shape
    g = g.reshape(shp[:axis] + (4, 2, shp[axis] // 8) + shp[axis + 1:])
    g = jnp.moveaxis(g, (axis, axis + 1), (0, 1))
    return g.reshape(4, 2, -1)


def _from_pieces(p, shard_shape, axis):
    k = p.shape[0]
    rest = shard_shape[:axis] + (shard_shape[axis] // k,) + shard_shape[axis + 1:]
    t = jnp.moveaxis(p.reshape((k,) + rest), 0, axis)
    return t.reshape(shard_shape)


def _pad_rows(flat, mult):
    n = flat.shape[-1]
    pad = (-n) % mult
    if pad:
        flat = jnp.concatenate([flat, jnp.zeros(flat.shape[:-1] + (pad,), flat.dtype)], axis=-1)
    return flat


def kernel(x, c, norm_g, w_ada, b_ada, w_in, rg_conv_w, rg_conv_b, rg_w_a, rg_b_a, rg_w_x, rg_b_x, rg_lambda, ml_conv_w, ml_conv_b, ml_w_q, ml_w_k, ml_w_v, ml_w_if, ml_b_if, ml_norm_g, w_out, final_g, loss_target, m_norm_g, m_w_ada, m_b_ada, m_w_in, m_rg_conv_w, m_rg_conv_b, m_rg_w_a, m_rg_b_a, m_rg_w_x, m_rg_b_x, m_rg_lambda, m_ml_conv_w, m_ml_conv_b, m_ml_w_q, m_ml_w_k, m_ml_w_v, m_ml_w_if, m_ml_b_if, m_ml_norm_g, m_w_out, m_final_g, v_norm_g, v_w_ada, v_b_ada, v_w_in, v_rg_conv_w, v_rg_conv_b, v_rg_w_a, v_rg_b_a, v_rg_w_x, v_rg_b_x, v_rg_lambda, v_ml_conv_w, v_ml_conv_b, v_ml_w_q, v_ml_w_k, v_ml_w_v, v_ml_w_if, v_ml_b_if, v_ml_norm_g, v_w_out, v_final_g):
    given = dict(locals())
    ax, ay, ac = lax.axis_index("x"), lax.axis_index("y"), lax.axis_index("c")
    chip = 2 * ax + ay
    me = 2 * chip + ac
    depth, d = norm_g.shape
    n_ada = w_ada.shape[2]
    pick = lambda a, i, axis=0: lax.dynamic_index_in_dim(a, i, axis, keepdims=False)

    convs = jnp.stack([rg_conv_w, ml_conv_w])
    n_conv = 2 * depth * CONV_WIDTH // 4
    blk = jnp.concatenate([c, convs.reshape(n_conv, d), jnp.zeros((8 - 1 - n_conv, d), F32)], axis=0)
    w_in_first = lax.dynamic_slice_in_dim(w_in[0], ac * (d // 2), d // 2, 0).astype(BF16)
    g0, w_in_first = _all_gather8([blk, w_in_first], pltpu.HBM)
    g0 = g0.reshape(8, 8, d)
    c_all = g0[:, 0, :]
    conv_full = g0[0::2, 1:1 + n_conv].reshape(4, 2, depth, CONV_WIDTH, d // 4)
    conv_full = conv_full.transpose(1, 2, 3, 0, 4).reshape(2, depth, CONV_WIDTH, d)

    b_cols = lax.dynamic_slice_in_dim(b_ada, chip * n_ada, n_ada, axis=1)[:, None, :]
    mod_part = _ada_mod(c_all, w_ada, b_cols)
    g1 = _all_gather8([mod_part.transpose(1, 0, 2).reshape(8, depth * n_ada)], pltpu.VMEM)[0]
    g1 = g1.reshape(8, 8, depth, n_ada)[0::2]
    mod_me = pick(g1.transpose(1, 2, 0, 3).reshape(8, depth, 4 * n_ada), me)

    def half_of(w, axis):
        n = w.shape[axis] // 2
        return lax.dynamic_slice_in_dim(w, ac * n, n, axis).astype(BF16)

    n_sh = w_in.shape[2]
    heads, hd_cut, hd = ml_w_q.shape[1:]

    def blocks_of(l):
        wqkv = jnp.stack([ml_w_q[l], ml_w_k[l], ml_w_v[l]])
        return [half_of(w_in[l], 0), half_of(w_out[l], 0), half_of(wqkv, 2).reshape(-1, hd), half_of(ml_w_if[l], 0)]

    def layer_of(l, w4, rest):
        return dict(
            norm_g=norm_g[l][None], shift=mod_me[l, 0:d][None], scale=mod_me[l, d:2 * d][None],
            gate=mod_me[l, 2 * d:3 * d][None], w4=w4.reshape(4, d, n_sh),
            rg_conv_w=conv_full[0, l], rg_conv_b=rg_conv_b[l][None], rg_wa_b=_bf(rg_w_a[l]), rg_ba=rg_b_a[l][None],
            rg_wx_b=_bf(rg_w_x[l]), rg_bx=rg_b_x[l][None], rg_lam=rg_lambda[l][None],
            ml_conv_w=conv_full[1, l], ml_conv_b=ml_conv_b[l][None], b_if=ml_b_if[l][None], b_ift=ml_b_if[l][:, None],
            ml_g=ml_norm_g[l][None], **rest)

    def rest_of(gathered):
        w_out_b, wqkv_g, wif = gathered
        return dict(w_out_b=w_out_b, wqkv_b=_from_pieces(wqkv_g.reshape(8, -1), (3, heads, hd, hd), 2), wif_b=wif,
                    wift_b=wif.T)

    spread = lambda blocks: [Leg(b, "spread") for b in blocks]
    fill = lambda landed: [Leg(t, "sib_fill") for t in landed]
    flat = lambda filled: [t.reshape(-1, t.shape[-1]) for t in filled]
    first = blocks_of(0)
    n_rest = len(first) - 1
    p = layer_of(0, w_in_first, {})
    layers, saved = [], []
    xl = x[0]
    for l in range(depth):
        nxt = blocks_of(l + 1) if l + 1 < depth else []
        skip = n_rest if l == 0 else 0
        rides = dict(rg_fwd=lambda landed, nxt=nxt: spread(nxt[:1]))
        if l == 0:
            rides.update(ln_inproj=lambda landed: spread(first[1:]),
                         rg_fwd=lambda landed, nxt=nxt: fill(landed["ln_inproj"]) + spread(nxt[:1]),
                         late=lambda landed: rest_of(flat(landed["rg_fwd"][:n_rest])))
        if nxt:
            rides.update(mlstm_fwd=lambda landed, nxt=nxt: spread(nxt[1:]),
                         out_proj=lambda landed, skip=skip: fill(list(landed["rg_fwd"][skip:]) + list(landed["mlstm_fwd"])))
        xl, s, p, landed = _layer_fwd(xl, p, rides, None if nxt else (final_g[None], loss_target[0]))
        layers.append(p)
        saved.append(s)
        if nxt:
            arrived = flat(landed["out_proj"])
            p = layer_of(l + 1, arrived[0], rest_of(arrived[1:]))
    dx, g_final, loss = xl

    half = ac.reshape(1)
    ids = jnp.stack([chip, ac])
    r_out = w_out.shape[1] // 2

    def pair_in(g_w_in, got_in):
        return _pair_sum(
            half, g_w_in, pl.BlockSpec((None, d // 2, n_sh), lambda s, h: (0, h[0], s)),
            got_in, pl.BlockSpec((None, None, d // 2, n_sh), lambda s, h: (0, s, 0, 0)),
            jax.ShapeDtypeStruct((4, 1, d // 2, n_sh), BF16),
            pl.BlockSpec((None, None, d // 2, n_sh), lambda s, h: (s, 0, 0, 0)), (4,))

    def pair_out(g_out5, got_out):
        return _pair_sum(
            half, g_out5, pl.BlockSpec((None, None, None, r_out, d), lambda s, h: (0, s, h[0], 0, 0)),
            got_out, pl.BlockSpec((None, None, r_out, d), lambda s, h: (0, s, 0, 0)),
            jax.ShapeDtypeStruct((4, 1, r_out, d), BF16),
            pl.BlockSpec((None, None, r_out, d), lambda s, h: (s, 0, 0, 0)), (4,))

    def pair_slab(slab, got, dtype):
        rows = got.shape[0] // 4
        blk = pl.BlockSpec((rows, LANES), lambda s, h: (s, 0))
        return _pair_sum(half, slab, pl.BlockSpec((None, rows, LANES), lambda s, h: (h[0], s, 0)), got, blk,
                         jax.ShapeDtypeStruct((4 * rows, LANES), dtype), blk, (4,)).reshape(4, 1, rows, LANES)

    row_pad = lambda n: -(-n // (8 * LANES)) * (8 * LANES)

    def as_rows(t):
        if t.shape[-1] == LANES and t.size % (8 * LANES) == 0:
            return t.reshape(-1, LANES)
        return _pad_rows(t.reshape(-1), 8 * LANES).reshape(-1, LANES)

    chips = lambda arrs: [Leg(a, "chips") for a in arrs]
    out5 = lambda g: g["w_out"].reshape(1, 4, 2, r_out, d)
    r_q = hd // 8
    qkv5 = lambda g: g["ml_w_qkv"].reshape(3 * heads, 4, 2, r_q, hd)

    def pair_qkv(g5, got):
        return _pair_sum(
            half, g5, pl.BlockSpec((3 * heads, None, None, r_q, hd), lambda s, h: (0, s, h[0], 0, 0)),
            got, pl.BlockSpec((3 * heads, None, r_q, hd), lambda s, h: (0, s, 0, 0)),
            jax.ShapeDtypeStruct((4, 1, 3 * heads, r_q, hd), BF16),
            pl.BlockSpec((None, None, 3 * heads, r_q, hd), lambda s, h: (s, 0, 0, 0, 0)), (4,))

    grads, dmods, parts, mets = [None] * depth, [None] * depth, [None] * depth, [None] * depth
    small = {}

    def early_exchange(g, landed):
        every = [g] + grads[1:]
        sm = jnp.concatenate([_to_pieces(every[l][name], axis) for l in range(depth)
                              for name, axis in SMALL_SHARDED.items()], axis=-1)
        sm = _pad_rows(sm, 16 * LANES)
        sm = sm.transpose(1, 0, 2).reshape(2, -1, LANES)
        rep = [as_rows(every[l][name]) for l in range(depth) for name in REPLICATED]
        rep = jnp.concatenate(rep + [as_rows(g_final), as_rows(loss)], axis=0)
        rep = jnp.concatenate([rep, jnp.zeros(((-rep.shape[0]) % 64, LANES), F32)], axis=0)
        rep = rep.reshape(4, 2, -1, LANES).transpose(1, 0, 2, 3).reshape(2, -1, LANES)
        got_sm, got_rep, got_q = _exchange([Leg(sm, "sib_slab"), Leg(rep, "sib_slab"), Leg(qkv5(g), "sib_w_out")])
        small["parts"] = [pair_out(out5(g), landed["mlstm_bwd"][0]), pair_slab(sm, got_sm, BF16),
                          pair_slab(rep, got_rep, F32), pair_qkv(qkv5(g), got_q)]
        return chips(small["parts"])

    def last_exchange(g, landed):
        (got_in,) = _exchange([Leg(g["w_in"], "sib_w_in")])
        small["part_in"] = pair_in(g["w_in"], got_in)
        return chips([small["part_in"]])

    for l in reversed(range(depth)):
        above = parts[l + 1] if l + 1 < depth else []
        rides = dict(mlstm_bwd=lambda g, landed, above=above: [Leg(out5(g), "sib_w_out")] + chips(above),
                     in_bwd=lambda g, landed: [Leg(g["w_in"], "sib_w_in"), Leg(qkv5(g), "sib_w_out")])
        if l == 0:
            rides.update(grad_w_in=early_exchange, in_bwd=last_exchange)
        dx, grads[l], dmods[l], got = _layer_bwd(dx, layers[l], saved[l], rides)
        if above:
            mets[l + 1] = got["mlstm_bwd"][1:]
        if l > 0:
            parts[l] = [pair_in(grads[l]["w_in"], got["in_bwd"][0]), pair_out(out5(grads[l]), got["mlstm_bwd"][0]),
                        pair_qkv(qkv5(grads[l]), got["in_bwd"][1])]
    part_out, part_sm, part_rep, part_q = small["parts"]
    met_out, met_sm, met_rep, met_q = got["grad_w_in"]
    parts[0], mets[0] = [small["part_in"], part_out, part_q], [got["in_bwd"][0], met_out, met_q]
    n_rep = part_rep.shape[2]

    pad = lambda t: jnp.concatenate([t, jnp.zeros((1, 2 * d), F32)], axis=1)
    rows = [r for l in range(depth) for r in (dmods[l], pad(grads[l]["norm_g"]))]
    blk = jnp.concatenate(rows + [jnp.zeros((8 - 2 * depth, 3 * d), F32)], axis=0)
    red_rep = _chip_sum(ids, part_rep, met_rep, False).reshape(n_rep, LANES)
    rows_all, rep_all = _all_gather8([blk, red_rep], pltpu.VMEM)
    rows_all, rep_all = rows_all.reshape(8, 8, 3 * d)[:, :2 * depth], rep_all.reshape(-1)
    rows_all = rows_all.transpose(1, 0, 2).reshape(depth, 2, 8, 3 * d)
    dm_cols = lax.dynamic_slice_in_dim(rows_all[:, 0], chip * n_ada, n_ada, axis=2)
    g_w_ada, summed = _ada_grad(c_all, dm_cols, rows_all)

    g = dict(w_ada=g_w_ada, b_ada=summed[:, 0, 0], norm_g=summed[:, 1, 0, :d])
    item = lambda name: (given[name], g[name], given["m_" + name], given["v_" + name])
    both_in, both_out, both_q = depth, depth, depth
    flat_q = lambda t: t.reshape(4, 1, 3 * heads * r_q, hd)
    for l in range(depth):
        both_in = _chip_sum(ids, parts[l][0], mets[l][0], True, l, both_in)
        both_out = _chip_sum(ids, parts[l][1], mets[l][1], True, l, both_out)
        both_q = _chip_sum(ids, flat_q(parts[l][2]), flat_q(mets[l][2]), True, l, both_q)
    both_in, both_out, both_q, both_sm = _exchange(fill([both_in, both_out, both_q,
                                                         _chip_sum(ids, part_sm, met_sm, True)]))

    g.update(w_in=both_in.reshape(w_in.shape), w_out=both_out.reshape(w_out.shape))
    g_qkv = both_q.reshape(depth, 2, 3, heads, r_q, hd).transpose(0, 2, 3, 1, 4, 5)
    g_qkv = g_qkv.reshape(depth, 3, heads, 2 * r_q, hd)
    for i, name in enumerate(["ml_w_q", "ml_w_k", "ml_w_v"]):
        g[name] = g_qkv[:, i]
    shard = both_sm.reshape(2, -1)
    off = 0
    per_layer = {name: [] for name in SMALL_SHARDED}
    for l in range(depth):
        for name, axis in SMALL_SHARDED.items():
            n = grads[l][name].size // 8
            per_layer[name].append(_from_pieces(shard[:, off:off + n], given[name].shape[1:], axis))
            off += n
    for name in SMALL_SHARDED:
        g[name] = jnp.stack(per_layer[name])
    off = 0
    per_layer = {name: [] for name in REPLICATED}
    for l in range(depth):
        for name in REPLICATED:
            n = given[name][l].size
            per_layer[name].append(rep_all[off:off + n].reshape(given[name].shape[1:]))
            off += row_pad(n)
    for name in REPLICATED:
        g[name] = jnp.stack(per_layer[name])
    g["final_g"] = rep_all[off:off + d]
    loss_all = rep_all[off + row_pad(d)]

    stepped = {}
    rg_mats, ml_mats = ["rg_w_a", "rg_w_x"], ["ml_w_q", "ml_w_k", "ml_w_v"]
    vectors = [n for n in WEIGHTS if n not in ["w_ada", "w_in", "w_out"] + rg_mats + ml_mats]
    for names in (["w_ada"], ["w_in"], ["w_out"], rg_mats, ml_mats, vectors):
        stepped.update(zip(names, _adamw([item(name) for name in names])[0]))
    deltas, new_m, new_v = zip(*[stepped[name] for name in WEIGHTS])
    return (loss_all, dx[None], *[g[name] for name in WEIGHTS], *deltas, *new_m, *new_v)
```

```python
import functools
from typing import NamedTuple

import jax
import jax.numpy as jnp
from jax import lax
from jax.experimental import pallas as pl
from jax.experimental.pallas import tpu as pltpu

F32 = jnp.float32
BF16 = jnp.bfloat16

EPS = 1e-6
RG_C = 8.0
CONV_WIDTH = 4
ML_CHUNK = 512
HALO = 8
ROWS_VECTOR = 512
ROWS_MATMUL = 1024
ROWS_IN_BWD = 512
ROWS_GRAD_MATMUL = 2048
BLOCK_ELEMS = 1 << 18
RELAY_BYTES = 1 << 18
ADAM_LR = 0.001
ADAM_B1 = 0.9
ADAM_B2 = 0.999
ADAM_EPS = 1e-08
ADAM_WD = 0.01
ADAM_STEP = 10
MESH = pl.DeviceIdType.MESH


def _pcall(body, **kw):
    return pl.pallas_call(body, **kw)


class Leg(NamedTuple):
    src: jax.Array
    kind: str

    def landing(self):
        a = self.src
        shape = {"chips": lambda: a.shape, "spread": lambda: (4, 2) + a.shape, "sib_fill": lambda: a.shape,
                 "sib_w_in": lambda: (a.shape[0], 4, a.shape[1] // 2, a.shape[2] // 4),
                 "sib_w_out": lambda: a.shape[:2] + a.shape[3:], "sib_slab": lambda: a.shape[1:]}[self.kind]()
        return jax.ShapeDtypeStruct(shape, a.dtype)

    def relayed(self):
        a = self.src
        return self.kind == "spread" and a.size * a.dtype.itemsize >= RELAY_BYTES and a.shape[0] % 32 == 0

    def copies(self, src, dst, x, y, c):
        a, me_s, o = self.src, 2 * x + y, 1 - c
        chips = [(1 - x, y), (x, 1 - y), (1 - x, 1 - y)]
        if self.kind == "chips":
            return [(src.at[2 * px + py], dst.at[me_s], (px, py, c)) for px, py in chips], [], []
        if self.kind == "spread":
            own = dst.at[me_s, c]
            if not self.relayed():
                return [(src, own, (px, py, c)) for px, py in chips], [(src, own)], []
            by_x, by_y, half = chips[0], chips[1], a.shape[0] // 2
            part = lambda chip, k: dst.at[2 * chip[0] + chip[1], c, pl.ds(k * half, half)]
            return ([(src, own, (*by_x, c)), (src, own, (*by_y, c))], [(src, own)],
                    [(part(by_x, 0), part(by_x, 0), (*by_y, c), 0), (part(by_y, 1), part(by_y, 1), (*by_x, c), 1)])
        depth = pl.ds(0, a.shape[0])
        if self.kind == "sib_fill":
            return [(dst.at[depth, c], dst.at[depth, c], (x, y, o))], [], []
        if self.kind == "sib_w_in":
            half, n = a.shape[1] // 2, a.shape[2] // 4
            return [(src.at[depth, pl.ds(o * half, half), pl.ds(s * n, n)], dst.at[depth, s], (x, y, o))
                    for s in range(4)], [], []
        if self.kind == "sib_w_out":
            return [(src.at[depth, pl.ds(0, 4), o], dst, (x, y, o))], [], []
        return [(src.at[o], dst, (x, y, o))], [], []

    def n_copies(self):
        return 4 if self.relayed() else {"chips": 3, "spread": 3, "sib_w_in": 4}.get(self.kind, 1)


def _exchange_body(legs, srcs, dsts, send_sems, recv_sems, local_sems):
    x, y, c = _me()
    remote, local, relays, k = [], [], [], 0
    for i, leg in enumerate(legs):
        far, near, handed = leg.copies(srcs[i], dsts[i], x, y, c)
        at = len(remote)
        for src, dst, to in far:
            remote.append(_remote(src, dst, send_sems.at[k], recv_sems.at[k], to))
            k += 1
        for src, dst, to, after in handed:
            relays.append((_remote(src, dst, send_sems.at[k], recv_sems.at[k], to), remote[at + after]))
            k += 1
        local += [pltpu.make_async_copy(src, dst, local_sems.at[i]) for src, dst in near]
    return remote, local, relays


def _hand_on(relays):
    for cp, after in relays:
        after.wait_recv()
        cp.start()


def _wait_all(copies, local, relays):
    arrived, handed = [after for _, after in relays], [cp for cp, _ in relays]
    for cp in [cp for cp in copies if not any(cp is a for a in arrived)] + handed:
        cp.wait_recv()
    for cp in copies + handed:
        cp.wait_send()
    for cp in local:
        cp.wait()


def _exchange_sems(legs):
    n = sum(leg.n_copies() for leg in legs)
    return [pltpu.SemaphoreType.DMA((n,)), pltpu.SemaphoreType.DMA((n,)), pltpu.SemaphoreType.DMA((len(legs),))]


def _exchange_aliases(legs, n_in, n_out):
    return {n_in + i: n_out + i for i, leg in enumerate(legs) if leg.kind == "sib_fill"}


def _pcall_ride(body, ride, *, grid, in_specs, out_specs, out_shape, args, scratch_shapes=(), **kw):
    n_in, n_out, n_scr = len(in_specs), len(out_specs), len(scratch_shapes)
    if not ride:
        res = _pcall(body, grid=grid, in_specs=in_specs, out_specs=out_specs, out_shape=out_shape,
                     scratch_shapes=list(scratch_shapes), **kw)(*args)
        return res, []
    nr = len(ride)

    def riding(*refs):
        ins, rsrc = refs[:n_in], refs[n_in:n_in + nr]
        outs, rdst = refs[n_in + nr:n_in + nr + n_out], refs[n_in + nr + n_out:n_in + 2 * nr + n_out]
        scr = refs[n_in + 2 * nr + n_out:n_in + 2 * nr + n_out + n_scr]
        copies, local, relays = _exchange_body(ride, rsrc, rdst, *refs[n_in + 2 * nr + n_out + n_scr:])
        at_step = lambda steps: functools.reduce(jnp.logical_and, [pl.program_id(a) == s for a, s in enumerate(steps)])

        @pl.when(at_step([0] * len(grid)))
        def _():
            for cp in copies + local:
                cp.start()

        body(*ins, *outs, *scr)

        if relays:
            @pl.when(at_step([grid[0] // 2] + [0] * (len(grid) - 1)))
            def _():
                _hand_on(relays)

        @pl.when(at_step([g - 1 for g in grid]))
        def _():
            _wait_all(copies, local, relays)

    hbm = pl.BlockSpec(memory_space=pltpu.HBM)
    aliases = {**kw.pop("input_output_aliases", {}), **_exchange_aliases(ride, n_in, n_out)}
    res = _pcall(
        riding, grid=grid, in_specs=list(in_specs) + [hbm] * nr, out_specs=list(out_specs) + [hbm] * nr,
        out_shape=list(out_shape) + [leg.landing() for leg in ride], input_output_aliases=aliases,
        scratch_shapes=list(scratch_shapes) + _exchange_sems(ride), **kw)(*args, *[leg.src for leg in ride])
    return res[:n_out], res[n_out:]


def _seq(n=1):
    return pltpu.CompilerParams(dimension_semantics=("arbitrary",) * n)


def _dot(a, b):
    return jnp.dot(a, b, preferred_element_type=F32)


def _dot_nt(a, b):
    return lax.dot_general(a, b, (((1,), (1,)), ((), ())), preferred_element_type=F32)


def _dot_tn(a, b):
    return lax.dot_general(a, b, (((0,), (0,)), ((), ())), preferred_element_type=F32)


def _bf(x):
    return x.astype(BF16)


def _sigmoid(x):
    return 0.5 * jnp.tanh(0.5 * x) + 0.5


def _log1p(z):
    u = 1.0 + z
    return jnp.where(u == 1.0, z, jnp.log(u) * (z / jnp.where(u == 1.0, 1.0, u - 1.0)))


def _softplus(x):
    return jnp.maximum(x, 0.0) + _log1p(jnp.exp(-jnp.abs(x)))


def _log_sigmoid(x):
    return -_softplus(-x)


def _one_minus_sq(a, log_a):
    x = 2.0 * log_a
    small = -x * (1.0 + x * (0.5 + x * (1.0 / 6.0)))
    return jnp.where(x > -0.004, small, 1.0 - a * a)


def _dsilu(x, s):
    return s * (1.0 + x * (1.0 - s))


def _rowsum(x):
    return jnp.sum(x, axis=1, keepdims=True)


def _colsum(x):
    return jnp.sum(x, axis=0, keepdims=True)


def _shift_down(win, s):
    return win if s == 0 else pltpu.roll(win, s, 0)


def _shift_up(win, s):
    return win if s == 0 else pltpu.roll(win, win.shape[0] - s, 0)


def _conv_taps(win):
    return [_shift_down(win, CONV_WIDTH - 1 - k)[HALO:] for k in range(CONV_WIDTH)]


def _conv_fwd(taps, w_ref, b_ref):
    acc = b_ref[...] + w_ref[CONV_WIDTH - 1:CONV_WIDTH, :] * taps[CONV_WIDTH - 1]
    for k in range(CONV_WIDTH - 1):
        acc = acc + w_ref[k:k + 1, :] * taps[k]
    return acc


def _split3(x):
    hi = _bf(x)
    r1 = x - hi.astype(F32)
    mid = _bf(r1)
    lo = _bf(r1 - mid.astype(F32))
    return hi, mid, lo


def _tri_dot_left(tri, x):
    hi, mid, lo = _split3(x)
    return _dot(tri, hi) + _dot(tri, mid) + _dot(tri, lo)


def _tri_dot_right(x, tri):
    hi, mid, lo = _split3(x)
    return _dot(hi, tri) + _dot(mid, tri) + _dot(lo, tri)


def _tile(n, want):
    t = min(n, want)
    assert n % t == 0
    return t


def _ln_inproj(x, g, scale, shift, w4, ride=None):
    s_len, d = x.shape
    nj, _, nsh = w4.shape
    tm = _tile(s_len, ROWS_MATMUL)
    ni = s_len // tm

    def body(x_ref, g_ref, sc_ref, sh_ref, w_ref, h_ref, u_ref, hs):
        rows = pl.ds(pl.multiple_of(pl.program_id(1) * tm, tm), tm)

        @pl.when(pl.program_id(0) == 0)
        def _():
            xv = x_ref[...]
            r = lax.rsqrt(jnp.mean(xv * xv, axis=-1, keepdims=True) + EPS)
            hv = (xv * r * g_ref[...]) * (1.0 + sc_ref[...]) + sh_ref[...]
            hs[rows, :] = _bf(hv)
            h_ref[...] = hs[rows, :]

        u_ref[...] = _dot(hs[rows, :], w_ref[0])

    vec = pl.BlockSpec((1, d), lambda j, i: (0, 0))
    once = pl.BlockSpec((tm, d), lambda j, i: (jnp.where(j == 0, i, ni - 1), 0))
    return _pcall_ride(
        body, ride, name="ln_inproj", grid=(nj, ni),
        in_specs=[once, vec, vec, vec, pl.BlockSpec((1, d, nsh), lambda j, i: (j, 0, 0))],
        out_specs=[once, pl.BlockSpec((tm, nsh), lambda j, i: (i, j))],
        out_shape=[jax.ShapeDtypeStruct((s_len, d), BF16), jax.ShapeDtypeStruct((s_len, nj * nsh), F32)],
        scratch_shapes=[pltpu.VMEM((s_len, d), BF16)],
        compiler_params=_seq(2),
        args=(x, g, scale, shift, w4))


def _rg_gates(xc, wa_ref, ba_ref, wx_ref, bx_ref, lam_ref):
    heads, hd, _ = wa_ref.shape
    xb = _bf(xc)
    ga = jnp.concatenate([_dot(xb[:, h * hd:(h + 1) * hd], wa_ref[h]) for h in range(heads)], axis=1) + ba_ref[...]
    gx = jnp.concatenate([_dot(xb[:, h * hd:(h + 1) * hd], wx_ref[h]) for h in range(heads)], axis=1) + bx_ref[...]
    r = _sigmoid(ga)
    ig = _sigmoid(gx)
    sp = _softplus(-lam_ref[...])
    log_a = (-RG_C) * r * sp
    a = jnp.exp(log_a)
    mult = jnp.sqrt(_one_minus_sq(a, log_a))
    return r, ig, sp, log_a, a, mult


def _scan_groups(a, u, reverse):
    n, c = a.shape
    a = a.reshape(n // 8, 8, c)
    u = u.reshape(n // 8, 8, c)
    row = lax.broadcasted_iota(jnp.int32, a.shape, 1)
    for k in (1, 2, 4):
        sft = 8 - k if reverse else k
        a_sh, u_sh = pltpu.roll(a, sft, 1), pltpu.roll(u, sft, 1)
        ok = row < 8 - k if reverse else row >= k
        u = jnp.where(ok, a * u_sh + u, u)
        a = jnp.where(ok, a * a_sh, a)
    return a.reshape(n, c), u.reshape(n, c)


def _rg_fwd(u, conv_w, conv_b, wa_b, ba, wx_b, bx, lam, ride=None):
    s_len = u.shape[0]
    d = conv_w.shape[1]
    tm = _tile(s_len, ROWS_VECTOR)
    per = tm // HALO

    def body(x_ref, xp_ref, z_ref, cw_ref, cb_ref, wa_ref, ba_ref, wx_ref, bx_ref, lam_ref,
             hh_ref, y_ref, carry):
        i = pl.program_id(0)

        @pl.when(i == 0)
        def _():
            carry[...] = jnp.zeros_like(carry)

        prev = jnp.where(i == 0, 0.0, xp_ref[...])
        xc = _conv_fwd(_conv_taps(jnp.concatenate([prev, x_ref[...]], axis=0)), cw_ref, cb_ref)
        _, ig, _, _, a, mult = _rg_gates(xc, wa_ref, ba_ref, wx_ref, bx_ref, lam_ref)
        ca, cu = _scan_groups(a, mult * (ig * xc), reverse=False)
        c = carry[0:1, :]
        for j in range(tm // 8):
            blk = ca[j * 8:(j + 1) * 8] * c + cu[j * 8:(j + 1) * 8]
            hh_ref[j * 8:(j + 1) * 8, :] = blk
            c = blk[7:8]
        carry[0:1, :] = c
        z = z_ref[...]
        y_ref[0] = _bf(hh_ref[...] * (z * _sigmoid(z)))

    vec = pl.BlockSpec((1, d), lambda i: (0, 0))
    whole3 = lambda a: pl.BlockSpec(a.shape, lambda i: (0, 0, 0))
    return _pcall_ride(
        body, ride, name="rg_fwd", grid=(s_len // tm,),
        in_specs=[pl.BlockSpec((tm, d), lambda i: (i, 0)),
                  pl.BlockSpec((HALO, d), lambda i: (jnp.maximum(i * per - 1, 0), 0)),
                  pl.BlockSpec((tm, d), lambda i: (i, 1)),
                  pl.BlockSpec((CONV_WIDTH, d), lambda i: (0, 0)), vec,
                  whole3(wa_b), vec, whole3(wx_b), vec, vec],
        out_specs=[pl.BlockSpec((tm, d), lambda i: (i, 0)), pl.BlockSpec((1, tm, d), lambda i: (0, i, 0))],
        out_shape=[jax.ShapeDtypeStruct((s_len, d), F32), jax.ShapeDtypeStruct((2, s_len, d), BF16)],
        scratch_shapes=[pltpu.VMEM((8, d), F32)],
        compiler_params=_seq(),
        args=(u, u, u, conv_w, conv_b, wa_b, ba, wx_b, bx, lam))


def _ml_pre(u, conv_w, conv_b, wqkv_b, wif_b, wift_b, b_if, b_ift):
    s_len = u.shape[0]
    d = conv_w.shape[1]
    _, heads, hd, _ = wqkv_b.shape
    ng = 2 * heads
    tm = _tile(s_len, max(ROWS_VECTOR, ML_CHUNK))
    per = tm // HALO

    def body(x_ref, xp_ref, cw_ref, cb_ref, w_ref, wif_ref, wift_ref, bif_ref, bift_ref,
             qkv_ref, gt_ref, gtt_ref, bc_ref, bct_ref):
        i = pl.program_id(0)
        prev = jnp.where(i == 0, 0.0, xp_ref[...])
        xm = x_ref[...]
        pre = _conv_fwd(_conv_taps(jnp.concatenate([prev, xm], axis=0)), cw_ref, cb_ref)
        xcb = _bf(pre * _sigmoid(pre))
        xmb = _bf(xm)
        for h in range(heads):
            hs = slice(h * hd, (h + 1) * hd)
            qkv_ref[0, :, hs] = _bf(_dot(xcb[:, hs], w_ref[0, h]))
            qkv_ref[1, :, hs] = _bf(_dot(xcb[:, hs], w_ref[1, h]))
            qkv_ref[2, :, hs] = _bf(_dot(xmb[:, hs], w_ref[2, h]))
        qb, kb, vb = qkv_ref[0], qkv_ref[1], qkv_ref[2]
        gt = (_dot(qb, wif_ref[0:d, :]) + _dot(kb, wif_ref[d:2 * d, :]) + _dot(vb, wif_ref[2 * d:3 * d, :])
              + bif_ref[...])
        gtt = (_dot_nt(wift_ref[:, 0:d], qb) + _dot_nt(wift_ref[:, d:2 * d], kb)
               + _dot_nt(wift_ref[:, 2 * d:3 * d], vb) + bift_ref[...])
        gt_ref[...] = gt
        gtt_ref[...] = gtt
        r = lax.broadcasted_iota(jnp.int32, (tm, tm), 0)
        c = lax.broadcasted_iota(jnp.int32, (tm, tm), 1)
        same = (r // ML_CHUNK) == (c // ML_CHUNK)
        bc_ref[...] = _tri_dot_left(((r >= c) & same).astype(BF16), _log_sigmoid(gt))
        bct_ref[...] = _tri_dot_right(_log_sigmoid(gtt), ((r <= c) & same).astype(BF16))

    vec = pl.BlockSpec((1, d), lambda i: (0, 0))
    whole2 = lambda a: pl.BlockSpec(a.shape, lambda i: (0, 0))
    col = pl.BlockSpec((tm, ng), lambda i: (i, 0))
    row = pl.BlockSpec((ng, tm), lambda i: (0, i))
    return _pcall(
        body, name="ml_pre", grid=(s_len // tm,),
        in_specs=[pl.BlockSpec((tm, d), lambda i: (i, 2)),
                  pl.BlockSpec((HALO, d), lambda i: (jnp.maximum(i * per - 1, 0), 2)),
                  pl.BlockSpec((CONV_WIDTH, d), lambda i: (0, 0)), vec,
                  pl.BlockSpec(wqkv_b.shape, lambda i: (0, 0, 0, 0)), whole2(wif_b), whole2(wift_b), whole2(b_if),
                  whole2(b_ift)],
        out_specs=[pl.BlockSpec((3, tm, d), lambda i: (0, i, 0)), col, row, col, row],
        out_shape=[jax.ShapeDtypeStruct((3, s_len, d), BF16), jax.ShapeDtypeStruct((s_len, ng), F32),
                   jax.ShapeDtypeStruct((ng, s_len), F32), jax.ShapeDtypeStruct((s_len, ng), F32),
                   jax.ShapeDtypeStruct((ng, s_len), F32)],
        compiler_params=_seq(),
    )(u, u, conv_w, conv_b, wqkv_b, wif_b, wift_b, b_if, b_ift)


def _chunk_gates(gt, gtt, bc, bct, h, heads):
    li_c = gt[:, h:h + 1]
    li_r = gtt[h:h + 1, :]
    gf_c = gt[:, heads + h:heads + h + 1]
    b_c = bc[:, heads + h:heads + h + 1]
    b_r = bct[heads + h:heads + h + 1, :]
    return li_c, li_r, gf_c, b_c, b_r


def _chunk_weights(li_c, li_r, b_c, b_r, m_prev, causal):
    lc = b_c.shape[0]
    b_last = b_c[lc - 1:lc, :]
    dmat = jnp.where(causal, b_c - b_r + li_r, -jnp.inf)
    m_inter = b_c + m_prev
    m_t = jnp.maximum(m_inter, jnp.max(dmat, axis=1, keepdims=True))
    w_intra = jnp.exp(dmat - m_t)
    w_inter = jnp.exp(m_inter - m_t)
    g_c = b_last - b_c + li_c
    m_new = jnp.maximum(b_last + m_prev, jnp.max(g_c, axis=0, keepdims=True))
    w_state = jnp.exp(g_c - m_new)
    decay = jnp.exp(b_last + m_prev - m_new)
    return m_t, w_intra, w_inter, m_new, w_state, decay


def _tri_masks(lc):
    r = lax.broadcasted_iota(jnp.int32, (lc, lc), 0)
    c = lax.broadcasted_iota(jnp.int32, (lc, lc), 1)
    causal = r >= c
    return causal, causal.astype(BF16), (r <= c).astype(BF16)


def _mlstm_fwd(qkv, gates, u, ml_g, ycat, ride=None):
    _, s_len, d = qkv.shape
    ng = gates[0].shape[1]
    heads = ng // 2
    hd = d // heads
    lc = ML_CHUNK
    nc = s_len // lc
    kscale = hd ** -0.5

    def body(qkv_ref, gt_ref, gtt_ref, bc_ref, bct_ref, o_ref, z_ref, g_ref, _, cell_ref, y_ref, cst_ref, nst_ref,
             mst_ref, cs, ns, ms):
        @pl.when(pl.program_id(0) == 0)
        def _():
            cs[...] = jnp.zeros_like(cs)
            ns[...] = jnp.zeros_like(ns)
            ms[...] = jnp.zeros_like(ms)

        causal = _tri_masks(lc)[0]
        gtv, gttv, bcv, bctv = gt_ref[...], gtt_ref[...], bc_ref[...], bct_ref[...]
        old = [(cs[h], ns[h], ms[h]) for h in range(heads)]
        new, cells, ys = [], [], []
        for h in range(heads):
            hs = slice(h * hd, (h + 1) * hd)
            li_c, li_r, _, b_c, b_r = _chunk_gates(gtv, gttv, bcv, bctv, h, heads)
            c_old, n_old, m_old = old[h]
            m_prev = m_old[:, 0:1]
            m_t, w_intra, w_inter, m_new, w_state, decay = _chunk_weights(li_c, li_r, b_c, b_r, m_prev, causal)
            qb = qkv_ref[0, :, hs]
            ks = qkv_ref[1, :, hs].astype(F32) * kscale
            kb = _bf(ks)
            vb = qkv_ref[2, :, hs]
            s = _dot_nt(qb, kb) * w_intra
            num = _dot(_bf(s), vb) + w_inter * _dot(qb, _bf(c_old))
            den = _rowsum(s) + w_inter * _rowsum(qb.astype(F32) * n_old)
            cell = num / jnp.maximum(jnp.abs(den), jnp.exp(-m_t))
            kw = ks * w_state
            new.append((decay * c_old + _dot_tn(_bf(kw), vb), decay * n_old + _colsum(kw),
                        jnp.broadcast_to(m_new, m_old.shape)))
            cells.append(cell)
            hm = _sigmoid(o_ref[:, hs]) * cell
            hn = hm * lax.rsqrt(jnp.mean(hm * hm, axis=-1, keepdims=True) + EPS)
            z = z_ref[:, hs]
            ys.append(_bf((hn * g_ref[:, hs]) * (z * _sigmoid(z))))
        for h in range(heads):
            cst_ref[0, h] = _bf(old[h][0])
            nst_ref[0, h] = old[h][1]
            mst_ref[0, h] = old[h][2]
            cs[h], ns[h], ms[h] = new[h]
        cell_ref[...] = jnp.concatenate(cells, axis=1)
        y_ref[0] = jnp.concatenate(ys, axis=1)

    row = pl.BlockSpec((lc, d), lambda c: (c, 0))
    gcol = pl.BlockSpec((lc, ng), lambda c: (c, 0))
    grow = pl.BlockSpec((ng, lc), lambda c: (0, c))
    return _pcall_ride(
        body, ride, name="mlstm_fwd", grid=(nc,),
        in_specs=[pl.BlockSpec((3, lc, d), lambda c: (0, c, 0)), gcol, grow, gcol, grow,
                  pl.BlockSpec((lc, d), lambda c: (c, 3)), pl.BlockSpec((lc, d), lambda c: (c, 4)),
                  pl.BlockSpec((1, d), lambda c: (0, 0)), pl.BlockSpec(memory_space=pl.ANY)],
        out_specs=[row, pl.BlockSpec((1, lc, d), lambda c: (1, c, 0)),
                   pl.BlockSpec((1, heads, hd, hd), lambda c: (c, 0, 0, 0)),
                   pl.BlockSpec((1, heads, 1, hd), lambda c: (c, 0, 0, 0)),
                   pl.BlockSpec((1, heads, 1, 128), lambda c: (c, 0, 0, 0))],
        out_shape=[jax.ShapeDtypeStruct((s_len, d), F32), jax.ShapeDtypeStruct(ycat.shape, BF16),
                   jax.ShapeDtypeStruct((nc, heads, hd, hd), BF16),
                   jax.ShapeDtypeStruct((nc, heads, 1, hd), F32),
                   jax.ShapeDtypeStruct((nc, heads, 1, 128), F32)],
        scratch_shapes=[pltpu.VMEM((heads, hd, hd), F32), pltpu.VMEM((heads, 1, hd), F32),
                        pltpu.VMEM((heads, 1, 128), F32)],
        input_output_aliases={8: 1},
        compiler_params=_seq(),
        args=(qkv, *gates, u, u, ml_g, ycat))


def _out_proj(ycat, w_out_b, x, gate, ride=None):
    s_len, d = x.shape
    tm = _tile(s_len, ROWS_MATMUL)

    def body(a_ref, w_ref, x_ref, g_ref, y_ref, xn_ref):
        y = _dot(a_ref[0], w_ref[0:d, :]) + _dot(a_ref[1], w_ref[d:2 * d, :])
        y_ref[...] = y
        xn_ref[...] = x_ref[...] + g_ref[...] * y

    row = pl.BlockSpec((tm, d), lambda i: (i, 0))
    return _pcall_ride(
        body, ride, name="out_proj", grid=(s_len // tm,),
        in_specs=[pl.BlockSpec((2, tm, d), lambda i: (0, i, 0)), pl.BlockSpec((2 * d, d), lambda i: (0, 0)), row,
                  pl.BlockSpec((1, d), lambda i: (0, 0))],
        out_specs=[row, row],
        out_shape=[jax.ShapeDtypeStruct((s_len, d), F32)] * 2,
        compiler_params=_seq(),
        args=(ycat, w_out_b, x, gate))


def _out_proj_loss(ycat, w_out_b, x, gate, g, target):
    s_len, d = x.shape
    tm = _tile(s_len, ROWS_IN_BWD)

    def body(a_ref, w_ref, x_ref, gate_ref, g_ref, t_ref, y_ref, dx_ref, dg_ref, loss_ref):
        @pl.when(pl.program_id(0) == 0)
        def _():
            dg_ref[...] = jnp.zeros_like(dg_ref)
            loss_ref[...] = jnp.zeros_like(loss_ref)

        y = _dot(a_ref[0], w_ref[0:d, :]) + _dot(a_ref[1], w_ref[d:2 * d, :])
        y_ref[...] = y
        xv = x_ref[...] + gate_ref[...] * y
        r = lax.rsqrt(jnp.mean(xv * xv, axis=-1, keepdims=True) + EPS)
        xn = xv * r
        err = xn * g_ref[...] - t_ref[...]
        loss_ref[...] += 0.5 * jnp.sum(jnp.mean(err * err, axis=-1, keepdims=True))
        dout = err * (1.0 / d)
        dg_ref[...] += _colsum(dout * xn)
        dxn = dout * g_ref[...]
        dx_ref[...] = r * (dxn - xn * jnp.mean(dxn * xn, axis=-1, keepdims=True))

    row = pl.BlockSpec((tm, d), lambda i: (i, 0))
    vec = pl.BlockSpec((1, d), lambda i: (0, 0))
    return _pcall(
        body, name="out_proj_loss", grid=(s_len // tm,),
        in_specs=[pl.BlockSpec((2, tm, d), lambda i: (0, i, 0)), pl.BlockSpec((2 * d, d), lambda i: (0, 0)), row, vec,
                  vec, row],
        out_specs=[row, row, vec, pl.BlockSpec((1, 128), lambda i: (0, 0))],
        out_shape=[jax.ShapeDtypeStruct((s_len, d), F32), jax.ShapeDtypeStruct((s_len, d), F32),
                   jax.ShapeDtypeStruct((1, d), F32), jax.ShapeDtypeStruct((1, 128), F32)],
        compiler_params=_seq(),
    )(ycat, w_out_b, x, gate, g, target)


def _out_bwd(dxn, y, gate, w_out_b):
    s_len, d = dxn.shape
    tm = _tile(s_len, ROWS_MATMUL)

    def body(dx_ref, y_ref, g_ref, w_ref, dg_ref, dy_ref, dc_ref):
        @pl.when(pl.program_id(0) == 0)
        def _():
            dg_ref[...] = jnp.zeros_like(dg_ref)

        dx = dx_ref[...]
        dg_ref[...] += _colsum(dx * y_ref[...])
        dy = _bf(g_ref[...] * dx)
        dy_ref[...] = dy
        dc_ref[0] = _dot_nt(dy, w_ref[0:d, :])
        dc_ref[1] = _dot_nt(dy, w_ref[d:2 * d, :])

    row = pl.BlockSpec((tm, d), lambda i: (i, 0))
    vec = pl.BlockSpec((1, d), lambda i: (0, 0))
    return _pcall(
        body, name="out_bwd", grid=(s_len // tm,),
        in_specs=[row, row, vec, pl.BlockSpec((2 * d, d), lambda i: (0, 0))],
        out_specs=[vec, row, pl.BlockSpec((2, tm, d), lambda i: (0, i, 0))],
        out_shape=[jax.ShapeDtypeStruct((1, d), F32), jax.ShapeDtypeStruct((s_len, d), BF16),
                   jax.ShapeDtypeStruct((2, s_len, d), F32)],
        compiler_params=_seq(),
    )(dxn, y, gate, w_out_b)


def _grad_matmul(a3, b3, nblk, a_idx, b_idx, out_shape, out_block, out_idx, ride=None):
    _, s_len, m = a3.shape
    n = b3.shape[2]
    tk = _tile(s_len, ROWS_GRAD_MATMUL)

    def body(a_ref, b_ref, o_ref):
        @pl.when(pl.program_id(1) == 0)
        def _():
            o_ref[...] = jnp.zeros_like(o_ref)

        o_ref[...] += _dot_tn(a_ref[0], b_ref[0])

    (out,), got = _pcall_ride(
        body, ride, name="grad_matmul", grid=(nblk, s_len // tk),
        in_specs=[pl.BlockSpec((1, tk, m), lambda p, t: (a_idx(p), t, 0)),
                  pl.BlockSpec((1, tk, n), lambda p, t: (b_idx(p), t, 0))],
        out_specs=[pl.BlockSpec((None,) + out_block, lambda p, t: (0,) + out_idx(p))],
        out_shape=[jax.ShapeDtypeStruct((1,) + out_shape, F32)],
        compiler_params=_seq(2), args=(a3, b3))
    return out, got


DU_PLANE = (2, 3, 4, 0, 1)


def _mlstm_bwd(qkv, gates, cst, nst, mst, cell, u, ml_g, d_ycat, wif_b, ride=None):
    _, s_len, d = qkv.shape
    ng = gates[0].shape[1]
    heads = ng // 2
    hd = d // heads
    lc = ML_CHUNK
    nc = s_len // lc
    kscale = hd ** -0.5

    def body(qkv_ref, gt_ref, gtt_ref, bc_ref, bct_ref, cst_ref, nst_ref, mst_ref, cell_ref, o_ref, z_ref, g_ref, dy_ref,
             wif_ref, dqkv_ref, dgt_ref, dbif_ref, du_ref, dg_ref, dcs, dns):
        @pl.when(pl.program_id(0) == 0)
        def _():
            dbif_ref[...] = jnp.zeros_like(dbif_ref)
            dcs[...] = jnp.zeros_like(dcs)
            dns[...] = jnp.zeros_like(dns)
            dg_ref[...] = jnp.zeros_like(dg_ref)

        causal, tril, triu = _tri_masks(lc)
        tril_strict = (tril.astype(F32) - (tril * triu).astype(F32)).astype(BF16)
        gtv, gttv, bcv, bctv = gt_ref[...], gtt_ref[...], bc_ref[...], bct_ref[...]
        lane = lax.broadcasted_iota(jnp.int32, (lc, ng), 1)
        dli_all = jnp.zeros((lc, ng), F32)
        from_later = jnp.zeros((lc, ng), F32)
        from_earlier = jnp.zeros((lc, ng), F32)
        across_all = jnp.zeros((1, ng), F32)
        old = [(dcs[h], dns[h]) for h in range(heads)]
        new, d_o, d_z, d_g, dqs, dks, dvs = [], [], [], [], [], [], []
        for h in range(heads):
            hs = slice(h * hd, (h + 1) * hd)
            li_c, li_r, gf_c, b_c, b_r = _chunk_gates(gtv, gttv, bcv, bctv, h, heads)
            m_prev = mst_ref[0, h][:, 0:1]
            m_t, w_intra, w_inter, _, w_state, decay = _chunk_weights(li_c, li_r, b_c, b_r, m_prev, causal)
            qb = qkv_ref[0, :, hs]
            qf = qb.astype(F32)
            ks = qkv_ref[1, :, hs].astype(F32) * kscale
            kb = _bf(ks)
            vb = qkv_ref[2, :, hs]
            c_b = cst_ref[0, h]
            n_old = nst_ref[0, h]
            s = _dot_nt(qb, kb) * w_intra
            den = _rowsum(s) + w_inter * _rowsum(qf * n_old)
            floor = jnp.exp(-m_t)
            dstab = jnp.maximum(jnp.abs(den), floor)
            cell = cell_ref[:, hs]
            o = o_ref[:, hs]
            so = _sigmoid(o)
            hm = so * cell
            rinv = lax.rsqrt(jnp.mean(hm * hm, axis=-1, keepdims=True) + EPS)
            hn = hm * rinv
            z = z_ref[:, hs]
            sgz = _sigmoid(z)
            sz = z * sgz
            gh = g_ref[:, hs]
            dy = dy_ref[0, :, hs]
            d_z.append(_bf(dy * (hn * gh) * _dsilu(z, sgz)))
            d_g.append(_colsum(dy * hn * sz))
            dhn = dy * gh * sz
            dhm = rinv * (dhn - hn * jnp.mean(dhn * hn, axis=-1, keepdims=True))
            d_o.append(_bf(dhm * cell * so * (1.0 - so)))
            dcell = dhm * so
            dnum = dcell / dstab
            dnb = _bf(dnum)
            dden = -_rowsum(dcell * cell) / dstab * jnp.where(jnp.abs(den) > floor, jnp.where(den > 0.0, 1.0, -1.0), 0.0)
            dst = _dot_nt(dnb, vb) + dden
            dsdb = _bf(dst * w_intra)
            dc_out, dn_out = old[h]
            dcb = _bf(dc_out)
            dq_inter = w_inter * (_dot_nt(dnb, c_b) + dden * n_old)
            dk_inter = w_state * (_dot_nt(vb, dcb) + dn_out)
            dq = _dot(dsdb, kb) + dq_inter
            dk = _dot_tn(dsdb, qb) + dk_inter
            dv = _dot_tn(_bf(s), dnb) + _dot(_bf(ks * w_state), dcb)
            wq = w_inter * qf
            new.append((decay * dc_out + _dot_tn(_bf(wq), dnb), decay * dn_out + _colsum(wq * dden)))
            pmat = dst * s
            p_rows = _rowsum(pmat)
            p_cols = _rowsum(pmat.T)
            q_in = _rowsum(qf * dq_inter)
            k_in = _rowsum(ks * dk_inter)
            across = decay * (jnp.sum(dc_out * c_b.astype(F32), keepdims=True) + jnp.sum(dn_out * n_old, keepdims=True))
            dli_all = dli_all + jnp.where(lane == h, p_cols + k_in, 0.0)
            from_later = from_later + jnp.where(lane == heads + h, p_rows - p_cols + q_in, 0.0)
            from_earlier = from_earlier + jnp.where(lane == heads + h, k_in, 0.0)
            across_all = across_all + jnp.where(lane[0:1] == heads + h, across, 0.0)
            dqs.append(dq)
            dks.append(dk * kscale)
            dvs.append(dv)
        for h in range(heads):
            dcs[h], dns[h] = new[h]
        du_ref[0] = jnp.concatenate(d_o, axis=1)
        du_ref[1] = jnp.concatenate(d_z, axis=1)
        dg_ref[...] += jnp.concatenate(d_g, axis=1)
        dlf = _tri_dot_left(triu, from_later) + _tri_dot_left(tril_strict, from_earlier) + across_all
        dgt = dli_all + dlf * _sigmoid(-gtv)
        dgt_ref[...] = dgt
        dbif_ref[...] += _colsum(dgt)
        dgb = _bf(dgt)
        dqkv_ref[0] = _bf(jnp.concatenate(dqs, axis=1) + _dot_nt(dgb, wif_ref[0:d, :]))
        dqkv_ref[1] = _bf(jnp.concatenate(dks, axis=1) + _dot_nt(dgb, wif_ref[d:2 * d, :]))
        dqkv_ref[2] = _bf(jnp.concatenate(dvs, axis=1) + _dot_nt(dgb, wif_ref[2 * d:3 * d, :]))

    rev = lambda c: nc - 1 - c
    row = pl.BlockSpec((lc, d), lambda c: (rev(c), 0))
    gcol = pl.BlockSpec((lc, ng), lambda c: (rev(c), 0))
    grow = pl.BlockSpec((ng, lc), lambda c: (0, rev(c)))
    return _pcall_ride(
        body, ride, name="mlstm_bwd", grid=(nc,),
        in_specs=[pl.BlockSpec((3, lc, d), lambda c: (0, rev(c), 0)), gcol, grow, gcol, grow,
                  pl.BlockSpec((1, heads, hd, hd), lambda c: (rev(c), 0, 0, 0)),
                  pl.BlockSpec((1, heads, 1, hd), lambda c: (rev(c), 0, 0, 0)),
                  pl.BlockSpec((1, heads, 1, 128), lambda c: (rev(c), 0, 0, 0)),
                  row, pl.BlockSpec((lc, d), lambda c: (rev(c), 3)), pl.BlockSpec((lc, d), lambda c: (rev(c), 4)),
                  pl.BlockSpec((1, d), lambda c: (0, 0)), pl.BlockSpec((1, lc, d), lambda c: (1, rev(c), 0)),
                  pl.BlockSpec((3 * d, ng), lambda c: (0, 0))],
        out_specs=[pl.BlockSpec((3, lc, d), lambda c: (0, rev(c), 0)), pl.BlockSpec((lc, ng), lambda c: (rev(c), 0)),
                   pl.BlockSpec((1, ng), lambda c: (0, 0)), pl.BlockSpec((2, lc, d), lambda c: (0, rev(c), 0)),
                   pl.BlockSpec((1, d), lambda c: (0, 0))],
        out_shape=[jax.ShapeDtypeStruct((3, s_len, d), BF16), jax.ShapeDtypeStruct((s_len, ng), F32),
                   jax.ShapeDtypeStruct((1, ng), F32), jax.ShapeDtypeStruct((5, s_len, d), BF16),
                   jax.ShapeDtypeStruct((1, d), F32)],
        scratch_shapes=[pltpu.VMEM((heads, hd, hd), F32), pltpu.VMEM((heads, 1, hd), F32)],
        compiler_params=_seq(),
        args=(qkv, *gates, cst, nst, mst, cell, u, u, ml_g, d_ycat, wif_b))


def _conv_bwd_tile(dp, later, taps, cw_ref, gw_ref, gb_ref):
    tm = dp.shape[0]
    dwin = jnp.concatenate([dp, later[...]], axis=0)
    later[...] = dp[0:HALO]
    acc = cw_ref[CONV_WIDTH - 1:CONV_WIDTH, :] * dp
    for k in range(CONV_WIDTH):
        if k < CONV_WIDTH - 1:
            acc = acc + cw_ref[k:k + 1, :] * _shift_up(dwin, CONV_WIDTH - 1 - k)[0:tm]
        gw_ref[k:k + 1, :] += _colsum(dp * taps[k])
    gb_ref[...] += _colsum(dp)
    return acc


def _ml_pre_bwd(dqkv, u, conv_w, conv_b, wqkv_b, du):
    s_len = u.shape[0]
    d = conv_w.shape[1]
    _, heads, hd, _ = wqkv_b.shape
    tm = _tile(s_len, ROWS_VECTOR)
    per = tm // HALO
    nt = s_len // tm

    def body(dqkv_ref, x_ref, xp_ref, cw_ref, cb_ref, w_ref, _, dx_ref, gw_ref, gcw_ref, gcb_ref, later, dps, dxs):
        i = pl.program_id(0)

        @pl.when(i == 0)
        def _():
            gw_ref[...] = jnp.zeros_like(gw_ref)
            gcw_ref[...] = jnp.zeros_like(gcw_ref)
            gcb_ref[...] = jnp.zeros_like(gcb_ref)
            later[...] = jnp.zeros_like(later)

        prev = jnp.where(i == nt - 1, 0.0, xp_ref[...])
        xm = x_ref[...]
        taps = _conv_taps(jnp.concatenate([prev, xm], axis=0))
        pre = _conv_fwd(taps, cw_ref, cb_ref)
        sg = _sigmoid(pre)
        xcb = _bf(pre * sg)
        xmb = _bf(xm)
        for h in range(heads):
            hs = slice(h * hd, (h + 1) * hd)
            dqh, dkh, dvh = dqkv_ref[0, :, hs], dqkv_ref[1, :, hs], dqkv_ref[2, :, hs]
            dxc = _dot_nt(dqh, w_ref[0, h]) + _dot_nt(dkh, w_ref[1, h])
            dps[:, hs] = dxc * _dsilu(pre[:, hs], sg[:, hs])
            dxs[:, hs] = _dot_nt(dvh, w_ref[2, h])
            gw_ref[0, h] += _dot_tn(xcb[:, hs], dqh)
            gw_ref[1, h] += _dot_tn(xcb[:, hs], dkh)
            gw_ref[2, h] += _dot_tn(xmb[:, hs], dvh)
        dx_ref[0] = _bf(_conv_bwd_tile(dps[...], later, taps, cw_ref, gcw_ref, gcb_ref) + dxs[...])

    rev = lambda i: nt - 1 - i
    vec = pl.BlockSpec((1, d), lambda i: (0, 0))
    cwb = pl.BlockSpec((CONV_WIDTH, d), lambda i: (0, 0))
    whole4 = pl.BlockSpec(wqkv_b.shape, lambda i: (0, 0, 0, 0))
    return _pcall(
        body, name="ml_pre_bwd", grid=(nt,),
        in_specs=[pl.BlockSpec((3, tm, d), lambda i: (0, rev(i), 0)), pl.BlockSpec((tm, d), lambda i: (rev(i), 2)),
                  pl.BlockSpec((HALO, d), lambda i: (jnp.maximum(rev(i) * per - 1, 0), 2)),
                  cwb, vec, whole4, pl.BlockSpec(memory_space=pl.ANY)],
        out_specs=[pl.BlockSpec((1, tm, d), lambda i: (DU_PLANE[2], rev(i), 0)), whole4, cwb, vec],
        out_shape=[jax.ShapeDtypeStruct(du.shape, BF16), jax.ShapeDtypeStruct(wqkv_b.shape, F32),
                   jax.ShapeDtypeStruct((CONV_WIDTH, d), F32), jax.ShapeDtypeStruct((1, d), F32)],
        scratch_shapes=[pltpu.VMEM((HALO, d), F32), pltpu.VMEM((tm, d), F32), pltpu.VMEM((tm, d), F32)],
        input_output_aliases={6: 0},
        compiler_params=_seq(),
    )(dqkv, u, u, conv_w, conv_b, wqkv_b, du)


def _rg_bwd(d_ycat, u, hh, conv_w, conv_b, wa_b, ba, wx_b, bx, lam, du):
    s_len = u.shape[0]
    d = conv_w.shape[1]
    heads, hd, _ = wa_b.shape
    tm = _tile(s_len, ROWS_VECTOR)
    per = tm // HALO
    nt = s_len // tm

    def body(dy_ref, x_ref, xp_ref, z_ref, hh_ref, hp_ref, cw_ref, cb_ref, wa_ref, ba_ref, wx_ref, bx_ref, lam_ref, _,
             du_ref, gwa_ref, gwx_ref, gba_ref, gbx_ref, glam_ref, gcw_ref, gcb_ref, carry, gbuf, later, dxcs):
        i = pl.program_id(0)
        first = i == nt - 1

        @pl.when(i == 0)
        def _():
            carry[...] = jnp.zeros_like(carry)
            later[...] = jnp.zeros_like(later)
            gwa_ref[...] = jnp.zeros_like(gwa_ref)
            gwx_ref[...] = jnp.zeros_like(gwx_ref)
            gba_ref[...] = jnp.zeros_like(gba_ref)
            gbx_ref[...] = jnp.zeros_like(gbx_ref)
            glam_ref[...] = jnp.zeros_like(glam_ref)
            gcw_ref[...] = jnp.zeros_like(gcw_ref)
            gcb_ref[...] = jnp.zeros_like(gcb_ref)

        prev = jnp.where(first, 0.0, xp_ref[...])
        taps = _conv_taps(jnp.concatenate([prev, x_ref[...]], axis=0))
        xc = _conv_fwd(taps, cw_ref, cb_ref)
        r, ig, sp, log_a, a, mult = _rg_gates(xc, wa_ref, ba_ref, wx_ref, bx_ref, lam_ref)
        z = z_ref[...]
        sgz = _sigmoid(z)
        dy = dy_ref[0]
        hh_v = hh_ref[...]
        du_ref[1] = _bf(dy * hh_v * _dsilu(z, sgz))
        dhh = dy * (z * sgz)
        rows = lax.broadcasted_iota(jnp.int32, a.shape, 0)
        coef = jnp.where(rows == tm - 1, carry[1:2, :], _shift_up(a, 1))
        ca, cu = _scan_groups(coef, dhh, reverse=True)
        c = carry[0:1, :]
        for j in range(tm // 8 - 1, -1, -1):
            blk = ca[j * 8:(j + 1) * 8] * c + cu[j * 8:(j + 1) * 8]
            gbuf[j * 8:(j + 1) * 8, :] = blk
            c = blk[0:1]
        carry[0:1, :] = c
        carry[1:2, :] = a[0:1]
        g = gbuf[...]
        hprev_tile = jnp.where(first, 0.0, hp_ref[...])
        hprev = _shift_down(jnp.concatenate([hprev_tile, hh_v], axis=0), 1)[HALO:]
        da = g * hprev
        gx_ = g * xc
        d_mult = gx_ * ig
        d_ig = gx_ * mult
        dxc = g * mult * ig
        dlog_a = da * a - d_mult * (a * a / mult)
        d_r = dlog_a * ((-RG_C) * sp)
        glam_ref[...] += _colsum(dlog_a * ((-RG_C) * r)) * (-_sigmoid(-lam_ref[...]))
        d_ga = d_r * r * (1.0 - r)
        d_gx = d_ig * ig * (1.0 - ig)
        gba_ref[...] += _colsum(d_ga)
        gbx_ref[...] += _colsum(d_gx)
        xb = _bf(xc)
        dgab = _bf(d_ga)
        dgxb = _bf(d_gx)
        for h in range(heads):
            hs = slice(h * hd, (h + 1) * hd)
            dxcs[:, hs] = dxc[:, hs] + _dot_nt(dgab[:, hs], wa_ref[h]) + _dot_nt(dgxb[:, hs], wx_ref[h])
            gwa_ref[h] += _dot_tn(xb[:, hs], dgab[:, hs])
            gwx_ref[h] += _dot_tn(xb[:, hs], dgxb[:, hs])
        du_ref[0] = _bf(_conv_bwd_tile(dxcs[...], later, taps, cw_ref, gcw_ref, gcb_ref))

    assert DU_PLANE[0] % 2 == 0 and DU_PLANE[1] == DU_PLANE[0] + 1
    rev = lambda i: nt - 1 - i
    row = pl.BlockSpec((tm, d), lambda i: (rev(i), 0))
    halo_prev = lambda col: pl.BlockSpec((HALO, d), lambda i: (jnp.maximum(rev(i) * per - 1, 0), col))
    vec = pl.BlockSpec((1, d), lambda i: (0, 0))
    cwb = pl.BlockSpec((CONV_WIDTH, d), lambda i: (0, 0))
    whole3 = lambda a: pl.BlockSpec(a.shape, lambda i: (0, 0, 0))
    return _pcall(
        body, name="rg_bwd", grid=(nt,),
        in_specs=[pl.BlockSpec((1, tm, d), lambda i: (0, rev(i), 0)), row, halo_prev(0),
                  pl.BlockSpec((tm, d), lambda i: (rev(i), 1)), row, halo_prev(0),
                  cwb, vec, whole3(wa_b), vec, whole3(wx_b), vec, vec, pl.BlockSpec(memory_space=pl.ANY)],
        out_specs=[pl.BlockSpec((2, tm, d), lambda i: (DU_PLANE[0] // 2, rev(i), 0)), whole3(wa_b), whole3(wa_b),
                   vec, vec, vec, cwb, vec],
        out_shape=[jax.ShapeDtypeStruct(du.shape, BF16), jax.ShapeDtypeStruct(wa_b.shape, F32),
                   jax.ShapeDtypeStruct(wa_b.shape, F32)] + [jax.ShapeDtypeStruct((1, d), F32)] * 3
        + [jax.ShapeDtypeStruct((CONV_WIDTH, d), F32), jax.ShapeDtypeStruct((1, d), F32)],
        scratch_shapes=[pltpu.VMEM((8, d), F32), pltpu.VMEM((tm, d), F32), pltpu.VMEM((HALO, d), F32),
                        pltpu.VMEM((tm, d), F32)],
        input_output_aliases={13: 0},
        compiler_params=_seq(),
    )(d_ycat, u, u, u, hh, hh, conv_w, conv_b, wa_b, ba, wx_b, bx, lam, du)


def _in_bwd(du, w4, x, dxn, g, scale, ride=None):
    s_len, d = x.shape
    tm = _tile(s_len, ROWS_IN_BWD)
    nsh_chips, _, nsh = w4.shape
    npc = du.shape[0]
    ck = d // 4
    assert nsh % ck == 0 and npc * d == nsh_chips * nsh

    def body(du_ref, w_ref, x_ref, dxn_ref, g_ref, sc_ref, dx_ref, dsh_ref, dsc_ref, dg_ref):
        @pl.when(pl.program_id(0) == 0)
        def _():
            dsh_ref[...] = jnp.zeros_like(dsh_ref)
            dsc_ref[...] = jnp.zeros_like(dsc_ref)
            dg_ref[...] = jnp.zeros_like(dg_ref)

        dh = None
        for q in range(npc * d // ck):
            col = q * ck
            p, pc = col // d, col % d
            s, sc = col // nsh, col % nsh
            t = _dot_nt(du_ref[DU_PLANE[p], :, pc:pc + ck], w_ref[s, :, sc:sc + ck])
            dh = t if dh is None else dh + t
        xv = x_ref[...]
        r = lax.rsqrt(jnp.mean(xv * xv, axis=-1, keepdims=True) + EPS)
        xn = xv * r
        gv = g_ref[...]
        onesc = 1.0 + sc_ref[...]
        dsh_ref[...] += _colsum(dh)
        dsc_ref[...] += _colsum(dh * (xn * gv))
        dg_ref[...] += _colsum(dh * xn * onesc)
        dxh = dh * (gv * onesc)
        dx_ref[...] = dxn_ref[...] + r * (dxh - xn * jnp.mean(dxh * xn, axis=-1, keepdims=True))

    row = pl.BlockSpec((tm, d), lambda i: (i, 0))
    vec = pl.BlockSpec((1, d), lambda i: (0, 0))
    return _pcall_ride(
        body, ride, name="in_bwd", grid=(s_len // tm,),
        in_specs=[pl.BlockSpec((npc, tm, d), lambda i: (0, i, 0)), pl.BlockSpec(w4.shape, lambda i: (0, 0, 0)), row, row,
                  vec, vec],
        out_specs=[row, vec, vec, vec],
        out_shape=[jax.ShapeDtypeStruct((s_len, d), F32)] + [jax.ShapeDtypeStruct((1, d), F32)] * 3,
        compiler_params=_seq(),
        args=(du, w4, x, dxn, g, scale))


def _layer_fwd(x, p, rides=None, loss_head=None):
    rides = rides or {}
    landed = {}
    ride = lambda kernel: rides[kernel](landed) if kernel in rides else None
    (h_b, u), landed["ln_inproj"] = _ln_inproj(x, p["norm_g"], p["scale"], p["shift"], p["w4"], ride("ln_inproj"))
    (hh, ycat), landed["rg_fwd"] = _rg_fwd(u, p["rg_conv_w"], p["rg_conv_b"], p["rg_wa_b"], p["rg_ba"], p["rg_wx_b"],
                                           p["rg_bx"], p["rg_lam"], ride("rg_fwd"))
    if "late" in rides:
        p = {**p, **rides["late"](landed)}
    qkv, *gates = _ml_pre(u, p["ml_conv_w"], p["ml_conv_b"], p["wqkv_b"], p["wif_b"], p["wift_b"], p["b_if"],
                          p["b_ift"])
    (cell, ycat, cst, nst, mst), landed["mlstm_fwd"] = _mlstm_fwd(qkv, gates, u, p["ml_g"], ycat, ride("mlstm_fwd"))
    if loss_head is None:
        (y, x_new), landed["out_proj"] = _out_proj(ycat, p["w_out_b"], x, p["gate"], ride("out_proj"))
    else:
        y, *x_new = _out_proj_loss(ycat, p["w_out_b"], x, p["gate"], *loss_head)
    saved = dict(x=x, h_b=h_b, u=u, hh=hh, qkv=qkv, gates=gates, cell=cell, ycat=ycat, cst=cst, nst=nst, mst=mst, y=y)
    return x_new, saved, p, landed


def _layer_bwd(dxn, p, s, rides=None):
    rides = rides or {}
    landed = {}
    ride = lambda kernel: rides[kernel](grads, landed) if kernel in rides else None
    u = s["u"]
    d = dxn.shape[1]
    d_gate, dy_b, d_ycat = _out_bwd(dxn, s["y"], p["gate"], p["w_out_b"])
    grads = dict(w_out=_grad_matmul(s["ycat"], dy_b[None], 2, lambda b: b, lambda b: 0, (2 * d, d), (d, d),
                                    lambda b: (b, 0))[0])
    (dqkv, dgt, g_b_if, du, g_ml_g), landed["mlstm_bwd"] = _mlstm_bwd(
        s["qkv"], s["gates"], s["cst"], s["nst"], s["mst"], s["cell"], u, p["ml_g"], d_ycat, p["wif_b"],
        ride("mlstm_bwd"))
    ng = dgt.shape[1]
    g_w_if = _grad_matmul(s["qkv"], _bf(dgt)[None], 3, lambda b: b, lambda b: 0, (3 * d, ng), (d, ng),
                          lambda b: (b, 0))[0][0]
    du, g_wqkv, g_ml_cw, g_ml_cb = _ml_pre_bwd(dqkv, u, p["ml_conv_w"], p["ml_conv_b"], p["wqkv_b"], du)
    du, g_wa, g_wx, g_ba, g_bx, g_lam, g_rg_cw, g_rg_cb = _rg_bwd(d_ycat, u, s["hh"], p["rg_conv_w"], p["rg_conv_b"],
                                                                  p["rg_wa_b"], p["rg_ba"], p["rg_wx_b"], p["rg_bx"],
                                                                  p["rg_lam"], du)
    grads.update(rg_conv_w=g_rg_cw, rg_conv_b=g_rg_cb, rg_w_a=g_wa, rg_b_a=g_ba, rg_w_x=g_wx, rg_b_x=g_bx,
                 rg_lambda=g_lam, ml_conv_w=g_ml_cw, ml_conv_b=g_ml_cb, ml_w_qkv=g_wqkv, ml_w_if=g_w_if, ml_b_if=g_b_if,
                 ml_norm_g=g_ml_g)
    npc = du.shape[0]
    grads["w_in"], landed["grad_w_in"] = _grad_matmul(
        s["h_b"][None], du, npc, lambda b: 0, lambda b: (b + DU_PLANE[0]) % npc, (d, npc * d), (d, d),
        lambda b: (0, b), ride("grad_w_in"))
    (dx, d_shift, d_scale, grads["norm_g"]), landed["in_bwd"] = _in_bwd(du, p["w4"], s["x"], dxn, p["norm_g"],
                                                                        p["scale"], ride("in_bwd"))
    return dx, grads, jnp.concatenate([d_shift, d_scale, d_gate], axis=1), landed


def _me():
    return lax.axis_index("x"), lax.axis_index("y"), lax.axis_index("c")


def _remote(src, dst, send_sem, recv_sem, to):
    return pltpu.make_async_remote_copy(src_ref=src, dst_ref=dst, send_sem=send_sem, recv_sem=recv_sem,
                                        device_id=to, device_id_type=MESH)


def _all_gather8(blocks, space):
    n = len(blocks)
    relay = [b.size * b.dtype.itemsize >= RELAY_BYTES and b.shape[0] % 32 == 0 for b in blocks]

    def body(*refs):
        x_refs, out_refs = refs[:n], refs[n:2 * n]
        send_sems, recv_sems, local_sems = refs[2 * n:]
        x, y, c = _me()
        me, sibling = (x, y, c), (x, y, 1 - c)
        by_x, by_y, across = (1 - x, y, c), (x, 1 - y, c), (1 - x, 1 - y, c)

        def rows(i, blk, part=None):
            m_per = blocks[i].shape[0]
            at = (4 * blk[0] + 2 * blk[1] + blk[2]) * m_per
            if part is not None:
                m_per //= 2
                at += part * m_per
            return out_refs[i].at[pl.ds(at, m_per), :]

        def copy(i, k, blk, to, src=None, part=None):
            return _remote(rows(i, blk, part) if src is None else src, rows(i, blk, part), send_sems.at[8 * i + k],
                           recv_sems.at[8 * i + k], to)

        mine = [pltpu.make_async_copy(x_refs[i], rows(i, me), local_sems.at[i]) for i in range(n)]
        first = []
        for i in range(n):
            first.append(copy(i, 0, me, sibling, src=x_refs[i]))
            first += [copy(i, 1, me, by_x, src=x_refs[i]), copy(i, 2, me, by_y, src=x_refs[i])]
            if not relay[i]:
                first.append(copy(i, 3, me, across, src=x_refs[i]))
        for cp in mine + first:
            cp.start()
        passed = []
        for i in range(n):
            copy(i, 1, by_x, me).wait_recv()
            passed.append(copy(i, 4, by_x, sibling))
            if relay[i]:
                passed.append(copy(i, 3, by_x, by_y, part=0))
        for cp in passed:
            cp.start()
        n_x = len(passed)
        for i in range(n):
            copy(i, 2, by_y, me).wait_recv()
            passed.append(copy(i, 5, by_y, sibling))
            if relay[i]:
                passed.append(copy(i, 7, by_y, by_x, part=1))
        for cp in passed[n_x:]:
            cp.start()
        for i in range(n):
            if relay[i]:
                copy(i, 3, across, me, part=0).wait_recv()
                copy(i, 7, across, me, part=1).wait_recv()
            else:
                copy(i, 3, across, me).wait_recv()
            passed.append(copy(i, 6, across, sibling))
            passed[-1].start()
        for i in range(n):
            copy(i, 0, sibling, me).wait_recv()
            for k, blk in ((4, by_x), (5, by_y), (6, across)):
                copy(i, k, (blk[0], blk[1], 1 - c), me).wait_recv()
        for cp in first + passed:
            cp.wait_send()
        for cp in mine:
            cp.wait()

    spec = pl.BlockSpec(memory_space=space)
    return _pcall(
        body, name="all_gather8",
        out_shape=[jax.ShapeDtypeStruct((8 * b.shape[0], b.shape[1]), b.dtype) for b in blocks],
        in_specs=[spec] * n, out_specs=[spec] * n,
        scratch_shapes=[pltpu.SemaphoreType.DMA((8 * n,)), pltpu.SemaphoreType.DMA((8 * n,)),
                        pltpu.SemaphoreType.DMA((n,))],
    )(*blocks)


def _exchange(legs):
    n = len(legs)

    def body(*refs):
        copies, local, relays = _exchange_body(legs, refs[:n], refs[n:2 * n], *refs[2 * n:])
        for cp in copies + local:
            cp.start()
        _hand_on(relays)
        _wait_all(copies, local, relays)

    hbm = pl.BlockSpec(memory_space=pltpu.HBM)
    return _pcall(body, name="exchange", out_shape=[leg.landing() for leg in legs], in_specs=[hbm] * n,
                  out_specs=[hbm] * n, input_output_aliases=_exchange_aliases(legs, 0, 0),
                  scratch_shapes=_exchange_sems(legs))(*[leg.src for leg in legs])


def _row_tile(rows, cap=4096, mult=16):
    best = None
    for t in range(mult, min(rows, cap) + 1, mult):
        if rows % t == 0:
            best = t
    return rows if best is None else best


def _pair_sum(half, own, own_spec, got, got_spec, out_shape, out_spec, grid):
    def body(_, a_ref, b_ref, o_ref):
        o_ref[...] = (a_ref[...] + b_ref[...].astype(F32)).astype(o_ref.dtype)

    return _pcall(
        body, name="pair_sum",
        grid_spec=pltpu.PrefetchScalarGridSpec(num_scalar_prefetch=1, grid=grid, in_specs=[own_spec, got_spec],
                                               out_specs=out_spec),
        out_shape=out_shape, compiler_params=_seq(len(grid)))(half, own, got)


def _chip_sum(ids, part, met, fill, layer=0, stack=1):
    _, _, rows, n = part.shape
    tr = _row_tile(rows, cap=max(16, BLOCK_ELEMS // n))
    first = isinstance(stack, int)

    def body(_, own_ref, a_ref, b_ref, c_ref, *rest):
        acc = own_ref[...].astype(F32) + a_ref[...].astype(F32)
        acc = acc + b_ref[...].astype(F32)
        rest[-1][...] = acc + c_ref[...].astype(F32)

    blk = (None, None, tr, n)
    other = lambda k: pl.BlockSpec(blk, lambda j, ids: ((ids[0] + k) % 4, 0, j, 0))
    in_specs = [pl.BlockSpec(blk, lambda j, ids: (ids[0], 0, j, 0)), other(1), other(2), other(3)]
    return _pcall(
        body, name="chip_sum",
        grid_spec=pltpu.PrefetchScalarGridSpec(
            num_scalar_prefetch=1, grid=(rows // tr,),
            in_specs=in_specs if first else in_specs + [pl.BlockSpec(memory_space=pl.ANY)],
            out_specs=pl.BlockSpec(blk, lambda j, ids: (layer, ids[1] if fill else 0, j, 0))),
        out_shape=jax.ShapeDtypeStruct(((stack,) if first else stack.shape[:1]) + (2 if fill else 1, rows, n), F32),
        input_output_aliases={} if first else {5: 0},
        compiler_params=_seq())(*((ids, part, met, met, met) if first else (ids, part, met, met, met, stack)))


def _ada_mod(c_all, w_ada, b_ada_cols):
    depth, d, n = w_ada.shape
    nb = c_all.shape[0]

    def body(c_ref, w_ref, b_ref, o_ref):
        cv = c_ref[...]
        ca = _bf(cv * _sigmoid(cv))
        o_ref[0] = _dot(ca, _bf(w_ref[0])) + b_ref[0]

    return _pcall(body, name="ada_mod", grid=(depth,),
                  in_specs=[pl.BlockSpec((nb, d), lambda l: (0, 0)), pl.BlockSpec((1, d, n), lambda l: (l, 0, 0)),
                            pl.BlockSpec((1, 1, n), lambda l: (l, 0, 0))],
                  out_specs=pl.BlockSpec((1, nb, n), lambda l: (l, 0, 0)),
                  out_shape=jax.ShapeDtypeStruct((depth, nb, n), F32), compiler_params=_seq())(c_all, w_ada, b_ada_cols)


def _ada_grad(c_all, dmod_cols, rows_all):
    nb, d = c_all.shape
    depth, _, n = dmod_cols.shape
    kinds, n_all = rows_all.shape[1], rows_all.shape[3]

    def body(c_ref, dm_ref, da_ref, gw_ref, gb_ref):
        cv = c_ref[...]
        ca = _bf(cv * _sigmoid(cv))
        gw_ref[0] = _dot_tn(ca, _bf(dm_ref[0]))
        for k in range(kinds):
            gb_ref[0, k] = _colsum(da_ref[0, k])

    return _pcall(body, name="ada_grad", grid=(depth,),
                  in_specs=[pl.BlockSpec((nb, d), lambda l: (0, 0)), pl.BlockSpec((1, nb, n), lambda l: (l, 0, 0)),
                            pl.BlockSpec((1, kinds, nb, n_all), lambda l: (l, 0, 0, 0))],
                  out_specs=[pl.BlockSpec((1, d, n), lambda l: (l, 0, 0)),
                             pl.BlockSpec((1, kinds, 1, n_all), lambda l: (l, 0, 0, 0))],
                  out_shape=[jax.ShapeDtypeStruct((depth, d, n), F32), jax.ShapeDtypeStruct((depth, kinds, 1, n_all), F32)],
                  compiler_params=_seq())(c_all, dmod_cols, rows_all)


def _adamw(items, ride=None):
    two_d = [tuple(t.reshape(w.size // w.shape[-1], w.shape[-1]) for t in (w, g, m, v)) for w, g, m, v in items]
    n = len(items)
    if n == 1:
        rows, cols = two_d[0][0].shape
        tr = _row_tile(rows, cap=max(8, BLOCK_ELEMS // cols), mult=8)
        blocks = [pl.BlockSpec((tr, cols), lambda i: (i, 0))]
        grid = (rows // tr,)
    else:
        blocks = [pl.BlockSpec(t[0].shape, lambda i: (0, 0)) for t in two_d]
        grid = (1,)

    def body(*refs):
        for k in range(n):
            w_ref, g_ref, m_ref, v_ref = refs[4 * k:4 * k + 4]
            d_ref, mo_ref, vo_ref = refs[4 * n + 3 * k:4 * n + 3 * k + 3]
            gv = g_ref[...]
            mn = ADAM_B1 * m_ref[...] + (1.0 - ADAM_B1) * gv
            vn = ADAM_B2 * v_ref[...] + (1.0 - ADAM_B2) * (gv * gv)
            m_hat = mn / (1.0 - ADAM_B1 ** ADAM_STEP)
            v_hat = vn / (1.0 - ADAM_B2 ** ADAM_STEP)
            d_ref[...] = -ADAM_LR * (m_hat / (jnp.sqrt(v_hat) + ADAM_EPS) + ADAM_WD * w_ref[...])
            mo_ref[...] = mn
            vo_ref[...] = vn

    outs, got = _pcall_ride(
        body, ride, name="adamw", grid=grid,
        in_specs=[b for b in blocks for _ in range(4)], out_specs=[b for b in blocks for _ in range(3)],
        out_shape=[jax.ShapeDtypeStruct(t[0].shape, F32) for t in two_d for _ in range(3)],
        compiler_params=_seq(), args=tuple(a for t in two_d for a in t))
    return [tuple(o.reshape(items[k][0].shape) for o in outs[3 * k:3 * k + 3]) for k in range(n)], got


WEIGHTS = ["norm_g", "w_ada", "b_ada", "w_in", "rg_conv_w", "rg_conv_b", "rg_w_a", "rg_b_a", "rg_w_x", "rg_b_x",
           "rg_lambda", "ml_conv_w", "ml_conv_b", "ml_w_q", "ml_w_k", "ml_w_v", "ml_w_if", "ml_b_if", "ml_norm_g",
           "w_out", "final_g"]
SMALL_SHARDED = {"rg_conv_w": 1, "ml_conv_w": 1, "ml_w_if": 0}
REPLICATED = ["rg_w_a", "rg_w_x", "rg_conv_b", "rg_b_a", "rg_b_x", "rg_lambda", "ml_conv_b", "ml_norm_g", "ml_b_if"]
LANES = 128


def _to_pieces(g, axis):
    shp = g.shape
    g = g.reshape(shp[:axis] + (4, 2, shp[axis] // 8) + shp[axis + 1:])
    g = jnp.moveaxis(g, (axis, axis + 1), (0, 1))
    return g.reshape(4, 2, -1)


def _from_pieces(p, shard_shape, axis):
    k = p.shape[0]
    rest = shard_shape[:axis] + (shard_shape[axis] // k,) + shard_shape[axis + 1:]
    t = jnp.moveaxis(p.reshape((k,) + rest), 0, axis)
    return t.reshape(shard_shape)


def _pad_rows(flat, mult):
    n = flat.shape[-1]
    pad = (-n) % mult
    if pad:
        flat = jnp.concatenate([flat, jnp.zeros(flat.shape[:-1] + (pad,), flat.dtype)], axis=-1)
    return flat


def kernel(x, c, norm_g, w_ada, b_ada, w_in, rg_conv_w, rg_conv_b, rg_w_a, rg_b_a, rg_w_x, rg_b_x, rg_lambda, ml_conv_w, ml_conv_b, ml_w_q, ml_w_k, ml_w_v, ml_w_if, ml_b_if, ml_norm_g, w_out, final_g, loss_target, m_norm_g, m_w_ada, m_b_ada, m_w_in, m_rg_conv_w, m_rg_conv_b, m_rg_w_a, m_rg_b_a, m_rg_w_x, m_rg_b_x, m_rg_lambda, m_ml_conv_w, m_ml_conv_b, m_ml_w_q, m_ml_w_k, m_ml_w_v, m_ml_w_if, m_ml_b_if, m_ml_norm_g, m_w_out, m_final_g, v_norm_g, v_w_ada, v_b_ada, v_w_in, v_rg_conv_w, v_rg_conv_b, v_rg_w_a, v_rg_b_a, v_rg_w_x, v_rg_b_x, v_rg_lambda, v_ml_conv_w, v_ml_conv_b, v_ml_w_q, v_ml_w_k, v_ml_w_v, v_ml_w_if, v_ml_b_if, v_ml_norm_g, v_w_out, v_final_g):
    given = dict(locals())
    ax, ay, ac = lax.axis_index("x"), lax.axis_index("y"), lax.axis_index("c")
    chip = 2 * ax + ay
    me = 2 * chip + ac
    depth, d = norm_g.shape
    n_ada = w_ada.shape[2]
    pick = lambda a, i, axis=0: lax.dynamic_index_in_dim(a, i, axis, keepdims=False)

    convs = jnp.stack([rg_conv_w, ml_conv_w])
    n_conv = 2 * depth * CONV_WIDTH // 4
    blk = jnp.concatenate([c, convs.reshape(n_conv, d), jnp.zeros((8 - 1 - n_conv, d), F32)], axis=0)
    w_in_first = lax.dynamic_slice_in_dim(w_in[0], ac * (d // 2), d // 2, 0).astype(BF16)
    g0, w_in_first = _all_gather8([blk, w_in_first], pltpu.HBM)
    g0 = g0.reshape(8, 8, d)
    c_all = g0[:, 0, :]
    conv_full = g0[0::2, 1:1 + n_conv].reshape(4, 2, depth, CONV_WIDTH, d // 4)
    conv_full = conv_full.transpose(1, 2, 3, 0, 4).reshape(2, depth, CONV_WIDTH, d)

    b_cols = lax.dynamic_slice_in_dim(b_ada, chip * n_ada, n_ada, axis=1)[:, None, :]
    mod_part = _ada_mod(c_all, w_ada, b_cols)
    g1 = _all_gather8([mod_part.transpose(1, 0, 2).reshape(8, depth * n_ada)], pltpu.VMEM)[0]
    g1 = g1.reshape(8, 8, depth, n_ada)[0::2]
    mod_me = pick(g1.transpose(1, 2, 0, 3).reshape(8, depth, 4 * n_ada), me)

    def half_of(w, axis):
        n = w.shape[axis] // 2
        return lax.dynamic_slice_in_dim(w, ac * n, n, axis).astype(BF16)

    n_sh = w_in.shape[2]
    heads, hd_cut, hd = ml_w_q.shape[1:]

    def blocks_of(l):
        wqkv = jnp.stack([ml_w_q[l], ml_w_k[l], ml_w_v[l]])
        return [half_of(w_in[l], 0), half_of(w_out[l], 0), half_of(wqkv, 2).reshape(-1, hd), half_of(ml_w_if[l], 0)]

    def layer_of(l, w4, rest):
        return dict(
            norm_g=norm_g[l][None], shift=mod_me[l, 0:d][None], scale=mod_me[l, d:2 * d][None],
            gate=mod_me[l, 2 * d:3 * d][None], w4=w4.reshape(4, d, n_sh),
            rg_conv_w=conv_full[0, l], rg_conv_b=rg_conv_b[l][None], rg_wa_b=_bf(rg_w_a[l]), rg_ba=rg_b_a[l][None],
            rg_wx_b=_bf(rg_w_x[l]), rg_bx=rg_b_x[l][None], rg_lam=rg_lambda[l][None],
            ml_conv_w=conv_full[1, l], ml_conv_b=ml_conv_b[l][None], b_if=ml_b_if[l][None], b_ift=ml_b_if[l][:, None],
            ml_g=ml_norm_g[l][None], **rest)

    def rest_of(gathered):
        w_out_b, wqkv_g, wif = gathered
        return dict(w_out_b=w_out_b, wqkv_b=_from_pieces(wqkv_g.reshape(8, -1), (3, heads, hd, hd), 2), wif_b=wif,
                    wift_b=wif.T)

    spread = lambda blocks: [Leg(b, "spread") for b in blocks]
    fill = lambda landed: [Leg(t, "sib_fill") for t in landed]
    flat = lambda filled: [t.reshape(-1, t.shape[-1]) for t in filled]
    first = blocks_of(0)
    n_rest = len(first) - 1
    p = layer_of(0, w_in_first, {})
    layers, saved = [], []
    xl = x[0]
    for l in range(depth):
        nxt = blocks_of(l + 1) if l + 1 < depth else []
        skip = n_rest if l == 0 else 0
        rides = dict(rg_fwd=lambda landed, nxt=nxt: spread(nxt[:1]))
        if l == 0:
            rides.update(ln_inproj=lambda landed: spread(first[1:]),
                         rg_fwd=lambda landed, nxt=nxt: fill(landed["ln_inproj"]) + spread(nxt[:1]),
                         late=lambda landed: rest_of(flat(landed["rg_fwd"][:n_rest])))
        if nxt:
            rides.update(mlstm_fwd=lambda landed, nxt=nxt: spread(nxt[1:]),
                         out_proj=lambda landed, skip=skip: fill(list(landed["rg_fwd"][skip:]) + list(landed["mlstm_fwd"])))
        xl, s, p, landed = _layer_fwd(xl, p, rides, None if nxt else (final_g[None], loss_target[0]))
        layers.append(p)
        saved.append(s)
        if nxt:
            arrived = flat(landed["out_proj"])
            p = layer_of(l + 1, arrived[0], rest_of(arrived[1:]))
    dx, g_final, loss = xl

    half = ac.reshape(1)
    ids = jnp.stack([chip, ac])
    r_out = w_out.shape[1] // 2

    def pair_in(g_w_in, got_in):
        return _pair_sum(
            half, g_w_in, pl.BlockSpec((None, d // 2, n_sh), lambda s, h: (0, h[0], s)),
            got_in, pl.BlockSpec((None, None, d // 2, n_sh), lambda s, h: (0, s, 0, 0)),
            jax.ShapeDtypeStruct((4, 1, d // 2, n_sh), BF16),
            pl.BlockSpec((None, None, d // 2, n_sh), lambda s, h: (s, 0, 0, 0)), (4,))

    def pair_out(g_out5, got_out):
        return _pair_sum(
            half, g_out5, pl.BlockSpec((None, None, None, r_out, d), lambda s, h: (0, s, h[0], 0, 0)),
            got_out, pl.BlockSpec((None, None, r_out, d), lambda s, h: (0, s, 0, 0)),
            jax.ShapeDtypeStruct((4, 1, r_out, d), BF16),
            pl.BlockSpec((None, None, r_out, d), lambda s, h: (s, 0, 0, 0)), (4,))

    def pair_slab(slab, got, dtype):
        rows = got.shape[0] // 4
        blk = pl.BlockSpec((rows, LANES), lambda s, h: (s, 0))
        return _pair_sum(half, slab, pl.BlockSpec((None, rows, LANES), lambda s, h: (h[0], s, 0)), got, blk,
                         jax.ShapeDtypeStruct((4 * rows, LANES), dtype), blk, (4,)).reshape(4, 1, rows, LANES)

    row_pad = lambda n: -(-n // (8 * LANES)) * (8 * LANES)

    def as_rows(t):
        if t.shape[-1] == LANES and t.size % (8 * LANES) == 0:
            return t.reshape(-1, LANES)
        return _pad_rows(t.reshape(-1), 8 * LANES).reshape(-1, LANES)

    chips = lambda arrs: [Leg(a, "chips") for a in arrs]
    out5 = lambda g: g["w_out"].reshape(1, 4, 2, r_out, d)
    r_q = hd // 8
    qkv5 = lambda g: g["ml_w_qkv"].reshape(3 * heads, 4, 2, r_q, hd)

    def pair_qkv(g5, got):
        return _pair_sum(
            half, g5, pl.BlockSpec((3 * heads, None, None, r_q, hd), lambda s, h: (0, s, h[0], 0, 0)),
            got, pl.BlockSpec((3 * heads, None, r_q, hd), lambda s, h: (0, s, 0, 0)),
            jax.ShapeDtypeStruct((4, 1, 3 * heads, r_q, hd), BF16),
            pl.BlockSpec((None, None, 3 * heads, r_q, hd), lambda s, h: (s, 0, 0, 0, 0)), (4,))

    grads, dmods, parts, mets = [None] * depth, [None] * depth, [None] * depth, [None] * depth
    small = {}

    def early_exchange(g, landed):
        every = [g] + grads[1:]
        sm = jnp.concatenate([_to_pieces(every[l][name], axis) for l in range(depth)
                              for name, axis in SMALL_SHARDED.items()], axis=-1)
        sm = _pad_rows(sm, 16 * LANES)
        sm = sm.transpose(1, 0, 2).reshape(2, -1, LANES)
        rep = [as_rows(every[l][name]) for l in range(depth) for name in REPLICATED]
        rep = jnp.concatenate(rep + [as_rows(g_final), as_rows(loss)], axis=0)
        rep = jnp.concatenate([rep, jnp.zeros(((-rep.shape[0]) % 64, LANES), F32)], axis=0)
        rep = rep.reshape(4, 2, -1, LANES).transpose(1, 0, 2, 3).reshape(2, -1, LANES)
        got_sm, got_rep, got_q = _exchange([Leg(sm, "sib_slab"), Leg(rep, "sib_slab"), Leg(qkv5(g), "sib_w_out")])
        small["parts"] = [pair_out(out5(g), landed["mlstm_bwd"][0]), pair_slab(sm, got_sm, BF16),
                          pair_slab(rep, got_rep, F32), pair_qkv(qkv5(g), got_q)]
        return chips(small["parts"])

    def last_exchange(g, landed):
        (got_in,) = _exchange([Leg(g["w_in"], "sib_w_in")])
        small["part_in"] = pair_in(g["w_in"], got_in)
        return chips([small["part_in"]])

    for l in reversed(range(depth)):
        above = parts[l + 1] if l + 1 < depth else []
        rides = dict(mlstm_bwd=lambda g, landed, above=above: [Leg(out5(g), "sib_w_out")] + chips(above),
                     in_bwd=lambda g, landed: [Leg(g["w_in"], "sib_w_in"), Leg(qkv5(g), "sib_w_out")])
        if l == 0:
            rides.update(grad_w_in=early_exchange, in_bwd=last_exchange)
        dx, grads[l], dmods[l], got = _layer_bwd(dx, layers[l], saved[l], rides)
        if above:
            mets[l + 1] = got["mlstm_bwd"][1:]
        if l > 0:
            parts[l] = [pair_in(grads[l]["w_in"], got["in_bwd"][0]), pair_out(out5(grads[l]), got["mlstm_bwd"][0]),
                        pair_qkv(qkv5(grads[l]), got["in_bwd"][1])]
    part_out, part_sm, part_rep, part_q = small["parts"]
    met_out, met_sm, met_rep, met_q = got["grad_w_in"]
    parts[0], mets[0] = [small["part_in"], part_out, part_q], [got["in_bwd"][0], met_out, met_q]
    n_rep = part_rep.shape[2]

    pad = lambda t: jnp.concatenate([t, jnp.zeros((1, 2 * d), F32)], axis=1)
    rows = [r for l in range(depth) for r in (dmods[l], pad(grads[l]["norm_g"]))]
    blk = jnp.concatenate(rows + [jnp.zeros((8 - 2 * depth, 3 * d), F32)], axis=0)
    red_rep = _chip_sum(ids, part_rep, met_rep, False).reshape(n_rep, LANES)
    rows_all, rep_all = _all_gather8([blk, red_rep], pltpu.VMEM)
    rows_all, rep_all = rows_all.reshape(8, 8, 3 * d)[:, :2 * depth], rep_all.reshape(-1)
    rows_all = rows_all.transpose(1, 0, 2).reshape(depth, 2, 8, 3 * d)
    dm_cols = lax.dynamic_slice_in_dim(rows_all[:, 0], chip * n_ada, n_ada, axis=2)
    g_w_ada, summed = _ada_grad(c_all, dm_cols, rows_all)

    g = dict(w_ada=g_w_ada, b_ada=summed[:, 0, 0], norm_g=summed[:, 1, 0, :d])
    item = lambda name: (given[name], g[name], given["m_" + name], given["v_" + name])
    both_in, both_out, both_q = depth, depth, depth
    flat_q = lambda t: t.reshape(4, 1, 3 * heads * r_q, hd)
    for l in range(depth):
        both_in = _chip_sum(ids, parts[l][0], mets[l][0], True, l, both_in)
        both_out = _chip_sum(ids, parts[l][1], mets[l][1], True, l, both_out)
        both_q = _chip_sum(ids, flat_q(parts[l][2]), flat_q(mets[l][2]), True, l, both_q)
    both_in, both_out, both_q, both_sm = _exchange(fill([both_in, both_out, both_q,
                                                         _chip_sum(ids, part_sm, met_sm, True)]))

    g.update(w_in=both_in.reshape(w_in.shape), w_out=both_out.reshape(w_out.shape))
    g_qkv = both_q.reshape(depth, 2, 3, heads, r_q, hd).transpose(0, 2, 3, 1, 4, 5)
    g_qkv = g_qkv.reshape(depth, 3, heads, 2 * r_q, hd)
    for i, name in enumerate(["ml_w_q", "ml_w_k", "ml_w_v"]):
        g[name] = g_qkv[:, i]
    shard = both_sm.reshape(2, -1)
    off = 0
    per_layer = {name: [] for name in SMALL_SHARDED}
    for l in range(depth):
        for name, axis in SMALL_SHARDED.items():
            n = grads[l][name].size // 8
            per_layer[name].append(_from_pieces(shard[:, off:off + n], given[name].shape[1:], axis))
            off += n
    for name in SMALL_SHARDED:
        g[name] = jnp.stack(per_layer[name])
    off = 0
    per_layer = {name: [] for name in REPLICATED}
    for l in range(depth):
        for name in REPLICATED:
            n = given[name][l].size
            per_layer[name].append(rep_all[off:off + n].reshape(given[name].shape[1:]))
            off += row_pad(n)
    for name in REPLICATED:
        g[name] = jnp.stack(per_layer[name])
    g["final_g"] = rep_all[off:off + d]
    loss_all = rep_all[off + row_pad(d)]

    stepped = {}
    rg_mats, ml_mats = ["rg_w_a", "rg_w_x"], ["ml_w_q", "ml_w_k", "ml_w_v"]
    vectors = [n for n in WEIGHTS if n not in ["w_ada", "w_in", "w_out"] + rg_mats + ml_mats]
    for names in (["w_ada"], ["w_in"], ["w_out"], rg_mats, ml_mats, vectors):
        stepped.update(zip(names, _adamw([item(name) for name in names])[0]))
    deltas, new_m, new_v = zip(*[stepped[name] for name in WEIGHTS])
    return (loss_all, dx[None], *[g[name] for name in WEIGHTS], *deltas, *new_m, *new_v)
```

```python
import functools
from typing import NamedTuple

import jax
import jax.numpy as jnp
from jax import lax
from jax.experimental import pallas as pl
from jax.experimental.pallas import tpu as pltpu

F32 = jnp.float32
BF16 = jnp.bfloat16

EPS = 1e-6
RG_C = 8.0
CONV_WIDTH = 4
ML_CHUNK = 512
HALO = 8
ROWS_VECTOR = 512
ROWS_MATMUL = 1024
ROWS_IN_BWD = 512
ROWS_GRAD_MATMUL = 2048
BLOCK_ELEMS = 1 << 18
RELAY_BYTES = 1 << 18
ADAM_LR = 0.001
ADAM_B1 = 0.9
ADAM_B2 = 0.999
ADAM_EPS = 1e-08
ADAM_WD = 0.01
ADAM_STEP = 10
MESH = pl.DeviceIdType.MESH


def _pcall(body, **kw):
    return pl.pallas_call(body, **kw)


class Leg(NamedTuple):
    src: jax.Array
    kind: str

    def landing(self):
        a = self.src
        shape = {"chips": lambda: a.shape, "spread": lambda: (4, 2) + a.shape, "sib_fill": lambda: a.shape,
                 "sib_w_in": lambda: (a.shape[0], 4, a.shape[1] // 2, a.shape[2] // 4),
                 "sib_w_out": lambda: a.shape[:2] + a.shape[3:], "sib_slab": lambda: a.shape[1:]}[self.kind]()
        return jax.ShapeDtypeStruct(shape, a.dtype)

    def relayed(self):
        a = self.src
        return self.kind == "spread" and a.size * a.dtype.itemsize >= RELAY_BYTES and a.shape[0] % 32 == 0

    def copies(self, src, dst, x, y, c):
        a, me_s, o = self.src, 2 * x + y, 1 - c
        chips = [(1 - x, y), (x, 1 - y), (1 - x, 1 - y)]
        if self.kind == "chips":
            return [(src.at[2 * px + py], dst.at[me_s], (px, py, c)) for px, py in chips], [], []
        if self.kind == "spread":
            own = dst.at[me_s, c]
            if not self.relayed():
                return [(src, own, (px, py, c)) for px, py in chips], [(src, own)], []
            by_x, by_y, half = chips[0], chips[1], a.shape[0] // 2
            part = lambda chip, k: dst.at[2 * chip[0] + chip[1], c, pl.ds(k * half, half)]
            return ([(src, own, (*by_x, c)), (src, own, (*by_y, c))], [(src, own)],
                    [(part(by_x, 0), part(by_x, 0), (*by_y, c), 0), (part(by_y, 1), part(by_y, 1), (*by_x, c), 1)])
        depth = pl.ds(0, a.shape[0])
        if self.kind == "sib_fill":
            return [(dst.at[depth, c], dst.at[depth, c], (x, y, o))], [], []
        if self.kind == "sib_w_in":
            half, n = a.shape[1] // 2, a.shape[2] // 4
            return [(src.at[depth, pl.ds(o * half, half), pl.ds(s * n, n)], dst.at[depth, s], (x, y, o))
                    for s in range(4)], [], []
        if self.kind == "sib_w_out":
            return [(src.at[depth, pl.ds(0, 4), o], dst, (x, y, o))], [], []
        return [(src.at[o], dst, (x, y, o))], [], []

    def n_copies(self):
        return 4 if self.relayed() else {"chips": 3, "spread": 3, "sib_w_in": 4}.get(self.kind, 1)


def _exchange_body(legs, srcs, dsts, send_sems, recv_sems, local_sems):
    x, y, c = _me()
    remote, local, relays, k = [], [], [], 0
    for i, leg in enumerate(legs):
        far, near, handed = leg.copies(srcs[i], dsts[i], x, y, c)
        at = len(remote)
        for src, dst, to in far:
            remote.append(_remote(src, dst, send_sems.at[k], recv_sems.at[k], to))
            k += 1
        for src, dst, to, after in handed:
            relays.append((_remote(src, dst, send_sems.at[k], recv_sems.at[k], to), remote[at + after]))
            k += 1
        local += [pltpu.make_async_copy(src, dst, local_sems.at[i]) for src, dst in near]
    return remote, local, relays


def _hand_on(relays):
    for cp, after in relays:
        after.wait_recv()
        cp.start()


def _wait_all(copies, local, relays):
    arrived, handed = [after for _, after in relays], [cp for cp, _ in relays]
    for cp in [cp for cp in copies if not any(cp is a for a in arrived)] + handed:
        cp.wait_recv()
    for cp in copies + handed:
        cp.wait_send()
    for cp in local:
        cp.wait()


def _exchange_sems(legs):
    n = sum(leg.n_copies() for leg in legs)
    return [pltpu.SemaphoreType.DMA((n,)), pltpu.SemaphoreType.DMA((n,)), pltpu.SemaphoreType.DMA((len(legs),))]


def _exchange_aliases(legs, n_in, n_out):
    return {n_in + i: n_out + i for i, leg in enumerate(legs) if leg.kind == "sib_fill"}


def _pcall_ride(body, ride, *, grid, in_specs, out_specs, out_shape, args, scratch_shapes=(), **kw):
    n_in, n_out, n_scr = len(in_specs), len(out_specs), len(scratch_shapes)
    if not ride:
        res = _pcall(body, grid=grid, in_specs=in_specs, out_specs=out_specs, out_shape=out_shape,
                     scratch_shapes=list(scratch_shapes), **kw)(*args)
        return res, []
    nr = len(ride)

    def riding(*refs):
        ins, rsrc = refs[:n_in], refs[n_in:n_in + nr]
        outs, rdst = refs[n_in + nr:n_in + nr + n_out], refs[n_in + nr + n_out:n_in + 2 * nr + n_out]
        scr = refs[n_in + 2 * nr + n_out:n_in + 2 * nr + n_out + n_scr]
        copies, local, relays = _exchange_body(ride, rsrc, rdst, *refs[n_in + 2 * nr + n_out + n_scr:])
        at_step = lambda steps: functools.reduce(jnp.logical_and, [pl.program_id(a) == s for a, s in enumerate(steps)])

        @pl.when(at_step([0] * len(grid)))
        def _():
            for cp in copies + local:
                cp.start()

        body(*ins, *outs, *scr)

        if relays:
            @pl.when(at_step([grid[0] // 2] + [0] * (len(grid) - 1)))
            def _():
                _hand_on(relays)

        @pl.when(at_step([g - 1 for g in grid]))
        def _():
            _wait_all(copies, local, relays)

    hbm = pl.BlockSpec(memory_space=pltpu.HBM)
    aliases = {**kw.pop("input_output_aliases", {}), **_exchange_aliases(ride, n_in, n_out)}
    res = _pcall(
        riding, grid=grid, in_specs=list(in_specs) + [hbm] * nr, out_specs=list(out_specs) + [hbm] * nr,
        out_shape=list(out_shape) + [leg.landing() for leg in ride], input_output_aliases=aliases,
        scratch_shapes=list(scratch_shapes) + _exchange_sems(ride), **kw)(*args, *[leg.src for leg in ride])
    return res[:n_out], res[n_out:]


def _seq(n=1):
    return pltpu.CompilerParams(dimension_semantics=("arbitrary",) * n)


def _dot(a, b):
    return jnp.dot(a, b, preferred_element_type=F32)


def _dot_nt(a, b):
    return lax.dot_general(a, b, (((1,), (1,)), ((), ())), preferred_element_type=F32)


def _dot_tn(a, b):
    return lax.dot_general(a, b, (((0,), (0,)), ((), ())), preferred_element_type=F32)


def _bf(x):
    return x.astype(BF16)


def _sigmoid(x):
    return 0.5 * jnp.tanh(0.5 * x) + 0.5


def _log1p(z):
    u = 1.0 + z
    return jnp.where(u == 1.0, z, jnp.log(u) * (z / jnp.where(u == 1.0, 1.0, u - 1.0)))


def _softplus(x):
    return jnp.maximum(x, 0.0) + _log1p(jnp.exp(-jnp.abs(x)))


def _log_sigmoid(x):
    return -_softplus(-x)


def _one_minus_sq(a, log_a):
    x = 2.0 * log_a
    small = -x * (1.0 + x * (0.5 + x * (1.0 / 6.0)))
    return jnp.where(x > -0.004, small, 1.0 - a * a)


def _dsilu(x, s):
    return s * (1.0 + x * (1.0 - s))


def _rowsum(x):
    return jnp.sum(x, axis=1, keepdims=True)


def _colsum(x):
    return jnp.sum(x, axis=0, keepdims=True)


def _shift_down(win, s):
    return win if s == 0 else pltpu.roll(win, s, 0)


def _shift_up(win, s):
    return win if s == 0 else pltpu.roll(win, win.shape[0] - s, 0)


def _conv_taps(win):
    return [_shift_down(win, CONV_WIDTH - 1 - k)[HALO:] for k in range(CONV_WIDTH)]


def _conv_fwd(taps, w_ref, b_ref):
    acc = b_ref[...] + w_ref[CONV_WIDTH - 1:CONV_WIDTH, :] * taps[CONV_WIDTH - 1]
    for k in range(CONV_WIDTH - 1):
        acc = acc + w_ref[k:k + 1, :] * taps[k]
    return acc


def _split3(x):
    hi = _bf(x)
    r1 = x - hi.astype(F32)
    mid = _bf(r1)
    lo = _bf(r1 - mid.astype(F32))
    return hi, mid, lo


def _tri_dot_left(tri, x):
    hi, mid, lo = _split3(x)
    return _dot(tri, hi) + _dot(tri, mid) + _dot(tri, lo)


def _tri_dot_right(x, tri):
    hi, mid, lo = _split3(x)
    return _dot(hi, tri) + _dot(mid, tri) + _dot(lo, tri)


def _tile(n, want):
    t = min(n, want)
    assert n % t == 0
    return t


def _ln_inproj(x, g, scale, shift, w4, ride=None):
    s_len, d = x.shape
    nj, _, nsh = w4.shape
    tm = _tile(s_len, ROWS_MATMUL)
    ni = s_len // tm

    def body(x_ref, g_ref, sc_ref, sh_ref, w_ref, h_ref, u_ref, hs):
        rows = pl.ds(pl.multiple_of(pl.program_id(1) * tm, tm), tm)

        @pl.when(pl.program_id(0) == 0)
        def _():
            xv = x_ref[...]
            r = lax.rsqrt(jnp.mean(xv * xv, axis=-1, keepdims=True) + EPS)
            hv = (xv * r * g_ref[...]) * (1.0 + sc_ref[...]) + sh_ref[...]
            hs[rows, :] = _bf(hv)
            h_ref[...] = hs[rows, :]

        u_ref[...] = _dot(hs[rows, :], w_ref[0])

    vec = pl.BlockSpec((1, d), lambda j, i: (0, 0))
    once = pl.BlockSpec((tm, d), lambda j, i: (jnp.where(j == 0, i, ni - 1), 0))
    return _pcall_ride(
        body, ride, name="ln_inproj", grid=(nj, ni),
        in_specs=[once, vec, vec, vec, pl.BlockSpec((1, d, nsh), lambda j, i: (j, 0, 0))],
        out_specs=[once, pl.BlockSpec((tm, nsh), lambda j, i: (i, j))],
        out_shape=[jax.ShapeDtypeStruct((s_len, d), BF16), jax.ShapeDtypeStruct((s_len, nj * nsh), F32)],
        scratch_shapes=[pltpu.VMEM((s_len, d), BF16)],
        compiler_params=_seq(2),
        args=(x, g, scale, shift, w4))


def _rg_gates(xc, wa_ref, ba_ref, wx_ref, bx_ref, lam_ref):
    heads, hd, _ = wa_ref.shape
    xb = _bf(xc)
    ga = jnp.concatenate([_dot(xb[:, h * hd:(h + 1) * hd], wa_ref[h]) for h in range(heads)], axis=1) + ba_ref[...]
    gx = jnp.concatenate([_dot(xb[:, h * hd:(h + 1) * hd], wx_ref[h]) for h in range(heads)], axis=1) + bx_ref[...]
    r = _sigmoid(ga)
    ig = _sigmoid(gx)
    sp = _softplus(-lam_ref[...])
    log_a = (-RG_C) * r * sp
    a = jnp.exp(log_a)
    mult = jnp.sqrt(_one_minus_sq(a, log_a))
    return r, ig, sp, log_a, a, mult


def _scan_groups(a, u, reverse):
    n, c = a.shape
    a = a.reshape(n // 8, 8, c)
    u = u.reshape(n // 8, 8, c)
    row = lax.broadcasted_iota(jnp.int32, a.shape, 1)
    for k in (1, 2, 4):
        sft = 8 - k if reverse else k
        a_sh, u_sh = pltpu.roll(a, sft, 1), pltpu.roll(u, sft, 1)
        ok = row < 8 - k if reverse else row >= k
        u = jnp.where(ok, a * u_sh + u, u)
        a = jnp.where(ok, a * a_sh, a)
    return a.reshape(n, c), u.reshape(n, c)


def _rg_fwd(u, conv_w, conv_b, wa_b, ba, wx_b, bx, lam, ride=None):
    s_len = u.shape[0]
    d = conv_w.shape[1]
    tm = _tile(s_len, ROWS_VECTOR)
    per = tm // HALO

    def body(x_ref, xp_ref, z_ref, cw_ref, cb_ref, wa_ref, ba_ref, wx_ref, bx_ref, lam_ref,
             hh_ref, y_ref, carry):
        i = pl.program_id(0)

        @pl.when(i == 0)
        def _():
            carry[...] = jnp.zeros_like(carry)

        prev = jnp.where(i == 0, 0.0, xp_ref[...])
        xc = _conv_fwd(_conv_taps(jnp.concatenate([prev, x_ref[...]], axis=0)), cw_ref, cb_ref)
        _, ig, _, _, a, mult = _rg_gates(xc, wa_ref, ba_ref, wx_ref, bx_ref, lam_ref)
        ca, cu = _scan_groups(a, mult * (ig * xc), reverse=False)
        c = carry[0:1, :]
        for j in range(tm // 8):
            blk = ca[j * 8:(j + 1) * 8] * c + cu[j * 8:(j + 1) * 8]
            hh_ref[j * 8:(j + 1) * 8, :] = blk
            c = blk[7:8]
        carry[0:1, :] = c
        z = z_ref[...]
        y_ref[0] = _bf(hh_ref[...] * (z * _sigmoid(z)))

    vec = pl.BlockSpec((1, d), lambda i: (0, 0))
    whole3 = lambda a: pl.BlockSpec(a.shape, lambda i: (0, 0, 0))
    return _pcall_ride(
        body, ride, name="rg_fwd", grid=(s_len // tm,),
        in_specs=[pl.BlockSpec((tm, d), lambda i: (i, 0)),
                  pl.BlockSpec((HALO, d), lambda i: (jnp.maximum(i * per - 1, 0), 0)),
                  pl.BlockSpec((tm, d), lambda i: (i, 1)),
                  pl.BlockSpec((CONV_WIDTH, d), lambda i: (0, 0)), vec,
                  whole3(wa_b), vec, whole3(wx_b), vec, vec],
        out_specs=[pl.BlockSpec((tm, d), lambda i: (i, 0)), pl.BlockSpec((1, tm, d), lambda i: (0, i, 0))],
        out_shape=[jax.ShapeDtypeStruct((s_len, d), F32), jax.ShapeDtypeStruct((2, s_len, d), BF16)],
        scratch_shapes=[pltpu.VMEM((8, d), F32)],
        compiler_params=_seq(),
        args=(u, u, u, conv_w, conv_b, wa_b, ba, wx_b, bx, lam))


def _ml_pre(u, conv_w, conv_b, wqkv_b, wif_b, wift_b, b_if, b_ift):
    s_len = u.shape[0]
    d = conv_w.shape[1]
    _, heads, hd, _ = wqkv_b.shape
    ng = 2 * heads
    tm = _tile(s_len, max(ROWS_VECTOR, ML_CHUNK))
    per = tm // HALO

    def body(x_ref, xp_ref, cw_ref, cb_ref, w_ref, wif_ref, wift_ref, bif_ref, bift_ref,
             qkv_ref, gt_ref, gtt_ref, bc_ref, bct_ref):
        i = pl.program_id(0)
        prev = jnp.where(i == 0, 0.0, xp_ref[...])
        xm = x_ref[...]
        pre = _conv_fwd(_conv_taps(jnp.concatenate([prev, xm], axis=0)), cw_ref, cb_ref)
        xcb = _bf(pre * _sigmoid(pre))
        xmb = _bf(xm)
        for h in range(heads):
            hs = slice(h * hd, (h + 1) * hd)
            qkv_ref[0, :, hs] = _bf(_dot(xcb[:, hs], w_ref[0, h]))
            qkv_ref[1, :, hs] = _bf(_dot(xcb[:, hs], w_ref[1, h]))
            qkv_ref[2, :, hs] = _bf(_dot(xmb[:, hs], w_ref[2, h]))
        qb, kb, vb = qkv_ref[0], qkv_ref[1], qkv_ref[2]
        gt = (_dot(qb, wif_ref[0:d, :]) + _dot(kb, wif_ref[d:2 * d, :]) + _dot(vb, wif_ref[2 * d:3 * d, :])
              + bif_ref[...])
        gtt = (_dot_nt(wift_ref[:, 0:d], qb) + _dot_nt(wift_ref[:, d:2 * d], kb)
               + _dot_nt(wift_ref[:, 2 * d:3 * d], vb) + bift_ref[...])
        gt_ref[...] = gt
        gtt_ref[...] = gtt
        r = lax.broadcasted_iota(jnp.int32, (tm, tm), 0)
        c = lax.broadcasted_iota(jnp.int32, (tm, tm), 1)
        same = (r // ML_CHUNK) == (c // ML_CHUNK)
        bc_ref[...] = _tri_dot_left(((r >= c) & same).astype(BF16), _log_sigmoid(gt))
        bct_ref[...] = _tri_dot_right(_log_sigmoid(gtt), ((r <= c) & same).astype(BF16))

    vec = pl.BlockSpec((1, d), lambda i: (0, 0))
    whole2 = lambda a: pl.BlockSpec(a.shape, lambda i: (0, 0))
    col = pl.BlockSpec((tm, ng), lambda i: (i, 0))
    row = pl.BlockSpec((ng, tm), lambda i: (0, i))
    return _pcall(
        body, name="ml_pre", grid=(s_len // tm,),
        in_specs=[pl.BlockSpec((tm, d), lambda i: (i, 2)),
                  pl.BlockSpec((HALO, d), lambda i: (jnp.maximum(i * per - 1, 0), 2)),
                  pl.BlockSpec((CONV_WIDTH, d), lambda i: (0, 0)), vec,
                  pl.BlockSpec(wqkv_b.shape, lambda i: (0, 0, 0, 0)), whole2(wif_b), whole2(wift_b), whole2(b_if),
                  whole2(b_ift)],
        out_specs=[pl.BlockSpec((3, tm, d), lambda i: (0, i, 0)), col, row, col, row],
        out_shape=[jax.ShapeDtypeStruct((3, s_len, d), BF16), jax.ShapeDtypeStruct((s_len, ng), F32),
                   jax.ShapeDtypeStruct((ng, s_len), F32), jax.ShapeDtypeStruct((s_len, ng), F32),
                   jax.ShapeDtypeStruct((ng, s_len), F32)],
        compiler_params=_seq(),
    )(u, u, conv_w, conv_b, wqkv_b, wif_b, wift_b, b_if, b_ift)


def _chunk_gates(gt, gtt, bc, bct, h, heads):
    li_c = gt[:, h:h + 1]
    li_r = gtt[h:h + 1, :]
    gf_c = gt[:, heads + h:heads + h + 1]
    b_c = bc[:, heads + h:heads + h + 1]
    b_r = bct[heads + h:heads + h + 1, :]
    return li_c, li_r, gf_c, b_c, b_r


def _chunk_weights(li_c, li_r, b_c, b_r, m_prev, causal):
    lc = b_c.shape[0]
    b_last = b_c[lc - 1:lc, :]
    dmat = jnp.where(causal, b_c - b_r + li_r, -jnp.inf)
    m_inter = b_c + m_prev
    m_t = jnp.maximum(m_inter, jnp.max(dmat, axis=1, keepdims=True))
    w_intra = jnp.exp(dmat - m_t)
    w_inter = jnp.exp(m_inter - m_t)
    g_c = b_last - b_c + li_c
    m_new = jnp.maximum(b_last + m_prev, jnp.max(g_c, axis=0, keepdims=True))
    w_state = jnp.exp(g_c - m_new)
    decay = jnp.exp(b_last + m_prev - m_new)
    return m_t, w_intra, w_inter, m_new, w_state, decay


def _tri_masks(lc):
    r = lax.broadcasted_iota(jnp.int32, (lc, lc), 0)
    c = lax.broadcasted_iota(jnp.int32, (lc, lc), 1)
    causal = r >= c
    return causal, causal.astype(BF16), (r <= c).astype(BF16)


def _mlstm_fwd(qkv, gates, u, ml_g, ycat, ride=None):
    _, s_len, d = qkv.shape
    ng = gates[0].shape[1]
    heads = ng // 2
    hd = d // heads
    lc = ML_CHUNK
    nc = s_len // lc
    kscale = hd ** -0.5

    def body(qkv_ref, gt_ref, gtt_ref, bc_ref, bct_ref, o_ref, z_ref, g_ref, _, cell_ref, y_ref, cst_ref, nst_ref,
             mst_ref, cs, ns, ms):
        @pl.when(pl.program_id(0) == 0)
        def _():
            cs[...] = jnp.zeros_like(cs)
            ns[...] = jnp.zeros_like(ns)
            ms[...] = jnp.zeros_like(ms)

        causal = _tri_masks(lc)[0]
        gtv, gttv, bcv, bctv = gt_ref[...], gtt_ref[...], bc_ref[...], bct_ref[...]
        old = [(cs[h], ns[h], ms[h]) for h in range(heads)]
        new, cells, ys = [], [], []
        for h in range(heads):
            hs = slice(h * hd, (h + 1) * hd)
            li_c, li_r, _, b_c, b_r = _chunk_gates(gtv, gttv, bcv, bctv, h, heads)
            c_old, n_old, m_old = old[h]
            m_prev = m_old[:, 0:1]
            m_t, w_intra, w_inter, m_new, w_state, decay = _chunk_weights(li_c, li_r, b_c, b_r, m_prev, causal)
            qb = qkv_ref[0, :, hs]
            ks = qkv_ref[1, :, hs].astype(F32) * kscale
            kb = _bf(ks)
            vb = qkv_ref[2, :, hs]
            s = _dot_nt(qb, kb) * w_intra
            num = _dot(_bf(s), vb) + w_inter * _dot(qb, _bf(c_old))
            den = _rowsum(s) + w_inter * _rowsum(qb.astype(F32) * n_old)
            cell = num / jnp.maximum(jnp.abs(den), jnp.exp(-m_t))
            kw = ks * w_state
            new.append((decay * c_old + _dot_tn(_bf(kw), vb), decay * n_old + _colsum(kw),
                        jnp.broadcast_to(m_new, m_old.shape)))
            cells.append(cell)
            hm = _sigmoid(o_ref[:, hs]) * cell
            hn = hm * lax.rsqrt(jnp.mean(hm * hm, axis=-1, keepdims=True) + EPS)
            z = z_ref[:, hs]
            ys.append(_bf((hn * g_ref[:, hs]) * (z * _sigmoid(z))))
        for h in range(heads):
            cst_ref[0, h] = _bf(old[h][0])
            nst_ref[0, h] = old[h][1]
            mst_ref[0, h] = old[h][2]
            cs[h], ns[h], ms[h] = new[h]
        cell_ref[...] = jnp.concatenate(cells, axis=1)
        y_ref[0] = jnp.concatenate(ys, axis=1)

    row = pl.BlockSpec((lc, d), lambda c: (c, 0))
    gcol = pl.BlockSpec((lc, ng), lambda c: (c, 0))
    grow = pl.BlockSpec((ng, lc), lambda c: (0, c))
    return _pcall_ride(
        body, ride, name="mlstm_fwd", grid=(nc,),
        in_specs=[pl.BlockSpec((3, lc, d), lambda c: (0, c, 0)), gcol, grow, gcol, grow,
                  pl.BlockSpec((lc, d), lambda c: (c, 3)), pl.BlockSpec((lc, d), lambda c: (c, 4)),
                  pl.BlockSpec((1, d), lambda c: (0, 0)), pl.BlockSpec(memory_space=pl.ANY)],
        out_specs=[row, pl.BlockSpec((1, lc, d), lambda c: (1, c, 0)),
                   pl.BlockSpec((1, heads, hd, hd), lambda c: (c, 0, 0, 0)),
                   pl.BlockSpec((1, heads, 1, hd), lambda c: (c, 0, 0, 0)),
                   pl.BlockSpec((1, heads, 1, 128), lambda c: (c, 0, 0, 0))],
        out_shape=[jax.ShapeDtypeStruct((s_len, d), F32), jax.ShapeDtypeStruct(ycat.shape, BF16),
                   jax.ShapeDtypeStruct((nc, heads, hd, hd), BF16),
                   jax.ShapeDtypeStruct((nc, heads, 1, hd), F32),
                   jax.ShapeDtypeStruct((nc, heads, 1, 128), F32)],
        scratch_shapes=[pltpu.VMEM((heads, hd, hd), F32), pltpu.VMEM((heads, 1, hd), F32),
                        pltpu.VMEM((heads, 1, 128), F32)],
        input_output_aliases={8: 1},
        compiler_params=_seq(),
        args=(qkv, *gates, u, u, ml_g, ycat))


def _out_proj(ycat, w_out_b, x, gate, ride=None):
    s_len, d = x.shape
    tm = _tile(s_len, ROWS_MATMUL)

    def body(a_ref, w_ref, x_ref, g_ref, y_ref, xn_ref):
        y = _dot(a_ref[0], w_ref[0:d, :]) + _dot(a_ref[1], w_ref[d:2 * d, :])
        y_ref[...] = y
        xn_ref[...] = x_ref[...] + g_ref[...] * y

    row = pl.BlockSpec((tm, d), lambda i: (i, 0))
    return _pcall_ride(
        body, ride, name="out_proj", grid=(s_len // tm,),
        in_specs=[pl.BlockSpec((2, tm, d), lambda i: (0, i, 0)), pl.BlockSpec((2 * d, d), lambda i: (0, 0)), row,
                  pl.BlockSpec((1, d), lambda i: (0, 0))],
        out_specs=[row, row],
        out_shape=[jax.ShapeDtypeStruct((s_len, d), F32)] * 2,
        compiler_params=_seq(),
        args=(ycat, w_out_b, x, gate))


def _out_proj_loss(ycat, w_out_b, x, gate, g, target):
    s_len, d = x.shape
    tm = _tile(s_len, ROWS_IN_BWD)

    def body(a_ref, w_ref, x_ref, gate_ref, g_ref, t_ref, y_ref, dx_ref, dg_ref, loss_ref):
        @pl.when(pl.program_id(0) == 0)
        def _():
            dg_ref[...] = jnp.zeros_like(dg_ref)
            loss_ref[...] = jnp.zeros_like(loss_ref)

        y = _dot(a_ref[0], w_ref[0:d, :]) + _dot(a_ref[1], w_ref[d:2 * d, :])
        y_ref[...] = y
        xv = x_ref[...] + gate_ref[...] * y
        r = lax.rsqrt(jnp.mean(xv * xv, axis=-1, keepdims=True) + EPS)
        xn = xv * r
        err = xn * g_ref[...] - t_ref[...]
        loss_ref[...] += 0.5 * jnp.sum(jnp.mean(err * err, axis=-1, keepdims=True))
        dout = err * (1.0 / d)
        dg_ref[...] += _colsum(dout * xn)
        dxn = dout * g_ref[...]
        dx_ref[...] = r * (dxn - xn * jnp.mean(dxn * xn, axis=-1, keepdims=True))

    row = pl.BlockSpec((tm, d), lambda i: (i, 0))
    vec = pl.BlockSpec((1, d), lambda i: (0, 0))
    return _pcall(
        body, name="out_proj_loss", grid=(s_len // tm,),
        in_specs=[pl.BlockSpec((2, tm, d), lambda i: (0, i, 0)), pl.BlockSpec((2 * d, d), lambda i: (0, 0)), row, vec,
                  vec, row],
        out_specs=[row, row, vec, pl.BlockSpec((1, 128), lambda i: (0, 0))],
        out_shape=[jax.ShapeDtypeStruct((s_len, d), F32), jax.ShapeDtypeStruct((s_len, d), F32),
                   jax.ShapeDtypeStruct((1, d), F32), jax.ShapeDtypeStruct((1, 128), F32)],
        compiler_params=_seq(),
    )(ycat, w_out_b, x, gate, g, target)


def _out_bwd(dxn, y, gate, w_out_b):
    s_len, d = dxn.shape
    tm = _tile(s_len, ROWS_MATMUL)

    def body(dx_ref, y_ref, g_ref, w_ref, dg_ref, dy_ref, dc_ref):
        @pl.when(pl.program_id(0) == 0)
        def _():
            dg_ref[...] = jnp.zeros_like(dg_ref)

        dx = dx_ref[...]
        dg_ref[...] += _colsum(dx * y_ref[...])
        dy = _bf(g_ref[...] * dx)
        dy_ref[...] = dy
        dc_ref[0] = _dot_nt(dy, w_ref[0:d, :])
        dc_ref[1] = _dot_nt(dy, w_ref[d:2 * d, :])

    row = pl.BlockSpec((tm, d), lambda i: (i, 0))
    vec = pl.BlockSpec((1, d), lambda i: (0, 0))
    return _pcall(
        body, name="out_bwd", grid=(s_len // tm,),
        in_specs=[row, row, vec, pl.BlockSpec((2 * d, d), lambda i: (0, 0))],
        out_specs=[vec, row, pl.BlockSpec((2, tm, d), lambda i: (0, i, 0))],
        out_shape=[jax.ShapeDtypeStruct((1, d), F32), jax.ShapeDtypeStruct((s_len, d), BF16),
                   jax.ShapeDtypeStruct((2, s_len, d), F32)],
        compiler_params=_seq(),
    )(dxn, y, gate, w_out_b)


def _grad_matmul(a3, b3, nblk, a_idx, b_idx, out_shape, out_block, out_idx, ride=None):
    _, s_len, m = a3.shape
    n = b3.shape[2]
    tk = _tile(s_len, ROWS_GRAD_MATMUL)
    whole = [t3.shape[0] == 1 for t3 in (a3, b3)]

    def body(a_ref, b_ref, o_ref):
        @pl.when(pl.program_id(1) == 0)
        def _():
            o_ref[...] = jnp.zeros_like(o_ref)

        rows = pl.ds(pl.multiple_of(pl.program_id(1) * tk, tk), tk)
        a, b = [ref[0, rows, :] if w else ref[0] for ref, w in zip((a_ref, b_ref), whole)]
        o_ref[...] += _dot_tn(a, b)

    def spec(width, idx, w):
        if w:
            return pl.BlockSpec((1, s_len, width), lambda p, t: (0, 0, 0))
        return pl.BlockSpec((1, tk, width), lambda p, t: (idx(p), t, 0))

    (out,), got = _pcall_ride(
        body, ride, name="grad_matmul", grid=(nblk, s_len // tk),
        in_specs=[spec(m, a_idx, whole[0]), spec(n, b_idx, whole[1])],
        out_specs=[pl.BlockSpec((None,) + out_block, lambda p, t: (0,) + out_idx(p))],
        out_shape=[jax.ShapeDtypeStruct((1,) + out_shape, F32)],
        compiler_params=_seq(2), args=(a3, b3))
    return out, got


DU_PLANE = (2, 3, 4, 0, 1)


def _mlstm_bwd(qkv, gates, cst, nst, mst, cell, u, ml_g, d_ycat, wif_b, ride=None):
    _, s_len, d = qkv.shape
    ng = gates[0].shape[1]
    heads = ng // 2
    hd = d // heads
    lc = ML_CHUNK
    nc = s_len // lc
    kscale = hd ** -0.5

    def body(qkv_ref, gt_ref, gtt_ref, bc_ref, bct_ref, cst_ref, nst_ref, mst_ref, cell_ref, o_ref, z_ref, g_ref, dy_ref,
             wif_ref, dqkv_ref, dgt_ref, dbif_ref, du_ref, dg_ref, dcs, dns):
        @pl.when(pl.program_id(0) == 0)
        def _():
            dbif_ref[...] = jnp.zeros_like(dbif_ref)
            dcs[...] = jnp.zeros_like(dcs)
            dns[...] = jnp.zeros_like(dns)
            dg_ref[...] = jnp.zeros_like(dg_ref)

        causal, tril, triu = _tri_masks(lc)
        tril_strict = (tril.astype(F32) - (tril * triu).astype(F32)).astype(BF16)
        gtv, gttv, bcv, bctv = gt_ref[...], gtt_ref[...], bc_ref[...], bct_ref[...]
        lane = lax.broadcasted_iota(jnp.int32, (lc, ng), 1)
        dli_all = jnp.zeros((lc, ng), F32)
        from_later = jnp.zeros((lc, ng), F32)
        from_earlier = jnp.zeros((lc, ng), F32)
        across_all = jnp.zeros((1, ng), F32)
        old = [(dcs[h], dns[h]) for h in range(heads)]
        new, d_o, d_z, d_g, dqs, dks, dvs = [], [], [], [], [], [], []
        for h in range(heads):
            hs = slice(h * hd, (h + 1) * hd)
            li_c, li_r, gf_c, b_c, b_r = _chunk_gates(gtv, gttv, bcv, bctv, h, heads)
            m_prev = mst_ref[0, h][:, 0:1]
            m_t, w_intra, w_inter, _, w_state, decay = _chunk_weights(li_c, li_r, b_c, b_r, m_prev, causal)
            qb = qkv_ref[0, :, hs]
            qf = qb.astype(F32)
            ks = qkv_ref[1, :, hs].astype(F32) * kscale
            kb = _bf(ks)
            vb = qkv_ref[2, :, hs]
            c_b = cst_ref[0, h]
            n_old = nst_ref[0, h]
            s = _dot_nt(qb, kb) * w_intra
            den = _rowsum(s) + w_inter * _rowsum(qf * n_old)
            floor = jnp.exp(-m_t)
            dstab = jnp.maximum(jnp.abs(den), floor)
            cell = cell_ref[:, hs]
            o = o_ref[:, hs]
            so = _sigmoid(o)
            hm = so * cell
            rinv = lax.rsqrt(jnp.mean(hm * hm, axis=-1, keepdims=True) + EPS)
            hn = hm * rinv
            z = z_ref[:, hs]
            sgz = _sigmoid(z)
            sz = z * sgz
            gh = g_ref[:, hs]
            dy = dy_ref[0, :, hs]
            d_z.append(_bf(dy * (hn * gh) * _dsilu(z, sgz)))
            d_g.append(_colsum(dy * hn * sz))
            dhn = dy * gh * sz
            dhm = rinv * (dhn - hn * jnp.mean(dhn * hn, axis=-1, keepdims=True))
            d_o.append(_bf(dhm * cell * so * (1.0 - so)))
            dcell = dhm * so
            dnum = dcell / dstab
            dnb = _bf(dnum)
            dden = -_rowsum(dcell * cell) / dstab * jnp.where(jnp.abs(den) > floor, jnp.where(den > 0.0, 1.0, -1.0), 0.0)
            dst = _dot_nt(dnb, vb) + dden
            dsdb = _bf(dst * w_intra)
            dc_out, dn_out = old[h]
            dcb = _bf(dc_out)
            dq_inter = w_inter * (_dot_nt(dnb, c_b) + dden * n_old)
            dk_inter = w_state * (_dot_nt(vb, dcb) + dn_out)
            dq = _dot(dsdb, kb) + dq_inter
            dk = _dot_tn(dsdb, qb) + dk_inter
            dv = _dot_tn(_bf(s), dnb) + _dot(_bf(ks * w_state), dcb)
            wq = w_inter * qf
            new.append((decay * dc_out + _dot_tn(_bf(wq), dnb), decay * dn_out + _colsum(wq * dden)))
            pmat = dst * s
            p_rows = _rowsum(pmat)
            p_cols = _rowsum(pmat.T)
            q_in = _rowsum(qf * dq_inter)
            k_in = _rowsum(ks * dk_inter)
            across = decay * (jnp.sum(dc_out * c_b.astype(F32), keepdims=True) + jnp.sum(dn_out * n_old, keepdims=True))
            dli_all = dli_all + jnp.where(lane == h, p_cols + k_in, 0.0)
            from_later = from_later + jnp.where(lane == heads + h, p_rows - p_cols + q_in, 0.0)
            from_earlier = from_earlier + jnp.where(lane == heads + h, k_in, 0.0)
            across_all = across_all + jnp.where(lane[0:1] == heads + h, across, 0.0)
            dqs.append(dq)
            dks.append(dk * kscale)
            dvs.append(dv)
        for h in range(heads):
            dcs[h], dns[h] = new[h]
        du_ref[0] = jnp.concatenate(d_o, axis=1)
        du_ref[1] = jnp.concatenate(d_z, axis=1)
        dg_ref[...] += jnp.concatenate(d_g, axis=1)
        dlf = _tri_dot_left(triu, from_later) + _tri_dot_left(tril_strict, from_earlier) + across_all
        dgt = dli_all + dlf * _sigmoid(-gtv)
        dgt_ref[...] = dgt
        dbif_ref[...] += _colsum(dgt)
        dgb = _bf(dgt)
        dqkv_ref[0] = _bf(jnp.concatenate(dqs, axis=1) + _dot_nt(dgb, wif_ref[0:d, :]))
        dqkv_ref[1] = _bf(jnp.concatenate(dks, axis=1) + _dot_nt(dgb, wif_ref[d:2 * d, :]))
        dqkv_ref[2] = _bf(jnp.concatenate(dvs, axis=1) + _dot_nt(dgb, wif_ref[2 * d:3 * d, :]))

    rev = lambda c: nc - 1 - c
    row = pl.BlockSpec((lc, d), lambda c: (rev(c), 0))
    gcol = pl.BlockSpec((lc, ng), lambda c: (rev(c), 0))
    grow = pl.BlockSpec((ng, lc), lambda c: (0, rev(c)))
    return _pcall_ride(
        body, ride, name="mlstm_bwd", grid=(nc,),
        in_specs=[pl.BlockSpec((3, lc, d), lambda c: (0, rev(c), 0)), gcol, grow, gcol, grow,
                  pl.BlockSpec((1, heads, hd, hd), lambda c: (rev(c), 0, 0, 0)),
                  pl.BlockSpec((1, heads, 1, hd), lambda c: (rev(c), 0, 0, 0)),
                  pl.BlockSpec((1, heads, 1, 128), lambda c: (rev(c), 0, 0, 0)),
                  row, pl.BlockSpec((lc, d), lambda c: (rev(c), 3)), pl.BlockSpec((lc, d), lambda c: (rev(c), 4)),
                  pl.BlockSpec((1, d), lambda c: (0, 0)), pl.BlockSpec((1, lc, d), lambda c: (1, rev(c), 0)),
                  pl.BlockSpec((3 * d, ng), lambda c: (0, 0))],
        out_specs=[pl.BlockSpec((3, lc, d), lambda c: (0, rev(c), 0)), pl.BlockSpec((lc, ng), lambda c: (rev(c), 0)),
                   pl.BlockSpec((1, ng), lambda c: (0, 0)), pl.BlockSpec((2, lc, d), lambda c: (0, rev(c), 0)),
                   pl.BlockSpec((1, d), lambda c: (0, 0))],
        out_shape=[jax.ShapeDtypeStruct((3, s_len, d), BF16), jax.ShapeDtypeStruct((s_len, ng), F32),
                   jax.ShapeDtypeStruct((1, ng), F32), jax.ShapeDtypeStruct((5, s_len, d), BF16),
                   jax.ShapeDtypeStruct((1, d), F32)],
        scratch_shapes=[pltpu.VMEM((heads, hd, hd), F32), pltpu.VMEM((heads, 1, hd), F32)],
        compiler_params=_seq(),
        args=(qkv, *gates, cst, nst, mst, cell, u, u, ml_g, d_ycat, wif_b))


def _conv_bwd_tile(dp, later, taps, cw_ref, gw_ref, gb_ref):
    tm = dp.shape[0]
    dwin = jnp.concatenate([dp, later[...]], axis=0)
    later[...] = dp[0:HALO]
    acc = cw_ref[CONV_WIDTH - 1:CONV_WIDTH, :] * dp
    for k in range(CONV_WIDTH):
        if k < CONV_WIDTH - 1:
            acc = acc + cw_ref[k:k + 1, :] * _shift_up(dwin, CONV_WIDTH - 1 - k)[0:tm]
        gw_ref[k:k + 1, :] += _colsum(dp * taps[k])
    gb_ref[...] += _colsum(dp)
    return acc


def _ml_pre_bwd(dqkv, u, conv_w, conv_b, wqkv_b, du):
    s_len = u.shape[0]
    d = conv_w.shape[1]
    _, heads, hd, _ = wqkv_b.shape
    tm = _tile(s_len, ROWS_VECTOR)
    per = tm // HALO
    nt = s_len // tm

    def body(dqkv_ref, x_ref, xp_ref, cw_ref, cb_ref, w_ref, _, dx_ref, gw_ref, gcw_ref, gcb_ref, later, dps, dxs):
        i = pl.program_id(0)

        @pl.when(i == 0)
        def _():
            gw_ref[...] = jnp.zeros_like(gw_ref)
            gcw_ref[...] = jnp.zeros_like(gcw_ref)
            gcb_ref[...] = jnp.zeros_like(gcb_ref)
            later[...] = jnp.zeros_like(later)

        prev = jnp.where(i == nt - 1, 0.0, xp_ref[...])
        xm = x_ref[...]
        taps = _conv_taps(jnp.concatenate([prev, xm], axis=0))
        pre = _conv_fwd(taps, cw_ref, cb_ref)
        sg = _sigmoid(pre)
        xcb = _bf(pre * sg)
        xmb = _bf(xm)
        for h in range(heads):
            hs = slice(h * hd, (h + 1) * hd)
            dqh, dkh, dvh = dqkv_ref[0, :, hs], dqkv_ref[1, :, hs], dqkv_ref[2, :, hs]
            dxc = _dot_nt(dqh, w_ref[0, h]) + _dot_nt(dkh, w_ref[1, h])
            dps[:, hs] = dxc * _dsilu(pre[:, hs], sg[:, hs])
            dxs[:, hs] = _dot_nt(dvh, w_ref[2, h])
            gw_ref[0, h] += _dot_tn(xcb[:, hs], dqh)
            gw_ref[1, h] += _dot_tn(xcb[:, hs], dkh)
            gw_ref[2, h] += _dot_tn(xmb[:, hs], dvh)
        dx_ref[0] = _bf(_conv_bwd_tile(dps[...], later, taps, cw_ref, gcw_ref, gcb_ref) + dxs[...])

    rev = lambda i: nt - 1 - i
    vec = pl.BlockSpec((1, d), lambda i: (0, 0))
    cwb = pl.BlockSpec((CONV_WIDTH, d), lambda i: (0, 0))
    whole4 = pl.BlockSpec(wqkv_b.shape, lambda i: (0, 0, 0, 0))
    return _pcall(
        body, name="ml_pre_bwd", grid=(nt,),
        in_specs=[pl.BlockSpec((3, tm, d), lambda i: (0, rev(i), 0)), pl.BlockSpec((tm, d), lambda i: (rev(i), 2)),
                  pl.BlockSpec((HALO, d), lambda i: (jnp.maximum(rev(i) * per - 1, 0), 2)),
                  cwb, vec, whole4, pl.BlockSpec(memory_space=pl.ANY)],
        out_specs=[pl.BlockSpec((1, tm, d), lambda i: (DU_PLANE[2], rev(i), 0)), whole4, cwb, vec],
        out_shape=[jax.ShapeDtypeStruct(du.shape, BF16), jax.ShapeDtypeStruct(wqkv_b.shape, F32),
                   jax.ShapeDtypeStruct((CONV_WIDTH, d), F32), jax.ShapeDtypeStruct((1, d), F32)],
        scratch_shapes=[pltpu.VMEM((HALO, d), F32), pltpu.VMEM((tm, d), F32), pltpu.VMEM((tm, d), F32)],
        input_output_aliases={6: 0},
        compiler_params=_seq(),
    )(dqkv, u, u, conv_w, conv_b, wqkv_b, du)


def _rg_bwd(d_ycat, u, hh, conv_w, conv_b, wa_b, ba, wx_b, bx, lam, du):
    s_len = u.shape[0]
    d = conv_w.shape[1]
    heads, hd, _ = wa_b.shape
    tm = _tile(s_len, ROWS_VECTOR)
    per = tm // HALO
    nt = s_len // tm

    def body(dy_ref, x_ref, xp_ref, z_ref, hh_ref, hp_ref, cw_ref, cb_ref, wa_ref, ba_ref, wx_ref, bx_ref, lam_ref, _,
             du_ref, gwa_ref, gwx_ref, gba_ref, gbx_ref, glam_ref, gcw_ref, gcb_ref, carry, gbuf, later, dxcs):
        i = pl.program_id(0)
        first = i == nt - 1

        @pl.when(i == 0)
        def _():
            carry[...] = jnp.zeros_like(carry)
            later[...] = jnp.zeros_like(later)
            gwa_ref[...] = jnp.zeros_like(gwa_ref)
            gwx_ref[...] = jnp.zeros_like(gwx_ref)
            gba_ref[...] = jnp.zeros_like(gba_ref)
            gbx_ref[...] = jnp.zeros_like(gbx_ref)
            glam_ref[...] = jnp.zeros_like(glam_ref)
            gcw_ref[...] = jnp.zeros_like(gcw_ref)
            gcb_ref[...] = jnp.zeros_like(gcb_ref)

        prev = jnp.where(first, 0.0, xp_ref[...])
        taps = _conv_taps(jnp.concatenate([prev, x_ref[...]], axis=0))
        xc = _conv_fwd(taps, cw_ref, cb_ref)
        r, ig, sp, log_a, a, mult = _rg_gates(xc, wa_ref, ba_ref, wx_ref, bx_ref, lam_ref)
        z = z_ref[...]
        sgz = _sigmoid(z)
        dy = dy_ref[0]
        hh_v = hh_ref[...]
        du_ref[1] = _bf(dy * hh_v * _dsilu(z, sgz))
        dhh = dy * (z * sgz)
        rows = lax.broadcasted_iota(jnp.int32, a.shape, 0)
        coef = jnp.where(rows == tm - 1, carry[1:2, :], _shift_up(a, 1))
        ca, cu = _scan_groups(coef, dhh, reverse=True)
        c = carry[0:1, :]
        for j in range(tm // 8 - 1, -1, -1):
            blk = ca[j * 8:(j + 1) * 8] * c + cu[j * 8:(j + 1) * 8]
            gbuf[j * 8:(j + 1) * 8, :] = blk
            c = blk[0:1]
        carry[0:1, :] = c
        carry[1:2, :] = a[0:1]
        g = gbuf[...]
        hprev_tile = jnp.where(first, 0.0, hp_ref[...])
        hprev = _shift_down(jnp.concatenate([hprev_tile, hh_v], axis=0), 1)[HALO:]
        da = g * hprev
        gx_ = g * xc
        d_mult = gx_ * ig
        d_ig = gx_ * mult
        dxc = g * mult * ig
        dlog_a = da * a - d_mult * (a * a / mult)
        d_r = dlog_a * ((-RG_C) * sp)
        glam_ref[...] += _colsum(dlog_a * ((-RG_C) * r)) * (-_sigmoid(-lam_ref[...]))
        d_ga = d_r * r * (1.0 - r)
        d_gx = d_ig * ig * (1.0 - ig)
        gba_ref[...] += _colsum(d_ga)
        gbx_ref[...] += _colsum(d_gx)
        xb = _bf(xc)
        dgab = _bf(d_ga)
        dgxb = _bf(d_gx)
        for h in range(heads):
            hs = slice(h * hd, (h + 1) * hd)
            dxcs[:, hs] = dxc[:, hs] + _dot_nt(dgab[:, hs], wa_ref[h]) + _dot_nt(dgxb[:, hs], wx_ref[h])
            gwa_ref[h] += _dot_tn(xb[:, hs], dgab[:, hs])
            gwx_ref[h] += _dot_tn(xb[:, hs], dgxb[:, hs])
        du_ref[0] = _bf(_conv_bwd_tile(dxcs[...], later, taps, cw_ref, gcw_ref, gcb_ref))

    assert DU_PLANE[0] % 2 == 0 and DU_PLANE[1] == DU_PLANE[0] + 1
    rev = lambda i: nt - 1 - i
    row = pl.BlockSpec((tm, d), lambda i: (rev(i), 0))
    halo_prev = lambda col: pl.BlockSpec((HALO, d), lambda i: (jnp.maximum(rev(i) * per - 1, 0), col))
    vec = pl.BlockSpec((1, d), lambda i: (0, 0))
    cwb = pl.BlockSpec((CONV_WIDTH, d), lambda i: (0, 0))
    whole3 = lambda a: pl.BlockSpec(a.shape, lambda i: (0, 0, 0))
    return _pcall(
        body, name="rg_bwd", grid=(nt,),
        in_specs=[pl.BlockSpec((1, tm, d), lambda i: (0, rev(i), 0)), row, halo_prev(0),
                  pl.BlockSpec((tm, d), lambda i: (rev(i), 1)), row, halo_prev(0),
                  cwb, vec, whole3(wa_b), vec, whole3(wx_b), vec, vec, pl.BlockSpec(memory_space=pl.ANY)],
        out_specs=[pl.BlockSpec((2, tm, d), lambda i: (DU_PLANE[0] // 2, rev(i), 0)), whole3(wa_b), whole3(wa_b),
                   vec, vec, vec, cwb, vec],
        out_shape=[jax.ShapeDtypeStruct(du.shape, BF16), jax.ShapeDtypeStruct(wa_b.shape, F32),
                   jax.ShapeDtypeStruct(wa_b.shape, F32)] + [jax.ShapeDtypeStruct((1, d), F32)] * 3
        + [jax.ShapeDtypeStruct((CONV_WIDTH, d), F32), jax.ShapeDtypeStruct((1, d), F32)],
        scratch_shapes=[pltpu.VMEM((8, d), F32), pltpu.VMEM((tm, d), F32), pltpu.VMEM((HALO, d), F32),
                        pltpu.VMEM((tm, d), F32)],
        input_output_aliases={13: 0},
        compiler_params=_seq(),
    )(d_ycat, u, u, u, hh, hh, conv_w, conv_b, wa_b, ba, wx_b, bx, lam, du)


def _in_bwd(du, w4, x, dxn, g, scale, ride=None):
    s_len, d = x.shape
    tm = _tile(s_len, ROWS_IN_BWD)
    nsh_chips, _, nsh = w4.shape
    npc = du.shape[0]
    ck = d // 4
    assert nsh % ck == 0 and npc * d == nsh_chips * nsh

    def body(du_ref, w_ref, x_ref, dxn_ref, g_ref, sc_ref, dx_ref, dsh_ref, dsc_ref, dg_ref):
        @pl.when(pl.program_id(0) == 0)
        def _():
            dsh_ref[...] = jnp.zeros_like(dsh_ref)
            dsc_ref[...] = jnp.zeros_like(dsc_ref)
            dg_ref[...] = jnp.zeros_like(dg_ref)

        dh = None
        for q in range(npc * d // ck):
            col = q * ck
            p, pc = col // d, col % d
            s, sc = col // nsh, col % nsh
            t = _dot_nt(du_ref[DU_PLANE[p], :, pc:pc + ck], w_ref[s, :, sc:sc + ck])
            dh = t if dh is None else dh + t
        xv = x_ref[...]
        r = lax.rsqrt(jnp.mean(xv * xv, axis=-1, keepdims=True) + EPS)
        xn = xv * r
        gv = g_ref[...]
        onesc = 1.0 + sc_ref[...]
        dsh_ref[...] += _colsum(dh)
        dsc_ref[...] += _colsum(dh * (xn * gv))
        dg_ref[...] += _colsum(dh * xn * onesc)
        dxh = dh * (gv * onesc)
        dx_ref[...] = dxn_ref[...] + r * (dxh - xn * jnp.mean(dxh * xn, axis=-1, keepdims=True))

    row = pl.BlockSpec((tm, d), lambda i: (i, 0))
    vec = pl.BlockSpec((1, d), lambda i: (0, 0))
    return _pcall_ride(
        body, ride, name="in_bwd", grid=(s_len // tm,),
        in_specs=[pl.BlockSpec((npc, tm, d), lambda i: (0, i, 0)), pl.BlockSpec(w4.shape, lambda i: (0, 0, 0)), row, row,
                  vec, vec],
        out_specs=[row, vec, vec, vec],
        out_shape=[jax.ShapeDtypeStruct((s_len, d), F32)] + [jax.ShapeDtypeStruct((1, d), F32)] * 3,
        compiler_params=_seq(),
        args=(du, w4, x, dxn, g, scale))


def _layer_fwd(x, p, rides=None, loss_head=None):
    rides = rides or {}
    landed = {}
    ride = lambda kernel: rides[kernel](landed) if kernel in rides else None
    (h_b, u), landed["ln_inproj"] = _ln_inproj(x, p["norm_g"], p["scale"], p["shift"], p["w4"], ride("ln_inproj"))
    (hh, ycat), landed["rg_fwd"] = _rg_fwd(u, p["rg_conv_w"], p["rg_conv_b"], p["rg_wa_b"], p["rg_ba"], p["rg_wx_b"],
                                           p["rg_bx"], p["rg_lam"], ride("rg_fwd"))
    if "late" in rides:
        p = {**p, **rides["late"](landed)}
    qkv, *gates = _ml_pre(u, p["ml_conv_w"], p["ml_conv_b"], p["wqkv_b"], p["wif_b"], p["wift_b"], p["b_if"],
                          p["b_ift"])
    (cell, ycat, cst, nst, mst), landed["mlstm_fwd"] = _mlstm_fwd(qkv, gates, u, p["ml_g"], ycat, ride("mlstm_fwd"))
    if loss_head is None:
        (y, x_new), landed["out_proj"] = _out_proj(ycat, p["w_out_b"], x, p["gate"], ride("out_proj"))
    else:
        y, *x_new = _out_proj_loss(ycat, p["w_out_b"], x, p["gate"], *loss_head)
    saved = dict(x=x, h_b=h_b, u=u, hh=hh, qkv=qkv, gates=gates, cell=cell, ycat=ycat, cst=cst, nst=nst, mst=mst, y=y)
    return x_new, saved, p, landed


def _layer_bwd(dxn, p, s, rides=None):
    rides = rides or {}
    landed = {}
    ride = lambda kernel: rides[kernel](grads, landed) if kernel in rides else None
    u = s["u"]
    d = dxn.shape[1]
    d_gate, dy_b, d_ycat = _out_bwd(dxn, s["y"], p["gate"], p["w_out_b"])
    grads = dict(w_out=_grad_matmul(s["ycat"], dy_b[None], 2, lambda b: b, lambda b: 0, (2 * d, d), (d, d),
                                    lambda b: (b, 0))[0])
    (dqkv, dgt, g_b_if, du, g_ml_g), landed["mlstm_bwd"] = _mlstm_bwd(
        s["qkv"], s["gates"], s["cst"], s["nst"], s["mst"], s["cell"], u, p["ml_g"], d_ycat, p["wif_b"],
        ride("mlstm_bwd"))
    ng = dgt.shape[1]
    g_w_if = _grad_matmul(s["qkv"], _bf(dgt)[None], 3, lambda b: b, lambda b: 0, (3 * d, ng), (d, ng),
                          lambda b: (b, 0))[0][0]
    du, g_wqkv, g_ml_cw, g_ml_cb = _ml_pre_bwd(dqkv, u, p["ml_conv_w"], p["ml_conv_b"], p["wqkv_b"], du)
    du, g_wa, g_wx, g_ba, g_bx, g_lam, g_rg_cw, g_rg_cb = _rg_bwd(d_ycat, u, s["hh"], p["rg_conv_w"], p["rg_conv_b"],
                                                                  p["rg_wa_b"], p["rg_ba"], p["rg_wx_b"], p["rg_bx"],
                                                                  p["rg_lam"], du)
    grads.update(rg_conv_w=g_rg_cw, rg_conv_b=g_rg_cb, rg_w_a=g_wa, rg_b_a=g_ba, rg_w_x=g_wx, rg_b_x=g_bx,
                 rg_lambda=g_lam, ml_conv_w=g_ml_cw, ml_conv_b=g_ml_cb, ml_w_qkv=g_wqkv, ml_w_if=g_w_if, ml_b_if=g_b_if,
                 ml_norm_g=g_ml_g)
    npc = du.shape[0]
    grads["w_in"], landed["grad_w_in"] = _grad_matmul(
        s["h_b"][None], du, npc, lambda b: 0, lambda b: (b + DU_PLANE[0]) % npc, (d, npc * d), (d, d),
        lambda b: (0, b), ride("grad_w_in"))
    (dx, d_shift, d_scale, grads["norm_g"]), landed["in_bwd"] = _in_bwd(du, p["w4"], s["x"], dxn, p["norm_g"],
                                                                        p["scale"], ride("in_bwd"))
    return dx, grads, jnp.concatenate([d_shift, d_scale, d_gate], axis=1), landed


def _me():
    return lax.axis_index("x"), lax.axis_index("y"), lax.axis_index("c")


def _remote(src, dst, send_sem, recv_sem, to):
    return pltpu.make_async_remote_copy(src_ref=src, dst_ref=dst, send_sem=send_sem, recv_sem=recv_sem,
                                        device_id=to, device_id_type=MESH)


def _all_gather8(blocks, space):
    n = len(blocks)
    relay = [b.size * b.dtype.itemsize >= RELAY_BYTES and b.shape[0] % 32 == 0 for b in blocks]

    def body(*refs):
        x_refs, out_refs = refs[:n], refs[n:2 * n]
        send_sems, recv_sems, local_sems = refs[2 * n:]
        x, y, c = _me()
        me, sibling = (x, y, c), (x, y, 1 - c)
        by_x, by_y, across = (1 - x, y, c), (x, 1 - y, c), (1 - x, 1 - y, c)

        def rows(i, blk, part=None):
            m_per = blocks[i].shape[0]
            at = (4 * blk[0] + 2 * blk[1] + blk[2]) * m_per
            if part is not None:
                m_per //= 2
                at += part * m_per
            return out_refs[i].at[pl.ds(at, m_per), :]

        def copy(i, k, blk, to, src=None, part=None):
            return _remote(rows(i, blk, part) if src is None else src, rows(i, blk, part), send_sems.at[8 * i + k],
                           recv_sems.at[8 * i + k], to)

        mine = [pltpu.make_async_copy(x_refs[i], rows(i, me), local_sems.at[i]) for i in range(n)]
        first = []
        for i in range(n):
            first.append(copy(i, 0, me, sibling, src=x_refs[i]))
            first += [copy(i, 1, me, by_x, src=x_refs[i]), copy(i, 2, me, by_y, src=x_refs[i])]
            if not relay[i]:
                first.append(copy(i, 3, me, across, src=x_refs[i]))
        for cp in mine + first:
            cp.start()
        passed = []
        for i in range(n):
            copy(i, 1, by_x, me).wait_recv()
            passed.append(copy(i, 4, by_x, sibling))
            if relay[i]:
                passed.append(copy(i, 3, by_x, by_y, part=0))
        for cp in passed:
            cp.start()
        n_x = len(passed)
        for i in range(n):
            copy(i, 2, by_y, me).wait_recv()
            passed.append(copy(i, 5, by_y, sibling))
            if relay[i]:
                passed.append(copy(i, 7, by_y, by_x, part=1))
        for cp in passed[n_x:]:
            cp.start()
        for i in range(n):
            if relay[i]:
                copy(i, 3, across, me, part=0).wait_recv()
                copy(i, 7, across, me, part=1).wait_recv()
            else:
                copy(i, 3, across, me).wait_recv()
            passed.append(copy(i, 6, across, sibling))
            passed[-1].start()
        for i in range(n):
            copy(i, 0, sibling, me).wait_recv()
            for k, blk in ((4, by_x), (5, by_y), (6, across)):
                copy(i, k, (blk[0], blk[1], 1 - c), me).wait_recv()
        for cp in first + passed:
            cp.wait_send()
        for cp in mine:
            cp.wait()

    spec = pl.BlockSpec(memory_space=space)
    return _pcall(
        body, name="all_gather8",
        out_shape=[jax.ShapeDtypeStruct((8 * b.shape[0], b.shape[1]), b.dtype) for b in blocks],
        in_specs=[spec] * n, out_specs=[spec] * n,
        scratch_shapes=[pltpu.SemaphoreType.DMA((8 * n,)), pltpu.SemaphoreType.DMA((8 * n,)),
                        pltpu.SemaphoreType.DMA((n,))],
    )(*blocks)


def _exchange(legs):
    n = len(legs)

    def body(*refs):
        copies, local, relays = _exchange_body(legs, refs[:n], refs[n:2 * n], *refs[2 * n:])
        for cp in copies + local:
            cp.start()
        _hand_on(relays)
        _wait_all(copies, local, relays)

    hbm = pl.BlockSpec(memory_space=pltpu.HBM)
    return _pcall(body, name="exchange", out_shape=[leg.landing() for leg in legs], in_specs=[hbm] * n,
                  out_specs=[hbm] * n, input_output_aliases=_exchange_aliases(legs, 0, 0),
                  scratch_shapes=_exchange_sems(legs))(*[leg.src for leg in legs])


def _row_tile(rows, cap=4096, mult=16):
    best = None
    for t in range(mult, min(rows, cap) + 1, mult):
        if rows % t == 0:
            best = t
    return rows if best is None else best


def _pair_sum(half, own, own_spec, got, got_spec, out_shape, out_spec, grid):
    def body(_, a_ref, b_ref, o_ref):
        o_ref[...] = (a_ref[...] + b_ref[...].astype(F32)).astype(o_ref.dtype)

    return _pcall(
        body, name="pair_sum",
        grid_spec=pltpu.PrefetchScalarGridSpec(num_scalar_prefetch=1, grid=grid, in_specs=[own_spec, got_spec],
                                               out_specs=out_spec),
        out_shape=out_shape, compiler_params=_seq(len(grid)))(half, own, got)


def _chip_sum(ids, part, met, fill, layer=0, stack=1):
    _, _, rows, n = part.shape
    tr = _row_tile(rows, cap=max(16, BLOCK_ELEMS // n))
    first = isinstance(stack, int)

    def body(_, own_ref, a_ref, b_ref, c_ref, *rest):
        acc = own_ref[...].astype(F32) + a_ref[...].astype(F32)
        acc = acc + b_ref[...].astype(F32)
        rest[-1][...] = acc + c_ref[...].astype(F32)

    blk = (None, None, tr, n)
    other = lambda k: pl.BlockSpec(blk, lambda j, ids: ((ids[0] + k) % 4, 0, j, 0))
    in_specs = [pl.BlockSpec(blk, lambda j, ids: (ids[0], 0, j, 0)), other(1), other(2), other(3)]
    return _pcall(
        body, name="chip_sum",
        grid_spec=pltpu.PrefetchScalarGridSpec(
            num_scalar_prefetch=1, grid=(rows // tr,),
            in_specs=in_specs if first else in_specs + [pl.BlockSpec(memory_space=pl.ANY)],
            out_specs=pl.BlockSpec(blk, lambda j, ids: (layer, ids[1] if fill else 0, j, 0))),
        out_shape=jax.ShapeDtypeStruct(((stack,) if first else stack.shape[:1]) + (2 if fill else 1, rows, n), F32),
        input_output_aliases={} if first else {5: 0},
        compiler_params=_seq())(*((ids, part, met, met, met) if first else (ids, part, met, met, met, stack)))


def _ada_mod(c_all, w_ada, b_ada_cols):
    depth, d, n = w_ada.shape
    nb = c_all.shape[0]

    def body(c_ref, w_ref, b_ref, o_ref):
        cv = c_ref[...]
        ca = _bf(cv * _sigmoid(cv))
        o_ref[0] = _dot(ca, _bf(w_ref[0])) + b_ref[0]

    return _pcall(body, name="ada_mod", grid=(depth,),
                  in_specs=[pl.BlockSpec((nb, d), lambda l: (0, 0)), pl.BlockSpec((1, d, n), lambda l: (l, 0, 0)),
                            pl.BlockSpec((1, 1, n), lambda l: (l, 0, 0))],
                  out_specs=pl.BlockSpec((1, nb, n), lambda l: (l, 0, 0)),
                  out_shape=jax.ShapeDtypeStruct((depth, nb, n), F32), compiler_params=_seq())(c_all, w_ada, b_ada_cols)


def _ada_grad(c_all, dmod_cols, rows_all):
    nb, d = c_all.shape
    depth, _, n = dmod_cols.shape
    kinds, n_all = rows_all.shape[1], rows_all.shape[3]

    def body(c_ref, dm_ref, da_ref, gw_ref, gb_ref):
        cv = c_ref[...]
        ca = _bf(cv * _sigmoid(cv))
        gw_ref[0] = _dot_tn(ca, _bf(dm_ref[0]))
        for k in range(kinds):
            gb_ref[0, k] = _colsum(da_ref[0, k])

    return _pcall(body, name="ada_grad", grid=(depth,),
                  in_specs=[pl.BlockSpec((nb, d), lambda l: (0, 0)), pl.BlockSpec((1, nb, n), lambda l: (l, 0, 0)),
                            pl.BlockSpec((1, kinds, nb, n_all), lambda l: (l, 0, 0, 0))],
                  out_specs=[pl.BlockSpec((1, d, n), lambda l: (l, 0, 0)),
                             pl.BlockSpec((1, kinds, 1, n_all), lambda l: (l, 0, 0, 0))],
                  out_shape=[jax.ShapeDtypeStruct((depth, d, n), F32), jax.ShapeDtypeStruct((depth, kinds, 1, n_all), F32)],
                  compiler_params=_seq())(c_all, dmod_cols, rows_all)


def _adamw(items, ride=None):
    two_d = [tuple(t.reshape(w.size // w.shape[-1], w.shape[-1]) for t in (w, g, m, v)) for w, g, m, v in items]
    n = len(items)
    if n == 1:
        rows, cols = two_d[0][0].shape
        tr = _row_tile(rows, cap=max(8, BLOCK_ELEMS // cols), mult=8)
        blocks = [pl.BlockSpec((tr, cols), lambda i: (i, 0))]
        grid = (rows // tr,)
    else:
        blocks = [pl.BlockSpec(t[0].shape, lambda i: (0, 0)) for t in two_d]
        grid = (1,)

    def body(*refs):
        for k in range(n):
            w_ref, g_ref, m_ref, v_ref = refs[4 * k:4 * k + 4]
            d_ref, mo_ref, vo_ref = refs[4 * n + 3 * k:4 * n + 3 * k + 3]
            gv = g_ref[...]
            mn = ADAM_B1 * m_ref[...] + (1.0 - ADAM_B1) * gv
            vn = ADAM_B2 * v_ref[...] + (1.0 - ADAM_B2) * (gv * gv)
            m_hat = mn / (1.0 - ADAM_B1 ** ADAM_STEP)
            v_hat = vn / (1.0 - ADAM_B2 ** ADAM_STEP)
            d_ref[...] = -ADAM_LR * (m_hat / (jnp.sqrt(v_hat) + ADAM_EPS) + ADAM_WD * w_ref[...])
            mo_ref[...] = mn
            vo_ref[...] = vn

    outs, got = _pcall_ride(
        body, ride, name="adamw", grid=grid,
        in_specs=[b for b in blocks for _ in range(4)], out_specs=[b for b in blocks for _ in range(3)],
        out_shape=[jax.ShapeDtypeStruct(t[0].shape, F32) for t in two_d for _ in range(3)],
        compiler_params=_seq(), args=tuple(a for t in two_d for a in t))
    return [tuple(o.reshape(items[k][0].shape) for o in outs[3 * k:3 * k + 3]) for k in range(n)], got


WEIGHTS = ["norm_g", "w_ada", "b_ada", "w_in", "rg_conv_w", "rg_conv_b", "rg_w_a", "rg_b_a", "rg_w_x", "rg_b_x",
           "rg_lambda", "ml_conv_w", "ml_conv_b", "ml_w_q", "ml_w_k", "ml_w_v", "ml_w_if", "ml_b_if", "ml_norm_g",
           "w_out", "final_g"]
SMALL_SHARDED = {"rg_conv_w": 1, "ml_conv_w": 1, "ml_w_if": 0}
REPLICATED = ["rg_w_a", "rg_w_x", "rg_conv_b", "rg_b_a", "rg_b_x", "rg_lambda", "ml_conv_b", "ml_norm_g", "ml_b_if"]
LANES = 128


def _to_pieces(g, axis):
    shp = g.shape
    g = g.reshape(shp[:axis] + (4, 2, shp[axis] // 8) + shp[axis + 1:])
    g = jnp.moveaxis(g, (axis, axis + 1), (0, 1))
    return g.reshape(4, 2, -1)


def _from_pieces(p, shard_shape, axis):
    k = p.shape[0]
    rest = shard_shape[:axis] + (shard_shape[axis] // k,) + shard_shape[axis + 1:]
    t = jnp.moveaxis(p.reshape((k,) + rest), 0, axis)
    return t.reshape(shard_shape)


def _pad_rows(flat, mult):
    n = flat.shape[-1]
    pad = (-n) % mult
    if pad:
        flat = jnp.concatenate([flat, jnp.zeros(flat.shape[:-1] + (pad,), flat.dtype)], axis=-1)
    return flat


def kernel(x, c, norm_g, w_ada, b_ada, w_in, rg_conv_w, rg_conv_b, rg_w_a, rg_b_a, rg_w_x, rg_b_x, rg_lambda, ml_conv_w, ml_conv_b, ml_w_q, ml_w_k, ml_w_v, ml_w_if, ml_b_if, ml_norm_g, w_out, final_g, loss_target, m_norm_g, m_w_ada, m_b_ada, m_w_in, m_rg_conv_w, m_rg_conv_b, m_rg_w_a, m_rg_b_a, m_rg_w_x, m_rg_b_x, m_rg_lambda, m_ml_conv_w, m_ml_conv_b, m_ml_w_q, m_ml_w_k, m_ml_w_v, m_ml_w_if, m_ml_b_if, m_ml_norm_g, m_w_out, m_final_g, v_norm_g, v_w_ada, v_b_ada, v_w_in, v_rg_conv_w, v_rg_conv_b, v_rg_w_a, v_rg_b_a, v_rg_w_x, v_rg_b_x, v_rg_lambda, v_ml_conv_w, v_ml_conv_b, v_ml_w_q, v_ml_w_k, v_ml_w_v, v_ml_w_if, v_ml_b_if, v_ml_norm_g, v_w_out, v_final_g):
    given = dict(locals())
    ax, ay, ac = lax.axis_index("x"), lax.axis_index("y"), lax.axis_index("c")
    chip = 2 * ax + ay
    me = 2 * chip + ac
    depth, d = norm_g.shape
    n_ada = w_ada.shape[2]
    pick = lambda a, i, axis=0: lax.dynamic_index_in_dim(a, i, axis, keepdims=False)

    convs = jnp.stack([rg_conv_w, ml_conv_w])
    n_conv = 2 * depth * CONV_WIDTH // 4
    blk = jnp.concatenate([c, convs.reshape(n_conv, d), jnp.zeros((8 - 1 - n_conv, d), F32)], axis=0)
    w_in_first = lax.dynamic_slice_in_dim(w_in[0], ac * (d // 2), d // 2, 0).astype(BF16)
    g0, w_in_first = _all_gather8([blk, w_in_first], pltpu.HBM)
    g0 = g0.reshape(8, 8, d)
    c_all = g0[:, 0, :]
    conv_full = g0[0::2, 1:1 + n_conv].reshape(4, 2, depth, CONV_WIDTH, d // 4)
    conv_full = conv_full.transpose(1, 2, 3, 0, 4).reshape(2, depth, CONV_WIDTH, d)

    b_cols = lax.dynamic_slice_in_dim(b_ada, chip * n_ada, n_ada, axis=1)[:, None, :]
    mod_part = _ada_mod(c_all, w_ada, b_cols)
    g1 = _all_gather8([mod_part.transpose(1, 0, 2).reshape(8, depth * n_ada)], pltpu.VMEM)[0]
    g1 = g1.reshape(8, 8, depth, n_ada)[0::2]
    mod_me = pick(g1.transpose(1, 2, 0, 3).reshape(8, depth, 4 * n_ada), me)

    def half_of(w, axis):
        n = w.shape[axis] // 2
        return lax.dynamic_slice_in_dim(w, ac * n, n, axis).astype(BF16)

    n_sh = w_in.shape[2]
    heads, hd_cut, hd = ml_w_q.shape[1:]

    def blocks_of(l):
        wqkv = jnp.stack([ml_w_q[l], ml_w_k[l], ml_w_v[l]])
        return [half_of(w_in[l], 0), half_of(w_out[l], 0), half_of(wqkv, 2).reshape(-1, hd), half_of(ml_w_if[l], 0)]

    def layer_of(l, w4, rest):
        return dict(
            norm_g=norm_g[l][None], shift=mod_me[l, 0:d][None], scale=mod_me[l, d:2 * d][None],
            gate=mod_me[l, 2 * d:3 * d][None], w4=w4.reshape(4, d, n_sh),
            rg_conv_w=conv_full[0, l], rg_conv_b=rg_conv_b[l][None], rg_wa_b=_bf(rg_w_a[l]), rg_ba=rg_b_a[l][None],
            rg_wx_b=_bf(rg_w_x[l]), rg_bx=rg_b_x[l][None], rg_lam=rg_lambda[l][None],
            ml_conv_w=conv_full[1, l], ml_conv_b=ml_conv_b[l][None], b_if=ml_b_if[l][None], b_ift=ml_b_if[l][:, None],
            ml_g=ml_norm_g[l][None], **rest)

    def rest_of(gathered):
        w_out_b, wqkv_g, wif = gathered
        return dict(w_out_b=w_out_b, wqkv_b=_from_pieces(wqkv_g.reshape(8, -1), (3, heads, hd, hd), 2), wif_b=wif,
                    wift_b=wif.T)

    spread = lambda blocks: [Leg(b, "spread") for b in blocks]
    fill = lambda landed: [Leg(t, "sib_fill") for t in landed]
    flat = lambda filled: [t.reshape(-1, t.shape[-1]) for t in filled]
    first = blocks_of(0)
    n_rest = len(first) - 1
    p = layer_of(0, w_in_first, {})
    layers, saved = [], []
    xl = x[0]
    for l in range(depth):
        nxt = blocks_of(l + 1) if l + 1 < depth else []
        skip = n_rest if l == 0 else 0
        rides = dict(rg_fwd=lambda landed, nxt=nxt: spread(nxt[:1]))
        if l == 0:
            rides.update(ln_inproj=lambda landed: spread(first[1:]),
                         rg_fwd=lambda landed, nxt=nxt: fill(landed["ln_inproj"]) + spread(nxt[:1]),
                         late=lambda landed: rest_of(flat(landed["rg_fwd"][:n_rest])))
        if nxt:
            rides.update(mlstm_fwd=lambda landed, nxt=nxt: spread(nxt[1:]),
                         out_proj=lambda landed, skip=skip: fill(list(landed["rg_fwd"][skip:]) + list(landed["mlstm_fwd"])))
        xl, s, p, landed = _layer_fwd(xl, p, rides, None if nxt else (final_g[None], loss_target[0]))
        layers.append(p)
        saved.append(s)
        if nxt:
            arrived = flat(landed["out_proj"])
            p = layer_of(l + 1, arrived[0], rest_of(arrived[1:]))
    dx, g_final, loss = xl

    half = ac.reshape(1)
    ids = jnp.stack([chip, ac])
    r_out = w_out.shape[1] // 2

    def pair_in(g_w_in, got_in):
        return _pair_sum(
            half, g_w_in, pl.BlockSpec((None, d // 2, n_sh), lambda s, h: (0, h[0], s)),
            got_in, pl.BlockSpec((None, None, d // 2, n_sh), lambda s, h: (0, s, 0, 0)),
            jax.ShapeDtypeStruct((4, 1, d // 2, n_sh), BF16),
            pl.BlockSpec((None, None, d // 2, n_sh), lambda s, h: (s, 0, 0, 0)), (4,))

    def pair_out(g_out5, got_out):
        return _pair_sum(
            half, g_out5, pl.BlockSpec((None, None, None, r_out, d), lambda s, h: (0, s, h[0], 0, 0)),
            got_out, pl.BlockSpec((None, None, r_out, d), lambda s, h: (0, s, 0, 0)),
            jax.ShapeDtypeStruct((4, 1, r_out, d), BF16),
            pl.BlockSpec((None, None, r_out, d), lambda s, h: (s, 0, 0, 0)), (4,))

    def pair_slab(slab, got, dtype):
        rows = got.shape[0] // 4
        blk = pl.BlockSpec((rows, LANES), lambda s, h: (s, 0))
        return _pair_sum(half, slab, pl.BlockSpec((None, rows, LANES), lambda s, h: (h[0], s, 0)), got, blk,
                         jax.ShapeDtypeStruct((4 * rows, LANES), dtype), blk, (4,)).reshape(4, 1, rows, LANES)

    row_pad = lambda n: -(-n // (8 * LANES)) * (8 * LANES)

    def as_rows(t):
        if t.shape[-1] == LANES and t.size % (8 * LANES) == 0:
            return t.reshape(-1, LANES)
        return _pad_rows(t.reshape(-1), 8 * LANES).reshape(-1, LANES)

    chips = lambda arrs: [Leg(a, "chips") for a in arrs]
    out5 = lambda g: g["w_out"].reshape(1, 4, 2, r_out, d)
    r_q = hd // 8
    qkv5 = lambda g: g["ml_w_qkv"].reshape(3 * heads, 4, 2, r_q, hd)

    def pair_qkv(g5, got):
        return _pair_sum(
            half, g5, pl.BlockSpec((3 * heads, None, None, r_q, hd), lambda s, h: (0, s, h[0], 0, 0)),
            got, pl.BlockSpec((3 * heads, None, r_q, hd), lambda s, h: (0, s, 0, 0)),
            jax.ShapeDtypeStruct((4, 1, 3 * heads, r_q, hd), BF16),
            pl.BlockSpec((None, None, 3 * heads, r_q, hd), lambda s, h: (s, 0, 0, 0, 0)), (4,))

    grads, dmods, parts, mets = [None] * depth, [None] * depth, [None] * depth, [None] * depth
    small = {}

    def early_exchange(g, landed):
        every = [g] + grads[1:]
        sm = jnp.concatenate([_to_pieces(every[l][name], axis) for l in range(depth)
                              for name, axis in SMALL_SHARDED.items()], axis=-1)
        sm = _pad_rows(sm, 16 * LANES)
        sm = sm.transpose(1, 0, 2).reshape(2, -1, LANES)
        rep = [as_rows(every[l][name]) for l in range(depth) for name in REPLICATED]
        rep = jnp.concatenate(rep + [as_rows(g_final), as_rows(loss)], axis=0)
        rep = jnp.concatenate([rep, jnp.zeros(((-rep.shape[0]) % 64, LANES), F32)], axis=0)
        rep = rep.reshape(4, 2, -1, LANES).transpose(1, 0, 2, 3).reshape(2, -1, LANES)
        got_sm, got_rep, got_q = _exchange([Leg(sm, "sib_slab"), Leg(rep, "sib_slab"), Leg(qkv5(g), "sib_w_out")])
        small["parts"] = [pair_out(out5(g), landed["mlstm_bwd"][0]), pair_slab(sm, got_sm, BF16),
                          pair_slab(rep, got_rep, F32), pair_qkv(qkv5(g), got_q)]
        return chips(small["parts"])

    def last_exchange(g, landed):
        (got_in,) = _exchange([Leg(g["w_in"], "sib_w_in")])
        small["part_in"] = pair_in(g["w_in"], got_in)
        return chips([small["part_in"]])

    for l in reversed(range(depth)):
        above = parts[l + 1] if l + 1 < depth else []
        rides = dict(mlstm_bwd=lambda g, landed, above=above: [Leg(out5(g), "sib_w_out")] + chips(above),
                     in_bwd=lambda g, landed: [Leg(g["w_in"], "sib_w_in"), Leg(qkv5(g), "sib_w_out")])
        if l == 0:
            rides.update(grad_w_in=early_exchange, in_bwd=last_exchange)
        dx, grads[l], dmods[l], got = _layer_bwd(dx, layers[l], saved[l], rides)
        if above:
            mets[l + 1] = got["mlstm_bwd"][1:]
        if l > 0:
            parts[l] = [pair_in(grads[l]["w_in"], got["in_bwd"][0]), pair_out(out5(grads[l]), got["mlstm_bwd"][0]),
                        pair_qkv(qkv5(grads[l]), got["in_bwd"][1])]
    part_out, part_sm, part_rep, part_q = small["parts"]
    met_out, met_sm, met_rep, met_q = got["grad_w_in"]
    parts[0], mets[0] = [small["part_in"], part_out, part_q], [got["in_bwd"][0], met_out, met_q]
    n_rep = part_rep.shape[2]

    pad = lambda t: jnp.concatenate([t, jnp.zeros((1, 2 * d), F32)], axis=1)
    rows = [r for l in range(depth) for r in (dmods[l], pad(grads[l]["norm_g"]))]
    blk = jnp.concatenate(rows + [jnp.zeros((8 - 2 * depth, 3 * d), F32)], axis=0)
    red_rep = _chip_sum(ids, part_rep, met_rep, False).reshape(n_rep, LANES)
    rows_all, rep_all = _all_gather8([blk, red_rep], pltpu.VMEM)
    rows_all, rep_all = rows_all.reshape(8, 8, 3 * d)[:, :2 * depth], rep_all.reshape(-1)
    rows_all = rows_all.transpose(1, 0, 2).reshape(depth, 2, 8, 3 * d)
    dm_cols = lax.dynamic_slice_in_dim(rows_all[:, 0], chip * n_ada, n_ada, axis=2)
    g_w_ada, summed = _ada_grad(c_all, dm_cols, rows_all)

    g = dict(w_ada=g_w_ada, b_ada=summed[:, 0, 0], norm_g=summed[:, 1, 0, :d])
    item = lambda name: (given[name], g[name], given["m_" + name], given["v_" + name])
    both_in, both_out, both_q = depth, depth, depth
    flat_q = lambda t: t.reshape(4, 1, 3 * heads * r_q, hd)
    for l in range(depth):
        both_in = _chip_sum(ids, parts[l][0], mets[l][0], True, l, both_in)
        both_out = _chip_sum(ids, parts[l][1], mets[l][1], True, l, both_out)
        both_q = _chip_sum(ids, flat_q(parts[l][2]), flat_q(mets[l][2]), True, l, both_q)
    both_in, both_out, both_q, both_sm = _exchange(fill([both_in, both_out, both_q,
                                                         _chip_sum(ids, part_sm, met_sm, True)]))

    g.update(w_in=both_in.reshape(w_in.shape), w_out=both_out.reshape(w_out.shape))
    g_qkv = both_q.reshape(depth, 2, 3, heads, r_q, hd).transpose(0, 2, 3, 1, 4, 5)
    g_qkv = g_qkv.reshape(depth, 3, heads, 2 * r_q, hd)
    for i, name in enumerate(["ml_w_q", "ml_w_k", "ml_w_v"]):
        g[name] = g_qkv[:, i]
    shard = both_sm.reshape(2, -1)
    off = 0
    per_layer = {name: [] for name in SMALL_SHARDED}
    for l in range(depth):
        for name, axis in SMALL_SHARDED.items():
            n = grads[l][name].size // 8
            per_layer[name].append(_from_pieces(shard[:, off:off + n], given[name].shape[1:], axis))
            off += n
    for name in SMALL_SHARDED:
        g[name] = jnp.stack(per_layer[name])
    off = 0
    per_layer = {name: [] for name in REPLICATED}
    for l in range(depth):
        for name in REPLICATED:
            n = given[name][l].size
            per_layer[name].append(rep_all[off:off + n].reshape(given[name].shape[1:]))
            off += row_pad(n)
    for name in REPLICATED:
        g[name] = jnp.stack(per_layer[name])
    g["final_g"] = rep_all[off:off + d]
    loss_all = rep_all[off + row_pad(d)]

    stepped = {}
    rg_mats, ml_mats = ["rg_w_a", "rg_w_x"], ["ml_w_q", "ml_w_k", "ml_w_v"]
    vectors = [n for n in WEIGHTS if n not in ["w_ada", "w_in", "w_out"] + rg_mats + ml_mats]
    for names in (["w_ada"], ["w_in"], ["w_out"], rg_mats, ml_mats, vectors):
        stepped.update(zip(names, _adamw([item(name) for name in names])[0]))
    deltas, new_m, new_v = zip(*[stepped[name] for name in WEIGHTS])
    return (loss_all, dx[None], *[g[name] for name in WEIGHTS], *deltas, *new_m, *new_v)
```

```python
import functools
from typing import NamedTuple

import jax
import jax.numpy as jnp
from jax import lax
from jax.experimental import pallas as pl
from jax.experimental.pallas import tpu as pltpu

F32 = jnp.float32
BF16 = jnp.bfloat16

EPS = 1e-6
RG_C = 8.0
CONV_WIDTH = 4
ML_CHUNK = 512
HALO = 8
ROWS_VECTOR = 512
ROWS_MATMUL = 1024
ROWS_IN_BWD = 512
ROWS_GRAD_MATMUL = 2048
ROWS_W_OUT = 512
BLOCK_ELEMS = 1 << 18
RELAY_BYTES = 1 << 18
ADAM_LR = 0.001
ADAM_B1 = 0.9
ADAM_B2 = 0.999
ADAM_EPS = 1e-08
ADAM_WD = 0.01
ADAM_STEP = 10
MESH = pl.DeviceIdType.MESH


def _pcall(body, **kw):
    return pl.pallas_call(body, **kw)


class Leg(NamedTuple):
    src: jax.Array
    kind: str

    def landing(self):
        a = self.src
        shape = {"chips": lambda: a.shape, "spread": lambda: (4, 2) + a.shape, "sib_fill": lambda: a.shape,
                 "sib_w_in": lambda: (a.shape[0], 4, a.shape[1] // 2, a.shape[2] // 4),
                 "sib_w_out": lambda: a.shape[:2] + a.shape[3:], "sib_slab": lambda: a.shape[1:]}[self.kind]()
        return jax.ShapeDtypeStruct(shape, a.dtype)

    def relayed(self):
        a = self.src
        return self.kind == "spread" and a.size * a.dtype.itemsize >= RELAY_BYTES and a.shape[0] % 32 == 0

    def copies(self, src, dst, x, y, c):
        a, me_s, o = self.src, 2 * x + y, 1 - c
        chips = [(1 - x, y), (x, 1 - y), (1 - x, 1 - y)]
        if self.kind == "chips":
            return [(src.at[2 * px + py], dst.at[me_s], (px, py, c)) for px, py in chips], [], []
        if self.kind == "spread":
            own = dst.at[me_s, c]
            if not self.relayed():
                return [(src, own, (px, py, c)) for px, py in chips], [(src, own)], []
            by_x, by_y, half = chips[0], chips[1], a.shape[0] // 2
            part = lambda chip, k: dst.at[2 * chip[0] + chip[1], c, pl.ds(k * half, half)]
            return ([(src, own, (*by_x, c)), (src, own, (*by_y, c))], [(src, own)],
                    [(part(by_x, 0), part(by_x, 0), (*by_y, c), 0), (part(by_y, 1), part(by_y, 1), (*by_x, c), 1)])
        depth = pl.ds(0, a.shape[0])
        if self.kind == "sib_fill":
            return [(dst.at[depth, c], dst.at[depth, c], (x, y, o))], [], []
        if self.kind == "sib_w_in":
            half, n = a.shape[1] // 2, a.shape[2] // 4
            return [(src.at[depth, pl.ds(o * half, half), pl.ds(s * n, n)], dst.at[depth, s], (x, y, o))
                    for s in range(4)], [], []
        if self.kind == "sib_w_out":
            return [(src.at[depth, pl.ds(0, 4), o], dst, (x, y, o))], [], []
        return [(src.at[o], dst, (x, y, o))], [], []

    def n_copies(self):
        return 4 if self.relayed() else {"chips": 3, "spread": 3, "sib_w_in": 4}.get(self.kind, 1)


def _exchange_body(legs, srcs, dsts, send_sems, recv_sems, local_sems):
    x, y, c = _me()
    remote, local, relays, k = [], [], [], 0
    for i, leg in enumerate(legs):
        far, near, handed = leg.copies(srcs[i], dsts[i], x, y, c)
        at = len(remote)
        for src, dst, to in far:
            remote.append(_remote(src, dst, send_sems.at[k], recv_sems.at[k], to))
            k += 1
        for src, dst, to, after in handed:
            relays.append((_remote(src, dst, send_sems.at[k], recv_sems.at[k], to), remote[at + after]))
            k += 1
        local += [pltpu.make_async_copy(src, dst, local_sems.at[i]) for src, dst in near]
    return remote, local, relays


def _hand_on(relays):
    for cp, after in relays:
        after.wait_recv()
        cp.start()


def _wait_all(copies, local, relays):
    arrived, handed = [after for _, after in relays], [cp for cp, _ in relays]
    for cp in [cp for cp in copies if not any(cp is a for a in arrived)] + handed:
        cp.wait_recv()
    for cp in copies + handed:
        cp.wait_send()
    for cp in local:
        cp.wait()


def _exchange_sems(legs):
    n = sum(leg.n_copies() for leg in legs)
    return [pltpu.SemaphoreType.DMA((n,)), pltpu.SemaphoreType.DMA((n,)), pltpu.SemaphoreType.DMA((len(legs),))]


def _exchange_aliases(legs, n_in, n_out):
    return {n_in + i: n_out + i for i, leg in enumerate(legs) if leg.kind == "sib_fill"}


def _pcall_ride(body, ride, *, grid, in_specs, out_specs, out_shape, args, scratch_shapes=(), **kw):
    n_in, n_out, n_scr = len(in_specs), len(out_specs), len(scratch_shapes)
    if not ride:
        res = _pcall(body, grid=grid, in_specs=in_specs, out_specs=out_specs, out_shape=out_shape,
                     scratch_shapes=list(scratch_shapes), **kw)(*args)
        return res, []
    nr = len(ride)

    def riding(*refs):
        ins, rsrc = refs[:n_in], refs[n_in:n_in + nr]
        outs, rdst = refs[n_in + nr:n_in + nr + n_out], refs[n_in + nr + n_out:n_in + 2 * nr + n_out]
        scr = refs[n_in + 2 * nr + n_out:n_in + 2 * nr + n_out + n_scr]
        copies, local, relays = _exchange_body(ride, rsrc, rdst, *refs[n_in + 2 * nr + n_out + n_scr:])
        at_step = lambda steps: functools.reduce(jnp.logical_and, [pl.program_id(a) == s for a, s in enumerate(steps)])

        @pl.when(at_step([0] * len(grid)))
        def _():
            for cp in copies + local:
                cp.start()

        body(*ins, *outs, *scr)

        if relays:
            @pl.when(at_step([grid[0] // 2] + [0] * (len(grid) - 1)))
            def _():
                _hand_on(relays)

        @pl.when(at_step([g - 1 for g in grid]))
        def _():
            _wait_all(copies, local, relays)

    hbm = pl.BlockSpec(memory_space=pltpu.HBM)
    aliases = {**kw.pop("input_output_aliases", {}), **_exchange_aliases(ride, n_in, n_out)}
    res = _pcall(
        riding, grid=grid, in_specs=list(in_specs) + [hbm] * nr, out_specs=list(out_specs) + [hbm] * nr,
        out_shape=list(out_shape) + [leg.landing() for leg in ride], input_output_aliases=aliases,
        scratch_shapes=list(scratch_shapes) + _exchange_sems(ride), **kw)(*args, *[leg.src for leg in ride])
    return res[:n_out], res[n_out:]


def _seq(n=1):
    return pltpu.CompilerParams(dimension_semantics=("arbitrary",) * n)


def _dot(a, b):
    return jnp.dot(a, b, preferred_element_type=F32)


def _dot_nt(a, b):
    return lax.dot_general(a, b, (((1,), (1,)), ((), ())), preferred_element_type=F32)


def _dot_tn(a, b):
    return lax.dot_general(a, b, (((0,), (0,)), ((), ())), preferred_element_type=F32)


def _bf(x):
    return x.astype(BF16)


def _sigmoid(x):
    return 0.5 * jnp.tanh(0.5 * x) + 0.5


def _log1p(z):
    u = 1.0 + z
    return jnp.where(u == 1.0, z, jnp.log(u) * (z / jnp.where(u == 1.0, 1.0, u - 1.0)))


def _softplus(x):
    return jnp.maximum(x, 0.0) + _log1p(jnp.exp(-jnp.abs(x)))


def _log_sigmoid(x):
    return -_softplus(-x)


def _one_minus_sq(a, log_a):
    x = 2.0 * log_a
    small = -x * (1.0 + x * (0.5 + x * (1.0 / 6.0)))
    return jnp.where(x > -0.004, small, 1.0 - a * a)


def _dsilu(x, s):
    return s * (1.0 + x * (1.0 - s))


def _rowsum(x):
    return jnp.sum(x, axis=1, keepdims=True)


def _colsum(x):
    return jnp.sum(x, axis=0, keepdims=True)


def _shift_down(win, s):
    return win if s == 0 else pltpu.roll(win, s, 0)


def _shift_up(win, s):
    return win if s == 0 else pltpu.roll(win, win.shape[0] - s, 0)


def _conv_taps(win):
    return [_shift_down(win, CONV_WIDTH - 1 - k)[HALO:] for k in range(CONV_WIDTH)]


def _conv_fwd(taps, w_ref, b_ref):
    acc = b_ref[...] + w_ref[CONV_WIDTH - 1:CONV_WIDTH, :] * taps[CONV_WIDTH - 1]
    for k in range(CONV_WIDTH - 1):
        acc = acc + w_ref[k:k + 1, :] * taps[k]
    return acc


def _split3(x):
    hi = _bf(x)
    r1 = x - hi.astype(F32)
    mid = _bf(r1)
    lo = _bf(r1 - mid.astype(F32))
    return hi, mid, lo


def _tri_dot_left(tri, x):
    hi, mid, lo = _split3(x)
    return _dot(tri, hi) + _dot(tri, mid) + _dot(tri, lo)


def _tri_dot_right(x, tri):
    hi, mid, lo = _split3(x)
    return _dot(hi, tri) + _dot(mid, tri) + _dot(lo, tri)


def _tile(n, want):
    t = min(n, want)
    assert n % t == 0
    return t


def _ln_inproj(x, g, scale, shift, w4, ride=None):
    s_len, d = x.shape
    nj, _, nsh = w4.shape
    tm = _tile(s_len, ROWS_MATMUL)
    ni = s_len // tm

    def body(x_ref, g_ref, sc_ref, sh_ref, w_ref, h_ref, u_ref, hs):
        rows = pl.ds(pl.multiple_of(pl.program_id(1) * tm, tm), tm)

        @pl.when(pl.program_id(0) == 0)
        def _():
            xv = x_ref[...]
            r = lax.rsqrt(jnp.mean(xv * xv, axis=-1, keepdims=True) + EPS)
            hv = (xv * r * g_ref[...]) * (1.0 + sc_ref[...]) + sh_ref[...]
            hs[rows, :] = _bf(hv)
            h_ref[...] = hs[rows, :]

        u_ref[...] = _dot(hs[rows, :], w_ref[0])

    vec = pl.BlockSpec((1, d), lambda j, i: (0, 0))
    once = pl.BlockSpec((tm, d), lambda j, i: (jnp.where(j == 0, i, ni - 1), 0))
    return _pcall_ride(
        body, ride, name="ln_inproj", grid=(nj, ni),
        in_specs=[once, vec, vec, vec, pl.BlockSpec((1, d, nsh), lambda j, i: (j, 0, 0))],
        out_specs=[once, pl.BlockSpec((tm, nsh), lambda j, i: (i, j))],
        out_shape=[jax.ShapeDtypeStruct((s_len, d), BF16), jax.ShapeDtypeStruct((s_len, nj * nsh), F32)],
        scratch_shapes=[pltpu.VMEM((s_len, d), BF16)],
        compiler_params=_seq(2),
        args=(x, g, scale, shift, w4))


def _rg_gates(xc, wa_ref, ba_ref, wx_ref, bx_ref, lam_ref):
    heads, hd, _ = wa_ref.shape
    xb = _bf(xc)
    ga = jnp.concatenate([_dot(xb[:, h * hd:(h + 1) * hd], wa_ref[h]) for h in range(heads)], axis=1) + ba_ref[...]
    gx = jnp.concatenate([_dot(xb[:, h * hd:(h + 1) * hd], wx_ref[h]) for h in range(heads)], axis=1) + bx_ref[...]
    r = _sigmoid(ga)
    ig = _sigmoid(gx)
    sp = _softplus(-lam_ref[...])
    log_a = (-RG_C) * r * sp
    a = jnp.exp(log_a)
    mult = jnp.sqrt(_one_minus_sq(a, log_a))
    return r, ig, sp, log_a, a, mult


def _scan_groups(a, u, reverse):
    n, c = a.shape
    a = a.reshape(n // 8, 8, c)
    u = u.reshape(n // 8, 8, c)
    row = lax.broadcasted_iota(jnp.int32, a.shape, 1)
    for k in (1, 2, 4):
        sft = 8 - k if reverse else k
        a_sh, u_sh = pltpu.roll(a, sft, 1), pltpu.roll(u, sft, 1)
        ok = row < 8 - k if reverse else row >= k
        u = jnp.where(ok, a * u_sh + u, u)
        a = jnp.where(ok, a * a_sh, a)
    return a.reshape(n, c), u.reshape(n, c)


def _rg_fwd(u, conv_w, conv_b, wa_b, ba, wx_b, bx, lam, ride=None):
    s_len = u.shape[0]
    d = conv_w.shape[1]
    tm = _tile(s_len, ROWS_VECTOR)
    per = tm // HALO

    def body(x_ref, xp_ref, z_ref, cw_ref, cb_ref, wa_ref, ba_ref, wx_ref, bx_ref, lam_ref,
             hh_ref, y_ref, carry):
        i = pl.program_id(0)

        @pl.when(i == 0)
        def _():
            carry[...] = jnp.zeros_like(carry)

        prev = jnp.where(i == 0, 0.0, xp_ref[...])
        xc = _conv_fwd(_conv_taps(jnp.concatenate([prev, x_ref[...]], axis=0)), cw_ref, cb_ref)
        _, ig, _, _, a, mult = _rg_gates(xc, wa_ref, ba_ref, wx_ref, bx_ref, lam_ref)
        ca, cu = _scan_groups(a, mult * (ig * xc), reverse=False)
        c = carry[0:1, :]
        for j in range(tm // 8):
            blk = ca[j * 8:(j + 1) * 8] * c + cu[j * 8:(j + 1) * 8]
            hh_ref[j * 8:(j + 1) * 8, :] = blk
            c = blk[7:8]
        carry[0:1, :] = c
        z = z_ref[...]
        y_ref[0] = _bf(hh_ref[...] * (z * _sigmoid(z)))

    vec = pl.BlockSpec((1, d), lambda i: (0, 0))
    whole3 = lambda a: pl.BlockSpec(a.shape, lambda i: (0, 0, 0))
    return _pcall_ride(
        body, ride, name="rg_fwd", grid=(s_len // tm,),
        in_specs=[pl.BlockSpec((tm, d), lambda i: (i, 0)),
                  pl.BlockSpec((HALO, d), lambda i: (jnp.maximum(i * per - 1, 0), 0)),
                  pl.BlockSpec((tm, d), lambda i: (i, 1)),
                  pl.BlockSpec((CONV_WIDTH, d), lambda i: (0, 0)), vec,
                  whole3(wa_b), vec, whole3(wx_b), vec, vec],
        out_specs=[pl.BlockSpec((tm, d), lambda i: (i, 0)), pl.BlockSpec((1, tm, d), lambda i: (0, i, 0))],
        out_shape=[jax.ShapeDtypeStruct((s_len, d), F32), jax.ShapeDtypeStruct((2, s_len, d), BF16)],
        scratch_shapes=[pltpu.VMEM((8, d), F32)],
        compiler_params=_seq(),
        args=(u, u, u, conv_w, conv_b, wa_b, ba, wx_b, bx, lam))


def _ml_pre(u, conv_w, conv_b, wqkv_b, wif_b, wift_b, b_if, b_ift):
    s_len = u.shape[0]
    d = conv_w.shape[1]
    _, heads, hd, _ = wqkv_b.shape
    ng = 2 * heads
    tm = _tile(s_len, max(ROWS_VECTOR, ML_CHUNK))
    per = tm // HALO

    def body(x_ref, xp_ref, cw_ref, cb_ref, w_ref, wif_ref, wift_ref, bif_ref, bift_ref,
             qkv_ref, gt_ref, gtt_ref, bc_ref, bct_ref):
        i = pl.program_id(0)
        prev = jnp.where(i == 0, 0.0, xp_ref[...])
        xm = x_ref[...]
        pre = _conv_fwd(_conv_taps(jnp.concatenate([prev, xm], axis=0)), cw_ref, cb_ref)
        xcb = _bf(pre * _sigmoid(pre))
        xmb = _bf(xm)
        for h in range(heads):
            hs = slice(h * hd, (h + 1) * hd)
            qkv_ref[0, :, hs] = _bf(_dot(xcb[:, hs], w_ref[0, h]))
            qkv_ref[1, :, hs] = _bf(_dot(xcb[:, hs], w_ref[1, h]))
            qkv_ref[2, :, hs] = _bf(_dot(xmb[:, hs], w_ref[2, h]))
        qb, kb, vb = qkv_ref[0], qkv_ref[1], qkv_ref[2]
        gt = (_dot(qb, wif_ref[0:d, :]) + _dot(kb, wif_ref[d:2 * d, :]) + _dot(vb, wif_ref[2 * d:3 * d, :])
              + bif_ref[...])
        gtt = (_dot_nt(wift_ref[:, 0:d], qb) + _dot_nt(wift_ref[:, d:2 * d], kb)
               + _dot_nt(wift_ref[:, 2 * d:3 * d], vb) + bift_ref[...])
        gt_ref[...] = gt
        gtt_ref[...] = gtt
        r = lax.broadcasted_iota(jnp.int32, (tm, tm), 0)
        c = lax.broadcasted_iota(jnp.int32, (tm, tm), 1)
        same = (r // ML_CHUNK) == (c // ML_CHUNK)
        bc_ref[...] = _tri_dot_left(((r >= c) & same).astype(BF16), _log_sigmoid(gt))
        bct_ref[...] = _tri_dot_right(_log_sigmoid(gtt), ((r <= c) & same).astype(BF16))

    vec = pl.BlockSpec((1, d), lambda i: (0, 0))
    whole2 = lambda a: pl.BlockSpec(a.shape, lambda i: (0, 0))
    col = pl.BlockSpec((tm, ng), lambda i: (i, 0))
    row = pl.BlockSpec((ng, tm), lambda i: (0, i))
    return _pcall(
        body, name="ml_pre", grid=(s_len // tm,),
        in_specs=[pl.BlockSpec((tm, d), lambda i: (i, 2)),
                  pl.BlockSpec((HALO, d), lambda i: (jnp.maximum(i * per - 1, 0), 2)),
                  pl.BlockSpec((CONV_WIDTH, d), lambda i: (0, 0)), vec,
                  pl.BlockSpec(wqkv_b.shape, lambda i: (0, 0, 0, 0)), whole2(wif_b), whole2(wift_b), whole2(b_if),
                  whole2(b_ift)],
        out_specs=[pl.BlockSpec((3, tm, d), lambda i: (0, i, 0)), col, row, col, row],
        out_shape=[jax.ShapeDtypeStruct((3, s_len, d), BF16), jax.ShapeDtypeStruct((s_len, ng), F32),
                   jax.ShapeDtypeStruct((ng, s_len), F32), jax.ShapeDtypeStruct((s_len, ng), F32),
                   jax.ShapeDtypeStruct((ng, s_len), F32)],
        compiler_params=_seq(),
    )(u, u, conv_w, conv_b, wqkv_b, wif_b, wift_b, b_if, b_ift)


def _chunk_gates(gt, gtt, bc, bct, h, heads):
    li_c = gt[:, h:h + 1]
    li_r = gtt[h:h + 1, :]
    gf_c = gt[:, heads + h:heads + h + 1]
    b_c = bc[:, heads + h:heads + h + 1]
    b_r = bct[heads + h:heads + h + 1, :]
    return li_c, li_r, gf_c, b_c, b_r


def _chunk_weights(li_c, li_r, b_c, b_r, m_prev, causal):
    lc = b_c.shape[0]
    b_last = b_c[lc - 1:lc, :]
    dmat = jnp.where(causal, b_c - b_r + li_r, -jnp.inf)
    m_inter = b_c + m_prev
    m_t = jnp.maximum(m_inter, jnp.max(dmat, axis=1, keepdims=True))
    w_intra = jnp.exp(dmat - m_t)
    w_inter = jnp.exp(m_inter - m_t)
    g_c = b_last - b_c + li_c
    m_new = jnp.maximum(b_last + m_prev, jnp.max(g_c, axis=0, keepdims=True))
    w_state = jnp.exp(g_c - m_new)
    decay = jnp.exp(b_last + m_prev - m_new)
    return m_t, w_intra, w_inter, m_new, w_state, decay


def _tri_masks(lc):
    r = lax.broadcasted_iota(jnp.int32, (lc, lc), 0)
    c = lax.broadcasted_iota(jnp.int32, (lc, lc), 1)
    causal = r >= c
    return causal, causal.astype(BF16), (r <= c).astype(BF16)


def _mlstm_fwd(qkv, gates, u, ml_g, ycat, ride=None):
    _, s_len, d = qkv.shape
    ng = gates[0].shape[1]
    heads = ng // 2
    hd = d // heads
    lc = ML_CHUNK
    nc = s_len // lc
    kscale = hd ** -0.5

    def body(qkv_ref, gt_ref, gtt_ref, bc_ref, bct_ref, o_ref, z_ref, g_ref, _, cell_ref, y_ref, cst_ref, nst_ref,
             mst_ref, cs, ns, ms):
        @pl.when(pl.program_id(0) == 0)
        def _():
            cs[...] = jnp.zeros_like(cs)
            ns[...] = jnp.zeros_like(ns)
            ms[...] = jnp.zeros_like(ms)

        causal = _tri_masks(lc)[0]
        gtv, gttv, bcv, bctv = gt_ref[...], gtt_ref[...], bc_ref[...], bct_ref[...]
        old = [(cs[h], ns[h], ms[h]) for h in range(heads)]
        new, cells, ys = [], [], []
        for h in range(heads):
            hs = slice(h * hd, (h + 1) * hd)
            li_c, li_r, _, b_c, b_r = _chunk_gates(gtv, gttv, bcv, bctv, h, heads)
            c_old, n_old, m_old = old[h]
            m_prev = m_old[:, 0:1]
            m_t, w_intra, w_inter, m_new, w_state, decay = _chunk_weights(li_c, li_r, b_c, b_r, m_prev, causal)
            qb = qkv_ref[0, :, hs]
            ks = qkv_ref[1, :, hs].astype(F32) * kscale
            kb = _bf(ks)
            vb = qkv_ref[2, :, hs]
            s = _dot_nt(qb, kb) * w_intra
            num = _dot(_bf(s), vb) + w_inter * _dot(qb, _bf(c_old))
            den = _rowsum(s) + w_inter * _rowsum(qb.astype(F32) * n_old)
            cell = num / jnp.maximum(jnp.abs(den), jnp.exp(-m_t))
            kw = ks * w_state
            new.append((decay * c_old + _dot_tn(_bf(kw), vb), decay * n_old + _colsum(kw),
                        jnp.broadcast_to(m_new, m_old.shape)))
            cells.append(cell)
            hm = _sigmoid(o_ref[:, hs]) * cell
            hn = hm * lax.rsqrt(jnp.mean(hm * hm, axis=-1, keepdims=True) + EPS)
            z = z_ref[:, hs]
            ys.append(_bf((hn * g_ref[:, hs]) * (z * _sigmoid(z))))
        for h in range(heads):
            cst_ref[0, h] = _bf(old[h][0])
            nst_ref[0, h] = old[h][1]
            mst_ref[0, h] = old[h][2]
            cs[h], ns[h], ms[h] = new[h]
        cell_ref[...] = jnp.concatenate(cells, axis=1)
        y_ref[0] = jnp.concatenate(ys, axis=1)

    row = pl.BlockSpec((lc, d), lambda c: (c, 0))
    gcol = pl.BlockSpec((lc, ng), lambda c: (c, 0))
    grow = pl.BlockSpec((ng, lc), lambda c: (0, c))
    return _pcall_ride(
        body, ride, name="mlstm_fwd", grid=(nc,),
        in_specs=[pl.BlockSpec((3, lc, d), lambda c: (0, c, 0)), gcol, grow, gcol, grow,
                  pl.BlockSpec((lc, d), lambda c: (c, 3)), pl.BlockSpec((lc, d), lambda c: (c, 4)),
                  pl.BlockSpec((1, d), lambda c: (0, 0)), pl.BlockSpec(memory_space=pl.ANY)],
        out_specs=[row, pl.BlockSpec((1, lc, d), lambda c: (1, c, 0)),
                   pl.BlockSpec((1, heads, hd, hd), lambda c: (c, 0, 0, 0)),
                   pl.BlockSpec((1, heads, 1, hd), lambda c: (c, 0, 0, 0)),
                   pl.BlockSpec((1, heads, 1, 128), lambda c: (c, 0, 0, 0))],
        out_shape=[jax.ShapeDtypeStruct((s_len, d), F32), jax.ShapeDtypeStruct(ycat.shape, BF16),
                   jax.ShapeDtypeStruct((nc, heads, hd, hd), BF16),
                   jax.ShapeDtypeStruct((nc, heads, 1, hd), F32),
                   jax.ShapeDtypeStruct((nc, heads, 1, 128), F32)],
        scratch_shapes=[pltpu.VMEM((heads, hd, hd), F32), pltpu.VMEM((heads, 1, hd), F32),
                        pltpu.VMEM((heads, 1, 128), F32)],
        input_output_aliases={8: 1},
        compiler_params=_seq(),
        args=(qkv, *gates, u, u, ml_g, ycat))


def _out_proj(ycat, w_out_b, x, gate, ride=None):
    s_len, d = x.shape
    tm = _tile(s_len, ROWS_W_OUT)

    def body(a_ref, w_ref, x_ref, g_ref, y_ref, xn_ref):
        y = _dot(a_ref[0], w_ref[0:d, :]) + _dot(a_ref[1], w_ref[d:2 * d, :])
        y_ref[...] = y
        xn_ref[...] = x_ref[...] + g_ref[...] * y

    row = pl.BlockSpec((tm, d), lambda i: (i, 0))
    return _pcall_ride(
        body, ride, name="out_proj", grid=(s_len // tm,),
        in_specs=[pl.BlockSpec((2, tm, d), lambda i: (0, i, 0)), pl.BlockSpec((2 * d, d), lambda i: (0, 0)), row,
                  pl.BlockSpec((1, d), lambda i: (0, 0))],
        out_specs=[row, row],
        out_shape=[jax.ShapeDtypeStruct((s_len, d), F32)] * 2,
        compiler_params=_seq(),
        args=(ycat, w_out_b, x, gate))


def _out_proj_loss(ycat, w_out_b, x, gate, g, target):
    s_len, d = x.shape
    tm = _tile(s_len, ROWS_IN_BWD)

    def body(a_ref, w_ref, x_ref, gate_ref, g_ref, t_ref, y_ref, dx_ref, dg_ref, loss_ref):
        @pl.when(pl.program_id(0) == 0)
        def _():
            dg_ref[...] = jnp.zeros_like(dg_ref)
            loss_ref[...] = jnp.zeros_like(loss_ref)

        y = _dot(a_ref[0], w_ref[0:d, :]) + _dot(a_ref[1], w_ref[d:2 * d, :])
        y_ref[...] = y
        xv = x_ref[...] + gate_ref[...] * y
        r = lax.rsqrt(jnp.mean(xv * xv, axis=-1, keepdims=True) + EPS)
        xn = xv * r
        err = xn * g_ref[...] - t_ref[...]
        loss_ref[...] += 0.5 * jnp.sum(jnp.mean(err * err, axis=-1, keepdims=True))
        dout = err * (1.0 / d)
        dg_ref[...] += _colsum(dout * xn)
        dxn = dout * g_ref[...]
        dx_ref[...] = r * (dxn - xn * jnp.mean(dxn * xn, axis=-1, keepdims=True))

    row = pl.BlockSpec((tm, d), lambda i: (i, 0))
    vec = pl.BlockSpec((1, d), lambda i: (0, 0))
    return _pcall(
        body, name="out_proj_loss", grid=(s_len // tm,),
        in_specs=[pl.BlockSpec((2, tm, d), lambda i: (0, i, 0)), pl.BlockSpec((2 * d, d), lambda i: (0, 0)), row, vec,
                  vec, row],
        out_specs=[row, row, vec, pl.BlockSpec((1, 128), lambda i: (0, 0))],
        out_shape=[jax.ShapeDtypeStruct((s_len, d), F32), jax.ShapeDtypeStruct((s_len, d), F32),
                   jax.ShapeDtypeStruct((1, d), F32), jax.ShapeDtypeStruct((1, 128), F32)],
        compiler_params=_seq(),
    )(ycat, w_out_b, x, gate, g, target)


def _out_bwd(dxn, y, gate, w_out_b):
    s_len, d = dxn.shape
    tm = _tile(s_len, ROWS_W_OUT)

    def body(dx_ref, y_ref, g_ref, w_ref, dg_ref, dy_ref, dc_ref):
        @pl.when(pl.program_id(0) == 0)
        def _():
            dg_ref[...] = jnp.zeros_like(dg_ref)

        dx = dx_ref[...]
        dg_ref[...] += _colsum(dx * y_ref[...])
        dy = _bf(g_ref[...] * dx)
        dy_ref[...] = dy
        dc_ref[0] = _dot_nt(dy, w_ref[0:d, :])
        dc_ref[1] = _dot_nt(dy, w_ref[d:2 * d, :])

    row = pl.BlockSpec((tm, d), lambda i: (i, 0))
    vec = pl.BlockSpec((1, d), lambda i: (0, 0))
    return _pcall(
        body, name="out_bwd", grid=(s_len // tm,),
        in_specs=[row, row, vec, pl.BlockSpec((2 * d, d), lambda i: (0, 0))],
        out_specs=[vec, row, pl.BlockSpec((2, tm, d), lambda i: (0, i, 0))],
        out_shape=[jax.ShapeDtypeStruct((1, d), F32), jax.ShapeDtypeStruct((s_len, d), BF16),
                   jax.ShapeDtypeStruct((2, s_len, d), F32)],
        compiler_params=_seq(),
    )(dxn, y, gate, w_out_b)


def _grad_matmul(a3, b3, nblk, a_idx, b_idx, out_shape, out_block, out_idx, ride=None):
    _, s_len, m = a3.shape
    n = b3.shape[2]
    tk = _tile(s_len, ROWS_GRAD_MATMUL)

    def body(a_ref, b_ref, o_ref):
        @pl.when(pl.program_id(1) == 0)
        def _():
            o_ref[...] = jnp.zeros_like(o_ref)

        o_ref[...] += _dot_tn(a_ref[0], b_ref[0])

    (out,), got = _pcall_ride(
        body, ride, name="grad_matmul", grid=(nblk, s_len // tk),
        in_specs=[pl.BlockSpec((1, tk, m), lambda p, t: (a_idx(p), t, 0)),
                  pl.BlockSpec((1, tk, n), lambda p, t: (b_idx(p), t, 0))],
        out_specs=[pl.BlockSpec((None,) + out_block, lambda p, t: (0,) + out_idx(p))],
        out_shape=[jax.ShapeDtypeStruct((1,) + out_shape, F32)],
        compiler_params=_seq(2), args=(a3, b3))
    return out, got


DU_PLANE = (2, 3, 4, 0, 1)


def _mlstm_bwd(qkv, gates, cst, nst, mst, cell, u, ml_g, d_ycat, wif_b, ride=None):
    _, s_len, d = qkv.shape
    ng = gates[0].shape[1]
    heads = ng // 2
    hd = d // heads
    lc = ML_CHUNK
    nc = s_len // lc
    kscale = hd ** -0.5

    def body(qkv_ref, gt_ref, gtt_ref, bc_ref, bct_ref, cst_ref, nst_ref, mst_ref, cell_ref, o_ref, z_ref, g_ref, dy_ref,
             wif_ref, dqkv_ref, dgt_ref, dbif_ref, du_ref, dg_ref, dcs, dns):
        @pl.when(pl.program_id(0) == 0)
        def _():
            dbif_ref[...] = jnp.zeros_like(dbif_ref)
            dcs[...] = jnp.zeros_like(dcs)
            dns[...] = jnp.zeros_like(dns)
            dg_ref[...] = jnp.zeros_like(dg_ref)

        causal, tril, triu = _tri_masks(lc)
        tril_strict = (tril.astype(F32) - (tril * triu).astype(F32)).astype(BF16)
        gtv, gttv, bcv, bctv = gt_ref[...], gtt_ref[...], bc_ref[...], bct_ref[...]
        lane = lax.broadcasted_iota(jnp.int32, (lc, ng), 1)
        dli_all = jnp.zeros((lc, ng), F32)
        from_later = jnp.zeros((lc, ng), F32)
        from_earlier = jnp.zeros((lc, ng), F32)
        across_all = jnp.zeros((1, ng), F32)
        old = [(dcs[h], dns[h]) for h in range(heads)]
        new, d_o, d_z, d_g, dqs, dks, dvs = [], [], [], [], [], [], []
        for h in range(heads):
            hs = slice(h * hd, (h + 1) * hd)
            li_c, li_r, gf_c, b_c, b_r = _chunk_gates(gtv, gttv, bcv, bctv, h, heads)
            m_prev = mst_ref[0, h][:, 0:1]
            m_t, w_intra, w_inter, _, w_state, decay = _chunk_weights(li_c, li_r, b_c, b_r, m_prev, causal)
            qb = qkv_ref[0, :, hs]
            qf = qb.astype(F32)
            ks = qkv_ref[1, :, hs].astype(F32) * kscale
            kb = _bf(ks)
            vb = qkv_ref[2, :, hs]
            c_b = cst_ref[0, h]
            n_old = nst_ref[0, h]
            s = _dot_nt(qb, kb) * w_intra
            den = _rowsum(s) + w_inter * _rowsum(qf * n_old)
            floor = jnp.exp(-m_t)
            dstab = jnp.maximum(jnp.abs(den), floor)
            cell = cell_ref[:, hs]
            o = o_ref[:, hs]
            so = _sigmoid(o)
            hm = so * cell
            rinv = lax.rsqrt(jnp.mean(hm * hm, axis=-1, keepdims=True) + EPS)
            hn = hm * rinv
            z = z_ref[:, hs]
            sgz = _sigmoid(z)
            sz = z * sgz
            gh = g_ref[:, hs]
            dy = dy_ref[0, :, hs]
            d_z.append(_bf(dy * (hn * gh) * _dsilu(z, sgz)))
            d_g.append(_colsum(dy * hn * sz))
            dhn = dy * gh * sz
            dhm = rinv * (dhn - hn * jnp.mean(dhn * hn, axis=-1, keepdims=True))
            d_o.append(_bf(dhm * cell * so * (1.0 - so)))
            dcell = dhm * so
            dnum = dcell / dstab
            dnb = _bf(dnum)
            dden = -_rowsum(dcell * cell) / dstab * jnp.where(jnp.abs(den) > floor, jnp.where(den > 0.0, 1.0, -1.0), 0.0)
            dst = _dot_nt(dnb, vb) + dden
            dsdb = _bf(dst * w_intra)
            dc_out, dn_out = old[h]
            dcb = _bf(dc_out)
            dq_inter = w_inter * (_dot_nt(dnb, c_b) + dden * n_old)
            dk_inter = w_state * (_dot_nt(vb, dcb) + dn_out)
            dq = _dot(dsdb, kb) + dq_inter
            dk = _dot_tn(dsdb, qb) + dk_inter
            dv = _dot_tn(_bf(s), dnb) + _dot(_bf(ks * w_state), dcb)
            wq = w_inter * qf
            new.append((decay * dc_out + _dot_tn(_bf(wq), dnb), decay * dn_out + _colsum(wq * dden)))
            pmat = dst * s
            p_rows = _rowsum(pmat)
            p_cols = _rowsum(pmat.T)
            q_in = _rowsum(qf * dq_inter)
            k_in = _rowsum(ks * dk_inter)
            across = decay * (jnp.sum(dc_out * c_b.astype(F32), keepdims=True) + jnp.sum(dn_out * n_old, keepdims=True))
            dli_all = dli_all + jnp.where(lane == h, p_cols + k_in, 0.0)
            from_later = from_later + jnp.where(lane == heads + h, p_rows - p_cols + q_in, 0.0)
            from_earlier = from_earlier + jnp.where(lane == heads + h, k_in, 0.0)
            across_all = across_all + jnp.where(lane[0:1] == heads + h, across, 0.0)
            dqs.append(dq)
            dks.append(dk * kscale)
            dvs.append(dv)
        for h in range(heads):
            dcs[h], dns[h] = new[h]
        du_ref[0] = jnp.concatenate(d_o, axis=1)
        du_ref[1] = jnp.concatenate(d_z, axis=1)
        dg_ref[...] += jnp.concatenate(d_g, axis=1)
        dlf = _tri_dot_left(triu, from_later) + _tri_dot_left(tril_strict, from_earlier) + across_all
        dgt = dli_all + dlf * _sigmoid(-gtv)
        dgt_ref[...] = dgt
        dbif_ref[...] += _colsum(dgt)
        dgb = _bf(dgt)
        dqkv_ref[0] = _bf(jnp.concatenate(dqs, axis=1) + _dot_nt(dgb, wif_ref[0:d, :]))
        dqkv_ref[1] = _bf(jnp.concatenate(dks, axis=1) + _dot_nt(dgb, wif_ref[d:2 * d, :]))
        dqkv_ref[2] = _bf(jnp.concatenate(dvs, axis=1) + _dot_nt(dgb, wif_ref[2 * d:3 * d, :]))

    rev = lambda c: nc - 1 - c
    row = pl.BlockSpec((lc, d), lambda c: (rev(c), 0))
    gcol = pl.BlockSpec((lc, ng), lambda c: (rev(c), 0))
    grow = pl.BlockSpec((ng, lc), lambda c: (0, rev(c)))
    return _pcall_ride(
        body, ride, name="mlstm_bwd", grid=(nc,),
        in_specs=[pl.BlockSpec((3, lc, d), lambda c: (0, rev(c), 0)), gcol, grow, gcol, grow,
                  pl.BlockSpec((1, heads, hd, hd), lambda c: (rev(c), 0, 0, 0)),
                  pl.BlockSpec((1, heads, 1, hd), lambda c: (rev(c), 0, 0, 0)),
                  pl.BlockSpec((1, heads, 1, 128), lambda c: (rev(c), 0, 0, 0)),
                  row, pl.BlockSpec((lc, d), lambda c: (rev(c), 3)), pl.BlockSpec((lc, d), lambda c: (rev(c), 4)),
                  pl.BlockSpec((1, d), lambda c: (0, 0)), pl.BlockSpec((1, lc, d), lambda c: (1, rev(c), 0)),
                  pl.BlockSpec((3 * d, ng), lambda c: (0, 0))],
        out_specs=[pl.BlockSpec((3, lc, d), lambda c: (0, rev(c), 0)), pl.BlockSpec((lc, ng), lambda c: (rev(c), 0)),
                   pl.BlockSpec((1, ng), lambda c: (0, 0)), pl.BlockSpec((2, lc, d), lambda c: (0, rev(c), 0)),
                   pl.BlockSpec((1, d), lambda c: (0, 0))],
        out_shape=[jax.ShapeDtypeStruct((3, s_len, d), BF16), jax.ShapeDtypeStruct((s_len, ng), F32),
                   jax.ShapeDtypeStruct((1, ng), F32), jax.ShapeDtypeStruct((5, s_len, d), BF16),
                   jax.ShapeDtypeStruct((1, d), F32)],
        scratch_shapes=[pltpu.VMEM((heads, hd, hd), F32), pltpu.VMEM((heads, 1, hd), F32)],
        compiler_params=_seq(),
        args=(qkv, *gates, cst, nst, mst, cell, u, u, ml_g, d_ycat, wif_b))


def _conv_bwd_tile(dp, later, taps, cw_ref, gw_ref, gb_ref):
    tm = dp.shape[0]
    dwin = jnp.concatenate([dp, later[...]], axis=0)
    later[...] = dp[0:HALO]
    acc = cw_ref[CONV_WIDTH - 1:CONV_WIDTH, :] * dp
    for k in range(CONV_WIDTH):
        if k < CONV_WIDTH - 1:
            acc = acc + cw_ref[k:k + 1, :] * _shift_up(dwin, CONV_WIDTH - 1 - k)[0:tm]
        gw_ref[k:k + 1, :] += _colsum(dp * taps[k])
    gb_ref[...] += _colsum(dp)
    return acc


def _ml_pre_bwd(dqkv, u, conv_w, conv_b, wqkv_b, du):
    s_len = u.shape[0]
    d = conv_w.shape[1]
    _, heads, hd, _ = wqkv_b.shape
    tm = _tile(s_len, ROWS_VECTOR)
    per = tm // HALO
    nt = s_len // tm

    def body(dqkv_ref, x_ref, xp_ref, cw_ref, cb_ref, w_ref, _, dx_ref, gw_ref, gcw_ref, gcb_ref, later, dps, dxs):
        i = pl.program_id(0)

        @pl.when(i == 0)
        def _():
            gw_ref[...] = jnp.zeros_like(gw_ref)
            gcw_ref[...] = jnp.zeros_like(gcw_ref)
            gcb_ref[...] = jnp.zeros_like(gcb_ref)
            later[...] = jnp.zeros_like(later)

        prev = jnp.where(i == nt - 1, 0.0, xp_ref[...])
        xm = x_ref[...]
        taps = _conv_taps(jnp.concatenate([prev, xm], axis=0))
        pre = _conv_fwd(taps, cw_ref, cb_ref)
        sg = _sigmoid(pre)
        xcb = _bf(pre * sg)
        xmb = _bf(xm)
        for h in range(heads):
            hs = slice(h * hd, (h + 1) * hd)
            dqh, dkh, dvh = dqkv_ref[0, :, hs], dqkv_ref[1, :, hs], dqkv_ref[2, :, hs]
            dxc = _dot_nt(dqh, w_ref[0, h]) + _dot_nt(dkh, w_ref[1, h])
            dps[:, hs] = dxc * _dsilu(pre[:, hs], sg[:, hs])
            dxs[:, hs] = _dot_nt(dvh, w_ref[2, h])
            gw_ref[0, h] += _dot_tn(xcb[:, hs], dqh)
            gw_ref[1, h] += _dot_tn(xcb[:, hs], dkh)
            gw_ref[2, h] += _dot_tn(xmb[:, hs], dvh)
        dx_ref[0] = _bf(_conv_bwd_tile(dps[...], later, taps, cw_ref, gcw_ref, gcb_ref) + dxs[...])

    rev = lambda i: nt - 1 - i
    vec = pl.BlockSpec((1, d), lambda i: (0, 0))
    cwb = pl.BlockSpec((CONV_WIDTH, d), lambda i: (0, 0))
    whole4 = pl.BlockSpec(wqkv_b.shape, lambda i: (0, 0, 0, 0))
    return _pcall(
        body, name="ml_pre_bwd", grid=(nt,),
        in_specs=[pl.BlockSpec((3, tm, d), lambda i: (0, rev(i), 0)), pl.BlockSpec((tm, d), lambda i: (rev(i), 2)),
                  pl.BlockSpec((HALO, d), lambda i: (jnp.maximum(rev(i) * per - 1, 0), 2)),
                  cwb, vec, whole4, pl.BlockSpec(memory_space=pl.ANY)],
        out_specs=[pl.BlockSpec((1, tm, d), lambda i: (DU_PLANE[2], rev(i), 0)), whole4, cwb, vec],
        out_shape=[jax.ShapeDtypeStruct(du.shape, BF16), jax.ShapeDtypeStruct(wqkv_b.shape, F32),
                   jax.ShapeDtypeStruct((CONV_WIDTH, d), F32), jax.ShapeDtypeStruct((1, d), F32)],
        scratch_shapes=[pltpu.VMEM((HALO, d), F32), pltpu.VMEM((tm, d), F32), pltpu.VMEM((tm, d), F32)],
        input_output_aliases={6: 0},
        compiler_params=_seq(),
    )(dqkv, u, u, conv_w, conv_b, wqkv_b, du)


def _rg_bwd(d_ycat, u, hh, conv_w, conv_b, wa_b, ba, wx_b, bx, lam, du):
    s_len = u.shape[0]
    d = conv_w.shape[1]
    heads, hd, _ = wa_b.shape
    tm = _tile(s_len, ROWS_VECTOR)
    per = tm // HALO
    nt = s_len // tm

    def body(dy_ref, x_ref, xp_ref, z_ref, hh_ref, hp_ref, cw_ref, cb_ref, wa_ref, ba_ref, wx_ref, bx_ref, lam_ref, _,
             du_ref, gwa_ref, gwx_ref, gba_ref, gbx_ref, glam_ref, gcw_ref, gcb_ref, carry, gbuf, later, dxcs):
        i = pl.program_id(0)
        first = i == nt - 1

        @pl.when(i == 0)
        def _():
            carry[...] = jnp.zeros_like(carry)
            later[...] = jnp.zeros_like(later)
            gwa_ref[...] = jnp.zeros_like(gwa_ref)
            gwx_ref[...] = jnp.zeros_like(gwx_ref)
            gba_ref[...] = jnp.zeros_like(gba_ref)
            gbx_ref[...] = jnp.zeros_like(gbx_ref)
            glam_ref[...] = jnp.zeros_like(glam_ref)
            gcw_ref[...] = jnp.zeros_like(gcw_ref)
            gcb_ref[...] = jnp.zeros_like(gcb_ref)

        prev = jnp.where(first, 0.0, xp_ref[...])
        taps = _conv_taps(jnp.concatenate([prev, x_ref[...]], axis=0))
        xc = _conv_fwd(taps, cw_ref, cb_ref)
        r, ig, sp, log_a, a, mult = _rg_gates(xc, wa_ref, ba_ref, wx_ref, bx_ref, lam_ref)
        z = z_ref[...]
        sgz = _sigmoid(z)
        dy = dy_ref[0]
        hh_v = hh_ref[...]
        du_ref[1] = _bf(dy * hh_v * _dsilu(z, sgz))
        dhh = dy * (z * sgz)
        rows = lax.broadcasted_iota(jnp.int32, a.shape, 0)
        coef = jnp.where(rows == tm - 1, carry[1:2, :], _shift_up(a, 1))
        ca, cu = _scan_groups(coef, dhh, reverse=True)
        c = carry[0:1, :]
        for j in range(tm // 8 - 1, -1, -1):
            blk = ca[j * 8:(j + 1) * 8] * c + cu[j * 8:(j + 1) * 8]
            gbuf[j * 8:(j + 1) * 8, :] = blk
            c = blk[0:1]
        carry[0:1, :] = c
        carry[1:2, :] = a[0:1]
        g = gbuf[...]
        hprev_tile = jnp.where(first, 0.0, hp_ref[...])
        hprev = _shift_down(jnp.concatenate([hprev_tile, hh_v], axis=0), 1)[HALO:]
        da = g * hprev
        gx_ = g * xc
        d_mult = gx_ * ig
        d_ig = gx_ * mult
        dxc = g * mult * ig
        dlog_a = da * a - d_mult * (a * a / mult)
        d_r = dlog_a * ((-RG_C) * sp)
        glam_ref[...] += _colsum(dlog_a * ((-RG_C) * r)) * (-_sigmoid(-lam_ref[...]))
        d_ga = d_r * r * (1.0 - r)
        d_gx = d_ig * ig * (1.0 - ig)
        gba_ref[...] += _colsum(d_ga)
        gbx_ref[...] += _colsum(d_gx)
        xb = _bf(xc)
        dgab = _bf(d_ga)
        dgxb = _bf(d_gx)
        for h in range(heads):
            hs = slice(h * hd, (h + 1) * hd)
            dxcs[:, hs] = dxc[:, hs] + _dot_nt(dgab[:, hs], wa_ref[h]) + _dot_nt(dgxb[:, hs], wx_ref[h])
            gwa_ref[h] += _dot_tn(xb[:, hs], dgab[:, hs])
            gwx_ref[h] += _dot_tn(xb[:, hs], dgxb[:, hs])
        du_ref[0] = _bf(_conv_bwd_tile(dxcs[...], later, taps, cw_ref, gcw_ref, gcb_ref))

    assert DU_PLANE[0] % 2 == 0 and DU_PLANE[1] == DU_PLANE[0] + 1
    rev = lambda i: nt - 1 - i
    row = pl.BlockSpec((tm, d), lambda i: (rev(i), 0))
    halo_prev = lambda col: pl.BlockSpec((HALO, d), lambda i: (jnp.maximum(rev(i) * per - 1, 0), col))
    vec = pl.BlockSpec((1, d), lambda i: (0, 0))
    cwb = pl.BlockSpec((CONV_WIDTH, d), lambda i: (0, 0))
    whole3 = lambda a: pl.BlockSpec(a.shape, lambda i: (0, 0, 0))
    return _pcall(
        body, name="rg_bwd", grid=(nt,),
        in_specs=[pl.BlockSpec((1, tm, d), lambda i: (0, rev(i), 0)), row, halo_prev(0),
                  pl.BlockSpec((tm, d), lambda i: (rev(i), 1)), row, halo_prev(0),
                  cwb, vec, whole3(wa_b), vec, whole3(wx_b), vec, vec, pl.BlockSpec(memory_space=pl.ANY)],
        out_specs=[pl.BlockSpec((2, tm, d), lambda i: (DU_PLANE[0] // 2, rev(i), 0)), whole3(wa_b), whole3(wa_b),
                   vec, vec, vec, cwb, vec],
        out_shape=[jax.ShapeDtypeStruct(du.shape, BF16), jax.ShapeDtypeStruct(wa_b.shape, F32),
                   jax.ShapeDtypeStruct(wa_b.shape, F32)] + [jax.ShapeDtypeStruct((1, d), F32)] * 3
        + [jax.ShapeDtypeStruct((CONV_WIDTH, d), F32), jax.ShapeDtypeStruct((1, d), F32)],
        scratch_shapes=[pltpu.VMEM((8, d), F32), pltpu.VMEM((tm, d), F32), pltpu.VMEM((HALO, d), F32),
                        pltpu.VMEM((tm, d), F32)],
        input_output_aliases={13: 0},
        compiler_params=_seq(),
    )(d_ycat, u, u, u, hh, hh, conv_w, conv_b, wa_b, ba, wx_b, bx, lam, du)


def _in_bwd(du, w4, x, dxn, g, scale, ride=None):
    s_len, d = x.shape
    tm = _tile(s_len, ROWS_IN_BWD)
    nsh_chips, _, nsh = w4.shape
    npc = du.shape[0]
    ck = d // 4
    assert nsh % ck == 0 and npc * d == nsh_chips * nsh

    def body(du_ref, w_ref, x_ref, dxn_ref, g_ref, sc_ref, dx_ref, dsh_ref, dsc_ref, dg_ref):
        @pl.when(pl.program_id(0) == 0)
        def _():
            dsh_ref[...] = jnp.zeros_like(dsh_ref)
            dsc_ref[...] = jnp.zeros_like(dsc_ref)
            dg_ref[...] = jnp.zeros_like(dg_ref)

        dh = None
        for q in range(npc * d // ck):
            col = q * ck
            p, pc = col // d, col % d
            s, sc = col // nsh, col % nsh
            t = _dot_nt(du_ref[DU_PLANE[p], :, pc:pc + ck], w_ref[s, :, sc:sc + ck])
            dh = t if dh is None else dh + t
        xv = x_ref[...]
        r = lax.rsqrt(jnp.mean(xv * xv, axis=-1, keepdims=True) + EPS)
        xn = xv * r
        gv = g_ref[...]
        onesc = 1.0 + sc_ref[...]
        dsh_ref[...] += _colsum(dh)
        dsc_ref[...] += _colsum(dh * (xn * gv))
        dg_ref[...] += _colsum(dh * xn * onesc)
        dxh = dh * (gv * onesc)
        dx_ref[...] = dxn_ref[...] + r * (dxh - xn * jnp.mean(dxh * xn, axis=-1, keepdims=True))

    row = pl.BlockSpec((tm, d), lambda i: (i, 0))
    vec = pl.BlockSpec((1, d), lambda i: (0, 0))
    return _pcall_ride(
        body, ride, name="in_bwd", grid=(s_len // tm,),
        in_specs=[pl.BlockSpec((npc, tm, d), lambda i: (0, i, 0)), pl.BlockSpec(w4.shape, lambda i: (0, 0, 0)), row, row,
                  vec, vec],
        out_specs=[row, vec, vec, vec],
        out_shape=[jax.ShapeDtypeStruct((s_len, d), F32)] + [jax.ShapeDtypeStruct((1, d), F32)] * 3,
        compiler_params=_seq(),
        args=(du, w4, x, dxn, g, scale))


def _layer_fwd(x, p, rides=None, loss_head=None):
    rides = rides or {}
    landed = {}
    ride = lambda kernel: rides[kernel](landed) if kernel in rides else None
    (h_b, u), landed["ln_inproj"] = _ln_inproj(x, p["norm_g"], p["scale"], p["shift"], p["w4"], ride("ln_inproj"))
    (hh, ycat), landed["rg_fwd"] = _rg_fwd(u, p["rg_conv_w"], p["rg_conv_b"], p["rg_wa_b"], p["rg_ba"], p["rg_wx_b"],
                                           p["rg_bx"], p["rg_lam"], ride("rg_fwd"))
    if "late" in rides:
        p = {**p, **rides["late"](landed)}
    qkv, *gates = _ml_pre(u, p["ml_conv_w"], p["ml_conv_b"], p["wqkv_b"], p["wif_b"], p["wift_b"], p["b_if"],
                          p["b_ift"])
    (cell, ycat, cst, nst, mst), landed["mlstm_fwd"] = _mlstm_fwd(qkv, gates, u, p["ml_g"], ycat, ride("mlstm_fwd"))
    if loss_head is None:
        (y, x_new), landed["out_proj"] = _out_proj(ycat, p["w_out_b"], x, p["gate"], ride("out_proj"))
    else:
        y, *x_new = _out_proj_loss(ycat, p["w_out_b"], x, p["gate"], *loss_head)
    saved = dict(x=x, h_b=h_b, u=u, hh=hh, qkv=qkv, gates=gates, cell=cell, ycat=ycat, cst=cst, nst=nst, mst=mst, y=y)
    return x_new, saved, p, landed


def _layer_bwd(dxn, p, s, rides=None):
    rides = rides or {}
    landed = {}
    ride = lambda kernel: rides[kernel](grads, landed) if kernel in rides else None
    u = s["u"]
    d = dxn.shape[1]
    d_gate, dy_b, d_ycat = _out_bwd(dxn, s["y"], p["gate"], p["w_out_b"])
    grads = dict(w_out=_grad_matmul(s["ycat"], dy_b[None], 2, lambda b: b, lambda b: 0, (2 * d, d), (d, d),
                                    lambda b: (b, 0))[0])
    (dqkv, dgt, g_b_if, du, g_ml_g), landed["mlstm_bwd"] = _mlstm_bwd(
        s["qkv"], s["gates"], s["cst"], s["nst"], s["mst"], s["cell"], u, p["ml_g"], d_ycat, p["wif_b"],
        ride("mlstm_bwd"))
    ng = dgt.shape[1]
    g_w_if = _grad_matmul(s["qkv"], _bf(dgt)[None], 3, lambda b: b, lambda b: 0, (3 * d, ng), (d, ng),
                          lambda b: (b, 0))[0][0]
    du, g_wqkv, g_ml_cw, g_ml_cb = _ml_pre_bwd(dqkv, u, p["ml_conv_w"], p["ml_conv_b"], p["wqkv_b"], du)
    du, g_wa, g_wx, g_ba, g_bx, g_lam, g_rg_cw, g_rg_cb = _rg_bwd(d_ycat, u, s["hh"], p["rg_conv_w"], p["rg_conv_b"],
                                                                  p["rg_wa_b"], p["rg_ba"], p["rg_wx_b"], p["rg_bx"],
                                                                  p["rg_lam"], du)
    grads.update(rg_conv_w=g_rg_cw, rg_conv_b=g_rg_cb, rg_w_a=g_wa, rg_b_a=g_ba, rg_w_x=g_wx, rg_b_x=g_bx,
                 rg_lambda=g_lam, ml_conv_w=g_ml_cw, ml_conv_b=g_ml_cb, ml_w_qkv=g_wqkv, ml_w_if=g_w_if, ml_b_if=g_b_if,
                 ml_norm_g=g_ml_g)
    npc = du.shape[0]
    grads["w_in"], landed["grad_w_in"] = _grad_matmul(
        s["h_b"][None], du, npc, lambda b: 0, lambda b: (b + DU_PLANE[0]) % npc, (d, npc * d), (d, d),
        lambda b: (0, b), ride("grad_w_in"))
    (dx, d_shift, d_scale, grads["norm_g"]), landed["in_bwd"] = _in_bwd(du, p["w4"], s["x"], dxn, p["norm_g"],
                                                                        p["scale"], ride("in_bwd"))
    return dx, grads, jnp.concatenate([d_shift, d_scale, d_gate], axis=1), landed


def _me():
    return lax.axis_index("x"), lax.axis_index("y"), lax.axis_index("c")


def _remote(src, dst, send_sem, recv_sem, to):
    return pltpu.make_async_remote_copy(src_ref=src, dst_ref=dst, send_sem=send_sem, recv_sem=recv_sem,
                                        device_id=to, device_id_type=MESH)


def _all_gather8(blocks, space):
    n = len(blocks)
    relay = [b.size * b.dtype.itemsize >= RELAY_BYTES and b.shape[0] % 32 == 0 for b in blocks]

    def body(*refs):
        x_refs, out_refs = refs[:n], refs[n:2 * n]
        send_sems, recv_sems, local_sems = refs[2 * n:]
        x, y, c = _me()
        me, sibling = (x, y, c), (x, y, 1 - c)
        by_x, by_y, across = (1 - x, y, c), (x, 1 - y, c), (1 - x, 1 - y, c)

        def rows(i, blk, part=None):
            m_per = blocks[i].shape[0]
            at = (4 * blk[0] + 2 * blk[1] + blk[2]) * m_per
            if part is not None:
                m_per //= 2
                at += part * m_per
            return out_refs[i].at[pl.ds(at, m_per), :]

        def copy(i, k, blk, to, src=None, part=None):
            return _remote(rows(i, blk, part) if src is None else src, rows(i, blk, part), send_sems.at[8 * i + k],
                           recv_sems.at[8 * i + k], to)

        mine = [pltpu.make_async_copy(x_refs[i], rows(i, me), local_sems.at[i]) for i in range(n)]
        first = []
        for i in range(n):
            first.append(copy(i, 0, me, sibling, src=x_refs[i]))
            first += [copy(i, 1, me, by_x, src=x_refs[i]), copy(i, 2, me, by_y, src=x_refs[i])]
            if not relay[i]:
                first.append(copy(i, 3, me, across, src=x_refs[i]))
        for cp in mine + first:
            cp.start()
        passed = []
        for i in range(n):
            copy(i, 1, by_x, me).wait_recv()
            passed.append(copy(i, 4, by_x, sibling))
            if relay[i]:
                passed.append(copy(i, 3, by_x, by_y, part=0))
        for cp in passed:
            cp.start()
        n_x = len(passed)
        for i in range(n):
            copy(i, 2, by_y, me).wait_recv()
            passed.append(copy(i, 5, by_y, sibling))
            if relay[i]:
                passed.append(copy(i, 7, by_y, by_x, part=1))
        for cp in passed[n_x:]:
            cp.start()
        for i in range(n):
            if relay[i]:
                copy(i, 3, across, me, part=0).wait_recv()
                copy(i, 7, across, me, part=1).wait_recv()
            else:
                copy(i, 3, across, me).wait_recv()
            passed.append(copy(i, 6, across, sibling))
            passed[-1].start()
        for i in range(n):
            copy(i, 0, sibling, me).wait_recv()
            for k, blk in ((4, by_x), (5, by_y), (6, across)):
                copy(i, k, (blk[0], blk[1], 1 - c), me).wait_recv()
        for cp in first + passed:
            cp.wait_send()
        for cp in mine:
            cp.wait()

    spec = pl.BlockSpec(memory_space=space)
    return _pcall(
        body, name="all_gather8",
        out_shape=[jax.ShapeDtypeStruct((8 * b.shape[0], b.shape[1]), b.dtype) for b in blocks],
        in_specs=[spec] * n, out_specs=[spec] * n,
        scratch_shapes=[pltpu.SemaphoreType.DMA((8 * n,)), pltpu.SemaphoreType.DMA((8 * n,)),
                        pltpu.SemaphoreType.DMA((n,))],
    )(*blocks)


def _exchange(legs):
    n = len(legs)

    def body(*refs):
        copies, local, relays = _exchange_body(legs, refs[:n], refs[n:2 * n], *refs[2 * n:])
        for cp in copies + local:
            cp.start()
        _hand_on(relays)
        _wait_all(copies, local, relays)

    hbm = pl.BlockSpec(memory_space=pltpu.HBM)
    return _pcall(body, name="exchange", out_shape=[leg.landing() for leg in legs], in_specs=[hbm] * n,
                  out_specs=[hbm] * n, input_output_aliases=_exchange_aliases(legs, 0, 0),
                  scratch_shapes=_exchange_sems(legs))(*[leg.src for leg in legs])


def _row_tile(rows, cap=4096, mult=16):
    best = None
    for t in range(mult, min(rows, cap) + 1, mult):
        if rows % t == 0:
            best = t
    return rows if best is None else best


def _pair_sum(half, own, own_spec, got, got_spec, out_shape, out_spec, grid):
    def body(_, a_ref, b_ref, o_ref):
        o_ref[...] = (a_ref[...] + b_ref[...].astype(F32)).astype(o_ref.dtype)

    return _pcall(
        body, name="pair_sum",
        grid_spec=pltpu.PrefetchScalarGridSpec(num_scalar_prefetch=1, grid=grid, in_specs=[own_spec, got_spec],
                                               out_specs=out_spec),
        out_shape=out_shape, compiler_params=_seq(len(grid)))(half, own, got)


def _chip_sum(ids, part, met, fill, layer=0, stack=1):
    _, _, rows, n = part.shape
    tr = _row_tile(rows, cap=max(16, BLOCK_ELEMS // n))
    first = isinstance(stack, int)

    def body(_, own_ref, a_ref, b_ref, c_ref, *rest):
        acc = own_ref[...].astype(F32) + a_ref[...].astype(F32)
        acc = acc + b_ref[...].astype(F32)
        rest[-1][...] = acc + c_ref[...].astype(F32)

    blk = (None, None, tr, n)
    other = lambda k: pl.BlockSpec(blk, lambda j, ids: ((ids[0] + k) % 4, 0, j, 0))
    in_specs = [pl.BlockSpec(blk, lambda j, ids: (ids[0], 0, j, 0)), other(1), other(2), other(3)]
    return _pcall(
        body, name="chip_sum",
        grid_spec=pltpu.PrefetchScalarGridSpec(
            num_scalar_prefetch=1, grid=(rows // tr,),
            in_specs=in_specs if first else in_specs + [pl.BlockSpec(memory_space=pl.ANY)],
            out_specs=pl.BlockSpec(blk, lambda j, ids: (layer, ids[1] if fill else 0, j, 0))),
        out_shape=jax.ShapeDtypeStruct(((stack,) if first else stack.shape[:1]) + (2 if fill else 1, rows, n), F32),
        input_output_aliases={} if first else {5: 0},
        compiler_params=_seq())(*((ids, part, met, met, met) if first else (ids, part, met, met, met, stack)))


def _ada_mod(c_all, w_ada, b_ada_cols):
    depth, d, n = w_ada.shape
    nb = c_all.shape[0]

    def body(c_ref, w_ref, b_ref, o_ref):
        cv = c_ref[...]
        ca = _bf(cv * _sigmoid(cv))
        o_ref[0] = _dot(ca, _bf(w_ref[0])) + b_ref[0]

    return _pcall(body, name="ada_mod", grid=(depth,),
                  in_specs=[pl.BlockSpec((nb, d), lambda l: (0, 0)), pl.BlockSpec((1, d, n), lambda l: (l, 0, 0)),
                            pl.BlockSpec((1, 1, n), lambda l: (l, 0, 0))],
                  out_specs=pl.BlockSpec((1, nb, n), lambda l: (l, 0, 0)),
                  out_shape=jax.ShapeDtypeStruct((depth, nb, n), F32), compiler_params=_seq())(c_all, w_ada, b_ada_cols)


def _ada_grad(c_all, dmod_cols, rows_all):
    nb, d = c_all.shape
    depth, _, n = dmod_cols.shape
    kinds, n_all = rows_all.shape[1], rows_all.shape[3]

    def body(c_ref, dm_ref, da_ref, gw_ref, gb_ref):
        cv = c_ref[...]
        ca = _bf(cv * _sigmoid(cv))
        gw_ref[0] = _dot_tn(ca, _bf(dm_ref[0]))
        for k in range(kinds):
            gb_ref[0, k] = _colsum(da_ref[0, k])

    return _pcall(body, name="ada_grad", grid=(depth,),
                  in_specs=[pl.BlockSpec((nb, d), lambda l: (0, 0)), pl.BlockSpec((1, nb, n), lambda l: (l, 0, 0)),
                            pl.BlockSpec((1, kinds, nb, n_all), lambda l: (l, 0, 0, 0))],
                  out_specs=[pl.BlockSpec((1, d, n), lambda l: (l, 0, 0)),
                             pl.BlockSpec((1, kinds, 1, n_all), lambda l: (l, 0, 0, 0))],
                  out_shape=[jax.ShapeDtypeStruct((depth, d, n), F32), jax.ShapeDtypeStruct((depth, kinds, 1, n_all), F32)],
                  compiler_params=_seq())(c_all, dmod_cols, rows_all)


def _adamw(items, ride=None):
    two_d = [tuple(t.reshape(w.size // w.shape[-1], w.shape[-1]) for t in (w, g, m, v)) for w, g, m, v in items]
    n = len(items)
    if n == 1:
        rows, cols = two_d[0][0].shape
        tr = _row_tile(rows, cap=max(8, BLOCK_ELEMS // cols), mult=8)
        blocks = [pl.BlockSpec((tr, cols), lambda i: (i, 0))]
        grid = (rows // tr,)
    else:
        blocks = [pl.BlockSpec(t[0].shape, lambda i: (0, 0)) for t in two_d]
        grid = (1,)

    def body(*refs):
        for k in range(n):
            w_ref, g_ref, m_ref, v_ref = refs[4 * k:4 * k + 4]
            d_ref, mo_ref, vo_ref = refs[4 * n + 3 * k:4 * n + 3 * k + 3]
            gv = g_ref[...]
            mn = ADAM_B1 * m_ref[...] + (1.0 - ADAM_B1) * gv
            vn = ADAM_B2 * v_ref[...] + (1.0 - ADAM_B2) * (gv * gv)
            m_hat = mn / (1.0 - ADAM_B1 ** ADAM_STEP)
            v_hat = vn / (1.0 - ADAM_B2 ** ADAM_STEP)
            d_ref[...] = -ADAM_LR * (m_hat / (jnp.sqrt(v_hat) + ADAM_EPS) + ADAM_WD * w_ref[...])
            mo_ref[...] = mn
            vo_ref[...] = vn

    outs, got = _pcall_ride(
        body, ride, name="adamw", grid=grid,
        in_specs=[b for b in blocks for _ in range(4)], out_specs=[b for b in blocks for _ in range(3)],
        out_shape=[jax.ShapeDtypeStruct(t[0].shape, F32) for t in two_d for _ in range(3)],
        compiler_params=_seq(), args=tuple(a for t in two_d for a in t))
    return [tuple(o.reshape(items[k][0].shape) for o in outs[3 * k:3 * k + 3]) for k in range(n)], got


WEIGHTS = ["norm_g", "w_ada", "b_ada", "w_in", "rg_conv_w", "rg_conv_b", "rg_w_a", "rg_b_a", "rg_w_x", "rg_b_x",
           "rg_lambda", "ml_conv_w", "ml_conv_b", "ml_w_q", "ml_w_k", "ml_w_v", "ml_w_if", "ml_b_if", "ml_norm_g",
           "w_out", "final_g"]
SMALL_SHARDED = {"rg_conv_w": 1, "ml_conv_w": 1, "ml_w_if": 0}
REPLICATED = ["rg_w_a", "rg_w_x", "rg_conv_b", "rg_b_a", "rg_b_x", "rg_lambda", "ml_conv_b", "ml_norm_g", "ml_b_if"]
LANES = 128


def _to_pieces(g, axis):
    shp = g.shape
    g = g.reshape(shp[:axis] + (4, 2, shp[axis] // 8) + shp[axis + 1:])
    g = jnp.moveaxis(g, (axis, axis + 1), (0, 1))
    return g.reshape(4, 2, -1)


def _from_pieces(p, shard_shape, axis):
    k = p.shape[0]
    rest = shard_shape[:axis] + (shard_shape[axis] // k,) + shard_shape[axis + 1:]
    t = jnp.moveaxis(p.reshape((k,) + rest), 0, axis)
    return t.reshape(shard_shape)


def _pad_rows(flat, mult):
    n = flat.shape[-1]
    pad = (-n) % mult
    if pad:
        flat = jnp.concatenate([flat, jnp.zeros(flat.shape[:-1] + (pad,), flat.dtype)], axis=-1)
    return flat


def kernel(x, c, norm_g, w_ada, b_ada, w_in, rg_conv_w, rg_conv_b, rg_w_a, rg_b_a, rg_w_x, rg_b_x, rg_lambda, ml_conv_w, ml_conv_b, ml_w_q, ml_w_k, ml_w_v, ml_w_if, ml_b_if, ml_norm_g, w_out, final_g, loss_target, m_norm_g, m_w_ada, m_b_ada, m_w_in, m_rg_conv_w, m_rg_conv_b, m_rg_w_a, m_rg_b_a, m_rg_w_x, m_rg_b_x, m_rg_lambda, m_ml_conv_w, m_ml_conv_b, m_ml_w_q, m_ml_w_k, m_ml_w_v, m_ml_w_if, m_ml_b_if, m_ml_norm_g, m_w_out, m_final_g, v_norm_g, v_w_ada, v_b_ada, v_w_in, v_rg_conv_w, v_rg_conv_b, v_rg_w_a, v_rg_b_a, v_rg_w_x, v_rg_b_x, v_rg_lambda, v_ml_conv_w, v_ml_conv_b, v_ml_w_q, v_ml_w_k, v_ml_w_v, v_ml_w_if, v_ml_b_if, v_ml_norm_g, v_w_out, v_final_g):
    given = dict(locals())
    ax, ay, ac = lax.axis_index("x"), lax.axis_index("y"), lax.axis_index("c")
    chip = 2 * ax + ay
    me = 2 * chip + ac
    depth, d = norm_g.shape
    n_ada = w_ada.shape[2]
    pick = lambda a, i, axis=0: lax.dynamic_index_in_dim(a, i, axis, keepdims=False)

    convs = jnp.stack([rg_conv_w, ml_conv_w])
    n_conv = 2 * depth * CONV_WIDTH // 4
    blk = jnp.concatenate([c, convs.reshape(n_conv, d), jnp.zeros((8 - 1 - n_conv, d), F32)], axis=0)
    w_in_first = lax.dynamic_slice_in_dim(w_in[0], ac * (d // 2), d // 2, 0).astype(BF16)
    g0, w_in_first = _all_gather8([blk, w_in_first], pltpu.HBM)
    g0 = g0.reshape(8, 8, d)
    c_all = g0[:, 0, :]
    conv_full = g0[0::2, 1:1 + n_conv].reshape(4, 2, depth, CONV_WIDTH, d // 4)
    conv_full = conv_full.transpose(1, 2, 3, 0, 4).reshape(2, depth, CONV_WIDTH, d)

    b_cols = lax.dynamic_slice_in_dim(b_ada, chip * n_ada, n_ada, axis=1)[:, None, :]
    mod_part = _ada_mod(c_all, w_ada, b_cols)
    g1 = _all_gather8([mod_part.transpose(1, 0, 2).reshape(8, depth * n_ada)], pltpu.VMEM)[0]
    g1 = g1.reshape(8, 8, depth, n_ada)[0::2]
    mod_me = pick(g1.transpose(1, 2, 0, 3).reshape(8, depth, 4 * n_ada), me)

    def half_of(w, axis):
        n = w.shape[axis] // 2
        return lax.dynamic_slice_in_dim(w, ac * n, n, axis).astype(BF16)

    n_sh = w_in.shape[2]
    heads, hd_cut, hd = ml_w_q.shape[1:]

    def blocks_of(l):
        wqkv = jnp.stack([ml_w_q[l], ml_w_k[l], ml_w_v[l]])
        return [half_of(w_in[l], 0), half_of(w_out[l], 0), half_of(wqkv, 2).reshape(-1, hd), half_of(ml_w_if[l], 0)]

    def layer_of(l, w4, rest):
        return dict(
            norm_g=norm_g[l][None], shift=mod_me[l, 0:d][None], scale=mod_me[l, d:2 * d][None],
            gate=mod_me[l, 2 * d:3 * d][None], w4=w4.reshape(4, d, n_sh),
            rg_conv_w=conv_full[0, l], rg_conv_b=rg_conv_b[l][None], rg_wa_b=_bf(rg_w_a[l]), rg_ba=rg_b_a[l][None],
            rg_wx_b=_bf(rg_w_x[l]), rg_bx=rg_b_x[l][None], rg_lam=rg_lambda[l][None],
            ml_conv_w=conv_full[1, l], ml_conv_b=ml_conv_b[l][None], b_if=ml_b_if[l][None], b_ift=ml_b_if[l][:, None],
            ml_g=ml_norm_g[l][None], **rest)

    def rest_of(gathered):
        w_out_b, wqkv_g, wif = gathered
        return dict(w_out_b=w_out_b, wqkv_b=_from_pieces(wqkv_g.reshape(8, -1), (3, heads, hd, hd), 2), wif_b=wif,
                    wift_b=wif.T)

    spread = lambda blocks: [Leg(b, "spread") for b in blocks]
    fill = lambda landed: [Leg(t, "sib_fill") for t in landed]
    flat = lambda filled: [t.reshape(-1, t.shape[-1]) for t in filled]
    first = blocks_of(0)
    n_rest = len(first) - 1
    p = layer_of(0, w_in_first, {})
    layers, saved = [], []
    xl = x[0]
    for l in range(depth):
        nxt = blocks_of(l + 1) if l + 1 < depth else []
        skip = n_rest if l == 0 else 0
        rides = dict(rg_fwd=lambda landed, nxt=nxt: spread(nxt[:1]))
        if l == 0:
            rides.update(ln_inproj=lambda landed: spread(first[1:]),
                         rg_fwd=lambda landed, nxt=nxt: fill(landed["ln_inproj"]) + spread(nxt[:1]),
                         late=lambda landed: rest_of(flat(landed["rg_fwd"][:n_rest])))
        if nxt:
            rides.update(mlstm_fwd=lambda landed, nxt=nxt: spread(nxt[1:]),
                         out_proj=lambda landed, skip=skip: fill(list(landed["rg_fwd"][skip:]) + list(landed["mlstm_fwd"])))
        xl, s, p, landed = _layer_fwd(xl, p, rides, None if nxt else (final_g[None], loss_target[0]))
        layers.append(p)
        saved.append(s)
        if nxt:
            arrived = flat(landed["out_proj"])
            p = layer_of(l + 1, arrived[0], rest_of(arrived[1:]))
    dx, g_final, loss = xl

    half = ac.reshape(1)
    ids = jnp.stack([chip, ac])
    r_out = w_out.shape[1] // 2

    def pair_in(g_w_in, got_in):
        return _pair_sum(
            half, g_w_in, pl.BlockSpec((None, d // 2, n_sh), lambda s, h: (0, h[0], s)),
            got_in, pl.BlockSpec((None, None, d // 2, n_sh), lambda s, h: (0, s, 0, 0)),
            jax.ShapeDtypeStruct((4, 1, d // 2, n_sh), BF16),
            pl.BlockSpec((None, None, d // 2, n_sh), lambda s, h: (s, 0, 0, 0)), (4,))

    def pair_out(g_out5, got_out):
        return _pair_sum(
            half, g_out5, pl.BlockSpec((None, None, None, r_out, d), lambda s, h: (0, s, h[0], 0, 0)),
            got_out, pl.BlockSpec((None, None, r_out, d), lambda s, h: (0, s, 0, 0)),
            jax.ShapeDtypeStruct((4, 1, r_out, d), BF16),
            pl.BlockSpec((None, None, r_out, d), lambda s, h: (s, 0, 0, 0)), (4,))

    def pair_slab(slab, got, dtype):
        rows = got.shape[0] // 4
        blk = pl.BlockSpec((rows, LANES), lambda s, h: (s, 0))
        return _pair_sum(half, slab, pl.BlockSpec((None, rows, LANES), lambda s, h: (h[0], s, 0)), got, blk,
                         jax.ShapeDtypeStruct((4 * rows, LANES), dtype), blk, (4,)).reshape(4, 1, rows, LANES)

    row_pad = lambda n: -(-n // (8 * LANES)) * (8 * LANES)

    def as_rows(t):
        if t.shape[-1] == LANES and t.size % (8 * LANES) == 0:
            return t.reshape(-1, LANES)
        return _pad_rows(t.reshape(-1), 8 * LANES).reshape(-1, LANES)

    chips = lambda arrs: [Leg(a, "chips") for a in arrs]
    out5 = lambda g: g["w_out"].reshape(1, 4, 2, r_out, d)
    r_q = hd // 8
    qkv5 = lambda g: g["ml_w_qkv"].reshape(3 * heads, 4, 2, r_q, hd)

    def pair_qkv(g5, got):
        return _pair_sum(
            half, g5, pl.BlockSpec((3 * heads, None, None, r_q, hd), lambda s, h: (0, s, h[0], 0, 0)),
            got, pl.BlockSpec((3 * heads, None, r_q, hd), lambda s, h: (0, s, 0, 0)),
            jax.ShapeDtypeStruct((4, 1, 3 * heads, r_q, hd), BF16),
            pl.BlockSpec((None, None, 3 * heads, r_q, hd), lambda s, h: (s, 0, 0, 0, 0)), (4,))

    grads, dmods, parts, mets = [None] * depth, [None] * depth, [None] * depth, [None] * depth
    small = {}

    def early_exchange(g, landed):
        every = [g] + grads[1:]
        sm = jnp.concatenate([_to_pieces(every[l][name], axis) for l in range(depth)
                              for name, axis in SMALL_SHARDED.items()], axis=-1)
        sm = _pad_rows(sm, 16 * LANES)
        sm = sm.transpose(1, 0, 2).reshape(2, -1, LANES)
        rep = [as_rows(every[l][name]) for l in range(depth) for name in REPLICATED]
        rep = jnp.concatenate(rep + [as_rows(g_final), as_rows(loss)], axis=0)
        rep = jnp.concatenate([rep, jnp.zeros(((-rep.shape[0]) % 64, LANES), F32)], axis=0)
        rep = rep.reshape(4, 2, -1, LANES).transpose(1, 0, 2, 3).reshape(2, -1, LANES)
        got_sm, got_rep, got_q = _exchange([Leg(sm, "sib_slab"), Leg(rep, "sib_slab"), Leg(qkv5(g), "sib_w_out")])
        small["parts"] = [pair_out(out5(g), landed["mlstm_bwd"][0]), pair_slab(sm, got_sm, BF16),
                          pair_slab(rep, got_rep, F32), pair_qkv(qkv5(g), got_q)]
        return chips(small["parts"])

    def last_exchange(g, landed):
        (got_in,) = _exchange([Leg(g["w_in"], "sib_w_in")])
        small["part_in"] = pair_in(g["w_in"], got_in)
        return chips([small["part_in"]])

    for l in reversed(range(depth)):
        above = parts[l + 1] if l + 1 < depth else []
        rides = dict(mlstm_bwd=lambda g, landed, above=above: [Leg(out5(g), "sib_w_out")] + chips(above),
                     in_bwd=lambda g, landed: [Leg(g["w_in"], "sib_w_in"), Leg(qkv5(g), "sib_w_out")])
        if l == 0:
            rides.update(grad_w_in=early_exchange, in_bwd=last_exchange)
        dx, grads[l], dmods[l], got = _layer_bwd(dx, layers[l], saved[l], rides)
        if above:
            mets[l + 1] = got["mlstm_bwd"][1:]
        if l > 0:
            parts[l] = [pair_in(grads[l]["w_in"], got["in_bwd"][0]), pair_out(out5(grads[l]), got["mlstm_bwd"][0]),
                        pair_qkv(qkv5(grads[l]), got["in_bwd"][1])]
    part_out, part_sm, part_rep, part_q = small["parts"]
    met_out, met_sm, met_rep, met_q = got["grad_w_in"]
    parts[0], mets[0] = [small["part_in"], part_out, part_q], [got["in_bwd"][0], met_out, met_q]
    n_rep = part_rep.shape[2]

    pad = lambda t: jnp.concatenate([t, jnp.zeros((1, 2 * d), F32)], axis=1)
    rows = [r for l in range(depth) for r in (dmods[l], pad(grads[l]["norm_g"]))]
    blk = jnp.concatenate(rows + [jnp.zeros((8 - 2 * depth, 3 * d), F32)], axis=0)
    red_rep = _chip_sum(ids, part_rep, met_rep, False).reshape(n_rep, LANES)
    rows_all, rep_all = _all_gather8([blk, red_rep], pltpu.VMEM)
    rows_all, rep_all = rows_all.reshape(8, 8, 3 * d)[:, :2 * depth], rep_all.reshape(-1)
    rows_all = rows_all.transpose(1, 0, 2).reshape(depth, 2, 8, 3 * d)
    dm_cols = lax.dynamic_slice_in_dim(rows_all[:, 0], chip * n_ada, n_ada, axis=2)
    g_w_ada, summed = _ada_grad(c_all, dm_cols, rows_all)

    g = dict(w_ada=g_w_ada, b_ada=summed[:, 0, 0], norm_g=summed[:, 1, 0, :d])
    item = lambda name: (given[name], g[name], given["m_" + name], given["v_" + name])
    both_in, both_out, both_q = depth, depth, depth
    flat_q = lambda t: t.reshape(4, 1, 3 * heads * r_q, hd)
    for l in range(depth):
        both_in = _chip_sum(ids, parts[l][0], mets[l][0], True, l, both_in)
        both_out = _chip_sum(ids, parts[l][1], mets[l][1], True, l, both_out)
        both_q = _chip_sum(ids, flat_q(parts[l][2]), flat_q(mets[l][2]), True, l, both_q)
    both_in, both_out, both_q, both_sm = _exchange(fill([both_in, both_out, both_q,
                                                         _chip_sum(ids, part_sm, met_sm, True)]))

    g.update(w_in=both_in.reshape(w_in.shape), w_out=both_out.reshape(w_out.shape))
    g_qkv = both_q.reshape(depth, 2, 3, heads, r_q, hd).transpose(0, 2, 3, 1, 4, 5)
    g_qkv = g_qkv.reshape(depth, 3, heads, 2 * r_q, hd)
    for i, name in enumerate(["ml_w_q", "ml_w_k", "ml_w_v"]):
        g[name] = g_qkv[:, i]
    shard = both_sm.reshape(2, -1)
    off = 0
    per_layer = {name: [] for name in SMALL_SHARDED}
    for l in range(depth):
        for name, axis in SMALL_SHARDED.items():
            n = grads[l][name].size // 8
            per_layer[name].append(_from_pieces(shard[:, off:off + n], given[name].shape[1:], axis))
            off += n
    for name in SMALL_SHARDED:
        g[name] = jnp.stack(per_layer[name])
    off = 0
    per_layer = {name: [] for name in REPLICATED}
    for l in range(depth):
        for name in REPLICATED:
            n = given[name][l].size
            per_layer[name].append(rep_all[off:off + n].reshape(given[name].shape[1:]))
            off += row_pad(n)
    for name in REPLICATED:
        g[name] = jnp.stack(per_layer[name])
    g["final_g"] = rep_all[off:off + d]
    loss_all = rep_all[off + row_pad(d)]

    stepped = {}
    rg_mats, ml_mats = ["rg_w_a", "rg_w_x"], ["ml_w_q", "ml_w_k", "ml_w_v"]
    vectors = [n for n in WEIGHTS if n not in ["w_ada", "w_in", "w_out"] + rg_mats + ml_mats]
    for names in (["w_ada"], ["w_in"], ["w_out"], rg_mats, ml_mats, vectors):
        stepped.update(zip(names, _adamw([item(name) for name in names])[0]))
    deltas, new_m, new_v = zip(*[stepped[name] for name in WEIGHTS])
    return (loss_all, dx[None], *[g[name] for name in WEIGHTS], *deltas, *new_m, *new_v)
```

```python
import functools
from typing import NamedTuple

import jax
import jax.numpy as jnp
from jax import lax
from jax.experimental import pallas as pl
from jax.experimental.pallas import tpu as pltpu

F32 = jnp.float32
BF16 = jnp.bfloat16

EPS = 1e-6
RG_C = 8.0
CONV_WIDTH = 4
ML_CHUNK = 512
HALO = 8
ROWS_VECTOR = 512
ROWS_MATMUL = 1024
ROWS_IN_BWD = 512
ROWS_GRAD_MATMUL = 4096
BLOCK_ELEMS = 1 << 18
RELAY_BYTES = 1 << 18
ADAM_LR = 0.001
ADAM_B1 = 0.9
ADAM_B2 = 0.999
ADAM_EPS = 1e-08
ADAM_WD = 0.01
ADAM_STEP = 10
MESH = pl.DeviceIdType.MESH


def _pcall(body, **kw):
    return pl.pallas_call(body, **kw)


class Leg(NamedTuple):
    src: jax.Array
    kind: str

    def landing(self):
        a = self.src
        shape = {"chips": lambda: a.shape, "spread": lambda: (4, 2) + a.shape, "sib_fill": lambda: a.shape,
                 "sib_w_in": lambda: (a.shape[0], 4, a.shape[1] // 2, a.shape[2] // 4),
                 "sib_w_out": lambda: a.shape[:2] + a.shape[3:], "sib_slab": lambda: a.shape[1:]}[self.kind]()
        return jax.ShapeDtypeStruct(shape, a.dtype)

    def relayed(self):
        a = self.src
        return self.kind == "spread" and a.size * a.dtype.itemsize >= RELAY_BYTES and a.shape[0] % 32 == 0

    def copies(self, src, dst, x, y, c):
        a, me_s, o = self.src, 2 * x + y, 1 - c
        chips = [(1 - x, y), (x, 1 - y), (1 - x, 1 - y)]
        if self.kind == "chips":
            return [(src.at[2 * px + py], dst.at[me_s], (px, py, c)) for px, py in chips], [], []
        if self.kind == "spread":
            own = dst.at[me_s, c]
            if not self.relayed():
                return [(src, own, (px, py, c)) for px, py in chips], [(src, own)], []
            by_x, by_y, half = chips[0], chips[1], a.shape[0] // 2
            part = lambda chip, k: dst.at[2 * chip[0] + chip[1], c, pl.ds(k * half, half)]
            return ([(src, own, (*by_x, c)), (src, own, (*by_y, c))], [(src, own)],
                    [(part(by_x, 0), part(by_x, 0), (*by_y, c), 0), (part(by_y, 1), part(by_y, 1), (*by_x, c), 1)])
        depth = pl.ds(0, a.shape[0])
        if self.kind == "sib_fill":
            return [(dst.at[depth, c], dst.at[depth, c], (x, y, o))], [], []
        if self.kind == "sib_w_in":
            half, n = a.shape[1] // 2, a.shape[2] // 4
            return [(src.at[depth, pl.ds(o * half, half), pl.ds(s * n, n)], dst.at[depth, s], (x, y, o))
                    for s in range(4)], [], []
        if self.kind == "sib_w_out":
            return [(src.at[depth, pl.ds(0, 4), o], dst, (x, y, o))], [], []
        return [(src.at[o], dst, (x, y, o))], [], []

    def n_copies(self):
        return 4 if self.relayed() else {"chips": 3, "spread": 3, "sib_w_in": 4}.get(self.kind, 1)


def _exchange_body(legs, srcs, dsts, send_sems, recv_sems, local_sems):
    x, y, c = _me()
    remote, local, relays, k = [], [], [], 0
    for i, leg in enumerate(legs):
        far, near, handed = leg.copies(srcs[i], dsts[i], x, y, c)
        at = len(remote)
        for src, dst, to in far:
            remote.append(_remote(src, dst, send_sems.at[k], recv_sems.at[k], to))
            k += 1
        for src, dst, to, after in handed:
            relays.append((_remote(src, dst, send_sems.at[k], recv_sems.at[k], to), remote[at + after]))
            k += 1
        local += [pltpu.make_async_copy(src, dst, local_sems.at[i]) for src, dst in near]
    return remote, local, relays


def _hand_on(relays):
    for cp, after in relays:
        after.wait_recv()
        cp.start()


def _wait_all(copies, local, relays):
    arrived, handed = [after for _, after in relays], [cp for cp, _ in relays]
    for cp in [cp for cp in copies if not any(cp is a for a in arrived)] + handed:
        cp.wait_recv()
    for cp in copies + handed:
        cp.wait_send()
    for cp in local:
        cp.wait()


def _exchange_sems(legs):
    n = sum(leg.n_copies() for leg in legs)
    return [pltpu.SemaphoreType.DMA((n,)), pltpu.SemaphoreType.DMA((n,)), pltpu.SemaphoreType.DMA((len(legs),))]


def _exchange_aliases(legs, n_in, n_out):
    return {n_in + i: n_out + i for i, leg in enumerate(legs) if leg.kind == "sib_fill"}


def _pcall_ride(body, ride, *, grid, in_specs, out_specs, out_shape, args, scratch_shapes=(), **kw):
    n_in, n_out, n_scr = len(in_specs), len(out_specs), len(scratch_shapes)
    if not ride:
        res = _pcall(body, grid=grid, in_specs=in_specs, out_specs=out_specs, out_shape=out_shape,
                     scratch_shapes=list(scratch_shapes), **kw)(*args)
        return res, []
    nr = len(ride)

    def riding(*refs):
        ins, rsrc = refs[:n_in], refs[n_in:n_in + nr]
        outs, rdst = refs[n_in + nr:n_in + nr + n_out], refs[n_in + nr + n_out:n_in + 2 * nr + n_out]
        scr = refs[n_in + 2 * nr + n_out:n_in + 2 * nr + n_out + n_scr]
        copies, local, relays = _exchange_body(ride, rsrc, rdst, *refs[n_in + 2 * nr + n_out + n_scr:])
        at_step = lambda steps: functools.reduce(jnp.logical_and, [pl.program_id(a) == s for a, s in enumerate(steps)])

        @pl.when(at_step([0] * len(grid)))
        def _():
            for cp in copies + local:
                cp.start()

        body(*ins, *outs, *scr)

        if relays:
            @pl.when(at_step([grid[0] // 2] + [0] * (len(grid) - 1)))
            def _():
                _hand_on(relays)

        @pl.when(at_step([g - 1 for g in grid]))
        def _():
            _wait_all(copies, local, relays)

    hbm = pl.BlockSpec(memory_space=pltpu.HBM)
    aliases = {**kw.pop("input_output_aliases", {}), **_exchange_aliases(ride, n_in, n_out)}
    res = _pcall(
        riding, grid=grid, in_specs=list(in_specs) + [hbm] * nr, out_specs=list(out_specs) + [hbm] * nr,
        out_shape=list(out_shape) + [leg.landing() for leg in ride], input_output_aliases=aliases,
        scratch_shapes=list(scratch_shapes) + _exchange_sems(ride), **kw)(*args, *[leg.src for leg in ride])
    return res[:n_out], res[n_out:]


def _seq(n=1):
    return pltpu.CompilerParams(dimension_semantics=("arbitrary",) * n)


def _dot(a, b):
    return jnp.dot(a, b, preferred_element_type=F32)


def _dot_nt(a, b):
    return lax.dot_general(a, b, (((1,), (1,)), ((), ())), preferred_element_type=F32)


def _dot_tn(a, b):
    return lax.dot_general(a, b, (((0,), (0,)), ((), ())), preferred_element_type=F32)


def _bf(x):
    return x.astype(BF16)


def _sigmoid(x):
    return 0.5 * jnp.tanh(0.5 * x) + 0.5


def _log1p(z):
    u = 1.0 + z
    return jnp.where(u == 1.0, z, jnp.log(u) * (z / jnp.where(u == 1.0, 1.0, u - 1.0)))


def _softplus(x):
    return jnp.maximum(x, 0.0) + _log1p(jnp.exp(-jnp.abs(x)))


def _log_sigmoid(x):
    return -_softplus(-x)


def _one_minus_sq(a, log_a):
    x = 2.0 * log_a
    small = -x * (1.0 + x * (0.5 + x * (1.0 / 6.0)))
    return jnp.where(x > -0.004, small, 1.0 - a * a)


def _dsilu(x, s):
    return s * (1.0 + x * (1.0 - s))


def _rowsum(x):
    return jnp.sum(x, axis=1, keepdims=True)


def _colsum(x):
    return jnp.sum(x, axis=0, keepdims=True)


def _shift_down(win, s):
    return win if s == 0 else pltpu.roll(win, s, 0)


def _shift_up(win, s):
    return win if s == 0 else pltpu.roll(win, win.shape[0] - s, 0)


def _conv_taps(win):
    return [_shift_down(win, CONV_WIDTH - 1 - k)[HALO:] for k in range(CONV_WIDTH)]


def _conv_fwd(taps, w_ref, b_ref):
    acc = b_ref[...] + w_ref[CONV_WIDTH - 1:CONV_WIDTH, :] * taps[CONV_WIDTH - 1]
    for k in range(CONV_WIDTH - 1):
        acc = acc + w_ref[k:k + 1, :] * taps[k]
    return acc


def _split3(x):
    hi = _bf(x)
    r1 = x - hi.astype(F32)
    mid = _bf(r1)
    lo = _bf(r1 - mid.astype(F32))
    return hi, mid, lo


def _tri_dot_left(tri, x):
    hi, mid, lo = _split3(x)
    return _dot(tri, hi) + _dot(tri, mid) + _dot(tri, lo)


def _tri_dot_right(x, tri):
    hi, mid, lo = _split3(x)
    return _dot(hi, tri) + _dot(mid, tri) + _dot(lo, tri)


def _tile(n, want):
    t = min(n, want)
    assert n % t == 0
    return t


def _ln_inproj(x, g, scale, shift, w4, ride=None):
    s_len, d = x.shape
    nj, _, nsh = w4.shape
    tm = _tile(s_len, ROWS_MATMUL)
    ni = s_len // tm

    def body(x_ref, g_ref, sc_ref, sh_ref, w_ref, h_ref, u_ref, hs):
        rows = pl.ds(pl.multiple_of(pl.program_id(1) * tm, tm), tm)

        @pl.when(pl.program_id(0) == 0)
        def _():
            xv = x_ref[...]
            r = lax.rsqrt(jnp.mean(xv * xv, axis=-1, keepdims=True) + EPS)
            hv = (xv * r * g_ref[...]) * (1.0 + sc_ref[...]) + sh_ref[...]
            hs[rows, :] = _bf(hv)
            h_ref[...] = hs[rows, :]

        u_ref[...] = _dot(hs[rows, :], w_ref[0])

    vec = pl.BlockSpec((1, d), lambda j, i: (0, 0))
    once = pl.BlockSpec((tm, d), lambda j, i: (jnp.where(j == 0, i, ni - 1), 0))
    return _pcall_ride(
        body, ride, name="ln_inproj", grid=(nj, ni),
        in_specs=[once, vec, vec, vec, pl.BlockSpec((1, d, nsh), lambda j, i: (j, 0, 0))],
        out_specs=[once, pl.BlockSpec((tm, nsh), lambda j, i: (i, j))],
        out_shape=[jax.ShapeDtypeStruct((s_len, d), BF16), jax.ShapeDtypeStruct((s_len, nj * nsh), F32)],
        scratch_shapes=[pltpu.VMEM((s_len, d), BF16)],
        compiler_params=_seq(2),
        args=(x, g, scale, shift, w4))


def _rg_gates(xc, wa_ref, ba_ref, wx_ref, bx_ref, lam_ref):
    heads, hd, _ = wa_ref.shape
    xb = _bf(xc)
    ga = jnp.concatenate([_dot(xb[:, h * hd:(h + 1) * hd], wa_ref[h]) for h in range(heads)], axis=1) + ba_ref[...]
    gx = jnp.concatenate([_dot(xb[:, h * hd:(h + 1) * hd], wx_ref[h]) for h in range(heads)], axis=1) + bx_ref[...]
    r = _sigmoid(ga)
    ig = _sigmoid(gx)
    sp = _softplus(-lam_ref[...])
    log_a = (-RG_C) * r * sp
    a = jnp.exp(log_a)
    mult = jnp.sqrt(_one_minus_sq(a, log_a))
    return r, ig, sp, log_a, a, mult


def _scan_groups(a, u, reverse):
    n, c = a.shape
    a = a.reshape(n // 8, 8, c)
    u = u.reshape(n // 8, 8, c)
    row = lax.broadcasted_iota(jnp.int32, a.shape, 1)
    for k in (1, 2, 4):
        sft = 8 - k if reverse else k
        a_sh, u_sh = pltpu.roll(a, sft, 1), pltpu.roll(u, sft, 1)
        ok = row < 8 - k if reverse else row >= k
        u = jnp.where(ok, a * u_sh + u, u)
        a = jnp.where(ok, a * a_sh, a)
    return a.reshape(n, c), u.reshape(n, c)


def _rg_fwd(u, conv_w, conv_b, wa_b, ba, wx_b, bx, lam, ride=None):
    s_len = u.shape[0]
    d = conv_w.shape[1]
    tm = _tile(s_len, ROWS_VECTOR)
    per = tm // HALO

    def body(x_ref, xp_ref, z_ref, cw_ref, cb_ref, wa_ref, ba_ref, wx_ref, bx_ref, lam_ref,
             hh_ref, y_ref, carry):
        i = pl.program_id(0)

        @pl.when(i == 0)
        def _():
            carry[...] = jnp.zeros_like(carry)

        prev = jnp.where(i == 0, 0.0, xp_ref[...])
        xc = _conv_fwd(_conv_taps(jnp.concatenate([prev, x_ref[...]], axis=0)), cw_ref, cb_ref)
        _, ig, _, _, a, mult = _rg_gates(xc, wa_ref, ba_ref, wx_ref, bx_ref, lam_ref)
        ca, cu = _scan_groups(a, mult * (ig * xc), reverse=False)
        c = carry[0:1, :]
        for j in range(tm // 8):
            blk = ca[j * 8:(j + 1) * 8] * c + cu[j * 8:(j + 1) * 8]
            hh_ref[j * 8:(j + 1) * 8, :] = blk
            c = blk[7:8]
        carry[0:1, :] = c
        z = z_ref[...]
        y_ref[0] = _bf(hh_ref[...] * (z * _sigmoid(z)))

    vec = pl.BlockSpec((1, d), lambda i: (0, 0))
    whole3 = lambda a: pl.BlockSpec(a.shape, lambda i: (0, 0, 0))
    return _pcall_ride(
        body, ride, name="rg_fwd", grid=(s_len // tm,),
        in_specs=[pl.BlockSpec((tm, d), lambda i: (i, 0)),
                  pl.BlockSpec((HALO, d), lambda i: (jnp.maximum(i * per - 1, 0), 0)),
                  pl.BlockSpec((tm, d), lambda i: (i, 1)),
                  pl.BlockSpec((CONV_WIDTH, d), lambda i: (0, 0)), vec,
                  whole3(wa_b), vec, whole3(wx_b), vec, vec],
        out_specs=[pl.BlockSpec((tm, d), lambda i: (i, 0)), pl.BlockSpec((1, tm, d), lambda i: (0, i, 0))],
        out_shape=[jax.ShapeDtypeStruct((s_len, d), F32), jax.ShapeDtypeStruct((2, s_len, d), BF16)],
        scratch_shapes=[pltpu.VMEM((8, d), F32)],
        compiler_params=_seq(),
        args=(u, u, u, conv_w, conv_b, wa_b, ba, wx_b, bx, lam))


def _ml_pre(u, conv_w, conv_b, wqkv_b, wif_b, wift_b, b_if, b_ift):
    s_len = u.shape[0]
    d = conv_w.shape[1]
    _, heads, hd, _ = wqkv_b.shape
    ng = 2 * heads
    tm = _tile(s_len, max(ROWS_VECTOR, ML_CHUNK))
    per = tm // HALO

    def body(x_ref, xp_ref, cw_ref, cb_ref, w_ref, wif_ref, wift_ref, bif_ref, bift_ref,
             qkv_ref, gt_ref, gtt_ref, bc_ref, bct_ref):
        i = pl.program_id(0)
        prev = jnp.where(i == 0, 0.0, xp_ref[...])
        xm = x_ref[...]
        pre = _conv_fwd(_conv_taps(jnp.concatenate([prev, xm], axis=0)), cw_ref, cb_ref)
        xcb = _bf(pre * _sigmoid(pre))
        xmb = _bf(xm)
        for h in range(heads):
            hs = slice(h * hd, (h + 1) * hd)
            qkv_ref[0, :, hs] = _bf(_dot(xcb[:, hs], w_ref[0, h]))
            qkv_ref[1, :, hs] = _bf(_dot(xcb[:, hs], w_ref[1, h]))
            qkv_ref[2, :, hs] = _bf(_dot(xmb[:, hs], w_ref[2, h]))
        qb, kb, vb = qkv_ref[0], qkv_ref[1], qkv_ref[2]
        gt = (_dot(qb, wif_ref[0:d, :]) + _dot(kb, wif_ref[d:2 * d, :]) + _dot(vb, wif_ref[2 * d:3 * d, :])
              + bif_ref[...])
        gtt = (_dot_nt(wift_ref[:, 0:d], qb) + _dot_nt(wift_ref[:, d:2 * d], kb)
               + _dot_nt(wift_ref[:, 2 * d:3 * d], vb) + bift_ref[...])
        gt_ref[...] = gt
        gtt_ref[...] = gtt
        r = lax.broadcasted_iota(jnp.int32, (tm, tm), 0)
        c = lax.broadcasted_iota(jnp.int32, (tm, tm), 1)
        same = (r // ML_CHUNK) == (c // ML_CHUNK)
        bc_ref[...] = _tri_dot_left(((r >= c) & same).astype(BF16), _log_sigmoid(gt))
        bct_ref[...] = _tri_dot_right(_log_sigmoid(gtt), ((r <= c) & same).astype(BF16))

    vec = pl.BlockSpec((1, d), lambda i: (0, 0))
    whole2 = lambda a: pl.BlockSpec(a.shape, lambda i: (0, 0))
    col = pl.BlockSpec((tm, ng), lambda i: (i, 0))
    row = pl.BlockSpec((ng, tm), lambda i: (0, i))
    return _pcall(
        body, name="ml_pre", grid=(s_len // tm,),
        in_specs=[pl.BlockSpec((tm, d), lambda i: (i, 2)),
                  pl.BlockSpec((HALO, d), lambda i: (jnp.maximum(i * per - 1, 0), 2)),
                  pl.BlockSpec((CONV_WIDTH, d), lambda i: (0, 0)), vec,
                  pl.BlockSpec(wqkv_b.shape, lambda i: (0, 0, 0, 0)), whole2(wif_b), whole2(wift_b), whole2(b_if),
                  whole2(b_ift)],
        out_specs=[pl.BlockSpec((3, tm, d), lambda i: (0, i, 0)), col, row, col, row],
        out_shape=[jax.ShapeDtypeStruct((3, s_len, d), BF16), jax.ShapeDtypeStruct((s_len, ng), F32),
                   jax.ShapeDtypeStruct((ng, s_len), F32), jax.ShapeDtypeStruct((s_len, ng), F32),
                   jax.ShapeDtypeStruct((ng, s_len), F32)],
        compiler_params=_seq(),
    )(u, u, conv_w, conv_b, wqkv_b, wif_b, wift_b, b_if, b_ift)


def _chunk_gates(gt, gtt, bc, bct, h, heads):
    li_c = gt[:, h:h + 1]
    li_r = gtt[h:h + 1, :]
    gf_c = gt[:, heads + h:heads + h + 1]
    b_c = bc[:, heads + h:heads + h + 1]
    b_r = bct[heads + h:heads + h + 1, :]
    return li_c, li_r, gf_c, b_c, b_r


def _chunk_weights(li_c, li_r, b_c, b_r, m_prev, causal):
    lc = b_c.shape[0]
    b_last = b_c[lc - 1:lc, :]
    dmat = jnp.where(causal, b_c - b_r + li_r, -jnp.inf)
    m_inter = b_c + m_prev
    m_t = jnp.maximum(m_inter, jnp.max(dmat, axis=1, keepdims=True))
    w_intra = jnp.exp(dmat - m_t)
    w_inter = jnp.exp(m_inter - m_t)
    g_c = b_last - b_c + li_c
    m_new = jnp.maximum(b_last + m_prev, jnp.max(g_c, axis=0, keepdims=True))
    w_state = jnp.exp(g_c - m_new)
    decay = jnp.exp(b_last + m_prev - m_new)
    return m_t, w_intra, w_inter, m_new, w_state, decay


def _tri_masks(lc):
    r = lax.broadcasted_iota(jnp.int32, (lc, lc), 0)
    c = lax.broadcasted_iota(jnp.int32, (lc, lc), 1)
    causal = r >= c
    return causal, causal.astype(BF16), (r <= c).astype(BF16)


def _mlstm_fwd(qkv, gates, u, ml_g, ycat, ride=None):
    _, s_len, d = qkv.shape
    ng = gates[0].shape[1]
    heads = ng // 2
    hd = d // heads
    lc = ML_CHUNK
    nc = s_len // lc
    kscale = hd ** -0.5

    def body(qkv_ref, gt_ref, gtt_ref, bc_ref, bct_ref, o_ref, z_ref, g_ref, _, cell_ref, y_ref, cst_ref, nst_ref,
             mst_ref, cs, ns, ms):
        @pl.when(pl.program_id(0) == 0)
        def _():
            cs[...] = jnp.zeros_like(cs)
            ns[...] = jnp.zeros_like(ns)
            ms[...] = jnp.zeros_like(ms)

        causal = _tri_masks(lc)[0]
        gtv, gttv, bcv, bctv = gt_ref[...], gtt_ref[...], bc_ref[...], bct_ref[...]
        old = [(cs[h], ns[h], ms[h]) for h in range(heads)]
        new, cells, ys = [], [], []
        for h in range(heads):
            hs = slice(h * hd, (h + 1) * hd)
            li_c, li_r, _, b_c, b_r = _chunk_gates(gtv, gttv, bcv, bctv, h, heads)
            c_old, n_old, m_old = old[h]
            m_prev = m_old[:, 0:1]
            m_t, w_intra, w_inter, m_new, w_state, decay = _chunk_weights(li_c, li_r, b_c, b_r, m_prev, causal)
            qb = qkv_ref[0, :, hs]
            ks = qkv_ref[1, :, hs].astype(F32) * kscale
            kb = _bf(ks)
            vb = qkv_ref[2, :, hs]
            s = _dot_nt(qb, kb) * w_intra
            num = _dot(_bf(s), vb) + w_inter * _dot(qb, _bf(c_old))
            den = _rowsum(s) + w_inter * _rowsum(qb.astype(F32) * n_old)
            cell = num / jnp.maximum(jnp.abs(den), jnp.exp(-m_t))
            kw = ks * w_state
            new.append((decay * c_old + _dot_tn(_bf(kw), vb), decay * n_old + _colsum(kw),
                        jnp.broadcast_to(m_new, m_old.shape)))
            cells.append(cell)
            hm = _sigmoid(o_ref[:, hs]) * cell
            hn = hm * lax.rsqrt(jnp.mean(hm * hm, axis=-1, keepdims=True) + EPS)
            z = z_ref[:, hs]
            ys.append(_bf((hn * g_ref[:, hs]) * (z * _sigmoid(z))))
        for h in range(heads):
            cst_ref[0, h] = _bf(old[h][0])
            nst_ref[0, h] = old[h][1]
            mst_ref[0, h] = old[h][2]
            cs[h], ns[h], ms[h] = new[h]
        cell_ref[...] = jnp.concatenate(cells, axis=1)
        y_ref[0] = jnp.concatenate(ys, axis=1)

    row = pl.BlockSpec((lc, d), lambda c: (c, 0))
    gcol = pl.BlockSpec((lc, ng), lambda c: (c, 0))
    grow = pl.BlockSpec((ng, lc), lambda c: (0, c))
    return _pcall_ride(
        body, ride, name="mlstm_fwd", grid=(nc,),
        in_specs=[pl.BlockSpec((3, lc, d), lambda c: (0, c, 0)), gcol, grow, gcol, grow,
                  pl.BlockSpec((lc, d), lambda c: (c, 3)), pl.BlockSpec((lc, d), lambda c: (c, 4)),
                  pl.BlockSpec((1, d), lambda c: (0, 0)), pl.BlockSpec(memory_space=pl.ANY)],
        out_specs=[row, pl.BlockSpec((1, lc, d), lambda c: (1, c, 0)),
                   pl.BlockSpec((1, heads, hd, hd), lambda c: (c, 0, 0, 0)),
                   pl.BlockSpec((1, heads, 1, hd), lambda c: (c, 0, 0, 0)),
                   pl.BlockSpec((1, heads, 1, 128), lambda c: (c, 0, 0, 0))],
        out_shape=[jax.ShapeDtypeStruct((s_len, d), F32), jax.ShapeDtypeStruct(ycat.shape, BF16),
                   jax.ShapeDtypeStruct((nc, heads, hd, hd), BF16),
                   jax.ShapeDtypeStruct((nc, heads, 1, hd), F32),
                   jax.ShapeDtypeStruct((nc, heads, 1, 128), F32)],
        scratch_shapes=[pltpu.VMEM((heads, hd, hd), F32), pltpu.VMEM((heads, 1, hd), F32),
                        pltpu.VMEM((heads, 1, 128), F32)],
        input_output_aliases={8: 1},
        compiler_params=_seq(),
        args=(qkv, *gates, u, u, ml_g, ycat))


def _out_proj(ycat, w_out_b, x, gate, ride=None):
    s_len, d = x.shape
    tm = _tile(s_len, ROWS_MATMUL)

    def body(a_ref, w_ref, x_ref, g_ref, y_ref, xn_ref):
        y = _dot(a_ref[0], w_ref[0:d, :]) + _dot(a_ref[1], w_ref[d:2 * d, :])
        y_ref[...] = y
        xn_ref[...] = x_ref[...] + g_ref[...] * y

    row = pl.BlockSpec((tm, d), lambda i: (i, 0))
    return _pcall_ride(
        body, ride, name="out_proj", grid=(s_len // tm,),
        in_specs=[pl.BlockSpec((2, tm, d), lambda i: (0, i, 0)), pl.BlockSpec((2 * d, d), lambda i: (0, 0)), row,
                  pl.BlockSpec((1, d), lambda i: (0, 0))],
        out_specs=[row, row],
        out_shape=[jax.ShapeDtypeStruct((s_len, d), F32)] * 2,
        compiler_params=_seq(),
        args=(ycat, w_out_b, x, gate))


def _out_proj_loss(ycat, w_out_b, x, gate, g, target):
    s_len, d = x.shape
    tm = _tile(s_len, ROWS_IN_BWD)

    def body(a_ref, w_ref, x_ref, gate_ref, g_ref, t_ref, y_ref, dx_ref, dg_ref, loss_ref):
        @pl.when(pl.program_id(0) == 0)
        def _():
            dg_ref[...] = jnp.zeros_like(dg_ref)
            loss_ref[...] = jnp.zeros_like(loss_ref)

        y = _dot(a_ref[0], w_ref[0:d, :]) + _dot(a_ref[1], w_ref[d:2 * d, :])
        y_ref[...] = y
        xv = x_ref[...] + gate_ref[...] * y
        r = lax.rsqrt(jnp.mean(xv * xv, axis=-1, keepdims=True) + EPS)
        xn = xv * r
        err = xn * g_ref[...] - t_ref[...]
        loss_ref[...] += 0.5 * jnp.sum(jnp.mean(err * err, axis=-1, keepdims=True))
        dout = err * (1.0 / d)
        dg_ref[...] += _colsum(dout * xn)
        dxn = dout * g_ref[...]
        dx_ref[...] = r * (dxn - xn * jnp.mean(dxn * xn, axis=-1, keepdims=True))

    row = pl.BlockSpec((tm, d), lambda i: (i, 0))
    vec = pl.BlockSpec((1, d), lambda i: (0, 0))
    return _pcall(
        body, name="out_proj_loss", grid=(s_len // tm,),
        in_specs=[pl.BlockSpec((2, tm, d), lambda i: (0, i, 0)), pl.BlockSpec((2 * d, d), lambda i: (0, 0)), row, vec,
                  vec, row],
        out_specs=[row, row, vec, pl.BlockSpec((1, 128), lambda i: (0, 0))],
        out_shape=[jax.ShapeDtypeStruct((s_len, d), F32), jax.ShapeDtypeStruct((s_len, d), F32),
                   jax.ShapeDtypeStruct((1, d), F32), jax.ShapeDtypeStruct((1, 128), F32)],
        compiler_params=_seq(),
    )(ycat, w_out_b, x, gate, g, target)


def _out_bwd(dxn, y, gate, w_out_b):
    s_len, d = dxn.shape
    tm = _tile(s_len, ROWS_MATMUL)

    def body(dx_ref, y_ref, g_ref, w_ref, dg_ref, dy_ref, dc_ref):
        @pl.when(pl.program_id(0) == 0)
        def _():
            dg_ref[...] = jnp.zeros_like(dg_ref)

        dx = dx_ref[...]
        dg_ref[...] += _colsum(dx * y_ref[...])
        dy = _bf(g_ref[...] * dx)
        dy_ref[...] = dy
        dc_ref[0] = _dot_nt(dy, w_ref[0:d, :])
        dc_ref[1] = _dot_nt(dy, w_ref[d:2 * d, :])

    row = pl.BlockSpec((tm, d), lambda i: (i, 0))
    vec = pl.BlockSpec((1, d), lambda i: (0, 0))
    return _pcall(
        body, name="out_bwd", grid=(s_len // tm,),
        in_specs=[row, row, vec, pl.BlockSpec((2 * d, d), lambda i: (0, 0))],
        out_specs=[vec, row, pl.BlockSpec((2, tm, d), lambda i: (0, i, 0))],
        out_shape=[jax.ShapeDtypeStruct((1, d), F32), jax.ShapeDtypeStruct((s_len, d), BF16),
                   jax.ShapeDtypeStruct((2, s_len, d), F32)],
        compiler_params=_seq(),
    )(dxn, y, gate, w_out_b)


def _grad_matmul(a3, b3, nblk, a_idx, b_idx, out_shape, out_block, out_idx, ride=None):
    _, s_len, m = a3.shape
    n = b3.shape[2]
    tk = _tile(s_len, ROWS_GRAD_MATMUL)

    def body(a_ref, b_ref, o_ref):
        @pl.when(pl.program_id(1) == 0)
        def _():
            o_ref[...] = jnp.zeros_like(o_ref)

        o_ref[...] += _dot_tn(a_ref[0], b_ref[0])

    (out,), got = _pcall_ride(
        body, ride, name="grad_matmul", grid=(nblk, s_len // tk),
        in_specs=[pl.BlockSpec((1, tk, m), lambda p, t: (a_idx(p), t, 0)),
                  pl.BlockSpec((1, tk, n), lambda p, t: (b_idx(p), t, 0))],
        out_specs=[pl.BlockSpec((None,) + out_block, lambda p, t: (0,) + out_idx(p))],
        out_shape=[jax.ShapeDtypeStruct((1,) + out_shape, F32)],
        compiler_params=_seq(2), args=(a3, b3))
    return out, got


DU_PLANE = (2, 3, 4, 0, 1)


def _mlstm_bwd(qkv, gates, cst, nst, mst, cell, u, ml_g, d_ycat, wif_b, ride=None):
    _, s_len, d = qkv.shape
    ng = gates[0].shape[1]
    heads = ng // 2
    hd = d // heads
    lc = ML_CHUNK
    nc = s_len // lc
    kscale = hd ** -0.5

    def body(qkv_ref, gt_ref, gtt_ref, bc_ref, bct_ref, cst_ref, nst_ref, mst_ref, cell_ref, o_ref, z_ref, g_ref, dy_ref,
             wif_ref, dqkv_ref, dgt_ref, dbif_ref, du_ref, dg_ref, dcs, dns):
        @pl.when(pl.program_id(0) == 0)
        def _():
            dbif_ref[...] = jnp.zeros_like(dbif_ref)
            dcs[...] = jnp.zeros_like(dcs)
            dns[...] = jnp.zeros_like(dns)
            dg_ref[...] = jnp.zeros_like(dg_ref)

        causal, tril, triu = _tri_masks(lc)
        tril_strict = (tril.astype(F32) - (tril * triu).astype(F32)).astype(BF16)
        gtv, gttv, bcv, bctv = gt_ref[...], gtt_ref[...], bc_ref[...], bct_ref[...]
        lane = lax.broadcasted_iota(jnp.int32, (lc, ng), 1)
        dli_all = jnp.zeros((lc, ng), F32)
        from_later = jnp.zeros((lc, ng), F32)
        from_earlier = jnp.zeros((lc, ng), F32)
        across_all = jnp.zeros((1, ng), F32)
        old = [(dcs[h], dns[h]) for h in range(heads)]
        new, d_o, d_z, d_g, dqs, dks, dvs = [], [], [], [], [], [], []
        for h in range(heads):
            hs = slice(h * hd, (h + 1) * hd)
            li_c, li_r, gf_c, b_c, b_r = _chunk_gates(gtv, gttv, bcv, bctv, h, heads)
            m_prev = mst_ref[0, h][:, 0:1]
            m_t, w_intra, w_inter, _, w_state, decay = _chunk_weights(li_c, li_r, b_c, b_r, m_prev, causal)
            qb = qkv_ref[0, :, hs]
            qf = qb.astype(F32)
            ks = qkv_ref[1, :, hs].astype(F32) * kscale
            kb = _bf(ks)
            vb = qkv_ref[2, :, hs]
            c_b = cst_ref[0, h]
            n_old = nst_ref[0, h]
            s = _dot_nt(qb, kb) * w_intra
            den = _rowsum(s) + w_inter * _rowsum(qf * n_old)
            floor = jnp.exp(-m_t)
            dstab = jnp.maximum(jnp.abs(den), floor)
            cell = cell_ref[:, hs]
            o = o_ref[:, hs]
            so = _sigmoid(o)
            hm = so * cell
            rinv = lax.rsqrt(jnp.mean(hm * hm, axis=-1, keepdims=True) + EPS)
            hn = hm * rinv
            z = z_ref[:, hs]
            sgz = _sigmoid(z)
            sz = z * sgz
            gh = g_ref[:, hs]
            dy = dy_ref[0, :, hs]
            d_z.append(_bf(dy * (hn * gh) * _dsilu(z, sgz)))
            d_g.append(_colsum(dy * hn * sz))
            dhn = dy * gh * sz
            dhm = rinv * (dhn - hn * jnp.mean(dhn * hn, axis=-1, keepdims=True))
            d_o.append(_bf(dhm * cell * so * (1.0 - so)))
            dcell = dhm * so
            dnum = dcell / dstab
            dnb = _bf(dnum)
            dden = -_rowsum(dcell * cell) / dstab * jnp.where(jnp.abs(den) > floor, jnp.where(den > 0.0, 1.0, -1.0), 0.0)
            dst = _dot_nt(dnb, vb) + dden
            dsdb = _bf(dst * w_intra)
            dc_out, dn_out = old[h]
            dcb = _bf(dc_out)
            dq_inter = w_inter * (_dot_nt(dnb, c_b) + dden * n_old)
            dk_inter = w_state * (_dot_nt(vb, dcb) + dn_out)
            dq = _dot(dsdb, kb) + dq_inter
            dk = _dot_tn(dsdb, qb) + dk_inter
            dv = _dot_tn(_bf(s), dnb) + _dot(_bf(ks * w_state), dcb)
            wq = w_inter * qf
            new.append((decay * dc_out + _dot_tn(_bf(wq), dnb), decay * dn_out + _colsum(wq * dden)))
            pmat = dst * s
            p_rows = _rowsum(pmat)
            p_cols = _rowsum(pmat.T)
            q_in = _rowsum(qf * dq_inter)
            k_in = _rowsum(ks * dk_inter)
            across = decay * (jnp.sum(dc_out * c_b.astype(F32), keepdims=True) + jnp.sum(dn_out * n_old, keepdims=True))
            dli_all = dli_all + jnp.where(lane == h, p_cols + k_in, 0.0)
            from_later = from_later + jnp.where(lane == heads + h, p_rows - p_cols + q_in, 0.0)
            from_earlier = from_earlier + jnp.where(lane == heads + h, k_in, 0.0)
            across_all = across_all + jnp.where(lane[0:1] == heads + h, across, 0.0)
            dqs.append(dq)
            dks.append(dk * kscale)
            dvs.append(dv)
        for h in range(heads):
            dcs[h], dns[h] = new[h]
        du_ref[0] = jnp.concatenate(d_o, axis=1)
        du_ref[1] = jnp.concatenate(d_z, axis=1)
        dg_ref[...] += jnp.concatenate(d_g, axis=1)
        dlf = _tri_dot_left(triu, from_later) + _tri_dot_left(tril_strict, from_earlier) + across_all
        dgt = dli_all + dlf * _sigmoid(-gtv)
        dgt_ref[...] = dgt
        dbif_ref[...] += _colsum(dgt)
        dgb = _bf(dgt)
        dqkv_ref[0] = _bf(jnp.concatenate(dqs, axis=1) + _dot_nt(dgb, wif_ref[0:d, :]))
        dqkv_ref[1] = _bf(jnp.concatenate(dks, axis=1) + _dot_nt(dgb, wif_ref[d:2 * d, :]))
        dqkv_ref[2] = _bf(jnp.concatenate(dvs, axis=1) + _dot_nt(dgb, wif_ref[2 * d:3 * d, :]))

    rev = lambda c: nc - 1 - c
    row = pl.BlockSpec((lc, d), lambda c: (rev(c), 0))
    gcol = pl.BlockSpec((lc, ng), lambda c: (rev(c), 0))
    grow = pl.BlockSpec((ng, lc), lambda c: (0, rev(c)))
    return _pcall_ride(
        body, ride, name="mlstm_bwd", grid=(nc,),
        in_specs=[pl.BlockSpec((3, lc, d), lambda c: (0, rev(c), 0)), gcol, grow, gcol, grow,
                  pl.BlockSpec((1, heads, hd, hd), lambda c: (rev(c), 0, 0, 0)),
                  pl.BlockSpec((1, heads, 1, hd), lambda c: (rev(c), 0, 0, 0)),
                  pl.BlockSpec((1, heads, 1, 128), lambda c: (rev(c), 0, 0, 0)),
                  row, pl.BlockSpec((lc, d), lambda c: (rev(c), 3)), pl.BlockSpec((lc, d), lambda c: (rev(c), 4)),
                  pl.BlockSpec((1, d), lambda c: (0, 0)), pl.BlockSpec((1, lc, d), lambda c: (1, rev(c), 0)),
                  pl.BlockSpec((3 * d, ng), lambda c: (0, 0))],
        out_specs=[pl.BlockSpec((3, lc, d), lambda c: (0, rev(c), 0)), pl.BlockSpec((lc, ng), lambda c: (rev(c), 0)),
                   pl.BlockSpec((1, ng), lambda c: (0, 0)), pl.BlockSpec((2, lc, d), lambda c: (0, rev(c), 0)),
                   pl.BlockSpec((1, d), lambda c: (0, 0))],
        out_shape=[jax.ShapeDtypeStruct((3, s_len, d), BF16), jax.ShapeDtypeStruct((s_len, ng), F32),
                   jax.ShapeDtypeStruct((1, ng), F32), jax.ShapeDtypeStruct((5, s_len, d), BF16),
                   jax.ShapeDtypeStruct((1, d), F32)],
        scratch_shapes=[pltpu.VMEM((heads, hd, hd), F32), pltpu.VMEM((heads, 1, hd), F32)],
        compiler_params=_seq(),
        args=(qkv, *gates, cst, nst, mst, cell, u, u, ml_g, d_ycat, wif_b))


def _conv_bwd_tile(dp, later, taps, cw_ref, gw_ref, gb_ref):
    tm = dp.shape[0]
    dwin = jnp.concatenate([dp, later[...]], axis=0)
    later[...] = dp[0:HALO]
    acc = cw_ref[CONV_WIDTH - 1:CONV_WIDTH, :] * dp
    for k in range(CONV_WIDTH):
        if k < CONV_WIDTH - 1:
            acc = acc + cw_ref[k:k + 1, :] * _shift_up(dwin, CONV_WIDTH - 1 - k)[0:tm]
        gw_ref[k:k + 1, :] += _colsum(dp * taps[k])
    gb_ref[...] += _colsum(dp)
    return acc


def _ml_pre_bwd(dqkv, u, conv_w, conv_b, wqkv_b, du):
    s_len = u.shape[0]
    d = conv_w.shape[1]
    _, heads, hd, _ = wqkv_b.shape
    tm = _tile(s_len, ROWS_VECTOR)
    per = tm // HALO
    nt = s_len // tm

    def body(dqkv_ref, x_ref, xp_ref, cw_ref, cb_ref, w_ref, _, dx_ref, gw_ref, gcw_ref, gcb_ref, later, dps, dxs):
        i = pl.program_id(0)

        @pl.when(i == 0)
        def _():
            gw_ref[...] = jnp.zeros_like(gw_ref)
            gcw_ref[...] = jnp.zeros_like(gcw_ref)
            gcb_ref[...] = jnp.zeros_like(gcb_ref)
            later[...] = jnp.zeros_like(later)

        prev = jnp.where(i == nt - 1, 0.0, xp_ref[...])
        xm = x_ref[...]
        taps = _conv_taps(jnp.concatenate([prev, xm], axis=0))
        pre = _conv_fwd(taps, cw_ref, cb_ref)
        sg = _sigmoid(pre)
        xcb = _bf(pre * sg)
        xmb = _bf(xm)
        for h in range(heads):
            hs = slice(h * hd, (h + 1) * hd)
            dqh, dkh, dvh = dqkv_ref[0, :, hs], dqkv_ref[1, :, hs], dqkv_ref[2, :, hs]
            dxc = _dot_nt(dqh, w_ref[0, h]) + _dot_nt(dkh, w_ref[1, h])
            dps[:, hs] = dxc * _dsilu(pre[:, hs], sg[:, hs])
            dxs[:, hs] = _dot_nt(dvh, w_ref[2, h])
            gw_ref[0, h] += _dot_tn(xcb[:, hs], dqh)
            gw_ref[1, h] += _dot_tn(xcb[:, hs], dkh)
            gw_ref[2, h] += _dot_tn(xmb[:, hs], dvh)
        dx_ref[0] = _bf(_conv_bwd_tile(dps[...], later, taps, cw_ref, gcw_ref, gcb_ref) + dxs[...])

    rev = lambda i: nt - 1 - i
    vec = pl.BlockSpec((1, d), lambda i: (0, 0))
    cwb = pl.BlockSpec((CONV_WIDTH, d), lambda i: (0, 0))
    whole4 = pl.BlockSpec(wqkv_b.shape, lambda i: (0, 0, 0, 0))
    return _pcall(
        body, name="ml_pre_bwd", grid=(nt,),
        in_specs=[pl.BlockSpec((3, tm, d), lambda i: (0, rev(i), 0)), pl.BlockSpec((tm, d), lambda i: (rev(i), 2)),
                  pl.BlockSpec((HALO, d), lambda i: (jnp.maximum(rev(i) * per - 1, 0), 2)),
                  cwb, vec, whole4, pl.BlockSpec(memory_space=pl.ANY)],
        out_specs=[pl.BlockSpec((1, tm, d), lambda i: (DU_PLANE[2], rev(i), 0)), whole4, cwb, vec],
        out_shape=[jax.ShapeDtypeStruct(du.shape, BF16), jax.ShapeDtypeStruct(wqkv_b.shape, F32),
                   jax.ShapeDtypeStruct((CONV_WIDTH, d), F32), jax.ShapeDtypeStruct((1, d), F32)],
        scratch_shapes=[pltpu.VMEM((HALO, d), F32), pltpu.VMEM((tm, d), F32), pltpu.VMEM((tm, d), F32)],
        input_output_aliases={6: 0},
        compiler_params=_seq(),
    )(dqkv, u, u, conv_w, conv_b, wqkv_b, du)


def _rg_bwd(d_ycat, u, hh, conv_w, conv_b, wa_b, ba, wx_b, bx, lam, du):
    s_len = u.shape[0]
    d = conv_w.shape[1]
    heads, hd, _ = wa_b.shape
    tm = _tile(s_len, ROWS_VECTOR)
    per = tm // HALO
    nt = s_len // tm

    def body(dy_ref, x_ref, xp_ref, z_ref, hh_ref, hp_ref, cw_ref, cb_ref, wa_ref, ba_ref, wx_ref, bx_ref, lam_ref, _,
             du_ref, gwa_ref, gwx_ref, gba_ref, gbx_ref, glam_ref, gcw_ref, gcb_ref, carry, gbuf, later, dxcs):
        i = pl.program_id(0)
        first = i == nt - 1

        @pl.when(i == 0)
        def _():
            carry[...] = jnp.zeros_like(carry)
            later[...] = jnp.zeros_like(later)
            gwa_ref[...] = jnp.zeros_like(gwa_ref)
            gwx_ref[...] = jnp.zeros_like(gwx_ref)
            gba_ref[...] = jnp.zeros_like(gba_ref)
            gbx_ref[...] = jnp.zeros_like(gbx_ref)
            glam_ref[...] = jnp.zeros_like(glam_ref)
            gcw_ref[...] = jnp.zeros_like(gcw_ref)
            gcb_ref[...] = jnp.zeros_like(gcb_ref)

        prev = jnp.where(first, 0.0, xp_ref[...])
        taps = _conv_taps(jnp.concatenate([prev, x_ref[...]], axis=0))
        xc = _conv_fwd(taps, cw_ref, cb_ref)
        r, ig, sp, log_a, a, mult = _rg_gates(xc, wa_ref, ba_ref, wx_ref, bx_ref, lam_ref)
        z = z_ref[...]
        sgz = _sigmoid(z)
        dy = dy_ref[0]
        hh_v = hh_ref[...]
        du_ref[1] = _bf(dy * hh_v * _dsilu(z, sgz))
        dhh = dy * (z * sgz)
        rows = lax.broadcasted_iota(jnp.int32, a.shape, 0)
        coef = jnp.where(rows == tm - 1, carry[1:2, :], _shift_up(a, 1))
        ca, cu = _scan_groups(coef, dhh, reverse=True)
        c = carry[0:1, :]
        for j in range(tm // 8 - 1, -1, -1):
            blk = ca[j * 8:(j + 1) * 8] * c + cu[j * 8:(j + 1) * 8]
            gbuf[j * 8:(j + 1) * 8, :] = blk
            c = blk[0:1]
        carry[0:1, :] = c
        carry[1:2, :] = a[0:1]
        g = gbuf[...]
        hprev_tile = jnp.where(first, 0.0, hp_ref[...])
        hprev = _shift_down(jnp.concatenate([hprev_tile, hh_v], axis=0), 1)[HALO:]
        da = g * hprev
        gx_ = g * xc
        d_mult = gx_ * ig
        d_ig = gx_ * mult
        dxc = g * mult * ig
        dlog_a = da * a - d_mult * (a * a / mult)
        d_r = dlog_a * ((-RG_C) * sp)
        glam_ref[...] += _colsum(dlog_a * ((-RG_C) * r)) * (-_sigmoid(-lam_ref[...]))
        d_ga = d_r * r * (1.0 - r)
        d_gx = d_ig * ig * (1.0 - ig)
        gba_ref[...] += _colsum(d_ga)
        gbx_ref[...] += _colsum(d_gx)
        xb = _bf(xc)
        dgab = _bf(d_ga)
        dgxb = _bf(d_gx)
        for h in range(heads):
            hs = slice(h * hd, (h + 1) * hd)
            dxcs[:, hs] = dxc[:, hs] + _dot_nt(dgab[:, hs], wa_ref[h]) + _dot_nt(dgxb[:, hs], wx_ref[h])
            gwa_ref[h] += _dot_tn(xb[:, hs], dgab[:, hs])
            gwx_ref[h] += _dot_tn(xb[:, hs], dgxb[:, hs])
        du_ref[0] = _bf(_conv_bwd_tile(dxcs[...], later, taps, cw_ref, gcw_ref, gcb_ref))

    assert DU_PLANE[0] % 2 == 0 and DU_PLANE[1] == DU_PLANE[0] + 1
    rev = lambda i: nt - 1 - i
    row = pl.BlockSpec((tm, d), lambda i: (rev(i), 0))
    halo_prev = lambda col: pl.BlockSpec((HALO, d), lambda i: (jnp.maximum(rev(i) * per - 1, 0), col))
    vec = pl.BlockSpec((1, d), lambda i: (0, 0))
    cwb = pl.BlockSpec((CONV_WIDTH, d), lambda i: (0, 0))
    whole3 = lambda a: pl.BlockSpec(a.shape, lambda i: (0, 0, 0))
    return _pcall(
        body, name="rg_bwd", grid=(nt,),
        in_specs=[pl.BlockSpec((1, tm, d), lambda i: (0, rev(i), 0)), row, halo_prev(0),
                  pl.BlockSpec((tm, d), lambda i: (rev(i), 1)), row, halo_prev(0),
                  cwb, vec, whole3(wa_b), vec, whole3(wx_b), vec, vec, pl.BlockSpec(memory_space=pl.ANY)],
        out_specs=[pl.BlockSpec((2, tm, d), lambda i: (DU_PLANE[0] // 2, rev(i), 0)), whole3(wa_b), whole3(wa_b),
                   vec, vec, vec, cwb, vec],
        out_shape=[jax.ShapeDtypeStruct(du.shape, BF16), jax.ShapeDtypeStruct(wa_b.shape, F32),
                   jax.ShapeDtypeStruct(wa_b.shape, F32)] + [jax.ShapeDtypeStruct((1, d), F32)] * 3
        + [jax.ShapeDtypeStruct((CONV_WIDTH, d), F32), jax.ShapeDtypeStruct((1, d), F32)],
        scratch_shapes=[pltpu.VMEM((8, d), F32), pltpu.VMEM((tm, d), F32), pltpu.VMEM((HALO, d), F32),
                        pltpu.VMEM((tm, d), F32)],
        input_output_aliases={13: 0},
        compiler_params=_seq(),
    )(d_ycat, u, u, u, hh, hh, conv_w, conv_b, wa_b, ba, wx_b, bx, lam, du)


def _in_bwd(du, w4, x, dxn, g, scale, ride=None):
    s_len, d = x.shape
    tm = _tile(s_len, ROWS_IN_BWD)
    nsh_chips, _, nsh = w4.shape
    npc = du.shape[0]
    ck = d // 4
    assert nsh % ck == 0 and npc * d == nsh_chips * nsh

    def body(du_ref, w_ref, x_ref, dxn_ref, g_ref, sc_ref, dx_ref, dsh_ref, dsc_ref, dg_ref):
        @pl.when(pl.program_id(0) == 0)
        def _():
            dsh_ref[...] = jnp.zeros_like(dsh_ref)
            dsc_ref[...] = jnp.zeros_like(dsc_ref)
            dg_ref[...] = jnp.zeros_like(dg_ref)

        dh = None
        for q in range(npc * d // ck):
            col = q * ck
            p, pc = col // d, col % d
            s, sc = col // nsh, col % nsh
            t = _dot_nt(du_ref[DU_PLANE[p], :, pc:pc + ck], w_ref[s, :, sc:sc + ck])
            dh = t if dh is None else dh + t
        xv = x_ref[...]
        r = lax.rsqrt(jnp.mean(xv * xv, axis=-1, keepdims=True) + EPS)
        xn = xv * r
        gv = g_ref[...]
        onesc = 1.0 + sc_ref[...]
        dsh_ref[...] += _colsum(dh)
        dsc_ref[...] += _colsum(dh * (xn * gv))
        dg_ref[...] += _colsum(dh * xn * onesc)
        dxh = dh * (gv * onesc)
        dx_ref[...] = dxn_ref[...] + r * (dxh - xn * jnp.mean(dxh * xn, axis=-1, keepdims=True))

    row = pl.BlockSpec((tm, d), lambda i: (i, 0))
    vec = pl.BlockSpec((1, d), lambda i: (0, 0))
    return _pcall_ride(
        body, ride, name="in_bwd", grid=(s_len // tm,),
        in_specs=[pl.BlockSpec((npc, tm, d), lambda i: (0, i, 0)), pl.BlockSpec(w4.shape, lambda i: (0, 0, 0)), row, row,
                  vec, vec],
        out_specs=[row, vec, vec, vec],
        out_shape=[jax.ShapeDtypeStruct((s_len, d), F32)] + [jax.ShapeDtypeStruct((1, d), F32)] * 3,
        compiler_params=_seq(),
        args=(du, w4, x, dxn, g, scale))


def _layer_fwd(x, p, rides=None, loss_head=None):
    rides = rides or {}
    landed = {}
    ride = lambda kernel: rides[kernel](landed) if kernel in rides else None
    (h_b, u), landed["ln_inproj"] = _ln_inproj(x, p["norm_g"], p["scale"], p["shift"], p["w4"], ride("ln_inproj"))
    (hh, ycat), landed["rg_fwd"] = _rg_fwd(u, p["rg_conv_w"], p["rg_conv_b"], p["rg_wa_b"], p["rg_ba"], p["rg_wx_b"],
                                           p["rg_bx"], p["rg_lam"], ride("rg_fwd"))
    if "late" in rides:
        p = {**p, **rides["late"](landed)}
    qkv, *gates = _ml_pre(u, p["ml_conv_w"], p["ml_conv_b"], p["wqkv_b"], p["wif_b"], p["wift_b"], p["b_if"],
                          p["b_ift"])
    (cell, ycat, cst, nst, mst), landed["mlstm_fwd"] = _mlstm_fwd(qkv, gates, u, p["ml_g"], ycat, ride("mlstm_fwd"))
    if loss_head is None:
        (y, x_new), landed["out_proj"] = _out_proj(ycat, p["w_out_b"], x, p["gate"], ride("out_proj"))
    else:
        y, *x_new = _out_proj_loss(ycat, p["w_out_b"], x, p["gate"], *loss_head)
    saved = dict(x=x, h_b=h_b, u=u, hh=hh, qkv=qkv, gates=gates, cell=cell, ycat=ycat, cst=cst, nst=nst, mst=mst, y=y)
    return x_new, saved, p, landed


def _layer_bwd(dxn, p, s, rides=None):
    rides = rides or {}
    landed = {}
    ride = lambda kernel: rides[kernel](grads, landed) if kernel in rides else None
    u = s["u"]
    d = dxn.shape[1]
    d_gate, dy_b, d_ycat = _out_bwd(dxn, s["y"], p["gate"], p["w_out_b"])
    grads = dict(w_out=_grad_matmul(s["ycat"], dy_b[None], 2, lambda b: b, lambda b: 0, (2 * d, d), (d, d),
                                    lambda b: (b, 0))[0])
    (dqkv, dgt, g_b_if, du, g_ml_g), landed["mlstm_bwd"] = _mlstm_bwd(
        s["qkv"], s["gates"], s["cst"], s["nst"], s["mst"], s["cell"], u, p["ml_g"], d_ycat, p["wif_b"],
        ride("mlstm_bwd"))
    ng = dgt.shape[1]
    g_w_if = _grad_matmul(s["qkv"], _bf(dgt)[None], 3, lambda b: b, lambda b: 0, (3 * d, ng), (d, ng),
                          lambda b: (b, 0))[0][0]
    du, g_wqkv, g_ml_cw, g_ml_cb = _ml_pre_bwd(dqkv, u, p["ml_conv_w"], p["ml_conv_b"], p["wqkv_b"], du)
    du, g_wa, g_wx, g_ba, g_bx, g_lam, g_rg_cw, g_rg_cb = _rg_bwd(d_ycat, u, s["hh"], p["rg_conv_w"], p["rg_conv_b"],
                                                                  p["rg_wa_b"], p["rg_ba"], p["rg_wx_b"], p["rg_bx"],
                                                                  p["rg_lam"], du)
    grads.update(rg_conv_w=g_rg_cw, rg_conv_b=g_rg_cb, rg_w_a=g_wa, rg_b_a=g_ba, rg_w_x=g_wx, rg_b_x=g_bx,
                 rg_lambda=g_lam, ml_conv_w=g_ml_cw, ml_conv_b=g_ml_cb, ml_w_qkv=g_wqkv, ml_w_if=g_w_if, ml_b_if=g_b_if,
                 ml_norm_g=g_ml_g)
    npc = du.shape[0]
    grads["w_in"], landed["grad_w_in"] = _grad_matmul(
        s["h_b"][None], du, npc, lambda b: 0, lambda b: (b + DU_PLANE[0]) % npc, (d, npc * d), (d, d),
        lambda b: (0, b), ride("grad_w_in"))
    (dx, d_shift, d_scale, grads["norm_g"]), landed["in_bwd"] = _in_bwd(du, p["w4"], s["x"], dxn, p["norm_g"],
                                                                        p["scale"], ride("in_bwd"))
    return dx, grads, jnp.concatenate([d_shift, d_scale, d_gate], axis=1), landed


def _me():
    return lax.axis_index("x"), lax.axis_index("y"), lax.axis_index("c")


def _remote(src, dst, send_sem, recv_sem, to):
    return pltpu.make_async_remote_copy(src_ref=src, dst_ref=dst, send_sem=send_sem, recv_sem=recv_sem,
                                        device_id=to, device_id_type=MESH)


def _all_gather8(blocks, space):
    n = len(blocks)
    relay = [b.size * b.dtype.itemsize >= RELAY_BYTES and b.shape[0] % 32 == 0 for b in blocks]

    def body(*refs):
        x_refs, out_refs = refs[:n], refs[n:2 * n]
        send_sems, recv_sems, local_sems = refs[2 * n:]
        x, y, c = _me()
        me, sibling = (x, y, c), (x, y, 1 - c)
        by_x, by_y, across = (1 - x, y, c), (x, 1 - y, c), (1 - x, 1 - y, c)

        def rows(i, blk, part=None):
            m_per = blocks[i].shape[0]
            at = (4 * blk[0] + 2 * blk[1] + blk[2]) * m_per
            if part is not None:
                m_per //= 2
                at += part * m_per
            return out_refs[i].at[pl.ds(at, m_per), :]

        def copy(i, k, blk, to, src=None, part=None):
            return _remote(rows(i, blk, part) if src is None else src, rows(i, blk, part), send_sems.at[8 * i + k],
                           recv_sems.at[8 * i + k], to)

        mine = [pltpu.make_async_copy(x_refs[i], rows(i, me), local_sems.at[i]) for i in range(n)]
        first = []
        for i in range(n):
            first.append(copy(i, 0, me, sibling, src=x_refs[i]))
            first += [copy(i, 1, me, by_x, src=x_refs[i]), copy(i, 2, me, by_y, src=x_refs[i])]
            if not relay[i]:
                first.append(copy(i, 3, me, across, src=x_refs[i]))
        for cp in mine + first:
            cp.start()
        passed = []
        for i in range(n):
            copy(i, 1, by_x, me).wait_recv()
            passed.append(copy(i, 4, by_x, sibling))
            if relay[i]:
                passed.append(copy(i, 3, by_x, by_y, part=0))
        for cp in passed:
            cp.start()
        n_x = len(passed)
        for i in range(n):
            copy(i, 2, by_y, me).wait_recv()
            passed.append(copy(i, 5, by_y, sibling))
            if relay[i]:
                passed.append(copy(i, 7, by_y, by_x, part=1))
        for cp in passed[n_x:]:
            cp.start()
        for i in range(n):
            if relay[i]:
                copy(i, 3, across, me, part=0).wait_recv()
                copy(i, 7, across, me, part=1).wait_recv()
            else:
                copy(i, 3, across, me).wait_recv()
            passed.append(copy(i, 6, across, sibling))
            passed[-1].start()
        for i in range(n):
            copy(i, 0, sibling, me).wait_recv()
            for k, blk in ((4, by_x), (5, by_y), (6, across)):
                copy(i, k, (blk[0], blk[1], 1 - c), me).wait_recv()
        for cp in first + passed:
            cp.wait_send()
        for cp in mine:
            cp.wait()

    spec = pl.BlockSpec(memory_space=space)
    return _pcall(
        body, name="all_gather8",
        out_shape=[jax.ShapeDtypeStruct((8 * b.shape[0], b.shape[1]), b.dtype) for b in blocks],
        in_specs=[spec] * n, out_specs=[spec] * n,
        scratch_shapes=[pltpu.SemaphoreType.DMA((8 * n,)), pltpu.SemaphoreType.DMA((8 * n,)),
                        pltpu.SemaphoreType.DMA((n,))],
    )(*blocks)


def _exchange(legs):
    n = len(legs)

    def body(*refs):
        copies, local, relays = _exchange_body(legs, refs[:n], refs[n:2 * n], *refs[2 * n:])
        for cp in copies + local:
            cp.start()
        _hand_on(relays)
        _wait_all(copies, local, relays)

    hbm = pl.BlockSpec(memory_space=pltpu.HBM)
    return _pcall(body, name="exchange", out_shape=[leg.landing() for leg in legs], in_specs=[hbm] * n,
                  out_specs=[hbm] * n, input_output_aliases=_exchange_aliases(legs, 0, 0),
                  scratch_shapes=_exchange_sems(legs))(*[leg.src for leg in legs])


def _row_tile(rows, cap=4096, mult=16):
    best = None
    for t in range(mult, min(rows, cap) + 1, mult):
        if rows % t == 0:
            best = t
    return rows if best is None else best


def _pair_sum(half, own, own_spec, got, got_spec, out_shape, out_spec, grid):
    def body(_, a_ref, b_ref, o_ref):
        o_ref[...] = (a_ref[...] + b_ref[...].astype(F32)).astype(o_ref.dtype)

    return _pcall(
        body, name="pair_sum",
        grid_spec=pltpu.PrefetchScalarGridSpec(num_scalar_prefetch=1, grid=grid, in_specs=[own_spec, got_spec],
                                               out_specs=out_spec),
        out_shape=out_shape, compiler_params=_seq(len(grid)))(half, own, got)


def _chip_sum(ids, part, met, fill, layer=0, stack=1):
    _, _, rows, n = part.shape
    tr = _row_tile(rows, cap=max(16, BLOCK_ELEMS // n))
    first = isinstance(stack, int)

    def body(_, own_ref, a_ref, b_ref, c_ref, *rest):
        acc = own_ref[...].astype(F32) + a_ref[...].astype(F32)
        acc = acc + b_ref[...].astype(F32)
        rest[-1][...] = acc + c_ref[...].astype(F32)

    blk = (None, None, tr, n)
    other = lambda k: pl.BlockSpec(blk, lambda j, ids: ((ids[0] + k) % 4, 0, j, 0))
    in_specs = [pl.BlockSpec(blk, lambda j, ids: (ids[0], 0, j, 0)), other(1), other(2), other(3)]
    return _pcall(
        body, name="chip_sum",
        grid_spec=pltpu.PrefetchScalarGridSpec(
            num_scalar_prefetch=1, grid=(rows // tr,),
            in_specs=in_specs if first else in_specs + [pl.BlockSpec(memory_space=pl.ANY)],
            out_specs=pl.BlockSpec(blk, lambda j, ids: (layer, ids[1] if fill else 0, j, 0))),
        out_shape=jax.ShapeDtypeStruct(((stack,) if first else stack.shape[:1]) + (2 if fill else 1, rows, n), F32),
        input_output_aliases={} if first else {5: 0},
        compiler_params=_seq())(*((ids, part, met, met, met) if first else (ids, part, met, met, met, stack)))


def _ada_mod(c_all, w_ada, b_ada_cols):
    depth, d, n = w_ada.shape
    nb = c_all.shape[0]

    def body(c_ref, w_ref, b_ref, o_ref):
        cv = c_ref[...]
        ca = _bf(cv * _sigmoid(cv))
        o_ref[0] = _dot(ca, _bf(w_ref[0])) + b_ref[0]

    return _pcall(body, name="ada_mod", grid=(depth,),
                  in_specs=[pl.BlockSpec((nb, d), lambda l: (0, 0)), pl.BlockSpec((1, d, n), lambda l: (l, 0, 0)),
                            pl.BlockSpec((1, 1, n), lambda l: (l, 0, 0))],
                  out_specs=pl.BlockSpec((1, nb, n), lambda l: (l, 0, 0)),
                  out_shape=jax.ShapeDtypeStruct((depth, nb, n), F32), compiler_params=_seq())(c_all, w_ada, b_ada_cols)


def _ada_grad(c_all, dmod_cols, rows_all):
    nb, d = c_all.shape
    depth, _, n = dmod_cols.shape
    kinds, n_all = rows_all.shape[1], rows_all.shape[3]

    def body(c_ref, dm_ref, da_ref, gw_ref, gb_ref):
        cv = c_ref[...]
        ca = _bf(cv * _sigmoid(cv))
        gw_ref[0] = _dot_tn(ca, _bf(dm_ref[0]))
        for k in range(kinds):
            gb_ref[0, k] = _colsum(da_ref[0, k])

    return _pcall(body, name="ada_grad", grid=(depth,),
                  in_specs=[pl.BlockSpec((nb, d), lambda l: (0, 0)), pl.BlockSpec((1, nb, n), lambda l: (l, 0, 0)),
                            pl.BlockSpec((1, kinds, nb, n_all), lambda l: (l, 0, 0, 0))],
                  out_specs=[pl.BlockSpec((1, d, n), lambda l: (l, 0, 0)),
                             pl.BlockSpec((1, kinds, 1, n_all), lambda l: (l, 0, 0, 0))],
                  out_shape=[jax.ShapeDtypeStruct((depth, d, n), F32), jax.ShapeDtypeStruct((depth, kinds, 1, n_all), F32)],
                  compiler_params=_seq())(c_all, dmod_cols, rows_all)


def _adamw(items, ride=None):
    two_d = [tuple(t.reshape(w.size // w.shape[-1], w.shape[-1]) for t in (w, g, m, v)) for w, g, m, v in items]
    n = len(items)
    if n == 1:
        rows, cols = two_d[0][0].shape
        tr = _row_tile(rows, cap=max(8, BLOCK_ELEMS // cols), mult=8)
        blocks = [pl.BlockSpec((tr, cols), lambda i: (i, 0))]
        grid = (rows // tr,)
    else:
        blocks = [pl.BlockSpec(t[0].shape, lambda i: (0, 0)) for t in two_d]
        grid = (1,)

    def body(*refs):
        for k in range(n):
            w_ref, g_ref, m_ref, v_ref = refs[4 * k:4 * k + 4]
            d_ref, mo_ref, vo_ref = refs[4 * n + 3 * k:4 * n + 3 * k + 3]
            gv = g_ref[...]
            mn = ADAM_B1 * m_ref[...] + (1.0 - ADAM_B1) * gv
            vn = ADAM_B2 * v_ref[...] + (1.0 - ADAM_B2) * (gv * gv)
            m_hat = mn / (1.0 - ADAM_B1 ** ADAM_STEP)
            v_hat = vn / (1.0 - ADAM_B2 ** ADAM_STEP)
            d_ref[...] = -ADAM_LR * (m_hat / (jnp.sqrt(v_hat) + ADAM_EPS) + ADAM_WD * w_ref[...])
            mo_ref[...] = mn
            vo_ref[...] = vn

    outs, got = _pcall_ride(
        body, ride, name="adamw", grid=grid,
        in_specs=[b for b in blocks for _ in range(4)], out_specs=[b for b in blocks for _ in range(3)],
        out_shape=[jax.ShapeDtypeStruct(t[0].shape, F32) for t in two_d for _ in range(3)],
        compiler_params=_seq(), args=tuple(a for t in two_d for a in t))
    return [tuple(o.reshape(items[k][0].shape) for o in outs[3 * k:3 * k + 3]) for k in range(n)], got


WEIGHTS = ["norm_g", "w_ada", "b_ada", "w_in", "rg_conv_w", "rg_conv_b", "rg_w_a", "rg_b_a", "rg_w_x", "rg_b_x",
           "rg_lambda", "ml_conv_w", "ml_conv_b", "ml_w_q", "ml_w_k", "ml_w_v", "ml_w_if", "ml_b_if", "ml_norm_g",
           "w_out", "final_g"]
SMALL_SHARDED = {"rg_conv_w": 1, "ml_conv_w": 1, "ml_w_if": 0}
REPLICATED = ["rg_w_a", "rg_w_x", "rg_conv_b", "rg_b_a", "rg_b_x", "rg_lambda", "ml_conv_b", "ml_norm_g", "ml_b_if"]
LANES = 128


def _to_pieces(g, axis):
    shp = g.shape
    g = g.reshape(shp[:axis] + (4, 2, shp[axis] // 8) + shp[axis + 1:])
    g = jnp.moveaxis(g, (axis, axis + 1), (0, 1))
    return g.reshape(4, 2, -1)


def _from_pieces(p, shard_shape, axis):
    k = p.shape[0]
    rest = shard_shape[:axis] + (shard_shape[axis] // k,) + shard_shape[axis + 1:]
    t = jnp.moveaxis(p.reshape((k,) + rest), 0, axis)
    return t.reshape(shard_shape)


def _pad_rows(flat, mult):
    n = flat.shape[-1]
    pad = (-n) % mult
    if pad:
        flat = jnp.concatenate([flat, jnp.zeros(flat.shape[:-1] + (pad,), flat.dtype)], axis=-1)
    return flat


def kernel(x, c, norm_g, w_ada, b_ada, w_in, rg_conv_w, rg_conv_b, rg_w_a, rg_b_a, rg_w_x, rg_b_x, rg_lambda, ml_conv_w, ml_conv_b, ml_w_q, ml_w_k, ml_w_v, ml_w_if, ml_b_if, ml_norm_g, w_out, final_g, loss_target, m_norm_g, m_w_ada, m_b_ada, m_w_in, m_rg_conv_w, m_rg_conv_b, m_rg_w_a, m_rg_b_a, m_rg_w_x, m_rg_b_x, m_rg_lambda, m_ml_conv_w, m_ml_conv_b, m_ml_w_q, m_ml_w_k, m_ml_w_v, m_ml_w_if, m_ml_b_if, m_ml_norm_g, m_w_out, m_final_g, v_norm_g, v_w_ada, v_b_ada, v_w_in, v_rg_conv_w, v_rg_conv_b, v_rg_w_a, v_rg_b_a, v_rg_w_x, v_rg_b_x, v_rg_lambda, v_ml_conv_w, v_ml_conv_b, v_ml_w_q, v_ml_w_k, v_ml_w_v, v_ml_w_if, v_ml_b_if, v_ml_norm_g, v_w_out, v_final_g):
    given = dict(locals())
    ax, ay, ac = lax.axis_index("x"), lax.axis_index("y"), lax.axis_index("c")
    chip = 2 * ax + ay
    me = 2 * chip + ac
    depth, d = norm_g.shape
    n_ada = w_ada.shape[2]
    pick = lambda a, i, axis=0: lax.dynamic_index_in_dim(a, i, axis, keepdims=False)

    convs = jnp.stack([rg_conv_w, ml_conv_w])
    n_conv = 2 * depth * CONV_WIDTH // 4
    blk = jnp.concatenate([c, convs.reshape(n_conv, d), jnp.zeros((8 - 1 - n_conv, d), F32)], axis=0)
    w_in_first = lax.dynamic_slice_in_dim(w_in[0], ac * (d // 2), d // 2, 0).astype(BF16)
    g0, w_in_first = _all_gather8([blk, w_in_first], pltpu.HBM)
    g0 = g0.reshape(8, 8, d)
    c_all = g0[:, 0, :]
    conv_full = g0[0::2, 1:1 + n_conv].reshape(4, 2, depth, CONV_WIDTH, d // 4)
    conv_full = conv_full.transpose(1, 2, 3, 0, 4).reshape(2, depth, CONV_WIDTH, d)

    b_cols = lax.dynamic_slice_in_dim(b_ada, chip * n_ada, n_ada, axis=1)[:, None, :]
    mod_part = _ada_mod(c_all, w_ada, b_cols)
    g1 = _all_gather8([mod_part.transpose(1, 0, 2).reshape(8, depth * n_ada)], pltpu.VMEM)[0]
    g1 = g1.reshape(8, 8, depth, n_ada)[0::2]
    mod_me = pick(g1.transpose(1, 2, 0, 3).reshape(8, depth, 4 * n_ada), me)

    def half_of(w, axis):
        n = w.shape[axis] // 2
        return lax.dynamic_slice_in_dim(w, ac * n, n, axis).astype(BF16)

    n_sh = w_in.shape[2]
    heads, hd_cut, hd = ml_w_q.shape[1:]

    def blocks_of(l):
        wqkv = jnp.stack([ml_w_q[l], ml_w_k[l], ml_w_v[l]])
        return [half_of(w_in[l], 0), half_of(w_out[l], 0), half_of(wqkv, 2).reshape(-1, hd), half_of(ml_w_if[l], 0)]

    def layer_of(l, w4, rest):
        return dict(
            norm_g=norm_g[l][None], shift=mod_me[l, 0:d][None], scale=mod_me[l, d:2 * d][None],
            gate=mod_me[l, 2 * d:3 * d][None], w4=w4.reshape(4, d, n_sh),
            rg_conv_w=conv_full[0, l], rg_conv_b=rg_conv_b[l][None], rg_wa_b=_bf(rg_w_a[l]), rg_ba=rg_b_a[l][None],
            rg_wx_b=_bf(rg_w_x[l]), rg_bx=rg_b_x[l][None], rg_lam=rg_lambda[l][None],
            ml_conv_w=conv_full[1, l], ml_conv_b=ml_conv_b[l][None], b_if=ml_b_if[l][None], b_ift=ml_b_if[l][:, None],
            ml_g=ml_norm_g[l][None], **rest)

    def rest_of(gathered):
        w_out_b, wqkv_g, wif = gathered
        return dict(w_out_b=w_out_b, wqkv_b=_from_pieces(wqkv_g.reshape(8, -1), (3, heads, hd, hd), 2), wif_b=wif,
                    wift_b=wif.T)

    spread = lambda blocks: [Leg(b, "spread") for b in blocks]
    fill = lambda landed: [Leg(t, "sib_fill") for t in landed]
    flat = lambda filled: [t.reshape(-1, t.shape[-1]) for t in filled]
    first = blocks_of(0)
    n_rest = len(first) - 1
    p = layer_of(0, w_in_first, {})
    layers, saved = [], []
    xl = x[0]
    for l in range(depth):
        nxt = blocks_of(l + 1) if l + 1 < depth else []
        skip = n_rest if l == 0 else 0
        rides = dict(rg_fwd=lambda landed, nxt=nxt: spread(nxt[:1]))
        if l == 0:
            rides.update(ln_inproj=lambda landed: spread(first[1:]),
                         rg_fwd=lambda landed, nxt=nxt: fill(landed["ln_inproj"]) + spread(nxt[:1]),
                         late=lambda landed: rest_of(flat(landed["rg_fwd"][:n_rest])))
        if nxt:
            rides.update(mlstm_fwd=lambda landed, nxt=nxt: spread(nxt[1:]),
                         out_proj=lambda landed, skip=skip: fill(list(landed["rg_fwd"][skip:]) + list(landed["mlstm_fwd"])))
        xl, s, p, landed = _layer_fwd(xl, p, rides, None if nxt else (final_g[None], loss_target[0]))
        layers.append(p)
        saved.append(s)
        if nxt:
            arrived = flat(landed["out_proj"])
            p = layer_of(l + 1, arrived[0], rest_of(arrived[1:]))
    dx, g_final, loss = xl

    half = ac.reshape(1)
    ids = jnp.stack([chip, ac])
    r_out = w_out.shape[1] // 2

    def pair_in(g_w_in, got_in):
        return _pair_sum(
            half, g_w_in, pl.BlockSpec((None, d // 2, n_sh), lambda s, h: (0, h[0], s)),
            got_in, pl.BlockSpec((None, None, d // 2, n_sh), lambda s, h: (0, s, 0, 0)),
            jax.ShapeDtypeStruct((4, 1, d // 2, n_sh), BF16),
            pl.BlockSpec((None, None, d // 2, n_sh), lambda s, h: (s, 0, 0, 0)), (4,))

    def pair_out(g_out5, got_out):
        return _pair_sum(
            half, g_out5, pl.BlockSpec((None, None, None, r_out, d), lambda s, h: (0, s, h[0], 0, 0)),
            got_out, pl.BlockSpec((None, None, r_out, d), lambda s, h: (0, s, 0, 0)),
            jax.ShapeDtypeStruct((4, 1, r_out, d), BF16),
            pl.BlockSpec((None, None, r_out, d), lambda s, h: (s, 0, 0, 0)), (4,))

    def pair_slab(slab, got, dtype):
        rows = got.shape[0] // 4
        blk = pl.BlockSpec((rows, LANES), lambda s, h: (s, 0))
        return _pair_sum(half, slab, pl.BlockSpec((None, rows, LANES), lambda s, h: (h[0], s, 0)), got, blk,
                         jax.ShapeDtypeStruct((4 * rows, LANES), dtype), blk, (4,)).reshape(4, 1, rows, LANES)

    row_pad = lambda n: -(-n // (8 * LANES)) * (8 * LANES)

    def as_rows(t):
        if t.shape[-1] == LANES and t.size % (8 * LANES) == 0:
            return t.reshape(-1, LANES)
        return _pad_rows(t.reshape(-1), 8 * LANES).reshape(-1, LANES)

    chips = lambda arrs: [Leg(a, "chips") for a in arrs]
    out5 = lambda g: g["w_out"].reshape(1, 4, 2, r_out, d)
    r_q = hd // 8
    qkv5 = lambda g: g["ml_w_qkv"].reshape(3 * heads, 4, 2, r_q, hd)

    def pair_qkv(g5, got):
        return _pair_sum(
            half, g5, pl.BlockSpec((3 * heads, None, None, r_q, hd), lambda s, h: (0, s, h[0], 0, 0)),
            got, pl.BlockSpec((3 * heads, None, r_q, hd), lambda s, h: (0, s, 0, 0)),
            jax.ShapeDtypeStruct((4, 1, 3 * heads, r_q, hd), BF16),
            pl.BlockSpec((None, None, 3 * heads, r_q, hd), lambda s, h: (s, 0, 0, 0, 0)), (4,))

    grads, dmods, parts, mets = [None] * depth, [None] * depth, [None] * depth, [None] * depth
    small = {}

    def early_exchange(g, landed):
        every = [g] + grads[1:]
        sm = jnp.concatenate([_to_pieces(every[l][name], axis) for l in range(depth)
                              for name, axis in SMALL_SHARDED.items()], axis=-1)
        sm = _pad_rows(sm, 16 * LANES)
        sm = sm.transpose(1, 0, 2).reshape(2, -1, LANES)
        rep = [as_rows(every[l][name]) for l in range(depth) for name in REPLICATED]
        rep = jnp.concatenate(rep + [as_rows(g_final), as_rows(loss)], axis=0)
        rep = jnp.concatenate([rep, jnp.zeros(((-rep.shape[0]) % 64, LANES), F32)], axis=0)
        rep = rep.reshape(4, 2, -1, LANES).transpose(1, 0, 2, 3).reshape(2, -1, LANES)
        got_sm, got_rep, got_q = _exchange([Leg(sm, "sib_slab"), Leg(rep, "sib_slab"), Leg(qkv5(g), "sib_w_out")])
        small["parts"] = [pair_out(out5(g), landed["mlstm_bwd"][0]), pair_slab(sm, got_sm, BF16),
                          pair_slab(rep, got_rep, F32), pair_qkv(qkv5(g), got_q)]
        return chips(small["parts"])

    def last_exchange(g, landed):
        (got_in,) = _exchange([Leg(g["w_in"], "sib_w_in")])
        small["part_in"] = pair_in(g["w_in"], got_in)
        return chips([small["part_in"]])

    for l in reversed(range(depth)):
        above = parts[l + 1] if l + 1 < depth else []
        rides = dict(mlstm_bwd=lambda g, landed, above=above: [Leg(out5(g), "sib_w_out")] + chips(above),
                     in_bwd=lambda g, landed: [Leg(g["w_in"], "sib_w_in"), Leg(qkv5(g), "sib_w_out")])
        if l == 0:
            rides.update(grad_w_in=early_exchange, in_bwd=last_exchange)
        dx, grads[l], dmods[l], got = _layer_bwd(dx, layers[l], saved[l], rides)
        if above:
            mets[l + 1] = got["mlstm_bwd"][1:]
        if l > 0:
            parts[l] = [pair_in(grads[l]["w_in"], got["in_bwd"][0]), pair_out(out5(grads[l]), got["mlstm_bwd"][0]),
                        pair_qkv(qkv5(grads[l]), got["in_bwd"][1])]
    part_out, part_sm, part_rep, part_q = small["parts"]
    met_out, met_sm, met_rep, met_q = got["grad_w_in"]
    parts[0], mets[0] = [small["part_in"], part_out, part_q], [got["in_bwd"][0], met_out, met_q]
    n_rep = part_rep.shape[2]

    pad = lambda t: jnp.concatenate([t, jnp.zeros((1, 2 * d), F32)], axis=1)
    rows = [r for l in range(depth) for r in (dmods[l], pad(grads[l]["norm_g"]))]
    blk = jnp.concatenate(rows + [jnp.zeros((8 - 2 * depth, 3 * d), F32)], axis=0)
    red_rep = _chip_sum(ids, part_rep, met_rep, False).reshape(n_rep, LANES)
    rows_all, rep_all = _all_gather8([blk, red_rep], pltpu.VMEM)
    rows_all, rep_all = rows_all.reshape(8, 8, 3 * d)[:, :2 * depth], rep_all.reshape(-1)
    rows_all = rows_all.transpose(1, 0, 2).reshape(depth, 2, 8, 3 * d)
    dm_cols = lax.dynamic_slice_in_dim(rows_all[:, 0], chip * n_ada, n_ada, axis=2)
    g_w_ada, summed = _ada_grad(c_all, dm_cols, rows_all)

    g = dict(w_ada=g_w_ada, b_ada=summed[:, 0, 0], norm_g=summed[:, 1, 0, :d])
    item = lambda name: (given[name], g[name], given["m_" + name], given["v_" + name])
    both_in, both_out, both_q = depth, depth, depth
    flat_q = lambda t: t.reshape(4, 1, 3 * heads * r_q, hd)
    for l in range(depth):
        both_in = _chip_sum(ids, parts[l][0], mets[l][0], True, l, both_in)
        both_out = _chip_sum(ids, parts[l][1], mets[l][1], True, l, both_out)
        both_q = _chip_sum(ids, flat_q(parts[l][2]), flat_q(mets[l][2]), True, l, both_q)
    both_in, both_out, both_q, both_sm = _exchange(fill([both_in, both_out, both_q,
                                                         _chip_sum(ids, part_sm, met_sm, True)]))

    g.update(w_in=both_in.reshape(w_in.shape), w_out=both_out.reshape(w_out.shape))
    g_qkv = both_q.reshape(depth, 2, 3, heads, r_q, hd).transpose(0, 2, 3, 1, 4, 5)
    g_qkv = g_qkv.reshape(depth, 3, heads, 2 * r_q, hd)
    for i, name in enumerate(["ml_w_q", "ml_w_k", "ml_w_v"]):
        g[name] = g_qkv[:, i]
    shard = both_sm.reshape(2, -1)
    off = 0
    per_layer = {name: [] for name in SMALL_SHARDED}
    for l in range(depth):
        for name, axis in SMALL_SHARDED.items():
            n = grads[l][name].size // 8
            per_layer[name].append(_from_pieces(shard[:, off:off + n], given[name].shape[1:], axis))
            off += n
    for name in SMALL_SHARDED:
        g[name] = jnp.stack(per_layer[name])
    off = 0
    per_layer = {name: [] for name in REPLICATED}
    for l in range(depth):
        for name in REPLICATED:
            n = given[name][l].size
            per_layer[name].append(rep_all[off:off + n].reshape(given[name].shape[1:]))
            off += row_pad(n)
    for name in REPLICATED:
        g[name] = jnp.stack(per_layer[name])
    g["final_g"] = rep_all[off:off + d]
    loss_all = rep_all[off + row_pad(d)]

    stepped = {}
    rg_mats, ml_mats = ["rg_w_a", "rg_w_x"], ["ml_w_q", "ml_w_k", "ml_w_v"]
    vectors = [n for n in WEIGHTS if n not in ["w_ada", "w_in", "w_out"] + rg_mats + ml_mats]
    for names in (["w_ada"], ["w_in"], ["w_out"], rg_mats, ml_mats, vectors):
        stepped.update(zip(names, _adamw([item(name) for name in names])[0]))
    deltas, new_m, new_v = zip(*[stepped[name] for name in WEIGHTS])
    return (loss_all, dx[None], *[g[name] for name in WEIGHTS], *deltas, *new_m, *new_v)
```
